```python
import math
import jax, jax.numpy as jnp
from jax import lax
import numpy as np

D_MODEL = 1024
BATCH = 8
SEQ = 4096
DEPTH = 1

HEAD_DIM = 64
N_HEADS = D_MODEL // HEAD_DIM
N_KV_HEADS = N_HEADS // 4
ATTN_WIDTH = N_HEADS * HEAD_DIM
KV_WIDTH = N_KV_HEADS * HEAD_DIM
WINDOW = 128
BLOCK = 128
ROT_DIM = HEAD_DIM // 4
ROPE_THETA = 500000.0
RNN_WIDTH = D_MODEL
RNN_BLOCK_WIDTH = 256
RNN_N_BLOCKS = RNN_WIDTH // RNN_BLOCK_WIDTH
LRU_C = 8.0
CONV_WIDTH = 4
NORM_EPS = 1e-6
SPLIT_SIZES = (ATTN_WIDTH, KV_WIDTH, KV_WIDTH, ATTN_WIDTH,
               RNN_WIDTH, RNN_WIDTH,
               D_MODEL, D_MODEL)
IN_WIDTH = sum(SPLIT_SIZES)
SPLIT_POINTS = tuple(int(v) for v in np.cumsum(SPLIT_SIZES)[:-1])

kernel_name = "hybrid_swa_sink_rglru_gated_block"


def rmsnorm(x, g):
    xf = x.astype(jnp.float32)
    r = lax.rsqrt(jnp.mean(xf * xf, axis=-1, keepdims=True) + NORM_EPS)
    return (xf * r).astype(x.dtype) * g


def partial_rope(t, pos):
    half = ROT_DIM // 2
    inv_freq = ROPE_THETA ** (-jnp.arange(0, ROT_DIM, 2, dtype=jnp.float32) / ROT_DIM)
    ang = pos[..., None].astype(jnp.float32) * inv_freq
    cos = jnp.cos(ang)[:, :, None, :]
    sin = jnp.sin(ang)[:, :, None, :]
    rot = t[..., :ROT_DIM].astype(jnp.float32)
    x1, x2 = rot[..., :half], rot[..., half:]
    rotated = jnp.concatenate([x1 * cos - x2 * sin, x2 * cos + x1 * sin], axis=-1)
    return jnp.concatenate([rotated.astype(t.dtype), t[..., ROT_DIM:]], axis=-1)


def sliding_window_attention_with_sinks(q, k, v, sinks):
    B, S, H, hd = q.shape
    nb = S // BLOCK
    G = H // N_KV_HEADS
    qb = q.reshape(B, nb, BLOCK, N_KV_HEADS, G, hd).astype(jnp.float32)
    pad = ((0, 0), (BLOCK, 0), (0, 0), (0, 0))
    kp = jnp.pad(k, pad).reshape(B, nb + 1, BLOCK, N_KV_HEADS, hd)
    vp = jnp.pad(v, pad).reshape(B, nb + 1, BLOCK, N_KV_HEADS, hd)
    kb = jnp.concatenate([kp[:, :-1], kp[:, 1:]], axis=2).astype(jnp.float32)
    vb = jnp.concatenate([vp[:, :-1], vp[:, 1:]], axis=2).astype(jnp.float32)
    s = jnp.einsum('bnqkgd,bnskd->bnkgqs', qb, kb) * (1.0 / math.sqrt(hd))
    qi = jnp.arange(BLOCK)[:, None]
    kj = jnp.arange(2 * BLOCK)[None, :]
    diff = qi + BLOCK - kj
    band = (diff >= 0) & (diff < WINDOW)
    kpos = jnp.arange(nb)[:, None] * BLOCK - BLOCK + jnp.arange(2 * BLOCK)[None, :]
    mask = band[None] & (kpos >= 0)[:, None, :]
    s = jnp.where(mask[None, :, None, None], s, -1e30)
    sink = sinks.astype(jnp.float32).reshape(N_KV_HEADS, G)[None, None, :, :, None, None]
    m = jnp.maximum(jnp.max(s, axis=-1, keepdims=True), sink)
    p = jnp.exp(s - m)
    denom = jnp.sum(p, axis=-1, keepdims=True) + jnp.exp(sink - m)
    o = jnp.einsum('bnkgqs,bnskd->bnqkgd', p / denom, vb)
    return o.reshape(B, S, H * hd).astype(q.dtype)


def causal_depthwise_conv(x, w, b):
    S = x.shape[1]
    xp = jnp.pad(x, ((0, 0), (CONV_WIDTH - 1, 0), (0, 0)))
    y = sum(xp[:, k:k + S] * w[k] for k in range(CONV_WIDTH))
    return y + b


def rg_lru(xr, pos, wa, ba, wx, bx, lam):
    B, S, D = xr.shape
    xb = xr.reshape(B, S, RNN_N_BLOCKS, RNN_BLOCK_WIDTH)
    r = jax.nn.sigmoid(jnp.einsum('bshi,hij->bshj', xb, wa).reshape(B, S, D) + ba)
    i = jax.nn.sigmoid(jnp.einsum('bshi,hij->bshj', xb, wx).reshape(B, S, D) + bx)
    log_a = -LRU_C * r.astype(jnp.float32) * jax.nn.softplus(-lam.astype(jnp.float32))
    a = jnp.exp(log_a)
    mult = jnp.sqrt(-jnp.expm1(2.0 * log_a))
    reset = (pos == 0)[..., None]
    mult = jnp.where(reset, 1.0, mult)
    a = jnp.where(reset, 0.0, a)
    b = mult * (i * xr).astype(jnp.float32)

    def combine(lhs, rhs):
        a1, b1 = lhs
        a2, b2 = rhs
        return a1 * a2, a2 * b1 + b2

    _, h = lax.associative_scan(combine, (a, b), axis=1)
    return h.astype(xr.dtype)


def _fwd_setup_inputs(seed: int = 0) -> dict:
    key = jax.random.key(seed)
    ks = jax.random.split(key, 20)
    f32 = jnp.float32
    nrm = lambda k, shape, scale: jax.random.normal(k, shape, f32) * scale
    s = jax.nn.sigmoid(jnp.zeros(()))
    del s
    a_c = jax.random.uniform(ks[12], (DEPTH, RNN_WIDTH), f32, 0.9, 0.999)
    a_base = a_c ** (1.0 / LRU_C)
    lam = jnp.log(a_base) - jnp.log1p(-a_base)
    return {
        "x": nrm(ks[0], (BATCH, SEQ, D_MODEL), 1.0),
        "c": nrm(ks[1], (BATCH, D_MODEL), 1.0),
        "positions": jnp.broadcast_to(jnp.arange(SEQ, dtype=jnp.int32), (BATCH, SEQ)),
        "w_ada": nrm(ks[2], (DEPTH, D_MODEL, 3 * D_MODEL), 0.1 * D_MODEL ** -0.5),
        "b_ada": nrm(ks[3], (DEPTH, 3 * D_MODEL), 0.01),
        "norm_g": 1.0 + nrm(ks[4], (DEPTH, D_MODEL), 0.02),
        "w_in": nrm(ks[5], (DEPTH, D_MODEL, IN_WIDTH), D_MODEL ** -0.5),
        "attn_sinks": nrm(ks[6], (DEPTH, N_HEADS), 0.5),
        "conv_w": nrm(ks[7], (DEPTH, CONV_WIDTH, RNN_WIDTH), CONV_WIDTH ** -0.5),
        "conv_b": nrm(ks[8], (DEPTH, RNN_WIDTH), 0.01),
        "rg_wa": nrm(ks[9], (DEPTH, RNN_N_BLOCKS, RNN_BLOCK_WIDTH, RNN_BLOCK_WIDTH), RNN_BLOCK_WIDTH ** -0.5),
        "rg_ba": nrm(ks[10], (DEPTH, RNN_WIDTH), 0.01),
        "rg_wx": nrm(ks[11], (DEPTH, RNN_N_BLOCKS, RNN_BLOCK_WIDTH, RNN_BLOCK_WIDTH), RNN_BLOCK_WIDTH ** -0.5),
        "rg_bx": nrm(ks[13], (DEPTH, RNN_WIDTH), 0.01),
        "rg_lambda": lam,
        "w_attn_proj": nrm(ks[14], (DEPTH, ATTN_WIDTH, D_MODEL), ATTN_WIDTH ** -0.5),
        "w_rnn_proj": nrm(ks[15], (DEPTH, RNN_WIDTH, D_MODEL), RNN_WIDTH ** -0.5),
        "w_out": nrm(ks[16], (DEPTH, D_MODEL, D_MODEL), D_MODEL ** -0.5),
        "final_g": 1.0 + nrm(ks[17], (D_MODEL,), 0.02),
    }


def _fwd_reference(x, c, positions, w_ada, b_ada, norm_g, w_in, attn_sinks, conv_w, conv_b,
              rg_wa, rg_ba, rg_wx, rg_bx, rg_lambda, w_attn_proj, w_rnn_proj, w_out, final_g):
    B, S, _ = x.shape
    for l in range(DEPTH):
        mod = c @ w_ada[l] + b_ada[l]
        shift, scale, gate = jnp.split(mod, 3, axis=-1)
        h = rmsnorm(x, norm_g[l]) * (1.0 + scale[:, None, :]) + shift[:, None, :]
        proj = h @ w_in[l]
        q, k, v, g_attn, xr, g_rnn, m_attn, m_rnn = jnp.split(proj, SPLIT_POINTS, axis=-1)
        q = partial_rope(q.reshape(B, S, N_HEADS, HEAD_DIM), positions)
        k = partial_rope(k.reshape(B, S, N_KV_HEADS, HEAD_DIM), positions)
        v = v.reshape(B, S, N_KV_HEADS, HEAD_DIM)
        y_attn = sliding_window_attention_with_sinks(q, k, v, attn_sinks[l]) * jax.nn.silu(g_attn)
        xr = causal_depthwise_conv(xr, conv_w[l], conv_b[l])
        y_rnn = rg_lru(xr, positions, rg_wa[l], rg_ba[l], rg_wx[l], rg_bx[l], rg_lambda[l]) * jax.nn.silu(g_rnn)
        merged = (jax.nn.sigmoid(m_attn) * (y_attn @ w_attn_proj[l])
                  + jax.nn.sigmoid(m_rnn) * (y_rnn @ w_rnn_proj[l]))
        x = x + gate[:, None, :] * (merged @ w_out[l])
    return rmsnorm(x, final_g)


import jax as _jax
import jax.numpy as _jnp

TWIN_FORMAT = 'train_step'
FWD_PARAMS = ['x', 'c', 'positions', 'w_ada', 'b_ada', 'norm_g', 'w_in', 'attn_sinks', 'conv_w', 'conv_b', 'rg_wa', 'rg_ba', 'rg_wx', 'rg_bx', 'rg_lambda', 'w_attn_proj', 'w_rnn_proj', 'w_out', 'final_g']
TWIN_WEIGHTS = ['w_ada', 'b_ada', 'norm_g', 'w_in', 'attn_sinks', 'conv_w', 'conv_b', 'rg_wa', 'rg_ba', 'rg_wx', 'rg_bx', 'rg_lambda', 'w_attn_proj', 'w_rnn_proj', 'w_out', 'final_g']
TWIN_DIFF_INPUT = 'x'
TWIN_INPUTS = ['x', 'c', 'positions', 'w_ada', 'b_ada', 'norm_g', 'w_in', 'attn_sinks', 'conv_w', 'conv_b', 'rg_wa', 'rg_ba', 'rg_wx', 'rg_bx', 'rg_lambda', 'w_attn_proj', 'w_rnn_proj', 'w_out', 'final_g', 'loss_target', 'm_w_ada', 'm_b_ada', 'm_norm_g', 'm_w_in', 'm_attn_sinks', 'm_conv_w', 'm_conv_b', 'm_rg_wa', 'm_rg_ba', 'm_rg_wx', 'm_rg_bx', 'm_rg_lambda', 'm_w_attn_proj', 'm_w_rnn_proj', 'm_w_out', 'm_final_g', 'v_w_ada', 'v_b_ada', 'v_norm_g', 'v_w_in', 'v_attn_sinks', 'v_conv_w', 'v_conv_b', 'v_rg_wa', 'v_rg_ba', 'v_rg_wx', 'v_rg_bx', 'v_rg_lambda', 'v_w_attn_proj', 'v_w_rnn_proj', 'v_w_out', 'v_final_g']
TWIN_OUTPUTS = ['loss', 'grad_x', 'grad_w_ada', 'grad_b_ada', 'grad_norm_g', 'grad_w_in', 'grad_attn_sinks', 'grad_conv_w', 'grad_conv_b', 'grad_rg_wa', 'grad_rg_ba', 'grad_rg_wx', 'grad_rg_bx', 'grad_rg_lambda', 'grad_w_attn_proj', 'grad_w_rnn_proj', 'grad_w_out', 'grad_final_g', 'delta_w_ada', 'delta_b_ada', 'delta_norm_g', 'delta_w_in', 'delta_attn_sinks', 'delta_conv_w', 'delta_conv_b', 'delta_rg_wa', 'delta_rg_ba', 'delta_rg_wx', 'delta_rg_bx', 'delta_rg_lambda', 'delta_w_attn_proj', 'delta_w_rnn_proj', 'delta_w_out', 'delta_final_g', 'new_m_w_ada', 'new_m_b_ada', 'new_m_norm_g', 'new_m_w_in', 'new_m_attn_sinks', 'new_m_conv_w', 'new_m_conv_b', 'new_m_rg_wa', 'new_m_rg_ba', 'new_m_rg_wx', 'new_m_rg_bx', 'new_m_rg_lambda', 'new_m_w_attn_proj', 'new_m_w_rnn_proj', 'new_m_w_out', 'new_m_final_g', 'new_v_w_ada', 'new_v_b_ada', 'new_v_norm_g', 'new_v_w_in', 'new_v_attn_sinks', 'new_v_conv_w', 'new_v_conv_b', 'new_v_rg_wa', 'new_v_rg_ba', 'new_v_rg_wx', 'new_v_rg_bx', 'new_v_rg_lambda', 'new_v_w_attn_proj', 'new_v_w_rnn_proj', 'new_v_w_out', 'new_v_final_g']
TWIN_LEAF_KINDS = {'loss': 'loss', 'grad_x': 'grad_x', 'grad_w_ada': 'grad_w', 'grad_b_ada': 'grad_w', 'grad_norm_g': 'grad_w', 'grad_w_in': 'grad_w', 'grad_attn_sinks': 'grad_w', 'grad_conv_w': 'grad_w', 'grad_conv_b': 'grad_w', 'grad_rg_wa': 'grad_w', 'grad_rg_ba': 'grad_w', 'grad_rg_wx': 'grad_w', 'grad_rg_bx': 'grad_w', 'grad_rg_lambda': 'grad_w', 'grad_w_attn_proj': 'grad_w', 'grad_w_rnn_proj': 'grad_w', 'grad_w_out': 'grad_w', 'grad_final_g': 'grad_w', 'delta_w_ada': 'delta_w', 'delta_b_ada': 'delta_w', 'delta_norm_g': 'delta_w', 'delta_w_in': 'delta_w', 'delta_attn_sinks': 'delta_w', 'delta_conv_w': 'delta_w', 'delta_conv_b': 'delta_w', 'delta_rg_wa': 'delta_w', 'delta_rg_ba': 'delta_w', 'delta_rg_wx': 'delta_w', 'delta_rg_bx': 'delta_w', 'delta_rg_lambda': 'delta_w', 'delta_w_attn_proj': 'delta_w', 'delta_w_rnn_proj': 'delta_w', 'delta_w_out': 'delta_w', 'delta_final_g': 'delta_w', 'new_m_w_ada': 'new_m', 'new_m_b_ada': 'new_m', 'new_m_norm_g': 'new_m', 'new_m_w_in': 'new_m', 'new_m_attn_sinks': 'new_m', 'new_m_conv_w': 'new_m', 'new_m_conv_b': 'new_m', 'new_m_rg_wa': 'new_m', 'new_m_rg_ba': 'new_m', 'new_m_rg_wx': 'new_m', 'new_m_rg_bx': 'new_m', 'new_m_rg_lambda': 'new_m', 'new_m_w_attn_proj': 'new_m', 'new_m_w_rnn_proj': 'new_m', 'new_m_w_out': 'new_m', 'new_m_final_g': 'new_m', 'new_v_w_ada': 'new_v', 'new_v_b_ada': 'new_v', 'new_v_norm_g': 'new_v', 'new_v_w_in': 'new_v', 'new_v_attn_sinks': 'new_v', 'new_v_conv_w': 'new_v', 'new_v_conv_b': 'new_v', 'new_v_rg_wa': 'new_v', 'new_v_rg_ba': 'new_v', 'new_v_rg_wx': 'new_v', 'new_v_rg_bx': 'new_v', 'new_v_rg_lambda': 'new_v', 'new_v_w_attn_proj': 'new_v', 'new_v_w_rnn_proj': 'new_v', 'new_v_w_out': 'new_v', 'new_v_final_g': 'new_v'}


def _forward(args):
    return _fwd_reference(*[args[k] for k in FWD_PARAMS])


def _output_shape():
    out = _jax.eval_shape(lambda: _forward(_fwd_setup_inputs(0)))
    return out.shape, out.dtype

N_MICROBATCH = 1
ADAM_LR = 0.001
ADAM_B1 = 0.9
ADAM_B2 = 0.999
ADAM_EPS = 1e-08
ADAM_WD = 0.01
ADAM_STEP = 10
PER_EXAMPLE_BATCH_AXIS = {'x': 0, 'c': 0, 'positions': 0, 'loss_target': 0}
SHARED_INPUTS = []
_WEIGHT_DTYPES = {'w_ada': _jnp.float32, 'b_ada': _jnp.float32, 'norm_g': _jnp.float32, 'w_in': _jnp.float32, 'attn_sinks': _jnp.float32, 'conv_w': _jnp.float32, 'conv_b': _jnp.float32, 'rg_wa': _jnp.float32, 'rg_ba': _jnp.float32, 'rg_wx': _jnp.float32, 'rg_bx': _jnp.float32, 'rg_lambda': _jnp.float32, 'w_attn_proj': _jnp.float32, 'w_rnn_proj': _jnp.float32, 'w_out': _jnp.float32, 'final_g': _jnp.float32}
MOMENT_SCALE = {'w_ada': 4.762791e-02, 'b_ada': 5.312507e-02, 'norm_g': 7.581715e-03, 'w_in': 3.638008e-03, 'attn_sinks': 1.090039e-03, 'conv_w': 6.207240e-03, 'conv_b': 5.133221e-02, 'rg_wa': 1.218696e-03, 'rg_ba': 1.409639e-03, 'rg_wx': 2.202445e-03, 'rg_bx': 2.757729e-03, 'rg_lambda': 3.297831e-03, 'w_attn_proj': 1.385643e-03, 'w_rnn_proj': 6.550301e-03, 'w_out': 6.425179e-03, 'final_g': 3.200269e+01}


def _to_microbatches(a, axis):
    t = _jnp.moveaxis(a, axis, 0)
    t = t.reshape((N_MICROBATCH, t.shape[0] // N_MICROBATCH) + t.shape[1:])
    return _jnp.moveaxis(t, 1, axis + 1)


def setup_inputs(seed: int = 0) -> dict:
    inp = _fwd_setup_inputs(seed)
    key = _jax.random.fold_in(_jax.random.key(seed), 7919)
    shape, _ = _output_shape()
    out = dict(inp)
    out["loss_target"] = _jax.random.normal(_jax.random.fold_in(key, 0), shape, _jnp.float32)
    for i, name in enumerate(TWIN_WEIGHTS):
        w = inp[name].astype(_jnp.float32)
        if MOMENT_SCALE is None:
            s = _jnp.sqrt(_jnp.mean(_jnp.square(w)) + 1e-30)
        else:
            s = MOMENT_SCALE[name]
        km, kv = _jax.random.split(_jax.random.fold_in(key, i + 1))
        out[name] = w
        out["m_" + name] = s * _jax.random.normal(km, w.shape, _jnp.float32)
        out["v_" + name] = (s * s) * _jax.random.uniform(kv, w.shape, _jnp.float32, 0.5, 1.5)
    if N_MICROBATCH > 1:
        for name, axis in PER_EXAMPLE_BATCH_AXIS.items():
            out[name] = _to_microbatches(out[name], axis)
    return {'x': out['x'], 'c': out['c'], 'positions': out['positions'], 'w_ada': out['w_ada'], 'b_ada': out['b_ada'], 'norm_g': out['norm_g'], 'w_in': out['w_in'], 'attn_sinks': out['attn_sinks'], 'conv_w': out['conv_w'], 'conv_b': out['conv_b'], 'rg_wa': out['rg_wa'], 'rg_ba': out['rg_ba'], 'rg_wx': out['rg_wx'], 'rg_bx': out['rg_bx'], 'rg_lambda': out['rg_lambda'], 'w_attn_proj': out['w_attn_proj'], 'w_rnn_proj': out['w_rnn_proj'], 'w_out': out['w_out'], 'final_g': out['final_g'], 'loss_target': out['loss_target'], 'm_w_ada': out['m_w_ada'], 'm_b_ada': out['m_b_ada'], 'm_norm_g': out['m_norm_g'], 'm_w_in': out['m_w_in'], 'm_attn_sinks': out['m_attn_sinks'], 'm_conv_w': out['m_conv_w'], 'm_conv_b': out['m_conv_b'], 'm_rg_wa': out['m_rg_wa'], 'm_rg_ba': out['m_rg_ba'], 'm_rg_wx': out['m_rg_wx'], 'm_rg_bx': out['m_rg_bx'], 'm_rg_lambda': out['m_rg_lambda'], 'm_w_attn_proj': out['m_w_attn_proj'], 'm_w_rnn_proj': out['m_w_rnn_proj'], 'm_w_out': out['m_w_out'], 'm_final_g': out['m_final_g'], 'v_w_ada': out['v_w_ada'], 'v_b_ada': out['v_b_ada'], 'v_norm_g': out['v_norm_g'], 'v_w_in': out['v_w_in'], 'v_attn_sinks': out['v_attn_sinks'], 'v_conv_w': out['v_conv_w'], 'v_conv_b': out['v_conv_b'], 'v_rg_wa': out['v_rg_wa'], 'v_rg_ba': out['v_rg_ba'], 'v_rg_wx': out['v_rg_wx'], 'v_rg_bx': out['v_rg_bx'], 'v_rg_lambda': out['v_rg_lambda'], 'v_w_attn_proj': out['v_w_attn_proj'], 'v_w_rnn_proj': out['v_w_rnn_proj'], 'v_w_out': out['v_w_out'], 'v_final_g': out['v_final_g']}


def _loss(weights, diff, rest, loss_target):
    with _jax.named_scope("forward"):
        args = {**rest, TWIN_DIFF_INPUT: diff, **{k: w.astype(_WEIGHT_DTYPES[k]) for k, w in weights.items()}}
        y = _forward(args)
    with _jax.named_scope("loss_head"):
        err = _jnp.square(y.astype(_jnp.float32) - loss_target)
        return 0.5 * _jnp.sum(_jnp.mean(err, axis=-1)) if err.ndim else 0.5 * err


def _adamw(w, g, m, v):
    m = ADAM_B1 * m + (1.0 - ADAM_B1) * g
    v = ADAM_B2 * v + (1.0 - ADAM_B2) * _jnp.square(g)
    m_hat = m / (1.0 - ADAM_B1 ** ADAM_STEP)
    v_hat = v / (1.0 - ADAM_B2 ** ADAM_STEP)
    delta = -ADAM_LR * (m_hat / (_jnp.sqrt(v_hat) + ADAM_EPS) + ADAM_WD * w)
    return delta, m, v


def reference(x, c, positions, w_ada, b_ada, norm_g, w_in, attn_sinks, conv_w, conv_b, rg_wa, rg_ba, rg_wx, rg_bx, rg_lambda, w_attn_proj, w_rnn_proj, w_out, final_g, loss_target, m_w_ada, m_b_ada, m_norm_g, m_w_in, m_attn_sinks, m_conv_w, m_conv_b, m_rg_wa, m_rg_ba, m_rg_wx, m_rg_bx, m_rg_lambda, m_w_attn_proj, m_w_rnn_proj, m_w_out, m_final_g, v_w_ada, v_b_ada, v_norm_g, v_w_in, v_attn_sinks, v_conv_w, v_conv_b, v_rg_wa, v_rg_ba, v_rg_wx, v_rg_bx, v_rg_lambda, v_w_attn_proj, v_w_rnn_proj, v_w_out, v_final_g):
    given = dict(x=x, c=c, positions=positions, w_ada=w_ada, b_ada=b_ada, norm_g=norm_g, w_in=w_in, attn_sinks=attn_sinks, conv_w=conv_w, conv_b=conv_b, rg_wa=rg_wa, rg_ba=rg_ba, rg_wx=rg_wx, rg_bx=rg_bx, rg_lambda=rg_lambda, w_attn_proj=w_attn_proj, w_rnn_proj=w_rnn_proj, w_out=w_out, final_g=final_g, loss_target=loss_target, m_w_ada=m_w_ada, m_b_ada=m_b_ada, m_norm_g=m_norm_g, m_w_in=m_w_in, m_attn_sinks=m_attn_sinks, m_conv_w=m_conv_w, m_conv_b=m_conv_b, m_rg_wa=m_rg_wa, m_rg_ba=m_rg_ba, m_rg_wx=m_rg_wx, m_rg_bx=m_rg_bx, m_rg_lambda=m_rg_lambda, m_w_attn_proj=m_w_attn_proj, m_w_rnn_proj=m_w_rnn_proj, m_w_out=m_w_out, m_final_g=m_final_g, v_w_ada=v_w_ada, v_b_ada=v_b_ada, v_norm_g=v_norm_g, v_w_in=v_w_in, v_attn_sinks=v_attn_sinks, v_conv_w=v_conv_w, v_conv_b=v_conv_b, v_rg_wa=v_rg_wa, v_rg_ba=v_rg_ba, v_rg_wx=v_rg_wx, v_rg_bx=v_rg_bx, v_rg_lambda=v_rg_lambda, v_w_attn_proj=v_w_attn_proj, v_w_rnn_proj=v_w_rnn_proj, v_w_out=v_w_out, v_final_g=v_final_g)
    weights = {n: given[n] for n in TWIN_WEIGHTS}
    shared = {n: given[n] for n in SHARED_INPUTS}
    per_example = {n: given[n] for n in ['x', 'c', 'positions']}
    grad_fn = _jax.value_and_grad(_loss, argnums=(0, 1))

    def one_microbatch(ex, loss_target):
        ex = dict(ex)
        diff = ex.pop(TWIN_DIFF_INPUT)
        return grad_fn(weights, diff, {**shared, **ex}, loss_target)

    if N_MICROBATCH == 1:
        loss, (grad_w, grad_x) = one_microbatch(per_example, given["loss_target"])
    else:
        def body(carry, xs):
            loss_sum, grad_sum = carry
            l_k, (gw_k, gx_k) = one_microbatch(xs[0], xs[1])
            with _jax.named_scope("update"):
                return (loss_sum + l_k, _jax.tree.map(_jnp.add, grad_sum, gw_k)), gx_k

        init = (_jnp.zeros((), _jnp.float32), _jax.tree.map(_jnp.zeros_like, weights))
        (loss, grad_w), grad_x = _jax.lax.scan(body, init, (per_example, given["loss_target"]))
    with _jax.named_scope("update"):
        delta_w, new_m, new_v = {}, {}, {}
        for n in TWIN_WEIGHTS:
            delta_w[n], new_m[n], new_v[n] = _adamw(weights[n], grad_w[n], given["m_" + n], given["v_" + n])
    return (loss, grad_x, *[grad_w[n] for n in TWIN_WEIGHTS], *[delta_w[n] for n in TWIN_WEIGHTS],
            *[new_m[n] for n in TWIN_WEIGHTS], *[new_v[n] for n in TWIN_WEIGHTS])
```

```python
import functools
import math

import jax
import jax.numpy as jnp
import numpy as np
from jax import lax
from jax.experimental import pallas as pl
from jax.experimental.pallas import tpu as pltpu

F32 = jnp.float32
BF16 = jnp.bfloat16

D = 1024
HEAD_DIM = 64
ROT_DIM = 16
ROPE_THETA = 500000.0
WINDOW = 128
LRU_C = 8.0
CONV_WIDTH = 4
NORM_EPS = 1e-6
IN_W = 6656
SEG_Q, SEG_K, SEG_V, SEG_GA, SEG_XR, SEG_GR, SEG_MA, SEG_MR = 0, 1024, 1280, 1536, 2560, 3584, 4608, 5632
N_DEV = 8
W_IN_SHARD = IN_W // N_DEV
ROW_SHARD = D // N_DEV
GATE_SHARD = 256 // N_DEV

ADAM_LR = 0.001
ADAM_B1 = 0.9
ADAM_B2 = 0.999
ADAM_EPS = 1e-08
ADAM_WD = 0.01
ADAM_STEP = 10

VMEM_LIMIT = 60 * 1024 * 1024
MESH = pl.DeviceIdType.MESH
AXES = ("x", "y", "c")


def _dot(a, b):
    return jnp.dot(a, b, preferred_element_type=F32)


def _dot_nt(a, b):
    return lax.dot_general(a, b, (((1,), (1,)), ((), ())), preferred_element_type=F32)


def _dot_tn(a, b):
    return lax.dot_general(a, b, (((0,), (0,)), ((), ())), preferred_element_type=F32)


def _sigmoid(z):
    return 1.0 / (1.0 + jnp.exp(-z))


def _params(n_grid, vmem=VMEM_LIMIT):
    return pltpu.CompilerParams(dimension_semantics=("arbitrary",) * n_grid, vmem_limit_bytes=vmem)


def _position():
    return lax.axis_index("x"), lax.axis_index("y"), lax.axis_index("c")


def _flip(pos, k):
    x, y, c = pos
    return (1 - x if k & 4 else x, 1 - y if k & 2 else y, 1 - c if k & 1 else c)


def _slot(pos):
    return 4 * pos[0] + 2 * pos[1] + pos[2]


def _all_gather(name, arrs):
    n = len(arrs)

    def body(*refs):
        ins, outs = refs[:n], refs[n : 2 * n]
        send_sems, recv_sems, local_sems = refs[2 * n :]
        me = _position()
        sibling = _flip(me, 1)
        chips = [2, 4, 6]

        def copy(a, k, block, to, src=None):
            return pltpu.make_async_remote_copy(
                src_ref=outs[a].at[_slot(block)] if src is None else src,
                dst_ref=outs[a].at[_slot(block)],
                send_sem=send_sems.at[a * 7 + k],
                recv_sem=recv_sems.at[a * 7 + k],
                device_id=to,
                device_id_type=MESH,
            )

        mine = [pltpu.make_async_copy(ins[a], outs[a].at[_slot(me)], local_sems.at[a]) for a in range(n)]
        for cp in mine:
            cp.start()
        first = []
        for a in range(n):
            first.append(copy(a, 0, me, sibling, src=ins[a]))
            first += [copy(a, 1 + j, me, _flip(me, k), src=ins[a]) for j, k in enumerate(chips)]
        for cp in first:
            cp.start()
        passed = []
        for a in range(n):
            for j, k in enumerate(chips):
                copy(a, 1 + j, _flip(me, k), me).wait_recv()
                fwd = copy(a, 4 + j, _flip(me, k), sibling)
                fwd.start()
                passed.append(fwd)
        for a in range(n):
            copy(a, 0, sibling, me).wait_recv()
            for j, k in enumerate(chips):
                copy(a, 4 + j, _flip(sibling, k), me).wait_recv()
        for cp in first + passed:
            cp.wait_send()
        for cp in mine:
            cp.wait()

    any_spec = pl.BlockSpec(memory_space=pl.ANY)
    return pl.pallas_call(
        body,
        name=name,
        out_shape=[jax.ShapeDtypeStruct((N_DEV,) + a.shape, a.dtype) for a in arrs],
        in_specs=[any_spec] * n,
        out_specs=[any_spec] * n,
        scratch_shapes=[
            pltpu.SemaphoreType.DMA((7 * n,)),
            pltpu.SemaphoreType.DMA((7 * n,)),
            pltpu.SemaphoreType.DMA((n,)),
        ],
    )(*arrs)


def _all_to_all(name, arrs, views, out_shapes):
    n = len(arrs)

    def body(*refs):
        ins, outs = refs[:n], refs[n : 2 * n]
        send_sems, recv_sems, local_sems = refs[2 * n :]
        me = _position()
        my_slot = _slot(me)
        mine = [pltpu.make_async_copy(views[a](ins[a], my_slot), outs[a].at[my_slot], local_sems.at[a]) for a in range(n)]
        for cp in mine:
            cp.start()
        sent = []
        for a in range(n):
            for k in range(1, N_DEV):
                peer = _flip(me, k)
                cp = pltpu.make_async_remote_copy(
                    src_ref=views[a](ins[a], _slot(peer)),
                    dst_ref=outs[a].at[my_slot],
                    send_sem=send_sems.at[a * 7 + k - 1],
                    recv_sem=recv_sems.at[a * 7 + k - 1],
                    device_id=peer,
                    device_id_type=MESH,
                )
                cp.start()
                sent.append(cp)
        for a in range(n):
            for k in range(1, N_DEV):
                peer = _flip(me, k)
                pltpu.make_async_remote_copy(
                    src_ref=views[a](ins[a], my_slot),
                    dst_ref=outs[a].at[_slot(peer)],
                    send_sem=send_sems.at[a * 7 + k - 1],
                    recv_sem=recv_sems.at[a * 7 + k - 1],
                    device_id=peer,
                    device_id_type=MESH,
                ).wait_recv()
        for cp in sent:
            cp.wait_send()
        for cp in mine:
            cp.wait()

    any_spec = pl.BlockSpec(memory_space=pl.ANY)
    return pl.pallas_call(
        body,
        name=name,
        out_shape=[jax.ShapeDtypeStruct((N_DEV,) + tuple(s), a.dtype) for s, a in zip(out_shapes, arrs)],
        in_specs=[any_spec] * n,
        out_specs=[any_spec] * n,
        scratch_shapes=[
            pltpu.SemaphoreType.DMA((7 * n,)),
            pltpu.SemaphoreType.DMA((7 * n,)),
            pltpu.SemaphoreType.DMA((n,)),
        ],
    )(*arrs)


def _ada_mod(c_all, w_ada, b_cols):
    def body(c_ref, w_ref, b_ref, o_ref):
        o_ref[...] = _dot(c_ref[...].astype(BF16), w_ref[...].astype(BF16)) + b_ref[...]

    return pl.pallas_call(
        body,
        name="ada_mod",
        out_shape=jax.ShapeDtypeStruct((N_DEV, w_ada.shape[1]), F32),
        compiler_params=pltpu.CompilerParams(vmem_limit_bytes=VMEM_LIMIT),
    )(c_all, w_ada, b_cols)


def _fwd_in(x, mod, norm_g, w_t):
    t = x.shape[0]
    tm = min(512, t)
    tn = IN_W // 4

    def body(x_ref, mod_ref, ng_ref, w_ref, proj_ref, h_ref):
        @pl.when(pl.program_id(1) == 0)
        def _():
            xv = x_ref[...]
            r = lax.rsqrt(jnp.mean(xv * xv, axis=-1, keepdims=True) + NORM_EPS)
            shift = mod_ref[:, 0:D]
            scale = mod_ref[:, D : 2 * D]
            h_ref[...] = (((xv * r) * ng_ref[...]) * (1.0 + scale) + shift).astype(BF16)

        proj_ref[...] = _dot_nt(h_ref[...], w_ref[...])

    return pl.pallas_call(
        body,
        name="fwd_in",
        grid=(t // tm, IN_W // tn),
        in_specs=[
            pl.BlockSpec((tm, D), lambda i, j: (i, 0)),
            pl.BlockSpec((1, 3 * D), lambda i, j: (0, 0)),
            pl.BlockSpec((1, D), lambda i, j: (0, 0)),
            pl.BlockSpec((tn, D), lambda i, j: (j, 0)),
        ],
        out_specs=[pl.BlockSpec((tm, tn), lambda i, j: (i, j)), pl.BlockSpec((tm, D), lambda i, j: (i, 0))],
        out_shape=[jax.ShapeDtypeStruct((t, IN_W), F32), jax.ShapeDtypeStruct((t, D), BF16)],
        compiler_params=_params(2),
    )(x, mod, norm_g, w_t)


def _rope_tables(positions):
    half = ROT_DIM // 2
    inv_freq = ROPE_THETA ** (-jnp.arange(0, ROT_DIM, 2, dtype=F32) / ROT_DIM)
    ang = positions[:, None].astype(F32) * inv_freq
    cos, sin = jnp.cos(ang), jnp.sin(ang)
    t = positions.shape[0]
    ones = jnp.ones((t, HEAD_DIM - ROT_DIM), F32)
    zeros = jnp.zeros((t, HEAD_DIM - half), F32)
    cos_h = jnp.concatenate([cos, cos, ones], axis=1)
    s1_h = jnp.concatenate([-sin, zeros], axis=1)
    s2_h = jnp.concatenate([jnp.zeros((t, half), F32), sin, jnp.zeros((t, HEAD_DIM - ROT_DIM), F32)], axis=1)
    return tuple(jnp.concatenate([a, a], axis=1) for a in (cos_h, s1_h, s2_h))


def _rope(tv, cos, s1, s2):
    cols = []
    for g in range(tv.shape[1] // 128):
        blk = tv[:, 128 * g : 128 * (g + 1)]
        cols.append(blk * cos + pltpu.roll(blk, 128 - 8, 1) * s1 + pltpu.roll(blk, 8, 1) * s2)
    return cols[0] if len(cols) == 1 else jnp.concatenate(cols, axis=1)


def _rope_transposed(gv, cos, s1, s2):
    cols = []
    for g in range(gv.shape[1] // 128):
        blk = gv[:, 128 * g : 128 * (g + 1)]
        cols.append(blk * cos + pltpu.roll(blk * s1, 8, 1) + pltpu.roll(blk * s2, 128 - 8, 1))
    return cols[0] if len(cols) == 1 else jnp.concatenate(cols, axis=1)


def _stack_heads(blk, j):
    return jnp.concatenate([blk[:, 256 * j + 64 * h : 256 * j + 64 * (h + 1)] for h in range(4)], axis=0)


def _unstack_heads(parts):
    return jnp.concatenate([parts[j][128 * h : 128 * (h + 1), :] for j in range(2) for h in range(4)], axis=1)


def _band_mask(block_index):
    qi = lax.broadcasted_iota(jnp.int32, (128, 256), 0)
    kj = lax.broadcasted_iota(jnp.int32, (128, 256), 1)
    diff = qi + WINDOW - kj
    first_key = jnp.where(block_index == 0, 128, 0)
    band = (diff >= 0) & (diff < WINDOW) & (kj >= first_key)
    return jnp.concatenate([band] * 4, axis=0)


def _sink_column(sink_ref, pair, j):
    return jnp.concatenate([jnp.full((128, 1), sink_ref[0, 8 * pair + 4 * j + h], F32) for h in range(4)], axis=0)


def _softmax_parts(qs, kj, mask, sink):
    s = _dot_nt(qs, kj)
    s = jnp.where(mask, s, -1e30)
    m = jnp.maximum(jnp.max(s, axis=-1, keepdims=True), sink)
    p = jnp.exp(s - m)
    e_sink = jnp.exp(sink - m)
    denom = jnp.sum(p, axis=-1, keepdims=True) + e_sink
    return p, e_sink, denom


def _attn_specs(t, tq):
    return dict(
        q=pl.BlockSpec((tq, 512), lambda p, i: (i, p)),
        k=pl.BlockSpec((t, 128), lambda p, i: (0, SEG_K // 128 + p)),
        v=pl.BlockSpec((t, 128), lambda p, i: (0, SEG_V // 128 + p)),
        tab_q=pl.BlockSpec((tq, 128), lambda p, i: (i, 0)),
        tab_k=pl.BlockSpec((t, 128), lambda p, i: (0, 0)),
        sink=pl.BlockSpec(memory_space=pltpu.SMEM),
        act=pl.BlockSpec((tq, 512), lambda p, i: (i, p)),
    )


def _fill_kv(k_ref, v_ref, ck, s1k, s2k, kbuf, vbuf):
    zeros = jnp.zeros((128, 128), F32)
    kbuf[0:128, :] = zeros
    vbuf[0:128, :] = zeros
    kbuf[128:, :] = _rope(k_ref[...], ck[...], s1k[...], s2k[...])
    vbuf[128:, :] = v_ref[...]


def _attn_fwd(proj, tables, sinks):
    t = proj.shape[0]
    tq = min(512, t)
    nb = tq // 128
    cos, s1, s2 = tables
    sp = _attn_specs(t, tq)

    def body(q_ref, k_ref, v_ref, cq, s1q, s2q, ck, s1k, s2k, sink_ref, o_ref, kbuf, vbuf):
        pair = pl.program_id(0)
        i = pl.program_id(1)

        @pl.when(i == 0)
        def _():
            _fill_kv(k_ref, v_ref, ck, s1k, s2k, kbuf, vbuf)

        def block(b, carry):
            r0 = pl.multiple_of(b * 128, 128)
            gb = i * nb + b
            k0 = pl.multiple_of(gb * 128, 128)
            rows = pl.ds(r0, 128)
            qr = _rope(q_ref[rows, :], cq[rows, :], s1q[rows, :], s2q[rows, :]) * (1.0 / math.sqrt(HEAD_DIM))
            kw = kbuf[pl.ds(k0, 256), :]
            vw = vbuf[pl.ds(k0, 256), :]
            mask = _band_mask(gb)
            parts = []
            for j in range(2):
                qs = _stack_heads(qr, j).astype(BF16)
                kj = kw[:, 64 * j : 64 * (j + 1)].astype(BF16)
                vj = vw[:, 64 * j : 64 * (j + 1)].astype(BF16)
                p, _, denom = _softmax_parts(qs, kj, mask, _sink_column(sink_ref, pair, j))
                parts.append(_dot((p / denom).astype(BF16), vj))
            o_ref[rows, :] = _unstack_heads(parts)
            return carry

        lax.fori_loop(0, nb, block, 0)

    return pl.pallas_call(
        body,
        name="attn_fwd",
        grid=(2, t // tq),
        in_specs=[sp["q"], sp["k"], sp["v"], sp["tab_q"], sp["tab_q"], sp["tab_q"], sp["tab_k"], sp["tab_k"], sp["tab_k"], sp["sink"]],
        out_specs=sp["act"],
        out_shape=jax.ShapeDtypeStruct((t, D), F32),
        scratch_shapes=[pltpu.VMEM((t + 128, 128), F32), pltpu.VMEM((t + 128, 128), F32)],
        compiler_params=_params(2),
    )(proj, proj, proj, cos, s1, s2, cos, s1, s2, sinks)


def _attn_bwd(proj, o, d_o, tables, sinks):
    t = proj.shape[0]
    tq = min(512, t)
    nb = tq // 128
    n_tiles = t // tq
    cos, s1, s2 = tables
    sp = _attn_specs(t, tq)
    scale = 1.0 / math.sqrt(HEAD_DIM)

    def body(q_ref, k_ref, v_ref, o_ref, do_ref, cq, s1q, s2q, ck, s1k, s2k, sink_ref, dq_ref, dk_ref, dv_ref, dsink_ref, kbuf, vbuf, dkacc, dvacc):
        pair = pl.program_id(0)
        i = pl.program_id(1)

        @pl.when(i == 0)
        def _():
            _fill_kv(k_ref, v_ref, ck, s1k, s2k, kbuf, vbuf)
            dkacc[...] = jnp.zeros_like(dkacc)
            dvacc[...] = jnp.zeros_like(dvacc)
            dsink_ref[...] = jnp.zeros_like(dsink_ref)

        def block(b, carry):
            r0 = pl.multiple_of(b * 128, 128)
            gb = i * nb + b
            k0 = pl.multiple_of(gb * 128, 128)
            rows = pl.ds(r0, 128)
            keys = pl.ds(k0, 256)
            cqb, s1b, s2b = cq[rows, :], s1q[rows, :], s2q[rows, :]
            qr = _rope(q_ref[rows, :], cqb, s1b, s2b) * scale
            kw = kbuf[keys, :]
            vw = vbuf[keys, :]
            ob = o_ref[rows, :]
            dob = do_ref[rows, :]
            mask = _band_mask(gb)
            dq_parts, dk_parts, dv_parts = [], [], []
            for j in range(2):
                qs = _stack_heads(qr, j).astype(BF16)
                kj = kw[:, 64 * j : 64 * (j + 1)].astype(BF16)
                vj = vw[:, 64 * j : 64 * (j + 1)].astype(BF16)
                p, e_sink, denom = _softmax_parts(qs, kj, mask, _sink_column(sink_ref, pair, j))
                inv = 1.0 / denom
                prob = p * inv
                do_s = _stack_heads(dob, j)
                delta = jnp.sum(do_s * _stack_heads(ob, j), axis=-1, keepdims=True)
                do_b = do_s.astype(BF16)
                dp = _dot_nt(do_b, vj)
                ds = (prob * (dp - delta)).astype(BF16)
                dq_parts.append(_dot(ds, kj) * scale)
                dk_parts.append(_dot_tn(ds, qs))
                dv_parts.append(_dot_tn(prob.astype(BF16), do_b))
                dsink = -(e_sink * inv) * delta
                for h in range(4):
                    row = 4 * j + h
                    total = jnp.sum(dsink[128 * h : 128 * (h + 1), :], axis=0, keepdims=True)
                    dsink_ref[0, row : row + 1, :] = dsink_ref[0, row : row + 1, :] + jnp.broadcast_to(total, (1, 128))
            dq_ref[rows, :] = _rope_transposed(_unstack_heads(dq_parts), cqb, s1b, s2b).astype(BF16)
            dkacc[keys, :] = dkacc[keys, :] + jnp.concatenate(dk_parts, axis=1)
            dvacc[keys, :] = dvacc[keys, :] + jnp.concatenate(dv_parts, axis=1)
            return carry

        lax.fori_loop(0, nb, block, 0)

        @pl.when(i == n_tiles - 1)
        def _():
            dk_ref[...] = _rope_transposed(dkacc[128:, :], ck[...], s1k[...], s2k[...]).astype(BF16)
            dv_ref[...] = dvacc[128:, :].astype(BF16)

    kv_out = pl.BlockSpec((t, 128), lambda p, i: (0, p))
    return pl.pallas_call(
        body,
        name="attn_bwd",
        grid=(2, n_tiles),
        in_specs=[sp["q"], sp["k"], sp["v"], sp["act"], sp["act"], sp["tab_q"], sp["tab_q"], sp["tab_q"], sp["tab_k"], sp["tab_k"], sp["tab_k"], sp["sink"]],
        out_specs=[sp["act"], kv_out, kv_out, pl.BlockSpec((1, 8, 128), lambda p, i: (p, 0, 0))],
        out_shape=[
            jax.ShapeDtypeStruct((t, D), BF16),
            jax.ShapeDtypeStruct((t, 256), BF16),
            jax.ShapeDtypeStruct((t, 256), BF16),
            jax.ShapeDtypeStruct((2, 8, 128), F32),
        ],
        scratch_shapes=[pltpu.VMEM((t + 128, 128), F32)] * 4,
        compiler_params=_params(2),
    )(proj, proj, proj, o, d_o, cos, s1, s2, cos, s1, s2, sinks)


HALF = 512


def _softplus(y):
    u = jnp.exp(-jnp.abs(y))
    w = 1.0 + u
    log1p = jnp.where(w == 1.0, u, jnp.log(w) * (u / jnp.where(w == 1.0, 1.0, w - 1.0)))
    return jnp.maximum(y, 0.0) + log1p


def _gate_matmul(act_bf, w_ref, which):
    return jnp.concatenate([_dot(act_bf[:, 256 * blk : 256 * (blk + 1)], w_ref[which, blk]) for blk in range(2)], axis=1)


def _gate_matmul_t(grad_bf, w_ref, which):
    return jnp.concatenate([_dot_nt(grad_bf[:, 256 * blk : 256 * (blk + 1)], w_ref[which, blk]) for blk in range(2)], axis=1)


def _conv_and_gates(xe_ref, tm, cw_ref, cb_ref, w_ref, ba_ref, bx_ref, lam_ref, keep):
    xc = cb_ref[...] + sum(cw_ref[k : k + 1, :] * xe_ref[pl.ds(5 + k, tm), :] for k in range(CONV_WIDTH))
    xc_bf = xc.astype(BF16)
    r = _sigmoid(_gate_matmul(xc_bf, w_ref, 0) + ba_ref[...])
    ig = _sigmoid(_gate_matmul(xc_bf, w_ref, 1) + bx_ref[...])
    sp = _softplus(-lam_ref[...])
    log_a = -LRU_C * r * sp
    tanh = jnp.tanh(-log_a)
    mult = jnp.sqrt(2.0 * tanh / (1.0 + tanh))
    a = jnp.where(keep, jnp.exp(log_a), 0.0)
    mult = jnp.where(keep, mult, 1.0)
    return xc, xc_bf, r, ig, sp, a, mult


def _scan_rows(a_ref, b_ref, h_ref, carry_ref, tm, reverse):
    width = a_ref.shape[1]
    row = lax.broadcasted_iota(jnp.int32, (8, width), 0)

    def group(g, carry):
        gg = tm // 8 - 1 - g if reverse else g
        r0 = pl.multiple_of(gg * 8, 8)
        a = a_ref[pl.ds(r0, 8), :]
        b = b_ref[pl.ds(r0, 8), :]
        for s in (1, 2, 4):
            shift = 8 - s if reverse else s
            a_sh = pltpu.roll(a, shift, 0)
            b_sh = pltpu.roll(b, shift, 0)
            ok = (row < 8 - s) if reverse else (row >= s)
            b = jnp.where(ok, a * b_sh + b, b)
            a = jnp.where(ok, a * a_sh, a)
        h = a * carry + b
        h_ref[pl.ds(r0, 8), :] = h
        last = h[0:1, :] if reverse else h[7:8, :]
        return jnp.broadcast_to(last, (8, width))

    carry_ref[...] = lax.fori_loop(0, tm // 8, group, carry_ref[...])


def _rnn_specs(t, tm, order):
    n = t // tm
    return dict(
        tile=pl.BlockSpec((tm, HALF), lambda ch, i: (order(i, n), ch)),
        before=lambda col0: pl.BlockSpec((8, HALF), lambda ch, i: (jnp.maximum(order(i, n) * (tm // 8) - 1, 0), col0 + ch)),
        proj_tile=lambda col0: pl.BlockSpec((tm, HALF), lambda ch, i: (order(i, n), col0 + ch)),
        keep=pl.BlockSpec((tm, 1), lambda ch, i: (order(i, n), 0)),
        gates=pl.BlockSpec((2, 2, 256, 256), lambda ch, i: (0, ch, 0, 0)),
        vec=pl.BlockSpec((1, HALF), lambda ch, i: (0, ch)),
        conv=pl.BlockSpec((CONV_WIDTH, HALF), lambda ch, i: (0, ch)),
    )


def _rnn_fwd(proj, keep, gates_bf, rg_ba, rg_bx, rg_lambda, conv_w, conv_b):
    t = proj.shape[0]
    tm = min(512, t)
    sp = _rnn_specs(t, tm, lambda i, n: i)

    def body(xr_ref, xb_ref, keep_ref, w_ref, ba_ref, bx_ref, lam_ref, cw_ref, cb_ref, hh_ref, xe, a_buf, b_buf, carry):
        i = pl.program_id(1)

        @pl.when(i == 0)
        def _():
            carry[...] = jnp.zeros_like(carry)

        xe[0:8, :] = jnp.where(i > 0, xb_ref[...], 0.0)
        xe[8:, :] = xr_ref[...]
        keep_rows = keep_ref[...] != 0.0
        xc, _, _, ig, _, a, mult = _conv_and_gates(xe, tm, cw_ref, cb_ref, w_ref, ba_ref, bx_ref, lam_ref, keep_rows)
        a_buf[...] = a
        b_buf[...] = mult * (ig * xc)
        _scan_rows(a_buf, b_buf, hh_ref, carry, tm, reverse=False)

    return pl.pallas_call(
        body,
        name="rnn_fwd",
        grid=(2, t // tm),
        in_specs=[sp["proj_tile"](SEG_XR // HALF), sp["before"](SEG_XR // HALF), sp["keep"], sp["gates"], sp["vec"], sp["vec"], sp["vec"], sp["conv"], sp["vec"]],
        out_specs=sp["tile"],
        out_shape=jax.ShapeDtypeStruct((t, D), F32),
        scratch_shapes=[pltpu.VMEM((tm + 8, HALF), F32), pltpu.VMEM((tm, HALF), F32), pltpu.VMEM((tm, HALF), F32), pltpu.VMEM((8, HALF), F32)],
        compiler_params=_params(2),
    )(proj, proj, keep, gates_bf, rg_ba, rg_bx, rg_lambda, conv_w, conv_b)


def _rnn_bwd(proj, hh, d_hh, keep, gates_bf, rg_ba, rg_bx, rg_lambda, conv_w, conv_b):
    t = proj.shape[0]
    tm = min(512, t)
    n_tiles = t // tm
    sp = _rnn_specs(t, tm, lambda i, n: n - 1 - i)

    def body(xr_ref, xb_ref, hh_ref, hb_ref, dhh_ref, keep_ref, w_ref, ba_ref, bx_ref, lam_ref, cw_ref, cb_ref,
             dxr_ref, gw_ref, gvec_ref, xe, he, ae, c_buf, g_buf, dxe, carry, a_next):
        i = pl.program_id(1)
        first_tile = i == n_tiles - 1

        @pl.when(i == 0)
        def _():
            carry[...] = jnp.zeros_like(carry)
            a_next[...] = jnp.zeros_like(a_next)
            dxe[tm:, :] = jnp.zeros((8, HALF), F32)
            gw_ref[...] = jnp.zeros_like(gw_ref)
            gvec_ref[...] = jnp.zeros_like(gvec_ref)

        xe[0:8, :] = jnp.where(first_tile, 0.0, xb_ref[...])
        xe[8:, :] = xr_ref[...]
        he[0:8, :] = jnp.where(first_tile, 0.0, hb_ref[...])
        he[8:, :] = hh_ref[...]
        keep_rows = keep_ref[...] != 0.0
        xc, xc_bf, r, ig, sp_, a, mult = _conv_and_gates(xe, tm, cw_ref, cb_ref, w_ref, ba_ref, bx_ref, lam_ref, keep_rows)

        ae[0:tm, :] = a
        ae[tm:, :] = a_next[...]
        a_next[...] = jnp.broadcast_to(a[0:1, :], (8, HALF))
        c_buf[...] = ae[pl.ds(1, tm), :]
        g_buf[...] = dhh_ref[...]
        _scan_rows(c_buf, g_buf, g_buf, carry, tm, reverse=True)
        g = g_buf[...]

        h_prev = he[pl.ds(7, tm), :]
        bb = ig * xc
        d_ig = g * mult * xc
        d_xc = g * mult * ig
        d_log_a = jnp.where(keep_rows, g * h_prev * a - (g * bb) * (a * a) / mult, 0.0)
        d_r = d_log_a * (-LRU_C * sp_)
        d_sp = jnp.sum(d_log_a * (-LRU_C * r), axis=0, keepdims=True)
        d_zr = d_r * r * (1.0 - r)
        d_zi = d_ig * ig * (1.0 - ig)
        d_zr_bf = d_zr.astype(BF16)
        d_zi_bf = d_zi.astype(BF16)
        d_xc = d_xc + _gate_matmul_t(d_zr_bf, w_ref, 0) + _gate_matmul_t(d_zi_bf, w_ref, 1)
        for blk in range(2):
            cols = slice(256 * blk, 256 * (blk + 1))
            gw_ref[0, blk] = gw_ref[0, blk] + _dot_tn(xc_bf[:, cols], d_zr_bf[:, cols])
            gw_ref[1, blk] = gw_ref[1, blk] + _dot_tn(xc_bf[:, cols], d_zi_bf[:, cols])

        dxe[0:tm, :] = d_xc
        dxr = sum(cw_ref[k : k + 1, :] * dxe[pl.ds(3 - k, tm), :] for k in range(CONV_WIDTH))
        dxr_ref[...] = dxr.astype(BF16)
        dxe[tm:, :] = d_xc[0:8, :]

        lam = lam_ref[...]
        sums = [
            jnp.sum(d_zr, axis=0, keepdims=True),
            jnp.sum(d_zi, axis=0, keepdims=True),
            d_sp * (-_sigmoid(-lam)),
            jnp.sum(d_xc, axis=0, keepdims=True),
        ] + [jnp.sum(d_xc * xe[pl.ds(5 + k, tm), :], axis=0, keepdims=True) for k in range(CONV_WIDTH)]
        gvec_ref[...] = gvec_ref[...] + jnp.concatenate(sums, axis=0)

    return pl.pallas_call(
        body,
        name="rnn_bwd",
        grid=(2, n_tiles),
        in_specs=[
            sp["proj_tile"](SEG_XR // HALF), sp["before"](SEG_XR // HALF), sp["tile"], sp["before"](0), sp["tile"], sp["keep"],
            sp["gates"], sp["vec"], sp["vec"], sp["vec"], sp["conv"], sp["vec"],
        ],
        out_specs=[sp["tile"], sp["gates"], pl.BlockSpec((8, HALF), lambda ch, i: (0, ch))],
        out_shape=[
            jax.ShapeDtypeStruct((t, D), BF16),
            jax.ShapeDtypeStruct((2, 4, 256, 256), F32),
            jax.ShapeDtypeStruct((8, D), F32),
        ],
        scratch_shapes=[
            pltpu.VMEM((tm + 8, HALF), F32),
            pltpu.VMEM((tm + 8, HALF), F32),
            pltpu.VMEM((tm + 8, HALF), F32),
            pltpu.VMEM((tm, HALF), F32),
            pltpu.VMEM((tm, HALF), F32),
            pltpu.VMEM((tm + 8, HALF), F32),
            pltpu.VMEM((8, HALF), F32),
            pltpu.VMEM((8, HALF), F32),
        ],
        compiler_params=_params(2),
    )(proj, proj, hh, hh, d_hh, keep, gates_bf, rg_ba, rg_bx, rg_lambda, conv_w, conv_b)


def _mid(x, target, o, hh, proj, w3_bf, gate, final_g):
    t = x.shape[0]
    tm = min(256, t)
    n_tiles = t // tm
    full = pl.BlockSpec((tm, D), lambda i: (i, 0))
    half = lambda col: pl.BlockSpec((tm, 512), lambda i: (i, col))
    vec = pl.BlockSpec((1, D), lambda i: (0, 0))
    any_spec = pl.BlockSpec(memory_space=pl.ANY)

    def body(x_ref, tg_ref, o_ref, hh_ref, ga0, ga1, gr0, gr1, ma0, ma1, mr0, mr1, w_hbm, gate_ref, fg_ref,
             dres_ref, do_ref, dhh_ref, dga_ref, dgr_ref, dma_ref, dmr_ref, gw_hbm, vec_ref, w, gw, sem):
        i = pl.program_id(0)

        @pl.when(i == 0)
        def _():
            cp = pltpu.make_async_copy(w_hbm, w, sem)
            cp.start()
            cp.wait()
            gw[...] = jnp.zeros_like(gw)
            vec_ref[...] = jnp.zeros_like(vec_ref)

        cat = lambda a, b: jnp.concatenate([a[...], b[...]], axis=1)
        ga, gr, ma, mr = cat(ga0, ga1), cat(gr0, gr1), cat(ma0, ma1), cat(mr0, mr1)
        ov, hv, gate_v, fg = o_ref[...], hh_ref[...], gate_ref[...], fg_ref[...]
        sg_a = _sigmoid(ga)
        sg_r = _sigmoid(gr)
        silu_a = ga * sg_a
        silu_r = gr * sg_r
        ya = (ov * silu_a).astype(BF16)
        yr = (hv * silu_r).astype(BF16)
        pa = _dot(ya, w[0])
        pr = _dot(yr, w[1])
        sa = _sigmoid(ma)
        sr = _sigmoid(mr)
        merged = (sa * pa + sr * pr).astype(BF16)
        mo = _dot(merged, w[2])
        out = x_ref[...] + gate_v * mo
        r2 = lax.rsqrt(jnp.mean(out * out, axis=-1, keepdims=True) + NORM_EPS)
        outn = out * r2
        err = outn * fg - tg_ref[...]
        loss = 0.5 * jnp.sum(jnp.mean(err * err, axis=-1, keepdims=True), axis=0, keepdims=True)
        dy = err * (1.0 / D)
        g_final = jnp.sum(dy * outn, axis=0, keepdims=True)
        doutn = dy * fg
        dout = r2 * (doutn - outn * jnp.mean(doutn * outn, axis=-1, keepdims=True))
        dres_ref[...] = dout
        d_gate = jnp.sum(dout * mo, axis=0, keepdims=True)
        dmo = (dout * gate_v).astype(BF16)
        dmerged = _dot_nt(dmo, w[2])
        gw[2] = gw[2] + _dot_tn(merged, dmo)
        dpa = (dmerged * sa).astype(BF16)
        dpr = (dmerged * sr).astype(BF16)
        dma_ref[...] = (dmerged * pa * (sa * (1.0 - sa))).astype(BF16)
        dmr_ref[...] = (dmerged * pr * (sr * (1.0 - sr))).astype(BF16)
        dya = _dot_nt(dpa, w[0])
        dyr = _dot_nt(dpr, w[1])
        gw[0] = gw[0] + _dot_tn(ya, dpa)
        gw[1] = gw[1] + _dot_tn(yr, dpr)
        do_ref[...] = dya * silu_a
        dhh_ref[...] = dyr * silu_r
        dga_ref[...] = (dya * ov * (sg_a * (1.0 + ga * (1.0 - sg_a)))).astype(BF16)
        dgr_ref[...] = (dyr * hv * (sg_r * (1.0 + gr * (1.0 - sg_r)))).astype(BF16)
        zeros = jnp.zeros((5, D), F32)
        vec_ref[...] = vec_ref[...] + jnp.concatenate([g_final, d_gate, jnp.broadcast_to(loss, (1, D)), zeros], axis=0)

        @pl.when(i == n_tiles - 1)
        def _():
            cp = pltpu.make_async_copy(gw, gw_hbm, sem)
            cp.start()
            cp.wait()

    bf_tile = jax.ShapeDtypeStruct((t, D), BF16)
    f_tile = jax.ShapeDtypeStruct((t, D), F32)
    return pl.pallas_call(
        body,
        name="mid",
        grid=(n_tiles,),
        in_specs=[full, full, full, full,
                  half(SEG_GA // 512), half(SEG_GA // 512 + 1), half(SEG_GR // 512), half(SEG_GR // 512 + 1),
                  half(SEG_MA // 512), half(SEG_MA // 512 + 1), half(SEG_MR // 512), half(SEG_MR // 512 + 1),
                  any_spec, vec, vec],
        out_specs=[full, full, full, full, full, full, full, any_spec, pl.BlockSpec((8, D), lambda i: (0, 0))],
        out_shape=[f_tile, f_tile, f_tile, bf_tile, bf_tile, bf_tile, bf_tile,
                   jax.ShapeDtypeStruct((3, D, D), F32), jax.ShapeDtypeStruct((8, D), F32)],
        scratch_shapes=[pltpu.VMEM((3, D, D), BF16), pltpu.VMEM((3, D, D), F32), pltpu.SemaphoreType.DMA],
        compiler_params=_params(1),
    )(x, target, o, hh, proj, proj, proj, proj, proj, proj, proj, proj, w3_bf, gate, final_g)


_SEGMENTS = ((SEG_Q, 1024), (SEG_K, 256), (SEG_V, 256), (SEG_GA, 1024), (SEG_XR, 1024), (SEG_GR, 1024), (SEG_MA, 1024), (SEG_MR, 1024))


def _seg_specs(tm):
    return [pl.BlockSpec((tm, width), lambda i: (i, 0)) for _, width in _SEGMENTS]


def _bwd_dh(dsegs, w_t, x, dres, mod, norm_g):
    t = x.shape[0]
    tm = min(256, t)
    n_seg = len(_SEGMENTS)
    full = pl.BlockSpec((tm, D), lambda i: (i, 0))

    def body(*refs):
        segs = refs[:n_seg]
        w_hbm, x_ref, dres_ref, mod_ref, ng_ref, gx_ref, vec_ref, w, sem = refs[n_seg:]
        i = pl.program_id(0)

        @pl.when(i == 0)
        def _():
            cp = pltpu.make_async_copy(w_hbm, w, sem)
            cp.start()
            cp.wait()
            vec_ref[...] = jnp.zeros_like(vec_ref)

        dh = sum(_dot(seg[...], w[start : start + width, :]) for seg, (start, width) in zip(segs, _SEGMENTS))
        xv = x_ref[...]
        r1 = lax.rsqrt(jnp.mean(xv * xv, axis=-1, keepdims=True) + NORM_EPS)
        xn = xv * r1
        ng = ng_ref[...]
        scale1 = 1.0 + mod_ref[:, D : 2 * D]
        d_shift = jnp.sum(dh, axis=0, keepdims=True)
        d_scale = jnp.sum(dh * (xn * ng), axis=0, keepdims=True)
        g_norm = jnp.sum(dh * xn * scale1, axis=0, keepdims=True)
        dxn = dh * (ng * scale1)
        dx = r1 * (dxn - xn * jnp.mean(dxn * xn, axis=-1, keepdims=True))
        gx_ref[...] = dres_ref[...] + dx
        vec_ref[...] = vec_ref[...] + jnp.concatenate([d_shift, d_scale, g_norm, jnp.zeros((5, D), F32)], axis=0)

    return pl.pallas_call(
        body,
        name="bwd_dh",
        grid=(t // tm,),
        in_specs=_seg_specs(tm) + [pl.BlockSpec(memory_space=pl.ANY), full, full,
                                   pl.BlockSpec((1, 3 * D), lambda i: (0, 0)), pl.BlockSpec((1, D), lambda i: (0, 0))],
        out_specs=[full, pl.BlockSpec((8, D), lambda i: (0, 0))],
        out_shape=[jax.ShapeDtypeStruct((t, D), F32), jax.ShapeDtypeStruct((8, D), F32)],
        scratch_shapes=[pltpu.VMEM((IN_W, D), BF16), pltpu.SemaphoreType.DMA],
        compiler_params=_params(1),
    )(*dsegs, w_t, x, dres, mod, norm_g)


def _bwd_gw(dsegs, h):
    t = h.shape[0]
    tm = min(512, t)
    n_tiles = t // tm
    n_seg = len(_SEGMENTS)

    def body(*refs):
        segs = refs[:n_seg]
        h_ref, out_hbm, acc, sem = refs[n_seg:]
        i = pl.program_id(0)

        @pl.when(i == 0)
        def _():
            acc[...] = jnp.zeros_like(acc)

        hv = h_ref[...]
        for seg, (start, width) in zip(segs, _SEGMENTS):
            acc[start : start + width, :] = acc[start : start + width, :] + _dot_tn(seg[...], hv)

        @pl.when(i == n_tiles - 1)
        def _():
            cp = pltpu.make_async_copy(acc, out_hbm, sem)
            cp.start()
            cp.wait()

    return pl.pallas_call(
        body,
        name="bwd_gw",
        grid=(n_tiles,),
        in_specs=_seg_specs(tm) + [pl.BlockSpec((tm, D), lambda i: (i, 0))],
        out_specs=pl.BlockSpec(memory_space=pl.ANY),
        out_shape=jax.ShapeDtypeStruct((IN_W, D), F32),
        scratch_shapes=[pltpu.VMEM((IN_W, D), F32), pltpu.SemaphoreType.DMA],
        compiler_params=_params(1),
    )(*dsegs, h)


def _adamw(w, g, m, v):
    m = ADAM_B1 * m + (1.0 - ADAM_B1) * g
    v = ADAM_B2 * v + (1.0 - ADAM_B2) * (g * g)
    m_hat = m / (1.0 - ADAM_B1**ADAM_STEP)
    v_hat = v / (1.0 - ADAM_B2**ADAM_STEP)
    delta = -ADAM_LR * (m_hat / (jnp.sqrt(v_hat) + ADAM_EPS) + ADAM_WD * w)
    return delta, m, v


def _row_block(rows):
    return 128 if rows % 128 == 0 else rows


def _sum_slots(name, land):
    _, rows, cols = land.shape
    rb = 208 if rows == W_IN_SHARD else _row_block(rows)

    def body(l_ref, g_ref):
        total = l_ref[0]
        for s in range(1, N_DEV):
            total = total + l_ref[s]
        g_ref[...] = total

    return pl.pallas_call(
        body,
        name=name,
        grid=(rows // rb,),
        in_specs=[pl.BlockSpec((N_DEV, rb, cols), lambda i: (0, i, 0))],
        out_specs=pl.BlockSpec((rb, cols), lambda i: (i, 0)),
        out_shape=jax.ShapeDtypeStruct((rows, cols), F32),
        compiler_params=_params(1),
    )(land)


def _adam_update(name, g, w, m, v):
    rows, cols = w.shape
    rb = _row_block(rows)
    spec = pl.BlockSpec((rb, cols), lambda i: (i, 0))

    def body(g_ref, w_ref, m_ref, v_ref, d_ref, m2_ref, v2_ref):
        d_ref[...], m2_ref[...], v2_ref[...] = _adamw(w_ref[...], g_ref[...], m_ref[...], v_ref[...])

    shape = jax.ShapeDtypeStruct((rows, cols), F32)
    return pl.pallas_call(
        body, name=name, grid=(rows // rb,), in_specs=[spec] * 4, out_specs=[spec] * 3, out_shape=[shape] * 3,
        compiler_params=_params(1),
    )(g, w, m, v)


def _sum_adam(name, land, w, m, v):
    rows, cols = w.shape
    rb = _row_block(rows)
    spec = pl.BlockSpec((rb, cols), lambda i: (i, 0))

    def body(l_ref, w_ref, m_ref, v_ref, g_ref, d_ref, m2_ref, v2_ref):
        total = l_ref[0]
        for s in range(1, N_DEV):
            total = total + l_ref[s]
        g_ref[...] = total
        d_ref[...], m2_ref[...], v2_ref[...] = _adamw(w_ref[...], total, m_ref[...], v_ref[...])

    shape = jax.ShapeDtypeStruct((rows, cols), F32)
    return pl.pallas_call(
        body, name=name, grid=(rows // rb,),
        in_specs=[pl.BlockSpec((N_DEV, rb, cols), lambda i: (0, i, 0)), spec, spec, spec],
        out_specs=[spec] * 4, out_shape=[shape] * 4, compiler_params=_params(1),
    )(land, w, m, v)


def _ada_update(c_all_t, dmod_cols, w, m, v):
    def body(c_ref, dm_ref, w_ref, m_ref, v_ref, g_ref, d_ref, m2_ref, v2_ref):
        g = _dot(c_ref[...], dm_ref[...])
        g_ref[...] = g
        d_ref[...], m2_ref[...], v2_ref[...] = _adamw(w_ref[...], g, m_ref[...], v_ref[...])

    shape = jax.ShapeDtypeStruct(w.shape, F32)
    return pl.pallas_call(
        body, name="ada_update", out_shape=[shape] * 4, compiler_params=pltpu.CompilerParams(vmem_limit_bytes=VMEM_LIMIT)
    )(c_all_t, dmod_cols, w, m, v)


def _vec_update(gathered, w, m, v):
    n_rep = w.shape[1]

    def body(ga_ref, w_ref, m_ref, v_ref, g_ref, d_ref, m2_ref, v2_ref):
        total = ga_ref[0:1, :]
        for s in range(1, N_DEV):
            total = total + ga_ref[s : s + 1, :]
        g_ref[...] = total
        d_ref[...], m2_ref[...], v2_ref[...] = _adamw(w_ref[...], total[:, 0:n_rep], m_ref[...], v_ref[...])

    shape = jax.ShapeDtypeStruct(w.shape, F32)
    return pl.pallas_call(
        body, name="vec_update", out_shape=[jax.ShapeDtypeStruct((1, gathered.shape[1]), F32)] + [shape] * 3,
        compiler_params=pltpu.CompilerParams(vmem_limit_bytes=VMEM_LIMIT),
    )(gathered, w, m, v)


def kernel(x, c, positions, w_ada, b_ada, norm_g, w_in, attn_sinks, conv_w, conv_b, rg_wa, rg_ba, rg_wx, rg_bx, rg_lambda, w_attn_proj, w_rnn_proj, w_out, final_g, loss_target, m_w_ada, m_b_ada, m_norm_g, m_w_in, m_attn_sinks, m_conv_w, m_conv_b, m_rg_wa, m_rg_ba, m_rg_wx, m_rg_bx, m_rg_lambda, m_w_attn_proj, m_w_rnn_proj, m_w_out, m_final_g, v_w_ada, v_b_ada, v_norm_g, v_w_in, v_attn_sinks, v_conv_w, v_conv_b, v_rg_wa, v_rg_ba, v_rg_wx, v_rg_bx, v_rg_lambda, v_w_attn_proj, v_w_rnn_proj, v_w_out, v_final_g):
    me = _slot(_position())
    x2, target = x[0], loss_target[0]
    pos = positions[0]

    w_pack = jnp.concatenate(
        [w_in[0].T, w_attn_proj[0], w_rnn_proj[0], w_out[0], rg_wa[0].reshape(GATE_SHARD, D), rg_wx[0].reshape(GATE_SHARD, D)], axis=0
    ).astype(BF16)
    n_conv = conv_w.shape[2]
    conv_rows = jnp.pad(conv_w[0], ((0, 8 - CONV_WIDTH), (0, 0)))
    c_rows, conv_all, w_all = _all_gather("ag_fwd", [jnp.broadcast_to(c, (8, D)), conv_rows, w_pack])
    c_all = c_rows[:, 0, :]
    conv_full = conv_all[:, :CONV_WIDTH, :].transpose(1, 0, 2).reshape(CONV_WIDTH, D)
    r0 = W_IN_SHARD
    w_t = w_all[:, 0:r0, :].reshape(IN_W, D)
    w3_bf = jnp.stack([w_all[:, r0 + ROW_SHARD * k : r0 + ROW_SHARD * (k + 1), :].reshape(D, D) for k in range(3)])
    r1 = r0 + 3 * ROW_SHARD
    gates_bf = jnp.stack(
        [w_all[:, r1 + GATE_SHARD * k : r1 + GATE_SHARD * (k + 1), :].reshape(N_DEV, 4, GATE_SHARD, 256).transpose(1, 0, 2, 3).reshape(4, 256, 256)
         for k in range(2)]
    )

    n_ada = w_ada.shape[2]
    b_cols = lax.dynamic_slice(b_ada, (0, me * n_ada), (1, n_ada))
    mod_cols = _ada_mod(c_all, w_ada[0], b_cols)
    (mod_all,) = _all_gather("ag_mod", [mod_cols])
    mod = lax.dynamic_slice(mod_all, (0, me, 0), (N_DEV, 1, n_ada)).reshape(1, 3 * D)
    gate = mod[:, 2 * D :]

    proj, h = _fwd_in(x2, mod, norm_g, w_t)
    tables = _rope_tables(pos)
    o = _attn_fwd(proj, tables, attn_sinks)
    keep = (pos != 0).astype(F32)[:, None]
    hh = _rnn_fwd(proj, keep, gates_bf, rg_ba, rg_bx, rg_lambda, conv_full, conv_b)

    dres, d_o, d_hh, d_ga, d_gr, d_ma, d_mr, gw3, mid_vec = _mid(x2, target, o, hh, proj, w3_bf, gate, final_g[None, :])
    d_q, d_k, d_v, d_sink = _attn_bwd(proj, o, d_o, tables, attn_sinks)
    d_xr, g_gates, rnn_vec = _rnn_bwd(proj, hh, d_hh, keep, gates_bf, rg_ba, rg_bx, rg_lambda, conv_full, conv_b)
    dsegs = (d_q, d_k, d_v, d_ga, d_xr, d_gr, d_ma, d_mr)
    grad_x, dh_vec = _bwd_dh(dsegs, w_t, x2, dres, mod, norm_g)
    g_w_t = _bwd_gw(dsegs, h)
    loss = lax.psum(mid_vec[2, 0], AXES)

    g_sinks = jnp.pad(d_sink[:, :, 0].reshape(1, 16), ((0, 0), (0, 112)))
    vec = jnp.concatenate(
        [dh_vec[0:1], dh_vec[1:2], mid_vec[1:2],
         dh_vec[2:3], g_sinks, rnn_vec[3:4], rnn_vec[0:1], rnn_vec[1:2], rnn_vec[2:3], mid_vec[0:1],
         rnn_vec[4:8].reshape(1, CONV_WIDTH * D)], axis=1)
    (vec_all,) = _all_gather("ag_vec", [jnp.broadcast_to(vec, (8, vec.shape[1]))])
    vec_all = vec_all[:, 0, :]
    pad16 = lambda a: jnp.pad(a, ((0, 0), (0, 112)))
    rep = lambda b, n, s, cb, ba, bx, lam, fg: jnp.concatenate([b, n, pad16(s), cb, ba, bx, lam, fg[None, :]], axis=1)
    rep_w = rep(b_ada, norm_g, attn_sinks, conv_b, rg_ba, rg_bx, rg_lambda, final_g)
    rep_m = rep(m_b_ada, m_norm_g, m_attn_sinks, m_conv_b, m_rg_ba, m_rg_bx, m_rg_lambda, m_final_g)
    rep_v = rep(v_b_ada, v_norm_g, v_attn_sinks, v_conv_b, v_rg_ba, v_rg_bx, v_rg_lambda, v_final_g)
    g_vec, d_rep, m_rep, v_rep = _vec_update(vec_all, rep_w, rep_m, rep_v)

    def split_rep(a):
        offs = np.cumsum([0, 3 * D, D, 128, D, D, D, D, D])
        parts = [a[:, offs[k] : offs[k + 1]] for k in range(8)]
        parts[2] = parts[2][:, :16]
        parts[7] = parts[7][0]
        return parts

    n_rep = rep_w.shape[1]
    g_conv_full = g_vec[:, n_rep:].reshape(CONV_WIDTH, D)
    n_conv = conv_w.shape[2]
    g_conv = lax.dynamic_slice(g_conv_full, (0, me * n_conv), (CONV_WIDTH, n_conv))

    dmod_all = vec_all[:, 0 : 3 * D]
    dmod_cols = lax.dynamic_slice(dmod_all, (0, me * n_ada), (N_DEV, n_ada))
    g_ada, d_ada, m_ada, v_ada = _ada_update(c_all.T, dmod_cols, w_ada[0], m_w_ada[0], v_w_ada[0])

    land_w, land_3, land_g = _all_to_all(
        "a2a_grads",
        [g_w_t.reshape(N_DEV, W_IN_SHARD, D), gw3, g_gates],
        [lambda ref, p: ref.at[p],
         lambda ref, p: ref.at[:, pl.ds(p * ROW_SHARD, ROW_SHARD), :],
         lambda ref, p: ref.at[:, :, pl.ds(p * GATE_SHARD, GATE_SHARD), :]],
        [(W_IN_SHARD, D), (3, ROW_SHARD, D), (2, 4, GATE_SHARD, 256)],
    )
    g_in = _sum_slots("sum_w_in", land_w).T
    d_in, m_in, v_in = _adam_update("adam_w_in", g_in, w_in[0], m_w_in[0], v_w_in[0])
    w3 = jnp.concatenate([w_attn_proj[0], w_rnn_proj[0], w_out[0]], axis=0)
    m3 = jnp.concatenate([m_w_attn_proj[0], m_w_rnn_proj[0], m_w_out[0]], axis=0)
    v3 = jnp.concatenate([v_w_attn_proj[0], v_w_rnn_proj[0], v_w_out[0]], axis=0)
    g3, d3, m3n, v3n = _sum_adam("adam_proj", land_3.reshape(N_DEV, 3 * ROW_SHARD, D), w3, m3, v3)
    gshape = (2 * 4 * GATE_SHARD, 256)
    wg = jnp.stack([rg_wa[0], rg_wx[0]]).reshape(gshape)
    mg = jnp.stack([m_rg_wa[0], m_rg_wx[0]]).reshape(gshape)
    vg = jnp.stack([v_rg_wa[0], v_rg_wx[0]]).reshape(gshape)
    gg, dg, mgn, vgn = _sum_adam("adam_gates", land_g.reshape((N_DEV,) + gshape), wg, mg, vg)
    d_conv, m_conv, v_conv = _adam_update("adam_conv", g_conv, conv_w[0], m_conv_w[0], v_conv_w[0])

    def by_weight(rep_parts, ada, w_in_, conv, gates, proj3):
        b, n, s, cb, ba, bx, lam, fg = rep_parts
        gates = gates.reshape(2, 1, 4, GATE_SHARD, 256)
        proj3 = proj3.reshape(3, 1, ROW_SHARD, D)
        return [ada[None], b, n, w_in_[None], s, conv[None], cb, gates[0], ba, gates[1], bx, lam, proj3[0], proj3[1], proj3[2], fg]

    grads = by_weight(split_rep(g_vec[:, :n_rep]), g_ada, g_in, g_conv, gg, g3)
    deltas = by_weight(split_rep(d_rep), d_ada, d_in, d_conv, dg, d3)
    new_m = by_weight(split_rep(m_rep), m_ada, m_in, m_conv, mgn, m3n)
    new_v = by_weight(split_rep(v_rep), v_ada, v_in, v_conv, vgn, v3n)
    return (loss, grad_x[None], *grads, *deltas, *new_m, *new_v)
```

```python
import functools
import math

import jax
import jax.numpy as jnp
import numpy as np
from jax import lax
from jax.experimental import pallas as pl
from jax.experimental.pallas import tpu as pltpu

F32 = jnp.float32
BF16 = jnp.bfloat16

D = 1024
HEAD_DIM = 64
ROT_DIM = 16
ROPE_THETA = 500000.0
WINDOW = 128
LRU_C = 8.0
CONV_WIDTH = 4
NORM_EPS = 1e-6
IN_W = 6656
SEG_Q, SEG_K, SEG_V, SEG_GA, SEG_XR, SEG_GR, SEG_MA, SEG_MR = 0, 1024, 1280, 1536, 2560, 3584, 4608, 5632
N_DEV = 8
W_IN_SHARD = IN_W // N_DEV
ROW_SHARD = D // N_DEV
GATE_SHARD = 256 // N_DEV

ADAM_LR = 0.001
ADAM_B1 = 0.9
ADAM_B2 = 0.999
ADAM_EPS = 1e-08
ADAM_WD = 0.01
ADAM_STEP = 10

VMEM_LIMIT = 60 * 1024 * 1024
MESH = pl.DeviceIdType.MESH
AXES = ("x", "y", "c")


def _dot(a, b):
    return jnp.dot(a, b, preferred_element_type=F32)


def _dot_nt(a, b):
    return lax.dot_general(a, b, (((1,), (1,)), ((), ())), preferred_element_type=F32)


def _dot_tn(a, b):
    return lax.dot_general(a, b, (((0,), (0,)), ((), ())), preferred_element_type=F32)


def _sigmoid(z):
    return 1.0 / (1.0 + jnp.exp(-z))


def _params(n_grid, vmem=VMEM_LIMIT):
    return pltpu.CompilerParams(dimension_semantics=("arbitrary",) * n_grid, vmem_limit_bytes=vmem)


def _position():
    return lax.axis_index("x"), lax.axis_index("y"), lax.axis_index("c")


def _flip(pos, k):
    x, y, c = pos
    return (1 - x if k & 4 else x, 1 - y if k & 2 else y, 1 - c if k & 1 else c)


def _slot(pos):
    return 4 * pos[0] + 2 * pos[1] + pos[2]


def _all_gather(name, arrs):
    n = len(arrs)

    def body(*refs):
        ins, outs = refs[:n], refs[n : 2 * n]
        send_sems, recv_sems, local_sems = refs[2 * n :]
        me = _position()
        sibling = _flip(me, 1)
        chips = [2, 4, 6]

        def copy(a, k, block, to, src=None):
            return pltpu.make_async_remote_copy(
                src_ref=outs[a].at[_slot(block)] if src is None else src,
                dst_ref=outs[a].at[_slot(block)],
                send_sem=send_sems.at[a * 7 + k],
                recv_sem=recv_sems.at[a * 7 + k],
                device_id=to,
                device_id_type=MESH,
            )

        mine = [pltpu.make_async_copy(ins[a], outs[a].at[_slot(me)], local_sems.at[a]) for a in range(n)]
        for cp in mine:
            cp.start()
        first = []
        for a in range(n):
            first.append(copy(a, 0, me, sibling, src=ins[a]))
            first += [copy(a, 1 + j, me, _flip(me, k), src=ins[a]) for j, k in enumerate(chips)]
        for cp in first:
            cp.start()
        passed = []
        for a in range(n):
            for j, k in enumerate(chips):
                copy(a, 1 + j, _flip(me, k), me).wait_recv()
                fwd = copy(a, 4 + j, _flip(me, k), sibling)
                fwd.start()
                passed.append(fwd)
        for a in range(n):
            copy(a, 0, sibling, me).wait_recv()
            for j, k in enumerate(chips):
                copy(a, 4 + j, _flip(sibling, k), me).wait_recv()
        for cp in first + passed:
            cp.wait_send()
        for cp in mine:
            cp.wait()

    any_spec = pl.BlockSpec(memory_space=pl.ANY)
    return pl.pallas_call(
        body,
        name=name,
        out_shape=[jax.ShapeDtypeStruct((N_DEV,) + a.shape, a.dtype) for a in arrs],
        in_specs=[any_spec] * n,
        out_specs=[any_spec] * n,
        scratch_shapes=[
            pltpu.SemaphoreType.DMA((7 * n,)),
            pltpu.SemaphoreType.DMA((7 * n,)),
            pltpu.SemaphoreType.DMA((n,)),
        ],
    )(*arrs)


def _exchange(name, arrs, views, out_shapes, flips):
    n = len(arrs)
    nf = len(flips)

    def body(*refs):
        ins, outs = refs[:n], refs[n : 2 * n]
        send_sems, recv_sems = refs[2 * n :]
        me = _position()

        def copy(a, j):
            peer = _flip(me, flips[j])
            return pltpu.make_async_remote_copy(
                src_ref=views[a](ins[a], peer),
                dst_ref=outs[a].at[j],
                send_sem=send_sems.at[a * nf + j],
                recv_sem=recv_sems.at[a * nf + j],
                device_id=peer,
                device_id_type=MESH,
            )

        sent = [copy(a, j) for a in range(n) for j in range(nf)]
        for cp in sent:
            cp.start()
        for cp in sent:
            cp.wait_recv()
        for cp in sent:
            cp.wait_send()

    any_spec = pl.BlockSpec(memory_space=pl.ANY)
    return pl.pallas_call(
        body,
        name=name,
        out_shape=[jax.ShapeDtypeStruct((nf,) + tuple(sh), a.dtype) for sh, a in zip(out_shapes, arrs)],
        in_specs=[any_spec] * n,
        out_specs=[any_spec] * n,
        scratch_shapes=[pltpu.SemaphoreType.DMA((nf * n,)), pltpu.SemaphoreType.DMA((nf * n,))],
    )(*arrs)


def _pair_add(name, own, recv, c_index, grid, own_spec, recv_spec):
    def body(c_ref, own_ref, recv_ref, s_ref, sb_ref):
        total = own_ref[...] + recv_ref[...]
        s_ref[...] = total
        sb_ref[...] = total.astype(BF16)

    return pl.pallas_call(
        body,
        name=name,
        grid_spec=pltpu.PrefetchScalarGridSpec(
            num_scalar_prefetch=1, grid=grid, in_specs=[own_spec, recv_spec], out_specs=[recv_spec, recv_spec]
        ),
        out_shape=[jax.ShapeDtypeStruct(recv.shape, F32), jax.ShapeDtypeStruct(recv.shape, BF16)],
        compiler_params=_params(len(grid)),
    )(c_index, own, recv)


def _ada_mod(c_all, w_ada, b_cols):
    def body(c_ref, w_ref, b_ref, o_ref):
        o_ref[...] = _dot(c_ref[...].astype(BF16), w_ref[...].astype(BF16)) + b_ref[...]

    return pl.pallas_call(
        body,
        name="ada_mod",
        out_shape=jax.ShapeDtypeStruct((N_DEV, w_ada.shape[1]), F32),
        compiler_params=pltpu.CompilerParams(vmem_limit_bytes=VMEM_LIMIT),
    )(c_all, w_ada, b_cols)


def _fwd_in(x, mod, norm_g, w_t):
    t = x.shape[0]
    tm = min(512, t)
    tn = IN_W // 4

    def body(x_ref, mod_ref, ng_ref, w_ref, proj_ref, h_ref):
        @pl.when(pl.program_id(1) == 0)
        def _():
            xv = x_ref[...]
            r = lax.rsqrt(jnp.mean(xv * xv, axis=-1, keepdims=True) + NORM_EPS)
            shift = mod_ref[:, 0:D]
            scale = mod_ref[:, D : 2 * D]
            h_ref[...] = (((xv * r) * ng_ref[...]) * (1.0 + scale) + shift).astype(BF16)

        proj_ref[...] = _dot_nt(h_ref[...], w_ref[...])

    return pl.pallas_call(
        body,
        name="fwd_in",
        grid=(t // tm, IN_W // tn),
        in_specs=[
            pl.BlockSpec((tm, D), lambda i, j: (i, 0)),
            pl.BlockSpec((1, 3 * D), lambda i, j: (0, 0)),
            pl.BlockSpec((1, D), lambda i, j: (0, 0)),
            pl.BlockSpec((tn, D), lambda i, j: (j, 0)),
        ],
        out_specs=[pl.BlockSpec((tm, tn), lambda i, j: (i, j)), pl.BlockSpec((tm, D), lambda i, j: (i, 0))],
        out_shape=[jax.ShapeDtypeStruct((t, IN_W), F32), jax.ShapeDtypeStruct((t, D), BF16)],
        compiler_params=_params(2),
    )(x, mod, norm_g, w_t)


def _rope_tables(positions):
    half = ROT_DIM // 2
    inv_freq = ROPE_THETA ** (-jnp.arange(0, ROT_DIM, 2, dtype=F32) / ROT_DIM)
    ang = positions[:, None].astype(F32) * inv_freq
    cos, sin = jnp.cos(ang), jnp.sin(ang)
    t = positions.shape[0]
    ones = jnp.ones((t, HEAD_DIM - ROT_DIM), F32)
    zeros = jnp.zeros((t, HEAD_DIM - half), F32)
    cos_h = jnp.concatenate([cos, cos, ones], axis=1)
    s1_h = jnp.concatenate([-sin, zeros], axis=1)
    s2_h = jnp.concatenate([jnp.zeros((t, half), F32), sin, jnp.zeros((t, HEAD_DIM - ROT_DIM), F32)], axis=1)
    return tuple(jnp.concatenate([a, a], axis=1) for a in (cos_h, s1_h, s2_h))


def _rope(tv, cos, s1, s2):
    cols = []
    for g in range(tv.shape[1] // 128):
        blk = tv[:, 128 * g : 128 * (g + 1)]
        cols.append(blk * cos + pltpu.roll(blk, 128 - 8, 1) * s1 + pltpu.roll(blk, 8, 1) * s2)
    return cols[0] if len(cols) == 1 else jnp.concatenate(cols, axis=1)


def _rope_transposed(gv, cos, s1, s2):
    cols = []
    for g in range(gv.shape[1] // 128):
        blk = gv[:, 128 * g : 128 * (g + 1)]
        cols.append(blk * cos + pltpu.roll(blk * s1, 8, 1) + pltpu.roll(blk * s2, 128 - 8, 1))
    return cols[0] if len(cols) == 1 else jnp.concatenate(cols, axis=1)


def _lane_lo():
    return lax.broadcasted_iota(jnp.int32, (1, 128), 1) < HEAD_DIM


def _to_kv_lanes(tile, j):
    lo = _lane_lo()
    keep = lo if j == 0 else jnp.logical_not(lo)
    parts = []
    for g in (2 * j, 2 * j + 1):
        blk = tile[:, 128 * g : 128 * (g + 1)]
        rolled = pltpu.roll(blk, HEAD_DIM, 1)
        first, second = (blk, rolled) if j == 0 else (rolled, blk)
        parts += [jnp.where(keep, first, 0.0), jnp.where(keep, second, 0.0)]
    return jnp.concatenate(parts, axis=0)


def _from_kv_lanes(stacks):
    lo = _lane_lo()
    cols = []
    for j in range(2):
        for gi in range(2):
            a = stacks[j][256 * gi : 256 * gi + 128]
            b = stacks[j][256 * gi + 128 : 256 * gi + 256]
            if j == 0:
                cols.append(jnp.where(lo, a, pltpu.roll(b, HEAD_DIM, 1)))
            else:
                cols.append(jnp.where(lo, pltpu.roll(a, HEAD_DIM, 1), b))
    return jnp.concatenate(cols, axis=1)


def _fill_bias(bias_ref):
    qi = lax.broadcasted_iota(jnp.int32, (128, 256), 0)
    kj = lax.broadcasted_iota(jnp.int32, (128, 256), 1)
    diff = qi + WINDOW - kj
    band = (diff >= 0) & (diff < WINDOW)
    bias_ref[0] = jnp.where(band, 0.0, -1e30)
    bias_ref[1] = jnp.where(band & (kj >= 128), 0.0, -1e30)


def _sink_column(sink_ref, pair, j):
    return jnp.concatenate([jnp.full((128, 1), sink_ref[0, 8 * pair + 4 * j + h], F32) for h in range(4)], axis=0)


def _softmax_parts(q4, kw, bias4, sink):
    s = _dot_nt(q4, kw) + bias4
    m = jnp.maximum(jnp.max(s, axis=-1, keepdims=True), sink)
    p = jnp.exp(s - m)
    e_sink = jnp.exp(sink - m)
    denom = jnp.sum(p, axis=-1, keepdims=True) + e_sink
    return p, e_sink, denom


def _attn_specs(t, tq):
    return dict(
        q=pl.BlockSpec((tq, 512), lambda p, i: (i, p)),
        k=pl.BlockSpec((t, 128), lambda p, i: (0, SEG_K // 128 + p)),
        v=pl.BlockSpec((t, 128), lambda p, i: (0, SEG_V // 128 + p)),
        tab_q=pl.BlockSpec((tq, 128), lambda p, i: (i, 0)),
        tab_k=pl.BlockSpec((t, 128), lambda p, i: (0, 0)),
        sink=pl.BlockSpec(memory_space=pltpu.SMEM),
        act=pl.BlockSpec((tq, 512), lambda p, i: (i, p)),
    )


def _fill_kv(k_ref, v_ref, ck, s1k, s2k, kbuf, vbuf, bias_ref):
    zeros = jnp.zeros((128, 128), BF16)
    kbuf[0:128, :] = zeros
    vbuf[0:128, :] = zeros
    kbuf[128:, :] = _rope(k_ref[...], ck[...], s1k[...], s2k[...]).astype(BF16)
    vbuf[128:, :] = v_ref[...].astype(BF16)
    _fill_bias(bias_ref)


def _attn_fwd(proj, tables, sinks):
    t = proj.shape[0]
    tq = min(512, t)
    nb = tq // 128
    cos, s1, s2 = tables
    sp = _attn_specs(t, tq)

    def body(q_ref, k_ref, v_ref, cq, s1q, s2q, ck, s1k, s2k, sink_ref, o_ref, kbuf, vbuf, bias_ref):
        pair = pl.program_id(0)
        i = pl.program_id(1)

        @pl.when(i == 0)
        def _():
            _fill_kv(k_ref, v_ref, ck, s1k, s2k, kbuf, vbuf, bias_ref)

        def block(b, carry):
            r0 = pl.multiple_of(b * 128, 128)
            gb = i * nb + b
            keys = pl.ds(pl.multiple_of(gb * 128, 128), 256)
            rows = pl.ds(r0, 128)
            qr = _rope(q_ref[rows, :], cq[rows, :], s1q[rows, :], s2q[rows, :]) * (1.0 / math.sqrt(HEAD_DIM))
            kw = kbuf[keys, :]
            vw = vbuf[keys, :]
            bias = bias_ref[jnp.where(gb == 0, 1, 0)]
            bias4 = jnp.concatenate([bias] * 4, axis=0)
            stacks = []
            for j in range(2):
                q4 = _to_kv_lanes(qr, j).astype(BF16)
                p, _, denom = _softmax_parts(q4, kw, bias4, _sink_column(sink_ref, pair, j))
                stacks.append(_dot(p.astype(BF16), vw) * (1.0 / denom))
            o_ref[rows, :] = _from_kv_lanes(stacks)
            return carry

        lax.fori_loop(0, nb, block, 0)

    return pl.pallas_call(
        body,
        name="attn_fwd",
        grid=(2, t // tq),
        in_specs=[sp["q"], sp["k"], sp["v"], sp["tab_q"], sp["tab_q"], sp["tab_q"], sp["tab_k"], sp["tab_k"], sp["tab_k"], sp["sink"]],
        out_specs=sp["act"],
        out_shape=jax.ShapeDtypeStruct((t, D), F32),
        scratch_shapes=[pltpu.VMEM((t + 128, 128), BF16), pltpu.VMEM((t + 128, 128), BF16), pltpu.VMEM((2, 128, 256), F32)],
        compiler_params=_params(2),
    )(proj, proj, proj, cos, s1, s2, cos, s1, s2, sinks)


def _attn_bwd(proj, o, d_o, tables, sinks):
    t = proj.shape[0]
    tq = min(512, t)
    nb = tq // 128
    n_tiles = t // tq
    cos, s1, s2 = tables
    sp = _attn_specs(t, tq)
    scale = 1.0 / math.sqrt(HEAD_DIM)

    def body(q_ref, k_ref, v_ref, o_ref, do_ref, cq, s1q, s2q, ck, s1k, s2k, sink_ref, dq_ref, dk_ref, dv_ref, dsink_ref,
             kbuf, vbuf, bias_ref, dkacc, dvacc):
        pair = pl.program_id(0)
        i = pl.program_id(1)

        @pl.when(i == 0)
        def _():
            _fill_kv(k_ref, v_ref, ck, s1k, s2k, kbuf, vbuf, bias_ref)
            dkacc[...] = jnp.zeros_like(dkacc)
            dvacc[...] = jnp.zeros_like(dvacc)
            dsink_ref[...] = jnp.zeros_like(dsink_ref)

        def block(b, carry):
            r0 = pl.multiple_of(b * 128, 128)
            gb = i * nb + b
            keys = pl.ds(pl.multiple_of(gb * 128, 128), 256)
            rows = pl.ds(r0, 128)
            cqb, s1b, s2b = cq[rows, :], s1q[rows, :], s2q[rows, :]
            qr = _rope(q_ref[rows, :], cqb, s1b, s2b) * scale
            kw = kbuf[keys, :]
            vw = vbuf[keys, :]
            dob = do_ref[rows, :]
            prod = dob * o_ref[rows, :]
            bias = bias_ref[jnp.where(gb == 0, 1, 0)]
            bias4 = jnp.concatenate([bias] * 4, axis=0)
            dq_stacks = []
            dk = jnp.zeros((256, 128), F32)
            dv = jnp.zeros((256, 128), F32)
            for j in range(2):
                q4 = _to_kv_lanes(qr, j).astype(BF16)
                p, e_sink, denom = _softmax_parts(q4, kw, bias4, _sink_column(sink_ref, pair, j))
                inv = 1.0 / denom
                prob = p * inv
                do4 = _to_kv_lanes(dob, j).astype(BF16)
                delta = jnp.sum(_to_kv_lanes(prod, j), axis=-1, keepdims=True)
                dp = _dot_nt(do4, vw)
                ds = (prob * (dp - delta)).astype(BF16)
                dq_stacks.append(_dot(ds, kw) * scale)
                dk = dk + _dot_tn(ds, q4)
                dv = dv + _dot_tn(prob.astype(BF16), do4)
                dsink = -(e_sink * inv) * delta
                for h in range(4):
                    row = 4 * j + h
                    total = jnp.sum(dsink[128 * h : 128 * (h + 1), :], axis=0, keepdims=True)
                    dsink_ref[0, row : row + 1, :] = dsink_ref[0, row : row + 1, :] + jnp.broadcast_to(total, (1, 128))
            dq_ref[rows, :] = _rope_transposed(_from_kv_lanes(dq_stacks), cqb, s1b, s2b).astype(BF16)
            dkacc[keys, :] = dkacc[keys, :] + dk
            dvacc[keys, :] = dvacc[keys, :] + dv
            return carry

        lax.fori_loop(0, nb, block, 0)

        @pl.when(i == n_tiles - 1)
        def _():
            dk_ref[...] = _rope_transposed(dkacc[128:, :], ck[...], s1k[...], s2k[...]).astype(BF16)
            dv_ref[...] = dvacc[128:, :].astype(BF16)

    kv_out = pl.BlockSpec((t, 128), lambda p, i: (0, p))
    return pl.pallas_call(
        body,
        name="attn_bwd",
        grid=(2, n_tiles),
        in_specs=[sp["q"], sp["k"], sp["v"], sp["act"], sp["act"], sp["tab_q"], sp["tab_q"], sp["tab_q"], sp["tab_k"], sp["tab_k"], sp["tab_k"], sp["sink"]],
        out_specs=[sp["act"], kv_out, kv_out, pl.BlockSpec((1, 8, 128), lambda p, i: (p, 0, 0))],
        out_shape=[
            jax.ShapeDtypeStruct((t, D), BF16),
            jax.ShapeDtypeStruct((t, 256), BF16),
            jax.ShapeDtypeStruct((t, 256), BF16),
            jax.ShapeDtypeStruct((2, 8, 128), F32),
        ],
        scratch_shapes=[pltpu.VMEM((t + 128, 128), BF16), pltpu.VMEM((t + 128, 128), BF16), pltpu.VMEM((2, 128, 256), F32),
                        pltpu.VMEM((t + 128, 128), F32), pltpu.VMEM((t + 128, 128), F32)],
        compiler_params=_params(2),
    )(proj, proj, proj, o, d_o, cos, s1, s2, cos, s1, s2, sinks)


HALF = 512


def _softplus(y):
    u = jnp.exp(-jnp.abs(y))
    w = 1.0 + u
    log1p = jnp.where(w == 1.0, u, jnp.log(w) * (u / jnp.where(w == 1.0, 1.0, w - 1.0)))
    return jnp.maximum(y, 0.0) + log1p


def _gate_matmul(act_bf, w_ref, which):
    return jnp.concatenate([_dot(act_bf[:, 256 * blk : 256 * (blk + 1)], w_ref[which, blk]) for blk in range(2)], axis=1)


def _gate_matmul_t(grad_bf, w_ref, which):
    return jnp.concatenate([_dot_nt(grad_bf[:, 256 * blk : 256 * (blk + 1)], w_ref[which, blk]) for blk in range(2)], axis=1)


def _conv_and_gates(xe_ref, tm, cw_ref, cb_ref, w_ref, ba_ref, bx_ref, lam_ref, keep):
    xc = cb_ref[...] + sum(cw_ref[k : k + 1, :] * xe_ref[pl.ds(5 + k, tm), :] for k in range(CONV_WIDTH))
    xc_bf = xc.astype(BF16)
    r = _sigmoid(_gate_matmul(xc_bf, w_ref, 0) + ba_ref[...])
    ig = _sigmoid(_gate_matmul(xc_bf, w_ref, 1) + bx_ref[...])
    sp = _softplus(-lam_ref[...])
    log_a = -LRU_C * r * sp
    tanh = jnp.tanh(-log_a)
    mult = jnp.sqrt(2.0 * tanh / (1.0 + tanh))
    a = jnp.where(keep, jnp.exp(log_a), 0.0)
    mult = jnp.where(keep, mult, 1.0)
    return xc, xc_bf, r, ig, sp, a, mult


def _scan_rows(a_ref, b_ref, h_ref, carry_ref, tm, reverse):
    width = a_ref.shape[1]
    row = lax.broadcasted_iota(jnp.int32, (8, width), 0)

    def group(g, carry):
        gg = tm // 8 - 1 - g if reverse else g
        r0 = pl.multiple_of(gg * 8, 8)
        a = a_ref[pl.ds(r0, 8), :]
        b = b_ref[pl.ds(r0, 8), :]
        for s in (1, 2, 4):
            shift = 8 - s if reverse else s
            a_sh = pltpu.roll(a, shift, 0)
            b_sh = pltpu.roll(b, shift, 0)
            ok = (row < 8 - s) if reverse else (row >= s)
            b = jnp.where(ok, a * b_sh + b, b)
            a = jnp.where(ok, a * a_sh, a)
        h = a * carry + b
        h_ref[pl.ds(r0, 8), :] = h
        last = h[0:1, :] if reverse else h[7:8, :]
        return jnp.broadcast_to(last, (8, width))

    carry_ref[...] = lax.fori_loop(0, tm // 8, group, carry_ref[...])


def _rnn_specs(t, tm, order):
    n = t // tm
    return dict(
        tile=pl.BlockSpec((tm, HALF), lambda ch, i: (order(i, n), ch)),
        before=lambda col0: pl.BlockSpec((8, HALF), lambda ch, i: (jnp.maximum(order(i, n) * (tm // 8) - 1, 0), col0 + ch)),
        proj_tile=lambda col0: pl.BlockSpec((tm, HALF), lambda ch, i: (order(i, n), col0 + ch)),
        keep=pl.BlockSpec((tm, 1), lambda ch, i: (order(i, n), 0)),
        gates=pl.BlockSpec((2, 2, 256, 256), lambda ch, i: (0, ch, 0, 0)),
        vec=pl.BlockSpec((1, HALF), lambda ch, i: (0, ch)),
        conv=pl.BlockSpec((CONV_WIDTH, HALF), lambda ch, i: (0, ch)),
    )


def _rnn_fwd(proj, keep, gates_bf, rg_ba, rg_bx, rg_lambda, conv_w, conv_b):
    t = proj.shape[0]
    tm = min(512, t)
    sp = _rnn_specs(t, tm, lambda i, n: i)

    def body(xr_ref, xb_ref, keep_ref, w_ref, ba_ref, bx_ref, lam_ref, cw_ref, cb_ref, hh_ref, xe, a_buf, b_buf, carry):
        i = pl.program_id(1)

        @pl.when(i == 0)
        def _():
            carry[...] = jnp.zeros_like(carry)

        xe[0:8, :] = jnp.where(i > 0, xb_ref[...], 0.0)
        xe[8:, :] = xr_ref[...]
        keep_rows = keep_ref[...] != 0.0
        xc, _, _, ig, _, a, mult = _conv_and_gates(xe, tm, cw_ref, cb_ref, w_ref, ba_ref, bx_ref, lam_ref, keep_rows)
        a_buf[...] = a
        b_buf[...] = mult * (ig * xc)
        _scan_rows(a_buf, b_buf, hh_ref, carry, tm, reverse=False)

    return pl.pallas_call(
        body,
        name="rnn_fwd",
        grid=(2, t // tm),
        in_specs=[sp["proj_tile"](SEG_XR // HALF), sp["before"](SEG_XR // HALF), sp["keep"], sp["gates"], sp["vec"], sp["vec"], sp["vec"], sp["conv"], sp["vec"]],
        out_specs=sp["tile"],
        out_shape=jax.ShapeDtypeStruct((t, D), F32),
        scratch_shapes=[pltpu.VMEM((tm + 8, HALF), F32), pltpu.VMEM((tm, HALF), F32), pltpu.VMEM((tm, HALF), F32), pltpu.VMEM((8, HALF), F32)],
        compiler_params=_params(2),
    )(proj, proj, keep, gates_bf, rg_ba, rg_bx, rg_lambda, conv_w, conv_b)


def _rnn_bwd(proj, hh, d_hh, keep, gates_bf, rg_ba, rg_bx, rg_lambda, conv_w, conv_b):
    t = proj.shape[0]
    tm = min(512, t)
    n_tiles = t // tm
    sp = _rnn_specs(t, tm, lambda i, n: n - 1 - i)

    def body(xr_ref, xb_ref, hh_ref, hb_ref, dhh_ref, keep_ref, w_ref, ba_ref, bx_ref, lam_ref, cw_ref, cb_ref,
             dxr_ref, gw_ref, gvec_ref, xe, he, ae, c_buf, g_buf, dxe, carry, a_next):
        i = pl.program_id(1)
        first_tile = i == n_tiles - 1

        @pl.when(i == 0)
        def _():
            carry[...] = jnp.zeros_like(carry)
            a_next[...] = jnp.zeros_like(a_next)
            dxe[tm:, :] = jnp.zeros((8, HALF), F32)
            gw_ref[...] = jnp.zeros_like(gw_ref)
            gvec_ref[...] = jnp.zeros_like(gvec_ref)

        xe[0:8, :] = jnp.where(first_tile, 0.0, xb_ref[...])
        xe[8:, :] = xr_ref[...]
        he[0:8, :] = jnp.where(first_tile, 0.0, hb_ref[...])
        he[8:, :] = hh_ref[...]
        keep_rows = keep_ref[...] != 0.0
        xc, xc_bf, r, ig, sp_, a, mult = _conv_and_gates(xe, tm, cw_ref, cb_ref, w_ref, ba_ref, bx_ref, lam_ref, keep_rows)

        ae[0:tm, :] = a
        ae[tm:, :] = a_next[...]
        a_next[...] = jnp.broadcast_to(a[0:1, :], (8, HALF))
        c_buf[...] = ae[pl.ds(1, tm), :]
        g_buf[...] = dhh_ref[...]
        _scan_rows(c_buf, g_buf, g_buf, carry, tm, reverse=True)
        g = g_buf[...]

        h_prev = he[pl.ds(7, tm), :]
        bb = ig * xc
        d_ig = g * mult * xc
        d_xc = g * mult * ig
        d_log_a = jnp.where(keep_rows, g * h_prev * a - (g * bb) * (a * a) / mult, 0.0)
        d_r = d_log_a * (-LRU_C * sp_)
        d_sp = jnp.sum(d_log_a * (-LRU_C * r), axis=0, keepdims=True)
        d_zr = d_r * r * (1.0 - r)
        d_zi = d_ig * ig * (1.0 - ig)
        d_zr_bf = d_zr.astype(BF16)
        d_zi_bf = d_zi.astype(BF16)
        d_xc = d_xc + _gate_matmul_t(d_zr_bf, w_ref, 0) + _gate_matmul_t(d_zi_bf, w_ref, 1)
        for blk in range(2):
            cols = slice(256 * blk, 256 * (blk + 1))
            gw_ref[0, blk] = gw_ref[0, blk] + _dot_tn(xc_bf[:, cols], d_zr_bf[:, cols])
            gw_ref[1, blk] = gw_ref[1, blk] + _dot_tn(xc_bf[:, cols], d_zi_bf[:, cols])

        dxe[0:tm, :] = d_xc
        dxr = sum(cw_ref[k : k + 1, :] * dxe[pl.ds(3 - k, tm), :] for k in range(CONV_WIDTH))
        dxr_ref[...] = dxr.astype(BF16)
        dxe[tm:, :] = d_xc[0:8, :]

        lam = lam_ref[...]
        sums = [
            jnp.sum(d_zr, axis=0, keepdims=True),
            jnp.sum(d_zi, axis=0, keepdims=True),
            d_sp * (-_sigmoid(-lam)),
            jnp.sum(d_xc, axis=0, keepdims=True),
        ] + [jnp.sum(d_xc * xe[pl.ds(5 + k, tm), :], axis=0, keepdims=True) for k in range(CONV_WIDTH)]
        gvec_ref[...] = gvec_ref[...] + jnp.concatenate(sums, axis=0)

    return pl.pallas_call(
        body,
        name="rnn_bwd",
        grid=(2, n_tiles),
        in_specs=[
            sp["proj_tile"](SEG_XR // HALF), sp["before"](SEG_XR // HALF), sp["tile"], sp["before"](0), sp["tile"], sp["keep"],
            sp["gates"], sp["vec"], sp["vec"], sp["vec"], sp["conv"], sp["vec"],
        ],
        out_specs=[sp["tile"], sp["gates"], pl.BlockSpec((8, HALF), lambda ch, i: (0, ch))],
        out_shape=[
            jax.ShapeDtypeStruct((t, D), BF16),
            jax.ShapeDtypeStruct((2, 4, 256, 256), F32),
            jax.ShapeDtypeStruct((8, D), F32),
        ],
        scratch_shapes=[
            pltpu.VMEM((tm + 8, HALF), F32),
            pltpu.VMEM((tm + 8, HALF), F32),
            pltpu.VMEM((tm + 8, HALF), F32),
            pltpu.VMEM((tm, HALF), F32),
            pltpu.VMEM((tm, HALF), F32),
            pltpu.VMEM((tm + 8, HALF), F32),
            pltpu.VMEM((8, HALF), F32),
            pltpu.VMEM((8, HALF), F32),
        ],
        compiler_params=_params(2),
    )(proj, proj, hh, hh, d_hh, keep, gates_bf, rg_ba, rg_bx, rg_lambda, conv_w, conv_b)


def _mid(x, target, o, hh, proj, w3_bf, gate, final_g):
    t = x.shape[0]
    tm = min(256, t)
    n_tiles = t // tm
    full = pl.BlockSpec((tm, D), lambda i: (i, 0))
    half = lambda col: pl.BlockSpec((tm, 512), lambda i: (i, col))
    vec = pl.BlockSpec((1, D), lambda i: (0, 0))
    any_spec = pl.BlockSpec(memory_space=pl.ANY)

    def body(x_ref, tg_ref, o_ref, hh_ref, ga0, ga1, gr0, gr1, ma0, ma1, mr0, mr1, w_hbm, gate_ref, fg_ref,
             dres_ref, do_ref, dhh_ref, dga_ref, dgr_ref, dma_ref, dmr_ref, gw_hbm, vec_ref, w, gw, sem):
        i = pl.program_id(0)

        @pl.when(i == 0)
        def _():
            cp = pltpu.make_async_copy(w_hbm, w, sem)
            cp.start()
            cp.wait()
            gw[...] = jnp.zeros_like(gw)
            vec_ref[...] = jnp.zeros_like(vec_ref)

        cat = lambda a, b: jnp.concatenate([a[...], b[...]], axis=1)
        ga, gr, ma, mr = cat(ga0, ga1), cat(gr0, gr1), cat(ma0, ma1), cat(mr0, mr1)
        ov, hv, gate_v, fg = o_ref[...], hh_ref[...], gate_ref[...], fg_ref[...]
        sg_a = _sigmoid(ga)
        sg_r = _sigmoid(gr)
        silu_a = ga * sg_a
        silu_r = gr * sg_r
        ya = (ov * silu_a).astype(BF16)
        yr = (hv * silu_r).astype(BF16)
        pa = _dot(ya, w[0])
        pr = _dot(yr, w[1])
        sa = _sigmoid(ma)
        sr = _sigmoid(mr)
        merged = (sa * pa + sr * pr).astype(BF16)
        mo = _dot(merged, w[2])
        out = x_ref[...] + gate_v * mo
        r2 = lax.rsqrt(jnp.mean(out * out, axis=-1, keepdims=True) + NORM_EPS)
        outn = out * r2
        err = outn * fg - tg_ref[...]
        loss = 0.5 * jnp.sum(jnp.mean(err * err, axis=-1, keepdims=True), axis=0, keepdims=True)
        dy = err * (1.0 / D)
        g_final = jnp.sum(dy * outn, axis=0, keepdims=True)
        doutn = dy * fg
        dout = r2 * (doutn - outn * jnp.mean(doutn * outn, axis=-1, keepdims=True))
        dres_ref[...] = dout
        d_gate = jnp.sum(dout * mo, axis=0, keepdims=True)
        dmo = (dout * gate_v).astype(BF16)
        dmerged = _dot_nt(dmo, w[2])
        gw[2] = gw[2] + _dot_tn(merged, dmo)
        dpa = (dmerged * sa).astype(BF16)
        dpr = (dmerged * sr).astype(BF16)
        dma_ref[...] = (dmerged * pa * (sa * (1.0 - sa))).astype(BF16)
        dmr_ref[...] = (dmerged * pr * (sr * (1.0 - sr))).astype(BF16)
        dya = _dot_nt(dpa, w[0])
        dyr = _dot_nt(dpr, w[1])
        gw[0] = gw[0] + _dot_tn(ya, dpa)
        gw[1] = gw[1] + _dot_tn(yr, dpr)
        do_ref[...] = dya * silu_a
        dhh_ref[...] = dyr * silu_r
        dga_ref[...] = (dya * ov * (sg_a * (1.0 + ga * (1.0 - sg_a)))).astype(BF16)
        dgr_ref[...] = (dyr * hv * (sg_r * (1.0 + gr * (1.0 - sg_r)))).astype(BF16)
        zeros = jnp.zeros((5, D), F32)
        vec_ref[...] = vec_ref[...] + jnp.concatenate([g_final, d_gate, jnp.broadcast_to(loss, (1, D)), zeros], axis=0)

        @pl.when(i == n_tiles - 1)
        def _():
            cp = pltpu.make_async_copy(gw, gw_hbm, sem)
            cp.start()
            cp.wait()

    bf_tile = jax.ShapeDtypeStruct((t, D), BF16)
    f_tile = jax.ShapeDtypeStruct((t, D), F32)
    return pl.pallas_call(
        body,
        name="mid",
        grid=(n_tiles,),
        in_specs=[full, full, full, full,
                  half(SEG_GA // 512), half(SEG_GA // 512 + 1), half(SEG_GR // 512), half(SEG_GR // 512 + 1),
                  half(SEG_MA // 512), half(SEG_MA // 512 + 1), half(SEG_MR // 512), half(SEG_MR // 512 + 1),
                  any_spec, vec, vec],
        out_specs=[full, full, full, full, full, full, full, any_spec, pl.BlockSpec((8, D), lambda i: (0, 0))],
        out_shape=[f_tile, f_tile, f_tile, bf_tile, bf_tile, bf_tile, bf_tile,
                   jax.ShapeDtypeStruct((3, D, D), F32), jax.ShapeDtypeStruct((8, D), F32)],
        scratch_shapes=[pltpu.VMEM((3, D, D), BF16), pltpu.VMEM((3, D, D), F32), pltpu.SemaphoreType.DMA],
        compiler_params=_params(1),
    )(x, target, o, hh, proj, proj, proj, proj, proj, proj, proj, proj, w3_bf, gate, final_g)


_SEGMENTS = ((SEG_Q, 1024), (SEG_K, 256), (SEG_V, 256), (SEG_GA, 1024), (SEG_XR, 1024), (SEG_GR, 1024), (SEG_MA, 1024), (SEG_MR, 1024))


def _seg_specs(tm):
    return [pl.BlockSpec((tm, width), lambda i: (i, 0)) for _, width in _SEGMENTS]


def _bwd_dh(dsegs, w_t, x, dres, mod, norm_g):
    t = x.shape[0]
    tm = min(256, t)
    n_seg = len(_SEGMENTS)
    full = pl.BlockSpec((tm, D), lambda i: (i, 0))

    def body(*refs):
        segs = refs[:n_seg]
        w_hbm, x_ref, dres_ref, mod_ref, ng_ref, gx_ref, vec_ref, w, sem = refs[n_seg:]
        i = pl.program_id(0)

        @pl.when(i == 0)
        def _():
            cp = pltpu.make_async_copy(w_hbm, w, sem)
            cp.start()
            cp.wait()
            vec_ref[...] = jnp.zeros_like(vec_ref)

        dh = sum(_dot(seg[...], w[start : start + width, :]) for seg, (start, width) in zip(segs, _SEGMENTS))
        xv = x_ref[...]
        r1 = lax.rsqrt(jnp.mean(xv * xv, axis=-1, keepdims=True) + NORM_EPS)
        xn = xv * r1
        ng = ng_ref[...]
        scale1 = 1.0 + mod_ref[:, D : 2 * D]
        d_shift = jnp.sum(dh, axis=0, keepdims=True)
        d_scale = jnp.sum(dh * (xn * ng), axis=0, keepdims=True)
        g_norm = jnp.sum(dh * xn * scale1, axis=0, keepdims=True)
        dxn = dh * (ng * scale1)
        dx = r1 * (dxn - xn * jnp.mean(dxn * xn, axis=-1, keepdims=True))
        gx_ref[...] = dres_ref[...] + dx
        vec_ref[...] = vec_ref[...] + jnp.concatenate([d_shift, d_scale, g_norm, jnp.zeros((5, D), F32)], axis=0)

    return pl.pallas_call(
        body,
        name="bwd_dh",
        grid=(t // tm,),
        in_specs=_seg_specs(tm) + [pl.BlockSpec(memory_space=pl.ANY), full, full,
                                   pl.BlockSpec((1, 3 * D), lambda i: (0, 0)), pl.BlockSpec((1, D), lambda i: (0, 0))],
        out_specs=[full, pl.BlockSpec((8, D), lambda i: (0, 0))],
        out_shape=[jax.ShapeDtypeStruct((t, D), F32), jax.ShapeDtypeStruct((8, D), F32)],
        scratch_shapes=[pltpu.VMEM((IN_W, D), BF16), pltpu.SemaphoreType.DMA],
        compiler_params=_params(1),
    )(*dsegs, w_t, x, dres, mod, norm_g)


def _bwd_gw(dsegs, h):
    t = h.shape[0]
    tm = min(512, t)
    n_tiles = t // tm
    n_seg = len(_SEGMENTS)

    def body(*refs):
        segs = refs[:n_seg]
        h_ref, out_hbm, acc, sem = refs[n_seg:]
        i = pl.program_id(0)

        @pl.when(i == 0)
        def _():
            acc[...] = jnp.zeros_like(acc)

        hv = h_ref[...]
        for seg, (start, width) in zip(segs, _SEGMENTS):
            acc[start : start + width, :] = acc[start : start + width, :] + _dot_tn(seg[...], hv)

        @pl.when(i == n_tiles - 1)
        def _():
            cp = pltpu.make_async_copy(acc, out_hbm, sem)
            cp.start()
            cp.wait()

    return pl.pallas_call(
        body,
        name="bwd_gw",
        grid=(n_tiles,),
        in_specs=_seg_specs(tm) + [pl.BlockSpec((tm, D), lambda i: (i, 0))],
        out_specs=pl.BlockSpec(memory_space=pl.ANY),
        out_shape=jax.ShapeDtypeStruct((IN_W, D), F32),
        scratch_shapes=[pltpu.VMEM((IN_W, D), F32), pltpu.SemaphoreType.DMA],
        compiler_params=_params(1),
    )(*dsegs, h)


def _adamw(w, g, m, v):
    m = ADAM_B1 * m + (1.0 - ADAM_B1) * g
    v = ADAM_B2 * v + (1.0 - ADAM_B2) * (g * g)
    m_hat = m / (1.0 - ADAM_B1**ADAM_STEP)
    v_hat = v / (1.0 - ADAM_B2**ADAM_STEP)
    delta = -ADAM_LR * (m_hat / (jnp.sqrt(v_hat) + ADAM_EPS) + ADAM_WD * w)
    return delta, m, v


def _row_block(rows):
    return 128 if rows % 128 == 0 else rows


def _add_slots(own, l_ref):
    total = own
    for s in range(l_ref.shape[0]):
        total = total + l_ref[s].astype(F32)
    return total


def _sum_slots(name, own, land):
    n_slots, rows, cols = land.shape
    rb = 208 if rows == W_IN_SHARD else _row_block(rows)

    def body(o_ref, l_ref, g_ref):
        g_ref[...] = _add_slots(o_ref[...], l_ref)

    spec = pl.BlockSpec((rb, cols), lambda i: (i, 0))
    return pl.pallas_call(
        body,
        name=name,
        grid=(rows // rb,),
        in_specs=[spec, pl.BlockSpec((n_slots, rb, cols), lambda i: (0, i, 0))],
        out_specs=spec,
        out_shape=jax.ShapeDtypeStruct((rows, cols), F32),
        compiler_params=_params(1),
    )(own, land)


def _adam_update(name, g, w, m, v):
    rows, cols = w.shape
    rb = _row_block(rows)
    spec = pl.BlockSpec((rb, cols), lambda i: (i, 0))

    def body(g_ref, w_ref, m_ref, v_ref, d_ref, m2_ref, v2_ref):
        d_ref[...], m2_ref[...], v2_ref[...] = _adamw(w_ref[...], g_ref[...], m_ref[...], v_ref[...])

    shape = jax.ShapeDtypeStruct((rows, cols), F32)
    return pl.pallas_call(
        body, name=name, grid=(rows // rb,), in_specs=[spec] * 4, out_specs=[spec] * 3, out_shape=[shape] * 3,
        compiler_params=_params(1),
    )(g, w, m, v)


def _sum_adam(name, own, land, w, m, v):
    rows, cols = w.shape
    rb = _row_block(rows)
    spec = pl.BlockSpec((rb, cols), lambda i: (i, 0))
    n_slots = land.shape[0]

    def body(o_ref, l_ref, w_ref, m_ref, v_ref, g_ref, d_ref, m2_ref, v2_ref):
        total = _add_slots(o_ref[...], l_ref)
        g_ref[...] = total
        d_ref[...], m2_ref[...], v2_ref[...] = _adamw(w_ref[...], total, m_ref[...], v_ref[...])

    shape = jax.ShapeDtypeStruct((rows, cols), F32)
    return pl.pallas_call(
        body, name=name, grid=(rows // rb,),
        in_specs=[spec, pl.BlockSpec((n_slots, rb, cols), lambda i: (0, i, 0)), spec, spec, spec],
        out_specs=[spec] * 4, out_shape=[shape] * 4, compiler_params=_params(1),
    )(own, land, w, m, v)


def _ada_update(c_all_t, dmod_cols, w, m, v):
    def body(c_ref, dm_ref, w_ref, m_ref, v_ref, g_ref, d_ref, m2_ref, v2_ref):
        g = _dot(c_ref[...], dm_ref[...])
        g_ref[...] = g
        d_ref[...], m2_ref[...], v2_ref[...] = _adamw(w_ref[...], g, m_ref[...], v_ref[...])

    shape = jax.ShapeDtypeStruct(w.shape, F32)
    return pl.pallas_call(
        body, name="ada_update", out_shape=[shape] * 4, compiler_params=pltpu.CompilerParams(vmem_limit_bytes=VMEM_LIMIT)
    )(c_all_t, dmod_cols, w, m, v)


def _vec_update(gathered, w, m, v):
    n_rep = w.shape[1]

    def body(ga_ref, w_ref, m_ref, v_ref, g_ref, d_ref, m2_ref, v2_ref):
        total = ga_ref[0:1, :]
        for s in range(1, N_DEV):
            total = total + ga_ref[s : s + 1, :]
        g_ref[...] = total
        d_ref[...], m2_ref[...], v2_ref[...] = _adamw(w_ref[...], total[:, 0:n_rep], m_ref[...], v_ref[...])

    shape = jax.ShapeDtypeStruct(w.shape, F32)
    return pl.pallas_call(
        body, name="vec_update", out_shape=[jax.ShapeDtypeStruct((1, gathered.shape[1]), F32)] + [shape] * 3,
        compiler_params=pltpu.CompilerParams(vmem_limit_bytes=VMEM_LIMIT),
    )(gathered, w, m, v)


def kernel(x, c, positions, w_ada, b_ada, norm_g, w_in, attn_sinks, conv_w, conv_b, rg_wa, rg_ba, rg_wx, rg_bx, rg_lambda, w_attn_proj, w_rnn_proj, w_out, final_g, loss_target, m_w_ada, m_b_ada, m_norm_g, m_w_in, m_attn_sinks, m_conv_w, m_conv_b, m_rg_wa, m_rg_ba, m_rg_wx, m_rg_bx, m_rg_lambda, m_w_attn_proj, m_w_rnn_proj, m_w_out, m_final_g, v_w_ada, v_b_ada, v_norm_g, v_w_in, v_attn_sinks, v_conv_w, v_conv_b, v_rg_wa, v_rg_ba, v_rg_wx, v_rg_bx, v_rg_lambda, v_w_attn_proj, v_w_rnn_proj, v_w_out, v_final_g):
    me = _slot(_position())
    x2, target = x[0], loss_target[0]
    pos = positions[0]

    w_pack = jnp.concatenate(
        [w_in[0].T, w_attn_proj[0], w_rnn_proj[0], w_out[0], rg_wa[0].reshape(GATE_SHARD, D), rg_wx[0].reshape(GATE_SHARD, D)], axis=0
    ).astype(BF16)
    n_conv = conv_w.shape[2]
    conv_rows = jnp.pad(conv_w[0], ((0, 8 - CONV_WIDTH), (0, 0)))
    c_rows, conv_all, w_all = _all_gather("ag_fwd", [jnp.broadcast_to(c, (8, D)), conv_rows, w_pack])
    c_all = c_rows[:, 0, :]
    conv_full = conv_all[:, :CONV_WIDTH, :].transpose(1, 0, 2).reshape(CONV_WIDTH, D)
    r0 = W_IN_SHARD
    w_t = w_all[:, 0:r0, :].reshape(IN_W, D)
    w3_bf = jnp.stack([w_all[:, r0 + ROW_SHARD * k : r0 + ROW_SHARD * (k + 1), :].reshape(D, D) for k in range(3)])
    r1 = r0 + 3 * ROW_SHARD
    gates_bf = jnp.stack(
        [w_all[:, r1 + GATE_SHARD * k : r1 + GATE_SHARD * (k + 1), :].reshape(N_DEV, 4, GATE_SHARD, 256).transpose(1, 0, 2, 3).reshape(4, 256, 256)
         for k in range(2)]
    )

    n_ada = w_ada.shape[2]
    b_cols = lax.dynamic_slice(b_ada, (0, me * n_ada), (1, n_ada))
    mod_cols = _ada_mod(c_all, w_ada[0], b_cols)
    (mod_all,) = _all_gather("ag_mod", [mod_cols])
    mod = lax.dynamic_slice(mod_all, (0, me, 0), (N_DEV, 1, n_ada)).reshape(1, 3 * D)
    gate = mod[:, 2 * D :]

    proj, h = _fwd_in(x2, mod, norm_g, w_t)
    tables = _rope_tables(pos)
    o = _attn_fwd(proj, tables, attn_sinks)
    keep = (pos != 0).astype(F32)[:, None]
    hh = _rnn_fwd(proj, keep, gates_bf, rg_ba, rg_bx, rg_lambda, conv_full, conv_b)

    dres, d_o, d_hh, d_ga, d_gr, d_ma, d_mr, gw3, mid_vec = _mid(x2, target, o, hh, proj, w3_bf, gate, final_g[None, :])
    d_q, d_k, d_v, d_sink = _attn_bwd(proj, o, d_o, tables, attn_sinks)
    d_xr, g_gates, rnn_vec = _rnn_bwd(proj, hh, d_hh, keep, gates_bf, rg_ba, rg_bx, rg_lambda, conv_full, conv_b)
    dsegs = (d_q, d_k, d_v, d_ga, d_xr, d_gr, d_ma, d_mr)
    grad_x, dh_vec = _bwd_dh(dsegs, w_t, x2, dres, mod, norm_g)
    g_w_t = _bwd_gw(dsegs, h)
    loss = lax.psum(mid_vec[2, 0], AXES)

    g_sinks = jnp.pad(d_sink[:, :, 0].reshape(1, 16), ((0, 0), (0, 112)))
    vec = jnp.concatenate(
        [dh_vec[0:1], dh_vec[1:2], mid_vec[1:2],
         dh_vec[2:3], g_sinks, rnn_vec[3:4], rnn_vec[0:1], rnn_vec[1:2], rnn_vec[2:3], mid_vec[0:1],
         rnn_vec[4:8].reshape(1, CONV_WIDTH * D)], axis=1)
    (vec_all,) = _all_gather("ag_vec", [jnp.broadcast_to(vec, (8, vec.shape[1]))])
    vec_all = vec_all[:, 0, :]
    pad16 = lambda a: jnp.pad(a, ((0, 0), (0, 112)))
    rep = lambda b, n, s, cb, ba, bx, lam, fg: jnp.concatenate([b, n, pad16(s), cb, ba, bx, lam, fg[None, :]], axis=1)
    rep_w = rep(b_ada, norm_g, attn_sinks, conv_b, rg_ba, rg_bx, rg_lambda, final_g)
    rep_m = rep(m_b_ada, m_norm_g, m_attn_sinks, m_conv_b, m_rg_ba, m_rg_bx, m_rg_lambda, m_final_g)
    rep_v = rep(v_b_ada, v_norm_g, v_attn_sinks, v_conv_b, v_rg_ba, v_rg_bx, v_rg_lambda, v_final_g)
    g_vec, d_rep, m_rep, v_rep = _vec_update(vec_all, rep_w, rep_m, rep_v)

    def split_rep(a):
        offs = np.cumsum([0, 3 * D, D, 128, D, D, D, D, D])
        parts = [a[:, offs[k] : offs[k + 1]] for k in range(8)]
        parts[2] = parts[2][:, :16]
        parts[7] = parts[7][0]
        return parts

    n_rep = rep_w.shape[1]
    g_conv_full = g_vec[:, n_rep:].reshape(CONV_WIDTH, D)
    n_conv = conv_w.shape[2]
    g_conv = lax.dynamic_slice(g_conv_full, (0, me * n_conv), (CONV_WIDTH, n_conv))

    dmod_all = vec_all[:, 0 : 3 * D]
    dmod_cols = lax.dynamic_slice(dmod_all, (0, me * n_ada), (N_DEV, n_ada))
    g_ada, d_ada, m_ada, v_ada = _ada_update(c_all.T, dmod_cols, w_ada[0], m_w_ada[0], v_w_ada[0])

    my_x, my_y, my_c = _position()
    part_w = g_w_t.reshape(4, 2, W_IN_SHARD, D)
    part_3 = gw3.reshape(3, 4, 2, ROW_SHARD, D)
    part_g = g_gates.reshape(2, 4, 4, 2, GATE_SHARD, 256)
    recv_w, recv_3, recv_g = _exchange(
        "rs_pair",
        [part_w, part_3, part_g],
        [lambda ref, peer: ref.at[:, peer[2]], lambda ref, peer: ref.at[:, :, peer[2]], lambda ref, peer: ref.at[:, :, :, peer[2]]],
        [(4, W_IN_SHARD, D), (3, 4, ROW_SHARD, D), (2, 4, 4, GATE_SHARD, 256)],
        flips=(1,),
    )
    c_idx = my_c.astype(jnp.int32).reshape(1)
    rb = 208
    sum_w, sum_w_bf = _pair_add(
        "pair_add_w", part_w, recv_w[0], c_idx, (W_IN_SHARD // rb,),
        pl.BlockSpec((4, None, rb, D), lambda r, c_ref: (0, c_ref[0], r, 0)),
        pl.BlockSpec((4, rb, D), lambda r, c_ref: (0, r, 0)),
    )
    sum_3, sum_3_bf = _pair_add(
        "pair_add_3", part_3, recv_3[0], c_idx, (3,),
        pl.BlockSpec((None, 4, None, ROW_SHARD, D), lambda a, c_ref: (a, 0, c_ref[0], 0, 0)),
        pl.BlockSpec((None, 4, ROW_SHARD, D), lambda a, c_ref: (a, 0, 0, 0)),
    )
    sum_g, sum_g_bf = _pair_add(
        "pair_add_g", part_g, recv_g[0], c_idx, (1,),
        pl.BlockSpec((2, 4, 4, None, GATE_SHARD, 256), lambda a, c_ref: (0, 0, 0, c_ref[0], 0, 0)),
        pl.BlockSpec((2, 4, 4, GATE_SHARD, 256), lambda a, c_ref: (0, 0, 0, 0, 0)),
    )
    chip = lambda peer: 2 * peer[0] + peer[1]
    land_w, land_3, land_g = _exchange(
        "rs_chips",
        [sum_w_bf, sum_3_bf, sum_g_bf],
        [lambda ref, peer: ref.at[chip(peer)], lambda ref, peer: ref.at[:, chip(peer)], lambda ref, peer: ref.at[:, :, chip(peer)]],
        [(W_IN_SHARD, D), (3, ROW_SHARD, D), (2, 4, GATE_SHARD, 256)],
        flips=(2, 4, 6),
    )
    my_chip = 2 * my_x + my_y
    own_w = lax.dynamic_index_in_dim(sum_w, my_chip, 0, keepdims=False)
    own_3 = lax.dynamic_index_in_dim(sum_3, my_chip, 1, keepdims=False).reshape(3 * ROW_SHARD, D)
    gshape = (2 * 4 * GATE_SHARD, 256)
    own_g = lax.dynamic_index_in_dim(sum_g, my_chip, 2, keepdims=False).reshape(gshape)
    g_in = _sum_slots("sum_w_in", own_w, land_w).T
    d_in, m_in, v_in = _adam_update("adam_w_in", g_in, w_in[0], m_w_in[0], v_w_in[0])
    w3 = jnp.concatenate([w_attn_proj[0], w_rnn_proj[0], w_out[0]], axis=0)
    m3 = jnp.concatenate([m_w_attn_proj[0], m_w_rnn_proj[0], m_w_out[0]], axis=0)
    v3 = jnp.concatenate([v_w_attn_proj[0], v_w_rnn_proj[0], v_w_out[0]], axis=0)
    g3, d3, m3n, v3n = _sum_adam("adam_proj", own_3, land_3.reshape(3, 3 * ROW_SHARD, D), w3, m3, v3)
    wg = jnp.stack([rg_wa[0], rg_wx[0]]).reshape(gshape)
    mg = jnp.stack([m_rg_wa[0], m_rg_wx[0]]).reshape(gshape)
    vg = jnp.stack([v_rg_wa[0], v_rg_wx[0]]).reshape(gshape)
    gg, dg, mgn, vgn = _sum_adam("adam_gates", own_g, land_g.reshape((3,) + gshape), wg, mg, vg)
    d_conv, m_conv, v_conv = _adam_update("adam_conv", g_conv, conv_w[0], m_conv_w[0], v_conv_w[0])

    def by_weight(rep_parts, ada, w_in_, conv, gates, proj3):
        b, n, s, cb, ba, bx, lam, fg = rep_parts
        gates = gates.reshape(2, 1, 4, GATE_SHARD, 256)
        proj3 = proj3.reshape(3, 1, ROW_SHARD, D)
        return [ada[None], b, n, w_in_[None], s, conv[None], cb, gates[0], ba, gates[1], bx, lam, proj3[0], proj3[1], proj3[2], fg]

    grads = by_weight(split_rep(g_vec[:, :n_rep]), g_ada, g_in, g_conv, gg, g3)
    deltas = by_weight(split_rep(d_rep), d_ada, d_in, d_conv, dg, d3)
    new_m = by_weight(split_rep(m_rep), m_ada, m_in, m_conv, mgn, m3n)
    new_v = by_weight(split_rep(v_rep), v_ada, v_in, v_conv, vgn, v3n)
    return (loss, grad_x[None], *grads, *deltas, *new_m, *new_v)
```

```python
import functools
import math

import jax
import jax.numpy as jnp
import numpy as np
from jax import lax
from jax.experimental import pallas as pl
from jax.experimental.pallas import tpu as pltpu

F32 = jnp.float32
BF16 = jnp.bfloat16

D = 1024
HEAD_DIM = 64
ROT_DIM = 16
ROPE_THETA = 500000.0
WINDOW = 128
LRU_C = 8.0
CONV_WIDTH = 4
NORM_EPS = 1e-6
IN_W = 6656
SEG_Q, SEG_K, SEG_V, SEG_GA, SEG_XR, SEG_GR, SEG_MA, SEG_MR = 0, 1024, 1280, 1536, 2560, 3584, 4608, 5632
N_DEV = 8
W_IN_SHARD = IN_W // N_DEV
ROW_SHARD = D // N_DEV
GATE_SHARD = 256 // N_DEV

ADAM_LR = 0.001
ADAM_B1 = 0.9
ADAM_B2 = 0.999
ADAM_EPS = 1e-08
ADAM_WD = 0.01
ADAM_STEP = 10

VMEM_LIMIT = 60 * 1024 * 1024
MESH = pl.DeviceIdType.MESH
AXES = ("x", "y", "c")


def _dot(a, b):
    return jnp.dot(a, b, preferred_element_type=F32)


def _dot_nt(a, b):
    return lax.dot_general(a, b, (((1,), (1,)), ((), ())), preferred_element_type=F32)


def _dot_tn(a, b):
    return lax.dot_general(a, b, (((0,), (0,)), ((), ())), preferred_element_type=F32)


def _sigmoid(z):
    return 1.0 / (1.0 + jnp.exp(-z))


def _params(n_grid, vmem=VMEM_LIMIT):
    return pltpu.CompilerParams(dimension_semantics=("arbitrary",) * n_grid, vmem_limit_bytes=vmem)


def _position():
    return lax.axis_index("x"), lax.axis_index("y"), lax.axis_index("c")


def _flip(pos, k):
    x, y, c = pos
    return (1 - x if k & 4 else x, 1 - y if k & 2 else y, 1 - c if k & 1 else c)


def _slot(pos):
    return 4 * pos[0] + 2 * pos[1] + pos[2]


def _all_gather(name, arrs):
    n = len(arrs)

    def body(*refs):
        ins, outs = refs[:n], refs[n : 2 * n]
        send_sems, recv_sems, local_sems = refs[2 * n :]
        me = _position()
        sibling = _flip(me, 1)
        chips = [2, 4, 6]

        def copy(a, k, block, to, src=None):
            return pltpu.make_async_remote_copy(
                src_ref=outs[a].at[_slot(block)] if src is None else src,
                dst_ref=outs[a].at[_slot(block)],
                send_sem=send_sems.at[a * 7 + k],
                recv_sem=recv_sems.at[a * 7 + k],
                device_id=to,
                device_id_type=MESH,
            )

        mine = [pltpu.make_async_copy(ins[a], outs[a].at[_slot(me)], local_sems.at[a]) for a in range(n)]
        for cp in mine:
            cp.start()
        first = []
        for a in range(n):
            first.append(copy(a, 0, me, sibling, src=ins[a]))
            first += [copy(a, 1 + j, me, _flip(me, k), src=ins[a]) for j, k in enumerate(chips)]
        for cp in first:
            cp.start()
        passed = []
        for a in range(n):
            for j, k in enumerate(chips):
                copy(a, 1 + j, _flip(me, k), me).wait_recv()
                fwd = copy(a, 4 + j, _flip(me, k), sibling)
                fwd.start()
                passed.append(fwd)
        for a in range(n):
            copy(a, 0, sibling, me).wait_recv()
            for j, k in enumerate(chips):
                copy(a, 4 + j, _flip(sibling, k), me).wait_recv()
        for cp in first + passed:
            cp.wait_send()
        for cp in mine:
            cp.wait()

    any_spec = pl.BlockSpec(memory_space=pl.ANY)
    return pl.pallas_call(
        body,
        name=name,
        out_shape=[jax.ShapeDtypeStruct((N_DEV,) + a.shape, a.dtype) for a in arrs],
        in_specs=[any_spec] * n,
        out_specs=[any_spec] * n,
        scratch_shapes=[
            pltpu.SemaphoreType.DMA((7 * n,)),
            pltpu.SemaphoreType.DMA((7 * n,)),
            pltpu.SemaphoreType.DMA((n,)),
        ],
    )(*arrs)


def _exchange(name, arrs, views, out_shapes, flips):
    n = len(arrs)
    nf = len(flips)

    def body(*refs):
        ins, outs = refs[:n], refs[n : 2 * n]
        send_sems, recv_sems = refs[2 * n :]
        me = _position()

        def copy(a, j):
            peer = _flip(me, flips[j])
            return pltpu.make_async_remote_copy(
                src_ref=views[a](ins[a], peer),
                dst_ref=outs[a].at[j],
                send_sem=send_sems.at[a * nf + j],
                recv_sem=recv_sems.at[a * nf + j],
                device_id=peer,
                device_id_type=MESH,
            )

        sent = [copy(a, j) for a in range(n) for j in range(nf)]
        for cp in sent:
            cp.start()
        for cp in sent:
            cp.wait_recv()
        for cp in sent:
            cp.wait_send()

    any_spec = pl.BlockSpec(memory_space=pl.ANY)
    return pl.pallas_call(
        body,
        name=name,
        out_shape=[jax.ShapeDtypeStruct((nf,) + tuple(sh), a.dtype) for sh, a in zip(out_shapes, arrs)],
        in_specs=[any_spec] * n,
        out_specs=[any_spec] * n,
        scratch_shapes=[pltpu.SemaphoreType.DMA((nf * n,)), pltpu.SemaphoreType.DMA((nf * n,))],
    )(*arrs)


def _pair_add(name, own, recv, c_index, grid, own_spec, recv_spec):
    def body(c_ref, own_ref, recv_ref, s_ref, sb_ref):
        total = own_ref[...] + recv_ref[...]
        s_ref[...] = total
        sb_ref[...] = total.astype(BF16)

    return pl.pallas_call(
        body,
        name=name,
        grid_spec=pltpu.PrefetchScalarGridSpec(
            num_scalar_prefetch=1, grid=grid, in_specs=[own_spec, recv_spec], out_specs=[recv_spec, recv_spec]
        ),
        out_shape=[jax.ShapeDtypeStruct(recv.shape, F32), jax.ShapeDtypeStruct(recv.shape, BF16)],
        compiler_params=_params(len(grid)),
    )(c_index, own, recv)


def _ada_mod(c_all, w_ada, b_cols):
    def body(c_ref, w_ref, b_ref, o_ref):
        o_ref[...] = _dot(c_ref[...].astype(BF16), w_ref[...].astype(BF16)) + b_ref[...]

    return pl.pallas_call(
        body,
        name="ada_mod",
        out_shape=jax.ShapeDtypeStruct((N_DEV, w_ada.shape[1]), F32),
        compiler_params=pltpu.CompilerParams(vmem_limit_bytes=VMEM_LIMIT),
    )(c_all, w_ada, b_cols)


def _fwd_in(x, mod, norm_g, w_t, tables):
    t = x.shape[0]
    tm = min(512, t)
    tn = IN_W // 4
    n_rope = (SEG_V - SEG_Q) // 128

    def body(x_ref, mod_ref, ng_ref, w_ref, cos_ref, s1_ref, s2_ref, proj_ref, h_ref):
        j = pl.program_id(1)

        @pl.when(j == 0)
        def _():
            xv = x_ref[...]
            r = lax.rsqrt(jnp.mean(xv * xv, axis=-1, keepdims=True) + NORM_EPS)
            shift = mod_ref[:, 0:D]
            scale = mod_ref[:, D : 2 * D]
            h_ref[...] = (((xv * r) * ng_ref[...]) * (1.0 + scale) + shift).astype(BF16)

        acc = _dot_nt(h_ref[...], w_ref[...])

        @pl.when(j == 0)
        def _():
            cos, s1, s2 = cos_ref[...], s1_ref[...], s2_ref[...]
            for g in range(n_rope):
                cols = slice(128 * g, 128 * (g + 1))
                proj_ref[:, cols] = _rope(acc[:, cols], cos, s1, s2)
            proj_ref[:, 128 * n_rope :] = acc[:, 128 * n_rope :]

        @pl.when(j != 0)
        def _():
            proj_ref[...] = acc

    tab = pl.BlockSpec((tm, 128), lambda i, j: (i, 0))
    return pl.pallas_call(
        body,
        name="fwd_in",
        grid=(t // tm, IN_W // tn),
        in_specs=[
            pl.BlockSpec((tm, D), lambda i, j: (i, 0)),
            pl.BlockSpec((1, 3 * D), lambda i, j: (0, 0)),
            pl.BlockSpec((1, D), lambda i, j: (0, 0)),
            pl.BlockSpec((tn, D), lambda i, j: (j, 0)),
            tab, tab, tab,
        ],
        out_specs=[pl.BlockSpec((tm, tn), lambda i, j: (i, j)), pl.BlockSpec((tm, D), lambda i, j: (i, 0))],
        out_shape=[jax.ShapeDtypeStruct((t, IN_W), F32), jax.ShapeDtypeStruct((t, D), BF16)],
        compiler_params=_params(2),
    )(x, mod, norm_g, w_t, *tables)


def _rope_tables(positions):
    half = ROT_DIM // 2
    inv_freq = ROPE_THETA ** (-jnp.arange(0, ROT_DIM, 2, dtype=F32) / ROT_DIM)
    pos = positions.astype(F32)
    lane = np.arange(128) % HEAD_DIM
    freq_lane = jnp.where(lane < ROT_DIM, jnp.tile(inv_freq, 128 // half), 0.0)
    ang = pos[:, None] * freq_lane[None, :]
    sin = jnp.sin(ang)
    lane_tables = (jnp.cos(ang), jnp.where(lane < half, -sin, 0.0), jnp.where((lane >= half) & (lane < ROT_DIM), sin, 0.0))
    ang_t = inv_freq[:, None] * pos[None, :]
    return lane_tables, (jnp.cos(ang_t), jnp.sin(ang_t))


def _rope(tv, cos, s1, s2):
    cols = []
    for g in range(tv.shape[1] // 128):
        blk = tv[:, 128 * g : 128 * (g + 1)]
        cols.append(blk * cos + pltpu.roll(blk, 128 - 8, 1) * s1 + pltpu.roll(blk, 8, 1) * s2)
    return cols[0] if len(cols) == 1 else jnp.concatenate(cols, axis=1)


def _rope_transposed(gv, cos, s1, s2):
    cols = []
    for g in range(gv.shape[1] // 128):
        blk = gv[:, 128 * g : 128 * (g + 1)]
        cols.append(blk * cos + pltpu.roll(blk * s1, 8, 1) + pltpu.roll(blk * s2, 128 - 8, 1))
    return cols[0] if len(cols) == 1 else jnp.concatenate(cols, axis=1)


def _lane_lo():
    return lax.broadcasted_iota(jnp.int32, (1, 128), 1) < HEAD_DIM


def _row_lo():
    return lax.broadcasted_iota(jnp.int32, (128, 1), 0) < HEAD_DIM


def _split_heads(tile, j):
    lo = _lane_lo()
    parts = []
    for g in (2 * j, 2 * j + 1):
        blk = tile[:, 128 * g : 128 * (g + 1)]
        parts += [jnp.where(lo, blk, 0.0), jnp.where(lo, 0.0, blk)]
    return jnp.concatenate(parts, axis=0)


def _dup_lanes(kv, j):
    rolled = pltpu.roll(kv, HEAD_DIM, 1)
    lo = _lane_lo()
    return jnp.where(lo, kv, rolled) if j == 0 else jnp.where(lo, rolled, kv)


def _fold_lanes(acc):
    return acc + pltpu.roll(acc, HEAD_DIM, 1)


def _fill_bias_t(bias_ref):
    kj = lax.broadcasted_iota(jnp.int32, (256, 128), 0)
    qi = lax.broadcasted_iota(jnp.int32, (256, 128), 1)
    diff = qi + WINDOW - kj
    band = (diff >= 0) & (diff < WINDOW)
    bias_ref[0] = jnp.where(band, 0.0, -1e30)
    bias_ref[1] = jnp.where(band & (kj >= 128), 0.0, -1e30)


def _sink_row(sink_ref, pair, j):
    return jnp.concatenate([jnp.full((1, 128), sink_ref[0, 8 * pair + 4 * j + h], F32) for h in range(4)], axis=1)


def _softmax_t(q4, kw, bias4, sink):
    s = _dot_nt(kw, q4) + bias4
    m = jnp.maximum(jnp.max(s, axis=0, keepdims=True), sink)
    p = jnp.exp(s - m)
    e_sink = jnp.exp(sink - m)
    denom = jnp.sum(p, axis=0, keepdims=True) + e_sink
    return p, e_sink, denom


def _attn_specs(t, tq):
    return dict(
        q=pl.BlockSpec((tq, 512), lambda p, i: (i, p)),
        k=pl.BlockSpec((t, 128), lambda p, i: (0, SEG_K // 128 + p)),
        v=pl.BlockSpec((t, 128), lambda p, i: (0, SEG_V // 128 + p)),
        sink=pl.BlockSpec(memory_space=pltpu.SMEM),
        act=pl.BlockSpec((tq, 512), lambda p, i: (i, p)),
    )


def _attn_fwd(proj, sinks):
    t = proj.shape[0]
    tq = min(512, t)
    nb = tq // 128
    sp = _attn_specs(t, tq)

    def body(q_ref, k_ref, v_ref, sink_ref, o_ref, kd, vdt, bias_ref):
        pair = pl.program_id(0)
        i = pl.program_id(1)

        @pl.when(i == 0)
        def _():
            kv, vv = k_ref[...], v_ref[...]
            for j in range(2):
                kd[j, 0:128, :] = jnp.zeros((128, 128), BF16)
                kd[j, 128:, :] = _dup_lanes(kv, j).astype(BF16)
                vdt[j, :, 0:128] = jnp.zeros((128, 128), BF16)
                vdt[j, :, 128:] = _dup_lanes(vv, j).T.astype(BF16)
            _fill_bias_t(bias_ref)

        row_lo = _row_lo()

        def block(b, carry):
            r0 = pl.multiple_of(b * 128, 128)
            gb = i * nb + b
            keys = pl.ds(pl.multiple_of(gb * 128, 128), 256)
            rows = pl.ds(r0, 128)
            qt = q_ref[rows, :] * (1.0 / math.sqrt(HEAD_DIM))
            bias = bias_ref[jnp.where(gb == 0, 1, 0)]
            bias4 = jnp.concatenate([bias] * 4, axis=1)
            cols = []
            for j in range(2):
                q4 = _split_heads(qt, j).astype(BF16)
                p, _, denom = _softmax_t(q4, kd[j, keys, :], bias4, _sink_row(sink_ref, pair, j))
                ot = _dot(vdt[j, :, keys], p.astype(BF16)) * (1.0 / denom)
                for gi in range(2):
                    z = jnp.where(row_lo, ot[:, 256 * gi : 256 * gi + 128], ot[:, 256 * gi + 128 : 256 * gi + 256])
                    cols.append(z.T)
            o_ref[rows, :] = jnp.concatenate(cols, axis=1)
            return carry

        lax.fori_loop(0, nb, block, 0)

    return pl.pallas_call(
        body,
        name="attn_fwd",
        grid=(2, t // tq),
        in_specs=[sp["q"], sp["k"], sp["v"], sp["sink"]],
        out_specs=sp["act"],
        out_shape=jax.ShapeDtypeStruct((t, D), F32),
        scratch_shapes=[pltpu.VMEM((2, t + 128, 128), BF16), pltpu.VMEM((2, 128, t + 128), BF16), pltpu.VMEM((2, 256, 128), F32)],
        compiler_params=_params(2),
    )(proj, proj, proj, sinks)


def _attn_bwd(proj, o, d_o, tables, sinks):
    t = proj.shape[0]
    tq = min(512, t)
    nb = tq // 128
    n_tiles = t // tq
    (cos, s1, s2), (cos_t, sin_t) = tables
    sp = _attn_specs(t, tq)
    scale = 1.0 / math.sqrt(HEAD_DIM)
    half = ROT_DIM // 2

    def unrope_rows(z, c, s):
        out = []
        for base in (0, HEAD_DIM):
            g0, g1 = z[base : base + half], z[base + half : base + ROT_DIM]
            out += [g0 * c + g1 * s, g1 * c - g0 * s, z[base + ROT_DIM : base + HEAD_DIM]]
        return jnp.concatenate(out, axis=0)

    def body(q_ref, k_ref, v_ref, o_ref, do_ref, ct_ref, st_ref, ck, s1k, s2k, sink_ref, dq_ref, dk_ref, dv_ref, dsink_ref,
             kd, kdt, vd, bias_ref, dkacc, dvacc):
        pair = pl.program_id(0)
        i = pl.program_id(1)

        @pl.when(i == 0)
        def _():
            kv, vv = k_ref[...], v_ref[...]
            for j in range(2):
                kdup = _dup_lanes(kv, j)
                kd[j, 0:128, :] = jnp.zeros((128, 128), BF16)
                kd[j, 128:, :] = kdup.astype(BF16)
                kdt[j, :, 0:128] = jnp.zeros((128, 128), BF16)
                kdt[j, :, 128:] = kdup.T.astype(BF16)
                vd[j, 0:128, :] = jnp.zeros((128, 128), BF16)
                vd[j, 128:, :] = _dup_lanes(vv, j).astype(BF16)
            _fill_bias_t(bias_ref)
            dkacc[...] = jnp.zeros_like(dkacc)
            dvacc[...] = jnp.zeros_like(dvacc)
            dsink_ref[...] = jnp.zeros_like(dsink_ref)

        row_lo = _row_lo()
        ones = jnp.ones((8, 128), BF16)

        def block(b, carry):
            r0 = pl.multiple_of(b * 128, 128)
            gb = i * nb + b
            keys = pl.ds(pl.multiple_of(gb * 128, 128), 256)
            rows = pl.ds(r0, 128)
            qt = q_ref[rows, :] * scale
            dob = do_ref[rows, :]
            prod = dob * o_ref[rows, :]
            ct, st = ct_ref[:, rows], st_ref[:, rows]
            bias = bias_ref[jnp.where(gb == 0, 1, 0)]
            bias4 = jnp.concatenate([bias] * 4, axis=1)
            cols = []
            for j in range(2):
                q4 = _split_heads(qt, j).astype(BF16)
                kw = kd[j, keys, :]
                p, e_sink, denom = _softmax_t(q4, kw, bias4, _sink_row(sink_ref, pair, j))
                inv = 1.0 / denom
                prob = p * inv
                do4 = _split_heads(dob, j).astype(BF16)
                delta = _dot_nt(ones, _split_heads(prod, j).astype(BF16))[0:1, :]
                dp = _dot_nt(vd[j, keys, :], do4)
                ds = (prob * (dp - delta)).astype(BF16)
                dkacc[j, keys, :] = dkacc[j, keys, :] + _dot(ds, q4)
                dvacc[j, keys, :] = dvacc[j, keys, :] + _dot(prob.astype(BF16), do4)
                dqt = _dot(kdt[j, :, keys], ds) * scale
                dsink = -(e_sink * inv) * delta
                for h in range(4):
                    row = 4 * j + h
                    total = jnp.sum(dsink[:, 128 * h : 128 * (h + 1)], axis=1, keepdims=True)
                    dsink_ref[0, row : row + 1, :] = dsink_ref[0, row : row + 1, :] + jnp.broadcast_to(total, (1, 128))
                for gi in range(2):
                    z = jnp.where(row_lo, dqt[:, 256 * gi : 256 * gi + 128], dqt[:, 256 * gi + 128 : 256 * gi + 256])
                    cols.append(unrope_rows(z, ct, st).T)
            dq_ref[rows, :] = jnp.concatenate(cols, axis=1).astype(BF16)
            return carry

        lax.fori_loop(0, nb, block, 0)

        @pl.when(i == n_tiles - 1)
        def _():
            lo = _lane_lo()
            dk = jnp.where(lo, _fold_lanes(dkacc[0, 128:, :]), _fold_lanes(dkacc[1, 128:, :]))
            dk_ref[...] = _rope_transposed(dk, ck[...], s1k[...], s2k[...]).astype(BF16)
            dv_ref[...] = jnp.where(lo, _fold_lanes(dvacc[0, 128:, :]), _fold_lanes(dvacc[1, 128:, :])).astype(BF16)

    kv_out = pl.BlockSpec((t, 128), lambda p, i: (0, p))
    tab_t = pl.BlockSpec((half, tq), lambda p, i: (0, i))
    tab_k = pl.BlockSpec((t, 128), lambda p, i: (0, 0))
    return pl.pallas_call(
        body,
        name="attn_bwd",
        grid=(2, n_tiles),
        in_specs=[sp["q"], sp["k"], sp["v"], sp["act"], sp["act"], tab_t, tab_t, tab_k, tab_k, tab_k, sp["sink"]],
        out_specs=[sp["act"], kv_out, kv_out, pl.BlockSpec((1, 8, 128), lambda p, i: (p, 0, 0))],
        out_shape=[
            jax.ShapeDtypeStruct((t, D), BF16),
            jax.ShapeDtypeStruct((t, 256), BF16),
            jax.ShapeDtypeStruct((t, 256), BF16),
            jax.ShapeDtypeStruct((2, 8, 128), F32),
        ],
        scratch_shapes=[pltpu.VMEM((2, t + 128, 128), BF16), pltpu.VMEM((2, 128, t + 128), BF16), pltpu.VMEM((2, t + 128, 128), BF16),
                        pltpu.VMEM((2, 256, 128), F32), pltpu.VMEM((2, t + 128, 128), F32), pltpu.VMEM((2, t + 128, 128), F32)],
        compiler_params=_params(2),
    )(proj, proj, proj, o, d_o, cos_t, sin_t, cos, s1, s2, sinks)


HALF = 512


def _softplus(y):
    u = jnp.exp(-jnp.abs(y))
    w = 1.0 + u
    log1p = jnp.where(w == 1.0, u, jnp.log(w) * (u / jnp.where(w == 1.0, 1.0, w - 1.0)))
    return jnp.maximum(y, 0.0) + log1p


def _gate_matmul(act_bf, w_ref, which):
    return jnp.concatenate([_dot(act_bf[:, 256 * blk : 256 * (blk + 1)], w_ref[which, blk]) for blk in range(2)], axis=1)


def _gate_matmul_t(grad_bf, w_ref, which):
    return jnp.concatenate([_dot_nt(grad_bf[:, 256 * blk : 256 * (blk + 1)], w_ref[which, blk]) for blk in range(2)], axis=1)


def _conv_and_gates(xe_ref, tm, cw_ref, cb_ref, w_ref, ba_ref, bx_ref, lam_ref, keep):
    xc = cb_ref[...] + sum(cw_ref[k : k + 1, :] * xe_ref[pl.ds(5 + k, tm), :] for k in range(CONV_WIDTH))
    xc_bf = xc.astype(BF16)
    r = _sigmoid(_gate_matmul(xc_bf, w_ref, 0) + ba_ref[...])
    ig = _sigmoid(_gate_matmul(xc_bf, w_ref, 1) + bx_ref[...])
    sp = _softplus(-lam_ref[...])
    log_a = -LRU_C * r * sp
    tanh = jnp.tanh(-log_a)
    mult = jnp.sqrt(2.0 * tanh / (1.0 + tanh))
    a = jnp.where(keep, jnp.exp(log_a), 0.0)
    mult = jnp.where(keep, mult, 1.0)
    return xc, xc_bf, r, ig, sp, a, mult


def _scan_rows(a_ref, b_ref, h_ref, carry_ref, tm, reverse):
    width = a_ref.shape[1]
    row = lax.broadcasted_iota(jnp.int32, (8, width), 0)

    def group(g, carry):
        gg = tm // 8 - 1 - g if reverse else g
        r0 = pl.multiple_of(gg * 8, 8)
        a = a_ref[pl.ds(r0, 8), :]
        b = b_ref[pl.ds(r0, 8), :]
        for s in (1, 2, 4):
            shift = 8 - s if reverse else s
            a_sh = pltpu.roll(a, shift, 0)
            b_sh = pltpu.roll(b, shift, 0)
            ok = (row < 8 - s) if reverse else (row >= s)
            b = jnp.where(ok, a * b_sh + b, b)
            a = jnp.where(ok, a * a_sh, a)
        h = a * carry + b
        h_ref[pl.ds(r0, 8), :] = h
        last = h[0:1, :] if reverse else h[7:8, :]
        return jnp.broadcast_to(last, (8, width))

    carry_ref[...] = lax.fori_loop(0, tm // 8, group, carry_ref[...])


def _rnn_specs(t, tm, order):
    n = t // tm
    return dict(
        tile=pl.BlockSpec((tm, HALF), lambda ch, i: (order(i, n), ch)),
        before=lambda col0: pl.BlockSpec((8, HALF), lambda ch, i: (jnp.maximum(order(i, n) * (tm // 8) - 1, 0), col0 + ch)),
        proj_tile=lambda col0: pl.BlockSpec((tm, HALF), lambda ch, i: (order(i, n), col0 + ch)),
        keep=pl.BlockSpec((tm, 1), lambda ch, i: (order(i, n), 0)),
        gates=pl.BlockSpec((2, 2, 256, 256), lambda ch, i: (0, ch, 0, 0)),
        vec=pl.BlockSpec((1, HALF), lambda ch, i: (0, ch)),
        conv=pl.BlockSpec((CONV_WIDTH, HALF), lambda ch, i: (0, ch)),
    )


def _rnn_fwd(proj, keep, gates_bf, rg_ba, rg_bx, rg_lambda, conv_w, conv_b):
    t = proj.shape[0]
    tm = min(512, t)
    sp = _rnn_specs(t, tm, lambda i, n: i)

    def body(xr_ref, xb_ref, keep_ref, w_ref, ba_ref, bx_ref, lam_ref, cw_ref, cb_ref, hh_ref, xe, a_buf, b_buf, carry):
        i = pl.program_id(1)

        @pl.when(i == 0)
        def _():
            carry[...] = jnp.zeros_like(carry)

        xe[0:8, :] = jnp.where(i > 0, xb_ref[...], 0.0)
        xe[8:, :] = xr_ref[...]
        keep_rows = keep_ref[...] != 0.0
        xc, _, _, ig, _, a, mult = _conv_and_gates(xe, tm, cw_ref, cb_ref, w_ref, ba_ref, bx_ref, lam_ref, keep_rows)
        a_buf[...] = a
        b_buf[...] = mult * (ig * xc)
        _scan_rows(a_buf, b_buf, hh_ref, carry, tm, reverse=False)

    return pl.pallas_call(
        body,
        name="rnn_fwd",
        grid=(2, t // tm),
        in_specs=[sp["proj_tile"](SEG_XR // HALF), sp["before"](SEG_XR // HALF), sp["keep"], sp["gates"], sp["vec"], sp["vec"], sp["vec"], sp["conv"], sp["vec"]],
        out_specs=sp["tile"],
        out_shape=jax.ShapeDtypeStruct((t, D), F32),
        scratch_shapes=[pltpu.VMEM((tm + 8, HALF), F32), pltpu.VMEM((tm, HALF), F32), pltpu.VMEM((tm, HALF), F32), pltpu.VMEM((8, HALF), F32)],
        compiler_params=_params(2),
    )(proj, proj, keep, gates_bf, rg_ba, rg_bx, rg_lambda, conv_w, conv_b)


def _rnn_bwd(proj, hh, d_hh, keep, gates_bf, rg_ba, rg_bx, rg_lambda, conv_w, conv_b):
    t = proj.shape[0]
    tm = min(512, t)
    n_tiles = t // tm
    sp = _rnn_specs(t, tm, lambda i, n: n - 1 - i)

    def body(xr_ref, xb_ref, hh_ref, hb_ref, dhh_ref, keep_ref, w_ref, ba_ref, bx_ref, lam_ref, cw_ref, cb_ref,
             dxr_ref, gw_ref, gvec_ref, xe, he, ae, c_buf, g_buf, dxe, carry, a_next):
        i = pl.program_id(1)
        first_tile = i == n_tiles - 1

        @pl.when(i == 0)
        def _():
            carry[...] = jnp.zeros_like(carry)
            a_next[...] = jnp.zeros_like(a_next)
            dxe[tm:, :] = jnp.zeros((8, HALF), F32)
            gw_ref[...] = jnp.zeros_like(gw_ref)
            gvec_ref[...] = jnp.zeros_like(gvec_ref)

        xe[0:8, :] = jnp.where(first_tile, 0.0, xb_ref[...])
        xe[8:, :] = xr_ref[...]
        he[0:8, :] = jnp.where(first_tile, 0.0, hb_ref[...])
        he[8:, :] = hh_ref[...]
        keep_rows = keep_ref[...] != 0.0
        xc, xc_bf, r, ig, sp_, a, mult = _conv_and_gates(xe, tm, cw_ref, cb_ref, w_ref, ba_ref, bx_ref, lam_ref, keep_rows)

        ae[0:tm, :] = a
        ae[tm:, :] = a_next[...]
        a_next[...] = jnp.broadcast_to(a[0:1, :], (8, HALF))
        c_buf[...] = ae[pl.ds(1, tm), :]
        g_buf[...] = dhh_ref[...]
        _scan_rows(c_buf, g_buf, g_buf, carry, tm, reverse=True)
        g = g_buf[...]

        h_prev = he[pl.ds(7, tm), :]
        bb = ig * xc
        d_ig = g * mult * xc
        d_xc = g * mult * ig
        d_log_a = jnp.where(keep_rows, g * h_prev * a - (g * bb) * (a * a) / mult, 0.0)
        d_r = d_log_a * (-LRU_C * sp_)
        d_sp = jnp.sum(d_log_a * (-LRU_C * r), axis=0, keepdims=True)
        d_zr = d_r * r * (1.0 - r)
        d_zi = d_ig * ig * (1.0 - ig)
        d_zr_bf = d_zr.astype(BF16)
        d_zi_bf = d_zi.astype(BF16)
        d_xc = d_xc + _gate_matmul_t(d_zr_bf, w_ref, 0) + _gate_matmul_t(d_zi_bf, w_ref, 1)
        for blk in range(2):
            cols = slice(256 * blk, 256 * (blk + 1))
            gw_ref[0, blk] = gw_ref[0, blk] + _dot_tn(xc_bf[:, cols], d_zr_bf[:, cols])
            gw_ref[1, blk] = gw_ref[1, blk] + _dot_tn(xc_bf[:, cols], d_zi_bf[:, cols])

        dxe[0:tm, :] = d_xc
        dxr = sum(cw_ref[k : k + 1, :] * dxe[pl.ds(3 - k, tm), :] for k in range(CONV_WIDTH))
        dxr_ref[...] = dxr.astype(BF16)
        dxe[tm:, :] = d_xc[0:8, :]

        lam = lam_ref[...]
        sums = [
            jnp.sum(d_zr, axis=0, keepdims=True),
            jnp.sum(d_zi, axis=0, keepdims=True),
            d_sp * (-_sigmoid(-lam)),
            jnp.sum(d_xc, axis=0, keepdims=True),
        ] + [jnp.sum(d_xc * xe[pl.ds(5 + k, tm), :], axis=0, keepdims=True) for k in range(CONV_WIDTH)]
        gvec_ref[...] = gvec_ref[...] + jnp.concatenate(sums, axis=0)

    return pl.pallas_call(
        body,
        name="rnn_bwd",
        grid=(2, n_tiles),
        in_specs=[
            sp["proj_tile"](SEG_XR // HALF), sp["before"](SEG_XR // HALF), sp["tile"], sp["before"](0), sp["tile"], sp["keep"],
            sp["gates"], sp["vec"], sp["vec"], sp["vec"], sp["conv"], sp["vec"],
        ],
        out_specs=[sp["tile"], sp["gates"], pl.BlockSpec((8, HALF), lambda ch, i: (0, ch))],
        out_shape=[
            jax.ShapeDtypeStruct((t, D), BF16),
            jax.ShapeDtypeStruct((2, 4, 256, 256), F32),
            jax.ShapeDtypeStruct((8, D), F32),
        ],
        scratch_shapes=[
            pltpu.VMEM((tm + 8, HALF), F32),
            pltpu.VMEM((tm + 8, HALF), F32),
            pltpu.VMEM((tm + 8, HALF), F32),
            pltpu.VMEM((tm, HALF), F32),
            pltpu.VMEM((tm, HALF), F32),
            pltpu.VMEM((tm + 8, HALF), F32),
            pltpu.VMEM((8, HALF), F32),
            pltpu.VMEM((8, HALF), F32),
        ],
        compiler_params=_params(2),
    )(proj, proj, hh, hh, d_hh, keep, gates_bf, rg_ba, rg_bx, rg_lambda, conv_w, conv_b)


def _mid(x, target, o, hh, proj, w3_bf, gate, final_g):
    t = x.shape[0]
    tm = min(256, t)
    n_tiles = t // tm
    full = pl.BlockSpec((tm, D), lambda i: (i, 0))
    half = lambda col: pl.BlockSpec((tm, 512), lambda i: (i, col))
    vec = pl.BlockSpec((1, D), lambda i: (0, 0))
    any_spec = pl.BlockSpec(memory_space=pl.ANY)

    def body(x_ref, tg_ref, o_ref, hh_ref, ga0, ga1, gr0, gr1, ma0, ma1, mr0, mr1, w_hbm, gate_ref, fg_ref,
             dres_ref, do_ref, dhh_ref, dga_ref, dgr_ref, dma_ref, dmr_ref, gw_hbm, vec_ref, w, gw, sem):
        i = pl.program_id(0)

        @pl.when(i == 0)
        def _():
            cp = pltpu.make_async_copy(w_hbm, w, sem)
            cp.start()
            cp.wait()
            gw[...] = jnp.zeros_like(gw)
            vec_ref[...] = jnp.zeros_like(vec_ref)

        cat = lambda a, b: jnp.concatenate([a[...], b[...]], axis=1)
        ga, gr, ma, mr = cat(ga0, ga1), cat(gr0, gr1), cat(ma0, ma1), cat(mr0, mr1)
        ov, hv, gate_v, fg = o_ref[...], hh_ref[...], gate_ref[...], fg_ref[...]
        sg_a = _sigmoid(ga)
        sg_r = _sigmoid(gr)
        silu_a = ga * sg_a
        silu_r = gr * sg_r
        ya = (ov * silu_a).astype(BF16)
        yr = (hv * silu_r).astype(BF16)
        pa = _dot(ya, w[0])
        pr = _dot(yr, w[1])
        sa = _sigmoid(ma)
        sr = _sigmoid(mr)
        merged = (sa * pa + sr * pr).astype(BF16)
        mo = _dot(merged, w[2])
        out = x_ref[...] + gate_v * mo
        r2 = lax.rsqrt(jnp.mean(out * out, axis=-1, keepdims=True) + NORM_EPS)
        outn = out * r2
        err = outn * fg - tg_ref[...]
        loss = 0.5 * jnp.sum(jnp.mean(err * err, axis=-1, keepdims=True), axis=0, keepdims=True)
        dy = err * (1.0 / D)
        g_final = jnp.sum(dy * outn, axis=0, keepdims=True)
        doutn = dy * fg
        dout = r2 * (doutn - outn * jnp.mean(doutn * outn, axis=-1, keepdims=True))
        dres_ref[...] = dout
        d_gate = jnp.sum(dout * mo, axis=0, keepdims=True)
        dmo = (dout * gate_v).astype(BF16)
        dmerged = _dot_nt(dmo, w[2])
        gw[2] = gw[2] + _dot_tn(merged, dmo)
        dpa = (dmerged * sa).astype(BF16)
        dpr = (dmerged * sr).astype(BF16)
        dma_ref[...] = (dmerged * pa * (sa * (1.0 - sa))).astype(BF16)
        dmr_ref[...] = (dmerged * pr * (sr * (1.0 - sr))).astype(BF16)
        dya = _dot_nt(dpa, w[0])
        dyr = _dot_nt(dpr, w[1])
        gw[0] = gw[0] + _dot_tn(ya, dpa)
        gw[1] = gw[1] + _dot_tn(yr, dpr)
        do_ref[...] = dya * silu_a
        dhh_ref[...] = dyr * silu_r
        dga_ref[...] = (dya * ov * (sg_a * (1.0 + ga * (1.0 - sg_a)))).astype(BF16)
        dgr_ref[...] = (dyr * hv * (sg_r * (1.0 + gr * (1.0 - sg_r)))).astype(BF16)
        zeros = jnp.zeros((5, D), F32)
        vec_ref[...] = vec_ref[...] + jnp.concatenate([g_final, d_gate, jnp.broadcast_to(loss, (1, D)), zeros], axis=0)

        @pl.when(i == n_tiles - 1)
        def _():
            cp = pltpu.make_async_copy(gw, gw_hbm, sem)
            cp.start()
            cp.wait()

    bf_tile = jax.ShapeDtypeStruct((t, D), BF16)
    f_tile = jax.ShapeDtypeStruct((t, D), F32)
    return pl.pallas_call(
        body,
        name="mid",
        grid=(n_tiles,),
        in_specs=[full, full, full, full,
                  half(SEG_GA // 512), half(SEG_GA // 512 + 1), half(SEG_GR // 512), half(SEG_GR // 512 + 1),
                  half(SEG_MA // 512), half(SEG_MA // 512 + 1), half(SEG_MR // 512), half(SEG_MR // 512 + 1),
                  any_spec, vec, vec],
        out_specs=[full, full, full, full, full, full, full, any_spec, pl.BlockSpec((8, D), lambda i: (0, 0))],
        out_shape=[f_tile, f_tile, f_tile, bf_tile, bf_tile, bf_tile, bf_tile,
                   jax.ShapeDtypeStruct((3, D, D), F32), jax.ShapeDtypeStruct((8, D), F32)],
        scratch_shapes=[pltpu.VMEM((3, D, D), BF16), pltpu.VMEM((3, D, D), F32), pltpu.SemaphoreType.DMA],
        compiler_params=_params(1),
    )(x, target, o, hh, proj, proj, proj, proj, proj, proj, proj, proj, w3_bf, gate, final_g)


_SEGMENTS = ((SEG_Q, 1024), (SEG_K, 256), (SEG_V, 256), (SEG_GA, 1024), (SEG_XR, 1024), (SEG_GR, 1024), (SEG_MA, 1024), (SEG_MR, 1024))


def _seg_specs(tm):
    return [pl.BlockSpec((tm, width), lambda i: (i, 0)) for _, width in _SEGMENTS]


def _bwd_dh(dsegs, w_t, x, dres, mod, norm_g):
    t = x.shape[0]
    tm = min(256, t)
    n_seg = len(_SEGMENTS)
    full = pl.BlockSpec((tm, D), lambda i: (i, 0))

    def body(*refs):
        segs = refs[:n_seg]
        w_hbm, x_ref, dres_ref, mod_ref, ng_ref, gx_ref, vec_ref, w, sem = refs[n_seg:]
        i = pl.program_id(0)

        @pl.when(i == 0)
        def _():
            cp = pltpu.make_async_copy(w_hbm, w, sem)
            cp.start()
            cp.wait()
            vec_ref[...] = jnp.zeros_like(vec_ref)

        dh = sum(_dot(seg[...], w[start : start + width, :]) for seg, (start, width) in zip(segs, _SEGMENTS))
        xv = x_ref[...]
        r1 = lax.rsqrt(jnp.mean(xv * xv, axis=-1, keepdims=True) + NORM_EPS)
        xn = xv * r1
        ng = ng_ref[...]
        scale1 = 1.0 + mod_ref[:, D : 2 * D]
        d_shift = jnp.sum(dh, axis=0, keepdims=True)
        d_scale = jnp.sum(dh * (xn * ng), axis=0, keepdims=True)
        g_norm = jnp.sum(dh * xn * scale1, axis=0, keepdims=True)
        dxn = dh * (ng * scale1)
        dx = r1 * (dxn - xn * jnp.mean(dxn * xn, axis=-1, keepdims=True))
        gx_ref[...] = dres_ref[...] + dx
        vec_ref[...] = vec_ref[...] + jnp.concatenate([d_shift, d_scale, g_norm, jnp.zeros((5, D), F32)], axis=0)

    return pl.pallas_call(
        body,
        name="bwd_dh",
        grid=(t // tm,),
        in_specs=_seg_specs(tm) + [pl.BlockSpec(memory_space=pl.ANY), full, full,
                                   pl.BlockSpec((1, 3 * D), lambda i: (0, 0)), pl.BlockSpec((1, D), lambda i: (0, 0))],
        out_specs=[full, pl.BlockSpec((8, D), lambda i: (0, 0))],
        out_shape=[jax.ShapeDtypeStruct((t, D), F32), jax.ShapeDtypeStruct((8, D), F32)],
        scratch_shapes=[pltpu.VMEM((IN_W, D), BF16), pltpu.SemaphoreType.DMA],
        compiler_params=_params(1),
    )(*dsegs, w_t, x, dres, mod, norm_g)


def _bwd_gw(dsegs, h):
    t = h.shape[0]
    tm = min(512, t)
    n_tiles = t // tm
    n_seg = len(_SEGMENTS)

    def body(*refs):
        segs = refs[:n_seg]
        h_ref, out_hbm, acc, sem = refs[n_seg:]
        i = pl.program_id(0)

        @pl.when(i == 0)
        def _():
            acc[...] = jnp.zeros_like(acc)

        hv = h_ref[...]
        for seg, (start, width) in zip(segs, _SEGMENTS):
            acc[start : start + width, :] = acc[start : start + width, :] + _dot_tn(seg[...], hv)

        @pl.when(i == n_tiles - 1)
        def _():
            cp = pltpu.make_async_copy(acc, out_hbm, sem)
            cp.start()
            cp.wait()

    return pl.pallas_call(
        body,
        name="bwd_gw",
        grid=(n_tiles,),
        in_specs=_seg_specs(tm) + [pl.BlockSpec((tm, D), lambda i: (i, 0))],
        out_specs=pl.BlockSpec(memory_space=pl.ANY),
        out_shape=jax.ShapeDtypeStruct((IN_W, D), F32),
        scratch_shapes=[pltpu.VMEM((IN_W, D), F32), pltpu.SemaphoreType.DMA],
        compiler_params=_params(1),
    )(*dsegs, h)


def _adamw(w, g, m, v):
    m = ADAM_B1 * m + (1.0 - ADAM_B1) * g
    v = ADAM_B2 * v + (1.0 - ADAM_B2) * (g * g)
    m_hat = m / (1.0 - ADAM_B1**ADAM_STEP)
    v_hat = v / (1.0 - ADAM_B2**ADAM_STEP)
    delta = -ADAM_LR * (m_hat / (jnp.sqrt(v_hat) + ADAM_EPS) + ADAM_WD * w)
    return delta, m, v


def _row_block(rows):
    return 128 if rows % 128 == 0 else rows


def _add_slots(own, l_ref):
    total = own
    for s in range(l_ref.shape[0]):
        total = total + l_ref[s].astype(F32)
    return total


def _sum_slots(name, own, land):
    n_slots, rows, cols = land.shape
    rb = 208 if rows == W_IN_SHARD else _row_block(rows)

    def body(o_ref, l_ref, g_ref):
        g_ref[...] = _add_slots(o_ref[...], l_ref)

    spec = pl.BlockSpec((rb, cols), lambda i: (i, 0))
    return pl.pallas_call(
        body,
        name=name,
        grid=(rows // rb,),
        in_specs=[spec, pl.BlockSpec((n_slots, rb, cols), lambda i: (0, i, 0))],
        out_specs=spec,
        out_shape=jax.ShapeDtypeStruct((rows, cols), F32),
        compiler_params=_params(1),
    )(own, land)


def _adam_update(name, g, w, m, v):
    rows, cols = w.shape
    rb = _row_block(rows)
    spec = pl.BlockSpec((rb, cols), lambda i: (i, 0))

    def body(g_ref, w_ref, m_ref, v_ref, d_ref, m2_ref, v2_ref):
        d_ref[...], m2_ref[...], v2_ref[...] = _adamw(w_ref[...], g_ref[...], m_ref[...], v_ref[...])

    shape = jax.ShapeDtypeStruct((rows, cols), F32)
    return pl.pallas_call(
        body, name=name, grid=(rows // rb,), in_specs=[spec] * 4, out_specs=[spec] * 3, out_shape=[shape] * 3,
        compiler_params=_params(1),
    )(g, w, m, v)


def _sum_adam(name, own, land, w, m, v):
    rows, cols = w.shape
    rb = _row_block(rows)
    spec = pl.BlockSpec((rb, cols), lambda i: (i, 0))
    n_slots = land.shape[0]

    def body(o_ref, l_ref, w_ref, m_ref, v_ref, g_ref, d_ref, m2_ref, v2_ref):
        total = _add_slots(o_ref[...], l_ref)
        g_ref[...] = total
        d_ref[...], m2_ref[...], v2_ref[...] = _adamw(w_ref[...], total, m_ref[...], v_ref[...])

    shape = jax.ShapeDtypeStruct((rows, cols), F32)
    return pl.pallas_call(
        body, name=name, grid=(rows // rb,),
        in_specs=[spec, pl.BlockSpec((n_slots, rb, cols), lambda i: (0, i, 0)), spec, spec, spec],
        out_specs=[spec] * 4, out_shape=[shape] * 4, compiler_params=_params(1),
    )(own, land, w, m, v)


def _ada_update(c_all_t, dmod_cols, w, m, v):
    def body(c_ref, dm_ref, w_ref, m_ref, v_ref, g_ref, d_ref, m2_ref, v2_ref):
        g = _dot(c_ref[...], dm_ref[...])
        g_ref[...] = g
        d_ref[...], m2_ref[...], v2_ref[...] = _adamw(w_ref[...], g, m_ref[...], v_ref[...])

    shape = jax.ShapeDtypeStruct(w.shape, F32)
    return pl.pallas_call(
        body, name="ada_update", out_shape=[shape] * 4, compiler_params=pltpu.CompilerParams(vmem_limit_bytes=VMEM_LIMIT)
    )(c_all_t, dmod_cols, w, m, v)


def _vec_update(gathered, w, m, v):
    n_rep = w.shape[1]

    def body(ga_ref, w_ref, m_ref, v_ref, g_ref, d_ref, m2_ref, v2_ref):
        total = ga_ref[0:1, :]
        for s in range(1, N_DEV):
            total = total + ga_ref[s : s + 1, :]
        g_ref[...] = total
        d_ref[...], m2_ref[...], v2_ref[...] = _adamw(w_ref[...], total[:, 0:n_rep], m_ref[...], v_ref[...])

    shape = jax.ShapeDtypeStruct(w.shape, F32)
    return pl.pallas_call(
        body, name="vec_update", out_shape=[jax.ShapeDtypeStruct((1, gathered.shape[1]), F32)] + [shape] * 3,
        compiler_params=pltpu.CompilerParams(vmem_limit_bytes=VMEM_LIMIT),
    )(gathered, w, m, v)


def kernel(x, c, positions, w_ada, b_ada, norm_g, w_in, attn_sinks, conv_w, conv_b, rg_wa, rg_ba, rg_wx, rg_bx, rg_lambda, w_attn_proj, w_rnn_proj, w_out, final_g, loss_target, m_w_ada, m_b_ada, m_norm_g, m_w_in, m_attn_sinks, m_conv_w, m_conv_b, m_rg_wa, m_rg_ba, m_rg_wx, m_rg_bx, m_rg_lambda, m_w_attn_proj, m_w_rnn_proj, m_w_out, m_final_g, v_w_ada, v_b_ada, v_norm_g, v_w_in, v_attn_sinks, v_conv_w, v_conv_b, v_rg_wa, v_rg_ba, v_rg_wx, v_rg_bx, v_rg_lambda, v_w_attn_proj, v_w_rnn_proj, v_w_out, v_final_g):
    me = _slot(_position())
    x2, target = x[0], loss_target[0]
    pos = positions[0]

    w_pack = jnp.concatenate(
        [w_in[0].T, w_attn_proj[0], w_rnn_proj[0], w_out[0], rg_wa[0].reshape(GATE_SHARD, D), rg_wx[0].reshape(GATE_SHARD, D)], axis=0
    ).astype(BF16)
    n_conv = conv_w.shape[2]
    conv_rows = jnp.pad(conv_w[0], ((0, 8 - CONV_WIDTH), (0, 0)))
    c_rows, conv_all, w_all = _all_gather("ag_fwd", [jnp.broadcast_to(c, (8, D)), conv_rows, w_pack])
    c_all = c_rows[:, 0, :]
    conv_full = conv_all[:, :CONV_WIDTH, :].transpose(1, 0, 2).reshape(CONV_WIDTH, D)
    r0 = W_IN_SHARD
    w_t = w_all[:, 0:r0, :].reshape(IN_W, D)
    w3_bf = jnp.stack([w_all[:, r0 + ROW_SHARD * k : r0 + ROW_SHARD * (k + 1), :].reshape(D, D) for k in range(3)])
    r1 = r0 + 3 * ROW_SHARD
    gates_bf = jnp.stack(
        [w_all[:, r1 + GATE_SHARD * k : r1 + GATE_SHARD * (k + 1), :].reshape(N_DEV, 4, GATE_SHARD, 256).transpose(1, 0, 2, 3).reshape(4, 256, 256)
         for k in range(2)]
    )

    n_ada = w_ada.shape[2]
    b_cols = lax.dynamic_slice(b_ada, (0, me * n_ada), (1, n_ada))
    mod_cols = _ada_mod(c_all, w_ada[0], b_cols)
    (mod_all,) = _all_gather("ag_mod", [mod_cols])
    mod = lax.dynamic_slice(mod_all, (0, me, 0), (N_DEV, 1, n_ada)).reshape(1, 3 * D)
    gate = mod[:, 2 * D :]

    tables = _rope_tables(pos)
    proj, h = _fwd_in(x2, mod, norm_g, w_t, tables[0])
    o = _attn_fwd(proj, attn_sinks)
    keep = (pos != 0).astype(F32)[:, None]
    hh = _rnn_fwd(proj, keep, gates_bf, rg_ba, rg_bx, rg_lambda, conv_full, conv_b)

    dres, d_o, d_hh, d_ga, d_gr, d_ma, d_mr, gw3, mid_vec = _mid(x2, target, o, hh, proj, w3_bf, gate, final_g[None, :])
    d_q, d_k, d_v, d_sink = _attn_bwd(proj, o, d_o, tables, attn_sinks)
    d_xr, g_gates, rnn_vec = _rnn_bwd(proj, hh, d_hh, keep, gates_bf, rg_ba, rg_bx, rg_lambda, conv_full, conv_b)
    dsegs = (d_q, d_k, d_v, d_ga, d_xr, d_gr, d_ma, d_mr)
    grad_x, dh_vec = _bwd_dh(dsegs, w_t, x2, dres, mod, norm_g)
    g_w_t = _bwd_gw(dsegs, h)
    loss = lax.psum(mid_vec[2, 0], AXES)

    g_sinks = jnp.pad(d_sink[:, :, 0].reshape(1, 16), ((0, 0), (0, 112)))
    vec = jnp.concatenate(
        [dh_vec[0:1], dh_vec[1:2], mid_vec[1:2],
         dh_vec[2:3], g_sinks, rnn_vec[3:4], rnn_vec[0:1], rnn_vec[1:2], rnn_vec[2:3], mid_vec[0:1],
         rnn_vec[4:8].reshape(1, CONV_WIDTH * D)], axis=1)
    (vec_all,) = _all_gather("ag_vec", [jnp.broadcast_to(vec, (8, vec.shape[1]))])
    vec_all = vec_all[:, 0, :]
    pad16 = lambda a: jnp.pad(a, ((0, 0), (0, 112)))
    rep = lambda b, n, s, cb, ba, bx, lam, fg: jnp.concatenate([b, n, pad16(s), cb, ba, bx, lam, fg[None, :]], axis=1)
    rep_w = rep(b_ada, norm_g, attn_sinks, conv_b, rg_ba, rg_bx, rg_lambda, final_g)
    rep_m = rep(m_b_ada, m_norm_g, m_attn_sinks, m_conv_b, m_rg_ba, m_rg_bx, m_rg_lambda, m_final_g)
    rep_v = rep(v_b_ada, v_norm_g, v_attn_sinks, v_conv_b, v_rg_ba, v_rg_bx, v_rg_lambda, v_final_g)
    g_vec, d_rep, m_rep, v_rep = _vec_update(vec_all, rep_w, rep_m, rep_v)

    def split_rep(a):
        offs = np.cumsum([0, 3 * D, D, 128, D, D, D, D, D])
        parts = [a[:, offs[k] : offs[k + 1]] for k in range(8)]
        parts[2] = parts[2][:, :16]
        parts[7] = parts[7][0]
        return parts

    n_rep = rep_w.shape[1]
    g_conv_full = g_vec[:, n_rep:].reshape(CONV_WIDTH, D)
    n_conv = conv_w.shape[2]
    g_conv = lax.dynamic_slice(g_conv_full, (0, me * n_conv), (CONV_WIDTH, n_conv))

    dmod_all = vec_all[:, 0 : 3 * D]
    dmod_cols = lax.dynamic_slice(dmod_all, (0, me * n_ada), (N_DEV, n_ada))
    g_ada, d_ada, m_ada, v_ada = _ada_update(c_all.T, dmod_cols, w_ada[0], m_w_ada[0], v_w_ada[0])

    my_x, my_y, my_c = _position()
    part_w = g_w_t.reshape(4, 2, W_IN_SHARD, D)
    part_3 = gw3.reshape(3, 4, 2, ROW_SHARD, D)
    part_g = g_gates.reshape(2, 4, 4, 2, GATE_SHARD, 256)
    recv_w, recv_3, recv_g = _exchange(
        "rs_pair",
        [part_w, part_3, part_g],
        [lambda ref, peer: ref.at[:, peer[2]], lambda ref, peer: ref.at[:, :, peer[2]], lambda ref, peer: ref.at[:, :, :, peer[2]]],
        [(4, W_IN_SHARD, D), (3, 4, ROW_SHARD, D), (2, 4, 4, GATE_SHARD, 256)],
        flips=(1,),
    )
    c_idx = my_c.astype(jnp.int32).reshape(1)
    rb = 208
    sum_w, sum_w_bf = _pair_add(
        "pair_add_w", part_w, recv_w[0], c_idx, (W_IN_SHARD // rb,),
        pl.BlockSpec((4, None, rb, D), lambda r, c_ref: (0, c_ref[0], r, 0)),
        pl.BlockSpec((4, rb, D), lambda r, c_ref: (0, r, 0)),
    )
    sum_3, sum_3_bf = _pair_add(
        "pair_add_3", part_3, recv_3[0], c_idx, (3,),
        pl.BlockSpec((None, 4, None, ROW_SHARD, D), lambda a, c_ref: (a, 0, c_ref[0], 0, 0)),
        pl.BlockSpec((None, 4, ROW_SHARD, D), lambda a, c_ref: (a, 0, 0, 0)),
    )
    sum_g, sum_g_bf = _pair_add(
        "pair_add_g", part_g, recv_g[0], c_idx, (1,),
        pl.BlockSpec((2, 4, 4, None, GATE_SHARD, 256), lambda a, c_ref: (0, 0, 0, c_ref[0], 0, 0)),
        pl.BlockSpec((2, 4, 4, GATE_SHARD, 256), lambda a, c_ref: (0, 0, 0, 0, 0)),
    )
    chip = lambda peer: 2 * peer[0] + peer[1]
    land_w, land_3, land_g = _exchange(
        "rs_chips",
        [sum_w_bf, sum_3_bf, sum_g_bf],
        [lambda ref, peer: ref.at[chip(peer)], lambda ref, peer: ref.at[:, chip(peer)], lambda ref, peer: ref.at[:, :, chip(peer)]],
        [(W_IN_SHARD, D), (3, ROW_SHARD, D), (2, 4, GATE_SHARD, 256)],
        flips=(2, 4, 6),
    )
    my_chip = 2 * my_x + my_y
    own_w = lax.dynamic_index_in_dim(sum_w, my_chip, 0, keepdims=False)
    own_3 = lax.dynamic_index_in_dim(sum_3, my_chip, 1, keepdims=False).reshape(3 * ROW_SHARD, D)
    gshape = (2 * 4 * GATE_SHARD, 256)
    own_g = lax.dynamic_index_in_dim(sum_g, my_chip, 2, keepdims=False).reshape(gshape)
    g_in = _sum_slots("sum_w_in", own_w, land_w).T
    d_in, m_in, v_in = _adam_update("adam_w_in", g_in, w_in[0], m_w_in[0], v_w_in[0])
    w3 = jnp.concatenate([w_attn_proj[0], w_rnn_proj[0], w_out[0]], axis=0)
    m3 = jnp.concatenate([m_w_attn_proj[0], m_w_rnn_proj[0], m_w_out[0]], axis=0)
    v3 = jnp.concatenate([v_w_attn_proj[0], v_w_rnn_proj[0], v_w_out[0]], axis=0)
    g3, d3, m3n, v3n = _sum_adam("adam_proj", own_3, land_3.reshape(3, 3 * ROW_SHARD, D), w3, m3, v3)
    wg = jnp.stack([rg_wa[0], rg_wx[0]]).reshape(gshape)
    mg = jnp.stack([m_rg_wa[0], m_rg_wx[0]]).reshape(gshape)
    vg = jnp.stack([v_rg_wa[0], v_rg_wx[0]]).reshape(gshape)
    gg, dg, mgn, vgn = _sum_adam("adam_gates", own_g, land_g.reshape((3,) + gshape), wg, mg, vg)
    d_conv, m_conv, v_conv = _adam_update("adam_conv", g_conv, conv_w[0], m_conv_w[0], v_conv_w[0])

    def by_weight(rep_parts, ada, w_in_, conv, gates, proj3):
        b, n, s, cb, ba, bx, lam, fg = rep_parts
        gates = gates.reshape(2, 1, 4, GATE_SHARD, 256)
        proj3 = proj3.reshape(3, 1, ROW_SHARD, D)
        return [ada[None], b, n, w_in_[None], s, conv[None], cb, gates[0], ba, gates[1], bx, lam, proj3[0], proj3[1], proj3[2], fg]

    grads = by_weight(split_rep(g_vec[:, :n_rep]), g_ada, g_in, g_conv, gg, g3)
    deltas = by_weight(split_rep(d_rep), d_ada, d_in, d_conv, dg, d3)
    new_m = by_weight(split_rep(m_rep), m_ada, m_in, m_conv, mgn, m3n)
    new_v = by_weight(split_rep(v_rep), v_ada, v_in, v_conv, vgn, v3n)
    return (loss, grad_x[None], *grads, *deltas, *new_m, *new_v)
```

```python
import functools
import math

import jax
import jax.numpy as jnp
import numpy as np
from jax import lax
from jax.experimental import pallas as pl
from jax.experimental.pallas import tpu as pltpu

F32 = jnp.float32
BF16 = jnp.bfloat16

D = 1024
HEAD_DIM = 64
ROT_DIM = 16
ROPE_THETA = 500000.0
WINDOW = 128
LRU_C = 8.0
CONV_WIDTH = 4
NORM_EPS = 1e-6
IN_W = 6656
SEG_Q, SEG_K, SEG_V, SEG_GA, SEG_XR, SEG_GR, SEG_MA, SEG_MR = 0, 1024, 1280, 1536, 2560, 3584, 4608, 5632
N_DEV = 8
W_IN_SHARD = IN_W // N_DEV
ROW_SHARD = D // N_DEV
GATE_SHARD = 256 // N_DEV

ADAM_LR = 0.001
ADAM_B1 = 0.9
ADAM_B2 = 0.999
ADAM_EPS = 1e-08
ADAM_WD = 0.01
ADAM_STEP = 10

VMEM_LIMIT = 60 * 1024 * 1024
MESH = pl.DeviceIdType.MESH
AXES = ("x", "y", "c")


def _dot(a, b):
    return jnp.dot(a, b, preferred_element_type=F32)


def _dot_nt(a, b):
    return lax.dot_general(a, b, (((1,), (1,)), ((), ())), preferred_element_type=F32)


def _dot_tn(a, b):
    return lax.dot_general(a, b, (((0,), (0,)), ((), ())), preferred_element_type=F32)


def _sigmoid(z):
    return 1.0 / (1.0 + jnp.exp(-z))


def _params(n_grid, vmem=VMEM_LIMIT):
    return pltpu.CompilerParams(dimension_semantics=("arbitrary",) * n_grid, vmem_limit_bytes=vmem)


def _position():
    return lax.axis_index("x"), lax.axis_index("y"), lax.axis_index("c")


def _flip(pos, k):
    x, y, c = pos
    return (1 - x if k & 4 else x, 1 - y if k & 2 else y, 1 - c if k & 1 else c)


def _slot(pos):
    return 4 * pos[0] + 2 * pos[1] + pos[2]


def _all_gather(name, arrs):
    n = len(arrs)

    def body(*refs):
        ins, outs = refs[:n], refs[n : 2 * n]
        send_sems, recv_sems, local_sems = refs[2 * n :]
        me = _position()
        sibling = _flip(me, 1)
        chips = [2, 4, 6]

        def copy(a, k, block, to, src=None):
            return pltpu.make_async_remote_copy(
                src_ref=outs[a].at[_slot(block)] if src is None else src,
                dst_ref=outs[a].at[_slot(block)],
                send_sem=send_sems.at[a * 7 + k],
                recv_sem=recv_sems.at[a * 7 + k],
                device_id=to,
                device_id_type=MESH,
            )

        mine = [pltpu.make_async_copy(ins[a], outs[a].at[_slot(me)], local_sems.at[a]) for a in range(n)]
        for cp in mine:
            cp.start()
        first = []
        for a in range(n):
            first.append(copy(a, 0, me, sibling, src=ins[a]))
            first += [copy(a, 1 + j, me, _flip(me, k), src=ins[a]) for j, k in enumerate(chips)]
        for cp in first:
            cp.start()
        passed = []
        for a in range(n):
            for j, k in enumerate(chips):
                copy(a, 1 + j, _flip(me, k), me).wait_recv()
                fwd = copy(a, 4 + j, _flip(me, k), sibling)
                fwd.start()
                passed.append(fwd)
        for a in range(n):
            copy(a, 0, sibling, me).wait_recv()
            for j, k in enumerate(chips):
                copy(a, 4 + j, _flip(sibling, k), me).wait_recv()
        for cp in first + passed:
            cp.wait_send()
        for cp in mine:
            cp.wait()

    any_spec = pl.BlockSpec(memory_space=pl.ANY)
    return pl.pallas_call(
        body,
        name=name,
        out_shape=[jax.ShapeDtypeStruct((N_DEV,) + a.shape, a.dtype) for a in arrs],
        in_specs=[any_spec] * n,
        out_specs=[any_spec] * n,
        scratch_shapes=[
            pltpu.SemaphoreType.DMA((7 * n,)),
            pltpu.SemaphoreType.DMA((7 * n,)),
            pltpu.SemaphoreType.DMA((n,)),
        ],
    )(*arrs)


def _exchange(name, arrs, views, out_shapes, flips):
    n = len(arrs)
    nf = len(flips)

    def body(*refs):
        ins, outs = refs[:n], refs[n : 2 * n]
        send_sems, recv_sems = refs[2 * n :]
        me = _position()

        def copy(a, j):
            peer = _flip(me, flips[j])
            return pltpu.make_async_remote_copy(
                src_ref=views[a](ins[a], peer),
                dst_ref=outs[a].at[j],
                send_sem=send_sems.at[a * nf + j],
                recv_sem=recv_sems.at[a * nf + j],
                device_id=peer,
                device_id_type=MESH,
            )

        sent = [copy(a, j) for a in range(n) for j in range(nf)]
        for cp in sent:
            cp.start()
        for cp in sent:
            cp.wait_recv()
        for cp in sent:
            cp.wait_send()

    any_spec = pl.BlockSpec(memory_space=pl.ANY)
    return pl.pallas_call(
        body,
        name=name,
        out_shape=[jax.ShapeDtypeStruct((nf,) + tuple(sh), a.dtype) for sh, a in zip(out_shapes, arrs)],
        in_specs=[any_spec] * n,
        out_specs=[any_spec] * n,
        scratch_shapes=[pltpu.SemaphoreType.DMA((nf * n,)), pltpu.SemaphoreType.DMA((nf * n,))],
    )(*arrs)


def _pair_add(name, own, recv, index, grid, chip_axis, own_spec, recv_spec, mine_spec, mine_shape):
    def body(idx_ref, own_ref, recv_ref, sb_ref, mine_ref):
        total = own_ref[...] + recv_ref[...]
        sb_ref[...] = total.astype(BF16)

        @pl.when(pl.program_id(chip_axis) == idx_ref[1])
        def _():
            mine_ref[...] = total

    return pl.pallas_call(
        body,
        name=name,
        grid_spec=pltpu.PrefetchScalarGridSpec(
            num_scalar_prefetch=1, grid=grid, in_specs=[own_spec, recv_spec], out_specs=[recv_spec, mine_spec]
        ),
        out_shape=[jax.ShapeDtypeStruct(recv.shape, BF16), jax.ShapeDtypeStruct(mine_shape, F32)],
        compiler_params=_params(len(grid)),
    )(index, own, recv)


def _ada_mod(c_all, w_ada, b_cols):
    def body(c_ref, w_ref, b_ref, o_ref):
        o_ref[...] = _dot(c_ref[...].astype(BF16), w_ref[...].astype(BF16)) + b_ref[...]

    return pl.pallas_call(
        body,
        name="ada_mod",
        out_shape=jax.ShapeDtypeStruct((N_DEV, w_ada.shape[1]), F32),
        compiler_params=pltpu.CompilerParams(vmem_limit_bytes=VMEM_LIMIT),
    )(c_all, w_ada, b_cols)


PACK_PROJ = W_IN_SHARD
PACK_GATE = PACK_PROJ + 3 * ROW_SHARD
PACK_ROWS = PACK_GATE + 2 * GATE_SHARD


def _fwd_in(x, mod, norm_g, w_all, tables):
    t = x.shape[0]
    tm = min(512, t)
    tn = IN_W // 4
    n_rope = (SEG_V - SEG_Q) // 128

    def body(x_ref, mod_ref, ng_ref, wa_ref, wb_ref, cos_ref, s1_ref, s2_ref, proj_ref, h_ref):
        j = pl.program_id(1)

        @pl.when(j == 0)
        def _():
            xv = x_ref[...]
            r = lax.rsqrt(jnp.mean(xv * xv, axis=-1, keepdims=True) + NORM_EPS)
            shift = mod_ref[:, 0:D]
            scale = mod_ref[:, D : 2 * D]
            h_ref[...] = (((xv * r) * ng_ref[...]) * (1.0 + scale) + shift).astype(BF16)

        acc = _dot_nt(h_ref[...], jnp.concatenate([wa_ref[...], wb_ref[...]], axis=0))

        @pl.when(j == 0)
        def _():
            cos, s1, s2 = cos_ref[...], s1_ref[...], s2_ref[...]
            for g in range(n_rope):
                cols = slice(128 * g, 128 * (g + 1))
                proj_ref[:, cols] = _rope(acc[:, cols], cos, s1, s2)
            proj_ref[:, 128 * n_rope :] = acc[:, 128 * n_rope :]

        @pl.when(j != 0)
        def _():
            proj_ref[...] = acc

    tab = pl.BlockSpec((tm, 128), lambda i, j: (i, 0))
    return pl.pallas_call(
        body,
        name="fwd_in",
        grid=(t // tm, IN_W // tn),
        in_specs=[
            pl.BlockSpec((tm, D), lambda i, j: (i, 0)),
            pl.BlockSpec((1, 3 * D), lambda i, j: (0, 0)),
            pl.BlockSpec((1, D), lambda i, j: (0, 0)),
            pl.BlockSpec((None, W_IN_SHARD, D), lambda i, j: (2 * j, 0, 0)),
            pl.BlockSpec((None, W_IN_SHARD, D), lambda i, j: (2 * j + 1, 0, 0)),
            tab, tab, tab,
        ],
        out_specs=[pl.BlockSpec((tm, tn), lambda i, j: (i, j)), pl.BlockSpec((tm, D), lambda i, j: (i, 0))],
        out_shape=[jax.ShapeDtypeStruct((t, IN_W), F32), jax.ShapeDtypeStruct((t, D), BF16)],
        compiler_params=_params(2),
    )(x, mod, norm_g, w_all, w_all, *tables)


def _rope_tables(positions):
    half = ROT_DIM // 2
    inv_freq = ROPE_THETA ** (-jnp.arange(0, ROT_DIM, 2, dtype=F32) / ROT_DIM)
    pos = positions.astype(F32)
    lane = np.arange(128) % HEAD_DIM
    freq_lane = jnp.where(lane < ROT_DIM, jnp.tile(inv_freq, 128 // half), 0.0)
    ang = pos[:, None] * freq_lane[None, :]
    sin = jnp.sin(ang)
    lane_tables = (jnp.cos(ang), jnp.where(lane < half, -sin, 0.0), jnp.where((lane >= half) & (lane < ROT_DIM), sin, 0.0))
    ang_t = inv_freq[:, None] * pos[None, :]
    return lane_tables, (jnp.cos(ang_t), jnp.sin(ang_t))


def _rope(tv, cos, s1, s2):
    cols = []
    for g in range(tv.shape[1] // 128):
        blk = tv[:, 128 * g : 128 * (g + 1)]
        cols.append(blk * cos + pltpu.roll(blk, 128 - 8, 1) * s1 + pltpu.roll(blk, 8, 1) * s2)
    return cols[0] if len(cols) == 1 else jnp.concatenate(cols, axis=1)


def _rope_transposed(gv, cos, s1, s2):
    cols = []
    for g in range(gv.shape[1] // 128):
        blk = gv[:, 128 * g : 128 * (g + 1)]
        cols.append(blk * cos + pltpu.roll(blk * s1, 8, 1) + pltpu.roll(blk * s2, 128 - 8, 1))
    return cols[0] if len(cols) == 1 else jnp.concatenate(cols, axis=1)


def _lane_lo():
    return lax.broadcasted_iota(jnp.int32, (1, 128), 1) < HEAD_DIM


def _row_lo():
    return lax.broadcasted_iota(jnp.int32, (128, 1), 0) < HEAD_DIM


def _split_heads(tile, j):
    lo = _lane_lo()
    parts = []
    for g in (2 * j, 2 * j + 1):
        blk = tile[:, 128 * g : 128 * (g + 1)]
        parts += [jnp.where(lo, blk, 0.0), jnp.where(lo, 0.0, blk)]
    return jnp.concatenate(parts, axis=0)


def _dup_lanes(kv, j):
    rolled = pltpu.roll(kv, HEAD_DIM, 1)
    lo = _lane_lo()
    return jnp.where(lo, kv, rolled) if j == 0 else jnp.where(lo, rolled, kv)


def _fold_lanes(acc):
    return acc + pltpu.roll(acc, HEAD_DIM, 1)


def _fill_bias_t(bias_ref):
    kj = lax.broadcasted_iota(jnp.int32, (256, 128), 0)
    qi = lax.broadcasted_iota(jnp.int32, (256, 128), 1)
    diff = qi + WINDOW - kj
    band = (diff >= 0) & (diff < WINDOW)
    bias_ref[0] = jnp.where(band, 0.0, -1e30)
    bias_ref[1] = jnp.where(band & (kj >= 128), 0.0, -1e30)


def _sink_row(sink_ref, pair, j):
    return jnp.concatenate([jnp.full((1, 128), sink_ref[0, 8 * pair + 4 * j + h], F32) for h in range(4)], axis=1)


def _softmax_t(q4, kw, bias4, sink):
    s = _dot_nt(kw, q4) + bias4
    m = jnp.maximum(jnp.max(s, axis=0, keepdims=True), sink)
    p = jnp.exp(s - m)
    e_sink = jnp.exp(sink - m)
    denom = jnp.sum(p, axis=0, keepdims=True) + e_sink
    return p, e_sink, denom


def _attn_specs(t, tq):
    return dict(
        q=pl.BlockSpec((tq, 512), lambda p, i: (i, p)),
        k=pl.BlockSpec((t, 128), lambda p, i: (0, SEG_K // 128 + p)),
        v=pl.BlockSpec((t, 128), lambda p, i: (0, SEG_V // 128 + p)),
        sink=pl.BlockSpec(memory_space=pltpu.SMEM),
        act=pl.BlockSpec((tq, 512), lambda p, i: (i, p)),
    )


def _attn_fwd(proj, sinks):
    t = proj.shape[0]
    tq = min(512, t)
    nb = tq // 128
    sp = _attn_specs(t, tq)

    def body(q_ref, k_ref, v_ref, sink_ref, o_ref, kd, vdt, bias_ref):
        pair = pl.program_id(0)
        i = pl.program_id(1)

        @pl.when(i == 0)
        def _():
            kv, vv = k_ref[...], v_ref[...]
            for j in range(2):
                kd[j, 0:128, :] = jnp.zeros((128, 128), BF16)
                kd[j, 128:, :] = _dup_lanes(kv, j).astype(BF16)
                vdt[j, :, 0:128] = jnp.zeros((128, 128), BF16)
                vdt[j, :, 128:] = _dup_lanes(vv, j).T.astype(BF16)
            _fill_bias_t(bias_ref)

        row_lo = _row_lo()

        def block(b, carry):
            r0 = pl.multiple_of(b * 128, 128)
            gb = i * nb + b
            keys = pl.ds(pl.multiple_of(gb * 128, 128), 256)
            rows = pl.ds(r0, 128)
            qt = q_ref[rows, :] * (1.0 / math.sqrt(HEAD_DIM))
            bias = bias_ref[jnp.where(gb == 0, 1, 0)]
            bias4 = jnp.concatenate([bias] * 4, axis=1)
            cols = []
            for j in range(2):
                q4 = _split_heads(qt, j).astype(BF16)
                p, _, denom = _softmax_t(q4, kd[j, keys, :], bias4, _sink_row(sink_ref, pair, j))
                ot = _dot(vdt[j, :, keys], p.astype(BF16)) * (1.0 / denom)
                for gi in range(2):
                    z = jnp.where(row_lo, ot[:, 256 * gi : 256 * gi + 128], ot[:, 256 * gi + 128 : 256 * gi + 256])
                    cols.append(z.T)
            o_ref[rows, :] = jnp.concatenate(cols, axis=1)
            return carry

        lax.fori_loop(0, nb, block, 0)

    return pl.pallas_call(
        body,
        name="attn_fwd",
        grid=(2, t // tq),
        in_specs=[sp["q"], sp["k"], sp["v"], sp["sink"]],
        out_specs=sp["act"],
        out_shape=jax.ShapeDtypeStruct((t, D), F32),
        scratch_shapes=[pltpu.VMEM((2, t + 128, 128), BF16), pltpu.VMEM((2, 128, t + 128), BF16), pltpu.VMEM((2, 256, 128), F32)],
        compiler_params=_params(2),
    )(proj, proj, proj, sinks)


def _attn_bwd(proj, o, d_o, tables, sinks):
    t = proj.shape[0]
    tq = min(512, t)
    nb = tq // 128
    n_tiles = t // tq
    (cos, s1, s2), (cos_t, sin_t) = tables
    sp = _attn_specs(t, tq)
    scale = 1.0 / math.sqrt(HEAD_DIM)
    half = ROT_DIM // 2

    def unrope_rows(z, c, s):
        out = []
        for base in (0, HEAD_DIM):
            g0, g1 = z[base : base + half], z[base + half : base + ROT_DIM]
            out += [g0 * c + g1 * s, g1 * c - g0 * s, z[base + ROT_DIM : base + HEAD_DIM]]
        return jnp.concatenate(out, axis=0)

    def body(q_ref, k_ref, v_ref, o_ref, do_ref, ct_ref, st_ref, ck, s1k, s2k, sink_ref, dq_ref, dk_ref, dv_ref, dsink_ref,
             kd, kdt, vd, bias_ref, dkacc, dvacc):
        pair = pl.program_id(0)
        i = pl.program_id(1)

        @pl.when(i == 0)
        def _():
            kv, vv = k_ref[...], v_ref[...]
            for j in range(2):
                kdup = _dup_lanes(kv, j)
                kd[j, 0:128, :] = jnp.zeros((128, 128), BF16)
                kd[j, 128:, :] = kdup.astype(BF16)
                kdt[j, :, 0:128] = jnp.zeros((128, 128), BF16)
                kdt[j, :, 128:] = kdup.T.astype(BF16)
                vd[j, 0:128, :] = jnp.zeros((128, 128), BF16)
                vd[j, 128:, :] = _dup_lanes(vv, j).astype(BF16)
            _fill_bias_t(bias_ref)
            dkacc[...] = jnp.zeros_like(dkacc)
            dvacc[...] = jnp.zeros_like(dvacc)
            dsink_ref[...] = jnp.zeros_like(dsink_ref)

        row_lo = _row_lo()
        ones = jnp.ones((8, 128), BF16)

        def block(b, carry):
            r0 = pl.multiple_of(b * 128, 128)
            gb = i * nb + b
            keys = pl.ds(pl.multiple_of(gb * 128, 128), 256)
            rows = pl.ds(r0, 128)
            qt = q_ref[rows, :] * scale
            dob = do_ref[rows, :]
            prod = dob * o_ref[rows, :]
            ct, st = ct_ref[:, rows], st_ref[:, rows]
            bias = bias_ref[jnp.where(gb == 0, 1, 0)]
            bias4 = jnp.concatenate([bias] * 4, axis=1)
            cols = []
            for j in range(2):
                q4 = _split_heads(qt, j).astype(BF16)
                kw = kd[j, keys, :]
                p, e_sink, denom = _softmax_t(q4, kw, bias4, _sink_row(sink_ref, pair, j))
                inv = 1.0 / denom
                prob = p * inv
                do4 = _split_heads(dob, j).astype(BF16)
                delta = _dot_nt(ones, _split_heads(prod, j).astype(BF16))[0:1, :]
                dp = _dot_nt(vd[j, keys, :], do4)
                ds = (prob * (dp - delta)).astype(BF16)
                dkacc[j, keys, :] = dkacc[j, keys, :] + _dot(ds, q4)
                dvacc[j, keys, :] = dvacc[j, keys, :] + _dot(prob.astype(BF16), do4)
                dqt = _dot(kdt[j, :, keys], ds) * scale
                dsink = -(e_sink * inv) * delta
                for h in range(4):
                    row = 4 * j + h
                    total = jnp.sum(dsink[:, 128 * h : 128 * (h + 1)], axis=1, keepdims=True)
                    dsink_ref[0, row : row + 1, :] = dsink_ref[0, row : row + 1, :] + jnp.broadcast_to(total, (1, 128))
                for gi in range(2):
                    z = jnp.where(row_lo, dqt[:, 256 * gi : 256 * gi + 128], dqt[:, 256 * gi + 128 : 256 * gi + 256])
                    cols.append(unrope_rows(z, ct, st).T)
            dq_ref[rows, :] = jnp.concatenate(cols, axis=1).astype(BF16)
            return carry

        lax.fori_loop(0, nb, block, 0)

        @pl.when(i == n_tiles - 1)
        def _():
            lo = _lane_lo()
            dk = jnp.where(lo, _fold_lanes(dkacc[0, 128:, :]), _fold_lanes(dkacc[1, 128:, :]))
            dk_ref[...] = _rope_transposed(dk, ck[...], s1k[...], s2k[...]).astype(BF16)
            dv_ref[...] = jnp.where(lo, _fold_lanes(dvacc[0, 128:, :]), _fold_lanes(dvacc[1, 128:, :])).astype(BF16)

    kv_out = pl.BlockSpec((t, 128), lambda p, i: (0, p))
    tab_t = pl.BlockSpec((half, tq), lambda p, i: (0, i))
    tab_k = pl.BlockSpec((t, 128), lambda p, i: (0, 0))
    return pl.pallas_call(
        body,
        name="attn_bwd",
        grid=(2, n_tiles),
        in_specs=[sp["q"], sp["k"], sp["v"], sp["act"], sp["act"], tab_t, tab_t, tab_k, tab_k, tab_k, sp["sink"]],
        out_specs=[sp["act"], kv_out, kv_out, pl.BlockSpec((1, 8, 128), lambda p, i: (p, 0, 0))],
        out_shape=[
            jax.ShapeDtypeStruct((t, D), BF16),
            jax.ShapeDtypeStruct((t, 256), BF16),
            jax.ShapeDtypeStruct((t, 256), BF16),
            jax.ShapeDtypeStruct((2, 8, 128), F32),
        ],
        scratch_shapes=[pltpu.VMEM((2, t + 128, 128), BF16), pltpu.VMEM((2, 128, t + 128), BF16), pltpu.VMEM((2, t + 128, 128), BF16),
                        pltpu.VMEM((2, 256, 128), F32), pltpu.VMEM((2, t + 128, 128), F32), pltpu.VMEM((2, t + 128, 128), F32)],
        compiler_params=_params(2),
    )(proj, proj, proj, o, d_o, cos_t, sin_t, cos, s1, s2, sinks)


HALF = 512


def _softplus(y):
    u = jnp.exp(-jnp.abs(y))
    w = 1.0 + u
    log1p = jnp.where(w == 1.0, u, jnp.log(w) * (u / jnp.where(w == 1.0, 1.0, w - 1.0)))
    return jnp.maximum(y, 0.0) + log1p


def _gate_block(w_ref, blk):
    return w_ref[:, :, 256 * blk : 256 * (blk + 1)].reshape(256, 256)


def _gate_matmul(act_bf, w_ref):
    return jnp.concatenate([_dot(act_bf[:, 256 * blk : 256 * (blk + 1)], _gate_block(w_ref, blk)) for blk in range(2)], axis=1)


def _gate_matmul_t(grad_bf, w_ref):
    return jnp.concatenate([_dot_nt(grad_bf[:, 256 * blk : 256 * (blk + 1)], _gate_block(w_ref, blk)) for blk in range(2)], axis=1)


def _conv_and_gates(xe_ref, tm, cw_ref, cb_ref, wa_ref, wx_ref, ba_ref, bx_ref, lam_ref, keep):
    xc = cb_ref[...] + sum(cw_ref[k : k + 1, :] * xe_ref[pl.ds(5 + k, tm), :] for k in range(CONV_WIDTH))
    xc_bf = xc.astype(BF16)
    r = _sigmoid(_gate_matmul(xc_bf, wa_ref) + ba_ref[...])
    ig = _sigmoid(_gate_matmul(xc_bf, wx_ref) + bx_ref[...])
    sp = _softplus(-lam_ref[...])
    log_a = -LRU_C * r * sp
    tanh = jnp.tanh(-log_a)
    mult = jnp.sqrt(2.0 * tanh / (1.0 + tanh))
    a = jnp.where(keep, jnp.exp(log_a), 0.0)
    mult = jnp.where(keep, mult, 1.0)
    return xc, xc_bf, r, ig, sp, a, mult


def _scan_rows(a_ref, b_ref, h_ref, carry_ref, tm, reverse):
    width = a_ref.shape[1]
    row = lax.broadcasted_iota(jnp.int32, (8, width), 0)

    def group(g, carry):
        gg = tm // 8 - 1 - g if reverse else g
        r0 = pl.multiple_of(gg * 8, 8)
        a = a_ref[pl.ds(r0, 8), :]
        b = b_ref[pl.ds(r0, 8), :]
        for s in (1, 2, 4):
            shift = 8 - s if reverse else s
            a_sh = pltpu.roll(a, shift, 0)
            b_sh = pltpu.roll(b, shift, 0)
            ok = (row < 8 - s) if reverse else (row >= s)
            b = jnp.where(ok, a * b_sh + b, b)
            a = jnp.where(ok, a * a_sh, a)
        h = a * carry + b
        h_ref[pl.ds(r0, 8), :] = h
        last = h[0:1, :] if reverse else h[7:8, :]
        return jnp.broadcast_to(last, (8, width))

    carry_ref[...] = lax.fori_loop(0, tm // 8, group, carry_ref[...])


def _rnn_specs(t, tm, order):
    n = t // tm
    return dict(
        tile=pl.BlockSpec((tm, HALF), lambda ch, i: (order(i, n), ch)),
        before=lambda col0: pl.BlockSpec((8, HALF), lambda ch, i: (jnp.maximum(order(i, n) * (tm // 8) - 1, 0), col0 + ch)),
        proj_tile=lambda col0: pl.BlockSpec((tm, HALF), lambda ch, i: (order(i, n), col0 + ch)),
        keep=pl.BlockSpec((tm, 1), lambda ch, i: (order(i, n), 0)),
        gates=lambda which: pl.BlockSpec((N_DEV, GATE_SHARD, HALF), lambda ch, i: (0, PACK_GATE // GATE_SHARD + which, ch)),
        gate_grads=pl.BlockSpec((2, 2, 256, 256), lambda ch, i: (0, ch, 0, 0)),
        vec=pl.BlockSpec((1, HALF), lambda ch, i: (0, ch)),
        conv=pl.BlockSpec((CONV_WIDTH, HALF), lambda ch, i: (0, ch)),
    )


def _rnn_fwd(proj, keep, w_all, rg_ba, rg_bx, rg_lambda, conv_w, conv_b):
    t = proj.shape[0]
    tm = min(512, t)
    sp = _rnn_specs(t, tm, lambda i, n: i)

    def body(xr_ref, xb_ref, keep_ref, wa_ref, wx_ref, ba_ref, bx_ref, lam_ref, cw_ref, cb_ref, hh_ref, xe, a_buf, b_buf, carry):
        i = pl.program_id(1)

        @pl.when(i == 0)
        def _():
            carry[...] = jnp.zeros_like(carry)

        xe[0:8, :] = jnp.where(i > 0, xb_ref[...], 0.0)
        xe[8:, :] = xr_ref[...]
        keep_rows = keep_ref[...] != 0.0
        xc, _, _, ig, _, a, mult = _conv_and_gates(xe, tm, cw_ref, cb_ref, wa_ref, wx_ref, ba_ref, bx_ref, lam_ref, keep_rows)
        a_buf[...] = a
        b_buf[...] = mult * (ig * xc)
        _scan_rows(a_buf, b_buf, hh_ref, carry, tm, reverse=False)

    return pl.pallas_call(
        body,
        name="rnn_fwd",
        grid=(2, t // tm),
        in_specs=[sp["proj_tile"](SEG_XR // HALF), sp["before"](SEG_XR // HALF), sp["keep"], sp["gates"](0), sp["gates"](1),
                  sp["vec"], sp["vec"], sp["vec"], sp["conv"], sp["vec"]],
        out_specs=sp["tile"],
        out_shape=jax.ShapeDtypeStruct((t, D), F32),
        scratch_shapes=[pltpu.VMEM((tm + 8, HALF), F32), pltpu.VMEM((tm, HALF), F32), pltpu.VMEM((tm, HALF), F32), pltpu.VMEM((8, HALF), F32)],
        compiler_params=_params(2),
    )(proj, proj, keep, w_all, w_all, rg_ba, rg_bx, rg_lambda, conv_w, conv_b)


def _rnn_bwd(proj, hh, d_hh, keep, w_all, rg_ba, rg_bx, rg_lambda, conv_w, conv_b):
    t = proj.shape[0]
    tm = min(512, t)
    n_tiles = t // tm
    sp = _rnn_specs(t, tm, lambda i, n: n - 1 - i)

    def body(xr_ref, xb_ref, hh_ref, hb_ref, dhh_ref, keep_ref, wa_ref, wx_ref, ba_ref, bx_ref, lam_ref, cw_ref, cb_ref,
             dxr_ref, gw_ref, gvec_ref, xe, he, ae, c_buf, g_buf, dxe, carry, a_next):
        i = pl.program_id(1)
        first_tile = i == n_tiles - 1

        @pl.when(i == 0)
        def _():
            carry[...] = jnp.zeros_like(carry)
            a_next[...] = jnp.zeros_like(a_next)
            dxe[tm:, :] = jnp.zeros((8, HALF), F32)
            gw_ref[...] = jnp.zeros_like(gw_ref)
            gvec_ref[...] = jnp.zeros_like(gvec_ref)

        xe[0:8, :] = jnp.where(first_tile, 0.0, xb_ref[...])
        xe[8:, :] = xr_ref[...]
        he[0:8, :] = jnp.where(first_tile, 0.0, hb_ref[...])
        he[8:, :] = hh_ref[...]
        keep_rows = keep_ref[...] != 0.0
        xc, xc_bf, r, ig, sp_, a, mult = _conv_and_gates(xe, tm, cw_ref, cb_ref, wa_ref, wx_ref, ba_ref, bx_ref, lam_ref, keep_rows)

        ae[0:tm, :] = a
        ae[tm:, :] = a_next[...]
        a_next[...] = jnp.broadcast_to(a[0:1, :], (8, HALF))
        c_buf[...] = ae[pl.ds(1, tm), :]
        g_buf[...] = dhh_ref[...]
        _scan_rows(c_buf, g_buf, g_buf, carry, tm, reverse=True)
        g = g_buf[...]

        h_prev = he[pl.ds(7, tm), :]
        bb = ig * xc
        d_ig = g * mult * xc
        d_xc = g * mult * ig
        d_log_a = jnp.where(keep_rows, g * h_prev * a - (g * bb) * (a * a) / mult, 0.0)
        d_r = d_log_a * (-LRU_C * sp_)
        d_sp = jnp.sum(d_log_a * (-LRU_C * r), axis=0, keepdims=True)
        d_zr = d_r * r * (1.0 - r)
        d_zi = d_ig * ig * (1.0 - ig)
        d_zr_bf = d_zr.astype(BF16)
        d_zi_bf = d_zi.astype(BF16)
        d_xc = d_xc + _gate_matmul_t(d_zr_bf, wa_ref) + _gate_matmul_t(d_zi_bf, wx_ref)
        for blk in range(2):
            cols = slice(256 * blk, 256 * (blk + 1))
            gw_ref[0, blk] = gw_ref[0, blk] + _dot_tn(xc_bf[:, cols], d_zr_bf[:, cols])
            gw_ref[1, blk] = gw_ref[1, blk] + _dot_tn(xc_bf[:, cols], d_zi_bf[:, cols])

        dxe[0:tm, :] = d_xc
        dxr = sum(cw_ref[k : k + 1, :] * dxe[pl.ds(3 - k, tm), :] for k in range(CONV_WIDTH))
        dxr_ref[...] = dxr.astype(BF16)
        dxe[tm:, :] = d_xc[0:8, :]

        lam = lam_ref[...]
        sums = [
            jnp.sum(d_zr, axis=0, keepdims=True),
            jnp.sum(d_zi, axis=0, keepdims=True),
            d_sp * (-_sigmoid(-lam)),
            jnp.sum(d_xc, axis=0, keepdims=True),
        ] + [jnp.sum(d_xc * xe[pl.ds(5 + k, tm), :], axis=0, keepdims=True) for k in range(CONV_WIDTH)]
        gvec_ref[...] = gvec_ref[...] + jnp.concatenate(sums, axis=0)

    return pl.pallas_call(
        body,
        name="rnn_bwd",
        grid=(2, n_tiles),
        in_specs=[
            sp["proj_tile"](SEG_XR // HALF), sp["before"](SEG_XR // HALF), sp["tile"], sp["before"](0), sp["tile"], sp["keep"],
            sp["gates"](0), sp["gates"](1), sp["vec"], sp["vec"], sp["vec"], sp["conv"], sp["vec"],
        ],
        out_specs=[sp["tile"], sp["gate_grads"], pl.BlockSpec((8, HALF), lambda ch, i: (0, ch))],
        out_shape=[
            jax.ShapeDtypeStruct((t, D), BF16),
            jax.ShapeDtypeStruct((2, 4, 256, 256), F32),
            jax.ShapeDtypeStruct((8, D), F32),
        ],
        scratch_shapes=[
            pltpu.VMEM((tm + 8, HALF), F32),
            pltpu.VMEM((tm + 8, HALF), F32),
            pltpu.VMEM((tm + 8, HALF), F32),
            pltpu.VMEM((tm, HALF), F32),
            pltpu.VMEM((tm, HALF), F32),
            pltpu.VMEM((tm + 8, HALF), F32),
            pltpu.VMEM((8, HALF), F32),
            pltpu.VMEM((8, HALF), F32),
        ],
        compiler_params=_params(2),
    )(proj, proj, hh, hh, d_hh, keep, w_all, w_all, rg_ba, rg_bx, rg_lambda, conv_w, conv_b)


def _mid(x, target, o, hh, proj, w_all, gate, final_g):
    t = x.shape[0]
    tm = min(256, t)
    n_tiles = t // tm
    full = pl.BlockSpec((tm, D), lambda i: (i, 0))
    half = lambda col: pl.BlockSpec((tm, 512), lambda i: (i, col))
    vec = pl.BlockSpec((1, D), lambda i: (0, 0))
    any_spec = pl.BlockSpec(memory_space=pl.ANY)

    def body(x_ref, tg_ref, o_ref, hh_ref, ga0, ga1, gr0, gr1, ma0, ma1, mr0, mr1, w_hbm, gate_ref, fg_ref,
             dres_ref, do_ref, dhh_ref, dga_ref, dgr_ref, dma_ref, dmr_ref, gw_hbm, vec_ref, w, gw, sem, w_sems):
        i = pl.program_id(0)

        @pl.when(i == 0)
        def _():
            loads = [
                pltpu.make_async_copy(
                    w_hbm.at[s, pl.ds(PACK_PROJ + ROW_SHARD * k, ROW_SHARD), :], w.at[k, pl.ds(ROW_SHARD * s, ROW_SHARD), :], w_sems.at[k * N_DEV + s]
                )
                for k in range(3)
                for s in range(N_DEV)
            ]
            for cp in loads:
                cp.start()
            for cp in loads:
                cp.wait()
            gw[...] = jnp.zeros_like(gw)
            vec_ref[...] = jnp.zeros_like(vec_ref)

        cat = lambda a, b: jnp.concatenate([a[...], b[...]], axis=1)
        ga, gr, ma, mr = cat(ga0, ga1), cat(gr0, gr1), cat(ma0, ma1), cat(mr0, mr1)
        ov, hv, gate_v, fg = o_ref[...], hh_ref[...], gate_ref[...], fg_ref[...]
        sg_a = _sigmoid(ga)
        sg_r = _sigmoid(gr)
        silu_a = ga * sg_a
        silu_r = gr * sg_r
        ya = (ov * silu_a).astype(BF16)
        yr = (hv * silu_r).astype(BF16)
        pa = _dot(ya, w[0])
        pr = _dot(yr, w[1])
        sa = _sigmoid(ma)
        sr = _sigmoid(mr)
        merged = (sa * pa + sr * pr).astype(BF16)
        mo = _dot(merged, w[2])
        out = x_ref[...] + gate_v * mo
        r2 = lax.rsqrt(jnp.mean(out * out, axis=-1, keepdims=True) + NORM_EPS)
        outn = out * r2
        err = outn * fg - tg_ref[...]
        loss = 0.5 * jnp.sum(jnp.mean(err * err, axis=-1, keepdims=True), axis=0, keepdims=True)
        dy = err * (1.0 / D)
        g_final = jnp.sum(dy * outn, axis=0, keepdims=True)
        doutn = dy * fg
        dout = r2 * (doutn - outn * jnp.mean(doutn * outn, axis=-1, keepdims=True))
        dres_ref[...] = dout
        d_gate = jnp.sum(dout * mo, axis=0, keepdims=True)
        dmo = (dout * gate_v).astype(BF16)
        dmerged = _dot_nt(dmo, w[2])
        gw[2] = gw[2] + _dot_tn(merged, dmo)
        dpa = (dmerged * sa).astype(BF16)
        dpr = (dmerged * sr).astype(BF16)
        dma_ref[...] = (dmerged * pa * (sa * (1.0 - sa))).astype(BF16)
        dmr_ref[...] = (dmerged * pr * (sr * (1.0 - sr))).astype(BF16)
        dya = _dot_nt(dpa, w[0])
        dyr = _dot_nt(dpr, w[1])
        gw[0] = gw[0] + _dot_tn(ya, dpa)
        gw[1] = gw[1] + _dot_tn(yr, dpr)
        do_ref[...] = dya * silu_a
        dhh_ref[...] = dyr * silu_r
        dga_ref[...] = (dya * ov * (sg_a * (1.0 + ga * (1.0 - sg_a)))).astype(BF16)
        dgr_ref[...] = (dyr * hv * (sg_r * (1.0 + gr * (1.0 - sg_r)))).astype(BF16)
        zeros = jnp.zeros((5, D), F32)
        vec_ref[...] = vec_ref[...] + jnp.concatenate([g_final, d_gate, jnp.broadcast_to(loss, (1, D)), zeros], axis=0)

        @pl.when(i == n_tiles - 1)
        def _():
            cp = pltpu.make_async_copy(gw, gw_hbm, sem)
            cp.start()
            cp.wait()

    bf_tile = jax.ShapeDtypeStruct((t, D), BF16)
    f_tile = jax.ShapeDtypeStruct((t, D), F32)
    return pl.pallas_call(
        body,
        name="mid",
        grid=(n_tiles,),
        in_specs=[full, full, full, full,
                  half(SEG_GA // 512), half(SEG_GA // 512 + 1), half(SEG_GR // 512), half(SEG_GR // 512 + 1),
                  half(SEG_MA // 512), half(SEG_MA // 512 + 1), half(SEG_MR // 512), half(SEG_MR // 512 + 1),
                  any_spec, vec, vec],
        out_specs=[full, full, full, full, full, full, full, any_spec, pl.BlockSpec((8, D), lambda i: (0, 0))],
        out_shape=[f_tile, f_tile, f_tile, bf_tile, bf_tile, bf_tile, bf_tile,
                   jax.ShapeDtypeStruct((3, D, D), F32), jax.ShapeDtypeStruct((8, D), F32)],
        scratch_shapes=[pltpu.VMEM((3, D, D), BF16), pltpu.VMEM((3, D, D), F32), pltpu.SemaphoreType.DMA, pltpu.SemaphoreType.DMA((3 * N_DEV,))],
        compiler_params=_params(1),
    )(x, target, o, hh, proj, proj, proj, proj, proj, proj, proj, proj, w_all, gate, final_g)


_SEGMENTS = ((SEG_Q, 1024), (SEG_K, 256), (SEG_V, 256), (SEG_GA, 1024), (SEG_XR, 1024), (SEG_GR, 1024), (SEG_MA, 1024), (SEG_MR, 1024))


def _seg_specs(tm):
    return [pl.BlockSpec((tm, width), lambda i: (i, 0)) for _, width in _SEGMENTS]


def _bwd_dh(dsegs, w_all, x, dres, mod, norm_g):
    t = x.shape[0]
    tm = min(256, t)
    n_seg = len(_SEGMENTS)
    full = pl.BlockSpec((tm, D), lambda i: (i, 0))

    def body(*refs):
        segs = refs[:n_seg]
        w_hbm, x_ref, dres_ref, mod_ref, ng_ref, gx_ref, vec_ref, w, sems = refs[n_seg:]
        i = pl.program_id(0)

        @pl.when(i == 0)
        def _():
            loads = [
                pltpu.make_async_copy(w_hbm.at[s, pl.ds(0, W_IN_SHARD), :], w.at[pl.ds(W_IN_SHARD * s, W_IN_SHARD), :], sems.at[s])
                for s in range(N_DEV)
            ]
            for cp in loads:
                cp.start()
            for cp in loads:
                cp.wait()
            vec_ref[...] = jnp.zeros_like(vec_ref)

        dh = sum(_dot(seg[...], w[start : start + width, :]) for seg, (start, width) in zip(segs, _SEGMENTS))
        xv = x_ref[...]
        r1 = lax.rsqrt(jnp.mean(xv * xv, axis=-1, keepdims=True) + NORM_EPS)
        xn = xv * r1
        ng = ng_ref[...]
        scale1 = 1.0 + mod_ref[:, D : 2 * D]
        d_shift = jnp.sum(dh, axis=0, keepdims=True)
        d_scale = jnp.sum(dh * (xn * ng), axis=0, keepdims=True)
        g_norm = jnp.sum(dh * xn * scale1, axis=0, keepdims=True)
        dxn = dh * (ng * scale1)
        dx = r1 * (dxn - xn * jnp.mean(dxn * xn, axis=-1, keepdims=True))
        gx_ref[...] = dres_ref[...] + dx
        vec_ref[...] = vec_ref[...] + jnp.concatenate([d_shift, d_scale, g_norm, jnp.zeros((5, D), F32)], axis=0)

    return pl.pallas_call(
        body,
        name="bwd_dh",
        grid=(t // tm,),
        in_specs=_seg_specs(tm) + [pl.BlockSpec(memory_space=pl.ANY), full, full,
                                   pl.BlockSpec((1, 3 * D), lambda i: (0, 0)), pl.BlockSpec((1, D), lambda i: (0, 0))],
        out_specs=[full, pl.BlockSpec((8, D), lambda i: (0, 0))],
        out_shape=[jax.ShapeDtypeStruct((t, D), F32), jax.ShapeDtypeStruct((8, D), F32)],
        scratch_shapes=[pltpu.VMEM((IN_W, D), BF16), pltpu.SemaphoreType.DMA((N_DEV,))],
        compiler_params=_params(1),
    )(*dsegs, w_all, x, dres, mod, norm_g)


def _bwd_gw(dsegs, h):
    t = h.shape[0]
    tm = min(512, t)
    n_tiles = t // tm
    n_seg = len(_SEGMENTS)

    def body(*refs):
        segs = refs[:n_seg]
        h_ref, out_hbm, acc, sem = refs[n_seg:]
        i = pl.program_id(0)

        @pl.when(i == 0)
        def _():
            acc[...] = jnp.zeros_like(acc)

        hv = h_ref[...]
        for seg, (start, width) in zip(segs, _SEGMENTS):
            acc[start : start + width, :] = acc[start : start + width, :] + _dot_tn(seg[...], hv)

        @pl.when(i == n_tiles - 1)
        def _():
            cp = pltpu.make_async_copy(acc, out_hbm, sem)
            cp.start()
            cp.wait()

    return pl.pallas_call(
        body,
        name="bwd_gw",
        grid=(n_tiles,),
        in_specs=_seg_specs(tm) + [pl.BlockSpec((tm, D), lambda i: (i, 0))],
        out_specs=pl.BlockSpec(memory_space=pl.ANY),
        out_shape=jax.ShapeDtypeStruct((IN_W, D), F32),
        scratch_shapes=[pltpu.VMEM((IN_W, D), F32), pltpu.SemaphoreType.DMA],
        compiler_params=_params(1),
    )(*dsegs, h)


def _adamw(w, g, m, v):
    m = ADAM_B1 * m + (1.0 - ADAM_B1) * g
    v = ADAM_B2 * v + (1.0 - ADAM_B2) * (g * g)
    m_hat = m / (1.0 - ADAM_B1**ADAM_STEP)
    v_hat = v / (1.0 - ADAM_B2**ADAM_STEP)
    delta = -ADAM_LR * (m_hat / (jnp.sqrt(v_hat) + ADAM_EPS) + ADAM_WD * w)
    return delta, m, v


def _row_block(rows):
    return 128 if rows % 128 == 0 else rows


def _add_slots(own, l_ref):
    total = own
    for s in range(l_ref.shape[0]):
        total = total + l_ref[s].astype(F32)
    return total


def _sum_slots(name, own, land):
    n_slots, rows, cols = land.shape
    rb = 208 if rows == W_IN_SHARD else _row_block(rows)

    def body(o_ref, l_ref, g_ref):
        g_ref[...] = _add_slots(o_ref[...], l_ref)

    spec = pl.BlockSpec((rb, cols), lambda i: (i, 0))
    return pl.pallas_call(
        body,
        name=name,
        grid=(rows // rb,),
        in_specs=[spec, pl.BlockSpec((n_slots, rb, cols), lambda i: (0, i, 0))],
        out_specs=spec,
        out_shape=jax.ShapeDtypeStruct((rows, cols), F32),
        compiler_params=_params(1),
    )(own, land)


def _adam_update(name, g, w, m, v):
    rows, cols = w.shape
    rb = _row_block(rows)
    spec = pl.BlockSpec((rb, cols), lambda i: (i, 0))

    def body(g_ref, w_ref, m_ref, v_ref, d_ref, m2_ref, v2_ref):
        d_ref[...], m2_ref[...], v2_ref[...] = _adamw(w_ref[...], g_ref[...], m_ref[...], v_ref[...])

    shape = jax.ShapeDtypeStruct((rows, cols), F32)
    return pl.pallas_call(
        body, name=name, grid=(rows // rb,), in_specs=[spec] * 4, out_specs=[spec] * 3, out_shape=[shape] * 3,
        compiler_params=_params(1),
    )(g, w, m, v)


def _shard_update(own_3, land_3, own_g, land_g, weights, m_in, v_in):
    n = len(weights)

    def body(*refs):
        o3_ref, l3_ref, og_ref, lg_ref = refs[:4]
        w_refs, m_refs, v_refs = (refs[4 + n * k : 4 + n * (k + 1)] for k in range(3))
        outs = refs[4 + 3 * n :]
        for k in range(n):
            if k < 3:
                g = o3_ref[k]
                for slot in range(3):
                    g = g + l3_ref[slot, k].astype(F32)
            else:
                g = og_ref[k - 3]
                for slot in range(3):
                    g = g + lg_ref[slot, k - 3].astype(F32)
            d, m2, v2 = _adamw(w_refs[k][0], g, m_refs[k][0], v_refs[k][0])
            for which, val in enumerate((g, d, m2, v2)):
                outs[which * n + k][0] = val

    shapes = [jax.ShapeDtypeStruct(w.shape, F32) for w in weights]
    res = pl.pallas_call(
        body, name="shard_update", out_shape=shapes * 4, compiler_params=pltpu.CompilerParams(vmem_limit_bytes=VMEM_LIMIT)
    )(own_3, land_3, own_g, land_g, *weights, *m_in, *v_in)
    return [res[n * k : n * (k + 1)] for k in range(4)]


VEC_LAYOUT = (("b_ada", 3 * D), ("norm_g", D), ("attn_sinks", 128), ("conv_b", D), ("rg_ba", D), ("rg_bx", D), ("rg_lambda", D),
              ("final_g", D), ("conv_w", CONV_WIDTH * D), ("loss", 128))
VEC_OFFSET = {name: sum(w for _, w in VEC_LAYOUT[:k]) for k, (name, _) in enumerate(VEC_LAYOUT)}
VEC_WIDTH = sum(w for _, w in VEC_LAYOUT)
N_REPLICATED = 8


def _small_update(vec_all, c_all_t, me, rep_w, rep_m, rep_v, ada, conv):
    n_ada = ada[0].shape[1]
    n_conv = conv[0].shape[1]

    def body(*refs):
        me_ref, vec_ref, ct_ref = refs[:3]
        w_refs, m_refs, v_refs = (refs[3 + N_REPLICATED * k : 3 + N_REPLICATED * (k + 1)] for k in range(3))
        pos = 3 + 3 * N_REPLICATED
        ada_refs, conv_refs = refs[pos : pos + 3], refs[pos + 3 : pos + 6]
        outs = refs[pos + 6 :]
        loss_ref, rep_outs = outs[0], outs[1 : 1 + 4 * N_REPLICATED]
        ada_outs, conv_outs = outs[1 + 4 * N_REPLICATED : 5 + 4 * N_REPLICATED], outs[5 + 4 * N_REPLICATED : 9 + 4 * N_REPLICATED]
        conv_buf = outs[9 + 4 * N_REPLICATED]

        total = vec_ref[0:1, :]
        for slot in range(1, N_DEV):
            total = total + vec_ref[slot : slot + 1, :]
        loss_ref[...] = total[:, VEC_OFFSET["loss"] : VEC_OFFSET["loss"] + 1]
        for k, (name, _) in enumerate(VEC_LAYOUT[:N_REPLICATED]):
            width = w_refs[k].shape[1]
            g = total[:, VEC_OFFSET[name] : VEC_OFFSET[name] + width]
            d, m2, v2 = _adamw(w_refs[k][...], g, m_refs[k][...], v_refs[k][...])
            for which, val in enumerate((g, d, m2, v2)):
                rep_outs[which * N_REPLICATED + k][...] = val

        device = me_ref[0]
        for k in range(CONV_WIDTH):
            conv_buf[k : k + 1, :] = total[:, VEC_OFFSET["conv_w"] + D * k : VEC_OFFSET["conv_w"] + D * (k + 1)]
        g_conv = conv_buf[:, pl.ds(pl.multiple_of(device * n_conv, 128), n_conv)]
        for ref, val in zip(conv_outs, (g_conv,) + _adamw(conv_refs[0][...], g_conv, conv_refs[1][...], conv_refs[2][...])):
            ref[...] = val

        dmod = vec_ref[:, pl.ds(pl.multiple_of(device * n_ada, 128), n_ada)]
        g_ada = _dot(ct_ref[...], dmod)
        for ref, val in zip(ada_outs, (g_ada,) + _adamw(ada_refs[0][...], g_ada, ada_refs[1][...], ada_refs[2][...])):
            ref[...] = val

    vmem = pl.BlockSpec(memory_space=pltpu.VMEM)
    rep_shapes = [jax.ShapeDtypeStruct(w.shape, F32) for w in rep_w]
    out_shape = ([jax.ShapeDtypeStruct((1, 1), F32)] + rep_shapes * 4 + [jax.ShapeDtypeStruct(ada[0].shape, F32)] * 4
                 + [jax.ShapeDtypeStruct(conv[0].shape, F32)] * 4)
    n_in = 3 + 3 * N_REPLICATED + 6
    res = pl.pallas_call(
        body,
        name="small_update",
        in_specs=[pl.BlockSpec(memory_space=pltpu.SMEM)] + [vmem] * (n_in - 1),
        out_specs=[vmem] * len(out_shape),
        out_shape=out_shape,
        scratch_shapes=[pltpu.VMEM((CONV_WIDTH, D), F32)],
        compiler_params=pltpu.CompilerParams(vmem_limit_bytes=VMEM_LIMIT),
    )(me, vec_all, c_all_t, *rep_w, *rep_m, *rep_v, *ada, *conv)
    loss = res[0]
    reps = [res[1 + N_REPLICATED * k : 1 + N_REPLICATED * (k + 1)] for k in range(4)]
    return loss, reps, res[1 + 4 * N_REPLICATED : 5 + 4 * N_REPLICATED], res[5 + 4 * N_REPLICATED : 9 + 4 * N_REPLICATED]


def kernel(x, c, positions, w_ada, b_ada, norm_g, w_in, attn_sinks, conv_w, conv_b, rg_wa, rg_ba, rg_wx, rg_bx, rg_lambda, w_attn_proj, w_rnn_proj, w_out, final_g, loss_target, m_w_ada, m_b_ada, m_norm_g, m_w_in, m_attn_sinks, m_conv_w, m_conv_b, m_rg_wa, m_rg_ba, m_rg_wx, m_rg_bx, m_rg_lambda, m_w_attn_proj, m_w_rnn_proj, m_w_out, m_final_g, v_w_ada, v_b_ada, v_norm_g, v_w_in, v_attn_sinks, v_conv_w, v_conv_b, v_rg_wa, v_rg_ba, v_rg_wx, v_rg_bx, v_rg_lambda, v_w_attn_proj, v_w_rnn_proj, v_w_out, v_final_g):
    my_x, my_y, my_c = _position()
    me = _slot((my_x, my_y, my_c))
    x2, target = x[0], loss_target[0]
    pos = positions[0]

    gate_rows = lambda w: w[0].transpose(1, 0, 2).reshape(GATE_SHARD, D)
    w_pack = jnp.concatenate([w_in[0].T, w_attn_proj[0], w_rnn_proj[0], w_out[0], gate_rows(rg_wa), gate_rows(rg_wx)], axis=0).astype(BF16)
    conv_rows = jnp.pad(conv_w[0], ((0, 8 - CONV_WIDTH), (0, 0)))
    c_rows, conv_all, w_all = _all_gather("ag_fwd", [jnp.broadcast_to(c, (8, D)), conv_rows, w_pack])
    c_all = c_rows[:, 0, :]
    conv_full = conv_all[:, :CONV_WIDTH, :].transpose(1, 0, 2).reshape(CONV_WIDTH, D)

    n_ada = w_ada.shape[2]
    b_cols = lax.dynamic_slice(b_ada, (0, me * n_ada), (1, n_ada))
    mod_cols = _ada_mod(c_all, w_ada[0], b_cols)
    (mod_all,) = _all_gather("ag_mod", [mod_cols])
    mod = lax.dynamic_slice(mod_all, (0, me, 0), (N_DEV, 1, n_ada)).reshape(1, 3 * D)
    gate = mod[:, 2 * D :]

    tables = _rope_tables(pos)
    proj, h = _fwd_in(x2, mod, norm_g, w_all, tables[0])
    o = _attn_fwd(proj, attn_sinks)
    keep = (pos != 0).astype(F32)[:, None]
    hh = _rnn_fwd(proj, keep, w_all, rg_ba, rg_bx, rg_lambda, conv_full, conv_b)

    dres, d_o, d_hh, d_ga, d_gr, d_ma, d_mr, gw3, mid_vec = _mid(x2, target, o, hh, proj, w_all, gate, final_g[None, :])
    d_q, d_k, d_v, d_sink = _attn_bwd(proj, o, d_o, tables, attn_sinks)
    d_xr, g_gates, rnn_vec = _rnn_bwd(proj, hh, d_hh, keep, w_all, rg_ba, rg_bx, rg_lambda, conv_full, conv_b)
    dsegs = (d_q, d_k, d_v, d_ga, d_xr, d_gr, d_ma, d_mr)
    g_w_t = _bwd_gw(dsegs, h)
    grad_x, dh_vec = _bwd_dh(dsegs, w_all, x2, dres, mod, norm_g)

    pieces = dict(
        b_ada=jnp.concatenate([dh_vec[0:1], dh_vec[1:2], mid_vec[1:2]], axis=1),
        norm_g=dh_vec[2:3],
        attn_sinks=jnp.pad(d_sink[:, :, 0].reshape(1, 16), ((0, 0), (0, 112))),
        conv_b=rnn_vec[3:4],
        rg_ba=rnn_vec[0:1],
        rg_bx=rnn_vec[1:2],
        rg_lambda=rnn_vec[2:3],
        final_g=mid_vec[0:1],
        conv_w=rnn_vec[4:8].reshape(1, CONV_WIDTH * D),
        loss=mid_vec[2:3, 0:128],
    )
    vec = jnp.concatenate([pieces[name] for name, _ in VEC_LAYOUT], axis=1)
    (vec_all,) = _all_gather("ag_vec", [jnp.broadcast_to(vec, (8, VEC_WIDTH))])
    rep_w = [b_ada, norm_g, attn_sinks, conv_b, rg_ba, rg_bx, rg_lambda, final_g[None, :]]
    rep_m = [m_b_ada, m_norm_g, m_attn_sinks, m_conv_b, m_rg_ba, m_rg_bx, m_rg_lambda, m_final_g[None, :]]
    rep_v = [v_b_ada, v_norm_g, v_attn_sinks, v_conv_b, v_rg_ba, v_rg_bx, v_rg_lambda, v_final_g[None, :]]
    loss, reps, ada_res, conv_res = _small_update(
        vec_all[:, 0, :], c_all.T, me.astype(jnp.int32).reshape(1), rep_w, rep_m, rep_v,
        (w_ada[0], m_w_ada[0], v_w_ada[0]), (conv_w[0], m_conv_w[0], v_conv_w[0]),
    )

    part_w = g_w_t.reshape(4, 2, W_IN_SHARD, D)
    part_3 = gw3.reshape(3, 4, 2, ROW_SHARD, D)
    part_g = g_gates.reshape(2, 4, 4, 2, GATE_SHARD, 256)
    recv_w, recv_3, recv_g = _exchange(
        "rs_pair",
        [part_w, part_3, part_g],
        [lambda ref, peer: ref.at[:, peer[2]], lambda ref, peer: ref.at[:, :, peer[2]], lambda ref, peer: ref.at[:, :, :, peer[2]]],
        [(4, W_IN_SHARD, D), (3, 4, ROW_SHARD, D), (2, 4, 4, GATE_SHARD, 256)],
        flips=(1,),
    )
    index = jnp.stack([my_c, 2 * my_x + my_y]).astype(jnp.int32)
    rb = 208
    sum_w_bf, own_w = _pair_add(
        "pair_add_w", part_w, recv_w[0], index, (W_IN_SHARD // rb, 4), 1,
        pl.BlockSpec((None, None, rb, D), lambda r, q, idx: (q, idx[0], r, 0)),
        pl.BlockSpec((None, rb, D), lambda r, q, idx: (q, r, 0)),
        pl.BlockSpec((rb, D), lambda r, q, idx: (r, 0)), (W_IN_SHARD, D),
    )
    sum_3_bf, own_3 = _pair_add(
        "pair_add_3", part_3, recv_3[0], index, (3, 4), 1,
        pl.BlockSpec((None, None, None, ROW_SHARD, D), lambda a, q, idx: (a, q, idx[0], 0, 0)),
        pl.BlockSpec((None, None, ROW_SHARD, D), lambda a, q, idx: (a, q, 0, 0)),
        pl.BlockSpec((None, ROW_SHARD, D), lambda a, q, idx: (a, 0, 0)), (3, ROW_SHARD, D),
    )
    sum_g_bf, own_g = _pair_add(
        "pair_add_g", part_g, recv_g[0], index, (4,), 0,
        pl.BlockSpec((2, 4, None, None, GATE_SHARD, 256), lambda q, idx: (0, 0, q, idx[0], 0, 0)),
        pl.BlockSpec((2, 4, None, GATE_SHARD, 256), lambda q, idx: (0, 0, q, 0, 0)),
        pl.BlockSpec((2, 4, GATE_SHARD, 256), lambda q, idx: (0, 0, 0, 0)), (2, 4, GATE_SHARD, 256),
    )
    chip = lambda peer: 2 * peer[0] + peer[1]
    land_w, land_3, land_g = _exchange(
        "rs_chips",
        [sum_w_bf, sum_3_bf, sum_g_bf],
        [lambda ref, peer: ref.at[chip(peer)], lambda ref, peer: ref.at[:, chip(peer)], lambda ref, peer: ref.at[:, :, chip(peer)]],
        [(W_IN_SHARD, D), (3, ROW_SHARD, D), (2, 4, GATE_SHARD, 256)],
        flips=(2, 4, 6),
    )
    g_in = _sum_slots("sum_w_in", own_w, land_w).T
    d_in, m_in, v_in = _adam_update("adam_w_in", g_in, w_in[0], m_w_in[0], v_w_in[0])
    shard_res = _shard_update(
        own_3, land_3, own_g, land_g,
        (w_attn_proj, w_rnn_proj, w_out, rg_wa, rg_wx),
        (m_w_attn_proj, m_w_rnn_proj, m_w_out, m_rg_wa, m_rg_wx),
        (v_w_attn_proj, v_w_rnn_proj, v_w_out, v_rg_wa, v_rg_wx),
    )

    def by_weight(k):
        b, n, s, cb, ba, bx, lam, fg = reps[k]
        ap, rp, out, wa, wx = shard_res[k]
        w_in_k = (g_in, d_in, m_in, v_in)[k]
        return [ada_res[k][None], b, n, w_in_k[None], s, conv_res[k][None], cb, wa, ba, wx, bx, lam, ap, rp, out, fg[0]]

    return (loss[0, 0], grad_x[None], *by_weight(0), *by_weight(1), *by_weight(2), *by_weight(3))
```

```python
import functools
import math

import jax
import jax.numpy as jnp
import numpy as np
from jax import lax
from jax.experimental import pallas as pl
from jax.experimental.pallas import tpu as pltpu

F32 = jnp.float32
BF16 = jnp.bfloat16

D = 1024
HEAD_DIM = 64
ROT_DIM = 16
ROPE_THETA = 500000.0
WINDOW = 128
LRU_C = 8.0
CONV_WIDTH = 4
NORM_EPS = 1e-6
IN_W = 6656
SEG_Q, SEG_K, SEG_V, SEG_GA, SEG_XR, SEG_GR, SEG_MA, SEG_MR = 0, 1024, 1280, 1536, 2560, 3584, 4608, 5632
N_DEV = 8
W_IN_SHARD = IN_W // N_DEV
ROW_SHARD = D // N_DEV
GATE_SHARD = 256 // N_DEV

ADAM_LR = 0.001
ADAM_B1 = 0.9
ADAM_B2 = 0.999
ADAM_EPS = 1e-08
ADAM_WD = 0.01
ADAM_STEP = 10

VMEM_LIMIT = 60 * 1024 * 1024
MESH = pl.DeviceIdType.MESH
AXES = ("x", "y", "c")


def _dot(a, b):
    return jnp.dot(a, b, preferred_element_type=F32)


def _dot_nt(a, b):
    return lax.dot_general(a, b, (((1,), (1,)), ((), ())), preferred_element_type=F32)


def _dot_tn(a, b):
    return lax.dot_general(a, b, (((0,), (0,)), ((), ())), preferred_element_type=F32)


def _sigmoid(z):
    return 1.0 / (1.0 + jnp.exp(-z))


def _params(n_grid, vmem=VMEM_LIMIT):
    return pltpu.CompilerParams(dimension_semantics=("arbitrary",) * n_grid, vmem_limit_bytes=vmem)


def _position():
    return lax.axis_index("x"), lax.axis_index("y"), lax.axis_index("c")


def _flip(pos, k):
    x, y, c = pos
    return (1 - x if k & 4 else x, 1 - y if k & 2 else y, 1 - c if k & 1 else c)


def _slot(pos):
    return 4 * pos[0] + 2 * pos[1] + pos[2]


def _all_gather(name, arrs):
    n = len(arrs)

    def body(*refs):
        ins, outs = refs[:n], refs[n : 2 * n]
        send_sems, recv_sems, local_sems = refs[2 * n :]
        me = _position()
        sibling = _flip(me, 1)
        chips = [2, 4, 6]

        def copy(a, k, block, to, src=None):
            return pltpu.make_async_remote_copy(
                src_ref=outs[a].at[_slot(block)] if src is None else src,
                dst_ref=outs[a].at[_slot(block)],
                send_sem=send_sems.at[a * 7 + k],
                recv_sem=recv_sems.at[a * 7 + k],
                device_id=to,
                device_id_type=MESH,
            )

        mine = [pltpu.make_async_copy(ins[a], outs[a].at[_slot(me)], local_sems.at[a]) for a in range(n)]
        for cp in mine:
            cp.start()
        first = []
        for a in range(n):
            first.append(copy(a, 0, me, sibling, src=ins[a]))
            first += [copy(a, 1 + j, me, _flip(me, k), src=ins[a]) for j, k in enumerate(chips)]
        for cp in first:
            cp.start()
        passed = []
        for a in range(n):
            for j, k in enumerate(chips):
                copy(a, 1 + j, _flip(me, k), me).wait_recv()
                fwd = copy(a, 4 + j, _flip(me, k), sibling)
                fwd.start()
                passed.append(fwd)
        for a in range(n):
            copy(a, 0, sibling, me).wait_recv()
            for j, k in enumerate(chips):
                copy(a, 4 + j, _flip(sibling, k), me).wait_recv()
        for cp in first + passed:
            cp.wait_send()
        for cp in mine:
            cp.wait()

    any_spec = pl.BlockSpec(memory_space=pl.ANY)
    return pl.pallas_call(
        body,
        name=name,
        out_shape=[jax.ShapeDtypeStruct((N_DEV,) + a.shape, a.dtype) for a in arrs],
        in_specs=[any_spec] * n,
        out_specs=[any_spec] * n,
        scratch_shapes=[
            pltpu.SemaphoreType.DMA((7 * n,)),
            pltpu.SemaphoreType.DMA((7 * n,)),
            pltpu.SemaphoreType.DMA((n,)),
        ],
    )(*arrs)


def _all_gather_direct(name, arrs):
    n = len(arrs)

    def body(*refs):
        ins, outs = refs[:n], refs[n : 2 * n]
        send_sems, recv_sems, local_sems = refs[2 * n :]
        me = _position()
        mine = [pltpu.make_async_copy(ins[a], outs[a].at[_slot(me)], local_sems.at[a]) for a in range(n)]
        for cp in mine:
            cp.start()

        def copy(a, k, block):
            return pltpu.make_async_remote_copy(
                src_ref=ins[a],
                dst_ref=outs[a].at[_slot(block)],
                send_sem=send_sems.at[a * 7 + k - 1],
                recv_sem=recv_sems.at[a * 7 + k - 1],
                device_id=_flip(me, k),
                device_id_type=MESH,
            )

        sent = [copy(a, k, me) for a in range(n) for k in range(1, N_DEV)]
        for cp in sent:
            cp.start()
        for a in range(n):
            for k in range(1, N_DEV):
                copy(a, k, _flip(me, k)).wait_recv()
        for cp in sent:
            cp.wait_send()
        for cp in mine:
            cp.wait()

    any_spec = pl.BlockSpec(memory_space=pl.ANY)
    return pl.pallas_call(
        body,
        name=name,
        out_shape=[jax.ShapeDtypeStruct((N_DEV,) + a.shape, a.dtype) for a in arrs],
        in_specs=[any_spec] * n,
        out_specs=[any_spec] * n,
        scratch_shapes=[
            pltpu.SemaphoreType.DMA((7 * n,)),
            pltpu.SemaphoreType.DMA((7 * n,)),
            pltpu.SemaphoreType.DMA((n,)),
        ],
    )(*arrs)


CHIP_FLIPS = (2, 4, 6)


def _chip_of(pos):
    return 2 * pos[0] + pos[1]


def _chip_copies(srcs, lands, views, send_sems, recv_sems):
    me = _position()
    copies = []
    for a, (src, land) in enumerate(zip(srcs, lands)):
        for j, k in enumerate(CHIP_FLIPS):
            peer = _flip(me, k)
            copies.append(
                pltpu.make_async_remote_copy(
                    src_ref=views[a](src, peer),
                    dst_ref=land.at[j],
                    send_sem=send_sems.at[len(CHIP_FLIPS) * a + j],
                    recv_sem=recv_sems.at[len(CHIP_FLIPS) * a + j],
                    device_id=peer,
                    device_id_type=MESH,
                )
            )
    return copies


def _exchange(name, arrs, views, out_shapes, flips):
    n = len(arrs)
    nf = len(flips)

    def body(*refs):
        ins, outs = refs[:n], refs[n : 2 * n]
        send_sems, recv_sems = refs[2 * n :]
        me = _position()

        def copy(a, j):
            peer = _flip(me, flips[j])
            return pltpu.make_async_remote_copy(
                src_ref=views[a](ins[a], peer),
                dst_ref=outs[a].at[j],
                send_sem=send_sems.at[a * nf + j],
                recv_sem=recv_sems.at[a * nf + j],
                device_id=peer,
                device_id_type=MESH,
            )

        sent = [copy(a, j) for a in range(n) for j in range(nf)]
        for cp in sent:
            cp.start()
        for cp in sent:
            cp.wait_recv()
        for cp in sent:
            cp.wait_send()

    any_spec = pl.BlockSpec(memory_space=pl.ANY)
    return pl.pallas_call(
        body,
        name=name,
        out_shape=[jax.ShapeDtypeStruct((nf,) + tuple(sh), a.dtype) for sh, a in zip(out_shapes, arrs)],
        in_specs=[any_spec] * n,
        out_specs=[any_spec] * n,
        scratch_shapes=[pltpu.SemaphoreType.DMA((nf * n,)), pltpu.SemaphoreType.DMA((nf * n,))],
    )(*arrs)


def _pair_add(name, own, recv, index, grid, chip_axis, own_spec, recv_spec, mine_spec, mine_shape):
    def body(idx_ref, own_ref, recv_ref, sb_ref, mine_ref):
        total = own_ref[...] + recv_ref[...]
        sb_ref[...] = total.astype(BF16)

        @pl.when(pl.program_id(chip_axis) == idx_ref[1])
        def _():
            mine_ref[...] = total

    return pl.pallas_call(
        body,
        name=name,
        grid_spec=pltpu.PrefetchScalarGridSpec(
            num_scalar_prefetch=1, grid=grid, in_specs=[own_spec, recv_spec], out_specs=[recv_spec, mine_spec]
        ),
        out_shape=[jax.ShapeDtypeStruct(recv.shape, BF16), jax.ShapeDtypeStruct(mine_shape, F32)],
        compiler_params=_params(len(grid)),
    )(index, own, recv)


def _ada_mod(c_all, w_ada, b_cols):
    def body(c_ref, w_ref, b_ref, o_ref):
        o_ref[...] = _dot(c_ref[...].astype(BF16), w_ref[...].astype(BF16)) + b_ref[...]

    return pl.pallas_call(
        body,
        name="ada_mod",
        out_shape=jax.ShapeDtypeStruct((N_DEV, w_ada.shape[1]), F32),
        compiler_params=pltpu.CompilerParams(vmem_limit_bytes=VMEM_LIMIT),
    )(c_all, w_ada, b_cols)


PACK_PROJ = W_IN_SHARD
PACK_GATE = PACK_PROJ + 3 * ROW_SHARD
PACK_ROWS = PACK_GATE + 2 * GATE_SHARD


def _fwd_in(x, mod, norm_g, w_all, tables):
    t = x.shape[0]
    tm = min(512, t)
    tn = IN_W // 4
    n_rope = (SEG_V - SEG_Q) // 128

    def body(x_ref, mod_ref, ng_ref, wa_ref, wb_ref, cos_ref, s1_ref, s2_ref, proj_ref, h_ref):
        j = pl.program_id(1)

        @pl.when(j == 0)
        def _():
            xv = x_ref[...]
            r = lax.rsqrt(jnp.mean(xv * xv, axis=-1, keepdims=True) + NORM_EPS)
            shift = mod_ref[:, 0:D]
            scale = mod_ref[:, D : 2 * D]
            h_ref[...] = (((xv * r) * ng_ref[...]) * (1.0 + scale) + shift).astype(BF16)

        acc = _dot_nt(h_ref[...], jnp.concatenate([wa_ref[...], wb_ref[...]], axis=0))

        @pl.when(j == 0)
        def _():
            cos, s1, s2 = cos_ref[...], s1_ref[...], s2_ref[...]
            for g in range(n_rope):
                cols = slice(128 * g, 128 * (g + 1))
                proj_ref[:, cols] = _rope(acc[:, cols], cos, s1, s2)
            proj_ref[:, 128 * n_rope :] = acc[:, 128 * n_rope :]

        @pl.when(j != 0)
        def _():
            proj_ref[...] = acc

    tab = pl.BlockSpec((tm, 128), lambda i, j: (i, 0))
    return pl.pallas_call(
        body,
        name="fwd_in",
        grid=(t // tm, IN_W // tn),
        in_specs=[
            pl.BlockSpec((tm, D), lambda i, j: (i, 0)),
            pl.BlockSpec((1, 3 * D), lambda i, j: (0, 0)),
            pl.BlockSpec((1, D), lambda i, j: (0, 0)),
            pl.BlockSpec((None, W_IN_SHARD, D), lambda i, j: (2 * j, 0, 0)),
            pl.BlockSpec((None, W_IN_SHARD, D), lambda i, j: (2 * j + 1, 0, 0)),
            tab, tab, tab,
        ],
        out_specs=[pl.BlockSpec((tm, tn), lambda i, j: (i, j)), pl.BlockSpec((tm, D), lambda i, j: (i, 0))],
        out_shape=[jax.ShapeDtypeStruct((t, IN_W), F32), jax.ShapeDtypeStruct((t, D), BF16)],
        compiler_params=_params(2),
    )(x, mod, norm_g, w_all, w_all, *tables)


def _rope_tables(positions):
    half = ROT_DIM // 2
    inv_freq = ROPE_THETA ** (-jnp.arange(0, ROT_DIM, 2, dtype=F32) / ROT_DIM)
    pos = positions.astype(F32)
    lane = np.arange(128) % HEAD_DIM
    freq_lane = jnp.where(lane < ROT_DIM, jnp.tile(inv_freq, 128 // half), 0.0)
    ang = pos[:, None] * freq_lane[None, :]
    sin = jnp.sin(ang)
    lane_tables = (jnp.cos(ang), jnp.where(lane < half, -sin, 0.0), jnp.where((lane >= half) & (lane < ROT_DIM), sin, 0.0))
    ang_t = inv_freq[:, None] * pos[None, :]
    return lane_tables, (jnp.cos(ang_t), jnp.sin(ang_t))


def _rope(tv, cos, s1, s2):
    cols = []
    for g in range(tv.shape[1] // 128):
        blk = tv[:, 128 * g : 128 * (g + 1)]
        cols.append(blk * cos + pltpu.roll(blk, 128 - 8, 1) * s1 + pltpu.roll(blk, 8, 1) * s2)
    return cols[0] if len(cols) == 1 else jnp.concatenate(cols, axis=1)


def _rope_transposed(gv, cos, s1, s2):
    cols = []
    for g in range(gv.shape[1] // 128):
        blk = gv[:, 128 * g : 128 * (g + 1)]
        cols.append(blk * cos + pltpu.roll(blk * s1, 8, 1) + pltpu.roll(blk * s2, 128 - 8, 1))
    return cols[0] if len(cols) == 1 else jnp.concatenate(cols, axis=1)


def _lane_lo():
    return lax.broadcasted_iota(jnp.int32, (1, 128), 1) < HEAD_DIM


def _row_lo():
    return lax.broadcasted_iota(jnp.int32, (128, 1), 0) < HEAD_DIM


def _split_heads(tile, j):
    lo = _lane_lo()
    parts = []
    for g in (2 * j, 2 * j + 1):
        blk = tile[:, 128 * g : 128 * (g + 1)]
        parts += [jnp.where(lo, blk, 0.0), jnp.where(lo, 0.0, blk)]
    return jnp.concatenate(parts, axis=0)


def _dup_lanes(kv, j):
    rolled = pltpu.roll(kv, HEAD_DIM, 1)
    lo = _lane_lo()
    return jnp.where(lo, kv, rolled) if j == 0 else jnp.where(lo, rolled, kv)


def _fold_lanes(acc):
    return acc + pltpu.roll(acc, HEAD_DIM, 1)


def _fill_bias_t(bias_ref):
    kj = lax.broadcasted_iota(jnp.int32, (256, 128), 0)
    qi = lax.broadcasted_iota(jnp.int32, (256, 128), 1)
    diff = qi + WINDOW - kj
    band = (diff >= 0) & (diff < WINDOW)
    bias_ref[0] = jnp.where(band, 0.0, -1e30)
    bias_ref[1] = jnp.where(band & (kj >= 128), 0.0, -1e30)


def _sink_row(sink_ref, pair, j):
    return jnp.concatenate([jnp.full((1, 128), sink_ref[0, 8 * pair + 4 * j + h], F32) for h in range(4)], axis=1)


def _softmax_t(q4, kw, bias4, sink):
    s = _dot_nt(kw, q4) + bias4
    m = jnp.maximum(jnp.max(s, axis=0, keepdims=True), sink)
    p = jnp.exp(s - m)
    e_sink = jnp.exp(sink - m)
    denom = jnp.sum(p, axis=0, keepdims=True) + e_sink
    return p, e_sink, denom


def _attn_specs(t, tq):
    return dict(
        q=pl.BlockSpec((tq, 512), lambda p, i: (i, p)),
        k=pl.BlockSpec((t, 128), lambda p, i: (0, SEG_K // 128 + p)),
        v=pl.BlockSpec((t, 128), lambda p, i: (0, SEG_V // 128 + p)),
        sink=pl.BlockSpec(memory_space=pltpu.SMEM),
        act=pl.BlockSpec((tq, 512), lambda p, i: (i, p)),
    )


def _attn_fwd(proj, sinks):
    t = proj.shape[0]
    tq = min(512, t)
    nb = tq // 128
    sp = _attn_specs(t, tq)

    def body(q_ref, k_ref, v_ref, sink_ref, o_ref, kd, vdt, bias_ref):
        pair = pl.program_id(0)
        i = pl.program_id(1)

        @pl.when(i == 0)
        def _():
            kv, vv = k_ref[...], v_ref[...]
            for j in range(2):
                kd[j, 0:128, :] = jnp.zeros((128, 128), BF16)
                kd[j, 128:, :] = _dup_lanes(kv, j).astype(BF16)
                vdt[j, :, 0:128] = jnp.zeros((128, 128), BF16)
                vdt[j, :, 128:] = _dup_lanes(vv, j).T.astype(BF16)
            _fill_bias_t(bias_ref)

        row_lo = _row_lo()

        def block(b, carry):
            r0 = pl.multiple_of(b * 128, 128)
            gb = i * nb + b
            keys = pl.ds(pl.multiple_of(gb * 128, 128), 256)
            rows = pl.ds(r0, 128)
            qt = q_ref[rows, :] * (1.0 / math.sqrt(HEAD_DIM))
            bias = bias_ref[jnp.where(gb == 0, 1, 0)]
            bias4 = jnp.concatenate([bias] * 4, axis=1)
            cols = []
            for j in range(2):
                q4 = _split_heads(qt, j).astype(BF16)
                p, _, denom = _softmax_t(q4, kd[j, keys, :], bias4, _sink_row(sink_ref, pair, j))
                ot = _dot(vdt[j, :, keys], p.astype(BF16)) * (1.0 / denom)
                for gi in range(2):
                    z = jnp.where(row_lo, ot[:, 256 * gi : 256 * gi + 128], ot[:, 256 * gi + 128 : 256 * gi + 256])
                    cols.append(z.T)
            o_ref[rows, :] = jnp.concatenate(cols, axis=1)
            return carry

        lax.fori_loop(0, nb, block, 0)

    return pl.pallas_call(
        body,
        name="attn_fwd",
        grid=(2, t // tq),
        in_specs=[sp["q"], sp["k"], sp["v"], sp["sink"]],
        out_specs=sp["act"],
        out_shape=jax.ShapeDtypeStruct((t, D), F32),
        scratch_shapes=[pltpu.VMEM((2, t + 128, 128), BF16), pltpu.VMEM((2, 128, t + 128), BF16), pltpu.VMEM((2, 256, 128), F32)],
        compiler_params=_params(2),
    )(proj, proj, proj, sinks)


def _attn_bwd(proj, o, d_o, tables, sinks):
    t = proj.shape[0]
    tq = min(512, t)
    nb = tq // 128
    n_tiles = t // tq
    (cos, s1, s2), (cos_t, sin_t) = tables
    sp = _attn_specs(t, tq)
    scale = 1.0 / math.sqrt(HEAD_DIM)
    half = ROT_DIM // 2

    def unrope_rows(z, c, s):
        out = []
        for base in (0, HEAD_DIM):
            g0, g1 = z[base : base + half], z[base + half : base + ROT_DIM]
            out += [g0 * c + g1 * s, g1 * c - g0 * s, z[base + ROT_DIM : base + HEAD_DIM]]
        return jnp.concatenate(out, axis=0)

    def body(q_ref, k_ref, v_ref, o_ref, do_ref, ct_ref, st_ref, ck, s1k, s2k, sink_ref, dq_ref, dk_ref, dv_ref, dsink_ref,
             kd, kdt, vd, bias_ref, dkacc, dvacc):
        pair = pl.program_id(0)
        i = pl.program_id(1)

        @pl.when(i == 0)
        def _():
            kv, vv = k_ref[...], v_ref[...]
            for j in range(2):
                kdup = _dup_lanes(kv, j)
                kd[j, 0:128, :] = jnp.zeros((128, 128), BF16)
                kd[j, 128:, :] = kdup.astype(BF16)
                kdt[j, :, 0:128] = jnp.zeros((128, 128), BF16)
                kdt[j, :, 128:] = kdup.T.astype(BF16)
                vd[j, 0:128, :] = jnp.zeros((128, 128), BF16)
                vd[j, 128:, :] = _dup_lanes(vv, j).astype(BF16)
            _fill_bias_t(bias_ref)
            dkacc[...] = jnp.zeros_like(dkacc)
            dvacc[...] = jnp.zeros_like(dvacc)
            dsink_ref[...] = jnp.zeros_like(dsink_ref)

        row_lo = _row_lo()
        ones = jnp.ones((8, 128), BF16)

        def block(b, carry):
            r0 = pl.multiple_of(b * 128, 128)
            gb = i * nb + b
            keys = pl.ds(pl.multiple_of(gb * 128, 128), 256)
            rows = pl.ds(r0, 128)
            qt = q_ref[rows, :] * scale
            dob = do_ref[rows, :]
            prod = dob * o_ref[rows, :]
            ct, st = ct_ref[:, rows], st_ref[:, rows]
            bias = bias_ref[jnp.where(gb == 0, 1, 0)]
            bias4 = jnp.concatenate([bias] * 4, axis=1)
            cols = []
            for j in range(2):
                q4 = _split_heads(qt, j).astype(BF16)
                kw = kd[j, keys, :]
                p, e_sink, denom = _softmax_t(q4, kw, bias4, _sink_row(sink_ref, pair, j))
                inv = 1.0 / denom
                prob = p * inv
                do4 = _split_heads(dob, j).astype(BF16)
                delta = _dot_nt(ones, _split_heads(prod, j).astype(BF16))[0:1, :]
                dp = _dot_nt(vd[j, keys, :], do4)
                ds = (prob * (dp - delta)).astype(BF16)
                dkacc[j, keys, :] = dkacc[j, keys, :] + _dot(ds, q4)
                dvacc[j, keys, :] = dvacc[j, keys, :] + _dot(prob.astype(BF16), do4)
                dqt = _dot(kdt[j, :, keys], ds) * scale
                dsink = -(e_sink * inv) * delta
                for h in range(4):
                    row = 4 * j + h
                    total = jnp.sum(dsink[:, 128 * h : 128 * (h + 1)], axis=1, keepdims=True)
                    dsink_ref[0, row : row + 1, :] = dsink_ref[0, row : row + 1, :] + jnp.broadcast_to(total, (1, 128))
                for gi in range(2):
                    z = jnp.where(row_lo, dqt[:, 256 * gi : 256 * gi + 128], dqt[:, 256 * gi + 128 : 256 * gi + 256])
                    cols.append(unrope_rows(z, ct, st).T)
            dq_ref[rows, :] = jnp.concatenate(cols, axis=1).astype(BF16)
            return carry

        lax.fori_loop(0, nb, block, 0)

        @pl.when(i == n_tiles - 1)
        def _():
            lo = _lane_lo()
            dk = jnp.where(lo, _fold_lanes(dkacc[0, 128:, :]), _fold_lanes(dkacc[1, 128:, :]))
            dk_ref[...] = _rope_transposed(dk, ck[...], s1k[...], s2k[...]).astype(BF16)
            dv_ref[...] = jnp.where(lo, _fold_lanes(dvacc[0, 128:, :]), _fold_lanes(dvacc[1, 128:, :])).astype(BF16)

    kv_out = pl.BlockSpec((t, 128), lambda p, i: (0, p))
    tab_t = pl.BlockSpec((half, tq), lambda p, i: (0, i))
    tab_k = pl.BlockSpec((t, 128), lambda p, i: (0, 0))
    return pl.pallas_call(
        body,
        name="attn_bwd",
        grid=(2, n_tiles),
        in_specs=[sp["q"], sp["k"], sp["v"], sp["act"], sp["act"], tab_t, tab_t, tab_k, tab_k, tab_k, sp["sink"]],
        out_specs=[sp["act"], kv_out, kv_out, pl.BlockSpec((1, 8, 128), lambda p, i: (p, 0, 0))],
        out_shape=[
            jax.ShapeDtypeStruct((t, D), BF16),
            jax.ShapeDtypeStruct((t, 256), BF16),
            jax.ShapeDtypeStruct((t, 256), BF16),
            jax.ShapeDtypeStruct((2, 8, 128), F32),
        ],
        scratch_shapes=[pltpu.VMEM((2, t + 128, 128), BF16), pltpu.VMEM((2, 128, t + 128), BF16), pltpu.VMEM((2, t + 128, 128), BF16),
                        pltpu.VMEM((2, 256, 128), F32), pltpu.VMEM((2, t + 128, 128), F32), pltpu.VMEM((2, t + 128, 128), F32)],
        compiler_params=_params(2),
    )(proj, proj, proj, o, d_o, cos_t, sin_t, cos, s1, s2, sinks)


HALF = 512


def _softplus(y):
    u = jnp.exp(-jnp.abs(y))
    w = 1.0 + u
    log1p = jnp.where(w == 1.0, u, jnp.log(w) * (u / jnp.where(w == 1.0, 1.0, w - 1.0)))
    return jnp.maximum(y, 0.0) + log1p


def _gate_block(w_ref, blk):
    return w_ref[:, :, 256 * blk : 256 * (blk + 1)].reshape(256, 256)


def _gate_matmul(act_bf, w_ref):
    return jnp.concatenate([_dot(act_bf[:, 256 * blk : 256 * (blk + 1)], _gate_block(w_ref, blk)) for blk in range(2)], axis=1)


def _gate_matmul_t(grad_bf, w_ref):
    return jnp.concatenate([_dot_nt(grad_bf[:, 256 * blk : 256 * (blk + 1)], _gate_block(w_ref, blk)) for blk in range(2)], axis=1)


def _conv_and_gates(xe_ref, tm, cw_ref, cb_ref, wa_ref, wx_ref, ba_ref, bx_ref, lam_ref, keep):
    xc = cb_ref[...] + sum(cw_ref[k : k + 1, :] * xe_ref[pl.ds(5 + k, tm), :] for k in range(CONV_WIDTH))
    xc_bf = xc.astype(BF16)
    r = _sigmoid(_gate_matmul(xc_bf, wa_ref) + ba_ref[...])
    ig = _sigmoid(_gate_matmul(xc_bf, wx_ref) + bx_ref[...])
    sp = _softplus(-lam_ref[...])
    log_a = -LRU_C * r * sp
    tanh = jnp.tanh(-log_a)
    mult = jnp.sqrt(2.0 * tanh / (1.0 + tanh))
    a = jnp.where(keep, jnp.exp(log_a), 0.0)
    mult = jnp.where(keep, mult, 1.0)
    return xc, xc_bf, r, ig, sp, a, mult


def _scan_rows(a_ref, b_ref, h_ref, carry_ref, tm, reverse):
    width = a_ref.shape[1]
    row = lax.broadcasted_iota(jnp.int32, (8, width), 0)

    def group(g, carry):
        gg = tm // 8 - 1 - g if reverse else g
        r0 = pl.multiple_of(gg * 8, 8)
        a = a_ref[pl.ds(r0, 8), :]
        b = b_ref[pl.ds(r0, 8), :]
        for s in (1, 2, 4):
            shift = 8 - s if reverse else s
            a_sh = pltpu.roll(a, shift, 0)
            b_sh = pltpu.roll(b, shift, 0)
            ok = (row < 8 - s) if reverse else (row >= s)
            b = jnp.where(ok, a * b_sh + b, b)
            a = jnp.where(ok, a * a_sh, a)
        h = a * carry + b
        h_ref[pl.ds(r0, 8), :] = h
        last = h[0:1, :] if reverse else h[7:8, :]
        return jnp.broadcast_to(last, (8, width))

    carry_ref[...] = lax.fori_loop(0, tm // 8, group, carry_ref[...])


def _rnn_specs(t, tm, order):
    n = t // tm
    return dict(
        tile=pl.BlockSpec((tm, HALF), lambda ch, i: (order(i, n), ch)),
        before=lambda col0: pl.BlockSpec((8, HALF), lambda ch, i: (jnp.maximum(order(i, n) * (tm // 8) - 1, 0), col0 + ch)),
        proj_tile=lambda col0: pl.BlockSpec((tm, HALF), lambda ch, i: (order(i, n), col0 + ch)),
        keep=pl.BlockSpec((tm, 1), lambda ch, i: (order(i, n), 0)),
        gates=lambda which: pl.BlockSpec((N_DEV, GATE_SHARD, HALF), lambda ch, i: (0, PACK_GATE // GATE_SHARD + which, ch)),
        gate_grads=pl.BlockSpec((2, 2, 256, 256), lambda ch, i: (0, ch, 0, 0)),
        vec=pl.BlockSpec((1, HALF), lambda ch, i: (0, ch)),
        conv=pl.BlockSpec((CONV_WIDTH, HALF), lambda ch, i: (0, ch)),
    )


def _rnn_fwd(proj, keep, w_all, rg_ba, rg_bx, rg_lambda, conv_w, conv_b):
    t = proj.shape[0]
    tm = min(512, t)
    sp = _rnn_specs(t, tm, lambda i, n: i)

    def body(xr_ref, xb_ref, keep_ref, wa_ref, wx_ref, ba_ref, bx_ref, lam_ref, cw_ref, cb_ref, hh_ref, xe, a_buf, b_buf, carry):
        i = pl.program_id(1)

        @pl.when(i == 0)
        def _():
            carry[...] = jnp.zeros_like(carry)

        xe[0:8, :] = jnp.where(i > 0, xb_ref[...], 0.0)
        xe[8:, :] = xr_ref[...]
        keep_rows = keep_ref[...] != 0.0
        xc, _, _, ig, _, a, mult = _conv_and_gates(xe, tm, cw_ref, cb_ref, wa_ref, wx_ref, ba_ref, bx_ref, lam_ref, keep_rows)
        a_buf[...] = a
        b_buf[...] = mult * (ig * xc)
        _scan_rows(a_buf, b_buf, hh_ref, carry, tm, reverse=False)

    return pl.pallas_call(
        body,
        name="rnn_fwd",
        grid=(2, t // tm),
        in_specs=[sp["proj_tile"](SEG_XR // HALF), sp["before"](SEG_XR // HALF), sp["keep"], sp["gates"](0), sp["gates"](1),
                  sp["vec"], sp["vec"], sp["vec"], sp["conv"], sp["vec"]],
        out_specs=sp["tile"],
        out_shape=jax.ShapeDtypeStruct((t, D), F32),
        scratch_shapes=[pltpu.VMEM((tm + 8, HALF), F32), pltpu.VMEM((tm, HALF), F32), pltpu.VMEM((tm, HALF), F32), pltpu.VMEM((8, HALF), F32)],
        compiler_params=_params(2),
    )(proj, proj, keep, w_all, w_all, rg_ba, rg_bx, rg_lambda, conv_w, conv_b)


def _rnn_bwd(proj, hh, d_hh, keep, w_all, rg_ba, rg_bx, rg_lambda, conv_w, conv_b):
    t = proj.shape[0]
    tm = min(512, t)
    n_tiles = t // tm
    sp = _rnn_specs(t, tm, lambda i, n: n - 1 - i)

    def body(xr_ref, xb_ref, hh_ref, hb_ref, dhh_ref, keep_ref, wa_ref, wx_ref, ba_ref, bx_ref, lam_ref, cw_ref, cb_ref,
             dxr_ref, gw_ref, gvec_ref, xe, he, ae, c_buf, g_buf, dxe, carry, a_next):
        i = pl.program_id(1)
        first_tile = i == n_tiles - 1

        @pl.when(i == 0)
        def _():
            carry[...] = jnp.zeros_like(carry)
            a_next[...] = jnp.zeros_like(a_next)
            dxe[tm:, :] = jnp.zeros((8, HALF), F32)
            gw_ref[...] = jnp.zeros_like(gw_ref)
            gvec_ref[...] = jnp.zeros_like(gvec_ref)

        xe[0:8, :] = jnp.where(first_tile, 0.0, xb_ref[...])
        xe[8:, :] = xr_ref[...]
        he[0:8, :] = jnp.where(first_tile, 0.0, hb_ref[...])
        he[8:, :] = hh_ref[...]
        keep_rows = keep_ref[...] != 0.0
        xc, xc_bf, r, ig, sp_, a, mult = _conv_and_gates(xe, tm, cw_ref, cb_ref, wa_ref, wx_ref, ba_ref, bx_ref, lam_ref, keep_rows)

        ae[0:tm, :] = a
        ae[tm:, :] = a_next[...]
        a_next[...] = jnp.broadcast_to(a[0:1, :], (8, HALF))
        c_buf[...] = ae[pl.ds(1, tm), :]
        g_buf[...] = dhh_ref[...]
        _scan_rows(c_buf, g_buf, g_buf, carry, tm, reverse=True)
        g = g_buf[...]

        h_prev = he[pl.ds(7, tm), :]
        bb = ig * xc
        d_ig = g * mult * xc
        d_xc = g * mult * ig
        d_log_a = jnp.where(keep_rows, g * h_prev * a - (g * bb) * (a * a) / mult, 0.0)
        d_r = d_log_a * (-LRU_C * sp_)
        d_sp = jnp.sum(d_log_a * (-LRU_C * r), axis=0, keepdims=True)
        d_zr = d_r * r * (1.0 - r)
        d_zi = d_ig * ig * (1.0 - ig)
        d_zr_bf = d_zr.astype(BF16)
        d_zi_bf = d_zi.astype(BF16)
        d_xc = d_xc + _gate_matmul_t(d_zr_bf, wa_ref) + _gate_matmul_t(d_zi_bf, wx_ref)
        for blk in range(2):
            cols = slice(256 * blk, 256 * (blk + 1))
            gw_ref[0, blk] = gw_ref[0, blk] + _dot_tn(xc_bf[:, cols], d_zr_bf[:, cols])
            gw_ref[1, blk] = gw_ref[1, blk] + _dot_tn(xc_bf[:, cols], d_zi_bf[:, cols])

        dxe[0:tm, :] = d_xc
        dxr = sum(cw_ref[k : k + 1, :] * dxe[pl.ds(3 - k, tm), :] for k in range(CONV_WIDTH))
        dxr_ref[...] = dxr.astype(BF16)
        dxe[tm:, :] = d_xc[0:8, :]

        lam = lam_ref[...]
        sums = [
            jnp.sum(d_zr, axis=0, keepdims=True),
            jnp.sum(d_zi, axis=0, keepdims=True),
            d_sp * (-_sigmoid(-lam)),
            jnp.sum(d_xc, axis=0, keepdims=True),
        ] + [jnp.sum(d_xc * xe[pl.ds(5 + k, tm), :], axis=0, keepdims=True) for k in range(CONV_WIDTH)]
        gvec_ref[...] = gvec_ref[...] + jnp.concatenate(sums, axis=0)

    return pl.pallas_call(
        body,
        name="rnn_bwd",
        grid=(2, n_tiles),
        in_specs=[
            sp["proj_tile"](SEG_XR // HALF), sp["before"](SEG_XR // HALF), sp["tile"], sp["before"](0), sp["tile"], sp["keep"],
            sp["gates"](0), sp["gates"](1), sp["vec"], sp["vec"], sp["vec"], sp["conv"], sp["vec"],
        ],
        out_specs=[sp["tile"], sp["gate_grads"], pl.BlockSpec((8, HALF), lambda ch, i: (0, ch))],
        out_shape=[
            jax.ShapeDtypeStruct((t, D), BF16),
            jax.ShapeDtypeStruct((2, 4, 256, 256), F32),
            jax.ShapeDtypeStruct((8, D), F32),
        ],
        scratch_shapes=[
            pltpu.VMEM((tm + 8, HALF), F32),
            pltpu.VMEM((tm + 8, HALF), F32),
            pltpu.VMEM((tm + 8, HALF), F32),
            pltpu.VMEM((tm, HALF), F32),
            pltpu.VMEM((tm, HALF), F32),
            pltpu.VMEM((tm + 8, HALF), F32),
            pltpu.VMEM((8, HALF), F32),
            pltpu.VMEM((8, HALF), F32),
        ],
        compiler_params=_params(2),
    )(proj, proj, hh, hh, d_hh, keep, w_all, w_all, rg_ba, rg_bx, rg_lambda, conv_w, conv_b)


def _mid(x, target, o, hh, proj, w_all, gate, final_g):
    t = x.shape[0]
    tm = min(256, t)
    n_tiles = t // tm
    full = pl.BlockSpec((tm, D), lambda i: (i, 0))
    half = lambda col: pl.BlockSpec((tm, 512), lambda i: (i, col))
    vec = pl.BlockSpec((1, D), lambda i: (0, 0))
    any_spec = pl.BlockSpec(memory_space=pl.ANY)

    def body(x_ref, tg_ref, o_ref, hh_ref, ga0, ga1, gr0, gr1, ma0, ma1, mr0, mr1, w_hbm, gate_ref, fg_ref,
             dres_ref, do_ref, dhh_ref, dga_ref, dgr_ref, dma_ref, dmr_ref, gw_hbm, vec_ref, w, gw, sem, w_sems):
        i = pl.program_id(0)

        @pl.when(i == 0)
        def _():
            loads = [
                pltpu.make_async_copy(
                    w_hbm.at[s, pl.ds(PACK_PROJ + ROW_SHARD * k, ROW_SHARD), :], w.at[k, pl.ds(ROW_SHARD * s, ROW_SHARD), :], w_sems.at[k * N_DEV + s]
                )
                for k in range(3)
                for s in range(N_DEV)
            ]
            for cp in loads:
                cp.start()
            for cp in loads:
                cp.wait()
            gw[...] = jnp.zeros_like(gw)
            vec_ref[...] = jnp.zeros_like(vec_ref)

        cat = lambda a, b: jnp.concatenate([a[...], b[...]], axis=1)
        ga, gr, ma, mr = cat(ga0, ga1), cat(gr0, gr1), cat(ma0, ma1), cat(mr0, mr1)
        ov, hv, gate_v, fg = o_ref[...], hh_ref[...], gate_ref[...], fg_ref[...]
        sg_a = _sigmoid(ga)
        sg_r = _sigmoid(gr)
        silu_a = ga * sg_a
        silu_r = gr * sg_r
        ya = (ov * silu_a).astype(BF16)
        yr = (hv * silu_r).astype(BF16)
        pa = _dot(ya, w[0])
        pr = _dot(yr, w[1])
        sa = _sigmoid(ma)
        sr = _sigmoid(mr)
        merged = (sa * pa + sr * pr).astype(BF16)
        mo = _dot(merged, w[2])
        out = x_ref[...] + gate_v * mo
        r2 = lax.rsqrt(jnp.mean(out * out, axis=-1, keepdims=True) + NORM_EPS)
        outn = out * r2
        err = outn * fg - tg_ref[...]
        loss = 0.5 * jnp.sum(jnp.mean(err * err, axis=-1, keepdims=True), axis=0, keepdims=True)
        dy = err * (1.0 / D)
        g_final = jnp.sum(dy * outn, axis=0, keepdims=True)
        doutn = dy * fg
        dout = r2 * (doutn - outn * jnp.mean(doutn * outn, axis=-1, keepdims=True))
        dres_ref[...] = dout
        d_gate = jnp.sum(dout * mo, axis=0, keepdims=True)
        dmo = (dout * gate_v).astype(BF16)
        dmerged = _dot_nt(dmo, w[2])
        gw[2] = gw[2] + _dot_tn(merged, dmo)
        dpa = (dmerged * sa).astype(BF16)
        dpr = (dmerged * sr).astype(BF16)
        dma_ref[...] = (dmerged * pa * (sa * (1.0 - sa))).astype(BF16)
        dmr_ref[...] = (dmerged * pr * (sr * (1.0 - sr))).astype(BF16)
        dya = _dot_nt(dpa, w[0])
        dyr = _dot_nt(dpr, w[1])
        gw[0] = gw[0] + _dot_tn(ya, dpa)
        gw[1] = gw[1] + _dot_tn(yr, dpr)
        do_ref[...] = dya * silu_a
        dhh_ref[...] = dyr * silu_r
        dga_ref[...] = (dya * ov * (sg_a * (1.0 + ga * (1.0 - sg_a)))).astype(BF16)
        dgr_ref[...] = (dyr * hv * (sg_r * (1.0 + gr * (1.0 - sg_r)))).astype(BF16)
        zeros = jnp.zeros((5, D), F32)
        vec_ref[...] = vec_ref[...] + jnp.concatenate([g_final, d_gate, jnp.broadcast_to(loss, (1, D)), zeros], axis=0)

        @pl.when(i == n_tiles - 1)
        def _():
            cp = pltpu.make_async_copy(gw, gw_hbm, sem)
            cp.start()
            cp.wait()

    bf_tile = jax.ShapeDtypeStruct((t, D), BF16)
    f_tile = jax.ShapeDtypeStruct((t, D), F32)
    return pl.pallas_call(
        body,
        name="mid",
        grid=(n_tiles,),
        in_specs=[full, full, full, full,
                  half(SEG_GA // 512), half(SEG_GA // 512 + 1), half(SEG_GR // 512), half(SEG_GR // 512 + 1),
                  half(SEG_MA // 512), half(SEG_MA // 512 + 1), half(SEG_MR // 512), half(SEG_MR // 512 + 1),
                  any_spec, vec, vec],
        out_specs=[full, full, full, full, full, full, full, any_spec, pl.BlockSpec((8, D), lambda i: (0, 0))],
        out_shape=[f_tile, f_tile, f_tile, bf_tile, bf_tile, bf_tile, bf_tile,
                   jax.ShapeDtypeStruct((3, D, D), F32), jax.ShapeDtypeStruct((8, D), F32)],
        scratch_shapes=[pltpu.VMEM((3, D, D), BF16), pltpu.VMEM((3, D, D), F32), pltpu.SemaphoreType.DMA, pltpu.SemaphoreType.DMA((3 * N_DEV,))],
        compiler_params=_params(1),
    )(x, target, o, hh, proj, proj, proj, proj, proj, proj, proj, proj, w_all, gate, final_g)


_SEGMENTS = ((SEG_Q, 1024), (SEG_K, 256), (SEG_V, 256), (SEG_GA, 1024), (SEG_XR, 1024), (SEG_GR, 1024), (SEG_MA, 1024), (SEG_MR, 1024))


def _seg_specs(tm):
    return [pl.BlockSpec((tm, width), lambda i: (i, 0)) for _, width in _SEGMENTS]


def _bwd_dh(dsegs, w_all, x, dres, mod, norm_g, sum_w_bf):
    t = x.shape[0]
    tm = min(256, t)
    n_tiles = t // tm
    n_seg = len(_SEGMENTS)
    full = pl.BlockSpec((tm, D), lambda i: (i, 0))
    views = [lambda ref, peer: ref.at[_chip_of(peer)]]

    def body(*refs):
        segs = refs[:n_seg]
        w_hbm, x_ref, dres_ref, mod_ref, ng_ref, sum_hbm, gx_ref, vec_ref, land_hbm, w, sems, send_sems, recv_sems = refs[n_seg:]
        i = pl.program_id(0)

        @pl.when(i == 0)
        def _():
            for cp in _chip_copies([sum_hbm], [land_hbm], views, send_sems, recv_sems):
                cp.start()
            loads = [
                pltpu.make_async_copy(w_hbm.at[s, pl.ds(0, W_IN_SHARD), :], w.at[pl.ds(W_IN_SHARD * s, W_IN_SHARD), :], sems.at[s])
                for s in range(N_DEV)
            ]
            for cp in loads:
                cp.start()
            for cp in loads:
                cp.wait()
            vec_ref[...] = jnp.zeros_like(vec_ref)

        dh = sum(_dot(seg[...], w[start : start + width, :]) for seg, (start, width) in zip(segs, _SEGMENTS))
        xv = x_ref[...]
        r1 = lax.rsqrt(jnp.mean(xv * xv, axis=-1, keepdims=True) + NORM_EPS)
        xn = xv * r1
        ng = ng_ref[...]
        scale1 = 1.0 + mod_ref[:, D : 2 * D]
        d_shift = jnp.sum(dh, axis=0, keepdims=True)
        d_scale = jnp.sum(dh * (xn * ng), axis=0, keepdims=True)
        g_norm = jnp.sum(dh * xn * scale1, axis=0, keepdims=True)
        dxn = dh * (ng * scale1)
        dx = r1 * (dxn - xn * jnp.mean(dxn * xn, axis=-1, keepdims=True))
        gx_ref[...] = dres_ref[...] + dx
        vec_ref[...] = vec_ref[...] + jnp.concatenate([d_shift, d_scale, g_norm, jnp.zeros((5, D), F32)], axis=0)

        @pl.when(i == n_tiles - 1)
        def _():
            for cp in _chip_copies([sum_hbm], [land_hbm], views, send_sems, recv_sems):
                cp.wait_recv()
                cp.wait_send()

    any_spec = pl.BlockSpec(memory_space=pl.ANY)
    n_copies = len(CHIP_FLIPS)
    return pl.pallas_call(
        body,
        name="bwd_dh",
        grid=(n_tiles,),
        in_specs=_seg_specs(tm) + [any_spec, full, full, pl.BlockSpec((1, 3 * D), lambda i: (0, 0)), pl.BlockSpec((1, D), lambda i: (0, 0)),
                                   any_spec],
        out_specs=[full, pl.BlockSpec((8, D), lambda i: (0, 0)), any_spec],
        out_shape=[jax.ShapeDtypeStruct((t, D), F32), jax.ShapeDtypeStruct((8, D), F32),
                   jax.ShapeDtypeStruct((n_copies, W_IN_SHARD, D), BF16)],
        scratch_shapes=[pltpu.VMEM((IN_W, D), BF16), pltpu.SemaphoreType.DMA((N_DEV,)),
                        pltpu.SemaphoreType.DMA((n_copies,)), pltpu.SemaphoreType.DMA((n_copies,))],
        compiler_params=_params(1),
    )(*dsegs, w_all, x, dres, mod, norm_g, sum_w_bf)


def _bwd_gw(dsegs, h, sum_3_bf, sum_g_bf):
    t = h.shape[0]
    tm = min(512, t)
    n_tiles = t // tm
    n_seg = len(_SEGMENTS)
    views = [lambda ref, peer: ref.at[:, _chip_of(peer)], lambda ref, peer: ref.at[:, :, _chip_of(peer)]]

    def body(*refs):
        segs = refs[:n_seg]
        h_ref, s3_hbm, sg_hbm, out_hbm, l3_hbm, lg_hbm, acc, sem, send_sems, recv_sems = refs[n_seg:]
        i = pl.program_id(0)

        @pl.when(i == 0)
        def _():
            for cp in _chip_copies([s3_hbm, sg_hbm], [l3_hbm, lg_hbm], views, send_sems, recv_sems):
                cp.start()
            acc[...] = jnp.zeros_like(acc)

        hv = h_ref[...]
        for seg, (start, width) in zip(segs, _SEGMENTS):
            acc[start : start + width, :] = acc[start : start + width, :] + _dot_tn(seg[...], hv)

        @pl.when(i == n_tiles - 1)
        def _():
            cp = pltpu.make_async_copy(acc, out_hbm, sem)
            cp.start()
            cp.wait()
            for cp in _chip_copies([s3_hbm, sg_hbm], [l3_hbm, lg_hbm], views, send_sems, recv_sems):
                cp.wait_recv()
                cp.wait_send()

    any_spec = pl.BlockSpec(memory_space=pl.ANY)
    n_copies = 2 * len(CHIP_FLIPS)
    return pl.pallas_call(
        body,
        name="bwd_gw",
        grid=(n_tiles,),
        in_specs=_seg_specs(tm) + [pl.BlockSpec((tm, D), lambda i: (i, 0)), any_spec, any_spec],
        out_specs=[any_spec, any_spec, any_spec],
        out_shape=[jax.ShapeDtypeStruct((IN_W, D), F32),
                   jax.ShapeDtypeStruct((len(CHIP_FLIPS), 3, ROW_SHARD, D), BF16),
                   jax.ShapeDtypeStruct((len(CHIP_FLIPS), 2, 4, GATE_SHARD, 256), BF16)],
        scratch_shapes=[pltpu.VMEM((IN_W, D), F32), pltpu.SemaphoreType.DMA,
                        pltpu.SemaphoreType.DMA((n_copies,)), pltpu.SemaphoreType.DMA((n_copies,))],
        compiler_params=_params(1),
    )(*dsegs, h, sum_3_bf, sum_g_bf)


def _adamw(w, g, m, v):
    m = ADAM_B1 * m + (1.0 - ADAM_B1) * g
    v = ADAM_B2 * v + (1.0 - ADAM_B2) * (g * g)
    m_hat = m / (1.0 - ADAM_B1**ADAM_STEP)
    v_hat = v / (1.0 - ADAM_B2**ADAM_STEP)
    delta = -ADAM_LR * (m_hat / (jnp.sqrt(v_hat) + ADAM_EPS) + ADAM_WD * w)
    return delta, m, v


def _row_block(rows):
    return 128 if rows % 128 == 0 else rows


def _add_slots(own, l_ref):
    total = own
    for s in range(l_ref.shape[0]):
        total = total + l_ref[s].astype(F32)
    return total


def _sum_slots(name, own, land):
    n_slots, rows, cols = land.shape
    rb = 208 if rows == W_IN_SHARD else _row_block(rows)

    def body(o_ref, l_ref, g_ref):
        g_ref[...] = _add_slots(o_ref[...], l_ref)

    spec = pl.BlockSpec((rb, cols), lambda i: (i, 0))
    return pl.pallas_call(
        body,
        name=name,
        grid=(rows // rb,),
        in_specs=[spec, pl.BlockSpec((n_slots, rb, cols), lambda i: (0, i, 0))],
        out_specs=spec,
        out_shape=jax.ShapeDtypeStruct((rows, cols), F32),
        compiler_params=_params(1),
    )(own, land)


def _adam_update(name, g, w, m, v):
    rows, cols = w.shape
    rb = _row_block(rows)
    spec = pl.BlockSpec((rb, cols), lambda i: (i, 0))

    def body(g_ref, w_ref, m_ref, v_ref, d_ref, m2_ref, v2_ref):
        d_ref[...], m2_ref[...], v2_ref[...] = _adamw(w_ref[...], g_ref[...], m_ref[...], v_ref[...])

    shape = jax.ShapeDtypeStruct((rows, cols), F32)
    return pl.pallas_call(
        body, name=name, grid=(rows // rb,), in_specs=[spec] * 4, out_specs=[spec] * 3, out_shape=[shape] * 3,
        compiler_params=_params(1),
    )(g, w, m, v)


def _shard_update(own_3, land_3, own_g, land_g, weights, m_in, v_in):
    n = len(weights)

    def body(*refs):
        o3_ref, l3_ref, og_ref, lg_ref = refs[:4]
        w_refs, m_refs, v_refs = (refs[4 + n * k : 4 + n * (k + 1)] for k in range(3))
        outs = refs[4 + 3 * n :]
        for k in range(n):
            if k < 3:
                g = o3_ref[k]
                for slot in range(3):
                    g = g + l3_ref[slot, k].astype(F32)
            else:
                g = og_ref[k - 3]
                for slot in range(3):
                    g = g + lg_ref[slot, k - 3].astype(F32)
            d, m2, v2 = _adamw(w_refs[k][0], g, m_refs[k][0], v_refs[k][0])
            for which, val in enumerate((g, d, m2, v2)):
                outs[which * n + k][0] = val

    shapes = [jax.ShapeDtypeStruct(w.shape, F32) for w in weights]
    res = pl.pallas_call(
        body, name="shard_update", out_shape=shapes * 4, compiler_params=pltpu.CompilerParams(vmem_limit_bytes=VMEM_LIMIT)
    )(own_3, land_3, own_g, land_g, *weights, *m_in, *v_in)
    return [res[n * k : n * (k + 1)] for k in range(4)]


VEC_LAYOUT = (("b_ada", 3 * D), ("norm_g", D), ("attn_sinks", 128), ("conv_b", D), ("rg_ba", D), ("rg_bx", D), ("rg_lambda", D),
              ("final_g", D), ("conv_w", CONV_WIDTH * D), ("loss", 128))
VEC_OFFSET = {name: sum(w for _, w in VEC_LAYOUT[:k]) for k, (name, _) in enumerate(VEC_LAYOUT)}
VEC_WIDTH = sum(w for _, w in VEC_LAYOUT)
N_REPLICATED = 8


def _small_update(vec_all, c_all_t, me, rep_w, rep_m, rep_v, ada, conv):
    n_ada = ada[0].shape[1]
    n_conv = conv[0].shape[1]

    def body(*refs):
        me_ref, vec_ref, ct_ref = refs[:3]
        w_refs, m_refs, v_refs = (refs[3 + N_REPLICATED * k : 3 + N_REPLICATED * (k + 1)] for k in range(3))
        pos = 3 + 3 * N_REPLICATED
        ada_refs, conv_refs = refs[pos : pos + 3], refs[pos + 3 : pos + 6]
        outs = refs[pos + 6 :]
        loss_ref, rep_outs = outs[0], outs[1 : 1 + 4 * N_REPLICATED]
        ada_outs, conv_outs = outs[1 + 4 * N_REPLICATED : 5 + 4 * N_REPLICATED], outs[5 + 4 * N_REPLICATED : 9 + 4 * N_REPLICATED]
        conv_buf = outs[9 + 4 * N_REPLICATED]

        total = vec_ref[0:1, :]
        for slot in range(1, N_DEV):
            total = total + vec_ref[slot : slot + 1, :]
        loss_ref[...] = total[:, VEC_OFFSET["loss"] : VEC_OFFSET["loss"] + 1]
        for k, (name, _) in enumerate(VEC_LAYOUT[:N_REPLICATED]):
            width = w_refs[k].shape[1]
            g = total[:, VEC_OFFSET[name] : VEC_OFFSET[name] + width]
            d, m2, v2 = _adamw(w_refs[k][...], g, m_refs[k][...], v_refs[k][...])
            for which, val in enumerate((g, d, m2, v2)):
                rep_outs[which * N_REPLICATED + k][...] = val

        device = me_ref[0]
        for k in range(CONV_WIDTH):
            conv_buf[k : k + 1, :] = total[:, VEC_OFFSET["conv_w"] + D * k : VEC_OFFSET["conv_w"] + D * (k + 1)]
        g_conv = conv_buf[:, pl.ds(pl.multiple_of(device * n_conv, 128), n_conv)]
        for ref, val in zip(conv_outs, (g_conv,) + _adamw(conv_refs[0][...], g_conv, conv_refs[1][...], conv_refs[2][...])):
            ref[...] = val

        dmod = vec_ref[:, pl.ds(pl.multiple_of(device * n_ada, 128), n_ada)]
        g_ada = _dot(ct_ref[...], dmod)
        for ref, val in zip(ada_outs, (g_ada,) + _adamw(ada_refs[0][...], g_ada, ada_refs[1][...], ada_refs[2][...])):
            ref[...] = val

    vmem = pl.BlockSpec(memory_space=pltpu.VMEM)
    rep_shapes = [jax.ShapeDtypeStruct(w.shape, F32) for w in rep_w]
    out_shape = ([jax.ShapeDtypeStruct((1, 1), F32)] + rep_shapes * 4 + [jax.ShapeDtypeStruct(ada[0].shape, F32)] * 4
                 + [jax.ShapeDtypeStruct(conv[0].shape, F32)] * 4)
    n_in = 3 + 3 * N_REPLICATED + 6
    res = pl.pallas_call(
        body,
        name="small_update",
        in_specs=[pl.BlockSpec(memory_space=pltpu.SMEM)] + [vmem] * (n_in - 1),
        out_specs=[vmem] * len(out_shape),
        out_shape=out_shape,
        scratch_shapes=[pltpu.VMEM((CONV_WIDTH, D), F32)],
        compiler_params=pltpu.CompilerParams(vmem_limit_bytes=VMEM_LIMIT),
    )(me, vec_all, c_all_t, *rep_w, *rep_m, *rep_v, *ada, *conv)
    loss = res[0]
    reps = [res[1 + N_REPLICATED * k : 1 + N_REPLICATED * (k + 1)] for k in range(4)]
    return loss, reps, res[1 + 4 * N_REPLICATED : 5 + 4 * N_REPLICATED], res[5 + 4 * N_REPLICATED : 9 + 4 * N_REPLICATED]


def kernel(x, c, positions, w_ada, b_ada, norm_g, w_in, attn_sinks, conv_w, conv_b, rg_wa, rg_ba, rg_wx, rg_bx, rg_lambda, w_attn_proj, w_rnn_proj, w_out, final_g, loss_target, m_w_ada, m_b_ada, m_norm_g, m_w_in, m_attn_sinks, m_conv_w, m_conv_b, m_rg_wa, m_rg_ba, m_rg_wx, m_rg_bx, m_rg_lambda, m_w_attn_proj, m_w_rnn_proj, m_w_out, m_final_g, v_w_ada, v_b_ada, v_norm_g, v_w_in, v_attn_sinks, v_conv_w, v_conv_b, v_rg_wa, v_rg_ba, v_rg_wx, v_rg_bx, v_rg_lambda, v_w_attn_proj, v_w_rnn_proj, v_w_out, v_final_g):
    my_x, my_y, my_c = _position()
    me = _slot((my_x, my_y, my_c))
    x2, target = x[0], loss_target[0]
    pos = positions[0]

    gate_rows = lambda w: w[0].transpose(1, 0, 2).reshape(GATE_SHARD, D)
    w_pack = jnp.concatenate([w_in[0].T, w_attn_proj[0], w_rnn_proj[0], w_out[0], gate_rows(rg_wa), gate_rows(rg_wx)], axis=0).astype(BF16)
    conv_rows = jnp.pad(conv_w[0], ((0, 8 - CONV_WIDTH), (0, 0)))
    c_rows, conv_all, w_all = _all_gather("ag_fwd", [jnp.broadcast_to(c, (8, D)), conv_rows, w_pack])
    c_all = c_rows[:, 0, :]
    conv_full = conv_all[:, :CONV_WIDTH, :].transpose(1, 0, 2).reshape(CONV_WIDTH, D)

    n_ada = w_ada.shape[2]
    b_cols = lax.dynamic_slice(b_ada, (0, me * n_ada), (1, n_ada))
    mod_cols = _ada_mod(c_all, w_ada[0], b_cols)
    (mod_all,) = _all_gather_direct("ag_mod", [mod_cols])
    mod = lax.dynamic_slice(mod_all, (0, me, 0), (N_DEV, 1, n_ada)).reshape(1, 3 * D)
    gate = mod[:, 2 * D :]

    tables = _rope_tables(pos)
    proj, h = _fwd_in(x2, mod, norm_g, w_all, tables[0])
    o = _attn_fwd(proj, attn_sinks)
    keep = (pos != 0).astype(F32)[:, None]
    hh = _rnn_fwd(proj, keep, w_all, rg_ba, rg_bx, rg_lambda, conv_full, conv_b)

    dres, d_o, d_hh, d_ga, d_gr, d_ma, d_mr, gw3, mid_vec = _mid(x2, target, o, hh, proj, w_all, gate, final_g[None, :])
    d_q, d_k, d_v, d_sink = _attn_bwd(proj, o, d_o, tables, attn_sinks)
    d_xr, g_gates, rnn_vec = _rnn_bwd(proj, hh, d_hh, keep, w_all, rg_ba, rg_bx, rg_lambda, conv_full, conv_b)
    dsegs = (d_q, d_k, d_v, d_ga, d_xr, d_gr, d_ma, d_mr)

    index = jnp.stack([my_c, _chip_of((my_x, my_y))]).astype(jnp.int32)
    part_3 = gw3.reshape(3, 4, 2, ROW_SHARD, D)
    part_g = g_gates.reshape(2, 4, 4, 2, GATE_SHARD, 256)
    recv_3, recv_g = _exchange(
        "rs_pair_small",
        [part_3, part_g],
        [lambda ref, peer: ref.at[:, :, peer[2]], lambda ref, peer: ref.at[:, :, :, peer[2]]],
        [(3, 4, ROW_SHARD, D), (2, 4, 4, GATE_SHARD, 256)],
        flips=(1,),
    )
    sum_3_bf, own_3 = _pair_add(
        "pair_add_3", part_3, recv_3[0], index, (3, 4), 1,
        pl.BlockSpec((None, None, None, ROW_SHARD, D), lambda a, q, idx: (a, q, idx[0], 0, 0)),
        pl.BlockSpec((None, None, ROW_SHARD, D), lambda a, q, idx: (a, q, 0, 0)),
        pl.BlockSpec((None, ROW_SHARD, D), lambda a, q, idx: (a, 0, 0)), (3, ROW_SHARD, D),
    )
    sum_g_bf, own_g = _pair_add(
        "pair_add_g", part_g, recv_g[0], index, (4,), 0,
        pl.BlockSpec((2, 4, None, None, GATE_SHARD, 256), lambda q, idx: (0, 0, q, idx[0], 0, 0)),
        pl.BlockSpec((2, 4, None, GATE_SHARD, 256), lambda q, idx: (0, 0, q, 0, 0)),
        pl.BlockSpec((2, 4, GATE_SHARD, 256), lambda q, idx: (0, 0, 0, 0)), (2, 4, GATE_SHARD, 256),
    )
    g_w_t, land_3, land_g = _bwd_gw(dsegs, h, sum_3_bf, sum_g_bf)
    part_w = g_w_t.reshape(4, 2, W_IN_SHARD, D)
    (recv_w,) = _exchange("rs_pair_w", [part_w], [lambda ref, peer: ref.at[:, peer[2]]], [(4, W_IN_SHARD, D)], flips=(1,))
    rb = 208
    sum_w_bf, own_w = _pair_add(
        "pair_add_w", part_w, recv_w[0], index, (W_IN_SHARD // rb, 4), 1,
        pl.BlockSpec((None, None, rb, D), lambda r, q, idx: (q, idx[0], r, 0)),
        pl.BlockSpec((None, rb, D), lambda r, q, idx: (q, r, 0)),
        pl.BlockSpec((rb, D), lambda r, q, idx: (r, 0)), (W_IN_SHARD, D),
    )
    grad_x, dh_vec, land_w = _bwd_dh(dsegs, w_all, x2, dres, mod, norm_g, sum_w_bf)

    pieces = dict(
        b_ada=jnp.concatenate([dh_vec[0:1], dh_vec[1:2], mid_vec[1:2]], axis=1),
        norm_g=dh_vec[2:3],
        attn_sinks=jnp.pad(d_sink[:, :, 0].reshape(1, 16), ((0, 0), (0, 112))),
        conv_b=rnn_vec[3:4],
        rg_ba=rnn_vec[0:1],
        rg_bx=rnn_vec[1:2],
        rg_lambda=rnn_vec[2:3],
        final_g=mid_vec[0:1],
        conv_w=rnn_vec[4:8].reshape(1, CONV_WIDTH * D),
        loss=mid_vec[2:3, 0:128],
    )
    vec = jnp.concatenate([pieces[name] for name, _ in VEC_LAYOUT], axis=1)
    (vec_all,) = _all_gather_direct("ag_vec", [jnp.broadcast_to(vec, (8, VEC_WIDTH))])
    rep_w = [b_ada, norm_g, attn_sinks, conv_b, rg_ba, rg_bx, rg_lambda, final_g[None, :]]
    rep_m = [m_b_ada, m_norm_g, m_attn_sinks, m_conv_b, m_rg_ba, m_rg_bx, m_rg_lambda, m_final_g[None, :]]
    rep_v = [v_b_ada, v_norm_g, v_attn_sinks, v_conv_b, v_rg_ba, v_rg_bx, v_rg_lambda, v_final_g[None, :]]
    loss, reps, ada_res, conv_res = _small_update(
        vec_all[:, 0, :], c_all.T, me.astype(jnp.int32).reshape(1), rep_w, rep_m, rep_v,
        (w_ada[0], m_w_ada[0], v_w_ada[0]), (conv_w[0], m_conv_w[0], v_conv_w[0]),
    )

    g_in = _sum_slots("sum_w_in", own_w, land_w).T
    d_in, m_in, v_in = _adam_update("adam_w_in", g_in, w_in[0], m_w_in[0], v_w_in[0])
    shard_res = _shard_update(
        own_3, land_3, own_g, land_g,
        (w_attn_proj, w_rnn_proj, w_out, rg_wa, rg_wx),
        (m_w_attn_proj, m_w_rnn_proj, m_w_out, m_rg_wa, m_rg_wx),
        (v_w_attn_proj, v_w_rnn_proj, v_w_out, v_rg_wa, v_rg_wx),
    )

    def by_weight(k):
        b, n, s, cb, ba, bx, lam, fg = reps[k]
        ap, rp, out, wa, wx = shard_res[k]
        w_in_k = (g_in, d_in, m_in, v_in)[k]
        return [ada_res[k][None], b, n, w_in_k[None], s, conv_res[k][None], cb, wa, ba, wx, bx, lam, ap, rp, out, fg[0]]

    return (loss[0, 0], grad_x[None], *by_weight(0), *by_weight(1), *by_weight(2), *by_weight(3))
```

```python
import functools
import math

import jax
import jax.numpy as jnp
import numpy as np
from jax import lax
from jax.experimental import pallas as pl
from jax.experimental.pallas import tpu as pltpu

F32 = jnp.float32
BF16 = jnp.bfloat16

D = 1024
HEAD_DIM = 64
ROT_DIM = 16
ROPE_THETA = 500000.0
WINDOW = 128
LRU_C = 8.0
CONV_WIDTH = 4
NORM_EPS = 1e-6
IN_W = 6656
SEG_Q, SEG_K, SEG_V, SEG_GA, SEG_XR, SEG_GR, SEG_MA, SEG_MR = 0, 1024, 1280, 1536, 2560, 3584, 4608, 5632
N_DEV = 8
W_IN_SHARD = IN_W // N_DEV
ROW_SHARD = D // N_DEV
GATE_SHARD = 256 // N_DEV

ADAM_LR = 0.001
ADAM_B1 = 0.9
ADAM_B2 = 0.999
ADAM_EPS = 1e-08
ADAM_WD = 0.01
ADAM_STEP = 10

VMEM_LIMIT = 60 * 1024 * 1024
MESH = pl.DeviceIdType.MESH
AXES = ("x", "y", "c")


def _dot(a, b):
    return jnp.dot(a, b, preferred_element_type=F32)


def _dot_nt(a, b):
    return lax.dot_general(a, b, (((1,), (1,)), ((), ())), preferred_element_type=F32)


def _dot_tn(a, b):
    return lax.dot_general(a, b, (((0,), (0,)), ((), ())), preferred_element_type=F32)


def _sigmoid(z):
    return 1.0 / (1.0 + jnp.exp(-z))


def _params(n_grid, vmem=VMEM_LIMIT):
    return pltpu.CompilerParams(dimension_semantics=("arbitrary",) * n_grid, vmem_limit_bytes=vmem)


def _position():
    return lax.axis_index("x"), lax.axis_index("y"), lax.axis_index("c")


def _flip(pos, k):
    x, y, c = pos
    return (1 - x if k & 4 else x, 1 - y if k & 2 else y, 1 - c if k & 1 else c)


def _slot(pos):
    return 4 * pos[0] + 2 * pos[1] + pos[2]


def _all_gather(name, arrs):
    n = len(arrs)

    def body(*refs):
        ins, outs = refs[:n], refs[n : 2 * n]
        send_sems, recv_sems, local_sems = refs[2 * n :]
        me = _position()
        sibling = _flip(me, 1)
        chips = [2, 4, 6]

        def copy(a, k, block, to, src=None):
            return pltpu.make_async_remote_copy(
                src_ref=outs[a].at[_slot(block)] if src is None else src,
                dst_ref=outs[a].at[_slot(block)],
                send_sem=send_sems.at[a * 7 + k],
                recv_sem=recv_sems.at[a * 7 + k],
                device_id=to,
                device_id_type=MESH,
            )

        mine = [pltpu.make_async_copy(ins[a], outs[a].at[_slot(me)], local_sems.at[a]) for a in range(n)]
        for cp in mine:
            cp.start()
        first = []
        for a in range(n):
            first.append(copy(a, 0, me, sibling, src=ins[a]))
            first += [copy(a, 1 + j, me, _flip(me, k), src=ins[a]) for j, k in enumerate(chips)]
        for cp in first:
            cp.start()
        passed = []
        for a in range(n):
            for j, k in enumerate(chips):
                copy(a, 1 + j, _flip(me, k), me).wait_recv()
                fwd = copy(a, 4 + j, _flip(me, k), sibling)
                fwd.start()
                passed.append(fwd)
        for a in range(n):
            copy(a, 0, sibling, me).wait_recv()
            for j, k in enumerate(chips):
                copy(a, 4 + j, _flip(sibling, k), me).wait_recv()
        for cp in first + passed:
            cp.wait_send()
        for cp in mine:
            cp.wait()

    any_spec = pl.BlockSpec(memory_space=pl.ANY)
    return pl.pallas_call(
        body,
        name=name,
        out_shape=[jax.ShapeDtypeStruct((N_DEV,) + a.shape, a.dtype) for a in arrs],
        in_specs=[any_spec] * n,
        out_specs=[any_spec] * n,
        scratch_shapes=[
            pltpu.SemaphoreType.DMA((7 * n,)),
            pltpu.SemaphoreType.DMA((7 * n,)),
            pltpu.SemaphoreType.DMA((n,)),
        ],
    )(*arrs)


def _all_gather_direct(name, arrs):
    n = len(arrs)

    def body(*refs):
        ins, outs = refs[:n], refs[n : 2 * n]
        send_sems, recv_sems, local_sems = refs[2 * n :]
        me = _position()
        mine = [pltpu.make_async_copy(ins[a], outs[a].at[_slot(me)], local_sems.at[a]) for a in range(n)]
        for cp in mine:
            cp.start()

        def copy(a, k, block):
            return pltpu.make_async_remote_copy(
                src_ref=ins[a],
                dst_ref=outs[a].at[_slot(block)],
                send_sem=send_sems.at[a * 7 + k - 1],
                recv_sem=recv_sems.at[a * 7 + k - 1],
                device_id=_flip(me, k),
                device_id_type=MESH,
            )

        sent = [copy(a, k, me) for a in range(n) for k in range(1, N_DEV)]
        for cp in sent:
            cp.start()
        for a in range(n):
            for k in range(1, N_DEV):
                copy(a, k, _flip(me, k)).wait_recv()
        for cp in sent:
            cp.wait_send()
        for cp in mine:
            cp.wait()

    any_spec = pl.BlockSpec(memory_space=pl.ANY)
    return pl.pallas_call(
        body,
        name=name,
        out_shape=[jax.ShapeDtypeStruct((N_DEV,) + a.shape, a.dtype) for a in arrs],
        in_specs=[any_spec] * n,
        out_specs=[any_spec] * n,
        scratch_shapes=[
            pltpu.SemaphoreType.DMA((7 * n,)),
            pltpu.SemaphoreType.DMA((7 * n,)),
            pltpu.SemaphoreType.DMA((n,)),
        ],
    )(*arrs)


CHIP_FLIPS = (2, 4, 6)


def _chip_of(pos):
    return 2 * pos[0] + pos[1]


def _chip_copies(srcs, lands, views, send_sems, recv_sems):
    me = _position()
    copies = []
    for a, (src, land) in enumerate(zip(srcs, lands)):
        for j, k in enumerate(CHIP_FLIPS):
            peer = _flip(me, k)
            copies.append(
                pltpu.make_async_remote_copy(
                    src_ref=views[a](src, peer),
                    dst_ref=land.at[j],
                    send_sem=send_sems.at[len(CHIP_FLIPS) * a + j],
                    recv_sem=recv_sems.at[len(CHIP_FLIPS) * a + j],
                    device_id=peer,
                    device_id_type=MESH,
                )
            )
    return copies


def _exchange(name, arrs, views, out_shapes, flips):
    n = len(arrs)
    nf = len(flips)

    def body(*refs):
        ins, outs = refs[:n], refs[n : 2 * n]
        send_sems, recv_sems = refs[2 * n :]
        me = _position()

        def copy(a, j):
            peer = _flip(me, flips[j])
            return pltpu.make_async_remote_copy(
                src_ref=views[a](ins[a], peer),
                dst_ref=outs[a].at[j],
                send_sem=send_sems.at[a * nf + j],
                recv_sem=recv_sems.at[a * nf + j],
                device_id=peer,
                device_id_type=MESH,
            )

        sent = [copy(a, j) for a in range(n) for j in range(nf)]
        for cp in sent:
            cp.start()
        for cp in sent:
            cp.wait_recv()
        for cp in sent:
            cp.wait_send()

    any_spec = pl.BlockSpec(memory_space=pl.ANY)
    return pl.pallas_call(
        body,
        name=name,
        out_shape=[jax.ShapeDtypeStruct((nf,) + tuple(sh), a.dtype) for sh, a in zip(out_shapes, arrs)],
        in_specs=[any_spec] * n,
        out_specs=[any_spec] * n,
        scratch_shapes=[pltpu.SemaphoreType.DMA((nf * n,)), pltpu.SemaphoreType.DMA((nf * n,))],
    )(*arrs)


def _pair_add(name, own, recv, index, grid, chip_axis, own_spec, recv_spec, mine_spec, mine_shape):
    def body(idx_ref, own_ref, recv_ref, sb_ref, mine_ref):
        total = own_ref[...] + recv_ref[...]
        sb_ref[...] = total.astype(BF16)

        @pl.when(pl.program_id(chip_axis) == idx_ref[1])
        def _():
            mine_ref[...] = total

    return pl.pallas_call(
        body,
        name=name,
        grid_spec=pltpu.PrefetchScalarGridSpec(
            num_scalar_prefetch=1, grid=grid, in_specs=[own_spec, recv_spec], out_specs=[recv_spec, mine_spec]
        ),
        out_shape=[jax.ShapeDtypeStruct(recv.shape, BF16), jax.ShapeDtypeStruct(mine_shape, F32)],
        compiler_params=_params(len(grid)),
    )(index, own, recv)


def _ada_mod(c_all, w_ada, b_cols):
    def body(c_ref, w_ref, b_ref, o_ref):
        o_ref[...] = _dot(c_ref[...].astype(BF16), w_ref[...].astype(BF16)) + b_ref[...]

    return pl.pallas_call(
        body,
        name="ada_mod",
        out_shape=jax.ShapeDtypeStruct((N_DEV, w_ada.shape[1]), F32),
        compiler_params=pltpu.CompilerParams(vmem_limit_bytes=VMEM_LIMIT),
    )(c_all, w_ada, b_cols)


PACK_GATE = 3 * ROW_SHARD
PACK_ROWS = PACK_GATE + 2 * GATE_SHARD


def _fwd_in(x, mod, norm_g, w_all, tables, pack):
    t = x.shape[0]
    tm = min(512, t)
    tn = IN_W // 4
    n_i, n_j = t // tm, IN_W // tn
    n_rope = (SEG_V - SEG_Q) // 128

    def gather_copies(pack_hbm, all_hbm, send_sems, recv_sems, arriving):
        me = _position()
        return [
            pltpu.make_async_remote_copy(
                src_ref=pack_hbm,
                dst_ref=all_hbm.at[_slot(_flip(me, k) if arriving else me)],
                send_sem=send_sems.at[k - 1],
                recv_sem=recv_sems.at[k - 1],
                device_id=_flip(me, k),
                device_id_type=MESH,
            )
            for k in range(1, N_DEV)
        ]

    def body(x_ref, mod_ref, ng_ref, wa_ref, wb_ref, cos_ref, s1_ref, s2_ref, pack_hbm, proj_ref, h_ref, all_hbm,
             send_sems, recv_sems, local_sem):
        i = pl.program_id(0)
        j = pl.program_id(1)
        own = pltpu.make_async_copy(pack_hbm, all_hbm.at[_slot(_position())], local_sem)

        @pl.when((i == 0) & (j == 0))
        def _():
            own.start()
            for cp in gather_copies(pack_hbm, all_hbm, send_sems, recv_sems, arriving=False):
                cp.start()

        @pl.when(j == 0)
        def _():
            xv = x_ref[...]
            r = lax.rsqrt(jnp.mean(xv * xv, axis=-1, keepdims=True) + NORM_EPS)
            shift = mod_ref[:, 0:D]
            scale = mod_ref[:, D : 2 * D]
            h_ref[...] = (((xv * r) * ng_ref[...]) * (1.0 + scale) + shift).astype(BF16)

        acc = _dot_nt(h_ref[...], jnp.concatenate([wa_ref[...], wb_ref[...]], axis=0))

        @pl.when(j == 0)
        def _():
            cos, s1, s2 = cos_ref[...], s1_ref[...], s2_ref[...]
            for g in range(n_rope):
                cols = slice(128 * g, 128 * (g + 1))
                proj_ref[:, cols] = _rope(acc[:, cols], cos, s1, s2)
            proj_ref[:, 128 * n_rope :] = acc[:, 128 * n_rope :]

        @pl.when(j != 0)
        def _():
            proj_ref[...] = acc

        @pl.when((i == n_i - 1) & (j == n_j - 1))
        def _():
            for cp in gather_copies(pack_hbm, all_hbm, send_sems, recv_sems, arriving=True):
                cp.wait_recv()
            for cp in gather_copies(pack_hbm, all_hbm, send_sems, recv_sems, arriving=False):
                cp.wait_send()
            own.wait()

    tab = pl.BlockSpec((tm, 128), lambda i, j: (i, 0))
    any_spec = pl.BlockSpec(memory_space=pl.ANY)
    return pl.pallas_call(
        body,
        name="fwd_in",
        grid=(n_i, n_j),
        in_specs=[
            pl.BlockSpec((tm, D), lambda i, j: (i, 0)),
            pl.BlockSpec((1, 3 * D), lambda i, j: (0, 0)),
            pl.BlockSpec((1, D), lambda i, j: (0, 0)),
            pl.BlockSpec((None, W_IN_SHARD, D), lambda i, j: (2 * j, 0, 0)),
            pl.BlockSpec((None, W_IN_SHARD, D), lambda i, j: (2 * j + 1, 0, 0)),
            tab, tab, tab, any_spec,
        ],
        out_specs=[pl.BlockSpec((tm, tn), lambda i, j: (i, j)), pl.BlockSpec((tm, D), lambda i, j: (i, 0)), any_spec],
        out_shape=[jax.ShapeDtypeStruct((t, IN_W), F32), jax.ShapeDtypeStruct((t, D), BF16),
                   jax.ShapeDtypeStruct((N_DEV,) + pack.shape, pack.dtype)],
        scratch_shapes=[pltpu.SemaphoreType.DMA((N_DEV - 1,)), pltpu.SemaphoreType.DMA((N_DEV - 1,)), pltpu.SemaphoreType.DMA],
        compiler_params=_params(2),
    )(x, mod, norm_g, w_all, w_all, *tables, pack)


def _rope_tables(positions):
    half = ROT_DIM // 2
    inv_freq = ROPE_THETA ** (-jnp.arange(0, ROT_DIM, 2, dtype=F32) / ROT_DIM)
    pos = positions.astype(F32)
    lane = np.arange(128) % HEAD_DIM
    freq_lane = jnp.where(lane < ROT_DIM, jnp.tile(inv_freq, 128 // half), 0.0)
    ang = pos[:, None] * freq_lane[None, :]
    sin = jnp.sin(ang)
    lane_tables = (jnp.cos(ang), jnp.where(lane < half, -sin, 0.0), jnp.where((lane >= half) & (lane < ROT_DIM), sin, 0.0))
    ang_t = inv_freq[:, None] * pos[None, :]
    return lane_tables, (jnp.cos(ang_t), jnp.sin(ang_t))


def _rope(tv, cos, s1, s2):
    cols = []
    for g in range(tv.shape[1] // 128):
        blk = tv[:, 128 * g : 128 * (g + 1)]
        cols.append(blk * cos + pltpu.roll(blk, 128 - 8, 1) * s1 + pltpu.roll(blk, 8, 1) * s2)
    return cols[0] if len(cols) == 1 else jnp.concatenate(cols, axis=1)


def _rope_transposed(gv, cos, s1, s2):
    cols = []
    for g in range(gv.shape[1] // 128):
        blk = gv[:, 128 * g : 128 * (g + 1)]
        cols.append(blk * cos + pltpu.roll(blk * s1, 8, 1) + pltpu.roll(blk * s2, 128 - 8, 1))
    return cols[0] if len(cols) == 1 else jnp.concatenate(cols, axis=1)


def _lane_lo():
    return lax.broadcasted_iota(jnp.int32, (1, 128), 1) < HEAD_DIM


def _row_lo():
    return lax.broadcasted_iota(jnp.int32, (128, 1), 0) < HEAD_DIM


def _split_heads(tile, j):
    lo = _lane_lo()
    parts = []
    for g in (2 * j, 2 * j + 1):
        blk = tile[:, 128 * g : 128 * (g + 1)]
        parts += [jnp.where(lo, blk, 0.0), jnp.where(lo, 0.0, blk)]
    return jnp.concatenate(parts, axis=0)


def _dup_lanes(kv, j):
    rolled = pltpu.roll(kv, HEAD_DIM, 1)
    lo = _lane_lo()
    return jnp.where(lo, kv, rolled) if j == 0 else jnp.where(lo, rolled, kv)


def _fold_lanes(acc):
    return acc + pltpu.roll(acc, HEAD_DIM, 1)


def _fill_bias_t(bias_ref):
    kj = lax.broadcasted_iota(jnp.int32, (256, 128), 0)
    qi = lax.broadcasted_iota(jnp.int32, (256, 128), 1)
    diff = qi + WINDOW - kj
    band = (diff >= 0) & (diff < WINDOW)
    bias_ref[0] = jnp.where(band, 0.0, -1e30)
    bias_ref[1] = jnp.where(band & (kj >= 128), 0.0, -1e30)


def _sink_row(sink_ref, pair, j):
    return jnp.concatenate([jnp.full((1, 128), sink_ref[0, 8 * pair + 4 * j + h], F32) for h in range(4)], axis=1)


def _softmax_t(q4, kw, bias4, sink):
    s = _dot_nt(kw, q4) + bias4
    m = jnp.maximum(jnp.max(s, axis=0, keepdims=True), sink)
    p = jnp.exp(s - m)
    e_sink = jnp.exp(sink - m)
    denom = jnp.sum(p, axis=0, keepdims=True) + e_sink
    return p, e_sink, denom


def _attn_specs(t, tq):
    return dict(
        q=pl.BlockSpec((tq, 512), lambda p, i: (i, p)),
        k=pl.BlockSpec((t, 128), lambda p, i: (0, SEG_K // 128 + p)),
        v=pl.BlockSpec((t, 128), lambda p, i: (0, SEG_V // 128 + p)),
        sink=pl.BlockSpec(memory_space=pltpu.SMEM),
        act=pl.BlockSpec((tq, 512), lambda p, i: (i, p)),
    )


def _attn_fwd(proj, sinks):
    t = proj.shape[0]
    tq = min(512, t)
    nb = tq // 128
    sp = _attn_specs(t, tq)

    def body(q_ref, k_ref, v_ref, sink_ref, o_ref, kd, vdt, bias_ref):
        pair = pl.program_id(0)
        i = pl.program_id(1)

        @pl.when(i == 0)
        def _():
            kv, vv = k_ref[...], v_ref[...]
            for j in range(2):
                kd[j, 0:128, :] = jnp.zeros((128, 128), BF16)
                kd[j, 128:, :] = _dup_lanes(kv, j).astype(BF16)
                vdt[j, :, 0:128] = jnp.zeros((128, 128), BF16)
                vdt[j, :, 128:] = _dup_lanes(vv, j).T.astype(BF16)
            _fill_bias_t(bias_ref)

        row_lo = _row_lo()

        def block(b, carry):
            r0 = pl.multiple_of(b * 128, 128)
            gb = i * nb + b
            keys = pl.ds(pl.multiple_of(gb * 128, 128), 256)
            rows = pl.ds(r0, 128)
            qt = q_ref[rows, :] * (1.0 / math.sqrt(HEAD_DIM))
            bias = bias_ref[jnp.where(gb == 0, 1, 0)]
            bias4 = jnp.concatenate([bias] * 4, axis=1)
            cols = []
            for j in range(2):
                q4 = _split_heads(qt, j).astype(BF16)
                p, _, denom = _softmax_t(q4, kd[j, keys, :], bias4, _sink_row(sink_ref, pair, j))
                ot = _dot(vdt[j, :, keys], p.astype(BF16)) * (1.0 / denom)
                for gi in range(2):
                    z = jnp.where(row_lo, ot[:, 256 * gi : 256 * gi + 128], ot[:, 256 * gi + 128 : 256 * gi + 256])
                    cols.append(z.T)
            o_ref[rows, :] = jnp.concatenate(cols, axis=1)
            return carry

        lax.fori_loop(0, nb, block, 0)

    return pl.pallas_call(
        body,
        name="attn_fwd",
        grid=(2, t // tq),
        in_specs=[sp["q"], sp["k"], sp["v"], sp["sink"]],
        out_specs=sp["act"],
        out_shape=jax.ShapeDtypeStruct((t, D), F32),
        scratch_shapes=[pltpu.VMEM((2, t + 128, 128), BF16), pltpu.VMEM((2, 128, t + 128), BF16), pltpu.VMEM((2, 256, 128), F32)],
        compiler_params=_params(2),
    )(proj, proj, proj, sinks)


def _attn_bwd(proj, o, d_o, tables, sinks):
    t = proj.shape[0]
    tq = min(512, t)
    nb = tq // 128
    n_tiles = t // tq
    (cos, s1, s2), (cos_t, sin_t) = tables
    sp = _attn_specs(t, tq)
    scale = 1.0 / math.sqrt(HEAD_DIM)
    half = ROT_DIM // 2

    def unrope_rows(z, c, s):
        out = []
        for base in (0, HEAD_DIM):
            g0, g1 = z[base : base + half], z[base + half : base + ROT_DIM]
            out += [g0 * c + g1 * s, g1 * c - g0 * s, z[base + ROT_DIM : base + HEAD_DIM]]
        return jnp.concatenate(out, axis=0)

    def body(q_ref, k_ref, v_ref, o_ref, do_ref, ct_ref, st_ref, ck, s1k, s2k, sink_ref, dq_ref, dk_ref, dv_ref, dsink_ref,
             kd, kdt, vd, bias_ref, dkacc, dvacc):
        pair = pl.program_id(0)
        i = pl.program_id(1)

        @pl.when(i == 0)
        def _():
            kv, vv = k_ref[...], v_ref[...]
            for j in range(2):
                kdup = _dup_lanes(kv, j)
                kd[j, 0:128, :] = jnp.zeros((128, 128), BF16)
                kd[j, 128:, :] = kdup.astype(BF16)
                kdt[j, :, 0:128] = jnp.zeros((128, 128), BF16)
                kdt[j, :, 128:] = kdup.T.astype(BF16)
                vd[j, 0:128, :] = jnp.zeros((128, 128), BF16)
                vd[j, 128:, :] = _dup_lanes(vv, j).astype(BF16)
            _fill_bias_t(bias_ref)
            dkacc[...] = jnp.zeros_like(dkacc)
            dvacc[...] = jnp.zeros_like(dvacc)
            dsink_ref[...] = jnp.zeros_like(dsink_ref)

        row_lo = _row_lo()
        ones = jnp.ones((8, 128), BF16)

        def block(b, carry):
            r0 = pl.multiple_of(b * 128, 128)
            gb = i * nb + b
            keys = pl.ds(pl.multiple_of(gb * 128, 128), 256)
            rows = pl.ds(r0, 128)
            qt = q_ref[rows, :] * scale
            dob = do_ref[rows, :]
            prod = dob * o_ref[rows, :]
            ct, st = ct_ref[:, rows], st_ref[:, rows]
            bias = bias_ref[jnp.where(gb == 0, 1, 0)]
            bias4 = jnp.concatenate([bias] * 4, axis=1)
            cols = []
            for j in range(2):
                q4 = _split_heads(qt, j).astype(BF16)
                kw = kd[j, keys, :]
                p, e_sink, denom = _softmax_t(q4, kw, bias4, _sink_row(sink_ref, pair, j))
                inv = 1.0 / denom
                prob = p * inv
                do4 = _split_heads(dob, j).astype(BF16)
                delta = _dot_nt(ones, _split_heads(prod, j).astype(BF16))[0:1, :]
                dp = _dot_nt(vd[j, keys, :], do4)
                ds = (prob * (dp - delta)).astype(BF16)
                dkacc[j, keys, :] = dkacc[j, keys, :] + _dot(ds, q4)
                dvacc[j, keys, :] = dvacc[j, keys, :] + _dot(prob.astype(BF16), do4)
                dqt = _dot(kdt[j, :, keys], ds) * scale
                dsink = -(e_sink * inv) * delta
                for h in range(4):
                    row = 4 * j + h
                    total = jnp.sum(dsink[:, 128 * h : 128 * (h + 1)], axis=1, keepdims=True)
                    dsink_ref[0, row : row + 1, :] = dsink_ref[0, row : row + 1, :] + jnp.broadcast_to(total, (1, 128))
                for gi in range(2):
                    z = jnp.where(row_lo, dqt[:, 256 * gi : 256 * gi + 128], dqt[:, 256 * gi + 128 : 256 * gi + 256])
                    cols.append(unrope_rows(z, ct, st).T)
            dq_ref[rows, :] = jnp.concatenate(cols, axis=1).astype(BF16)
            return carry

        lax.fori_loop(0, nb, block, 0)

        @pl.when(i == n_tiles - 1)
        def _():
            lo = _lane_lo()
            dk = jnp.where(lo, _fold_lanes(dkacc[0, 128:, :]), _fold_lanes(dkacc[1, 128:, :]))
            dk_ref[...] = _rope_transposed(dk, ck[...], s1k[...], s2k[...]).astype(BF16)
            dv_ref[...] = jnp.where(lo, _fold_lanes(dvacc[0, 128:, :]), _fold_lanes(dvacc[1, 128:, :])).astype(BF16)

    kv_out = pl.BlockSpec((t, 128), lambda p, i: (0, p))
    tab_t = pl.BlockSpec((half, tq), lambda p, i: (0, i))
    tab_k = pl.BlockSpec((t, 128), lambda p, i: (0, 0))
    return pl.pallas_call(
        body,
        name="attn_bwd",
        grid=(2, n_tiles),
        in_specs=[sp["q"], sp["k"], sp["v"], sp["act"], sp["act"], tab_t, tab_t, tab_k, tab_k, tab_k, sp["sink"]],
        out_specs=[sp["act"], kv_out, kv_out, pl.BlockSpec((1, 8, 128), lambda p, i: (p, 0, 0))],
        out_shape=[
            jax.ShapeDtypeStruct((t, D), BF16),
            jax.ShapeDtypeStruct((t, 256), BF16),
            jax.ShapeDtypeStruct((t, 256), BF16),
            jax.ShapeDtypeStruct((2, 8, 128), F32),
        ],
        scratch_shapes=[pltpu.VMEM((2, t + 128, 128), BF16), pltpu.VMEM((2, 128, t + 128), BF16), pltpu.VMEM((2, t + 128, 128), BF16),
                        pltpu.VMEM((2, 256, 128), F32), pltpu.VMEM((2, t + 128, 128), F32), pltpu.VMEM((2, t + 128, 128), F32)],
        compiler_params=_params(2),
    )(proj, proj, proj, o, d_o, cos_t, sin_t, cos, s1, s2, sinks)


HALF = 512


def _softplus(y):
    u = jnp.exp(-jnp.abs(y))
    w = 1.0 + u
    log1p = jnp.where(w == 1.0, u, jnp.log(w) * (u / jnp.where(w == 1.0, 1.0, w - 1.0)))
    return jnp.maximum(y, 0.0) + log1p


def _gate_block(w_ref, blk):
    return w_ref[:, :, 256 * blk : 256 * (blk + 1)].reshape(256, 256)


def _gate_matmul(act_bf, w_ref):
    return jnp.concatenate([_dot(act_bf[:, 256 * blk : 256 * (blk + 1)], _gate_block(w_ref, blk)) for blk in range(2)], axis=1)


def _gate_matmul_t(grad_bf, w_ref):
    return jnp.concatenate([_dot_nt(grad_bf[:, 256 * blk : 256 * (blk + 1)], _gate_block(w_ref, blk)) for blk in range(2)], axis=1)


def _conv_and_gates(xe_ref, tm, cw_ref, cb_ref, wa_ref, wx_ref, ba_ref, bx_ref, lam_ref, keep):
    xc = cb_ref[...] + sum(cw_ref[k : k + 1, :] * xe_ref[pl.ds(5 + k, tm), :] for k in range(CONV_WIDTH))
    xc_bf = xc.astype(BF16)
    r = _sigmoid(_gate_matmul(xc_bf, wa_ref) + ba_ref[...])
    ig = _sigmoid(_gate_matmul(xc_bf, wx_ref) + bx_ref[...])
    sp = _softplus(-lam_ref[...])
    log_a = -LRU_C * r * sp
    tanh = jnp.tanh(-log_a)
    mult = jnp.sqrt(2.0 * tanh / (1.0 + tanh))
    a = jnp.where(keep, jnp.exp(log_a), 0.0)
    mult = jnp.where(keep, mult, 1.0)
    return xc, xc_bf, r, ig, sp, a, mult


def _scan_rows(a_ref, b_ref, h_ref, carry_ref, tm, reverse):
    width = a_ref.shape[1]
    row = lax.broadcasted_iota(jnp.int32, (8, width), 0)

    def group(g, carry):
        gg = tm // 8 - 1 - g if reverse else g
        r0 = pl.multiple_of(gg * 8, 8)
        a = a_ref[pl.ds(r0, 8), :]
        b = b_ref[pl.ds(r0, 8), :]
        for s in (1, 2, 4):
            shift = 8 - s if reverse else s
            a_sh = pltpu.roll(a, shift, 0)
            b_sh = pltpu.roll(b, shift, 0)
            ok = (row < 8 - s) if reverse else (row >= s)
            b = jnp.where(ok, a * b_sh + b, b)
            a = jnp.where(ok, a * a_sh, a)
        h = a * carry + b
        h_ref[pl.ds(r0, 8), :] = h
        last = h[0:1, :] if reverse else h[7:8, :]
        return jnp.broadcast_to(last, (8, width))

    carry_ref[...] = lax.fori_loop(0, tm // 8, group, carry_ref[...])


def _rnn_specs(t, tm, order):
    n = t // tm
    return dict(
        tile=pl.BlockSpec((tm, HALF), lambda ch, i: (order(i, n), ch)),
        before=lambda col0: pl.BlockSpec((8, HALF), lambda ch, i: (jnp.maximum(order(i, n) * (tm // 8) - 1, 0), col0 + ch)),
        proj_tile=lambda col0: pl.BlockSpec((tm, HALF), lambda ch, i: (order(i, n), col0 + ch)),
        keep=pl.BlockSpec((tm, 1), lambda ch, i: (order(i, n), 0)),
        gates=lambda which: pl.BlockSpec((N_DEV, GATE_SHARD, HALF), lambda ch, i: (0, PACK_GATE // GATE_SHARD + which, ch)),
        gate_grads=pl.BlockSpec((2, 2, 256, 256), lambda ch, i: (0, ch, 0, 0)),
        vec=pl.BlockSpec((1, HALF), lambda ch, i: (0, ch)),
        conv=pl.BlockSpec((CONV_WIDTH, HALF), lambda ch, i: (0, ch)),
    )


def _rnn_fwd(proj, keep, w_all, rg_ba, rg_bx, rg_lambda, conv_w, conv_b):
    t = proj.shape[0]
    tm = min(512, t)
    sp = _rnn_specs(t, tm, lambda i, n: i)

    def body(xr_ref, xb_ref, keep_ref, wa_ref, wx_ref, ba_ref, bx_ref, lam_ref, cw_ref, cb_ref, hh_ref, xe, a_buf, b_buf, carry):
        i = pl.program_id(1)

        @pl.when(i == 0)
        def _():
            carry[...] = jnp.zeros_like(carry)

        xe[0:8, :] = jnp.where(i > 0, xb_ref[...], 0.0)
        xe[8:, :] = xr_ref[...]
        keep_rows = keep_ref[...] != 0.0
        xc, _, _, ig, _, a, mult = _conv_and_gates(xe, tm, cw_ref, cb_ref, wa_ref, wx_ref, ba_ref, bx_ref, lam_ref, keep_rows)
        a_buf[...] = a
        b_buf[...] = mult * (ig * xc)
        _scan_rows(a_buf, b_buf, hh_ref, carry, tm, reverse=False)

    return pl.pallas_call(
        body,
        name="rnn_fwd",
        grid=(2, t // tm),
        in_specs=[sp["proj_tile"](SEG_XR // HALF), sp["before"](SEG_XR // HALF), sp["keep"], sp["gates"](0), sp["gates"](1),
                  sp["vec"], sp["vec"], sp["vec"], sp["conv"], sp["vec"]],
        out_specs=sp["tile"],
        out_shape=jax.ShapeDtypeStruct((t, D), F32),
        scratch_shapes=[pltpu.VMEM((tm + 8, HALF), F32), pltpu.VMEM((tm, HALF), F32), pltpu.VMEM((tm, HALF), F32), pltpu.VMEM((8, HALF), F32)],
        compiler_params=_params(2),
    )(proj, proj, keep, w_all, w_all, rg_ba, rg_bx, rg_lambda, conv_w, conv_b)


def _rnn_bwd(proj, hh, d_hh, keep, w_all, rg_ba, rg_bx, rg_lambda, conv_w, conv_b):
    t = proj.shape[0]
    tm = min(512, t)
    n_tiles = t // tm
    sp = _rnn_specs(t, tm, lambda i, n: n - 1 - i)

    def body(xr_ref, xb_ref, hh_ref, hb_ref, dhh_ref, keep_ref, wa_ref, wx_ref, ba_ref, bx_ref, lam_ref, cw_ref, cb_ref,
             dxr_ref, gw_ref, gvec_ref, xe, he, ae, c_buf, g_buf, dxe, carry, a_next):
        i = pl.program_id(1)
        first_tile = i == n_tiles - 1

        @pl.when(i == 0)
        def _():
            carry[...] = jnp.zeros_like(carry)
            a_next[...] = jnp.zeros_like(a_next)
            dxe[tm:, :] = jnp.zeros((8, HALF), F32)
            gw_ref[...] = jnp.zeros_like(gw_ref)
            gvec_ref[...] = jnp.zeros_like(gvec_ref)

        xe[0:8, :] = jnp.where(first_tile, 0.0, xb_ref[...])
        xe[8:, :] = xr_ref[...]
        he[0:8, :] = jnp.where(first_tile, 0.0, hb_ref[...])
        he[8:, :] = hh_ref[...]
        keep_rows = keep_ref[...] != 0.0
        xc, xc_bf, r, ig, sp_, a, mult = _conv_and_gates(xe, tm, cw_ref, cb_ref, wa_ref, wx_ref, ba_ref, bx_ref, lam_ref, keep_rows)

        ae[0:tm, :] = a
        ae[tm:, :] = a_next[...]
        a_next[...] = jnp.broadcast_to(a[0:1, :], (8, HALF))
        c_buf[...] = ae[pl.ds(1, tm), :]
        g_buf[...] = dhh_ref[...]
        _scan_rows(c_buf, g_buf, g_buf, carry, tm, reverse=True)
        g = g_buf[...]

        h_prev = he[pl.ds(7, tm), :]
        bb = ig * xc
        d_ig = g * mult * xc
        d_xc = g * mult * ig
        d_log_a = jnp.where(keep_rows, g * h_prev * a - (g * bb) * (a * a) / mult, 0.0)
        d_r = d_log_a * (-LRU_C * sp_)
        d_sp = jnp.sum(d_log_a * (-LRU_C * r), axis=0, keepdims=True)
        d_zr = d_r * r * (1.0 - r)
        d_zi = d_ig * ig * (1.0 - ig)
        d_zr_bf = d_zr.astype(BF16)
        d_zi_bf = d_zi.astype(BF16)
        d_xc = d_xc + _gate_matmul_t(d_zr_bf, wa_ref) + _gate_matmul_t(d_zi_bf, wx_ref)
        for blk in range(2):
            cols = slice(256 * blk, 256 * (blk + 1))
            gw_ref[0, blk] = gw_ref[0, blk] + _dot_tn(xc_bf[:, cols], d_zr_bf[:, cols])
            gw_ref[1, blk] = gw_ref[1, blk] + _dot_tn(xc_bf[:, cols], d_zi_bf[:, cols])

        dxe[0:tm, :] = d_xc
        dxr = sum(cw_ref[k : k + 1, :] * dxe[pl.ds(3 - k, tm), :] for k in range(CONV_WIDTH))
        dxr_ref[...] = dxr.astype(BF16)
        dxe[tm:, :] = d_xc[0:8, :]

        lam = lam_ref[...]
        sums = [
            jnp.sum(d_zr, axis=0, keepdims=True),
            jnp.sum(d_zi, axis=0, keepdims=True),
            d_sp * (-_sigmoid(-lam)),
            jnp.sum(d_xc, axis=0, keepdims=True),
        ] + [jnp.sum(d_xc * xe[pl.ds(5 + k, tm), :], axis=0, keepdims=True) for k in range(CONV_WIDTH)]
        gvec_ref[...] = gvec_ref[...] + jnp.concatenate(sums, axis=0)

    return pl.pallas_call(
        body,
        name="rnn_bwd",
        grid=(2, n_tiles),
        in_specs=[
            sp["proj_tile"](SEG_XR // HALF), sp["before"](SEG_XR // HALF), sp["tile"], sp["before"](0), sp["tile"], sp["keep"],
            sp["gates"](0), sp["gates"](1), sp["vec"], sp["vec"], sp["vec"], sp["conv"], sp["vec"],
        ],
        out_specs=[sp["tile"], sp["gate_grads"], pl.BlockSpec((8, HALF), lambda ch, i: (0, ch))],
        out_shape=[
            jax.ShapeDtypeStruct((t, D), BF16),
            jax.ShapeDtypeStruct((2, 4, 256, 256), F32),
            jax.ShapeDtypeStruct((8, D), F32),
        ],
        scratch_shapes=[
            pltpu.VMEM((tm + 8, HALF), F32),
            pltpu.VMEM((tm + 8, HALF), F32),
            pltpu.VMEM((tm + 8, HALF), F32),
            pltpu.VMEM((tm, HALF), F32),
            pltpu.VMEM((tm, HALF), F32),
            pltpu.VMEM((tm + 8, HALF), F32),
            pltpu.VMEM((8, HALF), F32),
            pltpu.VMEM((8, HALF), F32),
        ],
        compiler_params=_params(2),
    )(proj, proj, hh, hh, d_hh, keep, w_all, w_all, rg_ba, rg_bx, rg_lambda, conv_w, conv_b)


def _mid(x, target, o, hh, proj, w_all, gate, final_g):
    t = x.shape[0]
    tm = min(256, t)
    n_tiles = t // tm
    full = pl.BlockSpec((tm, D), lambda i: (i, 0))
    half = lambda col: pl.BlockSpec((tm, 512), lambda i: (i, col))
    vec = pl.BlockSpec((1, D), lambda i: (0, 0))
    any_spec = pl.BlockSpec(memory_space=pl.ANY)

    def body(x_ref, tg_ref, o_ref, hh_ref, ga0, ga1, gr0, gr1, ma0, ma1, mr0, mr1, w_hbm, gate_ref, fg_ref,
             dres_ref, do_ref, dhh_ref, dga_ref, dgr_ref, dma_ref, dmr_ref, gw_hbm, vec_ref, w, gw, sem, w_sems):
        i = pl.program_id(0)

        @pl.when(i == 0)
        def _():
            loads = [
                pltpu.make_async_copy(
                    w_hbm.at[s, pl.ds(ROW_SHARD * k, ROW_SHARD), :], w.at[k, pl.ds(ROW_SHARD * s, ROW_SHARD), :], w_sems.at[k * N_DEV + s]
                )
                for k in range(3)
                for s in range(N_DEV)
            ]
            for cp in loads:
                cp.start()
            for cp in loads:
                cp.wait()
            gw[...] = jnp.zeros_like(gw)
            vec_ref[...] = jnp.zeros_like(vec_ref)

        cat = lambda a, b: jnp.concatenate([a[...], b[...]], axis=1)
        ga, gr, ma, mr = cat(ga0, ga1), cat(gr0, gr1), cat(ma0, ma1), cat(mr0, mr1)
        ov, hv, gate_v, fg = o_ref[...], hh_ref[...], gate_ref[...], fg_ref[...]
        sg_a = _sigmoid(ga)
        sg_r = _sigmoid(gr)
        silu_a = ga * sg_a
        silu_r = gr * sg_r
        ya = (ov * silu_a).astype(BF16)
        yr = (hv * silu_r).astype(BF16)
        pa = _dot(ya, w[0])
        pr = _dot(yr, w[1])
        sa = _sigmoid(ma)
        sr = _sigmoid(mr)
        merged = (sa * pa + sr * pr).astype(BF16)
        mo = _dot(merged, w[2])
        out = x_ref[...] + gate_v * mo
        r2 = lax.rsqrt(jnp.mean(out * out, axis=-1, keepdims=True) + NORM_EPS)
        outn = out * r2
        err = outn * fg - tg_ref[...]
        loss = 0.5 * jnp.sum(jnp.mean(err * err, axis=-1, keepdims=True), axis=0, keepdims=True)
        dy = err * (1.0 / D)
        g_final = jnp.sum(dy * outn, axis=0, keepdims=True)
        doutn = dy * fg
        dout = r2 * (doutn - outn * jnp.mean(doutn * outn, axis=-1, keepdims=True))
        dres_ref[...] = dout
        d_gate = jnp.sum(dout * mo, axis=0, keepdims=True)
        dmo = (dout * gate_v).astype(BF16)
        dmerged = _dot_nt(dmo, w[2])
        gw[2] = gw[2] + _dot_tn(merged, dmo)
        dpa = (dmerged * sa).astype(BF16)
        dpr = (dmerged * sr).astype(BF16)
        dma_ref[...] = (dmerged * pa * (sa * (1.0 - sa))).astype(BF16)
        dmr_ref[...] = (dmerged * pr * (sr * (1.0 - sr))).astype(BF16)
        dya = _dot_nt(dpa, w[0])
        dyr = _dot_nt(dpr, w[1])
        gw[0] = gw[0] + _dot_tn(ya, dpa)
        gw[1] = gw[1] + _dot_tn(yr, dpr)
        do_ref[...] = dya * silu_a
        dhh_ref[...] = dyr * silu_r
        dga_ref[...] = (dya * ov * (sg_a * (1.0 + ga * (1.0 - sg_a)))).astype(BF16)
        dgr_ref[...] = (dyr * hv * (sg_r * (1.0 + gr * (1.0 - sg_r)))).astype(BF16)
        zeros = jnp.zeros((5, D), F32)
        vec_ref[...] = vec_ref[...] + jnp.concatenate([g_final, d_gate, jnp.broadcast_to(loss, (1, D)), zeros], axis=0)

        @pl.when(i == n_tiles - 1)
        def _():
            cp = pltpu.make_async_copy(gw, gw_hbm, sem)
            cp.start()
            cp.wait()

    bf_tile = jax.ShapeDtypeStruct((t, D), BF16)
    f_tile = jax.ShapeDtypeStruct((t, D), F32)
    return pl.pallas_call(
        body,
        name="mid",
        grid=(n_tiles,),
        in_specs=[full, full, full, full,
                  half(SEG_GA // 512), half(SEG_GA // 512 + 1), half(SEG_GR // 512), half(SEG_GR // 512 + 1),
                  half(SEG_MA // 512), half(SEG_MA // 512 + 1), half(SEG_MR // 512), half(SEG_MR // 512 + 1),
                  any_spec, vec, vec],
        out_specs=[full, full, full, full, full, full, full, any_spec, pl.BlockSpec((8, D), lambda i: (0, 0))],
        out_shape=[f_tile, f_tile, f_tile, bf_tile, bf_tile, bf_tile, bf_tile,
                   jax.ShapeDtypeStruct((3, D, D), F32), jax.ShapeDtypeStruct((8, D), F32)],
        scratch_shapes=[pltpu.VMEM((3, D, D), BF16), pltpu.VMEM((3, D, D), F32), pltpu.SemaphoreType.DMA, pltpu.SemaphoreType.DMA((3 * N_DEV,))],
        compiler_params=_params(1),
    )(x, target, o, hh, proj, proj, proj, proj, proj, proj, proj, proj, w_all, gate, final_g)


_SEGMENTS = ((SEG_Q, 1024), (SEG_K, 256), (SEG_V, 256), (SEG_GA, 1024), (SEG_XR, 1024), (SEG_GR, 1024), (SEG_MA, 1024), (SEG_MR, 1024))


def _seg_specs(tm):
    return [pl.BlockSpec((tm, width), lambda i: (i, 0)) for _, width in _SEGMENTS]


def _bwd_dh(dsegs, w_all, x, dres, mod, norm_g, sum_w_bf):
    t = x.shape[0]
    tm = min(256, t)
    n_tiles = t // tm
    n_seg = len(_SEGMENTS)
    full = pl.BlockSpec((tm, D), lambda i: (i, 0))
    views = [lambda ref, peer: ref.at[_chip_of(peer)]]

    def body(*refs):
        segs = refs[:n_seg]
        w_hbm, x_ref, dres_ref, mod_ref, ng_ref, sum_hbm, gx_ref, vec_ref, land_hbm, w, sems, send_sems, recv_sems = refs[n_seg:]
        i = pl.program_id(0)

        @pl.when(i == 0)
        def _():
            for cp in _chip_copies([sum_hbm], [land_hbm], views, send_sems, recv_sems):
                cp.start()
            loads = [
                pltpu.make_async_copy(w_hbm.at[s], w.at[pl.ds(W_IN_SHARD * s, W_IN_SHARD), :], sems.at[s])
                for s in range(N_DEV)
            ]
            for cp in loads:
                cp.start()
            for cp in loads:
                cp.wait()
            vec_ref[...] = jnp.zeros_like(vec_ref)

        dh = sum(_dot(seg[...], w[start : start + width, :]) for seg, (start, width) in zip(segs, _SEGMENTS))
        xv = x_ref[...]
        r1 = lax.rsqrt(jnp.mean(xv * xv, axis=-1, keepdims=True) + NORM_EPS)
        xn = xv * r1
        ng = ng_ref[...]
        scale1 = 1.0 + mod_ref[:, D : 2 * D]
        d_shift = jnp.sum(dh, axis=0, keepdims=True)
        d_scale = jnp.sum(dh * (xn * ng), axis=0, keepdims=True)
        g_norm = jnp.sum(dh * xn * scale1, axis=0, keepdims=True)
        dxn = dh * (ng * scale1)
        dx = r1 * (dxn - xn * jnp.mean(dxn * xn, axis=-1, keepdims=True))
        gx_ref[...] = dres_ref[...] + dx
        vec_ref[...] = vec_ref[...] + jnp.concatenate([d_shift, d_scale, g_norm, jnp.zeros((5, D), F32)], axis=0)

        @pl.when(i == n_tiles - 1)
        def _():
            for cp in _chip_copies([sum_hbm], [land_hbm], views, send_sems, recv_sems):
                cp.wait_recv()
                cp.wait_send()

    any_spec = pl.BlockSpec(memory_space=pl.ANY)
    n_copies = len(CHIP_FLIPS)
    return pl.pallas_call(
        body,
        name="bwd_dh",
        grid=(n_tiles,),
        in_specs=_seg_specs(tm) + [any_spec, full, full, pl.BlockSpec((1, 3 * D), lambda i: (0, 0)), pl.BlockSpec((1, D), lambda i: (0, 0)),
                                   any_spec],
        out_specs=[full, pl.BlockSpec((8, D), lambda i: (0, 0)), any_spec],
        out_shape=[jax.ShapeDtypeStruct((t, D), F32), jax.ShapeDtypeStruct((8, D), F32),
                   jax.ShapeDtypeStruct((n_copies, W_IN_SHARD, D), BF16)],
        scratch_shapes=[pltpu.VMEM((IN_W, D), BF16), pltpu.SemaphoreType.DMA((N_DEV,)),
                        pltpu.SemaphoreType.DMA((n_copies,)), pltpu.SemaphoreType.DMA((n_copies,))],
        compiler_params=_params(1),
    )(*dsegs, w_all, x, dres, mod, norm_g, sum_w_bf)


def _bwd_gw(dsegs, h, sum_3_bf, sum_g_bf):
    t = h.shape[0]
    tm = min(512, t)
    n_tiles = t // tm
    n_seg = len(_SEGMENTS)
    views = [lambda ref, peer: ref.at[:, _chip_of(peer)], lambda ref, peer: ref.at[:, :, _chip_of(peer)]]

    def body(*refs):
        segs = refs[:n_seg]
        h_ref, s3_hbm, sg_hbm, out_hbm, l3_hbm, lg_hbm, acc, sem, send_sems, recv_sems = refs[n_seg:]
        i = pl.program_id(0)

        @pl.when(i == 0)
        def _():
            for cp in _chip_copies([s3_hbm, sg_hbm], [l3_hbm, lg_hbm], views, send_sems, recv_sems):
                cp.start()
            acc[...] = jnp.zeros_like(acc)

        hv = h_ref[...]
        for seg, (start, width) in zip(segs, _SEGMENTS):
            acc[start : start + width, :] = acc[start : start + width, :] + _dot_tn(seg[...], hv)

        @pl.when(i == n_tiles - 1)
        def _():
            cp = pltpu.make_async_copy(acc, out_hbm, sem)
            cp.start()
            cp.wait()
            for cp in _chip_copies([s3_hbm, sg_hbm], [l3_hbm, lg_hbm], views, send_sems, recv_sems):
                cp.wait_recv()
                cp.wait_send()

    any_spec = pl.BlockSpec(memory_space=pl.ANY)
    n_copies = 2 * len(CHIP_FLIPS)
    return pl.pallas_call(
        body,
        name="bwd_gw",
        grid=(n_tiles,),
        in_specs=_seg_specs(tm) + [pl.BlockSpec((tm, D), lambda i: (i, 0)), any_spec, any_spec],
        out_specs=[any_spec, any_spec, any_spec],
        out_shape=[jax.ShapeDtypeStruct((IN_W, D), F32),
                   jax.ShapeDtypeStruct((len(CHIP_FLIPS), 3, ROW_SHARD, D), BF16),
                   jax.ShapeDtypeStruct((len(CHIP_FLIPS), 2, 4, GATE_SHARD, 256), BF16)],
        scratch_shapes=[pltpu.VMEM((IN_W, D), F32), pltpu.SemaphoreType.DMA,
                        pltpu.SemaphoreType.DMA((n_copies,)), pltpu.SemaphoreType.DMA((n_copies,))],
        compiler_params=_params(1),
    )(*dsegs, h, sum_3_bf, sum_g_bf)


def _adamw(w, g, m, v):
    m = ADAM_B1 * m + (1.0 - ADAM_B1) * g
    v = ADAM_B2 * v + (1.0 - ADAM_B2) * (g * g)
    m_hat = m / (1.0 - ADAM_B1**ADAM_STEP)
    v_hat = v / (1.0 - ADAM_B2**ADAM_STEP)
    delta = -ADAM_LR * (m_hat / (jnp.sqrt(v_hat) + ADAM_EPS) + ADAM_WD * w)
    return delta, m, v


def _row_block(rows):
    return 128 if rows % 128 == 0 else rows


def _add_slots(own, l_ref):
    total = own
    for s in range(l_ref.shape[0]):
        total = total + l_ref[s].astype(F32)
    return total


def _sum_slots(name, own, land):
    n_slots, rows, cols = land.shape
    rb = 208 if rows == W_IN_SHARD else _row_block(rows)

    def body(o_ref, l_ref, g_ref):
        g_ref[...] = _add_slots(o_ref[...], l_ref)

    spec = pl.BlockSpec((rb, cols), lambda i: (i, 0))
    return pl.pallas_call(
        body,
        name=name,
        grid=(rows // rb,),
        in_specs=[spec, pl.BlockSpec((n_slots, rb, cols), lambda i: (0, i, 0))],
        out_specs=spec,
        out_shape=jax.ShapeDtypeStruct((rows, cols), F32),
        compiler_params=_params(1),
    )(own, land)


def _adam_update(name, g, w, m, v):
    rows, cols = w.shape
    rb = _row_block(rows)
    spec = pl.BlockSpec((rb, cols), lambda i: (i, 0))

    def body(g_ref, w_ref, m_ref, v_ref, d_ref, m2_ref, v2_ref):
        d_ref[...], m2_ref[...], v2_ref[...] = _adamw(w_ref[...], g_ref[...], m_ref[...], v_ref[...])

    shape = jax.ShapeDtypeStruct((rows, cols), F32)
    return pl.pallas_call(
        body, name=name, grid=(rows // rb,), in_specs=[spec] * 4, out_specs=[spec] * 3, out_shape=[shape] * 3,
        compiler_params=_params(1),
    )(g, w, m, v)


def _shard_update(own_3, land_3, own_g, land_g, weights, m_in, v_in):
    n = len(weights)

    def body(*refs):
        o3_ref, l3_ref, og_ref, lg_ref = refs[:4]
        w_refs, m_refs, v_refs = (refs[4 + n * k : 4 + n * (k + 1)] for k in range(3))
        outs = refs[4 + 3 * n :]
        for k in range(n):
            if k < 3:
                g = o3_ref[k]
                for slot in range(3):
                    g = g + l3_ref[slot, k].astype(F32)
            else:
                g = og_ref[k - 3]
                for slot in range(3):
                    g = g + lg_ref[slot, k - 3].astype(F32)
            d, m2, v2 = _adamw(w_refs[k][0], g, m_refs[k][0], v_refs[k][0])
            for which, val in enumerate((g, d, m2, v2)):
                outs[which * n + k][0] = val

    shapes = [jax.ShapeDtypeStruct(w.shape, F32) for w in weights]
    res = pl.pallas_call(
        body, name="shard_update", out_shape=shapes * 4, compiler_params=pltpu.CompilerParams(vmem_limit_bytes=VMEM_LIMIT)
    )(own_3, land_3, own_g, land_g, *weights, *m_in, *v_in)
    return [res[n * k : n * (k + 1)] for k in range(4)]


VEC_LAYOUT = (("b_ada", 3 * D), ("norm_g", D), ("attn_sinks", 128), ("conv_b", D), ("rg_ba", D), ("rg_bx", D), ("rg_lambda", D),
              ("final_g", D), ("conv_w", CONV_WIDTH * D), ("loss", 128))
VEC_OFFSET = {name: sum(w for _, w in VEC_LAYOUT[:k]) for k, (name, _) in enumerate(VEC_LAYOUT)}
VEC_ROWS = 112
VEC_WIDTH = VEC_ROWS * 128
assert sum(w for _, w in VEC_LAYOUT) <= VEC_WIDTH
N_REPLICATED = 8


def _small_update(vec_all, c_all_t, me, rep_w, rep_m, rep_v, ada, conv):
    n_ada = ada[0].shape[1]
    n_conv = conv[0].shape[1]

    def body(*refs):
        me_ref, vec_ref, ct_ref = refs[:3]
        w_refs, m_refs, v_refs = (refs[3 + N_REPLICATED * k : 3 + N_REPLICATED * (k + 1)] for k in range(3))
        pos = 3 + 3 * N_REPLICATED
        ada_refs, conv_refs = refs[pos : pos + 3], refs[pos + 3 : pos + 6]
        outs = refs[pos + 6 :]
        loss_ref, rep_outs = outs[0], outs[1 : 1 + 4 * N_REPLICATED]
        ada_outs, conv_outs = outs[1 + 4 * N_REPLICATED : 5 + 4 * N_REPLICATED], outs[5 + 4 * N_REPLICATED : 9 + 4 * N_REPLICATED]
        conv_buf = outs[9 + 4 * N_REPLICATED]

        total = vec_ref[0:1, :]
        for slot in range(1, N_DEV):
            total = total + vec_ref[slot : slot + 1, :]
        loss_ref[...] = total[:, VEC_OFFSET["loss"] : VEC_OFFSET["loss"] + 1]
        for k, (name, _) in enumerate(VEC_LAYOUT[:N_REPLICATED]):
            width = w_refs[k].shape[1]
            g = total[:, VEC_OFFSET[name] : VEC_OFFSET[name] + width]
            d, m2, v2 = _adamw(w_refs[k][...], g, m_refs[k][...], v_refs[k][...])
            for which, val in enumerate((g, d, m2, v2)):
                rep_outs[which * N_REPLICATED + k][...] = val

        device = me_ref[0]
        for k in range(CONV_WIDTH):
            conv_buf[k : k + 1, :] = total[:, VEC_OFFSET["conv_w"] + D * k : VEC_OFFSET["conv_w"] + D * (k + 1)]
        g_conv = conv_buf[:, pl.ds(pl.multiple_of(device * n_conv, 128), n_conv)]
        for ref, val in zip(conv_outs, (g_conv,) + _adamw(conv_refs[0][...], g_conv, conv_refs[1][...], conv_refs[2][...])):
            ref[...] = val

        dmod = vec_ref[:, pl.ds(pl.multiple_of(device * n_ada, 128), n_ada)]
        g_ada = _dot(ct_ref[...], dmod)
        for ref, val in zip(ada_outs, (g_ada,) + _adamw(ada_refs[0][...], g_ada, ada_refs[1][...], ada_refs[2][...])):
            ref[...] = val

    vmem = pl.BlockSpec(memory_space=pltpu.VMEM)
    rep_shapes = [jax.ShapeDtypeStruct(w.shape, F32) for w in rep_w]
    out_shape = ([jax.ShapeDtypeStruct((1, 1), F32)] + rep_shapes * 4 + [jax.ShapeDtypeStruct(ada[0].shape, F32)] * 4
                 + [jax.ShapeDtypeStruct(conv[0].shape, F32)] * 4)
    n_in = 3 + 3 * N_REPLICATED + 6
    res = pl.pallas_call(
        body,
        name="small_update",
        in_specs=[pl.BlockSpec(memory_space=pltpu.SMEM)] + [vmem] * (n_in - 1),
        out_specs=[vmem] * len(out_shape),
        out_shape=out_shape,
        scratch_shapes=[pltpu.VMEM((CONV_WIDTH, D), F32)],
        compiler_params=pltpu.CompilerParams(vmem_limit_bytes=VMEM_LIMIT),
    )(me, vec_all, c_all_t, *rep_w, *rep_m, *rep_v, *ada, *conv)
    loss = res[0]
    reps = [res[1 + N_REPLICATED * k : 1 + N_REPLICATED * (k + 1)] for k in range(4)]
    return loss, reps, res[1 + 4 * N_REPLICATED : 5 + 4 * N_REPLICATED], res[5 + 4 * N_REPLICATED : 9 + 4 * N_REPLICATED]


def kernel(x, c, positions, w_ada, b_ada, norm_g, w_in, attn_sinks, conv_w, conv_b, rg_wa, rg_ba, rg_wx, rg_bx, rg_lambda, w_attn_proj, w_rnn_proj, w_out, final_g, loss_target, m_w_ada, m_b_ada, m_norm_g, m_w_in, m_attn_sinks, m_conv_w, m_conv_b, m_rg_wa, m_rg_ba, m_rg_wx, m_rg_bx, m_rg_lambda, m_w_attn_proj, m_w_rnn_proj, m_w_out, m_final_g, v_w_ada, v_b_ada, v_norm_g, v_w_in, v_attn_sinks, v_conv_w, v_conv_b, v_rg_wa, v_rg_ba, v_rg_wx, v_rg_bx, v_rg_lambda, v_w_attn_proj, v_w_rnn_proj, v_w_out, v_final_g):
    my_x, my_y, my_c = _position()
    me = _slot((my_x, my_y, my_c))
    x2, target = x[0], loss_target[0]
    pos = positions[0]

    gate_rows = lambda w: w[0].transpose(1, 0, 2).reshape(GATE_SHARD, D)
    w_pack = jnp.concatenate([w_attn_proj[0], w_rnn_proj[0], w_out[0], gate_rows(rg_wa), gate_rows(rg_wx)], axis=0).astype(BF16)
    conv_rows = jnp.pad(conv_w[0], ((0, 8 - CONV_WIDTH), (0, 0)))
    c_rows, conv_all, w_t_all = _all_gather("ag_fwd", [jnp.broadcast_to(c, (8, D)), conv_rows, w_in[0].T.astype(BF16)])
    c_all = c_rows[:, 0, :]
    conv_full = conv_all[:, :CONV_WIDTH, :].transpose(1, 0, 2).reshape(CONV_WIDTH, D)

    n_ada = w_ada.shape[2]
    b_cols = lax.dynamic_slice(b_ada, (0, me * n_ada), (1, n_ada))
    mod_cols = _ada_mod(c_all, w_ada[0], b_cols)
    (mod_all,) = _all_gather_direct("ag_mod", [mod_cols])
    mod = lax.dynamic_slice(mod_all, (0, me, 0), (N_DEV, 1, n_ada)).reshape(1, 3 * D)
    gate = mod[:, 2 * D :]

    tables = _rope_tables(pos)
    proj, h, w_all = _fwd_in(x2, mod, norm_g, w_t_all, tables[0], w_pack)
    o = _attn_fwd(proj, attn_sinks)
    keep = (pos != 0).astype(F32)[:, None]
    hh = _rnn_fwd(proj, keep, w_all, rg_ba, rg_bx, rg_lambda, conv_full, conv_b)

    dres, d_o, d_hh, d_ga, d_gr, d_ma, d_mr, gw3, mid_vec = _mid(x2, target, o, hh, proj, w_all, gate, final_g[None, :])
    d_q, d_k, d_v, d_sink = _attn_bwd(proj, o, d_o, tables, attn_sinks)
    d_xr, g_gates, rnn_vec = _rnn_bwd(proj, hh, d_hh, keep, w_all, rg_ba, rg_bx, rg_lambda, conv_full, conv_b)
    dsegs = (d_q, d_k, d_v, d_ga, d_xr, d_gr, d_ma, d_mr)

    index = jnp.stack([my_c, _chip_of((my_x, my_y))]).astype(jnp.int32)
    part_3 = gw3.reshape(3, 4, 2, ROW_SHARD, D)
    part_g = g_gates.reshape(2, 4, 4, 2, GATE_SHARD, 256)
    recv_3, recv_g = _exchange(
        "rs_pair_small",
        [part_3, part_g],
        [lambda ref, peer: ref.at[:, :, peer[2]], lambda ref, peer: ref.at[:, :, :, peer[2]]],
        [(3, 4, ROW_SHARD, D), (2, 4, 4, GATE_SHARD, 256)],
        flips=(1,),
    )
    sum_3_bf, own_3 = _pair_add(
        "pair_add_3", part_3, recv_3[0], index, (3, 4), 1,
        pl.BlockSpec((None, None, None, ROW_SHARD, D), lambda a, q, idx: (a, q, idx[0], 0, 0)),
        pl.BlockSpec((None, None, ROW_SHARD, D), lambda a, q, idx: (a, q, 0, 0)),
        pl.BlockSpec((None, ROW_SHARD, D), lambda a, q, idx: (a, 0, 0)), (3, ROW_SHARD, D),
    )
    sum_g_bf, own_g = _pair_add(
        "pair_add_g", part_g, recv_g[0], index, (4,), 0,
        pl.BlockSpec((2, 4, None, None, GATE_SHARD, 256), lambda q, idx: (0, 0, q, idx[0], 0, 0)),
        pl.BlockSpec((2, 4, None, GATE_SHARD, 256), lambda q, idx: (0, 0, q, 0, 0)),
        pl.BlockSpec((2, 4, GATE_SHARD, 256), lambda q, idx: (0, 0, 0, 0)), (2, 4, GATE_SHARD, 256),
    )
    g_w_t, land_3, land_g = _bwd_gw(dsegs, h, sum_3_bf, sum_g_bf)
    part_w = g_w_t.reshape(4, 2, W_IN_SHARD, D)
    (recv_w,) = _exchange("rs_pair_w", [part_w], [lambda ref, peer: ref.at[:, peer[2]]], [(4, W_IN_SHARD, D)], flips=(1,))
    rb = 208
    sum_w_bf, own_w = _pair_add(
        "pair_add_w", part_w, recv_w[0], index, (W_IN_SHARD // rb, 4), 1,
        pl.BlockSpec((None, None, rb, D), lambda r, q, idx: (q, idx[0], r, 0)),
        pl.BlockSpec((None, rb, D), lambda r, q, idx: (q, r, 0)),
        pl.BlockSpec((rb, D), lambda r, q, idx: (r, 0)), (W_IN_SHARD, D),
    )
    grad_x, dh_vec, land_w = _bwd_dh(dsegs, w_t_all, x2, dres, mod, norm_g, sum_w_bf)

    pieces = dict(
        b_ada=jnp.concatenate([dh_vec[0:1], dh_vec[1:2], mid_vec[1:2]], axis=1),
        norm_g=dh_vec[2:3],
        attn_sinks=jnp.pad(d_sink[:, :, 0].reshape(1, 16), ((0, 0), (0, 112))),
        conv_b=rnn_vec[3:4],
        rg_ba=rnn_vec[0:1],
        rg_bx=rnn_vec[1:2],
        rg_lambda=rnn_vec[2:3],
        final_g=mid_vec[0:1],
        conv_w=rnn_vec[4:8].reshape(1, CONV_WIDTH * D),
        loss=mid_vec[2:3, 0:128],
    )
    vec = jnp.concatenate([pieces[name] for name, _ in VEC_LAYOUT], axis=1)
    vec = jnp.pad(vec, ((0, 0), (0, VEC_WIDTH - vec.shape[1]))).reshape(VEC_ROWS, 128)
    (vec_all,) = _all_gather_direct("ag_vec", [vec])
    rep_w = [b_ada, norm_g, attn_sinks, conv_b, rg_ba, rg_bx, rg_lambda, final_g[None, :]]
    rep_m = [m_b_ada, m_norm_g, m_attn_sinks, m_conv_b, m_rg_ba, m_rg_bx, m_rg_lambda, m_final_g[None, :]]
    rep_v = [v_b_ada, v_norm_g, v_attn_sinks, v_conv_b, v_rg_ba, v_rg_bx, v_rg_lambda, v_final_g[None, :]]
    loss, reps, ada_res, conv_res = _small_update(
        vec_all.reshape(N_DEV, VEC_WIDTH), c_all.T, me.astype(jnp.int32).reshape(1), rep_w, rep_m, rep_v,
        (w_ada[0], m_w_ada[0], v_w_ada[0]), (conv_w[0], m_conv_w[0], v_conv_w[0]),
    )

    g_in = _sum_slots("sum_w_in", own_w, land_w).T
    d_in, m_in, v_in = _adam_update("adam_w_in", g_in, w_in[0], m_w_in[0], v_w_in[0])
    shard_res = _shard_update(
        own_3, land_3, own_g, land_g,
        (w_attn_proj, w_rnn_proj, w_out, rg_wa, rg_wx),
        (m_w_attn_proj, m_w_rnn_proj, m_w_out, m_rg_wa, m_rg_wx),
        (v_w_attn_proj, v_w_rnn_proj, v_w_out, v_rg_wa, v_rg_wx),
    )

    def by_weight(k):
        b, n, s, cb, ba, bx, lam, fg = reps[k]
        ap, rp, out, wa, wx = shard_res[k]
        w_in_k = (g_in, d_in, m_in, v_in)[k]
        return [ada_res[k][None], b, n, w_in_k[None], s, conv_res[k][None], cb, wa, ba, wx, bx, lam, ap, rp, out, fg[0]]

    return (loss[0, 0], grad_x[None], *by_weight(0), *by_weight(1), *by_weight(2), *by_weight(3))
```

```python
import functools
import math

import jax
import jax.numpy as jnp
import numpy as np
from jax import lax
from jax.experimental import pallas as pl
from jax.experimental.pallas import tpu as pltpu

F32 = jnp.float32
BF16 = jnp.bfloat16

D = 1024
HEAD_DIM = 64
ROT_DIM = 16
ROPE_THETA = 500000.0
WINDOW = 128
LRU_C = 8.0
CONV_WIDTH = 4
NORM_EPS = 1e-6
IN_W = 6656
SEG_Q, SEG_K, SEG_V, SEG_GA, SEG_XR, SEG_GR, SEG_MA, SEG_MR = 0, 1024, 1280, 1536, 2560, 3584, 4608, 5632
N_DEV = 8
W_IN_SHARD = IN_W // N_DEV
ROW_SHARD = D // N_DEV
GATE_SHARD = 256 // N_DEV

ADAM_LR = 0.001
ADAM_B1 = 0.9
ADAM_B2 = 0.999
ADAM_EPS = 1e-08
ADAM_WD = 0.01
ADAM_STEP = 10

VMEM_LIMIT = 60 * 1024 * 1024
MESH = pl.DeviceIdType.MESH
AXES = ("x", "y", "c")


def _dot(a, b):
    return jnp.dot(a, b, preferred_element_type=F32)


def _dot_nt(a, b):
    return lax.dot_general(a, b, (((1,), (1,)), ((), ())), preferred_element_type=F32)


def _dot_tn(a, b):
    return lax.dot_general(a, b, (((0,), (0,)), ((), ())), preferred_element_type=F32)


def _sigmoid(z):
    return 1.0 / (1.0 + jnp.exp(-z))


def _params(n_grid, vmem=VMEM_LIMIT):
    return pltpu.CompilerParams(dimension_semantics=("arbitrary",) * n_grid, vmem_limit_bytes=vmem)


def _position():
    return lax.axis_index("x"), lax.axis_index("y"), lax.axis_index("c")


def _flip(pos, k):
    x, y, c = pos
    return (1 - x if k & 4 else x, 1 - y if k & 2 else y, 1 - c if k & 1 else c)


def _slot(pos):
    return 4 * pos[0] + 2 * pos[1] + pos[2]


def _all_gather(name, arrs):
    n = len(arrs)

    def body(*refs):
        ins, outs = refs[:n], refs[n : 2 * n]
        send_sems, recv_sems, local_sems = refs[2 * n :]
        me = _position()
        sibling = _flip(me, 1)
        chips = [2, 4, 6]

        def copy(a, k, block, to, src=None):
            return pltpu.make_async_remote_copy(
                src_ref=outs[a].at[_slot(block)] if src is None else src,
                dst_ref=outs[a].at[_slot(block)],
                send_sem=send_sems.at[a * 7 + k],
                recv_sem=recv_sems.at[a * 7 + k],
                device_id=to,
                device_id_type=MESH,
            )

        mine = [pltpu.make_async_copy(ins[a], outs[a].at[_slot(me)], local_sems.at[a]) for a in range(n)]
        for cp in mine:
            cp.start()
        first = []
        for a in range(n):
            first.append(copy(a, 0, me, sibling, src=ins[a]))
            first += [copy(a, 1 + j, me, _flip(me, k), src=ins[a]) for j, k in enumerate(chips)]
        for cp in first:
            cp.start()
        passed = []
        for a in range(n):
            for j, k in enumerate(chips):
                copy(a, 1 + j, _flip(me, k), me).wait_recv()
                fwd = copy(a, 4 + j, _flip(me, k), sibling)
                fwd.start()
                passed.append(fwd)
        for a in range(n):
            copy(a, 0, sibling, me).wait_recv()
            for j, k in enumerate(chips):
                copy(a, 4 + j, _flip(sibling, k), me).wait_recv()
        for cp in first + passed:
            cp.wait_send()
        for cp in mine:
            cp.wait()

    any_spec = pl.BlockSpec(memory_space=pl.ANY)
    return pl.pallas_call(
        body,
        name=name,
        out_shape=[jax.ShapeDtypeStruct((N_DEV,) + a.shape, a.dtype) for a in arrs],
        in_specs=[any_spec] * n,
        out_specs=[any_spec] * n,
        scratch_shapes=[
            pltpu.SemaphoreType.DMA((7 * n,)),
            pltpu.SemaphoreType.DMA((7 * n,)),
            pltpu.SemaphoreType.DMA((n,)),
        ],
    )(*arrs)


def _all_gather_direct(name, arrs):
    n = len(arrs)

    def body(*refs):
        ins, outs = refs[:n], refs[n : 2 * n]
        send_sems, recv_sems, local_sems = refs[2 * n :]
        me = _position()
        mine = [pltpu.make_async_copy(ins[a], outs[a].at[_slot(me)], local_sems.at[a]) for a in range(n)]
        for cp in mine:
            cp.start()

        def copy(a, k, block):
            return pltpu.make_async_remote_copy(
                src_ref=ins[a],
                dst_ref=outs[a].at[_slot(block)],
                send_sem=send_sems.at[a * 7 + k - 1],
                recv_sem=recv_sems.at[a * 7 + k - 1],
                device_id=_flip(me, k),
                device_id_type=MESH,
            )

        sent = [copy(a, k, me) for a in range(n) for k in range(1, N_DEV)]
        for cp in sent:
            cp.start()
        for a in range(n):
            for k in range(1, N_DEV):
                copy(a, k, _flip(me, k)).wait_recv()
        for cp in sent:
            cp.wait_send()
        for cp in mine:
            cp.wait()

    any_spec = pl.BlockSpec(memory_space=pl.ANY)
    return pl.pallas_call(
        body,
        name=name,
        out_shape=[jax.ShapeDtypeStruct((N_DEV,) + a.shape, a.dtype) for a in arrs],
        in_specs=[any_spec] * n,
        out_specs=[any_spec] * n,
        scratch_shapes=[
            pltpu.SemaphoreType.DMA((7 * n,)),
            pltpu.SemaphoreType.DMA((7 * n,)),
            pltpu.SemaphoreType.DMA((n,)),
        ],
    )(*arrs)


CHIP_FLIPS = (2, 4, 6)


def _chip_of(pos):
    return 2 * pos[0] + pos[1]


def _chip_copies(srcs, lands, views, send_sems, recv_sems):
    me = _position()
    copies = []
    for a, (src, land) in enumerate(zip(srcs, lands)):
        for j, k in enumerate(CHIP_FLIPS):
            peer = _flip(me, k)
            copies.append(
                pltpu.make_async_remote_copy(
                    src_ref=views[a](src, peer),
                    dst_ref=land.at[j],
                    send_sem=send_sems.at[len(CHIP_FLIPS) * a + j],
                    recv_sem=recv_sems.at[len(CHIP_FLIPS) * a + j],
                    device_id=peer,
                    device_id_type=MESH,
                )
            )
    return copies


def _exchange(name, arrs, views, out_shapes, flips):
    n = len(arrs)
    nf = len(flips)

    def body(*refs):
        ins, outs = refs[:n], refs[n : 2 * n]
        send_sems, recv_sems = refs[2 * n :]
        me = _position()

        def copy(a, j):
            peer = _flip(me, flips[j])
            return pltpu.make_async_remote_copy(
                src_ref=views[a](ins[a], peer),
                dst_ref=outs[a].at[j],
                send_sem=send_sems.at[a * nf + j],
                recv_sem=recv_sems.at[a * nf + j],
                device_id=peer,
                device_id_type=MESH,
            )

        sent = [copy(a, j) for a in range(n) for j in range(nf)]
        for cp in sent:
            cp.start()
        for cp in sent:
            cp.wait_recv()
        for cp in sent:
            cp.wait_send()

    any_spec = pl.BlockSpec(memory_space=pl.ANY)
    return pl.pallas_call(
        body,
        name=name,
        out_shape=[jax.ShapeDtypeStruct((nf,) + tuple(sh), a.dtype) for sh, a in zip(out_shapes, arrs)],
        in_specs=[any_spec] * n,
        out_specs=[any_spec] * n,
        scratch_shapes=[pltpu.SemaphoreType.DMA((nf * n,)), pltpu.SemaphoreType.DMA((nf * n,))],
    )(*arrs)


def _pair_add(name, own, recv, index, grid, chip_axis, own_spec, recv_spec, mine_spec, mine_shape):
    def body(idx_ref, own_ref, recv_ref, sb_ref, mine_ref):
        total = own_ref[...] + recv_ref[...]
        sb_ref[...] = total.astype(BF16)

        @pl.when(pl.program_id(chip_axis) == idx_ref[1])
        def _():
            mine_ref[...] = total

    return pl.pallas_call(
        body,
        name=name,
        grid_spec=pltpu.PrefetchScalarGridSpec(
            num_scalar_prefetch=1, grid=grid, in_specs=[own_spec, recv_spec], out_specs=[recv_spec, mine_spec]
        ),
        out_shape=[jax.ShapeDtypeStruct(recv.shape, BF16), jax.ShapeDtypeStruct(mine_shape, F32)],
        compiler_params=_params(len(grid)),
    )(index, own, recv)


def _ada_mod(c_all, w_ada, b_cols):
    def body(c_ref, w_ref, b_ref, o_ref):
        o_ref[...] = _dot(c_ref[...].astype(BF16), w_ref[...].astype(BF16)) + b_ref[...]

    return pl.pallas_call(
        body,
        name="ada_mod",
        out_shape=jax.ShapeDtypeStruct((N_DEV, w_ada.shape[1]), F32),
        compiler_params=pltpu.CompilerParams(vmem_limit_bytes=VMEM_LIMIT),
    )(c_all, w_ada, b_cols)


PACK_GATE = 3 * ROW_SHARD
PACK_ROWS = PACK_GATE + 2 * GATE_SHARD


def _fwd_in(x, mod, norm_g, w_all, tables, pack):
    t = x.shape[0]
    tm = min(1024, t)
    tn = IN_W // 4
    n_i, n_j = t // tm, IN_W // tn
    w_tiles = w_all.reshape(n_j, tn, D)
    n_rope = (SEG_V - SEG_Q) // 128

    def gather_copies(pack_hbm, all_hbm, send_sems, recv_sems, arriving):
        me = _position()
        return [
            pltpu.make_async_remote_copy(
                src_ref=pack_hbm,
                dst_ref=all_hbm.at[_slot(_flip(me, k) if arriving else me)],
                send_sem=send_sems.at[k - 1],
                recv_sem=recv_sems.at[k - 1],
                device_id=_flip(me, k),
                device_id_type=MESH,
            )
            for k in range(1, N_DEV)
        ]

    def body(x_ref, mod_ref, ng_ref, w_ref, cos_ref, s1_ref, s2_ref, pack_hbm, proj_ref, h_ref, all_hbm,
             send_sems, recv_sems, local_sem):
        i = pl.program_id(0)
        j = pl.program_id(1)
        own = pltpu.make_async_copy(pack_hbm, all_hbm.at[_slot(_position())], local_sem)

        @pl.when((i == 0) & (j == 0))
        def _():
            own.start()
            for cp in gather_copies(pack_hbm, all_hbm, send_sems, recv_sems, arriving=False):
                cp.start()

        @pl.when(j == 0)
        def _():
            xv = x_ref[...]
            r = lax.rsqrt(jnp.mean(xv * xv, axis=-1, keepdims=True) + NORM_EPS)
            shift = mod_ref[:, 0:D]
            scale = mod_ref[:, D : 2 * D]
            h_ref[...] = (((xv * r) * ng_ref[...]) * (1.0 + scale) + shift).astype(BF16)

        proj_ref[...] = _dot_nt(h_ref[...], w_ref[...])

        @pl.when(j == 0)
        def _():
            cos, s1, s2 = cos_ref[...], s1_ref[...], s2_ref[...]
            for g in range(n_rope):
                cols = slice(128 * g, 128 * (g + 1))
                proj_ref[:, cols] = _rope(proj_ref[:, cols], cos, s1, s2)

        @pl.when((i == n_i - 1) & (j == n_j - 1))
        def _():
            for cp in gather_copies(pack_hbm, all_hbm, send_sems, recv_sems, arriving=True):
                cp.wait_recv()
            for cp in gather_copies(pack_hbm, all_hbm, send_sems, recv_sems, arriving=False):
                cp.wait_send()
            own.wait()

    tab = pl.BlockSpec((tm, 128), lambda i, j: (i, 0))
    any_spec = pl.BlockSpec(memory_space=pl.ANY)
    return pl.pallas_call(
        body,
        name="fwd_in",
        grid=(n_i, n_j),
        in_specs=[
            pl.BlockSpec((tm, D), lambda i, j: (i, 0)),
            pl.BlockSpec((1, 3 * D), lambda i, j: (0, 0)),
            pl.BlockSpec((1, D), lambda i, j: (0, 0)),
            pl.BlockSpec((None, tn, D), lambda i, j: (j, 0, 0)),
            tab, tab, tab, any_spec,
        ],
        out_specs=[pl.BlockSpec((tm, tn), lambda i, j: (i, j)), pl.BlockSpec((tm, D), lambda i, j: (i, 0)), any_spec],
        out_shape=[jax.ShapeDtypeStruct((t, IN_W), F32), jax.ShapeDtypeStruct((t, D), BF16),
                   jax.ShapeDtypeStruct((N_DEV,) + pack.shape, pack.dtype)],
        scratch_shapes=[pltpu.SemaphoreType.DMA((N_DEV - 1,)), pltpu.SemaphoreType.DMA((N_DEV - 1,)), pltpu.SemaphoreType.DMA],
        compiler_params=_params(2),
    )(x, mod, norm_g, w_tiles, *tables, pack)


def _rope_tables(positions):
    half = ROT_DIM // 2
    inv_freq = ROPE_THETA ** (-jnp.arange(0, ROT_DIM, 2, dtype=F32) / ROT_DIM)
    pos = positions.astype(F32)
    lane = np.arange(128) % HEAD_DIM
    freq_lane = jnp.where(lane < ROT_DIM, jnp.tile(inv_freq, 128 // half), 0.0)
    ang = pos[:, None] * freq_lane[None, :]
    sin = jnp.sin(ang)
    lane_tables = (jnp.cos(ang), jnp.where(lane < half, -sin, 0.0), jnp.where((lane >= half) & (lane < ROT_DIM), sin, 0.0))
    ang_t = inv_freq[:, None] * pos[None, :]
    return lane_tables, (jnp.cos(ang_t), jnp.sin(ang_t))


def _rope(tv, cos, s1, s2):
    cols = []
    for g in range(tv.shape[1] // 128):
        blk = tv[:, 128 * g : 128 * (g + 1)]
        cols.append(blk * cos + pltpu.roll(blk, 128 - 8, 1) * s1 + pltpu.roll(blk, 8, 1) * s2)
    return cols[0] if len(cols) == 1 else jnp.concatenate(cols, axis=1)


def _rope_transposed(gv, cos, s1, s2):
    cols = []
    for g in range(gv.shape[1] // 128):
        blk = gv[:, 128 * g : 128 * (g + 1)]
        cols.append(blk * cos + pltpu.roll(blk * s1, 8, 1) + pltpu.roll(blk * s2, 128 - 8, 1))
    return cols[0] if len(cols) == 1 else jnp.concatenate(cols, axis=1)


def _lane_lo():
    return lax.broadcasted_iota(jnp.int32, (1, 128), 1) < HEAD_DIM


def _row_lo():
    return lax.broadcasted_iota(jnp.int32, (128, 1), 0) < HEAD_DIM


def _split_heads(tile, j):
    lo = _lane_lo()
    parts = []
    for g in (2 * j, 2 * j + 1):
        blk = tile[:, 128 * g : 128 * (g + 1)]
        parts += [jnp.where(lo, blk, 0.0), jnp.where(lo, 0.0, blk)]
    return jnp.concatenate(parts, axis=0)


def _dup_lanes(kv, j):
    rolled = pltpu.roll(kv, HEAD_DIM, 1)
    lo = _lane_lo()
    return jnp.where(lo, kv, rolled) if j == 0 else jnp.where(lo, rolled, kv)


def _fold_lanes(acc):
    return acc + pltpu.roll(acc, HEAD_DIM, 1)


def _fill_bias_t(bias_ref):
    kj = lax.broadcasted_iota(jnp.int32, (256, 128), 0)
    qi = lax.broadcasted_iota(jnp.int32, (256, 128), 1)
    diff = qi + WINDOW - kj
    band = (diff >= 0) & (diff < WINDOW)
    bias_ref[0] = jnp.where(band, 0.0, -1e30)
    bias_ref[1] = jnp.where(band & (kj >= 128), 0.0, -1e30)


def _sink_row(sink_ref, pair, j):
    return jnp.concatenate([jnp.full((1, 128), sink_ref[0, 8 * pair + 4 * j + h], F32) for h in range(4)], axis=1)


def _softmax_t(q4, kw, bias4, sink):
    s = _dot_nt(kw, q4) + bias4
    m = jnp.maximum(jnp.max(s, axis=0, keepdims=True), sink)
    p = jnp.exp(s - m)
    e_sink = jnp.exp(sink - m)
    denom = jnp.sum(p, axis=0, keepdims=True) + e_sink
    return p, e_sink, denom


def _attn_specs(t, tq):
    return dict(
        q=pl.BlockSpec((tq, 512), lambda p, i: (i, p)),
        k=pl.BlockSpec((t, 128), lambda p, i: (0, SEG_K // 128 + p)),
        v=pl.BlockSpec((t, 128), lambda p, i: (0, SEG_V // 128 + p)),
        sink=pl.BlockSpec(memory_space=pltpu.SMEM),
        act=pl.BlockSpec((tq, 512), lambda p, i: (i, p)),
    )


def _attn_fwd(proj, sinks):
    t = proj.shape[0]
    tq = min(512, t)
    nb = tq // 128
    sp = _attn_specs(t, tq)

    def body(q_ref, k_ref, v_ref, sink_ref, o_ref, kd, vdt, bias_ref):
        pair = pl.program_id(0)
        i = pl.program_id(1)

        @pl.when(i == 0)
        def _():
            kv, vv = k_ref[...], v_ref[...]
            for j in range(2):
                kd[j, 0:128, :] = jnp.zeros((128, 128), BF16)
                kd[j, 128:, :] = _dup_lanes(kv, j).astype(BF16)
                vdt[j, :, 0:128] = jnp.zeros((128, 128), BF16)
                vdt[j, :, 128:] = _dup_lanes(vv, j).T.astype(BF16)
            _fill_bias_t(bias_ref)

        row_lo = _row_lo()

        def block(b, carry):
            r0 = pl.multiple_of(b * 128, 128)
            gb = i * nb + b
            keys = pl.ds(pl.multiple_of(gb * 128, 128), 256)
            rows = pl.ds(r0, 128)
            qt = q_ref[rows, :] * (1.0 / math.sqrt(HEAD_DIM))
            bias = bias_ref[jnp.where(gb == 0, 1, 0)]
            bias4 = jnp.concatenate([bias] * 4, axis=1)
            cols = []
            for j in range(2):
                q4 = _split_heads(qt, j).astype(BF16)
                p, _, denom = _softmax_t(q4, kd[j, keys, :], bias4, _sink_row(sink_ref, pair, j))
                ot = _dot(vdt[j, :, keys], p.astype(BF16)) * (1.0 / denom)
                for gi in range(2):
                    z = jnp.where(row_lo, ot[:, 256 * gi : 256 * gi + 128], ot[:, 256 * gi + 128 : 256 * gi + 256])
                    cols.append(z.T)
            o_ref[rows, :] = jnp.concatenate(cols, axis=1)
            return carry

        lax.fori_loop(0, nb, block, 0)

    return pl.pallas_call(
        body,
        name="attn_fwd",
        grid=(2, t // tq),
        in_specs=[sp["q"], sp["k"], sp["v"], sp["sink"]],
        out_specs=sp["act"],
        out_shape=jax.ShapeDtypeStruct((t, D), F32),
        scratch_shapes=[pltpu.VMEM((2, t + 128, 128), BF16), pltpu.VMEM((2, 128, t + 128), BF16), pltpu.VMEM((2, 256, 128), F32)],
        compiler_params=_params(2),
    )(proj, proj, proj, sinks)


def _attn_bwd(proj, o, d_o, tables, sinks):
    t = proj.shape[0]
    tq = min(512, t)
    nb = tq // 128
    n_tiles = t // tq
    (cos, s1, s2), (cos_t, sin_t) = tables
    sp = _attn_specs(t, tq)
    scale = 1.0 / math.sqrt(HEAD_DIM)
    half = ROT_DIM // 2

    def unrope_rows(z, c, s):
        out = []
        for base in (0, HEAD_DIM):
            g0, g1 = z[base : base + half], z[base + half : base + ROT_DIM]
            out += [g0 * c + g1 * s, g1 * c - g0 * s, z[base + ROT_DIM : base + HEAD_DIM]]
        return jnp.concatenate(out, axis=0)

    def body(q_ref, k_ref, v_ref, o_ref, do_ref, ct_ref, st_ref, ck, s1k, s2k, sink_ref, dq_ref, dk_ref, dv_ref, dsink_ref,
             kd, kdt, vd, bias_ref, dkacc, dvacc):
        pair = pl.program_id(0)
        i = pl.program_id(1)

        @pl.when(i == 0)
        def _():
            kv, vv = k_ref[...], v_ref[...]
            for j in range(2):
                kdup = _dup_lanes(kv, j)
                kd[j, 0:128, :] = jnp.zeros((128, 128), BF16)
                kd[j, 128:, :] = kdup.astype(BF16)
                kdt[j, :, 0:128] = jnp.zeros((128, 128), BF16)
                kdt[j, :, 128:] = kdup.T.astype(BF16)
                vd[j, 0:128, :] = jnp.zeros((128, 128), BF16)
                vd[j, 128:, :] = _dup_lanes(vv, j).astype(BF16)
            _fill_bias_t(bias_ref)
            dkacc[...] = jnp.zeros_like(dkacc)
            dvacc[...] = jnp.zeros_like(dvacc)
            dsink_ref[...] = jnp.zeros_like(dsink_ref)

        row_lo = _row_lo()
        ones = jnp.ones((8, 128), BF16)

        def block(b, carry):
            r0 = pl.multiple_of(b * 128, 128)
            gb = i * nb + b
            keys = pl.ds(pl.multiple_of(gb * 128, 128), 256)
            rows = pl.ds(r0, 128)
            qt = q_ref[rows, :] * scale
            dob = do_ref[rows, :]
            prod = dob * o_ref[rows, :]
            ct, st = ct_ref[:, rows], st_ref[:, rows]
            bias = bias_ref[jnp.where(gb == 0, 1, 0)]
            bias4 = jnp.concatenate([bias] * 4, axis=1)
            cols = []
            for j in range(2):
                q4 = _split_heads(qt, j).astype(BF16)
                kw = kd[j, keys, :]
                p, e_sink, denom = _softmax_t(q4, kw, bias4, _sink_row(sink_ref, pair, j))
                inv = 1.0 / denom
                prob = p * inv
                do4 = _split_heads(dob, j).astype(BF16)
                delta = _dot_nt(ones, _split_heads(prod, j).astype(BF16))[0:1, :]
                dp = _dot_nt(vd[j, keys, :], do4)
                ds = (prob * (dp - delta)).astype(BF16)
                dkacc[j, keys, :] = dkacc[j, keys, :] + _dot(ds, q4)
                dvacc[j, keys, :] = dvacc[j, keys, :] + _dot(prob.astype(BF16), do4)
                dqt = _dot(kdt[j, :, keys], ds) * scale
                dsink = -(e_sink * inv) * delta
                for h in range(4):
                    row = 4 * j + h
                    total = jnp.sum(dsink[:, 128 * h : 128 * (h + 1)], axis=1, keepdims=True)
                    dsink_ref[0, row : row + 1, :] = dsink_ref[0, row : row + 1, :] + jnp.broadcast_to(total, (1, 128))
                for gi in range(2):
                    z = jnp.where(row_lo, dqt[:, 256 * gi : 256 * gi + 128], dqt[:, 256 * gi + 128 : 256 * gi + 256])
                    cols.append(unrope_rows(z, ct, st).T)
            dq_ref[rows, :] = jnp.concatenate(cols, axis=1).astype(BF16)
            return carry

        lax.fori_loop(0, nb, block, 0)

        @pl.when(i == n_tiles - 1)
        def _():
            lo = _lane_lo()
            dk = jnp.where(lo, _fold_lanes(dkacc[0, 128:, :]), _fold_lanes(dkacc[1, 128:, :]))
            dk_ref[...] = _rope_transposed(dk, ck[...], s1k[...], s2k[...]).astype(BF16)
            dv_ref[...] = jnp.where(lo, _fold_lanes(dvacc[0, 128:, :]), _fold_lanes(dvacc[1, 128:, :])).astype(BF16)

    kv_out = pl.BlockSpec((t, 128), lambda p, i: (0, p))
    tab_t = pl.BlockSpec((half, tq), lambda p, i: (0, i))
    tab_k = pl.BlockSpec((t, 128), lambda p, i: (0, 0))
    return pl.pallas_call(
        body,
        name="attn_bwd",
        grid=(2, n_tiles),
        in_specs=[sp["q"], sp["k"], sp["v"], sp["act"], sp["act"], tab_t, tab_t, tab_k, tab_k, tab_k, sp["sink"]],
        out_specs=[sp["act"], kv_out, kv_out, pl.BlockSpec((1, 8, 128), lambda p, i: (p, 0, 0))],
        out_shape=[
            jax.ShapeDtypeStruct((t, D), BF16),
            jax.ShapeDtypeStruct((t, 256), BF16),
            jax.ShapeDtypeStruct((t, 256), BF16),
            jax.ShapeDtypeStruct((2, 8, 128), F32),
        ],
        scratch_shapes=[pltpu.VMEM((2, t + 128, 128), BF16), pltpu.VMEM((2, 128, t + 128), BF16), pltpu.VMEM((2, t + 128, 128), BF16),
                        pltpu.VMEM((2, 256, 128), F32), pltpu.VMEM((2, t + 128, 128), F32), pltpu.VMEM((2, t + 128, 128), F32)],
        compiler_params=_params(2),
    )(proj, proj, proj, o, d_o, cos_t, sin_t, cos, s1, s2, sinks)


HALF = 512


def _softplus(y):
    u = jnp.exp(-jnp.abs(y))
    w = 1.0 + u
    log1p = jnp.where(w == 1.0, u, jnp.log(w) * (u / jnp.where(w == 1.0, 1.0, w - 1.0)))
    return jnp.maximum(y, 0.0) + log1p


def _gate_block(w_ref, blk):
    return w_ref[:, :, 256 * blk : 256 * (blk + 1)].reshape(256, 256)


def _gate_matmul(act_bf, w_ref):
    return jnp.concatenate([_dot(act_bf[:, 256 * blk : 256 * (blk + 1)], _gate_block(w_ref, blk)) for blk in range(2)], axis=1)


def _gate_matmul_t(grad_bf, w_ref):
    return jnp.concatenate([_dot_nt(grad_bf[:, 256 * blk : 256 * (blk + 1)], _gate_block(w_ref, blk)) for blk in range(2)], axis=1)


def _conv_and_gates(xe_ref, tm, cw_ref, cb_ref, wa_ref, wx_ref, ba_ref, bx_ref, lam_ref, keep):
    xc = cb_ref[...] + sum(cw_ref[k : k + 1, :] * xe_ref[pl.ds(5 + k, tm), :] for k in range(CONV_WIDTH))
    xc_bf = xc.astype(BF16)
    r = _sigmoid(_gate_matmul(xc_bf, wa_ref) + ba_ref[...])
    ig = _sigmoid(_gate_matmul(xc_bf, wx_ref) + bx_ref[...])
    sp = _softplus(-lam_ref[...])
    log_a = -LRU_C * r * sp
    tanh = jnp.tanh(-log_a)
    mult = jnp.sqrt(2.0 * tanh / (1.0 + tanh))
    a = jnp.where(keep, jnp.exp(log_a), 0.0)
    mult = jnp.where(keep, mult, 1.0)
    return xc, xc_bf, r, ig, sp, a, mult


def _scan_rows(a_ref, b_ref, h_ref, carry_ref, tm, reverse):
    width = a_ref.shape[1]
    row = lax.broadcasted_iota(jnp.int32, (8, width), 0)

    def group(g, carry):
        gg = tm // 8 - 1 - g if reverse else g
        r0 = pl.multiple_of(gg * 8, 8)
        a = a_ref[pl.ds(r0, 8), :]
        b = b_ref[pl.ds(r0, 8), :]
        for s in (1, 2, 4):
            shift = 8 - s if reverse else s
            a_sh = pltpu.roll(a, shift, 0)
            b_sh = pltpu.roll(b, shift, 0)
            ok = (row < 8 - s) if reverse else (row >= s)
            b = jnp.where(ok, a * b_sh + b, b)
            a = jnp.where(ok, a * a_sh, a)
        h = a * carry + b
        h_ref[pl.ds(r0, 8), :] = h
        last = h[0:1, :] if reverse else h[7:8, :]
        return jnp.broadcast_to(last, (8, width))

    carry_ref[...] = lax.fori_loop(0, tm // 8, group, carry_ref[...])


def _rnn_specs(t, tm, order):
    n = t // tm
    return dict(
        tile=pl.BlockSpec((tm, HALF), lambda ch, i: (order(i, n), ch)),
        before=lambda col0: pl.BlockSpec((8, HALF), lambda ch, i: (jnp.maximum(order(i, n) * (tm // 8) - 1, 0), col0 + ch)),
        proj_tile=lambda col0: pl.BlockSpec((tm, HALF), lambda ch, i: (order(i, n), col0 + ch)),
        keep=pl.BlockSpec((tm, 1), lambda ch, i: (order(i, n), 0)),
        gates=lambda which: pl.BlockSpec((N_DEV, GATE_SHARD, HALF), lambda ch, i: (0, PACK_GATE // GATE_SHARD + which, ch)),
        gate_grads=pl.BlockSpec((2, 2, 256, 256), lambda ch, i: (0, ch, 0, 0)),
        vec=pl.BlockSpec((1, HALF), lambda ch, i: (0, ch)),
        conv=pl.BlockSpec((CONV_WIDTH, HALF), lambda ch, i: (0, ch)),
    )


def _rnn_fwd(proj, keep, w_all, rg_ba, rg_bx, rg_lambda, conv_w, conv_b):
    t = proj.shape[0]
    tm = min(512, t)
    sp = _rnn_specs(t, tm, lambda i, n: i)

    def body(xr_ref, xb_ref, keep_ref, wa_ref, wx_ref, ba_ref, bx_ref, lam_ref, cw_ref, cb_ref, hh_ref, xe, a_buf, b_buf, carry):
        i = pl.program_id(1)

        @pl.when(i == 0)
        def _():
            carry[...] = jnp.zeros_like(carry)

        xe[0:8, :] = jnp.where(i > 0, xb_ref[...], 0.0)
        xe[8:, :] = xr_ref[...]
        keep_rows = keep_ref[...] != 0.0
        xc, _, _, ig, _, a, mult = _conv_and_gates(xe, tm, cw_ref, cb_ref, wa_ref, wx_ref, ba_ref, bx_ref, lam_ref, keep_rows)
        a_buf[...] = a
        b_buf[...] = mult * (ig * xc)
        _scan_rows(a_buf, b_buf, hh_ref, carry, tm, reverse=False)

    return pl.pallas_call(
        body,
        name="rnn_fwd",
        grid=(2, t // tm),
        in_specs=[sp["proj_tile"](SEG_XR // HALF), sp["before"](SEG_XR // HALF), sp["keep"], sp["gates"](0), sp["gates"](1),
                  sp["vec"], sp["vec"], sp["vec"], sp["conv"], sp["vec"]],
        out_specs=sp["tile"],
        out_shape=jax.ShapeDtypeStruct((t, D), F32),
        scratch_shapes=[pltpu.VMEM((tm + 8, HALF), F32), pltpu.VMEM((tm, HALF), F32), pltpu.VMEM((tm, HALF), F32), pltpu.VMEM((8, HALF), F32)],
        compiler_params=_params(2),
    )(proj, proj, keep, w_all, w_all, rg_ba, rg_bx, rg_lambda, conv_w, conv_b)


def _rnn_bwd(proj, hh, d_hh, keep, w_all, rg_ba, rg_bx, rg_lambda, conv_w, conv_b):
    t = proj.shape[0]
    tm = min(512, t)
    n_tiles = t // tm
    sp = _rnn_specs(t, tm, lambda i, n: n - 1 - i)

    def body(xr_ref, xb_ref, hh_ref, hb_ref, dhh_ref, keep_ref, wa_ref, wx_ref, ba_ref, bx_ref, lam_ref, cw_ref, cb_ref,
             dxr_ref, gw_ref, gvec_ref, xe, he, ae, c_buf, g_buf, dxe, carry, a_next):
        i = pl.program_id(1)
        first_tile = i == n_tiles - 1

        @pl.when(i == 0)
        def _():
            carry[...] = jnp.zeros_like(carry)
            a_next[...] = jnp.zeros_like(a_next)
            dxe[tm:, :] = jnp.zeros((8, HALF), F32)
            gw_ref[...] = jnp.zeros_like(gw_ref)
            gvec_ref[...] = jnp.zeros_like(gvec_ref)

        xe[0:8, :] = jnp.where(first_tile, 0.0, xb_ref[...])
        xe[8:, :] = xr_ref[...]
        he[0:8, :] = jnp.where(first_tile, 0.0, hb_ref[...])
        he[8:, :] = hh_ref[...]
        keep_rows = keep_ref[...] != 0.0
        xc, xc_bf, r, ig, sp_, a, mult = _conv_and_gates(xe, tm, cw_ref, cb_ref, wa_ref, wx_ref, ba_ref, bx_ref, lam_ref, keep_rows)

        ae[0:tm, :] = a
        ae[tm:, :] = a_next[...]
        a_next[...] = jnp.broadcast_to(a[0:1, :], (8, HALF))
        c_buf[...] = ae[pl.ds(1, tm), :]
        g_buf[...] = dhh_ref[...]
        _scan_rows(c_buf, g_buf, g_buf, carry, tm, reverse=True)
        g = g_buf[...]

        h_prev = he[pl.ds(7, tm), :]
        bb = ig * xc
        d_ig = g * mult * xc
        d_xc = g * mult * ig
        d_log_a = jnp.where(keep_rows, g * h_prev * a - (g * bb) * (a * a) / mult, 0.0)
        d_r = d_log_a * (-LRU_C * sp_)
        d_sp = jnp.sum(d_log_a * (-LRU_C * r), axis=0, keepdims=True)
        d_zr = d_r * r * (1.0 - r)
        d_zi = d_ig * ig * (1.0 - ig)
        d_zr_bf = d_zr.astype(BF16)
        d_zi_bf = d_zi.astype(BF16)
        d_xc = d_xc + _gate_matmul_t(d_zr_bf, wa_ref) + _gate_matmul_t(d_zi_bf, wx_ref)
        for blk in range(2):
            cols = slice(256 * blk, 256 * (blk + 1))
            gw_ref[0, blk] = gw_ref[0, blk] + _dot_tn(xc_bf[:, cols], d_zr_bf[:, cols])
            gw_ref[1, blk] = gw_ref[1, blk] + _dot_tn(xc_bf[:, cols], d_zi_bf[:, cols])

        dxe[0:tm, :] = d_xc
        dxr = sum(cw_ref[k : k + 1, :] * dxe[pl.ds(3 - k, tm), :] for k in range(CONV_WIDTH))
        dxr_ref[...] = dxr.astype(BF16)
        dxe[tm:, :] = d_xc[0:8, :]

        lam = lam_ref[...]
        sums = [
            jnp.sum(d_zr, axis=0, keepdims=True),
            jnp.sum(d_zi, axis=0, keepdims=True),
            d_sp * (-_sigmoid(-lam)),
            jnp.sum(d_xc, axis=0, keepdims=True),
        ] + [jnp.sum(d_xc * xe[pl.ds(5 + k, tm), :], axis=0, keepdims=True) for k in range(CONV_WIDTH)]
        gvec_ref[...] = gvec_ref[...] + jnp.concatenate(sums, axis=0)

    return pl.pallas_call(
        body,
        name="rnn_bwd",
        grid=(2, n_tiles),
        in_specs=[
            sp["proj_tile"](SEG_XR // HALF), sp["before"](SEG_XR // HALF), sp["tile"], sp["before"](0), sp["tile"], sp["keep"],
            sp["gates"](0), sp["gates"](1), sp["vec"], sp["vec"], sp["vec"], sp["conv"], sp["vec"],
        ],
        out_specs=[sp["tile"], sp["gate_grads"], pl.BlockSpec((8, HALF), lambda ch, i: (0, ch))],
        out_shape=[
            jax.ShapeDtypeStruct((t, D), BF16),
            jax.ShapeDtypeStruct((2, 4, 256, 256), F32),
            jax.ShapeDtypeStruct((8, D), F32),
        ],
        scratch_shapes=[
            pltpu.VMEM((tm + 8, HALF), F32),
            pltpu.VMEM((tm + 8, HALF), F32),
            pltpu.VMEM((tm + 8, HALF), F32),
            pltpu.VMEM((tm, HALF), F32),
            pltpu.VMEM((tm, HALF), F32),
            pltpu.VMEM((tm + 8, HALF), F32),
            pltpu.VMEM((8, HALF), F32),
            pltpu.VMEM((8, HALF), F32),
        ],
        compiler_params=_params(2),
    )(proj, proj, hh, hh, d_hh, keep, w_all, w_all, rg_ba, rg_bx, rg_lambda, conv_w, conv_b)


def _mid(x, target, o, hh, proj, w_all, gate, final_g):
    t = x.shape[0]
    tm = min(256, t)
    n_tiles = t // tm
    full = pl.BlockSpec((tm, D), lambda i: (i, 0))
    half = lambda col: pl.BlockSpec((tm, 512), lambda i: (i, col))
    vec = pl.BlockSpec((1, D), lambda i: (0, 0))
    any_spec = pl.BlockSpec(memory_space=pl.ANY)

    def body(x_ref, tg_ref, o_ref, hh_ref, ga0, ga1, gr0, gr1, ma0, ma1, mr0, mr1, w_hbm, gate_ref, fg_ref,
             dres_ref, do_ref, dhh_ref, dga_ref, dgr_ref, dma_ref, dmr_ref, gw_hbm, vec_ref, w, gw, sem, w_sems):
        i = pl.program_id(0)

        @pl.when(i == 0)
        def _():
            loads = [
                pltpu.make_async_copy(
                    w_hbm.at[s, pl.ds(ROW_SHARD * k, ROW_SHARD), :], w.at[k, pl.ds(ROW_SHARD * s, ROW_SHARD), :], w_sems.at[k * N_DEV + s]
                )
                for k in range(3)
                for s in range(N_DEV)
            ]
            for cp in loads:
                cp.start()
            for cp in loads:
                cp.wait()
            gw[...] = jnp.zeros_like(gw)
            vec_ref[...] = jnp.zeros_like(vec_ref)

        cat = lambda a, b: jnp.concatenate([a[...], b[...]], axis=1)
        ga, gr, ma, mr = cat(ga0, ga1), cat(gr0, gr1), cat(ma0, ma1), cat(mr0, mr1)
        ov, hv, gate_v, fg = o_ref[...], hh_ref[...], gate_ref[...], fg_ref[...]
        sg_a = _sigmoid(ga)
        sg_r = _sigmoid(gr)
        silu_a = ga * sg_a
        silu_r = gr * sg_r
        ya = (ov * silu_a).astype(BF16)
        yr = (hv * silu_r).astype(BF16)
        pa = _dot(ya, w[0])
        pr = _dot(yr, w[1])
        sa = _sigmoid(ma)
        sr = _sigmoid(mr)
        merged = (sa * pa + sr * pr).astype(BF16)
        mo = _dot(merged, w[2])
        out = x_ref[...] + gate_v * mo
        r2 = lax.rsqrt(jnp.mean(out * out, axis=-1, keepdims=True) + NORM_EPS)
        outn = out * r2
        err = outn * fg - tg_ref[...]
        loss = 0.5 * jnp.sum(jnp.mean(err * err, axis=-1, keepdims=True), axis=0, keepdims=True)
        dy = err * (1.0 / D)
        g_final = jnp.sum(dy * outn, axis=0, keepdims=True)
        doutn = dy * fg
        dout = r2 * (doutn - outn * jnp.mean(doutn * outn, axis=-1, keepdims=True))
        dres_ref[...] = dout
        d_gate = jnp.sum(dout * mo, axis=0, keepdims=True)
        dmo = (dout * gate_v).astype(BF16)
        dmerged = _dot_nt(dmo, w[2])
        gw[2] = gw[2] + _dot_tn(merged, dmo)
        dpa = (dmerged * sa).astype(BF16)
        dpr = (dmerged * sr).astype(BF16)
        dma_ref[...] = (dmerged * pa * (sa * (1.0 - sa))).astype(BF16)
        dmr_ref[...] = (dmerged * pr * (sr * (1.0 - sr))).astype(BF16)
        dya = _dot_nt(dpa, w[0])
        dyr = _dot_nt(dpr, w[1])
        gw[0] = gw[0] + _dot_tn(ya, dpa)
        gw[1] = gw[1] + _dot_tn(yr, dpr)
        do_ref[...] = dya * silu_a
        dhh_ref[...] = dyr * silu_r
        dga_ref[...] = (dya * ov * (sg_a * (1.0 + ga * (1.0 - sg_a)))).astype(BF16)
        dgr_ref[...] = (dyr * hv * (sg_r * (1.0 + gr * (1.0 - sg_r)))).astype(BF16)
        zeros = jnp.zeros((5, D), F32)
        vec_ref[...] = vec_ref[...] + jnp.concatenate([g_final, d_gate, jnp.broadcast_to(loss, (1, D)), zeros], axis=0)

        @pl.when(i == n_tiles - 1)
        def _():
            cp = pltpu.make_async_copy(gw, gw_hbm, sem)
            cp.start()
            cp.wait()

    bf_tile = jax.ShapeDtypeStruct((t, D), BF16)
    f_tile = jax.ShapeDtypeStruct((t, D), F32)
    return pl.pallas_call(
        body,
        name="mid",
        grid=(n_tiles,),
        in_specs=[full, full, full, full,
                  half(SEG_GA // 512), half(SEG_GA // 512 + 1), half(SEG_GR // 512), half(SEG_GR // 512 + 1),
                  half(SEG_MA // 512), half(SEG_MA // 512 + 1), half(SEG_MR // 512), half(SEG_MR // 512 + 1),
                  any_spec, vec, vec],
        out_specs=[full, full, full, full, full, full, full, any_spec, pl.BlockSpec((8, D), lambda i: (0, 0))],
        out_shape=[f_tile, f_tile, f_tile, bf_tile, bf_tile, bf_tile, bf_tile,
                   jax.ShapeDtypeStruct((3, D, D), F32), jax.ShapeDtypeStruct((8, D), F32)],
        scratch_shapes=[pltpu.VMEM((3, D, D), BF16), pltpu.VMEM((3, D, D), F32), pltpu.SemaphoreType.DMA, pltpu.SemaphoreType.DMA((3 * N_DEV,))],
        compiler_params=_params(1),
    )(x, target, o, hh, proj, proj, proj, proj, proj, proj, proj, proj, w_all, gate, final_g)


_SEGMENTS = ((SEG_Q, 1024), (SEG_K, 256), (SEG_V, 256), (SEG_GA, 1024), (SEG_XR, 1024), (SEG_GR, 1024), (SEG_MA, 1024), (SEG_MR, 1024))


def _seg_specs(tm):
    return [pl.BlockSpec((tm, width), lambda i: (i, 0)) for _, width in _SEGMENTS]


def _bwd_dh(dsegs, w_all, x, dres, mod, norm_g, sum_w_bf):
    t = x.shape[0]
    tm = min(512, t)
    n_tiles = t // tm
    n_seg = len(_SEGMENTS)
    full = pl.BlockSpec((tm, D), lambda i: (i, 0))
    views = [lambda ref, peer: ref.at[_chip_of(peer)]]

    def body(*refs):
        segs = refs[:n_seg]
        w_hbm, x_ref, dres_ref, mod_ref, ng_ref, sum_hbm, gx_ref, vec_ref, land_hbm, w, sems, send_sems, recv_sems = refs[n_seg:]
        i = pl.program_id(0)

        @pl.when(i == 0)
        def _():
            for cp in _chip_copies([sum_hbm], [land_hbm], views, send_sems, recv_sems):
                cp.start()
            loads = [
                pltpu.make_async_copy(w_hbm.at[s], w.at[pl.ds(W_IN_SHARD * s, W_IN_SHARD), :], sems.at[s])
                for s in range(N_DEV)
            ]
            for cp in loads:
                cp.start()
            for cp in loads:
                cp.wait()
            vec_ref[...] = jnp.zeros_like(vec_ref)

        dh = sum(_dot(seg[...], w[start : start + width, :]) for seg, (start, width) in zip(segs, _SEGMENTS))
        xv = x_ref[...]
        r1 = lax.rsqrt(jnp.mean(xv * xv, axis=-1, keepdims=True) + NORM_EPS)
        xn = xv * r1
        ng = ng_ref[...]
        scale1 = 1.0 + mod_ref[:, D : 2 * D]
        d_shift = jnp.sum(dh, axis=0, keepdims=True)
        d_scale = jnp.sum(dh * (xn * ng), axis=0, keepdims=True)
        g_norm = jnp.sum(dh * xn * scale1, axis=0, keepdims=True)
        dxn = dh * (ng * scale1)
        dx = r1 * (dxn - xn * jnp.mean(dxn * xn, axis=-1, keepdims=True))
        gx_ref[...] = dres_ref[...] + dx
        vec_ref[...] = vec_ref[...] + jnp.concatenate([d_shift, d_scale, g_norm, jnp.zeros((5, D), F32)], axis=0)

        @pl.when(i == n_tiles - 1)
        def _():
            for cp in _chip_copies([sum_hbm], [land_hbm], views, send_sems, recv_sems):
                cp.wait_recv()
                cp.wait_send()

    any_spec = pl.BlockSpec(memory_space=pl.ANY)
    n_copies = len(CHIP_FLIPS)
    return pl.pallas_call(
        body,
        name="bwd_dh",
        grid=(n_tiles,),
        in_specs=_seg_specs(tm) + [any_spec, full, full, pl.BlockSpec((1, 3 * D), lambda i: (0, 0)), pl.BlockSpec((1, D), lambda i: (0, 0)),
                                   any_spec],
        out_specs=[full, pl.BlockSpec((8, D), lambda i: (0, 0)), any_spec],
        out_shape=[jax.ShapeDtypeStruct((t, D), F32), jax.ShapeDtypeStruct((8, D), F32),
                   jax.ShapeDtypeStruct((n_copies, W_IN_SHARD, D), BF16)],
        scratch_shapes=[pltpu.VMEM((IN_W, D), BF16), pltpu.SemaphoreType.DMA((N_DEV,)),
                        pltpu.SemaphoreType.DMA((n_copies,)), pltpu.SemaphoreType.DMA((n_copies,))],
        compiler_params=_params(1),
    )(*dsegs, w_all, x, dres, mod, norm_g, sum_w_bf)


def _bwd_gw(dsegs, h, sum_3_bf, sum_g_bf):
    t = h.shape[0]
    tm = min(512, t)
    n_tiles = t // tm
    n_seg = len(_SEGMENTS)
    views = [lambda ref, peer: ref.at[:, _chip_of(peer)], lambda ref, peer: ref.at[:, :, _chip_of(peer)]]

    def body(*refs):
        segs = refs[:n_seg]
        h_ref, s3_hbm, sg_hbm, out_hbm, l3_hbm, lg_hbm, acc, sem, send_sems, recv_sems = refs[n_seg:]
        i = pl.program_id(0)

        @pl.when(i == 0)
        def _():
            for cp in _chip_copies([s3_hbm, sg_hbm], [l3_hbm, lg_hbm], views, send_sems, recv_sems):
                cp.start()
            acc[...] = jnp.zeros_like(acc)

        hv = h_ref[...]
        for seg, (start, width) in zip(segs, _SEGMENTS):
            acc[start : start + width, :] = acc[start : start + width, :] + _dot_tn(seg[...], hv)

        @pl.when(i == n_tiles - 1)
        def _():
            cp = pltpu.make_async_copy(acc, out_hbm, sem)
            cp.start()
            cp.wait()
            for cp in _chip_copies([s3_hbm, sg_hbm], [l3_hbm, lg_hbm], views, send_sems, recv_sems):
                cp.wait_recv()
                cp.wait_send()

    any_spec = pl.BlockSpec(memory_space=pl.ANY)
    n_copies = 2 * len(CHIP_FLIPS)
    return pl.pallas_call(
        body,
        name="bwd_gw",
        grid=(n_tiles,),
        in_specs=_seg_specs(tm) + [pl.BlockSpec((tm, D), lambda i: (i, 0)), any_spec, any_spec],
        out_specs=[any_spec, any_spec, any_spec],
        out_shape=[jax.ShapeDtypeStruct((IN_W, D), F32),
                   jax.ShapeDtypeStruct((len(CHIP_FLIPS), 3, ROW_SHARD, D), BF16),
                   jax.ShapeDtypeStruct((len(CHIP_FLIPS), 2, 4, GATE_SHARD, 256), BF16)],
        scratch_shapes=[pltpu.VMEM((IN_W, D), F32), pltpu.SemaphoreType.DMA,
                        pltpu.SemaphoreType.DMA((n_copies,)), pltpu.SemaphoreType.DMA((n_copies,))],
        compiler_params=_params(1),
    )(*dsegs, h, sum_3_bf, sum_g_bf)


def _adamw(w, g, m, v):
    m = ADAM_B1 * m + (1.0 - ADAM_B1) * g
    v = ADAM_B2 * v + (1.0 - ADAM_B2) * (g * g)
    m_hat = m / (1.0 - ADAM_B1**ADAM_STEP)
    v_hat = v / (1.0 - ADAM_B2**ADAM_STEP)
    delta = -ADAM_LR * (m_hat / (jnp.sqrt(v_hat) + ADAM_EPS) + ADAM_WD * w)
    return delta, m, v


def _row_block(rows):
    return 128 if rows % 128 == 0 else rows


def _add_slots(own, l_ref):
    total = own
    for s in range(l_ref.shape[0]):
        total = total + l_ref[s].astype(F32)
    return total


def _sum_slots(name, own, land):
    n_slots, rows, cols = land.shape
    rb = 208 if rows == W_IN_SHARD else _row_block(rows)

    def body(o_ref, l_ref, g_ref):
        g_ref[...] = _add_slots(o_ref[...], l_ref)

    spec = pl.BlockSpec((rb, cols), lambda i: (i, 0))
    return pl.pallas_call(
        body,
        name=name,
        grid=(rows // rb,),
        in_specs=[spec, pl.BlockSpec((n_slots, rb, cols), lambda i: (0, i, 0))],
        out_specs=spec,
        out_shape=jax.ShapeDtypeStruct((rows, cols), F32),
        compiler_params=_params(1),
    )(own, land)


def _adam_update(name, g, w, m, v):
    rows, cols = w.shape
    rb = _row_block(rows)
    spec = pl.BlockSpec((rb, cols), lambda i: (i, 0))

    def body(g_ref, w_ref, m_ref, v_ref, d_ref, m2_ref, v2_ref):
        d_ref[...], m2_ref[...], v2_ref[...] = _adamw(w_ref[...], g_ref[...], m_ref[...], v_ref[...])

    shape = jax.ShapeDtypeStruct((rows, cols), F32)
    return pl.pallas_call(
        body, name=name, grid=(rows // rb,), in_specs=[spec] * 4, out_specs=[spec] * 3, out_shape=[shape] * 3,
        compiler_params=_params(1),
    )(g, w, m, v)


def _shard_update(own_3, land_3, own_g, land_g, weights, m_in, v_in):
    n = len(weights)

    def body(*refs):
        o3_ref, l3_ref, og_ref, lg_ref = refs[:4]
        w_refs, m_refs, v_refs = (refs[4 + n * k : 4 + n * (k + 1)] for k in range(3))
        outs = refs[4 + 3 * n :]
        for k in range(n):
            if k < 3:
                g = o3_ref[k]
                for slot in range(3):
                    g = g + l3_ref[slot, k].astype(F32)
            else:
                g = og_ref[k - 3]
                for slot in range(3):
                    g = g + lg_ref[slot, k - 3].astype(F32)
            d, m2, v2 = _adamw(w_refs[k][0], g, m_refs[k][0], v_refs[k][0])
            for which, val in enumerate((g, d, m2, v2)):
                outs[which * n + k][0] = val

    shapes = [jax.ShapeDtypeStruct(w.shape, F32) for w in weights]
    res = pl.pallas_call(
        body, name="shard_update", out_shape=shapes * 4, compiler_params=pltpu.CompilerParams(vmem_limit_bytes=VMEM_LIMIT)
    )(own_3, land_3, own_g, land_g, *weights, *m_in, *v_in)
    return [res[n * k : n * (k + 1)] for k in range(4)]


VEC_LAYOUT = (("b_ada", 3 * D), ("norm_g", D), ("attn_sinks", 128), ("conv_b", D), ("rg_ba", D), ("rg_bx", D), ("rg_lambda", D),
              ("final_g", D), ("conv_w", CONV_WIDTH * D), ("loss", 128))
VEC_OFFSET = {name: sum(w for _, w in VEC_LAYOUT[:k]) for k, (name, _) in enumerate(VEC_LAYOUT)}
VEC_ROWS = 112
VEC_WIDTH = VEC_ROWS * 128
assert sum(w for _, w in VEC_LAYOUT) <= VEC_WIDTH
N_REPLICATED = 8


def _small_update(vec_all, c_all_t, me, rep_w, rep_m, rep_v, ada, conv):
    n_ada = ada[0].shape[1]
    n_conv = conv[0].shape[1]

    def body(*refs):
        me_ref, vec_ref, ct_ref = refs[:3]
        w_refs, m_refs, v_refs = (refs[3 + N_REPLICATED * k : 3 + N_REPLICATED * (k + 1)] for k in range(3))
        pos = 3 + 3 * N_REPLICATED
        ada_refs, conv_refs = refs[pos : pos + 3], refs[pos + 3 : pos + 6]
        outs = refs[pos + 6 :]
        loss_ref, rep_outs = outs[0], outs[1 : 1 + 4 * N_REPLICATED]
        ada_outs, conv_outs = outs[1 + 4 * N_REPLICATED : 5 + 4 * N_REPLICATED], outs[5 + 4 * N_REPLICATED : 9 + 4 * N_REPLICATED]
        conv_buf = outs[9 + 4 * N_REPLICATED]

        total = vec_ref[0:1, :]
        for slot in range(1, N_DEV):
            total = total + vec_ref[slot : slot + 1, :]
        loss_ref[...] = total[:, VEC_OFFSET["loss"] : VEC_OFFSET["loss"] + 1]
        for k, (name, _) in enumerate(VEC_LAYOUT[:N_REPLICATED]):
            width = w_refs[k].shape[1]
            g = total[:, VEC_OFFSET[name] : VEC_OFFSET[name] + width]
            d, m2, v2 = _adamw(w_refs[k][...], g, m_refs[k][...], v_refs[k][...])
            for which, val in enumerate((g, d, m2, v2)):
                rep_outs[which * N_REPLICATED + k][...] = val

        device = me_ref[0]
        for k in range(CONV_WIDTH):
            conv_buf[k : k + 1, :] = total[:, VEC_OFFSET["conv_w"] + D * k : VEC_OFFSET["conv_w"] + D * (k + 1)]
        g_conv = conv_buf[:, pl.ds(pl.multiple_of(device * n_conv, 128), n_conv)]
        for ref, val in zip(conv_outs, (g_conv,) + _adamw(conv_refs[0][...], g_conv, conv_refs[1][...], conv_refs[2][...])):
            ref[...] = val

        dmod = vec_ref[:, pl.ds(pl.multiple_of(device * n_ada, 128), n_ada)]
        g_ada = _dot(ct_ref[...], dmod)
        for ref, val in zip(ada_outs, (g_ada,) + _adamw(ada_refs[0][...], g_ada, ada_refs[1][...], ada_refs[2][...])):
            ref[...] = val

    vmem = pl.BlockSpec(memory_space=pltpu.VMEM)
    rep_shapes = [jax.ShapeDtypeStruct(w.shape, F32) for w in rep_w]
    out_shape = ([jax.ShapeDtypeStruct((1, 1), F32)] + rep_shapes * 4 + [jax.ShapeDtypeStruct(ada[0].shape, F32)] * 4
                 + [jax.ShapeDtypeStruct(conv[0].shape, F32)] * 4)
    n_in = 3 + 3 * N_REPLICATED + 6
    res = pl.pallas_call(
        body,
        name="small_update",
        in_specs=[pl.BlockSpec(memory_space=pltpu.SMEM)] + [vmem] * (n_in - 1),
        out_specs=[vmem] * len(out_shape),
        out_shape=out_shape,
        scratch_shapes=[pltpu.VMEM((CONV_WIDTH, D), F32)],
        compiler_params=pltpu.CompilerParams(vmem_limit_bytes=VMEM_LIMIT),
    )(me, vec_all, c_all_t, *rep_w, *rep_m, *rep_v, *ada, *conv)
    loss = res[0]
    reps = [res[1 + N_REPLICATED * k : 1 + N_REPLICATED * (k + 1)] for k in range(4)]
    return loss, reps, res[1 + 4 * N_REPLICATED : 5 + 4 * N_REPLICATED], res[5 + 4 * N_REPLICATED : 9 + 4 * N_REPLICATED]


def kernel(x, c, positions, w_ada, b_ada, norm_g, w_in, attn_sinks, conv_w, conv_b, rg_wa, rg_ba, rg_wx, rg_bx, rg_lambda, w_attn_proj, w_rnn_proj, w_out, final_g, loss_target, m_w_ada, m_b_ada, m_norm_g, m_w_in, m_attn_sinks, m_conv_w, m_conv_b, m_rg_wa, m_rg_ba, m_rg_wx, m_rg_bx, m_rg_lambda, m_w_attn_proj, m_w_rnn_proj, m_w_out, m_final_g, v_w_ada, v_b_ada, v_norm_g, v_w_in, v_attn_sinks, v_conv_w, v_conv_b, v_rg_wa, v_rg_ba, v_rg_wx, v_rg_bx, v_rg_lambda, v_w_attn_proj, v_w_rnn_proj, v_w_out, v_final_g):
    my_x, my_y, my_c = _position()
    me = _slot((my_x, my_y, my_c))
    x2, target = x[0], loss_target[0]
    pos = positions[0]

    gate_rows = lambda w: w[0].transpose(1, 0, 2).reshape(GATE_SHARD, D)
    w_pack = jnp.concatenate([w_attn_proj[0], w_rnn_proj[0], w_out[0], gate_rows(rg_wa), gate_rows(rg_wx)], axis=0).astype(BF16)
    conv_rows = jnp.pad(conv_w[0], ((0, 8 - CONV_WIDTH), (0, 0)))
    c_rows, conv_all, w_t_all = _all_gather("ag_fwd", [jnp.broadcast_to(c, (8, D)), conv_rows, w_in[0].T.astype(BF16)])
    c_all = c_rows[:, 0, :]
    conv_full = conv_all[:, :CONV_WIDTH, :].transpose(1, 0, 2).reshape(CONV_WIDTH, D)

    n_ada = w_ada.shape[2]
    b_cols = lax.dynamic_slice(b_ada, (0, me * n_ada), (1, n_ada))
    mod_cols = _ada_mod(c_all, w_ada[0], b_cols)
    (mod_all,) = _all_gather_direct("ag_mod", [mod_cols])
    mod = lax.dynamic_slice(mod_all, (0, me, 0), (N_DEV, 1, n_ada)).reshape(1, 3 * D)
    gate = mod[:, 2 * D :]

    tables = _rope_tables(pos)
    proj, h, w_all = _fwd_in(x2, mod, norm_g, w_t_all, tables[0], w_pack)
    o = _attn_fwd(proj, attn_sinks)
    keep = (pos != 0).astype(F32)[:, None]
    hh = _rnn_fwd(proj, keep, w_all, rg_ba, rg_bx, rg_lambda, conv_full, conv_b)

    dres, d_o, d_hh, d_ga, d_gr, d_ma, d_mr, gw3, mid_vec = _mid(x2, target, o, hh, proj, w_all, gate, final_g[None, :])
    d_q, d_k, d_v, d_sink = _attn_bwd(proj, o, d_o, tables, attn_sinks)
    d_xr, g_gates, rnn_vec = _rnn_bwd(proj, hh, d_hh, keep, w_all, rg_ba, rg_bx, rg_lambda, conv_full, conv_b)
    dsegs = (d_q, d_k, d_v, d_ga, d_xr, d_gr, d_ma, d_mr)

    index = jnp.stack([my_c, _chip_of((my_x, my_y))]).astype(jnp.int32)
    part_3 = gw3.reshape(3, 4, 2, ROW_SHARD, D)
    part_g = g_gates.reshape(2, 4, 4, 2, GATE_SHARD, 256)
    recv_3, recv_g = _exchange(
        "rs_pair_small",
        [part_3, part_g],
        [lambda ref, peer: ref.at[:, :, peer[2]], lambda ref, peer: ref.at[:, :, :, peer[2]]],
        [(3, 4, ROW_SHARD, D), (2, 4, 4, GATE_SHARD, 256)],
        flips=(1,),
    )
    sum_3_bf, own_3 = _pair_add(
        "pair_add_3", part_3, recv_3[0], index, (3, 4), 1,
        pl.BlockSpec((None, None, None, ROW_SHARD, D), lambda a, q, idx: (a, q, idx[0], 0, 0)),
        pl.BlockSpec((None, None, ROW_SHARD, D), lambda a, q, idx: (a, q, 0, 0)),
        pl.BlockSpec((None, ROW_SHARD, D), lambda a, q, idx: (a, 0, 0)), (3, ROW_SHARD, D),
    )
    sum_g_bf, own_g = _pair_add(
        "pair_add_g", part_g, recv_g[0], index, (4,), 0,
        pl.BlockSpec((2, 4, None, None, GATE_SHARD, 256), lambda q, idx: (0, 0, q, idx[0], 0, 0)),
        pl.BlockSpec((2, 4, None, GATE_SHARD, 256), lambda q, idx: (0, 0, q, 0, 0)),
        pl.BlockSpec((2, 4, GATE_SHARD, 256), lambda q, idx: (0, 0, 0, 0)), (2, 4, GATE_SHARD, 256),
    )
    g_w_t, land_3, land_g = _bwd_gw(dsegs, h, sum_3_bf, sum_g_bf)
    part_w = g_w_t.reshape(4, 2, W_IN_SHARD, D)
    (recv_w,) = _exchange("rs_pair_w", [part_w], [lambda ref, peer: ref.at[:, peer[2]]], [(4, W_IN_SHARD, D)], flips=(1,))
    rb = 208
    sum_w_bf, own_w = _pair_add(
        "pair_add_w", part_w, recv_w[0], index, (W_IN_SHARD // rb, 4), 1,
        pl.BlockSpec((None, None, rb, D), lambda r, q, idx: (q, idx[0], r, 0)),
        pl.BlockSpec((None, rb, D), lambda r, q, idx: (q, r, 0)),
        pl.BlockSpec((rb, D), lambda r, q, idx: (r, 0)), (W_IN_SHARD, D),
    )
    grad_x, dh_vec, land_w = _bwd_dh(dsegs, w_t_all, x2, dres, mod, norm_g, sum_w_bf)

    pieces = dict(
        b_ada=jnp.concatenate([dh_vec[0:1], dh_vec[1:2], mid_vec[1:2]], axis=1),
        norm_g=dh_vec[2:3],
        attn_sinks=jnp.pad(d_sink[:, :, 0].reshape(1, 16), ((0, 0), (0, 112))),
        conv_b=rnn_vec[3:4],
        rg_ba=rnn_vec[0:1],
        rg_bx=rnn_vec[1:2],
        rg_lambda=rnn_vec[2:3],
        final_g=mid_vec[0:1],
        conv_w=rnn_vec[4:8].reshape(1, CONV_WIDTH * D),
        loss=mid_vec[2:3, 0:128],
    )
    vec = jnp.concatenate([pieces[name] for name, _ in VEC_LAYOUT], axis=1)
    vec = jnp.pad(vec, ((0, 0), (0, VEC_WIDTH - vec.shape[1]))).reshape(VEC_ROWS, 128)
    (vec_all,) = _all_gather_direct("ag_vec", [vec])
    rep_w = [b_ada, norm_g, attn_sinks, conv_b, rg_ba, rg_bx, rg_lambda, final_g[None, :]]
    rep_m = [m_b_ada, m_norm_g, m_attn_sinks, m_conv_b, m_rg_ba, m_rg_bx, m_rg_lambda, m_final_g[None, :]]
    rep_v = [v_b_ada, v_norm_g, v_attn_sinks, v_conv_b, v_rg_ba, v_rg_bx, v_rg_lambda, v_final_g[None, :]]
    loss, reps, ada_res, conv_res = _small_update(
        vec_all.reshape(N_DEV, VEC_WIDTH), c_all.T, me.astype(jnp.int32).reshape(1), rep_w, rep_m, rep_v,
        (w_ada[0], m_w_ada[0], v_w_ada[0]), (conv_w[0], m_conv_w[0], v_conv_w[0]),
    )

    g_in = _sum_slots("sum_w_in", own_w, land_w).T
    d_in, m_in, v_in = _adam_update("adam_w_in", g_in, w_in[0], m_w_in[0], v_w_in[0])
    shard_res = _shard_update(
        own_3, land_3, own_g, land_g,
        (w_attn_proj, w_rnn_proj, w_out, rg_wa, rg_wx),
        (m_w_attn_proj, m_w_rnn_proj, m_w_out, m_rg_wa, m_rg_wx),
        (v_w_attn_proj, v_w_rnn_proj, v_w_out, v_rg_wa, v_rg_wx),
    )

    def by_weight(k):
        b, n, s, cb, ba, bx, lam, fg = reps[k]
        ap, rp, out, wa, wx = shard_res[k]
        w_in_k = (g_in, d_in, m_in, v_in)[k]
        return [ada_res[k][None], b, n, w_in_k[None], s, conv_res[k][None], cb, wa, ba, wx, bx, lam, ap, rp, out, fg[0]]

    return (loss[0, 0], grad_x[None], *by_weight(0), *by_weight(1), *by_weight(2), *by_weight(3))
```

```python
import functools
import math

import jax
import jax.numpy as jnp
import numpy as np
from jax import lax
from jax.experimental import pallas as pl
from jax.experimental.pallas import tpu as pltpu

F32 = jnp.float32
BF16 = jnp.bfloat16

D = 1024
HEAD_DIM = 64
ROT_DIM = 16
ROPE_THETA = 500000.0
WINDOW = 128
LRU_C = 8.0
CONV_WIDTH = 4
NORM_EPS = 1e-6
IN_W = 6656
SEG_Q, SEG_K, SEG_V, SEG_GA, SEG_XR, SEG_GR, SEG_MA, SEG_MR = 0, 1024, 1280, 1536, 2560, 3584, 4608, 5632
N_DEV = 8
W_IN_SHARD = IN_W // N_DEV
ROW_SHARD = D // N_DEV
GATE_SHARD = 256 // N_DEV

ADAM_LR = 0.001
ADAM_B1 = 0.9
ADAM_B2 = 0.999
ADAM_EPS = 1e-08
ADAM_WD = 0.01
ADAM_STEP = 10

VMEM_LIMIT = 60 * 1024 * 1024
MESH = pl.DeviceIdType.MESH
AXES = ("x", "y", "c")


def _dot(a, b):
    return jnp.dot(a, b, preferred_element_type=F32)


def _dot_nt(a, b):
    return lax.dot_general(a, b, (((1,), (1,)), ((), ())), preferred_element_type=F32)


def _dot_tn(a, b):
    return lax.dot_general(a, b, (((0,), (0,)), ((), ())), preferred_element_type=F32)


def _sigmoid(z):
    return 1.0 / (1.0 + jnp.exp(-z))


def _params(n_grid, vmem=VMEM_LIMIT):
    return pltpu.CompilerParams(dimension_semantics=("arbitrary",) * n_grid, vmem_limit_bytes=vmem)


def _position():
    return lax.axis_index("x"), lax.axis_index("y"), lax.axis_index("c")


def _flip(pos, k):
    x, y, c = pos
    return (1 - x if k & 4 else x, 1 - y if k & 2 else y, 1 - c if k & 1 else c)


def _slot(pos):
    return 4 * pos[0] + 2 * pos[1] + pos[2]


def _all_gather(name, arrs):
    n = len(arrs)

    def body(*refs):
        ins, outs = refs[:n], refs[n : 2 * n]
        send_sems, recv_sems, local_sems = refs[2 * n :]
        me = _position()
        sibling = _flip(me, 1)
        chips = [2, 4, 6]

        def copy(a, k, block, to, src=None):
            return pltpu.make_async_remote_copy(
                src_ref=outs[a].at[_slot(block)] if src is None else src,
                dst_ref=outs[a].at[_slot(block)],
                send_sem=send_sems.at[a * 7 + k],
                recv_sem=recv_sems.at[a * 7 + k],
                device_id=to,
                device_id_type=MESH,
            )

        mine = [pltpu.make_async_copy(ins[a], outs[a].at[_slot(me)], local_sems.at[a]) for a in range(n)]
        for cp in mine:
            cp.start()
        first = []
        for a in range(n):
            first.append(copy(a, 0, me, sibling, src=ins[a]))
            first += [copy(a, 1 + j, me, _flip(me, k), src=ins[a]) for j, k in enumerate(chips)]
        for cp in first:
            cp.start()
        passed = []
        for a in range(n):
            for j, k in enumerate(chips):
                copy(a, 1 + j, _flip(me, k), me).wait_recv()
                fwd = copy(a, 4 + j, _flip(me, k), sibling)
                fwd.start()
                passed.append(fwd)
        for a in range(n):
            copy(a, 0, sibling, me).wait_recv()
            for j, k in enumerate(chips):
                copy(a, 4 + j, _flip(sibling, k), me).wait_recv()
        for cp in first + passed:
            cp.wait_send()
        for cp in mine:
            cp.wait()

    any_spec = pl.BlockSpec(memory_space=pl.ANY)
    return pl.pallas_call(
        body,
        name=name,
        out_shape=[jax.ShapeDtypeStruct((N_DEV,) + a.shape, a.dtype) for a in arrs],
        in_specs=[any_spec] * n,
        out_specs=[any_spec] * n,
        scratch_shapes=[
            pltpu.SemaphoreType.DMA((7 * n,)),
            pltpu.SemaphoreType.DMA((7 * n,)),
            pltpu.SemaphoreType.DMA((n,)),
        ],
    )(*arrs)


def _all_gather_direct(name, arrs):
    n = len(arrs)

    def body(*refs):
        ins, outs = refs[:n], refs[n : 2 * n]
        send_sems, recv_sems, local_sems = refs[2 * n :]
        me = _position()
        mine = [pltpu.make_async_copy(ins[a], outs[a].at[_slot(me)], local_sems.at[a]) for a in range(n)]
        for cp in mine:
            cp.start()

        def copy(a, k, block):
            return pltpu.make_async_remote_copy(
                src_ref=ins[a],
                dst_ref=outs[a].at[_slot(block)],
                send_sem=send_sems.at[a * 7 + k - 1],
                recv_sem=recv_sems.at[a * 7 + k - 1],
                device_id=_flip(me, k),
                device_id_type=MESH,
            )

        sent = [copy(a, k, me) for a in range(n) for k in range(1, N_DEV)]
        for cp in sent:
            cp.start()
        for a in range(n):
            for k in range(1, N_DEV):
                copy(a, k, _flip(me, k)).wait_recv()
        for cp in sent:
            cp.wait_send()
        for cp in mine:
            cp.wait()

    any_spec = pl.BlockSpec(memory_space=pl.ANY)
    return pl.pallas_call(
        body,
        name=name,
        out_shape=[jax.ShapeDtypeStruct((N_DEV,) + a.shape, a.dtype) for a in arrs],
        in_specs=[any_spec] * n,
        out_specs=[any_spec] * n,
        scratch_shapes=[
            pltpu.SemaphoreType.DMA((7 * n,)),
            pltpu.SemaphoreType.DMA((7 * n,)),
            pltpu.SemaphoreType.DMA((n,)),
        ],
    )(*arrs)


CHIP_FLIPS = (2, 4, 6)


def _chip_of(pos):
    return 2 * pos[0] + pos[1]


def _chip_copies(srcs, lands, views, send_sems, recv_sems):
    me = _position()
    copies = []
    for a, (src, land) in enumerate(zip(srcs, lands)):
        for j, k in enumerate(CHIP_FLIPS):
            peer = _flip(me, k)
            copies.append(
                pltpu.make_async_remote_copy(
                    src_ref=views[a](src, peer),
                    dst_ref=land.at[j],
                    send_sem=send_sems.at[len(CHIP_FLIPS) * a + j],
                    recv_sem=recv_sems.at[len(CHIP_FLIPS) * a + j],
                    device_id=peer,
                    device_id_type=MESH,
                )
            )
    return copies


def _exchange(name, arrs, views, out_shapes, flips):
    n = len(arrs)
    nf = len(flips)

    def body(*refs):
        ins, outs = refs[:n], refs[n : 2 * n]
        send_sems, recv_sems = refs[2 * n :]
        me = _position()

        def copy(a, j):
            peer = _flip(me, flips[j])
            return pltpu.make_async_remote_copy(
                src_ref=views[a](ins[a], peer),
                dst_ref=outs[a].at[j],
                send_sem=send_sems.at[a * nf + j],
                recv_sem=recv_sems.at[a * nf + j],
                device_id=peer,
                device_id_type=MESH,
            )

        sent = [copy(a, j) for a in range(n) for j in range(nf)]
        for cp in sent:
            cp.start()
        for cp in sent:
            cp.wait_recv()
        for cp in sent:
            cp.wait_send()

    any_spec = pl.BlockSpec(memory_space=pl.ANY)
    return pl.pallas_call(
        body,
        name=name,
        out_shape=[jax.ShapeDtypeStruct((nf,) + tuple(sh), a.dtype) for sh, a in zip(out_shapes, arrs)],
        in_specs=[any_spec] * n,
        out_specs=[any_spec] * n,
        scratch_shapes=[pltpu.SemaphoreType.DMA((nf * n,)), pltpu.SemaphoreType.DMA((nf * n,))],
    )(*arrs)


def _pair_add(name, own, recv, index, grid, chip_axis, own_spec, recv_spec, mine_spec, mine_shape):
    def body(idx_ref, own_ref, recv_ref, sb_ref, mine_ref):
        total = own_ref[...] + recv_ref[...]
        sb_ref[...] = total.astype(BF16)

        @pl.when(pl.program_id(chip_axis) == idx_ref[1])
        def _():
            mine_ref[...] = total

    return pl.pallas_call(
        body,
        name=name,
        grid_spec=pltpu.PrefetchScalarGridSpec(
            num_scalar_prefetch=1, grid=grid, in_specs=[own_spec, recv_spec], out_specs=[recv_spec, mine_spec]
        ),
        out_shape=[jax.ShapeDtypeStruct(recv.shape, BF16), jax.ShapeDtypeStruct(mine_shape, F32)],
        compiler_params=_params(len(grid)),
    )(index, own, recv)


def _ada_mod(c_all, w_ada, b_cols):
    def body(c_ref, w_ref, b_ref, o_ref):
        o_ref[...] = _dot(c_ref[...].astype(BF16), w_ref[...].astype(BF16)) + b_ref[...]

    return pl.pallas_call(
        body,
        name="ada_mod",
        out_shape=jax.ShapeDtypeStruct((N_DEV, w_ada.shape[1]), F32),
        compiler_params=pltpu.CompilerParams(vmem_limit_bytes=VMEM_LIMIT),
    )(c_all, w_ada, b_cols)


PACK_GATE = 3 * ROW_SHARD
PACK_ROWS = PACK_GATE + 2 * GATE_SHARD


def _fwd_in(x, mod, norm_g, w_all, tables, pack):
    t = x.shape[0]
    tm = min(1024, t)
    tn = IN_W // 4
    n_i, n_j = t // tm, IN_W // tn
    w_tiles = w_all.reshape(n_j, tn, D)
    n_rope = (SEG_V - SEG_Q) // 128

    def gather_copies(pack_hbm, all_hbm, send_sems, recv_sems, arriving):
        me = _position()
        return [
            pltpu.make_async_remote_copy(
                src_ref=pack_hbm,
                dst_ref=all_hbm.at[_slot(_flip(me, k) if arriving else me)],
                send_sem=send_sems.at[k - 1],
                recv_sem=recv_sems.at[k - 1],
                device_id=_flip(me, k),
                device_id_type=MESH,
            )
            for k in range(1, N_DEV)
        ]

    def body(x_ref, mod_ref, ng_ref, w_ref, cos_ref, s1_ref, s2_ref, pack_hbm, proj_ref, h_ref, all_hbm,
             send_sems, recv_sems, local_sem):
        i = pl.program_id(0)
        j = pl.program_id(1)
        own = pltpu.make_async_copy(pack_hbm, all_hbm.at[_slot(_position())], local_sem)

        @pl.when((i == 0) & (j == 0))
        def _():
            own.start()
            for cp in gather_copies(pack_hbm, all_hbm, send_sems, recv_sems, arriving=False):
                cp.start()

        @pl.when(j == 0)
        def _():
            xv = x_ref[...]
            r = lax.rsqrt(jnp.mean(xv * xv, axis=-1, keepdims=True) + NORM_EPS)
            shift = mod_ref[:, 0:D]
            scale = mod_ref[:, D : 2 * D]
            h_ref[...] = (((xv * r) * ng_ref[...]) * (1.0 + scale) + shift).astype(BF16)

        proj_ref[...] = _dot_nt(h_ref[...], w_ref[...])

        @pl.when(j == 0)
        def _():
            cos, s1, s2 = cos_ref[...], s1_ref[...], s2_ref[...]
            for g in range(n_rope):
                cols = slice(128 * g, 128 * (g + 1))
                proj_ref[:, cols] = _rope(proj_ref[:, cols], cos, s1, s2)

        @pl.when((i == n_i - 1) & (j == n_j - 1))
        def _():
            for cp in gather_copies(pack_hbm, all_hbm, send_sems, recv_sems, arriving=True):
                cp.wait_recv()
            for cp in gather_copies(pack_hbm, all_hbm, send_sems, recv_sems, arriving=False):
                cp.wait_send()
            own.wait()

    tab = pl.BlockSpec((tm, 128), lambda i, j: (i, 0))
    any_spec = pl.BlockSpec(memory_space=pl.ANY)
    return pl.pallas_call(
        body,
        name="fwd_in",
        grid=(n_i, n_j),
        in_specs=[
            pl.BlockSpec((tm, D), lambda i, j: (i, 0)),
            pl.BlockSpec((1, 3 * D), lambda i, j: (0, 0)),
            pl.BlockSpec((1, D), lambda i, j: (0, 0)),
            pl.BlockSpec((None, tn, D), lambda i, j: (j, 0, 0)),
            tab, tab, tab, any_spec,
        ],
        out_specs=[pl.BlockSpec((tm, tn), lambda i, j: (i, j)), pl.BlockSpec((tm, D), lambda i, j: (i, 0)), any_spec],
        out_shape=[jax.ShapeDtypeStruct((t, IN_W), F32), jax.ShapeDtypeStruct((t, D), BF16),
                   jax.ShapeDtypeStruct((N_DEV,) + pack.shape, pack.dtype)],
        scratch_shapes=[pltpu.SemaphoreType.DMA((N_DEV - 1,)), pltpu.SemaphoreType.DMA((N_DEV - 1,)), pltpu.SemaphoreType.DMA],
        compiler_params=_params(2),
    )(x, mod, norm_g, w_tiles, *tables, pack)


def _rope_tables(positions):
    half = ROT_DIM // 2
    inv_freq = ROPE_THETA ** (-jnp.arange(0, ROT_DIM, 2, dtype=F32) / ROT_DIM)
    pos = positions.astype(F32)
    lane = np.arange(128) % HEAD_DIM
    freq_lane = jnp.where(lane < ROT_DIM, jnp.tile(inv_freq, 128 // half), 0.0)
    ang = pos[:, None] * freq_lane[None, :]
    sin = jnp.sin(ang)
    lane_tables = (jnp.cos(ang), jnp.where(lane < half, -sin, 0.0), jnp.where((lane >= half) & (lane < ROT_DIM), sin, 0.0))
    ang_t = inv_freq[:, None] * pos[None, :]
    return lane_tables, (jnp.cos(ang_t), jnp.sin(ang_t))


def _rope(tv, cos, s1, s2):
    cols = []
    for g in range(tv.shape[1] // 128):
        blk = tv[:, 128 * g : 128 * (g + 1)]
        cols.append(blk * cos + pltpu.roll(blk, 128 - 8, 1) * s1 + pltpu.roll(blk, 8, 1) * s2)
    return cols[0] if len(cols) == 1 else jnp.concatenate(cols, axis=1)


def _rope_transposed(gv, cos, s1, s2):
    cols = []
    for g in range(gv.shape[1] // 128):
        blk = gv[:, 128 * g : 128 * (g + 1)]
        cols.append(blk * cos + pltpu.roll(blk * s1, 8, 1) + pltpu.roll(blk * s2, 128 - 8, 1))
    return cols[0] if len(cols) == 1 else jnp.concatenate(cols, axis=1)


def _lane_lo():
    return lax.broadcasted_iota(jnp.int32, (1, 128), 1) < HEAD_DIM


def _row_lo():
    return lax.broadcasted_iota(jnp.int32, (128, 1), 0) < HEAD_DIM


def _split_heads(tile, j):
    lo = _lane_lo()
    parts = []
    for g in (2 * j, 2 * j + 1):
        blk = tile[:, 128 * g : 128 * (g + 1)]
        parts += [jnp.where(lo, blk, 0.0), jnp.where(lo, 0.0, blk)]
    return jnp.concatenate(parts, axis=0)


def _dup_lanes(kv, j):
    rolled = pltpu.roll(kv, HEAD_DIM, 1)
    lo = _lane_lo()
    return jnp.where(lo, kv, rolled) if j == 0 else jnp.where(lo, rolled, kv)


def _fold_lanes(acc):
    return acc + pltpu.roll(acc, HEAD_DIM, 1)


def _fill_bias_t(bias_ref):
    kj = lax.broadcasted_iota(jnp.int32, (256, 128), 0)
    qi = lax.broadcasted_iota(jnp.int32, (256, 128), 1)
    diff = qi + WINDOW - kj
    band = (diff >= 0) & (diff < WINDOW)
    bias_ref[0] = jnp.where(band, 0.0, -1e30)
    bias_ref[1] = jnp.where(band & (kj >= 128), 0.0, -1e30)


def _sink_row(sink_ref, pair, j):
    return jnp.concatenate([jnp.full((1, 128), sink_ref[0, 8 * pair + 4 * j + h], F32) for h in range(4)], axis=1)


def _softmax_t(q4, kw, bias4, sink):
    s = _dot_nt(kw, q4) + bias4
    m = jnp.maximum(jnp.max(s, axis=0, keepdims=True), sink)
    p = jnp.exp(s - m)
    e_sink = jnp.exp(sink - m)
    denom = jnp.sum(p, axis=0, keepdims=True) + e_sink
    return p, e_sink, denom


def _attn_specs(t, tq):
    return dict(
        q=pl.BlockSpec((tq, 512), lambda p, i: (i, p)),
        k=pl.BlockSpec((t, 128), lambda p, i: (0, SEG_K // 128 + p)),
        v=pl.BlockSpec((t, 128), lambda p, i: (0, SEG_V // 128 + p)),
        sink=pl.BlockSpec(memory_space=pltpu.SMEM),
        act=pl.BlockSpec((tq, 512), lambda p, i: (i, p)),
    )


def _attn_fwd(proj, sinks):
    t = proj.shape[0]
    tq = min(512, t)
    nb = tq // 128
    sp = _attn_specs(t, tq)

    def body(q_ref, k_ref, v_ref, sink_ref, o_ref, kd, vdt, bias_ref):
        pair = pl.program_id(0)
        i = pl.program_id(1)

        @pl.when(i == 0)
        def _():
            kv, vv = k_ref[...], v_ref[...]
            for j in range(2):
                kd[j, 0:128, :] = jnp.zeros((128, 128), BF16)
                kd[j, 128:, :] = _dup_lanes(kv, j).astype(BF16)
                vdt[j, :, 0:128] = jnp.zeros((128, 128), BF16)
                vdt[j, :, 128:] = _dup_lanes(vv, j).T.astype(BF16)
            _fill_bias_t(bias_ref)

        row_lo = _row_lo()

        def block(b, carry):
            r0 = pl.multiple_of(b * 128, 128)
            gb = i * nb + b
            keys = pl.ds(pl.multiple_of(gb * 128, 128), 256)
            rows = pl.ds(r0, 128)
            qt = q_ref[rows, :] * (1.0 / math.sqrt(HEAD_DIM))
            bias = bias_ref[jnp.where(gb == 0, 1, 0)]
            bias4 = jnp.concatenate([bias] * 4, axis=1)
            cols = []
            for j in range(2):
                q4 = _split_heads(qt, j).astype(BF16)
                p, _, denom = _softmax_t(q4, kd[j, keys, :], bias4, _sink_row(sink_ref, pair, j))
                ot = _dot(vdt[j, :, keys], p.astype(BF16)) * (1.0 / denom)
                for gi in range(2):
                    z = jnp.where(row_lo, ot[:, 256 * gi : 256 * gi + 128], ot[:, 256 * gi + 128 : 256 * gi + 256])
                    cols.append(z.T)
            o_ref[rows, :] = jnp.concatenate(cols, axis=1)
            return carry

        lax.fori_loop(0, nb, block, 0)

    return pl.pallas_call(
        body,
        name="attn_fwd",
        grid=(2, t // tq),
        in_specs=[sp["q"], sp["k"], sp["v"], sp["sink"]],
        out_specs=sp["act"],
        out_shape=jax.ShapeDtypeStruct((t, D), F32),
        scratch_shapes=[pltpu.VMEM((2, t + 128, 128), BF16), pltpu.VMEM((2, 128, t + 128), BF16), pltpu.VMEM((2, 256, 128), F32)],
        compiler_params=_params(2),
    )(proj, proj, proj, sinks)


def _attn_bwd(proj, o, d_o, tables, sinks):
    t = proj.shape[0]
    tq = min(512, t)
    nb = tq // 128
    n_tiles = t // tq
    (cos, s1, s2), (cos_t, sin_t) = tables
    sp = _attn_specs(t, tq)
    scale = 1.0 / math.sqrt(HEAD_DIM)
    half = ROT_DIM // 2

    def unrope_rows(z, c, s):
        out = []
        for base in (0, HEAD_DIM):
            g0, g1 = z[base : base + half], z[base + half : base + ROT_DIM]
            out += [g0 * c + g1 * s, g1 * c - g0 * s, z[base + ROT_DIM : base + HEAD_DIM]]
        return jnp.concatenate(out, axis=0)

    def body(q_ref, k_ref, v_ref, o_ref, do_ref, ct_ref, st_ref, ck, s1k, s2k, sink_ref, dq_ref, dk_ref, dv_ref, dsink_ref,
             kd, kdt, vd, bias_ref, dkacc, dvacc):
        pair = pl.program_id(0)
        i = pl.program_id(1)

        @pl.when(i == 0)
        def _():
            kv, vv = k_ref[...], v_ref[...]
            for j in range(2):
                kdup = _dup_lanes(kv, j)
                kd[j, 0:128, :] = jnp.zeros((128, 128), BF16)
                kd[j, 128:, :] = kdup.astype(BF16)
                kdt[j, :, 0:128] = jnp.zeros((128, 128), BF16)
                kdt[j, :, 128:] = kdup.T.astype(BF16)
                vd[j, 0:128, :] = jnp.zeros((128, 128), BF16)
                vd[j, 128:, :] = _dup_lanes(vv, j).astype(BF16)
            _fill_bias_t(bias_ref)
            dkacc[...] = jnp.zeros_like(dkacc)
            dvacc[...] = jnp.zeros_like(dvacc)
            dsink_ref[...] = jnp.zeros_like(dsink_ref)

        row_lo = _row_lo()
        ones = jnp.ones((8, 128), BF16)

        def block(b, carry):
            r0 = pl.multiple_of(b * 128, 128)
            gb = i * nb + b
            keys = pl.ds(pl.multiple_of(gb * 128, 128), 256)
            rows = pl.ds(r0, 128)
            qt = q_ref[rows, :] * scale
            dob = do_ref[rows, :]
            prod = dob * o_ref[rows, :]
            ct, st = ct_ref[:, rows], st_ref[:, rows]
            bias = bias_ref[jnp.where(gb == 0, 1, 0)]
            bias4 = jnp.concatenate([bias] * 4, axis=1)
            cols = []
            for j in range(2):
                q4 = _split_heads(qt, j).astype(BF16)
                kw = kd[j, keys, :]
                p, e_sink, denom = _softmax_t(q4, kw, bias4, _sink_row(sink_ref, pair, j))
                inv = 1.0 / denom
                prob = p * inv
                do4 = _split_heads(dob, j).astype(BF16)
                delta = _dot_nt(ones, _split_heads(prod, j).astype(BF16))[0:1, :]
                dp = _dot_nt(vd[j, keys, :], do4)
                ds = (prob * (dp - delta)).astype(BF16)
                dkacc[j, keys, :] = dkacc[j, keys, :] + _dot(ds, q4)
                dvacc[j, keys, :] = dvacc[j, keys, :] + _dot(prob.astype(BF16), do4)
                dqt = _dot(kdt[j, :, keys], ds) * scale
                dsink = -(e_sink * inv) * delta
                for h in range(4):
                    row = 4 * j + h
                    total = jnp.sum(dsink[:, 128 * h : 128 * (h + 1)], axis=1, keepdims=True)
                    dsink_ref[0, row : row + 1, :] = dsink_ref[0, row : row + 1, :] + jnp.broadcast_to(total, (1, 128))
                for gi in range(2):
                    z = jnp.where(row_lo, dqt[:, 256 * gi : 256 * gi + 128], dqt[:, 256 * gi + 128 : 256 * gi + 256])
                    cols.append(unrope_rows(z, ct, st).T)
            dq_ref[rows, :] = jnp.concatenate(cols, axis=1).astype(BF16)
            return carry

        lax.fori_loop(0, nb, block, 0)

        @pl.when(i == n_tiles - 1)
        def _():
            lo = _lane_lo()
            dk = jnp.where(lo, _fold_lanes(dkacc[0, 128:, :]), _fold_lanes(dkacc[1, 128:, :]))
            dk_ref[...] = _rope_transposed(dk, ck[...], s1k[...], s2k[...]).astype(BF16)
            dv_ref[...] = jnp.where(lo, _fold_lanes(dvacc[0, 128:, :]), _fold_lanes(dvacc[1, 128:, :])).astype(BF16)

    kv_out = pl.BlockSpec((t, 128), lambda p, i: (0, p))
    tab_t = pl.BlockSpec((half, tq), lambda p, i: (0, i))
    tab_k = pl.BlockSpec((t, 128), lambda p, i: (0, 0))
    return pl.pallas_call(
        body,
        name="attn_bwd",
        grid=(2, n_tiles),
        in_specs=[sp["q"], sp["k"], sp["v"], sp["act"], sp["act"], tab_t, tab_t, tab_k, tab_k, tab_k, sp["sink"]],
        out_specs=[sp["act"], kv_out, kv_out, pl.BlockSpec((1, 8, 128), lambda p, i: (p, 0, 0))],
        out_shape=[
            jax.ShapeDtypeStruct((t, D), BF16),
            jax.ShapeDtypeStruct((t, 256), BF16),
            jax.ShapeDtypeStruct((t, 256), BF16),
            jax.ShapeDtypeStruct((2, 8, 128), F32),
        ],
        scratch_shapes=[pltpu.VMEM((2, t + 128, 128), BF16), pltpu.VMEM((2, 128, t + 128), BF16), pltpu.VMEM((2, t + 128, 128), BF16),
                        pltpu.VMEM((2, 256, 128), F32), pltpu.VMEM((2, t + 128, 128), F32), pltpu.VMEM((2, t + 128, 128), F32)],
        compiler_params=_params(2),
    )(proj, proj, proj, o, d_o, cos_t, sin_t, cos, s1, s2, sinks)


HALF = 512


def _softplus(y):
    u = jnp.exp(-jnp.abs(y))
    w = 1.0 + u
    log1p = jnp.where(w == 1.0, u, jnp.log(w) * (u / jnp.where(w == 1.0, 1.0, w - 1.0)))
    return jnp.maximum(y, 0.0) + log1p


def _gate_block(w_ref, blk):
    return w_ref[:, :, 256 * blk : 256 * (blk + 1)].reshape(256, 256)


def _gate_matmul(act_bf, w_ref):
    return jnp.concatenate([_dot(act_bf[:, 256 * blk : 256 * (blk + 1)], _gate_block(w_ref, blk)) for blk in range(2)], axis=1)


def _gate_matmul_t(grad_bf, w_ref):
    return jnp.concatenate([_dot_nt(grad_bf[:, 256 * blk : 256 * (blk + 1)], _gate_block(w_ref, blk)) for blk in range(2)], axis=1)


def _conv_and_gates(xe_ref, tm, cw_ref, cb_ref, wa_ref, wx_ref, ba_ref, bx_ref, lam_ref, keep):
    xc = cb_ref[...] + sum(cw_ref[k : k + 1, :] * xe_ref[pl.ds(5 + k, tm), :] for k in range(CONV_WIDTH))
    xc_bf = xc.astype(BF16)
    r = _sigmoid(_gate_matmul(xc_bf, wa_ref) + ba_ref[...])
    ig = _sigmoid(_gate_matmul(xc_bf, wx_ref) + bx_ref[...])
    sp = _softplus(-lam_ref[...])
    log_a = -LRU_C * r * sp
    tanh = jnp.tanh(-log_a)
    mult = jnp.sqrt(2.0 * tanh / (1.0 + tanh))
    a = jnp.where(keep, jnp.exp(log_a), 0.0)
    mult = jnp.where(keep, mult, 1.0)
    return xc, xc_bf, r, ig, sp, a, mult


def _scan_rows(a_ref, b_ref, h_ref, carry_ref, tm, reverse):
    width = a_ref.shape[1]
    row = lax.broadcasted_iota(jnp.int32, (8, width), 0)

    def group(g, carry):
        gg = tm // 8 - 1 - g if reverse else g
        r0 = pl.multiple_of(gg * 8, 8)
        a = a_ref[pl.ds(r0, 8), :]
        b = b_ref[pl.ds(r0, 8), :]
        for s in (1, 2, 4):
            shift = 8 - s if reverse else s
            a_sh = pltpu.roll(a, shift, 0)
            b_sh = pltpu.roll(b, shift, 0)
            ok = (row < 8 - s) if reverse else (row >= s)
            b = jnp.where(ok, a * b_sh + b, b)
            a = jnp.where(ok, a * a_sh, a)
        h = a * carry + b
        h_ref[pl.ds(r0, 8), :] = h
        last = h[0:1, :] if reverse else h[7:8, :]
        return jnp.broadcast_to(last, (8, width))

    carry_ref[...] = lax.fori_loop(0, tm // 8, group, carry_ref[...])


def _rnn_specs(t, tm, order):
    n = t // tm
    return dict(
        tile=pl.BlockSpec((tm, HALF), lambda ch, i: (order(i, n), ch)),
        before=lambda col0: pl.BlockSpec((8, HALF), lambda ch, i: (jnp.maximum(order(i, n) * (tm // 8) - 1, 0), col0 + ch)),
        proj_tile=lambda col0: pl.BlockSpec((tm, HALF), lambda ch, i: (order(i, n), col0 + ch)),
        keep=pl.BlockSpec((tm, 1), lambda ch, i: (order(i, n), 0)),
        gates=lambda which: pl.BlockSpec((N_DEV, GATE_SHARD, HALF), lambda ch, i: (0, PACK_GATE // GATE_SHARD + which, ch)),
        gate_grads=pl.BlockSpec((2, 2, 256, 256), lambda ch, i: (0, ch, 0, 0)),
        vec=pl.BlockSpec((1, HALF), lambda ch, i: (0, ch)),
        conv=pl.BlockSpec((CONV_WIDTH, HALF), lambda ch, i: (0, ch)),
    )


def _rnn_fwd(proj, keep, w_all, rg_ba, rg_bx, rg_lambda, conv_w, conv_b):
    t = proj.shape[0]
    tm = min(512, t)
    sp = _rnn_specs(t, tm, lambda i, n: i)

    def body(xr_ref, xb_ref, keep_ref, wa_ref, wx_ref, ba_ref, bx_ref, lam_ref, cw_ref, cb_ref, hh_ref, xe, a_buf, b_buf, carry):
        i = pl.program_id(1)

        @pl.when(i == 0)
        def _():
            carry[...] = jnp.zeros_like(carry)

        xe[0:8, :] = jnp.where(i > 0, xb_ref[...], 0.0)
        xe[8:, :] = xr_ref[...]
        keep_rows = keep_ref[...] != 0.0
        xc, _, _, ig, _, a, mult = _conv_and_gates(xe, tm, cw_ref, cb_ref, wa_ref, wx_ref, ba_ref, bx_ref, lam_ref, keep_rows)
        a_buf[...] = a
        b_buf[...] = mult * (ig * xc)
        _scan_rows(a_buf, b_buf, hh_ref, carry, tm, reverse=False)

    return pl.pallas_call(
        body,
        name="rnn_fwd",
        grid=(2, t // tm),
        in_specs=[sp["proj_tile"](SEG_XR // HALF), sp["before"](SEG_XR // HALF), sp["keep"], sp["gates"](0), sp["gates"](1),
                  sp["vec"], sp["vec"], sp["vec"], sp["conv"], sp["vec"]],
        out_specs=sp["tile"],
        out_shape=jax.ShapeDtypeStruct((t, D), F32),
        scratch_shapes=[pltpu.VMEM((tm + 8, HALF), F32), pltpu.VMEM((tm, HALF), F32), pltpu.VMEM((tm, HALF), F32), pltpu.VMEM((8, HALF), F32)],
        compiler_params=_params(2),
    )(proj, proj, keep, w_all, w_all, rg_ba, rg_bx, rg_lambda, conv_w, conv_b)


def _rnn_bwd(proj, hh, d_hh, keep, w_all, rg_ba, rg_bx, rg_lambda, conv_w, conv_b):
    t = proj.shape[0]
    tm = min(512, t)
    n_tiles = t // tm
    sp = _rnn_specs(t, tm, lambda i, n: n - 1 - i)

    def body(xr_ref, xb_ref, hh_ref, hb_ref, dhh_ref, keep_ref, wa_ref, wx_ref, ba_ref, bx_ref, lam_ref, cw_ref, cb_ref,
             dxr_ref, gw_ref, gvec_ref, xe, he, ae, c_buf, g_buf, dxe, carry, a_next):
        i = pl.program_id(1)
        first_tile = i == n_tiles - 1

        @pl.when(i == 0)
        def _():
            carry[...] = jnp.zeros_like(carry)
            a_next[...] = jnp.zeros_like(a_next)
            dxe[tm:, :] = jnp.zeros((8, HALF), F32)
            gw_ref[...] = jnp.zeros_like(gw_ref)
            gvec_ref[...] = jnp.zeros_like(gvec_ref)

        xe[0:8, :] = jnp.where(first_tile, 0.0, xb_ref[...])
        xe[8:, :] = xr_ref[...]
        he[0:8, :] = jnp.where(first_tile, 0.0, hb_ref[...])
        he[8:, :] = hh_ref[...]
        keep_rows = keep_ref[...] != 0.0
        xc, xc_bf, r, ig, sp_, a, mult = _conv_and_gates(xe, tm, cw_ref, cb_ref, wa_ref, wx_ref, ba_ref, bx_ref, lam_ref, keep_rows)

        ae[0:tm, :] = a
        ae[tm:, :] = a_next[...]
        a_next[...] = jnp.broadcast_to(a[0:1, :], (8, HALF))
        c_buf[...] = ae[pl.ds(1, tm), :]
        g_buf[...] = dhh_ref[...]
        _scan_rows(c_buf, g_buf, g_buf, carry, tm, reverse=True)
        g = g_buf[...]

        h_prev = he[pl.ds(7, tm), :]
        bb = ig * xc
        d_ig = g * mult * xc
        d_xc = g * mult * ig
        d_log_a = jnp.where(keep_rows, g * h_prev * a - (g * bb) * (a * a) / mult, 0.0)
        d_r = d_log_a * (-LRU_C * sp_)
        d_sp = jnp.sum(d_log_a * (-LRU_C * r), axis=0, keepdims=True)
        d_zr = d_r * r * (1.0 - r)
        d_zi = d_ig * ig * (1.0 - ig)
        d_zr_bf = d_zr.astype(BF16)
        d_zi_bf = d_zi.astype(BF16)
        d_xc = d_xc + _gate_matmul_t(d_zr_bf, wa_ref) + _gate_matmul_t(d_zi_bf, wx_ref)
        for blk in range(2):
            cols = slice(256 * blk, 256 * (blk + 1))
            gw_ref[0, blk] = gw_ref[0, blk] + _dot_tn(xc_bf[:, cols], d_zr_bf[:, cols])
            gw_ref[1, blk] = gw_ref[1, blk] + _dot_tn(xc_bf[:, cols], d_zi_bf[:, cols])

        dxe[0:tm, :] = d_xc
        dxr = sum(cw_ref[k : k + 1, :] * dxe[pl.ds(3 - k, tm), :] for k in range(CONV_WIDTH))
        dxr_ref[...] = dxr.astype(BF16)
        dxe[tm:, :] = d_xc[0:8, :]

        lam = lam_ref[...]
        sums = [
            jnp.sum(d_zr, axis=0, keepdims=True),
            jnp.sum(d_zi, axis=0, keepdims=True),
            d_sp * (-_sigmoid(-lam)),
            jnp.sum(d_xc, axis=0, keepdims=True),
        ] + [jnp.sum(d_xc * xe[pl.ds(5 + k, tm), :], axis=0, keepdims=True) for k in range(CONV_WIDTH)]
        gvec_ref[...] = gvec_ref[...] + jnp.concatenate(sums, axis=0)

    return pl.pallas_call(
        body,
        name="rnn_bwd",
        grid=(2, n_tiles),
        in_specs=[
            sp["proj_tile"](SEG_XR // HALF), sp["before"](SEG_XR // HALF), sp["tile"], sp["before"](0), sp["tile"], sp["keep"],
            sp["gates"](0), sp["gates"](1), sp["vec"], sp["vec"], sp["vec"], sp["conv"], sp["vec"],
        ],
        out_specs=[sp["tile"], sp["gate_grads"], pl.BlockSpec((8, HALF), lambda ch, i: (0, ch))],
        out_shape=[
            jax.ShapeDtypeStruct((t, D), BF16),
            jax.ShapeDtypeStruct((2, 4, 256, 256), F32),
            jax.ShapeDtypeStruct((8, D), F32),
        ],
        scratch_shapes=[
            pltpu.VMEM((tm + 8, HALF), F32),
            pltpu.VMEM((tm + 8, HALF), F32),
            pltpu.VMEM((tm + 8, HALF), F32),
            pltpu.VMEM((tm, HALF), F32),
            pltpu.VMEM((tm, HALF), F32),
            pltpu.VMEM((tm + 8, HALF), F32),
            pltpu.VMEM((8, HALF), F32),
            pltpu.VMEM((8, HALF), F32),
        ],
        compiler_params=_params(2),
    )(proj, proj, hh, hh, d_hh, keep, w_all, w_all, rg_ba, rg_bx, rg_lambda, conv_w, conv_b)


def _mid(x, target, o, hh, proj, w_all, gate, final_g):
    t = x.shape[0]
    tm = min(256, t)
    n_tiles = t // tm
    full = pl.BlockSpec((tm, D), lambda i: (i, 0))
    half = lambda col: pl.BlockSpec((tm, 512), lambda i: (i, col))
    vec = pl.BlockSpec((1, D), lambda i: (0, 0))
    any_spec = pl.BlockSpec(memory_space=pl.ANY)

    def body(x_ref, tg_ref, o_ref, hh_ref, ga0, ga1, gr0, gr1, ma0, ma1, mr0, mr1, w_hbm, gate_ref, fg_ref,
             dres_ref, do_ref, dhh_ref, dga_ref, dgr_ref, dma_ref, dmr_ref, gw_hbm, vec_ref, w, gw, sem, w_sems):
        i = pl.program_id(0)

        @pl.when(i == 0)
        def _():
            loads = [
                pltpu.make_async_copy(
                    w_hbm.at[s, pl.ds(ROW_SHARD * k, ROW_SHARD), :], w.at[k, pl.ds(ROW_SHARD * s, ROW_SHARD), :], w_sems.at[k * N_DEV + s]
                )
                for k in range(3)
                for s in range(N_DEV)
            ]
            for cp in loads:
                cp.start()
            for cp in loads:
                cp.wait()
            gw[...] = jnp.zeros_like(gw)
            vec_ref[...] = jnp.zeros_like(vec_ref)

        cat = lambda a, b: jnp.concatenate([a[...], b[...]], axis=1)
        ga, gr, ma, mr = cat(ga0, ga1), cat(gr0, gr1), cat(ma0, ma1), cat(mr0, mr1)
        ov, hv, gate_v, fg = o_ref[...], hh_ref[...], gate_ref[...], fg_ref[...]
        sg_a = _sigmoid(ga)
        sg_r = _sigmoid(gr)
        silu_a = ga * sg_a
        silu_r = gr * sg_r
        ya = (ov * silu_a).astype(BF16)
        yr = (hv * silu_r).astype(BF16)
        pa = _dot(ya, w[0])
        pr = _dot(yr, w[1])
        sa = _sigmoid(ma)
        sr = _sigmoid(mr)
        merged = (sa * pa + sr * pr).astype(BF16)
        mo = _dot(merged, w[2])
        out = x_ref[...] + gate_v * mo
        r2 = lax.rsqrt(jnp.mean(out * out, axis=-1, keepdims=True) + NORM_EPS)
        outn = out * r2
        err = outn * fg - tg_ref[...]
        loss = 0.5 * jnp.sum(jnp.mean(err * err, axis=-1, keepdims=True), axis=0, keepdims=True)
        dy = err * (1.0 / D)
        g_final = jnp.sum(dy * outn, axis=0, keepdims=True)
        doutn = dy * fg
        dout = r2 * (doutn - outn * jnp.mean(doutn * outn, axis=-1, keepdims=True))
        dres_ref[...] = dout
        d_gate = jnp.sum(dout * mo, axis=0, keepdims=True)
        dmo = (dout * gate_v).astype(BF16)
        dmerged = _dot_nt(dmo, w[2])
        gw[2] = gw[2] + _dot_tn(merged, dmo)
        dpa = (dmerged * sa).astype(BF16)
        dpr = (dmerged * sr).astype(BF16)
        dma_ref[...] = (dmerged * pa * (sa * (1.0 - sa))).astype(BF16)
        dmr_ref[...] = (dmerged * pr * (sr * (1.0 - sr))).astype(BF16)
        dya = _dot_nt(dpa, w[0])
        dyr = _dot_nt(dpr, w[1])
        gw[0] = gw[0] + _dot_tn(ya, dpa)
        gw[1] = gw[1] + _dot_tn(yr, dpr)
        do_ref[...] = dya * silu_a
        dhh_ref[...] = dyr * silu_r
        dga_ref[...] = (dya * ov * (sg_a * (1.0 + ga * (1.0 - sg_a)))).astype(BF16)
        dgr_ref[...] = (dyr * hv * (sg_r * (1.0 + gr * (1.0 - sg_r)))).astype(BF16)
        zeros = jnp.zeros((5, D), F32)
        vec_ref[...] = vec_ref[...] + jnp.concatenate([g_final, d_gate, jnp.broadcast_to(loss, (1, D)), zeros], axis=0)

        @pl.when(i == n_tiles - 1)
        def _():
            cp = pltpu.make_async_copy(gw, gw_hbm, sem)
            cp.start()
            cp.wait()

    bf_tile = jax.ShapeDtypeStruct((t, D), BF16)
    f_tile = jax.ShapeDtypeStruct((t, D), F32)
    return pl.pallas_call(
        body,
        name="mid",
        grid=(n_tiles,),
        in_specs=[full, full, full, full,
                  half(SEG_GA // 512), half(SEG_GA // 512 + 1), half(SEG_GR // 512), half(SEG_GR // 512 + 1),
                  half(SEG_MA // 512), half(SEG_MA // 512 + 1), half(SEG_MR // 512), half(SEG_MR // 512 + 1),
                  any_spec, vec, vec],
        out_specs=[full, full, full, full, full, full, full, any_spec, pl.BlockSpec((8, D), lambda i: (0, 0))],
        out_shape=[f_tile, f_tile, f_tile, bf_tile, bf_tile, bf_tile, bf_tile,
                   jax.ShapeDtypeStruct((3, D, D), F32), jax.ShapeDtypeStruct((8, D), F32)],
        scratch_shapes=[pltpu.VMEM((3, D, D), BF16), pltpu.VMEM((3, D, D), F32), pltpu.SemaphoreType.DMA, pltpu.SemaphoreType.DMA((3 * N_DEV,))],
        compiler_params=_params(1),
    )(x, target, o, hh, proj, proj, proj, proj, proj, proj, proj, proj, w_all, gate, final_g)


_SEGMENTS = ((SEG_Q, 1024), (SEG_K, 256), (SEG_V, 256), (SEG_GA, 1024), (SEG_XR, 1024), (SEG_GR, 1024), (SEG_MA, 1024), (SEG_MR, 1024))


def _seg_specs(tm):
    return [pl.BlockSpec((tm, width), lambda i: (i, 0)) for _, width in _SEGMENTS]


def _bwd_dh(dsegs, w_all, x, dres, mod, norm_g, sum_w_bf):
    t = x.shape[0]
    tm = min(512, t)
    n_tiles = t // tm
    n_seg = len(_SEGMENTS)
    full = pl.BlockSpec((tm, D), lambda i: (i, 0))
    views = [lambda ref, peer: ref.at[_chip_of(peer)]]

    def body(*refs):
        segs = refs[:n_seg]
        w_hbm, x_ref, dres_ref, mod_ref, ng_ref, sum_hbm, gx_ref, vec_ref, land_hbm, w, sems, send_sems, recv_sems = refs[n_seg:]
        i = pl.program_id(0)

        @pl.when(i == 0)
        def _():
            for cp in _chip_copies([sum_hbm], [land_hbm], views, send_sems, recv_sems):
                cp.start()
            loads = [
                pltpu.make_async_copy(w_hbm.at[s], w.at[pl.ds(W_IN_SHARD * s, W_IN_SHARD), :], sems.at[s])
                for s in range(N_DEV)
            ]
            for cp in loads:
                cp.start()
            for cp in loads:
                cp.wait()
            vec_ref[...] = jnp.zeros_like(vec_ref)

        dh = sum(_dot(seg[...], w[start : start + width, :]) for seg, (start, width) in zip(segs, _SEGMENTS))
        xv = x_ref[...]
        r1 = lax.rsqrt(jnp.mean(xv * xv, axis=-1, keepdims=True) + NORM_EPS)
        xn = xv * r1
        ng = ng_ref[...]
        scale1 = 1.0 + mod_ref[:, D : 2 * D]
        d_shift = jnp.sum(dh, axis=0, keepdims=True)
        d_scale = jnp.sum(dh * (xn * ng), axis=0, keepdims=True)
        g_norm = jnp.sum(dh * xn * scale1, axis=0, keepdims=True)
        dxn = dh * (ng * scale1)
        dx = r1 * (dxn - xn * jnp.mean(dxn * xn, axis=-1, keepdims=True))
        gx_ref[...] = dres_ref[...] + dx
        vec_ref[...] = vec_ref[...] + jnp.concatenate([d_shift, d_scale, g_norm, jnp.zeros((5, D), F32)], axis=0)

        @pl.when(i == n_tiles - 1)
        def _():
            for cp in _chip_copies([sum_hbm], [land_hbm], views, send_sems, recv_sems):
                cp.wait_recv()
                cp.wait_send()

    any_spec = pl.BlockSpec(memory_space=pl.ANY)
    n_copies = len(CHIP_FLIPS)
    return pl.pallas_call(
        body,
        name="bwd_dh",
        grid=(n_tiles,),
        in_specs=_seg_specs(tm) + [any_spec, full, full, pl.BlockSpec((1, 3 * D), lambda i: (0, 0)), pl.BlockSpec((1, D), lambda i: (0, 0)),
                                   any_spec],
        out_specs=[full, pl.BlockSpec((8, D), lambda i: (0, 0)), any_spec],
        out_shape=[jax.ShapeDtypeStruct((t, D), F32), jax.ShapeDtypeStruct((8, D), F32),
                   jax.ShapeDtypeStruct((n_copies, W_IN_SHARD, D), BF16)],
        scratch_shapes=[pltpu.VMEM((IN_W, D), BF16), pltpu.SemaphoreType.DMA((N_DEV,)),
                        pltpu.SemaphoreType.DMA((n_copies,)), pltpu.SemaphoreType.DMA((n_copies,))],
        compiler_params=_params(1),
    )(*dsegs, w_all, x, dres, mod, norm_g, sum_w_bf)


def _bwd_gw(dsegs, h, sum_3_bf, sum_g_bf):
    t = h.shape[0]
    tm = min(512, t)
    n_tiles = t // tm
    n_seg = len(_SEGMENTS)
    views = [lambda ref, peer: ref.at[:, _chip_of(peer)], lambda ref, peer: ref.at[:, :, _chip_of(peer)]]

    def body(*refs):
        segs = refs[:n_seg]
        h_ref, s3_hbm, sg_hbm, out_hbm, l3_hbm, lg_hbm, acc, sem, send_sems, recv_sems = refs[n_seg:]
        i = pl.program_id(0)

        @pl.when(i == 0)
        def _():
            for cp in _chip_copies([s3_hbm, sg_hbm], [l3_hbm, lg_hbm], views, send_sems, recv_sems):
                cp.start()
            acc[...] = jnp.zeros_like(acc)

        hv = h_ref[...]
        for seg, (start, width) in zip(segs, _SEGMENTS):
            acc[start : start + width, :] = acc[start : start + width, :] + _dot_tn(seg[...], hv)

        @pl.when(i == n_tiles - 1)
        def _():
            cp = pltpu.make_async_copy(acc, out_hbm, sem)
            cp.start()
            cp.wait()
            for cp in _chip_copies([s3_hbm, sg_hbm], [l3_hbm, lg_hbm], views, send_sems, recv_sems):
                cp.wait_recv()
                cp.wait_send()

    any_spec = pl.BlockSpec(memory_space=pl.ANY)
    n_copies = 2 * len(CHIP_FLIPS)
    return pl.pallas_call(
        body,
        name="bwd_gw",
        grid=(n_tiles,),
        in_specs=_seg_specs(tm) + [pl.BlockSpec((tm, D), lambda i: (i, 0)), any_spec, any_spec],
        out_specs=[any_spec, any_spec, any_spec],
        out_shape=[jax.ShapeDtypeStruct((IN_W, D), F32),
                   jax.ShapeDtypeStruct((len(CHIP_FLIPS), 3, ROW_SHARD, D), BF16),
                   jax.ShapeDtypeStruct((len(CHIP_FLIPS), 2, 4, GATE_SHARD, 256), BF16)],
        scratch_shapes=[pltpu.VMEM((IN_W, D), F32), pltpu.SemaphoreType.DMA,
                        pltpu.SemaphoreType.DMA((n_copies,)), pltpu.SemaphoreType.DMA((n_copies,))],
        compiler_params=_params(1),
    )(*dsegs, h, sum_3_bf, sum_g_bf)


def _adamw(w, g, m, v):
    m = ADAM_B1 * m + (1.0 - ADAM_B1) * g
    v = ADAM_B2 * v + (1.0 - ADAM_B2) * (g * g)
    m_hat = m / (1.0 - ADAM_B1**ADAM_STEP)
    v_hat = v / (1.0 - ADAM_B2**ADAM_STEP)
    delta = -ADAM_LR * (m_hat / (jnp.sqrt(v_hat) + ADAM_EPS) + ADAM_WD * w)
    return delta, m, v


def _w_in_update(own, land, w_t, m_t, v_t):
    n_slots, rows, cols = land.shape
    rb = 208
    spec = pl.BlockSpec((rb, cols), lambda i: (i, 0))

    def body(o_ref, l_ref, w_ref, m_ref, v_ref, g_ref, d_ref, m2_ref, v2_ref):
        total = o_ref[...]
        for s in range(n_slots):
            total = total + l_ref[s].astype(F32)
        g_ref[...] = total
        d_ref[...], m2_ref[...], v2_ref[...] = _adamw(w_ref[...], total, m_ref[...], v_ref[...])

    shape = jax.ShapeDtypeStruct((rows, cols), F32)
    return pl.pallas_call(
        body,
        name="w_in_update",
        grid=(rows // rb,),
        in_specs=[spec, pl.BlockSpec((n_slots, rb, cols), lambda i: (0, i, 0)), spec, spec, spec],
        out_specs=[spec] * 4,
        out_shape=[shape] * 4,
        compiler_params=_params(1),
    )(own, land, w_t, m_t, v_t)


def _shard_update(own_3, land_3, own_g, land_g, weights, m_in, v_in):
    n = len(weights)

    def body(*refs):
        o3_ref, l3_ref, og_ref, lg_ref = refs[:4]
        w_refs, m_refs, v_refs = (refs[4 + n * k : 4 + n * (k + 1)] for k in range(3))
        outs = refs[4 + 3 * n :]
        for k in range(n):
            if k < 3:
                g = o3_ref[k]
                for slot in range(3):
                    g = g + l3_ref[slot, k].astype(F32)
            else:
                g = og_ref[k - 3]
                for slot in range(3):
                    g = g + lg_ref[slot, k - 3].astype(F32)
            d, m2, v2 = _adamw(w_refs[k][0], g, m_refs[k][0], v_refs[k][0])
            for which, val in enumerate((g, d, m2, v2)):
                outs[which * n + k][0] = val

    shapes = [jax.ShapeDtypeStruct(w.shape, F32) for w in weights]
    res = pl.pallas_call(
        body, name="shard_update", out_shape=shapes * 4, compiler_params=pltpu.CompilerParams(vmem_limit_bytes=VMEM_LIMIT)
    )(own_3, land_3, own_g, land_g, *weights, *m_in, *v_in)
    return [res[n * k : n * (k + 1)] for k in range(4)]


VEC_LAYOUT = (("b_ada", 3 * D), ("norm_g", D), ("attn_sinks", 128), ("conv_b", D), ("rg_ba", D), ("rg_bx", D), ("rg_lambda", D),
              ("final_g", D), ("conv_w", CONV_WIDTH * D), ("loss", 128))
VEC_OFFSET = {name: sum(w for _, w in VEC_LAYOUT[:k]) for k, (name, _) in enumerate(VEC_LAYOUT)}
VEC_ROWS = 112
VEC_WIDTH = VEC_ROWS * 128
assert sum(w for _, w in VEC_LAYOUT) <= VEC_WIDTH
N_REPLICATED = 8


def _small_update(vec_all, c_all_t, me, rep_w, rep_m, rep_v, ada, conv):
    n_ada = ada[0].shape[1]
    n_conv = conv[0].shape[1]

    def body(*refs):
        me_ref, vec_ref, ct_ref = refs[:3]
        w_refs, m_refs, v_refs = (refs[3 + N_REPLICATED * k : 3 + N_REPLICATED * (k + 1)] for k in range(3))
        pos = 3 + 3 * N_REPLICATED
        ada_refs, conv_refs = refs[pos : pos + 3], refs[pos + 3 : pos + 6]
        outs = refs[pos + 6 :]
        loss_ref, rep_outs = outs[0], outs[1 : 1 + 4 * N_REPLICATED]
        ada_outs, conv_outs = outs[1 + 4 * N_REPLICATED : 5 + 4 * N_REPLICATED], outs[5 + 4 * N_REPLICATED : 9 + 4 * N_REPLICATED]
        conv_buf = outs[9 + 4 * N_REPLICATED]

        total = vec_ref[0:1, :]
        for slot in range(1, N_DEV):
            total = total + vec_ref[slot : slot + 1, :]
        loss_ref[...] = total[:, VEC_OFFSET["loss"] : VEC_OFFSET["loss"] + 1]
        for k, (name, _) in enumerate(VEC_LAYOUT[:N_REPLICATED]):
            width = w_refs[k].shape[1]
            g = total[:, VEC_OFFSET[name] : VEC_OFFSET[name] + width]
            d, m2, v2 = _adamw(w_refs[k][...], g, m_refs[k][...], v_refs[k][...])
            for which, val in enumerate((g, d, m2, v2)):
                rep_outs[which * N_REPLICATED + k][...] = val

        device = me_ref[0]
        for k in range(CONV_WIDTH):
            conv_buf[k : k + 1, :] = total[:, VEC_OFFSET["conv_w"] + D * k : VEC_OFFSET["conv_w"] + D * (k + 1)]
        g_conv = conv_buf[:, pl.ds(pl.multiple_of(device * n_conv, 128), n_conv)]
        for ref, val in zip(conv_outs, (g_conv,) + _adamw(conv_refs[0][...], g_conv, conv_refs[1][...], conv_refs[2][...])):
            ref[...] = val

        dmod = vec_ref[:, pl.ds(pl.multiple_of(device * n_ada, 128), n_ada)]
        g_ada = _dot(ct_ref[...], dmod)
        for ref, val in zip(ada_outs, (g_ada,) + _adamw(ada_refs[0][...], g_ada, ada_refs[1][...], ada_refs[2][...])):
            ref[...] = val

    vmem = pl.BlockSpec(memory_space=pltpu.VMEM)
    rep_shapes = [jax.ShapeDtypeStruct(w.shape, F32) for w in rep_w]
    out_shape = ([jax.ShapeDtypeStruct((1, 1), F32)] + rep_shapes * 4 + [jax.ShapeDtypeStruct(ada[0].shape, F32)] * 4
                 + [jax.ShapeDtypeStruct(conv[0].shape, F32)] * 4)
    n_in = 3 + 3 * N_REPLICATED + 6
    res = pl.pallas_call(
        body,
        name="small_update",
        in_specs=[pl.BlockSpec(memory_space=pltpu.SMEM)] + [vmem] * (n_in - 1),
        out_specs=[vmem] * len(out_shape),
        out_shape=out_shape,
        scratch_shapes=[pltpu.VMEM((CONV_WIDTH, D), F32)],
        compiler_params=pltpu.CompilerParams(vmem_limit_bytes=VMEM_LIMIT),
    )(me, vec_all, c_all_t, *rep_w, *rep_m, *rep_v, *ada, *conv)
    loss = res[0]
    reps = [res[1 + N_REPLICATED * k : 1 + N_REPLICATED * (k + 1)] for k in range(4)]
    return loss, reps, res[1 + 4 * N_REPLICATED : 5 + 4 * N_REPLICATED], res[5 + 4 * N_REPLICATED : 9 + 4 * N_REPLICATED]


def kernel(x, c, positions, w_ada, b_ada, norm_g, w_in, attn_sinks, conv_w, conv_b, rg_wa, rg_ba, rg_wx, rg_bx, rg_lambda, w_attn_proj, w_rnn_proj, w_out, final_g, loss_target, m_w_ada, m_b_ada, m_norm_g, m_w_in, m_attn_sinks, m_conv_w, m_conv_b, m_rg_wa, m_rg_ba, m_rg_wx, m_rg_bx, m_rg_lambda, m_w_attn_proj, m_w_rnn_proj, m_w_out, m_final_g, v_w_ada, v_b_ada, v_norm_g, v_w_in, v_attn_sinks, v_conv_w, v_conv_b, v_rg_wa, v_rg_ba, v_rg_wx, v_rg_bx, v_rg_lambda, v_w_attn_proj, v_w_rnn_proj, v_w_out, v_final_g):
    my_x, my_y, my_c = _position()
    me = _slot((my_x, my_y, my_c))
    x2, target = x[0], loss_target[0]
    pos = positions[0]

    gate_rows = lambda w: w[0].transpose(1, 0, 2).reshape(GATE_SHARD, D)
    w_pack = jnp.concatenate([w_attn_proj[0], w_rnn_proj[0], w_out[0], gate_rows(rg_wa), gate_rows(rg_wx)], axis=0).astype(BF16)
    conv_rows = jnp.pad(conv_w[0], ((0, 8 - CONV_WIDTH), (0, 0)))
    w_in_t = w_in[0].T
    c_rows, conv_all, w_t_all = _all_gather("ag_fwd", [jnp.broadcast_to(c, (8, D)), conv_rows, w_in_t.astype(BF16)])
    c_all = c_rows[:, 0, :]
    conv_full = conv_all[:, :CONV_WIDTH, :].transpose(1, 0, 2).reshape(CONV_WIDTH, D)

    n_ada = w_ada.shape[2]
    b_cols = lax.dynamic_slice(b_ada, (0, me * n_ada), (1, n_ada))
    mod_cols = _ada_mod(c_all, w_ada[0], b_cols)
    (mod_all,) = _all_gather_direct("ag_mod", [mod_cols])
    mod = lax.dynamic_slice(mod_all, (0, me, 0), (N_DEV, 1, n_ada)).reshape(1, 3 * D)
    gate = mod[:, 2 * D :]

    tables = _rope_tables(pos)
    proj, h, w_all = _fwd_in(x2, mod, norm_g, w_t_all, tables[0], w_pack)
    o = _attn_fwd(proj, attn_sinks)
    keep = (pos != 0).astype(F32)[:, None]
    hh = _rnn_fwd(proj, keep, w_all, rg_ba, rg_bx, rg_lambda, conv_full, conv_b)

    dres, d_o, d_hh, d_ga, d_gr, d_ma, d_mr, gw3, mid_vec = _mid(x2, target, o, hh, proj, w_all, gate, final_g[None, :])
    d_q, d_k, d_v, d_sink = _attn_bwd(proj, o, d_o, tables, attn_sinks)
    d_xr, g_gates, rnn_vec = _rnn_bwd(proj, hh, d_hh, keep, w_all, rg_ba, rg_bx, rg_lambda, conv_full, conv_b)
    dsegs = (d_q, d_k, d_v, d_ga, d_xr, d_gr, d_ma, d_mr)

    index = jnp.stack([my_c, _chip_of((my_x, my_y))]).astype(jnp.int32)
    part_3 = gw3.reshape(3, 4, 2, ROW_SHARD, D)
    part_g = g_gates.reshape(2, 4, 4, 2, GATE_SHARD, 256)
    recv_3, recv_g = _exchange(
        "rs_pair_small",
        [part_3, part_g],
        [lambda ref, peer: ref.at[:, :, peer[2]], lambda ref, peer: ref.at[:, :, :, peer[2]]],
        [(3, 4, ROW_SHARD, D), (2, 4, 4, GATE_SHARD, 256)],
        flips=(1,),
    )
    sum_3_bf, own_3 = _pair_add(
        "pair_add_3", part_3, recv_3[0], index, (3, 4), 1,
        pl.BlockSpec((None, None, None, ROW_SHARD, D), lambda a, q, idx: (a, q, idx[0], 0, 0)),
        pl.BlockSpec((None, None, ROW_SHARD, D), lambda a, q, idx: (a, q, 0, 0)),
        pl.BlockSpec((None, ROW_SHARD, D), lambda a, q, idx: (a, 0, 0)), (3, ROW_SHARD, D),
    )
    sum_g_bf, own_g = _pair_add(
        "pair_add_g", part_g, recv_g[0], index, (4,), 0,
        pl.BlockSpec((2, 4, None, None, GATE_SHARD, 256), lambda q, idx: (0, 0, q, idx[0], 0, 0)),
        pl.BlockSpec((2, 4, None, GATE_SHARD, 256), lambda q, idx: (0, 0, q, 0, 0)),
        pl.BlockSpec((2, 4, GATE_SHARD, 256), lambda q, idx: (0, 0, 0, 0)), (2, 4, GATE_SHARD, 256),
    )
    g_w_t, land_3, land_g = _bwd_gw(dsegs, h, sum_3_bf, sum_g_bf)
    part_w = g_w_t.reshape(4, 2, W_IN_SHARD, D)
    (recv_w,) = _exchange("rs_pair_w", [part_w], [lambda ref, peer: ref.at[:, peer[2]]], [(4, W_IN_SHARD, D)], flips=(1,))
    rb = 208
    sum_w_bf, own_w = _pair_add(
        "pair_add_w", part_w, recv_w[0], index, (W_IN_SHARD // rb, 4), 1,
        pl.BlockSpec((None, None, rb, D), lambda r, q, idx: (q, idx[0], r, 0)),
        pl.BlockSpec((None, rb, D), lambda r, q, idx: (q, r, 0)),
        pl.BlockSpec((rb, D), lambda r, q, idx: (r, 0)), (W_IN_SHARD, D),
    )
    grad_x, dh_vec, land_w = _bwd_dh(dsegs, w_t_all, x2, dres, mod, norm_g, sum_w_bf)

    pieces = dict(
        b_ada=jnp.concatenate([dh_vec[0:1], dh_vec[1:2], mid_vec[1:2]], axis=1),
        norm_g=dh_vec[2:3],
        attn_sinks=jnp.pad(d_sink[:, :, 0].reshape(1, 16), ((0, 0), (0, 112))),
        conv_b=rnn_vec[3:4],
        rg_ba=rnn_vec[0:1],
        rg_bx=rnn_vec[1:2],
        rg_lambda=rnn_vec[2:3],
        final_g=mid_vec[0:1],
        conv_w=rnn_vec[4:8].reshape(1, CONV_WIDTH * D),
        loss=mid_vec[2:3, 0:128],
    )
    vec = jnp.concatenate([pieces[name] for name, _ in VEC_LAYOUT], axis=1)
    vec = jnp.pad(vec, ((0, 0), (0, VEC_WIDTH - vec.shape[1]))).reshape(VEC_ROWS, 128)
    (vec_all,) = _all_gather_direct("ag_vec", [vec])
    rep_w = [b_ada, norm_g, attn_sinks, conv_b, rg_ba, rg_bx, rg_lambda, final_g[None, :]]
    rep_m = [m_b_ada, m_norm_g, m_attn_sinks, m_conv_b, m_rg_ba, m_rg_bx, m_rg_lambda, m_final_g[None, :]]
    rep_v = [v_b_ada, v_norm_g, v_attn_sinks, v_conv_b, v_rg_ba, v_rg_bx, v_rg_lambda, v_final_g[None, :]]
    loss, reps, ada_res, conv_res = _small_update(
        vec_all.reshape(N_DEV, VEC_WIDTH), c_all.T, me.astype(jnp.int32).reshape(1), rep_w, rep_m, rep_v,
        (w_ada[0], m_w_ada[0], v_w_ada[0]), (conv_w[0], m_conv_w[0], v_conv_w[0]),
    )

    w_in_res = _w_in_update(own_w, land_w, w_in_t, m_w_in[0].T, v_w_in[0].T)
    shard_res = _shard_update(
        own_3, land_3, own_g, land_g,
        (w_attn_proj, w_rnn_proj, w_out, rg_wa, rg_wx),
        (m_w_attn_proj, m_w_rnn_proj, m_w_out, m_rg_wa, m_rg_wx),
        (v_w_attn_proj, v_w_rnn_proj, v_w_out, v_rg_wa, v_rg_wx),
    )

    def by_weight(k):
        b, n, s, cb, ba, bx, lam, fg = reps[k]
        ap, rp, out, wa, wx = shard_res[k]
        return [ada_res[k][None], b, n, w_in_res[k].T[None], s, conv_res[k][None], cb, wa, ba, wx, bx, lam, ap, rp, out, fg[0]]

    return (loss[0, 0], grad_x[None], *by_weight(0), *by_weight(1), *by_weight(2), *by_weight(3))
```

```python
import functools
import math

import jax
import jax.numpy as jnp
import numpy as np
from jax import lax
from jax.experimental import pallas as pl
from jax.experimental.pallas import tpu as pltpu

F32 = jnp.float32
BF16 = jnp.bfloat16

D = 1024
HEAD_DIM = 64
ROT_DIM = 16
ROPE_THETA = 500000.0
WINDOW = 128
LRU_C = 8.0
CONV_WIDTH = 4
NORM_EPS = 1e-6
IN_W = 6656
SEG_Q, SEG_K, SEG_V, SEG_GA, SEG_XR, SEG_GR, SEG_MA, SEG_MR = 0, 1024, 1280, 1536, 2560, 3584, 4608, 5632
N_DEV = 8
W_IN_SHARD = IN_W // N_DEV
ROW_SHARD = D // N_DEV
GATE_SHARD = 256 // N_DEV

ADAM_LR = 0.001
ADAM_B1 = 0.9
ADAM_B2 = 0.999
ADAM_EPS = 1e-08
ADAM_WD = 0.01
ADAM_STEP = 10

VMEM_LIMIT = 60 * 1024 * 1024
MESH = pl.DeviceIdType.MESH
AXES = ("x", "y", "c")


def _dot(a, b):
    return jnp.dot(a, b, preferred_element_type=F32)


def _dot_nt(a, b):
    return lax.dot_general(a, b, (((1,), (1,)), ((), ())), preferred_element_type=F32)


def _dot_tn(a, b):
    return lax.dot_general(a, b, (((0,), (0,)), ((), ())), preferred_element_type=F32)


def _sigmoid(z):
    return 1.0 / (1.0 + jnp.exp(-z))


def _params(n_grid, vmem=VMEM_LIMIT):
    return pltpu.CompilerParams(dimension_semantics=("arbitrary",) * n_grid, vmem_limit_bytes=vmem)


def _position():
    return lax.axis_index("x"), lax.axis_index("y"), lax.axis_index("c")


def _flip(pos, k):
    x, y, c = pos
    return (1 - x if k & 4 else x, 1 - y if k & 2 else y, 1 - c if k & 1 else c)


def _slot(pos):
    return 4 * pos[0] + 2 * pos[1] + pos[2]


def _all_gather(name, arrs):
    n = len(arrs)

    def body(*refs):
        ins, outs = refs[:n], refs[n : 2 * n]
        send_sems, recv_sems, local_sems = refs[2 * n :]
        me = _position()
        sibling = _flip(me, 1)
        chips = [2, 4, 6]

        def copy(a, k, block, to, src=None):
            return pltpu.make_async_remote_copy(
                src_ref=outs[a].at[_slot(block)] if src is None else src,
                dst_ref=outs[a].at[_slot(block)],
                send_sem=send_sems.at[a * 7 + k],
                recv_sem=recv_sems.at[a * 7 + k],
                device_id=to,
                device_id_type=MESH,
            )

        mine = [pltpu.make_async_copy(ins[a], outs[a].at[_slot(me)], local_sems.at[a]) for a in range(n)]
        for cp in mine:
            cp.start()
        first = []
        for a in range(n):
            first.append(copy(a, 0, me, sibling, src=ins[a]))
            first += [copy(a, 1 + j, me, _flip(me, k), src=ins[a]) for j, k in enumerate(chips)]
        for cp in first:
            cp.start()
        passed = []
        for a in range(n):
            for j, k in enumerate(chips):
                copy(a, 1 + j, _flip(me, k), me).wait_recv()
                fwd = copy(a, 4 + j, _flip(me, k), sibling)
                fwd.start()
                passed.append(fwd)
        for a in range(n):
            copy(a, 0, sibling, me).wait_recv()
            for j, k in enumerate(chips):
                copy(a, 4 + j, _flip(sibling, k), me).wait_recv()
        for cp in first + passed:
            cp.wait_send()
        for cp in mine:
            cp.wait()

    any_spec = pl.BlockSpec(memory_space=pl.ANY)
    return pl.pallas_call(
        body,
        name=name,
        out_shape=[jax.ShapeDtypeStruct((N_DEV,) + a.shape, a.dtype) for a in arrs],
        in_specs=[any_spec] * n,
        out_specs=[any_spec] * n,
        scratch_shapes=[
            pltpu.SemaphoreType.DMA((7 * n,)),
            pltpu.SemaphoreType.DMA((7 * n,)),
            pltpu.SemaphoreType.DMA((n,)),
        ],
    )(*arrs)


def _all_gather_direct(name, arrs):
    n = len(arrs)

    def body(*refs):
        ins, outs = refs[:n], refs[n : 2 * n]
        send_sems, recv_sems, local_sems = refs[2 * n :]
        me = _position()
        mine = [pltpu.make_async_copy(ins[a], outs[a].at[_slot(me)], local_sems.at[a]) for a in range(n)]
        for cp in mine:
            cp.start()

        def copy(a, k, block):
            return pltpu.make_async_remote_copy(
                src_ref=ins[a],
                dst_ref=outs[a].at[_slot(block)],
                send_sem=send_sems.at[a * 7 + k - 1],
                recv_sem=recv_sems.at[a * 7 + k - 1],
                device_id=_flip(me, k),
                device_id_type=MESH,
            )

        sent = [copy(a, k, me) for a in range(n) for k in range(1, N_DEV)]
        for cp in sent:
            cp.start()
        for a in range(n):
            for k in range(1, N_DEV):
                copy(a, k, _flip(me, k)).wait_recv()
        for cp in sent:
            cp.wait_send()
        for cp in mine:
            cp.wait()

    any_spec = pl.BlockSpec(memory_space=pl.ANY)
    return pl.pallas_call(
        body,
        name=name,
        out_shape=[jax.ShapeDtypeStruct((N_DEV,) + a.shape, a.dtype) for a in arrs],
        in_specs=[any_spec] * n,
        out_specs=[any_spec] * n,
        scratch_shapes=[
            pltpu.SemaphoreType.DMA((7 * n,)),
            pltpu.SemaphoreType.DMA((7 * n,)),
            pltpu.SemaphoreType.DMA((n,)),
        ],
    )(*arrs)


CHIP_FLIPS = (2, 4, 6)


def _chip_of(pos):
    return 2 * pos[0] + pos[1]


def _chip_copies(srcs, lands, views, send_sems, recv_sems):
    me = _position()
    copies = []
    for a, (src, land) in enumerate(zip(srcs, lands)):
        for j, k in enumerate(CHIP_FLIPS):
            peer = _flip(me, k)
            copies.append(
                pltpu.make_async_remote_copy(
                    src_ref=views[a](src, peer),
                    dst_ref=land.at[j],
                    send_sem=send_sems.at[len(CHIP_FLIPS) * a + j],
                    recv_sem=recv_sems.at[len(CHIP_FLIPS) * a + j],
                    device_id=peer,
                    device_id_type=MESH,
                )
            )
    return copies


def _exchange(name, arrs, views, out_shapes, flips):
    n = len(arrs)
    nf = len(flips)

    def body(*refs):
        ins, outs = refs[:n], refs[n : 2 * n]
        send_sems, recv_sems = refs[2 * n :]
        me = _position()

        def copy(a, j):
            peer = _flip(me, flips[j])
            return pltpu.make_async_remote_copy(
                src_ref=views[a](ins[a], peer),
                dst_ref=outs[a].at[j],
                send_sem=send_sems.at[a * nf + j],
                recv_sem=recv_sems.at[a * nf + j],
                device_id=peer,
                device_id_type=MESH,
            )

        sent = [copy(a, j) for a in range(n) for j in range(nf)]
        for cp in sent:
            cp.start()
        for cp in sent:
            cp.wait_recv()
        for cp in sent:
            cp.wait_send()

    any_spec = pl.BlockSpec(memory_space=pl.ANY)
    return pl.pallas_call(
        body,
        name=name,
        out_shape=[jax.ShapeDtypeStruct((nf,) + tuple(sh), a.dtype) for sh, a in zip(out_shapes, arrs)],
        in_specs=[any_spec] * n,
        out_specs=[any_spec] * n,
        scratch_shapes=[pltpu.SemaphoreType.DMA((nf * n,)), pltpu.SemaphoreType.DMA((nf * n,))],
    )(*arrs)


def _pair_add(name, own, recv, index, grid, chip_axis, own_spec, recv_spec, mine_spec, mine_shape):
    def body(idx_ref, own_ref, recv_ref, sb_ref, mine_ref):
        total = own_ref[...] + recv_ref[...].astype(F32)
        sb_ref[...] = total.astype(BF16)

        @pl.when(pl.program_id(chip_axis) == idx_ref[1])
        def _():
            mine_ref[...] = total

    return pl.pallas_call(
        body,
        name=name,
        grid_spec=pltpu.PrefetchScalarGridSpec(
            num_scalar_prefetch=1, grid=grid, in_specs=[own_spec, recv_spec], out_specs=[recv_spec, mine_spec]
        ),
        out_shape=[jax.ShapeDtypeStruct(recv.shape, BF16), jax.ShapeDtypeStruct(mine_shape, F32)],
        compiler_params=_params(len(grid)),
    )(index, own, recv)


def _ada_mod(c_all, w_ada, b_cols):
    def body(c_ref, w_ref, b_ref, o_ref):
        o_ref[...] = _dot(c_ref[...].astype(BF16), w_ref[...].astype(BF16)) + b_ref[...]

    return pl.pallas_call(
        body,
        name="ada_mod",
        out_shape=jax.ShapeDtypeStruct((N_DEV, w_ada.shape[1]), F32),
        compiler_params=pltpu.CompilerParams(vmem_limit_bytes=VMEM_LIMIT),
    )(c_all, w_ada, b_cols)


PACK_GATE = 3 * ROW_SHARD
PACK_ROWS = PACK_GATE + 2 * GATE_SHARD


def _fwd_in(x, mod, norm_g, w_all, tables, pack):
    t = x.shape[0]
    tm = min(1024, t)
    tn = IN_W // 4
    n_i, n_j = t // tm, IN_W // tn
    w_tiles = w_all.reshape(n_j, tn, D)
    n_rope = (SEG_V - SEG_Q) // 128

    def gather_copies(pack_hbm, all_hbm, send_sems, recv_sems, arriving):
        me = _position()
        return [
            pltpu.make_async_remote_copy(
                src_ref=pack_hbm,
                dst_ref=all_hbm.at[_slot(_flip(me, k) if arriving else me)],
                send_sem=send_sems.at[k - 1],
                recv_sem=recv_sems.at[k - 1],
                device_id=_flip(me, k),
                device_id_type=MESH,
            )
            for k in range(1, N_DEV)
        ]

    def body(x_ref, mod_ref, ng_ref, w_ref, cos_ref, s1_ref, s2_ref, pack_hbm, proj_ref, h_ref, all_hbm,
             send_sems, recv_sems, local_sem):
        i = pl.program_id(0)
        j = pl.program_id(1)
        own = pltpu.make_async_copy(pack_hbm, all_hbm.at[_slot(_position())], local_sem)

        @pl.when((i == 0) & (j == 0))
        def _():
            own.start()
            for cp in gather_copies(pack_hbm, all_hbm, send_sems, recv_sems, arriving=False):
                cp.start()

        @pl.when(j == 0)
        def _():
            xv = x_ref[...]
            r = lax.rsqrt(jnp.mean(xv * xv, axis=-1, keepdims=True) + NORM_EPS)
            shift = mod_ref[:, 0:D]
            scale = mod_ref[:, D : 2 * D]
            h_ref[...] = (((xv * r) * ng_ref[...]) * (1.0 + scale) + shift).astype(BF16)

        proj_ref[...] = _dot_nt(h_ref[...], w_ref[...])

        @pl.when(j == 0)
        def _():
            cos, s1, s2 = cos_ref[...], s1_ref[...], s2_ref[...]
            for g in range(n_rope):
                cols = slice(128 * g, 128 * (g + 1))
                proj_ref[:, cols] = _rope(proj_ref[:, cols], cos, s1, s2)

        @pl.when((i == n_i - 1) & (j == n_j - 1))
        def _():
            for cp in gather_copies(pack_hbm, all_hbm, send_sems, recv_sems, arriving=True):
                cp.wait_recv()
            for cp in gather_copies(pack_hbm, all_hbm, send_sems, recv_sems, arriving=False):
                cp.wait_send()
            own.wait()

    tab = pl.BlockSpec((tm, 128), lambda i, j: (i, 0))
    any_spec = pl.BlockSpec(memory_space=pl.ANY)
    return pl.pallas_call(
        body,
        name="fwd_in",
        grid=(n_i, n_j),
        in_specs=[
            pl.BlockSpec((tm, D), lambda i, j: (i, 0)),
            pl.BlockSpec((1, 3 * D), lambda i, j: (0, 0)),
            pl.BlockSpec((1, D), lambda i, j: (0, 0)),
            pl.BlockSpec((None, tn, D), lambda i, j: (j, 0, 0)),
            tab, tab, tab, any_spec,
        ],
        out_specs=[pl.BlockSpec((tm, tn), lambda i, j: (i, j)), pl.BlockSpec((tm, D), lambda i, j: (i, 0)), any_spec],
        out_shape=[jax.ShapeDtypeStruct((t, IN_W), F32), jax.ShapeDtypeStruct((t, D), BF16),
                   jax.ShapeDtypeStruct((N_DEV,) + pack.shape, pack.dtype)],
        scratch_shapes=[pltpu.SemaphoreType.DMA((N_DEV - 1,)), pltpu.SemaphoreType.DMA((N_DEV - 1,)), pltpu.SemaphoreType.DMA],
        compiler_params=_params(2),
    )(x, mod, norm_g, w_tiles, *tables, pack)


def _rope_tables(positions):
    half = ROT_DIM // 2
    inv_freq = ROPE_THETA ** (-jnp.arange(0, ROT_DIM, 2, dtype=F32) / ROT_DIM)
    pos = positions.astype(F32)
    lane = np.arange(128) % HEAD_DIM
    freq_lane = jnp.where(lane < ROT_DIM, jnp.tile(inv_freq, 128 // half), 0.0)
    ang = pos[:, None] * freq_lane[None, :]
    sin = jnp.sin(ang)
    lane_tables = (jnp.cos(ang), jnp.where(lane < half, -sin, 0.0), jnp.where((lane >= half) & (lane < ROT_DIM), sin, 0.0))
    ang_t = inv_freq[:, None] * pos[None, :]
    return lane_tables, (jnp.cos(ang_t), jnp.sin(ang_t))


def _rope(tv, cos, s1, s2):
    cols = []
    for g in range(tv.shape[1] // 128):
        blk = tv[:, 128 * g : 128 * (g + 1)]
        cols.append(blk * cos + pltpu.roll(blk, 128 - 8, 1) * s1 + pltpu.roll(blk, 8, 1) * s2)
    return cols[0] if len(cols) == 1 else jnp.concatenate(cols, axis=1)


def _rope_transposed(gv, cos, s1, s2):
    cols = []
    for g in range(gv.shape[1] // 128):
        blk = gv[:, 128 * g : 128 * (g + 1)]
        cols.append(blk * cos + pltpu.roll(blk * s1, 8, 1) + pltpu.roll(blk * s2, 128 - 8, 1))
    return cols[0] if len(cols) == 1 else jnp.concatenate(cols, axis=1)


def _lane_lo():
    return lax.broadcasted_iota(jnp.int32, (1, 128), 1) < HEAD_DIM


def _row_lo():
    return lax.broadcasted_iota(jnp.int32, (128, 1), 0) < HEAD_DIM


def _split_heads(tile, j):
    lo = _lane_lo()
    parts = []
    for g in (2 * j, 2 * j + 1):
        blk = tile[:, 128 * g : 128 * (g + 1)]
        parts += [jnp.where(lo, blk, 0.0), jnp.where(lo, 0.0, blk)]
    return jnp.concatenate(parts, axis=0)


def _dup_lanes(kv, j):
    rolled = pltpu.roll(kv, HEAD_DIM, 1)
    lo = _lane_lo()
    return jnp.where(lo, kv, rolled) if j == 0 else jnp.where(lo, rolled, kv)


def _fold_lanes(acc):
    return acc + pltpu.roll(acc, HEAD_DIM, 1)


def _fill_bias_t(bias_ref):
    kj = lax.broadcasted_iota(jnp.int32, (256, 128), 0)
    qi = lax.broadcasted_iota(jnp.int32, (256, 128), 1)
    diff = qi + WINDOW - kj
    band = (diff >= 0) & (diff < WINDOW)
    bias_ref[0] = jnp.where(band, 0.0, -1e30)
    bias_ref[1] = jnp.where(band & (kj >= 128), 0.0, -1e30)


def _sink_row(sink_ref, pair, j):
    return jnp.concatenate([jnp.full((1, 128), sink_ref[0, 8 * pair + 4 * j + h], F32) for h in range(4)], axis=1)


def _softmax_t(q4, kw, bias4, sink):
    s = _dot_nt(kw, q4) + bias4
    m = jnp.maximum(jnp.max(s, axis=0, keepdims=True), sink)
    p = jnp.exp(s - m)
    e_sink = jnp.exp(sink - m)
    denom = jnp.sum(p, axis=0, keepdims=True) + e_sink
    return p, e_sink, denom


def _attn_specs(t, tq):
    return dict(
        q=pl.BlockSpec((tq, 512), lambda p, i: (i, p)),
        k=pl.BlockSpec((t, 128), lambda p, i: (0, SEG_K // 128 + p)),
        v=pl.BlockSpec((t, 128), lambda p, i: (0, SEG_V // 128 + p)),
        sink=pl.BlockSpec(memory_space=pltpu.SMEM),
        act=pl.BlockSpec((tq, 512), lambda p, i: (i, p)),
    )


def _attn_fwd(proj, sinks):
    t = proj.shape[0]
    tq = min(512, t)
    nb = tq // 128
    sp = _attn_specs(t, tq)

    def body(q_ref, k_ref, v_ref, sink_ref, o_ref, kd, vdt, bias_ref):
        pair = pl.program_id(0)
        i = pl.program_id(1)

        @pl.when(i == 0)
        def _():
            kv, vv = k_ref[...], v_ref[...]
            for j in range(2):
                kd[j, 0:128, :] = jnp.zeros((128, 128), BF16)
                kd[j, 128:, :] = _dup_lanes(kv, j).astype(BF16)
                vdt[j, :, 0:128] = jnp.zeros((128, 128), BF16)
                vdt[j, :, 128:] = _dup_lanes(vv, j).T.astype(BF16)
            _fill_bias_t(bias_ref)

        row_lo = _row_lo()

        def block(b, carry):
            r0 = pl.multiple_of(b * 128, 128)
            gb = i * nb + b
            keys = pl.ds(pl.multiple_of(gb * 128, 128), 256)
            rows = pl.ds(r0, 128)
            qt = q_ref[rows, :] * (1.0 / math.sqrt(HEAD_DIM))
            bias = bias_ref[jnp.where(gb == 0, 1, 0)]
            bias4 = jnp.concatenate([bias] * 4, axis=1)
            cols = []
            for j in range(2):
                q4 = _split_heads(qt, j).astype(BF16)
                p, _, denom = _softmax_t(q4, kd[j, keys, :], bias4, _sink_row(sink_ref, pair, j))
                ot = _dot(vdt[j, :, keys], p.astype(BF16)) * (1.0 / denom)
                for gi in range(2):
                    z = jnp.where(row_lo, ot[:, 256 * gi : 256 * gi + 128], ot[:, 256 * gi + 128 : 256 * gi + 256])
                    cols.append(z.T)
            o_ref[rows, :] = jnp.concatenate(cols, axis=1)
            return carry

        lax.fori_loop(0, nb, block, 0, unroll=True)

    return pl.pallas_call(
        body,
        name="attn_fwd",
        grid=(2, t // tq),
        in_specs=[sp["q"], sp["k"], sp["v"], sp["sink"]],
        out_specs=sp["act"],
        out_shape=jax.ShapeDtypeStruct((t, D), F32),
        scratch_shapes=[pltpu.VMEM((2, t + 128, 128), BF16), pltpu.VMEM((2, 128, t + 128), BF16), pltpu.VMEM((2, 256, 128), F32)],
        compiler_params=_params(2),
    )(proj, proj, proj, sinks)


def _attn_bwd(proj, o, d_o, tables, sinks, part_3):
    t = proj.shape[0]
    tq = min(512, t)
    nb = tq // 128
    n_tiles = t // tq
    (cos, s1, s2), (cos_t, sin_t) = tables
    sp = _attn_specs(t, tq)
    scale = 1.0 / math.sqrt(HEAD_DIM)
    half = ROT_DIM // 2

    def unrope_rows(z, c, s):
        out = []
        for base in (0, HEAD_DIM):
            g0, g1 = z[base : base + half], z[base + half : base + ROT_DIM]
            out += [g0 * c + g1 * s, g1 * c - g0 * s, z[base + ROT_DIM : base + HEAD_DIM]]
        return jnp.concatenate(out, axis=0)

    def swap(part_hbm, recv_hbm, send_sem, recv_sem):
        sibling = _flip(_position(), 1)
        return pltpu.make_async_remote_copy(
            src_ref=part_hbm.at[:, :, sibling[2]], dst_ref=recv_hbm, send_sem=send_sem, recv_sem=recv_sem,
            device_id=sibling, device_id_type=MESH,
        )

    def body(q_ref, k_ref, v_ref, o_ref, do_ref, ct_ref, st_ref, ck, s1k, s2k, sink_ref, part_hbm,
             dq_ref, dk_ref, dv_ref, dsink_ref, recv_hbm, kd, kdt, vd, bias_ref, dkacc, dvacc, send_sem, recv_sem):
        pair = pl.program_id(0)
        i = pl.program_id(1)

        @pl.when((pair == 0) & (i == 0))
        def _():
            swap(part_hbm, recv_hbm, send_sem, recv_sem).start()

        @pl.when((pair == 1) & (i == n_tiles - 1))
        def _():
            swap(part_hbm, recv_hbm, send_sem, recv_sem).wait()

        @pl.when(i == 0)
        def _():
            kv, vv = k_ref[...], v_ref[...]
            for j in range(2):
                kdup = _dup_lanes(kv, j)
                kd[j, 0:128, :] = jnp.zeros((128, 128), BF16)
                kd[j, 128:, :] = kdup.astype(BF16)
                kdt[j, :, 0:128] = jnp.zeros((128, 128), BF16)
                kdt[j, :, 128:] = kdup.T.astype(BF16)
                vd[j, 0:128, :] = jnp.zeros((128, 128), BF16)
                vd[j, 128:, :] = _dup_lanes(vv, j).astype(BF16)
            _fill_bias_t(bias_ref)
            dkacc[...] = jnp.zeros_like(dkacc)
            dvacc[...] = jnp.zeros_like(dvacc)
            dsink_ref[...] = jnp.zeros_like(dsink_ref)

        row_lo = _row_lo()
        ones = jnp.ones((8, 128), BF16)

        def block(b, carry):
            r0 = pl.multiple_of(b * 128, 128)
            gb = i * nb + b
            keys = pl.ds(pl.multiple_of(gb * 128, 128), 256)
            rows = pl.ds(r0, 128)
            qt = q_ref[rows, :] * scale
            dob = do_ref[rows, :]
            prod = dob * o_ref[rows, :]
            ct, st = ct_ref[:, rows], st_ref[:, rows]
            bias = bias_ref[jnp.where(gb == 0, 1, 0)]
            bias4 = jnp.concatenate([bias] * 4, axis=1)
            cols = []
            for j in range(2):
                q4 = _split_heads(qt, j).astype(BF16)
                kw = kd[j, keys, :]
                p, e_sink, denom = _softmax_t(q4, kw, bias4, _sink_row(sink_ref, pair, j))
                inv = 1.0 / denom
                prob = p * inv
                do4 = _split_heads(dob, j).astype(BF16)
                delta = _dot_nt(ones, _split_heads(prod, j).astype(BF16))[0:1, :]
                dp = _dot_nt(vd[j, keys, :], do4)
                ds = (prob * (dp - delta)).astype(BF16)
                dkacc[j, keys, :] = dkacc[j, keys, :] + _dot(ds, q4)
                dvacc[j, keys, :] = dvacc[j, keys, :] + _dot(prob.astype(BF16), do4)
                dqt = _dot(kdt[j, :, keys], ds) * scale
                dsink = -(e_sink * inv) * delta
                for h in range(4):
                    row = 4 * j + h
                    total = jnp.sum(dsink[:, 128 * h : 128 * (h + 1)], axis=1, keepdims=True)
                    dsink_ref[0, row : row + 1, :] = dsink_ref[0, row : row + 1, :] + jnp.broadcast_to(total, (1, 128))
                for gi in range(2):
                    z = jnp.where(row_lo, dqt[:, 256 * gi : 256 * gi + 128], dqt[:, 256 * gi + 128 : 256 * gi + 256])
                    cols.append(unrope_rows(z, ct, st).T)
            dq_ref[rows, :] = jnp.concatenate(cols, axis=1).astype(BF16)
            return carry

        lax.fori_loop(0, nb, block, 0, unroll=True)

        @pl.when(i == n_tiles - 1)
        def _():
            lo = _lane_lo()
            dk = jnp.where(lo, _fold_lanes(dkacc[0, 128:, :]), _fold_lanes(dkacc[1, 128:, :]))
            dk_ref[...] = _rope_transposed(dk, ck[...], s1k[...], s2k[...]).astype(BF16)
            dv_ref[...] = jnp.where(lo, _fold_lanes(dvacc[0, 128:, :]), _fold_lanes(dvacc[1, 128:, :])).astype(BF16)

    kv_out = pl.BlockSpec((t, 128), lambda p, i: (0, p))
    tab_t = pl.BlockSpec((half, tq), lambda p, i: (0, i))
    tab_k = pl.BlockSpec((t, 128), lambda p, i: (0, 0))
    return pl.pallas_call(
        body,
        name="attn_bwd",
        grid=(2, n_tiles),
        in_specs=[sp["q"], sp["k"], sp["v"], sp["act"], sp["act"], tab_t, tab_t, tab_k, tab_k, tab_k, sp["sink"],
                  pl.BlockSpec(memory_space=pl.ANY)],
        out_specs=[sp["act"], kv_out, kv_out, pl.BlockSpec((1, 8, 128), lambda p, i: (p, 0, 0)), pl.BlockSpec(memory_space=pl.ANY)],
        out_shape=[
            jax.ShapeDtypeStruct((t, D), BF16),
            jax.ShapeDtypeStruct((t, 256), BF16),
            jax.ShapeDtypeStruct((t, 256), BF16),
            jax.ShapeDtypeStruct((2, 8, 128), F32),
            jax.ShapeDtypeStruct((3, 4, ROW_SHARD, D), F32),
        ],
        scratch_shapes=[pltpu.VMEM((2, t + 128, 128), BF16), pltpu.VMEM((2, 128, t + 128), BF16), pltpu.VMEM((2, t + 128, 128), BF16),
                        pltpu.VMEM((2, 256, 128), F32), pltpu.VMEM((2, t + 128, 128), F32), pltpu.VMEM((2, t + 128, 128), F32),
                        pltpu.SemaphoreType.DMA, pltpu.SemaphoreType.DMA],
        compiler_params=_params(2),
    )(proj, proj, proj, o, d_o, cos_t, sin_t, cos, s1, s2, sinks, part_3)


HALF = 512


def _softplus(y):
    u = jnp.exp(-jnp.abs(y))
    w = 1.0 + u
    log1p = jnp.where(w == 1.0, u, jnp.log(w) * (u / jnp.where(w == 1.0, 1.0, w - 1.0)))
    return jnp.maximum(y, 0.0) + log1p


def _gate_block(w_ref, blk):
    return w_ref[:, :, 256 * blk : 256 * (blk + 1)].reshape(256, 256)


def _gate_matmul(act_bf, w_ref):
    return jnp.concatenate([_dot(act_bf[:, 256 * blk : 256 * (blk + 1)], _gate_block(w_ref, blk)) for blk in range(2)], axis=1)


def _gate_matmul_t(grad_bf, w_ref):
    return jnp.concatenate([_dot_nt(grad_bf[:, 256 * blk : 256 * (blk + 1)], _gate_block(w_ref, blk)) for blk in range(2)], axis=1)


def _conv_and_gates(xe_ref, tm, cw_ref, cb_ref, wa_ref, wx_ref, ba_ref, bx_ref, lam_ref, keep):
    xc = cb_ref[...] + sum(cw_ref[k : k + 1, :] * xe_ref[pl.ds(5 + k, tm), :] for k in range(CONV_WIDTH))
    xc_bf = xc.astype(BF16)
    r = _sigmoid(_gate_matmul(xc_bf, wa_ref) + ba_ref[...])
    ig = _sigmoid(_gate_matmul(xc_bf, wx_ref) + bx_ref[...])
    sp = _softplus(-lam_ref[...])
    log_a = -LRU_C * r * sp
    tanh = jnp.tanh(-log_a)
    mult = jnp.sqrt(2.0 * tanh / (1.0 + tanh))
    a = jnp.where(keep, jnp.exp(log_a), 0.0)
    mult = jnp.where(keep, mult, 1.0)
    return xc, xc_bf, r, ig, sp, a, mult


def _scan_rows(a_ref, b_ref, h_ref, carry_ref, tm, reverse):
    width = a_ref.shape[1]
    row = lax.broadcasted_iota(jnp.int32, (8, width), 0)

    def group(g, carry):
        gg = tm // 8 - 1 - g if reverse else g
        r0 = pl.multiple_of(gg * 8, 8)
        a = a_ref[pl.ds(r0, 8), :]
        b = b_ref[pl.ds(r0, 8), :]
        for s in (1, 2, 4):
            shift = 8 - s if reverse else s
            a_sh = pltpu.roll(a, shift, 0)
            b_sh = pltpu.roll(b, shift, 0)
            ok = (row < 8 - s) if reverse else (row >= s)
            b = jnp.where(ok, a * b_sh + b, b)
            a = jnp.where(ok, a * a_sh, a)
        h = a * carry + b
        h_ref[pl.ds(r0, 8), :] = h
        last = h[0:1, :] if reverse else h[7:8, :]
        return jnp.broadcast_to(last, (8, width))

    carry_ref[...] = lax.fori_loop(0, tm // 8, group, carry_ref[...])


def _rnn_specs(t, tm, order):
    n = t // tm
    return dict(
        tile=pl.BlockSpec((tm, HALF), lambda ch, i: (order(i, n), ch)),
        before=lambda col0: pl.BlockSpec((8, HALF), lambda ch, i: (jnp.maximum(order(i, n) * (tm // 8) - 1, 0), col0 + ch)),
        proj_tile=lambda col0: pl.BlockSpec((tm, HALF), lambda ch, i: (order(i, n), col0 + ch)),
        keep=pl.BlockSpec((tm, 1), lambda ch, i: (order(i, n), 0)),
        gates=lambda which: pl.BlockSpec((N_DEV, GATE_SHARD, HALF), lambda ch, i: (0, PACK_GATE // GATE_SHARD + which, ch)),
        gate_grads=pl.BlockSpec((2, 2, 256, 256), lambda ch, i: (0, ch, 0, 0)),
        vec=pl.BlockSpec((1, HALF), lambda ch, i: (0, ch)),
        conv=pl.BlockSpec((CONV_WIDTH, HALF), lambda ch, i: (0, ch)),
    )


def _rnn_fwd(proj, keep, w_all, rg_ba, rg_bx, rg_lambda, conv_w, conv_b):
    t = proj.shape[0]
    tm = min(512, t)
    sp = _rnn_specs(t, tm, lambda i, n: i)

    def body(xr_ref, xb_ref, keep_ref, wa_ref, wx_ref, ba_ref, bx_ref, lam_ref, cw_ref, cb_ref, hh_ref, xe, a_buf, b_buf, carry):
        i = pl.program_id(1)

        @pl.when(i == 0)
        def _():
            carry[...] = jnp.zeros_like(carry)

        xe[0:8, :] = jnp.where(i > 0, xb_ref[...], 0.0)
        xe[8:, :] = xr_ref[...]
        keep_rows = keep_ref[...] != 0.0
        xc, _, _, ig, _, a, mult = _conv_and_gates(xe, tm, cw_ref, cb_ref, wa_ref, wx_ref, ba_ref, bx_ref, lam_ref, keep_rows)
        a_buf[...] = a
        b_buf[...] = mult * (ig * xc)
        _scan_rows(a_buf, b_buf, hh_ref, carry, tm, reverse=False)

    return pl.pallas_call(
        body,
        name="rnn_fwd",
        grid=(2, t // tm),
        in_specs=[sp["proj_tile"](SEG_XR // HALF), sp["before"](SEG_XR // HALF), sp["keep"], sp["gates"](0), sp["gates"](1),
                  sp["vec"], sp["vec"], sp["vec"], sp["conv"], sp["vec"]],
        out_specs=sp["tile"],
        out_shape=jax.ShapeDtypeStruct((t, D), F32),
        scratch_shapes=[pltpu.VMEM((tm + 8, HALF), F32), pltpu.VMEM((tm, HALF), F32), pltpu.VMEM((tm, HALF), F32), pltpu.VMEM((8, HALF), F32)],
        compiler_params=_params(2),
    )(proj, proj, keep, w_all, w_all, rg_ba, rg_bx, rg_lambda, conv_w, conv_b)


def _rnn_bwd(proj, hh, d_hh, keep, w_all, rg_ba, rg_bx, rg_lambda, conv_w, conv_b, sum_3_bf):
    t = proj.shape[0]
    tm = min(512, t)
    n_tiles = t // tm
    sp = _rnn_specs(t, tm, lambda i, n: n - 1 - i)

    def body(xr_ref, xb_ref, hh_ref, hb_ref, dhh_ref, keep_ref, wa_ref, wx_ref, ba_ref, bx_ref, lam_ref, cw_ref, cb_ref,
             s3_hbm, dxr_ref, gw_ref, gvec_ref, l3_hbm, xe, he, ae, c_buf, g_buf, dxe, carry, a_next, send_sems, recv_sems):
        ch = pl.program_id(0)
        i = pl.program_id(1)
        first_tile = i == n_tiles - 1
        views = [lambda ref, peer: ref.at[:, _chip_of(peer)]]

        @pl.when((ch == 0) & (i == 0))
        def _():
            for cp in _chip_copies([s3_hbm], [l3_hbm], views, send_sems, recv_sems):
                cp.start()

        @pl.when((ch == 1) & (i == n_tiles - 1))
        def _():
            for cp in _chip_copies([s3_hbm], [l3_hbm], views, send_sems, recv_sems):
                cp.wait_recv()
                cp.wait_send()

        @pl.when(i == 0)
        def _():
            carry[...] = jnp.zeros_like(carry)
            a_next[...] = jnp.zeros_like(a_next)
            dxe[tm:, :] = jnp.zeros((8, HALF), F32)
            gw_ref[...] = jnp.zeros_like(gw_ref)
            gvec_ref[...] = jnp.zeros_like(gvec_ref)

        xe[0:8, :] = jnp.where(first_tile, 0.0, xb_ref[...])
        xe[8:, :] = xr_ref[...]
        he[0:8, :] = jnp.where(first_tile, 0.0, hb_ref[...])
        he[8:, :] = hh_ref[...]
        keep_rows = keep_ref[...] != 0.0
        xc, xc_bf, r, ig, sp_, a, mult = _conv_and_gates(xe, tm, cw_ref, cb_ref, wa_ref, wx_ref, ba_ref, bx_ref, lam_ref, keep_rows)

        ae[0:tm, :] = a
        ae[tm:, :] = a_next[...]
        a_next[...] = jnp.broadcast_to(a[0:1, :], (8, HALF))
        c_buf[...] = ae[pl.ds(1, tm), :]
        g_buf[...] = dhh_ref[...]
        _scan_rows(c_buf, g_buf, g_buf, carry, tm, reverse=True)
        g = g_buf[...]

        h_prev = he[pl.ds(7, tm), :]
        bb = ig * xc
        d_ig = g * mult * xc
        d_xc = g * mult * ig
        d_log_a = jnp.where(keep_rows, g * h_prev * a - (g * bb) * (a * a) / mult, 0.0)
        d_r = d_log_a * (-LRU_C * sp_)
        d_sp = jnp.sum(d_log_a * (-LRU_C * r), axis=0, keepdims=True)
        d_zr = d_r * r * (1.0 - r)
        d_zi = d_ig * ig * (1.0 - ig)
        d_zr_bf = d_zr.astype(BF16)
        d_zi_bf = d_zi.astype(BF16)
        d_xc = d_xc + _gate_matmul_t(d_zr_bf, wa_ref) + _gate_matmul_t(d_zi_bf, wx_ref)
        for blk in range(2):
            cols = slice(256 * blk, 256 * (blk + 1))
            gw_ref[0, blk] = gw_ref[0, blk] + _dot_tn(xc_bf[:, cols], d_zr_bf[:, cols])
            gw_ref[1, blk] = gw_ref[1, blk] + _dot_tn(xc_bf[:, cols], d_zi_bf[:, cols])

        dxe[0:tm, :] = d_xc
        dxr = sum(cw_ref[k : k + 1, :] * dxe[pl.ds(3 - k, tm), :] for k in range(CONV_WIDTH))
        dxr_ref[...] = dxr.astype(BF16)
        dxe[tm:, :] = d_xc[0:8, :]

        lam = lam_ref[...]
        sums = [
            jnp.sum(d_zr, axis=0, keepdims=True),
            jnp.sum(d_zi, axis=0, keepdims=True),
            d_sp * (-_sigmoid(-lam)),
            jnp.sum(d_xc, axis=0, keepdims=True),
        ] + [jnp.sum(d_xc * xe[pl.ds(5 + k, tm), :], axis=0, keepdims=True) for k in range(CONV_WIDTH)]
        gvec_ref[...] = gvec_ref[...] + jnp.concatenate(sums, axis=0)

    return pl.pallas_call(
        body,
        name="rnn_bwd",
        grid=(2, n_tiles),
        in_specs=[
            sp["proj_tile"](SEG_XR // HALF), sp["before"](SEG_XR // HALF), sp["tile"], sp["before"](0), sp["tile"], sp["keep"],
            sp["gates"](0), sp["gates"](1), sp["vec"], sp["vec"], sp["vec"], sp["conv"], sp["vec"], pl.BlockSpec(memory_space=pl.ANY),
        ],
        out_specs=[sp["tile"], sp["gate_grads"], pl.BlockSpec((8, HALF), lambda ch, i: (0, ch)), pl.BlockSpec(memory_space=pl.ANY)],
        out_shape=[
            jax.ShapeDtypeStruct((t, D), BF16),
            jax.ShapeDtypeStruct((2, 4, 256, 256), F32),
            jax.ShapeDtypeStruct((8, D), F32),
            jax.ShapeDtypeStruct((len(CHIP_FLIPS), 3, ROW_SHARD, D), BF16),
        ],
        scratch_shapes=[
            pltpu.VMEM((tm + 8, HALF), F32),
            pltpu.VMEM((tm + 8, HALF), F32),
            pltpu.VMEM((tm + 8, HALF), F32),
            pltpu.VMEM((tm, HALF), F32),
            pltpu.VMEM((tm, HALF), F32),
            pltpu.VMEM((tm + 8, HALF), F32),
            pltpu.VMEM((8, HALF), F32),
            pltpu.VMEM((8, HALF), F32),
            pltpu.SemaphoreType.DMA((len(CHIP_FLIPS),)),
            pltpu.SemaphoreType.DMA((len(CHIP_FLIPS),)),
        ],
        compiler_params=_params(2),
    )(proj, proj, hh, hh, d_hh, keep, w_all, w_all, rg_ba, rg_bx, rg_lambda, conv_w, conv_b, sum_3_bf)


def _mid(x, target, o, hh, proj, w_all, gate, final_g):
    t = x.shape[0]
    tm = min(256, t)
    n_tiles = t // tm
    full = pl.BlockSpec((tm, D), lambda i: (i, 0))
    half = lambda col: pl.BlockSpec((tm, 512), lambda i: (i, col))
    vec = pl.BlockSpec((1, D), lambda i: (0, 0))
    any_spec = pl.BlockSpec(memory_space=pl.ANY)

    def body(x_ref, tg_ref, o_ref, hh_ref, ga0, ga1, gr0, gr1, ma0, ma1, mr0, mr1, w_hbm, gate_ref, fg_ref,
             dres_ref, do_ref, dhh_ref, dga_ref, dgr_ref, dma_ref, dmr_ref, gw_hbm, vec_ref, w, gw, sem, w_sems):
        i = pl.program_id(0)

        @pl.when(i == 0)
        def _():
            loads = [
                pltpu.make_async_copy(
                    w_hbm.at[s, pl.ds(ROW_SHARD * k, ROW_SHARD), :], w.at[k, pl.ds(ROW_SHARD * s, ROW_SHARD), :], w_sems.at[k * N_DEV + s]
                )
                for k in range(3)
                for s in range(N_DEV)
            ]
            for cp in loads:
                cp.start()
            for cp in loads:
                cp.wait()
            gw[...] = jnp.zeros_like(gw)
            vec_ref[...] = jnp.zeros_like(vec_ref)

        cat = lambda a, b: jnp.concatenate([a[...], b[...]], axis=1)
        ga, gr, ma, mr = cat(ga0, ga1), cat(gr0, gr1), cat(ma0, ma1), cat(mr0, mr1)
        ov, hv, gate_v, fg = o_ref[...], hh_ref[...], gate_ref[...], fg_ref[...]
        sg_a = _sigmoid(ga)
        sg_r = _sigmoid(gr)
        silu_a = ga * sg_a
        silu_r = gr * sg_r
        ya = (ov * silu_a).astype(BF16)
        yr = (hv * silu_r).astype(BF16)
        pa = _dot(ya, w[0])
        pr = _dot(yr, w[1])
        sa = _sigmoid(ma)
        sr = _sigmoid(mr)
        merged = (sa * pa + sr * pr).astype(BF16)
        mo = _dot(merged, w[2])
        out = x_ref[...] + gate_v * mo
        r2 = lax.rsqrt(jnp.mean(out * out, axis=-1, keepdims=True) + NORM_EPS)
        outn = out * r2
        err = outn * fg - tg_ref[...]
        loss = 0.5 * jnp.sum(jnp.mean(err * err, axis=-1, keepdims=True), axis=0, keepdims=True)
        dy = err * (1.0 / D)
        g_final = jnp.sum(dy * outn, axis=0, keepdims=True)
        doutn = dy * fg
        dout = r2 * (doutn - outn * jnp.mean(doutn * outn, axis=-1, keepdims=True))
        dres_ref[...] = dout
        d_gate = jnp.sum(dout * mo, axis=0, keepdims=True)
        dmo = (dout * gate_v).astype(BF16)
        dmerged = _dot_nt(dmo, w[2])
        gw[2] = gw[2] + _dot_tn(merged, dmo)
        dpa = (dmerged * sa).astype(BF16)
        dpr = (dmerged * sr).astype(BF16)
        dma_ref[...] = (dmerged * pa * (sa * (1.0 - sa))).astype(BF16)
        dmr_ref[...] = (dmerged * pr * (sr * (1.0 - sr))).astype(BF16)
        dya = _dot_nt(dpa, w[0])
        dyr = _dot_nt(dpr, w[1])
        gw[0] = gw[0] + _dot_tn(ya, dpa)
        gw[1] = gw[1] + _dot_tn(yr, dpr)
        do_ref[...] = dya * silu_a
        dhh_ref[...] = dyr * silu_r
        dga_ref[...] = (dya * ov * (sg_a * (1.0 + ga * (1.0 - sg_a)))).astype(BF16)
        dgr_ref[...] = (dyr * hv * (sg_r * (1.0 + gr * (1.0 - sg_r)))).astype(BF16)
        zeros = jnp.zeros((5, D), F32)
        vec_ref[...] = vec_ref[...] + jnp.concatenate([g_final, d_gate, jnp.broadcast_to(loss, (1, D)), zeros], axis=0)

        @pl.when(i == n_tiles - 1)
        def _():
            cp = pltpu.make_async_copy(gw, gw_hbm, sem)
            cp.start()
            cp.wait()

    bf_tile = jax.ShapeDtypeStruct((t, D), BF16)
    f_tile = jax.ShapeDtypeStruct((t, D), F32)
    return pl.pallas_call(
        body,
        name="mid",
        grid=(n_tiles,),
        in_specs=[full, full, full, full,
                  half(SEG_GA // 512), half(SEG_GA // 512 + 1), half(SEG_GR // 512), half(SEG_GR // 512 + 1),
                  half(SEG_MA // 512), half(SEG_MA // 512 + 1), half(SEG_MR // 512), half(SEG_MR // 512 + 1),
                  any_spec, vec, vec],
        out_specs=[full, full, full, full, full, full, full, any_spec, pl.BlockSpec((8, D), lambda i: (0, 0))],
        out_shape=[f_tile, f_tile, f_tile, bf_tile, bf_tile, bf_tile, bf_tile,
                   jax.ShapeDtypeStruct((3, D, D), F32), jax.ShapeDtypeStruct((8, D), F32)],
        scratch_shapes=[pltpu.VMEM((3, D, D), BF16), pltpu.VMEM((3, D, D), F32), pltpu.SemaphoreType.DMA, pltpu.SemaphoreType.DMA((3 * N_DEV,))],
        compiler_params=_params(1),
    )(x, target, o, hh, proj, proj, proj, proj, proj, proj, proj, proj, w_all, gate, final_g)


_SEGMENTS = ((SEG_Q, 1024), (SEG_K, 256), (SEG_V, 256), (SEG_GA, 1024), (SEG_XR, 1024), (SEG_GR, 1024), (SEG_MA, 1024), (SEG_MR, 1024))


def _seg_specs(tm):
    return [pl.BlockSpec((tm, width), lambda i: (i, 0)) for _, width in _SEGMENTS]


def _bwd_dh(dsegs, w_all, x, dres, mod, norm_g, sum_w_bf):
    t = x.shape[0]
    tm = min(512, t)
    n_tiles = t // tm
    n_seg = len(_SEGMENTS)
    full = pl.BlockSpec((tm, D), lambda i: (i, 0))
    views = [lambda ref, peer: ref.at[_chip_of(peer)]]

    def body(*refs):
        segs = refs[:n_seg]
        w_hbm, x_ref, dres_ref, mod_ref, ng_ref, sum_hbm, gx_ref, vec_ref, land_hbm, w, sems, send_sems, recv_sems = refs[n_seg:]
        i = pl.program_id(0)

        @pl.when(i == 0)
        def _():
            for cp in _chip_copies([sum_hbm], [land_hbm], views, send_sems, recv_sems):
                cp.start()
            loads = [
                pltpu.make_async_copy(w_hbm.at[s], w.at[pl.ds(W_IN_SHARD * s, W_IN_SHARD), :], sems.at[s])
                for s in range(N_DEV)
            ]
            for cp in loads:
                cp.start()
            for cp in loads:
                cp.wait()
            vec_ref[...] = jnp.zeros_like(vec_ref)

        dh = sum(_dot(seg[...], w[start : start + width, :]) for seg, (start, width) in zip(segs, _SEGMENTS))
        xv = x_ref[...]
        r1 = lax.rsqrt(jnp.mean(xv * xv, axis=-1, keepdims=True) + NORM_EPS)
        xn = xv * r1
        ng = ng_ref[...]
        scale1 = 1.0 + mod_ref[:, D : 2 * D]
        d_shift = jnp.sum(dh, axis=0, keepdims=True)
        d_scale = jnp.sum(dh * (xn * ng), axis=0, keepdims=True)
        g_norm = jnp.sum(dh * xn * scale1, axis=0, keepdims=True)
        dxn = dh * (ng * scale1)
        dx = r1 * (dxn - xn * jnp.mean(dxn * xn, axis=-1, keepdims=True))
        gx_ref[...] = dres_ref[...] + dx
        vec_ref[...] = vec_ref[...] + jnp.concatenate([d_shift, d_scale, g_norm, jnp.zeros((5, D), F32)], axis=0)

        @pl.when(i == n_tiles - 1)
        def _():
            for cp in _chip_copies([sum_hbm], [land_hbm], views, send_sems, recv_sems):
                cp.wait_recv()
                cp.wait_send()

    any_spec = pl.BlockSpec(memory_space=pl.ANY)
    n_copies = len(CHIP_FLIPS)
    return pl.pallas_call(
        body,
        name="bwd_dh",
        grid=(n_tiles,),
        in_specs=_seg_specs(tm) + [any_spec, full, full, pl.BlockSpec((1, 3 * D), lambda i: (0, 0)), pl.BlockSpec((1, D), lambda i: (0, 0)),
                                   any_spec],
        out_specs=[full, pl.BlockSpec((8, D), lambda i: (0, 0)), any_spec],
        out_shape=[jax.ShapeDtypeStruct((t, D), F32), jax.ShapeDtypeStruct((8, D), F32),
                   jax.ShapeDtypeStruct((n_copies, W_IN_SHARD, D), BF16)],
        scratch_shapes=[pltpu.VMEM((IN_W, D), BF16), pltpu.SemaphoreType.DMA((N_DEV,)),
                        pltpu.SemaphoreType.DMA((n_copies,)), pltpu.SemaphoreType.DMA((n_copies,))],
        compiler_params=_params(1),
    )(*dsegs, w_all, x, dres, mod, norm_g, sum_w_bf)


def _bwd_gw(dsegs, h, sum_g_bf):
    t = h.shape[0]
    tm = min(512, t)
    n_tiles = t // tm
    n_seg = len(_SEGMENTS)
    n_chips = N_DEV // 2
    views = [lambda ref, peer: ref.at[:, :, _chip_of(peer)]]

    def body(*refs):
        segs = refs[:n_seg]
        (h_ref, sg_hbm, own_hbm, recv_hbm, lg_hbm, acc, stage, own_sems, pair_send, pair_recv, send_sems, recv_sems) = refs[n_seg:]
        i = pl.program_id(0)

        @pl.when(i == 0)
        def _():
            for cp in _chip_copies([sg_hbm], [lg_hbm], views, send_sems, recv_sems):
                cp.start()
            acc[...] = jnp.zeros_like(acc)

        hv = h_ref[...]
        for seg, (start, width) in zip(segs, _SEGMENTS):
            acc[start : start + width, :] = acc[start : start + width, :] + _dot_tn(seg[...], hv)

        @pl.when(i == n_tiles - 1)
        def _():
            me = _position()
            sibling = _flip(me, 1)

            def rows(q, c):
                return pl.ds(pl.multiple_of(W_IN_SHARD * (2 * q + c), 8), W_IN_SHARD)

            def to_sibling(q):
                return pltpu.make_async_remote_copy(
                    src_ref=stage.at[q % 2], dst_ref=recv_hbm.at[q], send_sem=pair_send.at[q], recv_sem=pair_recv.at[q],
                    device_id=sibling, device_id_type=MESH,
                )

            kept = [pltpu.make_async_copy(acc.at[rows(q, me[2]), :], own_hbm.at[q], own_sems.at[q]) for q in range(n_chips)]
            for cp in kept:
                cp.start()
            for q in range(n_chips):
                if q >= 2:
                    to_sibling(q - 2).wait_send()
                stage[q % 2] = acc[rows(q, sibling[2]), :].astype(BF16)
                to_sibling(q).start()
            for q in range(n_chips - 2, n_chips):
                to_sibling(q).wait_send()
            for q in range(n_chips):
                to_sibling(q).wait_recv()
            for cp in kept:
                cp.wait()
            for cp in _chip_copies([sg_hbm], [lg_hbm], views, send_sems, recv_sems):
                cp.wait_recv()
                cp.wait_send()

    any_spec = pl.BlockSpec(memory_space=pl.ANY)
    n_copies = len(CHIP_FLIPS)
    return pl.pallas_call(
        body,
        name="bwd_gw",
        grid=(n_tiles,),
        in_specs=_seg_specs(tm) + [pl.BlockSpec((tm, D), lambda i: (i, 0)), any_spec],
        out_specs=[any_spec, any_spec, any_spec],
        out_shape=[jax.ShapeDtypeStruct((n_chips, W_IN_SHARD, D), F32),
                   jax.ShapeDtypeStruct((n_chips, W_IN_SHARD, D), BF16),
                   jax.ShapeDtypeStruct((len(CHIP_FLIPS), 2, 4, GATE_SHARD, 256), BF16)],
        scratch_shapes=[pltpu.VMEM((IN_W, D), F32), pltpu.VMEM((2, W_IN_SHARD, D), BF16),
                        pltpu.SemaphoreType.DMA((n_chips,)), pltpu.SemaphoreType.DMA((n_chips,)), pltpu.SemaphoreType.DMA((n_chips,)),
                        pltpu.SemaphoreType.DMA((n_copies,)), pltpu.SemaphoreType.DMA((n_copies,))],
        compiler_params=_params(1),
    )(*dsegs, h, sum_g_bf)


def _adamw(w, g, m, v):
    m = ADAM_B1 * m + (1.0 - ADAM_B1) * g
    v = ADAM_B2 * v + (1.0 - ADAM_B2) * (g * g)
    m_hat = m / (1.0 - ADAM_B1**ADAM_STEP)
    v_hat = v / (1.0 - ADAM_B2**ADAM_STEP)
    delta = -ADAM_LR * (m_hat / (jnp.sqrt(v_hat) + ADAM_EPS) + ADAM_WD * w)
    return delta, m, v


def _w_in_update(own, land, w_t, m_t, v_t):
    n_slots, rows, cols = land.shape
    rb = 208
    spec = pl.BlockSpec((rb, cols), lambda i: (i, 0))

    def body(o_ref, l_ref, w_ref, m_ref, v_ref, g_ref, d_ref, m2_ref, v2_ref):
        total = o_ref[...]
        for s in range(n_slots):
            total = total + l_ref[s].astype(F32)
        g_ref[...] = total
        d_ref[...], m2_ref[...], v2_ref[...] = _adamw(w_ref[...], total, m_ref[...], v_ref[...])

    shape = jax.ShapeDtypeStruct((rows, cols), F32)
    return pl.pallas_call(
        body,
        name="w_in_update",
        grid=(rows // rb,),
        in_specs=[spec, pl.BlockSpec((n_slots, rb, cols), lambda i: (0, i, 0)), spec, spec, spec],
        out_specs=[spec] * 4,
        out_shape=[shape] * 4,
        compiler_params=_params(1),
    )(own, land, w_t, m_t, v_t)


def _shard_update(own_3, land_3, own_g, land_g, weights, m_in, v_in):
    n = len(weights)

    def body(*refs):
        o3_ref, l3_ref, og_ref, lg_ref = refs[:4]
        w_refs, m_refs, v_refs = (refs[4 + n * k : 4 + n * (k + 1)] for k in range(3))
        outs = refs[4 + 3 * n :]
        for k in range(n):
            if k < 3:
                g = o3_ref[k]
                for slot in range(3):
                    g = g + l3_ref[slot, k].astype(F32)
            else:
                g = og_ref[k - 3]
                for slot in range(3):
                    g = g + lg_ref[slot, k - 3].astype(F32)
            d, m2, v2 = _adamw(w_refs[k][0], g, m_refs[k][0], v_refs[k][0])
            for which, val in enumerate((g, d, m2, v2)):
                outs[which * n + k][0] = val

    shapes = [jax.ShapeDtypeStruct(w.shape, F32) for w in weights]
    res = pl.pallas_call(
        body, name="shard_update", out_shape=shapes * 4, compiler_params=pltpu.CompilerParams(vmem_limit_bytes=VMEM_LIMIT)
    )(own_3, land_3, own_g, land_g, *weights, *m_in, *v_in)
    return [res[n * k : n * (k + 1)] for k in range(4)]


VEC_LAYOUT = (("b_ada", 3 * D), ("norm_g", D), ("attn_sinks", 128), ("conv_b", D), ("rg_ba", D), ("rg_bx", D), ("rg_lambda", D),
              ("final_g", D), ("conv_w", CONV_WIDTH * D), ("loss", 128))
VEC_OFFSET = {name: sum(w for _, w in VEC_LAYOUT[:k]) for k, (name, _) in enumerate(VEC_LAYOUT)}
VEC_ROWS = 112
VEC_WIDTH = VEC_ROWS * 128
assert sum(w for _, w in VEC_LAYOUT) <= VEC_WIDTH
N_REPLICATED = 8


def _small_update(vec_all, c_all_t, me, rep_w, rep_m, rep_v, ada, conv):
    n_ada = ada[0].shape[1]
    n_conv = conv[0].shape[1]

    def body(*refs):
        me_ref, vec_ref, ct_ref = refs[:3]
        w_refs, m_refs, v_refs = (refs[3 + N_REPLICATED * k : 3 + N_REPLICATED * (k + 1)] for k in range(3))
        pos = 3 + 3 * N_REPLICATED
        ada_refs, conv_refs = refs[pos : pos + 3], refs[pos + 3 : pos + 6]
        outs = refs[pos + 6 :]
        loss_ref, rep_outs = outs[0], outs[1 : 1 + 4 * N_REPLICATED]
        ada_outs, conv_outs = outs[1 + 4 * N_REPLICATED : 5 + 4 * N_REPLICATED], outs[5 + 4 * N_REPLICATED : 9 + 4 * N_REPLICATED]
        conv_buf = outs[9 + 4 * N_REPLICATED]

        total = vec_ref[0:1, :]
        for slot in range(1, N_DEV):
            total = total + vec_ref[slot : slot + 1, :]
        loss_ref[...] = total[:, VEC_OFFSET["loss"] : VEC_OFFSET["loss"] + 1]
        for k, (name, _) in enumerate(VEC_LAYOUT[:N_REPLICATED]):
            width = w_refs[k].shape[1]
            g = total[:, VEC_OFFSET[name] : VEC_OFFSET[name] + width]
            d, m2, v2 = _adamw(w_refs[k][...], g, m_refs[k][...], v_refs[k][...])
            for which, val in enumerate((g, d, m2, v2)):
                rep_outs[which * N_REPLICATED + k][...] = val

        device = me_ref[0]
        for k in range(CONV_WIDTH):
            conv_buf[k : k + 1, :] = total[:, VEC_OFFSET["conv_w"] + D * k : VEC_OFFSET["conv_w"] + D * (k + 1)]
        g_conv = conv_buf[:, pl.ds(pl.multiple_of(device * n_conv, 128), n_conv)]
        for ref, val in zip(conv_outs, (g_conv,) + _adamw(conv_refs[0][...], g_conv, conv_refs[1][...], conv_refs[2][...])):
            ref[...] = val

        dmod = vec_ref[:, pl.ds(pl.multiple_of(device * n_ada, 128), n_ada)]
        g_ada = _dot(ct_ref[...], dmod)
        for ref, val in zip(ada_outs, (g_ada,) + _adamw(ada_refs[0][...], g_ada, ada_refs[1][...], ada_refs[2][...])):
            ref[...] = val

    vmem = pl.BlockSpec(memory_space=pltpu.VMEM)
    rep_shapes = [jax.ShapeDtypeStruct(w.shape, F32) for w in rep_w]
    out_shape = ([jax.ShapeDtypeStruct((1, 1), F32)] + rep_shapes * 4 + [jax.ShapeDtypeStruct(ada[0].shape, F32)] * 4
                 + [jax.ShapeDtypeStruct(conv[0].shape, F32)] * 4)
    n_in = 3 + 3 * N_REPLICATED + 6
    res = pl.pallas_call(
        body,
        name="small_update",
        in_specs=[pl.BlockSpec(memory_space=pltpu.SMEM)] + [vmem] * (n_in - 1),
        out_specs=[vmem] * len(out_shape),
        out_shape=out_shape,
        scratch_shapes=[pltpu.VMEM((CONV_WIDTH, D), F32)],
        compiler_params=pltpu.CompilerParams(vmem_limit_bytes=VMEM_LIMIT),
    )(me, vec_all, c_all_t, *rep_w, *rep_m, *rep_v, *ada, *conv)
    loss = res[0]
    reps = [res[1 + N_REPLICATED * k : 1 + N_REPLICATED * (k + 1)] for k in range(4)]
    return loss, reps, res[1 + 4 * N_REPLICATED : 5 + 4 * N_REPLICATED], res[5 + 4 * N_REPLICATED : 9 + 4 * N_REPLICATED]


def kernel(x, c, positions, w_ada, b_ada, norm_g, w_in, attn_sinks, conv_w, conv_b, rg_wa, rg_ba, rg_wx, rg_bx, rg_lambda, w_attn_proj, w_rnn_proj, w_out, final_g, loss_target, m_w_ada, m_b_ada, m_norm_g, m_w_in, m_attn_sinks, m_conv_w, m_conv_b, m_rg_wa, m_rg_ba, m_rg_wx, m_rg_bx, m_rg_lambda, m_w_attn_proj, m_w_rnn_proj, m_w_out, m_final_g, v_w_ada, v_b_ada, v_norm_g, v_w_in, v_attn_sinks, v_conv_w, v_conv_b, v_rg_wa, v_rg_ba, v_rg_wx, v_rg_bx, v_rg_lambda, v_w_attn_proj, v_w_rnn_proj, v_w_out, v_final_g):
    my_x, my_y, my_c = _position()
    me = _slot((my_x, my_y, my_c))
    x2, target = x[0], loss_target[0]
    pos = positions[0]

    gate_rows = lambda w: w[0].transpose(1, 0, 2).reshape(GATE_SHARD, D)
    w_pack = jnp.concatenate([w_attn_proj[0], w_rnn_proj[0], w_out[0], gate_rows(rg_wa), gate_rows(rg_wx)], axis=0).astype(BF16)
    conv_rows = jnp.pad(conv_w[0], ((0, 8 - CONV_WIDTH), (0, 0)))
    w_in_t = w_in[0].T
    c_rows, conv_all, w_t_all = _all_gather("ag_fwd", [jnp.broadcast_to(c, (8, D)), conv_rows, w_in_t.astype(BF16)])
    c_all = c_rows[:, 0, :]
    conv_full = conv_all[:, :CONV_WIDTH, :].transpose(1, 0, 2).reshape(CONV_WIDTH, D)

    n_ada = w_ada.shape[2]
    b_cols = lax.dynamic_slice(b_ada, (0, me * n_ada), (1, n_ada))
    mod_cols = _ada_mod(c_all, w_ada[0], b_cols)
    (mod_all,) = _all_gather_direct("ag_mod", [mod_cols])
    mod = lax.dynamic_slice(mod_all, (0, me, 0), (N_DEV, 1, n_ada)).reshape(1, 3 * D)
    gate = mod[:, 2 * D :]

    tables = _rope_tables(pos)
    proj, h, w_all = _fwd_in(x2, mod, norm_g, w_t_all, tables[0], w_pack)
    o = _attn_fwd(proj, attn_sinks)
    keep = (pos != 0).astype(F32)[:, None]
    hh = _rnn_fwd(proj, keep, w_all, rg_ba, rg_bx, rg_lambda, conv_full, conv_b)

    dres, d_o, d_hh, d_ga, d_gr, d_ma, d_mr, gw3, mid_vec = _mid(x2, target, o, hh, proj, w_all, gate, final_g[None, :])
    index = jnp.stack([my_c, _chip_of((my_x, my_y))]).astype(jnp.int32)
    part_3 = gw3.reshape(3, 4, 2, ROW_SHARD, D)
    d_q, d_k, d_v, d_sink, recv_3 = _attn_bwd(proj, o, d_o, tables, attn_sinks, part_3)
    sum_3_bf, own_3 = _pair_add(
        "pair_add_3", part_3, recv_3, index, (3, 4), 1,
        pl.BlockSpec((None, None, None, ROW_SHARD, D), lambda a, q, idx: (a, q, idx[0], 0, 0)),
        pl.BlockSpec((None, None, ROW_SHARD, D), lambda a, q, idx: (a, q, 0, 0)),
        pl.BlockSpec((None, ROW_SHARD, D), lambda a, q, idx: (a, 0, 0)), (3, ROW_SHARD, D),
    )
    d_xr, g_gates, rnn_vec, land_3 = _rnn_bwd(proj, hh, d_hh, keep, w_all, rg_ba, rg_bx, rg_lambda, conv_full, conv_b, sum_3_bf)
    dsegs = (d_q, d_k, d_v, d_ga, d_xr, d_gr, d_ma, d_mr)
    part_g = g_gates.reshape(2, 4, 4, 2, GATE_SHARD, 256)
    (recv_g,) = _exchange("rs_pair_g", [part_g], [lambda ref, peer: ref.at[:, :, :, peer[2]]], [(2, 4, 4, GATE_SHARD, 256)], flips=(1,))
    sum_g_bf, own_g = _pair_add(
        "pair_add_g", part_g, recv_g[0], index, (4,), 0,
        pl.BlockSpec((2, 4, None, None, GATE_SHARD, 256), lambda q, idx: (0, 0, q, idx[0], 0, 0)),
        pl.BlockSpec((2, 4, None, GATE_SHARD, 256), lambda q, idx: (0, 0, q, 0, 0)),
        pl.BlockSpec((2, 4, GATE_SHARD, 256), lambda q, idx: (0, 0, 0, 0)), (2, 4, GATE_SHARD, 256),
    )
    part_w, recv_w, land_g = _bwd_gw(dsegs, h, sum_g_bf)
    rb = 208
    w_rows = pl.BlockSpec((None, rb, D), lambda r, q, idx: (q, r, 0))
    sum_w_bf, own_w = _pair_add(
        "pair_add_w", part_w, recv_w, index, (W_IN_SHARD // rb, 4), 1, w_rows, w_rows,
        pl.BlockSpec((rb, D), lambda r, q, idx: (r, 0)), (W_IN_SHARD, D),
    )
    grad_x, dh_vec, land_w = _bwd_dh(dsegs, w_t_all, x2, dres, mod, norm_g, sum_w_bf)

    pieces = dict(
        b_ada=jnp.concatenate([dh_vec[0:1], dh_vec[1:2], mid_vec[1:2]], axis=1),
        norm_g=dh_vec[2:3],
        attn_sinks=jnp.pad(d_sink[:, :, 0].reshape(1, 16), ((0, 0), (0, 112))),
        conv_b=rnn_vec[3:4],
        rg_ba=rnn_vec[0:1],
        rg_bx=rnn_vec[1:2],
        rg_lambda=rnn_vec[2:3],
        final_g=mid_vec[0:1],
        conv_w=rnn_vec[4:8].reshape(1, CONV_WIDTH * D),
        loss=mid_vec[2:3, 0:128],
    )
    vec = jnp.concatenate([pieces[name] for name, _ in VEC_LAYOUT], axis=1)
    vec = jnp.pad(vec, ((0, 0), (0, VEC_WIDTH - vec.shape[1]))).reshape(VEC_ROWS, 128)
    (vec_all,) = _all_gather_direct("ag_vec", [vec])
    rep_w = [b_ada, norm_g, attn_sinks, conv_b, rg_ba, rg_bx, rg_lambda, final_g[None, :]]
    rep_m = [m_b_ada, m_norm_g, m_attn_sinks, m_conv_b, m_rg_ba, m_rg_bx, m_rg_lambda, m_final_g[None, :]]
    rep_v = [v_b_ada, v_norm_g, v_attn_sinks, v_conv_b, v_rg_ba, v_rg_bx, v_rg_lambda, v_final_g[None, :]]
    loss, reps, ada_res, conv_res = _small_update(
        vec_all.reshape(N_DEV, VEC_WIDTH), c_all.T, me.astype(jnp.int32).reshape(1), rep_w, rep_m, rep_v,
        (w_ada[0], m_w_ada[0], v_w_ada[0]), (conv_w[0], m_conv_w[0], v_conv_w[0]),
    )

    w_in_res = _w_in_update(own_w, land_w, w_in_t, m_w_in[0].T, v_w_in[0].T)
    shard_res = _shard_update(
        own_3, land_3, own_g, land_g,
        (w_attn_proj, w_rnn_proj, w_out, rg_wa, rg_wx),
        (m_w_attn_proj, m_w_rnn_proj, m_w_out, m_rg_wa, m_rg_wx),
        (v_w_attn_proj, v_w_rnn_proj, v_w_out, v_rg_wa, v_rg_wx),
    )

    def by_weight(k):
        b, n, s, cb, ba, bx, lam, fg = reps[k]
        ap, rp, out, wa, wx = shard_res[k]
        return [ada_res[k][None], b, n, w_in_res[k].T[None], s, conv_res[k][None], cb, wa, ba, wx, bx, lam, ap, rp, out, fg[0]]

    return (loss[0, 0], grad_x[None], *by_weight(0), *by_weight(1), *by_weight(2), *by_weight(3))
```

```python
import functools
import math

import jax
import jax.numpy as jnp
import numpy as np
from jax import lax
from jax.experimental import pallas as pl
from jax.experimental.pallas import tpu as pltpu

F32 = jnp.float32
BF16 = jnp.bfloat16

D = 1024
HEAD_DIM = 64
ROT_DIM = 16
ROPE_THETA = 500000.0
WINDOW = 128
LRU_C = 8.0
CONV_WIDTH = 4
NORM_EPS = 1e-6
IN_W = 6656
SEG_Q, SEG_K, SEG_V, SEG_GA, SEG_XR, SEG_GR, SEG_MA, SEG_MR = 0, 1024, 1280, 1536, 2560, 3584, 4608, 5632
N_DEV = 8
W_IN_SHARD = IN_W // N_DEV
ROW_SHARD = D // N_DEV
GATE_SHARD = 256 // N_DEV

ADAM_LR = 0.001
ADAM_B1 = 0.9
ADAM_B2 = 0.999
ADAM_EPS = 1e-08
ADAM_WD = 0.01
ADAM_STEP = 10

VMEM_LIMIT = 60 * 1024 * 1024
MESH = pl.DeviceIdType.MESH
AXES = ("x", "y", "c")


def _dot(a, b):
    return jnp.dot(a, b, preferred_element_type=F32)


def _dot_nt(a, b):
    return lax.dot_general(a, b, (((1,), (1,)), ((), ())), preferred_element_type=F32)


def _dot_tn(a, b):
    return lax.dot_general(a, b, (((0,), (0,)), ((), ())), preferred_element_type=F32)


def _sigmoid(z):
    return 1.0 / (1.0 + jnp.exp(-z))


def _params(n_grid, vmem=VMEM_LIMIT):
    return pltpu.CompilerParams(dimension_semantics=("arbitrary",) * n_grid, vmem_limit_bytes=vmem)


def _position():
    return lax.axis_index("x"), lax.axis_index("y"), lax.axis_index("c")


def _flip(pos, k):
    x, y, c = pos
    return (1 - x if k & 4 else x, 1 - y if k & 2 else y, 1 - c if k & 1 else c)


def _slot(pos):
    return 4 * pos[0] + 2 * pos[1] + pos[2]


def _all_gather_direct(name, arrs):
    n = len(arrs)

    def body(*refs):
        ins, outs = refs[:n], refs[n : 2 * n]
        send_sems, recv_sems, local_sems = refs[2 * n :]
        me = _position()
        mine = [pltpu.make_async_copy(ins[a], outs[a].at[_slot(me)], local_sems.at[a]) for a in range(n)]
        for cp in mine:
            cp.start()

        def copy(a, k, block):
            return pltpu.make_async_remote_copy(
                src_ref=ins[a],
                dst_ref=outs[a].at[_slot(block)],
                send_sem=send_sems.at[a * 7 + k - 1],
                recv_sem=recv_sems.at[a * 7 + k - 1],
                device_id=_flip(me, k),
                device_id_type=MESH,
            )

        sent = [copy(a, k, me) for a in range(n) for k in range(1, N_DEV)]
        for cp in sent:
            cp.start()
        for a in range(n):
            for k in range(1, N_DEV):
                copy(a, k, _flip(me, k)).wait_recv()
        for cp in sent:
            cp.wait_send()
        for cp in mine:
            cp.wait()

    any_spec = pl.BlockSpec(memory_space=pl.ANY)
    return pl.pallas_call(
        body,
        name=name,
        out_shape=[jax.ShapeDtypeStruct((N_DEV,) + a.shape, a.dtype) for a in arrs],
        in_specs=[any_spec] * n,
        out_specs=[any_spec] * n,
        scratch_shapes=[
            pltpu.SemaphoreType.DMA((7 * n,)),
            pltpu.SemaphoreType.DMA((7 * n,)),
            pltpu.SemaphoreType.DMA((n,)),
        ],
    )(*arrs)


CHIP_FLIPS = (2, 4, 6)


def _chip_of(pos):
    return 2 * pos[0] + pos[1]


def _chip_copies(srcs, lands, views, send_sems, recv_sems):
    me = _position()
    copies = []
    for a, (src, land) in enumerate(zip(srcs, lands)):
        for j, k in enumerate(CHIP_FLIPS):
            peer = _flip(me, k)
            copies.append(
                pltpu.make_async_remote_copy(
                    src_ref=views[a](src, peer),
                    dst_ref=land.at[j],
                    send_sem=send_sems.at[len(CHIP_FLIPS) * a + j],
                    recv_sem=recv_sems.at[len(CHIP_FLIPS) * a + j],
                    device_id=peer,
                    device_id_type=MESH,
                )
            )
    return copies


def _exchange(name, arrs, views, out_shapes, flips):
    n = len(arrs)
    nf = len(flips)

    def body(*refs):
        ins, outs = refs[:n], refs[n : 2 * n]
        send_sems, recv_sems = refs[2 * n :]
        me = _position()

        def copy(a, j):
            peer = _flip(me, flips[j])
            return pltpu.make_async_remote_copy(
                src_ref=views[a](ins[a], peer),
                dst_ref=outs[a].at[j],
                send_sem=send_sems.at[a * nf + j],
                recv_sem=recv_sems.at[a * nf + j],
                device_id=peer,
                device_id_type=MESH,
            )

        sent = [copy(a, j) for a in range(n) for j in range(nf)]
        for cp in sent:
            cp.start()
        for cp in sent:
            cp.wait_recv()
        for cp in sent:
            cp.wait_send()

    any_spec = pl.BlockSpec(memory_space=pl.ANY)
    return pl.pallas_call(
        body,
        name=name,
        out_shape=[jax.ShapeDtypeStruct((nf,) + tuple(sh), a.dtype) for sh, a in zip(out_shapes, arrs)],
        in_specs=[any_spec] * n,
        out_specs=[any_spec] * n,
        scratch_shapes=[pltpu.SemaphoreType.DMA((nf * n,)), pltpu.SemaphoreType.DMA((nf * n,))],
    )(*arrs)


def _pair_add(name, own, recv, index, grid, chip_axis, own_spec, recv_spec, mine_spec, mine_shape):
    def body(idx_ref, own_ref, recv_ref, sb_ref, mine_ref):
        total = own_ref[...] + recv_ref[...].astype(F32)
        sb_ref[...] = total.astype(BF16)

        @pl.when(pl.program_id(chip_axis) == idx_ref[1])
        def _():
            mine_ref[...] = total

    return pl.pallas_call(
        body,
        name=name,
        grid_spec=pltpu.PrefetchScalarGridSpec(
            num_scalar_prefetch=1, grid=grid, in_specs=[own_spec, recv_spec], out_specs=[recv_spec, mine_spec]
        ),
        out_shape=[jax.ShapeDtypeStruct(recv.shape, BF16), jax.ShapeDtypeStruct(mine_shape, F32)],
        compiler_params=_params(len(grid)),
    )(index, own, recv)


def _ada_mod(c_all, w_ada, b_cols):
    def body(c_ref, w_ref, b_ref, o_ref):
        o_ref[...] = _dot(c_ref[...].astype(BF16), w_ref[...].astype(BF16)) + b_ref[...]

    return pl.pallas_call(
        body,
        name="ada_mod",
        out_shape=jax.ShapeDtypeStruct((N_DEV, w_ada.shape[1]), F32),
        compiler_params=pltpu.CompilerParams(vmem_limit_bytes=VMEM_LIMIT),
    )(c_all, w_ada, b_cols)


PACK_GATE = 3 * ROW_SHARD
PACK_ROWS = PACK_GATE + 2 * GATE_SHARD


def _fwd_in(x, mod, norm_g, w_shard, tables, pack, my_chip):
    t = x.shape[0]
    tm = min(1024, t)
    tn = IN_W // 4
    n_i, n_j = t // tm, IN_W // tn
    n_rope = (SEG_V - SEG_Q) // 128

    def pack_copies(pack_hbm, all_hbm, send_sems, recv_sems, arriving):
        me = _position()
        return [
            pltpu.make_async_remote_copy(
                src_ref=pack_hbm,
                dst_ref=all_hbm.at[_slot(_flip(me, k) if arriving else me)],
                send_sem=send_sems.at[k - 1],
                recv_sem=recv_sems.at[k - 1],
                device_id=_flip(me, k),
                device_id_type=MESH,
            )
            for k in range(1, N_DEV)
        ]

    def body(chip_ref, x_ref, mod_ref, ng_ref, cos_ref, s1_ref, s2_ref, shard_hbm, pack_hbm,
             proj_ref, h_ref, wt_hbm, all_hbm, wbuf, hbuf, w_send, w_recv, load_sems, local_sems, send_sems, recv_sems):
        j = pl.program_id(0)
        i = pl.program_id(1)
        me = _position()
        sibling = _flip(me, 1)
        chip = chip_ref[0]

        def shard_copy(k, block, to, src=None):
            return pltpu.make_async_remote_copy(
                src_ref=wt_hbm.at[_slot(block)] if src is None else src,
                dst_ref=wt_hbm.at[_slot(block)],
                send_sem=w_send.at[k],
                recv_sem=w_recv.at[k],
                device_id=to,
                device_id_type=MESH,
            )

        own_pack = pltpu.make_async_copy(pack_hbm, all_hbm.at[_slot(me)], local_sems.at[0])
        own_shard = pltpu.make_async_copy(shard_hbm, wt_hbm.at[_slot(me)], local_sems.at[1])

        def load_tile(q):
            loads = [pltpu.make_async_copy(wt_hbm.at[2 * q + c], wbuf.at[pl.ds(W_IN_SHARD * c, W_IN_SHARD), :], load_sems.at[c]) for c in range(2)]
            for cp in loads:
                cp.start()
            for cp in loads:
                cp.wait()

        @pl.when((j == 0) & (i == 0))
        def _():
            own_shard.start()
            shard_copy(0, me, sibling, src=shard_hbm).start()
            for rel in range(1, 4):
                shard_copy(rel, me, _flip(me, 2 * rel), src=shard_hbm).start()
            own_pack.start()
            for cp in pack_copies(pack_hbm, all_hbm, send_sems, recv_sems, arriving=False):
                cp.start()
            own_shard.wait()
            shard_copy(0, sibling, me).wait_recv()
            load_tile(chip)

        for rel in range(1, 4):

            @pl.when((j == rel) & (i == 0))
            def _(rel=rel):
                there = _flip(me, 2 * rel)
                shard_copy(rel, there, me).wait_recv()
                shard_copy(3 + rel, there, sibling).start()
                shard_copy(3 + rel, _flip(sibling, 2 * rel), me).wait_recv()
                load_tile(chip ^ rel)

        rows = pl.ds(pl.multiple_of(i * tm, tm), tm)

        @pl.when(j == 0)
        def _():
            xv = x_ref[...]
            r = lax.rsqrt(jnp.mean(xv * xv, axis=-1, keepdims=True) + NORM_EPS)
            shift = mod_ref[:, 0:D]
            scale = mod_ref[:, D : 2 * D]
            hv = (((xv * r) * ng_ref[...]) * (1.0 + scale) + shift).astype(BF16)
            hbuf[rows, :] = hv
            h_ref[...] = hv

        proj_ref[...] = _dot_nt(hbuf[rows, :], wbuf[...])

        @pl.when((chip ^ j) == 0)
        def _():
            cos, s1, s2 = cos_ref[...], s1_ref[...], s2_ref[...]
            for g in range(n_rope):
                cols = slice(128 * g, 128 * (g + 1))
                proj_ref[:, cols] = _rope(proj_ref[:, cols], cos, s1, s2)

        @pl.when((j == n_j - 1) & (i == n_i - 1))
        def _():
            shard_copy(0, me, sibling, src=shard_hbm).wait_send()
            for rel in range(1, 4):
                shard_copy(rel, me, _flip(me, 2 * rel), src=shard_hbm).wait_send()
                shard_copy(3 + rel, _flip(me, 2 * rel), sibling).wait_send()
            for cp in pack_copies(pack_hbm, all_hbm, send_sems, recv_sems, arriving=True):
                cp.wait_recv()
            for cp in pack_copies(pack_hbm, all_hbm, send_sems, recv_sems, arriving=False):
                cp.wait_send()
            own_pack.wait()

    tab = pl.BlockSpec((tm, 128), lambda j, i, c: (i, 0))
    any_spec = pl.BlockSpec(memory_space=pl.ANY)
    return pl.pallas_call(
        body,
        name="fwd_in",
        grid_spec=pltpu.PrefetchScalarGridSpec(
            num_scalar_prefetch=1,
            grid=(n_j, n_i),
            in_specs=[
                pl.BlockSpec((tm, D), lambda j, i, c: (jnp.where(j == 0, i, 0), 0)),
                pl.BlockSpec((1, 3 * D), lambda j, i, c: (0, 0)),
                pl.BlockSpec((1, D), lambda j, i, c: (0, 0)),
                tab, tab, tab, any_spec, any_spec,
            ],
            out_specs=[pl.BlockSpec((tm, tn), lambda j, i, c: (i, c[0] ^ j)),
                       pl.BlockSpec((tm, D), lambda j, i, c: (jnp.where(j == 0, i, n_i - 1), 0)),
                       any_spec, any_spec],
            scratch_shapes=[
                pltpu.VMEM((tn, D), BF16),
                pltpu.VMEM((t, D), BF16),
                pltpu.SemaphoreType.DMA((N_DEV - 1,)),
                pltpu.SemaphoreType.DMA((N_DEV - 1,)),
                pltpu.SemaphoreType.DMA((2,)),
                pltpu.SemaphoreType.DMA((2,)),
                pltpu.SemaphoreType.DMA((N_DEV - 1,)),
                pltpu.SemaphoreType.DMA((N_DEV - 1,)),
            ],
        ),
        out_shape=[jax.ShapeDtypeStruct((t, IN_W), F32), jax.ShapeDtypeStruct((t, D), BF16),
                   jax.ShapeDtypeStruct((N_DEV,) + w_shard.shape, w_shard.dtype),
                   jax.ShapeDtypeStruct((N_DEV,) + pack.shape, pack.dtype)],
        compiler_params=_params(2),
    )(my_chip, x, mod, norm_g, *tables, w_shard, pack)


def _rope_tables(positions):
    half = ROT_DIM // 2
    inv_freq = ROPE_THETA ** (-jnp.arange(0, ROT_DIM, 2, dtype=F32) / ROT_DIM)
    pos = positions.astype(F32)
    lane = np.arange(128) % HEAD_DIM
    freq_lane = jnp.where(lane < ROT_DIM, jnp.tile(inv_freq, 128 // half), 0.0)
    ang = pos[:, None] * freq_lane[None, :]
    sin = jnp.sin(ang)
    lane_tables = (jnp.cos(ang), jnp.where(lane < half, -sin, 0.0), jnp.where((lane >= half) & (lane < ROT_DIM), sin, 0.0))
    ang_t = inv_freq[:, None] * pos[None, :]
    return lane_tables, (jnp.cos(ang_t), jnp.sin(ang_t))


def _rope(tv, cos, s1, s2):
    cols = []
    for g in range(tv.shape[1] // 128):
        blk = tv[:, 128 * g : 128 * (g + 1)]
        cols.append(blk * cos + pltpu.roll(blk, 128 - 8, 1) * s1 + pltpu.roll(blk, 8, 1) * s2)
    return cols[0] if len(cols) == 1 else jnp.concatenate(cols, axis=1)


def _rope_transposed(gv, cos, s1, s2):
    cols = []
    for g in range(gv.shape[1] // 128):
        blk = gv[:, 128 * g : 128 * (g + 1)]
        cols.append(blk * cos + pltpu.roll(blk * s1, 8, 1) + pltpu.roll(blk * s2, 128 - 8, 1))
    return cols[0] if len(cols) == 1 else jnp.concatenate(cols, axis=1)


def _lane_lo():
    return lax.broadcasted_iota(jnp.int32, (1, 128), 1) < HEAD_DIM


def _row_lo():
    return lax.broadcasted_iota(jnp.int32, (128, 1), 0) < HEAD_DIM


def _split_heads(tile, j):
    lo = _lane_lo()
    parts = []
    for g in (2 * j, 2 * j + 1):
        blk = tile[:, 128 * g : 128 * (g + 1)]
        parts += [jnp.where(lo, blk, 0.0), jnp.where(lo, 0.0, blk)]
    return jnp.concatenate(parts, axis=0)


def _dup_lanes(kv, j):
    rolled = pltpu.roll(kv, HEAD_DIM, 1)
    lo = _lane_lo()
    return jnp.where(lo, kv, rolled) if j == 0 else jnp.where(lo, rolled, kv)


def _fold_lanes(acc):
    return acc + pltpu.roll(acc, HEAD_DIM, 1)


def _fill_bias_t(bias_ref):
    kj = lax.broadcasted_iota(jnp.int32, (256, 128), 0)
    qi = lax.broadcasted_iota(jnp.int32, (256, 128), 1)
    diff = qi + WINDOW - kj
    band = (diff >= 0) & (diff < WINDOW)
    bias_ref[0] = jnp.where(band, 0.0, -1e30)
    bias_ref[1] = jnp.where(band & (kj >= 128), 0.0, -1e30)


def _sink_row(sink_ref, pair, j):
    return jnp.concatenate([jnp.full((1, 128), sink_ref[0, 8 * pair + 4 * j + h], F32) for h in range(4)], axis=1)


def _softmax_t(q4, kw, bias4, sink):
    s = _dot_nt(kw, q4) + bias4
    m = jnp.maximum(jnp.max(s, axis=0, keepdims=True), sink)
    p = jnp.exp(s - m)
    e_sink = jnp.exp(sink - m)
    denom = jnp.sum(p, axis=0, keepdims=True) + e_sink
    return p, e_sink, denom


def _attn_specs(t, tq):
    return dict(
        q=pl.BlockSpec((tq, 512), lambda p, i: (i, p)),
        k=pl.BlockSpec((t, 128), lambda p, i: (0, SEG_K // 128 + p)),
        v=pl.BlockSpec((t, 128), lambda p, i: (0, SEG_V // 128 + p)),
        sink=pl.BlockSpec(memory_space=pltpu.SMEM),
        act=pl.BlockSpec((tq, 512), lambda p, i: (i, p)),
    )


def _attn_fwd(proj, sinks):
    t = proj.shape[0]
    tq = min(512, t)
    nb = tq // 128
    sp = _attn_specs(t, tq)

    def body(q_ref, k_ref, v_ref, sink_ref, o_ref, kd, vdt, bias_ref):
        pair = pl.program_id(0)
        i = pl.program_id(1)

        @pl.when(i == 0)
        def _():
            kv, vv = k_ref[...], v_ref[...]
            for j in range(2):
                kd[j, 0:128, :] = jnp.zeros((128, 128), BF16)
                kd[j, 128:, :] = _dup_lanes(kv, j).astype(BF16)
                vdt[j, :, 0:128] = jnp.zeros((128, 128), BF16)
                vdt[j, :, 128:] = _dup_lanes(vv, j).T.astype(BF16)
            _fill_bias_t(bias_ref)

        row_lo = _row_lo()

        def block(b, carry):
            r0 = pl.multiple_of(b * 128, 128)
            gb = i * nb + b
            keys = pl.ds(pl.multiple_of(gb * 128, 128), 256)
            rows = pl.ds(r0, 128)
            qt = q_ref[rows, :] * (1.0 / math.sqrt(HEAD_DIM))
            bias = bias_ref[jnp.where(gb == 0, 1, 0)]
            bias4 = jnp.concatenate([bias] * 4, axis=1)
            cols = []
            for j in range(2):
                q4 = _split_heads(qt, j).astype(BF16)
                p, _, denom = _softmax_t(q4, kd[j, keys, :], bias4, _sink_row(sink_ref, pair, j))
                ot = _dot(vdt[j, :, keys], p.astype(BF16)) * (1.0 / denom)
                for gi in range(2):
                    z = jnp.where(row_lo, ot[:, 256 * gi : 256 * gi + 128], ot[:, 256 * gi + 128 : 256 * gi + 256])
                    cols.append(z.T)
            o_ref[rows, :] = jnp.concatenate(cols, axis=1)
            return carry

        lax.fori_loop(0, nb, block, 0, unroll=True)

    return pl.pallas_call(
        body,
        name="attn_fwd",
        grid=(2, t // tq),
        in_specs=[sp["q"], sp["k"], sp["v"], sp["sink"]],
        out_specs=sp["act"],
        out_shape=jax.ShapeDtypeStruct((t, D), F32),
        scratch_shapes=[pltpu.VMEM((2, t + 128, 128), BF16), pltpu.VMEM((2, 128, t + 128), BF16), pltpu.VMEM((2, 256, 128), F32)],
        compiler_params=_params(2),
    )(proj, proj, proj, sinks)


def _attn_bwd(proj, o, d_o, tables, sinks, part_3):
    t = proj.shape[0]
    tq = min(512, t)
    nb = tq // 128
    n_tiles = t // tq
    (cos, s1, s2), (cos_t, sin_t) = tables
    sp = _attn_specs(t, tq)
    scale = 1.0 / math.sqrt(HEAD_DIM)
    half = ROT_DIM // 2

    def unrope_rows(z, c, s):
        out = []
        for base in (0, HEAD_DIM):
            g0, g1 = z[base : base + half], z[base + half : base + ROT_DIM]
            out += [g0 * c + g1 * s, g1 * c - g0 * s, z[base + ROT_DIM : base + HEAD_DIM]]
        return jnp.concatenate(out, axis=0)

    def swap(part_hbm, recv_hbm, send_sem, recv_sem):
        sibling = _flip(_position(), 1)
        return pltpu.make_async_remote_copy(
            src_ref=part_hbm.at[:, :, sibling[2]], dst_ref=recv_hbm, send_sem=send_sem, recv_sem=recv_sem,
            device_id=sibling, device_id_type=MESH,
        )

    def body(q_ref, k_ref, v_ref, o_ref, do_ref, ct_ref, st_ref, ck, s1k, s2k, sink_ref, part_hbm,
             dq_ref, dk_ref, dv_ref, dsink_ref, recv_hbm, kd, kdt, vd, bias_ref, dkacc, dvacc, send_sem, recv_sem):
        pair = pl.program_id(0)
        i = pl.program_id(1)

        @pl.when((pair == 0) & (i == 0))
        def _():
            swap(part_hbm, recv_hbm, send_sem, recv_sem).start()

        @pl.when((pair == 1) & (i == n_tiles - 1))
        def _():
            swap(part_hbm, recv_hbm, send_sem, recv_sem).wait()

        @pl.when(i == 0)
        def _():
            kv, vv = k_ref[...], v_ref[...]
            for j in range(2):
                kdup = _dup_lanes(kv, j)
                kd[j, 0:128, :] = jnp.zeros((128, 128), BF16)
                kd[j, 128:, :] = kdup.astype(BF16)
                kdt[j, :, 0:128] = jnp.zeros((128, 128), BF16)
                kdt[j, :, 128:] = kdup.T.astype(BF16)
                vd[j, 0:128, :] = jnp.zeros((128, 128), BF16)
                vd[j, 128:, :] = _dup_lanes(vv, j).astype(BF16)
            _fill_bias_t(bias_ref)
            dkacc[...] = jnp.zeros_like(dkacc)
            dvacc[...] = jnp.zeros_like(dvacc)
            dsink_ref[...] = jnp.zeros_like(dsink_ref)

        row_lo = _row_lo()
        ones = jnp.ones((8, 128), BF16)

        def block(b, carry):
            r0 = pl.multiple_of(b * 128, 128)
            gb = i * nb + b
            keys = pl.ds(pl.multiple_of(gb * 128, 128), 256)
            rows = pl.ds(r0, 128)
            qt = q_ref[rows, :] * scale
            dob = do_ref[rows, :]
            prod = dob * o_ref[rows, :]
            ct, st = ct_ref[:, rows], st_ref[:, rows]
            bias = bias_ref[jnp.where(gb == 0, 1, 0)]
            bias4 = jnp.concatenate([bias] * 4, axis=1)
            cols = []
            for j in range(2):
                q4 = _split_heads(qt, j).astype(BF16)
                kw = kd[j, keys, :]
                p, e_sink, denom = _softmax_t(q4, kw, bias4, _sink_row(sink_ref, pair, j))
                inv = 1.0 / denom
                prob = p * inv
                do4 = _split_heads(dob, j).astype(BF16)
                delta = _dot_nt(ones, _split_heads(prod, j).astype(BF16))[0:1, :]
                dp = _dot_nt(vd[j, keys, :], do4)
                ds = (prob * (dp - delta)).astype(BF16)
                dkacc[j, keys, :] = dkacc[j, keys, :] + _dot(ds, q4)
                dvacc[j, keys, :] = dvacc[j, keys, :] + _dot(prob.astype(BF16), do4)
                dqt = _dot(kdt[j, :, keys], ds) * scale
                dsink = -(e_sink * inv) * delta
                for h in range(4):
                    row = 4 * j + h
                    total = jnp.sum(dsink[:, 128 * h : 128 * (h + 1)], axis=1, keepdims=True)
                    dsink_ref[0, row : row + 1, :] = dsink_ref[0, row : row + 1, :] + jnp.broadcast_to(total, (1, 128))
                for gi in range(2):
                    z = jnp.where(row_lo, dqt[:, 256 * gi : 256 * gi + 128], dqt[:, 256 * gi + 128 : 256 * gi + 256])
                    cols.append(unrope_rows(z, ct, st).T)
            dq_ref[rows, :] = jnp.concatenate(cols, axis=1).astype(BF16)
            return carry

        lax.fori_loop(0, nb, block, 0, unroll=True)

        @pl.when(i == n_tiles - 1)
        def _():
            lo = _lane_lo()
            dk = jnp.where(lo, _fold_lanes(dkacc[0, 128:, :]), _fold_lanes(dkacc[1, 128:, :]))
            dk_ref[...] = _rope_transposed(dk, ck[...], s1k[...], s2k[...]).astype(BF16)
            dv_ref[...] = jnp.where(lo, _fold_lanes(dvacc[0, 128:, :]), _fold_lanes(dvacc[1, 128:, :])).astype(BF16)

    kv_out = pl.BlockSpec((t, 128), lambda p, i: (0, p))
    tab_t = pl.BlockSpec((half, tq), lambda p, i: (0, i))
    tab_k = pl.BlockSpec((t, 128), lambda p, i: (0, 0))
    return pl.pallas_call(
        body,
        name="attn_bwd",
        grid=(2, n_tiles),
        in_specs=[sp["q"], sp["k"], sp["v"], sp["act"], sp["act"], tab_t, tab_t, tab_k, tab_k, tab_k, sp["sink"],
                  pl.BlockSpec(memory_space=pl.ANY)],
        out_specs=[sp["act"], kv_out, kv_out, pl.BlockSpec((1, 8, 128), lambda p, i: (p, 0, 0)), pl.BlockSpec(memory_space=pl.ANY)],
        out_shape=[
            jax.ShapeDtypeStruct((t, D), BF16),
            jax.ShapeDtypeStruct((t, 256), BF16),
            jax.ShapeDtypeStruct((t, 256), BF16),
            jax.ShapeDtypeStruct((2, 8, 128), F32),
            jax.ShapeDtypeStruct((3, 4, ROW_SHARD, D), F32),
        ],
        scratch_shapes=[pltpu.VMEM((2, t + 128, 128), BF16), pltpu.VMEM((2, 128, t + 128), BF16), pltpu.VMEM((2, t + 128, 128), BF16),
                        pltpu.VMEM((2, 256, 128), F32), pltpu.VMEM((2, t + 128, 128), F32), pltpu.VMEM((2, t + 128, 128), F32),
                        pltpu.SemaphoreType.DMA, pltpu.SemaphoreType.DMA],
        compiler_params=_params(2),
    )(proj, proj, proj, o, d_o, cos_t, sin_t, cos, s1, s2, sinks, part_3)


HALF = 512


def _softplus(y):
    u = jnp.exp(-jnp.abs(y))
    w = 1.0 + u
    log1p = jnp.where(w == 1.0, u, jnp.log(w) * (u / jnp.where(w == 1.0, 1.0, w - 1.0)))
    return jnp.maximum(y, 0.0) + log1p


def _gate_block(w_ref, blk):
    return w_ref[:, :, 256 * blk : 256 * (blk + 1)].reshape(256, 256)


def _gate_matmul(act_bf, w_ref):
    return jnp.concatenate([_dot(act_bf[:, 256 * blk : 256 * (blk + 1)], _gate_block(w_ref, blk)) for blk in range(2)], axis=1)


def _gate_matmul_t(grad_bf, w_ref):
    return jnp.concatenate([_dot_nt(grad_bf[:, 256 * blk : 256 * (blk + 1)], _gate_block(w_ref, blk)) for blk in range(2)], axis=1)


def _conv_and_gates(xe_ref, tm, cw_ref, cb_ref, wa_ref, wx_ref, ba_ref, bx_ref, lam_ref, keep):
    xc = cb_ref[...] + sum(cw_ref[k : k + 1, :] * xe_ref[pl.ds(5 + k, tm), :] for k in range(CONV_WIDTH))
    xc_bf = xc.astype(BF16)
    r = _sigmoid(_gate_matmul(xc_bf, wa_ref) + ba_ref[...])
    ig = _sigmoid(_gate_matmul(xc_bf, wx_ref) + bx_ref[...])
    sp = _softplus(-lam_ref[...])
    log_a = -LRU_C * r * sp
    tanh = jnp.tanh(-log_a)
    mult = jnp.sqrt(2.0 * tanh / (1.0 + tanh))
    a = jnp.where(keep, jnp.exp(log_a), 0.0)
    mult = jnp.where(keep, mult, 1.0)
    return xc, xc_bf, r, ig, sp, a, mult


def _scan_rows(a_ref, b_ref, h_ref, carry_ref, tm, reverse):
    width = a_ref.shape[1]
    row = lax.broadcasted_iota(jnp.int32, (8, width), 0)

    def group(g, carry):
        gg = tm // 8 - 1 - g if reverse else g
        r0 = pl.multiple_of(gg * 8, 8)
        a = a_ref[pl.ds(r0, 8), :]
        b = b_ref[pl.ds(r0, 8), :]
        for s in (1, 2, 4):
            shift = 8 - s if reverse else s
            a_sh = pltpu.roll(a, shift, 0)
            b_sh = pltpu.roll(b, shift, 0)
            ok = (row < 8 - s) if reverse else (row >= s)
            b = jnp.where(ok, a * b_sh + b, b)
            a = jnp.where(ok, a * a_sh, a)
        h = a * carry + b
        h_ref[pl.ds(r0, 8), :] = h
        last = h[0:1, :] if reverse else h[7:8, :]
        return jnp.broadcast_to(last, (8, width))

    carry_ref[...] = lax.fori_loop(0, tm // 8, group, carry_ref[...])


def _rnn_specs(t, tm, order):
    n = t // tm
    return dict(
        tile=pl.BlockSpec((tm, HALF), lambda ch, i: (order(i, n), ch)),
        before=lambda col0: pl.BlockSpec((8, HALF), lambda ch, i: (jnp.maximum(order(i, n) * (tm // 8) - 1, 0), col0 + ch)),
        proj_tile=lambda col0: pl.BlockSpec((tm, HALF), lambda ch, i: (order(i, n), col0 + ch)),
        keep=pl.BlockSpec((tm, 1), lambda ch, i: (order(i, n), 0)),
        gates=lambda which: pl.BlockSpec((N_DEV, GATE_SHARD, HALF), lambda ch, i: (0, PACK_GATE // GATE_SHARD + which, ch)),
        gate_grads=pl.BlockSpec((2, 2, 256, 256), lambda ch, i: (0, ch, 0, 0)),
        vec=pl.BlockSpec((1, HALF), lambda ch, i: (0, ch)),
        conv=pl.BlockSpec((CONV_WIDTH, HALF), lambda ch, i: (0, ch)),
    )


def _rnn_fwd(proj, keep, w_all, rg_ba, rg_bx, rg_lambda, conv_w, conv_b):
    t = proj.shape[0]
    tm = min(512, t)
    sp = _rnn_specs(t, tm, lambda i, n: i)

    def body(xr_ref, xb_ref, keep_ref, wa_ref, wx_ref, ba_ref, bx_ref, lam_ref, cw_ref, cb_ref, hh_ref, xe, a_buf, b_buf, carry):
        i = pl.program_id(1)

        @pl.when(i == 0)
        def _():
            carry[...] = jnp.zeros_like(carry)

        xe[0:8, :] = jnp.where(i > 0, xb_ref[...], 0.0)
        xe[8:, :] = xr_ref[...]
        keep_rows = keep_ref[...] != 0.0
        xc, _, _, ig, _, a, mult = _conv_and_gates(xe, tm, cw_ref, cb_ref, wa_ref, wx_ref, ba_ref, bx_ref, lam_ref, keep_rows)
        a_buf[...] = a
        b_buf[...] = mult * (ig * xc)
        _scan_rows(a_buf, b_buf, hh_ref, carry, tm, reverse=False)

    return pl.pallas_call(
        body,
        name="rnn_fwd",
        grid=(2, t // tm),
        in_specs=[sp["proj_tile"](SEG_XR // HALF), sp["before"](SEG_XR // HALF), sp["keep"], sp["gates"](0), sp["gates"](1),
                  sp["vec"], sp["vec"], sp["vec"], sp["conv"], sp["vec"]],
        out_specs=sp["tile"],
        out_shape=jax.ShapeDtypeStruct((t, D), F32),
        scratch_shapes=[pltpu.VMEM((tm + 8, HALF), F32), pltpu.VMEM((tm, HALF), F32), pltpu.VMEM((tm, HALF), F32), pltpu.VMEM((8, HALF), F32)],
        compiler_params=_params(2),
    )(proj, proj, keep, w_all, w_all, rg_ba, rg_bx, rg_lambda, conv_w, conv_b)


def _rnn_bwd(proj, hh, d_hh, keep, w_all, rg_ba, rg_bx, rg_lambda, conv_w, conv_b, sum_3_bf):
    t = proj.shape[0]
    tm = min(512, t)
    n_tiles = t // tm
    sp = _rnn_specs(t, tm, lambda i, n: n - 1 - i)

    def body(xr_ref, xb_ref, hh_ref, hb_ref, dhh_ref, keep_ref, wa_ref, wx_ref, ba_ref, bx_ref, lam_ref, cw_ref, cb_ref,
             s3_hbm, dxr_ref, gw_ref, gvec_ref, l3_hbm, xe, he, ae, c_buf, g_buf, dxe, carry, a_next, send_sems, recv_sems):
        ch = pl.program_id(0)
        i = pl.program_id(1)
        first_tile = i == n_tiles - 1
        views = [lambda ref, peer: ref.at[:, _chip_of(peer)]]

        @pl.when((ch == 0) & (i == 0))
        def _():
            for cp in _chip_copies([s3_hbm], [l3_hbm], views, send_sems, recv_sems):
                cp.start()

        @pl.when((ch == 1) & (i == n_tiles - 1))
        def _():
            for cp in _chip_copies([s3_hbm], [l3_hbm], views, send_sems, recv_sems):
                cp.wait_recv()
                cp.wait_send()

        @pl.when(i == 0)
        def _():
            carry[...] = jnp.zeros_like(carry)
            a_next[...] = jnp.zeros_like(a_next)
            dxe[tm:, :] = jnp.zeros((8, HALF), F32)
            gw_ref[...] = jnp.zeros_like(gw_ref)
            gvec_ref[...] = jnp.zeros_like(gvec_ref)

        xe[0:8, :] = jnp.where(first_tile, 0.0, xb_ref[...])
        xe[8:, :] = xr_ref[...]
        he[0:8, :] = jnp.where(first_tile, 0.0, hb_ref[...])
        he[8:, :] = hh_ref[...]
        keep_rows = keep_ref[...] != 0.0
        xc, xc_bf, r, ig, sp_, a, mult = _conv_and_gates(xe, tm, cw_ref, cb_ref, wa_ref, wx_ref, ba_ref, bx_ref, lam_ref, keep_rows)

        ae[0:tm, :] = a
        ae[tm:, :] = a_next[...]
        a_next[...] = jnp.broadcast_to(a[0:1, :], (8, HALF))
        c_buf[...] = ae[pl.ds(1, tm), :]
        g_buf[...] = dhh_ref[...]
        _scan_rows(c_buf, g_buf, g_buf, carry, tm, reverse=True)
        g = g_buf[...]

        h_prev = he[pl.ds(7, tm), :]
        bb = ig * xc
        d_ig = g * mult * xc
        d_xc = g * mult * ig
        d_log_a = jnp.where(keep_rows, g * h_prev * a - (g * bb) * (a * a) / mult, 0.0)
        d_r = d_log_a * (-LRU_C * sp_)
        d_sp = jnp.sum(d_log_a * (-LRU_C * r), axis=0, keepdims=True)
        d_zr = d_r * r * (1.0 - r)
        d_zi = d_ig * ig * (1.0 - ig)
        d_zr_bf = d_zr.astype(BF16)
        d_zi_bf = d_zi.astype(BF16)
        d_xc = d_xc + _gate_matmul_t(d_zr_bf, wa_ref) + _gate_matmul_t(d_zi_bf, wx_ref)
        for blk in range(2):
            cols = slice(256 * blk, 256 * (blk + 1))
            gw_ref[0, blk] = gw_ref[0, blk] + _dot_tn(xc_bf[:, cols], d_zr_bf[:, cols])
            gw_ref[1, blk] = gw_ref[1, blk] + _dot_tn(xc_bf[:, cols], d_zi_bf[:, cols])

        dxe[0:tm, :] = d_xc
        dxr = sum(cw_ref[k : k + 1, :] * dxe[pl.ds(3 - k, tm), :] for k in range(CONV_WIDTH))
        dxr_ref[...] = dxr.astype(BF16)
        dxe[tm:, :] = d_xc[0:8, :]

        lam = lam_ref[...]
        sums = [
            jnp.sum(d_zr, axis=0, keepdims=True),
            jnp.sum(d_zi, axis=0, keepdims=True),
            d_sp * (-_sigmoid(-lam)),
            jnp.sum(d_xc, axis=0, keepdims=True),
        ] + [jnp.sum(d_xc * xe[pl.ds(5 + k, tm), :], axis=0, keepdims=True) for k in range(CONV_WIDTH)]
        gvec_ref[...] = gvec_ref[...] + jnp.concatenate(sums, axis=0)

    return pl.pallas_call(
        body,
        name="rnn_bwd",
        grid=(2, n_tiles),
        in_specs=[
            sp["proj_tile"](SEG_XR // HALF), sp["before"](SEG_XR // HALF), sp["tile"], sp["before"](0), sp["tile"], sp["keep"],
            sp["gates"](0), sp["gates"](1), sp["vec"], sp["vec"], sp["vec"], sp["conv"], sp["vec"], pl.BlockSpec(memory_space=pl.ANY),
        ],
        out_specs=[sp["tile"], sp["gate_grads"], pl.BlockSpec((8, HALF), lambda ch, i: (0, ch)), pl.BlockSpec(memory_space=pl.ANY)],
        out_shape=[
            jax.ShapeDtypeStruct((t, D), BF16),
            jax.ShapeDtypeStruct((2, 4, 256, 256), F32),
            jax.ShapeDtypeStruct((8, D), F32),
            jax.ShapeDtypeStruct((len(CHIP_FLIPS), 3, ROW_SHARD, D), BF16),
        ],
        scratch_shapes=[
            pltpu.VMEM((tm + 8, HALF), F32),
            pltpu.VMEM((tm + 8, HALF), F32),
            pltpu.VMEM((tm + 8, HALF), F32),
            pltpu.VMEM((tm, HALF), F32),
            pltpu.VMEM((tm, HALF), F32),
            pltpu.VMEM((tm + 8, HALF), F32),
            pltpu.VMEM((8, HALF), F32),
            pltpu.VMEM((8, HALF), F32),
            pltpu.SemaphoreType.DMA((len(CHIP_FLIPS),)),
            pltpu.SemaphoreType.DMA((len(CHIP_FLIPS),)),
        ],
        compiler_params=_params(2),
    )(proj, proj, hh, hh, d_hh, keep, w_all, w_all, rg_ba, rg_bx, rg_lambda, conv_w, conv_b, sum_3_bf)


def _mid(x, target, o, hh, proj, w_all, gate, final_g):
    t = x.shape[0]
    tm = min(256, t)
    n_tiles = t // tm
    full = pl.BlockSpec((tm, D), lambda i: (i, 0))
    half = lambda col: pl.BlockSpec((tm, 512), lambda i: (i, col))
    vec = pl.BlockSpec((1, D), lambda i: (0, 0))
    any_spec = pl.BlockSpec(memory_space=pl.ANY)

    def body(x_ref, tg_ref, o_ref, hh_ref, ga0, ga1, gr0, gr1, ma0, ma1, mr0, mr1, w_hbm, gate_ref, fg_ref,
             dres_ref, do_ref, dhh_ref, dga_ref, dgr_ref, dma_ref, dmr_ref, gw_hbm, vec_ref, w, gw, sem, w_sems):
        i = pl.program_id(0)

        @pl.when(i == 0)
        def _():
            loads = [
                pltpu.make_async_copy(
                    w_hbm.at[s, pl.ds(ROW_SHARD * k, ROW_SHARD), :], w.at[k, pl.ds(ROW_SHARD * s, ROW_SHARD), :], w_sems.at[k * N_DEV + s]
                )
                for k in range(3)
                for s in range(N_DEV)
            ]
            for cp in loads:
                cp.start()
            for cp in loads:
                cp.wait()
            gw[...] = jnp.zeros_like(gw)
            vec_ref[...] = jnp.zeros_like(vec_ref)

        cat = lambda a, b: jnp.concatenate([a[...], b[...]], axis=1)
        ga, gr, ma, mr = cat(ga0, ga1), cat(gr0, gr1), cat(ma0, ma1), cat(mr0, mr1)
        ov, hv, gate_v, fg = o_ref[...], hh_ref[...], gate_ref[...], fg_ref[...]
        sg_a = _sigmoid(ga)
        sg_r = _sigmoid(gr)
        silu_a = ga * sg_a
        silu_r = gr * sg_r
        ya = (ov * silu_a).astype(BF16)
        yr = (hv * silu_r).astype(BF16)
        pa = _dot(ya, w[0])
        pr = _dot(yr, w[1])
        sa = _sigmoid(ma)
        sr = _sigmoid(mr)
        merged = (sa * pa + sr * pr).astype(BF16)
        mo = _dot(merged, w[2])
        out = x_ref[...] + gate_v * mo
        r2 = lax.rsqrt(jnp.mean(out * out, axis=-1, keepdims=True) + NORM_EPS)
        outn = out * r2
        err = outn * fg - tg_ref[...]
        loss = 0.5 * jnp.sum(jnp.mean(err * err, axis=-1, keepdims=True), axis=0, keepdims=True)
        dy = err * (1.0 / D)
        g_final = jnp.sum(dy * outn, axis=0, keepdims=True)
        doutn = dy * fg
        dout = r2 * (doutn - outn * jnp.mean(doutn * outn, axis=-1, keepdims=True))
        dres_ref[...] = dout
        d_gate = jnp.sum(dout * mo, axis=0, keepdims=True)
        dmo = (dout * gate_v).astype(BF16)
        dmerged = _dot_nt(dmo, w[2])
        gw[2] = gw[2] + _dot_tn(merged, dmo)
        dpa = (dmerged * sa).astype(BF16)
        dpr = (dmerged * sr).astype(BF16)
        dma_ref[...] = (dmerged * pa * (sa * (1.0 - sa))).astype(BF16)
        dmr_ref[...] = (dmerged * pr * (sr * (1.0 - sr))).astype(BF16)
        dya = _dot_nt(dpa, w[0])
        dyr = _dot_nt(dpr, w[1])
        gw[0] = gw[0] + _dot_tn(ya, dpa)
        gw[1] = gw[1] + _dot_tn(yr, dpr)
        do_ref[...] = dya * silu_a
        dhh_ref[...] = dyr * silu_r
        dga_ref[...] = (dya * ov * (sg_a * (1.0 + ga * (1.0 - sg_a)))).astype(BF16)
        dgr_ref[...] = (dyr * hv * (sg_r * (1.0 + gr * (1.0 - sg_r)))).astype(BF16)
        zeros = jnp.zeros((5, D), F32)
        vec_ref[...] = vec_ref[...] + jnp.concatenate([g_final, d_gate, jnp.broadcast_to(loss, (1, D)), zeros], axis=0)

        @pl.when(i == n_tiles - 1)
        def _():
            cp = pltpu.make_async_copy(gw, gw_hbm, sem)
            cp.start()
            cp.wait()

    bf_tile = jax.ShapeDtypeStruct((t, D), BF16)
    f_tile = jax.ShapeDtypeStruct((t, D), F32)
    return pl.pallas_call(
        body,
        name="mid",
        grid=(n_tiles,),
        in_specs=[full, full, full, full,
                  half(SEG_GA // 512), half(SEG_GA // 512 + 1), half(SEG_GR // 512), half(SEG_GR // 512 + 1),
                  half(SEG_MA // 512), half(SEG_MA // 512 + 1), half(SEG_MR // 512), half(SEG_MR // 512 + 1),
                  any_spec, vec, vec],
        out_specs=[full, full, full, full, full, full, full, any_spec, pl.BlockSpec((8, D), lambda i: (0, 0))],
        out_shape=[f_tile, f_tile, f_tile, bf_tile, bf_tile, bf_tile, bf_tile,
                   jax.ShapeDtypeStruct((3, D, D), F32), jax.ShapeDtypeStruct((8, D), F32)],
        scratch_shapes=[pltpu.VMEM((3, D, D), BF16), pltpu.VMEM((3, D, D), F32), pltpu.SemaphoreType.DMA, pltpu.SemaphoreType.DMA((3 * N_DEV,))],
        compiler_params=_params(1),
    )(x, target, o, hh, proj, proj, proj, proj, proj, proj, proj, proj, w_all, gate, final_g)


_SEGMENTS = ((SEG_Q, 1024), (SEG_K, 256), (SEG_V, 256), (SEG_GA, 1024), (SEG_XR, 1024), (SEG_GR, 1024), (SEG_MA, 1024), (SEG_MR, 1024))


def _seg_specs(tm):
    return [pl.BlockSpec((tm, width), lambda i: (i, 0)) for _, width in _SEGMENTS]


def _bwd_dh(dsegs, w_all, x, dres, mod, norm_g, sum_w_bf):
    t = x.shape[0]
    tm = min(512, t)
    n_tiles = t // tm
    n_seg = len(_SEGMENTS)
    full = pl.BlockSpec((tm, D), lambda i: (i, 0))
    views = [lambda ref, peer: ref.at[_chip_of(peer)]]

    def body(*refs):
        segs = refs[:n_seg]
        w_hbm, x_ref, dres_ref, mod_ref, ng_ref, sum_hbm, gx_ref, vec_ref, land_hbm, w, sems, send_sems, recv_sems = refs[n_seg:]
        i = pl.program_id(0)

        @pl.when(i == 0)
        def _():
            for cp in _chip_copies([sum_hbm], [land_hbm], views, send_sems, recv_sems):
                cp.start()
            loads = [
                pltpu.make_async_copy(w_hbm.at[s], w.at[pl.ds(W_IN_SHARD * s, W_IN_SHARD), :], sems.at[s])
                for s in range(N_DEV)
            ]
            for cp in loads:
                cp.start()
            for cp in loads:
                cp.wait()
            vec_ref[...] = jnp.zeros_like(vec_ref)

        dh = sum(_dot(seg[...], w[start : start + width, :]) for seg, (start, width) in zip(segs, _SEGMENTS))
        xv = x_ref[...]
        r1 = lax.rsqrt(jnp.mean(xv * xv, axis=-1, keepdims=True) + NORM_EPS)
        xn = xv * r1
        ng = ng_ref[...]
        scale1 = 1.0 + mod_ref[:, D : 2 * D]
        d_shift = jnp.sum(dh, axis=0, keepdims=True)
        d_scale = jnp.sum(dh * (xn * ng), axis=0, keepdims=True)
        g_norm = jnp.sum(dh * xn * scale1, axis=0, keepdims=True)
        dxn = dh * (ng * scale1)
        dx = r1 * (dxn - xn * jnp.mean(dxn * xn, axis=-1, keepdims=True))
        gx_ref[...] = dres_ref[...] + dx
        vec_ref[...] = vec_ref[...] + jnp.concatenate([d_shift, d_scale, g_norm, jnp.zeros((5, D), F32)], axis=0)

        @pl.when(i == n_tiles - 1)
        def _():
            for cp in _chip_copies([sum_hbm], [land_hbm], views, send_sems, recv_sems):
                cp.wait_recv()
                cp.wait_send()

    any_spec = pl.BlockSpec(memory_space=pl.ANY)
    n_copies = len(CHIP_FLIPS)
    return pl.pallas_call(
        body,
        name="bwd_dh",
        grid=(n_tiles,),
        in_specs=_seg_specs(tm) + [any_spec, full, full, pl.BlockSpec((1, 3 * D), lambda i: (0, 0)), pl.BlockSpec((1, D), lambda i: (0, 0)),
                                   any_spec],
        out_specs=[full, pl.BlockSpec((8, D), lambda i: (0, 0)), any_spec],
        out_shape=[jax.ShapeDtypeStruct((t, D), F32), jax.ShapeDtypeStruct((8, D), F32),
                   jax.ShapeDtypeStruct((n_copies, W_IN_SHARD, D), BF16)],
        scratch_shapes=[pltpu.VMEM((IN_W, D), BF16), pltpu.SemaphoreType.DMA((N_DEV,)),
                        pltpu.SemaphoreType.DMA((n_copies,)), pltpu.SemaphoreType.DMA((n_copies,))],
        compiler_params=_params(1),
    )(*dsegs, w_all, x, dres, mod, norm_g, sum_w_bf)


def _bwd_gw(dsegs, h, sum_g_bf):
    t = h.shape[0]
    tm = min(512, t)
    n_tiles = t // tm
    n_seg = len(_SEGMENTS)
    n_chips = N_DEV // 2
    views = [lambda ref, peer: ref.at[:, :, _chip_of(peer)]]

    def body(*refs):
        segs = refs[:n_seg]
        (h_ref, sg_hbm, own_hbm, recv_hbm, lg_hbm, acc, stage, own_sems, pair_send, pair_recv, send_sems, recv_sems) = refs[n_seg:]
        i = pl.program_id(0)

        @pl.when(i == 0)
        def _():
            for cp in _chip_copies([sg_hbm], [lg_hbm], views, send_sems, recv_sems):
                cp.start()
            acc[...] = jnp.zeros_like(acc)

        hv = h_ref[...]
        for seg, (start, width) in zip(segs, _SEGMENTS):
            acc[start : start + width, :] = acc[start : start + width, :] + _dot_tn(seg[...], hv)

        @pl.when(i == n_tiles - 1)
        def _():
            me = _position()
            sibling = _flip(me, 1)

            def rows(q, c):
                return pl.ds(pl.multiple_of(W_IN_SHARD * (2 * q + c), 8), W_IN_SHARD)

            def to_sibling(q):
                return pltpu.make_async_remote_copy(
                    src_ref=stage.at[q % 2], dst_ref=recv_hbm.at[q], send_sem=pair_send.at[q], recv_sem=pair_recv.at[q],
                    device_id=sibling, device_id_type=MESH,
                )

            kept = [pltpu.make_async_copy(acc.at[rows(q, me[2]), :], own_hbm.at[q], own_sems.at[q]) for q in range(n_chips)]
            for cp in kept:
                cp.start()
            for q in range(n_chips):
                if q >= 2:
                    to_sibling(q - 2).wait_send()
                stage[q % 2] = acc[rows(q, sibling[2]), :].astype(BF16)
                to_sibling(q).start()
            for q in range(n_chips - 2, n_chips):
                to_sibling(q).wait_send()
            for q in range(n_chips):
                to_sibling(q).wait_recv()
            for cp in kept:
                cp.wait()
            for cp in _chip_copies([sg_hbm], [lg_hbm], views, send_sems, recv_sems):
                cp.wait_recv()
                cp.wait_send()

    any_spec = pl.BlockSpec(memory_space=pl.ANY)
    n_copies = len(CHIP_FLIPS)
    return pl.pallas_call(
        body,
        name="bwd_gw",
        grid=(n_tiles,),
        in_specs=_seg_specs(tm) + [pl.BlockSpec((tm, D), lambda i: (i, 0)), any_spec],
        out_specs=[any_spec, any_spec, any_spec],
        out_shape=[jax.ShapeDtypeStruct((n_chips, W_IN_SHARD, D), F32),
                   jax.ShapeDtypeStruct((n_chips, W_IN_SHARD, D), BF16),
                   jax.ShapeDtypeStruct((len(CHIP_FLIPS), 2, 4, GATE_SHARD, 256), BF16)],
        scratch_shapes=[pltpu.VMEM((IN_W, D), F32), pltpu.VMEM((2, W_IN_SHARD, D), BF16),
                        pltpu.SemaphoreType.DMA((n_chips,)), pltpu.SemaphoreType.DMA((n_chips,)), pltpu.SemaphoreType.DMA((n_chips,)),
                        pltpu.SemaphoreType.DMA((n_copies,)), pltpu.SemaphoreType.DMA((n_copies,))],
        compiler_params=_params(1),
    )(*dsegs, h, sum_g_bf)


def _adamw(w, g, m, v):
    m = ADAM_B1 * m + (1.0 - ADAM_B1) * g
    v = ADAM_B2 * v + (1.0 - ADAM_B2) * (g * g)
    m_hat = m / (1.0 - ADAM_B1**ADAM_STEP)
    v_hat = v / (1.0 - ADAM_B2**ADAM_STEP)
    delta = -ADAM_LR * (m_hat / (jnp.sqrt(v_hat) + ADAM_EPS) + ADAM_WD * w)
    return delta, m, v


def _w_in_update(own, land, w_t, m_t, v_t):
    n_slots, rows, cols = land.shape
    rb = 208
    spec = pl.BlockSpec((rb, cols), lambda i: (i, 0))

    def body(o_ref, l_ref, w_ref, m_ref, v_ref, g_ref, d_ref, m2_ref, v2_ref):
        total = o_ref[...]
        for s in range(n_slots):
            total = total + l_ref[s].astype(F32)
        g_ref[...] = total
        d_ref[...], m2_ref[...], v2_ref[...] = _adamw(w_ref[...], total, m_ref[...], v_ref[...])

    shape = jax.ShapeDtypeStruct((rows, cols), F32)
    return pl.pallas_call(
        body,
        name="w_in_update",
        grid=(rows // rb,),
        in_specs=[spec, pl.BlockSpec((n_slots, rb, cols), lambda i: (0, i, 0)), spec, spec, spec],
        out_specs=[spec] * 4,
        out_shape=[shape] * 4,
        compiler_params=_params(1),
    )(own, land, w_t, m_t, v_t)


def _shard_update(own_3, land_3, own_g, land_g, weights, m_in, v_in):
    n = len(weights)

    def body(*refs):
        o3_ref, l3_ref, og_ref, lg_ref = refs[:4]
        w_refs, m_refs, v_refs = (refs[4 + n * k : 4 + n * (k + 1)] for k in range(3))
        outs = refs[4 + 3 * n :]
        for k in range(n):
            if k < 3:
                g = o3_ref[k]
                for slot in range(3):
                    g = g + l3_ref[slot, k].astype(F32)
            else:
                g = og_ref[k - 3]
                for slot in range(3):
                    g = g + lg_ref[slot, k - 3].astype(F32)
            d, m2, v2 = _adamw(w_refs[k][0], g, m_refs[k][0], v_refs[k][0])
            for which, val in enumerate((g, d, m2, v2)):
                outs[which * n + k][0] = val

    shapes = [jax.ShapeDtypeStruct(w.shape, F32) for w in weights]
    res = pl.pallas_call(
        body, name="shard_update", out_shape=shapes * 4, compiler_params=pltpu.CompilerParams(vmem_limit_bytes=VMEM_LIMIT)
    )(own_3, land_3, own_g, land_g, *weights, *m_in, *v_in)
    return [res[n * k : n * (k + 1)] for k in range(4)]


VEC_LAYOUT = (("b_ada", 3 * D), ("norm_g", D), ("attn_sinks", 128), ("conv_b", D), ("rg_ba", D), ("rg_bx", D), ("rg_lambda", D),
              ("final_g", D), ("conv_w", CONV_WIDTH * D), ("loss", 128))
VEC_OFFSET = {name: sum(w for _, w in VEC_LAYOUT[:k]) for k, (name, _) in enumerate(VEC_LAYOUT)}
VEC_ROWS = 112
VEC_WIDTH = VEC_ROWS * 128
assert sum(w for _, w in VEC_LAYOUT) <= VEC_WIDTH
N_REPLICATED = 8


def _small_update(vec_all, c_all_t, me, rep_w, rep_m, rep_v, ada, conv):
    n_ada = ada[0].shape[1]
    n_conv = conv[0].shape[1]

    def body(*refs):
        me_ref, vec_ref, ct_ref = refs[:3]
        w_refs, m_refs, v_refs = (refs[3 + N_REPLICATED * k : 3 + N_REPLICATED * (k + 1)] for k in range(3))
        pos = 3 + 3 * N_REPLICATED
        ada_refs, conv_refs = refs[pos : pos + 3], refs[pos + 3 : pos + 6]
        outs = refs[pos + 6 :]
        loss_ref, rep_outs = outs[0], outs[1 : 1 + 4 * N_REPLICATED]
        ada_outs, conv_outs = outs[1 + 4 * N_REPLICATED : 5 + 4 * N_REPLICATED], outs[5 + 4 * N_REPLICATED : 9 + 4 * N_REPLICATED]
        conv_buf = outs[9 + 4 * N_REPLICATED]

        total = vec_ref[0:1, :]
        for slot in range(1, N_DEV):
            total = total + vec_ref[slot : slot + 1, :]
        loss_ref[...] = total[:, VEC_OFFSET["loss"] : VEC_OFFSET["loss"] + 1]
        for k, (name, _) in enumerate(VEC_LAYOUT[:N_REPLICATED]):
            width = w_refs[k].shape[1]
            g = total[:, VEC_OFFSET[name] : VEC_OFFSET[name] + width]
            d, m2, v2 = _adamw(w_refs[k][...], g, m_refs[k][...], v_refs[k][...])
            for which, val in enumerate((g, d, m2, v2)):
                rep_outs[which * N_REPLICATED + k][...] = val

        device = me_ref[0]
        for k in range(CONV_WIDTH):
            conv_buf[k : k + 1, :] = total[:, VEC_OFFSET["conv_w"] + D * k : VEC_OFFSET["conv_w"] + D * (k + 1)]
        g_conv = conv_buf[:, pl.ds(pl.multiple_of(device * n_conv, 128), n_conv)]
        for ref, val in zip(conv_outs, (g_conv,) + _adamw(conv_refs[0][...], g_conv, conv_refs[1][...], conv_refs[2][...])):
            ref[...] = val

        dmod = vec_ref[:, pl.ds(pl.multiple_of(device * n_ada, 128), n_ada)]
        g_ada = _dot(ct_ref[...], dmod)
        for ref, val in zip(ada_outs, (g_ada,) + _adamw(ada_refs[0][...], g_ada, ada_refs[1][...], ada_refs[2][...])):
            ref[...] = val

    vmem = pl.BlockSpec(memory_space=pltpu.VMEM)
    rep_shapes = [jax.ShapeDtypeStruct(w.shape, F32) for w in rep_w]
    out_shape = ([jax.ShapeDtypeStruct((1, 1), F32)] + rep_shapes * 4 + [jax.ShapeDtypeStruct(ada[0].shape, F32)] * 4
                 + [jax.ShapeDtypeStruct(conv[0].shape, F32)] * 4)
    n_in = 3 + 3 * N_REPLICATED + 6
    res = pl.pallas_call(
        body,
        name="small_update",
        in_specs=[pl.BlockSpec(memory_space=pltpu.SMEM)] + [vmem] * (n_in - 1),
        out_specs=[vmem] * len(out_shape),
        out_shape=out_shape,
        scratch_shapes=[pltpu.VMEM((CONV_WIDTH, D), F32)],
        compiler_params=pltpu.CompilerParams(vmem_limit_bytes=VMEM_LIMIT),
    )(me, vec_all, c_all_t, *rep_w, *rep_m, *rep_v, *ada, *conv)
    loss = res[0]
    reps = [res[1 + N_REPLICATED * k : 1 + N_REPLICATED * (k + 1)] for k in range(4)]
    return loss, reps, res[1 + 4 * N_REPLICATED : 5 + 4 * N_REPLICATED], res[5 + 4 * N_REPLICATED : 9 + 4 * N_REPLICATED]


def kernel(x, c, positions, w_ada, b_ada, norm_g, w_in, attn_sinks, conv_w, conv_b, rg_wa, rg_ba, rg_wx, rg_bx, rg_lambda, w_attn_proj, w_rnn_proj, w_out, final_g, loss_target, m_w_ada, m_b_ada, m_norm_g, m_w_in, m_attn_sinks, m_conv_w, m_conv_b, m_rg_wa, m_rg_ba, m_rg_wx, m_rg_bx, m_rg_lambda, m_w_attn_proj, m_w_rnn_proj, m_w_out, m_final_g, v_w_ada, v_b_ada, v_norm_g, v_w_in, v_attn_sinks, v_conv_w, v_conv_b, v_rg_wa, v_rg_ba, v_rg_wx, v_rg_bx, v_rg_lambda, v_w_attn_proj, v_w_rnn_proj, v_w_out, v_final_g):
    my_x, my_y, my_c = _position()
    me = _slot((my_x, my_y, my_c))
    x2, target = x[0], loss_target[0]
    pos = positions[0]

    gate_rows = lambda w: w[0].transpose(1, 0, 2).reshape(GATE_SHARD, D)
    w_pack = jnp.concatenate([w_attn_proj[0], w_rnn_proj[0], w_out[0], gate_rows(rg_wa), gate_rows(rg_wx)], axis=0).astype(BF16)
    conv_rows = jnp.pad(conv_w[0], ((0, 8 - CONV_WIDTH), (0, 0)))
    w_in_t = w_in[0].T
    c_rows, conv_all = _all_gather_direct("ag_small", [jnp.broadcast_to(c, (8, D)), conv_rows])
    c_all = c_rows[:, 0, :]
    conv_full = conv_all[:, :CONV_WIDTH, :].transpose(1, 0, 2).reshape(CONV_WIDTH, D)

    n_ada = w_ada.shape[2]
    b_cols = lax.dynamic_slice(b_ada, (0, me * n_ada), (1, n_ada))
    mod_cols = _ada_mod(c_all, w_ada[0], b_cols)
    (mod_all,) = _all_gather_direct("ag_mod", [mod_cols])
    mod = lax.dynamic_slice(mod_all, (0, me, 0), (N_DEV, 1, n_ada)).reshape(1, 3 * D)
    gate = mod[:, 2 * D :]

    tables = _rope_tables(pos)
    my_chip = _chip_of((my_x, my_y)).astype(jnp.int32).reshape(1)
    proj, h, w_t_all, w_all = _fwd_in(x2, mod, norm_g, w_in_t.astype(BF16), tables[0], w_pack, my_chip)
    o = _attn_fwd(proj, attn_sinks)
    keep = (pos != 0).astype(F32)[:, None]
    hh = _rnn_fwd(proj, keep, w_all, rg_ba, rg_bx, rg_lambda, conv_full, conv_b)

    dres, d_o, d_hh, d_ga, d_gr, d_ma, d_mr, gw3, mid_vec = _mid(x2, target, o, hh, proj, w_all, gate, final_g[None, :])
    index = jnp.stack([my_c, _chip_of((my_x, my_y))]).astype(jnp.int32)
    part_3 = gw3.reshape(3, 4, 2, ROW_SHARD, D)
    d_q, d_k, d_v, d_sink, recv_3 = _attn_bwd(proj, o, d_o, tables, attn_sinks, part_3)
    sum_3_bf, own_3 = _pair_add(
        "pair_add_3", part_3, recv_3, index, (3, 4), 1,
        pl.BlockSpec((None, None, None, ROW_SHARD, D), lambda a, q, idx: (a, q, idx[0], 0, 0)),
        pl.BlockSpec((None, None, ROW_SHARD, D), lambda a, q, idx: (a, q, 0, 0)),
        pl.BlockSpec((None, ROW_SHARD, D), lambda a, q, idx: (a, 0, 0)), (3, ROW_SHARD, D),
    )
    d_xr, g_gates, rnn_vec, land_3 = _rnn_bwd(proj, hh, d_hh, keep, w_all, rg_ba, rg_bx, rg_lambda, conv_full, conv_b, sum_3_bf)
    dsegs = (d_q, d_k, d_v, d_ga, d_xr, d_gr, d_ma, d_mr)
    part_g = g_gates.reshape(2, 4, 4, 2, GATE_SHARD, 256)
    (recv_g,) = _exchange("rs_pair_g", [part_g], [lambda ref, peer: ref.at[:, :, :, peer[2]]], [(2, 4, 4, GATE_SHARD, 256)], flips=(1,))
    sum_g_bf, own_g = _pair_add(
        "pair_add_g", part_g, recv_g[0], index, (4,), 0,
        pl.BlockSpec((2, 4, None, None, GATE_SHARD, 256), lambda q, idx: (0, 0, q, idx[0], 0, 0)),
        pl.BlockSpec((2, 4, None, GATE_SHARD, 256), lambda q, idx: (0, 0, q, 0, 0)),
        pl.BlockSpec((2, 4, GATE_SHARD, 256), lambda q, idx: (0, 0, 0, 0)), (2, 4, GATE_SHARD, 256),
    )
    part_w, recv_w, land_g = _bwd_gw(dsegs, h, sum_g_bf)
    rb = 208
    w_rows = pl.BlockSpec((None, rb, D), lambda r, q, idx: (q, r, 0))
    sum_w_bf, own_w = _pair_add(
        "pair_add_w", part_w, recv_w, index, (W_IN_SHARD // rb, 4), 1, w_rows, w_rows,
        pl.BlockSpec((rb, D), lambda r, q, idx: (r, 0)), (W_IN_SHARD, D),
    )
    grad_x, dh_vec, land_w = _bwd_dh(dsegs, w_t_all, x2, dres, mod, norm_g, sum_w_bf)

    pieces = dict(
        b_ada=jnp.concatenate([dh_vec[0:1], dh_vec[1:2], mid_vec[1:2]], axis=1),
        norm_g=dh_vec[2:3],
        attn_sinks=jnp.pad(d_sink[:, :, 0].reshape(1, 16), ((0, 0), (0, 112))),
        conv_b=rnn_vec[3:4],
        rg_ba=rnn_vec[0:1],
        rg_bx=rnn_vec[1:2],
        rg_lambda=rnn_vec[2:3],
        final_g=mid_vec[0:1],
        conv_w=rnn_vec[4:8].reshape(1, CONV_WIDTH * D),
        loss=mid_vec[2:3, 0:128],
    )
    vec = jnp.concatenate([pieces[name] for name, _ in VEC_LAYOUT], axis=1)
    vec = jnp.pad(vec, ((0, 0), (0, VEC_WIDTH - vec.shape[1]))).reshape(VEC_ROWS, 128)
    (vec_all,) = _all_gather_direct("ag_vec", [vec])
    rep_w = [b_ada, norm_g, attn_sinks, conv_b, rg_ba, rg_bx, rg_lambda, final_g[None, :]]
    rep_m = [m_b_ada, m_norm_g, m_attn_sinks, m_conv_b, m_rg_ba, m_rg_bx, m_rg_lambda, m_final_g[None, :]]
    rep_v = [v_b_ada, v_norm_g, v_attn_sinks, v_conv_b, v_rg_ba, v_rg_bx, v_rg_lambda, v_final_g[None, :]]
    loss, reps, ada_res, conv_res = _small_update(
        vec_all.reshape(N_DEV, VEC_WIDTH), c_all.T, me.astype(jnp.int32).reshape(1), rep_w, rep_m, rep_v,
        (w_ada[0], m_w_ada[0], v_w_ada[0]), (conv_w[0], m_conv_w[0], v_conv_w[0]),
    )

    w_in_res = _w_in_update(own_w, land_w, w_in_t, m_w_in[0].T, v_w_in[0].T)
    shard_res = _shard_update(
        own_3, land_3, own_g, land_g,
        (w_attn_proj, w_rnn_proj, w_out, rg_wa, rg_wx),
        (m_w_attn_proj, m_w_rnn_proj, m_w_out, m_rg_wa, m_rg_wx),
        (v_w_attn_proj, v_w_rnn_proj, v_w_out, v_rg_wa, v_rg_wx),
    )

    def by_weight(k):
        b, n, s, cb, ba, bx, lam, fg = reps[k]
        ap, rp, out, wa, wx = shard_res[k]
        return [ada_res[k][None], b, n, w_in_res[k].T[None], s, conv_res[k][None], cb, wa, ba, wx, bx, lam, ap, rp, out, fg[0]]

    return (loss[0, 0], grad_x[None], *by_weight(0), *by_weight(1), *by_weight(2), *by_weight(3))
```

```python
import functools
import math

import jax
import jax.numpy as jnp
import numpy as np
from jax import lax
from jax.experimental import pallas as pl
from jax.experimental.pallas import tpu as pltpu

F32 = jnp.float32
BF16 = jnp.bfloat16

D = 1024
HEAD_DIM = 64
ROT_DIM = 16
ROPE_THETA = 500000.0
WINDOW = 128
LRU_C = 8.0
CONV_WIDTH = 4
NORM_EPS = 1e-6
IN_W = 6656
SEG_Q, SEG_K, SEG_V, SEG_GA, SEG_XR, SEG_GR, SEG_MA, SEG_MR = 0, 1024, 1280, 1536, 2560, 3584, 4608, 5632
N_DEV = 8
W_IN_SHARD = IN_W // N_DEV
ROW_SHARD = D // N_DEV
GATE_SHARD = 256 // N_DEV

ADAM_LR = 0.001
ADAM_B1 = 0.9
ADAM_B2 = 0.999
ADAM_EPS = 1e-08
ADAM_WD = 0.01
ADAM_STEP = 10

VMEM_LIMIT = 60 * 1024 * 1024
MESH = pl.DeviceIdType.MESH
AXES = ("x", "y", "c")


def _dot(a, b):
    return jnp.dot(a, b, preferred_element_type=F32)


def _dot_nt(a, b):
    return lax.dot_general(a, b, (((1,), (1,)), ((), ())), preferred_element_type=F32)


def _dot_tn(a, b):
    return lax.dot_general(a, b, (((0,), (0,)), ((), ())), preferred_element_type=F32)


def _sigmoid(z):
    return 1.0 / (1.0 + jnp.exp(-z))


def _params(n_grid, vmem=VMEM_LIMIT):
    return pltpu.CompilerParams(dimension_semantics=("arbitrary",) * n_grid, vmem_limit_bytes=vmem)


def _position():
    return lax.axis_index("x"), lax.axis_index("y"), lax.axis_index("c")


def _flip(pos, k):
    x, y, c = pos
    return (1 - x if k & 4 else x, 1 - y if k & 2 else y, 1 - c if k & 1 else c)


def _slot(pos):
    return 4 * pos[0] + 2 * pos[1] + pos[2]


def _all_gather_direct(name, arrs):
    n = len(arrs)

    def body(*refs):
        ins, outs = refs[:n], refs[n : 2 * n]
        send_sems, recv_sems, local_sems = refs[2 * n :]
        me = _position()
        mine = [pltpu.make_async_copy(ins[a], outs[a].at[_slot(me)], local_sems.at[a]) for a in range(n)]
        for cp in mine:
            cp.start()

        def copy(a, k, block):
            return pltpu.make_async_remote_copy(
                src_ref=ins[a],
                dst_ref=outs[a].at[_slot(block)],
                send_sem=send_sems.at[a * 7 + k - 1],
                recv_sem=recv_sems.at[a * 7 + k - 1],
                device_id=_flip(me, k),
                device_id_type=MESH,
            )

        sent = [copy(a, k, me) for a in range(n) for k in range(1, N_DEV)]
        for cp in sent:
            cp.start()
        for a in range(n):
            for k in range(1, N_DEV):
                copy(a, k, _flip(me, k)).wait_recv()
        for cp in sent:
            cp.wait_send()
        for cp in mine:
            cp.wait()

    any_spec = pl.BlockSpec(memory_space=pl.ANY)
    return pl.pallas_call(
        body,
        name=name,
        out_shape=[jax.ShapeDtypeStruct((N_DEV,) + a.shape, a.dtype) for a in arrs],
        in_specs=[any_spec] * n,
        out_specs=[any_spec] * n,
        scratch_shapes=[
            pltpu.SemaphoreType.DMA((7 * n,)),
            pltpu.SemaphoreType.DMA((7 * n,)),
            pltpu.SemaphoreType.DMA((n,)),
        ],
    )(*arrs)


def _direct_gather_copies(src_hbm, all_hbm, send_sems, recv_sems, arriving):
    me = _position()
    return [
        pltpu.make_async_remote_copy(
            src_ref=src_hbm,
            dst_ref=all_hbm.at[_slot(_flip(me, k) if arriving else me)],
            send_sem=send_sems.at[k - 1],
            recv_sem=recv_sems.at[k - 1],
            device_id=_flip(me, k),
            device_id_type=MESH,
        )
        for k in range(1, N_DEV)
    ]


CHIP_FLIPS = (2, 4, 6)


def _chip_of(pos):
    return 2 * pos[0] + pos[1]


def _chip_copies(srcs, lands, views, send_sems, recv_sems):
    me = _position()
    copies = []
    for a, (src, land) in enumerate(zip(srcs, lands)):
        for j, k in enumerate(CHIP_FLIPS):
            peer = _flip(me, k)
            copies.append(
                pltpu.make_async_remote_copy(
                    src_ref=views[a](src, peer),
                    dst_ref=land.at[j],
                    send_sem=send_sems.at[len(CHIP_FLIPS) * a + j],
                    recv_sem=recv_sems.at[len(CHIP_FLIPS) * a + j],
                    device_id=peer,
                    device_id_type=MESH,
                )
            )
    return copies


def _exchange(name, arrs, views, out_shapes, flips):
    n = len(arrs)
    nf = len(flips)

    def body(*refs):
        ins, outs = refs[:n], refs[n : 2 * n]
        send_sems, recv_sems = refs[2 * n :]
        me = _position()

        def copy(a, j):
            peer = _flip(me, flips[j])
            return pltpu.make_async_remote_copy(
                src_ref=views[a](ins[a], peer),
                dst_ref=outs[a].at[j],
                send_sem=send_sems.at[a * nf + j],
                recv_sem=recv_sems.at[a * nf + j],
                device_id=peer,
                device_id_type=MESH,
            )

        sent = [copy(a, j) for a in range(n) for j in range(nf)]
        for cp in sent:
            cp.start()
        for cp in sent:
            cp.wait_recv()
        for cp in sent:
            cp.wait_send()

    any_spec = pl.BlockSpec(memory_space=pl.ANY)
    return pl.pallas_call(
        body,
        name=name,
        out_shape=[jax.ShapeDtypeStruct((nf,) + tuple(sh), a.dtype) for sh, a in zip(out_shapes, arrs)],
        in_specs=[any_spec] * n,
        out_specs=[any_spec] * n,
        scratch_shapes=[pltpu.SemaphoreType.DMA((nf * n,)), pltpu.SemaphoreType.DMA((nf * n,))],
    )(*arrs)


def _pair_add(name, own, recv, index, grid, chip_axis, own_spec, recv_spec, mine_spec, mine_shape):
    def body(idx_ref, own_ref, recv_ref, sb_ref, mine_ref):
        total = own_ref[...] + recv_ref[...].astype(F32)
        sb_ref[...] = total.astype(BF16)

        @pl.when(pl.program_id(chip_axis) == idx_ref[1])
        def _():
            mine_ref[...] = total

    return pl.pallas_call(
        body,
        name=name,
        grid_spec=pltpu.PrefetchScalarGridSpec(
            num_scalar_prefetch=1, grid=grid, in_specs=[own_spec, recv_spec], out_specs=[recv_spec, mine_spec]
        ),
        out_shape=[jax.ShapeDtypeStruct(recv.shape, BF16), jax.ShapeDtypeStruct(mine_shape, F32)],
        compiler_params=_params(len(grid)),
    )(index, own, recv)


def _ada_mod(c_all, w_ada, b_cols):
    def body(c_ref, w_ref, b_ref, o_ref):
        o_ref[...] = _dot(c_ref[...].astype(BF16), w_ref[...].astype(BF16)) + b_ref[...]

    return pl.pallas_call(
        body,
        name="ada_mod",
        out_shape=jax.ShapeDtypeStruct((N_DEV, w_ada.shape[1]), F32),
        compiler_params=pltpu.CompilerParams(vmem_limit_bytes=VMEM_LIMIT),
    )(c_all, w_ada, b_cols)


PACK_GATE = 3 * ROW_SHARD
PACK_ROWS = PACK_GATE + 2 * GATE_SHARD


def _fwd_in(x, mod, norm_g, w_shard, tables, my_chip):
    t = x.shape[0]
    tm = min(1024, t)
    tn = IN_W // 4
    n_i, n_j = t // tm, IN_W // tn
    n_rope = (SEG_V - SEG_Q) // 128

    def body(chip_ref, x_ref, mod_ref, ng_ref, cos_ref, s1_ref, s2_ref, shard_hbm,
             proj_ref, h_ref, wt_hbm, wbuf, hbuf, w_send, w_recv, load_sems, local_sem):
        j = pl.program_id(0)
        i = pl.program_id(1)
        me = _position()
        sibling = _flip(me, 1)
        chip = chip_ref[0]

        def shard_copy(k, block, to, src=None):
            return pltpu.make_async_remote_copy(
                src_ref=wt_hbm.at[_slot(block)] if src is None else src,
                dst_ref=wt_hbm.at[_slot(block)],
                send_sem=w_send.at[k],
                recv_sem=w_recv.at[k],
                device_id=to,
                device_id_type=MESH,
            )

        own_shard = pltpu.make_async_copy(shard_hbm, wt_hbm.at[_slot(me)], local_sem)

        def load_tile(q):
            loads = [pltpu.make_async_copy(wt_hbm.at[2 * q + c], wbuf.at[pl.ds(W_IN_SHARD * c, W_IN_SHARD), :], load_sems.at[c]) for c in range(2)]
            for cp in loads:
                cp.start()
            for cp in loads:
                cp.wait()

        @pl.when((j == 0) & (i == 0))
        def _():
            own_shard.start()
            shard_copy(0, me, sibling, src=shard_hbm).start()
            for rel in (1, 2):
                shard_copy(rel, me, _flip(me, 2 * rel), src=shard_hbm).start()
            own_shard.wait()
            shard_copy(0, sibling, me).wait_recv()
            load_tile(chip)

        for rel in range(1, 4):

            @pl.when((j == rel) & (i == 0))
            def _(rel=rel):
                if rel == 1:
                    shard_copy(3, me, _flip(me, 6), src=shard_hbm).start()
                there = _flip(me, 2 * rel)
                shard_copy(rel, there, me).wait_recv()
                shard_copy(3 + rel, there, sibling).start()
                shard_copy(3 + rel, _flip(sibling, 2 * rel), me).wait_recv()
                load_tile(chip ^ rel)

        rows = pl.ds(pl.multiple_of(i * tm, tm), tm)

        @pl.when(j == 0)
        def _():
            xv = x_ref[...]
            r = lax.rsqrt(jnp.mean(xv * xv, axis=-1, keepdims=True) + NORM_EPS)
            shift = mod_ref[:, 0:D]
            scale = mod_ref[:, D : 2 * D]
            hv = (((xv * r) * ng_ref[...]) * (1.0 + scale) + shift).astype(BF16)
            hbuf[rows, :] = hv
            h_ref[...] = hv

        proj_ref[...] = _dot_nt(hbuf[rows, :], wbuf[...])

        @pl.when((chip ^ j) == 0)
        def _():
            cos, s1, s2 = cos_ref[...], s1_ref[...], s2_ref[...]
            for g in range(n_rope):
                cols = slice(128 * g, 128 * (g + 1))
                proj_ref[:, cols] = _rope(proj_ref[:, cols], cos, s1, s2)

        @pl.when((j == n_j - 1) & (i == n_i - 1))
        def _():
            shard_copy(0, me, sibling, src=shard_hbm).wait_send()
            for rel in range(1, 4):
                shard_copy(rel, me, _flip(me, 2 * rel), src=shard_hbm).wait_send()
                shard_copy(3 + rel, _flip(me, 2 * rel), sibling).wait_send()

    tab = pl.BlockSpec((tm, 128), lambda j, i, c: (i, 0))
    any_spec = pl.BlockSpec(memory_space=pl.ANY)
    return pl.pallas_call(
        body,
        name="fwd_in",
        grid_spec=pltpu.PrefetchScalarGridSpec(
            num_scalar_prefetch=1,
            grid=(n_j, n_i),
            in_specs=[
                pl.BlockSpec((tm, D), lambda j, i, c: (jnp.where(j == 0, i, 0), 0)),
                pl.BlockSpec((1, 3 * D), lambda j, i, c: (0, 0)),
                pl.BlockSpec((1, D), lambda j, i, c: (0, 0)),
                tab, tab, tab, any_spec,
            ],
            out_specs=[pl.BlockSpec((tm, tn), lambda j, i, c: (i, c[0] ^ j)),
                       pl.BlockSpec((tm, D), lambda j, i, c: (jnp.where(j == 0, i, n_i - 1), 0)),
                       any_spec],
            scratch_shapes=[
                pltpu.VMEM((tn, D), BF16),
                pltpu.VMEM((t, D), BF16),
                pltpu.SemaphoreType.DMA((N_DEV - 1,)),
                pltpu.SemaphoreType.DMA((N_DEV - 1,)),
                pltpu.SemaphoreType.DMA((2,)),
                pltpu.SemaphoreType.DMA,
            ],
        ),
        out_shape=[jax.ShapeDtypeStruct((t, IN_W), F32), jax.ShapeDtypeStruct((t, D), BF16),
                   jax.ShapeDtypeStruct((N_DEV,) + w_shard.shape, w_shard.dtype)],
        compiler_params=_params(2),
    )(my_chip, x, mod, norm_g, *tables, w_shard)


def _rope_tables(positions):
    half = ROT_DIM // 2
    inv_freq = ROPE_THETA ** (-jnp.arange(0, ROT_DIM, 2, dtype=F32) / ROT_DIM)
    pos = positions.astype(F32)
    lane = np.arange(128) % HEAD_DIM
    freq_lane = jnp.where(lane < ROT_DIM, jnp.tile(inv_freq, 128 // half), 0.0)
    ang = pos[:, None] * freq_lane[None, :]
    sin = jnp.sin(ang)
    lane_tables = (jnp.cos(ang), jnp.where(lane < half, -sin, 0.0), jnp.where((lane >= half) & (lane < ROT_DIM), sin, 0.0))
    ang_t = inv_freq[:, None] * pos[None, :]
    return lane_tables, (jnp.cos(ang_t), jnp.sin(ang_t))


def _rope(tv, cos, s1, s2):
    cols = []
    for g in range(tv.shape[1] // 128):
        blk = tv[:, 128 * g : 128 * (g + 1)]
        cols.append(blk * cos + pltpu.roll(blk, 128 - 8, 1) * s1 + pltpu.roll(blk, 8, 1) * s2)
    return cols[0] if len(cols) == 1 else jnp.concatenate(cols, axis=1)


def _rope_transposed(gv, cos, s1, s2):
    cols = []
    for g in range(gv.shape[1] // 128):
        blk = gv[:, 128 * g : 128 * (g + 1)]
        cols.append(blk * cos + pltpu.roll(blk * s1, 8, 1) + pltpu.roll(blk * s2, 128 - 8, 1))
    return cols[0] if len(cols) == 1 else jnp.concatenate(cols, axis=1)


def _lane_lo():
    return lax.broadcasted_iota(jnp.int32, (1, 128), 1) < HEAD_DIM


def _row_lo():
    return lax.broadcasted_iota(jnp.int32, (128, 1), 0) < HEAD_DIM


def _split_heads(tile, j):
    lo = _lane_lo()
    parts = []
    for g in (2 * j, 2 * j + 1):
        blk = tile[:, 128 * g : 128 * (g + 1)]
        parts += [jnp.where(lo, blk, 0.0), jnp.where(lo, 0.0, blk)]
    return jnp.concatenate(parts, axis=0)


def _dup_lanes(kv, j):
    rolled = pltpu.roll(kv, HEAD_DIM, 1)
    lo = _lane_lo()
    return jnp.where(lo, kv, rolled) if j == 0 else jnp.where(lo, rolled, kv)


def _fold_lanes(acc):
    return acc + pltpu.roll(acc, HEAD_DIM, 1)


def _fill_bias_t(bias_ref):
    kj = lax.broadcasted_iota(jnp.int32, (256, 128), 0)
    qi = lax.broadcasted_iota(jnp.int32, (256, 128), 1)
    diff = qi + WINDOW - kj
    band = (diff >= 0) & (diff < WINDOW)
    bias_ref[0] = jnp.where(band, 0.0, -1e30)
    bias_ref[1] = jnp.where(band & (kj >= 128), 0.0, -1e30)


def _sink_row(sink_ref, pair, j):
    return jnp.concatenate([jnp.full((1, 128), sink_ref[0, 8 * pair + 4 * j + h], F32) for h in range(4)], axis=1)


def _softmax_t(q4, kw, bias4, sink):
    s = _dot_nt(kw, q4) + bias4
    m = jnp.maximum(jnp.max(s, axis=0, keepdims=True), sink)
    p = jnp.exp(s - m)
    e_sink = jnp.exp(sink - m)
    denom = jnp.sum(p, axis=0, keepdims=True) + e_sink
    return p, e_sink, denom


def _attn_specs(t, tq):
    return dict(
        q=pl.BlockSpec((tq, 512), lambda p, i: (i, p)),
        k=pl.BlockSpec((t, 128), lambda p, i: (0, SEG_K // 128 + p)),
        v=pl.BlockSpec((t, 128), lambda p, i: (0, SEG_V // 128 + p)),
        sink=pl.BlockSpec(memory_space=pltpu.SMEM),
        act=pl.BlockSpec((tq, 512), lambda p, i: (i, p)),
    )


def _attn_fwd(proj, sinks, pack):
    t = proj.shape[0]
    tq = min(512, t)
    nb = tq // 128
    n_tiles = t // tq
    sp = _attn_specs(t, tq)

    def body(q_ref, k_ref, v_ref, sink_ref, pack_hbm, o_ref, all_hbm, kd, vdt, bias_ref, send_sems, recv_sems, local_sem):
        pair = pl.program_id(0)
        i = pl.program_id(1)
        own_pack = pltpu.make_async_copy(pack_hbm, all_hbm.at[_slot(_position())], local_sem)

        @pl.when((pair == 0) & (i == 0))
        def _():
            own_pack.start()
            for cp in _direct_gather_copies(pack_hbm, all_hbm, send_sems, recv_sems, arriving=False):
                cp.start()

        @pl.when((pair == 1) & (i == n_tiles - 1))
        def _():
            for cp in _direct_gather_copies(pack_hbm, all_hbm, send_sems, recv_sems, arriving=True):
                cp.wait_recv()
            for cp in _direct_gather_copies(pack_hbm, all_hbm, send_sems, recv_sems, arriving=False):
                cp.wait_send()
            own_pack.wait()

        @pl.when(i == 0)
        def _():
            kv, vv = k_ref[...], v_ref[...]
            for j in range(2):
                kd[j, 0:128, :] = jnp.zeros((128, 128), BF16)
                kd[j, 128:, :] = _dup_lanes(kv, j).astype(BF16)
                vdt[j, :, 0:128] = jnp.zeros((128, 128), BF16)
                vdt[j, :, 128:] = _dup_lanes(vv, j).T.astype(BF16)
            _fill_bias_t(bias_ref)

        row_lo = _row_lo()

        def block(b, carry):
            r0 = pl.multiple_of(b * 128, 128)
            gb = i * nb + b
            keys = pl.ds(pl.multiple_of(gb * 128, 128), 256)
            rows = pl.ds(r0, 128)
            qt = q_ref[rows, :] * (1.0 / math.sqrt(HEAD_DIM))
            bias = bias_ref[jnp.where(gb == 0, 1, 0)]
            bias4 = jnp.concatenate([bias] * 4, axis=1)
            cols = []
            for j in range(2):
                q4 = _split_heads(qt, j).astype(BF16)
                p, _, denom = _softmax_t(q4, kd[j, keys, :], bias4, _sink_row(sink_ref, pair, j))
                ot = _dot(vdt[j, :, keys], p.astype(BF16)) * (1.0 / denom)
                for gi in range(2):
                    z = jnp.where(row_lo, ot[:, 256 * gi : 256 * gi + 128], ot[:, 256 * gi + 128 : 256 * gi + 256])
                    cols.append(z.T)
            o_ref[rows, :] = jnp.concatenate(cols, axis=1)
            return carry

        lax.fori_loop(0, nb, block, 0, unroll=True)

    return pl.pallas_call(
        body,
        name="attn_fwd",
        grid=(2, n_tiles),
        in_specs=[sp["q"], sp["k"], sp["v"], sp["sink"], pl.BlockSpec(memory_space=pl.ANY)],
        out_specs=[sp["act"], pl.BlockSpec(memory_space=pl.ANY)],
        out_shape=[jax.ShapeDtypeStruct((t, D), F32), jax.ShapeDtypeStruct((N_DEV,) + pack.shape, pack.dtype)],
        scratch_shapes=[pltpu.VMEM((2, t + 128, 128), BF16), pltpu.VMEM((2, 128, t + 128), BF16), pltpu.VMEM((2, 256, 128), F32),
                        pltpu.SemaphoreType.DMA((N_DEV - 1,)), pltpu.SemaphoreType.DMA((N_DEV - 1,)), pltpu.SemaphoreType.DMA],
        compiler_params=_params(2),
    )(proj, proj, proj, sinks, pack)


def _attn_bwd(proj, o, d_o, tables, sinks, part_3):
    t = proj.shape[0]
    tq = min(512, t)
    nb = tq // 128
    n_tiles = t // tq
    (cos, s1, s2), (cos_t, sin_t) = tables
    sp = _attn_specs(t, tq)
    scale = 1.0 / math.sqrt(HEAD_DIM)
    half = ROT_DIM // 2

    def unrope_rows(z, c, s):
        out = []
        for base in (0, HEAD_DIM):
            g0, g1 = z[base : base + half], z[base + half : base + ROT_DIM]
            out += [g0 * c + g1 * s, g1 * c - g0 * s, z[base + ROT_DIM : base + HEAD_DIM]]
        return jnp.concatenate(out, axis=0)

    def swap(part_hbm, recv_hbm, send_sem, recv_sem):
        sibling = _flip(_position(), 1)
        return pltpu.make_async_remote_copy(
            src_ref=part_hbm.at[:, :, sibling[2]], dst_ref=recv_hbm, send_sem=send_sem, recv_sem=recv_sem,
            device_id=sibling, device_id_type=MESH,
        )

    def body(q_ref, k_ref, v_ref, o_ref, do_ref, ct_ref, st_ref, ck, s1k, s2k, sink_ref, part_hbm,
             dq_ref, dk_ref, dv_ref, dsink_ref, recv_hbm, kd, kdt, vd, bias_ref, dkacc, dvacc, send_sem, recv_sem):
        pair = pl.program_id(0)
        i = pl.program_id(1)

        @pl.when((pair == 0) & (i == 0))
        def _():
            swap(part_hbm, recv_hbm, send_sem, recv_sem).start()

        @pl.when((pair == 1) & (i == n_tiles - 1))
        def _():
            swap(part_hbm, recv_hbm, send_sem, recv_sem).wait()

        @pl.when(i == 0)
        def _():
            kv, vv = k_ref[...], v_ref[...]
            for j in range(2):
                kdup = _dup_lanes(kv, j)
                kd[j, 0:128, :] = jnp.zeros((128, 128), BF16)
                kd[j, 128:, :] = kdup.astype(BF16)
                kdt[j, :, 0:128] = jnp.zeros((128, 128), BF16)
                kdt[j, :, 128:] = kdup.T.astype(BF16)
                vd[j, 0:128, :] = jnp.zeros((128, 128), BF16)
                vd[j, 128:, :] = _dup_lanes(vv, j).astype(BF16)
            _fill_bias_t(bias_ref)
            dkacc[...] = jnp.zeros_like(dkacc)
            dvacc[...] = jnp.zeros_like(dvacc)
            dsink_ref[...] = jnp.zeros_like(dsink_ref)

        row_lo = _row_lo()
        ones = jnp.ones((8, 128), BF16)

        def block(b, carry):
            r0 = pl.multiple_of(b * 128, 128)
            gb = i * nb + b
            keys = pl.ds(pl.multiple_of(gb * 128, 128), 256)
            rows = pl.ds(r0, 128)
            qt = q_ref[rows, :] * scale
            dob = do_ref[rows, :]
            prod = dob * o_ref[rows, :]
            ct, st = ct_ref[:, rows], st_ref[:, rows]
            bias = bias_ref[jnp.where(gb == 0, 1, 0)]
            bias4 = jnp.concatenate([bias] * 4, axis=1)
            cols = []
            for j in range(2):
                q4 = _split_heads(qt, j).astype(BF16)
                kw = kd[j, keys, :]
                p, e_sink, denom = _softmax_t(q4, kw, bias4, _sink_row(sink_ref, pair, j))
                inv = 1.0 / denom
                prob = p * inv
                do4 = _split_heads(dob, j).astype(BF16)
                delta = _dot_nt(ones, _split_heads(prod, j).astype(BF16))[0:1, :]
                dp = _dot_nt(vd[j, keys, :], do4)
                ds = (prob * (dp - delta)).astype(BF16)
                dkacc[j, keys, :] = dkacc[j, keys, :] + _dot(ds, q4)
                dvacc[j, keys, :] = dvacc[j, keys, :] + _dot(prob.astype(BF16), do4)
                dqt = _dot(kdt[j, :, keys], ds) * scale
                dsink = -(e_sink * inv) * delta
                for h in range(4):
                    row = 4 * j + h
                    total = jnp.sum(dsink[:, 128 * h : 128 * (h + 1)], axis=1, keepdims=True)
                    dsink_ref[0, row : row + 1, :] = dsink_ref[0, row : row + 1, :] + jnp.broadcast_to(total, (1, 128))
                for gi in range(2):
                    z = jnp.where(row_lo, dqt[:, 256 * gi : 256 * gi + 128], dqt[:, 256 * gi + 128 : 256 * gi + 256])
                    cols.append(unrope_rows(z, ct, st).T)
            dq_ref[rows, :] = jnp.concatenate(cols, axis=1).astype(BF16)
            return carry

        lax.fori_loop(0, nb, block, 0, unroll=True)

        @pl.when(i == n_tiles - 1)
        def _():
            lo = _lane_lo()
            dk = jnp.where(lo, _fold_lanes(dkacc[0, 128:, :]), _fold_lanes(dkacc[1, 128:, :]))
            dk_ref[...] = _rope_transposed(dk, ck[...], s1k[...], s2k[...]).astype(BF16)
            dv_ref[...] = jnp.where(lo, _fold_lanes(dvacc[0, 128:, :]), _fold_lanes(dvacc[1, 128:, :])).astype(BF16)

    kv_out = pl.BlockSpec((t, 128), lambda p, i: (0, p))
    tab_t = pl.BlockSpec((half, tq), lambda p, i: (0, i))
    tab_k = pl.BlockSpec((t, 128), lambda p, i: (0, 0))
    return pl.pallas_call(
        body,
        name="attn_bwd",
        grid=(2, n_tiles),
        in_specs=[sp["q"], sp["k"], sp["v"], sp["act"], sp["act"], tab_t, tab_t, tab_k, tab_k, tab_k, sp["sink"],
                  pl.BlockSpec(memory_space=pl.ANY)],
        out_specs=[sp["act"], kv_out, kv_out, pl.BlockSpec((1, 8, 128), lambda p, i: (p, 0, 0)), pl.BlockSpec(memory_space=pl.ANY)],
        out_shape=[
            jax.ShapeDtypeStruct((t, D), BF16),
            jax.ShapeDtypeStruct((t, 256), BF16),
            jax.ShapeDtypeStruct((t, 256), BF16),
            jax.ShapeDtypeStruct((2, 8, 128), F32),
            jax.ShapeDtypeStruct((3, 4, ROW_SHARD, D), F32),
        ],
        scratch_shapes=[pltpu.VMEM((2, t + 128, 128), BF16), pltpu.VMEM((2, 128, t + 128), BF16), pltpu.VMEM((2, t + 128, 128), BF16),
                        pltpu.VMEM((2, 256, 128), F32), pltpu.VMEM((2, t + 128, 128), F32), pltpu.VMEM((2, t + 128, 128), F32),
                        pltpu.SemaphoreType.DMA, pltpu.SemaphoreType.DMA],
        compiler_params=_params(2),
    )(proj, proj, proj, o, d_o, cos_t, sin_t, cos, s1, s2, sinks, part_3)


HALF = 512


def _softplus(y):
    u = jnp.exp(-jnp.abs(y))
    w = 1.0 + u
    log1p = jnp.where(w == 1.0, u, jnp.log(w) * (u / jnp.where(w == 1.0, 1.0, w - 1.0)))
    return jnp.maximum(y, 0.0) + log1p


def _gate_block(w_ref, blk):
    return w_ref[:, :, 256 * blk : 256 * (blk + 1)].reshape(256, 256)


def _gate_matmul(act_bf, w_ref):
    return jnp.concatenate([_dot(act_bf[:, 256 * blk : 256 * (blk + 1)], _gate_block(w_ref, blk)) for blk in range(2)], axis=1)


def _gate_matmul_t(grad_bf, w_ref):
    return jnp.concatenate([_dot_nt(grad_bf[:, 256 * blk : 256 * (blk + 1)], _gate_block(w_ref, blk)) for blk in range(2)], axis=1)


def _conv_and_gates(xe_ref, tm, cw_ref, cb_ref, wa_ref, wx_ref, ba_ref, bx_ref, lam_ref, keep):
    xc = cb_ref[...] + sum(cw_ref[k : k + 1, :] * xe_ref[pl.ds(5 + k, tm), :] for k in range(CONV_WIDTH))
    xc_bf = xc.astype(BF16)
    r = _sigmoid(_gate_matmul(xc_bf, wa_ref) + ba_ref[...])
    ig = _sigmoid(_gate_matmul(xc_bf, wx_ref) + bx_ref[...])
    sp = _softplus(-lam_ref[...])
    log_a = -LRU_C * r * sp
    tanh = jnp.tanh(-log_a)
    mult = jnp.sqrt(2.0 * tanh / (1.0 + tanh))
    a = jnp.where(keep, jnp.exp(log_a), 0.0)
    mult = jnp.where(keep, mult, 1.0)
    return xc, xc_bf, r, ig, sp, a, mult


def _scan_rows(a_ref, b_ref, h_ref, carry_ref, tm, reverse):
    width = a_ref.shape[1]
    row = lax.broadcasted_iota(jnp.int32, (8, width), 0)

    def group(g, carry):
        gg = tm // 8 - 1 - g if reverse else g
        r0 = pl.multiple_of(gg * 8, 8)
        a = a_ref[pl.ds(r0, 8), :]
        b = b_ref[pl.ds(r0, 8), :]
        for s in (1, 2, 4):
            shift = 8 - s if reverse else s
            a_sh = pltpu.roll(a, shift, 0)
            b_sh = pltpu.roll(b, shift, 0)
            ok = (row < 8 - s) if reverse else (row >= s)
            b = jnp.where(ok, a * b_sh + b, b)
            a = jnp.where(ok, a * a_sh, a)
        h = a * carry + b
        h_ref[pl.ds(r0, 8), :] = h
        last = h[0:1, :] if reverse else h[7:8, :]
        return jnp.broadcast_to(last, (8, width))

    carry_ref[...] = lax.fori_loop(0, tm // 8, group, carry_ref[...])


def _rnn_specs(t, tm, order):
    n = t // tm
    return dict(
        tile=pl.BlockSpec((tm, HALF), lambda ch, i: (order(i, n), ch)),
        before=lambda col0: pl.BlockSpec((8, HALF), lambda ch, i: (jnp.maximum(order(i, n) * (tm // 8) - 1, 0), col0 + ch)),
        proj_tile=lambda col0: pl.BlockSpec((tm, HALF), lambda ch, i: (order(i, n), col0 + ch)),
        keep=pl.BlockSpec((tm, 1), lambda ch, i: (order(i, n), 0)),
        gates=lambda which: pl.BlockSpec((N_DEV, GATE_SHARD, HALF), lambda ch, i: (0, PACK_GATE // GATE_SHARD + which, ch)),
        gate_grads=pl.BlockSpec((2, 2, 256, 256), lambda ch, i: (0, ch, 0, 0)),
        vec=pl.BlockSpec((1, HALF), lambda ch, i: (0, ch)),
        conv=pl.BlockSpec((CONV_WIDTH, HALF), lambda ch, i: (0, ch)),
    )


def _rnn_fwd(proj, keep, w_all, rg_ba, rg_bx, rg_lambda, conv_w, conv_b):
    t = proj.shape[0]
    tm = min(512, t)
    sp = _rnn_specs(t, tm, lambda i, n: i)

    def body(xr_ref, xb_ref, keep_ref, wa_ref, wx_ref, ba_ref, bx_ref, lam_ref, cw_ref, cb_ref, hh_ref, xe, a_buf, b_buf, carry):
        i = pl.program_id(1)

        @pl.when(i == 0)
        def _():
            carry[...] = jnp.zeros_like(carry)

        xe[0:8, :] = jnp.where(i > 0, xb_ref[...], 0.0)
        xe[8:, :] = xr_ref[...]
        keep_rows = keep_ref[...] != 0.0
        xc, _, _, ig, _, a, mult = _conv_and_gates(xe, tm, cw_ref, cb_ref, wa_ref, wx_ref, ba_ref, bx_ref, lam_ref, keep_rows)
        a_buf[...] = a
        b_buf[...] = mult * (ig * xc)
        _scan_rows(a_buf, b_buf, hh_ref, carry, tm, reverse=False)

    return pl.pallas_call(
        body,
        name="rnn_fwd",
        grid=(2, t // tm),
        in_specs=[sp["proj_tile"](SEG_XR // HALF), sp["before"](SEG_XR // HALF), sp["keep"], sp["gates"](0), sp["gates"](1),
                  sp["vec"], sp["vec"], sp["vec"], sp["conv"], sp["vec"]],
        out_specs=sp["tile"],
        out_shape=jax.ShapeDtypeStruct((t, D), F32),
        scratch_shapes=[pltpu.VMEM((tm + 8, HALF), F32), pltpu.VMEM((tm, HALF), F32), pltpu.VMEM((tm, HALF), F32), pltpu.VMEM((8, HALF), F32)],
        compiler_params=_params(2),
    )(proj, proj, keep, w_all, w_all, rg_ba, rg_bx, rg_lambda, conv_w, conv_b)


def _rnn_bwd(proj, hh, d_hh, keep, w_all, rg_ba, rg_bx, rg_lambda, conv_w, conv_b, sum_3_bf):
    t = proj.shape[0]
    tm = min(512, t)
    n_tiles = t // tm
    sp = _rnn_specs(t, tm, lambda i, n: n - 1 - i)

    def body(xr_ref, xb_ref, hh_ref, hb_ref, dhh_ref, keep_ref, wa_ref, wx_ref, ba_ref, bx_ref, lam_ref, cw_ref, cb_ref,
             s3_hbm, dxr_ref, gw_ref, gvec_ref, l3_hbm, xe, he, ae, c_buf, g_buf, dxe, carry, a_next, send_sems, recv_sems):
        ch = pl.program_id(0)
        i = pl.program_id(1)
        first_tile = i == n_tiles - 1
        views = [lambda ref, peer: ref.at[:, _chip_of(peer)]]

        @pl.when((ch == 0) & (i == 0))
        def _():
            for cp in _chip_copies([s3_hbm], [l3_hbm], views, send_sems, recv_sems):
                cp.start()

        @pl.when((ch == 1) & (i == n_tiles - 1))
        def _():
            for cp in _chip_copies([s3_hbm], [l3_hbm], views, send_sems, recv_sems):
                cp.wait_recv()
                cp.wait_send()

        @pl.when(i == 0)
        def _():
            carry[...] = jnp.zeros_like(carry)
            a_next[...] = jnp.zeros_like(a_next)
            dxe[tm:, :] = jnp.zeros((8, HALF), F32)
            gw_ref[...] = jnp.zeros_like(gw_ref)
            gvec_ref[...] = jnp.zeros_like(gvec_ref)

        xe[0:8, :] = jnp.where(first_tile, 0.0, xb_ref[...])
        xe[8:, :] = xr_ref[...]
        he[0:8, :] = jnp.where(first_tile, 0.0, hb_ref[...])
        he[8:, :] = hh_ref[...]
        keep_rows = keep_ref[...] != 0.0
        xc, xc_bf, r, ig, sp_, a, mult = _conv_and_gates(xe, tm, cw_ref, cb_ref, wa_ref, wx_ref, ba_ref, bx_ref, lam_ref, keep_rows)

        ae[0:tm, :] = a
        ae[tm:, :] = a_next[...]
        a_next[...] = jnp.broadcast_to(a[0:1, :], (8, HALF))
        c_buf[...] = ae[pl.ds(1, tm), :]
        g_buf[...] = dhh_ref[...]
        _scan_rows(c_buf, g_buf, g_buf, carry, tm, reverse=True)
        g = g_buf[...]

        h_prev = he[pl.ds(7, tm), :]
        bb = ig * xc
        d_ig = g * mult * xc
        d_xc = g * mult * ig
        d_log_a = jnp.where(keep_rows, g * h_prev * a - (g * bb) * (a * a) / mult, 0.0)
        d_r = d_log_a * (-LRU_C * sp_)
        d_sp = jnp.sum(d_log_a * (-LRU_C * r), axis=0, keepdims=True)
        d_zr = d_r * r * (1.0 - r)
        d_zi = d_ig * ig * (1.0 - ig)
        d_zr_bf = d_zr.astype(BF16)
        d_zi_bf = d_zi.astype(BF16)
        d_xc = d_xc + _gate_matmul_t(d_zr_bf, wa_ref) + _gate_matmul_t(d_zi_bf, wx_ref)
        for blk in range(2):
            cols = slice(256 * blk, 256 * (blk + 1))
            gw_ref[0, blk] = gw_ref[0, blk] + _dot_tn(xc_bf[:, cols], d_zr_bf[:, cols])
            gw_ref[1, blk] = gw_ref[1, blk] + _dot_tn(xc_bf[:, cols], d_zi_bf[:, cols])

        dxe[0:tm, :] = d_xc
        dxr = sum(cw_ref[k : k + 1, :] * dxe[pl.ds(3 - k, tm), :] for k in range(CONV_WIDTH))
        dxr_ref[...] = dxr.astype(BF16)
        dxe[tm:, :] = d_xc[0:8, :]

        lam = lam_ref[...]
        sums = [
            jnp.sum(d_zr, axis=0, keepdims=True),
            jnp.sum(d_zi, axis=0, keepdims=True),
            d_sp * (-_sigmoid(-lam)),
            jnp.sum(d_xc, axis=0, keepdims=True),
        ] + [jnp.sum(d_xc * xe[pl.ds(5 + k, tm), :], axis=0, keepdims=True) for k in range(CONV_WIDTH)]
        gvec_ref[...] = gvec_ref[...] + jnp.concatenate(sums, axis=0)

    return pl.pallas_call(
        body,
        name="rnn_bwd",
        grid=(2, n_tiles),
        in_specs=[
            sp["proj_tile"](SEG_XR // HALF), sp["before"](SEG_XR // HALF), sp["tile"], sp["before"](0), sp["tile"], sp["keep"],
            sp["gates"](0), sp["gates"](1), sp["vec"], sp["vec"], sp["vec"], sp["conv"], sp["vec"], pl.BlockSpec(memory_space=pl.ANY),
        ],
        out_specs=[sp["tile"], sp["gate_grads"], pl.BlockSpec((8, HALF), lambda ch, i: (0, ch)), pl.BlockSpec(memory_space=pl.ANY)],
        out_shape=[
            jax.ShapeDtypeStruct((t, D), BF16),
            jax.ShapeDtypeStruct((2, 4, 256, 256), F32),
            jax.ShapeDtypeStruct((8, D), F32),
            jax.ShapeDtypeStruct((len(CHIP_FLIPS), 3, ROW_SHARD, D), BF16),
        ],
        scratch_shapes=[
            pltpu.VMEM((tm + 8, HALF), F32),
            pltpu.VMEM((tm + 8, HALF), F32),
            pltpu.VMEM((tm + 8, HALF), F32),
            pltpu.VMEM((tm, HALF), F32),
            pltpu.VMEM((tm, HALF), F32),
            pltpu.VMEM((tm + 8, HALF), F32),
            pltpu.VMEM((8, HALF), F32),
            pltpu.VMEM((8, HALF), F32),
            pltpu.SemaphoreType.DMA((len(CHIP_FLIPS),)),
            pltpu.SemaphoreType.DMA((len(CHIP_FLIPS),)),
        ],
        compiler_params=_params(2),
    )(proj, proj, hh, hh, d_hh, keep, w_all, w_all, rg_ba, rg_bx, rg_lambda, conv_w, conv_b, sum_3_bf)


def _mid(x, target, o, hh, proj, w_all, gate, final_g):
    t = x.shape[0]
    tm = min(256, t)
    n_tiles = t // tm
    full = pl.BlockSpec((tm, D), lambda i: (i, 0))
    half = lambda col: pl.BlockSpec((tm, 512), lambda i: (i, col))
    vec = pl.BlockSpec((1, D), lambda i: (0, 0))
    any_spec = pl.BlockSpec(memory_space=pl.ANY)

    def body(x_ref, tg_ref, o_ref, hh_ref, ga0, ga1, gr0, gr1, ma0, ma1, mr0, mr1, w_hbm, gate_ref, fg_ref,
             dres_ref, do_ref, dhh_ref, dga_ref, dgr_ref, dma_ref, dmr_ref, gw_hbm, vec_ref, w, gw, sem, w_sems):
        i = pl.program_id(0)

        @pl.when(i == 0)
        def _():
            loads = [
                pltpu.make_async_copy(
                    w_hbm.at[s, pl.ds(ROW_SHARD * k, ROW_SHARD), :], w.at[k, pl.ds(ROW_SHARD * s, ROW_SHARD), :], w_sems.at[k * N_DEV + s]
                )
                for k in range(3)
                for s in range(N_DEV)
            ]
            for cp in loads:
                cp.start()
            for cp in loads:
                cp.wait()
            gw[...] = jnp.zeros_like(gw)
            vec_ref[...] = jnp.zeros_like(vec_ref)

        cat = lambda a, b: jnp.concatenate([a[...], b[...]], axis=1)
        ga, gr, ma, mr = cat(ga0, ga1), cat(gr0, gr1), cat(ma0, ma1), cat(mr0, mr1)
        ov, hv, gate_v, fg = o_ref[...], hh_ref[...], gate_ref[...], fg_ref[...]
        sg_a = _sigmoid(ga)
        sg_r = _sigmoid(gr)
        silu_a = ga * sg_a
        silu_r = gr * sg_r
        ya = (ov * silu_a).astype(BF16)
        yr = (hv * silu_r).astype(BF16)
        pa = _dot(ya, w[0])
        pr = _dot(yr, w[1])
        sa = _sigmoid(ma)
        sr = _sigmoid(mr)
        merged = (sa * pa + sr * pr).astype(BF16)
        mo = _dot(merged, w[2])
        out = x_ref[...] + gate_v * mo
        r2 = lax.rsqrt(jnp.mean(out * out, axis=-1, keepdims=True) + NORM_EPS)
        outn = out * r2
        err = outn * fg - tg_ref[...]
        loss = 0.5 * jnp.sum(jnp.mean(err * err, axis=-1, keepdims=True), axis=0, keepdims=True)
        dy = err * (1.0 / D)
        g_final = jnp.sum(dy * outn, axis=0, keepdims=True)
        doutn = dy * fg
        dout = r2 * (doutn - outn * jnp.mean(doutn * outn, axis=-1, keepdims=True))
        dres_ref[...] = dout
        d_gate = jnp.sum(dout * mo, axis=0, keepdims=True)
        dmo = (dout * gate_v).astype(BF16)
        dmerged = _dot_nt(dmo, w[2])
        gw[2] = gw[2] + _dot_tn(merged, dmo)
        dpa = (dmerged * sa).astype(BF16)
        dpr = (dmerged * sr).astype(BF16)
        dma_ref[...] = (dmerged * pa * (sa * (1.0 - sa))).astype(BF16)
        dmr_ref[...] = (dmerged * pr * (sr * (1.0 - sr))).astype(BF16)
        dya = _dot_nt(dpa, w[0])
        dyr = _dot_nt(dpr, w[1])
        gw[0] = gw[0] + _dot_tn(ya, dpa)
        gw[1] = gw[1] + _dot_tn(yr, dpr)
        do_ref[...] = dya * silu_a
        dhh_ref[...] = dyr * silu_r
        dga_ref[...] = (dya * ov * (sg_a * (1.0 + ga * (1.0 - sg_a)))).astype(BF16)
        dgr_ref[...] = (dyr * hv * (sg_r * (1.0 + gr * (1.0 - sg_r)))).astype(BF16)
        zeros = jnp.zeros((5, D), F32)
        vec_ref[...] = vec_ref[...] + jnp.concatenate([g_final, d_gate, jnp.broadcast_to(loss, (1, D)), zeros], axis=0)

        @pl.when(i == n_tiles - 1)
        def _():
            cp = pltpu.make_async_copy(gw, gw_hbm, sem)
            cp.start()
            cp.wait()

    bf_tile = jax.ShapeDtypeStruct((t, D), BF16)
    f_tile = jax.ShapeDtypeStruct((t, D), F32)
    return pl.pallas_call(
        body,
        name="mid",
        grid=(n_tiles,),
        in_specs=[full, full, full, full,
                  half(SEG_GA // 512), half(SEG_GA // 512 + 1), half(SEG_GR // 512), half(SEG_GR // 512 + 1),
                  half(SEG_MA // 512), half(SEG_MA // 512 + 1), half(SEG_MR // 512), half(SEG_MR // 512 + 1),
                  any_spec, vec, vec],
        out_specs=[full, full, full, full, full, full, full, any_spec, pl.BlockSpec((8, D), lambda i: (0, 0))],
        out_shape=[f_tile, f_tile, f_tile, bf_tile, bf_tile, bf_tile, bf_tile,
                   jax.ShapeDtypeStruct((3, D, D), F32), jax.ShapeDtypeStruct((8, D), F32)],
        scratch_shapes=[pltpu.VMEM((3, D, D), BF16), pltpu.VMEM((3, D, D), F32), pltpu.SemaphoreType.DMA, pltpu.SemaphoreType.DMA((3 * N_DEV,))],
        compiler_params=_params(1),
    )(x, target, o, hh, proj, proj, proj, proj, proj, proj, proj, proj, w_all, gate, final_g)


_SEGMENTS = ((SEG_Q, 1024), (SEG_K, 256), (SEG_V, 256), (SEG_GA, 1024), (SEG_XR, 1024), (SEG_GR, 1024), (SEG_MA, 1024), (SEG_MR, 1024))


def _seg_specs(tm):
    return [pl.BlockSpec((tm, width), lambda i: (i, 0)) for _, width in _SEGMENTS]


def _bwd_dh(dsegs, w_all, x, dres, mod, norm_g, sum_w_bf):
    t = x.shape[0]
    tm = min(512, t)
    n_tiles = t // tm
    n_seg = len(_SEGMENTS)
    full = pl.BlockSpec((tm, D), lambda i: (i, 0))
    views = [lambda ref, peer: ref.at[_chip_of(peer)]]

    def body(*refs):
        segs = refs[:n_seg]
        w_hbm, x_ref, dres_ref, mod_ref, ng_ref, sum_hbm, gx_ref, vec_ref, land_hbm, w, sems, send_sems, recv_sems = refs[n_seg:]
        i = pl.program_id(0)

        @pl.when(i == 0)
        def _():
            for cp in _chip_copies([sum_hbm], [land_hbm], views, send_sems, recv_sems):
                cp.start()
            loads = [
                pltpu.make_async_copy(w_hbm.at[s], w.at[pl.ds(W_IN_SHARD * s, W_IN_SHARD), :], sems.at[s])
                for s in range(N_DEV)
            ]
            for cp in loads:
                cp.start()
            for cp in loads:
                cp.wait()
            vec_ref[...] = jnp.zeros_like(vec_ref)

        dh = sum(_dot(seg[...], w[start : start + width, :]) for seg, (start, width) in zip(segs, _SEGMENTS))
        xv = x_ref[...]
        r1 = lax.rsqrt(jnp.mean(xv * xv, axis=-1, keepdims=True) + NORM_EPS)
        xn = xv * r1
        ng = ng_ref[...]
        scale1 = 1.0 + mod_ref[:, D : 2 * D]
        d_shift = jnp.sum(dh, axis=0, keepdims=True)
        d_scale = jnp.sum(dh * (xn * ng), axis=0, keepdims=True)
        g_norm = jnp.sum(dh * xn * scale1, axis=0, keepdims=True)
        dxn = dh * (ng * scale1)
        dx = r1 * (dxn - xn * jnp.mean(dxn * xn, axis=-1, keepdims=True))
        gx_ref[...] = dres_ref[...] + dx
        vec_ref[...] = vec_ref[...] + jnp.concatenate([d_shift, d_scale, g_norm, jnp.zeros((5, D), F32)], axis=0)

        @pl.when(i == n_tiles - 1)
        def _():
            for cp in _chip_copies([sum_hbm], [land_hbm], views, send_sems, recv_sems):
                cp.wait_recv()
                cp.wait_send()

    any_spec = pl.BlockSpec(memory_space=pl.ANY)
    n_copies = len(CHIP_FLIPS)
    return pl.pallas_call(
        body,
        name="bwd_dh",
        grid=(n_tiles,),
        in_specs=_seg_specs(tm) + [any_spec, full, full, pl.BlockSpec((1, 3 * D), lambda i: (0, 0)), pl.BlockSpec((1, D), lambda i: (0, 0)),
                                   any_spec],
        out_specs=[full, pl.BlockSpec((8, D), lambda i: (0, 0)), any_spec],
        out_shape=[jax.ShapeDtypeStruct((t, D), F32), jax.ShapeDtypeStruct((8, D), F32),
                   jax.ShapeDtypeStruct((n_copies, W_IN_SHARD, D), BF16)],
        scratch_shapes=[pltpu.VMEM((IN_W, D), BF16), pltpu.SemaphoreType.DMA((N_DEV,)),
                        pltpu.SemaphoreType.DMA((n_copies,)), pltpu.SemaphoreType.DMA((n_copies,))],
        compiler_params=_params(1),
    )(*dsegs, w_all, x, dres, mod, norm_g, sum_w_bf)


def _bwd_gw(dsegs, h, sum_g_bf):
    t = h.shape[0]
    tm = min(512, t)
    n_tiles = t // tm
    n_seg = len(_SEGMENTS)
    n_chips = N_DEV // 2
    views = [lambda ref, peer: ref.at[:, :, _chip_of(peer)]]

    def body(*refs):
        segs = refs[:n_seg]
        (h_ref, sg_hbm, own_hbm, recv_hbm, lg_hbm, acc, stage, own_sems, pair_send, pair_recv, send_sems, recv_sems) = refs[n_seg:]
        i = pl.program_id(0)

        @pl.when(i == 0)
        def _():
            for cp in _chip_copies([sg_hbm], [lg_hbm], views, send_sems, recv_sems):
                cp.start()
            acc[...] = jnp.zeros_like(acc)

        hv = h_ref[...]
        for seg, (start, width) in zip(segs, _SEGMENTS):
            acc[start : start + width, :] = acc[start : start + width, :] + _dot_tn(seg[...], hv)

        @pl.when(i == n_tiles - 1)
        def _():
            me = _position()
            sibling = _flip(me, 1)

            def rows(q, c):
                return pl.ds(pl.multiple_of(W_IN_SHARD * (2 * q + c), 8), W_IN_SHARD)

            def to_sibling(q):
                return pltpu.make_async_remote_copy(
                    src_ref=stage.at[q % 2], dst_ref=recv_hbm.at[q], send_sem=pair_send.at[q], recv_sem=pair_recv.at[q],
                    device_id=sibling, device_id_type=MESH,
                )

            kept = [pltpu.make_async_copy(acc.at[rows(q, me[2]), :], own_hbm.at[q], own_sems.at[q]) for q in range(n_chips)]
            for cp in kept:
                cp.start()
            for q in range(n_chips):
                if q >= 2:
                    to_sibling(q - 2).wait_send()
                stage[q % 2] = acc[rows(q, sibling[2]), :].astype(BF16)
                to_sibling(q).start()
            for q in range(n_chips - 2, n_chips):
                to_sibling(q).wait_send()
            for q in range(n_chips):
                to_sibling(q).wait_recv()
            for cp in kept:
                cp.wait()
            for cp in _chip_copies([sg_hbm], [lg_hbm], views, send_sems, recv_sems):
                cp.wait_recv()
                cp.wait_send()

    any_spec = pl.BlockSpec(memory_space=pl.ANY)
    n_copies = len(CHIP_FLIPS)
    return pl.pallas_call(
        body,
        name="bwd_gw",
        grid=(n_tiles,),
        in_specs=_seg_specs(tm) + [pl.BlockSpec((tm, D), lambda i: (i, 0)), any_spec],
        out_specs=[any_spec, any_spec, any_spec],
        out_shape=[jax.ShapeDtypeStruct((n_chips, W_IN_SHARD, D), F32),
                   jax.ShapeDtypeStruct((n_chips, W_IN_SHARD, D), BF16),
                   jax.ShapeDtypeStruct((len(CHIP_FLIPS), 2, 4, GATE_SHARD, 256), BF16)],
        scratch_shapes=[pltpu.VMEM((IN_W, D), F32), pltpu.VMEM((2, W_IN_SHARD, D), BF16),
                        pltpu.SemaphoreType.DMA((n_chips,)), pltpu.SemaphoreType.DMA((n_chips,)), pltpu.SemaphoreType.DMA((n_chips,)),
                        pltpu.SemaphoreType.DMA((n_copies,)), pltpu.SemaphoreType.DMA((n_copies,))],
        compiler_params=_params(1),
    )(*dsegs, h, sum_g_bf)


def _adamw(w, g, m, v):
    m = ADAM_B1 * m + (1.0 - ADAM_B1) * g
    v = ADAM_B2 * v + (1.0 - ADAM_B2) * (g * g)
    m_hat = m / (1.0 - ADAM_B1**ADAM_STEP)
    v_hat = v / (1.0 - ADAM_B2**ADAM_STEP)
    delta = -ADAM_LR * (m_hat / (jnp.sqrt(v_hat) + ADAM_EPS) + ADAM_WD * w)
    return delta, m, v


def _w_in_update(own, land, w_t, m_t, v_t):
    n_slots, rows, cols = land.shape
    rb = 208
    spec = pl.BlockSpec((rb, cols), lambda i: (i, 0))

    def body(o_ref, l_ref, w_ref, m_ref, v_ref, g_ref, d_ref, m2_ref, v2_ref):
        total = o_ref[...]
        for s in range(n_slots):
            total = total + l_ref[s].astype(F32)
        g_ref[...] = total
        d_ref[...], m2_ref[...], v2_ref[...] = _adamw(w_ref[...], total, m_ref[...], v_ref[...])

    shape = jax.ShapeDtypeStruct((rows, cols), F32)
    return pl.pallas_call(
        body,
        name="w_in_update",
        grid=(rows // rb,),
        in_specs=[spec, pl.BlockSpec((n_slots, rb, cols), lambda i: (0, i, 0)), spec, spec, spec],
        out_specs=[spec] * 4,
        out_shape=[shape] * 4,
        compiler_params=_params(1),
    )(own, land, w_t, m_t, v_t)


def _shard_update(own_3, land_3, own_g, land_g, weights, m_in, v_in):
    n = len(weights)

    def body(*refs):
        o3_ref, l3_ref, og_ref, lg_ref = refs[:4]
        w_refs, m_refs, v_refs = (refs[4 + n * k : 4 + n * (k + 1)] for k in range(3))
        outs = refs[4 + 3 * n :]
        for k in range(n):
            if k < 3:
                g = o3_ref[k]
                for slot in range(3):
                    g = g + l3_ref[slot, k].astype(F32)
            else:
                g = og_ref[k - 3]
                for slot in range(3):
                    g = g + lg_ref[slot, k - 3].astype(F32)
            d, m2, v2 = _adamw(w_refs[k][0], g, m_refs[k][0], v_refs[k][0])
            for which, val in enumerate((g, d, m2, v2)):
                outs[which * n + k][0] = val

    shapes = [jax.ShapeDtypeStruct(w.shape, F32) for w in weights]
    res = pl.pallas_call(
        body, name="shard_update", out_shape=shapes * 4, compiler_params=pltpu.CompilerParams(vmem_limit_bytes=VMEM_LIMIT)
    )(own_3, land_3, own_g, land_g, *weights, *m_in, *v_in)
    return [res[n * k : n * (k + 1)] for k in range(4)]


VEC_LAYOUT = (("b_ada", 3 * D), ("norm_g", D), ("attn_sinks", 128), ("conv_b", D), ("rg_ba", D), ("rg_bx", D), ("rg_lambda", D),
              ("final_g", D), ("conv_w", CONV_WIDTH * D), ("loss", 128))
VEC_OFFSET = {name: sum(w for _, w in VEC_LAYOUT[:k]) for k, (name, _) in enumerate(VEC_LAYOUT)}
VEC_ROWS = 112
VEC_WIDTH = VEC_ROWS * 128
assert sum(w for _, w in VEC_LAYOUT) <= VEC_WIDTH
N_REPLICATED = 8


def _small_update(vec_all, c_all_t, me, rep_w, rep_m, rep_v, ada, conv):
    n_ada = ada[0].shape[1]
    n_conv = conv[0].shape[1]

    def body(*refs):
        me_ref, vec_ref, ct_ref = refs[:3]
        w_refs, m_refs, v_refs = (refs[3 + N_REPLICATED * k : 3 + N_REPLICATED * (k + 1)] for k in range(3))
        pos = 3 + 3 * N_REPLICATED
        ada_refs, conv_refs = refs[pos : pos + 3], refs[pos + 3 : pos + 6]
        outs = refs[pos + 6 :]
        loss_ref, rep_outs = outs[0], outs[1 : 1 + 4 * N_REPLICATED]
        ada_outs, conv_outs = outs[1 + 4 * N_REPLICATED : 5 + 4 * N_REPLICATED], outs[5 + 4 * N_REPLICATED : 9 + 4 * N_REPLICATED]
        conv_buf = outs[9 + 4 * N_REPLICATED]

        total = vec_ref[0:1, :]
        for slot in range(1, N_DEV):
            total = total + vec_ref[slot : slot + 1, :]
        loss_ref[...] = total[:, VEC_OFFSET["loss"] : VEC_OFFSET["loss"] + 1]
        for k, (name, _) in enumerate(VEC_LAYOUT[:N_REPLICATED]):
            width = w_refs[k].shape[1]
            g = total[:, VEC_OFFSET[name] : VEC_OFFSET[name] + width]
            d, m2, v2 = _adamw(w_refs[k][...], g, m_refs[k][...], v_refs[k][...])
            for which, val in enumerate((g, d, m2, v2)):
                rep_outs[which * N_REPLICATED + k][...] = val

        device = me_ref[0]
        for k in range(CONV_WIDTH):
            conv_buf[k : k + 1, :] = total[:, VEC_OFFSET["conv_w"] + D * k : VEC_OFFSET["conv_w"] + D * (k + 1)]
        g_conv = conv_buf[:, pl.ds(pl.multiple_of(device * n_conv, 128), n_conv)]
        for ref, val in zip(conv_outs, (g_conv,) + _adamw(conv_refs[0][...], g_conv, conv_refs[1][...], conv_refs[2][...])):
            ref[...] = val

        dmod = vec_ref[:, pl.ds(pl.multiple_of(device * n_ada, 128), n_ada)]
        g_ada = _dot(ct_ref[...], dmod)
        for ref, val in zip(ada_outs, (g_ada,) + _adamw(ada_refs[0][...], g_ada, ada_refs[1][...], ada_refs[2][...])):
            ref[...] = val

    vmem = pl.BlockSpec(memory_space=pltpu.VMEM)
    rep_shapes = [jax.ShapeDtypeStruct(w.shape, F32) for w in rep_w]
    out_shape = ([jax.ShapeDtypeStruct((1, 1), F32)] + rep_shapes * 4 + [jax.ShapeDtypeStruct(ada[0].shape, F32)] * 4
                 + [jax.ShapeDtypeStruct(conv[0].shape, F32)] * 4)
    n_in = 3 + 3 * N_REPLICATED + 6
    res = pl.pallas_call(
        body,
        name="small_update",
        in_specs=[pl.BlockSpec(memory_space=pltpu.SMEM)] + [vmem] * (n_in - 1),
        out_specs=[vmem] * len(out_shape),
        out_shape=out_shape,
        scratch_shapes=[pltpu.VMEM((CONV_WIDTH, D), F32)],
        compiler_params=pltpu.CompilerParams(vmem_limit_bytes=VMEM_LIMIT),
    )(me, vec_all, c_all_t, *rep_w, *rep_m, *rep_v, *ada, *conv)
    loss = res[0]
    reps = [res[1 + N_REPLICATED * k : 1 + N_REPLICATED * (k + 1)] for k in range(4)]
    return loss, reps, res[1 + 4 * N_REPLICATED : 5 + 4 * N_REPLICATED], res[5 + 4 * N_REPLICATED : 9 + 4 * N_REPLICATED]


def kernel(x, c, positions, w_ada, b_ada, norm_g, w_in, attn_sinks, conv_w, conv_b, rg_wa, rg_ba, rg_wx, rg_bx, rg_lambda, w_attn_proj, w_rnn_proj, w_out, final_g, loss_target, m_w_ada, m_b_ada, m_norm_g, m_w_in, m_attn_sinks, m_conv_w, m_conv_b, m_rg_wa, m_rg_ba, m_rg_wx, m_rg_bx, m_rg_lambda, m_w_attn_proj, m_w_rnn_proj, m_w_out, m_final_g, v_w_ada, v_b_ada, v_norm_g, v_w_in, v_attn_sinks, v_conv_w, v_conv_b, v_rg_wa, v_rg_ba, v_rg_wx, v_rg_bx, v_rg_lambda, v_w_attn_proj, v_w_rnn_proj, v_w_out, v_final_g):
    my_x, my_y, my_c = _position()
    me = _slot((my_x, my_y, my_c))
    x2, target = x[0], loss_target[0]
    pos = positions[0]

    gate_rows = lambda w: w[0].transpose(1, 0, 2).reshape(GATE_SHARD, D)
    w_pack = jnp.concatenate([w_attn_proj[0], w_rnn_proj[0], w_out[0], gate_rows(rg_wa), gate_rows(rg_wx)], axis=0).astype(BF16)
    conv_rows = jnp.pad(conv_w[0], ((0, 8 - CONV_WIDTH), (0, 0)))
    w_in_t = w_in[0].T
    c_rows, conv_all = _all_gather_direct("ag_small", [jnp.broadcast_to(c, (8, D)), conv_rows])
    c_all = c_rows[:, 0, :]
    conv_full = conv_all[:, :CONV_WIDTH, :].transpose(1, 0, 2).reshape(CONV_WIDTH, D)

    n_ada = w_ada.shape[2]
    b_cols = lax.dynamic_slice(b_ada, (0, me * n_ada), (1, n_ada))
    mod_cols = _ada_mod(c_all, w_ada[0], b_cols)
    (mod_all,) = _all_gather_direct("ag_mod", [mod_cols])
    mod = lax.dynamic_slice(mod_all, (0, me, 0), (N_DEV, 1, n_ada)).reshape(1, 3 * D)
    gate = mod[:, 2 * D :]

    tables = _rope_tables(pos)
    my_chip = _chip_of((my_x, my_y)).astype(jnp.int32).reshape(1)
    proj, h, w_t_all = _fwd_in(x2, mod, norm_g, w_in_t.astype(BF16), tables[0], my_chip)
    o, w_all = _attn_fwd(proj, attn_sinks, w_pack)
    keep = (pos != 0).astype(F32)[:, None]
    hh = _rnn_fwd(proj, keep, w_all, rg_ba, rg_bx, rg_lambda, conv_full, conv_b)

    dres, d_o, d_hh, d_ga, d_gr, d_ma, d_mr, gw3, mid_vec = _mid(x2, target, o, hh, proj, w_all, gate, final_g[None, :])
    index = jnp.stack([my_c, _chip_of((my_x, my_y))]).astype(jnp.int32)
    part_3 = gw3.reshape(3, 4, 2, ROW_SHARD, D)
    d_q, d_k, d_v, d_sink, recv_3 = _attn_bwd(proj, o, d_o, tables, attn_sinks, part_3)
    sum_3_bf, own_3 = _pair_add(
        "pair_add_3", part_3, recv_3, index, (3, 4), 1,
        pl.BlockSpec((None, None, None, ROW_SHARD, D), lambda a, q, idx: (a, q, idx[0], 0, 0)),
        pl.BlockSpec((None, None, ROW_SHARD, D), lambda a, q, idx: (a, q, 0, 0)),
        pl.BlockSpec((None, ROW_SHARD, D), lambda a, q, idx: (a, 0, 0)), (3, ROW_SHARD, D),
    )
    d_xr, g_gates, rnn_vec, land_3 = _rnn_bwd(proj, hh, d_hh, keep, w_all, rg_ba, rg_bx, rg_lambda, conv_full, conv_b, sum_3_bf)
    dsegs = (d_q, d_k, d_v, d_ga, d_xr, d_gr, d_ma, d_mr)
    part_g = g_gates.reshape(2, 4, 4, 2, GATE_SHARD, 256)
    (recv_g,) = _exchange("rs_pair_g", [part_g], [lambda ref, peer: ref.at[:, :, :, peer[2]]], [(2, 4, 4, GATE_SHARD, 256)], flips=(1,))
    sum_g_bf, own_g = _pair_add(
        "pair_add_g", part_g, recv_g[0], index, (4,), 0,
        pl.BlockSpec((2, 4, None, None, GATE_SHARD, 256), lambda q, idx: (0, 0, q, idx[0], 0, 0)),
        pl.BlockSpec((2, 4, None, GATE_SHARD, 256), lambda q, idx: (0, 0, q, 0, 0)),
        pl.BlockSpec((2, 4, GATE_SHARD, 256), lambda q, idx: (0, 0, 0, 0)), (2, 4, GATE_SHARD, 256),
    )
    part_w, recv_w, land_g = _bwd_gw(dsegs, h, sum_g_bf)
    rb = 208
    w_rows = pl.BlockSpec((None, rb, D), lambda r, q, idx: (q, r, 0))
    sum_w_bf, own_w = _pair_add(
        "pair_add_w", part_w, recv_w, index, (W_IN_SHARD // rb, 4), 1, w_rows, w_rows,
        pl.BlockSpec((rb, D), lambda r, q, idx: (r, 0)), (W_IN_SHARD, D),
    )
    grad_x, dh_vec, land_w = _bwd_dh(dsegs, w_t_all, x2, dres, mod, norm_g, sum_w_bf)

    pieces = dict(
        b_ada=jnp.concatenate([dh_vec[0:1], dh_vec[1:2], mid_vec[1:2]], axis=1),
        norm_g=dh_vec[2:3],
        attn_sinks=jnp.pad(d_sink[:, :, 0].reshape(1, 16), ((0, 0), (0, 112))),
        conv_b=rnn_vec[3:4],
        rg_ba=rnn_vec[0:1],
        rg_bx=rnn_vec[1:2],
        rg_lambda=rnn_vec[2:3],
        final_g=mid_vec[0:1],
        conv_w=rnn_vec[4:8].reshape(1, CONV_WIDTH * D),
        loss=mid_vec[2:3, 0:128],
    )
    vec = jnp.concatenate([pieces[name] for name, _ in VEC_LAYOUT], axis=1)
    vec = jnp.pad(vec, ((0, 0), (0, VEC_WIDTH - vec.shape[1]))).reshape(VEC_ROWS, 128)
    (vec_all,) = _all_gather_direct("ag_vec", [vec])
    rep_w = [b_ada, norm_g, attn_sinks, conv_b, rg_ba, rg_bx, rg_lambda, final_g[None, :]]
    rep_m = [m_b_ada, m_norm_g, m_attn_sinks, m_conv_b, m_rg_ba, m_rg_bx, m_rg_lambda, m_final_g[None, :]]
    rep_v = [v_b_ada, v_norm_g, v_attn_sinks, v_conv_b, v_rg_ba, v_rg_bx, v_rg_lambda, v_final_g[None, :]]
    loss, reps, ada_res, conv_res = _small_update(
        vec_all.reshape(N_DEV, VEC_WIDTH), c_all.T, me.astype(jnp.int32).reshape(1), rep_w, rep_m, rep_v,
        (w_ada[0], m_w_ada[0], v_w_ada[0]), (conv_w[0], m_conv_w[0], v_conv_w[0]),
    )

    w_in_res = _w_in_update(own_w, land_w, w_in_t, m_w_in[0].T, v_w_in[0].T)
    shard_res = _shard_update(
        own_3, land_3, own_g, land_g,
        (w_attn_proj, w_rnn_proj, w_out, rg_wa, rg_wx),
        (m_w_attn_proj, m_w_rnn_proj, m_w_out, m_rg_wa, m_rg_wx),
        (v_w_attn_proj, v_w_rnn_proj, v_w_out, v_rg_wa, v_rg_wx),
    )

    def by_weight(k):
        b, n, s, cb, ba, bx, lam, fg = reps[k]
        ap, rp, out, wa, wx = shard_res[k]
        return [ada_res[k][None], b, n, w_in_res[k].T[None], s, conv_res[k][None], cb, wa, ba, wx, bx, lam, ap, rp, out, fg[0]]

    return (loss[0, 0], grad_x[None], *by_weight(0), *by_weight(1), *by_weight(2), *by_weight(3))
```

```python
import functools
import math

import jax
import jax.numpy as jnp
import numpy as np
from jax import lax
from jax.experimental import pallas as pl
from jax.experimental.pallas import tpu as pltpu

F32 = jnp.float32
BF16 = jnp.bfloat16

D = 1024
HEAD_DIM = 64
ROT_DIM = 16
ROPE_THETA = 500000.0
WINDOW = 128
LRU_C = 8.0
CONV_WIDTH = 4
NORM_EPS = 1e-6
IN_W = 6656
SEG_Q, SEG_K, SEG_V, SEG_GA, SEG_XR, SEG_GR, SEG_MA, SEG_MR = 0, 1024, 1280, 1536, 2560, 3584, 4608, 5632
N_DEV = 8
W_IN_SHARD = IN_W // N_DEV
ROW_SHARD = D // N_DEV
GATE_SHARD = 256 // N_DEV

ADAM_LR = 0.001
ADAM_B1 = 0.9
ADAM_B2 = 0.999
ADAM_EPS = 1e-08
ADAM_WD = 0.01
ADAM_STEP = 10

VMEM_LIMIT = 60 * 1024 * 1024
MESH = pl.DeviceIdType.MESH
AXES = ("x", "y", "c")


def _dot(a, b):
    return jnp.dot(a, b, preferred_element_type=F32)


def _dot_nt(a, b):
    return lax.dot_general(a, b, (((1,), (1,)), ((), ())), preferred_element_type=F32)


def _dot_tn(a, b):
    return lax.dot_general(a, b, (((0,), (0,)), ((), ())), preferred_element_type=F32)


def _sigmoid(z):
    return 1.0 / (1.0 + jnp.exp(-z))


def _params(n_grid, vmem=VMEM_LIMIT):
    return pltpu.CompilerParams(dimension_semantics=("arbitrary",) * n_grid, vmem_limit_bytes=vmem)


def _position():
    return lax.axis_index("x"), lax.axis_index("y"), lax.axis_index("c")


def _flip(pos, k):
    x, y, c = pos
    return (1 - x if k & 4 else x, 1 - y if k & 2 else y, 1 - c if k & 1 else c)


def _slot(pos):
    return 4 * pos[0] + 2 * pos[1] + pos[2]


def _all_gather_direct(name, arrs):
    n = len(arrs)

    def body(*refs):
        ins, outs = refs[:n], refs[n : 2 * n]
        send_sems, recv_sems, local_sems = refs[2 * n :]
        me = _position()
        mine = [pltpu.make_async_copy(ins[a], outs[a].at[_slot(me)], local_sems.at[a]) for a in range(n)]
        for cp in mine:
            cp.start()

        def copy(a, k, block):
            return pltpu.make_async_remote_copy(
                src_ref=ins[a],
                dst_ref=outs[a].at[_slot(block)],
                send_sem=send_sems.at[a * 7 + k - 1],
                recv_sem=recv_sems.at[a * 7 + k - 1],
                device_id=_flip(me, k),
                device_id_type=MESH,
            )

        sent = [copy(a, k, me) for a in range(n) for k in range(1, N_DEV)]
        for cp in sent:
            cp.start()
        for a in range(n):
            for k in range(1, N_DEV):
                copy(a, k, _flip(me, k)).wait_recv()
        for cp in sent:
            cp.wait_send()
        for cp in mine:
            cp.wait()

    any_spec = pl.BlockSpec(memory_space=pl.ANY)
    return pl.pallas_call(
        body,
        name=name,
        out_shape=[jax.ShapeDtypeStruct((N_DEV,) + a.shape, a.dtype) for a in arrs],
        in_specs=[any_spec] * n,
        out_specs=[any_spec] * n,
        scratch_shapes=[
            pltpu.SemaphoreType.DMA((7 * n,)),
            pltpu.SemaphoreType.DMA((7 * n,)),
            pltpu.SemaphoreType.DMA((n,)),
        ],
    )(*arrs)


def _direct_gather_copies(src_hbm, all_hbm, send_sems, recv_sems, arriving):
    me = _position()
    return [
        pltpu.make_async_remote_copy(
            src_ref=src_hbm,
            dst_ref=all_hbm.at[_slot(_flip(me, k) if arriving else me)],
            send_sem=send_sems.at[k - 1],
            recv_sem=recv_sems.at[k - 1],
            device_id=_flip(me, k),
            device_id_type=MESH,
        )
        for k in range(1, N_DEV)
    ]


CHIP_FLIPS = (2, 4, 6)


def _chip_of(pos):
    return 2 * pos[0] + pos[1]


def _chip_copies(srcs, lands, views, send_sems, recv_sems):
    me = _position()
    copies = []
    for a, (src, land) in enumerate(zip(srcs, lands)):
        for j, k in enumerate(CHIP_FLIPS):
            peer = _flip(me, k)
            copies.append(
                pltpu.make_async_remote_copy(
                    src_ref=views[a](src, peer),
                    dst_ref=land.at[j],
                    send_sem=send_sems.at[len(CHIP_FLIPS) * a + j],
                    recv_sem=recv_sems.at[len(CHIP_FLIPS) * a + j],
                    device_id=peer,
                    device_id_type=MESH,
                )
            )
    return copies


def _exchange(name, arrs, views, out_shapes, flips):
    n = len(arrs)
    nf = len(flips)

    def body(*refs):
        ins, outs = refs[:n], refs[n : 2 * n]
        send_sems, recv_sems = refs[2 * n :]
        me = _position()

        def copy(a, j):
            peer = _flip(me, flips[j])
            return pltpu.make_async_remote_copy(
                src_ref=views[a](ins[a], peer),
                dst_ref=outs[a].at[j],
                send_sem=send_sems.at[a * nf + j],
                recv_sem=recv_sems.at[a * nf + j],
                device_id=peer,
                device_id_type=MESH,
            )

        sent = [copy(a, j) for a in range(n) for j in range(nf)]
        for cp in sent:
            cp.start()
        for cp in sent:
            cp.wait_recv()
        for cp in sent:
            cp.wait_send()

    any_spec = pl.BlockSpec(memory_space=pl.ANY)
    return pl.pallas_call(
        body,
        name=name,
        out_shape=[jax.ShapeDtypeStruct((nf,) + tuple(sh), a.dtype) for sh, a in zip(out_shapes, arrs)],
        in_specs=[any_spec] * n,
        out_specs=[any_spec] * n,
        scratch_shapes=[pltpu.SemaphoreType.DMA((nf * n,)), pltpu.SemaphoreType.DMA((nf * n,))],
    )(*arrs)


def _pair_add(name, own, recv, index, grid, chip_axis, own_spec, recv_spec, mine_spec, mine_shape):
    def body(idx_ref, own_ref, recv_ref, sb_ref, mine_ref):
        total = own_ref[...] + recv_ref[...].astype(F32)
        sb_ref[...] = total.astype(BF16)

        @pl.when(pl.program_id(chip_axis) == idx_ref[1])
        def _():
            mine_ref[...] = total

    return pl.pallas_call(
        body,
        name=name,
        grid_spec=pltpu.PrefetchScalarGridSpec(
            num_scalar_prefetch=1, grid=grid, in_specs=[own_spec, recv_spec], out_specs=[recv_spec, mine_spec]
        ),
        out_shape=[jax.ShapeDtypeStruct(recv.shape, BF16), jax.ShapeDtypeStruct(mine_shape, F32)],
        compiler_params=_params(len(grid)),
    )(index, own, recv)


def _ada_mod(c_all, w_ada, b_cols):
    def body(c_ref, w_ref, b_ref, o_ref):
        o_ref[...] = _dot(c_ref[...].astype(BF16), w_ref[...].astype(BF16)) + b_ref[...]

    return pl.pallas_call(
        body,
        name="ada_mod",
        out_shape=jax.ShapeDtypeStruct((N_DEV, w_ada.shape[1]), F32),
        compiler_params=pltpu.CompilerParams(vmem_limit_bytes=VMEM_LIMIT),
    )(c_all, w_ada, b_cols)


PACK_GATE = 3 * ROW_SHARD
PACK_ROWS = PACK_GATE + 2 * GATE_SHARD


def _fwd_in(x, mod, norm_g, w_shard, tables, pack, my_chip):
    t = x.shape[0]
    tm = min(1024, t)
    tn = IN_W // 4
    n_i, n_j = t // tm, IN_W // tn
    n_rope = (SEG_V - SEG_Q) // 128
    N_PARTS = 4

    def body(chip_ref, x_ref, mod_ref, ng_ref, cos_ref, s1_ref, s2_ref, shard_hbm, pack_hbm,
             proj_ref, h_ref, wt_hbm, all_hbm, wbuf, hbuf, w_send, w_recv, load_sems, local_sem, p_send, p_recv, p_local):
        j = pl.program_id(0)
        i = pl.program_id(1)
        me = _position()
        sibling = _flip(me, 1)
        chip = chip_ref[0]

        class shard_copy:
            def __init__(self, k, block, to, src=None):
                part = W_IN_SHARD // N_PARTS
                self.parts = [
                    pltpu.make_async_remote_copy(
                        src_ref=(wt_hbm.at[_slot(block)] if src is None else src).at[pl.ds(part * n, part), :],
                        dst_ref=wt_hbm.at[_slot(block), pl.ds(part * n, part), :],
                        send_sem=w_send.at[k * N_PARTS + n],
                        recv_sem=w_recv.at[k * N_PARTS + n],
                        device_id=to,
                        device_id_type=MESH,
                    )
                    for n in range(N_PARTS)
                ]

            def start(self):
                for cp in self.parts:
                    cp.start()

            def wait_recv(self):
                for cp in self.parts:
                    cp.wait_recv()

            def wait_send(self):
                for cp in self.parts:
                    cp.wait_send()

        own_shard = pltpu.make_async_copy(shard_hbm, wt_hbm.at[_slot(me)], local_sem)
        own_pack = pltpu.make_async_copy(pack_hbm, all_hbm.at[_slot(me)], p_local)

        def load_tile(q):
            loads = [pltpu.make_async_copy(wt_hbm.at[2 * q + c], wbuf.at[pl.ds(W_IN_SHARD * c, W_IN_SHARD), :], load_sems.at[c]) for c in range(2)]
            for cp in loads:
                cp.start()
            for cp in loads:
                cp.wait()

        @pl.when((j == 0) & (i == 0))
        def _():
            own_shard.start()
            shard_copy(0, me, sibling, src=shard_hbm).start()
            for rel in (1, 2):
                shard_copy(rel, me, _flip(me, 2 * rel), src=shard_hbm).start()
            own_shard.wait()
            shard_copy(0, sibling, me).wait_recv()
            load_tile(chip)

        for rel in range(1, 4):

            @pl.when((j == rel) & (i == 0))
            def _(rel=rel):
                if rel == 1:
                    shard_copy(3, me, _flip(me, 6), src=shard_hbm).start()
                    own_pack.start()
                    for cp in _direct_gather_copies(pack_hbm, all_hbm, p_send, p_recv, arriving=False):
                        cp.start()
                there = _flip(me, 2 * rel)
                shard_copy(rel, there, me).wait_recv()
                shard_copy(3 + rel, there, sibling).start()
                shard_copy(3 + rel, _flip(sibling, 2 * rel), me).wait_recv()
                load_tile(chip ^ rel)

        rows = pl.ds(pl.multiple_of(i * tm, tm), tm)

        @pl.when(j == 0)
        def _():
            xv = x_ref[...]
            r = lax.rsqrt(jnp.mean(xv * xv, axis=-1, keepdims=True) + NORM_EPS)
            shift = mod_ref[:, 0:D]
            scale = mod_ref[:, D : 2 * D]
            hv = (((xv * r) * ng_ref[...]) * (1.0 + scale) + shift).astype(BF16)
            hbuf[rows, :] = hv
            h_ref[...] = hv

        proj_ref[...] = _dot_nt(hbuf[rows, :], wbuf[...])

        @pl.when((chip ^ j) == 0)
        def _():
            cos, s1, s2 = cos_ref[...], s1_ref[...], s2_ref[...]
            for g in range(n_rope):
                cols = slice(128 * g, 128 * (g + 1))
                proj_ref[:, cols] = _rope(proj_ref[:, cols], cos, s1, s2)

        @pl.when((j == n_j - 1) & (i == n_i - 1))
        def _():
            shard_copy(0, me, sibling, src=shard_hbm).wait_send()
            for rel in range(1, 4):
                shard_copy(rel, me, _flip(me, 2 * rel), src=shard_hbm).wait_send()
                shard_copy(3 + rel, _flip(me, 2 * rel), sibling).wait_send()
            for cp in _direct_gather_copies(pack_hbm, all_hbm, p_send, p_recv, arriving=True):
                cp.wait_recv()
            for cp in _direct_gather_copies(pack_hbm, all_hbm, p_send, p_recv, arriving=False):
                cp.wait_send()
            own_pack.wait()

    tab = pl.BlockSpec((tm, 128), lambda j, i, c: (i, 0))
    any_spec = pl.BlockSpec(memory_space=pl.ANY)
    return pl.pallas_call(
        body,
        name="fwd_in",
        grid_spec=pltpu.PrefetchScalarGridSpec(
            num_scalar_prefetch=1,
            grid=(n_j, n_i),
            in_specs=[
                pl.BlockSpec((tm, D), lambda j, i, c: (jnp.where(j == 0, i, 0), 0)),
                pl.BlockSpec((1, 3 * D), lambda j, i, c: (0, 0)),
                pl.BlockSpec((1, D), lambda j, i, c: (0, 0)),
                tab, tab, tab, any_spec, any_spec,
            ],
            out_specs=[pl.BlockSpec((tm, tn), lambda j, i, c: (i, c[0] ^ j)),
                       pl.BlockSpec((tm, D), lambda j, i, c: (jnp.where(j == 0, i, n_i - 1), 0)),
                       any_spec, any_spec],
            scratch_shapes=[
                pltpu.VMEM((tn, D), BF16),
                pltpu.VMEM((t, D), BF16),
                pltpu.SemaphoreType.DMA(((N_DEV - 1) * N_PARTS,)),
                pltpu.SemaphoreType.DMA(((N_DEV - 1) * N_PARTS,)),
                pltpu.SemaphoreType.DMA((2,)),
                pltpu.SemaphoreType.DMA,
                pltpu.SemaphoreType.DMA((N_DEV - 1,)),
                pltpu.SemaphoreType.DMA((N_DEV - 1,)),
                pltpu.SemaphoreType.DMA,
            ],
        ),
        out_shape=[jax.ShapeDtypeStruct((t, IN_W), F32), jax.ShapeDtypeStruct((t, D), BF16),
                   jax.ShapeDtypeStruct((N_DEV,) + w_shard.shape, w_shard.dtype),
                   jax.ShapeDtypeStruct((N_DEV,) + pack.shape, pack.dtype)],
        compiler_params=_params(2),
    )(my_chip, x, mod, norm_g, *tables, w_shard, pack)


def _rope_tables(positions):
    half = ROT_DIM // 2
    inv_freq = ROPE_THETA ** (-jnp.arange(0, ROT_DIM, 2, dtype=F32) / ROT_DIM)
    pos = positions.astype(F32)
    lane = np.arange(128) % HEAD_DIM
    freq_lane = jnp.where(lane < ROT_DIM, jnp.tile(inv_freq, 128 // half), 0.0)
    ang = pos[:, None] * freq_lane[None, :]
    sin = jnp.sin(ang)
    lane_tables = (jnp.cos(ang), jnp.where(lane < half, -sin, 0.0), jnp.where((lane >= half) & (lane < ROT_DIM), sin, 0.0))
    ang_t = inv_freq[:, None] * pos[None, :]
    return lane_tables, (jnp.cos(ang_t), jnp.sin(ang_t))


def _rope(tv, cos, s1, s2):
    cols = []
    for g in range(tv.shape[1] // 128):
        blk = tv[:, 128 * g : 128 * (g + 1)]
        cols.append(blk * cos + pltpu.roll(blk, 128 - 8, 1) * s1 + pltpu.roll(blk, 8, 1) * s2)
    return cols[0] if len(cols) == 1 else jnp.concatenate(cols, axis=1)


def _rope_transposed(gv, cos, s1, s2):
    cols = []
    for g in range(gv.shape[1] // 128):
        blk = gv[:, 128 * g : 128 * (g + 1)]
        cols.append(blk * cos + pltpu.roll(blk * s1, 8, 1) + pltpu.roll(blk * s2, 128 - 8, 1))
    return cols[0] if len(cols) == 1 else jnp.concatenate(cols, axis=1)


def _lane_lo():
    return lax.broadcasted_iota(jnp.int32, (1, 128), 1) < HEAD_DIM


def _row_lo():
    return lax.broadcasted_iota(jnp.int32, (128, 1), 0) < HEAD_DIM


def _split_heads(tile, j):
    lo = _lane_lo()
    parts = []
    for g in (2 * j, 2 * j + 1):
        blk = tile[:, 128 * g : 128 * (g + 1)]
        parts += [jnp.where(lo, blk, 0.0), jnp.where(lo, 0.0, blk)]
    return jnp.concatenate(parts, axis=0)


def _dup_lanes(kv, j):
    rolled = pltpu.roll(kv, HEAD_DIM, 1)
    lo = _lane_lo()
    return jnp.where(lo, kv, rolled) if j == 0 else jnp.where(lo, rolled, kv)


def _fold_lanes(acc):
    return acc + pltpu.roll(acc, HEAD_DIM, 1)


def _fill_bias_t(bias_ref):
    kj = lax.broadcasted_iota(jnp.int32, (256, 128), 0)
    qi = lax.broadcasted_iota(jnp.int32, (256, 128), 1)
    diff = qi + WINDOW - kj
    band = (diff >= 0) & (diff < WINDOW)
    bias_ref[0] = jnp.where(band, 0.0, -1e30)
    bias_ref[1] = jnp.where(band & (kj >= 128), 0.0, -1e30)


def _sink_row(sink_ref, pair, j):
    return jnp.concatenate([jnp.full((1, 128), sink_ref[0, 8 * pair + 4 * j + h], F32) for h in range(4)], axis=1)


def _softmax_t(q4, kw, bias4, sink):
    s = _dot_nt(kw, q4) + bias4
    m = jnp.maximum(jnp.max(s, axis=0, keepdims=True), sink)
    p = jnp.exp(s - m)
    e_sink = jnp.exp(sink - m)
    denom = jnp.sum(p, axis=0, keepdims=True) + e_sink
    return p, e_sink, denom


def _attn_specs(t, tq):
    return dict(
        q=pl.BlockSpec((tq, 512), lambda p, i: (i, p)),
        k=pl.BlockSpec((t, 128), lambda p, i: (0, SEG_K // 128 + p)),
        v=pl.BlockSpec((t, 128), lambda p, i: (0, SEG_V // 128 + p)),
        sink=pl.BlockSpec(memory_space=pltpu.SMEM),
        act=pl.BlockSpec((tq, 512), lambda p, i: (i, p)),
    )


def _attn_fwd(proj, sinks):
    t = proj.shape[0]
    tq = min(512, t)
    nb = tq // 128
    n_tiles = t // tq
    sp = _attn_specs(t, tq)

    def body(q_ref, k_ref, v_ref, sink_ref, o_ref, kd, vdt, bias_ref):
        pair = pl.program_id(0)
        i = pl.program_id(1)

        @pl.when(i == 0)
        def _():
            kv, vv = k_ref[...], v_ref[...]
            for j in range(2):
                kd[j, 0:128, :] = jnp.zeros((128, 128), BF16)
                kd[j, 128:, :] = _dup_lanes(kv, j).astype(BF16)
                vdt[j, :, 0:128] = jnp.zeros((128, 128), BF16)
                vdt[j, :, 128:] = _dup_lanes(vv, j).T.astype(BF16)
            _fill_bias_t(bias_ref)

        row_lo = _row_lo()

        def block(b, carry):
            r0 = pl.multiple_of(b * 128, 128)
            gb = i * nb + b
            keys = pl.ds(pl.multiple_of(gb * 128, 128), 256)
            rows = pl.ds(r0, 128)
            qt = q_ref[rows, :] * (1.0 / math.sqrt(HEAD_DIM))
            bias = bias_ref[jnp.where(gb == 0, 1, 0)]
            bias4 = jnp.concatenate([bias] * 4, axis=1)
            cols = []
            for j in range(2):
                q4 = _split_heads(qt, j).astype(BF16)
                p, _, denom = _softmax_t(q4, kd[j, keys, :], bias4, _sink_row(sink_ref, pair, j))
                ot = _dot(vdt[j, :, keys], p.astype(BF16)) * (1.0 / denom)
                for gi in range(2):
                    z = jnp.where(row_lo, ot[:, 256 * gi : 256 * gi + 128], ot[:, 256 * gi + 128 : 256 * gi + 256])
                    cols.append(z.T)
            o_ref[rows, :] = jnp.concatenate(cols, axis=1)
            return carry

        lax.fori_loop(0, nb, block, 0, unroll=True)

    return pl.pallas_call(
        body,
        name="attn_fwd",
        grid=(2, n_tiles),
        in_specs=[sp["q"], sp["k"], sp["v"], sp["sink"]],
        out_specs=sp["act"],
        out_shape=jax.ShapeDtypeStruct((t, D), F32),
        scratch_shapes=[pltpu.VMEM((2, t + 128, 128), BF16), pltpu.VMEM((2, 128, t + 128), BF16), pltpu.VMEM((2, 256, 128), F32)],
        compiler_params=_params(2),
    )(proj, proj, proj, sinks)


def _attn_bwd(proj, o, d_o, tables, sinks, part_3):
    t = proj.shape[0]
    tq = min(512, t)
    nb = tq // 128
    n_tiles = t // tq
    (cos, s1, s2), (cos_t, sin_t) = tables
    sp = _attn_specs(t, tq)
    scale = 1.0 / math.sqrt(HEAD_DIM)
    half = ROT_DIM // 2

    def unrope_rows(z, c, s):
        out = []
        for base in (0, HEAD_DIM):
            g0, g1 = z[base : base + half], z[base + half : base + ROT_DIM]
            out += [g0 * c + g1 * s, g1 * c - g0 * s, z[base + ROT_DIM : base + HEAD_DIM]]
        return jnp.concatenate(out, axis=0)

    def swap(part_hbm, recv_hbm, send_sem, recv_sem):
        sibling = _flip(_position(), 1)
        return pltpu.make_async_remote_copy(
            src_ref=part_hbm.at[:, :, sibling[2]], dst_ref=recv_hbm, send_sem=send_sem, recv_sem=recv_sem,
            device_id=sibling, device_id_type=MESH,
        )

    def body(q_ref, k_ref, v_ref, o_ref, do_ref, ct_ref, st_ref, ck, s1k, s2k, sink_ref, part_hbm,
             dq_ref, dk_ref, dv_ref, dsink_ref, recv_hbm, kd, kdt, vd, bias_ref, dkacc, dvacc, send_sem, recv_sem):
        pair = pl.program_id(0)
        i = pl.program_id(1)

        @pl.when((pair == 0) & (i == 0))
        def _():
            swap(part_hbm, recv_hbm, send_sem, recv_sem).start()

        @pl.when((pair == 1) & (i == n_tiles - 1))
        def _():
            swap(part_hbm, recv_hbm, send_sem, recv_sem).wait()

        @pl.when(i == 0)
        def _():
            kv, vv = k_ref[...], v_ref[...]
            for j in range(2):
                kdup = _dup_lanes(kv, j)
                kd[j, 0:128, :] = jnp.zeros((128, 128), BF16)
                kd[j, 128:, :] = kdup.astype(BF16)
                kdt[j, :, 0:128] = jnp.zeros((128, 128), BF16)
                kdt[j, :, 128:] = kdup.T.astype(BF16)
                vd[j, 0:128, :] = jnp.zeros((128, 128), BF16)
                vd[j, 128:, :] = _dup_lanes(vv, j).astype(BF16)
            _fill_bias_t(bias_ref)
            dkacc[...] = jnp.zeros_like(dkacc)
            dvacc[...] = jnp.zeros_like(dvacc)
            dsink_ref[...] = jnp.zeros_like(dsink_ref)

        row_lo = _row_lo()
        ones = jnp.ones((8, 128), BF16)

        def block(b, carry):
            r0 = pl.multiple_of(b * 128, 128)
            gb = i * nb + b
            keys = pl.ds(pl.multiple_of(gb * 128, 128), 256)
            rows = pl.ds(r0, 128)
            qt = q_ref[rows, :] * scale
            dob = do_ref[rows, :]
            prod = dob * o_ref[rows, :]
            ct, st = ct_ref[:, rows], st_ref[:, rows]
            bias = bias_ref[jnp.where(gb == 0, 1, 0)]
            bias4 = jnp.concatenate([bias] * 4, axis=1)
            cols = []
            for j in range(2):
                q4 = _split_heads(qt, j).astype(BF16)
                kw = kd[j, keys, :]
                p, e_sink, denom = _softmax_t(q4, kw, bias4, _sink_row(sink_ref, pair, j))
                inv = 1.0 / denom
                prob = p * inv
                do4 = _split_heads(dob, j).astype(BF16)
                delta = _dot_nt(ones, _split_heads(prod, j).astype(BF16))[0:1, :]
                dp = _dot_nt(vd[j, keys, :], do4)
                ds = (prob * (dp - delta)).astype(BF16)
                dkacc[j, keys, :] = dkacc[j, keys, :] + _dot(ds, q4)
                dvacc[j, keys, :] = dvacc[j, keys, :] + _dot(prob.astype(BF16), do4)
                dqt = _dot(kdt[j, :, keys], ds) * scale
                dsink = -(e_sink * inv) * delta
                for h in range(4):
                    row = 4 * j + h
                    total = jnp.sum(dsink[:, 128 * h : 128 * (h + 1)], axis=1, keepdims=True)
                    dsink_ref[0, row : row + 1, :] = dsink_ref[0, row : row + 1, :] + jnp.broadcast_to(total, (1, 128))
                for gi in range(2):
                    z = jnp.where(row_lo, dqt[:, 256 * gi : 256 * gi + 128], dqt[:, 256 * gi + 128 : 256 * gi + 256])
                    cols.append(unrope_rows(z, ct, st).T)
            dq_ref[rows, :] = jnp.concatenate(cols, axis=1).astype(BF16)
            return carry

        lax.fori_loop(0, nb, block, 0, unroll=True)

        @pl.when(i == n_tiles - 1)
        def _():
            lo = _lane_lo()
            dk = jnp.where(lo, _fold_lanes(dkacc[0, 128:, :]), _fold_lanes(dkacc[1, 128:, :]))
            dk_ref[...] = _rope_transposed(dk, ck[...], s1k[...], s2k[...]).astype(BF16)
            dv_ref[...] = jnp.where(lo, _fold_lanes(dvacc[0, 128:, :]), _fold_lanes(dvacc[1, 128:, :])).astype(BF16)

    kv_out = pl.BlockSpec((t, 128), lambda p, i: (0, p))
    tab_t = pl.BlockSpec((half, tq), lambda p, i: (0, i))
    tab_k = pl.BlockSpec((t, 128), lambda p, i: (0, 0))
    return pl.pallas_call(
        body,
        name="attn_bwd",
        grid=(2, n_tiles),
        in_specs=[sp["q"], sp["k"], sp["v"], sp["act"], sp["act"], tab_t, tab_t, tab_k, tab_k, tab_k, sp["sink"],
                  pl.BlockSpec(memory_space=pl.ANY)],
        out_specs=[sp["act"], kv_out, kv_out, pl.BlockSpec((1, 8, 128), lambda p, i: (p, 0, 0)), pl.BlockSpec(memory_space=pl.ANY)],
        out_shape=[
            jax.ShapeDtypeStruct((t, D), BF16),
            jax.ShapeDtypeStruct((t, 256), BF16),
            jax.ShapeDtypeStruct((t, 256), BF16),
            jax.ShapeDtypeStruct((2, 8, 128), F32),
            jax.ShapeDtypeStruct((3, 4, ROW_SHARD, D), F32),
        ],
        scratch_shapes=[pltpu.VMEM((2, t + 128, 128), BF16), pltpu.VMEM((2, 128, t + 128), BF16), pltpu.VMEM((2, t + 128, 128), BF16),
                        pltpu.VMEM((2, 256, 128), F32), pltpu.VMEM((2, t + 128, 128), F32), pltpu.VMEM((2, t + 128, 128), F32),
                        pltpu.SemaphoreType.DMA, pltpu.SemaphoreType.DMA],
        compiler_params=_params(2),
    )(proj, proj, proj, o, d_o, cos_t, sin_t, cos, s1, s2, sinks, part_3)


HALF = 512


def _softplus(y):
    u = jnp.exp(-jnp.abs(y))
    w = 1.0 + u
    log1p = jnp.where(w == 1.0, u, jnp.log(w) * (u / jnp.where(w == 1.0, 1.0, w - 1.0)))
    return jnp.maximum(y, 0.0) + log1p


def _gate_block(w_ref, blk):
    return w_ref[:, :, 256 * blk : 256 * (blk + 1)].reshape(256, 256)


def _gate_matmul(act_bf, w_ref):
    return jnp.concatenate([_dot(act_bf[:, 256 * blk : 256 * (blk + 1)], _gate_block(w_ref, blk)) for blk in range(2)], axis=1)


def _gate_matmul_t(grad_bf, w_ref):
    return jnp.concatenate([_dot_nt(grad_bf[:, 256 * blk : 256 * (blk + 1)], _gate_block(w_ref, blk)) for blk in range(2)], axis=1)


def _conv_and_gates(xe_ref, tm, cw_ref, cb_ref, wa_ref, wx_ref, ba_ref, bx_ref, lam_ref, keep):
    xc = cb_ref[...] + sum(cw_ref[k : k + 1, :] * xe_ref[pl.ds(5 + k, tm), :] for k in range(CONV_WIDTH))
    xc_bf = xc.astype(BF16)
    r = _sigmoid(_gate_matmul(xc_bf, wa_ref) + ba_ref[...])
    ig = _sigmoid(_gate_matmul(xc_bf, wx_ref) + bx_ref[...])
    sp = _softplus(-lam_ref[...])
    log_a = -LRU_C * r * sp
    tanh = jnp.tanh(-log_a)
    mult = jnp.sqrt(2.0 * tanh / (1.0 + tanh))
    a = jnp.where(keep, jnp.exp(log_a), 0.0)
    mult = jnp.where(keep, mult, 1.0)
    return xc, xc_bf, r, ig, sp, a, mult


def _scan_rows(a_ref, b_ref, h_ref, carry_ref, tm, reverse):
    width = a_ref.shape[1]
    row = lax.broadcasted_iota(jnp.int32, (8, width), 0)

    def group(g, carry):
        gg = tm // 8 - 1 - g if reverse else g
        r0 = pl.multiple_of(gg * 8, 8)
        a = a_ref[pl.ds(r0, 8), :]
        b = b_ref[pl.ds(r0, 8), :]
        for s in (1, 2, 4):
            shift = 8 - s if reverse else s
            a_sh = pltpu.roll(a, shift, 0)
            b_sh = pltpu.roll(b, shift, 0)
            ok = (row < 8 - s) if reverse else (row >= s)
            b = jnp.where(ok, a * b_sh + b, b)
            a = jnp.where(ok, a * a_sh, a)
        h = a * carry + b
        h_ref[pl.ds(r0, 8), :] = h
        last = h[0:1, :] if reverse else h[7:8, :]
        return jnp.broadcast_to(last, (8, width))

    carry_ref[...] = lax.fori_loop(0, tm // 8, group, carry_ref[...])


def _rnn_specs(t, tm, order):
    n = t // tm
    return dict(
        tile=pl.BlockSpec((tm, HALF), lambda ch, i: (order(i, n), ch)),
        before=lambda col0: pl.BlockSpec((8, HALF), lambda ch, i: (jnp.maximum(order(i, n) * (tm // 8) - 1, 0), col0 + ch)),
        proj_tile=lambda col0: pl.BlockSpec((tm, HALF), lambda ch, i: (order(i, n), col0 + ch)),
        keep=pl.BlockSpec((tm, 1), lambda ch, i: (order(i, n), 0)),
        gates=lambda which: pl.BlockSpec((N_DEV, GATE_SHARD, HALF), lambda ch, i: (0, PACK_GATE // GATE_SHARD + which, ch)),
        gate_grads=pl.BlockSpec((2, 2, 256, 256), lambda ch, i: (0, ch, 0, 0)),
        vec=pl.BlockSpec((1, HALF), lambda ch, i: (0, ch)),
        conv=pl.BlockSpec((CONV_WIDTH, HALF), lambda ch, i: (0, ch)),
    )


def _rnn_fwd(proj, keep, w_all, rg_ba, rg_bx, rg_lambda, conv_w, conv_b):
    t = proj.shape[0]
    tm = min(512, t)
    sp = _rnn_specs(t, tm, lambda i, n: i)

    def body(xr_ref, xb_ref, keep_ref, wa_ref, wx_ref, ba_ref, bx_ref, lam_ref, cw_ref, cb_ref, hh_ref, xe, a_buf, b_buf, carry):
        i = pl.program_id(1)

        @pl.when(i == 0)
        def _():
            carry[...] = jnp.zeros_like(carry)

        xe[0:8, :] = jnp.where(i > 0, xb_ref[...], 0.0)
        xe[8:, :] = xr_ref[...]
        keep_rows = keep_ref[...] != 0.0
        xc, _, _, ig, _, a, mult = _conv_and_gates(xe, tm, cw_ref, cb_ref, wa_ref, wx_ref, ba_ref, bx_ref, lam_ref, keep_rows)
        a_buf[...] = a
        b_buf[...] = mult * (ig * xc)
        _scan_rows(a_buf, b_buf, hh_ref, carry, tm, reverse=False)

    return pl.pallas_call(
        body,
        name="rnn_fwd",
        grid=(2, t // tm),
        in_specs=[sp["proj_tile"](SEG_XR // HALF), sp["before"](SEG_XR // HALF), sp["keep"], sp["gates"](0), sp["gates"](1),
                  sp["vec"], sp["vec"], sp["vec"], sp["conv"], sp["vec"]],
        out_specs=sp["tile"],
        out_shape=jax.ShapeDtypeStruct((t, D), F32),
        scratch_shapes=[pltpu.VMEM((tm + 8, HALF), F32), pltpu.VMEM((tm, HALF), F32), pltpu.VMEM((tm, HALF), F32), pltpu.VMEM((8, HALF), F32)],
        compiler_params=_params(2),
    )(proj, proj, keep, w_all, w_all, rg_ba, rg_bx, rg_lambda, conv_w, conv_b)


def _rnn_bwd(proj, hh, d_hh, keep, w_all, rg_ba, rg_bx, rg_lambda, conv_w, conv_b, sum_3_bf):
    t = proj.shape[0]
    tm = min(512, t)
    n_tiles = t // tm
    sp = _rnn_specs(t, tm, lambda i, n: n - 1 - i)

    def body(xr_ref, xb_ref, hh_ref, hb_ref, dhh_ref, keep_ref, wa_ref, wx_ref, ba_ref, bx_ref, lam_ref, cw_ref, cb_ref,
             s3_hbm, dxr_ref, gw_ref, gvec_ref, l3_hbm, xe, he, ae, c_buf, g_buf, dxe, carry, a_next, send_sems, recv_sems):
        ch = pl.program_id(0)
        i = pl.program_id(1)
        first_tile = i == n_tiles - 1
        views = [lambda ref, peer: ref.at[:, _chip_of(peer)]]

        @pl.when((ch == 0) & (i == 0))
        def _():
            for cp in _chip_copies([s3_hbm], [l3_hbm], views, send_sems, recv_sems):
                cp.start()

        @pl.when((ch == 1) & (i == n_tiles - 1))
        def _():
            for cp in _chip_copies([s3_hbm], [l3_hbm], views, send_sems, recv_sems):
                cp.wait_recv()
                cp.wait_send()

        @pl.when(i == 0)
        def _():
            carry[...] = jnp.zeros_like(carry)
            a_next[...] = jnp.zeros_like(a_next)
            dxe[tm:, :] = jnp.zeros((8, HALF), F32)
            gw_ref[...] = jnp.zeros_like(gw_ref)
            gvec_ref[...] = jnp.zeros_like(gvec_ref)

        xe[0:8, :] = jnp.where(first_tile, 0.0, xb_ref[...])
        xe[8:, :] = xr_ref[...]
        he[0:8, :] = jnp.where(first_tile, 0.0, hb_ref[...])
        he[8:, :] = hh_ref[...]
        keep_rows = keep_ref[...] != 0.0
        xc, xc_bf, r, ig, sp_, a, mult = _conv_and_gates(xe, tm, cw_ref, cb_ref, wa_ref, wx_ref, ba_ref, bx_ref, lam_ref, keep_rows)

        ae[0:tm, :] = a
        ae[tm:, :] = a_next[...]
        a_next[...] = jnp.broadcast_to(a[0:1, :], (8, HALF))
        c_buf[...] = ae[pl.ds(1, tm), :]
        g_buf[...] = dhh_ref[...]
        _scan_rows(c_buf, g_buf, g_buf, carry, tm, reverse=True)
        g = g_buf[...]

        h_prev = he[pl.ds(7, tm), :]
        bb = ig * xc
        d_ig = g * mult * xc
        d_xc = g * mult * ig
        d_log_a = jnp.where(keep_rows, g * h_prev * a - (g * bb) * (a * a) / mult, 0.0)
        d_r = d_log_a * (-LRU_C * sp_)
        d_sp = jnp.sum(d_log_a * (-LRU_C * r), axis=0, keepdims=True)
        d_zr = d_r * r * (1.0 - r)
        d_zi = d_ig * ig * (1.0 - ig)
        d_zr_bf = d_zr.astype(BF16)
        d_zi_bf = d_zi.astype(BF16)
        d_xc = d_xc + _gate_matmul_t(d_zr_bf, wa_ref) + _gate_matmul_t(d_zi_bf, wx_ref)
        for blk in range(2):
            cols = slice(256 * blk, 256 * (blk + 1))
            gw_ref[0, blk] = gw_ref[0, blk] + _dot_tn(xc_bf[:, cols], d_zr_bf[:, cols])
            gw_ref[1, blk] = gw_ref[1, blk] + _dot_tn(xc_bf[:, cols], d_zi_bf[:, cols])

        dxe[0:tm, :] = d_xc
        dxr = sum(cw_ref[k : k + 1, :] * dxe[pl.ds(3 - k, tm), :] for k in range(CONV_WIDTH))
        dxr_ref[...] = dxr.astype(BF16)
        dxe[tm:, :] = d_xc[0:8, :]

        lam = lam_ref[...]
        sums = [
            jnp.sum(d_zr, axis=0, keepdims=True),
            jnp.sum(d_zi, axis=0, keepdims=True),
            d_sp * (-_sigmoid(-lam)),
            jnp.sum(d_xc, axis=0, keepdims=True),
        ] + [jnp.sum(d_xc * xe[pl.ds(5 + k, tm), :], axis=0, keepdims=True) for k in range(CONV_WIDTH)]
        gvec_ref[...] = gvec_ref[...] + jnp.concatenate(sums, axis=0)

    return pl.pallas_call(
        body,
        name="rnn_bwd",
        grid=(2, n_tiles),
        in_specs=[
            sp["proj_tile"](SEG_XR // HALF), sp["before"](SEG_XR // HALF), sp["tile"], sp["before"](0), sp["tile"], sp["keep"],
            sp["gates"](0), sp["gates"](1), sp["vec"], sp["vec"], sp["vec"], sp["conv"], sp["vec"], pl.BlockSpec(memory_space=pl.ANY),
        ],
        out_specs=[sp["tile"], sp["gate_grads"], pl.BlockSpec((8, HALF), lambda ch, i: (0, ch)), pl.BlockSpec(memory_space=pl.ANY)],
        out_shape=[
            jax.ShapeDtypeStruct((t, D), BF16),
            jax.ShapeDtypeStruct((2, 4, 256, 256), F32),
            jax.ShapeDtypeStruct((8, D), F32),
            jax.ShapeDtypeStruct((len(CHIP_FLIPS), 3, ROW_SHARD, D), BF16),
        ],
        scratch_shapes=[
            pltpu.VMEM((tm + 8, HALF), F32),
            pltpu.VMEM((tm + 8, HALF), F32),
            pltpu.VMEM((tm + 8, HALF), F32),
            pltpu.VMEM((tm, HALF), F32),
            pltpu.VMEM((tm, HALF), F32),
            pltpu.VMEM((tm + 8, HALF), F32),
            pltpu.VMEM((8, HALF), F32),
            pltpu.VMEM((8, HALF), F32),
            pltpu.SemaphoreType.DMA((len(CHIP_FLIPS),)),
            pltpu.SemaphoreType.DMA((len(CHIP_FLIPS),)),
        ],
        compiler_params=_params(2),
    )(proj, proj, hh, hh, d_hh, keep, w_all, w_all, rg_ba, rg_bx, rg_lambda, conv_w, conv_b, sum_3_bf)


def _mid(x, target, o, hh, proj, w_all, gate, final_g):
    t = x.shape[0]
    tm = min(256, t)
    n_tiles = t // tm
    full = pl.BlockSpec((tm, D), lambda i: (i, 0))
    half = lambda col: pl.BlockSpec((tm, 512), lambda i: (i, col))
    vec = pl.BlockSpec((1, D), lambda i: (0, 0))
    any_spec = pl.BlockSpec(memory_space=pl.ANY)

    def body(x_ref, tg_ref, o_ref, hh_ref, ga0, ga1, gr0, gr1, ma0, ma1, mr0, mr1, w_hbm, gate_ref, fg_ref,
             dres_ref, do_ref, dhh_ref, dga_ref, dgr_ref, dma_ref, dmr_ref, gw_hbm, vec_ref, w, gw, sem, w_sems):
        i = pl.program_id(0)

        @pl.when(i == 0)
        def _():
            loads = [
                pltpu.make_async_copy(
                    w_hbm.at[s, pl.ds(ROW_SHARD * k, ROW_SHARD), :], w.at[k, pl.ds(ROW_SHARD * s, ROW_SHARD), :], w_sems.at[k * N_DEV + s]
                )
                for k in range(3)
                for s in range(N_DEV)
            ]
            for cp in loads:
                cp.start()
            for cp in loads:
                cp.wait()
            gw[...] = jnp.zeros_like(gw)
            vec_ref[...] = jnp.zeros_like(vec_ref)

        cat = lambda a, b: jnp.concatenate([a[...], b[...]], axis=1)
        ga, gr, ma, mr = cat(ga0, ga1), cat(gr0, gr1), cat(ma0, ma1), cat(mr0, mr1)
        ov, hv, gate_v, fg = o_ref[...], hh_ref[...], gate_ref[...], fg_ref[...]
        sg_a = _sigmoid(ga)
        sg_r = _sigmoid(gr)
        silu_a = ga * sg_a
        silu_r = gr * sg_r
        ya = (ov * silu_a).astype(BF16)
        yr = (hv * silu_r).astype(BF16)
        pa = _dot(ya, w[0])
        pr = _dot(yr, w[1])
        sa = _sigmoid(ma)
        sr = _sigmoid(mr)
        merged = (sa * pa + sr * pr).astype(BF16)
        mo = _dot(merged, w[2])
        out = x_ref[...] + gate_v * mo
        r2 = lax.rsqrt(jnp.mean(out * out, axis=-1, keepdims=True) + NORM_EPS)
        outn = out * r2
        err = outn * fg - tg_ref[...]
        loss = 0.5 * jnp.sum(jnp.mean(err * err, axis=-1, keepdims=True), axis=0, keepdims=True)
        dy = err * (1.0 / D)
        g_final = jnp.sum(dy * outn, axis=0, keepdims=True)
        doutn = dy * fg
        dout = r2 * (doutn - outn * jnp.mean(doutn * outn, axis=-1, keepdims=True))
        dres_ref[...] = dout
        d_gate = jnp.sum(dout * mo, axis=0, keepdims=True)
        dmo = (dout * gate_v).astype(BF16)
        dmerged = _dot_nt(dmo, w[2])
        gw[2] = gw[2] + _dot_tn(merged, dmo)
        dpa = (dmerged * sa).astype(BF16)
        dpr = (dmerged * sr).astype(BF16)
        dma_ref[...] = (dmerged * pa * (sa * (1.0 - sa))).astype(BF16)
        dmr_ref[...] = (dmerged * pr * (sr * (1.0 - sr))).astype(BF16)
        dya = _dot_nt(dpa, w[0])
        dyr = _dot_nt(dpr, w[1])
        gw[0] = gw[0] + _dot_tn(ya, dpa)
        gw[1] = gw[1] + _dot_tn(yr, dpr)
        do_ref[...] = dya * silu_a
        dhh_ref[...] = dyr * silu_r
        dga_ref[...] = (dya * ov * (sg_a * (1.0 + ga * (1.0 - sg_a)))).astype(BF16)
        dgr_ref[...] = (dyr * hv * (sg_r * (1.0 + gr * (1.0 - sg_r)))).astype(BF16)
        zeros = jnp.zeros((5, D), F32)
        vec_ref[...] = vec_ref[...] + jnp.concatenate([g_final, d_gate, jnp.broadcast_to(loss, (1, D)), zeros], axis=0)

        @pl.when(i == n_tiles - 1)
        def _():
            cp = pltpu.make_async_copy(gw, gw_hbm, sem)
            cp.start()
            cp.wait()

    bf_tile = jax.ShapeDtypeStruct((t, D), BF16)
    f_tile = jax.ShapeDtypeStruct((t, D), F32)
    return pl.pallas_call(
        body,
        name="mid",
        grid=(n_tiles,),
        in_specs=[full, full, full, full,
                  half(SEG_GA // 512), half(SEG_GA // 512 + 1), half(SEG_GR // 512), half(SEG_GR // 512 + 1),
                  half(SEG_MA // 512), half(SEG_MA // 512 + 1), half(SEG_MR // 512), half(SEG_MR // 512 + 1),
                  any_spec, vec, vec],
        out_specs=[full, full, full, full, full, full, full, any_spec, pl.BlockSpec((8, D), lambda i: (0, 0))],
        out_shape=[f_tile, f_tile, f_tile, bf_tile, bf_tile, bf_tile, bf_tile,
                   jax.ShapeDtypeStruct((3, D, D), F32), jax.ShapeDtypeStruct((8, D), F32)],
        scratch_shapes=[pltpu.VMEM((3, D, D), BF16), pltpu.VMEM((3, D, D), F32), pltpu.SemaphoreType.DMA, pltpu.SemaphoreType.DMA((3 * N_DEV,))],
        compiler_params=_params(1),
    )(x, target, o, hh, proj, proj, proj, proj, proj, proj, proj, proj, w_all, gate, final_g)


_SEGMENTS = ((SEG_Q, 1024), (SEG_K, 256), (SEG_V, 256), (SEG_GA, 1024), (SEG_XR, 1024), (SEG_GR, 1024), (SEG_MA, 1024), (SEG_MR, 1024))


def _seg_specs(tm):
    return [pl.BlockSpec((tm, width), lambda i: (i, 0)) for _, width in _SEGMENTS]


def _bwd_dh(dsegs, w_all, x, dres, mod, norm_g, sum_w_bf):
    t = x.shape[0]
    tm = min(512, t)
    n_tiles = t // tm
    n_seg = len(_SEGMENTS)
    full = pl.BlockSpec((tm, D), lambda i: (i, 0))
    views = [lambda ref, peer: ref.at[_chip_of(peer)]]

    def body(*refs):
        segs = refs[:n_seg]
        w_hbm, x_ref, dres_ref, mod_ref, ng_ref, sum_hbm, gx_ref, vec_ref, land_hbm, w, sems, send_sems, recv_sems = refs[n_seg:]
        i = pl.program_id(0)

        @pl.when(i == 0)
        def _():
            for cp in _chip_copies([sum_hbm], [land_hbm], views, send_sems, recv_sems):
                cp.start()
            loads = [
                pltpu.make_async_copy(w_hbm.at[s], w.at[pl.ds(W_IN_SHARD * s, W_IN_SHARD), :], sems.at[s])
                for s in range(N_DEV)
            ]
            for cp in loads:
                cp.start()
            for cp in loads:
                cp.wait()
            vec_ref[...] = jnp.zeros_like(vec_ref)

        dh = sum(_dot(seg[...], w[start : start + width, :]) for seg, (start, width) in zip(segs, _SEGMENTS))
        xv = x_ref[...]
        r1 = lax.rsqrt(jnp.mean(xv * xv, axis=-1, keepdims=True) + NORM_EPS)
        xn = xv * r1
        ng = ng_ref[...]
        scale1 = 1.0 + mod_ref[:, D : 2 * D]
        d_shift = jnp.sum(dh, axis=0, keepdims=True)
        d_scale = jnp.sum(dh * (xn * ng), axis=0, keepdims=True)
        g_norm = jnp.sum(dh * xn * scale1, axis=0, keepdims=True)
        dxn = dh * (ng * scale1)
        dx = r1 * (dxn - xn * jnp.mean(dxn * xn, axis=-1, keepdims=True))
        gx_ref[...] = dres_ref[...] + dx
        vec_ref[...] = vec_ref[...] + jnp.concatenate([d_shift, d_scale, g_norm, jnp.zeros((5, D), F32)], axis=0)

        @pl.when(i == n_tiles - 1)
        def _():
            for cp in _chip_copies([sum_hbm], [land_hbm], views, send_sems, recv_sems):
                cp.wait_recv()
                cp.wait_send()

    any_spec = pl.BlockSpec(memory_space=pl.ANY)
    n_copies = len(CHIP_FLIPS)
    return pl.pallas_call(
        body,
        name="bwd_dh",
        grid=(n_tiles,),
        in_specs=_seg_specs(tm) + [any_spec, full, full, pl.BlockSpec((1, 3 * D), lambda i: (0, 0)), pl.BlockSpec((1, D), lambda i: (0, 0)),
                                   any_spec],
        out_specs=[full, pl.BlockSpec((8, D), lambda i: (0, 0)), any_spec],
        out_shape=[jax.ShapeDtypeStruct((t, D), F32), jax.ShapeDtypeStruct((8, D), F32),
                   jax.ShapeDtypeStruct((n_copies, W_IN_SHARD, D), BF16)],
        scratch_shapes=[pltpu.VMEM((IN_W, D), BF16), pltpu.SemaphoreType.DMA((N_DEV,)),
                        pltpu.SemaphoreType.DMA((n_copies,)), pltpu.SemaphoreType.DMA((n_copies,))],
        compiler_params=_params(1),
    )(*dsegs, w_all, x, dres, mod, norm_g, sum_w_bf)


def _bwd_gw(dsegs, h, sum_g_bf):
    t = h.shape[0]
    tm = min(512, t)
    n_tiles = t // tm
    n_seg = len(_SEGMENTS)
    n_chips = N_DEV // 2
    views = [lambda ref, peer: ref.at[:, :, _chip_of(peer)]]

    def body(*refs):
        segs = refs[:n_seg]
        (h_ref, sg_hbm, own_hbm, recv_hbm, lg_hbm, acc, stage, own_sems, pair_send, pair_recv, send_sems, recv_sems) = refs[n_seg:]
        i = pl.program_id(0)

        @pl.when(i == 0)
        def _():
            for cp in _chip_copies([sg_hbm], [lg_hbm], views, send_sems, recv_sems):
                cp.start()
            acc[...] = jnp.zeros_like(acc)

        hv = h_ref[...]
        for seg, (start, width) in zip(segs, _SEGMENTS):
            acc[start : start + width, :] = acc[start : start + width, :] + _dot_tn(seg[...], hv)

        @pl.when(i == n_tiles - 1)
        def _():
            me = _position()
            sibling = _flip(me, 1)

            def rows(q, c):
                return pl.ds(pl.multiple_of(W_IN_SHARD * (2 * q + c), 8), W_IN_SHARD)

            def to_sibling(q):
                return pltpu.make_async_remote_copy(
                    src_ref=stage.at[q % 2], dst_ref=recv_hbm.at[q], send_sem=pair_send.at[q], recv_sem=pair_recv.at[q],
                    device_id=sibling, device_id_type=MESH,
                )

            kept = [pltpu.make_async_copy(acc.at[rows(q, me[2]), :], own_hbm.at[q], own_sems.at[q]) for q in range(n_chips)]
            for cp in kept:
                cp.start()
            for q in range(n_chips):
                if q >= 2:
                    to_sibling(q - 2).wait_send()
                stage[q % 2] = acc[rows(q, sibling[2]), :].astype(BF16)
                to_sibling(q).start()
            for q in range(n_chips - 2, n_chips):
                to_sibling(q).wait_send()
            for q in range(n_chips):
                to_sibling(q).wait_recv()
            for cp in kept:
                cp.wait()
            for cp in _chip_copies([sg_hbm], [lg_hbm], views, send_sems, recv_sems):
                cp.wait_recv()
                cp.wait_send()

    any_spec = pl.BlockSpec(memory_space=pl.ANY)
    n_copies = len(CHIP_FLIPS)
    return pl.pallas_call(
        body,
        name="bwd_gw",
        grid=(n_tiles,),
        in_specs=_seg_specs(tm) + [pl.BlockSpec((tm, D), lambda i: (i, 0)), any_spec],
        out_specs=[any_spec, any_spec, any_spec],
        out_shape=[jax.ShapeDtypeStruct((n_chips, W_IN_SHARD, D), F32),
                   jax.ShapeDtypeStruct((n_chips, W_IN_SHARD, D), BF16),
                   jax.ShapeDtypeStruct((len(CHIP_FLIPS), 2, 4, GATE_SHARD, 256), BF16)],
        scratch_shapes=[pltpu.VMEM((IN_W, D), F32), pltpu.VMEM((2, W_IN_SHARD, D), BF16),
                        pltpu.SemaphoreType.DMA((n_chips,)), pltpu.SemaphoreType.DMA((n_chips,)), pltpu.SemaphoreType.DMA((n_chips,)),
                        pltpu.SemaphoreType.DMA((n_copies,)), pltpu.SemaphoreType.DMA((n_copies,))],
        compiler_params=_params(1),
    )(*dsegs, h, sum_g_bf)


def _adamw(w, g, m, v):
    m = ADAM_B1 * m + (1.0 - ADAM_B1) * g
    v = ADAM_B2 * v + (1.0 - ADAM_B2) * (g * g)
    m_hat = m / (1.0 - ADAM_B1**ADAM_STEP)
    v_hat = v / (1.0 - ADAM_B2**ADAM_STEP)
    delta = -ADAM_LR * (m_hat / (jnp.sqrt(v_hat) + ADAM_EPS) + ADAM_WD * w)
    return delta, m, v


def _w_in_update(own, land, w_t, m_t, v_t):
    n_slots, rows, cols = land.shape
    rb = 208
    spec = pl.BlockSpec((rb, cols), lambda i: (i, 0))

    def body(o_ref, l_ref, w_ref, m_ref, v_ref, g_ref, d_ref, m2_ref, v2_ref):
        total = o_ref[...]
        for s in range(n_slots):
            total = total + l_ref[s].astype(F32)
        g_ref[...] = total
        d_ref[...], m2_ref[...], v2_ref[...] = _adamw(w_ref[...], total, m_ref[...], v_ref[...])

    shape = jax.ShapeDtypeStruct((rows, cols), F32)
    return pl.pallas_call(
        body,
        name="w_in_update",
        grid=(rows // rb,),
        in_specs=[spec, pl.BlockSpec((n_slots, rb, cols), lambda i: (0, i, 0)), spec, spec, spec],
        out_specs=[spec] * 4,
        out_shape=[shape] * 4,
        compiler_params=_params(1),
    )(own, land, w_t, m_t, v_t)


def _shard_update(own_3, land_3, own_g, land_g, weights, m_in, v_in):
    n = len(weights)

    def body(*refs):
        o3_ref, l3_ref, og_ref, lg_ref = refs[:4]
        w_refs, m_refs, v_refs = (refs[4 + n * k : 4 + n * (k + 1)] for k in range(3))
        outs = refs[4 + 3 * n :]
        for k in range(n):
            if k < 3:
                g = o3_ref[k]
                for slot in range(3):
                    g = g + l3_ref[slot, k].astype(F32)
            else:
                g = og_ref[k - 3]
                for slot in range(3):
                    g = g + lg_ref[slot, k - 3].astype(F32)
            d, m2, v2 = _adamw(w_refs[k][0], g, m_refs[k][0], v_refs[k][0])
            for which, val in enumerate((g, d, m2, v2)):
                outs[which * n + k][0] = val

    shapes = [jax.ShapeDtypeStruct(w.shape, F32) for w in weights]
    res = pl.pallas_call(
        body, name="shard_update", out_shape=shapes * 4, compiler_params=pltpu.CompilerParams(vmem_limit_bytes=VMEM_LIMIT)
    )(own_3, land_3, own_g, land_g, *weights, *m_in, *v_in)
    return [res[n * k : n * (k + 1)] for k in range(4)]


VEC_LAYOUT = (("b_ada", 3 * D), ("norm_g", D), ("attn_sinks", 128), ("conv_b", D), ("rg_ba", D), ("rg_bx", D), ("rg_lambda", D),
              ("final_g", D), ("conv_w", CONV_WIDTH * D), ("loss", 128))
VEC_OFFSET = {name: sum(w for _, w in VEC_LAYOUT[:k]) for k, (name, _) in enumerate(VEC_LAYOUT)}
VEC_ROWS = 112
VEC_WIDTH = VEC_ROWS * 128
assert sum(w for _, w in VEC_LAYOUT) <= VEC_WIDTH
N_REPLICATED = 8


def _small_update(vec_all, c_all_t, me, rep_w, rep_m, rep_v, ada, conv):
    n_ada = ada[0].shape[1]
    n_conv = conv[0].shape[1]

    def body(*refs):
        me_ref, vec_ref, ct_ref = refs[:3]
        w_refs, m_refs, v_refs = (refs[3 + N_REPLICATED * k : 3 + N_REPLICATED * (k + 1)] for k in range(3))
        pos = 3 + 3 * N_REPLICATED
        ada_refs, conv_refs = refs[pos : pos + 3], refs[pos + 3 : pos + 6]
        outs = refs[pos + 6 :]
        loss_ref, rep_outs = outs[0], outs[1 : 1 + 4 * N_REPLICATED]
        ada_outs, conv_outs = outs[1 + 4 * N_REPLICATED : 5 + 4 * N_REPLICATED], outs[5 + 4 * N_REPLICATED : 9 + 4 * N_REPLICATED]
        conv_buf = outs[9 + 4 * N_REPLICATED]

        total = vec_ref[0:1, :]
        for slot in range(1, N_DEV):
            total = total + vec_ref[slot : slot + 1, :]
        loss_ref[...] = total[:, VEC_OFFSET["loss"] : VEC_OFFSET["loss"] + 1]
        for k, (name, _) in enumerate(VEC_LAYOUT[:N_REPLICATED]):
            width = w_refs[k].shape[1]
            g = total[:, VEC_OFFSET[name] : VEC_OFFSET[name] + width]
            d, m2, v2 = _adamw(w_refs[k][...], g, m_refs[k][...], v_refs[k][...])
            for which, val in enumerate((g, d, m2, v2)):
                rep_outs[which * N_REPLICATED + k][...] = val

        device = me_ref[0]
        for k in range(CONV_WIDTH):
            conv_buf[k : k + 1, :] = total[:, VEC_OFFSET["conv_w"] + D * k : VEC_OFFSET["conv_w"] + D * (k + 1)]
        g_conv = conv_buf[:, pl.ds(pl.multiple_of(device * n_conv, 128), n_conv)]
        for ref, val in zip(conv_outs, (g_conv,) + _adamw(conv_refs[0][...], g_conv, conv_refs[1][...], conv_refs[2][...])):
            ref[...] = val

        dmod = vec_ref[:, pl.ds(pl.multiple_of(device * n_ada, 128), n_ada)]
        g_ada = _dot(ct_ref[...], dmod)
        for ref, val in zip(ada_outs, (g_ada,) + _adamw(ada_refs[0][...], g_ada, ada_refs[1][...], ada_refs[2][...])):
            ref[...] = val

    vmem = pl.BlockSpec(memory_space=pltpu.VMEM)
    rep_shapes = [jax.ShapeDtypeStruct(w.shape, F32) for w in rep_w]
    out_shape = ([jax.ShapeDtypeStruct((1, 1), F32)] + rep_shapes * 4 + [jax.ShapeDtypeStruct(ada[0].shape, F32)] * 4
                 + [jax.ShapeDtypeStruct(conv[0].shape, F32)] * 4)
    n_in = 3 + 3 * N_REPLICATED + 6
    res = pl.pallas_call(
        body,
        name="small_update",
        in_specs=[pl.BlockSpec(memory_space=pltpu.SMEM)] + [vmem] * (n_in - 1),
        out_specs=[vmem] * len(out_shape),
        out_shape=out_shape,
        scratch_shapes=[pltpu.VMEM((CONV_WIDTH, D), F32)],
        compiler_params=pltpu.CompilerParams(vmem_limit_bytes=VMEM_LIMIT),
    )(me, vec_all, c_all_t, *rep_w, *rep_m, *rep_v, *ada, *conv)
    loss = res[0]
    reps = [res[1 + N_REPLICATED * k : 1 + N_REPLICATED * (k + 1)] for k in range(4)]
    return loss, reps, res[1 + 4 * N_REPLICATED : 5 + 4 * N_REPLICATED], res[5 + 4 * N_REPLICATED : 9 + 4 * N_REPLICATED]


def kernel(x, c, positions, w_ada, b_ada, norm_g, w_in, attn_sinks, conv_w, conv_b, rg_wa, rg_ba, rg_wx, rg_bx, rg_lambda, w_attn_proj, w_rnn_proj, w_out, final_g, loss_target, m_w_ada, m_b_ada, m_norm_g, m_w_in, m_attn_sinks, m_conv_w, m_conv_b, m_rg_wa, m_rg_ba, m_rg_wx, m_rg_bx, m_rg_lambda, m_w_attn_proj, m_w_rnn_proj, m_w_out, m_final_g, v_w_ada, v_b_ada, v_norm_g, v_w_in, v_attn_sinks, v_conv_w, v_conv_b, v_rg_wa, v_rg_ba, v_rg_wx, v_rg_bx, v_rg_lambda, v_w_attn_proj, v_w_rnn_proj, v_w_out, v_final_g):
    my_x, my_y, my_c = _position()
    me = _slot((my_x, my_y, my_c))
    x2, target = x[0], loss_target[0]
    pos = positions[0]

    gate_rows = lambda w: w[0].transpose(1, 0, 2).reshape(GATE_SHARD, D)
    w_pack = jnp.concatenate([w_attn_proj[0], w_rnn_proj[0], w_out[0], gate_rows(rg_wa), gate_rows(rg_wx)], axis=0).astype(BF16)
    conv_rows = jnp.pad(conv_w[0], ((0, 8 - CONV_WIDTH), (0, 0)))
    w_in_t = w_in[0].T
    c_rows, conv_all = _all_gather_direct("ag_small", [jnp.broadcast_to(c, (8, D)), conv_rows])
    c_all = c_rows[:, 0, :]
    conv_full = conv_all[:, :CONV_WIDTH, :].transpose(1, 0, 2).reshape(CONV_WIDTH, D)

    n_ada = w_ada.shape[2]
    b_cols = lax.dynamic_slice(b_ada, (0, me * n_ada), (1, n_ada))
    mod_cols = _ada_mod(c_all, w_ada[0], b_cols)
    (mod_all,) = _all_gather_direct("ag_mod", [mod_cols])
    mod = lax.dynamic_slice(mod_all, (0, me, 0), (N_DEV, 1, n_ada)).reshape(1, 3 * D)
    gate = mod[:, 2 * D :]

    tables = _rope_tables(pos)
    my_chip = _chip_of((my_x, my_y)).astype(jnp.int32).reshape(1)
    proj, h, w_t_all, w_all = _fwd_in(x2, mod, norm_g, w_in_t.astype(BF16), tables[0], w_pack, my_chip)
    o = _attn_fwd(proj, attn_sinks)
    keep = (pos != 0).astype(F32)[:, None]
    hh = _rnn_fwd(proj, keep, w_all, rg_ba, rg_bx, rg_lambda, conv_full, conv_b)

    dres, d_o, d_hh, d_ga, d_gr, d_ma, d_mr, gw3, mid_vec = _mid(x2, target, o, hh, proj, w_all, gate, final_g[None, :])
    index = jnp.stack([my_c, _chip_of((my_x, my_y))]).astype(jnp.int32)
    part_3 = gw3.reshape(3, 4, 2, ROW_SHARD, D)
    d_q, d_k, d_v, d_sink, recv_3 = _attn_bwd(proj, o, d_o, tables, attn_sinks, part_3)
    sum_3_bf, own_3 = _pair_add(
        "pair_add_3", part_3, recv_3, index, (3, 4), 1,
        pl.BlockSpec((None, None, None, ROW_SHARD, D), lambda a, q, idx: (a, q, idx[0], 0, 0)),
        pl.BlockSpec((None, None, ROW_SHARD, D), lambda a, q, idx: (a, q, 0, 0)),
        pl.BlockSpec((None, ROW_SHARD, D), lambda a, q, idx: (a, 0, 0)), (3, ROW_SHARD, D),
    )
    d_xr, g_gates, rnn_vec, land_3 = _rnn_bwd(proj, hh, d_hh, keep, w_all, rg_ba, rg_bx, rg_lambda, conv_full, conv_b, sum_3_bf)
    dsegs = (d_q, d_k, d_v, d_ga, d_xr, d_gr, d_ma, d_mr)
    part_g = g_gates.reshape(2, 4, 4, 2, GATE_SHARD, 256)
    (recv_g,) = _exchange("rs_pair_g", [part_g], [lambda ref, peer: ref.at[:, :, :, peer[2]]], [(2, 4, 4, GATE_SHARD, 256)], flips=(1,))
    sum_g_bf, own_g = _pair_add(
        "pair_add_g", part_g, recv_g[0], index, (4,), 0,
        pl.BlockSpec((2, 4, None, None, GATE_SHARD, 256), lambda q, idx: (0, 0, q, idx[0], 0, 0)),
        pl.BlockSpec((2, 4, None, GATE_SHARD, 256), lambda q, idx: (0, 0, q, 0, 0)),
        pl.BlockSpec((2, 4, GATE_SHARD, 256), lambda q, idx: (0, 0, 0, 0)), (2, 4, GATE_SHARD, 256),
    )
    part_w, recv_w, land_g = _bwd_gw(dsegs, h, sum_g_bf)
    rb = 208
    w_rows = pl.BlockSpec((None, rb, D), lambda r, q, idx: (q, r, 0))
    sum_w_bf, own_w = _pair_add(
        "pair_add_w", part_w, recv_w, index, (W_IN_SHARD // rb, 4), 1, w_rows, w_rows,
        pl.BlockSpec((rb, D), lambda r, q, idx: (r, 0)), (W_IN_SHARD, D),
    )
    grad_x, dh_vec, land_w = _bwd_dh(dsegs, w_t_all, x2, dres, mod, norm_g, sum_w_bf)

    pieces = dict(
        b_ada=jnp.concatenate([dh_vec[0:1], dh_vec[1:2], mid_vec[1:2]], axis=1),
        norm_g=dh_vec[2:3],
        attn_sinks=jnp.pad(d_sink[:, :, 0].reshape(1, 16), ((0, 0), (0, 112))),
        conv_b=rnn_vec[3:4],
        rg_ba=rnn_vec[0:1],
        rg_bx=rnn_vec[1:2],
        rg_lambda=rnn_vec[2:3],
        final_g=mid_vec[0:1],
        conv_w=rnn_vec[4:8].reshape(1, CONV_WIDTH * D),
        loss=mid_vec[2:3, 0:128],
    )
    vec = jnp.concatenate([pieces[name] for name, _ in VEC_LAYOUT], axis=1)
    vec = jnp.pad(vec, ((0, 0), (0, VEC_WIDTH - vec.shape[1]))).reshape(VEC_ROWS, 128)
    (vec_all,) = _all_gather_direct("ag_vec", [vec])
    rep_w = [b_ada, norm_g, attn_sinks, conv_b, rg_ba, rg_bx, rg_lambda, final_g[None, :]]
    rep_m = [m_b_ada, m_norm_g, m_attn_sinks, m_conv_b, m_rg_ba, m_rg_bx, m_rg_lambda, m_final_g[None, :]]
    rep_v = [v_b_ada, v_norm_g, v_attn_sinks, v_conv_b, v_rg_ba, v_rg_bx, v_rg_lambda, v_final_g[None, :]]
    loss, reps, ada_res, conv_res = _small_update(
        vec_all.reshape(N_DEV, VEC_WIDTH), c_all.T, me.astype(jnp.int32).reshape(1), rep_w, rep_m, rep_v,
        (w_ada[0], m_w_ada[0], v_w_ada[0]), (conv_w[0], m_conv_w[0], v_conv_w[0]),
    )

    w_in_res = _w_in_update(own_w, land_w, w_in_t, m_w_in[0].T, v_w_in[0].T)
    shard_res = _shard_update(
        own_3, land_3, own_g, land_g,
        (w_attn_proj, w_rnn_proj, w_out, rg_wa, rg_wx),
        (m_w_attn_proj, m_w_rnn_proj, m_w_out, m_rg_wa, m_rg_wx),
        (v_w_attn_proj, v_w_rnn_proj, v_w_out, v_rg_wa, v_rg_wx),
    )

    def by_weight(k):
        b, n, s, cb, ba, bx, lam, fg = reps[k]
        ap, rp, out, wa, wx = shard_res[k]
        return [ada_res[k][None], b, n, w_in_res[k].T[None], s, conv_res[k][None], cb, wa, ba, wx, bx, lam, ap, rp, out, fg[0]]

    return (loss[0, 0], grad_x[None], *by_weight(0), *by_weight(1), *by_weight(2), *by_weight(3))
```

```python
import functools
import math

import jax
import jax.numpy as jnp
import numpy as np
from jax import lax
from jax.experimental import pallas as pl
from jax.experimental.pallas import tpu as pltpu

F32 = jnp.float32
BF16 = jnp.bfloat16

D = 1024
HEAD_DIM = 64
ROT_DIM = 16
ROPE_THETA = 500000.0
WINDOW = 128
LRU_C = 8.0
CONV_WIDTH = 4
NORM_EPS = 1e-6
IN_W = 6656
SEG_Q, SEG_K, SEG_V, SEG_GA, SEG_XR, SEG_GR, SEG_MA, SEG_MR = 0, 1024, 1280, 1536, 2560, 3584, 4608, 5632
N_DEV = 8
W_IN_SHARD = IN_W // N_DEV
ROW_SHARD = D // N_DEV
GATE_SHARD = 256 // N_DEV

ADAM_LR = 0.001
ADAM_B1 = 0.9
ADAM_B2 = 0.999
ADAM_EPS = 1e-08
ADAM_WD = 0.01
ADAM_STEP = 10

VMEM_LIMIT = 60 * 1024 * 1024
MESH = pl.DeviceIdType.MESH
AXES = ("x", "y", "c")


def _dot(a, b):
    return jnp.dot(a, b, preferred_element_type=F32)


def _dot_nt(a, b):
    return lax.dot_general(a, b, (((1,), (1,)), ((), ())), preferred_element_type=F32)


def _dot_tn(a, b):
    return lax.dot_general(a, b, (((0,), (0,)), ((), ())), preferred_element_type=F32)


def _sigmoid(z):
    return 1.0 / (1.0 + jnp.exp(-z))


def _params(n_grid, vmem=VMEM_LIMIT):
    return pltpu.CompilerParams(dimension_semantics=("arbitrary",) * n_grid, vmem_limit_bytes=vmem)


def _position():
    return lax.axis_index("x"), lax.axis_index("y"), lax.axis_index("c")


def _flip(pos, k):
    x, y, c = pos
    return (1 - x if k & 4 else x, 1 - y if k & 2 else y, 1 - c if k & 1 else c)


def _slot(pos):
    return 4 * pos[0] + 2 * pos[1] + pos[2]


def _all_gather_direct(name, arrs):
    n = len(arrs)

    def body(*refs):
        ins, outs = refs[:n], refs[n : 2 * n]
        send_sems, recv_sems, local_sems = refs[2 * n :]
        me = _position()
        mine = [pltpu.make_async_copy(ins[a], outs[a].at[_slot(me)], local_sems.at[a]) for a in range(n)]
        for cp in mine:
            cp.start()

        def copy(a, k, block):
            return pltpu.make_async_remote_copy(
                src_ref=ins[a],
                dst_ref=outs[a].at[_slot(block)],
                send_sem=send_sems.at[a * 7 + k - 1],
                recv_sem=recv_sems.at[a * 7 + k - 1],
                device_id=_flip(me, k),
                device_id_type=MESH,
            )

        sent = [copy(a, k, me) for a in range(n) for k in range(1, N_DEV)]
        for cp in sent:
            cp.start()
        for a in range(n):
            for k in range(1, N_DEV):
                copy(a, k, _flip(me, k)).wait_recv()
        for cp in sent:
            cp.wait_send()
        for cp in mine:
            cp.wait()

    any_spec = pl.BlockSpec(memory_space=pl.ANY)
    return pl.pallas_call(
        body,
        name=name,
        out_shape=[jax.ShapeDtypeStruct((N_DEV,) + a.shape, a.dtype) for a in arrs],
        in_specs=[any_spec] * n,
        out_specs=[any_spec] * n,
        scratch_shapes=[
            pltpu.SemaphoreType.DMA((7 * n,)),
            pltpu.SemaphoreType.DMA((7 * n,)),
            pltpu.SemaphoreType.DMA((n,)),
        ],
    )(*arrs)


def _direct_gather_copies(src_hbm, all_hbm, send_sems, recv_sems, arriving):
    me = _position()
    return [
        pltpu.make_async_remote_copy(
            src_ref=src_hbm,
            dst_ref=all_hbm.at[_slot(_flip(me, k) if arriving else me)],
            send_sem=send_sems.at[k - 1],
            recv_sem=recv_sems.at[k - 1],
            device_id=_flip(me, k),
            device_id_type=MESH,
        )
        for k in range(1, N_DEV)
    ]


CHIP_FLIPS = (2, 4, 6)


def _chip_of(pos):
    return 2 * pos[0] + pos[1]


def _chip_copies(srcs, lands, views, send_sems, recv_sems):
    me = _position()
    copies = []
    for a, (src, land) in enumerate(zip(srcs, lands)):
        for j, k in enumerate(CHIP_FLIPS):
            peer = _flip(me, k)
            copies.append(
                pltpu.make_async_remote_copy(
                    src_ref=views[a](src, peer),
                    dst_ref=land.at[j],
                    send_sem=send_sems.at[len(CHIP_FLIPS) * a + j],
                    recv_sem=recv_sems.at[len(CHIP_FLIPS) * a + j],
                    device_id=peer,
                    device_id_type=MESH,
                )
            )
    return copies


def _exchange(name, arrs, views, out_shapes, flips):
    n = len(arrs)
    nf = len(flips)

    def body(*refs):
        ins, outs = refs[:n], refs[n : 2 * n]
        send_sems, recv_sems = refs[2 * n :]
        me = _position()

        def copy(a, j):
            peer = _flip(me, flips[j])
            return pltpu.make_async_remote_copy(
                src_ref=views[a](ins[a], peer),
                dst_ref=outs[a].at[j],
                send_sem=send_sems.at[a * nf + j],
                recv_sem=recv_sems.at[a * nf + j],
                device_id=peer,
                device_id_type=MESH,
            )

        sent = [copy(a, j) for a in range(n) for j in range(nf)]
        for cp in sent:
            cp.start()
        for cp in sent:
            cp.wait_recv()
        for cp in sent:
            cp.wait_send()

    any_spec = pl.BlockSpec(memory_space=pl.ANY)
    return pl.pallas_call(
        body,
        name=name,
        out_shape=[jax.ShapeDtypeStruct((nf,) + tuple(sh), a.dtype) for sh, a in zip(out_shapes, arrs)],
        in_specs=[any_spec] * n,
        out_specs=[any_spec] * n,
        scratch_shapes=[pltpu.SemaphoreType.DMA((nf * n,)), pltpu.SemaphoreType.DMA((nf * n,))],
    )(*arrs)


def _pair_add(name, own, recv, index, grid, chip_axis, own_spec, recv_spec, mine_spec, mine_shape):
    def body(idx_ref, own_ref, recv_ref, sb_ref, mine_ref):
        total = own_ref[...] + recv_ref[...].astype(F32)
        sb_ref[...] = total.astype(BF16)

        @pl.when(pl.program_id(chip_axis) == idx_ref[1])
        def _():
            mine_ref[...] = total

    return pl.pallas_call(
        body,
        name=name,
        grid_spec=pltpu.PrefetchScalarGridSpec(
            num_scalar_prefetch=1, grid=grid, in_specs=[own_spec, recv_spec], out_specs=[recv_spec, mine_spec]
        ),
        out_shape=[jax.ShapeDtypeStruct(recv.shape, BF16), jax.ShapeDtypeStruct(mine_shape, F32)],
        compiler_params=_params(len(grid)),
    )(index, own, recv)


PACK_GATE = 3 * ROW_SHARD
PACK_ROWS = PACK_GATE + 2 * GATE_SHARD


def _fwd_in(x, c_rows, conv_rows, w_ada, b_cols, norm_g, w_shard, tables, pack, where):
    t = x.shape[0]
    tm = min(1024, t)
    tn = IN_W // 4
    n_i, n_j = t // tm, IN_W // tn
    n_rope = (SEG_V - SEG_Q) // 128

    n_ada = w_ada.shape[1]

    def body(where_ref, x_ref, wada_ref, bcols_ref, ng_ref, cos_ref, s1_ref, s2_ref, shard_hbm, pack_hbm, c_hbm, conv_hbm,
             proj_ref, h_ref, mod_ref, wt_hbm, all_hbm, call_hbm, convall_hbm,
             wbuf, hbuf, cbuf, mcols, mod_all, mod_buf,
             w_send, w_recv, load_sems, local_sem, p_send, p_recv, p_local,
             c_send, c_recv, c_local, v_send, v_recv, v_local, m_send, m_recv):
        j = pl.program_id(0)
        i = pl.program_id(1)
        me = _position()
        sibling = _flip(me, 1)
        chip = where_ref[0]
        slot = where_ref[1]

        def modulation():
            own_c = pltpu.make_async_copy(c_hbm, call_hbm.at[_slot(me)], c_local)
            own_v = pltpu.make_async_copy(conv_hbm, convall_hbm.at[_slot(me)], v_local)
            own_c.start()
            own_v.start()
            for cp in _direct_gather_copies(c_hbm, call_hbm, c_send, c_recv, arriving=False):
                cp.start()
            for cp in _direct_gather_copies(conv_hbm, convall_hbm, v_send, v_recv, arriving=False):
                cp.start()
            for cp in _direct_gather_copies(c_hbm, call_hbm, c_send, c_recv, arriving=True):
                cp.wait_recv()
            own_c.wait()
            fetch = pltpu.make_async_copy(call_hbm, cbuf, c_local)
            fetch.start()
            fetch.wait()
            mcols[...] = _dot(cbuf[:, 0, :].astype(BF16), wada_ref[...].astype(BF16)) + bcols_ref[...]
            mod_all[slot] = mcols[...]
            swaps = lambda arriving: [
                pltpu.make_async_remote_copy(
                    src_ref=mcols,
                    dst_ref=mod_all.at[_slot(_flip(me, k) if arriving else me)],
                    send_sem=m_send.at[k - 1],
                    recv_sem=m_recv.at[k - 1],
                    device_id=_flip(me, k),
                    device_id_type=MESH,
                )
                for k in range(1, N_DEV)
            ]
            for cp in swaps(False):
                cp.start()
            for cp in swaps(True):
                cp.wait_recv()
            mine = lax.broadcasted_iota(jnp.int32, (N_DEV, n_ada), 0) == slot
            for s in range(N_DEV):
                row = jnp.sum(jnp.where(mine, mod_all[s], 0.0), axis=0, keepdims=True)
                mod_buf[:, n_ada * s : n_ada * (s + 1)] = row
            for cp in swaps(False):
                cp.wait_send()
            for cp in _direct_gather_copies(c_hbm, call_hbm, c_send, c_recv, arriving=False):
                cp.wait_send()
            for cp in _direct_gather_copies(conv_hbm, convall_hbm, v_send, v_recv, arriving=True):
                cp.wait_recv()
            for cp in _direct_gather_copies(conv_hbm, convall_hbm, v_send, v_recv, arriving=False):
                cp.wait_send()
            own_v.wait()

        def shard_copy(k, block, to, src=None):
            return pltpu.make_async_remote_copy(
                src_ref=wt_hbm.at[_slot(block)] if src is None else src,
                dst_ref=wt_hbm.at[_slot(block)],
                send_sem=w_send.at[k],
                recv_sem=w_recv.at[k],
                device_id=to,
                device_id_type=MESH,
            )

        own_shard = pltpu.make_async_copy(shard_hbm, wt_hbm.at[_slot(me)], local_sem)
        own_pack = pltpu.make_async_copy(pack_hbm, all_hbm.at[_slot(me)], p_local)

        def load_tile(q):
            loads = [pltpu.make_async_copy(wt_hbm.at[2 * q + c], wbuf.at[pl.ds(W_IN_SHARD * c, W_IN_SHARD), :], load_sems.at[c]) for c in range(2)]
            for cp in loads:
                cp.start()
            for cp in loads:
                cp.wait()

        @pl.when((j == 0) & (i == 0))
        def _():
            own_shard.start()
            shard_copy(0, me, sibling, src=shard_hbm).start()
            for rel in range(1, 4):
                shard_copy(rel, me, _flip(me, 2 * rel), src=shard_hbm).start()
            own_pack.start()
            for cp in _direct_gather_copies(pack_hbm, all_hbm, p_send, p_recv, arriving=False):
                cp.start()
            modulation()
            mod_ref[...] = mod_buf[...]
            own_shard.wait()
            shard_copy(0, sibling, me).wait_recv()
            load_tile(chip)

        for rel in range(1, 4):

            @pl.when((j == rel) & (i == 0))
            def _(rel=rel):
                there = _flip(me, 2 * rel)
                shard_copy(rel, there, me).wait_recv()
                shard_copy(3 + rel, there, sibling).start()
                shard_copy(3 + rel, _flip(sibling, 2 * rel), me).wait_recv()
                load_tile(chip ^ rel)

        rows = pl.ds(pl.multiple_of(i * tm, tm), tm)

        @pl.when(j == 0)
        def _():
            xv = x_ref[...]
            r = lax.rsqrt(jnp.mean(xv * xv, axis=-1, keepdims=True) + NORM_EPS)
            shift = mod_buf[:, 0:D]
            scale = mod_buf[:, D : 2 * D]
            hv = (((xv * r) * ng_ref[...]) * (1.0 + scale) + shift).astype(BF16)
            hbuf[rows, :] = hv
            h_ref[...] = hv

        proj_ref[...] = _dot_nt(hbuf[rows, :], wbuf[...])

        @pl.when((chip ^ j) == 0)
        def _():
            cos, s1, s2 = cos_ref[...], s1_ref[...], s2_ref[...]
            for g in range(n_rope):
                cols = slice(128 * g, 128 * (g + 1))
                proj_ref[:, cols] = _rope(proj_ref[:, cols], cos, s1, s2)

        @pl.when((j == n_j - 1) & (i == n_i - 1))
        def _():
            shard_copy(0, me, sibling, src=shard_hbm).wait_send()
            for rel in range(1, 4):
                shard_copy(rel, me, _flip(me, 2 * rel), src=shard_hbm).wait_send()
                shard_copy(3 + rel, _flip(me, 2 * rel), sibling).wait_send()
            for cp in _direct_gather_copies(pack_hbm, all_hbm, p_send, p_recv, arriving=True):
                cp.wait_recv()
            for cp in _direct_gather_copies(pack_hbm, all_hbm, p_send, p_recv, arriving=False):
                cp.wait_send()
            own_pack.wait()

    tab = pl.BlockSpec((tm, 128), lambda j, i, c: (i, 0))
    any_spec = pl.BlockSpec(memory_space=pl.ANY)
    return pl.pallas_call(
        body,
        name="fwd_in",
        grid_spec=pltpu.PrefetchScalarGridSpec(
            num_scalar_prefetch=1,
            grid=(n_j, n_i),
            in_specs=[
                pl.BlockSpec((tm, D), lambda j, i, c: (jnp.where(j == 0, i, 0), 0)),
                pl.BlockSpec(w_ada.shape, lambda j, i, c: (0, 0)),
                pl.BlockSpec((1, n_ada), lambda j, i, c: (0, 0)),
                pl.BlockSpec((1, D), lambda j, i, c: (0, 0)),
                tab, tab, tab, any_spec, any_spec, any_spec, any_spec,
            ],
            out_specs=[pl.BlockSpec((tm, tn), lambda j, i, c: (i, c[0] ^ j)),
                       pl.BlockSpec((tm, D), lambda j, i, c: (jnp.where(j == 0, i, n_i - 1), 0)),
                       pl.BlockSpec((1, 3 * D), lambda j, i, c: (0, 0)),
                       any_spec, any_spec, any_spec, any_spec],
            scratch_shapes=[
                pltpu.VMEM((tn, D), BF16),
                pltpu.VMEM((t, D), BF16),
                pltpu.VMEM((N_DEV,) + c_rows.shape, F32),
                pltpu.VMEM((N_DEV, n_ada), F32),
                pltpu.VMEM((N_DEV, N_DEV, n_ada), F32),
                pltpu.VMEM((1, 3 * D), F32),
                pltpu.SemaphoreType.DMA((N_DEV - 1,)),
                pltpu.SemaphoreType.DMA((N_DEV - 1,)),
                pltpu.SemaphoreType.DMA((2,)),
                pltpu.SemaphoreType.DMA,
                pltpu.SemaphoreType.DMA((N_DEV - 1,)),
                pltpu.SemaphoreType.DMA((N_DEV - 1,)),
                pltpu.SemaphoreType.DMA,
                pltpu.SemaphoreType.DMA((N_DEV - 1,)),
                pltpu.SemaphoreType.DMA((N_DEV - 1,)),
                pltpu.SemaphoreType.DMA,
                pltpu.SemaphoreType.DMA((N_DEV - 1,)),
                pltpu.SemaphoreType.DMA((N_DEV - 1,)),
                pltpu.SemaphoreType.DMA,
                pltpu.SemaphoreType.DMA((N_DEV - 1,)),
                pltpu.SemaphoreType.DMA((N_DEV - 1,)),
            ],
        ),
        out_shape=[jax.ShapeDtypeStruct((t, IN_W), F32), jax.ShapeDtypeStruct((t, D), BF16),
                   jax.ShapeDtypeStruct((1, 3 * D), F32),
                   jax.ShapeDtypeStruct((N_DEV,) + w_shard.shape, w_shard.dtype),
                   jax.ShapeDtypeStruct((N_DEV,) + pack.shape, pack.dtype),
                   jax.ShapeDtypeStruct((N_DEV,) + c_rows.shape, F32),
                   jax.ShapeDtypeStruct((N_DEV,) + conv_rows.shape, F32)],
        compiler_params=_params(2),
    )(where, x, w_ada, b_cols, norm_g, *tables, w_shard, pack, c_rows, conv_rows)


def _rope_tables(positions):
    half = ROT_DIM // 2
    inv_freq = ROPE_THETA ** (-jnp.arange(0, ROT_DIM, 2, dtype=F32) / ROT_DIM)
    pos = positions.astype(F32)
    lane = np.arange(128) % HEAD_DIM
    freq_lane = jnp.where(lane < ROT_DIM, jnp.tile(inv_freq, 128 // half), 0.0)
    ang = pos[:, None] * freq_lane[None, :]
    sin = jnp.sin(ang)
    lane_tables = (jnp.cos(ang), jnp.where(lane < half, -sin, 0.0), jnp.where((lane >= half) & (lane < ROT_DIM), sin, 0.0))
    ang_t = inv_freq[:, None] * pos[None, :]
    return lane_tables, (jnp.cos(ang_t), jnp.sin(ang_t))


def _rope(tv, cos, s1, s2):
    cols = []
    for g in range(tv.shape[1] // 128):
        blk = tv[:, 128 * g : 128 * (g + 1)]
        cols.append(blk * cos + pltpu.roll(blk, 128 - 8, 1) * s1 + pltpu.roll(blk, 8, 1) * s2)
    return cols[0] if len(cols) == 1 else jnp.concatenate(cols, axis=1)


def _rope_transposed(gv, cos, s1, s2):
    cols = []
    for g in range(gv.shape[1] // 128):
        blk = gv[:, 128 * g : 128 * (g + 1)]
        cols.append(blk * cos + pltpu.roll(blk * s1, 8, 1) + pltpu.roll(blk * s2, 128 - 8, 1))
    return cols[0] if len(cols) == 1 else jnp.concatenate(cols, axis=1)


def _lane_lo():
    return lax.broadcasted_iota(jnp.int32, (1, 128), 1) < HEAD_DIM


def _row_lo():
    return lax.broadcasted_iota(jnp.int32, (128, 1), 0) < HEAD_DIM


def _split_heads(tile, j):
    lo = _lane_lo()
    parts = []
    for g in (2 * j, 2 * j + 1):
        blk = tile[:, 128 * g : 128 * (g + 1)]
        parts += [jnp.where(lo, blk, 0.0), jnp.where(lo, 0.0, blk)]
    return jnp.concatenate(parts, axis=0)


def _dup_lanes(kv, j):
    rolled = pltpu.roll(kv, HEAD_DIM, 1)
    lo = _lane_lo()
    return jnp.where(lo, kv, rolled) if j == 0 else jnp.where(lo, rolled, kv)


def _fold_lanes(acc):
    return acc + pltpu.roll(acc, HEAD_DIM, 1)


def _fill_bias_t(bias_ref):
    kj = lax.broadcasted_iota(jnp.int32, (256, 128), 0)
    qi = lax.broadcasted_iota(jnp.int32, (256, 128), 1)
    diff = qi + WINDOW - kj
    band = (diff >= 0) & (diff < WINDOW)
    bias_ref[0] = jnp.where(band, 0.0, -1e30)
    bias_ref[1] = jnp.where(band & (kj >= 128), 0.0, -1e30)


def _sink_row(sink_ref, pair, j):
    return jnp.concatenate([jnp.full((1, 128), sink_ref[0, 8 * pair + 4 * j + h], F32) for h in range(4)], axis=1)


def _softmax_t(q4, kw, bias4, sink):
    s = _dot_nt(kw, q4) + bias4
    m = jnp.maximum(jnp.max(s, axis=0, keepdims=True), sink)
    p = jnp.exp(s - m)
    e_sink = jnp.exp(sink - m)
    denom = jnp.sum(p, axis=0, keepdims=True) + e_sink
    return p, e_sink, denom


def _attn_specs(t, tq):
    return dict(
        q=pl.BlockSpec((tq, 512), lambda p, i: (i, p)),
        k=pl.BlockSpec((t, 128), lambda p, i: (0, SEG_K // 128 + p)),
        v=pl.BlockSpec((t, 128), lambda p, i: (0, SEG_V // 128 + p)),
        sink=pl.BlockSpec(memory_space=pltpu.SMEM),
        act=pl.BlockSpec((tq, 512), lambda p, i: (i, p)),
    )


def _attn_fwd(proj, sinks):
    t = proj.shape[0]
    tq = min(512, t)
    nb = tq // 128
    n_tiles = t // tq
    sp = _attn_specs(t, tq)

    def body(q_ref, k_ref, v_ref, sink_ref, o_ref, kd, vdt, bias_ref):
        pair = pl.program_id(0)
        i = pl.program_id(1)

        @pl.when(i == 0)
        def _():
            kv, vv = k_ref[...], v_ref[...]
            for j in range(2):
                kd[j, 0:128, :] = jnp.zeros((128, 128), BF16)
                kd[j, 128:, :] = _dup_lanes(kv, j).astype(BF16)
                vdt[j, :, 0:128] = jnp.zeros((128, 128), BF16)
                vdt[j, :, 128:] = _dup_lanes(vv, j).T.astype(BF16)
            _fill_bias_t(bias_ref)

        row_lo = _row_lo()

        def block(b, carry):
            r0 = pl.multiple_of(b * 128, 128)
            gb = i * nb + b
            keys = pl.ds(pl.multiple_of(gb * 128, 128), 256)
            rows = pl.ds(r0, 128)
            qt = q_ref[rows, :] * (1.0 / math.sqrt(HEAD_DIM))
            bias = bias_ref[jnp.where(gb == 0, 1, 0)]
            bias4 = jnp.concatenate([bias] * 4, axis=1)
            cols = []
            for j in range(2):
                q4 = _split_heads(qt, j).astype(BF16)
                p, _, denom = _softmax_t(q4, kd[j, keys, :], bias4, _sink_row(sink_ref, pair, j))
                ot = _dot(vdt[j, :, keys], p.astype(BF16)) * (1.0 / denom)
                for gi in range(2):
                    z = jnp.where(row_lo, ot[:, 256 * gi : 256 * gi + 128], ot[:, 256 * gi + 128 : 256 * gi + 256])
                    cols.append(z.T)
            o_ref[rows, :] = jnp.concatenate(cols, axis=1)
            return carry

        lax.fori_loop(0, nb, block, 0, unroll=True)

    return pl.pallas_call(
        body,
        name="attn_fwd",
        grid=(2, n_tiles),
        in_specs=[sp["q"], sp["k"], sp["v"], sp["sink"]],
        out_specs=sp["act"],
        out_shape=jax.ShapeDtypeStruct((t, D), F32),
        scratch_shapes=[pltpu.VMEM((2, t + 128, 128), BF16), pltpu.VMEM((2, 128, t + 128), BF16), pltpu.VMEM((2, 256, 128), F32)],
        compiler_params=_params(2),
    )(proj, proj, proj, sinks)


def _attn_bwd(proj, o, d_o, tables, sinks, part_3):
    t = proj.shape[0]
    tq = min(512, t)
    nb = tq // 128
    n_tiles = t // tq
    (cos, s1, s2), (cos_t, sin_t) = tables
    sp = _attn_specs(t, tq)
    scale = 1.0 / math.sqrt(HEAD_DIM)
    half = ROT_DIM // 2

    def unrope_rows(z, c, s):
        out = []
        for base in (0, HEAD_DIM):
            g0, g1 = z[base : base + half], z[base + half : base + ROT_DIM]
            out += [g0 * c + g1 * s, g1 * c - g0 * s, z[base + ROT_DIM : base + HEAD_DIM]]
        return jnp.concatenate(out, axis=0)

    def swap(part_hbm, recv_hbm, send_sem, recv_sem):
        sibling = _flip(_position(), 1)
        return pltpu.make_async_remote_copy(
            src_ref=part_hbm.at[:, :, sibling[2]], dst_ref=recv_hbm, send_sem=send_sem, recv_sem=recv_sem,
            device_id=sibling, device_id_type=MESH,
        )

    def body(q_ref, k_ref, v_ref, o_ref, do_ref, ct_ref, st_ref, ck, s1k, s2k, sink_ref, part_hbm,
             dq_ref, dk_ref, dv_ref, dsink_ref, recv_hbm, kd, kdt, vd, bias_ref, dkacc, dvacc, send_sem, recv_sem):
        pair = pl.program_id(0)
        i = pl.program_id(1)

        @pl.when((pair == 0) & (i == 0))
        def _():
            swap(part_hbm, recv_hbm, send_sem, recv_sem).start()

        @pl.when((pair == 1) & (i == n_tiles - 1))
        def _():
            swap(part_hbm, recv_hbm, send_sem, recv_sem).wait()

        @pl.when(i == 0)
        def _():
            kv, vv = k_ref[...], v_ref[...]
            for j in range(2):
                kdup = _dup_lanes(kv, j)
                kd[j, 0:128, :] = jnp.zeros((128, 128), BF16)
                kd[j, 128:, :] = kdup.astype(BF16)
                kdt[j, :, 0:128] = jnp.zeros((128, 128), BF16)
                kdt[j, :, 128:] = kdup.T.astype(BF16)
                vd[j, 0:128, :] = jnp.zeros((128, 128), BF16)
                vd[j, 128:, :] = _dup_lanes(vv, j).astype(BF16)
            _fill_bias_t(bias_ref)
            dkacc[...] = jnp.zeros_like(dkacc)
            dvacc[...] = jnp.zeros_like(dvacc)
            dsink_ref[...] = jnp.zeros_like(dsink_ref)

        row_lo = _row_lo()
        ones = jnp.ones((8, 128), BF16)

        def block(b, carry):
            r0 = pl.multiple_of(b * 128, 128)
            gb = i * nb + b
            keys = pl.ds(pl.multiple_of(gb * 128, 128), 256)
            rows = pl.ds(r0, 128)
            qt = q_ref[rows, :] * scale
            dob = do_ref[rows, :]
            prod = dob * o_ref[rows, :]
            ct, st = ct_ref[:, rows], st_ref[:, rows]
            bias = bias_ref[jnp.where(gb == 0, 1, 0)]
            bias4 = jnp.concatenate([bias] * 4, axis=1)
            cols = []
            for j in range(2):
                q4 = _split_heads(qt, j).astype(BF16)
                kw = kd[j, keys, :]
                p, e_sink, denom = _softmax_t(q4, kw, bias4, _sink_row(sink_ref, pair, j))
                inv = 1.0 / denom
                prob = p * inv
                do4 = _split_heads(dob, j).astype(BF16)
                delta = _dot_nt(ones, _split_heads(prod, j).astype(BF16))[0:1, :]
                dp = _dot_nt(vd[j, keys, :], do4)
                ds = (prob * (dp - delta)).astype(BF16)
                dkacc[j, keys, :] = dkacc[j, keys, :] + _dot(ds, q4)
                dvacc[j, keys, :] = dvacc[j, keys, :] + _dot(prob.astype(BF16), do4)
                dqt = _dot(kdt[j, :, keys], ds) * scale
                dsink = -(e_sink * inv) * delta
                for h in range(4):
                    row = 4 * j + h
                    total = jnp.sum(dsink[:, 128 * h : 128 * (h + 1)], axis=1, keepdims=True)
                    dsink_ref[0, row : row + 1, :] = dsink_ref[0, row : row + 1, :] + jnp.broadcast_to(total, (1, 128))
                for gi in range(2):
                    z = jnp.where(row_lo, dqt[:, 256 * gi : 256 * gi + 128], dqt[:, 256 * gi + 128 : 256 * gi + 256])
                    cols.append(unrope_rows(z, ct, st).T)
            dq_ref[rows, :] = jnp.concatenate(cols, axis=1).astype(BF16)
            return carry

        lax.fori_loop(0, nb, block, 0, unroll=True)

        @pl.when(i == n_tiles - 1)
        def _():
            lo = _lane_lo()
            dk = jnp.where(lo, _fold_lanes(dkacc[0, 128:, :]), _fold_lanes(dkacc[1, 128:, :]))
            dk_ref[...] = _rope_transposed(dk, ck[...], s1k[...], s2k[...]).astype(BF16)
            dv_ref[...] = jnp.where(lo, _fold_lanes(dvacc[0, 128:, :]), _fold_lanes(dvacc[1, 128:, :])).astype(BF16)

    kv_out = pl.BlockSpec((t, 128), lambda p, i: (0, p))
    tab_t = pl.BlockSpec((half, tq), lambda p, i: (0, i))
    tab_k = pl.BlockSpec((t, 128), lambda p, i: (0, 0))
    return pl.pallas_call(
        body,
        name="attn_bwd",
        grid=(2, n_tiles),
        in_specs=[sp["q"], sp["k"], sp["v"], sp["act"], sp["act"], tab_t, tab_t, tab_k, tab_k, tab_k, sp["sink"],
                  pl.BlockSpec(memory_space=pl.ANY)],
        out_specs=[sp["act"], kv_out, kv_out, pl.BlockSpec((1, 8, 128), lambda p, i: (p, 0, 0)), pl.BlockSpec(memory_space=pl.ANY)],
        out_shape=[
            jax.ShapeDtypeStruct((t, D), BF16),
            jax.ShapeDtypeStruct((t, 256), BF16),
            jax.ShapeDtypeStruct((t, 256), BF16),
            jax.ShapeDtypeStruct((2, 8, 128), F32),
            jax.ShapeDtypeStruct((3, 4, ROW_SHARD, D), F32),
        ],
        scratch_shapes=[pltpu.VMEM((2, t + 128, 128), BF16), pltpu.VMEM((2, 128, t + 128), BF16), pltpu.VMEM((2, t + 128, 128), BF16),
                        pltpu.VMEM((2, 256, 128), F32), pltpu.VMEM((2, t + 128, 128), F32), pltpu.VMEM((2, t + 128, 128), F32),
                        pltpu.SemaphoreType.DMA, pltpu.SemaphoreType.DMA],
        compiler_params=_params(2),
    )(proj, proj, proj, o, d_o, cos_t, sin_t, cos, s1, s2, sinks, part_3)


HALF = 512


def _softplus(y):
    u = jnp.exp(-jnp.abs(y))
    w = 1.0 + u
    log1p = jnp.where(w == 1.0, u, jnp.log(w) * (u / jnp.where(w == 1.0, 1.0, w - 1.0)))
    return jnp.maximum(y, 0.0) + log1p


def _gate_block(w_ref, blk):
    return w_ref[:, :, 256 * blk : 256 * (blk + 1)].reshape(256, 256)


def _gate_matmul(act_bf, w_ref):
    return jnp.concatenate([_dot(act_bf[:, 256 * blk : 256 * (blk + 1)], _gate_block(w_ref, blk)) for blk in range(2)], axis=1)


def _gate_matmul_t(grad_bf, w_ref):
    return jnp.concatenate([_dot_nt(grad_bf[:, 256 * blk : 256 * (blk + 1)], _gate_block(w_ref, blk)) for blk in range(2)], axis=1)


def _conv_and_gates(xe_ref, tm, cw_ref, cb_ref, wa_ref, wx_ref, ba_ref, bx_ref, lam_ref, keep):
    xc = cb_ref[...] + sum(cw_ref[k : k + 1, :] * xe_ref[pl.ds(5 + k, tm), :] for k in range(CONV_WIDTH))
    xc_bf = xc.astype(BF16)
    r = _sigmoid(_gate_matmul(xc_bf, wa_ref) + ba_ref[...])
    ig = _sigmoid(_gate_matmul(xc_bf, wx_ref) + bx_ref[...])
    sp = _softplus(-lam_ref[...])
    log_a = -LRU_C * r * sp
    tanh = jnp.tanh(-log_a)
    mult = jnp.sqrt(2.0 * tanh / (1.0 + tanh))
    a = jnp.where(keep, jnp.exp(log_a), 0.0)
    mult = jnp.where(keep, mult, 1.0)
    return xc, xc_bf, r, ig, sp, a, mult


def _scan_rows(a_ref, b_ref, h_ref, carry_ref, tm, reverse):
    width = a_ref.shape[1]
    row = lax.broadcasted_iota(jnp.int32, (8, width), 0)

    def group(g, carry):
        gg = tm // 8 - 1 - g if reverse else g
        r0 = pl.multiple_of(gg * 8, 8)
        a = a_ref[pl.ds(r0, 8), :]
        b = b_ref[pl.ds(r0, 8), :]
        for s in (1, 2, 4):
            shift = 8 - s if reverse else s
            a_sh = pltpu.roll(a, shift, 0)
            b_sh = pltpu.roll(b, shift, 0)
            ok = (row < 8 - s) if reverse else (row >= s)
            b = jnp.where(ok, a * b_sh + b, b)
            a = jnp.where(ok, a * a_sh, a)
        h = a * carry + b
        h_ref[pl.ds(r0, 8), :] = h
        last = h[0:1, :] if reverse else h[7:8, :]
        return jnp.broadcast_to(last, (8, width))

    carry_ref[...] = lax.fori_loop(0, tm // 8, group, carry_ref[...])


def _rnn_specs(t, tm, order):
    n = t // tm
    return dict(
        tile=pl.BlockSpec((tm, HALF), lambda ch, i: (order(i, n), ch)),
        before=lambda col0: pl.BlockSpec((8, HALF), lambda ch, i: (jnp.maximum(order(i, n) * (tm // 8) - 1, 0), col0 + ch)),
        proj_tile=lambda col0: pl.BlockSpec((tm, HALF), lambda ch, i: (order(i, n), col0 + ch)),
        keep=pl.BlockSpec((tm, 1), lambda ch, i: (order(i, n), 0)),
        gates=lambda which: pl.BlockSpec((N_DEV, GATE_SHARD, HALF), lambda ch, i: (0, PACK_GATE // GATE_SHARD + which, ch)),
        gate_grads=pl.BlockSpec((2, 2, 256, 256), lambda ch, i: (0, ch, 0, 0)),
        vec=pl.BlockSpec((1, HALF), lambda ch, i: (0, ch)),
        conv=pl.BlockSpec((CONV_WIDTH, HALF), lambda ch, i: (0, ch)),
    )


def _rnn_fwd(proj, keep, w_all, rg_ba, rg_bx, rg_lambda, conv_w, conv_b):
    t = proj.shape[0]
    tm = min(512, t)
    sp = _rnn_specs(t, tm, lambda i, n: i)

    def body(xr_ref, xb_ref, keep_ref, wa_ref, wx_ref, ba_ref, bx_ref, lam_ref, cw_ref, cb_ref, hh_ref, xe, a_buf, b_buf, carry):
        i = pl.program_id(1)

        @pl.when(i == 0)
        def _():
            carry[...] = jnp.zeros_like(carry)

        xe[0:8, :] = jnp.where(i > 0, xb_ref[...], 0.0)
        xe[8:, :] = xr_ref[...]
        keep_rows = keep_ref[...] != 0.0
        xc, _, _, ig, _, a, mult = _conv_and_gates(xe, tm, cw_ref, cb_ref, wa_ref, wx_ref, ba_ref, bx_ref, lam_ref, keep_rows)
        a_buf[...] = a
        b_buf[...] = mult * (ig * xc)
        _scan_rows(a_buf, b_buf, hh_ref, carry, tm, reverse=False)

    return pl.pallas_call(
        body,
        name="rnn_fwd",
        grid=(2, t // tm),
        in_specs=[sp["proj_tile"](SEG_XR // HALF), sp["before"](SEG_XR // HALF), sp["keep"], sp["gates"](0), sp["gates"](1),
                  sp["vec"], sp["vec"], sp["vec"], sp["conv"], sp["vec"]],
        out_specs=sp["tile"],
        out_shape=jax.ShapeDtypeStruct((t, D), F32),
        scratch_shapes=[pltpu.VMEM((tm + 8, HALF), F32), pltpu.VMEM((tm, HALF), F32), pltpu.VMEM((tm, HALF), F32), pltpu.VMEM((8, HALF), F32)],
        compiler_params=_params(2),
    )(proj, proj, keep, w_all, w_all, rg_ba, rg_bx, rg_lambda, conv_w, conv_b)


def _rnn_bwd(proj, hh, d_hh, keep, w_all, rg_ba, rg_bx, rg_lambda, conv_w, conv_b, sum_3_bf):
    t = proj.shape[0]
    tm = min(512, t)
    n_tiles = t // tm
    sp = _rnn_specs(t, tm, lambda i, n: n - 1 - i)

    def body(xr_ref, xb_ref, hh_ref, hb_ref, dhh_ref, keep_ref, wa_ref, wx_ref, ba_ref, bx_ref, lam_ref, cw_ref, cb_ref,
             s3_hbm, dxr_ref, gw_ref, gvec_ref, l3_hbm, xe, he, ae, c_buf, g_buf, dxe, carry, a_next, send_sems, recv_sems):
        ch = pl.program_id(0)
        i = pl.program_id(1)
        first_tile = i == n_tiles - 1
        views = [lambda ref, peer: ref.at[:, _chip_of(peer)]]

        @pl.when((ch == 0) & (i == 0))
        def _():
            for cp in _chip_copies([s3_hbm], [l3_hbm], views, send_sems, recv_sems):
                cp.start()

        @pl.when((ch == 1) & (i == n_tiles - 1))
        def _():
            for cp in _chip_copies([s3_hbm], [l3_hbm], views, send_sems, recv_sems):
                cp.wait_recv()
                cp.wait_send()

        @pl.when(i == 0)
        def _():
            carry[...] = jnp.zeros_like(carry)
            a_next[...] = jnp.zeros_like(a_next)
            dxe[tm:, :] = jnp.zeros((8, HALF), F32)
            gw_ref[...] = jnp.zeros_like(gw_ref)
            gvec_ref[...] = jnp.zeros_like(gvec_ref)

        xe[0:8, :] = jnp.where(first_tile, 0.0, xb_ref[...])
        xe[8:, :] = xr_ref[...]
        he[0:8, :] = jnp.where(first_tile, 0.0, hb_ref[...])
        he[8:, :] = hh_ref[...]
        keep_rows = keep_ref[...] != 0.0
        xc, xc_bf, r, ig, sp_, a, mult = _conv_and_gates(xe, tm, cw_ref, cb_ref, wa_ref, wx_ref, ba_ref, bx_ref, lam_ref, keep_rows)

        ae[0:tm, :] = a
        ae[tm:, :] = a_next[...]
        a_next[...] = jnp.broadcast_to(a[0:1, :], (8, HALF))
        c_buf[...] = ae[pl.ds(1, tm), :]
        g_buf[...] = dhh_ref[...]
        _scan_rows(c_buf, g_buf, g_buf, carry, tm, reverse=True)
        g = g_buf[...]

        h_prev = he[pl.ds(7, tm), :]
        bb = ig * xc
        d_ig = g * mult * xc
        d_xc = g * mult * ig
        d_log_a = jnp.where(keep_rows, g * h_prev * a - (g * bb) * (a * a) / mult, 0.0)
        d_r = d_log_a * (-LRU_C * sp_)
        d_sp = jnp.sum(d_log_a * (-LRU_C * r), axis=0, keepdims=True)
        d_zr = d_r * r * (1.0 - r)
        d_zi = d_ig * ig * (1.0 - ig)
        d_zr_bf = d_zr.astype(BF16)
        d_zi_bf = d_zi.astype(BF16)
        d_xc = d_xc + _gate_matmul_t(d_zr_bf, wa_ref) + _gate_matmul_t(d_zi_bf, wx_ref)
        for blk in range(2):
            cols = slice(256 * blk, 256 * (blk + 1))
            gw_ref[0, blk] = gw_ref[0, blk] + _dot_tn(xc_bf[:, cols], d_zr_bf[:, cols])
            gw_ref[1, blk] = gw_ref[1, blk] + _dot_tn(xc_bf[:, cols], d_zi_bf[:, cols])

        dxe[0:tm, :] = d_xc
        dxr = sum(cw_ref[k : k + 1, :] * dxe[pl.ds(3 - k, tm), :] for k in range(CONV_WIDTH))
        dxr_ref[...] = dxr.astype(BF16)
        dxe[tm:, :] = d_xc[0:8, :]

        lam = lam_ref[...]
        sums = [
            jnp.sum(d_zr, axis=0, keepdims=True),
            jnp.sum(d_zi, axis=0, keepdims=True),
            d_sp * (-_sigmoid(-lam)),
            jnp.sum(d_xc, axis=0, keepdims=True),
        ] + [jnp.sum(d_xc * xe[pl.ds(5 + k, tm), :], axis=0, keepdims=True) for k in range(CONV_WIDTH)]
        gvec_ref[...] = gvec_ref[...] + jnp.concatenate(sums, axis=0)

    return pl.pallas_call(
        body,
        name="rnn_bwd",
        grid=(2, n_tiles),
        in_specs=[
            sp["proj_tile"](SEG_XR // HALF), sp["before"](SEG_XR // HALF), sp["tile"], sp["before"](0), sp["tile"], sp["keep"],
            sp["gates"](0), sp["gates"](1), sp["vec"], sp["vec"], sp["vec"], sp["conv"], sp["vec"], pl.BlockSpec(memory_space=pl.ANY),
        ],
        out_specs=[sp["tile"], sp["gate_grads"], pl.BlockSpec((8, HALF), lambda ch, i: (0, ch)), pl.BlockSpec(memory_space=pl.ANY)],
        out_shape=[
            jax.ShapeDtypeStruct((t, D), BF16),
            jax.ShapeDtypeStruct((2, 4, 256, 256), F32),
            jax.ShapeDtypeStruct((8, D), F32),
            jax.ShapeDtypeStruct((len(CHIP_FLIPS), 3, ROW_SHARD, D), BF16),
        ],
        scratch_shapes=[
            pltpu.VMEM((tm + 8, HALF), F32),
            pltpu.VMEM((tm + 8, HALF), F32),
            pltpu.VMEM((tm + 8, HALF), F32),
            pltpu.VMEM((tm, HALF), F32),
            pltpu.VMEM((tm, HALF), F32),
            pltpu.VMEM((tm + 8, HALF), F32),
            pltpu.VMEM((8, HALF), F32),
            pltpu.VMEM((8, HALF), F32),
            pltpu.SemaphoreType.DMA((len(CHIP_FLIPS),)),
            pltpu.SemaphoreType.DMA((len(CHIP_FLIPS),)),
        ],
        compiler_params=_params(2),
    )(proj, proj, hh, hh, d_hh, keep, w_all, w_all, rg_ba, rg_bx, rg_lambda, conv_w, conv_b, sum_3_bf)


def _mid(x, target, o, hh, proj, w_all, gate, final_g):
    t = x.shape[0]
    tm = min(256, t)
    n_tiles = t // tm
    full = pl.BlockSpec((tm, D), lambda i: (i, 0))
    half = lambda col: pl.BlockSpec((tm, 512), lambda i: (i, col))
    vec = pl.BlockSpec((1, D), lambda i: (0, 0))
    any_spec = pl.BlockSpec(memory_space=pl.ANY)

    def body(x_ref, tg_ref, o_ref, hh_ref, ga0, ga1, gr0, gr1, ma0, ma1, mr0, mr1, w_hbm, gate_ref, fg_ref,
             dres_ref, do_ref, dhh_ref, dga_ref, dgr_ref, dma_ref, dmr_ref, gw_hbm, vec_ref, w, gw, sem, w_sems):
        i = pl.program_id(0)

        @pl.when(i == 0)
        def _():
            loads = [
                pltpu.make_async_copy(
                    w_hbm.at[s, pl.ds(ROW_SHARD * k, ROW_SHARD), :], w.at[k, pl.ds(ROW_SHARD * s, ROW_SHARD), :], w_sems.at[k * N_DEV + s]
                )
                for k in range(3)
                for s in range(N_DEV)
            ]
            for cp in loads:
                cp.start()
            for cp in loads:
                cp.wait()
            gw[...] = jnp.zeros_like(gw)
            vec_ref[...] = jnp.zeros_like(vec_ref)

        cat = lambda a, b: jnp.concatenate([a[...], b[...]], axis=1)
        ga, gr, ma, mr = cat(ga0, ga1), cat(gr0, gr1), cat(ma0, ma1), cat(mr0, mr1)
        ov, hv, gate_v, fg = o_ref[...], hh_ref[...], gate_ref[...], fg_ref[...]
        sg_a = _sigmoid(ga)
        sg_r = _sigmoid(gr)
        silu_a = ga * sg_a
        silu_r = gr * sg_r
        ya = (ov * silu_a).astype(BF16)
        yr = (hv * silu_r).astype(BF16)
        pa = _dot(ya, w[0])
        pr = _dot(yr, w[1])
        sa = _sigmoid(ma)
        sr = _sigmoid(mr)
        merged = (sa * pa + sr * pr).astype(BF16)
        mo = _dot(merged, w[2])
        out = x_ref[...] + gate_v * mo
        r2 = lax.rsqrt(jnp.mean(out * out, axis=-1, keepdims=True) + NORM_EPS)
        outn = out * r2
        err = outn * fg - tg_ref[...]
        loss = 0.5 * jnp.sum(jnp.mean(err * err, axis=-1, keepdims=True), axis=0, keepdims=True)
        dy = err * (1.0 / D)
        g_final = jnp.sum(dy * outn, axis=0, keepdims=True)
        doutn = dy * fg
        dout = r2 * (doutn - outn * jnp.mean(doutn * outn, axis=-1, keepdims=True))
        dres_ref[...] = dout
        d_gate = jnp.sum(dout * mo, axis=0, keepdims=True)
        dmo = (dout * gate_v).astype(BF16)
        dmerged = _dot_nt(dmo, w[2])
        gw[2] = gw[2] + _dot_tn(merged, dmo)
        dpa = (dmerged * sa).astype(BF16)
        dpr = (dmerged * sr).astype(BF16)
        dma_ref[...] = (dmerged * pa * (sa * (1.0 - sa))).astype(BF16)
        dmr_ref[...] = (dmerged * pr * (sr * (1.0 - sr))).astype(BF16)
        dya = _dot_nt(dpa, w[0])
        dyr = _dot_nt(dpr, w[1])
        gw[0] = gw[0] + _dot_tn(ya, dpa)
        gw[1] = gw[1] + _dot_tn(yr, dpr)
        do_ref[...] = dya * silu_a
        dhh_ref[...] = dyr * silu_r
        dga_ref[...] = (dya * ov * (sg_a * (1.0 + ga * (1.0 - sg_a)))).astype(BF16)
        dgr_ref[...] = (dyr * hv * (sg_r * (1.0 + gr * (1.0 - sg_r)))).astype(BF16)
        zeros = jnp.zeros((5, D), F32)
        vec_ref[...] = vec_ref[...] + jnp.concatenate([g_final, d_gate, jnp.broadcast_to(loss, (1, D)), zeros], axis=0)

        @pl.when(i == n_tiles - 1)
        def _():
            cp = pltpu.make_async_copy(gw, gw_hbm, sem)
            cp.start()
            cp.wait()

    bf_tile = jax.ShapeDtypeStruct((t, D), BF16)
    f_tile = jax.ShapeDtypeStruct((t, D), F32)
    return pl.pallas_call(
        body,
        name="mid",
        grid=(n_tiles,),
        in_specs=[full, full, full, full,
                  half(SEG_GA // 512), half(SEG_GA // 512 + 1), half(SEG_GR // 512), half(SEG_GR // 512 + 1),
                  half(SEG_MA // 512), half(SEG_MA // 512 + 1), half(SEG_MR // 512), half(SEG_MR // 512 + 1),
                  any_spec, vec, vec],
        out_specs=[full, full, full, full, full, full, full, any_spec, pl.BlockSpec((8, D), lambda i: (0, 0))],
        out_shape=[f_tile, f_tile, f_tile, bf_tile, bf_tile, bf_tile, bf_tile,
                   jax.ShapeDtypeStruct((3, D, D), F32), jax.ShapeDtypeStruct((8, D), F32)],
        scratch_shapes=[pltpu.VMEM((3, D, D), BF16), pltpu.VMEM((3, D, D), F32), pltpu.SemaphoreType.DMA, pltpu.SemaphoreType.DMA((3 * N_DEV,))],
        compiler_params=_params(1),
    )(x, target, o, hh, proj, proj, proj, proj, proj, proj, proj, proj, w_all, gate, final_g)


_SEGMENTS = ((SEG_Q, 1024), (SEG_K, 256), (SEG_V, 256), (SEG_GA, 1024), (SEG_XR, 1024), (SEG_GR, 1024), (SEG_MA, 1024), (SEG_MR, 1024))


def _seg_specs(tm):
    return [pl.BlockSpec((tm, width), lambda i: (i, 0)) for _, width in _SEGMENTS]


def _bwd_dh(dsegs, w_all, x, dres, mod, norm_g, sum_w_bf):
    t = x.shape[0]
    tm = min(512, t)
    n_tiles = t // tm
    n_seg = len(_SEGMENTS)
    full = pl.BlockSpec((tm, D), lambda i: (i, 0))
    views = [lambda ref, peer: ref.at[_chip_of(peer)]]

    def body(*refs):
        segs = refs[:n_seg]
        w_hbm, x_ref, dres_ref, mod_ref, ng_ref, sum_hbm, gx_ref, vec_ref, land_hbm, w, sems, send_sems, recv_sems = refs[n_seg:]
        i = pl.program_id(0)

        @pl.when(i == 0)
        def _():
            for cp in _chip_copies([sum_hbm], [land_hbm], views, send_sems, recv_sems):
                cp.start()
            loads = [
                pltpu.make_async_copy(w_hbm.at[s], w.at[pl.ds(W_IN_SHARD * s, W_IN_SHARD), :], sems.at[s])
                for s in range(N_DEV)
            ]
            for cp in loads:
                cp.start()
            for cp in loads:
                cp.wait()
            vec_ref[...] = jnp.zeros_like(vec_ref)

        dh = sum(_dot(seg[...], w[start : start + width, :]) for seg, (start, width) in zip(segs, _SEGMENTS))
        xv = x_ref[...]
        r1 = lax.rsqrt(jnp.mean(xv * xv, axis=-1, keepdims=True) + NORM_EPS)
        xn = xv * r1
        ng = ng_ref[...]
        scale1 = 1.0 + mod_ref[:, D : 2 * D]
        d_shift = jnp.sum(dh, axis=0, keepdims=True)
        d_scale = jnp.sum(dh * (xn * ng), axis=0, keepdims=True)
        g_norm = jnp.sum(dh * xn * scale1, axis=0, keepdims=True)
        dxn = dh * (ng * scale1)
        dx = r1 * (dxn - xn * jnp.mean(dxn * xn, axis=-1, keepdims=True))
        gx_ref[...] = dres_ref[...] + dx
        vec_ref[...] = vec_ref[...] + jnp.concatenate([d_shift, d_scale, g_norm, jnp.zeros((5, D), F32)], axis=0)

        @pl.when(i == n_tiles - 1)
        def _():
            for cp in _chip_copies([sum_hbm], [land_hbm], views, send_sems, recv_sems):
                cp.wait_recv()
                cp.wait_send()

    any_spec = pl.BlockSpec(memory_space=pl.ANY)
    n_copies = len(CHIP_FLIPS)
    return pl.pallas_call(
        body,
        name="bwd_dh",
        grid=(n_tiles,),
        in_specs=_seg_specs(tm) + [any_spec, full, full, pl.BlockSpec((1, 3 * D), lambda i: (0, 0)), pl.BlockSpec((1, D), lambda i: (0, 0)),
                                   any_spec],
        out_specs=[full, pl.BlockSpec((8, D), lambda i: (0, 0)), any_spec],
        out_shape=[jax.ShapeDtypeStruct((t, D), F32), jax.ShapeDtypeStruct((8, D), F32),
                   jax.ShapeDtypeStruct((n_copies, W_IN_SHARD, D), BF16)],
        scratch_shapes=[pltpu.VMEM((IN_W, D), BF16), pltpu.SemaphoreType.DMA((N_DEV,)),
                        pltpu.SemaphoreType.DMA((n_copies,)), pltpu.SemaphoreType.DMA((n_copies,))],
        compiler_params=_params(1),
    )(*dsegs, w_all, x, dres, mod, norm_g, sum_w_bf)


def _bwd_gw(dsegs, h, sum_g_bf):
    t = h.shape[0]
    tm = min(512, t)
    n_tiles = t // tm
    n_seg = len(_SEGMENTS)
    n_chips = N_DEV // 2
    views = [lambda ref, peer: ref.at[:, :, _chip_of(peer)]]

    def body(*refs):
        segs = refs[:n_seg]
        (h_ref, sg_hbm, own_hbm, recv_hbm, lg_hbm, acc, stage, own_sems, pair_send, pair_recv, send_sems, recv_sems) = refs[n_seg:]
        i = pl.program_id(0)

        @pl.when(i == 0)
        def _():
            for cp in _chip_copies([sg_hbm], [lg_hbm], views, send_sems, recv_sems):
                cp.start()
            acc[...] = jnp.zeros_like(acc)

        hv = h_ref[...]
        for seg, (start, width) in zip(segs, _SEGMENTS):
            acc[start : start + width, :] = acc[start : start + width, :] + _dot_tn(seg[...], hv)

        @pl.when(i == n_tiles - 1)
        def _():
            me = _position()
            sibling = _flip(me, 1)

            def rows(q, c):
                return pl.ds(pl.multiple_of(W_IN_SHARD * (2 * q + c), 8), W_IN_SHARD)

            def to_sibling(q):
                return pltpu.make_async_remote_copy(
                    src_ref=stage.at[q % 2], dst_ref=recv_hbm.at[q], send_sem=pair_send.at[q], recv_sem=pair_recv.at[q],
                    device_id=sibling, device_id_type=MESH,
                )

            kept = [pltpu.make_async_copy(acc.at[rows(q, me[2]), :], own_hbm.at[q], own_sems.at[q]) for q in range(n_chips)]
            for cp in kept:
                cp.start()
            for q in range(n_chips):
                if q >= 2:
                    to_sibling(q - 2).wait_send()
                stage[q % 2] = acc[rows(q, sibling[2]), :].astype(BF16)
                to_sibling(q).start()
            for q in range(n_chips - 2, n_chips):
                to_sibling(q).wait_send()
            for q in range(n_chips):
                to_sibling(q).wait_recv()
            for cp in kept:
                cp.wait()
            for cp in _chip_copies([sg_hbm], [lg_hbm], views, send_sems, recv_sems):
                cp.wait_recv()
                cp.wait_send()

    any_spec = pl.BlockSpec(memory_space=pl.ANY)
    n_copies = len(CHIP_FLIPS)
    return pl.pallas_call(
        body,
        name="bwd_gw",
        grid=(n_tiles,),
        in_specs=_seg_specs(tm) + [pl.BlockSpec((tm, D), lambda i: (i, 0)), any_spec],
        out_specs=[any_spec, any_spec, any_spec],
        out_shape=[jax.ShapeDtypeStruct((n_chips, W_IN_SHARD, D), F32),
                   jax.ShapeDtypeStruct((n_chips, W_IN_SHARD, D), BF16),
                   jax.ShapeDtypeStruct((len(CHIP_FLIPS), 2, 4, GATE_SHARD, 256), BF16)],
        scratch_shapes=[pltpu.VMEM((IN_W, D), F32), pltpu.VMEM((2, W_IN_SHARD, D), BF16),
                        pltpu.SemaphoreType.DMA((n_chips,)), pltpu.SemaphoreType.DMA((n_chips,)), pltpu.SemaphoreType.DMA((n_chips,)),
                        pltpu.SemaphoreType.DMA((n_copies,)), pltpu.SemaphoreType.DMA((n_copies,))],
        compiler_params=_params(1),
    )(*dsegs, h, sum_g_bf)


def _adamw(w, g, m, v):
    m = ADAM_B1 * m + (1.0 - ADAM_B1) * g
    v = ADAM_B2 * v + (1.0 - ADAM_B2) * (g * g)
    m_hat = m / (1.0 - ADAM_B1**ADAM_STEP)
    v_hat = v / (1.0 - ADAM_B2**ADAM_STEP)
    delta = -ADAM_LR * (m_hat / (jnp.sqrt(v_hat) + ADAM_EPS) + ADAM_WD * w)
    return delta, m, v


def _w_in_update(own, land, w_t, m_t, v_t):
    n_slots, rows, cols = land.shape
    rb = 208
    spec = pl.BlockSpec((rb, cols), lambda i: (i, 0))

    def body(o_ref, l_ref, w_ref, m_ref, v_ref, g_ref, d_ref, m2_ref, v2_ref):
        total = o_ref[...]
        for s in range(n_slots):
            total = total + l_ref[s].astype(F32)
        g_ref[...] = total
        d_ref[...], m2_ref[...], v2_ref[...] = _adamw(w_ref[...], total, m_ref[...], v_ref[...])

    shape = jax.ShapeDtypeStruct((rows, cols), F32)
    return pl.pallas_call(
        body,
        name="w_in_update",
        grid=(rows // rb,),
        in_specs=[spec, pl.BlockSpec((n_slots, rb, cols), lambda i: (0, i, 0)), spec, spec, spec],
        out_specs=[spec] * 4,
        out_shape=[shape] * 4,
        compiler_params=_params(1),
    )(own, land, w_t, m_t, v_t)


def _shard_update(own_3, land_3, own_g, land_g, weights, m_in, v_in):
    n = len(weights)

    def body(*refs):
        o3_ref, l3_ref, og_ref, lg_ref = refs[:4]
        w_refs, m_refs, v_refs = (refs[4 + n * k : 4 + n * (k + 1)] for k in range(3))
        outs = refs[4 + 3 * n :]
        for k in range(n):
            if k < 3:
                g = o3_ref[k]
                for slot in range(3):
                    g = g + l3_ref[slot, k].astype(F32)
            else:
                g = og_ref[k - 3]
                for slot in range(3):
                    g = g + lg_ref[slot, k - 3].astype(F32)
            d, m2, v2 = _adamw(w_refs[k][0], g, m_refs[k][0], v_refs[k][0])
            for which, val in enumerate((g, d, m2, v2)):
                outs[which * n + k][0] = val

    shapes = [jax.ShapeDtypeStruct(w.shape, F32) for w in weights]
    res = pl.pallas_call(
        body, name="shard_update", out_shape=shapes * 4, compiler_params=pltpu.CompilerParams(vmem_limit_bytes=VMEM_LIMIT)
    )(own_3, land_3, own_g, land_g, *weights, *m_in, *v_in)
    return [res[n * k : n * (k + 1)] for k in range(4)]


VEC_LAYOUT = (("b_ada", 3 * D), ("norm_g", D), ("attn_sinks", 128), ("conv_b", D), ("rg_ba", D), ("rg_bx", D), ("rg_lambda", D),
              ("final_g", D), ("conv_w", CONV_WIDTH * D), ("loss", 128))
VEC_OFFSET = {name: sum(w for _, w in VEC_LAYOUT[:k]) for k, (name, _) in enumerate(VEC_LAYOUT)}
VEC_ROWS = 112
VEC_WIDTH = VEC_ROWS * 128
assert sum(w for _, w in VEC_LAYOUT) <= VEC_WIDTH
N_REPLICATED = 8


def _small_update(vec_all, c_all_t, me, rep_w, rep_m, rep_v, ada, conv):
    n_ada = ada[0].shape[1]
    n_conv = conv[0].shape[1]

    def body(*refs):
        me_ref, vec_ref, ct_ref = refs[:3]
        w_refs, m_refs, v_refs = (refs[3 + N_REPLICATED * k : 3 + N_REPLICATED * (k + 1)] for k in range(3))
        pos = 3 + 3 * N_REPLICATED
        ada_refs, conv_refs = refs[pos : pos + 3], refs[pos + 3 : pos + 6]
        outs = refs[pos + 6 :]
        loss_ref, rep_outs = outs[0], outs[1 : 1 + 4 * N_REPLICATED]
        ada_outs, conv_outs = outs[1 + 4 * N_REPLICATED : 5 + 4 * N_REPLICATED], outs[5 + 4 * N_REPLICATED : 9 + 4 * N_REPLICATED]
        conv_buf = outs[9 + 4 * N_REPLICATED]

        total = vec_ref[0:1, :]
        for slot in range(1, N_DEV):
            total = total + vec_ref[slot : slot + 1, :]
        loss_ref[...] = total[:, VEC_OFFSET["loss"] : VEC_OFFSET["loss"] + 1]
        for k, (name, _) in enumerate(VEC_LAYOUT[:N_REPLICATED]):
            width = w_refs[k].shape[1]
            g = total[:, VEC_OFFSET[name] : VEC_OFFSET[name] + width]
            d, m2, v2 = _adamw(w_refs[k][...], g, m_refs[k][...], v_refs[k][...])
            for which, val in enumerate((g, d, m2, v2)):
                rep_outs[which * N_REPLICATED + k][...] = val

        device = me_ref[0]
        for k in range(CONV_WIDTH):
            conv_buf[k : k + 1, :] = total[:, VEC_OFFSET["conv_w"] + D * k : VEC_OFFSET["conv_w"] + D * (k + 1)]
        g_conv = conv_buf[:, pl.ds(pl.multiple_of(device * n_conv, 128), n_conv)]
        for ref, val in zip(conv_outs, (g_conv,) + _adamw(conv_refs[0][...], g_conv, conv_refs[1][...], conv_refs[2][...])):
            ref[...] = val

        dmod = vec_ref[:, pl.ds(pl.multiple_of(device * n_ada, 128), n_ada)]
        g_ada = _dot(ct_ref[...], dmod)
        for ref, val in zip(ada_outs, (g_ada,) + _adamw(ada_refs[0][...], g_ada, ada_refs[1][...], ada_refs[2][...])):
            ref[...] = val

    vmem = pl.BlockSpec(memory_space=pltpu.VMEM)
    rep_shapes = [jax.ShapeDtypeStruct(w.shape, F32) for w in rep_w]
    out_shape = ([jax.ShapeDtypeStruct((1, 1), F32)] + rep_shapes * 4 + [jax.ShapeDtypeStruct(ada[0].shape, F32)] * 4
                 + [jax.ShapeDtypeStruct(conv[0].shape, F32)] * 4)
    n_in = 3 + 3 * N_REPLICATED + 6
    res = pl.pallas_call(
        body,
        name="small_update",
        in_specs=[pl.BlockSpec(memory_space=pltpu.SMEM)] + [vmem] * (n_in - 1),
        out_specs=[vmem] * len(out_shape),
        out_shape=out_shape,
        scratch_shapes=[pltpu.VMEM((CONV_WIDTH, D), F32)],
        compiler_params=pltpu.CompilerParams(vmem_limit_bytes=VMEM_LIMIT),
    )(me, vec_all, c_all_t, *rep_w, *rep_m, *rep_v, *ada, *conv)
    loss = res[0]
    reps = [res[1 + N_REPLICATED * k : 1 + N_REPLICATED * (k + 1)] for k in range(4)]
    return loss, reps, res[1 + 4 * N_REPLICATED : 5 + 4 * N_REPLICATED], res[5 + 4 * N_REPLICATED : 9 + 4 * N_REPLICATED]


def kernel(x, c, positions, w_ada, b_ada, norm_g, w_in, attn_sinks, conv_w, conv_b, rg_wa, rg_ba, rg_wx, rg_bx, rg_lambda, w_attn_proj, w_rnn_proj, w_out, final_g, loss_target, m_w_ada, m_b_ada, m_norm_g, m_w_in, m_attn_sinks, m_conv_w, m_conv_b, m_rg_wa, m_rg_ba, m_rg_wx, m_rg_bx, m_rg_lambda, m_w_attn_proj, m_w_rnn_proj, m_w_out, m_final_g, v_w_ada, v_b_ada, v_norm_g, v_w_in, v_attn_sinks, v_conv_w, v_conv_b, v_rg_wa, v_rg_ba, v_rg_wx, v_rg_bx, v_rg_lambda, v_w_attn_proj, v_w_rnn_proj, v_w_out, v_final_g):
    my_x, my_y, my_c = _position()
    me = _slot((my_x, my_y, my_c))
    x2, target = x[0], loss_target[0]
    pos = positions[0]

    gate_rows = lambda w: w[0].transpose(1, 0, 2).reshape(GATE_SHARD, D)
    w_pack = jnp.concatenate([w_attn_proj[0], w_rnn_proj[0], w_out[0], gate_rows(rg_wa), gate_rows(rg_wx)], axis=0).astype(BF16)
    conv_rows = jnp.pad(conv_w[0], ((0, 8 - CONV_WIDTH), (0, 0)))
    w_in_t = w_in[0].T
    n_ada = w_ada.shape[2]
    b_cols = lax.dynamic_slice(b_ada, (0, me * n_ada), (1, n_ada))

    tables = _rope_tables(pos)
    where = jnp.stack([_chip_of((my_x, my_y)), me]).astype(jnp.int32)
    proj, h, mod, w_t_all, w_all, c_rows, conv_all = _fwd_in(
        x2, jnp.broadcast_to(c, (8, D)), conv_rows, w_ada[0], b_cols, norm_g, w_in_t.astype(BF16), tables[0], w_pack, where
    )
    c_all = c_rows[:, 0, :]
    conv_full = conv_all[:, :CONV_WIDTH, :].transpose(1, 0, 2).reshape(CONV_WIDTH, D)
    gate = mod[:, 2 * D :]
    o = _attn_fwd(proj, attn_sinks)
    keep = (pos != 0).astype(F32)[:, None]
    hh = _rnn_fwd(proj, keep, w_all, rg_ba, rg_bx, rg_lambda, conv_full, conv_b)

    dres, d_o, d_hh, d_ga, d_gr, d_ma, d_mr, gw3, mid_vec = _mid(x2, target, o, hh, proj, w_all, gate, final_g[None, :])
    index = jnp.stack([my_c, _chip_of((my_x, my_y))]).astype(jnp.int32)
    part_3 = gw3.reshape(3, 4, 2, ROW_SHARD, D)
    d_q, d_k, d_v, d_sink, recv_3 = _attn_bwd(proj, o, d_o, tables, attn_sinks, part_3)
    sum_3_bf, own_3 = _pair_add(
        "pair_add_3", part_3, recv_3, index, (3, 4), 1,
        pl.BlockSpec((None, None, None, ROW_SHARD, D), lambda a, q, idx: (a, q, idx[0], 0, 0)),
        pl.BlockSpec((None, None, ROW_SHARD, D), lambda a, q, idx: (a, q, 0, 0)),
        pl.BlockSpec((None, ROW_SHARD, D), lambda a, q, idx: (a, 0, 0)), (3, ROW_SHARD, D),
    )
    d_xr, g_gates, rnn_vec, land_3 = _rnn_bwd(proj, hh, d_hh, keep, w_all, rg_ba, rg_bx, rg_lambda, conv_full, conv_b, sum_3_bf)
    dsegs = (d_q, d_k, d_v, d_ga, d_xr, d_gr, d_ma, d_mr)
    part_g = g_gates.reshape(2, 4, 4, 2, GATE_SHARD, 256)
    (recv_g,) = _exchange("rs_pair_g", [part_g], [lambda ref, peer: ref.at[:, :, :, peer[2]]], [(2, 4, 4, GATE_SHARD, 256)], flips=(1,))
    sum_g_bf, own_g = _pair_add(
        "pair_add_g", part_g, recv_g[0], index, (4,), 0,
        pl.BlockSpec((2, 4, None, None, GATE_SHARD, 256), lambda q, idx: (0, 0, q, idx[0], 0, 0)),
        pl.BlockSpec((2, 4, None, GATE_SHARD, 256), lambda q, idx: (0, 0, q, 0, 0)),
        pl.BlockSpec((2, 4, GATE_SHARD, 256), lambda q, idx: (0, 0, 0, 0)), (2, 4, GATE_SHARD, 256),
    )
    part_w, recv_w, land_g = _bwd_gw(dsegs, h, sum_g_bf)
    rb = 208
    w_rows = pl.BlockSpec((None, rb, D), lambda r, q, idx: (q, r, 0))
    sum_w_bf, own_w = _pair_add(
        "pair_add_w", part_w, recv_w, index, (W_IN_SHARD // rb, 4), 1, w_rows, w_rows,
        pl.BlockSpec((rb, D), lambda r, q, idx: (r, 0)), (W_IN_SHARD, D),
    )
    grad_x, dh_vec, land_w = _bwd_dh(dsegs, w_t_all, x2, dres, mod, norm_g, sum_w_bf)

    pieces = dict(
        b_ada=jnp.concatenate([dh_vec[0:1], dh_vec[1:2], mid_vec[1:2]], axis=1),
        norm_g=dh_vec[2:3],
        attn_sinks=jnp.pad(d_sink[:, :, 0].reshape(1, 16), ((0, 0), (0, 112))),
        conv_b=rnn_vec[3:4],
        rg_ba=rnn_vec[0:1],
        rg_bx=rnn_vec[1:2],
        rg_lambda=rnn_vec[2:3],
        final_g=mid_vec[0:1],
        conv_w=rnn_vec[4:8].reshape(1, CONV_WIDTH * D),
        loss=mid_vec[2:3, 0:128],
    )
    vec = jnp.concatenate([pieces[name] for name, _ in VEC_LAYOUT], axis=1)
    vec = jnp.pad(vec, ((0, 0), (0, VEC_WIDTH - vec.shape[1]))).reshape(VEC_ROWS, 128)
    (vec_all,) = _all_gather_direct("ag_vec", [vec])
    rep_w = [b_ada, norm_g, attn_sinks, conv_b, rg_ba, rg_bx, rg_lambda, final_g[None, :]]
    rep_m = [m_b_ada, m_norm_g, m_attn_sinks, m_conv_b, m_rg_ba, m_rg_bx, m_rg_lambda, m_final_g[None, :]]
    rep_v = [v_b_ada, v_norm_g, v_attn_sinks, v_conv_b, v_rg_ba, v_rg_bx, v_rg_lambda, v_final_g[None, :]]
    loss, reps, ada_res, conv_res = _small_update(
        vec_all.reshape(N_DEV, VEC_WIDTH), c_all.T, me.astype(jnp.int32).reshape(1), rep_w, rep_m, rep_v,
        (w_ada[0], m_w_ada[0], v_w_ada[0]), (conv_w[0], m_conv_w[0], v_conv_w[0]),
    )

    w_in_res = _w_in_update(own_w, land_w, w_in_t, m_w_in[0].T, v_w_in[0].T)
    shard_res = _shard_update(
        own_3, land_3, own_g, land_g,
        (w_attn_proj, w_rnn_proj, w_out, rg_wa, rg_wx),
        (m_w_attn_proj, m_w_rnn_proj, m_w_out, m_rg_wa, m_rg_wx),
        (v_w_attn_proj, v_w_rnn_proj, v_w_out, v_rg_wa, v_rg_wx),
    )

    def by_weight(k):
        b, n, s, cb, ba, bx, lam, fg = reps[k]
        ap, rp, out, wa, wx = shard_res[k]
        return [ada_res[k][None], b, n, w_in_res[k].T[None], s, conv_res[k][None], cb, wa, ba, wx, bx, lam, ap, rp, out, fg[0]]

    return (loss[0, 0], grad_x[None], *by_weight(0), *by_weight(1), *by_weight(2), *by_weight(3))
```

```python
import functools
import math

import jax
import jax.numpy as jnp
import numpy as np
from jax import lax
from jax.experimental import pallas as pl
from jax.experimental.pallas import tpu as pltpu

F32 = jnp.float32
BF16 = jnp.bfloat16

D = 1024
HEAD_DIM = 64
ROT_DIM = 16
ROPE_THETA = 500000.0
WINDOW = 128
LRU_C = 8.0
CONV_WIDTH = 4
NORM_EPS = 1e-6
IN_W = 6656
SEG_Q, SEG_K, SEG_V, SEG_GA, SEG_XR, SEG_GR, SEG_MA, SEG_MR = 0, 1024, 1280, 1536, 2560, 3584, 4608, 5632
N_DEV = 8
W_IN_SHARD = IN_W // N_DEV
ROW_SHARD = D // N_DEV
GATE_SHARD = 256 // N_DEV

ADAM_LR = 0.001
ADAM_B1 = 0.9
ADAM_B2 = 0.999
ADAM_EPS = 1e-08
ADAM_WD = 0.01
ADAM_STEP = 10

VMEM_LIMIT = 60 * 1024 * 1024
MESH = pl.DeviceIdType.MESH
AXES = ("x", "y", "c")


def _dot(a, b):
    return jnp.dot(a, b, preferred_element_type=F32)


def _dot_nt(a, b):
    return lax.dot_general(a, b, (((1,), (1,)), ((), ())), preferred_element_type=F32)


def _dot_tn(a, b):
    return lax.dot_general(a, b, (((0,), (0,)), ((), ())), preferred_element_type=F32)


def _sigmoid(z):
    return 1.0 / (1.0 + jnp.exp(-z))


def _params(n_grid, vmem=VMEM_LIMIT):
    return pltpu.CompilerParams(dimension_semantics=("arbitrary",) * n_grid, vmem_limit_bytes=vmem)


def _position():
    return lax.axis_index("x"), lax.axis_index("y"), lax.axis_index("c")


def _flip(pos, k):
    x, y, c = pos
    return (1 - x if k & 4 else x, 1 - y if k & 2 else y, 1 - c if k & 1 else c)


def _slot(pos):
    return 4 * pos[0] + 2 * pos[1] + pos[2]


def _all_gather_direct(name, arrs):
    n = len(arrs)

    def body(*refs):
        ins, outs = refs[:n], refs[n : 2 * n]
        send_sems, recv_sems, local_sems = refs[2 * n :]
        me = _position()
        mine = [pltpu.make_async_copy(ins[a], outs[a].at[_slot(me)], local_sems.at[a]) for a in range(n)]
        for cp in mine:
            cp.start()

        def copy(a, k, block):
            return pltpu.make_async_remote_copy(
                src_ref=ins[a],
                dst_ref=outs[a].at[_slot(block)],
                send_sem=send_sems.at[a * 7 + k - 1],
                recv_sem=recv_sems.at[a * 7 + k - 1],
                device_id=_flip(me, k),
                device_id_type=MESH,
            )

        sent = [copy(a, k, me) for a in range(n) for k in range(1, N_DEV)]
        for cp in sent:
            cp.start()
        for a in range(n):
            for k in range(1, N_DEV):
                copy(a, k, _flip(me, k)).wait_recv()
        for cp in sent:
            cp.wait_send()
        for cp in mine:
            cp.wait()

    any_spec = pl.BlockSpec(memory_space=pl.ANY)
    return pl.pallas_call(
        body,
        name=name,
        out_shape=[jax.ShapeDtypeStruct((N_DEV,) + a.shape, a.dtype) for a in arrs],
        in_specs=[any_spec] * n,
        out_specs=[any_spec] * n,
        scratch_shapes=[
            pltpu.SemaphoreType.DMA((7 * n,)),
            pltpu.SemaphoreType.DMA((7 * n,)),
            pltpu.SemaphoreType.DMA((n,)),
        ],
    )(*arrs)


def _direct_gather_copies(src_hbm, all_hbm, send_sems, recv_sems, arriving):
    me = _position()
    return [
        pltpu.make_async_remote_copy(
            src_ref=src_hbm,
            dst_ref=all_hbm.at[_slot(_flip(me, k) if arriving else me)],
            send_sem=send_sems.at[k - 1],
            recv_sem=recv_sems.at[k - 1],
            device_id=_flip(me, k),
            device_id_type=MESH,
        )
        for k in range(1, N_DEV)
    ]


CHIP_FLIPS = (2, 4, 6)


def _chip_of(pos):
    return 2 * pos[0] + pos[1]


def _chip_copies(srcs, lands, views, send_sems, recv_sems):
    me = _position()
    copies = []
    for a, (src, land) in enumerate(zip(srcs, lands)):
        for j, k in enumerate(CHIP_FLIPS):
            peer = _flip(me, k)
            copies.append(
                pltpu.make_async_remote_copy(
                    src_ref=views[a](src, peer),
                    dst_ref=land.at[j],
                    send_sem=send_sems.at[len(CHIP_FLIPS) * a + j],
                    recv_sem=recv_sems.at[len(CHIP_FLIPS) * a + j],
                    device_id=peer,
                    device_id_type=MESH,
                )
            )
    return copies


def _exchange(name, arrs, views, out_shapes, flips):
    n = len(arrs)
    nf = len(flips)

    def body(*refs):
        ins, outs = refs[:n], refs[n : 2 * n]
        send_sems, recv_sems = refs[2 * n :]
        me = _position()

        def copy(a, j):
            peer = _flip(me, flips[j])
            return pltpu.make_async_remote_copy(
                src_ref=views[a](ins[a], peer),
                dst_ref=outs[a].at[j],
                send_sem=send_sems.at[a * nf + j],
                recv_sem=recv_sems.at[a * nf + j],
                device_id=peer,
                device_id_type=MESH,
            )

        sent = [copy(a, j) for a in range(n) for j in range(nf)]
        for cp in sent:
            cp.start()
        for cp in sent:
            cp.wait_recv()
        for cp in sent:
            cp.wait_send()

    any_spec = pl.BlockSpec(memory_space=pl.ANY)
    return pl.pallas_call(
        body,
        name=name,
        out_shape=[jax.ShapeDtypeStruct((nf,) + tuple(sh), a.dtype) for sh, a in zip(out_shapes, arrs)],
        in_specs=[any_spec] * n,
        out_specs=[any_spec] * n,
        scratch_shapes=[pltpu.SemaphoreType.DMA((nf * n,)), pltpu.SemaphoreType.DMA((nf * n,))],
    )(*arrs)


def _pair_add(name, own, recv, index, grid, chip_axis, own_spec, recv_spec, mine_spec, mine_shape):
    def body(idx_ref, own_ref, recv_ref, sb_ref, mine_ref):
        total = own_ref[...] + recv_ref[...].astype(F32)
        sb_ref[...] = total.astype(BF16)

        @pl.when(pl.program_id(chip_axis) == idx_ref[1])
        def _():
            mine_ref[...] = total

    return pl.pallas_call(
        body,
        name=name,
        grid_spec=pltpu.PrefetchScalarGridSpec(
            num_scalar_prefetch=1, grid=grid, in_specs=[own_spec, recv_spec], out_specs=[recv_spec, mine_spec]
        ),
        out_shape=[jax.ShapeDtypeStruct(recv.shape, BF16), jax.ShapeDtypeStruct(mine_shape, F32)],
        compiler_params=_params(len(grid)),
    )(index, own, recv)


PACK_GATE = 3 * ROW_SHARD
PACK_ROWS = PACK_GATE + 2 * GATE_SHARD


def _fwd_in(x, c_rows, conv_rows, w_ada, b_cols, norm_g, w_shard, tables, pack, where):
    t = x.shape[0]
    tm = min(1024, t)
    tn = IN_W // 4
    n_i, n_j = t // tm, IN_W // tn
    n_rope = (SEG_V - SEG_Q) // 128

    n_ada = w_ada.shape[1]

    def body(where_ref, x_ref, wada_ref, bcols_ref, ng_ref, cos_ref, s1_ref, s2_ref, shard_hbm, pack_hbm, c_hbm, conv_hbm,
             proj_ref, h_ref, mod_ref, wt_hbm, all_hbm, call_hbm, convall_hbm,
             wbuf, hbuf, cbuf, mcols, mod_all, mod_buf,
             w_send, w_recv, load_sems, local_sem, p_send, p_recv, p_local,
             c_send, c_recv, c_local, v_send, v_recv, v_local, m_send, m_recv):
        j = pl.program_id(0)
        i = pl.program_id(1)
        me = _position()
        sibling = _flip(me, 1)
        chip = where_ref[0]
        slot = where_ref[1]

        def modulation():
            own_c = pltpu.make_async_copy(c_hbm, call_hbm.at[_slot(me)], c_local)
            own_v = pltpu.make_async_copy(conv_hbm, convall_hbm.at[_slot(me)], v_local)
            own_c.start()
            own_v.start()
            for cp in _direct_gather_copies(c_hbm, call_hbm, c_send, c_recv, arriving=False):
                cp.start()
            for cp in _direct_gather_copies(conv_hbm, convall_hbm, v_send, v_recv, arriving=False):
                cp.start()
            for cp in _direct_gather_copies(c_hbm, call_hbm, c_send, c_recv, arriving=True):
                cp.wait_recv()
            own_c.wait()
            fetch = pltpu.make_async_copy(call_hbm, cbuf, c_local)
            fetch.start()
            fetch.wait()
            mcols[...] = _dot(cbuf[:, 0, :].astype(BF16), wada_ref[...].astype(BF16)) + bcols_ref[...]
            mod_all[slot] = mcols[...]
            swaps = lambda arriving: [
                pltpu.make_async_remote_copy(
                    src_ref=mcols,
                    dst_ref=mod_all.at[_slot(_flip(me, k) if arriving else me)],
                    send_sem=m_send.at[k - 1],
                    recv_sem=m_recv.at[k - 1],
                    device_id=_flip(me, k),
                    device_id_type=MESH,
                )
                for k in range(1, N_DEV)
            ]
            for cp in swaps(False):
                cp.start()
            for cp in swaps(True):
                cp.wait_recv()
            mine = lax.broadcasted_iota(jnp.int32, (N_DEV, n_ada), 0) == slot
            for s in range(N_DEV):
                row = jnp.sum(jnp.where(mine, mod_all[s], 0.0), axis=0, keepdims=True)
                mod_buf[:, n_ada * s : n_ada * (s + 1)] = row
            for cp in swaps(False):
                cp.wait_send()
            for cp in _direct_gather_copies(c_hbm, call_hbm, c_send, c_recv, arriving=False):
                cp.wait_send()
            for cp in _direct_gather_copies(conv_hbm, convall_hbm, v_send, v_recv, arriving=True):
                cp.wait_recv()
            for cp in _direct_gather_copies(conv_hbm, convall_hbm, v_send, v_recv, arriving=False):
                cp.wait_send()
            own_v.wait()

        def shard_copy(k, block, to, src=None):
            return pltpu.make_async_remote_copy(
                src_ref=wt_hbm.at[_slot(block)] if src is None else src,
                dst_ref=wt_hbm.at[_slot(block)],
                send_sem=w_send.at[k],
                recv_sem=w_recv.at[k],
                device_id=to,
                device_id_type=MESH,
            )

        own_shard = pltpu.make_async_copy(shard_hbm, wt_hbm.at[_slot(me)], local_sem)
        own_pack = pltpu.make_async_copy(pack_hbm, all_hbm.at[_slot(me)], p_local)

        def load_tile(q):
            loads = [pltpu.make_async_copy(wt_hbm.at[2 * q + c], wbuf.at[pl.ds(W_IN_SHARD * c, W_IN_SHARD), :], load_sems.at[c]) for c in range(2)]
            for cp in loads:
                cp.start()
            for cp in loads:
                cp.wait()

        @pl.when((j == 0) & (i == 0))
        def _():
            modulation()
            mod_ref[...] = mod_buf[...]
            own_shard.start()
            shard_copy(0, me, sibling, src=shard_hbm).start()
            for rel in range(1, 4):
                shard_copy(rel, me, _flip(me, 2 * rel), src=shard_hbm).start()
            own_pack.start()
            for cp in _direct_gather_copies(pack_hbm, all_hbm, p_send, p_recv, arriving=False):
                cp.start()
            own_shard.wait()
            shard_copy(0, sibling, me).wait_recv()
            load_tile(chip)

        for rel in range(1, 4):

            @pl.when((j == rel) & (i == 0))
            def _(rel=rel):
                there = _flip(me, 2 * rel)
                shard_copy(rel, there, me).wait_recv()
                shard_copy(3 + rel, there, sibling).start()
                shard_copy(3 + rel, _flip(sibling, 2 * rel), me).wait_recv()
                load_tile(chip ^ rel)

        rows = pl.ds(pl.multiple_of(i * tm, tm), tm)

        @pl.when(j == 0)
        def _():
            xv = x_ref[...]
            r = lax.rsqrt(jnp.mean(xv * xv, axis=-1, keepdims=True) + NORM_EPS)
            shift = mod_buf[:, 0:D]
            scale = mod_buf[:, D : 2 * D]
            hv = (((xv * r) * ng_ref[...]) * (1.0 + scale) + shift).astype(BF16)
            hbuf[rows, :] = hv
            h_ref[...] = hv

        proj_ref[...] = _dot_nt(hbuf[rows, :], wbuf[...])

        @pl.when((chip ^ j) == 0)
        def _():
            cos, s1, s2 = cos_ref[...], s1_ref[...], s2_ref[...]
            for g in range(n_rope):
                cols = slice(128 * g, 128 * (g + 1))
                proj_ref[:, cols] = _rope(proj_ref[:, cols], cos, s1, s2)

        @pl.when((j == n_j - 1) & (i == n_i - 1))
        def _():
            shard_copy(0, me, sibling, src=shard_hbm).wait_send()
            for rel in range(1, 4):
                shard_copy(rel, me, _flip(me, 2 * rel), src=shard_hbm).wait_send()
                shard_copy(3 + rel, _flip(me, 2 * rel), sibling).wait_send()
            for cp in _direct_gather_copies(pack_hbm, all_hbm, p_send, p_recv, arriving=True):
                cp.wait_recv()
            for cp in _direct_gather_copies(pack_hbm, all_hbm, p_send, p_recv, arriving=False):
                cp.wait_send()
            own_pack.wait()

    tab = pl.BlockSpec((tm, 128), lambda j, i, c: (i, 0))
    any_spec = pl.BlockSpec(memory_space=pl.ANY)
    return pl.pallas_call(
        body,
        name="fwd_in",
        grid_spec=pltpu.PrefetchScalarGridSpec(
            num_scalar_prefetch=1,
            grid=(n_j, n_i),
            in_specs=[
                pl.BlockSpec((tm, D), lambda j, i, c: (jnp.where(j == 0, i, 0), 0)),
                pl.BlockSpec(w_ada.shape, lambda j, i, c: (0, 0)),
                pl.BlockSpec((1, n_ada), lambda j, i, c: (0, 0)),
                pl.BlockSpec((1, D), lambda j, i, c: (0, 0)),
                tab, tab, tab, any_spec, any_spec, any_spec, any_spec,
            ],
            out_specs=[pl.BlockSpec((tm, tn), lambda j, i, c: (i, c[0] ^ j)),
                       pl.BlockSpec((tm, D), lambda j, i, c: (jnp.where(j == 0, i, n_i - 1), 0)),
                       pl.BlockSpec((1, 3 * D), lambda j, i, c: (0, 0)),
                       any_spec, any_spec, any_spec, any_spec],
            scratch_shapes=[
                pltpu.VMEM((tn, D), BF16),
                pltpu.VMEM((t, D), BF16),
                pltpu.VMEM((N_DEV,) + c_rows.shape, F32),
                pltpu.VMEM((N_DEV, n_ada), F32),
                pltpu.VMEM((N_DEV, N_DEV, n_ada), F32),
                pltpu.VMEM((1, 3 * D), F32),
                pltpu.SemaphoreType.DMA((N_DEV - 1,)),
                pltpu.SemaphoreType.DMA((N_DEV - 1,)),
                pltpu.SemaphoreType.DMA((2,)),
                pltpu.SemaphoreType.DMA,
                pltpu.SemaphoreType.DMA((N_DEV - 1,)),
                pltpu.SemaphoreType.DMA((N_DEV - 1,)),
                pltpu.SemaphoreType.DMA,
                pltpu.SemaphoreType.DMA((N_DEV - 1,)),
                pltpu.SemaphoreType.DMA((N_DEV - 1,)),
                pltpu.SemaphoreType.DMA,
                pltpu.SemaphoreType.DMA((N_DEV - 1,)),
                pltpu.SemaphoreType.DMA((N_DEV - 1,)),
                pltpu.SemaphoreType.DMA,
                pltpu.SemaphoreType.DMA((N_DEV - 1,)),
                pltpu.SemaphoreType.DMA((N_DEV - 1,)),
            ],
        ),
        out_shape=[jax.ShapeDtypeStruct((t, IN_W), F32), jax.ShapeDtypeStruct((t, D), BF16),
                   jax.ShapeDtypeStruct((1, 3 * D), F32),
                   jax.ShapeDtypeStruct((N_DEV,) + w_shard.shape, w_shard.dtype),
                   jax.ShapeDtypeStruct((N_DEV,) + pack.shape, pack.dtype),
                   jax.ShapeDtypeStruct((N_DEV,) + c_rows.shape, F32),
                   jax.ShapeDtypeStruct((N_DEV,) + conv_rows.shape, F32)],
        compiler_params=_params(2),
    )(where, x, w_ada, b_cols, norm_g, *tables, w_shard, pack, c_rows, conv_rows)


def _rope_tables(positions):
    half = ROT_DIM // 2
    inv_freq = ROPE_THETA ** (-jnp.arange(0, ROT_DIM, 2, dtype=F32) / ROT_DIM)
    pos = positions.astype(F32)
    lane = np.arange(128) % HEAD_DIM
    freq_lane = jnp.where(lane < ROT_DIM, jnp.tile(inv_freq, 128 // half), 0.0)
    ang = pos[:, None] * freq_lane[None, :]
    sin = jnp.sin(ang)
    lane_tables = (jnp.cos(ang), jnp.where(lane < half, -sin, 0.0), jnp.where((lane >= half) & (lane < ROT_DIM), sin, 0.0))
    ang_t = inv_freq[:, None] * pos[None, :]
    return lane_tables, (jnp.cos(ang_t), jnp.sin(ang_t))


def _rope(tv, cos, s1, s2):
    cols = []
    for g in range(tv.shape[1] // 128):
        blk = tv[:, 128 * g : 128 * (g + 1)]
        cols.append(blk * cos + pltpu.roll(blk, 128 - 8, 1) * s1 + pltpu.roll(blk, 8, 1) * s2)
    return cols[0] if len(cols) == 1 else jnp.concatenate(cols, axis=1)


def _rope_transposed(gv, cos, s1, s2):
    cols = []
    for g in range(gv.shape[1] // 128):
        blk = gv[:, 128 * g : 128 * (g + 1)]
        cols.append(blk * cos + pltpu.roll(blk * s1, 8, 1) + pltpu.roll(blk * s2, 128 - 8, 1))
    return cols[0] if len(cols) == 1 else jnp.concatenate(cols, axis=1)


def _lane_lo():
    return lax.broadcasted_iota(jnp.int32, (1, 128), 1) < HEAD_DIM


def _row_lo():
    return lax.broadcasted_iota(jnp.int32, (128, 1), 0) < HEAD_DIM


def _split_heads(tile, j):
    lo = _lane_lo()
    parts = []
    for g in (2 * j, 2 * j + 1):
        blk = tile[:, 128 * g : 128 * (g + 1)]
        parts += [jnp.where(lo, blk, 0.0), jnp.where(lo, 0.0, blk)]
    return jnp.concatenate(parts, axis=0)


def _dup_lanes(kv, j):
    rolled = pltpu.roll(kv, HEAD_DIM, 1)
    lo = _lane_lo()
    return jnp.where(lo, kv, rolled) if j == 0 else jnp.where(lo, rolled, kv)


def _fold_lanes(acc):
    return acc + pltpu.roll(acc, HEAD_DIM, 1)


def _fill_bias_t(bias_ref):
    kj = lax.broadcasted_iota(jnp.int32, (256, 128), 0)
    qi = lax.broadcasted_iota(jnp.int32, (256, 128), 1)
    diff = qi + WINDOW - kj
    band = (diff >= 0) & (diff < WINDOW)
    bias_ref[0] = jnp.where(band, 0.0, -1e30)
    bias_ref[1] = jnp.where(band & (kj >= 128), 0.0, -1e30)


def _sink_row(sink_ref, pair, j):
    return jnp.concatenate([jnp.full((1, 128), sink_ref[0, 8 * pair + 4 * j + h], F32) for h in range(4)], axis=1)


def _softmax_t(q4, kw, bias4, sink):
    s = _dot_nt(kw, q4) + bias4
    m = jnp.maximum(jnp.max(s, axis=0, keepdims=True), sink)
    p = jnp.exp(s - m)
    e_sink = jnp.exp(sink - m)
    denom = jnp.sum(p, axis=0, keepdims=True) + e_sink
    return p, e_sink, denom


def _attn_specs(t, tq):
    return dict(
        q=pl.BlockSpec((tq, 512), lambda p, i: (i, p)),
        k=pl.BlockSpec((t, 128), lambda p, i: (0, SEG_K // 128 + p)),
        v=pl.BlockSpec((t, 128), lambda p, i: (0, SEG_V // 128 + p)),
        sink=pl.BlockSpec(memory_space=pltpu.SMEM),
        act=pl.BlockSpec((tq, 512), lambda p, i: (i, p)),
    )


def _attn_fwd(proj, sinks):
    t = proj.shape[0]
    tq = min(512, t)
    nb = tq // 128
    n_tiles = t // tq
    sp = _attn_specs(t, tq)

    def body(q_ref, k_ref, v_ref, sink_ref, o_ref, kd, vdt, bias_ref):
        pair = pl.program_id(0)
        i = pl.program_id(1)

        @pl.when(i == 0)
        def _():
            kv, vv = k_ref[...], v_ref[...]
            for j in range(2):
                kd[j, 0:128, :] = jnp.zeros((128, 128), BF16)
                kd[j, 128:, :] = _dup_lanes(kv, j).astype(BF16)
                vdt[j, :, 0:128] = jnp.zeros((128, 128), BF16)
                vdt[j, :, 128:] = _dup_lanes(vv, j).T.astype(BF16)
            _fill_bias_t(bias_ref)

        row_lo = _row_lo()

        def block(b, carry):
            r0 = pl.multiple_of(b * 128, 128)
            gb = i * nb + b
            keys = pl.ds(pl.multiple_of(gb * 128, 128), 256)
            rows = pl.ds(r0, 128)
            qt = q_ref[rows, :] * (1.0 / math.sqrt(HEAD_DIM))
            bias = bias_ref[jnp.where(gb == 0, 1, 0)]
            bias4 = jnp.concatenate([bias] * 4, axis=1)
            cols = []
            for j in range(2):
                q4 = _split_heads(qt, j).astype(BF16)
                p, _, denom = _softmax_t(q4, kd[j, keys, :], bias4, _sink_row(sink_ref, pair, j))
                ot = _dot(vdt[j, :, keys], p.astype(BF16)) * (1.0 / denom)
                for gi in range(2):
                    z = jnp.where(row_lo, ot[:, 256 * gi : 256 * gi + 128], ot[:, 256 * gi + 128 : 256 * gi + 256])
                    cols.append(z.T)
            o_ref[rows, :] = jnp.concatenate(cols, axis=1)
            return carry

        lax.fori_loop(0, nb, block, 0, unroll=True)

    return pl.pallas_call(
        body,
        name="attn_fwd",
        grid=(2, n_tiles),
        in_specs=[sp["q"], sp["k"], sp["v"], sp["sink"]],
        out_specs=sp["act"],
        out_shape=jax.ShapeDtypeStruct((t, D), F32),
        scratch_shapes=[pltpu.VMEM((2, t + 128, 128), BF16), pltpu.VMEM((2, 128, t + 128), BF16), pltpu.VMEM((2, 256, 128), F32)],
        compiler_params=_params(2),
    )(proj, proj, proj, sinks)


def _attn_bwd(proj, o, d_o, tables, sinks, part_3):
    t = proj.shape[0]
    tq = min(512, t)
    nb = tq // 128
    n_tiles = t // tq
    (cos, s1, s2), (cos_t, sin_t) = tables
    sp = _attn_specs(t, tq)
    scale = 1.0 / math.sqrt(HEAD_DIM)
    half = ROT_DIM // 2

    def unrope_rows(z, c, s):
        out = []
        for base in (0, HEAD_DIM):
            g0, g1 = z[base : base + half], z[base + half : base + ROT_DIM]
            out += [g0 * c + g1 * s, g1 * c - g0 * s, z[base + ROT_DIM : base + HEAD_DIM]]
        return jnp.concatenate(out, axis=0)

    def swap(part_hbm, recv_hbm, send_sem, recv_sem):
        sibling = _flip(_position(), 1)
        return pltpu.make_async_remote_copy(
            src_ref=part_hbm.at[:, :, sibling[2]], dst_ref=recv_hbm, send_sem=send_sem, recv_sem=recv_sem,
            device_id=sibling, device_id_type=MESH,
        )

    def body(q_ref, k_ref, v_ref, o_ref, do_ref, ct_ref, st_ref, ck, s1k, s2k, sink_ref, part_hbm,
             dq_ref, dk_ref, dv_ref, dsink_ref, recv_hbm, kd, kdt, vd, bias_ref, dkacc, dvacc, send_sem, recv_sem):
        pair = pl.program_id(0)
        i = pl.program_id(1)

        @pl.when((pair == 0) & (i == 0))
        def _():
            swap(part_hbm, recv_hbm, send_sem, recv_sem).start()

        @pl.when((pair == 1) & (i == n_tiles - 1))
        def _():
            swap(part_hbm, recv_hbm, send_sem, recv_sem).wait()

        @pl.when(i == 0)
        def _():
            kv, vv = k_ref[...], v_ref[...]
            for j in range(2):
                kdup = _dup_lanes(kv, j)
                kd[j, 0:128, :] = jnp.zeros((128, 128), BF16)
                kd[j, 128:, :] = kdup.astype(BF16)
                kdt[j, :, 0:128] = jnp.zeros((128, 128), BF16)
                kdt[j, :, 128:] = kdup.T.astype(BF16)
                vd[j, 0:128, :] = jnp.zeros((128, 128), BF16)
                vd[j, 128:, :] = _dup_lanes(vv, j).astype(BF16)
            _fill_bias_t(bias_ref)
            dkacc[...] = jnp.zeros_like(dkacc)
            dvacc[...] = jnp.zeros_like(dvacc)
            dsink_ref[...] = jnp.zeros_like(dsink_ref)

        row_lo = _row_lo()
        ones = jnp.ones((8, 128), BF16)

        def block(b, carry):
            r0 = pl.multiple_of(b * 128, 128)
            gb = i * nb + b
            keys = pl.ds(pl.multiple_of(gb * 128, 128), 256)
            rows = pl.ds(r0, 128)
            qt = q_ref[rows, :] * scale
            dob = do_ref[rows, :]
            prod = dob * o_ref[rows, :]
            ct, st = ct_ref[:, rows], st_ref[:, rows]
            bias = bias_ref[jnp.where(gb == 0, 1, 0)]
            bias4 = jnp.concatenate([bias] * 4, axis=1)
            cols = []
            for j in range(2):
                q4 = _split_heads(qt, j).astype(BF16)
                kw = kd[j, keys, :]
                p, e_sink, denom = _softmax_t(q4, kw, bias4, _sink_row(sink_ref, pair, j))
                inv = 1.0 / denom
                prob = p * inv
                do4 = _split_heads(dob, j).astype(BF16)
                delta = _dot_nt(ones, _split_heads(prod, j).astype(BF16))[0:1, :]
                dp = _dot_nt(vd[j, keys, :], do4)
                ds = (prob * (dp - delta)).astype(BF16)
                dkacc[j, keys, :] = dkacc[j, keys, :] + _dot(ds, q4)
                dvacc[j, keys, :] = dvacc[j, keys, :] + _dot(prob.astype(BF16), do4)
                dqt = _dot(kdt[j, :, keys], ds) * scale
                dsink = -(e_sink * inv) * delta
                for h in range(4):
                    row = 4 * j + h
                    total = jnp.sum(dsink[:, 128 * h : 128 * (h + 1)], axis=1, keepdims=True)
                    dsink_ref[0, row : row + 1, :] = dsink_ref[0, row : row + 1, :] + jnp.broadcast_to(total, (1, 128))
                for gi in range(2):
                    z = jnp.where(row_lo, dqt[:, 256 * gi : 256 * gi + 128], dqt[:, 256 * gi + 128 : 256 * gi + 256])
                    cols.append(unrope_rows(z, ct, st).T)
            dq_ref[rows, :] = jnp.concatenate(cols, axis=1).astype(BF16)
            return carry

        lax.fori_loop(0, nb, block, 0, unroll=True)

        @pl.when(i == n_tiles - 1)
        def _():
            lo = _lane_lo()
            dk = jnp.where(lo, _fold_lanes(dkacc[0, 128:, :]), _fold_lanes(dkacc[1, 128:, :]))
            dk_ref[...] = _rope_transposed(dk, ck[...], s1k[...], s2k[...]).astype(BF16)
            dv_ref[...] = jnp.where(lo, _fold_lanes(dvacc[0, 128:, :]), _fold_lanes(dvacc[1, 128:, :])).astype(BF16)

    kv_out = pl.BlockSpec((t, 128), lambda p, i: (0, p))
    tab_t = pl.BlockSpec((half, tq), lambda p, i: (0, i))
    tab_k = pl.BlockSpec((t, 128), lambda p, i: (0, 0))
    return pl.pallas_call(
        body,
        name="attn_bwd",
        grid=(2, n_tiles),
        in_specs=[sp["q"], sp["k"], sp["v"], sp["act"], sp["act"], tab_t, tab_t, tab_k, tab_k, tab_k, sp["sink"],
                  pl.BlockSpec(memory_space=pl.ANY)],
        out_specs=[sp["act"], kv_out, kv_out, pl.BlockSpec((1, 8, 128), lambda p, i: (p, 0, 0)), pl.BlockSpec(memory_space=pl.ANY)],
        out_shape=[
            jax.ShapeDtypeStruct((t, D), BF16),
            jax.ShapeDtypeStruct((t, 256), BF16),
            jax.ShapeDtypeStruct((t, 256), BF16),
            jax.ShapeDtypeStruct((2, 8, 128), F32),
            jax.ShapeDtypeStruct((3, 4, ROW_SHARD, D), F32),
        ],
        scratch_shapes=[pltpu.VMEM((2, t + 128, 128), BF16), pltpu.VMEM((2, 128, t + 128), BF16), pltpu.VMEM((2, t + 128, 128), BF16),
                        pltpu.VMEM((2, 256, 128), F32), pltpu.VMEM((2, t + 128, 128), F32), pltpu.VMEM((2, t + 128, 128), F32),
                        pltpu.SemaphoreType.DMA, pltpu.SemaphoreType.DMA],
        compiler_params=_params(2),
    )(proj, proj, proj, o, d_o, cos_t, sin_t, cos, s1, s2, sinks, part_3)


HALF = 512


def _softplus(y):
    u = jnp.exp(-jnp.abs(y))
    w = 1.0 + u
    log1p = jnp.where(w == 1.0, u, jnp.log(w) * (u / jnp.where(w == 1.0, 1.0, w - 1.0)))
    return jnp.maximum(y, 0.0) + log1p


def _gate_block(w_ref, blk):
    return w_ref[:, :, 256 * blk : 256 * (blk + 1)].reshape(256, 256)


def _gate_matmul(act_bf, w_ref):
    return jnp.concatenate([_dot(act_bf[:, 256 * blk : 256 * (blk + 1)], _gate_block(w_ref, blk)) for blk in range(2)], axis=1)


def _gate_matmul_t(grad_bf, w_ref):
    return jnp.concatenate([_dot_nt(grad_bf[:, 256 * blk : 256 * (blk + 1)], _gate_block(w_ref, blk)) for blk in range(2)], axis=1)


def _conv_and_gates(xe_ref, tm, cw_ref, cb_ref, wa_ref, wx_ref, ba_ref, bx_ref, lam_ref, keep):
    xc = cb_ref[...] + sum(cw_ref[k : k + 1, :] * xe_ref[pl.ds(5 + k, tm), :] for k in range(CONV_WIDTH))
    xc_bf = xc.astype(BF16)
    r = _sigmoid(_gate_matmul(xc_bf, wa_ref) + ba_ref[...])
    ig = _sigmoid(_gate_matmul(xc_bf, wx_ref) + bx_ref[...])
    sp = _softplus(-lam_ref[...])
    log_a = -LRU_C * r * sp
    tanh = jnp.tanh(-log_a)
    mult = jnp.sqrt(2.0 * tanh / (1.0 + tanh))
    a = jnp.where(keep, jnp.exp(log_a), 0.0)
    mult = jnp.where(keep, mult, 1.0)
    return xc, xc_bf, r, ig, sp, a, mult


def _scan_rows(a_ref, b_ref, h_ref, carry_ref, tm, reverse):
    width = a_ref.shape[1]
    row = lax.broadcasted_iota(jnp.int32, (8, width), 0)

    def group(g, carry):
        gg = tm // 8 - 1 - g if reverse else g
        r0 = pl.multiple_of(gg * 8, 8)
        a = a_ref[pl.ds(r0, 8), :]
        b = b_ref[pl.ds(r0, 8), :]
        for s in (1, 2, 4):
            shift = 8 - s if reverse else s
            a_sh = pltpu.roll(a, shift, 0)
            b_sh = pltpu.roll(b, shift, 0)
            ok = (row < 8 - s) if reverse else (row >= s)
            b = jnp.where(ok, a * b_sh + b, b)
            a = jnp.where(ok, a * a_sh, a)
        h = a * carry + b
        h_ref[pl.ds(r0, 8), :] = h
        last = h[0:1, :] if reverse else h[7:8, :]
        return jnp.broadcast_to(last, (8, width))

    carry_ref[...] = lax.fori_loop(0, tm // 8, group, carry_ref[...])


def _rnn_specs(t, tm, order):
    n = t // tm
    return dict(
        tile=pl.BlockSpec((tm, HALF), lambda ch, i: (order(i, n), ch)),
        before=lambda col0: pl.BlockSpec((8, HALF), lambda ch, i: (jnp.maximum(order(i, n) * (tm // 8) - 1, 0), col0 + ch)),
        proj_tile=lambda col0: pl.BlockSpec((tm, HALF), lambda ch, i: (order(i, n), col0 + ch)),
        keep=pl.BlockSpec((tm, 1), lambda ch, i: (order(i, n), 0)),
        gates=lambda which: pl.BlockSpec((N_DEV, GATE_SHARD, HALF), lambda ch, i: (0, PACK_GATE // GATE_SHARD + which, ch)),
        gate_grads=pl.BlockSpec((2, 2, 256, 256), lambda ch, i: (0, ch, 0, 0)),
        vec=pl.BlockSpec((1, HALF), lambda ch, i: (0, ch)),
        conv=pl.BlockSpec((CONV_WIDTH, HALF), lambda ch, i: (0, ch)),
    )


def _rnn_fwd(proj, keep, w_all, rg_ba, rg_bx, rg_lambda, conv_w, conv_b):
    t = proj.shape[0]
    tm = min(512, t)
    sp = _rnn_specs(t, tm, lambda i, n: i)

    def body(xr_ref, xb_ref, keep_ref, wa_ref, wx_ref, ba_ref, bx_ref, lam_ref, cw_ref, cb_ref, hh_ref, xe, a_buf, b_buf, carry):
        i = pl.program_id(1)

        @pl.when(i == 0)
        def _():
            carry[...] = jnp.zeros_like(carry)

        xe[0:8, :] = jnp.where(i > 0, xb_ref[...], 0.0)
        xe[8:, :] = xr_ref[...]
        keep_rows = keep_ref[...] != 0.0
        xc, _, _, ig, _, a, mult = _conv_and_gates(xe, tm, cw_ref, cb_ref, wa_ref, wx_ref, ba_ref, bx_ref, lam_ref, keep_rows)
        a_buf[...] = a
        b_buf[...] = mult * (ig * xc)
        _scan_rows(a_buf, b_buf, hh_ref, carry, tm, reverse=False)

    return pl.pallas_call(
        body,
        name="rnn_fwd",
        grid=(2, t // tm),
        in_specs=[sp["proj_tile"](SEG_XR // HALF), sp["before"](SEG_XR // HALF), sp["keep"], sp["gates"](0), sp["gates"](1),
                  sp["vec"], sp["vec"], sp["vec"], sp["conv"], sp["vec"]],
        out_specs=sp["tile"],
        out_shape=jax.ShapeDtypeStruct((t, D), F32),
        scratch_shapes=[pltpu.VMEM((tm + 8, HALF), F32), pltpu.VMEM((tm, HALF), F32), pltpu.VMEM((tm, HALF), F32), pltpu.VMEM((8, HALF), F32)],
        compiler_params=_params(2),
    )(proj, proj, keep, w_all, w_all, rg_ba, rg_bx, rg_lambda, conv_w, conv_b)


def _rnn_bwd(proj, hh, d_hh, keep, w_all, rg_ba, rg_bx, rg_lambda, conv_w, conv_b, sum_3_bf):
    t = proj.shape[0]
    tm = min(512, t)
    n_tiles = t // tm
    sp = _rnn_specs(t, tm, lambda i, n: n - 1 - i)

    def body(xr_ref, xb_ref, hh_ref, hb_ref, dhh_ref, keep_ref, wa_ref, wx_ref, ba_ref, bx_ref, lam_ref, cw_ref, cb_ref,
             s3_hbm, dxr_ref, gw_ref, gvec_ref, l3_hbm, xe, he, ae, c_buf, g_buf, dxe, carry, a_next, send_sems, recv_sems):
        ch = pl.program_id(0)
        i = pl.program_id(1)
        first_tile = i == n_tiles - 1
        views = [lambda ref, peer: ref.at[:, _chip_of(peer)]]

        @pl.when((ch == 0) & (i == 0))
        def _():
            for cp in _chip_copies([s3_hbm], [l3_hbm], views, send_sems, recv_sems):
                cp.start()

        @pl.when((ch == 1) & (i == n_tiles - 1))
        def _():
            for cp in _chip_copies([s3_hbm], [l3_hbm], views, send_sems, recv_sems):
                cp.wait_recv()
                cp.wait_send()

        @pl.when(i == 0)
        def _():
            carry[...] = jnp.zeros_like(carry)
            a_next[...] = jnp.zeros_like(a_next)
            dxe[tm:, :] = jnp.zeros((8, HALF), F32)
            gw_ref[...] = jnp.zeros_like(gw_ref)
            gvec_ref[...] = jnp.zeros_like(gvec_ref)

        xe[0:8, :] = jnp.where(first_tile, 0.0, xb_ref[...])
        xe[8:, :] = xr_ref[...]
        he[0:8, :] = jnp.where(first_tile, 0.0, hb_ref[...])
        he[8:, :] = hh_ref[...]
        keep_rows = keep_ref[...] != 0.0
        xc, xc_bf, r, ig, sp_, a, mult = _conv_and_gates(xe, tm, cw_ref, cb_ref, wa_ref, wx_ref, ba_ref, bx_ref, lam_ref, keep_rows)

        ae[0:tm, :] = a
        ae[tm:, :] = a_next[...]
        a_next[...] = jnp.broadcast_to(a[0:1, :], (8, HALF))
        c_buf[...] = ae[pl.ds(1, tm), :]
        g_buf[...] = dhh_ref[...]
        _scan_rows(c_buf, g_buf, g_buf, carry, tm, reverse=True)
        g = g_buf[...]

        h_prev = he[pl.ds(7, tm), :]
        bb = ig * xc
        d_ig = g * mult * xc
        d_xc = g * mult * ig
        d_log_a = jnp.where(keep_rows, g * h_prev * a - (g * bb) * (a * a) / mult, 0.0)
        d_r = d_log_a * (-LRU_C * sp_)
        d_sp = jnp.sum(d_log_a * (-LRU_C * r), axis=0, keepdims=True)
        d_zr = d_r * r * (1.0 - r)
        d_zi = d_ig * ig * (1.0 - ig)
        d_zr_bf = d_zr.astype(BF16)
        d_zi_bf = d_zi.astype(BF16)
        d_xc = d_xc + _gate_matmul_t(d_zr_bf, wa_ref) + _gate_matmul_t(d_zi_bf, wx_ref)
        for blk in range(2):
            cols = slice(256 * blk, 256 * (blk + 1))
            gw_ref[0, blk] = gw_ref[0, blk] + _dot_tn(xc_bf[:, cols], d_zr_bf[:, cols])
            gw_ref[1, blk] = gw_ref[1, blk] + _dot_tn(xc_bf[:, cols], d_zi_bf[:, cols])

        dxe[0:tm, :] = d_xc
        dxr = sum(cw_ref[k : k + 1, :] * dxe[pl.ds(3 - k, tm), :] for k in range(CONV_WIDTH))
        dxr_ref[...] = dxr.astype(BF16)
        dxe[tm:, :] = d_xc[0:8, :]

        lam = lam_ref[...]
        sums = [
            jnp.sum(d_zr, axis=0, keepdims=True),
            jnp.sum(d_zi, axis=0, keepdims=True),
            d_sp * (-_sigmoid(-lam)),
            jnp.sum(d_xc, axis=0, keepdims=True),
        ] + [jnp.sum(d_xc * xe[pl.ds(5 + k, tm), :], axis=0, keepdims=True) for k in range(CONV_WIDTH)]
        gvec_ref[...] = gvec_ref[...] + jnp.concatenate(sums, axis=0)

    return pl.pallas_call(
        body,
        name="rnn_bwd",
        grid=(2, n_tiles),
        in_specs=[
            sp["proj_tile"](SEG_XR // HALF), sp["before"](SEG_XR // HALF), sp["tile"], sp["before"](0), sp["tile"], sp["keep"],
            sp["gates"](0), sp["gates"](1), sp["vec"], sp["vec"], sp["vec"], sp["conv"], sp["vec"], pl.BlockSpec(memory_space=pl.ANY),
        ],
        out_specs=[sp["tile"], sp["gate_grads"], pl.BlockSpec((8, HALF), lambda ch, i: (0, ch)), pl.BlockSpec(memory_space=pl.ANY)],
        out_shape=[
            jax.ShapeDtypeStruct((t, D), BF16),
            jax.ShapeDtypeStruct((2, 4, 256, 256), F32),
            jax.ShapeDtypeStruct((8, D), F32),
            jax.ShapeDtypeStruct((len(CHIP_FLIPS), 3, ROW_SHARD, D), BF16),
        ],
        scratch_shapes=[
            pltpu.VMEM((tm + 8, HALF), F32),
            pltpu.VMEM((tm + 8, HALF), F32),
            pltpu.VMEM((tm + 8, HALF), F32),
            pltpu.VMEM((tm, HALF), F32),
            pltpu.VMEM((tm, HALF), F32),
            pltpu.VMEM((tm + 8, HALF), F32),
            pltpu.VMEM((8, HALF), F32),
            pltpu.VMEM((8, HALF), F32),
            pltpu.SemaphoreType.DMA((len(CHIP_FLIPS),)),
            pltpu.SemaphoreType.DMA((len(CHIP_FLIPS),)),
        ],
        compiler_params=_params(2),
    )(proj, proj, hh, hh, d_hh, keep, w_all, w_all, rg_ba, rg_bx, rg_lambda, conv_w, conv_b, sum_3_bf)


def _mid(x, target, o, hh, proj, w_all, gate, final_g):
    t = x.shape[0]
    tm = min(256, t)
    n_tiles = t // tm
    full = pl.BlockSpec((tm, D), lambda i: (i, 0))
    half = lambda col: pl.BlockSpec((tm, 512), lambda i: (i, col))
    vec = pl.BlockSpec((1, D), lambda i: (0, 0))
    any_spec = pl.BlockSpec(memory_space=pl.ANY)

    def body(x_ref, tg_ref, o_ref, hh_ref, ga0, ga1, gr0, gr1, ma0, ma1, mr0, mr1, w_hbm, gate_ref, fg_ref,
             dres_ref, do_ref, dhh_ref, dga_ref, dgr_ref, dma_ref, dmr_ref, gw_hbm, vec_ref, w, gw, sem, w_sems):
        i = pl.program_id(0)

        @pl.when(i == 0)
        def _():
            loads = [
                pltpu.make_async_copy(
                    w_hbm.at[s, pl.ds(ROW_SHARD * k, ROW_SHARD), :], w.at[k, pl.ds(ROW_SHARD * s, ROW_SHARD), :], w_sems.at[k * N_DEV + s]
                )
                for k in range(3)
                for s in range(N_DEV)
            ]
            for cp in loads:
                cp.start()
            for cp in loads:
                cp.wait()
            gw[...] = jnp.zeros_like(gw)
            vec_ref[...] = jnp.zeros_like(vec_ref)

        cat = lambda a, b: jnp.concatenate([a[...], b[...]], axis=1)
        ga, gr, ma, mr = cat(ga0, ga1), cat(gr0, gr1), cat(ma0, ma1), cat(mr0, mr1)
        ov, hv, gate_v, fg = o_ref[...], hh_ref[...], gate_ref[...], fg_ref[...]
        sg_a = _sigmoid(ga)
        sg_r = _sigmoid(gr)
        silu_a = ga * sg_a
        silu_r = gr * sg_r
        ya = (ov * silu_a).astype(BF16)
        yr = (hv * silu_r).astype(BF16)
        pa = _dot(ya, w[0])
        pr = _dot(yr, w[1])
        sa = _sigmoid(ma)
        sr = _sigmoid(mr)
        merged = (sa * pa + sr * pr).astype(BF16)
        mo = _dot(merged, w[2])
        out = x_ref[...] + gate_v * mo
        r2 = lax.rsqrt(jnp.mean(out * out, axis=-1, keepdims=True) + NORM_EPS)
        outn = out * r2
        err = outn * fg - tg_ref[...]
        loss = 0.5 * jnp.sum(jnp.mean(err * err, axis=-1, keepdims=True), axis=0, keepdims=True)
        dy = err * (1.0 / D)
        g_final = jnp.sum(dy * outn, axis=0, keepdims=True)
        doutn = dy * fg
        dout = r2 * (doutn - outn * jnp.mean(doutn * outn, axis=-1, keepdims=True))
        dres_ref[...] = dout
        d_gate = jnp.sum(dout * mo, axis=0, keepdims=True)
        dmo = (dout * gate_v).astype(BF16)
        dmerged = _dot_nt(dmo, w[2])
        gw[2] = gw[2] + _dot_tn(merged, dmo)
        dpa = (dmerged * sa).astype(BF16)
        dpr = (dmerged * sr).astype(BF16)
        dma_ref[...] = (dmerged * pa * (sa * (1.0 - sa))).astype(BF16)
        dmr_ref[...] = (dmerged * pr * (sr * (1.0 - sr))).astype(BF16)
        dya = _dot_nt(dpa, w[0])
        dyr = _dot_nt(dpr, w[1])
        gw[0] = gw[0] + _dot_tn(ya, dpa)
        gw[1] = gw[1] + _dot_tn(yr, dpr)
        do_ref[...] = dya * silu_a
        dhh_ref[...] = dyr * silu_r
        dga_ref[...] = (dya * ov * (sg_a * (1.0 + ga * (1.0 - sg_a)))).astype(BF16)
        dgr_ref[...] = (dyr * hv * (sg_r * (1.0 + gr * (1.0 - sg_r)))).astype(BF16)
        zeros = jnp.zeros((5, D), F32)
        vec_ref[...] = vec_ref[...] + jnp.concatenate([g_final, d_gate, jnp.broadcast_to(loss, (1, D)), zeros], axis=0)

        @pl.when(i == n_tiles - 1)
        def _():
            cp = pltpu.make_async_copy(gw, gw_hbm, sem)
            cp.start()
            cp.wait()

    bf_tile = jax.ShapeDtypeStruct((t, D), BF16)
    f_tile = jax.ShapeDtypeStruct((t, D), F32)
    return pl.pallas_call(
        body,
        name="mid",
        grid=(n_tiles,),
        in_specs=[full, full, full, full,
                  half(SEG_GA // 512), half(SEG_GA // 512 + 1), half(SEG_GR // 512), half(SEG_GR // 512 + 1),
                  half(SEG_MA // 512), half(SEG_MA // 512 + 1), half(SEG_MR // 512), half(SEG_MR // 512 + 1),
                  any_spec, vec, vec],
        out_specs=[full, full, full, full, full, full, full, any_spec, pl.BlockSpec((8, D), lambda i: (0, 0))],
        out_shape=[f_tile, f_tile, f_tile, bf_tile, bf_tile, bf_tile, bf_tile,
                   jax.ShapeDtypeStruct((3, D, D), F32), jax.ShapeDtypeStruct((8, D), F32)],
        scratch_shapes=[pltpu.VMEM((3, D, D), BF16), pltpu.VMEM((3, D, D), F32), pltpu.SemaphoreType.DMA, pltpu.SemaphoreType.DMA((3 * N_DEV,))],
        compiler_params=_params(1),
    )(x, target, o, hh, proj, proj, proj, proj, proj, proj, proj, proj, w_all, gate, final_g)


_SEGMENTS = ((SEG_Q, 1024), (SEG_K, 256), (SEG_V, 256), (SEG_GA, 1024), (SEG_XR, 1024), (SEG_GR, 1024), (SEG_MA, 1024), (SEG_MR, 1024))


def _seg_specs(tm):
    return [pl.BlockSpec((tm, width), lambda i: (i, 0)) for _, width in _SEGMENTS]


def _bwd_dh(dsegs, w_all, x, dres, mod, norm_g, sum_w_bf):
    t = x.shape[0]
    tm = min(512, t)
    n_tiles = t // tm
    n_seg = len(_SEGMENTS)
    full = pl.BlockSpec((tm, D), lambda i: (i, 0))
    views = [lambda ref, peer: ref.at[_chip_of(peer)]]

    def body(*refs):
        segs = refs[:n_seg]
        w_hbm, x_ref, dres_ref, mod_ref, ng_ref, sum_hbm, gx_ref, vec_ref, land_hbm, w, sems, send_sems, recv_sems = refs[n_seg:]
        i = pl.program_id(0)

        @pl.when(i == 0)
        def _():
            for cp in _chip_copies([sum_hbm], [land_hbm], views, send_sems, recv_sems):
                cp.start()
            loads = [
                pltpu.make_async_copy(w_hbm.at[s], w.at[pl.ds(W_IN_SHARD * s, W_IN_SHARD), :], sems.at[s])
                for s in range(N_DEV)
            ]
            for cp in loads:
                cp.start()
            for cp in loads:
                cp.wait()
            vec_ref[...] = jnp.zeros_like(vec_ref)

        dh = sum(_dot(seg[...], w[start : start + width, :]) for seg, (start, width) in zip(segs, _SEGMENTS))
        xv = x_ref[...]
        r1 = lax.rsqrt(jnp.mean(xv * xv, axis=-1, keepdims=True) + NORM_EPS)
        xn = xv * r1
        ng = ng_ref[...]
        scale1 = 1.0 + mod_ref[:, D : 2 * D]
        d_shift = jnp.sum(dh, axis=0, keepdims=True)
        d_scale = jnp.sum(dh * (xn * ng), axis=0, keepdims=True)
        g_norm = jnp.sum(dh * xn * scale1, axis=0, keepdims=True)
        dxn = dh * (ng * scale1)
        dx = r1 * (dxn - xn * jnp.mean(dxn * xn, axis=-1, keepdims=True))
        gx_ref[...] = dres_ref[...] + dx
        vec_ref[...] = vec_ref[...] + jnp.concatenate([d_shift, d_scale, g_norm, jnp.zeros((5, D), F32)], axis=0)

        @pl.when(i == n_tiles - 1)
        def _():
            for cp in _chip_copies([sum_hbm], [land_hbm], views, send_sems, recv_sems):
                cp.wait_recv()
                cp.wait_send()

    any_spec = pl.BlockSpec(memory_space=pl.ANY)
    n_copies = len(CHIP_FLIPS)
    return pl.pallas_call(
        body,
        name="bwd_dh",
        grid=(n_tiles,),
        in_specs=_seg_specs(tm) + [any_spec, full, full, pl.BlockSpec((1, 3 * D), lambda i: (0, 0)), pl.BlockSpec((1, D), lambda i: (0, 0)),
                                   any_spec],
        out_specs=[full, pl.BlockSpec((8, D), lambda i: (0, 0)), any_spec],
        out_shape=[jax.ShapeDtypeStruct((t, D), F32), jax.ShapeDtypeStruct((8, D), F32),
                   jax.ShapeDtypeStruct((n_copies, W_IN_SHARD, D), BF16)],
        scratch_shapes=[pltpu.VMEM((IN_W, D), BF16), pltpu.SemaphoreType.DMA((N_DEV,)),
                        pltpu.SemaphoreType.DMA((n_copies,)), pltpu.SemaphoreType.DMA((n_copies,))],
        compiler_params=_params(1),
    )(*dsegs, w_all, x, dres, mod, norm_g, sum_w_bf)


def _bwd_gw(dsegs, h, sum_g_bf):
    t = h.shape[0]
    tm = min(512, t)
    n_tiles = t // tm
    n_seg = len(_SEGMENTS)
    n_chips = N_DEV // 2
    views = [lambda ref, peer: ref.at[:, :, _chip_of(peer)]]

    def body(*refs):
        segs = refs[:n_seg]
        (h_ref, sg_hbm, own_hbm, recv_hbm, lg_hbm, acc, stage, own_sems, pair_send, pair_recv, send_sems, recv_sems) = refs[n_seg:]
        i = pl.program_id(0)

        @pl.when(i == 0)
        def _():
            for cp in _chip_copies([sg_hbm], [lg_hbm], views, send_sems, recv_sems):
                cp.start()
            acc[...] = jnp.zeros_like(acc)

        hv = h_ref[...]
        for seg, (start, width) in zip(segs, _SEGMENTS):
            acc[start : start + width, :] = acc[start : start + width, :] + _dot_tn(seg[...], hv)

        @pl.when(i == n_tiles - 1)
        def _():
            me = _position()
            sibling = _flip(me, 1)

            def rows(q, c):
                return pl.ds(pl.multiple_of(W_IN_SHARD * (2 * q + c), 8), W_IN_SHARD)

            def to_sibling(q):
                return pltpu.make_async_remote_copy(
                    src_ref=stage.at[q % 2], dst_ref=recv_hbm.at[q], send_sem=pair_send.at[q], recv_sem=pair_recv.at[q],
                    device_id=sibling, device_id_type=MESH,
                )

            kept = [pltpu.make_async_copy(acc.at[rows(q, me[2]), :], own_hbm.at[q], own_sems.at[q]) for q in range(n_chips)]
            for cp in kept:
                cp.start()
            for q in range(n_chips):
                if q >= 2:
                    to_sibling(q - 2).wait_send()
                stage[q % 2] = acc[rows(q, sibling[2]), :].astype(BF16)
                to_sibling(q).start()
            for q in range(n_chips - 2, n_chips):
                to_sibling(q).wait_send()
            for q in range(n_chips):
                to_sibling(q).wait_recv()
            for cp in kept:
                cp.wait()
            for cp in _chip_copies([sg_hbm], [lg_hbm], views, send_sems, recv_sems):
                cp.wait_recv()
                cp.wait_send()

    any_spec = pl.BlockSpec(memory_space=pl.ANY)
    n_copies = len(CHIP_FLIPS)
    return pl.pallas_call(
        body,
        name="bwd_gw",
        grid=(n_tiles,),
        in_specs=_seg_specs(tm) + [pl.BlockSpec((tm, D), lambda i: (i, 0)), any_spec],
        out_specs=[any_spec, any_spec, any_spec],
        out_shape=[jax.ShapeDtypeStruct((n_chips, W_IN_SHARD, D), F32),
                   jax.ShapeDtypeStruct((n_chips, W_IN_SHARD, D), BF16),
                   jax.ShapeDtypeStruct((len(CHIP_FLIPS), 2, 4, GATE_SHARD, 256), BF16)],
        scratch_shapes=[pltpu.VMEM((IN_W, D), F32), pltpu.VMEM((2, W_IN_SHARD, D), BF16),
                        pltpu.SemaphoreType.DMA((n_chips,)), pltpu.SemaphoreType.DMA((n_chips,)), pltpu.SemaphoreType.DMA((n_chips,)),
                        pltpu.SemaphoreType.DMA((n_copies,)), pltpu.SemaphoreType.DMA((n_copies,))],
        compiler_params=_params(1),
    )(*dsegs, h, sum_g_bf)


def _adamw(w, g, m, v):
    m = ADAM_B1 * m + (1.0 - ADAM_B1) * g
    v = ADAM_B2 * v + (1.0 - ADAM_B2) * (g * g)
    m_hat = m / (1.0 - ADAM_B1**ADAM_STEP)
    v_hat = v / (1.0 - ADAM_B2**ADAM_STEP)
    delta = -ADAM_LR * (m_hat / (jnp.sqrt(v_hat) + ADAM_EPS) + ADAM_WD * w)
    return delta, m, v


def _w_in_update(own, land, w_t, m_t, v_t):
    n_slots, rows, cols = land.shape
    rb = 208
    spec = pl.BlockSpec((rb, cols), lambda i: (i, 0))

    def body(o_ref, l_ref, w_ref, m_ref, v_ref, g_ref, d_ref, m2_ref, v2_ref):
        total = o_ref[...]
        for s in range(n_slots):
            total = total + l_ref[s].astype(F32)
        g_ref[...] = total
        d_ref[...], m2_ref[...], v2_ref[...] = _adamw(w_ref[...], total, m_ref[...], v_ref[...])

    shape = jax.ShapeDtypeStruct((rows, cols), F32)
    return pl.pallas_call(
        body,
        name="w_in_update",
        grid=(rows // rb,),
        in_specs=[spec, pl.BlockSpec((n_slots, rb, cols), lambda i: (0, i, 0)), spec, spec, spec],
        out_specs=[spec] * 4,
        out_shape=[shape] * 4,
        compiler_params=_params(1),
    )(own, land, w_t, m_t, v_t)


def _shard_update(own_3, land_3, own_g, land_g, weights, m_in, v_in):
    n = len(weights)

    def body(*refs):
        o3_ref, l3_ref, og_ref, lg_ref = refs[:4]
        w_refs, m_refs, v_refs = (refs[4 + n * k : 4 + n * (k + 1)] for k in range(3))
        outs = refs[4 + 3 * n :]
        for k in range(n):
            if k < 3:
                g = o3_ref[k]
                for slot in range(3):
                    g = g + l3_ref[slot, k].astype(F32)
            else:
                g = og_ref[k - 3]
                for slot in range(3):
                    g = g + lg_ref[slot, k - 3].astype(F32)
            d, m2, v2 = _adamw(w_refs[k][0], g, m_refs[k][0], v_refs[k][0])
            for which, val in enumerate((g, d, m2, v2)):
                outs[which * n + k][0] = val

    shapes = [jax.ShapeDtypeStruct(w.shape, F32) for w in weights]
    res = pl.pallas_call(
        body, name="shard_update", out_shape=shapes * 4, compiler_params=pltpu.CompilerParams(vmem_limit_bytes=VMEM_LIMIT)
    )(own_3, land_3, own_g, land_g, *weights, *m_in, *v_in)
    return [res[n * k : n * (k + 1)] for k in range(4)]


VEC_LAYOUT = (("b_ada", 3 * D), ("norm_g", D), ("attn_sinks", 128), ("conv_b", D), ("rg_ba", D), ("rg_bx", D), ("rg_lambda", D),
              ("final_g", D), ("conv_w", CONV_WIDTH * D), ("loss", 128))
VEC_OFFSET = {name: sum(w for _, w in VEC_LAYOUT[:k]) for k, (name, _) in enumerate(VEC_LAYOUT)}
VEC_ROWS = 112
VEC_WIDTH = VEC_ROWS * 128
assert sum(w for _, w in VEC_LAYOUT) <= VEC_WIDTH
N_REPLICATED = 8


def _small_update(vec_all, c_all_t, me, rep_w, rep_m, rep_v, ada, conv):
    n_ada = ada[0].shape[1]
    n_conv = conv[0].shape[1]

    def body(*refs):
        me_ref, vec_ref, ct_ref = refs[:3]
        w_refs, m_refs, v_refs = (refs[3 + N_REPLICATED * k : 3 + N_REPLICATED * (k + 1)] for k in range(3))
        pos = 3 + 3 * N_REPLICATED
        ada_refs, conv_refs = refs[pos : pos + 3], refs[pos + 3 : pos + 6]
        outs = refs[pos + 6 :]
        loss_ref, rep_outs = outs[0], outs[1 : 1 + 4 * N_REPLICATED]
        ada_outs, conv_outs = outs[1 + 4 * N_REPLICATED : 5 + 4 * N_REPLICATED], outs[5 + 4 * N_REPLICATED : 9 + 4 * N_REPLICATED]
        conv_buf = outs[9 + 4 * N_REPLICATED]

        total = vec_ref[0:1, :]
        for slot in range(1, N_DEV):
            total = total + vec_ref[slot : slot + 1, :]
        loss_ref[...] = total[:, VEC_OFFSET["loss"] : VEC_OFFSET["loss"] + 1]
        for k, (name, _) in enumerate(VEC_LAYOUT[:N_REPLICATED]):
            width = w_refs[k].shape[1]
            g = total[:, VEC_OFFSET[name] : VEC_OFFSET[name] + width]
            d, m2, v2 = _adamw(w_refs[k][...], g, m_refs[k][...], v_refs[k][...])
            for which, val in enumerate((g, d, m2, v2)):
                rep_outs[which * N_REPLICATED + k][...] = val

        device = me_ref[0]
        for k in range(CONV_WIDTH):
            conv_buf[k : k + 1, :] = total[:, VEC_OFFSET["conv_w"] + D * k : VEC_OFFSET["conv_w"] + D * (k + 1)]
        g_conv = conv_buf[:, pl.ds(pl.multiple_of(device * n_conv, 128), n_conv)]
        for ref, val in zip(conv_outs, (g_conv,) + _adamw(conv_refs[0][...], g_conv, conv_refs[1][...], conv_refs[2][...])):
            ref[...] = val

        dmod = vec_ref[:, pl.ds(pl.multiple_of(device * n_ada, 128), n_ada)]
        g_ada = _dot(ct_ref[...], dmod)
        for ref, val in zip(ada_outs, (g_ada,) + _adamw(ada_refs[0][...], g_ada, ada_refs[1][...], ada_refs[2][...])):
            ref[...] = val

    vmem = pl.BlockSpec(memory_space=pltpu.VMEM)
    rep_shapes = [jax.ShapeDtypeStruct(w.shape, F32) for w in rep_w]
    out_shape = ([jax.ShapeDtypeStruct((1, 1), F32)] + rep_shapes * 4 + [jax.ShapeDtypeStruct(ada[0].shape, F32)] * 4
                 + [jax.ShapeDtypeStruct(conv[0].shape, F32)] * 4)
    n_in = 3 + 3 * N_REPLICATED + 6
    res = pl.pallas_call(
        body,
        name="small_update",
        in_specs=[pl.BlockSpec(memory_space=pltpu.SMEM)] + [vmem] * (n_in - 1),
        out_specs=[vmem] * len(out_shape),
        out_shape=out_shape,
        scratch_shapes=[pltpu.VMEM((CONV_WIDTH, D), F32)],
        compiler_params=pltpu.CompilerParams(vmem_limit_bytes=VMEM_LIMIT),
    )(me, vec_all, c_all_t, *rep_w, *rep_m, *rep_v, *ada, *conv)
    loss = res[0]
    reps = [res[1 + N_REPLICATED * k : 1 + N_REPLICATED * (k + 1)] for k in range(4)]
    return loss, reps, res[1 + 4 * N_REPLICATED : 5 + 4 * N_REPLICATED], res[5 + 4 * N_REPLICATED : 9 + 4 * N_REPLICATED]


def kernel(x, c, positions, w_ada, b_ada, norm_g, w_in, attn_sinks, conv_w, conv_b, rg_wa, rg_ba, rg_wx, rg_bx, rg_lambda, w_attn_proj, w_rnn_proj, w_out, final_g, loss_target, m_w_ada, m_b_ada, m_norm_g, m_w_in, m_attn_sinks, m_conv_w, m_conv_b, m_rg_wa, m_rg_ba, m_rg_wx, m_rg_bx, m_rg_lambda, m_w_attn_proj, m_w_rnn_proj, m_w_out, m_final_g, v_w_ada, v_b_ada, v_norm_g, v_w_in, v_attn_sinks, v_conv_w, v_conv_b, v_rg_wa, v_rg_ba, v_rg_wx, v_rg_bx, v_rg_lambda, v_w_attn_proj, v_w_rnn_proj, v_w_out, v_final_g):
    my_x, my_y, my_c = _position()
    me = _slot((my_x, my_y, my_c))
    x2, target = x[0], loss_target[0]
    pos = positions[0]

    gate_rows = lambda w: w[0].transpose(1, 0, 2).reshape(GATE_SHARD, D)
    w_pack = jnp.concatenate([w_attn_proj[0], w_rnn_proj[0], w_out[0], gate_rows(rg_wa), gate_rows(rg_wx)], axis=0).astype(BF16)
    conv_rows = jnp.pad(conv_w[0], ((0, 8 - CONV_WIDTH), (0, 0)))
    w_in_t = w_in[0].T
    n_ada = w_ada.shape[2]
    b_cols = lax.dynamic_slice(b_ada, (0, me * n_ada), (1, n_ada))

    tables = _rope_tables(pos)
    where = jnp.stack([_chip_of((my_x, my_y)), me]).astype(jnp.int32)
    proj, h, mod, w_t_all, w_all, c_rows, conv_all = _fwd_in(
        x2, jnp.broadcast_to(c, (8, D)), conv_rows, w_ada[0], b_cols, norm_g, w_in_t.astype(BF16), tables[0], w_pack, where
    )
    c_all = c_rows[:, 0, :]
    conv_full = conv_all[:, :CONV_WIDTH, :].transpose(1, 0, 2).reshape(CONV_WIDTH, D)
    gate = mod[:, 2 * D :]
    o = _attn_fwd(proj, attn_sinks)
    keep = (pos != 0).astype(F32)[:, None]
    hh = _rnn_fwd(proj, keep, w_all, rg_ba, rg_bx, rg_lambda, conv_full, conv_b)

    dres, d_o, d_hh, d_ga, d_gr, d_ma, d_mr, gw3, mid_vec = _mid(x2, target, o, hh, proj, w_all, gate, final_g[None, :])
    index = jnp.stack([my_c, _chip_of((my_x, my_y))]).astype(jnp.int32)
    part_3 = gw3.reshape(3, 4, 2, ROW_SHARD, D)
    d_q, d_k, d_v, d_sink, recv_3 = _attn_bwd(proj, o, d_o, tables, attn_sinks, part_3)
    sum_3_bf, own_3 = _pair_add(
        "pair_add_3", part_3, recv_3, index, (3, 4), 1,
        pl.BlockSpec((None, None, None, ROW_SHARD, D), lambda a, q, idx: (a, q, idx[0], 0, 0)),
        pl.BlockSpec((None, None, ROW_SHARD, D), lambda a, q, idx: (a, q, 0, 0)),
        pl.BlockSpec((None, ROW_SHARD, D), lambda a, q, idx: (a, 0, 0)), (3, ROW_SHARD, D),
    )
    d_xr, g_gates, rnn_vec, land_3 = _rnn_bwd(proj, hh, d_hh, keep, w_all, rg_ba, rg_bx, rg_lambda, conv_full, conv_b, sum_3_bf)
    dsegs = (d_q, d_k, d_v, d_ga, d_xr, d_gr, d_ma, d_mr)
    part_g = g_gates.reshape(2, 4, 4, 2, GATE_SHARD, 256)
    (recv_g,) = _exchange("rs_pair_g", [part_g], [lambda ref, peer: ref.at[:, :, :, peer[2]]], [(2, 4, 4, GATE_SHARD, 256)], flips=(1,))
    sum_g_bf, own_g = _pair_add(
        "pair_add_g", part_g, recv_g[0], index, (4,), 0,
        pl.BlockSpec((2, 4, None, None, GATE_SHARD, 256), lambda q, idx: (0, 0, q, idx[0], 0, 0)),
        pl.BlockSpec((2, 4, None, GATE_SHARD, 256), lambda q, idx: (0, 0, q, 0, 0)),
        pl.BlockSpec((2, 4, GATE_SHARD, 256), lambda q, idx: (0, 0, 0, 0)), (2, 4, GATE_SHARD, 256),
    )
    part_w, recv_w, land_g = _bwd_gw(dsegs, h, sum_g_bf)
    rb = 208
    w_rows = pl.BlockSpec((None, rb, D), lambda r, q, idx: (q, r, 0))
    sum_w_bf, own_w = _pair_add(
        "pair_add_w", part_w, recv_w, index, (W_IN_SHARD // rb, 4), 1, w_rows, w_rows,
        pl.BlockSpec((rb, D), lambda r, q, idx: (r, 0)), (W_IN_SHARD, D),
    )
    grad_x, dh_vec, land_w = _bwd_dh(dsegs, w_t_all, x2, dres, mod, norm_g, sum_w_bf)

    pieces = dict(
        b_ada=jnp.concatenate([dh_vec[0:1], dh_vec[1:2], mid_vec[1:2]], axis=1),
        norm_g=dh_vec[2:3],
        attn_sinks=jnp.pad(d_sink[:, :, 0].reshape(1, 16), ((0, 0), (0, 112))),
        conv_b=rnn_vec[3:4],
        rg_ba=rnn_vec[0:1],
        rg_bx=rnn_vec[1:2],
        rg_lambda=rnn_vec[2:3],
        final_g=mid_vec[0:1],
        conv_w=rnn_vec[4:8].reshape(1, CONV_WIDTH * D),
        loss=mid_vec[2:3, 0:128],
    )
    vec = jnp.concatenate([pieces[name] for name, _ in VEC_LAYOUT], axis=1)
    vec = jnp.pad(vec, ((0, 0), (0, VEC_WIDTH - vec.shape[1]))).reshape(VEC_ROWS, 128)
    (vec_all,) = _all_gather_direct("ag_vec", [vec])
    rep_w = [b_ada, norm_g, attn_sinks, conv_b, rg_ba, rg_bx, rg_lambda, final_g[None, :]]
    rep_m = [m_b_ada, m_norm_g, m_attn_sinks, m_conv_b, m_rg_ba, m_rg_bx, m_rg_lambda, m_final_g[None, :]]
    rep_v = [v_b_ada, v_norm_g, v_attn_sinks, v_conv_b, v_rg_ba, v_rg_bx, v_rg_lambda, v_final_g[None, :]]
    loss, reps, ada_res, conv_res = _small_update(
        vec_all.reshape(N_DEV, VEC_WIDTH), c_all.T, me.astype(jnp.int32).reshape(1), rep_w, rep_m, rep_v,
        (w_ada[0], m_w_ada[0], v_w_ada[0]), (conv_w[0], m_conv_w[0], v_conv_w[0]),
    )

    w_in_res = _w_in_update(own_w, land_w, w_in_t, m_w_in[0].T, v_w_in[0].T)
    shard_res = _shard_update(
        own_3, land_3, own_g, land_g,
        (w_attn_proj, w_rnn_proj, w_out, rg_wa, rg_wx),
        (m_w_attn_proj, m_w_rnn_proj, m_w_out, m_rg_wa, m_rg_wx),
        (v_w_attn_proj, v_w_rnn_proj, v_w_out, v_rg_wa, v_rg_wx),
    )

    def by_weight(k):
        b, n, s, cb, ba, bx, lam, fg = reps[k]
        ap, rp, out, wa, wx = shard_res[k]
        return [ada_res[k][None], b, n, w_in_res[k].T[None], s, conv_res[k][None], cb, wa, ba, wx, bx, lam, ap, rp, out, fg[0]]

    return (loss[0, 0], grad_x[None], *by_weight(0), *by_weight(1), *by_weight(2), *by_weight(3))
```

```python
import functools
import math

import jax
import jax.numpy as jnp
import numpy as np
from jax import lax
from jax.experimental import pallas as pl
from jax.experimental.pallas import tpu as pltpu

F32 = jnp.float32
BF16 = jnp.bfloat16

D = 1024
HEAD_DIM = 64
ROT_DIM = 16
ROPE_THETA = 500000.0
WINDOW = 128
LRU_C = 8.0
CONV_WIDTH = 4
NORM_EPS = 1e-6
IN_W = 6656
SEG_Q, SEG_K, SEG_V, SEG_GA, SEG_XR, SEG_GR, SEG_MA, SEG_MR = 0, 1024, 1280, 1536, 2560, 3584, 4608, 5632
N_DEV = 8
W_IN_SHARD = IN_W // N_DEV
ROW_SHARD = D // N_DEV
GATE_SHARD = 256 // N_DEV

ADAM_LR = 0.001
ADAM_B1 = 0.9
ADAM_B2 = 0.999
ADAM_EPS = 1e-08
ADAM_WD = 0.01
ADAM_STEP = 10

VMEM_LIMIT = 60 * 1024 * 1024
MESH = pl.DeviceIdType.MESH
AXES = ("x", "y", "c")


def _dot(a, b):
    return jnp.dot(a, b, preferred_element_type=F32)


def _dot_nt(a, b):
    return lax.dot_general(a, b, (((1,), (1,)), ((), ())), preferred_element_type=F32)


def _dot_tn(a, b):
    return lax.dot_general(a, b, (((0,), (0,)), ((), ())), preferred_element_type=F32)


def _sigmoid(z):
    return 1.0 / (1.0 + jnp.exp(-z))


def _params(n_grid, vmem=VMEM_LIMIT):
    return pltpu.CompilerParams(dimension_semantics=("arbitrary",) * n_grid, vmem_limit_bytes=vmem)


def _position():
    return lax.axis_index("x"), lax.axis_index("y"), lax.axis_index("c")


def _flip(pos, k):
    x, y, c = pos
    return (1 - x if k & 4 else x, 1 - y if k & 2 else y, 1 - c if k & 1 else c)


def _slot(pos):
    return 4 * pos[0] + 2 * pos[1] + pos[2]


def _all_gather_direct(name, arrs):
    n = len(arrs)

    def body(*refs):
        ins, outs = refs[:n], refs[n : 2 * n]
        send_sems, recv_sems, local_sems = refs[2 * n :]
        me = _position()
        mine = [pltpu.make_async_copy(ins[a], outs[a].at[_slot(me)], local_sems.at[a]) for a in range(n)]
        for cp in mine:
            cp.start()

        def copy(a, k, block):
            return pltpu.make_async_remote_copy(
                src_ref=ins[a],
                dst_ref=outs[a].at[_slot(block)],
                send_sem=send_sems.at[a * 7 + k - 1],
                recv_sem=recv_sems.at[a * 7 + k - 1],
                device_id=_flip(me, k),
                device_id_type=MESH,
            )

        sent = [copy(a, k, me) for a in range(n) for k in range(1, N_DEV)]
        for cp in sent:
            cp.start()
        for a in range(n):
            for k in range(1, N_DEV):
                copy(a, k, _flip(me, k)).wait_recv()
        for cp in sent:
            cp.wait_send()
        for cp in mine:
            cp.wait()

    any_spec = pl.BlockSpec(memory_space=pl.ANY)
    return pl.pallas_call(
        body,
        name=name,
        out_shape=[jax.ShapeDtypeStruct((N_DEV,) + a.shape, a.dtype) for a in arrs],
        in_specs=[any_spec] * n,
        out_specs=[any_spec] * n,
        scratch_shapes=[
            pltpu.SemaphoreType.DMA((7 * n,)),
            pltpu.SemaphoreType.DMA((7 * n,)),
            pltpu.SemaphoreType.DMA((n,)),
        ],
    )(*arrs)


def _direct_gather_copies(src_hbm, all_hbm, send_sems, recv_sems, arriving):
    me = _position()
    return [
        pltpu.make_async_remote_copy(
            src_ref=src_hbm,
            dst_ref=all_hbm.at[_slot(_flip(me, k) if arriving else me)],
            send_sem=send_sems.at[k - 1],
            recv_sem=recv_sems.at[k - 1],
            device_id=_flip(me, k),
            device_id_type=MESH,
        )
        for k in range(1, N_DEV)
    ]


CHIP_FLIPS = (2, 4, 6)


def _chip_of(pos):
    return 2 * pos[0] + pos[1]


def _chip_copies(srcs, lands, views, send_sems, recv_sems):
    me = _position()
    copies = []
    for a, (src, land) in enumerate(zip(srcs, lands)):
        for j, k in enumerate(CHIP_FLIPS):
            peer = _flip(me, k)
            copies.append(
                pltpu.make_async_remote_copy(
                    src_ref=views[a](src, peer),
                    dst_ref=land.at[j],
                    send_sem=send_sems.at[len(CHIP_FLIPS) * a + j],
                    recv_sem=recv_sems.at[len(CHIP_FLIPS) * a + j],
                    device_id=peer,
                    device_id_type=MESH,
                )
            )
    return copies


def _exchange(name, arrs, views, out_shapes, flips):
    n = len(arrs)
    nf = len(flips)

    def body(*refs):
        ins, outs = refs[:n], refs[n : 2 * n]
        send_sems, recv_sems = refs[2 * n :]
        me = _position()

        def copy(a, j):
            peer = _flip(me, flips[j])
            return pltpu.make_async_remote_copy(
                src_ref=views[a](ins[a], peer),
                dst_ref=outs[a].at[j],
                send_sem=send_sems.at[a * nf + j],
                recv_sem=recv_sems.at[a * nf + j],
                device_id=peer,
                device_id_type=MESH,
            )

        sent = [copy(a, j) for a in range(n) for j in range(nf)]
        for cp in sent:
            cp.start()
        for cp in sent:
            cp.wait_recv()
        for cp in sent:
            cp.wait_send()

    any_spec = pl.BlockSpec(memory_space=pl.ANY)
    return pl.pallas_call(
        body,
        name=name,
        out_shape=[jax.ShapeDtypeStruct((nf,) + tuple(sh), a.dtype) for sh, a in zip(out_shapes, arrs)],
        in_specs=[any_spec] * n,
        out_specs=[any_spec] * n,
        scratch_shapes=[pltpu.SemaphoreType.DMA((nf * n,)), pltpu.SemaphoreType.DMA((nf * n,))],
    )(*arrs)


def _pair_add(name, own, recv, index, grid, chip_axis, own_spec, recv_spec, mine_spec, mine_shape):
    def body(idx_ref, own_ref, recv_ref, sb_ref, mine_ref):
        total = own_ref[...] + recv_ref[...].astype(F32)
        sb_ref[...] = total.astype(BF16)

        @pl.when(pl.program_id(chip_axis) == idx_ref[1])
        def _():
            mine_ref[...] = total

    return pl.pallas_call(
        body,
        name=name,
        grid_spec=pltpu.PrefetchScalarGridSpec(
            num_scalar_prefetch=1, grid=grid, in_specs=[own_spec, recv_spec], out_specs=[recv_spec, mine_spec]
        ),
        out_shape=[jax.ShapeDtypeStruct(recv.shape, BF16), jax.ShapeDtypeStruct(mine_shape, F32)],
        compiler_params=_params(len(grid)),
    )(index, own, recv)


PACK_GATE = 3 * ROW_SHARD
PACK_ROWS = PACK_GATE + 2 * GATE_SHARD


def _fwd_in(x, c_rows, conv_rows, w_ada, b_cols, norm_g, w_shard, tables, pack, where):
    t = x.shape[0]
    tm = min(1024, t)
    tn = IN_W // 4
    n_i, n_j = t // tm, IN_W // tn
    n_rope = (SEG_V - SEG_Q) // 128

    n_ada = w_ada.shape[1]

    def body(where_ref, x_ref, wada_ref, bcols_ref, ng_ref, cos_ref, s1_ref, s2_ref, shard_hbm, pack_hbm, c_hbm, conv_hbm,
             proj_ref, h_ref, mod_ref, wt_hbm, all_hbm, call_hbm, convall_hbm,
             wbuf, hbuf, cbuf, mcols, mod_all, mod_buf,
             w_send, w_recv, load_sems, local_sem, p_send, p_recv, p_local,
             c_send, c_recv, c_local, v_send, v_recv, v_local, m_send, m_recv):
        j = pl.program_id(0)
        i = pl.program_id(1)
        me = _position()
        sibling = _flip(me, 1)
        chip = where_ref[0]
        slot = where_ref[1]

        def modulation():
            own_c = pltpu.make_async_copy(c_hbm, call_hbm.at[_slot(me)], c_local)
            own_v = pltpu.make_async_copy(conv_hbm, convall_hbm.at[_slot(me)], v_local)
            own_c.start()
            own_v.start()
            for cp in _direct_gather_copies(c_hbm, call_hbm, c_send, c_recv, arriving=False):
                cp.start()
            for cp in _direct_gather_copies(conv_hbm, convall_hbm, v_send, v_recv, arriving=False):
                cp.start()
            for cp in _direct_gather_copies(c_hbm, call_hbm, c_send, c_recv, arriving=True):
                cp.wait_recv()
            own_c.wait()
            fetch = pltpu.make_async_copy(call_hbm, cbuf, c_local)
            fetch.start()
            fetch.wait()
            mcols[...] = _dot(cbuf[:, 0, :].astype(BF16), wada_ref[...].astype(BF16)) + bcols_ref[...]
            mod_all[slot] = mcols[...]
            swaps = lambda arriving: [
                pltpu.make_async_remote_copy(
                    src_ref=mcols,
                    dst_ref=mod_all.at[_slot(_flip(me, k) if arriving else me)],
                    send_sem=m_send.at[k - 1],
                    recv_sem=m_recv.at[k - 1],
                    device_id=_flip(me, k),
                    device_id_type=MESH,
                )
                for k in range(1, N_DEV)
            ]
            for cp in swaps(False):
                cp.start()
            for cp in swaps(True):
                cp.wait_recv()
            mine = lax.broadcasted_iota(jnp.int32, (N_DEV, n_ada), 0) == slot
            for s in range(N_DEV):
                row = jnp.sum(jnp.where(mine, mod_all[s], 0.0), axis=0, keepdims=True)
                mod_buf[:, n_ada * s : n_ada * (s + 1)] = row
            for cp in swaps(False):
                cp.wait_send()
            for cp in _direct_gather_copies(c_hbm, call_hbm, c_send, c_recv, arriving=False):
                cp.wait_send()
            for cp in _direct_gather_copies(conv_hbm, convall_hbm, v_send, v_recv, arriving=True):
                cp.wait_recv()
            for cp in _direct_gather_copies(conv_hbm, convall_hbm, v_send, v_recv, arriving=False):
                cp.wait_send()
            own_v.wait()

        def shard_copy(k, block, to, src=None):
            return pltpu.make_async_remote_copy(
                src_ref=wt_hbm.at[_slot(block)] if src is None else src,
                dst_ref=wt_hbm.at[_slot(block)],
                send_sem=w_send.at[k],
                recv_sem=w_recv.at[k],
                device_id=to,
                device_id_type=MESH,
            )

        own_shard = pltpu.make_async_copy(shard_hbm, wt_hbm.at[_slot(me)], local_sem)
        own_pack = pltpu.make_async_copy(pack_hbm, all_hbm.at[_slot(me)], p_local)

        def load_tile(q):
            loads = [pltpu.make_async_copy(wt_hbm.at[2 * q + c], wbuf.at[pl.ds(W_IN_SHARD * c, W_IN_SHARD), :], load_sems.at[c]) for c in range(2)]
            for cp in loads:
                cp.start()
            for cp in loads:
                cp.wait()

        @pl.when((j == 0) & (i == 0))
        def _():
            modulation()
            mod_ref[...] = mod_buf[...]
            own_shard.start()
            shard_copy(0, me, sibling, src=shard_hbm).start()
            for rel in range(1, 4):
                shard_copy(rel, me, _flip(me, 2 * rel), src=shard_hbm).start()
            own_pack.start()
            for cp in _direct_gather_copies(pack_hbm, all_hbm, p_send, p_recv, arriving=False):
                cp.start()
            own_shard.wait()
            shard_copy(0, sibling, me).wait_recv()
            load_tile(chip)

        for rel in range(1, 4):

            @pl.when((j == rel) & (i == 0))
            def _(rel=rel):
                there = _flip(me, 2 * rel)
                shard_copy(rel, there, me).wait_recv()
                shard_copy(3 + rel, there, sibling).start()
                shard_copy(3 + rel, _flip(sibling, 2 * rel), me).wait_recv()
                load_tile(chip ^ rel)

        rows = pl.ds(pl.multiple_of(i * tm, tm), tm)

        @pl.when(j == 0)
        def _():
            xv = x_ref[...]
            r = lax.rsqrt(jnp.mean(xv * xv, axis=-1, keepdims=True) + NORM_EPS)
            shift = mod_buf[:, 0:D]
            scale = mod_buf[:, D : 2 * D]
            hv = (((xv * r) * ng_ref[...]) * (1.0 + scale) + shift).astype(BF16)
            hbuf[rows, :] = hv
            h_ref[...] = hv

        proj_ref[...] = _dot_nt(hbuf[rows, :], wbuf[...])

        @pl.when((chip ^ j) == 0)
        def _():
            cos, s1, s2 = cos_ref[...], s1_ref[...], s2_ref[...]
            for g in range(n_rope):
                cols = slice(128 * g, 128 * (g + 1))
                proj_ref[:, cols] = _rope(proj_ref[:, cols], cos, s1, s2)

        @pl.when((j == n_j - 1) & (i == n_i - 1))
        def _():
            shard_copy(0, me, sibling, src=shard_hbm).wait_send()
            for rel in range(1, 4):
                shard_copy(rel, me, _flip(me, 2 * rel), src=shard_hbm).wait_send()
                shard_copy(3 + rel, _flip(me, 2 * rel), sibling).wait_send()
            for cp in _direct_gather_copies(pack_hbm, all_hbm, p_send, p_recv, arriving=True):
                cp.wait_recv()
            for cp in _direct_gather_copies(pack_hbm, all_hbm, p_send, p_recv, arriving=False):
                cp.wait_send()
            own_pack.wait()

    tab = pl.BlockSpec((tm, 128), lambda j, i, c: (i, 0))
    any_spec = pl.BlockSpec(memory_space=pl.ANY)
    return pl.pallas_call(
        body,
        name="fwd_in",
        grid_spec=pltpu.PrefetchScalarGridSpec(
            num_scalar_prefetch=1,
            grid=(n_j, n_i),
            in_specs=[
                pl.BlockSpec((tm, D), lambda j, i, c: (jnp.where(j == 0, i, 0), 0)),
                pl.BlockSpec(w_ada.shape, lambda j, i, c: (0, 0)),
                pl.BlockSpec((1, n_ada), lambda j, i, c: (0, 0)),
                pl.BlockSpec((1, D), lambda j, i, c: (0, 0)),
                tab, tab, tab, any_spec, any_spec, any_spec, any_spec,
            ],
            out_specs=[pl.BlockSpec((tm, tn), lambda j, i, c: (i, c[0] ^ j)),
                       pl.BlockSpec((tm, D), lambda j, i, c: (jnp.where(j == 0, i, n_i - 1), 0)),
                       pl.BlockSpec((1, 3 * D), lambda j, i, c: (0, 0)),
                       any_spec, any_spec, any_spec, any_spec],
            scratch_shapes=[
                pltpu.VMEM((tn, D), BF16),
                pltpu.VMEM((t, D), BF16),
                pltpu.VMEM((N_DEV,) + c_rows.shape, F32),
                pltpu.VMEM((N_DEV, n_ada), F32),
                pltpu.VMEM((N_DEV, N_DEV, n_ada), F32),
                pltpu.VMEM((1, 3 * D), F32),
                pltpu.SemaphoreType.DMA((N_DEV - 1,)),
                pltpu.SemaphoreType.DMA((N_DEV - 1,)),
                pltpu.SemaphoreType.DMA((2,)),
                pltpu.SemaphoreType.DMA,
                pltpu.SemaphoreType.DMA((N_DEV - 1,)),
                pltpu.SemaphoreType.DMA((N_DEV - 1,)),
                pltpu.SemaphoreType.DMA,
                pltpu.SemaphoreType.DMA((N_DEV - 1,)),
                pltpu.SemaphoreType.DMA((N_DEV - 1,)),
                pltpu.SemaphoreType.DMA,
                pltpu.SemaphoreType.DMA((N_DEV - 1,)),
                pltpu.SemaphoreType.DMA((N_DEV - 1,)),
                pltpu.SemaphoreType.DMA,
                pltpu.SemaphoreType.DMA((N_DEV - 1,)),
                pltpu.SemaphoreType.DMA((N_DEV - 1,)),
            ],
        ),
        out_shape=[jax.ShapeDtypeStruct((t, IN_W), F32), jax.ShapeDtypeStruct((t, D), BF16),
                   jax.ShapeDtypeStruct((1, 3 * D), F32),
                   jax.ShapeDtypeStruct((N_DEV,) + w_shard.shape, w_shard.dtype),
                   jax.ShapeDtypeStruct((N_DEV,) + pack.shape, pack.dtype),
                   jax.ShapeDtypeStruct((N_DEV,) + c_rows.shape, F32),
                   jax.ShapeDtypeStruct((N_DEV,) + conv_rows.shape, F32)],
        compiler_params=_params(2),
    )(where, x, w_ada, b_cols, norm_g, *tables, w_shard, pack, c_rows, conv_rows)


def _rope_tables(positions):
    half = ROT_DIM // 2
    inv_freq = ROPE_THETA ** (-jnp.arange(0, ROT_DIM, 2, dtype=F32) / ROT_DIM)
    pos = positions.astype(F32)
    lane = np.arange(128) % HEAD_DIM
    freq_lane = jnp.where(lane < ROT_DIM, jnp.tile(inv_freq, 128 // half), 0.0)
    ang = pos[:, None] * freq_lane[None, :]
    sin = jnp.sin(ang)
    lane_tables = (jnp.cos(ang), jnp.where(lane < half, -sin, 0.0), jnp.where((lane >= half) & (lane < ROT_DIM), sin, 0.0))
    ang_t = inv_freq[:, None] * pos[None, :]
    return lane_tables, (jnp.cos(ang_t), jnp.sin(ang_t))


def _rope(tv, cos, s1, s2):
    cols = []
    for g in range(tv.shape[1] // 128):
        blk = tv[:, 128 * g : 128 * (g + 1)]
        cols.append(blk * cos + pltpu.roll(blk, 128 - 8, 1) * s1 + pltpu.roll(blk, 8, 1) * s2)
    return cols[0] if len(cols) == 1 else jnp.concatenate(cols, axis=1)


def _rope_transposed(gv, cos, s1, s2):
    cols = []
    for g in range(gv.shape[1] // 128):
        blk = gv[:, 128 * g : 128 * (g + 1)]
        cols.append(blk * cos + pltpu.roll(blk * s1, 8, 1) + pltpu.roll(blk * s2, 128 - 8, 1))
    return cols[0] if len(cols) == 1 else jnp.concatenate(cols, axis=1)


def _lane_lo():
    return lax.broadcasted_iota(jnp.int32, (1, 128), 1) < HEAD_DIM


def _row_lo():
    return lax.broadcasted_iota(jnp.int32, (128, 1), 0) < HEAD_DIM


def _split_heads(tile, j):
    lo = _lane_lo()
    parts = []
    for g in (2 * j, 2 * j + 1):
        blk = tile[:, 128 * g : 128 * (g + 1)]
        parts += [jnp.where(lo, blk, 0.0), jnp.where(lo, 0.0, blk)]
    return jnp.concatenate(parts, axis=0)


def _dup_lanes(kv, j):
    rolled = pltpu.roll(kv, HEAD_DIM, 1)
    lo = _lane_lo()
    return jnp.where(lo, kv, rolled) if j == 0 else jnp.where(lo, rolled, kv)


def _fold_lanes(acc):
    return acc + pltpu.roll(acc, HEAD_DIM, 1)


def _fill_bias_t(bias_ref):
    kj = lax.broadcasted_iota(jnp.int32, (256, 128), 0)
    qi = lax.broadcasted_iota(jnp.int32, (256, 128), 1)
    diff = qi + WINDOW - kj
    band = (diff >= 0) & (diff < WINDOW)
    bias_ref[0] = jnp.where(band, 0.0, -1e30)
    bias_ref[1] = jnp.where(band & (kj >= 128), 0.0, -1e30)


def _sink_row(sink_ref, pair, j):
    return jnp.concatenate([jnp.full((1, 128), sink_ref[0, 8 * pair + 4 * j + h], F32) for h in range(4)], axis=1)


def _softmax_t(q4, kw, bias4, sink):
    s = _dot_nt(kw, q4) + bias4
    m = jnp.maximum(jnp.max(s, axis=0, keepdims=True), sink)
    p = jnp.exp(s - m)
    e_sink = jnp.exp(sink - m)
    denom = jnp.sum(p, axis=0, keepdims=True) + e_sink
    return p, e_sink, denom


def _attn_specs(t, tq):
    return dict(
        q=pl.BlockSpec((tq, 512), lambda p, i: (i, p)),
        k=pl.BlockSpec((t, 128), lambda p, i: (0, SEG_K // 128 + p)),
        v=pl.BlockSpec((t, 128), lambda p, i: (0, SEG_V // 128 + p)),
        sink=pl.BlockSpec(memory_space=pltpu.SMEM),
        act=pl.BlockSpec((tq, 512), lambda p, i: (i, p)),
    )


def _attn_fwd(proj, sinks):
    t = proj.shape[0]
    tq = min(512, t)
    nb = tq // 128
    n_tiles = t // tq
    sp = _attn_specs(t, tq)

    def body(q_ref, k_ref, v_ref, sink_ref, o_ref, kd, vdt, bias_ref):
        pair = pl.program_id(0)
        i = pl.program_id(1)

        @pl.when(i == 0)
        def _():
            kv, vv = k_ref[...], v_ref[...]
            for j in range(2):
                kd[j, 0:128, :] = jnp.zeros((128, 128), BF16)
                kd[j, 128:, :] = _dup_lanes(kv, j).astype(BF16)
                vdt[j, :, 0:128] = jnp.zeros((128, 128), BF16)
                vdt[j, :, 128:] = _dup_lanes(vv, j).T.astype(BF16)
            _fill_bias_t(bias_ref)

        row_lo = _row_lo()

        def block(b, carry):
            r0 = pl.multiple_of(b * 128, 128)
            gb = i * nb + b
            keys = pl.ds(pl.multiple_of(gb * 128, 128), 256)
            rows = pl.ds(r0, 128)
            qt = q_ref[rows, :] * (1.0 / math.sqrt(HEAD_DIM))
            bias = bias_ref[jnp.where(gb == 0, 1, 0)]
            bias4 = jnp.concatenate([bias] * 4, axis=1)
            cols = []
            for j in range(2):
                q4 = _split_heads(qt, j).astype(BF16)
                p, _, denom = _softmax_t(q4, kd[j, keys, :], bias4, _sink_row(sink_ref, pair, j))
                ot = _dot(vdt[j, :, keys], p.astype(BF16)) * (1.0 / denom)
                for gi in range(2):
                    z = jnp.where(row_lo, ot[:, 256 * gi : 256 * gi + 128], ot[:, 256 * gi + 128 : 256 * gi + 256])
                    cols.append(z.T)
            o_ref[rows, :] = jnp.concatenate(cols, axis=1).astype(BF16)
            return carry

        lax.fori_loop(0, nb, block, 0, unroll=True)

    return pl.pallas_call(
        body,
        name="attn_fwd",
        grid=(2, n_tiles),
        in_specs=[sp["q"], sp["k"], sp["v"], sp["sink"]],
        out_specs=sp["act"],
        out_shape=jax.ShapeDtypeStruct((t, D), BF16),
        scratch_shapes=[pltpu.VMEM((2, t + 128, 128), BF16), pltpu.VMEM((2, 128, t + 128), BF16), pltpu.VMEM((2, 256, 128), F32)],
        compiler_params=_params(2),
    )(proj, proj, proj, sinks)


def _attn_bwd(proj, o, d_o, tables, sinks, part_3):
    t = proj.shape[0]
    tq = min(512, t)
    nb = tq // 128
    n_tiles = t // tq
    (cos, s1, s2), (cos_t, sin_t) = tables
    sp = _attn_specs(t, tq)
    scale = 1.0 / math.sqrt(HEAD_DIM)
    half = ROT_DIM // 2

    def unrope_rows(z, c, s):
        out = []
        for base in (0, HEAD_DIM):
            g0, g1 = z[base : base + half], z[base + half : base + ROT_DIM]
            out += [g0 * c + g1 * s, g1 * c - g0 * s, z[base + ROT_DIM : base + HEAD_DIM]]
        return jnp.concatenate(out, axis=0)

    def swap(part_hbm, recv_hbm, send_sem, recv_sem):
        sibling = _flip(_position(), 1)
        return pltpu.make_async_remote_copy(
            src_ref=part_hbm.at[:, :, sibling[2]], dst_ref=recv_hbm, send_sem=send_sem, recv_sem=recv_sem,
            device_id=sibling, device_id_type=MESH,
        )

    def body(q_ref, k_ref, v_ref, o_ref, do_ref, ct_ref, st_ref, ck, s1k, s2k, sink_ref, part_hbm,
             dq_ref, dk_ref, dv_ref, dsink_ref, recv_hbm, kd, kdt, vd, bias_ref, dkacc, dvacc, send_sem, recv_sem):
        pair = pl.program_id(0)
        i = pl.program_id(1)

        @pl.when((pair == 0) & (i == 0))
        def _():
            swap(part_hbm, recv_hbm, send_sem, recv_sem).start()

        @pl.when((pair == 1) & (i == n_tiles - 1))
        def _():
            swap(part_hbm, recv_hbm, send_sem, recv_sem).wait()

        @pl.when(i == 0)
        def _():
            kv, vv = k_ref[...], v_ref[...]
            for j in range(2):
                kdup = _dup_lanes(kv, j)
                kd[j, 0:128, :] = jnp.zeros((128, 128), BF16)
                kd[j, 128:, :] = kdup.astype(BF16)
                kdt[j, :, 0:128] = jnp.zeros((128, 128), BF16)
                kdt[j, :, 128:] = kdup.T.astype(BF16)
                vd[j, 0:128, :] = jnp.zeros((128, 128), BF16)
                vd[j, 128:, :] = _dup_lanes(vv, j).astype(BF16)
            _fill_bias_t(bias_ref)
            dkacc[...] = jnp.zeros_like(dkacc)
            dvacc[...] = jnp.zeros_like(dvacc)
            dsink_ref[...] = jnp.zeros_like(dsink_ref)

        row_lo = _row_lo()
        ones = jnp.ones((8, 128), BF16)

        def block(b, carry):
            r0 = pl.multiple_of(b * 128, 128)
            gb = i * nb + b
            keys = pl.ds(pl.multiple_of(gb * 128, 128), 256)
            rows = pl.ds(r0, 128)
            qt = q_ref[rows, :] * scale
            dob = do_ref[rows, :].astype(F32)
            prod = dob * o_ref[rows, :].astype(F32)
            ct, st = ct_ref[:, rows], st_ref[:, rows]
            bias = bias_ref[jnp.where(gb == 0, 1, 0)]
            bias4 = jnp.concatenate([bias] * 4, axis=1)
            cols = []
            for j in range(2):
                q4 = _split_heads(qt, j).astype(BF16)
                kw = kd[j, keys, :]
                p, e_sink, denom = _softmax_t(q4, kw, bias4, _sink_row(sink_ref, pair, j))
                inv = 1.0 / denom
                prob = p * inv
                do4 = _split_heads(dob, j).astype(BF16)
                delta = _dot_nt(ones, _split_heads(prod, j).astype(BF16))[0:1, :]
                dp = _dot_nt(vd[j, keys, :], do4)
                ds = (prob * (dp - delta)).astype(BF16)
                dkacc[j, keys, :] = dkacc[j, keys, :] + _dot(ds, q4)
                dvacc[j, keys, :] = dvacc[j, keys, :] + _dot(prob.astype(BF16), do4)
                dqt = _dot(kdt[j, :, keys], ds) * scale
                dsink = -(e_sink * inv) * delta
                for h in range(4):
                    row = 4 * j + h
                    total = jnp.sum(dsink[:, 128 * h : 128 * (h + 1)], axis=1, keepdims=True)
                    dsink_ref[0, row : row + 1, :] = dsink_ref[0, row : row + 1, :] + jnp.broadcast_to(total, (1, 128))
                for gi in range(2):
                    z = jnp.where(row_lo, dqt[:, 256 * gi : 256 * gi + 128], dqt[:, 256 * gi + 128 : 256 * gi + 256])
                    cols.append(unrope_rows(z, ct, st).T)
            dq_ref[rows, :] = jnp.concatenate(cols, axis=1).astype(BF16)
            return carry

        lax.fori_loop(0, nb, block, 0, unroll=True)

        @pl.when(i == n_tiles - 1)
        def _():
            lo = _lane_lo()
            dk = jnp.where(lo, _fold_lanes(dkacc[0, 128:, :]), _fold_lanes(dkacc[1, 128:, :]))
            dk_ref[...] = _rope_transposed(dk, ck[...], s1k[...], s2k[...]).astype(BF16)
            dv_ref[...] = jnp.where(lo, _fold_lanes(dvacc[0, 128:, :]), _fold_lanes(dvacc[1, 128:, :])).astype(BF16)

    kv_out = pl.BlockSpec((t, 128), lambda p, i: (0, p))
    tab_t = pl.BlockSpec((half, tq), lambda p, i: (0, i))
    tab_k = pl.BlockSpec((t, 128), lambda p, i: (0, 0))
    return pl.pallas_call(
        body,
        name="attn_bwd",
        grid=(2, n_tiles),
        in_specs=[sp["q"], sp["k"], sp["v"], sp["act"], sp["act"], tab_t, tab_t, tab_k, tab_k, tab_k, sp["sink"],
                  pl.BlockSpec(memory_space=pl.ANY)],
        out_specs=[sp["act"], kv_out, kv_out, pl.BlockSpec((1, 8, 128), lambda p, i: (p, 0, 0)), pl.BlockSpec(memory_space=pl.ANY)],
        out_shape=[
            jax.ShapeDtypeStruct((t, D), BF16),
            jax.ShapeDtypeStruct((t, 256), BF16),
            jax.ShapeDtypeStruct((t, 256), BF16),
            jax.ShapeDtypeStruct((2, 8, 128), F32),
            jax.ShapeDtypeStruct((3, 4, ROW_SHARD, D), F32),
        ],
        scratch_shapes=[pltpu.VMEM((2, t + 128, 128), BF16), pltpu.VMEM((2, 128, t + 128), BF16), pltpu.VMEM((2, t + 128, 128), BF16),
                        pltpu.VMEM((2, 256, 128), F32), pltpu.VMEM((2, t + 128, 128), F32), pltpu.VMEM((2, t + 128, 128), F32),
                        pltpu.SemaphoreType.DMA, pltpu.SemaphoreType.DMA],
        compiler_params=_params(2),
    )(proj, proj, proj, o, d_o, cos_t, sin_t, cos, s1, s2, sinks, part_3)


HALF = 512


def _softplus(y):
    u = jnp.exp(-jnp.abs(y))
    w = 1.0 + u
    log1p = jnp.where(w == 1.0, u, jnp.log(w) * (u / jnp.where(w == 1.0, 1.0, w - 1.0)))
    return jnp.maximum(y, 0.0) + log1p


def _gate_block(w_ref, blk):
    return w_ref[:, :, 256 * blk : 256 * (blk + 1)].reshape(256, 256)


def _gate_matmul(act_bf, w_ref):
    return jnp.concatenate([_dot(act_bf[:, 256 * blk : 256 * (blk + 1)], _gate_block(w_ref, blk)) for blk in range(2)], axis=1)


def _gate_matmul_t(grad_bf, w_ref):
    return jnp.concatenate([_dot_nt(grad_bf[:, 256 * blk : 256 * (blk + 1)], _gate_block(w_ref, blk)) for blk in range(2)], axis=1)


def _conv_and_gates(xe_ref, tm, cw_ref, cb_ref, wa_ref, wx_ref, ba_ref, bx_ref, lam_ref, keep):
    xc = cb_ref[...] + sum(cw_ref[k : k + 1, :] * xe_ref[pl.ds(5 + k, tm), :] for k in range(CONV_WIDTH))
    xc_bf = xc.astype(BF16)
    r = _sigmoid(_gate_matmul(xc_bf, wa_ref) + ba_ref[...])
    ig = _sigmoid(_gate_matmul(xc_bf, wx_ref) + bx_ref[...])
    sp = _softplus(-lam_ref[...])
    log_a = -LRU_C * r * sp
    tanh = jnp.tanh(-log_a)
    mult = jnp.sqrt(2.0 * tanh / (1.0 + tanh))
    a = jnp.where(keep, jnp.exp(log_a), 0.0)
    mult = jnp.where(keep, mult, 1.0)
    return xc, xc_bf, r, ig, sp, a, mult


def _scan_rows(a_ref, b_ref, h_ref, carry_ref, tm, reverse):
    width = a_ref.shape[1]
    row = lax.broadcasted_iota(jnp.int32, (8, width), 0)

    def group(g, carry):
        gg = tm // 8 - 1 - g if reverse else g
        r0 = pl.multiple_of(gg * 8, 8)
        a = a_ref[pl.ds(r0, 8), :]
        b = b_ref[pl.ds(r0, 8), :]
        for s in (1, 2, 4):
            shift = 8 - s if reverse else s
            a_sh = pltpu.roll(a, shift, 0)
            b_sh = pltpu.roll(b, shift, 0)
            ok = (row < 8 - s) if reverse else (row >= s)
            b = jnp.where(ok, a * b_sh + b, b)
            a = jnp.where(ok, a * a_sh, a)
        h = a * carry + b
        h_ref[pl.ds(r0, 8), :] = h
        last = h[0:1, :] if reverse else h[7:8, :]
        return jnp.broadcast_to(last, (8, width))

    carry_ref[...] = lax.fori_loop(0, tm // 8, group, carry_ref[...])


def _rnn_specs(t, tm, order):
    n = t // tm
    return dict(
        tile=pl.BlockSpec((tm, HALF), lambda ch, i: (order(i, n), ch)),
        before=lambda col0: pl.BlockSpec((8, HALF), lambda ch, i: (jnp.maximum(order(i, n) * (tm // 8) - 1, 0), col0 + ch)),
        proj_tile=lambda col0: pl.BlockSpec((tm, HALF), lambda ch, i: (order(i, n), col0 + ch)),
        keep=pl.BlockSpec((tm, 1), lambda ch, i: (order(i, n), 0)),
        gates=lambda which: pl.BlockSpec((N_DEV, GATE_SHARD, HALF), lambda ch, i: (0, PACK_GATE // GATE_SHARD + which, ch)),
        gate_grads=pl.BlockSpec((2, 2, 256, 256), lambda ch, i: (0, ch, 0, 0)),
        vec=pl.BlockSpec((1, HALF), lambda ch, i: (0, ch)),
        conv=pl.BlockSpec((CONV_WIDTH, HALF), lambda ch, i: (0, ch)),
    )


def _rnn_fwd(proj, keep, w_all, rg_ba, rg_bx, rg_lambda, conv_w, conv_b):
    t = proj.shape[0]
    tm = min(512, t)
    sp = _rnn_specs(t, tm, lambda i, n: i)

    def body(xr_ref, xb_ref, keep_ref, wa_ref, wx_ref, ba_ref, bx_ref, lam_ref, cw_ref, cb_ref, hh_ref, xe, a_buf, b_buf, carry):
        i = pl.program_id(1)

        @pl.when(i == 0)
        def _():
            carry[...] = jnp.zeros_like(carry)

        xe[0:8, :] = jnp.where(i > 0, xb_ref[...], 0.0)
        xe[8:, :] = xr_ref[...]
        keep_rows = keep_ref[...] != 0.0
        xc, _, _, ig, _, a, mult = _conv_and_gates(xe, tm, cw_ref, cb_ref, wa_ref, wx_ref, ba_ref, bx_ref, lam_ref, keep_rows)
        a_buf[...] = a
        b_buf[...] = mult * (ig * xc)
        _scan_rows(a_buf, b_buf, hh_ref, carry, tm, reverse=False)

    return pl.pallas_call(
        body,
        name="rnn_fwd",
        grid=(2, t // tm),
        in_specs=[sp["proj_tile"](SEG_XR // HALF), sp["before"](SEG_XR // HALF), sp["keep"], sp["gates"](0), sp["gates"](1),
                  sp["vec"], sp["vec"], sp["vec"], sp["conv"], sp["vec"]],
        out_specs=sp["tile"],
        out_shape=jax.ShapeDtypeStruct((t, D), F32),
        scratch_shapes=[pltpu.VMEM((tm + 8, HALF), F32), pltpu.VMEM((tm, HALF), F32), pltpu.VMEM((tm, HALF), F32), pltpu.VMEM((8, HALF), F32)],
        compiler_params=_params(2),
    )(proj, proj, keep, w_all, w_all, rg_ba, rg_bx, rg_lambda, conv_w, conv_b)


def _rnn_bwd(proj, hh, d_hh, keep, w_all, rg_ba, rg_bx, rg_lambda, conv_w, conv_b, sum_3_bf):
    t = proj.shape[0]
    tm = min(512, t)
    n_tiles = t // tm
    sp = _rnn_specs(t, tm, lambda i, n: n - 1 - i)

    def body(xr_ref, xb_ref, hh_ref, hb_ref, dhh_ref, keep_ref, wa_ref, wx_ref, ba_ref, bx_ref, lam_ref, cw_ref, cb_ref,
             s3_hbm, dxr_ref, gw_ref, gvec_ref, l3_hbm, xe, he, ae, c_buf, g_buf, dxe, carry, a_next, send_sems, recv_sems):
        ch = pl.program_id(0)
        i = pl.program_id(1)
        first_tile = i == n_tiles - 1
        views = [lambda ref, peer: ref.at[:, _chip_of(peer)]]

        @pl.when((ch == 0) & (i == 0))
        def _():
            for cp in _chip_copies([s3_hbm], [l3_hbm], views, send_sems, recv_sems):
                cp.start()

        @pl.when((ch == 1) & (i == n_tiles - 1))
        def _():
            for cp in _chip_copies([s3_hbm], [l3_hbm], views, send_sems, recv_sems):
                cp.wait_recv()
                cp.wait_send()

        @pl.when(i == 0)
        def _():
            carry[...] = jnp.zeros_like(carry)
            a_next[...] = jnp.zeros_like(a_next)
            dxe[tm:, :] = jnp.zeros((8, HALF), F32)
            gw_ref[...] = jnp.zeros_like(gw_ref)
            gvec_ref[...] = jnp.zeros_like(gvec_ref)

        xe[0:8, :] = jnp.where(first_tile, 0.0, xb_ref[...])
        xe[8:, :] = xr_ref[...]
        he[0:8, :] = jnp.where(first_tile, 0.0, hb_ref[...])
        he[8:, :] = hh_ref[...]
        keep_rows = keep_ref[...] != 0.0
        xc, xc_bf, r, ig, sp_, a, mult = _conv_and_gates(xe, tm, cw_ref, cb_ref, wa_ref, wx_ref, ba_ref, bx_ref, lam_ref, keep_rows)

        ae[0:tm, :] = a
        ae[tm:, :] = a_next[...]
        a_next[...] = jnp.broadcast_to(a[0:1, :], (8, HALF))
        c_buf[...] = ae[pl.ds(1, tm), :]
        g_buf[...] = dhh_ref[...]
        _scan_rows(c_buf, g_buf, g_buf, carry, tm, reverse=True)
        g = g_buf[...]

        h_prev = he[pl.ds(7, tm), :]
        bb = ig * xc
        d_ig = g * mult * xc
        d_xc = g * mult * ig
        d_log_a = jnp.where(keep_rows, g * h_prev * a - (g * bb) * (a * a) / mult, 0.0)
        d_r = d_log_a * (-LRU_C * sp_)
        d_sp = jnp.sum(d_log_a * (-LRU_C * r), axis=0, keepdims=True)
        d_zr = d_r * r * (1.0 - r)
        d_zi = d_ig * ig * (1.0 - ig)
        d_zr_bf = d_zr.astype(BF16)
        d_zi_bf = d_zi.astype(BF16)
        d_xc = d_xc + _gate_matmul_t(d_zr_bf, wa_ref) + _gate_matmul_t(d_zi_bf, wx_ref)
        for blk in range(2):
            cols = slice(256 * blk, 256 * (blk + 1))
            gw_ref[0, blk] = gw_ref[0, blk] + _dot_tn(xc_bf[:, cols], d_zr_bf[:, cols])
            gw_ref[1, blk] = gw_ref[1, blk] + _dot_tn(xc_bf[:, cols], d_zi_bf[:, cols])

        dxe[0:tm, :] = d_xc
        shifted = [dxe[pl.ds(3 - k, tm), :] for k in range(CONV_WIDTH)]
        dxr = sum(cw_ref[k : k + 1, :] * shifted[k] for k in range(CONV_WIDTH))
        dxr_ref[...] = dxr.astype(BF16)
        dxe[tm:, :] = d_xc[0:8, :]

        lam = lam_ref[...]
        xr_tile = xr_ref[...]
        sums = [
            jnp.sum(d_zr, axis=0, keepdims=True),
            jnp.sum(d_zi, axis=0, keepdims=True),
            d_sp * (-_sigmoid(-lam)),
            jnp.sum(d_xc, axis=0, keepdims=True),
        ] + [jnp.sum(xr_tile * shifted[k], axis=0, keepdims=True) for k in range(CONV_WIDTH)]
        gvec_ref[...] = gvec_ref[...] + jnp.concatenate(sums, axis=0)

    return pl.pallas_call(
        body,
        name="rnn_bwd",
        grid=(2, n_tiles),
        in_specs=[
            sp["proj_tile"](SEG_XR // HALF), sp["before"](SEG_XR // HALF), sp["tile"], sp["before"](0), sp["tile"], sp["keep"],
            sp["gates"](0), sp["gates"](1), sp["vec"], sp["vec"], sp["vec"], sp["conv"], sp["vec"], pl.BlockSpec(memory_space=pl.ANY),
        ],
        out_specs=[sp["tile"], sp["gate_grads"], pl.BlockSpec((8, HALF), lambda ch, i: (0, ch)), pl.BlockSpec(memory_space=pl.ANY)],
        out_shape=[
            jax.ShapeDtypeStruct((t, D), BF16),
            jax.ShapeDtypeStruct((2, 4, 256, 256), F32),
            jax.ShapeDtypeStruct((8, D), F32),
            jax.ShapeDtypeStruct((len(CHIP_FLIPS), 3, ROW_SHARD, D), BF16),
        ],
        scratch_shapes=[
            pltpu.VMEM((tm + 8, HALF), F32),
            pltpu.VMEM((tm + 8, HALF), F32),
            pltpu.VMEM((tm + 8, HALF), F32),
            pltpu.VMEM((tm, HALF), F32),
            pltpu.VMEM((tm, HALF), F32),
            pltpu.VMEM((tm + 8, HALF), F32),
            pltpu.VMEM((8, HALF), F32),
            pltpu.VMEM((8, HALF), F32),
            pltpu.SemaphoreType.DMA((len(CHIP_FLIPS),)),
            pltpu.SemaphoreType.DMA((len(CHIP_FLIPS),)),
        ],
        compiler_params=_params(2),
    )(proj, proj, hh, hh, d_hh, keep, w_all, w_all, rg_ba, rg_bx, rg_lambda, conv_w, conv_b, sum_3_bf)


def _mid(x, target, o, hh, proj, w_all, gate, final_g):
    t = x.shape[0]
    tm = min(256, t)
    n_tiles = t // tm
    full = pl.BlockSpec((tm, D), lambda i: (i, 0))
    half = lambda col: pl.BlockSpec((tm, 512), lambda i: (i, col))
    vec = pl.BlockSpec((1, D), lambda i: (0, 0))
    any_spec = pl.BlockSpec(memory_space=pl.ANY)

    def body(x_ref, tg_ref, o_ref, hh_ref, ga0, ga1, gr0, gr1, ma0, ma1, mr0, mr1, w_hbm, gate_ref, fg_ref,
             dres_ref, do_ref, dhh_ref, dga_ref, dgr_ref, dma_ref, dmr_ref, gw_hbm, vec_ref, w, gw, sem, w_sems):
        i = pl.program_id(0)

        @pl.when(i == 0)
        def _():
            loads = [
                pltpu.make_async_copy(
                    w_hbm.at[s, pl.ds(ROW_SHARD * k, ROW_SHARD), :], w.at[k, pl.ds(ROW_SHARD * s, ROW_SHARD), :], w_sems.at[k * N_DEV + s]
                )
                for k in range(3)
                for s in range(N_DEV)
            ]
            for cp in loads:
                cp.start()
            for cp in loads:
                cp.wait()
            gw[...] = jnp.zeros_like(gw)
            vec_ref[...] = jnp.zeros_like(vec_ref)

        cat = lambda a, b: jnp.concatenate([a[...], b[...]], axis=1)
        ga, gr, ma, mr = cat(ga0, ga1), cat(gr0, gr1), cat(ma0, ma1), cat(mr0, mr1)
        ov, hv, gate_v, fg = o_ref[...].astype(F32), hh_ref[...], gate_ref[...], fg_ref[...]
        sg_a = _sigmoid(ga)
        sg_r = _sigmoid(gr)
        silu_a = ga * sg_a
        silu_r = gr * sg_r
        ya = (ov * silu_a).astype(BF16)
        yr = (hv * silu_r).astype(BF16)
        pa = _dot(ya, w[0])
        pr = _dot(yr, w[1])
        sa = _sigmoid(ma)
        sr = _sigmoid(mr)
        merged = (sa * pa + sr * pr).astype(BF16)
        mo = _dot(merged, w[2])
        out = x_ref[...] + gate_v * mo
        r2 = lax.rsqrt(jnp.mean(out * out, axis=-1, keepdims=True) + NORM_EPS)
        outn = out * r2
        err = outn * fg - tg_ref[...]
        loss = 0.5 * jnp.sum(jnp.mean(err * err, axis=-1, keepdims=True), axis=0, keepdims=True)
        dy = err * (1.0 / D)
        g_final = jnp.sum(dy * outn, axis=0, keepdims=True)
        doutn = dy * fg
        dout = r2 * (doutn - outn * jnp.mean(doutn * outn, axis=-1, keepdims=True))
        dres_ref[...] = dout
        d_gate = jnp.sum(dout * mo, axis=0, keepdims=True)
        dmo = (dout * gate_v).astype(BF16)
        dmerged = _dot_nt(dmo, w[2])
        gw[2] = gw[2] + _dot_tn(merged, dmo)
        dpa = (dmerged * sa).astype(BF16)
        dpr = (dmerged * sr).astype(BF16)
        dma_ref[...] = (dmerged * pa * (sa * (1.0 - sa))).astype(BF16)
        dmr_ref[...] = (dmerged * pr * (sr * (1.0 - sr))).astype(BF16)
        dya = _dot_nt(dpa, w[0])
        dyr = _dot_nt(dpr, w[1])
        gw[0] = gw[0] + _dot_tn(ya, dpa)
        gw[1] = gw[1] + _dot_tn(yr, dpr)
        do_ref[...] = (dya * silu_a).astype(BF16)
        dhh_ref[...] = dyr * silu_r
        dga_ref[...] = (dya * ov * (sg_a * (1.0 + ga * (1.0 - sg_a)))).astype(BF16)
        dgr_ref[...] = (dyr * hv * (sg_r * (1.0 + gr * (1.0 - sg_r)))).astype(BF16)
        zeros = jnp.zeros((5, D), F32)
        vec_ref[...] = vec_ref[...] + jnp.concatenate([g_final, d_gate, jnp.broadcast_to(loss, (1, D)), zeros], axis=0)

        @pl.when(i == n_tiles - 1)
        def _():
            cp = pltpu.make_async_copy(gw, gw_hbm, sem)
            cp.start()
            cp.wait()

    bf_tile = jax.ShapeDtypeStruct((t, D), BF16)
    f_tile = jax.ShapeDtypeStruct((t, D), F32)
    return pl.pallas_call(
        body,
        name="mid",
        grid=(n_tiles,),
        in_specs=[full, full, full, full,
                  half(SEG_GA // 512), half(SEG_GA // 512 + 1), half(SEG_GR // 512), half(SEG_GR // 512 + 1),
                  half(SEG_MA // 512), half(SEG_MA // 512 + 1), half(SEG_MR // 512), half(SEG_MR // 512 + 1),
                  any_spec, vec, vec],
        out_specs=[full, full, full, full, full, full, full, any_spec, pl.BlockSpec((8, D), lambda i: (0, 0))],
        out_shape=[f_tile, bf_tile, f_tile, bf_tile, bf_tile, bf_tile, bf_tile,
                   jax.ShapeDtypeStruct((3, D, D), F32), jax.ShapeDtypeStruct((8, D), F32)],
        scratch_shapes=[pltpu.VMEM((3, D, D), BF16), pltpu.VMEM((3, D, D), F32), pltpu.SemaphoreType.DMA, pltpu.SemaphoreType.DMA((3 * N_DEV,))],
        compiler_params=_params(1),
    )(x, target, o, hh, proj, proj, proj, proj, proj, proj, proj, proj, w_all, gate, final_g)


_SEGMENTS = ((SEG_Q, 1024), (SEG_K, 256), (SEG_V, 256), (SEG_GA, 1024), (SEG_XR, 1024), (SEG_GR, 1024), (SEG_MA, 1024), (SEG_MR, 1024))


def _seg_specs(tm):
    return [pl.BlockSpec((tm, width), lambda i: (i, 0)) for _, width in _SEGMENTS]


def _bwd_dh(dsegs, w_all, x, dres, mod, norm_g, sum_w_bf):
    t = x.shape[0]
    tm = min(512, t)
    n_tiles = t // tm
    n_seg = len(_SEGMENTS)
    full = pl.BlockSpec((tm, D), lambda i: (i, 0))
    views = [lambda ref, peer: ref.at[_chip_of(peer)]]

    def body(*refs):
        segs = refs[:n_seg]
        w_hbm, x_ref, dres_ref, mod_ref, ng_ref, sum_hbm, gx_ref, vec_ref, land_hbm, w, sems, send_sems, recv_sems = refs[n_seg:]
        i = pl.program_id(0)

        @pl.when(i == 0)
        def _():
            for cp in _chip_copies([sum_hbm], [land_hbm], views, send_sems, recv_sems):
                cp.start()
            loads = [
                pltpu.make_async_copy(w_hbm.at[s], w.at[pl.ds(W_IN_SHARD * s, W_IN_SHARD), :], sems.at[s])
                for s in range(N_DEV)
            ]
            for cp in loads:
                cp.start()
            for cp in loads:
                cp.wait()
            vec_ref[...] = jnp.zeros_like(vec_ref)

        dh = sum(_dot(seg[...], w[start : start + width, :]) for seg, (start, width) in zip(segs, _SEGMENTS))
        xv = x_ref[...]
        r1 = lax.rsqrt(jnp.mean(xv * xv, axis=-1, keepdims=True) + NORM_EPS)
        xn = xv * r1
        ng = ng_ref[...]
        scale1 = 1.0 + mod_ref[:, D : 2 * D]
        d_shift = jnp.sum(dh, axis=0, keepdims=True)
        d_scale = jnp.sum(dh * (xn * ng), axis=0, keepdims=True)
        g_norm = jnp.sum(dh * xn * scale1, axis=0, keepdims=True)
        dxn = dh * (ng * scale1)
        dx = r1 * (dxn - xn * jnp.mean(dxn * xn, axis=-1, keepdims=True))
        gx_ref[...] = dres_ref[...] + dx
        vec_ref[...] = vec_ref[...] + jnp.concatenate([d_shift, d_scale, g_norm, jnp.zeros((5, D), F32)], axis=0)

        @pl.when(i == n_tiles - 1)
        def _():
            for cp in _chip_copies([sum_hbm], [land_hbm], views, send_sems, recv_sems):
                cp.wait_recv()
                cp.wait_send()

    any_spec = pl.BlockSpec(memory_space=pl.ANY)
    n_copies = len(CHIP_FLIPS)
    return pl.pallas_call(
        body,
        name="bwd_dh",
        grid=(n_tiles,),
        in_specs=_seg_specs(tm) + [any_spec, full, full, pl.BlockSpec((1, 3 * D), lambda i: (0, 0)), pl.BlockSpec((1, D), lambda i: (0, 0)),
                                   any_spec],
        out_specs=[full, pl.BlockSpec((8, D), lambda i: (0, 0)), any_spec],
        out_shape=[jax.ShapeDtypeStruct((t, D), F32), jax.ShapeDtypeStruct((8, D), F32),
                   jax.ShapeDtypeStruct((n_copies, W_IN_SHARD, D), BF16)],
        scratch_shapes=[pltpu.VMEM((IN_W, D), BF16), pltpu.SemaphoreType.DMA((N_DEV,)),
                        pltpu.SemaphoreType.DMA((n_copies,)), pltpu.SemaphoreType.DMA((n_copies,))],
        compiler_params=_params(1),
    )(*dsegs, w_all, x, dres, mod, norm_g, sum_w_bf)


def _bwd_gw(dsegs, h, sum_g_bf):
    t = h.shape[0]
    tm = min(512, t)
    n_tiles = t // tm
    n_seg = len(_SEGMENTS)
    n_chips = N_DEV // 2
    views = [lambda ref, peer: ref.at[:, :, _chip_of(peer)]]

    def body(*refs):
        segs = refs[:n_seg]
        (h_ref, sg_hbm, own_hbm, recv_hbm, lg_hbm, acc, stage, own_sems, pair_send, pair_recv, send_sems, recv_sems) = refs[n_seg:]
        i = pl.program_id(0)

        @pl.when(i == 0)
        def _():
            for cp in _chip_copies([sg_hbm], [lg_hbm], views, send_sems, recv_sems):
                cp.start()
            acc[...] = jnp.zeros_like(acc)

        hv = h_ref[...]
        for seg, (start, width) in zip(segs, _SEGMENTS):
            acc[start : start + width, :] = acc[start : start + width, :] + _dot_tn(seg[...], hv)

        @pl.when(i == n_tiles - 1)
        def _():
            me = _position()
            sibling = _flip(me, 1)

            def rows(q, c):
                return pl.ds(pl.multiple_of(W_IN_SHARD * (2 * q + c), 8), W_IN_SHARD)

            def to_sibling(q):
                return pltpu.make_async_remote_copy(
                    src_ref=stage.at[q % 2], dst_ref=recv_hbm.at[q], send_sem=pair_send.at[q], recv_sem=pair_recv.at[q],
                    device_id=sibling, device_id_type=MESH,
                )

            kept = [pltpu.make_async_copy(acc.at[rows(q, me[2]), :], own_hbm.at[q], own_sems.at[q]) for q in range(n_chips)]
            for cp in kept:
                cp.start()
            for q in range(n_chips):
                if q >= 2:
                    to_sibling(q - 2).wait_send()
                stage[q % 2] = acc[rows(q, sibling[2]), :].astype(BF16)
                to_sibling(q).start()
            for q in range(n_chips - 2, n_chips):
                to_sibling(q).wait_send()
            for q in range(n_chips):
                to_sibling(q).wait_recv()
            for cp in kept:
                cp.wait()
            for cp in _chip_copies([sg_hbm], [lg_hbm], views, send_sems, recv_sems):
                cp.wait_recv()
                cp.wait_send()

    any_spec = pl.BlockSpec(memory_space=pl.ANY)
    n_copies = len(CHIP_FLIPS)
    return pl.pallas_call(
        body,
        name="bwd_gw",
        grid=(n_tiles,),
        in_specs=_seg_specs(tm) + [pl.BlockSpec((tm, D), lambda i: (i, 0)), any_spec],
        out_specs=[any_spec, any_spec, any_spec],
        out_shape=[jax.ShapeDtypeStruct((n_chips, W_IN_SHARD, D), F32),
                   jax.ShapeDtypeStruct((n_chips, W_IN_SHARD, D), BF16),
                   jax.ShapeDtypeStruct((len(CHIP_FLIPS), 2, 4, GATE_SHARD, 256), BF16)],
        scratch_shapes=[pltpu.VMEM((IN_W, D), F32), pltpu.VMEM((2, W_IN_SHARD, D), BF16),
                        pltpu.SemaphoreType.DMA((n_chips,)), pltpu.SemaphoreType.DMA((n_chips,)), pltpu.SemaphoreType.DMA((n_chips,)),
                        pltpu.SemaphoreType.DMA((n_copies,)), pltpu.SemaphoreType.DMA((n_copies,))],
        compiler_params=_params(1),
    )(*dsegs, h, sum_g_bf)


def _adamw(w, g, m, v):
    m = ADAM_B1 * m + (1.0 - ADAM_B1) * g
    v = ADAM_B2 * v + (1.0 - ADAM_B2) * (g * g)
    m_hat = m / (1.0 - ADAM_B1**ADAM_STEP)
    v_hat = v / (1.0 - ADAM_B2**ADAM_STEP)
    delta = -ADAM_LR * (m_hat / (jnp.sqrt(v_hat) + ADAM_EPS) + ADAM_WD * w)
    return delta, m, v


def _w_in_update(own, land, w_t, m_t, v_t):
    n_slots, rows, cols = land.shape
    rb = 208
    spec = pl.BlockSpec((rb, cols), lambda i: (i, 0))

    def body(o_ref, l_ref, w_ref, m_ref, v_ref, g_ref, d_ref, m2_ref, v2_ref):
        total = o_ref[...]
        for s in range(n_slots):
            total = total + l_ref[s].astype(F32)
        g_ref[...] = total
        d_ref[...], m2_ref[...], v2_ref[...] = _adamw(w_ref[...], total, m_ref[...], v_ref[...])

    shape = jax.ShapeDtypeStruct((rows, cols), F32)
    return pl.pallas_call(
        body,
        name="w_in_update",
        grid=(rows // rb,),
        in_specs=[spec, pl.BlockSpec((n_slots, rb, cols), lambda i: (0, i, 0)), spec, spec, spec],
        out_specs=[spec] * 4,
        out_shape=[shape] * 4,
        compiler_params=_params(1),
    )(own, land, w_t, m_t, v_t)


def _shard_update(own_3, land_3, own_g, land_g, weights, m_in, v_in):
    n = len(weights)

    def body(*refs):
        o3_ref, l3_ref, og_ref, lg_ref = refs[:4]
        w_refs, m_refs, v_refs = (refs[4 + n * k : 4 + n * (k + 1)] for k in range(3))
        outs = refs[4 + 3 * n :]
        for k in range(n):
            if k < 3:
                g = o3_ref[k]
                for slot in range(3):
                    g = g + l3_ref[slot, k].astype(F32)
            else:
                g = og_ref[k - 3]
                for slot in range(3):
                    g = g + lg_ref[slot, k - 3].astype(F32)
            d, m2, v2 = _adamw(w_refs[k][0], g, m_refs[k][0], v_refs[k][0])
            for which, val in enumerate((g, d, m2, v2)):
                outs[which * n + k][0] = val

    shapes = [jax.ShapeDtypeStruct(w.shape, F32) for w in weights]
    res = pl.pallas_call(
        body, name="shard_update", out_shape=shapes * 4, compiler_params=pltpu.CompilerParams(vmem_limit_bytes=VMEM_LIMIT)
    )(own_3, land_3, own_g, land_g, *weights, *m_in, *v_in)
    return [res[n * k : n * (k + 1)] for k in range(4)]


VEC_LAYOUT = (("b_ada", 3 * D), ("norm_g", D), ("attn_sinks", 128), ("conv_b", D), ("rg_ba", D), ("rg_bx", D), ("rg_lambda", D),
              ("final_g", D), ("conv_w", CONV_WIDTH * D), ("loss", 128))
VEC_OFFSET = {name: sum(w for _, w in VEC_LAYOUT[:k]) for k, (name, _) in enumerate(VEC_LAYOUT)}
VEC_ROWS = 112
VEC_WIDTH = VEC_ROWS * 128
assert sum(w for _, w in VEC_LAYOUT) <= VEC_WIDTH
N_REPLICATED = 8


def _small_update(vec_all, c_all_t, me, rep_w, rep_m, rep_v, ada, conv):
    n_ada = ada[0].shape[1]
    n_conv = conv[0].shape[1]

    def body(*refs):
        me_ref, vec_ref, ct_ref = refs[:3]
        w_refs, m_refs, v_refs = (refs[3 + N_REPLICATED * k : 3 + N_REPLICATED * (k + 1)] for k in range(3))
        pos = 3 + 3 * N_REPLICATED
        ada_refs, conv_refs = refs[pos : pos + 3], refs[pos + 3 : pos + 6]
        outs = refs[pos + 6 :]
        loss_ref, rep_outs = outs[0], outs[1 : 1 + 4 * N_REPLICATED]
        ada_outs, conv_outs = outs[1 + 4 * N_REPLICATED : 5 + 4 * N_REPLICATED], outs[5 + 4 * N_REPLICATED : 9 + 4 * N_REPLICATED]
        conv_buf = outs[9 + 4 * N_REPLICATED]

        total = vec_ref[0:1, :]
        for slot in range(1, N_DEV):
            total = total + vec_ref[slot : slot + 1, :]
        loss_ref[...] = total[:, VEC_OFFSET["loss"] : VEC_OFFSET["loss"] + 1]
        for k, (name, _) in enumerate(VEC_LAYOUT[:N_REPLICATED]):
            width = w_refs[k].shape[1]
            g = total[:, VEC_OFFSET[name] : VEC_OFFSET[name] + width]
            d, m2, v2 = _adamw(w_refs[k][...], g, m_refs[k][...], v_refs[k][...])
            for which, val in enumerate((g, d, m2, v2)):
                rep_outs[which * N_REPLICATED + k][...] = val

        device = me_ref[0]
        for k in range(CONV_WIDTH):
            conv_buf[k : k + 1, :] = total[:, VEC_OFFSET["conv_w"] + D * k : VEC_OFFSET["conv_w"] + D * (k + 1)]
        g_conv = conv_buf[:, pl.ds(pl.multiple_of(device * n_conv, 128), n_conv)]
        for ref, val in zip(conv_outs, (g_conv,) + _adamw(conv_refs[0][...], g_conv, conv_refs[1][...], conv_refs[2][...])):
            ref[...] = val

        dmod = vec_ref[:, pl.ds(pl.multiple_of(device * n_ada, 128), n_ada)]
        g_ada = _dot(ct_ref[...], dmod)
        for ref, val in zip(ada_outs, (g_ada,) + _adamw(ada_refs[0][...], g_ada, ada_refs[1][...], ada_refs[2][...])):
            ref[...] = val

    vmem = pl.BlockSpec(memory_space=pltpu.VMEM)
    rep_shapes = [jax.ShapeDtypeStruct(w.shape, F32) for w in rep_w]
    out_shape = ([jax.ShapeDtypeStruct((1, 1), F32)] + rep_shapes * 4 + [jax.ShapeDtypeStruct(ada[0].shape, F32)] * 4
                 + [jax.ShapeDtypeStruct(conv[0].shape, F32)] * 4)
    n_in = 3 + 3 * N_REPLICATED + 6
    res = pl.pallas_call(
        body,
        name="small_update",
        in_specs=[pl.BlockSpec(memory_space=pltpu.SMEM)] + [vmem] * (n_in - 1),
        out_specs=[vmem] * len(out_shape),
        out_shape=out_shape,
        scratch_shapes=[pltpu.VMEM((CONV_WIDTH, D), F32)],
        compiler_params=pltpu.CompilerParams(vmem_limit_bytes=VMEM_LIMIT),
    )(me, vec_all, c_all_t, *rep_w, *rep_m, *rep_v, *ada, *conv)
    loss = res[0]
    reps = [res[1 + N_REPLICATED * k : 1 + N_REPLICATED * (k + 1)] for k in range(4)]
    return loss, reps, res[1 + 4 * N_REPLICATED : 5 + 4 * N_REPLICATED], res[5 + 4 * N_REPLICATED : 9 + 4 * N_REPLICATED]


def kernel(x, c, positions, w_ada, b_ada, norm_g, w_in, attn_sinks, conv_w, conv_b, rg_wa, rg_ba, rg_wx, rg_bx, rg_lambda, w_attn_proj, w_rnn_proj, w_out, final_g, loss_target, m_w_ada, m_b_ada, m_norm_g, m_w_in, m_attn_sinks, m_conv_w, m_conv_b, m_rg_wa, m_rg_ba, m_rg_wx, m_rg_bx, m_rg_lambda, m_w_attn_proj, m_w_rnn_proj, m_w_out, m_final_g, v_w_ada, v_b_ada, v_norm_g, v_w_in, v_attn_sinks, v_conv_w, v_conv_b, v_rg_wa, v_rg_ba, v_rg_wx, v_rg_bx, v_rg_lambda, v_w_attn_proj, v_w_rnn_proj, v_w_out, v_final_g):
    my_x, my_y, my_c = _position()
    me = _slot((my_x, my_y, my_c))
    x2, target = x[0], loss_target[0]
    pos = positions[0]

    gate_rows = lambda w: w[0].transpose(1, 0, 2).reshape(GATE_SHARD, D)
    w_pack = jnp.concatenate([w_attn_proj[0], w_rnn_proj[0], w_out[0], gate_rows(rg_wa), gate_rows(rg_wx)], axis=0).astype(BF16)
    conv_rows = jnp.pad(conv_w[0], ((0, 8 - CONV_WIDTH), (0, 0)))
    w_in_t = w_in[0].T
    n_ada = w_ada.shape[2]
    b_cols = lax.dynamic_slice(b_ada, (0, me * n_ada), (1, n_ada))

    tables = _rope_tables(pos)
    where = jnp.stack([_chip_of((my_x, my_y)), me]).astype(jnp.int32)
    proj, h, mod, w_t_all, w_all, c_rows, conv_all = _fwd_in(
        x2, jnp.broadcast_to(c, (8, D)), conv_rows, w_ada[0], b_cols, norm_g, w_in_t.astype(BF16), tables[0], w_pack, where
    )
    c_all = c_rows[:, 0, :]
    conv_full = conv_all[:, :CONV_WIDTH, :].transpose(1, 0, 2).reshape(CONV_WIDTH, D)
    gate = mod[:, 2 * D :]
    o = _attn_fwd(proj, attn_sinks)
    keep = (pos != 0).astype(F32)[:, None]
    hh = _rnn_fwd(proj, keep, w_all, rg_ba, rg_bx, rg_lambda, conv_full, conv_b)

    dres, d_o, d_hh, d_ga, d_gr, d_ma, d_mr, gw3, mid_vec = _mid(x2, target, o, hh, proj, w_all, gate, final_g[None, :])
    index = jnp.stack([my_c, _chip_of((my_x, my_y))]).astype(jnp.int32)
    part_3 = gw3.reshape(3, 4, 2, ROW_SHARD, D)
    d_q, d_k, d_v, d_sink, recv_3 = _attn_bwd(proj, o, d_o, tables, attn_sinks, part_3)
    sum_3_bf, own_3 = _pair_add(
        "pair_add_3", part_3, recv_3, index, (4,), 0,
        pl.BlockSpec((3, None, None, ROW_SHARD, D), lambda q, idx: (0, q, idx[0], 0, 0)),
        pl.BlockSpec((3, None, ROW_SHARD, D), lambda q, idx: (0, q, 0, 0)),
        pl.BlockSpec((3, ROW_SHARD, D), lambda q, idx: (0, 0, 0)), (3, ROW_SHARD, D),
    )
    d_xr, g_gates, rnn_vec, land_3 = _rnn_bwd(proj, hh, d_hh, keep, w_all, rg_ba, rg_bx, rg_lambda, conv_full, conv_b, sum_3_bf)
    dsegs = (d_q, d_k, d_v, d_ga, d_xr, d_gr, d_ma, d_mr)
    part_g = g_gates.reshape(2, 4, 4, 2, GATE_SHARD, 256)
    (recv_g,) = _exchange("rs_pair_g", [part_g], [lambda ref, peer: ref.at[:, :, :, peer[2]]], [(2, 4, 4, GATE_SHARD, 256)], flips=(1,))
    sum_g_bf, own_g = _pair_add(
        "pair_add_g", part_g, recv_g[0], index, (4,), 0,
        pl.BlockSpec((2, 4, None, None, GATE_SHARD, 256), lambda q, idx: (0, 0, q, idx[0], 0, 0)),
        pl.BlockSpec((2, 4, None, GATE_SHARD, 256), lambda q, idx: (0, 0, q, 0, 0)),
        pl.BlockSpec((2, 4, GATE_SHARD, 256), lambda q, idx: (0, 0, 0, 0)), (2, 4, GATE_SHARD, 256),
    )
    part_w, recv_w, land_g = _bwd_gw(dsegs, h, sum_g_bf)
    w_rows = pl.BlockSpec((None, W_IN_SHARD, D), lambda q, idx: (q, 0, 0))
    sum_w_bf, own_w = _pair_add(
        "pair_add_w", part_w, recv_w, index, (4,), 0, w_rows, w_rows,
        pl.BlockSpec((W_IN_SHARD, D), lambda q, idx: (0, 0)), (W_IN_SHARD, D),
    )
    grad_x, dh_vec, land_w = _bwd_dh(dsegs, w_t_all, x2, dres, mod, norm_g, sum_w_bf)

    pieces = dict(
        b_ada=jnp.concatenate([dh_vec[0:1], dh_vec[1:2], mid_vec[1:2]], axis=1),
        norm_g=dh_vec[2:3],
        attn_sinks=jnp.pad(d_sink[:, :, 0].reshape(1, 16), ((0, 0), (0, 112))),
        conv_b=rnn_vec[3:4],
        rg_ba=rnn_vec[0:1],
        rg_bx=rnn_vec[1:2],
        rg_lambda=rnn_vec[2:3],
        final_g=mid_vec[0:1],
        conv_w=rnn_vec[4:8].reshape(1, CONV_WIDTH * D),
        loss=mid_vec[2:3, 0:128],
    )
    vec = jnp.concatenate([pieces[name] for name, _ in VEC_LAYOUT], axis=1)
    vec = jnp.pad(vec, ((0, 0), (0, VEC_WIDTH - vec.shape[1]))).reshape(VEC_ROWS, 128)
    (vec_all,) = _all_gather_direct("ag_vec", [vec])
    rep_w = [b_ada, norm_g, attn_sinks, conv_b, rg_ba, rg_bx, rg_lambda, final_g[None, :]]
    rep_m = [m_b_ada, m_norm_g, m_attn_sinks, m_conv_b, m_rg_ba, m_rg_bx, m_rg_lambda, m_final_g[None, :]]
    rep_v = [v_b_ada, v_norm_g, v_attn_sinks, v_conv_b, v_rg_ba, v_rg_bx, v_rg_lambda, v_final_g[None, :]]
    loss, reps, ada_res, conv_res = _small_update(
        vec_all.reshape(N_DEV, VEC_WIDTH), c_all.T, me.astype(jnp.int32).reshape(1), rep_w, rep_m, rep_v,
        (w_ada[0], m_w_ada[0], v_w_ada[0]), (conv_w[0], m_conv_w[0], v_conv_w[0]),
    )

    w_in_res = _w_in_update(own_w, land_w, w_in_t, m_w_in[0].T, v_w_in[0].T)
    shard_res = _shard_update(
        own_3, land_3, own_g, land_g,
        (w_attn_proj, w_rnn_proj, w_out, rg_wa, rg_wx),
        (m_w_attn_proj, m_w_rnn_proj, m_w_out, m_rg_wa, m_rg_wx),
        (v_w_attn_proj, v_w_rnn_proj, v_w_out, v_rg_wa, v_rg_wx),
    )

    def by_weight(k):
        b, n, s, cb, ba, bx, lam, fg = reps[k]
        ap, rp, out, wa, wx = shard_res[k]
        return [ada_res[k][None], b, n, w_in_res[k].T[None], s, conv_res[k][None], cb, wa, ba, wx, bx, lam, ap, rp, out, fg[0]]

    return (loss[0, 0], grad_x[None], *by_weight(0), *by_weight(1), *by_weight(2), *by_weight(3))
```

```python
import functools
import math

import jax
import jax.numpy as jnp
import numpy as np
from jax import lax
from jax.experimental import pallas as pl
from jax.experimental.pallas import tpu as pltpu

F32 = jnp.float32
BF16 = jnp.bfloat16

D = 1024
HEAD_DIM = 64
ROT_DIM = 16
ROPE_THETA = 500000.0
WINDOW = 128
LRU_C = 8.0
CONV_WIDTH = 4
NORM_EPS = 1e-6
IN_W = 6656
SEG_Q, SEG_K, SEG_V, SEG_GA, SEG_XR, SEG_GR, SEG_MA, SEG_MR = 0, 1024, 1280, 1536, 2560, 3584, 4608, 5632
N_DEV = 8
W_IN_SHARD = IN_W // N_DEV
ROW_SHARD = D // N_DEV
GATE_SHARD = 256 // N_DEV

ADAM_LR = 0.001
ADAM_B1 = 0.9
ADAM_B2 = 0.999
ADAM_EPS = 1e-08
ADAM_WD = 0.01
ADAM_STEP = 10

VMEM_LIMIT = 60 * 1024 * 1024
MESH = pl.DeviceIdType.MESH
AXES = ("x", "y", "c")


def _dot(a, b):
    return jnp.dot(a, b, preferred_element_type=F32)


def _dot_nt(a, b):
    return lax.dot_general(a, b, (((1,), (1,)), ((), ())), preferred_element_type=F32)


def _dot_tn(a, b):
    return lax.dot_general(a, b, (((0,), (0,)), ((), ())), preferred_element_type=F32)


def _sigmoid(z):
    return 1.0 / (1.0 + jnp.exp(-z))


def _params(n_grid, vmem=VMEM_LIMIT):
    return pltpu.CompilerParams(dimension_semantics=("arbitrary",) * n_grid, vmem_limit_bytes=vmem)


def _position():
    return lax.axis_index("x"), lax.axis_index("y"), lax.axis_index("c")


def _flip(pos, k):
    x, y, c = pos
    return (1 - x if k & 4 else x, 1 - y if k & 2 else y, 1 - c if k & 1 else c)


def _slot(pos):
    return 4 * pos[0] + 2 * pos[1] + pos[2]


def _all_gather_direct(name, arrs):
    n = len(arrs)

    def body(*refs):
        ins, outs = refs[:n], refs[n : 2 * n]
        send_sems, recv_sems, local_sems = refs[2 * n :]
        me = _position()
        mine = [pltpu.make_async_copy(ins[a], outs[a].at[_slot(me)], local_sems.at[a]) for a in range(n)]
        for cp in mine:
            cp.start()

        def copy(a, k, block):
            return pltpu.make_async_remote_copy(
                src_ref=ins[a],
                dst_ref=outs[a].at[_slot(block)],
                send_sem=send_sems.at[a * 7 + k - 1],
                recv_sem=recv_sems.at[a * 7 + k - 1],
                device_id=_flip(me, k),
                device_id_type=MESH,
            )

        sent = [copy(a, k, me) for a in range(n) for k in range(1, N_DEV)]
        for cp in sent:
            cp.start()
        for a in range(n):
            for k in range(1, N_DEV):
                copy(a, k, _flip(me, k)).wait_recv()
        for cp in sent:
            cp.wait_send()
        for cp in mine:
            cp.wait()

    any_spec = pl.BlockSpec(memory_space=pl.ANY)
    return pl.pallas_call(
        body,
        name=name,
        out_shape=[jax.ShapeDtypeStruct((N_DEV,) + a.shape, a.dtype) for a in arrs],
        in_specs=[any_spec] * n,
        out_specs=[any_spec] * n,
        scratch_shapes=[
            pltpu.SemaphoreType.DMA((7 * n,)),
            pltpu.SemaphoreType.DMA((7 * n,)),
            pltpu.SemaphoreType.DMA((n,)),
        ],
    )(*arrs)


def _direct_gather_copies(src_hbm, all_hbm, send_sems, recv_sems, arriving):
    me = _position()
    return [
        pltpu.make_async_remote_copy(
            src_ref=src_hbm,
            dst_ref=all_hbm.at[_slot(_flip(me, k) if arriving else me)],
            send_sem=send_sems.at[k - 1],
            recv_sem=recv_sems.at[k - 1],
            device_id=_flip(me, k),
            device_id_type=MESH,
        )
        for k in range(1, N_DEV)
    ]


CHIP_FLIPS = (2, 4, 6)


def _chip_of(pos):
    return 2 * pos[0] + pos[1]


def _chip_copies(srcs, lands, views, send_sems, recv_sems):
    me = _position()
    copies = []
    for a, (src, land) in enumerate(zip(srcs, lands)):
        for j, k in enumerate(CHIP_FLIPS):
            peer = _flip(me, k)
            copies.append(
                pltpu.make_async_remote_copy(
                    src_ref=views[a](src, peer),
                    dst_ref=land.at[j],
                    send_sem=send_sems.at[len(CHIP_FLIPS) * a + j],
                    recv_sem=recv_sems.at[len(CHIP_FLIPS) * a + j],
                    device_id=peer,
                    device_id_type=MESH,
                )
            )
    return copies


def _exchange(name, arrs, views, out_shapes, flips):
    n = len(arrs)
    nf = len(flips)

    def body(*refs):
        ins, outs = refs[:n], refs[n : 2 * n]
        send_sems, recv_sems = refs[2 * n :]
        me = _position()

        def copy(a, j):
            peer = _flip(me, flips[j])
            return pltpu.make_async_remote_copy(
                src_ref=views[a](ins[a], peer),
                dst_ref=outs[a].at[j],
                send_sem=send_sems.at[a * nf + j],
                recv_sem=recv_sems.at[a * nf + j],
                device_id=peer,
                device_id_type=MESH,
            )

        sent = [copy(a, j) for a in range(n) for j in range(nf)]
        for cp in sent:
            cp.start()
        for cp in sent:
            cp.wait_recv()
        for cp in sent:
            cp.wait_send()

    any_spec = pl.BlockSpec(memory_space=pl.ANY)
    return pl.pallas_call(
        body,
        name=name,
        out_shape=[jax.ShapeDtypeStruct((nf,) + tuple(sh), a.dtype) for sh, a in zip(out_shapes, arrs)],
        in_specs=[any_spec] * n,
        out_specs=[any_spec] * n,
        scratch_shapes=[pltpu.SemaphoreType.DMA((nf * n,)), pltpu.SemaphoreType.DMA((nf * n,))],
    )(*arrs)


def _pair_add(name, own, recv, index, grid, chip_axis, own_spec, recv_spec, mine_spec, mine_shape):
    def body(idx_ref, own_ref, recv_ref, sb_ref, mine_ref):
        total = own_ref[...] + recv_ref[...].astype(F32)
        sb_ref[...] = total.astype(BF16)

        @pl.when(pl.program_id(chip_axis) == idx_ref[1])
        def _():
            mine_ref[...] = total

    return pl.pallas_call(
        body,
        name=name,
        grid_spec=pltpu.PrefetchScalarGridSpec(
            num_scalar_prefetch=1, grid=grid, in_specs=[own_spec, recv_spec], out_specs=[recv_spec, mine_spec]
        ),
        out_shape=[jax.ShapeDtypeStruct(recv.shape, BF16), jax.ShapeDtypeStruct(mine_shape, F32)],
        compiler_params=_params(len(grid)),
    )(index, own, recv)


PACK_GATE = 3 * ROW_SHARD
PACK_ROWS = PACK_GATE + 2 * GATE_SHARD


def _fwd_in(x, c_rows, conv_rows, w_ada, b_cols, norm_g, w_shard, tables, pack, where):
    t = x.shape[0]
    tm = min(1024, t)
    tn = IN_W // 4
    n_i, n_j = t // tm, IN_W // tn
    n_rope = (SEG_V - SEG_Q) // 128

    n_ada = w_ada.shape[1]

    def body(where_ref, x_ref, wada_ref, bcols_ref, ng_ref, cos_ref, s1_ref, s2_ref, shard_hbm, pack_hbm, c_hbm, conv_hbm,
             proj_ref, h_ref, mod_ref, wt_hbm, all_hbm, call_hbm, convall_hbm,
             wall, hbuf, cbuf, mcols, mod_all, mod_buf,
             w_send, w_recv, load_sems, local_sem, p_send, p_recv, p_local,
             c_send, c_recv, c_local, v_send, v_recv, v_local, m_send, m_recv):
        j = pl.program_id(0)
        i = pl.program_id(1)
        me = _position()
        sibling = _flip(me, 1)
        chip = where_ref[0]
        slot = where_ref[1]

        def modulation():
            own_c = pltpu.make_async_copy(c_hbm, call_hbm.at[_slot(me)], c_local)
            own_v = pltpu.make_async_copy(conv_hbm, convall_hbm.at[_slot(me)], v_local)
            own_c.start()
            own_v.start()
            for cp in _direct_gather_copies(c_hbm, call_hbm, c_send, c_recv, arriving=False):
                cp.start()
            for cp in _direct_gather_copies(conv_hbm, convall_hbm, v_send, v_recv, arriving=False):
                cp.start()
            for cp in _direct_gather_copies(c_hbm, call_hbm, c_send, c_recv, arriving=True):
                cp.wait_recv()
            own_c.wait()
            fetch = pltpu.make_async_copy(call_hbm, cbuf, c_local)
            fetch.start()
            fetch.wait()
            mcols[...] = _dot(cbuf[:, 0, :].astype(BF16), wada_ref[...].astype(BF16)) + bcols_ref[...]
            mod_all[slot] = mcols[...]
            swaps = lambda arriving: [
                pltpu.make_async_remote_copy(
                    src_ref=mcols,
                    dst_ref=mod_all.at[_slot(_flip(me, k) if arriving else me)],
                    send_sem=m_send.at[k - 1],
                    recv_sem=m_recv.at[k - 1],
                    device_id=_flip(me, k),
                    device_id_type=MESH,
                )
                for k in range(1, N_DEV)
            ]
            for cp in swaps(False):
                cp.start()
            for cp in swaps(True):
                cp.wait_recv()
            mine = lax.broadcasted_iota(jnp.int32, (N_DEV, n_ada), 0) == slot
            for s in range(N_DEV):
                row = jnp.sum(jnp.where(mine, mod_all[s], 0.0), axis=0, keepdims=True)
                mod_buf[:, n_ada * s : n_ada * (s + 1)] = row
            for cp in swaps(False):
                cp.wait_send()
            for cp in _direct_gather_copies(c_hbm, call_hbm, c_send, c_recv, arriving=False):
                cp.wait_send()
            for cp in _direct_gather_copies(conv_hbm, convall_hbm, v_send, v_recv, arriving=True):
                cp.wait_recv()
            for cp in _direct_gather_copies(conv_hbm, convall_hbm, v_send, v_recv, arriving=False):
                cp.wait_send()
            own_v.wait()

        def shard_of(block):
            return wall.at[_chip_of(block), pl.ds(W_IN_SHARD * block[2], W_IN_SHARD), :]

        def shard_copy(k, block, to, src=None):
            return pltpu.make_async_remote_copy(
                src_ref=shard_of(block),
                dst_ref=shard_of(block),
                send_sem=w_send.at[k],
                recv_sem=w_recv.at[k],
                device_id=to,
                device_id_type=MESH,
            )

        own_shard = pltpu.make_async_copy(shard_hbm, shard_of(me), local_sem)
        own_pack = pltpu.make_async_copy(pack_hbm, all_hbm.at[_slot(me)], p_local)

        def keep_tile(q):
            return [
                pltpu.make_async_copy(wall.at[q, pl.ds(W_IN_SHARD * c, W_IN_SHARD), :], wt_hbm.at[2 * q + c], load_sems.at[2 * (q ^ chip) + c])
                for c in range(2)
            ]

        @pl.when((j == 0) & (i == 0))
        def _():
            modulation()
            mod_ref[...] = mod_buf[...]
            own_shard.start()
            own_shard.wait()
            shard_copy(0, me, sibling).start()
            for rel in range(1, 4):
                shard_copy(rel, me, _flip(me, 2 * rel)).start()
            own_pack.start()
            for cp in _direct_gather_copies(pack_hbm, all_hbm, p_send, p_recv, arriving=False):
                cp.start()
            shard_copy(0, sibling, me).wait_recv()
            for cp in keep_tile(chip):
                cp.start()

        for rel in range(1, 4):

            @pl.when((j == rel) & (i == 0))
            def _(rel=rel):
                there = _flip(me, 2 * rel)
                shard_copy(rel, there, me).wait_recv()
                shard_copy(3 + rel, there, sibling).start()
                shard_copy(3 + rel, _flip(sibling, 2 * rel), me).wait_recv()
                for cp in keep_tile(chip ^ rel):
                    cp.start()

        rows = pl.ds(pl.multiple_of(i * tm, tm), tm)

        @pl.when(j == 0)
        def _():
            xv = x_ref[...]
            r = lax.rsqrt(jnp.mean(xv * xv, axis=-1, keepdims=True) + NORM_EPS)
            shift = mod_buf[:, 0:D]
            scale = mod_buf[:, D : 2 * D]
            hv = (((xv * r) * ng_ref[...]) * (1.0 + scale) + shift).astype(BF16)
            hbuf[rows, :] = hv
            h_ref[...] = hv

        proj_ref[...] = _dot_nt(hbuf[rows, :], wall[chip ^ j])

        @pl.when((chip ^ j) == 0)
        def _():
            cos, s1, s2 = cos_ref[...], s1_ref[...], s2_ref[...]
            for g in range(n_rope):
                cols = slice(128 * g, 128 * (g + 1))
                proj_ref[:, cols] = _rope(proj_ref[:, cols], cos, s1, s2)

        @pl.when((j == n_j - 1) & (i == n_i - 1))
        def _():
            shard_copy(0, me, sibling).wait_send()
            for rel in range(1, 4):
                shard_copy(rel, me, _flip(me, 2 * rel)).wait_send()
                shard_copy(3 + rel, _flip(me, 2 * rel), sibling).wait_send()
            for q in range(N_DEV // 2):
                for cp in keep_tile(q):
                    cp.wait()
            for cp in _direct_gather_copies(pack_hbm, all_hbm, p_send, p_recv, arriving=True):
                cp.wait_recv()
            for cp in _direct_gather_copies(pack_hbm, all_hbm, p_send, p_recv, arriving=False):
                cp.wait_send()
            own_pack.wait()

    tab = pl.BlockSpec((tm, 128), lambda j, i, c: (i, 0))
    any_spec = pl.BlockSpec(memory_space=pl.ANY)
    return pl.pallas_call(
        body,
        name="fwd_in",
        grid_spec=pltpu.PrefetchScalarGridSpec(
            num_scalar_prefetch=1,
            grid=(n_j, n_i),
            in_specs=[
                pl.BlockSpec((tm, D), lambda j, i, c: (jnp.where(j == 0, i, 0), 0)),
                pl.BlockSpec(w_ada.shape, lambda j, i, c: (0, 0)),
                pl.BlockSpec((1, n_ada), lambda j, i, c: (0, 0)),
                pl.BlockSpec((1, D), lambda j, i, c: (0, 0)),
                tab, tab, tab, any_spec, any_spec, any_spec, any_spec,
            ],
            out_specs=[pl.BlockSpec((tm, tn), lambda j, i, c: (i, c[0] ^ j)),
                       pl.BlockSpec((tm, D), lambda j, i, c: (jnp.where(j == 0, i, n_i - 1), 0)),
                       pl.BlockSpec((1, 3 * D), lambda j, i, c: (0, 0)),
                       any_spec, any_spec, any_spec, any_spec],
            scratch_shapes=[
                pltpu.VMEM((N_DEV // 2, tn, D), BF16),
                pltpu.VMEM((t, D), BF16),
                pltpu.VMEM((N_DEV,) + c_rows.shape, F32),
                pltpu.VMEM((N_DEV, n_ada), F32),
                pltpu.VMEM((N_DEV, N_DEV, n_ada), F32),
                pltpu.VMEM((1, 3 * D), F32),
                pltpu.SemaphoreType.DMA((N_DEV - 1,)),
                pltpu.SemaphoreType.DMA((N_DEV - 1,)),
                pltpu.SemaphoreType.DMA((N_DEV,)),
                pltpu.SemaphoreType.DMA,
                pltpu.SemaphoreType.DMA((N_DEV - 1,)),
                pltpu.SemaphoreType.DMA((N_DEV - 1,)),
                pltpu.SemaphoreType.DMA,
                pltpu.SemaphoreType.DMA((N_DEV - 1,)),
                pltpu.SemaphoreType.DMA((N_DEV - 1,)),
                pltpu.SemaphoreType.DMA,
                pltpu.SemaphoreType.DMA((N_DEV - 1,)),
                pltpu.SemaphoreType.DMA((N_DEV - 1,)),
                pltpu.SemaphoreType.DMA,
                pltpu.SemaphoreType.DMA((N_DEV - 1,)),
                pltpu.SemaphoreType.DMA((N_DEV - 1,)),
            ],
        ),
        out_shape=[jax.ShapeDtypeStruct((t, IN_W), F32), jax.ShapeDtypeStruct((t, D), BF16),
                   jax.ShapeDtypeStruct((1, 3 * D), F32),
                   jax.ShapeDtypeStruct((N_DEV,) + w_shard.shape, w_shard.dtype),
                   jax.ShapeDtypeStruct((N_DEV,) + pack.shape, pack.dtype),
                   jax.ShapeDtypeStruct((N_DEV,) + c_rows.shape, F32),
                   jax.ShapeDtypeStruct((N_DEV,) + conv_rows.shape, F32)],
        compiler_params=_params(2),
    )(where, x, w_ada, b_cols, norm_g, *tables, w_shard, pack, c_rows, conv_rows)


def _rope_tables(positions):
    half = ROT_DIM // 2
    inv_freq = ROPE_THETA ** (-jnp.arange(0, ROT_DIM, 2, dtype=F32) / ROT_DIM)
    pos = positions.astype(F32)
    lane = np.arange(128) % HEAD_DIM
    freq_lane = jnp.where(lane < ROT_DIM, jnp.tile(inv_freq, 128 // half), 0.0)
    ang = pos[:, None] * freq_lane[None, :]
    sin = jnp.sin(ang)
    lane_tables = (jnp.cos(ang), jnp.where(lane < half, -sin, 0.0), jnp.where((lane >= half) & (lane < ROT_DIM), sin, 0.0))
    ang_t = inv_freq[:, None] * pos[None, :]
    return lane_tables, (jnp.cos(ang_t), jnp.sin(ang_t))


def _rope(tv, cos, s1, s2):
    cols = []
    for g in range(tv.shape[1] // 128):
        blk = tv[:, 128 * g : 128 * (g + 1)]
        cols.append(blk * cos + pltpu.roll(blk, 128 - 8, 1) * s1 + pltpu.roll(blk, 8, 1) * s2)
    return cols[0] if len(cols) == 1 else jnp.concatenate(cols, axis=1)


def _rope_transposed(gv, cos, s1, s2):
    cols = []
    for g in range(gv.shape[1] // 128):
        blk = gv[:, 128 * g : 128 * (g + 1)]
        cols.append(blk * cos + pltpu.roll(blk * s1, 8, 1) + pltpu.roll(blk * s2, 128 - 8, 1))
    return cols[0] if len(cols) == 1 else jnp.concatenate(cols, axis=1)


def _lane_lo():
    return lax.broadcasted_iota(jnp.int32, (1, 128), 1) < HEAD_DIM


def _row_lo():
    return lax.broadcasted_iota(jnp.int32, (128, 1), 0) < HEAD_DIM


def _split_heads(tile, j):
    lo = _lane_lo()
    parts = []
    for g in (2 * j, 2 * j + 1):
        blk = tile[:, 128 * g : 128 * (g + 1)]
        parts += [jnp.where(lo, blk, 0.0), jnp.where(lo, 0.0, blk)]
    return jnp.concatenate(parts, axis=0)


def _dup_lanes(kv, j):
    rolled = pltpu.roll(kv, HEAD_DIM, 1)
    lo = _lane_lo()
    return jnp.where(lo, kv, rolled) if j == 0 else jnp.where(lo, rolled, kv)


def _fold_lanes(acc):
    return acc + pltpu.roll(acc, HEAD_DIM, 1)


def _fill_bias_t(bias_ref):
    kj = lax.broadcasted_iota(jnp.int32, (256, 128), 0)
    qi = lax.broadcasted_iota(jnp.int32, (256, 128), 1)
    diff = qi + WINDOW - kj
    band = (diff >= 0) & (diff < WINDOW)
    bias_ref[0] = jnp.where(band, 0.0, -1e30)
    bias_ref[1] = jnp.where(band & (kj >= 128), 0.0, -1e30)


def _sink_row(sink_ref, pair, j):
    return jnp.concatenate([jnp.full((1, 128), sink_ref[0, 8 * pair + 4 * j + h], F32) for h in range(4)], axis=1)


def _softmax_t(q4, kw, bias4, sink):
    s = _dot_nt(kw, q4) + bias4
    m = jnp.maximum(jnp.max(s, axis=0, keepdims=True), sink)
    p = jnp.exp(s - m)
    e_sink = jnp.exp(sink - m)
    denom = jnp.sum(p, axis=0, keepdims=True) + e_sink
    return p, e_sink, denom


def _attn_specs(t, tq):
    return dict(
        q=pl.BlockSpec((tq, 512), lambda p, i: (i, p)),
        k=pl.BlockSpec((t, 128), lambda p, i: (0, SEG_K // 128 + p)),
        v=pl.BlockSpec((t, 128), lambda p, i: (0, SEG_V // 128 + p)),
        sink=pl.BlockSpec(memory_space=pltpu.SMEM),
        act=pl.BlockSpec((tq, 512), lambda p, i: (i, p)),
    )


def _attn_fwd(proj, sinks):
    t = proj.shape[0]
    tq = min(512, t)
    nb = tq // 128
    n_tiles = t // tq
    sp = _attn_specs(t, tq)

    def body(q_ref, k_ref, v_ref, sink_ref, o_ref, kd, vdt, bias_ref):
        pair = pl.program_id(0)
        i = pl.program_id(1)

        @pl.when(i == 0)
        def _():
            kv, vv = k_ref[...], v_ref[...]
            for j in range(2):
                kd[j, 0:128, :] = jnp.zeros((128, 128), BF16)
                kd[j, 128:, :] = _dup_lanes(kv, j).astype(BF16)
                vdt[j, :, 0:128] = jnp.zeros((128, 128), BF16)
                vdt[j, :, 128:] = _dup_lanes(vv, j).T.astype(BF16)
            _fill_bias_t(bias_ref)

        row_lo = _row_lo()

        def block(b, carry):
            r0 = pl.multiple_of(b * 128, 128)
            gb = i * nb + b
            keys = pl.ds(pl.multiple_of(gb * 128, 128), 256)
            rows = pl.ds(r0, 128)
            qt = q_ref[rows, :] * (1.0 / math.sqrt(HEAD_DIM))
            bias = bias_ref[jnp.where(gb == 0, 1, 0)]
            bias4 = jnp.concatenate([bias] * 4, axis=1)
            cols = []
            for j in range(2):
                q4 = _split_heads(qt, j).astype(BF16)
                p, _, denom = _softmax_t(q4, kd[j, keys, :], bias4, _sink_row(sink_ref, pair, j))
                ot = _dot(vdt[j, :, keys], p.astype(BF16)) * (1.0 / denom)
                for gi in range(2):
                    z = jnp.where(row_lo, ot[:, 256 * gi : 256 * gi + 128], ot[:, 256 * gi + 128 : 256 * gi + 256])
                    cols.append(z.T)
            o_ref[rows, :] = jnp.concatenate(cols, axis=1).astype(BF16)
            return carry

        lax.fori_loop(0, nb, block, 0, unroll=True)

    return pl.pallas_call(
        body,
        name="attn_fwd",
        grid=(2, n_tiles),
        in_specs=[sp["q"], sp["k"], sp["v"], sp["sink"]],
        out_specs=sp["act"],
        out_shape=jax.ShapeDtypeStruct((t, D), BF16),
        scratch_shapes=[pltpu.VMEM((2, t + 128, 128), BF16), pltpu.VMEM((2, 128, t + 128), BF16), pltpu.VMEM((2, 256, 128), F32)],
        compiler_params=_params(2),
    )(proj, proj, proj, sinks)


def _attn_bwd(proj, o, d_o, tables, sinks, part_3):
    t = proj.shape[0]
    tq = min(512, t)
    nb = tq // 128
    n_tiles = t // tq
    (cos, s1, s2), (cos_t, sin_t) = tables
    sp = _attn_specs(t, tq)
    scale = 1.0 / math.sqrt(HEAD_DIM)
    half = ROT_DIM // 2

    def unrope_rows(z, c, s):
        out = []
        for base in (0, HEAD_DIM):
            g0, g1 = z[base : base + half], z[base + half : base + ROT_DIM]
            out += [g0 * c + g1 * s, g1 * c - g0 * s, z[base + ROT_DIM : base + HEAD_DIM]]
        return jnp.concatenate(out, axis=0)

    def swap(part_hbm, recv_hbm, send_sem, recv_sem):
        sibling = _flip(_position(), 1)
        return pltpu.make_async_remote_copy(
            src_ref=part_hbm.at[:, :, sibling[2]], dst_ref=recv_hbm, send_sem=send_sem, recv_sem=recv_sem,
            device_id=sibling, device_id_type=MESH,
        )

    def body(q_ref, k_ref, v_ref, o_ref, do_ref, ct_ref, st_ref, ck, s1k, s2k, sink_ref, part_hbm,
             dq_ref, dk_ref, dv_ref, dsink_ref, recv_hbm, kd, kdt, vd, bias_ref, dkacc, dvacc, send_sem, recv_sem):
        pair = pl.program_id(0)
        i = pl.program_id(1)

        @pl.when((pair == 0) & (i == 0))
        def _():
            swap(part_hbm, recv_hbm, send_sem, recv_sem).start()

        @pl.when((pair == 1) & (i == n_tiles - 1))
        def _():
            swap(part_hbm, recv_hbm, send_sem, recv_sem).wait()

        @pl.when(i == 0)
        def _():
            kv, vv = k_ref[...], v_ref[...]
            for j in range(2):
                kdup = _dup_lanes(kv, j)
                kd[j, 0:128, :] = jnp.zeros((128, 128), BF16)
                kd[j, 128:, :] = kdup.astype(BF16)
                kdt[j, :, 0:128] = jnp.zeros((128, 128), BF16)
                kdt[j, :, 128:] = kdup.T.astype(BF16)
                vd[j, 0:128, :] = jnp.zeros((128, 128), BF16)
                vd[j, 128:, :] = _dup_lanes(vv, j).astype(BF16)
            _fill_bias_t(bias_ref)
            dkacc[...] = jnp.zeros_like(dkacc)
            dvacc[...] = jnp.zeros_like(dvacc)
            dsink_ref[...] = jnp.zeros_like(dsink_ref)

        row_lo = _row_lo()
        ones = jnp.ones((8, 128), BF16)

        def block(b, carry):
            r0 = pl.multiple_of(b * 128, 128)
            gb = i * nb + b
            keys = pl.ds(pl.multiple_of(gb * 128, 128), 256)
            rows = pl.ds(r0, 128)
            qt = q_ref[rows, :] * scale
            dob = do_ref[rows, :].astype(F32)
            prod = dob * o_ref[rows, :].astype(F32)
            ct, st = ct_ref[:, rows], st_ref[:, rows]
            bias = bias_ref[jnp.where(gb == 0, 1, 0)]
            bias4 = jnp.concatenate([bias] * 4, axis=1)
            cols = []
            for j in range(2):
                q4 = _split_heads(qt, j).astype(BF16)
                kw = kd[j, keys, :]
                p, e_sink, denom = _softmax_t(q4, kw, bias4, _sink_row(sink_ref, pair, j))
                inv = 1.0 / denom
                prob = p * inv
                do4 = _split_heads(dob, j).astype(BF16)
                delta = _dot_nt(ones, _split_heads(prod, j).astype(BF16))[0:1, :]
                dp = _dot_nt(vd[j, keys, :], do4)
                ds = (prob * (dp - delta)).astype(BF16)
                dkacc[j, keys, :] = dkacc[j, keys, :] + _dot(ds, q4)
                dvacc[j, keys, :] = dvacc[j, keys, :] + _dot(prob.astype(BF16), do4)
                dqt = _dot(kdt[j, :, keys], ds) * scale
                dsink = -(e_sink * inv) * delta
                for h in range(4):
                    row = 4 * j + h
                    total = jnp.sum(dsink[:, 128 * h : 128 * (h + 1)], axis=1, keepdims=True)
                    dsink_ref[0, row : row + 1, :] = dsink_ref[0, row : row + 1, :] + jnp.broadcast_to(total, (1, 128))
                for gi in range(2):
                    z = jnp.where(row_lo, dqt[:, 256 * gi : 256 * gi + 128], dqt[:, 256 * gi + 128 : 256 * gi + 256])
                    cols.append(unrope_rows(z, ct, st).T)
            dq_ref[rows, :] = jnp.concatenate(cols, axis=1).astype(BF16)
            return carry

        lax.fori_loop(0, nb, block, 0, unroll=True)

        @pl.when(i == n_tiles - 1)
        def _():
            lo = _lane_lo()
            dk = jnp.where(lo, _fold_lanes(dkacc[0, 128:, :]), _fold_lanes(dkacc[1, 128:, :]))
            dk_ref[...] = _rope_transposed(dk, ck[...], s1k[...], s2k[...]).astype(BF16)
            dv_ref[...] = jnp.where(lo, _fold_lanes(dvacc[0, 128:, :]), _fold_lanes(dvacc[1, 128:, :])).astype(BF16)

    kv_out = pl.BlockSpec((t, 128), lambda p, i: (0, p))
    tab_t = pl.BlockSpec((half, tq), lambda p, i: (0, i))
    tab_k = pl.BlockSpec((t, 128), lambda p, i: (0, 0))
    return pl.pallas_call(
        body,
        name="attn_bwd",
        grid=(2, n_tiles),
        in_specs=[sp["q"], sp["k"], sp["v"], sp["act"], sp["act"], tab_t, tab_t, tab_k, tab_k, tab_k, sp["sink"],
                  pl.BlockSpec(memory_space=pl.ANY)],
        out_specs=[sp["act"], kv_out, kv_out, pl.BlockSpec((1, 8, 128), lambda p, i: (p, 0, 0)), pl.BlockSpec(memory_space=pl.ANY)],
        out_shape=[
            jax.ShapeDtypeStruct((t, D), BF16),
            jax.ShapeDtypeStruct((t, 256), BF16),
            jax.ShapeDtypeStruct((t, 256), BF16),
            jax.ShapeDtypeStruct((2, 8, 128), F32),
            jax.ShapeDtypeStruct((3, 4, ROW_SHARD, D), F32),
        ],
        scratch_shapes=[pltpu.VMEM((2, t + 128, 128), BF16), pltpu.VMEM((2, 128, t + 128), BF16), pltpu.VMEM((2, t + 128, 128), BF16),
                        pltpu.VMEM((2, 256, 128), F32), pltpu.VMEM((2, t + 128, 128), F32), pltpu.VMEM((2, t + 128, 128), F32),
                        pltpu.SemaphoreType.DMA, pltpu.SemaphoreType.DMA],
        compiler_params=_params(2),
    )(proj, proj, proj, o, d_o, cos_t, sin_t, cos, s1, s2, sinks, part_3)


HALF = 512


def _softplus(y):
    u = jnp.exp(-jnp.abs(y))
    w = 1.0 + u
    log1p = jnp.where(w == 1.0, u, jnp.log(w) * (u / jnp.where(w == 1.0, 1.0, w - 1.0)))
    return jnp.maximum(y, 0.0) + log1p


def _gate_block(w_ref, blk):
    return w_ref[:, :, 256 * blk : 256 * (blk + 1)].reshape(256, 256)


def _gate_matmul(act_bf, w_ref):
    return jnp.concatenate([_dot(act_bf[:, 256 * blk : 256 * (blk + 1)], _gate_block(w_ref, blk)) for blk in range(2)], axis=1)


def _gate_matmul_t(grad_bf, w_ref):
    return jnp.concatenate([_dot_nt(grad_bf[:, 256 * blk : 256 * (blk + 1)], _gate_block(w_ref, blk)) for blk in range(2)], axis=1)


def _conv_and_gates(xe_ref, tm, cw_ref, cb_ref, wa_ref, wx_ref, ba_ref, bx_ref, lam_ref, keep):
    xc = cb_ref[...] + sum(cw_ref[k : k + 1, :] * xe_ref[pl.ds(5 + k, tm), :] for k in range(CONV_WIDTH))
    xc_bf = xc.astype(BF16)
    r = _sigmoid(_gate_matmul(xc_bf, wa_ref) + ba_ref[...])
    ig = _sigmoid(_gate_matmul(xc_bf, wx_ref) + bx_ref[...])
    sp = _softplus(-lam_ref[...])
    log_a = -LRU_C * r * sp
    tanh = jnp.tanh(-log_a)
    mult = jnp.sqrt(2.0 * tanh / (1.0 + tanh))
    a = jnp.where(keep, jnp.exp(log_a), 0.0)
    mult = jnp.where(keep, mult, 1.0)
    return xc, xc_bf, r, ig, sp, a, mult


def _scan_rows(a_ref, b_ref, h_ref, carry_ref, tm, reverse):
    width = a_ref.shape[1]
    row = lax.broadcasted_iota(jnp.int32, (8, width), 0)

    def group(g, carry):
        gg = tm // 8 - 1 - g if reverse else g
        r0 = pl.multiple_of(gg * 8, 8)
        a = a_ref[pl.ds(r0, 8), :]
        b = b_ref[pl.ds(r0, 8), :]
        for s in (1, 2, 4):
            shift = 8 - s if reverse else s
            a_sh = pltpu.roll(a, shift, 0)
            b_sh = pltpu.roll(b, shift, 0)
            ok = (row < 8 - s) if reverse else (row >= s)
            b = jnp.where(ok, a * b_sh + b, b)
            a = jnp.where(ok, a * a_sh, a)
        h = a * carry + b
        h_ref[pl.ds(r0, 8), :] = h
        last = h[0:1, :] if reverse else h[7:8, :]
        return jnp.broadcast_to(last, (8, width))

    carry_ref[...] = lax.fori_loop(0, tm // 8, group, carry_ref[...])


def _rnn_specs(t, tm, order):
    n = t // tm
    return dict(
        tile=pl.BlockSpec((tm, HALF), lambda ch, i: (order(i, n), ch)),
        before=lambda col0: pl.BlockSpec((8, HALF), lambda ch, i: (jnp.maximum(order(i, n) * (tm // 8) - 1, 0), col0 + ch)),
        proj_tile=lambda col0: pl.BlockSpec((tm, HALF), lambda ch, i: (order(i, n), col0 + ch)),
        keep=pl.BlockSpec((tm, 1), lambda ch, i: (order(i, n), 0)),
        gates=lambda which: pl.BlockSpec((N_DEV, GATE_SHARD, HALF), lambda ch, i: (0, PACK_GATE // GATE_SHARD + which, ch)),
        gate_grads=pl.BlockSpec((2, 2, 256, 256), lambda ch, i: (0, ch, 0, 0)),
        vec=pl.BlockSpec((1, HALF), lambda ch, i: (0, ch)),
        conv=pl.BlockSpec((CONV_WIDTH, HALF), lambda ch, i: (0, ch)),
    )


def _rnn_fwd(proj, keep, w_all, rg_ba, rg_bx, rg_lambda, conv_w, conv_b):
    t = proj.shape[0]
    tm = min(512, t)
    sp = _rnn_specs(t, tm, lambda i, n: i)

    def body(xr_ref, xb_ref, keep_ref, wa_ref, wx_ref, ba_ref, bx_ref, lam_ref, cw_ref, cb_ref, hh_ref, xe, a_buf, b_buf, carry):
        i = pl.program_id(1)

        @pl.when(i == 0)
        def _():
            carry[...] = jnp.zeros_like(carry)

        xe[0:8, :] = jnp.where(i > 0, xb_ref[...], 0.0)
        xe[8:, :] = xr_ref[...]
        keep_rows = keep_ref[...] != 0.0
        xc, _, _, ig, _, a, mult = _conv_and_gates(xe, tm, cw_ref, cb_ref, wa_ref, wx_ref, ba_ref, bx_ref, lam_ref, keep_rows)
        a_buf[...] = a
        b_buf[...] = mult * (ig * xc)
        _scan_rows(a_buf, b_buf, hh_ref, carry, tm, reverse=False)

    return pl.pallas_call(
        body,
        name="rnn_fwd",
        grid=(2, t // tm),
        in_specs=[sp["proj_tile"](SEG_XR // HALF), sp["before"](SEG_XR // HALF), sp["keep"], sp["gates"](0), sp["gates"](1),
                  sp["vec"], sp["vec"], sp["vec"], sp["conv"], sp["vec"]],
        out_specs=sp["tile"],
        out_shape=jax.ShapeDtypeStruct((t, D), F32),
        scratch_shapes=[pltpu.VMEM((tm + 8, HALF), F32), pltpu.VMEM((tm, HALF), F32), pltpu.VMEM((tm, HALF), F32), pltpu.VMEM((8, HALF), F32)],
        compiler_params=_params(2),
    )(proj, proj, keep, w_all, w_all, rg_ba, rg_bx, rg_lambda, conv_w, conv_b)


def _rnn_bwd(proj, hh, d_hh, keep, w_all, rg_ba, rg_bx, rg_lambda, conv_w, conv_b, sum_3_bf):
    t = proj.shape[0]
    tm = min(512, t)
    n_tiles = t // tm
    sp = _rnn_specs(t, tm, lambda i, n: n - 1 - i)

    def body(xr_ref, xb_ref, hh_ref, hb_ref, dhh_ref, keep_ref, wa_ref, wx_ref, ba_ref, bx_ref, lam_ref, cw_ref, cb_ref,
             s3_hbm, dxr_ref, gw_ref, gvec_ref, l3_hbm, xe, he, ae, c_buf, g_buf, dxe, carry, a_next, send_sems, recv_sems):
        ch = pl.program_id(0)
        i = pl.program_id(1)
        first_tile = i == n_tiles - 1
        views = [lambda ref, peer: ref.at[:, _chip_of(peer)]]

        @pl.when((ch == 0) & (i == 0))
        def _():
            for cp in _chip_copies([s3_hbm], [l3_hbm], views, send_sems, recv_sems):
                cp.start()

        @pl.when((ch == 1) & (i == n_tiles - 1))
        def _():
            for cp in _chip_copies([s3_hbm], [l3_hbm], views, send_sems, recv_sems):
                cp.wait_recv()
                cp.wait_send()

        @pl.when(i == 0)
        def _():
            carry[...] = jnp.zeros_like(carry)
            a_next[...] = jnp.zeros_like(a_next)
            dxe[tm:, :] = jnp.zeros((8, HALF), F32)
            gw_ref[...] = jnp.zeros_like(gw_ref)
            gvec_ref[...] = jnp.zeros_like(gvec_ref)

        xe[0:8, :] = jnp.where(first_tile, 0.0, xb_ref[...])
        xe[8:, :] = xr_ref[...]
        he[0:8, :] = jnp.where(first_tile, 0.0, hb_ref[...])
        he[8:, :] = hh_ref[...]
        keep_rows = keep_ref[...] != 0.0
        xc, xc_bf, r, ig, sp_, a, mult = _conv_and_gates(xe, tm, cw_ref, cb_ref, wa_ref, wx_ref, ba_ref, bx_ref, lam_ref, keep_rows)

        ae[0:tm, :] = a
        ae[tm:, :] = a_next[...]
        a_next[...] = jnp.broadcast_to(a[0:1, :], (8, HALF))
        c_buf[...] = ae[pl.ds(1, tm), :]
        g_buf[...] = dhh_ref[...]
        _scan_rows(c_buf, g_buf, g_buf, carry, tm, reverse=True)
        g = g_buf[...]

        h_prev = he[pl.ds(7, tm), :]
        bb = ig * xc
        d_ig = g * mult * xc
        d_xc = g * mult * ig
        d_log_a = jnp.where(keep_rows, g * h_prev * a - (g * bb) * (a * a) / mult, 0.0)
        d_r = d_log_a * (-LRU_C * sp_)
        d_sp = jnp.sum(d_log_a * (-LRU_C * r), axis=0, keepdims=True)
        d_zr = d_r * r * (1.0 - r)
        d_zi = d_ig * ig * (1.0 - ig)
        d_zr_bf = d_zr.astype(BF16)
        d_zi_bf = d_zi.astype(BF16)
        d_xc = d_xc + _gate_matmul_t(d_zr_bf, wa_ref) + _gate_matmul_t(d_zi_bf, wx_ref)
        for blk in range(2):
            cols = slice(256 * blk, 256 * (blk + 1))
            gw_ref[0, blk] = gw_ref[0, blk] + _dot_tn(xc_bf[:, cols], d_zr_bf[:, cols])
            gw_ref[1, blk] = gw_ref[1, blk] + _dot_tn(xc_bf[:, cols], d_zi_bf[:, cols])

        dxe[0:tm, :] = d_xc
        shifted = [dxe[pl.ds(3 - k, tm), :] for k in range(CONV_WIDTH)]
        dxr = sum(cw_ref[k : k + 1, :] * shifted[k] for k in range(CONV_WIDTH))
        dxr_ref[...] = dxr.astype(BF16)
        dxe[tm:, :] = d_xc[0:8, :]

        lam = lam_ref[...]
        xr_tile = xr_ref[...]
        sums = [
            jnp.sum(d_zr, axis=0, keepdims=True),
            jnp.sum(d_zi, axis=0, keepdims=True),
            d_sp * (-_sigmoid(-lam)),
            jnp.sum(d_xc, axis=0, keepdims=True),
        ] + [jnp.sum(xr_tile * shifted[k], axis=0, keepdims=True) for k in range(CONV_WIDTH)]
        gvec_ref[...] = gvec_ref[...] + jnp.concatenate(sums, axis=0)

    return pl.pallas_call(
        body,
        name="rnn_bwd",
        grid=(2, n_tiles),
        in_specs=[
            sp["proj_tile"](SEG_XR // HALF), sp["before"](SEG_XR // HALF), sp["tile"], sp["before"](0), sp["tile"], sp["keep"],
            sp["gates"](0), sp["gates"](1), sp["vec"], sp["vec"], sp["vec"], sp["conv"], sp["vec"], pl.BlockSpec(memory_space=pl.ANY),
        ],
        out_specs=[sp["tile"], sp["gate_grads"], pl.BlockSpec((8, HALF), lambda ch, i: (0, ch)), pl.BlockSpec(memory_space=pl.ANY)],
        out_shape=[
            jax.ShapeDtypeStruct((t, D), BF16),
            jax.ShapeDtypeStruct((2, 4, 256, 256), F32),
            jax.ShapeDtypeStruct((8, D), F32),
            jax.ShapeDtypeStruct((len(CHIP_FLIPS), 3, ROW_SHARD, D), BF16),
        ],
        scratch_shapes=[
            pltpu.VMEM((tm + 8, HALF), F32),
            pltpu.VMEM((tm + 8, HALF), F32),
            pltpu.VMEM((tm + 8, HALF), F32),
            pltpu.VMEM((tm, HALF), F32),
            pltpu.VMEM((tm, HALF), F32),
            pltpu.VMEM((tm + 8, HALF), F32),
            pltpu.VMEM((8, HALF), F32),
            pltpu.VMEM((8, HALF), F32),
            pltpu.SemaphoreType.DMA((len(CHIP_FLIPS),)),
            pltpu.SemaphoreType.DMA((len(CHIP_FLIPS),)),
        ],
        compiler_params=_params(2),
    )(proj, proj, hh, hh, d_hh, keep, w_all, w_all, rg_ba, rg_bx, rg_lambda, conv_w, conv_b, sum_3_bf)


def _mid(x, target, o, hh, proj, w_all, gate, final_g):
    t = x.shape[0]
    tm = min(256, t)
    n_tiles = t // tm
    full = pl.BlockSpec((tm, D), lambda i: (i, 0))
    half = lambda col: pl.BlockSpec((tm, 512), lambda i: (i, col))
    vec = pl.BlockSpec((1, D), lambda i: (0, 0))
    any_spec = pl.BlockSpec(memory_space=pl.ANY)

    def body(x_ref, tg_ref, o_ref, hh_ref, ga0, ga1, gr0, gr1, ma0, ma1, mr0, mr1, w_hbm, gate_ref, fg_ref,
             dres_ref, do_ref, dhh_ref, dga_ref, dgr_ref, dma_ref, dmr_ref, gw_hbm, vec_ref, w, gw, sem, w_sems):
        i = pl.program_id(0)

        @pl.when(i == 0)
        def _():
            loads = [
                pltpu.make_async_copy(
                    w_hbm.at[s, pl.ds(ROW_SHARD * k, ROW_SHARD), :], w.at[k, pl.ds(ROW_SHARD * s, ROW_SHARD), :], w_sems.at[k * N_DEV + s]
                )
                for k in range(3)
                for s in range(N_DEV)
            ]
            for cp in loads:
                cp.start()
            for cp in loads:
                cp.wait()
            gw[...] = jnp.zeros_like(gw)
            vec_ref[...] = jnp.zeros_like(vec_ref)

        cat = lambda a, b: jnp.concatenate([a[...], b[...]], axis=1)
        ga, gr, ma, mr = cat(ga0, ga1), cat(gr0, gr1), cat(ma0, ma1), cat(mr0, mr1)
        ov, hv, gate_v, fg = o_ref[...].astype(F32), hh_ref[...], gate_ref[...], fg_ref[...]
        sg_a = _sigmoid(ga)
        sg_r = _sigmoid(gr)
        silu_a = ga * sg_a
        silu_r = gr * sg_r
        ya = (ov * silu_a).astype(BF16)
        yr = (hv * silu_r).astype(BF16)
        pa = _dot(ya, w[0])
        pr = _dot(yr, w[1])
        sa = _sigmoid(ma)
        sr = _sigmoid(mr)
        merged = (sa * pa + sr * pr).astype(BF16)
        mo = _dot(merged, w[2])
        out = x_ref[...] + gate_v * mo
        r2 = lax.rsqrt(jnp.mean(out * out, axis=-1, keepdims=True) + NORM_EPS)
        outn = out * r2
        err = outn * fg - tg_ref[...]
        loss = 0.5 * jnp.sum(jnp.mean(err * err, axis=-1, keepdims=True), axis=0, keepdims=True)
        dy = err * (1.0 / D)
        g_final = jnp.sum(dy * outn, axis=0, keepdims=True)
        doutn = dy * fg
        dout = r2 * (doutn - outn * jnp.mean(doutn * outn, axis=-1, keepdims=True))
        dres_ref[...] = dout
        d_gate = jnp.sum(dout * mo, axis=0, keepdims=True)
        dmo = (dout * gate_v).astype(BF16)
        dmerged = _dot_nt(dmo, w[2])
        gw[2] = gw[2] + _dot_tn(merged, dmo)
        dpa = (dmerged * sa).astype(BF16)
        dpr = (dmerged * sr).astype(BF16)
        dma_ref[...] = (dmerged * pa * (sa * (1.0 - sa))).astype(BF16)
        dmr_ref[...] = (dmerged * pr * (sr * (1.0 - sr))).astype(BF16)
        dya = _dot_nt(dpa, w[0])
        dyr = _dot_nt(dpr, w[1])
        gw[0] = gw[0] + _dot_tn(ya, dpa)
        gw[1] = gw[1] + _dot_tn(yr, dpr)
        do_ref[...] = (dya * silu_a).astype(BF16)
        dhh_ref[...] = dyr * silu_r
        dga_ref[...] = (dya * ov * (sg_a * (1.0 + ga * (1.0 - sg_a)))).astype(BF16)
        dgr_ref[...] = (dyr * hv * (sg_r * (1.0 + gr * (1.0 - sg_r)))).astype(BF16)
        zeros = jnp.zeros((5, D), F32)
        vec_ref[...] = vec_ref[...] + jnp.concatenate([g_final, d_gate, jnp.broadcast_to(loss, (1, D)), zeros], axis=0)

        @pl.when(i == n_tiles - 1)
        def _():
            cp = pltpu.make_async_copy(gw, gw_hbm, sem)
            cp.start()
            cp.wait()

    bf_tile = jax.ShapeDtypeStruct((t, D), BF16)
    f_tile = jax.ShapeDtypeStruct((t, D), F32)
    return pl.pallas_call(
        body,
        name="mid",
        grid=(n_tiles,),
        in_specs=[full, full, full, full,
                  half(SEG_GA // 512), half(SEG_GA // 512 + 1), half(SEG_GR // 512), half(SEG_GR // 512 + 1),
                  half(SEG_MA // 512), half(SEG_MA // 512 + 1), half(SEG_MR // 512), half(SEG_MR // 512 + 1),
                  any_spec, vec, vec],
        out_specs=[full, full, full, full, full, full, full, any_spec, pl.BlockSpec((8, D), lambda i: (0, 0))],
        out_shape=[f_tile, bf_tile, f_tile, bf_tile, bf_tile, bf_tile, bf_tile,
                   jax.ShapeDtypeStruct((3, D, D), F32), jax.ShapeDtypeStruct((8, D), F32)],
        scratch_shapes=[pltpu.VMEM((3, D, D), BF16), pltpu.VMEM((3, D, D), F32), pltpu.SemaphoreType.DMA, pltpu.SemaphoreType.DMA((3 * N_DEV,))],
        compiler_params=_params(1),
    )(x, target, o, hh, proj, proj, proj, proj, proj, proj, proj, proj, w_all, gate, final_g)


_SEGMENTS = ((SEG_Q, 1024), (SEG_K, 256), (SEG_V, 256), (SEG_GA, 1024), (SEG_XR, 1024), (SEG_GR, 1024), (SEG_MA, 1024), (SEG_MR, 1024))


def _seg_specs(tm):
    return [pl.BlockSpec((tm, width), lambda i: (i, 0)) for _, width in _SEGMENTS]


def _bwd_dh(dsegs, w_all, x, dres, mod, norm_g, sum_w_bf):
    t = x.shape[0]
    tm = min(512, t)
    n_tiles = t // tm
    n_seg = len(_SEGMENTS)
    full = pl.BlockSpec((tm, D), lambda i: (i, 0))
    views = [lambda ref, peer: ref.at[_chip_of(peer)]]

    def body(*refs):
        segs = refs[:n_seg]
        w_hbm, x_ref, dres_ref, mod_ref, ng_ref, sum_hbm, gx_ref, vec_ref, land_hbm, w, sems, send_sems, recv_sems = refs[n_seg:]
        i = pl.program_id(0)

        @pl.when(i == 0)
        def _():
            for cp in _chip_copies([sum_hbm], [land_hbm], views, send_sems, recv_sems):
                cp.start()
            loads = [
                pltpu.make_async_copy(w_hbm.at[s], w.at[pl.ds(W_IN_SHARD * s, W_IN_SHARD), :], sems.at[s])
                for s in range(N_DEV)
            ]
            for cp in loads:
                cp.start()
            for cp in loads:
                cp.wait()
            vec_ref[...] = jnp.zeros_like(vec_ref)

        dh = sum(_dot(seg[...], w[start : start + width, :]) for seg, (start, width) in zip(segs, _SEGMENTS))
        xv = x_ref[...]
        r1 = lax.rsqrt(jnp.mean(xv * xv, axis=-1, keepdims=True) + NORM_EPS)
        xn = xv * r1
        ng = ng_ref[...]
        scale1 = 1.0 + mod_ref[:, D : 2 * D]
        d_shift = jnp.sum(dh, axis=0, keepdims=True)
        d_scale = jnp.sum(dh * (xn * ng), axis=0, keepdims=True)
        g_norm = jnp.sum(dh * xn * scale1, axis=0, keepdims=True)
        dxn = dh * (ng * scale1)
        dx = r1 * (dxn - xn * jnp.mean(dxn * xn, axis=-1, keepdims=True))
        gx_ref[...] = dres_ref[...] + dx
        vec_ref[...] = vec_ref[...] + jnp.concatenate([d_shift, d_scale, g_norm, jnp.zeros((5, D), F32)], axis=0)

        @pl.when(i == n_tiles - 1)
        def _():
            for cp in _chip_copies([sum_hbm], [land_hbm], views, send_sems, recv_sems):
                cp.wait_recv()
                cp.wait_send()

    any_spec = pl.BlockSpec(memory_space=pl.ANY)
    n_copies = len(CHIP_FLIPS)
    return pl.pallas_call(
        body,
        name="bwd_dh",
        grid=(n_tiles,),
        in_specs=_seg_specs(tm) + [any_spec, full, full, pl.BlockSpec((1, 3 * D), lambda i: (0, 0)), pl.BlockSpec((1, D), lambda i: (0, 0)),
                                   any_spec],
        out_specs=[full, pl.BlockSpec((8, D), lambda i: (0, 0)), any_spec],
        out_shape=[jax.ShapeDtypeStruct((t, D), F32), jax.ShapeDtypeStruct((8, D), F32),
                   jax.ShapeDtypeStruct((n_copies, W_IN_SHARD, D), BF16)],
        scratch_shapes=[pltpu.VMEM((IN_W, D), BF16), pltpu.SemaphoreType.DMA((N_DEV,)),
                        pltpu.SemaphoreType.DMA((n_copies,)), pltpu.SemaphoreType.DMA((n_copies,))],
        compiler_params=_params(1),
    )(*dsegs, w_all, x, dres, mod, norm_g, sum_w_bf)


def _bwd_gw(dsegs, h, sum_g_bf):
    t = h.shape[0]
    tm = min(512, t)
    n_tiles = t // tm
    n_seg = len(_SEGMENTS)
    n_chips = N_DEV // 2
    views = [lambda ref, peer: ref.at[:, :, _chip_of(peer)]]

    def body(*refs):
        segs = refs[:n_seg]
        (h_ref, sg_hbm, own_hbm, recv_hbm, lg_hbm, acc, stage, own_sems, pair_send, pair_recv, send_sems, recv_sems) = refs[n_seg:]
        i = pl.program_id(0)

        @pl.when(i == 0)
        def _():
            for cp in _chip_copies([sg_hbm], [lg_hbm], views, send_sems, recv_sems):
                cp.start()
            acc[...] = jnp.zeros_like(acc)

        hv = h_ref[...]
        for seg, (start, width) in zip(segs, _SEGMENTS):
            acc[start : start + width, :] = acc[start : start + width, :] + _dot_tn(seg[...], hv)

        @pl.when(i == n_tiles - 1)
        def _():
            me = _position()
            sibling = _flip(me, 1)

            def rows(q, c):
                return pl.ds(pl.multiple_of(W_IN_SHARD * (2 * q + c), 8), W_IN_SHARD)

            def to_sibling(q):
                return pltpu.make_async_remote_copy(
                    src_ref=stage.at[q % 2], dst_ref=recv_hbm.at[q], send_sem=pair_send.at[q], recv_sem=pair_recv.at[q],
                    device_id=sibling, device_id_type=MESH,
                )

            kept = [pltpu.make_async_copy(acc.at[rows(q, me[2]), :], own_hbm.at[q], own_sems.at[q]) for q in range(n_chips)]
            for cp in kept:
                cp.start()
            for q in range(n_chips):
                if q >= 2:
                    to_sibling(q - 2).wait_send()
                stage[q % 2] = acc[rows(q, sibling[2]), :].astype(BF16)
                to_sibling(q).start()
            for q in range(n_chips - 2, n_chips):
                to_sibling(q).wait_send()
            for q in range(n_chips):
                to_sibling(q).wait_recv()
            for cp in kept:
                cp.wait()
            for cp in _chip_copies([sg_hbm], [lg_hbm], views, send_sems, recv_sems):
                cp.wait_recv()
                cp.wait_send()

    any_spec = pl.BlockSpec(memory_space=pl.ANY)
    n_copies = len(CHIP_FLIPS)
    return pl.pallas_call(
        body,
        name="bwd_gw",
        grid=(n_tiles,),
        in_specs=_seg_specs(tm) + [pl.BlockSpec((tm, D), lambda i: (i, 0)), any_spec],
        out_specs=[any_spec, any_spec, any_spec],
        out_shape=[jax.ShapeDtypeStruct((n_chips, W_IN_SHARD, D), F32),
                   jax.ShapeDtypeStruct((n_chips, W_IN_SHARD, D), BF16),
                   jax.ShapeDtypeStruct((len(CHIP_FLIPS), 2, 4, GATE_SHARD, 256), BF16)],
        scratch_shapes=[pltpu.VMEM((IN_W, D), F32), pltpu.VMEM((2, W_IN_SHARD, D), BF16),
                        pltpu.SemaphoreType.DMA((n_chips,)), pltpu.SemaphoreType.DMA((n_chips,)), pltpu.SemaphoreType.DMA((n_chips,)),
                        pltpu.SemaphoreType.DMA((n_copies,)), pltpu.SemaphoreType.DMA((n_copies,))],
        compiler_params=_params(1),
    )(*dsegs, h, sum_g_bf)


def _adamw(w, g, m, v):
    m = ADAM_B1 * m + (1.0 - ADAM_B1) * g
    v = ADAM_B2 * v + (1.0 - ADAM_B2) * (g * g)
    m_hat = m / (1.0 - ADAM_B1**ADAM_STEP)
    v_hat = v / (1.0 - ADAM_B2**ADAM_STEP)
    delta = -ADAM_LR * (m_hat / (jnp.sqrt(v_hat) + ADAM_EPS) + ADAM_WD * w)
    return delta, m, v


def _w_in_update(own, land, w_t, m_t, v_t):
    n_slots, rows, cols = land.shape
    rb = 208
    spec = pl.BlockSpec((rb, cols), lambda i: (i, 0))

    def body(o_ref, l_ref, w_ref, m_ref, v_ref, g_ref, d_ref, m2_ref, v2_ref):
        total = o_ref[...]
        for s in range(n_slots):
            total = total + l_ref[s].astype(F32)
        g_ref[...] = total
        d_ref[...], m2_ref[...], v2_ref[...] = _adamw(w_ref[...], total, m_ref[...], v_ref[...])

    shape = jax.ShapeDtypeStruct((rows, cols), F32)
    return pl.pallas_call(
        body,
        name="w_in_update",
        grid=(rows // rb,),
        in_specs=[spec, pl.BlockSpec((n_slots, rb, cols), lambda i: (0, i, 0)), spec, spec, spec],
        out_specs=[spec] * 4,
        out_shape=[shape] * 4,
        compiler_params=_params(1),
    )(own, land, w_t, m_t, v_t)


def _shard_update(own_3, land_3, own_g, land_g, weights, m_in, v_in):
    n = len(weights)

    def body(*refs):
        o3_ref, l3_ref, og_ref, lg_ref = refs[:4]
        w_refs, m_refs, v_refs = (refs[4 + n * k : 4 + n * (k + 1)] for k in range(3))
        outs = refs[4 + 3 * n :]
        for k in range(n):
            if k < 3:
                g = o3_ref[k]
                for slot in range(3):
                    g = g + l3_ref[slot, k].astype(F32)
            else:
                g = og_ref[k - 3]
                for slot in range(3):
                    g = g + lg_ref[slot, k - 3].astype(F32)
            d, m2, v2 = _adamw(w_refs[k][0], g, m_refs[k][0], v_refs[k][0])
            for which, val in enumerate((g, d, m2, v2)):
                outs[which * n + k][0] = val

    shapes = [jax.ShapeDtypeStruct(w.shape, F32) for w in weights]
    res = pl.pallas_call(
        body, name="shard_update", out_shape=shapes * 4, compiler_params=pltpu.CompilerParams(vmem_limit_bytes=VMEM_LIMIT)
    )(own_3, land_3, own_g, land_g, *weights, *m_in, *v_in)
    return [res[n * k : n * (k + 1)] for k in range(4)]


VEC_LAYOUT = (("b_ada", 3 * D), ("norm_g", D), ("attn_sinks", 128), ("conv_b", D), ("rg_ba", D), ("rg_bx", D), ("rg_lambda", D),
              ("final_g", D), ("conv_w", CONV_WIDTH * D), ("loss", 128))
VEC_OFFSET = {name: sum(w for _, w in VEC_LAYOUT[:k]) for k, (name, _) in enumerate(VEC_LAYOUT)}
VEC_ROWS = 112
VEC_WIDTH = VEC_ROWS * 128
assert sum(w for _, w in VEC_LAYOUT) <= VEC_WIDTH
N_REPLICATED = 8


def _small_update(vec_all, c_all_t, me, rep_w, rep_m, rep_v, ada, conv):
    n_ada = ada[0].shape[1]
    n_conv = conv[0].shape[1]

    def body(*refs):
        me_ref, vec_ref, ct_ref = refs[:3]
        w_refs, m_refs, v_refs = (refs[3 + N_REPLICATED * k : 3 + N_REPLICATED * (k + 1)] for k in range(3))
        pos = 3 + 3 * N_REPLICATED
        ada_refs, conv_refs = refs[pos : pos + 3], refs[pos + 3 : pos + 6]
        outs = refs[pos + 6 :]
        loss_ref, rep_outs = outs[0], outs[1 : 1 + 4 * N_REPLICATED]
        ada_outs, conv_outs = outs[1 + 4 * N_REPLICATED : 5 + 4 * N_REPLICATED], outs[5 + 4 * N_REPLICATED : 9 + 4 * N_REPLICATED]
        conv_buf = outs[9 + 4 * N_REPLICATED]

        total = vec_ref[0:1, :]
        for slot in range(1, N_DEV):
            total = total + vec_ref[slot : slot + 1, :]
        loss_ref[...] = total[:, VEC_OFFSET["loss"] : VEC_OFFSET["loss"] + 1]
        for k, (name, _) in enumerate(VEC_LAYOUT[:N_REPLICATED]):
            width = w_refs[k].shape[1]
            g = total[:, VEC_OFFSET[name] : VEC_OFFSET[name] + width]
            d, m2, v2 = _adamw(w_refs[k][...], g, m_refs[k][...], v_refs[k][...])
            for which, val in enumerate((g, d, m2, v2)):
                rep_outs[which * N_REPLICATED + k][...] = val

        device = me_ref[0]
        for k in range(CONV_WIDTH):
            conv_buf[k : k + 1, :] = total[:, VEC_OFFSET["conv_w"] + D * k : VEC_OFFSET["conv_w"] + D * (k + 1)]
        g_conv = conv_buf[:, pl.ds(pl.multiple_of(device * n_conv, 128), n_conv)]
        for ref, val in zip(conv_outs, (g_conv,) + _adamw(conv_refs[0][...], g_conv, conv_refs[1][...], conv_refs[2][...])):
            ref[...] = val

        dmod = vec_ref[:, pl.ds(pl.multiple_of(device * n_ada, 128), n_ada)]
        g_ada = _dot(ct_ref[...], dmod)
        for ref, val in zip(ada_outs, (g_ada,) + _adamw(ada_refs[0][...], g_ada, ada_refs[1][...], ada_refs[2][...])):
            ref[...] = val

    vmem = pl.BlockSpec(memory_space=pltpu.VMEM)
    rep_shapes = [jax.ShapeDtypeStruct(w.shape, F32) for w in rep_w]
    out_shape = ([jax.ShapeDtypeStruct((1, 1), F32)] + rep_shapes * 4 + [jax.ShapeDtypeStruct(ada[0].shape, F32)] * 4
                 + [jax.ShapeDtypeStruct(conv[0].shape, F32)] * 4)
    n_in = 3 + 3 * N_REPLICATED + 6
    res = pl.pallas_call(
        body,
        name="small_update",
        in_specs=[pl.BlockSpec(memory_space=pltpu.SMEM)] + [vmem] * (n_in - 1),
        out_specs=[vmem] * len(out_shape),
        out_shape=out_shape,
        scratch_shapes=[pltpu.VMEM((CONV_WIDTH, D), F32)],
        compiler_params=pltpu.CompilerParams(vmem_limit_bytes=VMEM_LIMIT),
    )(me, vec_all, c_all_t, *rep_w, *rep_m, *rep_v, *ada, *conv)
    loss = res[0]
    reps = [res[1 + N_REPLICATED * k : 1 + N_REPLICATED * (k + 1)] for k in range(4)]
    return loss, reps, res[1 + 4 * N_REPLICATED : 5 + 4 * N_REPLICATED], res[5 + 4 * N_REPLICATED : 9 + 4 * N_REPLICATED]


def kernel(x, c, positions, w_ada, b_ada, norm_g, w_in, attn_sinks, conv_w, conv_b, rg_wa, rg_ba, rg_wx, rg_bx, rg_lambda, w_attn_proj, w_rnn_proj, w_out, final_g, loss_target, m_w_ada, m_b_ada, m_norm_g, m_w_in, m_attn_sinks, m_conv_w, m_conv_b, m_rg_wa, m_rg_ba, m_rg_wx, m_rg_bx, m_rg_lambda, m_w_attn_proj, m_w_rnn_proj, m_w_out, m_final_g, v_w_ada, v_b_ada, v_norm_g, v_w_in, v_attn_sinks, v_conv_w, v_conv_b, v_rg_wa, v_rg_ba, v_rg_wx, v_rg_bx, v_rg_lambda, v_w_attn_proj, v_w_rnn_proj, v_w_out, v_final_g):
    my_x, my_y, my_c = _position()
    me = _slot((my_x, my_y, my_c))
    x2, target = x[0], loss_target[0]
    pos = positions[0]

    gate_rows = lambda w: w[0].transpose(1, 0, 2).reshape(GATE_SHARD, D)
    w_pack = jnp.concatenate([w_attn_proj[0], w_rnn_proj[0], w_out[0], gate_rows(rg_wa), gate_rows(rg_wx)], axis=0).astype(BF16)
    conv_rows = jnp.pad(conv_w[0], ((0, 8 - CONV_WIDTH), (0, 0)))
    w_in_t = w_in[0].T
    n_ada = w_ada.shape[2]
    b_cols = lax.dynamic_slice(b_ada, (0, me * n_ada), (1, n_ada))

    tables = _rope_tables(pos)
    where = jnp.stack([_chip_of((my_x, my_y)), me]).astype(jnp.int32)
    proj, h, mod, w_t_all, w_all, c_rows, conv_all = _fwd_in(
        x2, jnp.broadcast_to(c, (8, D)), conv_rows, w_ada[0], b_cols, norm_g, w_in_t.astype(BF16), tables[0], w_pack, where
    )
    c_all = c_rows[:, 0, :]
    conv_full = conv_all[:, :CONV_WIDTH, :].transpose(1, 0, 2).reshape(CONV_WIDTH, D)
    gate = mod[:, 2 * D :]
    o = _attn_fwd(proj, attn_sinks)
    keep = (pos != 0).astype(F32)[:, None]
    hh = _rnn_fwd(proj, keep, w_all, rg_ba, rg_bx, rg_lambda, conv_full, conv_b)

    dres, d_o, d_hh, d_ga, d_gr, d_ma, d_mr, gw3, mid_vec = _mid(x2, target, o, hh, proj, w_all, gate, final_g[None, :])
    index = jnp.stack([my_c, _chip_of((my_x, my_y))]).astype(jnp.int32)
    part_3 = gw3.reshape(3, 4, 2, ROW_SHARD, D)
    d_q, d_k, d_v, d_sink, recv_3 = _attn_bwd(proj, o, d_o, tables, attn_sinks, part_3)
    sum_3_bf, own_3 = _pair_add(
        "pair_add_3", part_3, recv_3, index, (4,), 0,
        pl.BlockSpec((3, None, None, ROW_SHARD, D), lambda q, idx: (0, q, idx[0], 0, 0)),
        pl.BlockSpec((3, None, ROW_SHARD, D), lambda q, idx: (0, q, 0, 0)),
        pl.BlockSpec((3, ROW_SHARD, D), lambda q, idx: (0, 0, 0)), (3, ROW_SHARD, D),
    )
    d_xr, g_gates, rnn_vec, land_3 = _rnn_bwd(proj, hh, d_hh, keep, w_all, rg_ba, rg_bx, rg_lambda, conv_full, conv_b, sum_3_bf)
    dsegs = (d_q, d_k, d_v, d_ga, d_xr, d_gr, d_ma, d_mr)
    part_g = g_gates.reshape(2, 4, 4, 2, GATE_SHARD, 256)
    (recv_g,) = _exchange("rs_pair_g", [part_g], [lambda ref, peer: ref.at[:, :, :, peer[2]]], [(2, 4, 4, GATE_SHARD, 256)], flips=(1,))
    sum_g_bf, own_g = _pair_add(
        "pair_add_g", part_g, recv_g[0], index, (4,), 0,
        pl.BlockSpec((2, 4, None, None, GATE_SHARD, 256), lambda q, idx: (0, 0, q, idx[0], 0, 0)),
        pl.BlockSpec((2, 4, None, GATE_SHARD, 256), lambda q, idx: (0, 0, q, 0, 0)),
        pl.BlockSpec((2, 4, GATE_SHARD, 256), lambda q, idx: (0, 0, 0, 0)), (2, 4, GATE_SHARD, 256),
    )
    part_w, recv_w, land_g = _bwd_gw(dsegs, h, sum_g_bf)
    w_rows = pl.BlockSpec((None, W_IN_SHARD, D), lambda q, idx: (q, 0, 0))
    sum_w_bf, own_w = _pair_add(
        "pair_add_w", part_w, recv_w, index, (4,), 0, w_rows, w_rows,
        pl.BlockSpec((W_IN_SHARD, D), lambda q, idx: (0, 0)), (W_IN_SHARD, D),
    )
    grad_x, dh_vec, land_w = _bwd_dh(dsegs, w_t_all, x2, dres, mod, norm_g, sum_w_bf)

    pieces = dict(
        b_ada=jnp.concatenate([dh_vec[0:1], dh_vec[1:2], mid_vec[1:2]], axis=1),
        norm_g=dh_vec[2:3],
        attn_sinks=jnp.pad(d_sink[:, :, 0].reshape(1, 16), ((0, 0), (0, 112))),
        conv_b=rnn_vec[3:4],
        rg_ba=rnn_vec[0:1],
        rg_bx=rnn_vec[1:2],
        rg_lambda=rnn_vec[2:3],
        final_g=mid_vec[0:1],
        conv_w=rnn_vec[4:8].reshape(1, CONV_WIDTH * D),
        loss=mid_vec[2:3, 0:128],
    )
    vec = jnp.concatenate([pieces[name] for name, _ in VEC_LAYOUT], axis=1)
    vec = jnp.pad(vec, ((0, 0), (0, VEC_WIDTH - vec.shape[1]))).reshape(VEC_ROWS, 128)
    (vec_all,) = _all_gather_direct("ag_vec", [vec])
    rep_w = [b_ada, norm_g, attn_sinks, conv_b, rg_ba, rg_bx, rg_lambda, final_g[None, :]]
    rep_m = [m_b_ada, m_norm_g, m_attn_sinks, m_conv_b, m_rg_ba, m_rg_bx, m_rg_lambda, m_final_g[None, :]]
    rep_v = [v_b_ada, v_norm_g, v_attn_sinks, v_conv_b, v_rg_ba, v_rg_bx, v_rg_lambda, v_final_g[None, :]]
    loss, reps, ada_res, conv_res = _small_update(
        vec_all.reshape(N_DEV, VEC_WIDTH), c_all.T, me.astype(jnp.int32).reshape(1), rep_w, rep_m, rep_v,
        (w_ada[0], m_w_ada[0], v_w_ada[0]), (conv_w[0], m_conv_w[0], v_conv_w[0]),
    )

    w_in_res = _w_in_update(own_w, land_w, w_in_t, m_w_in[0].T, v_w_in[0].T)
    shard_res = _shard_update(
        own_3, land_3, own_g, land_g,
        (w_attn_proj, w_rnn_proj, w_out, rg_wa, rg_wx),
        (m_w_attn_proj, m_w_rnn_proj, m_w_out, m_rg_wa, m_rg_wx),
        (v_w_attn_proj, v_w_rnn_proj, v_w_out, v_rg_wa, v_rg_wx),
    )

    def by_weight(k):
        b, n, s, cb, ba, bx, lam, fg = reps[k]
        ap, rp, out, wa, wx = shard_res[k]
        return [ada_res[k][None], b, n, w_in_res[k].T[None], s, conv_res[k][None], cb, wa, ba, wx, bx, lam, ap, rp, out, fg[0]]

    return (loss[0, 0], grad_x[None], *by_weight(0), *by_weight(1), *by_weight(2), *by_weight(3))
```

```python
import math

import jax
import jax.numpy as jnp
from jax import lax
from jax.experimental import pallas as pl
from jax.experimental.pallas import tpu as pltpu

F32 = jnp.float32
BF16 = jnp.bfloat16

D = 1024
HEAD_DIM = 64
ROT_DIM = 16
ROPE_THETA = 500000.0
WINDOW = 128
LRU_C = 8.0
CONV_WIDTH = 4
NORM_EPS = 1e-6
IN_W = 6656
SEG_Q, SEG_K, SEG_V, SEG_GA, SEG_XR, SEG_GR, SEG_MA, SEG_MR = 0, 1024, 1280, 1536, 2560, 3584, 4608, 5632
N_DEV = 8
W_IN_SHARD = IN_W // N_DEV
ROW_SHARD = D // N_DEV
GATE_SHARD = 256 // N_DEV

ADAM_LR = 0.001
ADAM_B1 = 0.9
ADAM_B2 = 0.999
ADAM_EPS = 1e-08
ADAM_WD = 0.01
ADAM_STEP = 10

VMEM_LIMIT = 60 * 1024 * 1024
MESH = pl.DeviceIdType.MESH


def _dot(a, b):
    return jnp.dot(a, b, preferred_element_type=F32)


def _dot_nt(a, b):
    return lax.dot_general(a, b, (((1,), (1,)), ((), ())), preferred_element_type=F32)


def _dot_tn(a, b):
    return lax.dot_general(a, b, (((0,), (0,)), ((), ())), preferred_element_type=F32)


def _sigmoid(z):
    return 1.0 / (1.0 + jnp.exp(-z))


def _params(n_grid, vmem=VMEM_LIMIT):
    return pltpu.CompilerParams(dimension_semantics=("arbitrary",) * n_grid, vmem_limit_bytes=vmem)


def _position():
    return lax.axis_index("x"), lax.axis_index("y"), lax.axis_index("c")


def _flip(pos, k):
    x, y, c = pos
    return (1 - x if k & 4 else x, 1 - y if k & 2 else y, 1 - c if k & 1 else c)


def _slot(pos):
    return 4 * pos[0] + 2 * pos[1] + pos[2]


def _all_gather_direct(name, arrs):
    n = len(arrs)

    def body(*refs):
        ins, outs = refs[:n], refs[n : 2 * n]
        send_sems, recv_sems, local_sems = refs[2 * n :]
        me = _position()
        mine = [pltpu.make_async_copy(ins[a], outs[a].at[_slot(me)], local_sems.at[a]) for a in range(n)]
        for cp in mine:
            cp.start()

        def copy(a, k, block):
            return pltpu.make_async_remote_copy(
                src_ref=ins[a],
                dst_ref=outs[a].at[_slot(block)],
                send_sem=send_sems.at[a * 7 + k - 1],
                recv_sem=recv_sems.at[a * 7 + k - 1],
                device_id=_flip(me, k),
                device_id_type=MESH,
            )

        sent = [copy(a, k, me) for a in range(n) for k in range(1, N_DEV)]
        for cp in sent:
            cp.start()
        for a in range(n):
            for k in range(1, N_DEV):
                copy(a, k, _flip(me, k)).wait_recv()
        for cp in sent:
            cp.wait_send()
        for cp in mine:
            cp.wait()

    any_spec = pl.BlockSpec(memory_space=pl.ANY)
    return pl.pallas_call(
        body,
        name=name,
        out_shape=[jax.ShapeDtypeStruct((N_DEV,) + a.shape, a.dtype) for a in arrs],
        in_specs=[any_spec] * n,
        out_specs=[any_spec] * n,
        scratch_shapes=[
            pltpu.SemaphoreType.DMA((7 * n,)),
            pltpu.SemaphoreType.DMA((7 * n,)),
            pltpu.SemaphoreType.DMA((n,)),
        ],
    )(*arrs)


def _direct_gather_copies(src_hbm, all_hbm, send_sems, recv_sems, arriving):
    me = _position()
    return [
        pltpu.make_async_remote_copy(
            src_ref=src_hbm,
            dst_ref=all_hbm.at[_slot(_flip(me, k) if arriving else me)],
            send_sem=send_sems.at[k - 1],
            recv_sem=recv_sems.at[k - 1],
            device_id=_flip(me, k),
            device_id_type=MESH,
        )
        for k in range(1, N_DEV)
    ]


CHIP_FLIPS = (2, 4, 6)


def _chip_of(pos):
    return 2 * pos[0] + pos[1]


def _chip_copies(srcs, lands, views, send_sems, recv_sems):
    me = _position()
    copies = []
    for a, (src, land) in enumerate(zip(srcs, lands)):
        for j, k in enumerate(CHIP_FLIPS):
            peer = _flip(me, k)
            copies.append(
                pltpu.make_async_remote_copy(
                    src_ref=views[a](src, peer),
                    dst_ref=land.at[j],
                    send_sem=send_sems.at[len(CHIP_FLIPS) * a + j],
                    recv_sem=recv_sems.at[len(CHIP_FLIPS) * a + j],
                    device_id=peer,
                    device_id_type=MESH,
                )
            )
    return copies


def _exchange(name, arrs, views, out_shapes, flips):
    n = len(arrs)
    nf = len(flips)

    def body(*refs):
        ins, outs = refs[:n], refs[n : 2 * n]
        send_sems, recv_sems = refs[2 * n :]
        me = _position()

        def copy(a, j):
            peer = _flip(me, flips[j])
            return pltpu.make_async_remote_copy(
                src_ref=views[a](ins[a], peer),
                dst_ref=outs[a].at[j],
                send_sem=send_sems.at[a * nf + j],
                recv_sem=recv_sems.at[a * nf + j],
                device_id=peer,
                device_id_type=MESH,
            )

        sent = [copy(a, j) for a in range(n) for j in range(nf)]
        for cp in sent:
            cp.start()
        for cp in sent:
            cp.wait_recv()
        for cp in sent:
            cp.wait_send()

    any_spec = pl.BlockSpec(memory_space=pl.ANY)
    return pl.pallas_call(
        body,
        name=name,
        out_shape=[jax.ShapeDtypeStruct((nf,) + tuple(sh), a.dtype) for sh, a in zip(out_shapes, arrs)],
        in_specs=[any_spec] * n,
        out_specs=[any_spec] * n,
        scratch_shapes=[pltpu.SemaphoreType.DMA((nf * n,)), pltpu.SemaphoreType.DMA((nf * n,))],
    )(*arrs)


def _pair_add(name, own, recv, index, grid, chip_axis, own_spec, recv_spec, mine_spec, mine_shape):
    def body(idx_ref, own_ref, recv_ref, sb_ref, mine_ref):
        total = own_ref[...] + recv_ref[...].astype(F32)
        sb_ref[...] = total.astype(BF16)

        @pl.when(pl.program_id(chip_axis) == idx_ref[1])
        def _():
            mine_ref[...] = total

    return pl.pallas_call(
        body,
        name=name,
        grid_spec=pltpu.PrefetchScalarGridSpec(
            num_scalar_prefetch=1, grid=grid, in_specs=[own_spec, recv_spec], out_specs=[recv_spec, mine_spec]
        ),
        out_shape=[jax.ShapeDtypeStruct(recv.shape, BF16), jax.ShapeDtypeStruct(mine_shape, F32)],
        compiler_params=_params(len(grid)),
    )(index, own, recv)


PACK_GATE = 3 * ROW_SHARD
PACK_ROWS = PACK_GATE + 2 * GATE_SHARD


def _fwd_in(x, c_rows, conv_rows, w_ada, b_cols, norm_g, w_shard, tables, pack, where):
    t = x.shape[0]
    tm = min(1024, t)
    tn = IN_W // 4
    n_i, n_j = t // tm, IN_W // tn
    n_rope = (SEG_V - SEG_Q) // 128

    n_ada = w_ada.shape[1]

    def body(where_ref, x_ref, wada_ref, bcols_ref, ng_ref, cos_ref, s1_ref, s2_ref, shard_hbm, pack_hbm, c_hbm, conv_hbm,
             proj_ref, h_ref, mod_ref, wt_hbm, all_hbm, call_hbm, convall_hbm,
             wall, hbuf, cbuf, mcols, mod_all, mod_buf,
             w_send, w_recv, load_sems, local_sem, p_send, p_recv, p_local,
             c_send, c_recv, c_local, v_send, v_recv, v_local, m_send, m_recv):
        j = pl.program_id(0)
        i = pl.program_id(1)
        me = _position()
        sibling = _flip(me, 1)
        chip = where_ref[0]
        slot = where_ref[1]

        def modulation():
            own_c = pltpu.make_async_copy(c_hbm, call_hbm.at[_slot(me)], c_local)
            own_v = pltpu.make_async_copy(conv_hbm, convall_hbm.at[_slot(me)], v_local)
            own_c.start()
            own_v.start()
            for cp in _direct_gather_copies(c_hbm, call_hbm, c_send, c_recv, arriving=False):
                cp.start()
            for cp in _direct_gather_copies(conv_hbm, convall_hbm, v_send, v_recv, arriving=False):
                cp.start()
            for cp in _direct_gather_copies(c_hbm, call_hbm, c_send, c_recv, arriving=True):
                cp.wait_recv()
            own_c.wait()
            fetch = pltpu.make_async_copy(call_hbm, cbuf, c_local)
            fetch.start()
            fetch.wait()
            mcols[...] = _dot(cbuf[:, 0, :].astype(BF16), wada_ref[...].astype(BF16)) + bcols_ref[...]
            mod_all[slot] = mcols[...]
            swaps = lambda arriving: [
                pltpu.make_async_remote_copy(
                    src_ref=mcols,
                    dst_ref=mod_all.at[_slot(_flip(me, k) if arriving else me)],
                    send_sem=m_send.at[k - 1],
                    recv_sem=m_recv.at[k - 1],
                    device_id=_flip(me, k),
                    device_id_type=MESH,
                )
                for k in range(1, N_DEV)
            ]
            for cp in swaps(False):
                cp.start()
            for cp in swaps(True):
                cp.wait_recv()
            mine = lax.broadcasted_iota(jnp.int32, (N_DEV, n_ada), 0) == slot
            for s in range(N_DEV):
                row = jnp.sum(jnp.where(mine, mod_all[s], 0.0), axis=0, keepdims=True)
                mod_buf[:, n_ada * s : n_ada * (s + 1)] = row
            for cp in swaps(False):
                cp.wait_send()
            for cp in _direct_gather_copies(c_hbm, call_hbm, c_send, c_recv, arriving=False):
                cp.wait_send()
            for cp in _direct_gather_copies(conv_hbm, convall_hbm, v_send, v_recv, arriving=True):
                cp.wait_recv()
            for cp in _direct_gather_copies(conv_hbm, convall_hbm, v_send, v_recv, arriving=False):
                cp.wait_send()
            own_v.wait()

        def shard_of(block):
            return wall.at[_chip_of(block), pl.ds(W_IN_SHARD * block[2], W_IN_SHARD), :]

        def shard_copy(k, block, to, src=None):
            return pltpu.make_async_remote_copy(
                src_ref=shard_of(block),
                dst_ref=shard_of(block),
                send_sem=w_send.at[k],
                recv_sem=w_recv.at[k],
                device_id=to,
                device_id_type=MESH,
            )

        own_shard = pltpu.make_async_copy(shard_hbm, shard_of(me), local_sem)
        own_pack = pltpu.make_async_copy(pack_hbm, all_hbm.at[_slot(me)], p_local)

        def keep_tile(q):
            return [
                pltpu.make_async_copy(wall.at[q, pl.ds(W_IN_SHARD * c, W_IN_SHARD), :], wt_hbm.at[2 * q + c], load_sems.at[2 * (q ^ chip) + c])
                for c in range(2)
            ]

        @pl.when((j == 0) & (i == 0))
        def _():
            modulation()
            mod_ref[...] = mod_buf[...]
            own_shard.start()
            own_shard.wait()
            shard_copy(0, me, sibling).start()
            for rel in range(1, 4):
                shard_copy(rel, me, _flip(me, 2 * rel)).start()
            own_pack.start()
            for cp in _direct_gather_copies(pack_hbm, all_hbm, p_send, p_recv, arriving=False):
                cp.start()
            shard_copy(0, sibling, me).wait_recv()
            for cp in keep_tile(chip):
                cp.start()

        for rel in range(1, 4):

            @pl.when((j == rel) & (i == 0))
            def _(rel=rel):
                there = _flip(me, 2 * rel)
                shard_copy(rel, there, me).wait_recv()
                shard_copy(3 + rel, there, sibling).start()
                shard_copy(3 + rel, _flip(sibling, 2 * rel), me).wait_recv()
                for cp in keep_tile(chip ^ rel):
                    cp.start()

        rows = pl.ds(pl.multiple_of(i * tm, tm), tm)

        @pl.when(j == 0)
        def _():
            xv = x_ref[...]
            r = lax.rsqrt(jnp.mean(xv * xv, axis=-1, keepdims=True) + NORM_EPS)
            shift = mod_buf[:, 0:D]
            scale = mod_buf[:, D : 2 * D]
            hv = (((xv * r) * ng_ref[...]) * (1.0 + scale) + shift).astype(BF16)
            hbuf[rows, :] = hv
            h_ref[...] = hv

        proj_ref[...] = _dot_nt(hbuf[rows, :], wall[chip ^ j])

        @pl.when((chip ^ j) == 0)
        def _():
            cos, s1, s2 = cos_ref[...], s1_ref[...], s2_ref[...]
            for g in range(n_rope):
                cols = slice(128 * g, 128 * (g + 1))
                proj_ref[:, cols] = _rope(proj_ref[:, cols], cos, s1, s2)

        @pl.when((j == n_j - 1) & (i == n_i - 1))
        def _():
            shard_copy(0, me, sibling).wait_send()
            for rel in range(1, 4):
                shard_copy(rel, me, _flip(me, 2 * rel)).wait_send()
                shard_copy(3 + rel, _flip(me, 2 * rel), sibling).wait_send()
            for q in range(N_DEV // 2):
                for cp in keep_tile(q):
                    cp.wait()
            for cp in _direct_gather_copies(pack_hbm, all_hbm, p_send, p_recv, arriving=True):
                cp.wait_recv()
            for cp in _direct_gather_copies(pack_hbm, all_hbm, p_send, p_recv, arriving=False):
                cp.wait_send()
            own_pack.wait()

    tab = pl.BlockSpec((tm, 128), lambda j, i, c: (i, 0))
    any_spec = pl.BlockSpec(memory_space=pl.ANY)
    return pl.pallas_call(
        body,
        name="fwd_in",
        grid_spec=pltpu.PrefetchScalarGridSpec(
            num_scalar_prefetch=1,
            grid=(n_j, n_i),
            in_specs=[
                pl.BlockSpec((tm, D), lambda j, i, c: (jnp.where(j == 0, i, 0), 0)),
                pl.BlockSpec(w_ada.shape, lambda j, i, c: (0, 0)),
                pl.BlockSpec((1, n_ada), lambda j, i, c: (0, 0)),
                pl.BlockSpec((1, D), lambda j, i, c: (0, 0)),
                tab, tab, tab, any_spec, any_spec, any_spec, any_spec,
            ],
            out_specs=[pl.BlockSpec((tm, tn), lambda j, i, c: (i, c[0] ^ j)),
                       pl.BlockSpec((tm, D), lambda j, i, c: (jnp.where(j == 0, i, n_i - 1), 0)),
                       pl.BlockSpec((1, 3 * D), lambda j, i, c: (0, 0)),
                       any_spec, any_spec, any_spec, any_spec],
            scratch_shapes=[
                pltpu.VMEM((N_DEV // 2, tn, D), BF16),
                pltpu.VMEM((t, D), BF16),
                pltpu.VMEM((N_DEV,) + c_rows.shape, F32),
                pltpu.VMEM((N_DEV, n_ada), F32),
                pltpu.VMEM((N_DEV, N_DEV, n_ada), F32),
                pltpu.VMEM((1, 3 * D), F32),
                pltpu.SemaphoreType.DMA((N_DEV - 1,)),
                pltpu.SemaphoreType.DMA((N_DEV - 1,)),
                pltpu.SemaphoreType.DMA((N_DEV,)),
                pltpu.SemaphoreType.DMA,
                pltpu.SemaphoreType.DMA((N_DEV - 1,)),
                pltpu.SemaphoreType.DMA((N_DEV - 1,)),
                pltpu.SemaphoreType.DMA,
                pltpu.SemaphoreType.DMA((N_DEV - 1,)),
                pltpu.SemaphoreType.DMA((N_DEV - 1,)),
                pltpu.SemaphoreType.DMA,
                pltpu.SemaphoreType.DMA((N_DEV - 1,)),
                pltpu.SemaphoreType.DMA((N_DEV - 1,)),
                pltpu.SemaphoreType.DMA,
                pltpu.SemaphoreType.DMA((N_DEV - 1,)),
                pltpu.SemaphoreType.DMA((N_DEV - 1,)),
            ],
        ),
        out_shape=[jax.ShapeDtypeStruct((t, IN_W), F32), jax.ShapeDtypeStruct((t, D), BF16),
                   jax.ShapeDtypeStruct((1, 3 * D), F32),
                   jax.ShapeDtypeStruct((N_DEV,) + w_shard.shape, w_shard.dtype),
                   jax.ShapeDtypeStruct((N_DEV,) + pack.shape, pack.dtype),
                   jax.ShapeDtypeStruct((N_DEV,) + c_rows.shape, F32),
                   jax.ShapeDtypeStruct((N_DEV,) + conv_rows.shape, F32)],
        compiler_params=_params(2),
    )(where, x, w_ada, b_cols, norm_g, *tables, w_shard, pack, c_rows, conv_rows)


def _rope_tables(positions):
    half = ROT_DIM // 2
    inv_freq = ROPE_THETA ** (-jnp.arange(0, ROT_DIM, 2, dtype=F32) / ROT_DIM)
    pos = positions.astype(F32)
    ang_t = inv_freq[:, None] * pos[None, :]
    cos_t, sin_t = jnp.cos(ang_t), jnp.sin(ang_t)
    cos, sin = cos_t.T, sin_t.T
    t = positions.shape[0]
    zeros = lambda n: jnp.zeros((t, n), F32)
    heads = (
        [cos, cos, jnp.ones((t, HEAD_DIM - ROT_DIM), F32)],
        [-sin, zeros(HEAD_DIM - half)],
        [zeros(half), sin, zeros(HEAD_DIM - ROT_DIM)],
    )
    lane_tables = tuple(jnp.concatenate(parts + parts, axis=1) for parts in heads)
    return lane_tables, (cos_t, sin_t)


def _rope(tv, cos, s1, s2):
    cols = []
    for g in range(tv.shape[1] // 128):
        blk = tv[:, 128 * g : 128 * (g + 1)]
        cols.append(blk * cos + pltpu.roll(blk, 128 - 8, 1) * s1 + pltpu.roll(blk, 8, 1) * s2)
    return cols[0] if len(cols) == 1 else jnp.concatenate(cols, axis=1)


def _rope_transposed(gv, cos, s1, s2):
    cols = []
    for g in range(gv.shape[1] // 128):
        blk = gv[:, 128 * g : 128 * (g + 1)]
        cols.append(blk * cos + pltpu.roll(blk * s1, 8, 1) + pltpu.roll(blk * s2, 128 - 8, 1))
    return cols[0] if len(cols) == 1 else jnp.concatenate(cols, axis=1)


def _lane_lo():
    return lax.broadcasted_iota(jnp.int32, (1, 128), 1) < HEAD_DIM


def _row_lo():
    return lax.broadcasted_iota(jnp.int32, (128, 1), 0) < HEAD_DIM


def _split_heads(tile, j):
    lo = _lane_lo()
    parts = []
    for g in (2 * j, 2 * j + 1):
        blk = tile[:, 128 * g : 128 * (g + 1)]
        parts += [jnp.where(lo, blk, 0.0), jnp.where(lo, 0.0, blk)]
    return jnp.concatenate(parts, axis=0)


def _dup_lanes(kv, j):
    rolled = pltpu.roll(kv, HEAD_DIM, 1)
    lo = _lane_lo()
    return jnp.where(lo, kv, rolled) if j == 0 else jnp.where(lo, rolled, kv)


def _fold_lanes(acc):
    return acc + pltpu.roll(acc, HEAD_DIM, 1)


def _fill_bias_t(bias_ref):
    kj = lax.broadcasted_iota(jnp.int32, (256, 128), 0)
    qi = lax.broadcasted_iota(jnp.int32, (256, 128), 1)
    diff = qi + WINDOW - kj
    band = (diff >= 0) & (diff < WINDOW)
    bias_ref[0] = jnp.where(band, 0.0, -1e30)
    bias_ref[1] = jnp.where(band & (kj >= 128), 0.0, -1e30)


def _sink_row(sink_ref, pair, j):
    return jnp.concatenate([jnp.full((1, 128), sink_ref[0, 8 * pair + 4 * j + h], F32) for h in range(4)], axis=1)


def _softmax_t(q4, kw, bias4, sink):
    s = _dot_nt(kw, q4) + bias4
    m = jnp.maximum(jnp.max(s, axis=0, keepdims=True), sink)
    p = jnp.exp(s - m)
    e_sink = jnp.exp(sink - m)
    denom = jnp.sum(p, axis=0, keepdims=True) + e_sink
    return p, e_sink, denom


def _attn_specs(t, tq):
    return dict(
        q=pl.BlockSpec((tq, 512), lambda p, i: (i, p)),
        k=pl.BlockSpec((t, 128), lambda p, i: (0, SEG_K // 128 + p)),
        v=pl.BlockSpec((t, 128), lambda p, i: (0, SEG_V // 128 + p)),
        sink=pl.BlockSpec(memory_space=pltpu.SMEM),
        act=pl.BlockSpec((tq, 512), lambda p, i: (i, p)),
    )


def _attn_fwd(proj, sinks):
    t = proj.shape[0]
    tq = min(512, t)
    nb = tq // 128
    n_tiles = t // tq
    sp = _attn_specs(t, tq)

    def body(q_ref, k_ref, v_ref, sink_ref, o_ref, kd, vdt, bias_ref):
        pair = pl.program_id(0)
        i = pl.program_id(1)

        @pl.when(i == 0)
        def _():
            kv, vv = k_ref[...], v_ref[...]
            for j in range(2):
                kd[j, 0:128, :] = jnp.zeros((128, 128), BF16)
                kd[j, 128:, :] = _dup_lanes(kv, j).astype(BF16)
                vdt[j, :, 0:128] = jnp.zeros((128, 128), BF16)
                vdt[j, :, 128:] = _dup_lanes(vv, j).T.astype(BF16)
            _fill_bias_t(bias_ref)

        row_lo = _row_lo()

        def block(b, carry):
            r0 = pl.multiple_of(b * 128, 128)
            gb = i * nb + b
            keys = pl.ds(pl.multiple_of(gb * 128, 128), 256)
            rows = pl.ds(r0, 128)
            qt = q_ref[rows, :] * (1.0 / math.sqrt(HEAD_DIM))
            bias = bias_ref[jnp.where(gb == 0, 1, 0)]
            bias4 = jnp.concatenate([bias] * 4, axis=1)
            cols = []
            for j in range(2):
                q4 = _split_heads(qt, j).astype(BF16)
                p, _, denom = _softmax_t(q4, kd[j, keys, :], bias4, _sink_row(sink_ref, pair, j))
                ot = _dot(vdt[j, :, keys], p.astype(BF16)) * (1.0 / denom)
                for gi in range(2):
                    z = jnp.where(row_lo, ot[:, 256 * gi : 256 * gi + 128], ot[:, 256 * gi + 128 : 256 * gi + 256])
                    cols.append(z.T)
            o_ref[rows, :] = jnp.concatenate(cols, axis=1).astype(BF16)
            return carry

        lax.fori_loop(0, nb, block, 0, unroll=True)

    return pl.pallas_call(
        body,
        name="attn_fwd",
        grid=(2, n_tiles),
        in_specs=[sp["q"], sp["k"], sp["v"], sp["sink"]],
        out_specs=sp["act"],
        out_shape=jax.ShapeDtypeStruct((t, D), BF16),
        scratch_shapes=[pltpu.VMEM((2, t + 128, 128), BF16), pltpu.VMEM((2, 128, t + 128), BF16), pltpu.VMEM((2, 256, 128), F32)],
        compiler_params=_params(2),
    )(proj, proj, proj, sinks)


def _attn_bwd(proj, o, d_o, tables, sinks, part_3):
    t = proj.shape[0]
    tq = min(512, t)
    nb = tq // 128
    n_tiles = t // tq
    (cos, s1, s2), (cos_t, sin_t) = tables
    sp = _attn_specs(t, tq)
    scale = 1.0 / math.sqrt(HEAD_DIM)
    half = ROT_DIM // 2

    def unrope_rows(z, c, s):
        out = []
        for base in (0, HEAD_DIM):
            g0, g1 = z[base : base + half], z[base + half : base + ROT_DIM]
            out += [g0 * c + g1 * s, g1 * c - g0 * s, z[base + ROT_DIM : base + HEAD_DIM]]
        return jnp.concatenate(out, axis=0)

    def swap(part_hbm, recv_hbm, send_sem, recv_sem):
        sibling = _flip(_position(), 1)
        return pltpu.make_async_remote_copy(
            src_ref=part_hbm.at[:, :, sibling[2]], dst_ref=recv_hbm, send_sem=send_sem, recv_sem=recv_sem,
            device_id=sibling, device_id_type=MESH,
        )

    def body(q_ref, k_ref, v_ref, o_ref, do_ref, ct_ref, st_ref, ck, s1k, s2k, sink_ref, part_hbm,
             dq_ref, dk_ref, dv_ref, dsink_ref, recv_hbm, kd, kdt, vd, bias_ref, dkacc, dvacc, send_sem, recv_sem):
        pair = pl.program_id(0)
        i = pl.program_id(1)

        @pl.when((pair == 0) & (i == 0))
        def _():
            swap(part_hbm, recv_hbm, send_sem, recv_sem).start()

        @pl.when((pair == 1) & (i == n_tiles - 1))
        def _():
            swap(part_hbm, recv_hbm, send_sem, recv_sem).wait()

        @pl.when(i == 0)
        def _():
            kv, vv = k_ref[...], v_ref[...]
            for j in range(2):
                kdup = _dup_lanes(kv, j)
                kd[j, 0:128, :] = jnp.zeros((128, 128), BF16)
                kd[j, 128:, :] = kdup.astype(BF16)
                kdt[j, :, 0:128] = jnp.zeros((128, 128), BF16)
                kdt[j, :, 128:] = kdup.T.astype(BF16)
                vd[j, 0:128, :] = jnp.zeros((128, 128), BF16)
                vd[j, 128:, :] = _dup_lanes(vv, j).astype(BF16)
            _fill_bias_t(bias_ref)
            dkacc[...] = jnp.zeros_like(dkacc)
            dvacc[...] = jnp.zeros_like(dvacc)
            dsink_ref[...] = jnp.zeros_like(dsink_ref)

        row_lo = _row_lo()
        ones = jnp.ones((8, 128), BF16)

        def block(b, carry):
            r0 = pl.multiple_of(b * 128, 128)
            gb = i * nb + b
            keys = pl.ds(pl.multiple_of(gb * 128, 128), 256)
            rows = pl.ds(r0, 128)
            qt = q_ref[rows, :] * scale
            dob = do_ref[rows, :].astype(F32)
            prod = dob * o_ref[rows, :].astype(F32)
            ct, st = ct_ref[:, rows], st_ref[:, rows]
            bias = bias_ref[jnp.where(gb == 0, 1, 0)]
            bias4 = jnp.concatenate([bias] * 4, axis=1)
            cols = []
            for j in range(2):
                q4 = _split_heads(qt, j).astype(BF16)
                kw = kd[j, keys, :]
                p, e_sink, denom = _softmax_t(q4, kw, bias4, _sink_row(sink_ref, pair, j))
                inv = 1.0 / denom
                prob = p * inv
                do4 = _split_heads(dob, j).astype(BF16)
                delta = _dot_nt(ones, _split_heads(prod, j).astype(BF16))[0:1, :]
                dp = _dot_nt(vd[j, keys, :], do4)
                ds = (prob * (dp - delta)).astype(BF16)
                dkacc[j, keys, :] = dkacc[j, keys, :] + _dot(ds, q4)
                dvacc[j, keys, :] = dvacc[j, keys, :] + _dot(prob.astype(BF16), do4)
                dqt = _dot(kdt[j, :, keys], ds) * scale
                dsink = -(e_sink * inv) * delta
                for h in range(4):
                    row = 4 * j + h
                    total = jnp.sum(dsink[:, 128 * h : 128 * (h + 1)], axis=1, keepdims=True)
                    dsink_ref[0, row : row + 1, :] = dsink_ref[0, row : row + 1, :] + jnp.broadcast_to(total, (1, 128))
                for gi in range(2):
                    z = jnp.where(row_lo, dqt[:, 256 * gi : 256 * gi + 128], dqt[:, 256 * gi + 128 : 256 * gi + 256])
                    cols.append(unrope_rows(z, ct, st).T)
            dq_ref[rows, :] = jnp.concatenate(cols, axis=1).astype(BF16)
            return carry

        lax.fori_loop(0, nb, block, 0, unroll=True)

        @pl.when(i == n_tiles - 1)
        def _():
            lo = _lane_lo()
            dk = jnp.where(lo, _fold_lanes(dkacc[0, 128:, :]), _fold_lanes(dkacc[1, 128:, :]))
            dk_ref[...] = _rope_transposed(dk, ck[...], s1k[...], s2k[...]).astype(BF16)
            dv_ref[...] = jnp.where(lo, _fold_lanes(dvacc[0, 128:, :]), _fold_lanes(dvacc[1, 128:, :])).astype(BF16)

    kv_out = pl.BlockSpec((t, 128), lambda p, i: (0, p))
    tab_t = pl.BlockSpec((half, tq), lambda p, i: (0, i))
    tab_k = pl.BlockSpec((t, 128), lambda p, i: (0, 0))
    return pl.pallas_call(
        body,
        name="attn_bwd",
        grid=(2, n_tiles),
        in_specs=[sp["q"], sp["k"], sp["v"], sp["act"], sp["act"], tab_t, tab_t, tab_k, tab_k, tab_k, sp["sink"],
                  pl.BlockSpec(memory_space=pl.ANY)],
        out_specs=[sp["act"], kv_out, kv_out, pl.BlockSpec((1, 8, 128), lambda p, i: (p, 0, 0)), pl.BlockSpec(memory_space=pl.ANY)],
        out_shape=[
            jax.ShapeDtypeStruct((t, D), BF16),
            jax.ShapeDtypeStruct((t, 256), BF16),
            jax.ShapeDtypeStruct((t, 256), BF16),
            jax.ShapeDtypeStruct((2, 8, 128), F32),
            jax.ShapeDtypeStruct((3, 4, ROW_SHARD, D), F32),
        ],
        scratch_shapes=[pltpu.VMEM((2, t + 128, 128), BF16), pltpu.VMEM((2, 128, t + 128), BF16), pltpu.VMEM((2, t + 128, 128), BF16),
                        pltpu.VMEM((2, 256, 128), F32), pltpu.VMEM((2, t + 128, 128), F32), pltpu.VMEM((2, t + 128, 128), F32),
                        pltpu.SemaphoreType.DMA, pltpu.SemaphoreType.DMA],
        compiler_params=_params(2),
    )(proj, proj, proj, o, d_o, cos_t, sin_t, cos, s1, s2, sinks, part_3)


HALF = 512


def _softplus(y):
    u = jnp.exp(-jnp.abs(y))
    w = 1.0 + u
    log1p = jnp.where(w == 1.0, u, jnp.log(w) * (u / jnp.where(w == 1.0, 1.0, w - 1.0)))
    return jnp.maximum(y, 0.0) + log1p


def _gate_block(w_ref, blk):
    return w_ref[:, :, 256 * blk : 256 * (blk + 1)].reshape(256, 256)


def _gate_matmul(act_bf, w_ref):
    return jnp.concatenate([_dot(act_bf[:, 256 * blk : 256 * (blk + 1)], _gate_block(w_ref, blk)) for blk in range(2)], axis=1)


def _gate_matmul_t(grad_bf, w_ref):
    return jnp.concatenate([_dot_nt(grad_bf[:, 256 * blk : 256 * (blk + 1)], _gate_block(w_ref, blk)) for blk in range(2)], axis=1)


def _conv_and_gates(xe_ref, tm, cw_ref, cb_ref, wa_ref, wx_ref, ba_ref, bx_ref, lam_ref, keep):
    xc = cb_ref[...] + sum(cw_ref[k : k + 1, :] * xe_ref[pl.ds(5 + k, tm), :] for k in range(CONV_WIDTH))
    xc_bf = xc.astype(BF16)
    r = _sigmoid(_gate_matmul(xc_bf, wa_ref) + ba_ref[...])
    ig = _sigmoid(_gate_matmul(xc_bf, wx_ref) + bx_ref[...])
    sp = _softplus(-lam_ref[...])
    log_a = -LRU_C * r * sp
    tanh = jnp.tanh(-log_a)
    mult = jnp.sqrt(2.0 * tanh / (1.0 + tanh))
    a = jnp.where(keep, jnp.exp(log_a), 0.0)
    mult = jnp.where(keep, mult, 1.0)
    return xc, xc_bf, r, ig, sp, a, mult


def _scan_rows(a_ref, b_ref, h_ref, carry_ref, tm, reverse):
    width = a_ref.shape[1]
    row = lax.broadcasted_iota(jnp.int32, (8, width), 0)

    def group(g, carry):
        gg = tm // 8 - 1 - g if reverse else g
        r0 = pl.multiple_of(gg * 8, 8)
        a = a_ref[pl.ds(r0, 8), :]
        b = b_ref[pl.ds(r0, 8), :]
        for s in (1, 2, 4):
            shift = 8 - s if reverse else s
            a_sh = pltpu.roll(a, shift, 0)
            b_sh = pltpu.roll(b, shift, 0)
            ok = (row < 8 - s) if reverse else (row >= s)
            b = jnp.where(ok, a * b_sh + b, b)
            a = jnp.where(ok, a * a_sh, a)
        h = a * carry + b
        h_ref[pl.ds(r0, 8), :] = h
        last = h[0:1, :] if reverse else h[7:8, :]
        return jnp.broadcast_to(last, (8, width))

    carry_ref[...] = lax.fori_loop(0, tm // 8, group, carry_ref[...])


def _rnn_specs(t, tm, order):
    n = t // tm
    return dict(
        tile=pl.BlockSpec((tm, HALF), lambda ch, i: (order(i, n), ch)),
        before=lambda col0: pl.BlockSpec((8, HALF), lambda ch, i: (jnp.maximum(order(i, n) * (tm // 8) - 1, 0), col0 + ch)),
        proj_tile=lambda col0: pl.BlockSpec((tm, HALF), lambda ch, i: (order(i, n), col0 + ch)),
        keep=pl.BlockSpec((tm, 1), lambda ch, i: (order(i, n), 0)),
        gates=lambda which: pl.BlockSpec((N_DEV, GATE_SHARD, HALF), lambda ch, i: (0, PACK_GATE // GATE_SHARD + which, ch)),
        gate_grads=pl.BlockSpec((2, 2, 256, 256), lambda ch, i: (0, ch, 0, 0)),
        vec=pl.BlockSpec((1, HALF), lambda ch, i: (0, ch)),
        conv=pl.BlockSpec((CONV_WIDTH, HALF), lambda ch, i: (0, ch)),
    )


def _rnn_fwd(proj, keep, w_all, rg_ba, rg_bx, rg_lambda, conv_w, conv_b):
    t = proj.shape[0]
    tm = min(512, t)
    sp = _rnn_specs(t, tm, lambda i, n: i)

    def body(xr_ref, xb_ref, keep_ref, wa_ref, wx_ref, ba_ref, bx_ref, lam_ref, cw_ref, cb_ref, hh_ref, xe, a_buf, b_buf, carry):
        i = pl.program_id(1)

        @pl.when(i == 0)
        def _():
            carry[...] = jnp.zeros_like(carry)

        xe[0:8, :] = jnp.where(i > 0, xb_ref[...], 0.0)
        xe[8:, :] = xr_ref[...]
        keep_rows = keep_ref[...] != 0.0
        xc, _, _, ig, _, a, mult = _conv_and_gates(xe, tm, cw_ref, cb_ref, wa_ref, wx_ref, ba_ref, bx_ref, lam_ref, keep_rows)
        a_buf[...] = a
        b_buf[...] = mult * (ig * xc)
        _scan_rows(a_buf, b_buf, hh_ref, carry, tm, reverse=False)

    return pl.pallas_call(
        body,
        name="rnn_fwd",
        grid=(2, t // tm),
        in_specs=[sp["proj_tile"](SEG_XR // HALF), sp["before"](SEG_XR // HALF), sp["keep"], sp["gates"](0), sp["gates"](1),
                  sp["vec"], sp["vec"], sp["vec"], sp["conv"], sp["vec"]],
        out_specs=sp["tile"],
        out_shape=jax.ShapeDtypeStruct((t, D), F32),
        scratch_shapes=[pltpu.VMEM((tm + 8, HALF), F32), pltpu.VMEM((tm, HALF), F32), pltpu.VMEM((tm, HALF), F32), pltpu.VMEM((8, HALF), F32)],
        compiler_params=_params(2),
    )(proj, proj, keep, w_all, w_all, rg_ba, rg_bx, rg_lambda, conv_w, conv_b)


def _rnn_bwd(proj, hh, d_hh, keep, w_all, rg_ba, rg_bx, rg_lambda, conv_w, conv_b, sum_3_bf):
    t = proj.shape[0]
    tm = min(512, t)
    n_tiles = t // tm
    sp = _rnn_specs(t, tm, lambda i, n: n - 1 - i)

    def body(xr_ref, xb_ref, hh_ref, hb_ref, dhh_ref, keep_ref, wa_ref, wx_ref, ba_ref, bx_ref, lam_ref, cw_ref, cb_ref,
             s3_hbm, dxr_ref, gw_ref, gvec_ref, l3_hbm, xe, he, ae, c_buf, g_buf, dxe, carry, a_next, send_sems, recv_sems):
        ch = pl.program_id(0)
        i = pl.program_id(1)
        first_tile = i == n_tiles - 1
        views = [lambda ref, peer: ref.at[:, _chip_of(peer)]]

        @pl.when((ch == 0) & (i == 0))
        def _():
            for cp in _chip_copies([s3_hbm], [l3_hbm], views, send_sems, recv_sems):
                cp.start()

        @pl.when((ch == 1) & (i == n_tiles - 1))
        def _():
            for cp in _chip_copies([s3_hbm], [l3_hbm], views, send_sems, recv_sems):
                cp.wait_recv()
                cp.wait_send()

        @pl.when(i == 0)
        def _():
            carry[...] = jnp.zeros_like(carry)
            a_next[...] = jnp.zeros_like(a_next)
            dxe[tm:, :] = jnp.zeros((8, HALF), F32)
            gw_ref[...] = jnp.zeros_like(gw_ref)
            gvec_ref[...] = jnp.zeros_like(gvec_ref)

        xe[0:8, :] = jnp.where(first_tile, 0.0, xb_ref[...])
        xe[8:, :] = xr_ref[...]
        he[0:8, :] = jnp.where(first_tile, 0.0, hb_ref[...])
        he[8:, :] = hh_ref[...]
        keep_rows = keep_ref[...] != 0.0
        xc, xc_bf, r, ig, sp_, a, mult = _conv_and_gates(xe, tm, cw_ref, cb_ref, wa_ref, wx_ref, ba_ref, bx_ref, lam_ref, keep_rows)

        ae[0:tm, :] = a
        ae[tm:, :] = a_next[...]
        a_next[...] = jnp.broadcast_to(a[0:1, :], (8, HALF))
        c_buf[...] = ae[pl.ds(1, tm), :]
        g_buf[...] = dhh_ref[...]
        _scan_rows(c_buf, g_buf, g_buf, carry, tm, reverse=True)
        g = g_buf[...]

        h_prev = he[pl.ds(7, tm), :]
        bb = ig * xc
        d_ig = g * mult * xc
        d_xc = g * mult * ig
        d_log_a = jnp.where(keep_rows, g * h_prev * a - (g * bb) * (a * a) / mult, 0.0)
        d_r = d_log_a * (-LRU_C * sp_)
        d_sp = jnp.sum(d_log_a * (-LRU_C * r), axis=0, keepdims=True)
        d_zr = d_r * r * (1.0 - r)
        d_zi = d_ig * ig * (1.0 - ig)
        d_zr_bf = d_zr.astype(BF16)
        d_zi_bf = d_zi.astype(BF16)
        d_xc = d_xc + _gate_matmul_t(d_zr_bf, wa_ref) + _gate_matmul_t(d_zi_bf, wx_ref)
        for blk in range(2):
            cols = slice(256 * blk, 256 * (blk + 1))
            gw_ref[0, blk] = gw_ref[0, blk] + _dot_tn(xc_bf[:, cols], d_zr_bf[:, cols])
            gw_ref[1, blk] = gw_ref[1, blk] + _dot_tn(xc_bf[:, cols], d_zi_bf[:, cols])

        dxe[0:tm, :] = d_xc
        shifted = [dxe[pl.ds(3 - k, tm), :] for k in range(CONV_WIDTH)]
        dxr = sum(cw_ref[k : k + 1, :] * shifted[k] for k in range(CONV_WIDTH))
        dxr_ref[...] = dxr.astype(BF16)
        dxe[tm:, :] = d_xc[0:8, :]

        lam = lam_ref[...]
        xr_tile = xr_ref[...]
        sums = [
            jnp.sum(d_zr, axis=0, keepdims=True),
            jnp.sum(d_zi, axis=0, keepdims=True),
            d_sp * (-_sigmoid(-lam)),
            jnp.sum(d_xc, axis=0, keepdims=True),
        ] + [jnp.sum(xr_tile * shifted[k], axis=0, keepdims=True) for k in range(CONV_WIDTH)]
        gvec_ref[...] = gvec_ref[...] + jnp.concatenate(sums, axis=0)

    return pl.pallas_call(
        body,
        name="rnn_bwd",
        grid=(2, n_tiles),
        in_specs=[
            sp["proj_tile"](SEG_XR // HALF), sp["before"](SEG_XR // HALF), sp["tile"], sp["before"](0), sp["tile"], sp["keep"],
            sp["gates"](0), sp["gates"](1), sp["vec"], sp["vec"], sp["vec"], sp["conv"], sp["vec"], pl.BlockSpec(memory_space=pl.ANY),
        ],
        out_specs=[sp["tile"], sp["gate_grads"], pl.BlockSpec((8, HALF), lambda ch, i: (0, ch)), pl.BlockSpec(memory_space=pl.ANY)],
        out_shape=[
            jax.ShapeDtypeStruct((t, D), BF16),
            jax.ShapeDtypeStruct((2, 4, 256, 256), F32),
            jax.ShapeDtypeStruct((8, D), F32),
            jax.ShapeDtypeStruct((len(CHIP_FLIPS), 3, ROW_SHARD, D), BF16),
        ],
        scratch_shapes=[
            pltpu.VMEM((tm + 8, HALF), F32),
            pltpu.VMEM((tm + 8, HALF), F32),
            pltpu.VMEM((tm + 8, HALF), F32),
            pltpu.VMEM((tm, HALF), F32),
            pltpu.VMEM((tm, HALF), F32),
            pltpu.VMEM((tm + 8, HALF), F32),
            pltpu.VMEM((8, HALF), F32),
            pltpu.VMEM((8, HALF), F32),
            pltpu.SemaphoreType.DMA((len(CHIP_FLIPS),)),
            pltpu.SemaphoreType.DMA((len(CHIP_FLIPS),)),
        ],
        compiler_params=_params(2),
    )(proj, proj, hh, hh, d_hh, keep, w_all, w_all, rg_ba, rg_bx, rg_lambda, conv_w, conv_b, sum_3_bf)


def _mid(x, target, o, hh, proj, w_all, gate, final_g):
    t = x.shape[0]
    tm = min(256, t)
    n_tiles = t // tm
    full = pl.BlockSpec((tm, D), lambda i: (i, 0))
    half = lambda col: pl.BlockSpec((tm, 512), lambda i: (i, col))
    vec = pl.BlockSpec((1, D), lambda i: (0, 0))
    any_spec = pl.BlockSpec(memory_space=pl.ANY)

    def body(x_ref, tg_ref, o_ref, hh_ref, ga0, ga1, gr0, gr1, ma0, ma1, mr0, mr1, w_hbm, gate_ref, fg_ref,
             dres_ref, do_ref, dhh_ref, dga_ref, dgr_ref, dma_ref, dmr_ref, gw_hbm, vec_ref, w, gw, sem, w_sems):
        i = pl.program_id(0)

        @pl.when(i == 0)
        def _():
            loads = [
                pltpu.make_async_copy(
                    w_hbm.at[s, pl.ds(ROW_SHARD * k, ROW_SHARD), :], w.at[k, pl.ds(ROW_SHARD * s, ROW_SHARD), :], w_sems.at[k * N_DEV + s]
                )
                for k in range(3)
                for s in range(N_DEV)
            ]
            for cp in loads:
                cp.start()
            for cp in loads:
                cp.wait()
            gw[...] = jnp.zeros_like(gw)
            vec_ref[...] = jnp.zeros_like(vec_ref)

        cat = lambda a, b: jnp.concatenate([a[...], b[...]], axis=1)
        ga, gr, ma, mr = cat(ga0, ga1), cat(gr0, gr1), cat(ma0, ma1), cat(mr0, mr1)
        ov, hv, gate_v, fg = o_ref[...].astype(F32), hh_ref[...], gate_ref[...], fg_ref[...]
        sg_a = _sigmoid(ga)
        sg_r = _sigmoid(gr)
        silu_a = ga * sg_a
        silu_r = gr * sg_r
        ya = (ov * silu_a).astype(BF16)
        yr = (hv * silu_r).astype(BF16)
        pa = _dot(ya, w[0])
        pr = _dot(yr, w[1])
        sa = _sigmoid(ma)
        sr = _sigmoid(mr)
        merged = (sa * pa + sr * pr).astype(BF16)
        mo = _dot(merged, w[2])
        out = x_ref[...] + gate_v * mo
        r2 = lax.rsqrt(jnp.mean(out * out, axis=-1, keepdims=True) + NORM_EPS)
        outn = out * r2
        err = outn * fg - tg_ref[...]
        loss = 0.5 * jnp.sum(jnp.mean(err * err, axis=-1, keepdims=True), axis=0, keepdims=True)
        dy = err * (1.0 / D)
        g_final = jnp.sum(dy * outn, axis=0, keepdims=True)
        doutn = dy * fg
        dout = r2 * (doutn - outn * jnp.mean(doutn * outn, axis=-1, keepdims=True))
        dres_ref[...] = dout
        d_gate = jnp.sum(dout * mo, axis=0, keepdims=True)
        dmo = (dout * gate_v).astype(BF16)
        dmerged = _dot_nt(dmo, w[2])
        gw[2] = gw[2] + _dot_tn(merged, dmo)
        dpa = (dmerged * sa).astype(BF16)
        dpr = (dmerged * sr).astype(BF16)
        dma_ref[...] = (dmerged * pa * (sa * (1.0 - sa))).astype(BF16)
        dmr_ref[...] = (dmerged * pr * (sr * (1.0 - sr))).astype(BF16)
        dya = _dot_nt(dpa, w[0])
        dyr = _dot_nt(dpr, w[1])
        gw[0] = gw[0] + _dot_tn(ya, dpa)
        gw[1] = gw[1] + _dot_tn(yr, dpr)
        do_ref[...] = (dya * silu_a).astype(BF16)
        dhh_ref[...] = dyr * silu_r
        dga_ref[...] = (dya * ov * (sg_a * (1.0 + ga * (1.0 - sg_a)))).astype(BF16)
        dgr_ref[...] = (dyr * hv * (sg_r * (1.0 + gr * (1.0 - sg_r)))).astype(BF16)
        zeros = jnp.zeros((5, D), F32)
        vec_ref[...] = vec_ref[...] + jnp.concatenate([g_final, d_gate, jnp.broadcast_to(loss, (1, D)), zeros], axis=0)

        @pl.when(i == n_tiles - 1)
        def _():
            cp = pltpu.make_async_copy(gw, gw_hbm, sem)
            cp.start()
            cp.wait()

    bf_tile = jax.ShapeDtypeStruct((t, D), BF16)
    f_tile = jax.ShapeDtypeStruct((t, D), F32)
    return pl.pallas_call(
        body,
        name="mid",
        grid=(n_tiles,),
        in_specs=[full, full, full, full,
                  half(SEG_GA // 512), half(SEG_GA // 512 + 1), half(SEG_GR // 512), half(SEG_GR // 512 + 1),
                  half(SEG_MA // 512), half(SEG_MA // 512 + 1), half(SEG_MR // 512), half(SEG_MR // 512 + 1),
                  any_spec, vec, vec],
        out_specs=[full, full, full, full, full, full, full, any_spec, pl.BlockSpec((8, D), lambda i: (0, 0))],
        out_shape=[f_tile, bf_tile, f_tile, bf_tile, bf_tile, bf_tile, bf_tile,
                   jax.ShapeDtypeStruct((3, D, D), F32), jax.ShapeDtypeStruct((8, D), F32)],
        scratch_shapes=[pltpu.VMEM((3, D, D), BF16), pltpu.VMEM((3, D, D), F32), pltpu.SemaphoreType.DMA, pltpu.SemaphoreType.DMA((3 * N_DEV,))],
        compiler_params=_params(1),
    )(x, target, o, hh, proj, proj, proj, proj, proj, proj, proj, proj, w_all, gate, final_g)


_SEGMENTS = ((SEG_Q, 1024), (SEG_K, 256), (SEG_V, 256), (SEG_GA, 1024), (SEG_XR, 1024), (SEG_GR, 1024), (SEG_MA, 1024), (SEG_MR, 1024))


def _seg_specs(tm):
    return [pl.BlockSpec((tm, width), lambda i: (i, 0)) for _, width in _SEGMENTS]


def _bwd_dh(dsegs, w_all, x, dres, mod, norm_g, sum_w_bf):
    t = x.shape[0]
    tm = min(512, t)
    n_tiles = t // tm
    n_seg = len(_SEGMENTS)
    full = pl.BlockSpec((tm, D), lambda i: (i, 0))
    views = [lambda ref, peer: ref.at[_chip_of(peer)]]

    def body(*refs):
        segs = refs[:n_seg]
        w_hbm, x_ref, dres_ref, mod_ref, ng_ref, sum_hbm, gx_ref, vec_ref, land_hbm, w, sems, send_sems, recv_sems = refs[n_seg:]
        i = pl.program_id(0)

        @pl.when(i == 0)
        def _():
            for cp in _chip_copies([sum_hbm], [land_hbm], views, send_sems, recv_sems):
                cp.start()
            loads = [
                pltpu.make_async_copy(w_hbm.at[s], w.at[pl.ds(W_IN_SHARD * s, W_IN_SHARD), :], sems.at[s])
                for s in range(N_DEV)
            ]
            for cp in loads:
                cp.start()
            for cp in loads:
                cp.wait()
            vec_ref[...] = jnp.zeros_like(vec_ref)

        dh = sum(_dot(seg[...], w[start : start + width, :]) for seg, (start, width) in zip(segs, _SEGMENTS))
        xv = x_ref[...]
        r1 = lax.rsqrt(jnp.mean(xv * xv, axis=-1, keepdims=True) + NORM_EPS)
        xn = xv * r1
        ng = ng_ref[...]
        scale1 = 1.0 + mod_ref[:, D : 2 * D]
        d_shift = jnp.sum(dh, axis=0, keepdims=True)
        d_scale = jnp.sum(dh * (xn * ng), axis=0, keepdims=True)
        g_norm = jnp.sum(dh * xn * scale1, axis=0, keepdims=True)
        dxn = dh * (ng * scale1)
        dx = r1 * (dxn - xn * jnp.mean(dxn * xn, axis=-1, keepdims=True))
        gx_ref[...] = dres_ref[...] + dx
        vec_ref[...] = vec_ref[...] + jnp.concatenate([d_shift, d_scale, g_norm, jnp.zeros((5, D), F32)], axis=0)

        @pl.when(i == n_tiles - 1)
        def _():
            for cp in _chip_copies([sum_hbm], [land_hbm], views, send_sems, recv_sems):
                cp.wait_recv()
                cp.wait_send()

    any_spec = pl.BlockSpec(memory_space=pl.ANY)
    n_copies = len(CHIP_FLIPS)
    return pl.pallas_call(
        body,
        name="bwd_dh",
        grid=(n_tiles,),
        in_specs=_seg_specs(tm) + [any_spec, full, full, pl.BlockSpec((1, 3 * D), lambda i: (0, 0)), pl.BlockSpec((1, D), lambda i: (0, 0)),
                                   any_spec],
        out_specs=[full, pl.BlockSpec((8, D), lambda i: (0, 0)), any_spec],
        out_shape=[jax.ShapeDtypeStruct((t, D), F32), jax.ShapeDtypeStruct((8, D), F32),
                   jax.ShapeDtypeStruct((n_copies, W_IN_SHARD, D), BF16)],
        scratch_shapes=[pltpu.VMEM((IN_W, D), BF16), pltpu.SemaphoreType.DMA((N_DEV,)),
                        pltpu.SemaphoreType.DMA((n_copies,)), pltpu.SemaphoreType.DMA((n_copies,))],
        compiler_params=_params(1),
    )(*dsegs, w_all, x, dres, mod, norm_g, sum_w_bf)


def _bwd_gw(dsegs, h, sum_g_bf):
    t = h.shape[0]
    tm = min(512, t)
    n_tiles = t // tm
    n_seg = len(_SEGMENTS)
    n_chips = N_DEV // 2
    views = [lambda ref, peer: ref.at[:, :, _chip_of(peer)]]

    def body(*refs):
        segs = refs[:n_seg]
        (h_ref, sg_hbm, own_hbm, recv_hbm, lg_hbm, acc, stage, own_sems, pair_send, pair_recv, send_sems, recv_sems) = refs[n_seg:]
        i = pl.program_id(0)

        @pl.when(i == 0)
        def _():
            for cp in _chip_copies([sg_hbm], [lg_hbm], views, send_sems, recv_sems):
                cp.start()
            acc[...] = jnp.zeros_like(acc)

        hv = h_ref[...]
        for seg, (start, width) in zip(segs, _SEGMENTS):
            acc[start : start + width, :] = acc[start : start + width, :] + _dot_tn(seg[...], hv)

        @pl.when(i == n_tiles - 1)
        def _():
            me = _position()
            sibling = _flip(me, 1)

            def rows(q, c):
                return pl.ds(pl.multiple_of(W_IN_SHARD * (2 * q + c), 8), W_IN_SHARD)

            def to_sibling(q):
                return pltpu.make_async_remote_copy(
                    src_ref=stage.at[q % 2], dst_ref=recv_hbm.at[q], send_sem=pair_send.at[q], recv_sem=pair_recv.at[q],
                    device_id=sibling, device_id_type=MESH,
                )

            kept = [pltpu.make_async_copy(acc.at[rows(q, me[2]), :], own_hbm.at[q], own_sems.at[q]) for q in range(n_chips)]
            for cp in kept:
                cp.start()
            for q in range(n_chips):
                if q >= 2:
                    to_sibling(q - 2).wait_send()
                stage[q % 2] = acc[rows(q, sibling[2]), :].astype(BF16)
                to_sibling(q).start()
            for q in range(n_chips - 2, n_chips):
                to_sibling(q).wait_send()
            for q in range(n_chips):
                to_sibling(q).wait_recv()
            for cp in kept:
                cp.wait()
            for cp in _chip_copies([sg_hbm], [lg_hbm], views, send_sems, recv_sems):
                cp.wait_recv()
                cp.wait_send()

    any_spec = pl.BlockSpec(memory_space=pl.ANY)
    n_copies = len(CHIP_FLIPS)
    return pl.pallas_call(
        body,
        name="bwd_gw",
        grid=(n_tiles,),
        in_specs=_seg_specs(tm) + [pl.BlockSpec((tm, D), lambda i: (i, 0)), any_spec],
        out_specs=[any_spec, any_spec, any_spec],
        out_shape=[jax.ShapeDtypeStruct((n_chips, W_IN_SHARD, D), F32),
                   jax.ShapeDtypeStruct((n_chips, W_IN_SHARD, D), BF16),
                   jax.ShapeDtypeStruct((len(CHIP_FLIPS), 2, 4, GATE_SHARD, 256), BF16)],
        scratch_shapes=[pltpu.VMEM((IN_W, D), F32), pltpu.VMEM((2, W_IN_SHARD, D), BF16),
                        pltpu.SemaphoreType.DMA((n_chips,)), pltpu.SemaphoreType.DMA((n_chips,)), pltpu.SemaphoreType.DMA((n_chips,)),
                        pltpu.SemaphoreType.DMA((n_copies,)), pltpu.SemaphoreType.DMA((n_copies,))],
        compiler_params=_params(1),
    )(*dsegs, h, sum_g_bf)


def _adamw(w, g, m, v):
    m = ADAM_B1 * m + (1.0 - ADAM_B1) * g
    v = ADAM_B2 * v + (1.0 - ADAM_B2) * (g * g)
    m_hat = m / (1.0 - ADAM_B1**ADAM_STEP)
    v_hat = v / (1.0 - ADAM_B2**ADAM_STEP)
    delta = -ADAM_LR * (m_hat / (jnp.sqrt(v_hat) + ADAM_EPS) + ADAM_WD * w)
    return delta, m, v


def _w_in_update(own, land, w_t, m_t, v_t):
    n_slots, rows, cols = land.shape
    rb = 208
    spec = pl.BlockSpec((rb, cols), lambda i: (i, 0))

    def body(o_ref, l_ref, w_ref, m_ref, v_ref, g_ref, d_ref, m2_ref, v2_ref):
        total = o_ref[...]
        for s in range(n_slots):
            total = total + l_ref[s].astype(F32)
        g_ref[...] = total
        d_ref[...], m2_ref[...], v2_ref[...] = _adamw(w_ref[...], total, m_ref[...], v_ref[...])

    shape = jax.ShapeDtypeStruct((rows, cols), F32)
    return pl.pallas_call(
        body,
        name="w_in_update",
        grid=(rows // rb,),
        in_specs=[spec, pl.BlockSpec((n_slots, rb, cols), lambda i: (0, i, 0)), spec, spec, spec],
        out_specs=[spec] * 4,
        out_shape=[shape] * 4,
        compiler_params=_params(1),
    )(own, land, w_t, m_t, v_t)


def _shard_update(own_3, land_3, own_g, land_g, weights, m_in, v_in):
    n = len(weights)

    def body(*refs):
        o3_ref, l3_ref, og_ref, lg_ref = refs[:4]
        w_refs, m_refs, v_refs = (refs[4 + n * k : 4 + n * (k + 1)] for k in range(3))
        outs = refs[4 + 3 * n :]
        for k in range(n):
            if k < 3:
                g = o3_ref[k]
                for slot in range(3):
                    g = g + l3_ref[slot, k].astype(F32)
            else:
                g = og_ref[k - 3]
                for slot in range(3):
                    g = g + lg_ref[slot, k - 3].astype(F32)
            d, m2, v2 = _adamw(w_refs[k][0], g, m_refs[k][0], v_refs[k][0])
            for which, val in enumerate((g, d, m2, v2)):
                outs[which * n + k][0] = val

    shapes = [jax.ShapeDtypeStruct(w.shape, F32) for w in weights]
    res = pl.pallas_call(
        body, name="shard_update", out_shape=shapes * 4, compiler_params=pltpu.CompilerParams(vmem_limit_bytes=VMEM_LIMIT)
    )(own_3, land_3, own_g, land_g, *weights, *m_in, *v_in)
    return [res[n * k : n * (k + 1)] for k in range(4)]


VEC_LAYOUT = (("b_ada", 3 * D), ("norm_g", D), ("attn_sinks", 128), ("conv_b", D), ("rg_ba", D), ("rg_bx", D), ("rg_lambda", D),
              ("final_g", D), ("conv_w", CONV_WIDTH * D), ("loss", 128))
VEC_OFFSET = {name: sum(w for _, w in VEC_LAYOUT[:k]) for k, (name, _) in enumerate(VEC_LAYOUT)}
VEC_ROWS = 112
VEC_WIDTH = VEC_ROWS * 128
assert sum(w for _, w in VEC_LAYOUT) <= VEC_WIDTH
N_REPLICATED = 8


def _small_update(vec_all, c_all_t, me, rep_w, rep_m, rep_v, ada, conv):
    n_ada = ada[0].shape[1]
    n_conv = conv[0].shape[1]

    def body(*refs):
        me_ref, vec_ref, ct_ref = refs[:3]
        w_refs, m_refs, v_refs = (refs[3 + N_REPLICATED * k : 3 + N_REPLICATED * (k + 1)] for k in range(3))
        pos = 3 + 3 * N_REPLICATED
        ada_refs, conv_refs = refs[pos : pos + 3], refs[pos + 3 : pos + 6]
        outs = refs[pos + 6 :]
        loss_ref, rep_outs = outs[0], outs[1 : 1 + 4 * N_REPLICATED]
        ada_outs, conv_outs = outs[1 + 4 * N_REPLICATED : 5 + 4 * N_REPLICATED], outs[5 + 4 * N_REPLICATED : 9 + 4 * N_REPLICATED]
        conv_buf = outs[9 + 4 * N_REPLICATED]

        total = vec_ref[0:1, :]
        for slot in range(1, N_DEV):
            total = total + vec_ref[slot : slot + 1, :]
        loss_ref[...] = total[:, VEC_OFFSET["loss"] : VEC_OFFSET["loss"] + 1]
        for k, (name, _) in enumerate(VEC_LAYOUT[:N_REPLICATED]):
            width = w_refs[k].shape[1]
            g = total[:, VEC_OFFSET[name] : VEC_OFFSET[name] + width]
            d, m2, v2 = _adamw(w_refs[k][...], g, m_refs[k][...], v_refs[k][...])
            for which, val in enumerate((g, d, m2, v2)):
                rep_outs[which * N_REPLICATED + k][...] = val

        device = me_ref[0]
        for k in range(CONV_WIDTH):
            conv_buf[k : k + 1, :] = total[:, VEC_OFFSET["conv_w"] + D * k : VEC_OFFSET["conv_w"] + D * (k + 1)]
        g_conv = conv_buf[:, pl.ds(pl.multiple_of(device * n_conv, 128), n_conv)]
        for ref, val in zip(conv_outs, (g_conv,) + _adamw(conv_refs[0][...], g_conv, conv_refs[1][...], conv_refs[2][...])):
            ref[...] = val

        dmod = vec_ref[:, pl.ds(pl.multiple_of(device * n_ada, 128), n_ada)]
        g_ada = _dot(ct_ref[...], dmod)
        for ref, val in zip(ada_outs, (g_ada,) + _adamw(ada_refs[0][...], g_ada, ada_refs[1][...], ada_refs[2][...])):
            ref[...] = val

    vmem = pl.BlockSpec(memory_space=pltpu.VMEM)
    rep_shapes = [jax.ShapeDtypeStruct(w.shape, F32) for w in rep_w]
    out_shape = ([jax.ShapeDtypeStruct((1, 1), F32)] + rep_shapes * 4 + [jax.ShapeDtypeStruct(ada[0].shape, F32)] * 4
                 + [jax.ShapeDtypeStruct(conv[0].shape, F32)] * 4)
    n_in = 3 + 3 * N_REPLICATED + 6
    res = pl.pallas_call(
        body,
        name="small_update",
        in_specs=[pl.BlockSpec(memory_space=pltpu.SMEM)] + [vmem] * (n_in - 1),
        out_specs=[vmem] * len(out_shape),
        out_shape=out_shape,
        scratch_shapes=[pltpu.VMEM((CONV_WIDTH, D), F32)],
        compiler_params=pltpu.CompilerParams(vmem_limit_bytes=VMEM_LIMIT),
    )(me, vec_all, c_all_t, *rep_w, *rep_m, *rep_v, *ada, *conv)
    loss = res[0]
    reps = [res[1 + N_REPLICATED * k : 1 + N_REPLICATED * (k + 1)] for k in range(4)]
    return loss, reps, res[1 + 4 * N_REPLICATED : 5 + 4 * N_REPLICATED], res[5 + 4 * N_REPLICATED : 9 + 4 * N_REPLICATED]


def kernel(x, c, positions, w_ada, b_ada, norm_g, w_in, attn_sinks, conv_w, conv_b, rg_wa, rg_ba, rg_wx, rg_bx, rg_lambda, w_attn_proj, w_rnn_proj, w_out, final_g, loss_target, m_w_ada, m_b_ada, m_norm_g, m_w_in, m_attn_sinks, m_conv_w, m_conv_b, m_rg_wa, m_rg_ba, m_rg_wx, m_rg_bx, m_rg_lambda, m_w_attn_proj, m_w_rnn_proj, m_w_out, m_final_g, v_w_ada, v_b_ada, v_norm_g, v_w_in, v_attn_sinks, v_conv_w, v_conv_b, v_rg_wa, v_rg_ba, v_rg_wx, v_rg_bx, v_rg_lambda, v_w_attn_proj, v_w_rnn_proj, v_w_out, v_final_g):
    my_x, my_y, my_c = _position()
    me = _slot((my_x, my_y, my_c))
    x2, target = x[0], loss_target[0]
    pos = positions[0]

    gate_rows = lambda w: w[0].transpose(1, 0, 2).reshape(GATE_SHARD, D)
    w_pack = jnp.concatenate([w_attn_proj[0], w_rnn_proj[0], w_out[0], gate_rows(rg_wa), gate_rows(rg_wx)], axis=0).astype(BF16)
    conv_rows = jnp.pad(conv_w[0], ((0, 8 - CONV_WIDTH), (0, 0)))
    w_in_t = w_in[0].T
    n_ada = w_ada.shape[2]
    b_cols = lax.dynamic_slice(b_ada, (0, me * n_ada), (1, n_ada))

    tables = _rope_tables(pos)
    where = jnp.stack([_chip_of((my_x, my_y)), me]).astype(jnp.int32)
    proj, h, mod, w_t_all, w_all, c_rows, conv_all = _fwd_in(
        x2, jnp.broadcast_to(c, (8, D)), conv_rows, w_ada[0], b_cols, norm_g, w_in_t.astype(BF16), tables[0], w_pack, where
    )
    c_all = c_rows[:, 0, :]
    conv_full = conv_all[:, :CONV_WIDTH, :].transpose(1, 0, 2).reshape(CONV_WIDTH, D)
    gate = mod[:, 2 * D :]
    o = _attn_fwd(proj, attn_sinks)
    keep = (pos != 0).astype(F32)[:, None]
    hh = _rnn_fwd(proj, keep, w_all, rg_ba, rg_bx, rg_lambda, conv_full, conv_b)

    dres, d_o, d_hh, d_ga, d_gr, d_ma, d_mr, gw3, mid_vec = _mid(x2, target, o, hh, proj, w_all, gate, final_g[None, :])
    index = jnp.stack([my_c, _chip_of((my_x, my_y))]).astype(jnp.int32)
    part_3 = gw3.reshape(3, 4, 2, ROW_SHARD, D)
    d_q, d_k, d_v, d_sink, recv_3 = _attn_bwd(proj, o, d_o, tables, attn_sinks, part_3)
    sum_3_bf, own_3 = _pair_add(
        "pair_add_3", part_3, recv_3, index, (4,), 0,
        pl.BlockSpec((3, None, None, ROW_SHARD, D), lambda q, idx: (0, q, idx[0], 0, 0)),
        pl.BlockSpec((3, None, ROW_SHARD, D), lambda q, idx: (0, q, 0, 0)),
        pl.BlockSpec((3, ROW_SHARD, D), lambda q, idx: (0, 0, 0)), (3, ROW_SHARD, D),
    )
    d_xr, g_gates, rnn_vec, land_3 = _rnn_bwd(proj, hh, d_hh, keep, w_all, rg_ba, rg_bx, rg_lambda, conv_full, conv_b, sum_3_bf)
    dsegs = (d_q, d_k, d_v, d_ga, d_xr, d_gr, d_ma, d_mr)
    part_g = g_gates.reshape(2, 4, 4, 2, GATE_SHARD, 256)
    (recv_g,) = _exchange("rs_pair_g", [part_g], [lambda ref, peer: ref.at[:, :, :, peer[2]]], [(2, 4, 4, GATE_SHARD, 256)], flips=(1,))
    sum_g_bf, own_g = _pair_add(
        "pair_add_g", part_g, recv_g[0], index, (4,), 0,
        pl.BlockSpec((2, 4, None, None, GATE_SHARD, 256), lambda q, idx: (0, 0, q, idx[0], 0, 0)),
        pl.BlockSpec((2, 4, None, GATE_SHARD, 256), lambda q, idx: (0, 0, q, 0, 0)),
        pl.BlockSpec((2, 4, GATE_SHARD, 256), lambda q, idx: (0, 0, 0, 0)), (2, 4, GATE_SHARD, 256),
    )
    part_w, recv_w, land_g = _bwd_gw(dsegs, h, sum_g_bf)
    w_rows = pl.BlockSpec((None, W_IN_SHARD, D), lambda q, idx: (q, 0, 0))
    sum_w_bf, own_w = _pair_add(
        "pair_add_w", part_w, recv_w, index, (4,), 0, w_rows, w_rows,
        pl.BlockSpec((W_IN_SHARD, D), lambda q, idx: (0, 0)), (W_IN_SHARD, D),
    )
    grad_x, dh_vec, land_w = _bwd_dh(dsegs, w_t_all, x2, dres, mod, norm_g, sum_w_bf)

    pieces = dict(
        b_ada=jnp.concatenate([dh_vec[0:1], dh_vec[1:2], mid_vec[1:2]], axis=1),
        norm_g=dh_vec[2:3],
        attn_sinks=jnp.pad(d_sink[:, :, 0].reshape(1, 16), ((0, 0), (0, 112))),
        conv_b=rnn_vec[3:4],
        rg_ba=rnn_vec[0:1],
        rg_bx=rnn_vec[1:2],
        rg_lambda=rnn_vec[2:3],
        final_g=mid_vec[0:1],
        conv_w=rnn_vec[4:8].reshape(1, CONV_WIDTH * D),
        loss=mid_vec[2:3, 0:128],
    )
    vec = jnp.concatenate([pieces[name] for name, _ in VEC_LAYOUT], axis=1)
    vec = jnp.pad(vec, ((0, 0), (0, VEC_WIDTH - vec.shape[1]))).reshape(VEC_ROWS, 128)
    (vec_all,) = _all_gather_direct("ag_vec", [vec])
    rep_w = [b_ada, norm_g, attn_sinks, conv_b, rg_ba, rg_bx, rg_lambda, final_g[None, :]]
    rep_m = [m_b_ada, m_norm_g, m_attn_sinks, m_conv_b, m_rg_ba, m_rg_bx, m_rg_lambda, m_final_g[None, :]]
    rep_v = [v_b_ada, v_norm_g, v_attn_sinks, v_conv_b, v_rg_ba, v_rg_bx, v_rg_lambda, v_final_g[None, :]]
    loss, reps, ada_res, conv_res = _small_update(
        vec_all.reshape(N_DEV, VEC_WIDTH), c_all.T, me.astype(jnp.int32).reshape(1), rep_w, rep_m, rep_v,
        (w_ada[0], m_w_ada[0], v_w_ada[0]), (conv_w[0], m_conv_w[0], v_conv_w[0]),
    )

    w_in_res = _w_in_update(own_w, land_w, w_in_t, m_w_in[0].T, v_w_in[0].T)
    shard_res = _shard_update(
        own_3, land_3, own_g, land_g,
        (w_attn_proj, w_rnn_proj, w_out, rg_wa, rg_wx),
        (m_w_attn_proj, m_w_rnn_proj, m_w_out, m_rg_wa, m_rg_wx),
        (v_w_attn_proj, v_w_rnn_proj, v_w_out, v_rg_wa, v_rg_wx),
    )

    def by_weight(k):
        b, n, s, cb, ba, bx, lam, fg = reps[k]
        ap, rp, out, wa, wx = shard_res[k]
        return [ada_res[k][None], b, n, w_in_res[k].T[None], s, conv_res[k][None], cb, wa, ba, wx, bx, lam, ap, rp, out, fg[0]]

    return (loss[0, 0], grad_x[None], *by_weight(0), *by_weight(1), *by_weight(2), *by_weight(3))
```

```python
import math

import jax
import jax.numpy as jnp
from jax import lax
from jax.experimental import pallas as pl
from jax.experimental.pallas import tpu as pltpu

F32 = jnp.float32
BF16 = jnp.bfloat16

D = 1024
HEAD_DIM = 64
ROT_DIM = 16
ROPE_THETA = 500000.0
WINDOW = 128
LRU_C = 8.0
CONV_WIDTH = 4
NORM_EPS = 1e-6
IN_W = 6656
SEG_Q, SEG_K, SEG_V, SEG_GA, SEG_XR, SEG_GR, SEG_MA, SEG_MR = 0, 1024, 1280, 1536, 2560, 3584, 4608, 5632
N_DEV = 8
W_IN_SHARD = IN_W // N_DEV
ROW_SHARD = D // N_DEV
GATE_SHARD = 256 // N_DEV

ADAM_LR = 0.001
ADAM_B1 = 0.9
ADAM_B2 = 0.999
ADAM_EPS = 1e-08
ADAM_WD = 0.01
ADAM_STEP = 10

VMEM_LIMIT = 60 * 1024 * 1024
MESH = pl.DeviceIdType.MESH


def _dot(a, b):
    return jnp.dot(a, b, preferred_element_type=F32)


def _dot_nt(a, b):
    return lax.dot_general(a, b, (((1,), (1,)), ((), ())), preferred_element_type=F32)


def _dot_tn(a, b):
    return lax.dot_general(a, b, (((0,), (0,)), ((), ())), preferred_element_type=F32)


def _sigmoid(z):
    return 1.0 / (1.0 + jnp.exp(-z))


def _params(n_grid, vmem=VMEM_LIMIT):
    return pltpu.CompilerParams(dimension_semantics=("arbitrary",) * n_grid, vmem_limit_bytes=vmem)


def _position():
    return lax.axis_index("x"), lax.axis_index("y"), lax.axis_index("c")


def _flip(pos, k):
    x, y, c = pos
    return (1 - x if k & 4 else x, 1 - y if k & 2 else y, 1 - c if k & 1 else c)


def _slot(pos):
    return 4 * pos[0] + 2 * pos[1] + pos[2]


def _all_gather_direct(name, arrs):
    n = len(arrs)

    def body(*refs):
        ins, outs = refs[:n], refs[n : 2 * n]
        send_sems, recv_sems, local_sems = refs[2 * n :]
        me = _position()
        mine = [pltpu.make_async_copy(ins[a], outs[a].at[_slot(me)], local_sems.at[a]) for a in range(n)]
        for cp in mine:
            cp.start()

        def copy(a, k, block):
            return pltpu.make_async_remote_copy(
                src_ref=ins[a],
                dst_ref=outs[a].at[_slot(block)],
                send_sem=send_sems.at[a * 7 + k - 1],
                recv_sem=recv_sems.at[a * 7 + k - 1],
                device_id=_flip(me, k),
                device_id_type=MESH,
            )

        sent = [copy(a, k, me) for a in range(n) for k in range(1, N_DEV)]
        for cp in sent:
            cp.start()
        for a in range(n):
            for k in range(1, N_DEV):
                copy(a, k, _flip(me, k)).wait_recv()
        for cp in sent:
            cp.wait_send()
        for cp in mine:
            cp.wait()

    any_spec = pl.BlockSpec(memory_space=pl.ANY)
    return pl.pallas_call(
        body,
        name=name,
        out_shape=[jax.ShapeDtypeStruct((N_DEV,) + a.shape, a.dtype) for a in arrs],
        in_specs=[any_spec] * n,
        out_specs=[any_spec] * n,
        scratch_shapes=[
            pltpu.SemaphoreType.DMA((7 * n,)),
            pltpu.SemaphoreType.DMA((7 * n,)),
            pltpu.SemaphoreType.DMA((n,)),
        ],
    )(*arrs)


def _direct_gather_copies(src_hbm, all_hbm, send_sems, recv_sems, arriving):
    me = _position()
    return [
        pltpu.make_async_remote_copy(
            src_ref=src_hbm,
            dst_ref=all_hbm.at[_slot(_flip(me, k) if arriving else me)],
            send_sem=send_sems.at[k - 1],
            recv_sem=recv_sems.at[k - 1],
            device_id=_flip(me, k),
            device_id_type=MESH,
        )
        for k in range(1, N_DEV)
    ]


CHIP_FLIPS = (2, 4, 6)


def _chip_of(pos):
    return 2 * pos[0] + pos[1]


def _chip_copies(srcs, lands, views, send_sems, recv_sems):
    me = _position()
    copies = []
    for a, (src, land) in enumerate(zip(srcs, lands)):
        for j, k in enumerate(CHIP_FLIPS):
            peer = _flip(me, k)
            copies.append(
                pltpu.make_async_remote_copy(
                    src_ref=views[a](src, peer),
                    dst_ref=land.at[j],
                    send_sem=send_sems.at[len(CHIP_FLIPS) * a + j],
                    recv_sem=recv_sems.at[len(CHIP_FLIPS) * a + j],
                    device_id=peer,
                    device_id_type=MESH,
                )
            )
    return copies


def _exchange(name, arrs, views, out_shapes, flips):
    n = len(arrs)
    nf = len(flips)

    def body(*refs):
        ins, outs = refs[:n], refs[n : 2 * n]
        send_sems, recv_sems = refs[2 * n :]
        me = _position()

        def copy(a, j):
            peer = _flip(me, flips[j])
            return pltpu.make_async_remote_copy(
                src_ref=views[a](ins[a], peer),
                dst_ref=outs[a].at[j],
                send_sem=send_sems.at[a * nf + j],
                recv_sem=recv_sems.at[a * nf + j],
                device_id=peer,
                device_id_type=MESH,
            )

        sent = [copy(a, j) for a in range(n) for j in range(nf)]
        for cp in sent:
            cp.start()
        for cp in sent:
            cp.wait_recv()
        for cp in sent:
            cp.wait_send()

    any_spec = pl.BlockSpec(memory_space=pl.ANY)
    return pl.pallas_call(
        body,
        name=name,
        out_shape=[jax.ShapeDtypeStruct((nf,) + tuple(sh), a.dtype) for sh, a in zip(out_shapes, arrs)],
        in_specs=[any_spec] * n,
        out_specs=[any_spec] * n,
        scratch_shapes=[pltpu.SemaphoreType.DMA((nf * n,)), pltpu.SemaphoreType.DMA((nf * n,))],
    )(*arrs)


def _pair_add(name, own, recv, index, grid, chip_axis, own_spec, recv_spec, mine_spec, mine_shape):
    def body(idx_ref, own_ref, recv_ref, sb_ref, mine_ref):
        total = own_ref[...] + recv_ref[...].astype(F32)
        sb_ref[...] = total.astype(BF16)

        @pl.when(pl.program_id(chip_axis) == idx_ref[1])
        def _():
            mine_ref[...] = total

    return pl.pallas_call(
        body,
        name=name,
        grid_spec=pltpu.PrefetchScalarGridSpec(
            num_scalar_prefetch=1, grid=grid, in_specs=[own_spec, recv_spec], out_specs=[recv_spec, mine_spec]
        ),
        out_shape=[jax.ShapeDtypeStruct(recv.shape, BF16), jax.ShapeDtypeStruct(mine_shape, F32)],
        compiler_params=_params(len(grid)),
    )(index, own, recv)


PACK_GATE = 3 * ROW_SHARD
PACK_ROWS = PACK_GATE + 2 * GATE_SHARD


def _fwd_in(x, c_rows, conv_rows, w_ada, b_cols, norm_g, w_shard, tables, pack, where):
    t = x.shape[0]
    tm = min(1024, t)
    tn = IN_W // 4
    n_i, n_j = t // tm, IN_W // tn
    n_rope = (SEG_V - SEG_Q) // 128

    n_ada = w_ada.shape[1]

    def body(where_ref, x_ref, wada_ref, bcols_ref, ng_ref, cos_ref, s1_ref, s2_ref, shard_hbm, pack_hbm, c_hbm, conv_hbm,
             proj_ref, h_ref, mod_ref, wt_hbm, all_hbm, call_hbm, convall_hbm,
             wall, hbuf, cbuf, mcols, mod_all, mod_buf,
             w_send, w_recv, load_sems, local_sem, p_send, p_recv, p_local,
             c_send, c_recv, c_local, v_send, v_recv, v_local, m_send, m_recv):
        j = pl.program_id(0)
        i = pl.program_id(1)
        me = _position()
        sibling = _flip(me, 1)
        chip = where_ref[0]
        slot = where_ref[1]

        def modulation():
            own_c = pltpu.make_async_copy(c_hbm, call_hbm.at[_slot(me)], c_local)
            own_v = pltpu.make_async_copy(conv_hbm, convall_hbm.at[_slot(me)], v_local)
            own_c.start()
            own_v.start()
            for cp in _direct_gather_copies(c_hbm, call_hbm, c_send, c_recv, arriving=False):
                cp.start()
            for cp in _direct_gather_copies(conv_hbm, convall_hbm, v_send, v_recv, arriving=False):
                cp.start()
            for cp in _direct_gather_copies(c_hbm, call_hbm, c_send, c_recv, arriving=True):
                cp.wait_recv()
            own_c.wait()
            fetch = pltpu.make_async_copy(call_hbm, cbuf, c_local)
            fetch.start()
            fetch.wait()
            mcols[...] = _dot(cbuf[:, 0, :].astype(BF16), wada_ref[...].astype(BF16)) + bcols_ref[...]
            mod_all[slot] = mcols[...]
            swaps = lambda arriving: [
                pltpu.make_async_remote_copy(
                    src_ref=mcols,
                    dst_ref=mod_all.at[_slot(_flip(me, k) if arriving else me)],
                    send_sem=m_send.at[k - 1],
                    recv_sem=m_recv.at[k - 1],
                    device_id=_flip(me, k),
                    device_id_type=MESH,
                )
                for k in range(1, N_DEV)
            ]
            for cp in swaps(False):
                cp.start()
            for cp in swaps(True):
                cp.wait_recv()
            mine = lax.broadcasted_iota(jnp.int32, (N_DEV, n_ada), 0) == slot
            for s in range(N_DEV):
                row = jnp.sum(jnp.where(mine, mod_all[s], 0.0), axis=0, keepdims=True)
                mod_buf[:, n_ada * s : n_ada * (s + 1)] = row
            for cp in swaps(False):
                cp.wait_send()
            for cp in _direct_gather_copies(c_hbm, call_hbm, c_send, c_recv, arriving=False):
                cp.wait_send()
            for cp in _direct_gather_copies(conv_hbm, convall_hbm, v_send, v_recv, arriving=True):
                cp.wait_recv()
            for cp in _direct_gather_copies(conv_hbm, convall_hbm, v_send, v_recv, arriving=False):
                cp.wait_send()
            own_v.wait()

        def shard_of(block):
            return wall.at[_chip_of(block), pl.ds(W_IN_SHARD * block[2], W_IN_SHARD), :]

        def shard_copy(k, block, to, src=None):
            return pltpu.make_async_remote_copy(
                src_ref=shard_of(block),
                dst_ref=shard_of(block),
                send_sem=w_send.at[k],
                recv_sem=w_recv.at[k],
                device_id=to,
                device_id_type=MESH,
            )

        own_shard = pltpu.make_async_copy(shard_hbm, shard_of(me), local_sem)
        own_pack = pltpu.make_async_copy(pack_hbm, all_hbm.at[_slot(me)], p_local)

        def keep_tile(q):
            return [
                pltpu.make_async_copy(wall.at[q, pl.ds(W_IN_SHARD * c, W_IN_SHARD), :], wt_hbm.at[2 * q + c], load_sems.at[2 * (q ^ chip) + c])
                for c in range(2)
            ]

        @pl.when((j == 0) & (i == 0))
        def _():
            modulation()
            mod_ref[...] = mod_buf[...]
            own_shard.start()
            own_shard.wait()
            shard_copy(0, me, sibling).start()
            for rel in range(1, 4):
                shard_copy(rel, me, _flip(me, 2 * rel)).start()
            own_pack.start()
            for cp in _direct_gather_copies(pack_hbm, all_hbm, p_send, p_recv, arriving=False):
                cp.start()
            shard_copy(0, sibling, me).wait_recv()
            for cp in keep_tile(chip):
                cp.start()

        for rel in range(1, 4):

            @pl.when((j == rel) & (i == 0))
            def _(rel=rel):
                there = _flip(me, 2 * rel)
                shard_copy(rel, there, me).wait_recv()
                shard_copy(3 + rel, there, sibling).start()
                shard_copy(3 + rel, _flip(sibling, 2 * rel), me).wait_recv()
                for cp in keep_tile(chip ^ rel):
                    cp.start()

        rows = pl.ds(pl.multiple_of(i * tm, tm), tm)

        @pl.when(j == 0)
        def _():
            xv = x_ref[...]
            r = lax.rsqrt(jnp.mean(xv * xv, axis=-1, keepdims=True) + NORM_EPS)
            shift = mod_buf[:, 0:D]
            scale = mod_buf[:, D : 2 * D]
            hv = (((xv * r) * ng_ref[...]) * (1.0 + scale) + shift).astype(BF16)
            hbuf[rows, :] = hv
            h_ref[...] = hv

        proj_ref[...] = _dot_nt(hbuf[rows, :], wall[chip ^ j])

        @pl.when((chip ^ j) == 0)
        def _():
            cos, s1, s2 = cos_ref[...], s1_ref[...], s2_ref[...]
            for g in range(n_rope):
                cols = slice(128 * g, 128 * (g + 1))
                proj_ref[:, cols] = _rope(proj_ref[:, cols], cos, s1, s2)

        @pl.when((j == n_j - 1) & (i == n_i - 1))
        def _():
            shard_copy(0, me, sibling).wait_send()
            for rel in range(1, 4):
                shard_copy(rel, me, _flip(me, 2 * rel)).wait_send()
                shard_copy(3 + rel, _flip(me, 2 * rel), sibling).wait_send()
            for q in range(N_DEV // 2):
                for cp in keep_tile(q):
                    cp.wait()
            for cp in _direct_gather_copies(pack_hbm, all_hbm, p_send, p_recv, arriving=True):
                cp.wait_recv()
            for cp in _direct_gather_copies(pack_hbm, all_hbm, p_send, p_recv, arriving=False):
                cp.wait_send()
            own_pack.wait()

    tab = pl.BlockSpec((tm, 128), lambda j, i, c: (i, 0))
    any_spec = pl.BlockSpec(memory_space=pl.ANY)
    return pl.pallas_call(
        body,
        name="fwd_in",
        grid_spec=pltpu.PrefetchScalarGridSpec(
            num_scalar_prefetch=1,
            grid=(n_j, n_i),
            in_specs=[
                pl.BlockSpec((tm, D), lambda j, i, c: (jnp.where(j == 0, i, 0), 0)),
                pl.BlockSpec(w_ada.shape, lambda j, i, c: (0, 0)),
                pl.BlockSpec((1, n_ada), lambda j, i, c: (0, 0)),
                pl.BlockSpec((1, D), lambda j, i, c: (0, 0)),
                tab, tab, tab, any_spec, any_spec, any_spec, any_spec,
            ],
            out_specs=[pl.BlockSpec((tm, tn), lambda j, i, c: (i, c[0] ^ j)),
                       pl.BlockSpec((tm, D), lambda j, i, c: (jnp.where(j == 0, i, n_i - 1), 0)),
                       pl.BlockSpec((1, 3 * D), lambda j, i, c: (0, 0)),
                       any_spec, any_spec, any_spec, any_spec],
            scratch_shapes=[
                pltpu.VMEM((N_DEV // 2, tn, D), BF16),
                pltpu.VMEM((t, D), BF16),
                pltpu.VMEM((N_DEV,) + c_rows.shape, F32),
                pltpu.VMEM((N_DEV, n_ada), F32),
                pltpu.VMEM((N_DEV, N_DEV, n_ada), F32),
                pltpu.VMEM((1, 3 * D), F32),
                pltpu.SemaphoreType.DMA((N_DEV - 1,)),
                pltpu.SemaphoreType.DMA((N_DEV - 1,)),
                pltpu.SemaphoreType.DMA((N_DEV,)),
                pltpu.SemaphoreType.DMA,
                pltpu.SemaphoreType.DMA((N_DEV - 1,)),
                pltpu.SemaphoreType.DMA((N_DEV - 1,)),
                pltpu.SemaphoreType.DMA,
                pltpu.SemaphoreType.DMA((N_DEV - 1,)),
                pltpu.SemaphoreType.DMA((N_DEV - 1,)),
                pltpu.SemaphoreType.DMA,
                pltpu.SemaphoreType.DMA((N_DEV - 1,)),
                pltpu.SemaphoreType.DMA((N_DEV - 1,)),
                pltpu.SemaphoreType.DMA,
                pltpu.SemaphoreType.DMA((N_DEV - 1,)),
                pltpu.SemaphoreType.DMA((N_DEV - 1,)),
            ],
        ),
        out_shape=[jax.ShapeDtypeStruct((t, IN_W), F32), jax.ShapeDtypeStruct((t, D), BF16),
                   jax.ShapeDtypeStruct((1, 3 * D), F32),
                   jax.ShapeDtypeStruct((N_DEV,) + w_shard.shape, w_shard.dtype),
                   jax.ShapeDtypeStruct((N_DEV,) + pack.shape, pack.dtype),
                   jax.ShapeDtypeStruct((N_DEV,) + c_rows.shape, F32),
                   jax.ShapeDtypeStruct((N_DEV,) + conv_rows.shape, F32)],
        compiler_params=_params(2),
    )(where, x, w_ada, b_cols, norm_g, *tables, w_shard, pack, c_rows, conv_rows)


def _rope_tables(positions):
    half = ROT_DIM // 2
    inv_freq = ROPE_THETA ** (-jnp.arange(0, ROT_DIM, 2, dtype=F32) / ROT_DIM)
    pos = positions.astype(F32)
    lane = jnp.arange(128) % HEAD_DIM
    freq_lane = jnp.where(lane < ROT_DIM, jnp.tile(inv_freq, 128 // half), 0.0)
    ang = pos[:, None] * freq_lane[None, :]
    sin = jnp.sin(ang)
    lane_tables = (jnp.cos(ang), jnp.where(lane < half, -sin, 0.0), jnp.where((lane >= half) & (lane < ROT_DIM), sin, 0.0))
    ang_t = inv_freq[:, None] * pos[None, :]
    return lane_tables, (jnp.cos(ang_t), jnp.sin(ang_t))


def _rope(tv, cos, s1, s2):
    cols = []
    for g in range(tv.shape[1] // 128):
        blk = tv[:, 128 * g : 128 * (g + 1)]
        cols.append(blk * cos + pltpu.roll(blk, 128 - 8, 1) * s1 + pltpu.roll(blk, 8, 1) * s2)
    return cols[0] if len(cols) == 1 else jnp.concatenate(cols, axis=1)


def _rope_transposed(gv, cos, s1, s2):
    cols = []
    for g in range(gv.shape[1] // 128):
        blk = gv[:, 128 * g : 128 * (g + 1)]
        cols.append(blk * cos + pltpu.roll(blk * s1, 8, 1) + pltpu.roll(blk * s2, 128 - 8, 1))
    return cols[0] if len(cols) == 1 else jnp.concatenate(cols, axis=1)


def _lane_lo():
    return lax.broadcasted_iota(jnp.int32, (1, 128), 1) < HEAD_DIM


def _row_lo():
    return lax.broadcasted_iota(jnp.int32, (128, 1), 0) < HEAD_DIM


def _split_heads(tile, j):
    lo = _lane_lo()
    parts = []
    for g in (2 * j, 2 * j + 1):
        blk = tile[:, 128 * g : 128 * (g + 1)]
        parts += [jnp.where(lo, blk, 0.0), jnp.where(lo, 0.0, blk)]
    return jnp.concatenate(parts, axis=0)


def _dup_lanes(kv, j):
    rolled = pltpu.roll(kv, HEAD_DIM, 1)
    lo = _lane_lo()
    return jnp.where(lo, kv, rolled) if j == 0 else jnp.where(lo, rolled, kv)


def _fold_lanes(acc):
    return acc + pltpu.roll(acc, HEAD_DIM, 1)


def _fill_bias_t(bias_ref):
    kj = lax.broadcasted_iota(jnp.int32, (256, 128), 0)
    qi = lax.broadcasted_iota(jnp.int32, (256, 128), 1)
    diff = qi + WINDOW - kj
    band = (diff >= 0) & (diff < WINDOW)
    bias_ref[0] = jnp.where(band, 0.0, -1e30)
    bias_ref[1] = jnp.where(band & (kj >= 128), 0.0, -1e30)


def _sink_row(sink_ref, pair, j):
    return jnp.concatenate([jnp.full((1, 128), sink_ref[0, 8 * pair + 4 * j + h], F32) for h in range(4)], axis=1)


def _softmax_t(q4, kw, bias4, sink):
    s = _dot_nt(kw, q4) + bias4
    m = jnp.maximum(jnp.max(s, axis=0, keepdims=True), sink)
    p = jnp.exp(s - m)
    e_sink = jnp.exp(sink - m)
    denom = jnp.sum(p, axis=0, keepdims=True) + e_sink
    return p, e_sink, denom


def _attn_specs(t, tq):
    return dict(
        q=pl.BlockSpec((tq, 512), lambda p, i: (i, p)),
        k=pl.BlockSpec((t, 128), lambda p, i: (0, SEG_K // 128 + p)),
        v=pl.BlockSpec((t, 128), lambda p, i: (0, SEG_V // 128 + p)),
        sink=pl.BlockSpec(memory_space=pltpu.SMEM),
        act=pl.BlockSpec((tq, 512), lambda p, i: (i, p)),
    )


def _attn_fwd(proj, sinks):
    t = proj.shape[0]
    tq = min(512, t)
    nb = tq // 128
    n_tiles = t // tq
    sp = _attn_specs(t, tq)

    def body(q_ref, k_ref, v_ref, sink_ref, o_ref, kd, vdt, bias_ref):
        pair = pl.program_id(0)
        i = pl.program_id(1)

        @pl.when(i == 0)
        def _():
            kv, vv = k_ref[...], v_ref[...]
            for j in range(2):
                kd[j, 0:128, :] = jnp.zeros((128, 128), BF16)
                kd[j, 128:, :] = _dup_lanes(kv, j).astype(BF16)
                vdt[j, :, 0:128] = jnp.zeros((128, 128), BF16)
                vdt[j, :, 128:] = _dup_lanes(vv, j).T.astype(BF16)
            _fill_bias_t(bias_ref)

        row_lo = _row_lo()

        def block(b, carry):
            r0 = pl.multiple_of(b * 128, 128)
            gb = i * nb + b
            keys = pl.ds(pl.multiple_of(gb * 128, 128), 256)
            rows = pl.ds(r0, 128)
            qt = q_ref[rows, :] * (1.0 / math.sqrt(HEAD_DIM))
            bias = bias_ref[jnp.where(gb == 0, 1, 0)]
            bias4 = jnp.concatenate([bias] * 4, axis=1)
            cols = []
            for j in range(2):
                q4 = _split_heads(qt, j).astype(BF16)
                p, _, denom = _softmax_t(q4, kd[j, keys, :], bias4, _sink_row(sink_ref, pair, j))
                ot = _dot(vdt[j, :, keys], p.astype(BF16)) * (1.0 / denom)
                for gi in range(2):
                    z = jnp.where(row_lo, ot[:, 256 * gi : 256 * gi + 128], ot[:, 256 * gi + 128 : 256 * gi + 256])
                    cols.append(z.T)
            o_ref[rows, :] = jnp.concatenate(cols, axis=1).astype(BF16)
            return carry

        lax.fori_loop(0, nb, block, 0, unroll=True)

    return pl.pallas_call(
        body,
        name="attn_fwd",
        grid=(2, n_tiles),
        in_specs=[sp["q"], sp["k"], sp["v"], sp["sink"]],
        out_specs=sp["act"],
        out_shape=jax.ShapeDtypeStruct((t, D), BF16),
        scratch_shapes=[pltpu.VMEM((2, t + 128, 128), BF16), pltpu.VMEM((2, 128, t + 128), BF16), pltpu.VMEM((2, 256, 128), F32)],
        compiler_params=_params(2),
    )(proj, proj, proj, sinks)


def _attn_bwd(proj, o, d_o, tables, sinks, part_3):
    t = proj.shape[0]
    tq = min(512, t)
    nb = tq // 128
    n_tiles = t // tq
    (cos, s1, s2), (cos_t, sin_t) = tables
    sp = _attn_specs(t, tq)
    scale = 1.0 / math.sqrt(HEAD_DIM)
    half = ROT_DIM // 2

    def unrope_rows(z, c, s):
        out = []
        for base in (0, HEAD_DIM):
            g0, g1 = z[base : base + half], z[base + half : base + ROT_DIM]
            out += [g0 * c + g1 * s, g1 * c - g0 * s, z[base + ROT_DIM : base + HEAD_DIM]]
        return jnp.concatenate(out, axis=0)

    def swap(part_hbm, recv_hbm, send_sem, recv_sem):
        sibling = _flip(_position(), 1)
        return pltpu.make_async_remote_copy(
            src_ref=part_hbm.at[:, :, sibling[2]], dst_ref=recv_hbm, send_sem=send_sem, recv_sem=recv_sem,
            device_id=sibling, device_id_type=MESH,
        )

    def body(q_ref, k_ref, v_ref, o_ref, do_ref, ct_ref, st_ref, ck, s1k, s2k, sink_ref, part_hbm,
             dq_ref, dk_ref, dv_ref, dsink_ref, recv_hbm, kd, kdt, vd, bias_ref, dkacc, dvacc, send_sem, recv_sem):
        pair = pl.program_id(0)
        i = pl.program_id(1)

        @pl.when((pair == 0) & (i == 0))
        def _():
            swap(part_hbm, recv_hbm, send_sem, recv_sem).start()

        @pl.when((pair == 1) & (i == n_tiles - 1))
        def _():
            swap(part_hbm, recv_hbm, send_sem, recv_sem).wait()

        @pl.when(i == 0)
        def _():
            kv, vv = k_ref[...], v_ref[...]
            for j in range(2):
                kdup = _dup_lanes(kv, j)
                kd[j, 0:128, :] = jnp.zeros((128, 128), BF16)
                kd[j, 128:, :] = kdup.astype(BF16)
                kdt[j, :, 0:128] = jnp.zeros((128, 128), BF16)
                kdt[j, :, 128:] = kdup.T.astype(BF16)
                vd[j, 0:128, :] = jnp.zeros((128, 128), BF16)
                vd[j, 128:, :] = _dup_lanes(vv, j).astype(BF16)
            _fill_bias_t(bias_ref)
            dkacc[...] = jnp.zeros_like(dkacc)
            dvacc[...] = jnp.zeros_like(dvacc)
            dsink_ref[...] = jnp.zeros_like(dsink_ref)

        row_lo = _row_lo()
        ones = jnp.ones((8, 128), BF16)

        def block(b, carry):
            r0 = pl.multiple_of(b * 128, 128)
            gb = i * nb + b
            keys = pl.ds(pl.multiple_of(gb * 128, 128), 256)
            rows = pl.ds(r0, 128)
            qt = q_ref[rows, :] * scale
            dob = do_ref[rows, :].astype(F32)
            prod = dob * o_ref[rows, :].astype(F32)
            ct, st = ct_ref[:, rows], st_ref[:, rows]
            bias = bias_ref[jnp.where(gb == 0, 1, 0)]
            bias4 = jnp.concatenate([bias] * 4, axis=1)
            cols = []
            for j in range(2):
                q4 = _split_heads(qt, j).astype(BF16)
                kw = kd[j, keys, :]
                p, e_sink, denom = _softmax_t(q4, kw, bias4, _sink_row(sink_ref, pair, j))
                inv = 1.0 / denom
                prob = p * inv
                do4 = _split_heads(dob, j).astype(BF16)
                delta = _dot_nt(ones, _split_heads(prod, j).astype(BF16))[0:1, :]
                dp = _dot_nt(vd[j, keys, :], do4)
                ds = (prob * (dp - delta)).astype(BF16)
                dkacc[j, keys, :] = dkacc[j, keys, :] + _dot(ds, q4)
                dvacc[j, keys, :] = dvacc[j, keys, :] + _dot(prob.astype(BF16), do4)
                dqt = _dot(kdt[j, :, keys], ds) * scale
                dsink = -(e_sink * inv) * delta
                for h in range(4):
                    row = 4 * j + h
                    total = jnp.sum(dsink[:, 128 * h : 128 * (h + 1)], axis=1, keepdims=True)
                    dsink_ref[0, row : row + 1, :] = dsink_ref[0, row : row + 1, :] + jnp.broadcast_to(total, (1, 128))
                for gi in range(2):
                    z = jnp.where(row_lo, dqt[:, 256 * gi : 256 * gi + 128], dqt[:, 256 * gi + 128 : 256 * gi + 256])
                    cols.append(unrope_rows(z, ct, st).T)
            dq_ref[rows, :] = jnp.concatenate(cols, axis=1).astype(BF16)
            return carry

        lax.fori_loop(0, nb, block, 0, unroll=True)

        @pl.when(i == n_tiles - 1)
        def _():
            lo = _lane_lo()
            dk = jnp.where(lo, _fold_lanes(dkacc[0, 128:, :]), _fold_lanes(dkacc[1, 128:, :]))
            dk_ref[...] = _rope_transposed(dk, ck[...], s1k[...], s2k[...]).astype(BF16)
            dv_ref[...] = jnp.where(lo, _fold_lanes(dvacc[0, 128:, :]), _fold_lanes(dvacc[1, 128:, :])).astype(BF16)

    kv_out = pl.BlockSpec((t, 128), lambda p, i: (0, p))
    tab_t = pl.BlockSpec((half, tq), lambda p, i: (0, i))
    tab_k = pl.BlockSpec((t, 128), lambda p, i: (0, 0))
    return pl.pallas_call(
        body,
        name="attn_bwd",
        grid=(2, n_tiles),
        in_specs=[sp["q"], sp["k"], sp["v"], sp["act"], sp["act"], tab_t, tab_t, tab_k, tab_k, tab_k, sp["sink"],
                  pl.BlockSpec(memory_space=pl.ANY)],
        out_specs=[sp["act"], kv_out, kv_out, pl.BlockSpec((1, 8, 128), lambda p, i: (p, 0, 0)), pl.BlockSpec(memory_space=pl.ANY)],
        out_shape=[
            jax.ShapeDtypeStruct((t, D), BF16),
            jax.ShapeDtypeStruct((t, 256), BF16),
            jax.ShapeDtypeStruct((t, 256), BF16),
            jax.ShapeDtypeStruct((2, 8, 128), F32),
            jax.ShapeDtypeStruct((3, 4, ROW_SHARD, D), F32),
        ],
        scratch_shapes=[pltpu.VMEM((2, t + 128, 128), BF16), pltpu.VMEM((2, 128, t + 128), BF16), pltpu.VMEM((2, t + 128, 128), BF16),
                        pltpu.VMEM((2, 256, 128), F32), pltpu.VMEM((2, t + 128, 128), F32), pltpu.VMEM((2, t + 128, 128), F32),
                        pltpu.SemaphoreType.DMA, pltpu.SemaphoreType.DMA],
        compiler_params=_params(2),
    )(proj, proj, proj, o, d_o, cos_t, sin_t, cos, s1, s2, sinks, part_3)


HALF = 512


def _softplus(y):
    u = jnp.exp(-jnp.abs(y))
    w = 1.0 + u
    log1p = jnp.where(w == 1.0, u, jnp.log(w) * (u / jnp.where(w == 1.0, 1.0, w - 1.0)))
    return jnp.maximum(y, 0.0) + log1p


def _gate_block(w_ref, blk):
    return w_ref[:, :, 256 * blk : 256 * (blk + 1)].reshape(256, 256)


def _gate_matmul(act_bf, w_ref):
    return jnp.concatenate([_dot(act_bf[:, 256 * blk : 256 * (blk + 1)], _gate_block(w_ref, blk)) for blk in range(2)], axis=1)


def _gate_matmul_t(grad_bf, w_ref):
    return jnp.concatenate([_dot_nt(grad_bf[:, 256 * blk : 256 * (blk + 1)], _gate_block(w_ref, blk)) for blk in range(2)], axis=1)


def _conv_and_gates(xe_ref, tm, cw_ref, cb_ref, wa_ref, wx_ref, ba_ref, bx_ref, lam_ref, keep):
    xc = cb_ref[...] + sum(cw_ref[k : k + 1, :] * xe_ref[pl.ds(5 + k, tm), :] for k in range(CONV_WIDTH))
    xc_bf = xc.astype(BF16)
    r = _sigmoid(_gate_matmul(xc_bf, wa_ref) + ba_ref[...])
    ig = _sigmoid(_gate_matmul(xc_bf, wx_ref) + bx_ref[...])
    sp = _softplus(-lam_ref[...])
    log_a = -LRU_C * r * sp
    tanh = jnp.tanh(-log_a)
    mult = jnp.sqrt(2.0 * tanh / (1.0 + tanh))
    a = jnp.where(keep, jnp.exp(log_a), 0.0)
    mult = jnp.where(keep, mult, 1.0)
    return xc, xc_bf, r, ig, sp, a, mult


def _scan_rows(a_ref, b_ref, h_ref, carry_ref, tm, reverse):
    width = a_ref.shape[1]
    row = lax.broadcasted_iota(jnp.int32, (8, width), 0)

    def group(g, carry):
        gg = tm // 8 - 1 - g if reverse else g
        r0 = pl.multiple_of(gg * 8, 8)
        a = a_ref[pl.ds(r0, 8), :]
        b = b_ref[pl.ds(r0, 8), :]
        for s in (1, 2, 4):
            shift = 8 - s if reverse else s
            a_sh = pltpu.roll(a, shift, 0)
            b_sh = pltpu.roll(b, shift, 0)
            ok = (row < 8 - s) if reverse else (row >= s)
            b = jnp.where(ok, a * b_sh + b, b)
            a = jnp.where(ok, a * a_sh, a)
        h = a * carry + b
        h_ref[pl.ds(r0, 8), :] = h
        last = h[0:1, :] if reverse else h[7:8, :]
        return jnp.broadcast_to(last, (8, width))

    carry_ref[...] = lax.fori_loop(0, tm // 8, group, carry_ref[...])


def _rnn_specs(t, tm, order):
    n = t // tm
    return dict(
        tile=pl.BlockSpec((tm, HALF), lambda ch, i: (order(i, n), ch)),
        before=lambda col0: pl.BlockSpec((8, HALF), lambda ch, i: (jnp.maximum(order(i, n) * (tm // 8) - 1, 0), col0 + ch)),
        proj_tile=lambda col0: pl.BlockSpec((tm, HALF), lambda ch, i: (order(i, n), col0 + ch)),
        keep=pl.BlockSpec((tm, 1), lambda ch, i: (order(i, n), 0)),
        gates=lambda which: pl.BlockSpec((N_DEV, GATE_SHARD, HALF), lambda ch, i: (0, PACK_GATE // GATE_SHARD + which, ch)),
        gate_grads=pl.BlockSpec((2, 2, 256, 256), lambda ch, i: (0, ch, 0, 0)),
        vec=pl.BlockSpec((1, HALF), lambda ch, i: (0, ch)),
        conv=pl.BlockSpec((CONV_WIDTH, HALF), lambda ch, i: (0, ch)),
    )


def _rnn_fwd(proj, keep, w_all, rg_ba, rg_bx, rg_lambda, conv_w, conv_b):
    t = proj.shape[0]
    tm = min(512, t)
    sp = _rnn_specs(t, tm, lambda i, n: i)

    def body(xr_ref, xb_ref, keep_ref, wa_ref, wx_ref, ba_ref, bx_ref, lam_ref, cw_ref, cb_ref, hh_ref, xe, a_buf, b_buf, carry):
        i = pl.program_id(1)

        @pl.when(i == 0)
        def _():
            carry[...] = jnp.zeros_like(carry)

        xe[0:8, :] = jnp.where(i > 0, xb_ref[...], 0.0)
        xe[8:, :] = xr_ref[...]
        keep_rows = keep_ref[...] != 0.0
        xc, _, _, ig, _, a, mult = _conv_and_gates(xe, tm, cw_ref, cb_ref, wa_ref, wx_ref, ba_ref, bx_ref, lam_ref, keep_rows)
        a_buf[...] = a
        b_buf[...] = mult * (ig * xc)
        _scan_rows(a_buf, b_buf, hh_ref, carry, tm, reverse=False)

    return pl.pallas_call(
        body,
        name="rnn_fwd",
        grid=(2, t // tm),
        in_specs=[sp["proj_tile"](SEG_XR // HALF), sp["before"](SEG_XR // HALF), sp["keep"], sp["gates"](0), sp["gates"](1),
                  sp["vec"], sp["vec"], sp["vec"], sp["conv"], sp["vec"]],
        out_specs=sp["tile"],
        out_shape=jax.ShapeDtypeStruct((t, D), F32),
        scratch_shapes=[pltpu.VMEM((tm + 8, HALF), F32), pltpu.VMEM((tm, HALF), F32), pltpu.VMEM((tm, HALF), F32), pltpu.VMEM((8, HALF), F32)],
        compiler_params=_params(2),
    )(proj, proj, keep, w_all, w_all, rg_ba, rg_bx, rg_lambda, conv_w, conv_b)


def _rnn_bwd(proj, hh, d_hh, keep, w_all, rg_ba, rg_bx, rg_lambda, conv_w, conv_b, sum_3_bf):
    t = proj.shape[0]
    tm = min(512, t)
    n_tiles = t // tm
    sp = _rnn_specs(t, tm, lambda i, n: n - 1 - i)

    def body(xr_ref, xb_ref, hh_ref, hb_ref, dhh_ref, keep_ref, wa_ref, wx_ref, ba_ref, bx_ref, lam_ref, cw_ref, cb_ref,
             s3_hbm, dxr_ref, gw_ref, gvec_ref, l3_hbm, xe, he, ae, c_buf, g_buf, dxe, carry, a_next, send_sems, recv_sems):
        ch = pl.program_id(0)
        i = pl.program_id(1)
        first_tile = i == n_tiles - 1
        views = [lambda ref, peer: ref.at[:, _chip_of(peer)]]

        @pl.when((ch == 0) & (i == 0))
        def _():
            for cp in _chip_copies([s3_hbm], [l3_hbm], views, send_sems, recv_sems):
                cp.start()

        @pl.when((ch == 1) & (i == n_tiles - 1))
        def _():
            for cp in _chip_copies([s3_hbm], [l3_hbm], views, send_sems, recv_sems):
                cp.wait_recv()
                cp.wait_send()

        @pl.when(i == 0)
        def _():
            carry[...] = jnp.zeros_like(carry)
            a_next[...] = jnp.zeros_like(a_next)
            dxe[tm:, :] = jnp.zeros((8, HALF), F32)
            gw_ref[...] = jnp.zeros_like(gw_ref)
            gvec_ref[...] = jnp.zeros_like(gvec_ref)

        xe[0:8, :] = jnp.where(first_tile, 0.0, xb_ref[...])
        xe[8:, :] = xr_ref[...]
        he[0:8, :] = jnp.where(first_tile, 0.0, hb_ref[...])
        he[8:, :] = hh_ref[...]
        keep_rows = keep_ref[...] != 0.0
        xc, xc_bf, r, ig, sp_, a, mult = _conv_and_gates(xe, tm, cw_ref, cb_ref, wa_ref, wx_ref, ba_ref, bx_ref, lam_ref, keep_rows)

        ae[0:tm, :] = a
        ae[tm:, :] = a_next[...]
        a_next[...] = jnp.broadcast_to(a[0:1, :], (8, HALF))
        c_buf[...] = ae[pl.ds(1, tm), :]
        g_buf[...] = dhh_ref[...]
        _scan_rows(c_buf, g_buf, g_buf, carry, tm, reverse=True)
        g = g_buf[...]

        h_prev = he[pl.ds(7, tm), :]
        bb = ig * xc
        d_ig = g * mult * xc
        d_xc = g * mult * ig
        d_log_a = jnp.where(keep_rows, g * h_prev * a - (g * bb) * (a * a) / mult, 0.0)
        d_r = d_log_a * (-LRU_C * sp_)
        d_sp = jnp.sum(d_log_a * (-LRU_C * r), axis=0, keepdims=True)
        d_zr = d_r * r * (1.0 - r)
        d_zi = d_ig * ig * (1.0 - ig)
        d_zr_bf = d_zr.astype(BF16)
        d_zi_bf = d_zi.astype(BF16)
        d_xc = d_xc + _gate_matmul_t(d_zr_bf, wa_ref) + _gate_matmul_t(d_zi_bf, wx_ref)
        for blk in range(2):
            cols = slice(256 * blk, 256 * (blk + 1))
            gw_ref[0, blk] = gw_ref[0, blk] + _dot_tn(xc_bf[:, cols], d_zr_bf[:, cols])
            gw_ref[1, blk] = gw_ref[1, blk] + _dot_tn(xc_bf[:, cols], d_zi_bf[:, cols])

        dxe[0:tm, :] = d_xc
        shifted = [dxe[pl.ds(3 - k, tm), :] for k in range(CONV_WIDTH)]
        dxr = sum(cw_ref[k : k + 1, :] * shifted[k] for k in range(CONV_WIDTH))
        dxr_ref[...] = dxr.astype(BF16)
        dxe[tm:, :] = d_xc[0:8, :]

        lam = lam_ref[...]
        xr_tile = xr_ref[...]
        sums = [
            jnp.sum(d_zr, axis=0, keepdims=True),
            jnp.sum(d_zi, axis=0, keepdims=True),
            d_sp * (-_sigmoid(-lam)),
            jnp.sum(d_xc, axis=0, keepdims=True),
        ] + [jnp.sum(xr_tile * shifted[k], axis=0, keepdims=True) for k in range(CONV_WIDTH)]
        gvec_ref[...] = gvec_ref[...] + jnp.concatenate(sums, axis=0)

    return pl.pallas_call(
        body,
        name="rnn_bwd",
        grid=(2, n_tiles),
        in_specs=[
            sp["proj_tile"](SEG_XR // HALF), sp["before"](SEG_XR // HALF), sp["tile"], sp["before"](0), sp["tile"], sp["keep"],
            sp["gates"](0), sp["gates"](1), sp["vec"], sp["vec"], sp["vec"], sp["conv"], sp["vec"], pl.BlockSpec(memory_space=pl.ANY),
        ],
        out_specs=[sp["tile"], sp["gate_grads"], pl.BlockSpec((8, HALF), lambda ch, i: (0, ch)), pl.BlockSpec(memory_space=pl.ANY)],
        out_shape=[
            jax.ShapeDtypeStruct((t, D), BF16),
            jax.ShapeDtypeStruct((2, 4, 256, 256), F32),
            jax.ShapeDtypeStruct((8, D), F32),
            jax.ShapeDtypeStruct((len(CHIP_FLIPS), 3, ROW_SHARD, D), BF16),
        ],
        scratch_shapes=[
            pltpu.VMEM((tm + 8, HALF), F32),
            pltpu.VMEM((tm + 8, HALF), F32),
            pltpu.VMEM((tm + 8, HALF), F32),
            pltpu.VMEM((tm, HALF), F32),
            pltpu.VMEM((tm, HALF), F32),
            pltpu.VMEM((tm + 8, HALF), F32),
            pltpu.VMEM((8, HALF), F32),
            pltpu.VMEM((8, HALF), F32),
            pltpu.SemaphoreType.DMA((len(CHIP_FLIPS),)),
            pltpu.SemaphoreType.DMA((len(CHIP_FLIPS),)),
        ],
        compiler_params=_params(2),
    )(proj, proj, hh, hh, d_hh, keep, w_all, w_all, rg_ba, rg_bx, rg_lambda, conv_w, conv_b, sum_3_bf)


def _mid(x, target, o, hh, proj, w_all, gate, final_g):
    t = x.shape[0]
    tm = min(256, t)
    n_tiles = t // tm
    full = pl.BlockSpec((tm, D), lambda i: (i, 0))
    half = lambda col: pl.BlockSpec((tm, 512), lambda i: (i, col))
    vec = pl.BlockSpec((1, D), lambda i: (0, 0))
    any_spec = pl.BlockSpec(memory_space=pl.ANY)

    def body(x_ref, tg_ref, o_ref, hh_ref, ga0, ga1, gr0, gr1, ma0, ma1, mr0, mr1, w_hbm, gate_ref, fg_ref,
             dres_ref, do_ref, dhh_ref, dga_ref, dgr_ref, dma_ref, dmr_ref, gw_hbm, vec_ref, w, gw, sem, w_sems):
        i = pl.program_id(0)

        @pl.when(i == 0)
        def _():
            loads = [
                pltpu.make_async_copy(
                    w_hbm.at[s, pl.ds(ROW_SHARD * k, ROW_SHARD), :], w.at[k, pl.ds(ROW_SHARD * s, ROW_SHARD), :], w_sems.at[k * N_DEV + s]
                )
                for k in range(3)
                for s in range(N_DEV)
            ]
            for cp in loads:
                cp.start()
            for cp in loads:
                cp.wait()
            gw[...] = jnp.zeros_like(gw)
            vec_ref[...] = jnp.zeros_like(vec_ref)

        cat = lambda a, b: jnp.concatenate([a[...], b[...]], axis=1)
        ga, gr, ma, mr = cat(ga0, ga1), cat(gr0, gr1), cat(ma0, ma1), cat(mr0, mr1)
        ov, hv, gate_v, fg = o_ref[...].astype(F32), hh_ref[...], gate_ref[...], fg_ref[...]
        sg_a = _sigmoid(ga)
        sg_r = _sigmoid(gr)
        silu_a = ga * sg_a
        silu_r = gr * sg_r
        ya = (ov * silu_a).astype(BF16)
        yr = (hv * silu_r).astype(BF16)
        pa = _dot(ya, w[0])
        pr = _dot(yr, w[1])
        sa = _sigmoid(ma)
        sr = _sigmoid(mr)
        merged = (sa * pa + sr * pr).astype(BF16)
        mo = _dot(merged, w[2])
        out = x_ref[...] + gate_v * mo
        r2 = lax.rsqrt(jnp.mean(out * out, axis=-1, keepdims=True) + NORM_EPS)
        outn = out * r2
        err = outn * fg - tg_ref[...]
        loss = 0.5 * jnp.sum(jnp.mean(err * err, axis=-1, keepdims=True), axis=0, keepdims=True)
        dy = err * (1.0 / D)
        g_final = jnp.sum(dy * outn, axis=0, keepdims=True)
        doutn = dy * fg
        dout = r2 * (doutn - outn * jnp.mean(doutn * outn, axis=-1, keepdims=True))
        dres_ref[...] = dout
        d_gate = jnp.sum(dout * mo, axis=0, keepdims=True)
        dmo = (dout * gate_v).astype(BF16)
        dmerged = _dot_nt(dmo, w[2])
        gw[2] = gw[2] + _dot_tn(merged, dmo)
        dpa = (dmerged * sa).astype(BF16)
        dpr = (dmerged * sr).astype(BF16)
        dma_ref[...] = (dmerged * pa * (sa * (1.0 - sa))).astype(BF16)
        dmr_ref[...] = (dmerged * pr * (sr * (1.0 - sr))).astype(BF16)
        dya = _dot_nt(dpa, w[0])
        dyr = _dot_nt(dpr, w[1])
        gw[0] = gw[0] + _dot_tn(ya, dpa)
        gw[1] = gw[1] + _dot_tn(yr, dpr)
        do_ref[...] = (dya * silu_a).astype(BF16)
        dhh_ref[...] = dyr * silu_r
        dga_ref[...] = (dya * ov * (sg_a * (1.0 + ga * (1.0 - sg_a)))).astype(BF16)
        dgr_ref[...] = (dyr * hv * (sg_r * (1.0 + gr * (1.0 - sg_r)))).astype(BF16)
        zeros = jnp.zeros((5, D), F32)
        vec_ref[...] = vec_ref[...] + jnp.concatenate([g_final, d_gate, jnp.broadcast_to(loss, (1, D)), zeros], axis=0)

        @pl.when(i == n_tiles - 1)
        def _():
            cp = pltpu.make_async_copy(gw, gw_hbm, sem)
            cp.start()
            cp.wait()

    bf_tile = jax.ShapeDtypeStruct((t, D), BF16)
    f_tile = jax.ShapeDtypeStruct((t, D), F32)
    return pl.pallas_call(
        body,
        name="mid",
        grid=(n_tiles,),
        in_specs=[full, full, full, full,
                  half(SEG_GA // 512), half(SEG_GA // 512 + 1), half(SEG_GR // 512), half(SEG_GR // 512 + 1),
                  half(SEG_MA // 512), half(SEG_MA // 512 + 1), half(SEG_MR // 512), half(SEG_MR // 512 + 1),
                  any_spec, vec, vec],
        out_specs=[full, full, full, full, full, full, full, any_spec, pl.BlockSpec((8, D), lambda i: (0, 0))],
        out_shape=[f_tile, bf_tile, f_tile, bf_tile, bf_tile, bf_tile, bf_tile,
                   jax.ShapeDtypeStruct((3, D, D), F32), jax.ShapeDtypeStruct((8, D), F32)],
        scratch_shapes=[pltpu.VMEM((3, D, D), BF16), pltpu.VMEM((3, D, D), F32), pltpu.SemaphoreType.DMA, pltpu.SemaphoreType.DMA((3 * N_DEV,))],
        compiler_params=_params(1),
    )(x, target, o, hh, proj, proj, proj, proj, proj, proj, proj, proj, w_all, gate, final_g)


_SEGMENTS = ((SEG_Q, 1024), (SEG_K, 256), (SEG_V, 256), (SEG_GA, 1024), (SEG_XR, 1024), (SEG_GR, 1024), (SEG_MA, 1024), (SEG_MR, 1024))


def _seg_specs(tm):
    return [pl.BlockSpec((tm, width), lambda i: (i, 0)) for _, width in _SEGMENTS]


def _bwd_dh(dsegs, w_all, x, dres, mod, norm_g, sum_w_bf):
    t = x.shape[0]
    tm = min(512, t)
    n_tiles = t // tm
    n_seg = len(_SEGMENTS)
    full = pl.BlockSpec((tm, D), lambda i: (i, 0))
    views = [lambda ref, peer: ref.at[_chip_of(peer)]]

    def body(*refs):
        segs = refs[:n_seg]
        w_hbm, x_ref, dres_ref, mod_ref, ng_ref, sum_hbm, gx_ref, vec_ref, land_hbm, w, sems, send_sems, recv_sems = refs[n_seg:]
        i = pl.program_id(0)

        @pl.when(i == 0)
        def _():
            for cp in _chip_copies([sum_hbm], [land_hbm], views, send_sems, recv_sems):
                cp.start()
            loads = [
                pltpu.make_async_copy(w_hbm.at[s], w.at[pl.ds(W_IN_SHARD * s, W_IN_SHARD), :], sems.at[s])
                for s in range(N_DEV)
            ]
            for cp in loads:
                cp.start()
            for cp in loads:
                cp.wait()
            vec_ref[...] = jnp.zeros_like(vec_ref)

        dh = sum(_dot(seg[...], w[start : start + width, :]) for seg, (start, width) in zip(segs, _SEGMENTS))
        xv = x_ref[...]
        r1 = lax.rsqrt(jnp.mean(xv * xv, axis=-1, keepdims=True) + NORM_EPS)
        xn = xv * r1
        ng = ng_ref[...]
        scale1 = 1.0 + mod_ref[:, D : 2 * D]
        d_shift = jnp.sum(dh, axis=0, keepdims=True)
        d_scale = jnp.sum(dh * (xn * ng), axis=0, keepdims=True)
        g_norm = jnp.sum(dh * xn * scale1, axis=0, keepdims=True)
        dxn = dh * (ng * scale1)
        dx = r1 * (dxn - xn * jnp.mean(dxn * xn, axis=-1, keepdims=True))
        gx_ref[...] = dres_ref[...] + dx
        vec_ref[...] = vec_ref[...] + jnp.concatenate([d_shift, d_scale, g_norm, jnp.zeros((5, D), F32)], axis=0)

        @pl.when(i == n_tiles - 1)
        def _():
            for cp in _chip_copies([sum_hbm], [land_hbm], views, send_sems, recv_sems):
                cp.wait_recv()
                cp.wait_send()

    any_spec = pl.BlockSpec(memory_space=pl.ANY)
    n_copies = len(CHIP_FLIPS)
    return pl.pallas_call(
        body,
        name="bwd_dh",
        grid=(n_tiles,),
        in_specs=_seg_specs(tm) + [any_spec, full, full, pl.BlockSpec((1, 3 * D), lambda i: (0, 0)), pl.BlockSpec((1, D), lambda i: (0, 0)),
                                   any_spec],
        out_specs=[full, pl.BlockSpec((8, D), lambda i: (0, 0)), any_spec],
        out_shape=[jax.ShapeDtypeStruct((t, D), F32), jax.ShapeDtypeStruct((8, D), F32),
                   jax.ShapeDtypeStruct((n_copies, W_IN_SHARD, D), BF16)],
        scratch_shapes=[pltpu.VMEM((IN_W, D), BF16), pltpu.SemaphoreType.DMA((N_DEV,)),
                        pltpu.SemaphoreType.DMA((n_copies,)), pltpu.SemaphoreType.DMA((n_copies,))],
        compiler_params=_params(1),
    )(*dsegs, w_all, x, dres, mod, norm_g, sum_w_bf)


def _bwd_gw(dsegs, h, sum_g_bf):
    t = h.shape[0]
    tm = min(512, t)
    n_tiles = t // tm
    n_seg = len(_SEGMENTS)
    n_chips = N_DEV // 2
    views = [lambda ref, peer: ref.at[:, :, _chip_of(peer)]]

    def body(*refs):
        segs = refs[:n_seg]
        (h_ref, sg_hbm, own_hbm, recv_hbm, lg_hbm, acc, stage, own_sems, pair_send, pair_recv, send_sems, recv_sems) = refs[n_seg:]
        i = pl.program_id(0)

        @pl.when(i == 0)
        def _():
            for cp in _chip_copies([sg_hbm], [lg_hbm], views, send_sems, recv_sems):
                cp.start()
            acc[...] = jnp.zeros_like(acc)

        hv = h_ref[...]
        for seg, (start, width) in zip(segs, _SEGMENTS):
            acc[start : start + width, :] = acc[start : start + width, :] + _dot_tn(seg[...], hv)

        @pl.when(i == n_tiles - 1)
        def _():
            me = _position()
            sibling = _flip(me, 1)

            def rows(q, c):
                return pl.ds(pl.multiple_of(W_IN_SHARD * (2 * q + c), 8), W_IN_SHARD)

            def to_sibling(q):
                return pltpu.make_async_remote_copy(
                    src_ref=stage.at[q % 2], dst_ref=recv_hbm.at[q], send_sem=pair_send.at[q], recv_sem=pair_recv.at[q],
                    device_id=sibling, device_id_type=MESH,
                )

            kept = [pltpu.make_async_copy(acc.at[rows(q, me[2]), :], own_hbm.at[q], own_sems.at[q]) for q in range(n_chips)]
            for cp in kept:
                cp.start()
            for q in range(n_chips):
                if q >= 2:
                    to_sibling(q - 2).wait_send()
                stage[q % 2] = acc[rows(q, sibling[2]), :].astype(BF16)
                to_sibling(q).start()
            for q in range(n_chips - 2, n_chips):
                to_sibling(q).wait_send()
            for q in range(n_chips):
                to_sibling(q).wait_recv()
            for cp in kept:
                cp.wait()
            for cp in _chip_copies([sg_hbm], [lg_hbm], views, send_sems, recv_sems):
                cp.wait_recv()
                cp.wait_send()

    any_spec = pl.BlockSpec(memory_space=pl.ANY)
    n_copies = len(CHIP_FLIPS)
    return pl.pallas_call(
        body,
        name="bwd_gw",
        grid=(n_tiles,),
        in_specs=_seg_specs(tm) + [pl.BlockSpec((tm, D), lambda i: (i, 0)), any_spec],
        out_specs=[any_spec, any_spec, any_spec],
        out_shape=[jax.ShapeDtypeStruct((n_chips, W_IN_SHARD, D), F32),
                   jax.ShapeDtypeStruct((n_chips, W_IN_SHARD, D), BF16),
                   jax.ShapeDtypeStruct((len(CHIP_FLIPS), 2, 4, GATE_SHARD, 256), BF16)],
        scratch_shapes=[pltpu.VMEM((IN_W, D), F32), pltpu.VMEM((2, W_IN_SHARD, D), BF16),
                        pltpu.SemaphoreType.DMA((n_chips,)), pltpu.SemaphoreType.DMA((n_chips,)), pltpu.SemaphoreType.DMA((n_chips,)),
                        pltpu.SemaphoreType.DMA((n_copies,)), pltpu.SemaphoreType.DMA((n_copies,))],
        compiler_params=_params(1),
    )(*dsegs, h, sum_g_bf)


def _adamw(w, g, m, v):
    m = ADAM_B1 * m + (1.0 - ADAM_B1) * g
    v = ADAM_B2 * v + (1.0 - ADAM_B2) * (g * g)
    m_hat = m / (1.0 - ADAM_B1**ADAM_STEP)
    v_hat = v / (1.0 - ADAM_B2**ADAM_STEP)
    delta = -ADAM_LR * (m_hat / (jnp.sqrt(v_hat) + ADAM_EPS) + ADAM_WD * w)
    return delta, m, v


def _w_in_update(own, land, w_t, m_t, v_t):
    n_slots, rows, cols = land.shape
    rb = 208
    spec = pl.BlockSpec((rb, cols), lambda i: (i, 0))

    def body(o_ref, l_ref, w_ref, m_ref, v_ref, g_ref, d_ref, m2_ref, v2_ref):
        total = o_ref[...]
        for s in range(n_slots):
            total = total + l_ref[s].astype(F32)
        g_ref[...] = total
        d_ref[...], m2_ref[...], v2_ref[...] = _adamw(w_ref[...], total, m_ref[...], v_ref[...])

    shape = jax.ShapeDtypeStruct((rows, cols), F32)
    return pl.pallas_call(
        body,
        name="w_in_update",
        grid=(rows // rb,),
        in_specs=[spec, pl.BlockSpec((n_slots, rb, cols), lambda i: (0, i, 0)), spec, spec, spec],
        out_specs=[spec] * 4,
        out_shape=[shape] * 4,
        compiler_params=_params(1),
    )(own, land, w_t, m_t, v_t)


def _shard_update(own_3, land_3, own_g, land_g, weights, m_in, v_in):
    n = len(weights)

    def body(*refs):
        o3_ref, l3_ref, og_ref, lg_ref = refs[:4]
        w_refs, m_refs, v_refs = (refs[4 + n * k : 4 + n * (k + 1)] for k in range(3))
        outs = refs[4 + 3 * n :]
        for k in range(n):
            if k < 3:
                g = o3_ref[k]
                for slot in range(3):
                    g = g + l3_ref[slot, k].astype(F32)
            else:
                g = og_ref[k - 3]
                for slot in range(3):
                    g = g + lg_ref[slot, k - 3].astype(F32)
            d, m2, v2 = _adamw(w_refs[k][0], g, m_refs[k][0], v_refs[k][0])
            for which, val in enumerate((g, d, m2, v2)):
                outs[which * n + k][0] = val

    shapes = [jax.ShapeDtypeStruct(w.shape, F32) for w in weights]
    res = pl.pallas_call(
        body, name="shard_update", out_shape=shapes * 4, compiler_params=pltpu.CompilerParams(vmem_limit_bytes=VMEM_LIMIT)
    )(own_3, land_3, own_g, land_g, *weights, *m_in, *v_in)
    return [res[n * k : n * (k + 1)] for k in range(4)]


VEC_LAYOUT = (("b_ada", 3 * D), ("norm_g", D), ("attn_sinks", 128), ("conv_b", D), ("rg_ba", D), ("rg_bx", D), ("rg_lambda", D),
              ("final_g", D), ("conv_w", CONV_WIDTH * D), ("loss", 128))
VEC_OFFSET = {name: sum(w for _, w in VEC_LAYOUT[:k]) for k, (name, _) in enumerate(VEC_LAYOUT)}
VEC_ROWS = 112
VEC_WIDTH = VEC_ROWS * 128
assert sum(w for _, w in VEC_LAYOUT) <= VEC_WIDTH
N_REPLICATED = 8


def _small_update(vec_all, c_all_t, me, rep_w, rep_m, rep_v, ada, conv):
    n_ada = ada[0].shape[1]
    n_conv = conv[0].shape[1]

    def body(*refs):
        me_ref, vec_ref, ct_ref = refs[:3]
        w_refs, m_refs, v_refs = (refs[3 + N_REPLICATED * k : 3 + N_REPLICATED * (k + 1)] for k in range(3))
        pos = 3 + 3 * N_REPLICATED
        ada_refs, conv_refs = refs[pos : pos + 3], refs[pos + 3 : pos + 6]
        outs = refs[pos + 6 :]
        loss_ref, rep_outs = outs[0], outs[1 : 1 + 4 * N_REPLICATED]
        ada_outs, conv_outs = outs[1 + 4 * N_REPLICATED : 5 + 4 * N_REPLICATED], outs[5 + 4 * N_REPLICATED : 9 + 4 * N_REPLICATED]
        conv_buf = outs[9 + 4 * N_REPLICATED]

        total = vec_ref[0:1, :]
        for slot in range(1, N_DEV):
            total = total + vec_ref[slot : slot + 1, :]
        loss_ref[...] = total[:, VEC_OFFSET["loss"] : VEC_OFFSET["loss"] + 1]
        for k, (name, _) in enumerate(VEC_LAYOUT[:N_REPLICATED]):
            width = w_refs[k].shape[1]
            g = total[:, VEC_OFFSET[name] : VEC_OFFSET[name] + width]
            d, m2, v2 = _adamw(w_refs[k][...], g, m_refs[k][...], v_refs[k][...])
            for which, val in enumerate((g, d, m2, v2)):
                rep_outs[which * N_REPLICATED + k][...] = val

        device = me_ref[0]
        for k in range(CONV_WIDTH):
            conv_buf[k : k + 1, :] = total[:, VEC_OFFSET["conv_w"] + D * k : VEC_OFFSET["conv_w"] + D * (k + 1)]
        g_conv = conv_buf[:, pl.ds(pl.multiple_of(device * n_conv, 128), n_conv)]
        for ref, val in zip(conv_outs, (g_conv,) + _adamw(conv_refs[0][...], g_conv, conv_refs[1][...], conv_refs[2][...])):
            ref[...] = val

        dmod = vec_ref[:, pl.ds(pl.multiple_of(device * n_ada, 128), n_ada)]
        g_ada = _dot(ct_ref[...], dmod)
        for ref, val in zip(ada_outs, (g_ada,) + _adamw(ada_refs[0][...], g_ada, ada_refs[1][...], ada_refs[2][...])):
            ref[...] = val

    vmem = pl.BlockSpec(memory_space=pltpu.VMEM)
    rep_shapes = [jax.ShapeDtypeStruct(w.shape, F32) for w in rep_w]
    out_shape = ([jax.ShapeDtypeStruct((1, 1), F32)] + rep_shapes * 4 + [jax.ShapeDtypeStruct(ada[0].shape, F32)] * 4
                 + [jax.ShapeDtypeStruct(conv[0].shape, F32)] * 4)
    n_in = 3 + 3 * N_REPLICATED + 6
    res = pl.pallas_call(
        body,
        name="small_update",
        in_specs=[pl.BlockSpec(memory_space=pltpu.SMEM)] + [vmem] * (n_in - 1),
        out_specs=[vmem] * len(out_shape),
        out_shape=out_shape,
        scratch_shapes=[pltpu.VMEM((CONV_WIDTH, D), F32)],
        compiler_params=pltpu.CompilerParams(vmem_limit_bytes=VMEM_LIMIT),
    )(me, vec_all, c_all_t, *rep_w, *rep_m, *rep_v, *ada, *conv)
    loss = res[0]
    reps = [res[1 + N_REPLICATED * k : 1 + N_REPLICATED * (k + 1)] for k in range(4)]
    return loss, reps, res[1 + 4 * N_REPLICATED : 5 + 4 * N_REPLICATED], res[5 + 4 * N_REPLICATED : 9 + 4 * N_REPLICATED]


def kernel(x, c, positions, w_ada, b_ada, norm_g, w_in, attn_sinks, conv_w, conv_b, rg_wa, rg_ba, rg_wx, rg_bx, rg_lambda, w_attn_proj, w_rnn_proj, w_out, final_g, loss_target, m_w_ada, m_b_ada, m_norm_g, m_w_in, m_attn_sinks, m_conv_w, m_conv_b, m_rg_wa, m_rg_ba, m_rg_wx, m_rg_bx, m_rg_lambda, m_w_attn_proj, m_w_rnn_proj, m_w_out, m_final_g, v_w_ada, v_b_ada, v_norm_g, v_w_in, v_attn_sinks, v_conv_w, v_conv_b, v_rg_wa, v_rg_ba, v_rg_wx, v_rg_bx, v_rg_lambda, v_w_attn_proj, v_w_rnn_proj, v_w_out, v_final_g):
    my_x, my_y, my_c = _position()
    me = _slot((my_x, my_y, my_c))
    x2, target = x[0], loss_target[0]
    pos = positions[0]

    gate_rows = lambda w: w[0].transpose(1, 0, 2).reshape(GATE_SHARD, D)
    w_pack = jnp.concatenate([w_attn_proj[0], w_rnn_proj[0], w_out[0], gate_rows(rg_wa), gate_rows(rg_wx)], axis=0).astype(BF16)
    conv_rows = jnp.pad(conv_w[0], ((0, 8 - CONV_WIDTH), (0, 0)))
    w_in_t = w_in[0].T
    n_ada = w_ada.shape[2]
    b_cols = lax.dynamic_slice(b_ada, (0, me * n_ada), (1, n_ada))

    tables = _rope_tables(pos)
    where = jnp.stack([_chip_of((my_x, my_y)), me]).astype(jnp.int32)
    proj, h, mod, w_t_all, w_all, c_rows, conv_all = _fwd_in(
        x2, jnp.broadcast_to(c, (8, D)), conv_rows, w_ada[0], b_cols, norm_g, w_in_t.astype(BF16), tables[0], w_pack, where
    )
    c_all = c_rows[:, 0, :]
    conv_full = conv_all[:, :CONV_WIDTH, :].transpose(1, 0, 2).reshape(CONV_WIDTH, D)
    gate = mod[:, 2 * D :]
    o = _attn_fwd(proj, attn_sinks)
    keep = (pos != 0).astype(F32)[:, None]
    hh = _rnn_fwd(proj, keep, w_all, rg_ba, rg_bx, rg_lambda, conv_full, conv_b)

    dres, d_o, d_hh, d_ga, d_gr, d_ma, d_mr, gw3, mid_vec = _mid(x2, target, o, hh, proj, w_all, gate, final_g[None, :])
    index = jnp.stack([my_c, _chip_of((my_x, my_y))]).astype(jnp.int32)
    part_3 = gw3.reshape(3, 4, 2, ROW_SHARD, D)
    d_q, d_k, d_v, d_sink, recv_3 = _attn_bwd(proj, o, d_o, tables, attn_sinks, part_3)
    sum_3_bf, own_3 = _pair_add(
        "pair_add_3", part_3, recv_3, index, (4,), 0,
        pl.BlockSpec((3, None, None, ROW_SHARD, D), lambda q, idx: (0, q, idx[0], 0, 0)),
        pl.BlockSpec((3, None, ROW_SHARD, D), lambda q, idx: (0, q, 0, 0)),
        pl.BlockSpec((3, ROW_SHARD, D), lambda q, idx: (0, 0, 0)), (3, ROW_SHARD, D),
    )
    d_xr, g_gates, rnn_vec, land_3 = _rnn_bwd(proj, hh, d_hh, keep, w_all, rg_ba, rg_bx, rg_lambda, conv_full, conv_b, sum_3_bf)
    dsegs = (d_q, d_k, d_v, d_ga, d_xr, d_gr, d_ma, d_mr)
    part_g = g_gates.reshape(2, 4, 4, 2, GATE_SHARD, 256)
    (recv_g,) = _exchange("rs_pair_g", [part_g], [lambda ref, peer: ref.at[:, :, :, peer[2]]], [(2, 4, 4, GATE_SHARD, 256)], flips=(1,))
    sum_g_bf, own_g = _pair_add(
        "pair_add_g", part_g, recv_g[0], index, (4,), 0,
        pl.BlockSpec((2, 4, None, None, GATE_SHARD, 256), lambda q, idx: (0, 0, q, idx[0], 0, 0)),
        pl.BlockSpec((2, 4, None, GATE_SHARD, 256), lambda q, idx: (0, 0, q, 0, 0)),
        pl.BlockSpec((2, 4, GATE_SHARD, 256), lambda q, idx: (0, 0, 0, 0)), (2, 4, GATE_SHARD, 256),
    )
    part_w, recv_w, land_g = _bwd_gw(dsegs, h, sum_g_bf)
    w_rows = pl.BlockSpec((None, W_IN_SHARD, D), lambda q, idx: (q, 0, 0))
    sum_w_bf, own_w = _pair_add(
        "pair_add_w", part_w, recv_w, index, (4,), 0, w_rows, w_rows,
        pl.BlockSpec((W_IN_SHARD, D), lambda q, idx: (0, 0)), (W_IN_SHARD, D),
    )
    grad_x, dh_vec, land_w = _bwd_dh(dsegs, w_t_all, x2, dres, mod, norm_g, sum_w_bf)

    pieces = dict(
        b_ada=jnp.concatenate([dh_vec[0:1], dh_vec[1:2], mid_vec[1:2]], axis=1),
        norm_g=dh_vec[2:3],
        attn_sinks=jnp.pad(d_sink[:, :, 0].reshape(1, 16), ((0, 0), (0, 112))),
        conv_b=rnn_vec[3:4],
        rg_ba=rnn_vec[0:1],
        rg_bx=rnn_vec[1:2],
        rg_lambda=rnn_vec[2:3],
        final_g=mid_vec[0:1],
        conv_w=rnn_vec[4:8].reshape(1, CONV_WIDTH * D),
        loss=mid_vec[2:3, 0:128],
    )
    vec = jnp.concatenate([pieces[name] for name, _ in VEC_LAYOUT], axis=1)
    vec = jnp.pad(vec, ((0, 0), (0, VEC_WIDTH - vec.shape[1]))).reshape(VEC_ROWS, 128)
    (vec_all,) = _all_gather_direct("ag_vec", [vec])
    rep_w = [b_ada, norm_g, attn_sinks, conv_b, rg_ba, rg_bx, rg_lambda, final_g[None, :]]
    rep_m = [m_b_ada, m_norm_g, m_attn_sinks, m_conv_b, m_rg_ba, m_rg_bx, m_rg_lambda, m_final_g[None, :]]
    rep_v = [v_b_ada, v_norm_g, v_attn_sinks, v_conv_b, v_rg_ba, v_rg_bx, v_rg_lambda, v_final_g[None, :]]
    loss, reps, ada_res, conv_res = _small_update(
        vec_all.reshape(N_DEV, VEC_WIDTH), c_all.T, me.astype(jnp.int32).reshape(1), rep_w, rep_m, rep_v,
        (w_ada[0], m_w_ada[0], v_w_ada[0]), (conv_w[0], m_conv_w[0], v_conv_w[0]),
    )

    w_in_res = _w_in_update(own_w, land_w, w_in_t, m_w_in[0].T, v_w_in[0].T)
    shard_res = _shard_update(
        own_3, land_3, own_g, land_g,
        (w_attn_proj, w_rnn_proj, w_out, rg_wa, rg_wx),
        (m_w_attn_proj, m_w_rnn_proj, m_w_out, m_rg_wa, m_rg_wx),
        (v_w_attn_proj, v_w_rnn_proj, v_w_out, v_rg_wa, v_rg_wx),
    )

    def by_weight(k):
        b, n, s, cb, ba, bx, lam, fg = reps[k]
        ap, rp, out, wa, wx = shard_res[k]
        return [ada_res[k][None], b, n, w_in_res[k].T[None], s, conv_res[k][None], cb, wa, ba, wx, bx, lam, ap, rp, out, fg[0]]

    return (loss[0, 0], grad_x[None], *by_weight(0), *by_weight(1), *by_weight(2), *by_weight(3))
```

```python
import math

import jax
import jax.numpy as jnp
from jax import lax
from jax.experimental import pallas as pl
from jax.experimental.pallas import tpu as pltpu

F32 = jnp.float32
BF16 = jnp.bfloat16

D = 1024
HEAD_DIM = 64
ROT_DIM = 16
ROPE_THETA = 500000.0
WINDOW = 128
LRU_C = 8.0
CONV_WIDTH = 4
NORM_EPS = 1e-6
IN_W = 6656
SEG_Q, SEG_K, SEG_V, SEG_GA, SEG_XR, SEG_GR, SEG_MA, SEG_MR = 0, 1024, 1280, 1536, 2560, 3584, 4608, 5632
N_DEV = 8
W_IN_SHARD = IN_W // N_DEV
ROW_SHARD = D // N_DEV
GATE_SHARD = 256 // N_DEV

ADAM_LR = 0.001
ADAM_B1 = 0.9
ADAM_B2 = 0.999
ADAM_EPS = 1e-08
ADAM_WD = 0.01
ADAM_STEP = 10

VMEM_LIMIT = 60 * 1024 * 1024
MESH = pl.DeviceIdType.MESH


def _dot(a, b):
    return jnp.dot(a, b, preferred_element_type=F32)


def _dot_nt(a, b):
    return lax.dot_general(a, b, (((1,), (1,)), ((), ())), preferred_element_type=F32)


def _dot_tn(a, b):
    return lax.dot_general(a, b, (((0,), (0,)), ((), ())), preferred_element_type=F32)


def _sigmoid(z):
    return 1.0 / (1.0 + jnp.exp(-z))


def _params(n_grid, vmem=VMEM_LIMIT):
    return pltpu.CompilerParams(dimension_semantics=("arbitrary",) * n_grid, vmem_limit_bytes=vmem)


def _position():
    return lax.axis_index("x"), lax.axis_index("y"), lax.axis_index("c")


def _flip(pos, k):
    x, y, c = pos
    return (1 - x if k & 4 else x, 1 - y if k & 2 else y, 1 - c if k & 1 else c)


def _slot(pos):
    return 4 * pos[0] + 2 * pos[1] + pos[2]


def _all_gather_direct(name, arrs):
    n = len(arrs)

    def body(*refs):
        ins, outs = refs[:n], refs[n : 2 * n]
        send_sems, recv_sems, local_sems = refs[2 * n :]
        me = _position()
        mine = [pltpu.make_async_copy(ins[a], outs[a].at[_slot(me)], local_sems.at[a]) for a in range(n)]
        for cp in mine:
            cp.start()

        def copy(a, k, block):
            return pltpu.make_async_remote_copy(
                src_ref=ins[a],
                dst_ref=outs[a].at[_slot(block)],
                send_sem=send_sems.at[a * 7 + k - 1],
                recv_sem=recv_sems.at[a * 7 + k - 1],
                device_id=_flip(me, k),
                device_id_type=MESH,
            )

        sent = [copy(a, k, me) for a in range(n) for k in range(1, N_DEV)]
        for cp in sent:
            cp.start()
        for a in range(n):
            for k in range(1, N_DEV):
                copy(a, k, _flip(me, k)).wait_recv()
        for cp in sent:
            cp.wait_send()
        for cp in mine:
            cp.wait()

    any_spec = pl.BlockSpec(memory_space=pl.ANY)
    return pl.pallas_call(
        body,
        name=name,
        out_shape=[jax.ShapeDtypeStruct((N_DEV,) + a.shape, a.dtype) for a in arrs],
        in_specs=[any_spec] * n,
        out_specs=[any_spec] * n,
        scratch_shapes=[
            pltpu.SemaphoreType.DMA((7 * n,)),
            pltpu.SemaphoreType.DMA((7 * n,)),
            pltpu.SemaphoreType.DMA((n,)),
        ],
    )(*arrs)


def _direct_gather_copies(src_hbm, all_hbm, send_sems, recv_sems, arriving):
    me = _position()
    return [
        pltpu.make_async_remote_copy(
            src_ref=src_hbm,
            dst_ref=all_hbm.at[_slot(_flip(me, k) if arriving else me)],
            send_sem=send_sems.at[k - 1],
            recv_sem=recv_sems.at[k - 1],
            device_id=_flip(me, k),
            device_id_type=MESH,
        )
        for k in range(1, N_DEV)
    ]


CHIP_FLIPS = (2, 4, 6)


def _chip_of(pos):
    return 2 * pos[0] + pos[1]


def _chip_copies(srcs, lands, views, send_sems, recv_sems):
    me = _position()
    copies = []
    for a, (src, land) in enumerate(zip(srcs, lands)):
        for j, k in enumerate(CHIP_FLIPS):
            peer = _flip(me, k)
            copies.append(
                pltpu.make_async_remote_copy(
                    src_ref=views[a](src, peer),
                    dst_ref=land.at[j],
                    send_sem=send_sems.at[len(CHIP_FLIPS) * a + j],
                    recv_sem=recv_sems.at[len(CHIP_FLIPS) * a + j],
                    device_id=peer,
                    device_id_type=MESH,
                )
            )
    return copies


def _exchange(name, arrs, views, out_shapes, flips):
    n = len(arrs)
    nf = len(flips)

    def body(*refs):
        ins, outs = refs[:n], refs[n : 2 * n]
        send_sems, recv_sems = refs[2 * n :]
        me = _position()

        def copy(a, j):
            peer = _flip(me, flips[j])
            return pltpu.make_async_remote_copy(
                src_ref=views[a](ins[a], peer),
                dst_ref=outs[a].at[j],
                send_sem=send_sems.at[a * nf + j],
                recv_sem=recv_sems.at[a * nf + j],
                device_id=peer,
                device_id_type=MESH,
            )

        sent = [copy(a, j) for a in range(n) for j in range(nf)]
        for cp in sent:
            cp.start()
        for cp in sent:
            cp.wait_recv()
        for cp in sent:
            cp.wait_send()

    any_spec = pl.BlockSpec(memory_space=pl.ANY)
    return pl.pallas_call(
        body,
        name=name,
        out_shape=[jax.ShapeDtypeStruct((nf,) + tuple(sh), a.dtype) for sh, a in zip(out_shapes, arrs)],
        in_specs=[any_spec] * n,
        out_specs=[any_spec] * n,
        scratch_shapes=[pltpu.SemaphoreType.DMA((nf * n,)), pltpu.SemaphoreType.DMA((nf * n,))],
    )(*arrs)


def _pair_add(name, own, recv, index, grid, chip_axis, own_spec, recv_spec, mine_spec, mine_shape):
    def body(idx_ref, own_ref, recv_ref, sb_ref, mine_ref):
        total = own_ref[...] + recv_ref[...].astype(F32)
        sb_ref[...] = total.astype(BF16)

        @pl.when(pl.program_id(chip_axis) == idx_ref[1])
        def _():
            mine_ref[...] = total

    return pl.pallas_call(
        body,
        name=name,
        grid_spec=pltpu.PrefetchScalarGridSpec(
            num_scalar_prefetch=1, grid=grid, in_specs=[own_spec, recv_spec], out_specs=[recv_spec, mine_spec]
        ),
        out_shape=[jax.ShapeDtypeStruct(recv.shape, BF16), jax.ShapeDtypeStruct(mine_shape, F32)],
        compiler_params=_params(len(grid)),
    )(index, own, recv)


PACK_GATE = 3 * ROW_SHARD
PACK_ROWS = PACK_GATE + 2 * GATE_SHARD


def _fwd_in(x, c_rows, conv_rows, w_ada, b_cols, norm_g, w_shard, tables, pack, where):
    t = x.shape[0]
    tm = min(1024, t)
    tn = IN_W // 4
    n_i, n_j = t // tm, IN_W // tn
    n_rope = (SEG_V - SEG_Q) // 128

    n_ada = w_ada.shape[1]

    def body(where_ref, x_ref, wada_ref, bcols_ref, ng_ref, cos_ref, s1_ref, s2_ref, shard_hbm, pack_hbm, c_hbm, conv_hbm,
             proj_ref, h_ref, mod_ref, wt_hbm, all_hbm, call_hbm, convall_hbm,
             wall, hbuf, cbuf, mcols, mod_all, mod_buf,
             w_send, w_recv, load_sems, local_sem, p_send, p_recv, p_local,
             c_send, c_recv, c_local, v_send, v_recv, v_local, m_send, m_recv):
        j = pl.program_id(0)
        i = pl.program_id(1)
        me = _position()
        sibling = _flip(me, 1)
        chip = where_ref[0]
        slot = where_ref[1]

        def modulation():
            own_c = pltpu.make_async_copy(c_hbm, call_hbm.at[_slot(me)], c_local)
            own_v = pltpu.make_async_copy(conv_hbm, convall_hbm.at[_slot(me)], v_local)
            own_c.start()
            own_v.start()
            for cp in _direct_gather_copies(c_hbm, call_hbm, c_send, c_recv, arriving=False):
                cp.start()
            for cp in _direct_gather_copies(conv_hbm, convall_hbm, v_send, v_recv, arriving=False):
                cp.start()
            for cp in _direct_gather_copies(c_hbm, call_hbm, c_send, c_recv, arriving=True):
                cp.wait_recv()
            own_c.wait()
            fetch = pltpu.make_async_copy(call_hbm, cbuf, c_local)
            fetch.start()
            fetch.wait()
            mcols[...] = _dot(cbuf[:, 0, :].astype(BF16), wada_ref[...].astype(BF16)) + bcols_ref[...]
            mod_all[slot] = mcols[...]
            swaps = lambda arriving: [
                pltpu.make_async_remote_copy(
                    src_ref=mcols,
                    dst_ref=mod_all.at[_slot(_flip(me, k) if arriving else me)],
                    send_sem=m_send.at[k - 1],
                    recv_sem=m_recv.at[k - 1],
                    device_id=_flip(me, k),
                    device_id_type=MESH,
                )
                for k in range(1, N_DEV)
            ]
            for cp in swaps(False):
                cp.start()
            for cp in swaps(True):
                cp.wait_recv()
            mine = lax.broadcasted_iota(jnp.int32, (N_DEV, n_ada), 0) == slot
            for s in range(N_DEV):
                row = jnp.sum(jnp.where(mine, mod_all[s], 0.0), axis=0, keepdims=True)
                mod_buf[:, n_ada * s : n_ada * (s + 1)] = row
            for cp in swaps(False):
                cp.wait_send()
            for cp in _direct_gather_copies(c_hbm, call_hbm, c_send, c_recv, arriving=False):
                cp.wait_send()
            for cp in _direct_gather_copies(conv_hbm, convall_hbm, v_send, v_recv, arriving=True):
                cp.wait_recv()
            for cp in _direct_gather_copies(conv_hbm, convall_hbm, v_send, v_recv, arriving=False):
                cp.wait_send()
            own_v.wait()

        def shard_of(block):
            return wall.at[_chip_of(block), pl.ds(W_IN_SHARD * block[2], W_IN_SHARD), :]

        def shard_copy(k, block, to, src=None):
            return pltpu.make_async_remote_copy(
                src_ref=shard_of(block),
                dst_ref=shard_of(block),
                send_sem=w_send.at[k],
                recv_sem=w_recv.at[k],
                device_id=to,
                device_id_type=MESH,
            )

        own_shard = pltpu.make_async_copy(shard_hbm, shard_of(me), local_sem)
        own_pack = pltpu.make_async_copy(pack_hbm, all_hbm.at[_slot(me)], p_local)

        def keep_tile(q):
            return [
                pltpu.make_async_copy(wall.at[q, pl.ds(W_IN_SHARD * c, W_IN_SHARD), :], wt_hbm.at[2 * q + c], load_sems.at[2 * (q ^ chip) + c])
                for c in range(2)
            ]

        @pl.when((j == 0) & (i == 0))
        def _():
            modulation()
            mod_ref[...] = mod_buf[...]
            own_shard.start()
            own_shard.wait()
            shard_copy(0, me, sibling).start()
            for rel in range(1, 4):
                shard_copy(rel, me, _flip(me, 2 * rel)).start()
            own_pack.start()
            for cp in _direct_gather_copies(pack_hbm, all_hbm, p_send, p_recv, arriving=False):
                cp.start()
            shard_copy(0, sibling, me).wait_recv()
            for cp in keep_tile(chip):
                cp.start()

        for rel in range(1, 4):

            @pl.when((j == rel) & (i == 0))
            def _(rel=rel):
                there = _flip(me, 2 * rel)
                shard_copy(rel, there, me).wait_recv()
                shard_copy(3 + rel, there, sibling).start()
                shard_copy(3 + rel, _flip(sibling, 2 * rel), me).wait_recv()
                for cp in keep_tile(chip ^ rel):
                    cp.start()

        rows = pl.ds(pl.multiple_of(i * tm, tm), tm)

        @pl.when(j == 0)
        def _():
            xv = x_ref[...]
            r = lax.rsqrt(jnp.mean(xv * xv, axis=-1, keepdims=True) + NORM_EPS)
            shift = mod_buf[:, 0:D]
            scale = mod_buf[:, D : 2 * D]
            hv = (((xv * r) * ng_ref[...]) * (1.0 + scale) + shift).astype(BF16)
            hbuf[rows, :] = hv
            h_ref[...] = hv

        proj_ref[...] = _dot_nt(hbuf[rows, :], wall[chip ^ j])

        @pl.when((chip ^ j) == 0)
        def _():
            cos, s1, s2 = cos_ref[...], s1_ref[...], s2_ref[...]
            for g in range(n_rope):
                cols = slice(128 * g, 128 * (g + 1))
                proj_ref[:, cols] = _rope(proj_ref[:, cols], cos, s1, s2)

        @pl.when((j == n_j - 1) & (i == n_i - 1))
        def _():
            shard_copy(0, me, sibling).wait_send()
            for rel in range(1, 4):
                shard_copy(rel, me, _flip(me, 2 * rel)).wait_send()
                shard_copy(3 + rel, _flip(me, 2 * rel), sibling).wait_send()
            for q in range(N_DEV // 2):
                for cp in keep_tile(q):
                    cp.wait()
            for cp in _direct_gather_copies(pack_hbm, all_hbm, p_send, p_recv, arriving=True):
                cp.wait_recv()
            for cp in _direct_gather_copies(pack_hbm, all_hbm, p_send, p_recv, arriving=False):
                cp.wait_send()
            own_pack.wait()

    tab = pl.BlockSpec((tm, 128), lambda j, i, c: (i, 0))
    any_spec = pl.BlockSpec(memory_space=pl.ANY)
    return pl.pallas_call(
        body,
        name="fwd_in",
        grid_spec=pltpu.PrefetchScalarGridSpec(
            num_scalar_prefetch=1,
            grid=(n_j, n_i),
            in_specs=[
                pl.BlockSpec((tm, D), lambda j, i, c: (jnp.where(j == 0, i, 0), 0)),
                pl.BlockSpec(w_ada.shape, lambda j, i, c: (0, 0)),
                pl.BlockSpec((1, n_ada), lambda j, i, c: (0, 0)),
                pl.BlockSpec((1, D), lambda j, i, c: (0, 0)),
                tab, tab, tab, any_spec, any_spec, any_spec, any_spec,
            ],
            out_specs=[pl.BlockSpec((tm, tn), lambda j, i, c: (i, c[0] ^ j)),
                       pl.BlockSpec((tm, D), lambda j, i, c: (jnp.where(j == 0, i, n_i - 1), 0)),
                       pl.BlockSpec((1, 3 * D), lambda j, i, c: (0, 0)),
                       any_spec, any_spec, any_spec, any_spec],
            scratch_shapes=[
                pltpu.VMEM((N_DEV // 2, tn, D), BF16),
                pltpu.VMEM((t, D), BF16),
                pltpu.VMEM((N_DEV,) + c_rows.shape, F32),
                pltpu.VMEM((N_DEV, n_ada), F32),
                pltpu.VMEM((N_DEV, N_DEV, n_ada), F32),
                pltpu.VMEM((1, 3 * D), F32),
                pltpu.SemaphoreType.DMA((N_DEV - 1,)),
                pltpu.SemaphoreType.DMA((N_DEV - 1,)),
                pltpu.SemaphoreType.DMA((N_DEV,)),
                pltpu.SemaphoreType.DMA,
                pltpu.SemaphoreType.DMA((N_DEV - 1,)),
                pltpu.SemaphoreType.DMA((N_DEV - 1,)),
                pltpu.SemaphoreType.DMA,
                pltpu.SemaphoreType.DMA((N_DEV - 1,)),
                pltpu.SemaphoreType.DMA((N_DEV - 1,)),
                pltpu.SemaphoreType.DMA,
                pltpu.SemaphoreType.DMA((N_DEV - 1,)),
                pltpu.SemaphoreType.DMA((N_DEV - 1,)),
                pltpu.SemaphoreType.DMA,
                pltpu.SemaphoreType.DMA((N_DEV - 1,)),
                pltpu.SemaphoreType.DMA((N_DEV - 1,)),
            ],
        ),
        out_shape=[jax.ShapeDtypeStruct((t, IN_W), F32), jax.ShapeDtypeStruct((t, D), BF16),
                   jax.ShapeDtypeStruct((1, 3 * D), F32),
                   jax.ShapeDtypeStruct((N_DEV,) + w_shard.shape, w_shard.dtype),
                   jax.ShapeDtypeStruct((N_DEV,) + pack.shape, pack.dtype),
                   jax.ShapeDtypeStruct((N_DEV,) + c_rows.shape, F32),
                   jax.ShapeDtypeStruct((N_DEV,) + conv_rows.shape, F32)],
        compiler_params=_params(2),
    )(where, x, w_ada, b_cols, norm_g, *tables, w_shard, pack, c_rows, conv_rows)


def _rope_tables(positions):
    half = ROT_DIM // 2
    inv_freq = ROPE_THETA ** (-jnp.arange(0, ROT_DIM, 2, dtype=F32) / ROT_DIM)
    pos = positions.astype(F32)
    lane = jnp.arange(128) % HEAD_DIM
    freq_lane = jnp.where(lane < ROT_DIM, jnp.tile(inv_freq, 128 // half), 0.0)
    ang = pos[:, None] * freq_lane[None, :]
    sin = jnp.sin(ang)
    lane_tables = (jnp.cos(ang), jnp.where(lane < half, -sin, 0.0), jnp.where((lane >= half) & (lane < ROT_DIM), sin, 0.0))
    ang_t = inv_freq[:, None] * pos[None, :]
    return lane_tables, (jnp.cos(ang_t), jnp.sin(ang_t))


def _rope(tv, cos, s1, s2):
    cols = []
    for g in range(tv.shape[1] // 128):
        blk = tv[:, 128 * g : 128 * (g + 1)]
        cols.append(blk * cos + pltpu.roll(blk, 128 - 8, 1) * s1 + pltpu.roll(blk, 8, 1) * s2)
    return cols[0] if len(cols) == 1 else jnp.concatenate(cols, axis=1)


def _rope_transposed(gv, cos, s1, s2):
    cols = []
    for g in range(gv.shape[1] // 128):
        blk = gv[:, 128 * g : 128 * (g + 1)]
        cols.append(blk * cos + pltpu.roll(blk * s1, 8, 1) + pltpu.roll(blk * s2, 128 - 8, 1))
    return cols[0] if len(cols) == 1 else jnp.concatenate(cols, axis=1)


def _lane_lo():
    return lax.broadcasted_iota(jnp.int32, (1, 128), 1) < HEAD_DIM


def _row_lo():
    return lax.broadcasted_iota(jnp.int32, (128, 1), 0) < HEAD_DIM


def _split_heads(tile, j):
    lo = _lane_lo()
    parts = []
    for g in (2 * j, 2 * j + 1):
        blk = tile[:, 128 * g : 128 * (g + 1)]
        parts += [jnp.where(lo, blk, 0.0), jnp.where(lo, 0.0, blk)]
    return jnp.concatenate(parts, axis=0)


def _dup_lanes(kv, j):
    rolled = pltpu.roll(kv, HEAD_DIM, 1)
    lo = _lane_lo()
    return jnp.where(lo, kv, rolled) if j == 0 else jnp.where(lo, rolled, kv)


def _fold_lanes(acc):
    return acc + pltpu.roll(acc, HEAD_DIM, 1)


def _fill_bias_t(bias_ref):
    kj = lax.broadcasted_iota(jnp.int32, (256, 128), 0)
    qi = lax.broadcasted_iota(jnp.int32, (256, 128), 1)
    diff = qi + WINDOW - kj
    band = (diff >= 0) & (diff < WINDOW)
    bias_ref[0] = jnp.where(band, 0.0, -1e30)
    bias_ref[1] = jnp.where(band & (kj >= 128), 0.0, -1e30)


def _sink_row(sink_ref, pair, j):
    return jnp.concatenate([jnp.full((1, 128), sink_ref[0, 8 * pair + 4 * j + h], F32) for h in range(4)], axis=1)


def _softmax_t(q4, kw, bias4, sink):
    s = _dot_nt(kw, q4) + bias4
    m = jnp.maximum(jnp.max(s, axis=0, keepdims=True), sink)
    p = jnp.exp(s - m)
    e_sink = jnp.exp(sink - m)
    denom = jnp.sum(p, axis=0, keepdims=True) + e_sink
    return p, e_sink, denom


def _attn_specs(t, tq):
    return dict(
        q=pl.BlockSpec((tq, 512), lambda p, i: (i, p)),
        k=pl.BlockSpec((t, 128), lambda p, i: (0, SEG_K // 128 + p)),
        v=pl.BlockSpec((t, 128), lambda p, i: (0, SEG_V // 128 + p)),
        sink=pl.BlockSpec(memory_space=pltpu.SMEM),
        act=pl.BlockSpec((tq, 512), lambda p, i: (i, p)),
    )


def _attn_fwd(proj, sinks):
    t = proj.shape[0]
    tq = min(512, t)
    nb = tq // 128
    n_tiles = t // tq
    sp = _attn_specs(t, tq)

    def body(q_ref, k_ref, v_ref, sink_ref, o_ref, kd, vdt, bias_ref):
        pair = pl.program_id(0)
        i = pl.program_id(1)

        @pl.when(i == 0)
        def _():
            kv, vv = k_ref[...], v_ref[...]
            for j in range(2):
                kd[j, 0:128, :] = jnp.zeros((128, 128), BF16)
                kd[j, 128:, :] = _dup_lanes(kv, j).astype(BF16)
                vdt[j, :, 0:128] = jnp.zeros((128, 128), BF16)
                vdt[j, :, 128:] = _dup_lanes(vv, j).T.astype(BF16)
            _fill_bias_t(bias_ref)

        row_lo = _row_lo()

        def block(b, carry):
            r0 = pl.multiple_of(b * 128, 128)
            gb = i * nb + b
            keys = pl.ds(pl.multiple_of(gb * 128, 128), 256)
            rows = pl.ds(r0, 128)
            qt = q_ref[rows, :] * (1.0 / math.sqrt(HEAD_DIM))
            bias = bias_ref[jnp.where(gb == 0, 1, 0)]
            bias4 = jnp.concatenate([bias] * 4, axis=1)
            cols = []
            for j in range(2):
                q4 = _split_heads(qt, j).astype(BF16)
                p, _, denom = _softmax_t(q4, kd[j, keys, :], bias4, _sink_row(sink_ref, pair, j))
                ot = _dot(vdt[j, :, keys], p.astype(BF16)) * (1.0 / denom)
                for gi in range(2):
                    z = jnp.where(row_lo, ot[:, 256 * gi : 256 * gi + 128], ot[:, 256 * gi + 128 : 256 * gi + 256])
                    cols.append(z.T)
            o_ref[rows, :] = jnp.concatenate(cols, axis=1).astype(BF16)
            return carry

        lax.fori_loop(0, nb, block, 0, unroll=True)

    return pl.pallas_call(
        body,
        name="attn_fwd",
        grid=(2, n_tiles),
        in_specs=[sp["q"], sp["k"], sp["v"], sp["sink"]],
        out_specs=sp["act"],
        out_shape=jax.ShapeDtypeStruct((t, D), BF16),
        scratch_shapes=[pltpu.VMEM((2, t + 128, 128), BF16), pltpu.VMEM((2, 128, t + 128), BF16), pltpu.VMEM((2, 256, 128), F32)],
        compiler_params=_params(2),
    )(proj, proj, proj, sinks)


def _attn_bwd(proj, o, d_o, tables, sinks, part_3):
    t = proj.shape[0]
    tq = min(512, t)
    nb = tq // 128
    n_tiles = t // tq
    (cos, s1, s2), (cos_t, sin_t) = tables
    sp = _attn_specs(t, tq)
    scale = 1.0 / math.sqrt(HEAD_DIM)
    half = ROT_DIM // 2

    def unrope_rows(z, c, s):
        out = []
        for base in (0, HEAD_DIM):
            g0, g1 = z[base : base + half], z[base + half : base + ROT_DIM]
            out += [g0 * c + g1 * s, g1 * c - g0 * s, z[base + ROT_DIM : base + HEAD_DIM]]
        return jnp.concatenate(out, axis=0)

    def swap(part_hbm, recv_hbm, send_sem, recv_sem):
        sibling = _flip(_position(), 1)
        return pltpu.make_async_remote_copy(
            src_ref=part_hbm.at[:, :, sibling[2]], dst_ref=recv_hbm, send_sem=send_sem, recv_sem=recv_sem,
            device_id=sibling, device_id_type=MESH,
        )

    def body(q_ref, k_ref, v_ref, o_ref, do_ref, ct_ref, st_ref, ck, s1k, s2k, sink_ref, part_hbm,
             dq_ref, dk_ref, dv_ref, dsink_ref, recv_hbm, kd, kdt, vd, bias_ref, dkacc, dvacc, send_sem, recv_sem):
        pair = pl.program_id(0)
        i = pl.program_id(1)

        @pl.when((pair == 0) & (i == 0))
        def _():
            swap(part_hbm, recv_hbm, send_sem, recv_sem).start()

        @pl.when((pair == 1) & (i == n_tiles - 1))
        def _():
            swap(part_hbm, recv_hbm, send_sem, recv_sem).wait()

        @pl.when(i == 0)
        def _():
            kv, vv = k_ref[...], v_ref[...]
            for j in range(2):
                kdup = _dup_lanes(kv, j)
                kd[j, 0:128, :] = jnp.zeros((128, 128), BF16)
                kd[j, 128:, :] = kdup.astype(BF16)
                kdt[j, :, 0:128] = jnp.zeros((128, 128), BF16)
                kdt[j, :, 128:] = kdup.T.astype(BF16)
                vd[j, 0:128, :] = jnp.zeros((128, 128), BF16)
                vd[j, 128:, :] = _dup_lanes(vv, j).astype(BF16)
            _fill_bias_t(bias_ref)
            dkacc[...] = jnp.zeros_like(dkacc)
            dvacc[...] = jnp.zeros_like(dvacc)
            dsink_ref[...] = jnp.zeros_like(dsink_ref)

        row_lo = _row_lo()
        ones = jnp.ones((8, 128), BF16)

        def block(b, carry):
            r0 = pl.multiple_of(b * 128, 128)
            gb = i * nb + b
            keys = pl.ds(pl.multiple_of(gb * 128, 128), 256)
            rows = pl.ds(r0, 128)
            qt = q_ref[rows, :] * scale
            dob = do_ref[rows, :].astype(F32)
            prod = dob * o_ref[rows, :].astype(F32)
            ct, st = ct_ref[:, rows], st_ref[:, rows]
            bias = bias_ref[jnp.where(gb == 0, 1, 0)]
            bias4 = jnp.concatenate([bias] * 4, axis=1)
            cols = []
            for j in range(2):
                q4 = _split_heads(qt, j).astype(BF16)
                kw = kd[j, keys, :]
                p, e_sink, denom = _softmax_t(q4, kw, bias4, _sink_row(sink_ref, pair, j))
                inv = 1.0 / denom
                prob = p * inv
                do4 = _split_heads(dob, j).astype(BF16)
                delta = _dot_nt(ones, _split_heads(prod, j).astype(BF16))[0:1, :]
                dp = _dot_nt(vd[j, keys, :], do4)
                ds = (prob * (dp - delta)).astype(BF16)
                dkacc[j, keys, :] = dkacc[j, keys, :] + _dot(ds, q4)
                dvacc[j, keys, :] = dvacc[j, keys, :] + _dot(prob.astype(BF16), do4)
                dqt = _dot(kdt[j, :, keys], ds) * scale
                dsink = -(e_sink * inv) * delta
                for h in range(4):
                    row = 4 * j + h
                    total = jnp.sum(dsink[:, 128 * h : 128 * (h + 1)], axis=1, keepdims=True)
                    dsink_ref[0, row : row + 1, :] = dsink_ref[0, row : row + 1, :] + jnp.broadcast_to(total, (1, 128))
                for gi in range(2):
                    z = jnp.where(row_lo, dqt[:, 256 * gi : 256 * gi + 128], dqt[:, 256 * gi + 128 : 256 * gi + 256])
                    cols.append(unrope_rows(z, ct, st).T)
            dq_ref[rows, :] = jnp.concatenate(cols, axis=1).astype(BF16)
            return carry

        lax.fori_loop(0, nb, block, 0, unroll=True)

        @pl.when(i == n_tiles - 1)
        def _():
            lo = _lane_lo()
            dk = jnp.where(lo, _fold_lanes(dkacc[0, 128:, :]), _fold_lanes(dkacc[1, 128:, :]))
            dk_ref[...] = _rope_transposed(dk, ck[...], s1k[...], s2k[...]).astype(BF16)
            dv_ref[...] = jnp.where(lo, _fold_lanes(dvacc[0, 128:, :]), _fold_lanes(dvacc[1, 128:, :])).astype(BF16)

    kv_out = pl.BlockSpec((t, 128), lambda p, i: (0, p))
    tab_t = pl.BlockSpec((half, tq), lambda p, i: (0, i))
    tab_k = pl.BlockSpec((t, 128), lambda p, i: (0, 0))
    return pl.pallas_call(
        body,
        name="attn_bwd",
        grid=(2, n_tiles),
        in_specs=[sp["q"], sp["k"], sp["v"], sp["act"], sp["act"], tab_t, tab_t, tab_k, tab_k, tab_k, sp["sink"],
                  pl.BlockSpec(memory_space=pl.ANY)],
        out_specs=[sp["act"], kv_out, kv_out, pl.BlockSpec((1, 8, 128), lambda p, i: (p, 0, 0)), pl.BlockSpec(memory_space=pl.ANY)],
        out_shape=[
            jax.ShapeDtypeStruct((t, D), BF16),
            jax.ShapeDtypeStruct((t, 256), BF16),
            jax.ShapeDtypeStruct((t, 256), BF16),
            jax.ShapeDtypeStruct((2, 8, 128), F32),
            jax.ShapeDtypeStruct((3, 4, ROW_SHARD, D), F32),
        ],
        scratch_shapes=[pltpu.VMEM((2, t + 128, 128), BF16), pltpu.VMEM((2, 128, t + 128), BF16), pltpu.VMEM((2, t + 128, 128), BF16),
                        pltpu.VMEM((2, 256, 128), F32), pltpu.VMEM((2, t + 128, 128), F32), pltpu.VMEM((2, t + 128, 128), F32),
                        pltpu.SemaphoreType.DMA, pltpu.SemaphoreType.DMA],
        compiler_params=_params(2),
    )(proj, proj, proj, o, d_o, cos_t, sin_t, cos, s1, s2, sinks, part_3)


HALF = 512


def _softplus(y):
    u = jnp.exp(-jnp.abs(y))
    w = 1.0 + u
    log1p = jnp.where(w == 1.0, u, jnp.log(w) * (u / jnp.where(w == 1.0, 1.0, w - 1.0)))
    return jnp.maximum(y, 0.0) + log1p


def _gate_block(w_ref, blk):
    return w_ref[:, :, 256 * blk : 256 * (blk + 1)].reshape(256, 256)


def _gate_matmul(act_bf, w_ref):
    return jnp.concatenate([_dot(act_bf[:, 256 * blk : 256 * (blk + 1)], _gate_block(w_ref, blk)) for blk in range(2)], axis=1)


def _gate_matmul_t(grad_bf, w_ref):
    return jnp.concatenate([_dot_nt(grad_bf[:, 256 * blk : 256 * (blk + 1)], _gate_block(w_ref, blk)) for blk in range(2)], axis=1)


def _conv_and_gates(xe_ref, tm, cw_ref, cb_ref, wa_ref, wx_ref, ba_ref, bx_ref, lam_ref, keep):
    xc = cb_ref[...] + sum(cw_ref[k : k + 1, :] * xe_ref[pl.ds(5 + k, tm), :] for k in range(CONV_WIDTH))
    xc_bf = xc.astype(BF16)
    r = _sigmoid(_gate_matmul(xc_bf, wa_ref) + ba_ref[...])
    ig = _sigmoid(_gate_matmul(xc_bf, wx_ref) + bx_ref[...])
    sp = _softplus(-lam_ref[...])
    log_a = -LRU_C * r * sp
    tanh = jnp.tanh(-log_a)
    mult = jnp.sqrt(2.0 * tanh / (1.0 + tanh))
    a = jnp.where(keep, jnp.exp(log_a), 0.0)
    mult = jnp.where(keep, mult, 1.0)
    return xc, xc_bf, r, ig, sp, a, mult


def _scan_rows(a_ref, b_ref, h_ref, carry_ref, tm, reverse):
    width = a_ref.shape[1]
    row = lax.broadcasted_iota(jnp.int32, (8, width), 0)

    def group(g, carry):
        gg = tm // 8 - 1 - g if reverse else g
        r0 = pl.multiple_of(gg * 8, 8)
        a = a_ref[pl.ds(r0, 8), :]
        b = b_ref[pl.ds(r0, 8), :]
        for s in (1, 2, 4):
            shift = 8 - s if reverse else s
            a_sh = pltpu.roll(a, shift, 0)
            b_sh = pltpu.roll(b, shift, 0)
            ok = (row < 8 - s) if reverse else (row >= s)
            b = jnp.where(ok, a * b_sh + b, b)
            a = jnp.where(ok, a * a_sh, a)
        h = a * carry + b
        h_ref[pl.ds(r0, 8), :] = h
        last = h[0:1, :] if reverse else h[7:8, :]
        return jnp.broadcast_to(last, (8, width))

    carry_ref[...] = lax.fori_loop(0, tm // 8, group, carry_ref[...])


def _rnn_specs(t, tm, order):
    n = t // tm
    return dict(
        tile=pl.BlockSpec((tm, HALF), lambda ch, i: (order(i, n), ch)),
        before=lambda col0: pl.BlockSpec((8, HALF), lambda ch, i: (jnp.maximum(order(i, n) * (tm // 8) - 1, 0), col0 + ch)),
        proj_tile=lambda col0: pl.BlockSpec((tm, HALF), lambda ch, i: (order(i, n), col0 + ch)),
        keep=pl.BlockSpec((tm, 1), lambda ch, i: (order(i, n), 0)),
        gates=lambda which: pl.BlockSpec((N_DEV, GATE_SHARD, HALF), lambda ch, i: (0, PACK_GATE // GATE_SHARD + which, ch)),
        gate_grads=pl.BlockSpec((2, 2, 256, 256), lambda ch, i: (0, ch, 0, 0)),
        vec=pl.BlockSpec((1, HALF), lambda ch, i: (0, ch)),
        conv=pl.BlockSpec((CONV_WIDTH, HALF), lambda ch, i: (0, ch)),
    )


def _rnn_fwd(proj, keep, w_all, rg_ba, rg_bx, rg_lambda, conv_w, conv_b):
    t = proj.shape[0]
    tm = min(512, t)
    sp = _rnn_specs(t, tm, lambda i, n: i)

    def body(xr_ref, xb_ref, keep_ref, wa_ref, wx_ref, ba_ref, bx_ref, lam_ref, cw_ref, cb_ref, hh_ref, xe, a_buf, b_buf, carry):
        i = pl.program_id(1)

        @pl.when(i == 0)
        def _():
            carry[...] = jnp.zeros_like(carry)

        xe[0:8, :] = jnp.where(i > 0, xb_ref[...], 0.0)
        xe[8:, :] = xr_ref[...]
        keep_rows = keep_ref[...] != 0.0
        xc, _, _, ig, _, a, mult = _conv_and_gates(xe, tm, cw_ref, cb_ref, wa_ref, wx_ref, ba_ref, bx_ref, lam_ref, keep_rows)
        a_buf[...] = a
        b_buf[...] = mult * (ig * xc)
        _scan_rows(a_buf, b_buf, hh_ref, carry, tm, reverse=False)

    return pl.pallas_call(
        body,
        name="rnn_fwd",
        grid=(2, t // tm),
        in_specs=[sp["proj_tile"](SEG_XR // HALF), sp["before"](SEG_XR // HALF), sp["keep"], sp["gates"](0), sp["gates"](1),
                  sp["vec"], sp["vec"], sp["vec"], sp["conv"], sp["vec"]],
        out_specs=sp["tile"],
        out_shape=jax.ShapeDtypeStruct((t, D), F32),
        scratch_shapes=[pltpu.VMEM((tm + 8, HALF), F32), pltpu.VMEM((tm, HALF), F32), pltpu.VMEM((tm, HALF), F32), pltpu.VMEM((8, HALF), F32)],
        compiler_params=_params(2),
    )(proj, proj, keep, w_all, w_all, rg_ba, rg_bx, rg_lambda, conv_w, conv_b)


def _rnn_bwd(proj, hh, d_hh, keep, w_all, rg_ba, rg_bx, rg_lambda, conv_w, conv_b, sum_3_bf):
    t = proj.shape[0]
    tm = min(512, t)
    n_tiles = t // tm
    sp = _rnn_specs(t, tm, lambda i, n: n - 1 - i)

    def body(xr_ref, xb_ref, hh_ref, hb_ref, dhh_ref, keep_ref, wa_ref, wx_ref, ba_ref, bx_ref, lam_ref, cw_ref, cb_ref,
             s3_hbm, dxr_ref, gw_ref, gvec_ref, l3_hbm, xe, he, ae, c_buf, g_buf, dxe, carry, a_next, send_sems, recv_sems):
        ch = pl.program_id(0)
        i = pl.program_id(1)
        first_tile = i == n_tiles - 1
        views = [lambda ref, peer: ref.at[:, _chip_of(peer)]]

        @pl.when((ch == 0) & (i == 0))
        def _():
            for cp in _chip_copies([s3_hbm], [l3_hbm], views, send_sems, recv_sems):
                cp.start()

        @pl.when((ch == 1) & (i == n_tiles - 1))
        def _():
            for cp in _chip_copies([s3_hbm], [l3_hbm], views, send_sems, recv_sems):
                cp.wait_recv()
                cp.wait_send()

        @pl.when(i == 0)
        def _():
            carry[...] = jnp.zeros_like(carry)
            a_next[...] = jnp.zeros_like(a_next)
            dxe[tm:, :] = jnp.zeros((8, HALF), F32)
            gw_ref[...] = jnp.zeros_like(gw_ref)
            gvec_ref[...] = jnp.zeros_like(gvec_ref)

        xe[0:8, :] = jnp.where(first_tile, 0.0, xb_ref[...])
        xe[8:, :] = xr_ref[...]
        he[0:8, :] = jnp.where(first_tile, 0.0, hb_ref[...])
        he[8:, :] = hh_ref[...]
        keep_rows = keep_ref[...] != 0.0
        xc, xc_bf, r, ig, sp_, a, mult = _conv_and_gates(xe, tm, cw_ref, cb_ref, wa_ref, wx_ref, ba_ref, bx_ref, lam_ref, keep_rows)

        ae[0:tm, :] = a
        ae[tm:, :] = a_next[...]
        a_next[...] = jnp.broadcast_to(a[0:1, :], (8, HALF))
        c_buf[...] = ae[pl.ds(1, tm), :]
        g_buf[...] = dhh_ref[...]
        _scan_rows(c_buf, g_buf, g_buf, carry, tm, reverse=True)
        g = g_buf[...]

        h_prev = he[pl.ds(7, tm), :]
        bb = ig * xc
        d_ig = g * mult * xc
        d_xc = g * mult * ig
        d_log_a = jnp.where(keep_rows, g * h_prev * a - (g * bb) * (a * a) / mult, 0.0)
        d_r = d_log_a * (-LRU_C * sp_)
        d_sp = jnp.sum(d_log_a * (-LRU_C * r), axis=0, keepdims=True)
        d_zr = d_r * r * (1.0 - r)
        d_zi = d_ig * ig * (1.0 - ig)
        d_zr_bf = d_zr.astype(BF16)
        d_zi_bf = d_zi.astype(BF16)
        d_xc = d_xc + _gate_matmul_t(d_zr_bf, wa_ref) + _gate_matmul_t(d_zi_bf, wx_ref)
        for blk in range(2):
            cols = slice(256 * blk, 256 * (blk + 1))
            gw_ref[0, blk] = gw_ref[0, blk] + _dot_tn(xc_bf[:, cols], d_zr_bf[:, cols])
            gw_ref[1, blk] = gw_ref[1, blk] + _dot_tn(xc_bf[:, cols], d_zi_bf[:, cols])

        dxe[0:tm, :] = d_xc
        shifted = [dxe[pl.ds(3 - k, tm), :] for k in range(CONV_WIDTH)]
        dxr = sum(cw_ref[k : k + 1, :] * shifted[k] for k in range(CONV_WIDTH))
        dxr_ref[...] = dxr.astype(BF16)
        dxe[tm:, :] = d_xc[0:8, :]

        lam = lam_ref[...]
        xr_tile = xr_ref[...]
        sums = [
            jnp.sum(d_zr, axis=0, keepdims=True),
            jnp.sum(d_zi, axis=0, keepdims=True),
            d_sp * (-_sigmoid(-lam)),
            jnp.sum(d_xc, axis=0, keepdims=True),
        ] + [jnp.sum(xr_tile * shifted[k], axis=0, keepdims=True) for k in range(CONV_WIDTH)]
        gvec_ref[...] = gvec_ref[...] + jnp.concatenate(sums, axis=0)

    return pl.pallas_call(
        body,
        name="rnn_bwd",
        grid=(2, n_tiles),
        in_specs=[
            sp["proj_tile"](SEG_XR // HALF), sp["before"](SEG_XR // HALF), sp["tile"], sp["before"](0), sp["tile"], sp["keep"],
            sp["gates"](0), sp["gates"](1), sp["vec"], sp["vec"], sp["vec"], sp["conv"], sp["vec"], pl.BlockSpec(memory_space=pl.ANY),
        ],
        out_specs=[sp["tile"], sp["gate_grads"], pl.BlockSpec((8, HALF), lambda ch, i: (0, ch)), pl.BlockSpec(memory_space=pl.ANY)],
        out_shape=[
            jax.ShapeDtypeStruct((t, D), BF16),
            jax.ShapeDtypeStruct((2, 4, 256, 256), F32),
            jax.ShapeDtypeStruct((8, D), F32),
            jax.ShapeDtypeStruct((len(CHIP_FLIPS), 3, ROW_SHARD, D), BF16),
        ],
        scratch_shapes=[
            pltpu.VMEM((tm + 8, HALF), F32),
            pltpu.VMEM((tm + 8, HALF), F32),
            pltpu.VMEM((tm + 8, HALF), F32),
            pltpu.VMEM((tm, HALF), F32),
            pltpu.VMEM((tm, HALF), F32),
            pltpu.VMEM((tm + 8, HALF), F32),
            pltpu.VMEM((8, HALF), F32),
            pltpu.VMEM((8, HALF), F32),
            pltpu.SemaphoreType.DMA((len(CHIP_FLIPS),)),
            pltpu.SemaphoreType.DMA((len(CHIP_FLIPS),)),
        ],
        compiler_params=_params(2),
    )(proj, proj, hh, hh, d_hh, keep, w_all, w_all, rg_ba, rg_bx, rg_lambda, conv_w, conv_b, sum_3_bf)


def _mid(x, target, o, hh, proj, w_all, gate, final_g):
    t = x.shape[0]
    tm = min(256, t)
    n_tiles = t // tm
    full = pl.BlockSpec((tm, D), lambda i: (i, 0))
    half = lambda col: pl.BlockSpec((tm, 512), lambda i: (i, col))
    vec = pl.BlockSpec((1, D), lambda i: (0, 0))
    any_spec = pl.BlockSpec(memory_space=pl.ANY)

    def body(x_ref, tg_ref, o_ref, hh_ref, ga0, ga1, gr0, gr1, ma0, ma1, mr0, mr1, w_hbm, gate_ref, fg_ref,
             dres_ref, do_ref, dhh_ref, dga_ref, dgr_ref, dma_ref, dmr_ref, gw_hbm, vec_ref, w, gw, sem, w_sems):
        i = pl.program_id(0)

        @pl.when(i == 0)
        def _():
            loads = [
                pltpu.make_async_copy(
                    w_hbm.at[s, pl.ds(ROW_SHARD * k, ROW_SHARD), :], w.at[k, pl.ds(ROW_SHARD * s, ROW_SHARD), :], w_sems.at[k * N_DEV + s]
                )
                for k in range(3)
                for s in range(N_DEV)
            ]
            for cp in loads:
                cp.start()
            for cp in loads:
                cp.wait()
            gw[...] = jnp.zeros_like(gw)
            vec_ref[...] = jnp.zeros_like(vec_ref)

        cat = lambda a, b: jnp.concatenate([a[...], b[...]], axis=1)
        ga, gr, ma, mr = cat(ga0, ga1), cat(gr0, gr1), cat(ma0, ma1), cat(mr0, mr1)
        ov, hv, gate_v, fg = o_ref[...].astype(F32), hh_ref[...], gate_ref[...], fg_ref[...]
        sg_a = _sigmoid(ga)
        sg_r = _sigmoid(gr)
        silu_a = ga * sg_a
        silu_r = gr * sg_r
        ya = (ov * silu_a).astype(BF16)
        yr = (hv * silu_r).astype(BF16)
        pa = _dot(ya, w[0])
        pr = _dot(yr, w[1])
        sa = _sigmoid(ma)
        sr = _sigmoid(mr)
        merged = (sa * pa + sr * pr).astype(BF16)
        mo = _dot(merged, w[2])
        out = x_ref[...] + gate_v * mo
        r2 = lax.rsqrt(jnp.mean(out * out, axis=-1, keepdims=True) + NORM_EPS)
        outn = out * r2
        err = outn * fg - tg_ref[...]
        loss = 0.5 * jnp.sum(jnp.mean(err * err, axis=-1, keepdims=True), axis=0, keepdims=True)
        dy = err * (1.0 / D)
        g_final = jnp.sum(dy * outn, axis=0, keepdims=True)
        doutn = dy * fg
        dout = r2 * (doutn - outn * jnp.mean(doutn * outn, axis=-1, keepdims=True))
        dres_ref[...] = dout
        d_gate = jnp.sum(dout * mo, axis=0, keepdims=True)
        dmo = (dout * gate_v).astype(BF16)
        dmerged = _dot_nt(dmo, w[2])
        gw[2] = gw[2] + _dot_tn(merged, dmo)
        dpa = (dmerged * sa).astype(BF16)
        dpr = (dmerged * sr).astype(BF16)
        dma_ref[...] = (dmerged * pa * (sa * (1.0 - sa))).astype(BF16)
        dmr_ref[...] = (dmerged * pr * (sr * (1.0 - sr))).astype(BF16)
        dya = _dot_nt(dpa, w[0])
        dyr = _dot_nt(dpr, w[1])
        gw[0] = gw[0] + _dot_tn(ya, dpa)
        gw[1] = gw[1] + _dot_tn(yr, dpr)
        do_ref[...] = (dya * silu_a).astype(BF16)
        dhh_ref[...] = dyr * silu_r
        dga_ref[...] = (dya * ov * (sg_a * (1.0 + ga * (1.0 - sg_a)))).astype(BF16)
        dgr_ref[...] = (dyr * hv * (sg_r * (1.0 + gr * (1.0 - sg_r)))).astype(BF16)
        zeros = jnp.zeros((5, D), F32)
        vec_ref[...] = vec_ref[...] + jnp.concatenate([g_final, d_gate, jnp.broadcast_to(loss, (1, D)), zeros], axis=0)

        @pl.when(i == n_tiles - 1)
        def _():
            cp = pltpu.make_async_copy(gw, gw_hbm, sem)
            cp.start()
            cp.wait()

    bf_tile = jax.ShapeDtypeStruct((t, D), BF16)
    f_tile = jax.ShapeDtypeStruct((t, D), F32)
    return pl.pallas_call(
        body,
        name="mid",
        grid=(n_tiles,),
        in_specs=[full, full, full, full,
                  half(SEG_GA // 512), half(SEG_GA // 512 + 1), half(SEG_GR // 512), half(SEG_GR // 512 + 1),
                  half(SEG_MA // 512), half(SEG_MA // 512 + 1), half(SEG_MR // 512), half(SEG_MR // 512 + 1),
                  any_spec, vec, vec],
        out_specs=[full, full, full, full, full, full, full, any_spec, pl.BlockSpec((8, D), lambda i: (0, 0))],
        out_shape=[f_tile, bf_tile, f_tile, bf_tile, bf_tile, bf_tile, bf_tile,
                   jax.ShapeDtypeStruct((3, D, D), F32), jax.ShapeDtypeStruct((8, D), F32)],
        scratch_shapes=[pltpu.VMEM((3, D, D), BF16), pltpu.VMEM((3, D, D), F32), pltpu.SemaphoreType.DMA, pltpu.SemaphoreType.DMA((3 * N_DEV,))],
        compiler_params=_params(1),
    )(x, target, o, hh, proj, proj, proj, proj, proj, proj, proj, proj, w_all, gate, final_g)


_SEGMENTS = ((SEG_Q, 1024), (SEG_K, 256), (SEG_V, 256), (SEG_GA, 1024), (SEG_XR, 1024), (SEG_GR, 1024), (SEG_MA, 1024), (SEG_MR, 1024))


def _seg_specs(tm):
    return [pl.BlockSpec((tm, width), lambda i: (i, 0)) for _, width in _SEGMENTS]


def _bwd_dh(dsegs, w_all, x, dres, mod, norm_g):
    t = x.shape[0]
    tm = min(512, t)
    n_tiles = t // tm
    n_seg = len(_SEGMENTS)
    full = pl.BlockSpec((tm, D), lambda i: (i, 0))

    def body(*refs):
        segs = refs[:n_seg]
        w_hbm, x_ref, dres_ref, mod_ref, ng_ref, gx_ref, vec_ref, w, sems = refs[n_seg:]
        i = pl.program_id(0)

        @pl.when(i == 0)
        def _():
            loads = [
                pltpu.make_async_copy(w_hbm.at[s], w.at[pl.ds(W_IN_SHARD * s, W_IN_SHARD), :], sems.at[s])
                for s in range(N_DEV)
            ]
            for cp in loads:
                cp.start()
            for cp in loads:
                cp.wait()
            vec_ref[...] = jnp.zeros_like(vec_ref)

        dh = sum(_dot(seg[...], w[start : start + width, :]) for seg, (start, width) in zip(segs, _SEGMENTS))
        xv = x_ref[...]
        r1 = lax.rsqrt(jnp.mean(xv * xv, axis=-1, keepdims=True) + NORM_EPS)
        xn = xv * r1
        ng = ng_ref[...]
        scale1 = 1.0 + mod_ref[:, D : 2 * D]
        d_shift = jnp.sum(dh, axis=0, keepdims=True)
        d_scale = jnp.sum(dh * (xn * ng), axis=0, keepdims=True)
        g_norm = jnp.sum(dh * xn * scale1, axis=0, keepdims=True)
        dxn = dh * (ng * scale1)
        dx = r1 * (dxn - xn * jnp.mean(dxn * xn, axis=-1, keepdims=True))
        gx_ref[...] = dres_ref[...] + dx
        vec_ref[...] = vec_ref[...] + jnp.concatenate([d_shift, d_scale, g_norm, jnp.zeros((5, D), F32)], axis=0)

    any_spec = pl.BlockSpec(memory_space=pl.ANY)
    return pl.pallas_call(
        body,
        name="bwd_dh",
        grid=(n_tiles,),
        in_specs=_seg_specs(tm) + [any_spec, full, full, pl.BlockSpec((1, 3 * D), lambda i: (0, 0)), pl.BlockSpec((1, D), lambda i: (0, 0))],
        out_specs=[full, pl.BlockSpec((8, D), lambda i: (0, 0))],
        out_shape=[jax.ShapeDtypeStruct((t, D), F32), jax.ShapeDtypeStruct((8, D), F32)],
        scratch_shapes=[pltpu.VMEM((IN_W, D), BF16), pltpu.SemaphoreType.DMA((N_DEV,))],
        compiler_params=_params(1),
    )(*dsegs, w_all, x, dres, mod, norm_g)


_HBM = pl.BlockSpec(memory_space=pltpu.HBM)
_SEM = pl.BlockSpec(memory_space=pltpu.SEMAPHORE)
_EFFECT = pltpu.SideEffectType.DATAFLOW_SIDE_EFFECTING


def _w_exchange_copies(sum_ref, land_ref, sems):
    me = _position()
    n = len(CHIP_FLIPS)
    return [
        pltpu.make_async_remote_copy(
            src_ref=sum_ref.at[_chip_of(_flip(me, k))], dst_ref=land_ref.at[j], send_sem=sems[j], recv_sem=sems[n + j],
            device_id=_flip(me, k), device_id_type=MESH,
        )
        for j, k in enumerate(CHIP_FLIPS)
    ]


def _w_exchange_start(sum_w_bf):
    n_sems = 2 * len(CHIP_FLIPS)
    land = lax.empty((len(CHIP_FLIPS), W_IN_SHARD, D), BF16)

    def body(sum_ref, land_ref, *outs):
        for cp in _w_exchange_copies(sum_ref, land_ref, outs[:n_sems]):
            cp.start()
        token = outs[n_sems + 2]
        token[...] = jnp.zeros_like(token)

    res = pl.pallas_call(
        body,
        name="w_exchange_start",
        out_shape=(*[pltpu.SemaphoreType.DMA(())] * n_sems, pltpu.HBM(sum_w_bf.shape, BF16), pltpu.HBM(land.shape, BF16),
                   jax.ShapeDtypeStruct((8, 128), F32)),
        in_specs=(_HBM, _HBM),
        out_specs=(*[_SEM] * n_sems, _HBM, _HBM, pl.BlockSpec(memory_space=pltpu.VMEM)),
        input_output_aliases={0: n_sems, 1: n_sems + 1},
        compiler_params=pltpu.CompilerParams(has_side_effects=_EFFECT),
    )(pltpu.with_memory_space_constraint(sum_w_bf, pltpu.HBM), pltpu.with_memory_space_constraint(land, pltpu.HBM))
    return res[:n_sems], res[n_sems], res[n_sems + 1], res[n_sems + 2]


def _w_exchange_wait(sems, sum_thru, land_thru, after):
    n_sems = len(sems)

    def body(sum_ref, land_ref, *rest):
        for cp in _w_exchange_copies(sum_ref, land_ref, rest[:n_sems]):
            cp.wait_send()
            cp.wait_recv()

    return pl.pallas_call(
        body,
        name="w_exchange_wait",
        out_shape=(pltpu.HBM(sum_thru.shape, BF16), pltpu.HBM(land_thru.shape, BF16)),
        in_specs=(_HBM, _HBM, *[_SEM] * n_sems, pl.BlockSpec(memory_space=pl.ANY)),
        out_specs=(_HBM, _HBM),
        input_output_aliases={0: 0, 1: 1},
        compiler_params=pltpu.CompilerParams(has_side_effects=_EFFECT),
    )(sum_thru, land_thru, *sems, after)[1]


def _bwd_gw(dsegs, h, sum_g_bf):
    t = h.shape[0]
    tm = min(512, t)
    n_tiles = t // tm
    n_seg = len(_SEGMENTS)
    n_chips = N_DEV // 2
    views = [lambda ref, peer: ref.at[:, :, _chip_of(peer)]]

    def body(*refs):
        segs = refs[:n_seg]
        (h_ref, sg_hbm, own_hbm, recv_hbm, lg_hbm, acc, stage, own_sems, pair_send, pair_recv, send_sems, recv_sems) = refs[n_seg:]
        i = pl.program_id(0)

        @pl.when(i == 0)
        def _():
            for cp in _chip_copies([sg_hbm], [lg_hbm], views, send_sems, recv_sems):
                cp.start()
            acc[...] = jnp.zeros_like(acc)

        hv = h_ref[...]
        for seg, (start, width) in zip(segs, _SEGMENTS):
            acc[start : start + width, :] = acc[start : start + width, :] + _dot_tn(seg[...], hv)

        @pl.when(i == n_tiles - 1)
        def _():
            me = _position()
            sibling = _flip(me, 1)

            def rows(q, c):
                return pl.ds(pl.multiple_of(W_IN_SHARD * (2 * q + c), 8), W_IN_SHARD)

            def to_sibling(q):
                return pltpu.make_async_remote_copy(
                    src_ref=stage.at[q % 2], dst_ref=recv_hbm.at[q], send_sem=pair_send.at[q], recv_sem=pair_recv.at[q],
                    device_id=sibling, device_id_type=MESH,
                )

            kept = [pltpu.make_async_copy(acc.at[rows(q, me[2]), :], own_hbm.at[q], own_sems.at[q]) for q in range(n_chips)]
            for cp in kept:
                cp.start()
            for q in range(n_chips):
                if q >= 2:
                    to_sibling(q - 2).wait_send()
                stage[q % 2] = acc[rows(q, sibling[2]), :].astype(BF16)
                to_sibling(q).start()
            for q in range(n_chips - 2, n_chips):
                to_sibling(q).wait_send()
            for q in range(n_chips):
                to_sibling(q).wait_recv()
            for cp in kept:
                cp.wait()
            for cp in _chip_copies([sg_hbm], [lg_hbm], views, send_sems, recv_sems):
                cp.wait_recv()
                cp.wait_send()

    any_spec = pl.BlockSpec(memory_space=pl.ANY)
    n_copies = len(CHIP_FLIPS)
    return pl.pallas_call(
        body,
        name="bwd_gw",
        grid=(n_tiles,),
        in_specs=_seg_specs(tm) + [pl.BlockSpec((tm, D), lambda i: (i, 0)), any_spec],
        out_specs=[any_spec, any_spec, any_spec],
        out_shape=[jax.ShapeDtypeStruct((n_chips, W_IN_SHARD, D), F32),
                   jax.ShapeDtypeStruct((n_chips, W_IN_SHARD, D), BF16),
                   jax.ShapeDtypeStruct((len(CHIP_FLIPS), 2, 4, GATE_SHARD, 256), BF16)],
        scratch_shapes=[pltpu.VMEM((IN_W, D), F32), pltpu.VMEM((2, W_IN_SHARD, D), BF16),
                        pltpu.SemaphoreType.DMA((n_chips,)), pltpu.SemaphoreType.DMA((n_chips,)), pltpu.SemaphoreType.DMA((n_chips,)),
                        pltpu.SemaphoreType.DMA((n_copies,)), pltpu.SemaphoreType.DMA((n_copies,))],
        compiler_params=_params(1),
    )(*dsegs, h, sum_g_bf)


def _adamw(w, g, m, v):
    m = ADAM_B1 * m + (1.0 - ADAM_B1) * g
    v = ADAM_B2 * v + (1.0 - ADAM_B2) * (g * g)
    m_hat = m / (1.0 - ADAM_B1**ADAM_STEP)
    v_hat = v / (1.0 - ADAM_B2**ADAM_STEP)
    delta = -ADAM_LR * (m_hat / (jnp.sqrt(v_hat) + ADAM_EPS) + ADAM_WD * w)
    return delta, m, v


def _w_in_update(own, land, w_t, m_t, v_t):
    n_slots, rows, cols = land.shape
    rb = 208
    spec = pl.BlockSpec((rb, cols), lambda i: (i, 0))

    def body(o_ref, l_ref, w_ref, m_ref, v_ref, g_ref, d_ref, m2_ref, v2_ref):
        total = o_ref[...]
        for s in range(n_slots):
            total = total + l_ref[s].astype(F32)
        g_ref[...] = total
        d_ref[...], m2_ref[...], v2_ref[...] = _adamw(w_ref[...], total, m_ref[...], v_ref[...])

    shape = jax.ShapeDtypeStruct((rows, cols), F32)
    return pl.pallas_call(
        body,
        name="w_in_update",
        grid=(rows // rb,),
        in_specs=[spec, pl.BlockSpec((n_slots, rb, cols), lambda i: (0, i, 0)), spec, spec, spec],
        out_specs=[spec] * 4,
        out_shape=[shape] * 4,
        compiler_params=_params(1),
    )(own, land, w_t, m_t, v_t)


def _shard_update(own_3, land_3, own_g, land_g, weights, m_in, v_in):
    n = len(weights)

    def body(*refs):
        o3_ref, l3_ref, og_ref, lg_ref = refs[:4]
        w_refs, m_refs, v_refs = (refs[4 + n * k : 4 + n * (k + 1)] for k in range(3))
        outs = refs[4 + 3 * n :]
        for k in range(n):
            if k < 3:
                g = o3_ref[k]
                for slot in range(3):
                    g = g + l3_ref[slot, k].astype(F32)
            else:
                g = og_ref[k - 3]
                for slot in range(3):
                    g = g + lg_ref[slot, k - 3].astype(F32)
            d, m2, v2 = _adamw(w_refs[k][0], g, m_refs[k][0], v_refs[k][0])
            for which, val in enumerate((g, d, m2, v2)):
                outs[which * n + k][0] = val

    shapes = [jax.ShapeDtypeStruct(w.shape, F32) for w in weights]
    res = pl.pallas_call(
        body, name="shard_update", out_shape=shapes * 4, compiler_params=pltpu.CompilerParams(vmem_limit_bytes=VMEM_LIMIT)
    )(own_3, land_3, own_g, land_g, *weights, *m_in, *v_in)
    return [res[n * k : n * (k + 1)] for k in range(4)]


VEC_LAYOUT = (("b_ada", 3 * D), ("norm_g", D), ("attn_sinks", 128), ("conv_b", D), ("rg_ba", D), ("rg_bx", D), ("rg_lambda", D),
              ("final_g", D), ("conv_w", CONV_WIDTH * D), ("loss", 128))
VEC_OFFSET = {name: sum(w for _, w in VEC_LAYOUT[:k]) for k, (name, _) in enumerate(VEC_LAYOUT)}
VEC_ROWS = 112
VEC_WIDTH = VEC_ROWS * 128
assert sum(w for _, w in VEC_LAYOUT) <= VEC_WIDTH
N_REPLICATED = 8


def _small_update(vec_all, c_all_t, me, rep_w, rep_m, rep_v, ada, conv):
    n_ada = ada[0].shape[1]
    n_conv = conv[0].shape[1]

    def body(*refs):
        me_ref, vec_ref, ct_ref = refs[:3]
        w_refs, m_refs, v_refs = (refs[3 + N_REPLICATED * k : 3 + N_REPLICATED * (k + 1)] for k in range(3))
        pos = 3 + 3 * N_REPLICATED
        ada_refs, conv_refs = refs[pos : pos + 3], refs[pos + 3 : pos + 6]
        outs = refs[pos + 6 :]
        loss_ref, rep_outs = outs[0], outs[1 : 1 + 4 * N_REPLICATED]
        ada_outs, conv_outs = outs[1 + 4 * N_REPLICATED : 5 + 4 * N_REPLICATED], outs[5 + 4 * N_REPLICATED : 9 + 4 * N_REPLICATED]
        conv_buf = outs[9 + 4 * N_REPLICATED]

        total = vec_ref[0:1, :]
        for slot in range(1, N_DEV):
            total = total + vec_ref[slot : slot + 1, :]
        loss_ref[...] = total[:, VEC_OFFSET["loss"] : VEC_OFFSET["loss"] + 1]
        for k, (name, _) in enumerate(VEC_LAYOUT[:N_REPLICATED]):
            width = w_refs[k].shape[1]
            g = total[:, VEC_OFFSET[name] : VEC_OFFSET[name] + width]
            d, m2, v2 = _adamw(w_refs[k][...], g, m_refs[k][...], v_refs[k][...])
            for which, val in enumerate((g, d, m2, v2)):
                rep_outs[which * N_REPLICATED + k][...] = val

        device = me_ref[0]
        for k in range(CONV_WIDTH):
            conv_buf[k : k + 1, :] = total[:, VEC_OFFSET["conv_w"] + D * k : VEC_OFFSET["conv_w"] + D * (k + 1)]
        g_conv = conv_buf[:, pl.ds(pl.multiple_of(device * n_conv, 128), n_conv)]
        for ref, val in zip(conv_outs, (g_conv,) + _adamw(conv_refs[0][...], g_conv, conv_refs[1][...], conv_refs[2][...])):
            ref[...] = val

        dmod = vec_ref[:, pl.ds(pl.multiple_of(device * n_ada, 128), n_ada)]
        g_ada = _dot(ct_ref[...], dmod)
        for ref, val in zip(ada_outs, (g_ada,) + _adamw(ada_refs[0][...], g_ada, ada_refs[1][...], ada_refs[2][...])):
            ref[...] = val

    vmem = pl.BlockSpec(memory_space=pltpu.VMEM)
    rep_shapes = [jax.ShapeDtypeStruct(w.shape, F32) for w in rep_w]
    out_shape = ([jax.ShapeDtypeStruct((1, 1), F32)] + rep_shapes * 4 + [jax.ShapeDtypeStruct(ada[0].shape, F32)] * 4
                 + [jax.ShapeDtypeStruct(conv[0].shape, F32)] * 4)
    n_in = 3 + 3 * N_REPLICATED + 6
    res = pl.pallas_call(
        body,
        name="small_update",
        in_specs=[pl.BlockSpec(memory_space=pltpu.SMEM)] + [vmem] * (n_in - 1),
        out_specs=[vmem] * len(out_shape),
        out_shape=out_shape,
        scratch_shapes=[pltpu.VMEM((CONV_WIDTH, D), F32)],
        compiler_params=pltpu.CompilerParams(vmem_limit_bytes=VMEM_LIMIT),
    )(me, vec_all, c_all_t, *rep_w, *rep_m, *rep_v, *ada, *conv)
    loss = res[0]
    reps = [res[1 + N_REPLICATED * k : 1 + N_REPLICATED * (k + 1)] for k in range(4)]
    return loss, reps, res[1 + 4 * N_REPLICATED : 5 + 4 * N_REPLICATED], res[5 + 4 * N_REPLICATED : 9 + 4 * N_REPLICATED]


def kernel(x, c, positions, w_ada, b_ada, norm_g, w_in, attn_sinks, conv_w, conv_b, rg_wa, rg_ba, rg_wx, rg_bx, rg_lambda, w_attn_proj, w_rnn_proj, w_out, final_g, loss_target, m_w_ada, m_b_ada, m_norm_g, m_w_in, m_attn_sinks, m_conv_w, m_conv_b, m_rg_wa, m_rg_ba, m_rg_wx, m_rg_bx, m_rg_lambda, m_w_attn_proj, m_w_rnn_proj, m_w_out, m_final_g, v_w_ada, v_b_ada, v_norm_g, v_w_in, v_attn_sinks, v_conv_w, v_conv_b, v_rg_wa, v_rg_ba, v_rg_wx, v_rg_bx, v_rg_lambda, v_w_attn_proj, v_w_rnn_proj, v_w_out, v_final_g):
    my_x, my_y, my_c = _position()
    me = _slot((my_x, my_y, my_c))
    x2, target = x[0], loss_target[0]
    pos = positions[0]

    gate_rows = lambda w: w[0].transpose(1, 0, 2).reshape(GATE_SHARD, D)
    w_pack = jnp.concatenate([w_attn_proj[0], w_rnn_proj[0], w_out[0], gate_rows(rg_wa), gate_rows(rg_wx)], axis=0).astype(BF16)
    conv_rows = jnp.pad(conv_w[0], ((0, 8 - CONV_WIDTH), (0, 0)))
    w_in_t = w_in[0].T
    n_ada = w_ada.shape[2]
    b_cols = lax.dynamic_slice(b_ada, (0, me * n_ada), (1, n_ada))

    tables = _rope_tables(pos)
    where = jnp.stack([_chip_of((my_x, my_y)), me]).astype(jnp.int32)
    proj, h, mod, w_t_all, w_all, c_rows, conv_all = _fwd_in(
        x2, jnp.broadcast_to(c, (8, D)), conv_rows, w_ada[0], b_cols, norm_g, w_in_t.astype(BF16), tables[0], w_pack, where
    )
    c_all = c_rows[:, 0, :]
    conv_full = conv_all[:, :CONV_WIDTH, :].transpose(1, 0, 2).reshape(CONV_WIDTH, D)
    gate = mod[:, 2 * D :]
    o = _attn_fwd(proj, attn_sinks)
    keep = (pos != 0).astype(F32)[:, None]
    hh = _rnn_fwd(proj, keep, w_all, rg_ba, rg_bx, rg_lambda, conv_full, conv_b)

    dres, d_o, d_hh, d_ga, d_gr, d_ma, d_mr, gw3, mid_vec = _mid(x2, target, o, hh, proj, w_all, gate, final_g[None, :])
    index = jnp.stack([my_c, _chip_of((my_x, my_y))]).astype(jnp.int32)
    part_3 = gw3.reshape(3, 4, 2, ROW_SHARD, D)
    d_q, d_k, d_v, d_sink, recv_3 = _attn_bwd(proj, o, d_o, tables, attn_sinks, part_3)
    sum_3_bf, own_3 = _pair_add(
        "pair_add_3", part_3, recv_3, index, (4,), 0,
        pl.BlockSpec((3, None, None, ROW_SHARD, D), lambda q, idx: (0, q, idx[0], 0, 0)),
        pl.BlockSpec((3, None, ROW_SHARD, D), lambda q, idx: (0, q, 0, 0)),
        pl.BlockSpec((3, ROW_SHARD, D), lambda q, idx: (0, 0, 0)), (3, ROW_SHARD, D),
    )
    d_xr, g_gates, rnn_vec, land_3 = _rnn_bwd(proj, hh, d_hh, keep, w_all, rg_ba, rg_bx, rg_lambda, conv_full, conv_b, sum_3_bf)
    dsegs = (d_q, d_k, d_v, d_ga, d_xr, d_gr, d_ma, d_mr)
    part_g = g_gates.reshape(2, 4, 4, 2, GATE_SHARD, 256)
    (recv_g,) = _exchange("rs_pair_g", [part_g], [lambda ref, peer: ref.at[:, :, :, peer[2]]], [(2, 4, 4, GATE_SHARD, 256)], flips=(1,))
    sum_g_bf, own_g = _pair_add(
        "pair_add_g", part_g, recv_g[0], index, (4,), 0,
        pl.BlockSpec((2, 4, None, None, GATE_SHARD, 256), lambda q, idx: (0, 0, q, idx[0], 0, 0)),
        pl.BlockSpec((2, 4, None, GATE_SHARD, 256), lambda q, idx: (0, 0, q, 0, 0)),
        pl.BlockSpec((2, 4, GATE_SHARD, 256), lambda q, idx: (0, 0, 0, 0)), (2, 4, GATE_SHARD, 256),
    )
    part_w, recv_w, land_g = _bwd_gw(dsegs, h, sum_g_bf)
    w_rows = pl.BlockSpec((None, W_IN_SHARD, D), lambda q, idx: (q, 0, 0))
    sum_w_bf, own_w = _pair_add(
        "pair_add_w", part_w, recv_w, index, (4,), 0, w_rows, w_rows,
        pl.BlockSpec((W_IN_SHARD, D), lambda q, idx: (0, 0)), (W_IN_SHARD, D),
    )
    w_sems, sum_w_thru, land_w_thru, token = _w_exchange_start(sum_w_bf)
    grad_x, dh_vec = _bwd_dh(dsegs, w_t_all, x2, dres, mod + token[0:1, 0:1], norm_g)
    land_w = _w_exchange_wait(w_sems, sum_w_thru, land_w_thru, dh_vec)

    pieces = dict(
        b_ada=jnp.concatenate([dh_vec[0:1], dh_vec[1:2], mid_vec[1:2]], axis=1),
        norm_g=dh_vec[2:3],
        attn_sinks=jnp.pad(d_sink[:, :, 0].reshape(1, 16), ((0, 0), (0, 112))),
        conv_b=rnn_vec[3:4],
        rg_ba=rnn_vec[0:1],
        rg_bx=rnn_vec[1:2],
        rg_lambda=rnn_vec[2:3],
        final_g=mid_vec[0:1],
        conv_w=rnn_vec[4:8].reshape(1, CONV_WIDTH * D),
        loss=mid_vec[2:3, 0:128],
    )
    vec = jnp.concatenate([pieces[name] for name, _ in VEC_LAYOUT], axis=1)
    vec = jnp.pad(vec, ((0, 0), (0, VEC_WIDTH - vec.shape[1]))).reshape(VEC_ROWS, 128)
    (vec_all,) = _all_gather_direct("ag_vec", [vec])
    rep_w = [b_ada, norm_g, attn_sinks, conv_b, rg_ba, rg_bx, rg_lambda, final_g[None, :]]
    rep_m = [m_b_ada, m_norm_g, m_attn_sinks, m_conv_b, m_rg_ba, m_rg_bx, m_rg_lambda, m_final_g[None, :]]
    rep_v = [v_b_ada, v_norm_g, v_attn_sinks, v_conv_b, v_rg_ba, v_rg_bx, v_rg_lambda, v_final_g[None, :]]
    loss, reps, ada_res, conv_res = _small_update(
        vec_all.reshape(N_DEV, VEC_WIDTH), c_all.T, me.astype(jnp.int32).reshape(1), rep_w, rep_m, rep_v,
        (w_ada[0], m_w_ada[0], v_w_ada[0]), (conv_w[0], m_conv_w[0], v_conv_w[0]),
    )

    w_in_res = _w_in_update(own_w, land_w, w_in_t, m_w_in[0].T, v_w_in[0].T)
    shard_res = _shard_update(
        own_3, land_3, own_g, land_g,
        (w_attn_proj, w_rnn_proj, w_out, rg_wa, rg_wx),
        (m_w_attn_proj, m_w_rnn_proj, m_w_out, m_rg_wa, m_rg_wx),
        (v_w_attn_proj, v_w_rnn_proj, v_w_out, v_rg_wa, v_rg_wx),
    )

    def by_weight(k):
        b, n, s, cb, ba, bx, lam, fg = reps[k]
        ap, rp, out, wa, wx = shard_res[k]
        return [ada_res[k][None], b, n, w_in_res[k].T[None], s, conv_res[k][None], cb, wa, ba, wx, bx, lam, ap, rp, out, fg[0]]

    return (loss[0, 0], grad_x[None], *by_weight(0), *by_weight(1), *by_weight(2), *by_weight(3))
```

```python
import math

import jax
import jax.numpy as jnp
from jax import lax
from jax.experimental import pallas as pl
from jax.experimental.pallas import tpu as pltpu

F32 = jnp.float32
BF16 = jnp.bfloat16

D = 1024
HEAD_DIM = 64
ROT_DIM = 16
ROPE_THETA = 500000.0
WINDOW = 128
LRU_C = 8.0
CONV_WIDTH = 4
NORM_EPS = 1e-6
IN_W = 6656
SEG_Q, SEG_K, SEG_V, SEG_GA, SEG_XR, SEG_GR, SEG_MA, SEG_MR = 0, 1024, 1280, 1536, 2560, 3584, 4608, 5632
N_DEV = 8
W_IN_SHARD = IN_W // N_DEV
ROW_SHARD = D // N_DEV
GATE_SHARD = 256 // N_DEV

ADAM_LR = 0.001
ADAM_B1 = 0.9
ADAM_B2 = 0.999
ADAM_EPS = 1e-08
ADAM_WD = 0.01
ADAM_STEP = 10

VMEM_LIMIT = 60 * 1024 * 1024
MESH = pl.DeviceIdType.MESH


def _dot(a, b):
    return jnp.dot(a, b, preferred_element_type=F32)


def _dot_nt(a, b):
    return lax.dot_general(a, b, (((1,), (1,)), ((), ())), preferred_element_type=F32)


def _dot_tn(a, b):
    return lax.dot_general(a, b, (((0,), (0,)), ((), ())), preferred_element_type=F32)


def _sigmoid(z):
    return 1.0 / (1.0 + jnp.exp(-z))


def _params(n_grid, vmem=VMEM_LIMIT):
    return pltpu.CompilerParams(dimension_semantics=("arbitrary",) * n_grid, vmem_limit_bytes=vmem)


def _position():
    return lax.axis_index("x"), lax.axis_index("y"), lax.axis_index("c")


def _flip(pos, k):
    x, y, c = pos
    return (1 - x if k & 4 else x, 1 - y if k & 2 else y, 1 - c if k & 1 else c)


def _slot(pos):
    return 4 * pos[0] + 2 * pos[1] + pos[2]


def _all_gather_direct(name, arrs):
    n = len(arrs)

    def body(*refs):
        ins, outs = refs[:n], refs[n : 2 * n]
        send_sems, recv_sems, local_sems = refs[2 * n :]
        me = _position()
        mine = [pltpu.make_async_copy(ins[a], outs[a].at[_slot(me)], local_sems.at[a]) for a in range(n)]
        for cp in mine:
            cp.start()

        def copy(a, k, block):
            return pltpu.make_async_remote_copy(
                src_ref=ins[a],
                dst_ref=outs[a].at[_slot(block)],
                send_sem=send_sems.at[a * 7 + k - 1],
                recv_sem=recv_sems.at[a * 7 + k - 1],
                device_id=_flip(me, k),
                device_id_type=MESH,
            )

        sent = [copy(a, k, me) for a in range(n) for k in range(1, N_DEV)]
        for cp in sent:
            cp.start()
        for a in range(n):
            for k in range(1, N_DEV):
                copy(a, k, _flip(me, k)).wait_recv()
        for cp in sent:
            cp.wait_send()
        for cp in mine:
            cp.wait()

    any_spec = pl.BlockSpec(memory_space=pl.ANY)
    return pl.pallas_call(
        body,
        name=name,
        out_shape=[jax.ShapeDtypeStruct((N_DEV,) + a.shape, a.dtype) for a in arrs],
        in_specs=[any_spec] * n,
        out_specs=[any_spec] * n,
        scratch_shapes=[
            pltpu.SemaphoreType.DMA((7 * n,)),
            pltpu.SemaphoreType.DMA((7 * n,)),
            pltpu.SemaphoreType.DMA((n,)),
        ],
    )(*arrs)


def _direct_gather_copies(src_hbm, all_hbm, send_sems, recv_sems, arriving):
    me = _position()
    return [
        pltpu.make_async_remote_copy(
            src_ref=src_hbm,
            dst_ref=all_hbm.at[_slot(_flip(me, k) if arriving else me)],
            send_sem=send_sems.at[k - 1],
            recv_sem=recv_sems.at[k - 1],
            device_id=_flip(me, k),
            device_id_type=MESH,
        )
        for k in range(1, N_DEV)
    ]


CHIP_FLIPS = (2, 4, 6)


def _chip_of(pos):
    return 2 * pos[0] + pos[1]


def _chip_copies(srcs, lands, views, send_sems, recv_sems):
    me = _position()
    copies = []
    for a, (src, land) in enumerate(zip(srcs, lands)):
        for j, k in enumerate(CHIP_FLIPS):
            peer = _flip(me, k)
            copies.append(
                pltpu.make_async_remote_copy(
                    src_ref=views[a](src, peer),
                    dst_ref=land.at[j],
                    send_sem=send_sems.at[len(CHIP_FLIPS) * a + j],
                    recv_sem=recv_sems.at[len(CHIP_FLIPS) * a + j],
                    device_id=peer,
                    device_id_type=MESH,
                )
            )
    return copies


def _pair_add(name, own, recv, index, grid, chip_axis, own_spec, recv_spec, mine_spec, mine_shape):
    def body(idx_ref, own_ref, recv_ref, sb_ref, mine_ref):
        total = own_ref[...] + recv_ref[...].astype(F32)
        sb_ref[...] = total.astype(BF16)

        @pl.when(pl.program_id(chip_axis) == idx_ref[1])
        def _():
            mine_ref[...] = total

    return pl.pallas_call(
        body,
        name=name,
        grid_spec=pltpu.PrefetchScalarGridSpec(
            num_scalar_prefetch=1, grid=grid, in_specs=[own_spec, recv_spec], out_specs=[recv_spec, mine_spec]
        ),
        out_shape=[jax.ShapeDtypeStruct(recv.shape, BF16), jax.ShapeDtypeStruct(mine_shape, F32)],
        compiler_params=_params(len(grid)),
    )(index, own, recv)


PACK_GATE = 3 * ROW_SHARD
PACK_ROWS = PACK_GATE + 2 * GATE_SHARD


def _fwd_in(x, c_rows, conv_rows, w_ada, b_cols, norm_g, w_shard, tables, pack, where):
    t = x.shape[0]
    tm = min(1024, t)
    tn = IN_W // 4
    n_i, n_j = t // tm, IN_W // tn
    n_rope = (SEG_V - SEG_Q) // 128

    n_ada = w_ada.shape[1]

    def body(where_ref, x_ref, wada_ref, bcols_ref, ng_ref, cos_ref, s1_ref, s2_ref, shard_hbm, pack_hbm, c_hbm, conv_hbm,
             proj_ref, h_ref, mod_ref, wt_hbm, all_hbm, call_hbm, convall_hbm,
             wall, hbuf, cbuf, mcols, mod_all, mod_buf,
             w_send, w_recv, load_sems, local_sem, p_send, p_recv, p_local,
             c_send, c_recv, c_local, v_send, v_recv, v_local, m_send, m_recv):
        j = pl.program_id(0)
        i = pl.program_id(1)
        me = _position()
        sibling = _flip(me, 1)
        chip = where_ref[0]
        slot = where_ref[1]

        def modulation():
            own_c = pltpu.make_async_copy(c_hbm, call_hbm.at[_slot(me)], c_local)
            own_v = pltpu.make_async_copy(conv_hbm, convall_hbm.at[_slot(me)], v_local)
            own_c.start()
            own_v.start()
            for cp in _direct_gather_copies(c_hbm, call_hbm, c_send, c_recv, arriving=False):
                cp.start()
            for cp in _direct_gather_copies(conv_hbm, convall_hbm, v_send, v_recv, arriving=False):
                cp.start()
            for cp in _direct_gather_copies(c_hbm, call_hbm, c_send, c_recv, arriving=True):
                cp.wait_recv()
            own_c.wait()
            fetch = pltpu.make_async_copy(call_hbm, cbuf, c_local)
            fetch.start()
            fetch.wait()
            mcols[...] = _dot(cbuf[:, 0, :].astype(BF16), wada_ref[...].astype(BF16)) + bcols_ref[...]
            mod_all[slot] = mcols[...]
            swaps = lambda arriving: [
                pltpu.make_async_remote_copy(
                    src_ref=mcols,
                    dst_ref=mod_all.at[_slot(_flip(me, k) if arriving else me)],
                    send_sem=m_send.at[k - 1],
                    recv_sem=m_recv.at[k - 1],
                    device_id=_flip(me, k),
                    device_id_type=MESH,
                )
                for k in range(1, N_DEV)
            ]
            for cp in swaps(False):
                cp.start()
            for cp in swaps(True):
                cp.wait_recv()
            mine = lax.broadcasted_iota(jnp.int32, (N_DEV, n_ada), 0) == slot
            for s in range(N_DEV):
                row = jnp.sum(jnp.where(mine, mod_all[s], 0.0), axis=0, keepdims=True)
                mod_buf[:, n_ada * s : n_ada * (s + 1)] = row
            for cp in swaps(False):
                cp.wait_send()
            for cp in _direct_gather_copies(c_hbm, call_hbm, c_send, c_recv, arriving=False):
                cp.wait_send()
            for cp in _direct_gather_copies(conv_hbm, convall_hbm, v_send, v_recv, arriving=True):
                cp.wait_recv()
            for cp in _direct_gather_copies(conv_hbm, convall_hbm, v_send, v_recv, arriving=False):
                cp.wait_send()
            own_v.wait()

        def shard_of(block):
            return wall.at[_chip_of(block), pl.ds(W_IN_SHARD * block[2], W_IN_SHARD), :]

        def shard_copy(k, block, to, src=None):
            return pltpu.make_async_remote_copy(
                src_ref=shard_of(block),
                dst_ref=shard_of(block),
                send_sem=w_send.at[k],
                recv_sem=w_recv.at[k],
                device_id=to,
                device_id_type=MESH,
            )

        own_shard = pltpu.make_async_copy(shard_hbm, shard_of(me), local_sem)
        own_pack = pltpu.make_async_copy(pack_hbm, all_hbm.at[_slot(me)], p_local)

        def keep_tile(q):
            return [
                pltpu.make_async_copy(wall.at[q, pl.ds(W_IN_SHARD * c, W_IN_SHARD), :], wt_hbm.at[2 * q + c], load_sems.at[2 * (q ^ chip) + c])
                for c in range(2)
            ]

        @pl.when((j == 0) & (i == 0))
        def _():
            modulation()
            mod_ref[...] = mod_buf[...]
            own_shard.start()
            own_shard.wait()
            shard_copy(0, me, sibling).start()
            for rel in range(1, 4):
                shard_copy(rel, me, _flip(me, 2 * rel)).start()
            own_pack.start()
            for cp in _direct_gather_copies(pack_hbm, all_hbm, p_send, p_recv, arriving=False):
                cp.start()
            shard_copy(0, sibling, me).wait_recv()
            for cp in keep_tile(chip):
                cp.start()

        for rel in range(1, 4):

            @pl.when((j == rel) & (i == 0))
            def _(rel=rel):
                there = _flip(me, 2 * rel)
                shard_copy(rel, there, me).wait_recv()
                shard_copy(3 + rel, there, sibling).start()
                shard_copy(3 + rel, _flip(sibling, 2 * rel), me).wait_recv()
                for cp in keep_tile(chip ^ rel):
                    cp.start()

        rows = pl.ds(pl.multiple_of(i * tm, tm), tm)

        @pl.when(j == 0)
        def _():
            xv = x_ref[...]
            r = lax.rsqrt(jnp.mean(xv * xv, axis=-1, keepdims=True) + NORM_EPS)
            shift = mod_buf[:, 0:D]
            scale = mod_buf[:, D : 2 * D]
            hv = (((xv * r) * ng_ref[...]) * (1.0 + scale) + shift).astype(BF16)
            hbuf[rows, :] = hv
            h_ref[...] = hv

        proj_ref[...] = _dot_nt(hbuf[rows, :], wall[chip ^ j])

        @pl.when((chip ^ j) == 0)
        def _():
            cos, s1, s2 = cos_ref[...], s1_ref[...], s2_ref[...]
            for g in range(n_rope):
                cols = slice(128 * g, 128 * (g + 1))
                proj_ref[:, cols] = _rope(proj_ref[:, cols], cos, s1, s2)

        @pl.when((j == n_j - 1) & (i == n_i - 1))
        def _():
            shard_copy(0, me, sibling).wait_send()
            for rel in range(1, 4):
                shard_copy(rel, me, _flip(me, 2 * rel)).wait_send()
                shard_copy(3 + rel, _flip(me, 2 * rel), sibling).wait_send()
            for q in range(N_DEV // 2):
                for cp in keep_tile(q):
                    cp.wait()
            for cp in _direct_gather_copies(pack_hbm, all_hbm, p_send, p_recv, arriving=True):
                cp.wait_recv()
            for cp in _direct_gather_copies(pack_hbm, all_hbm, p_send, p_recv, arriving=False):
                cp.wait_send()
            own_pack.wait()

    tab = pl.BlockSpec((tm, 128), lambda j, i, c: (i, 0))
    any_spec = pl.BlockSpec(memory_space=pl.ANY)
    return pl.pallas_call(
        body,
        name="fwd_in",
        grid_spec=pltpu.PrefetchScalarGridSpec(
            num_scalar_prefetch=1,
            grid=(n_j, n_i),
            in_specs=[
                pl.BlockSpec((tm, D), lambda j, i, c: (jnp.where(j == 0, i, 0), 0)),
                pl.BlockSpec(w_ada.shape, lambda j, i, c: (0, 0)),
                pl.BlockSpec((1, n_ada), lambda j, i, c: (0, 0)),
                pl.BlockSpec((1, D), lambda j, i, c: (0, 0)),
                tab, tab, tab, any_spec, any_spec, any_spec, any_spec,
            ],
            out_specs=[pl.BlockSpec((tm, tn), lambda j, i, c: (i, c[0] ^ j)),
                       pl.BlockSpec((tm, D), lambda j, i, c: (jnp.where(j == 0, i, n_i - 1), 0)),
                       pl.BlockSpec((1, 3 * D), lambda j, i, c: (0, 0)),
                       any_spec, any_spec, any_spec, any_spec],
            scratch_shapes=[
                pltpu.VMEM((N_DEV // 2, tn, D), BF16),
                pltpu.VMEM((t, D), BF16),
                pltpu.VMEM((N_DEV,) + c_rows.shape, F32),
                pltpu.VMEM((N_DEV, n_ada), F32),
                pltpu.VMEM((N_DEV, N_DEV, n_ada), F32),
                pltpu.VMEM((1, 3 * D), F32),
                pltpu.SemaphoreType.DMA((N_DEV - 1,)),
                pltpu.SemaphoreType.DMA((N_DEV - 1,)),
                pltpu.SemaphoreType.DMA((N_DEV,)),
                pltpu.SemaphoreType.DMA,
                pltpu.SemaphoreType.DMA((N_DEV - 1,)),
                pltpu.SemaphoreType.DMA((N_DEV - 1,)),
                pltpu.SemaphoreType.DMA,
                pltpu.SemaphoreType.DMA((N_DEV - 1,)),
                pltpu.SemaphoreType.DMA((N_DEV - 1,)),
                pltpu.SemaphoreType.DMA,
                pltpu.SemaphoreType.DMA((N_DEV - 1,)),
                pltpu.SemaphoreType.DMA((N_DEV - 1,)),
                pltpu.SemaphoreType.DMA,
                pltpu.SemaphoreType.DMA((N_DEV - 1,)),
                pltpu.SemaphoreType.DMA((N_DEV - 1,)),
            ],
        ),
        out_shape=[jax.ShapeDtypeStruct((t, IN_W), F32), jax.ShapeDtypeStruct((t, D), BF16),
                   jax.ShapeDtypeStruct((1, 3 * D), F32),
                   jax.ShapeDtypeStruct((N_DEV,) + w_shard.shape, w_shard.dtype),
                   jax.ShapeDtypeStruct((N_DEV,) + pack.shape, pack.dtype),
                   jax.ShapeDtypeStruct((N_DEV,) + c_rows.shape, F32),
                   jax.ShapeDtypeStruct((N_DEV,) + conv_rows.shape, F32)],
        compiler_params=_params(2),
    )(where, x, w_ada, b_cols, norm_g, *tables, w_shard, pack, c_rows, conv_rows)


def _rope_tables(positions):
    half = ROT_DIM // 2
    inv_freq = ROPE_THETA ** (-jnp.arange(0, ROT_DIM, 2, dtype=F32) / ROT_DIM)
    pos = positions.astype(F32)
    lane = jnp.arange(128) % HEAD_DIM
    freq_lane = jnp.where(lane < ROT_DIM, jnp.tile(inv_freq, 128 // half), 0.0)
    ang = pos[:, None] * freq_lane[None, :]
    sin = jnp.sin(ang)
    lane_tables = (jnp.cos(ang), jnp.where(lane < half, -sin, 0.0), jnp.where((lane >= half) & (lane < ROT_DIM), sin, 0.0))
    ang_t = inv_freq[:, None] * pos[None, :]
    return lane_tables, (jnp.cos(ang_t), jnp.sin(ang_t))


def _rope(tv, cos, s1, s2):
    cols = []
    for g in range(tv.shape[1] // 128):
        blk = tv[:, 128 * g : 128 * (g + 1)]
        cols.append(blk * cos + pltpu.roll(blk, 128 - 8, 1) * s1 + pltpu.roll(blk, 8, 1) * s2)
    return cols[0] if len(cols) == 1 else jnp.concatenate(cols, axis=1)


def _rope_transposed(gv, cos, s1, s2):
    cols = []
    for g in range(gv.shape[1] // 128):
        blk = gv[:, 128 * g : 128 * (g + 1)]
        cols.append(blk * cos + pltpu.roll(blk * s1, 8, 1) + pltpu.roll(blk * s2, 128 - 8, 1))
    return cols[0] if len(cols) == 1 else jnp.concatenate(cols, axis=1)


def _lane_lo():
    return lax.broadcasted_iota(jnp.int32, (1, 128), 1) < HEAD_DIM


def _row_lo():
    return lax.broadcasted_iota(jnp.int32, (128, 1), 0) < HEAD_DIM


def _split_heads(tile, j):
    lo = _lane_lo()
    parts = []
    for g in (2 * j, 2 * j + 1):
        blk = tile[:, 128 * g : 128 * (g + 1)]
        parts += [jnp.where(lo, blk, 0.0), jnp.where(lo, 0.0, blk)]
    return jnp.concatenate(parts, axis=0)


def _dup_lanes(kv, j):
    rolled = pltpu.roll(kv, HEAD_DIM, 1)
    lo = _lane_lo()
    return jnp.where(lo, kv, rolled) if j == 0 else jnp.where(lo, rolled, kv)


def _fold_lanes(acc):
    return acc + pltpu.roll(acc, HEAD_DIM, 1)


def _fill_bias_t(bias_ref):
    kj = lax.broadcasted_iota(jnp.int32, (256, 128), 0)
    qi = lax.broadcasted_iota(jnp.int32, (256, 128), 1)
    diff = qi + WINDOW - kj
    band = (diff >= 0) & (diff < WINDOW)
    bias_ref[0] = jnp.where(band, 0.0, -1e30)
    bias_ref[1] = jnp.where(band & (kj >= 128), 0.0, -1e30)


def _sink_row(sink_ref, pair, j):
    return jnp.concatenate([jnp.full((1, 128), sink_ref[0, 8 * pair + 4 * j + h], F32) for h in range(4)], axis=1)


def _softmax_t(q4, kw, bias4, sink):
    s = _dot_nt(kw, q4) + bias4
    m = jnp.maximum(jnp.max(s, axis=0, keepdims=True), sink)
    p = jnp.exp(s - m)
    e_sink = jnp.exp(sink - m)
    denom = jnp.sum(p, axis=0, keepdims=True) + e_sink
    return p, e_sink, denom


def _attn_specs(t, tq):
    return dict(
        q=pl.BlockSpec((tq, 512), lambda p, i: (i, p)),
        k=pl.BlockSpec((t, 128), lambda p, i: (0, SEG_K // 128 + p)),
        v=pl.BlockSpec((t, 128), lambda p, i: (0, SEG_V // 128 + p)),
        sink=pl.BlockSpec(memory_space=pltpu.SMEM),
        act=pl.BlockSpec((tq, 512), lambda p, i: (i, p)),
    )


def _attn_fwd(proj, sinks):
    t = proj.shape[0]
    tq = min(512, t)
    nb = tq // 128
    n_tiles = t // tq
    sp = _attn_specs(t, tq)

    def body(q_ref, k_ref, v_ref, sink_ref, o_ref, kd, vdt, bias_ref):
        pair = pl.program_id(0)
        i = pl.program_id(1)

        @pl.when(i == 0)
        def _():
            kv, vv = k_ref[...], v_ref[...]
            for j in range(2):
                kd[j, 0:128, :] = jnp.zeros((128, 128), BF16)
                kd[j, 128:, :] = _dup_lanes(kv, j).astype(BF16)
                vdt[j, :, 0:128] = jnp.zeros((128, 128), BF16)
                vdt[j, :, 128:] = _dup_lanes(vv, j).T.astype(BF16)
            _fill_bias_t(bias_ref)

        row_lo = _row_lo()

        def block(b, carry):
            r0 = pl.multiple_of(b * 128, 128)
            gb = i * nb + b
            keys = pl.ds(pl.multiple_of(gb * 128, 128), 256)
            rows = pl.ds(r0, 128)
            qt = q_ref[rows, :] * (1.0 / math.sqrt(HEAD_DIM))
            bias = bias_ref[jnp.where(gb == 0, 1, 0)]
            bias4 = jnp.concatenate([bias] * 4, axis=1)
            cols = []
            for j in range(2):
                q4 = _split_heads(qt, j).astype(BF16)
                p, _, denom = _softmax_t(q4, kd[j, keys, :], bias4, _sink_row(sink_ref, pair, j))
                ot = _dot(vdt[j, :, keys], p.astype(BF16)) * (1.0 / denom)
                for gi in range(2):
                    z = jnp.where(row_lo, ot[:, 256 * gi : 256 * gi + 128], ot[:, 256 * gi + 128 : 256 * gi + 256])
                    cols.append(z.T)
            o_ref[rows, :] = jnp.concatenate(cols, axis=1).astype(BF16)
            return carry

        lax.fori_loop(0, nb, block, 0, unroll=True)

    return pl.pallas_call(
        body,
        name="attn_fwd",
        grid=(2, n_tiles),
        in_specs=[sp["q"], sp["k"], sp["v"], sp["sink"]],
        out_specs=sp["act"],
        out_shape=jax.ShapeDtypeStruct((t, D), BF16),
        scratch_shapes=[pltpu.VMEM((2, t + 128, 128), BF16), pltpu.VMEM((2, 128, t + 128), BF16), pltpu.VMEM((2, 256, 128), F32)],
        compiler_params=_params(2),
    )(proj, proj, proj, sinks)


def _attn_bwd(proj, o, d_o, tables, sinks, part_3):
    t = proj.shape[0]
    tq = min(512, t)
    nb = tq // 128
    n_tiles = t // tq
    (cos, s1, s2), (cos_t, sin_t) = tables
    sp = _attn_specs(t, tq)
    scale = 1.0 / math.sqrt(HEAD_DIM)
    half = ROT_DIM // 2

    def unrope_rows(z, c, s):
        out = []
        for base in (0, HEAD_DIM):
            g0, g1 = z[base : base + half], z[base + half : base + ROT_DIM]
            out += [g0 * c + g1 * s, g1 * c - g0 * s, z[base + ROT_DIM : base + HEAD_DIM]]
        return jnp.concatenate(out, axis=0)

    def swap(part_hbm, recv_hbm, send_sem, recv_sem):
        sibling = _flip(_position(), 1)
        return pltpu.make_async_remote_copy(
            src_ref=part_hbm.at[:, :, sibling[2]], dst_ref=recv_hbm, send_sem=send_sem, recv_sem=recv_sem,
            device_id=sibling, device_id_type=MESH,
        )

    def body(q_ref, k_ref, v_ref, o_ref, do_ref, ct_ref, st_ref, ck, s1k, s2k, sink_ref, part_hbm,
             dq_ref, dk_ref, dv_ref, dsink_ref, recv_hbm, kd, kdt, vd, bias_ref, dkacc, dvacc, send_sem, recv_sem):
        pair = pl.program_id(0)
        i = pl.program_id(1)

        @pl.when((pair == 0) & (i == 0))
        def _():
            swap(part_hbm, recv_hbm, send_sem, recv_sem).start()

        @pl.when((pair == 1) & (i == n_tiles - 1))
        def _():
            swap(part_hbm, recv_hbm, send_sem, recv_sem).wait()

        @pl.when(i == 0)
        def _():
            kv, vv = k_ref[...], v_ref[...]
            for j in range(2):
                kdup = _dup_lanes(kv, j)
                kd[j, 0:128, :] = jnp.zeros((128, 128), BF16)
                kd[j, 128:, :] = kdup.astype(BF16)
                kdt[j, :, 0:128] = jnp.zeros((128, 128), BF16)
                kdt[j, :, 128:] = kdup.T.astype(BF16)
                vd[j, 0:128, :] = jnp.zeros((128, 128), BF16)
                vd[j, 128:, :] = _dup_lanes(vv, j).astype(BF16)
            _fill_bias_t(bias_ref)
            dkacc[...] = jnp.zeros_like(dkacc)
            dvacc[...] = jnp.zeros_like(dvacc)
            dsink_ref[...] = jnp.zeros_like(dsink_ref)

        row_lo = _row_lo()
        ones = jnp.ones((8, 128), BF16)

        def block(b, carry):
            r0 = pl.multiple_of(b * 128, 128)
            gb = i * nb + b
            keys = pl.ds(pl.multiple_of(gb * 128, 128), 256)
            rows = pl.ds(r0, 128)
            qt = q_ref[rows, :] * scale
            dob = do_ref[rows, :].astype(F32)
            prod = dob * o_ref[rows, :].astype(F32)
            ct, st = ct_ref[:, rows], st_ref[:, rows]
            bias = bias_ref[jnp.where(gb == 0, 1, 0)]
            bias4 = jnp.concatenate([bias] * 4, axis=1)
            cols = []
            for j in range(2):
                q4 = _split_heads(qt, j).astype(BF16)
                kw = kd[j, keys, :]
                p, e_sink, denom = _softmax_t(q4, kw, bias4, _sink_row(sink_ref, pair, j))
                inv = 1.0 / denom
                prob = p * inv
                do4 = _split_heads(dob, j).astype(BF16)
                delta = _dot_nt(ones, _split_heads(prod, j).astype(BF16))[0:1, :]
                dp = _dot_nt(vd[j, keys, :], do4)
                ds = (prob * (dp - delta)).astype(BF16)
                dkacc[j, keys, :] = dkacc[j, keys, :] + _dot(ds, q4)
                dvacc[j, keys, :] = dvacc[j, keys, :] + _dot(prob.astype(BF16), do4)
                dqt = _dot(kdt[j, :, keys], ds) * scale
                dsink = -(e_sink * inv) * delta
                for h in range(4):
                    row = 4 * j + h
                    total = jnp.sum(dsink[:, 128 * h : 128 * (h + 1)], axis=1, keepdims=True)
                    dsink_ref[0, row : row + 1, :] = dsink_ref[0, row : row + 1, :] + jnp.broadcast_to(total, (1, 128))
                for gi in range(2):
                    z = jnp.where(row_lo, dqt[:, 256 * gi : 256 * gi + 128], dqt[:, 256 * gi + 128 : 256 * gi + 256])
                    cols.append(unrope_rows(z, ct, st).T)
            dq_ref[rows, :] = jnp.concatenate(cols, axis=1).astype(BF16)
            return carry

        lax.fori_loop(0, nb, block, 0, unroll=True)

        @pl.when(i == n_tiles - 1)
        def _():
            lo = _lane_lo()
            dk = jnp.where(lo, _fold_lanes(dkacc[0, 128:, :]), _fold_lanes(dkacc[1, 128:, :]))
            dk_ref[...] = _rope_transposed(dk, ck[...], s1k[...], s2k[...]).astype(BF16)
            dv_ref[...] = jnp.where(lo, _fold_lanes(dvacc[0, 128:, :]), _fold_lanes(dvacc[1, 128:, :])).astype(BF16)

    kv_out = pl.BlockSpec((t, 128), lambda p, i: (0, p))
    tab_t = pl.BlockSpec((half, tq), lambda p, i: (0, i))
    tab_k = pl.BlockSpec((t, 128), lambda p, i: (0, 0))
    return pl.pallas_call(
        body,
        name="attn_bwd",
        grid=(2, n_tiles),
        in_specs=[sp["q"], sp["k"], sp["v"], sp["act"], sp["act"], tab_t, tab_t, tab_k, tab_k, tab_k, sp["sink"],
                  pl.BlockSpec(memory_space=pl.ANY)],
        out_specs=[sp["act"], kv_out, kv_out, pl.BlockSpec((1, 8, 128), lambda p, i: (p, 0, 0)), pl.BlockSpec(memory_space=pl.ANY)],
        out_shape=[
            jax.ShapeDtypeStruct((t, D), BF16),
            jax.ShapeDtypeStruct((t, 256), BF16),
            jax.ShapeDtypeStruct((t, 256), BF16),
            jax.ShapeDtypeStruct((2, 8, 128), F32),
            jax.ShapeDtypeStruct((3, 4, ROW_SHARD, D), F32),
        ],
        scratch_shapes=[pltpu.VMEM((2, t + 128, 128), BF16), pltpu.VMEM((2, 128, t + 128), BF16), pltpu.VMEM((2, t + 128, 128), BF16),
                        pltpu.VMEM((2, 256, 128), F32), pltpu.VMEM((2, t + 128, 128), F32), pltpu.VMEM((2, t + 128, 128), F32),
                        pltpu.SemaphoreType.DMA, pltpu.SemaphoreType.DMA],
        compiler_params=_params(2),
    )(proj, proj, proj, o, d_o, cos_t, sin_t, cos, s1, s2, sinks, part_3)


HALF = 512


def _softplus(y):
    u = jnp.exp(-jnp.abs(y))
    w = 1.0 + u
    log1p = jnp.where(w == 1.0, u, jnp.log(w) * (u / jnp.where(w == 1.0, 1.0, w - 1.0)))
    return jnp.maximum(y, 0.0) + log1p


def _gate_block(w_ref, blk):
    return w_ref[:, :, 256 * blk : 256 * (blk + 1)].reshape(256, 256)


def _gate_matmul(act_bf, w_ref):
    return jnp.concatenate([_dot(act_bf[:, 256 * blk : 256 * (blk + 1)], _gate_block(w_ref, blk)) for blk in range(2)], axis=1)


def _gate_matmul_t(grad_bf, w_ref):
    return jnp.concatenate([_dot_nt(grad_bf[:, 256 * blk : 256 * (blk + 1)], _gate_block(w_ref, blk)) for blk in range(2)], axis=1)


def _conv_and_gates(xe_ref, tm, cw_ref, cb_ref, wa_ref, wx_ref, ba_ref, bx_ref, lam_ref, keep):
    xc = cb_ref[...] + sum(cw_ref[k : k + 1, :] * xe_ref[pl.ds(5 + k, tm), :] for k in range(CONV_WIDTH))
    xc_bf = xc.astype(BF16)
    r = _sigmoid(_gate_matmul(xc_bf, wa_ref) + ba_ref[...])
    ig = _sigmoid(_gate_matmul(xc_bf, wx_ref) + bx_ref[...])
    sp = _softplus(-lam_ref[...])
    log_a = -LRU_C * r * sp
    tanh = jnp.tanh(-log_a)
    mult = jnp.sqrt(2.0 * tanh / (1.0 + tanh))
    a = jnp.where(keep, jnp.exp(log_a), 0.0)
    mult = jnp.where(keep, mult, 1.0)
    return xc, xc_bf, r, ig, sp, a, mult


def _scan_rows(a_ref, b_ref, h_ref, carry_ref, tm, reverse):
    width = a_ref.shape[1]
    row = lax.broadcasted_iota(jnp.int32, (8, width), 0)

    def group(g, carry):
        gg = tm // 8 - 1 - g if reverse else g
        r0 = pl.multiple_of(gg * 8, 8)
        a = a_ref[pl.ds(r0, 8), :]
        b = b_ref[pl.ds(r0, 8), :]
        for s in (1, 2, 4):
            shift = 8 - s if reverse else s
            a_sh = pltpu.roll(a, shift, 0)
            b_sh = pltpu.roll(b, shift, 0)
            ok = (row < 8 - s) if reverse else (row >= s)
            b = jnp.where(ok, a * b_sh + b, b)
            a = jnp.where(ok, a * a_sh, a)
        h = a * carry + b
        h_ref[pl.ds(r0, 8), :] = h
        last = h[0:1, :] if reverse else h[7:8, :]
        return jnp.broadcast_to(last, (8, width))

    carry_ref[...] = lax.fori_loop(0, tm // 8, group, carry_ref[...])


def _rnn_specs(t, tm, order):
    n = t // tm
    return dict(
        tile=pl.BlockSpec((tm, HALF), lambda ch, i: (order(i, n), ch)),
        before=lambda col0: pl.BlockSpec((8, HALF), lambda ch, i: (jnp.maximum(order(i, n) * (tm // 8) - 1, 0), col0 + ch)),
        proj_tile=lambda col0: pl.BlockSpec((tm, HALF), lambda ch, i: (order(i, n), col0 + ch)),
        keep=pl.BlockSpec((tm, 1), lambda ch, i: (order(i, n), 0)),
        gates=lambda which: pl.BlockSpec((N_DEV, GATE_SHARD, HALF), lambda ch, i: (0, PACK_GATE // GATE_SHARD + which, ch)),
        gate_grads=pl.BlockSpec((2, 2, 256, 256), lambda ch, i: (0, ch, 0, 0)),
        vec=pl.BlockSpec((1, HALF), lambda ch, i: (0, ch)),
        conv=pl.BlockSpec((CONV_WIDTH, HALF), lambda ch, i: (0, ch)),
    )


def _rnn_fwd(proj, keep, w_all, rg_ba, rg_bx, rg_lambda, conv_w, conv_b):
    t = proj.shape[0]
    tm = min(512, t)
    sp = _rnn_specs(t, tm, lambda i, n: i)

    def body(xr_ref, xb_ref, keep_ref, wa_ref, wx_ref, ba_ref, bx_ref, lam_ref, cw_ref, cb_ref, hh_ref, xe, a_buf, b_buf, carry):
        i = pl.program_id(1)

        @pl.when(i == 0)
        def _():
            carry[...] = jnp.zeros_like(carry)

        xe[0:8, :] = jnp.where(i > 0, xb_ref[...], 0.0)
        xe[8:, :] = xr_ref[...]
        keep_rows = keep_ref[...] != 0.0
        xc, _, _, ig, _, a, mult = _conv_and_gates(xe, tm, cw_ref, cb_ref, wa_ref, wx_ref, ba_ref, bx_ref, lam_ref, keep_rows)
        a_buf[...] = a
        b_buf[...] = mult * (ig * xc)
        _scan_rows(a_buf, b_buf, hh_ref, carry, tm, reverse=False)

    return pl.pallas_call(
        body,
        name="rnn_fwd",
        grid=(2, t // tm),
        in_specs=[sp["proj_tile"](SEG_XR // HALF), sp["before"](SEG_XR // HALF), sp["keep"], sp["gates"](0), sp["gates"](1),
                  sp["vec"], sp["vec"], sp["vec"], sp["conv"], sp["vec"]],
        out_specs=sp["tile"],
        out_shape=jax.ShapeDtypeStruct((t, D), F32),
        scratch_shapes=[pltpu.VMEM((tm + 8, HALF), F32), pltpu.VMEM((tm, HALF), F32), pltpu.VMEM((tm, HALF), F32), pltpu.VMEM((8, HALF), F32)],
        compiler_params=_params(2),
    )(proj, proj, keep, w_all, w_all, rg_ba, rg_bx, rg_lambda, conv_w, conv_b)


def _rnn_bwd(proj, hh, d_hh, keep, w_all, rg_ba, rg_bx, rg_lambda, conv_w, conv_b, sum_3_bf):
    t = proj.shape[0]
    tm = min(512, t)
    n_tiles = t // tm
    sp = _rnn_specs(t, tm, lambda i, n: n - 1 - i)

    def body(xr_ref, xb_ref, hh_ref, hb_ref, dhh_ref, keep_ref, wa_ref, wx_ref, ba_ref, bx_ref, lam_ref, cw_ref, cb_ref,
             s3_hbm, dxr_ref, gw_ref, gvec_ref, l3_hbm, xe, he, ae, c_buf, g_buf, dxe, carry, a_next, send_sems, recv_sems):
        ch = pl.program_id(0)
        i = pl.program_id(1)
        first_tile = i == n_tiles - 1
        views = [lambda ref, peer: ref.at[:, _chip_of(peer)]]

        @pl.when((ch == 0) & (i == 0))
        def _():
            for cp in _chip_copies([s3_hbm], [l3_hbm], views, send_sems, recv_sems):
                cp.start()

        @pl.when((ch == 1) & (i == n_tiles - 1))
        def _():
            for cp in _chip_copies([s3_hbm], [l3_hbm], views, send_sems, recv_sems):
                cp.wait_recv()
                cp.wait_send()

        @pl.when(i == 0)
        def _():
            carry[...] = jnp.zeros_like(carry)
            a_next[...] = jnp.zeros_like(a_next)
            dxe[tm:, :] = jnp.zeros((8, HALF), F32)
            gw_ref[...] = jnp.zeros_like(gw_ref)
            gvec_ref[...] = jnp.zeros_like(gvec_ref)

        xe[0:8, :] = jnp.where(first_tile, 0.0, xb_ref[...])
        xe[8:, :] = xr_ref[...]
        he[0:8, :] = jnp.where(first_tile, 0.0, hb_ref[...])
        he[8:, :] = hh_ref[...]
        keep_rows = keep_ref[...] != 0.0
        xc, xc_bf, r, ig, sp_, a, mult = _conv_and_gates(xe, tm, cw_ref, cb_ref, wa_ref, wx_ref, ba_ref, bx_ref, lam_ref, keep_rows)

        ae[0:tm, :] = a
        ae[tm:, :] = a_next[...]
        a_next[...] = jnp.broadcast_to(a[0:1, :], (8, HALF))
        c_buf[...] = ae[pl.ds(1, tm), :]
        g_buf[...] = dhh_ref[...]
        _scan_rows(c_buf, g_buf, g_buf, carry, tm, reverse=True)
        g = g_buf[...]

        h_prev = he[pl.ds(7, tm), :]
        bb = ig * xc
        d_ig = g * mult * xc
        d_xc = g * mult * ig
        d_log_a = jnp.where(keep_rows, g * h_prev * a - (g * bb) * (a * a) / mult, 0.0)
        d_r = d_log_a * (-LRU_C * sp_)
        d_sp = jnp.sum(d_log_a * (-LRU_C * r), axis=0, keepdims=True)
        d_zr = d_r * r * (1.0 - r)
        d_zi = d_ig * ig * (1.0 - ig)
        d_zr_bf = d_zr.astype(BF16)
        d_zi_bf = d_zi.astype(BF16)
        d_xc = d_xc + _gate_matmul_t(d_zr_bf, wa_ref) + _gate_matmul_t(d_zi_bf, wx_ref)
        for blk in range(2):
            cols = slice(256 * blk, 256 * (blk + 1))
            gw_ref[0, blk] = gw_ref[0, blk] + _dot_tn(xc_bf[:, cols], d_zr_bf[:, cols])
            gw_ref[1, blk] = gw_ref[1, blk] + _dot_tn(xc_bf[:, cols], d_zi_bf[:, cols])

        dxe[0:tm, :] = d_xc
        shifted = [dxe[pl.ds(3 - k, tm), :] for k in range(CONV_WIDTH)]
        dxr = sum(cw_ref[k : k + 1, :] * shifted[k] for k in range(CONV_WIDTH))
        dxr_ref[...] = dxr.astype(BF16)
        dxe[tm:, :] = d_xc[0:8, :]

        lam = lam_ref[...]
        xr_tile = xr_ref[...]
        sums = [
            jnp.sum(d_zr, axis=0, keepdims=True),
            jnp.sum(d_zi, axis=0, keepdims=True),
            d_sp * (-_sigmoid(-lam)),
            jnp.sum(d_xc, axis=0, keepdims=True),
        ] + [jnp.sum(xr_tile * shifted[k], axis=0, keepdims=True) for k in range(CONV_WIDTH)]
        gvec_ref[...] = gvec_ref[...] + jnp.concatenate(sums, axis=0)

    return pl.pallas_call(
        body,
        name="rnn_bwd",
        grid=(2, n_tiles),
        in_specs=[
            sp["proj_tile"](SEG_XR // HALF), sp["before"](SEG_XR // HALF), sp["tile"], sp["before"](0), sp["tile"], sp["keep"],
            sp["gates"](0), sp["gates"](1), sp["vec"], sp["vec"], sp["vec"], sp["conv"], sp["vec"], pl.BlockSpec(memory_space=pl.ANY),
        ],
        out_specs=[sp["tile"], sp["gate_grads"], pl.BlockSpec((8, HALF), lambda ch, i: (0, ch)), pl.BlockSpec(memory_space=pl.ANY)],
        out_shape=[
            jax.ShapeDtypeStruct((t, D), BF16),
            jax.ShapeDtypeStruct((2, 4, 256, 256), F32),
            jax.ShapeDtypeStruct((8, D), F32),
            jax.ShapeDtypeStruct((len(CHIP_FLIPS), 3, ROW_SHARD, D), BF16),
        ],
        scratch_shapes=[
            pltpu.VMEM((tm + 8, HALF), F32),
            pltpu.VMEM((tm + 8, HALF), F32),
            pltpu.VMEM((tm + 8, HALF), F32),
            pltpu.VMEM((tm, HALF), F32),
            pltpu.VMEM((tm, HALF), F32),
            pltpu.VMEM((tm + 8, HALF), F32),
            pltpu.VMEM((8, HALF), F32),
            pltpu.VMEM((8, HALF), F32),
            pltpu.SemaphoreType.DMA((len(CHIP_FLIPS),)),
            pltpu.SemaphoreType.DMA((len(CHIP_FLIPS),)),
        ],
        compiler_params=_params(2),
    )(proj, proj, hh, hh, d_hh, keep, w_all, w_all, rg_ba, rg_bx, rg_lambda, conv_w, conv_b, sum_3_bf)


def _mid(x, target, o, hh, proj, w_all, gate, final_g):
    t = x.shape[0]
    tm = min(256, t)
    n_tiles = t // tm
    full = pl.BlockSpec((tm, D), lambda i: (i, 0))
    half = lambda col: pl.BlockSpec((tm, 512), lambda i: (i, col))
    vec = pl.BlockSpec((1, D), lambda i: (0, 0))
    any_spec = pl.BlockSpec(memory_space=pl.ANY)

    def body(x_ref, tg_ref, o_ref, hh_ref, ga0, ga1, gr0, gr1, ma0, ma1, mr0, mr1, w_hbm, gate_ref, fg_ref,
             dres_ref, do_ref, dhh_ref, dga_ref, dgr_ref, dma_ref, dmr_ref, gw_hbm, vec_ref, w, gw, sem, w_sems):
        i = pl.program_id(0)

        @pl.when(i == 0)
        def _():
            loads = [
                pltpu.make_async_copy(
                    w_hbm.at[s, pl.ds(ROW_SHARD * k, ROW_SHARD), :], w.at[k, pl.ds(ROW_SHARD * s, ROW_SHARD), :], w_sems.at[k * N_DEV + s]
                )
                for k in range(3)
                for s in range(N_DEV)
            ]
            for cp in loads:
                cp.start()
            for cp in loads:
                cp.wait()
            gw[...] = jnp.zeros_like(gw)
            vec_ref[...] = jnp.zeros_like(vec_ref)

        cat = lambda a, b: jnp.concatenate([a[...], b[...]], axis=1)
        ga, gr, ma, mr = cat(ga0, ga1), cat(gr0, gr1), cat(ma0, ma1), cat(mr0, mr1)
        ov, hv, gate_v, fg = o_ref[...].astype(F32), hh_ref[...], gate_ref[...], fg_ref[...]
        sg_a = _sigmoid(ga)
        sg_r = _sigmoid(gr)
        silu_a = ga * sg_a
        silu_r = gr * sg_r
        ya = (ov * silu_a).astype(BF16)
        yr = (hv * silu_r).astype(BF16)
        pa = _dot(ya, w[0])
        pr = _dot(yr, w[1])
        sa = _sigmoid(ma)
        sr = _sigmoid(mr)
        merged = (sa * pa + sr * pr).astype(BF16)
        mo = _dot(merged, w[2])
        out = x_ref[...] + gate_v * mo
        r2 = lax.rsqrt(jnp.mean(out * out, axis=-1, keepdims=True) + NORM_EPS)
        outn = out * r2
        err = outn * fg - tg_ref[...]
        loss = 0.5 * jnp.sum(jnp.mean(err * err, axis=-1, keepdims=True), axis=0, keepdims=True)
        dy = err * (1.0 / D)
        g_final = jnp.sum(dy * outn, axis=0, keepdims=True)
        doutn = dy * fg
        dout = r2 * (doutn - outn * jnp.mean(doutn * outn, axis=-1, keepdims=True))
        dres_ref[...] = dout
        d_gate = jnp.sum(dout * mo, axis=0, keepdims=True)
        dmo = (dout * gate_v).astype(BF16)
        dmerged = _dot_nt(dmo, w[2])
        gw[2] = gw[2] + _dot_tn(merged, dmo)
        dpa = (dmerged * sa).astype(BF16)
        dpr = (dmerged * sr).astype(BF16)
        dma_ref[...] = (dmerged * pa * (sa * (1.0 - sa))).astype(BF16)
        dmr_ref[...] = (dmerged * pr * (sr * (1.0 - sr))).astype(BF16)
        dya = _dot_nt(dpa, w[0])
        dyr = _dot_nt(dpr, w[1])
        gw[0] = gw[0] + _dot_tn(ya, dpa)
        gw[1] = gw[1] + _dot_tn(yr, dpr)
        do_ref[...] = (dya * silu_a).astype(BF16)
        dhh_ref[...] = dyr * silu_r
        dga_ref[...] = (dya * ov * (sg_a * (1.0 + ga * (1.0 - sg_a)))).astype(BF16)
        dgr_ref[...] = (dyr * hv * (sg_r * (1.0 + gr * (1.0 - sg_r)))).astype(BF16)
        zeros = jnp.zeros((5, D), F32)
        vec_ref[...] = vec_ref[...] + jnp.concatenate([g_final, d_gate, jnp.broadcast_to(loss, (1, D)), zeros], axis=0)

        @pl.when(i == n_tiles - 1)
        def _():
            cp = pltpu.make_async_copy(gw, gw_hbm, sem)
            cp.start()
            cp.wait()

    bf_tile = jax.ShapeDtypeStruct((t, D), BF16)
    f_tile = jax.ShapeDtypeStruct((t, D), F32)
    return pl.pallas_call(
        body,
        name="mid",
        grid=(n_tiles,),
        in_specs=[full, full, full, full,
                  half(SEG_GA // 512), half(SEG_GA // 512 + 1), half(SEG_GR // 512), half(SEG_GR // 512 + 1),
                  half(SEG_MA // 512), half(SEG_MA // 512 + 1), half(SEG_MR // 512), half(SEG_MR // 512 + 1),
                  any_spec, vec, vec],
        out_specs=[full, full, full, full, full, full, full, any_spec, pl.BlockSpec((8, D), lambda i: (0, 0))],
        out_shape=[f_tile, bf_tile, f_tile, bf_tile, bf_tile, bf_tile, bf_tile,
                   jax.ShapeDtypeStruct((3, D, D), F32), jax.ShapeDtypeStruct((8, D), F32)],
        scratch_shapes=[pltpu.VMEM((3, D, D), BF16), pltpu.VMEM((3, D, D), F32), pltpu.SemaphoreType.DMA, pltpu.SemaphoreType.DMA((3 * N_DEV,))],
        compiler_params=_params(1),
    )(x, target, o, hh, proj, proj, proj, proj, proj, proj, proj, proj, w_all, gate, final_g)


_SEGMENTS = ((SEG_Q, 1024), (SEG_K, 256), (SEG_V, 256), (SEG_GA, 1024), (SEG_XR, 1024), (SEG_GR, 1024), (SEG_MA, 1024), (SEG_MR, 1024))


def _seg_specs(tm):
    return [pl.BlockSpec((tm, width), lambda i: (i, 0)) for _, width in _SEGMENTS]


def _bwd_dh(dsegs, w_all, x, dres, mod, norm_g):
    t = x.shape[0]
    tm = min(512, t)
    n_tiles = t // tm
    n_seg = len(_SEGMENTS)
    full = pl.BlockSpec((tm, D), lambda i: (i, 0))

    def body(*refs):
        segs = refs[:n_seg]
        w_hbm, x_ref, dres_ref, mod_ref, ng_ref, gx_ref, vec_ref, w, sems = refs[n_seg:]
        i = pl.program_id(0)

        @pl.when(i == 0)
        def _():
            loads = [
                pltpu.make_async_copy(w_hbm.at[s], w.at[pl.ds(W_IN_SHARD * s, W_IN_SHARD), :], sems.at[s])
                for s in range(N_DEV)
            ]
            for cp in loads:
                cp.start()
            for cp in loads:
                cp.wait()
            vec_ref[...] = jnp.zeros_like(vec_ref)

        dh = sum(_dot(seg[...], w[start : start + width, :]) for seg, (start, width) in zip(segs, _SEGMENTS))
        xv = x_ref[...]
        r1 = lax.rsqrt(jnp.mean(xv * xv, axis=-1, keepdims=True) + NORM_EPS)
        xn = xv * r1
        ng = ng_ref[...]
        scale1 = 1.0 + mod_ref[:, D : 2 * D]
        d_shift = jnp.sum(dh, axis=0, keepdims=True)
        d_scale = jnp.sum(dh * (xn * ng), axis=0, keepdims=True)
        g_norm = jnp.sum(dh * xn * scale1, axis=0, keepdims=True)
        dxn = dh * (ng * scale1)
        dx = r1 * (dxn - xn * jnp.mean(dxn * xn, axis=-1, keepdims=True))
        gx_ref[...] = dres_ref[...] + dx
        vec_ref[...] = vec_ref[...] + jnp.concatenate([d_shift, d_scale, g_norm, jnp.zeros((5, D), F32)], axis=0)

    any_spec = pl.BlockSpec(memory_space=pl.ANY)
    return pl.pallas_call(
        body,
        name="bwd_dh",
        grid=(n_tiles,),
        in_specs=_seg_specs(tm) + [any_spec, full, full, pl.BlockSpec((1, 3 * D), lambda i: (0, 0)), pl.BlockSpec((1, D), lambda i: (0, 0))],
        out_specs=[full, pl.BlockSpec((8, D), lambda i: (0, 0))],
        out_shape=[jax.ShapeDtypeStruct((t, D), F32), jax.ShapeDtypeStruct((8, D), F32)],
        scratch_shapes=[pltpu.VMEM((IN_W, D), BF16), pltpu.SemaphoreType.DMA((N_DEV,))],
        compiler_params=_params(1),
    )(*dsegs, w_all, x, dres, mod, norm_g)


_HBM = pl.BlockSpec(memory_space=pltpu.HBM)
_SEM = pl.BlockSpec(memory_space=pltpu.SEMAPHORE)
_EFFECT = pltpu.SideEffectType.DATAFLOW_SIDE_EFFECTING


def _w_exchange_copies(sum_ref, land_ref, sems):
    me = _position()
    n = len(CHIP_FLIPS)
    return [
        pltpu.make_async_remote_copy(
            src_ref=sum_ref.at[_chip_of(_flip(me, k))], dst_ref=land_ref.at[j], send_sem=sems[j], recv_sem=sems[n + j],
            device_id=_flip(me, k), device_id_type=MESH,
        )
        for j, k in enumerate(CHIP_FLIPS)
    ]


def _w_exchange_start(sum_w_bf):
    n_sems = 2 * len(CHIP_FLIPS)
    land = lax.empty((len(CHIP_FLIPS), W_IN_SHARD, D), BF16)

    def body(sum_ref, land_ref, *outs):
        for cp in _w_exchange_copies(sum_ref, land_ref, outs[:n_sems]):
            cp.start()
        token = outs[n_sems + 2]
        token[...] = jnp.zeros_like(token)

    res = pl.pallas_call(
        body,
        name="w_exchange_start",
        out_shape=(*[pltpu.SemaphoreType.DMA(())] * n_sems, pltpu.HBM(sum_w_bf.shape, BF16), pltpu.HBM(land.shape, BF16),
                   jax.ShapeDtypeStruct((8, 128), F32)),
        in_specs=(_HBM, _HBM),
        out_specs=(*[_SEM] * n_sems, _HBM, _HBM, pl.BlockSpec(memory_space=pltpu.VMEM)),
        input_output_aliases={0: n_sems, 1: n_sems + 1},
        compiler_params=pltpu.CompilerParams(has_side_effects=_EFFECT),
    )(pltpu.with_memory_space_constraint(sum_w_bf, pltpu.HBM), pltpu.with_memory_space_constraint(land, pltpu.HBM))
    return res[:n_sems], res[n_sems], res[n_sems + 1], res[n_sems + 2]


def _w_exchange_wait(sems, sum_thru, land_thru, after):
    n_sems = len(sems)

    def body(sum_ref, land_ref, *rest):
        for cp in _w_exchange_copies(sum_ref, land_ref, rest[:n_sems]):
            cp.wait_send()
            cp.wait_recv()

    return pl.pallas_call(
        body,
        name="w_exchange_wait",
        out_shape=(pltpu.HBM(sum_thru.shape, BF16), pltpu.HBM(land_thru.shape, BF16)),
        in_specs=(_HBM, _HBM, *[_SEM] * n_sems, pl.BlockSpec(memory_space=pl.ANY)),
        out_specs=(_HBM, _HBM),
        input_output_aliases={0: 0, 1: 1},
        compiler_params=pltpu.CompilerParams(has_side_effects=_EFFECT),
    )(sum_thru, land_thru, *sems, after)[1]


def _bwd_gw(dsegs, h, g_gates):
    t = h.shape[0]
    tm = min(512, t)
    n_tiles = t // tm
    n_seg = len(_SEGMENTS)
    n_chips = N_DEV // 2

    def gate_copies(g_hbm, lg_hbm, send_sems, recv_sems, arriving):
        me = _position()
        return [
            pltpu.make_async_remote_copy(
                src_ref=g_hbm.at[:, :, _slot(_flip(me, k))],
                dst_ref=lg_hbm.at[_slot(_flip(me, k) if arriving else me)],
                send_sem=send_sems.at[k - 1],
                recv_sem=recv_sems.at[k - 1],
                device_id=_flip(me, k),
                device_id_type=MESH,
            )
            for k in range(1, N_DEV)
        ]

    def body(*refs):
        segs = refs[:n_seg]
        (h_ref, g_hbm, own_hbm, recv_hbm, lg_hbm, acc, stage, own_sems, pair_send, pair_recv, send_sems, recv_sems, g_local) = refs[n_seg:]
        i = pl.program_id(0)
        slot = _slot(_position())
        own_gates = pltpu.make_async_copy(g_hbm.at[:, :, slot], lg_hbm.at[slot], g_local)

        @pl.when(i == 0)
        def _():
            own_gates.start()
            for cp in gate_copies(g_hbm, lg_hbm, send_sems, recv_sems, arriving=False):
                cp.start()
            acc[...] = jnp.zeros_like(acc)

        hv = h_ref[...]
        for seg, (start, width) in zip(segs, _SEGMENTS):
            acc[start : start + width, :] = acc[start : start + width, :] + _dot_tn(seg[...], hv)

        @pl.when(i == n_tiles - 1)
        def _():
            me = _position()
            sibling = _flip(me, 1)

            def rows(q, c):
                return pl.ds(pl.multiple_of(W_IN_SHARD * (2 * q + c), 8), W_IN_SHARD)

            def to_sibling(q):
                return pltpu.make_async_remote_copy(
                    src_ref=stage.at[q % 2], dst_ref=recv_hbm.at[q], send_sem=pair_send.at[q], recv_sem=pair_recv.at[q],
                    device_id=sibling, device_id_type=MESH,
                )

            kept = [pltpu.make_async_copy(acc.at[rows(q, me[2]), :], own_hbm.at[q], own_sems.at[q]) for q in range(n_chips)]
            for cp in kept:
                cp.start()
            for q in range(n_chips):
                if q >= 2:
                    to_sibling(q - 2).wait_send()
                stage[q % 2] = acc[rows(q, sibling[2]), :].astype(BF16)
                to_sibling(q).start()
            for q in range(n_chips - 2, n_chips):
                to_sibling(q).wait_send()
            for q in range(n_chips):
                to_sibling(q).wait_recv()
            for cp in kept:
                cp.wait()
            for cp in gate_copies(g_hbm, lg_hbm, send_sems, recv_sems, arriving=True):
                cp.wait_recv()
            for cp in gate_copies(g_hbm, lg_hbm, send_sems, recv_sems, arriving=False):
                cp.wait_send()
            own_gates.wait()

    any_spec = pl.BlockSpec(memory_space=pl.ANY)
    n_copies = N_DEV - 1
    return pl.pallas_call(
        body,
        name="bwd_gw",
        grid=(n_tiles,),
        in_specs=_seg_specs(tm) + [pl.BlockSpec((tm, D), lambda i: (i, 0)), any_spec],
        out_specs=[any_spec, any_spec, any_spec],
        out_shape=[jax.ShapeDtypeStruct((n_chips, W_IN_SHARD, D), F32),
                   jax.ShapeDtypeStruct((n_chips, W_IN_SHARD, D), BF16),
                   jax.ShapeDtypeStruct((N_DEV, 2, 4, GATE_SHARD, 256), F32)],
        scratch_shapes=[pltpu.VMEM((IN_W, D), F32), pltpu.VMEM((2, W_IN_SHARD, D), BF16),
                        pltpu.SemaphoreType.DMA((n_chips,)), pltpu.SemaphoreType.DMA((n_chips,)), pltpu.SemaphoreType.DMA((n_chips,)),
                        pltpu.SemaphoreType.DMA((n_copies,)), pltpu.SemaphoreType.DMA((n_copies,)), pltpu.SemaphoreType.DMA],
        compiler_params=_params(1),
    )(*dsegs, h, g_gates)


def _adamw(w, g, m, v):
    m = ADAM_B1 * m + (1.0 - ADAM_B1) * g
    v = ADAM_B2 * v + (1.0 - ADAM_B2) * (g * g)
    m_hat = m / (1.0 - ADAM_B1**ADAM_STEP)
    v_hat = v / (1.0 - ADAM_B2**ADAM_STEP)
    delta = -ADAM_LR * (m_hat / (jnp.sqrt(v_hat) + ADAM_EPS) + ADAM_WD * w)
    return delta, m, v


def _w_in_update(own, land, w_t, m_t, v_t):
    n_slots, rows, cols = land.shape
    rb = 208
    spec = pl.BlockSpec((rb, cols), lambda i: (i, 0))

    def body(o_ref, l_ref, w_ref, m_ref, v_ref, g_ref, d_ref, m2_ref, v2_ref):
        total = o_ref[...]
        for s in range(n_slots):
            total = total + l_ref[s].astype(F32)
        g_ref[...] = total
        d_ref[...], m2_ref[...], v2_ref[...] = _adamw(w_ref[...], total, m_ref[...], v_ref[...])

    shape = jax.ShapeDtypeStruct((rows, cols), F32)
    return pl.pallas_call(
        body,
        name="w_in_update",
        grid=(rows // rb,),
        in_specs=[spec, pl.BlockSpec((n_slots, rb, cols), lambda i: (0, i, 0)), spec, spec, spec],
        out_specs=[spec] * 4,
        out_shape=[shape] * 4,
        compiler_params=_params(1),
    )(own, land, w_t, m_t, v_t)


def _shard_update(own_3, land_3, land_g, weights, m_in, v_in):
    n = len(weights)

    def body(*refs):
        o3_ref, l3_ref, lg_ref = refs[:3]
        w_refs, m_refs, v_refs = (refs[3 + n * k : 3 + n * (k + 1)] for k in range(3))
        outs = refs[3 + 3 * n :]
        for k in range(n):
            if k < 3:
                g = o3_ref[k]
                for slot in range(3):
                    g = g + l3_ref[slot, k].astype(F32)
            else:
                g = lg_ref[0, k - 3]
                for slot in range(1, N_DEV):
                    g = g + lg_ref[slot, k - 3]
            d, m2, v2 = _adamw(w_refs[k][0], g, m_refs[k][0], v_refs[k][0])
            for which, val in enumerate((g, d, m2, v2)):
                outs[which * n + k][0] = val

    shapes = [jax.ShapeDtypeStruct(w.shape, F32) for w in weights]
    res = pl.pallas_call(
        body, name="shard_update", out_shape=shapes * 4, compiler_params=pltpu.CompilerParams(vmem_limit_bytes=VMEM_LIMIT)
    )(own_3, land_3, land_g, *weights, *m_in, *v_in)
    return [res[n * k : n * (k + 1)] for k in range(4)]


VEC_LAYOUT = (("b_ada", 3 * D), ("norm_g", D), ("attn_sinks", 128), ("conv_b", D), ("rg_ba", D), ("rg_bx", D), ("rg_lambda", D),
              ("final_g", D), ("conv_w", CONV_WIDTH * D), ("loss", 128))
VEC_OFFSET = {name: sum(w for _, w in VEC_LAYOUT[:k]) for k, (name, _) in enumerate(VEC_LAYOUT)}
VEC_ROWS = 112
VEC_WIDTH = VEC_ROWS * 128
assert sum(w for _, w in VEC_LAYOUT) <= VEC_WIDTH
N_REPLICATED = 8


def _small_update(vec_all, c_all_t, me, rep_w, rep_m, rep_v, ada, conv):
    n_ada = ada[0].shape[1]
    n_conv = conv[0].shape[1]

    def body(*refs):
        me_ref, vec_ref, ct_ref = refs[:3]
        w_refs, m_refs, v_refs = (refs[3 + N_REPLICATED * k : 3 + N_REPLICATED * (k + 1)] for k in range(3))
        pos = 3 + 3 * N_REPLICATED
        ada_refs, conv_refs = refs[pos : pos + 3], refs[pos + 3 : pos + 6]
        outs = refs[pos + 6 :]
        loss_ref, rep_outs = outs[0], outs[1 : 1 + 4 * N_REPLICATED]
        ada_outs, conv_outs = outs[1 + 4 * N_REPLICATED : 5 + 4 * N_REPLICATED], outs[5 + 4 * N_REPLICATED : 9 + 4 * N_REPLICATED]
        conv_buf = outs[9 + 4 * N_REPLICATED]

        total = vec_ref[0:1, :]
        for slot in range(1, N_DEV):
            total = total + vec_ref[slot : slot + 1, :]
        loss_ref[...] = total[:, VEC_OFFSET["loss"] : VEC_OFFSET["loss"] + 1]
        for k, (name, _) in enumerate(VEC_LAYOUT[:N_REPLICATED]):
            width = w_refs[k].shape[1]
            g = total[:, VEC_OFFSET[name] : VEC_OFFSET[name] + width]
            d, m2, v2 = _adamw(w_refs[k][...], g, m_refs[k][...], v_refs[k][...])
            for which, val in enumerate((g, d, m2, v2)):
                rep_outs[which * N_REPLICATED + k][...] = val

        device = me_ref[0]
        for k in range(CONV_WIDTH):
            conv_buf[k : k + 1, :] = total[:, VEC_OFFSET["conv_w"] + D * k : VEC_OFFSET["conv_w"] + D * (k + 1)]
        g_conv = conv_buf[:, pl.ds(pl.multiple_of(device * n_conv, 128), n_conv)]
        for ref, val in zip(conv_outs, (g_conv,) + _adamw(conv_refs[0][...], g_conv, conv_refs[1][...], conv_refs[2][...])):
            ref[...] = val

        dmod = vec_ref[:, pl.ds(pl.multiple_of(device * n_ada, 128), n_ada)]
        g_ada = _dot(ct_ref[...], dmod)
        for ref, val in zip(ada_outs, (g_ada,) + _adamw(ada_refs[0][...], g_ada, ada_refs[1][...], ada_refs[2][...])):
            ref[...] = val

    vmem = pl.BlockSpec(memory_space=pltpu.VMEM)
    rep_shapes = [jax.ShapeDtypeStruct(w.shape, F32) for w in rep_w]
    out_shape = ([jax.ShapeDtypeStruct((1, 1), F32)] + rep_shapes * 4 + [jax.ShapeDtypeStruct(ada[0].shape, F32)] * 4
                 + [jax.ShapeDtypeStruct(conv[0].shape, F32)] * 4)
    n_in = 3 + 3 * N_REPLICATED + 6
    res = pl.pallas_call(
        body,
        name="small_update",
        in_specs=[pl.BlockSpec(memory_space=pltpu.SMEM)] + [vmem] * (n_in - 1),
        out_specs=[vmem] * len(out_shape),
        out_shape=out_shape,
        scratch_shapes=[pltpu.VMEM((CONV_WIDTH, D), F32)],
        compiler_params=pltpu.CompilerParams(vmem_limit_bytes=VMEM_LIMIT),
    )(me, vec_all, c_all_t, *rep_w, *rep_m, *rep_v, *ada, *conv)
    loss = res[0]
    reps = [res[1 + N_REPLICATED * k : 1 + N_REPLICATED * (k + 1)] for k in range(4)]
    return loss, reps, res[1 + 4 * N_REPLICATED : 5 + 4 * N_REPLICATED], res[5 + 4 * N_REPLICATED : 9 + 4 * N_REPLICATED]


def kernel(x, c, positions, w_ada, b_ada, norm_g, w_in, attn_sinks, conv_w, conv_b, rg_wa, rg_ba, rg_wx, rg_bx, rg_lambda, w_attn_proj, w_rnn_proj, w_out, final_g, loss_target, m_w_ada, m_b_ada, m_norm_g, m_w_in, m_attn_sinks, m_conv_w, m_conv_b, m_rg_wa, m_rg_ba, m_rg_wx, m_rg_bx, m_rg_lambda, m_w_attn_proj, m_w_rnn_proj, m_w_out, m_final_g, v_w_ada, v_b_ada, v_norm_g, v_w_in, v_attn_sinks, v_conv_w, v_conv_b, v_rg_wa, v_rg_ba, v_rg_wx, v_rg_bx, v_rg_lambda, v_w_attn_proj, v_w_rnn_proj, v_w_out, v_final_g):
    my_x, my_y, my_c = _position()
    me = _slot((my_x, my_y, my_c))
    x2, target = x[0], loss_target[0]
    pos = positions[0]

    gate_rows = lambda w: w[0].transpose(1, 0, 2).reshape(GATE_SHARD, D)
    w_pack = jnp.concatenate([w_attn_proj[0], w_rnn_proj[0], w_out[0], gate_rows(rg_wa), gate_rows(rg_wx)], axis=0).astype(BF16)
    conv_rows = jnp.pad(conv_w[0], ((0, 8 - CONV_WIDTH), (0, 0)))
    w_in_t = w_in[0].T
    n_ada = w_ada.shape[2]
    b_cols = lax.dynamic_slice(b_ada, (0, me * n_ada), (1, n_ada))

    tables = _rope_tables(pos)
    where = jnp.stack([_chip_of((my_x, my_y)), me]).astype(jnp.int32)
    proj, h, mod, w_t_all, w_all, c_rows, conv_all = _fwd_in(
        x2, jnp.broadcast_to(c, (8, D)), conv_rows, w_ada[0], b_cols, norm_g, w_in_t.astype(BF16), tables[0], w_pack, where
    )
    c_all = c_rows[:, 0, :]
    conv_full = conv_all[:, :CONV_WIDTH, :].transpose(1, 0, 2).reshape(CONV_WIDTH, D)
    gate = mod[:, 2 * D :]
    o = _attn_fwd(proj, attn_sinks)
    keep = (pos != 0).astype(F32)[:, None]
    hh = _rnn_fwd(proj, keep, w_all, rg_ba, rg_bx, rg_lambda, conv_full, conv_b)

    dres, d_o, d_hh, d_ga, d_gr, d_ma, d_mr, gw3, mid_vec = _mid(x2, target, o, hh, proj, w_all, gate, final_g[None, :])
    index = jnp.stack([my_c, _chip_of((my_x, my_y))]).astype(jnp.int32)
    part_3 = gw3.reshape(3, 4, 2, ROW_SHARD, D)
    d_q, d_k, d_v, d_sink, recv_3 = _attn_bwd(proj, o, d_o, tables, attn_sinks, part_3)
    sum_3_bf, own_3 = _pair_add(
        "pair_add_3", part_3, recv_3, index, (4,), 0,
        pl.BlockSpec((3, None, None, ROW_SHARD, D), lambda q, idx: (0, q, idx[0], 0, 0)),
        pl.BlockSpec((3, None, ROW_SHARD, D), lambda q, idx: (0, q, 0, 0)),
        pl.BlockSpec((3, ROW_SHARD, D), lambda q, idx: (0, 0, 0)), (3, ROW_SHARD, D),
    )
    d_xr, g_gates, rnn_vec, land_3 = _rnn_bwd(proj, hh, d_hh, keep, w_all, rg_ba, rg_bx, rg_lambda, conv_full, conv_b, sum_3_bf)
    dsegs = (d_q, d_k, d_v, d_ga, d_xr, d_gr, d_ma, d_mr)
    part_w, recv_w, land_g = _bwd_gw(dsegs, h, g_gates.reshape(2, 4, N_DEV, GATE_SHARD, 256))
    w_rows = pl.BlockSpec((None, W_IN_SHARD, D), lambda q, idx: (q, 0, 0))
    sum_w_bf, own_w = _pair_add(
        "pair_add_w", part_w, recv_w, index, (4,), 0, w_rows, w_rows,
        pl.BlockSpec((W_IN_SHARD, D), lambda q, idx: (0, 0)), (W_IN_SHARD, D),
    )
    w_sems, sum_w_thru, land_w_thru, token = _w_exchange_start(sum_w_bf)
    grad_x, dh_vec = _bwd_dh(dsegs, w_t_all, x2, dres, mod + token[0:1, 0:1], norm_g)
    land_w = _w_exchange_wait(w_sems, sum_w_thru, land_w_thru, dh_vec)

    pieces = dict(
        b_ada=jnp.concatenate([dh_vec[0:1], dh_vec[1:2], mid_vec[1:2]], axis=1),
        norm_g=dh_vec[2:3],
        attn_sinks=jnp.pad(d_sink[:, :, 0].reshape(1, 16), ((0, 0), (0, 112))),
        conv_b=rnn_vec[3:4],
        rg_ba=rnn_vec[0:1],
        rg_bx=rnn_vec[1:2],
        rg_lambda=rnn_vec[2:3],
        final_g=mid_vec[0:1],
        conv_w=rnn_vec[4:8].reshape(1, CONV_WIDTH * D),
        loss=mid_vec[2:3, 0:128],
    )
    vec = jnp.concatenate([pieces[name] for name, _ in VEC_LAYOUT], axis=1)
    vec = jnp.pad(vec, ((0, 0), (0, VEC_WIDTH - vec.shape[1]))).reshape(VEC_ROWS, 128)
    (vec_all,) = _all_gather_direct("ag_vec", [vec])
    rep_w = [b_ada, norm_g, attn_sinks, conv_b, rg_ba, rg_bx, rg_lambda, final_g[None, :]]
    rep_m = [m_b_ada, m_norm_g, m_attn_sinks, m_conv_b, m_rg_ba, m_rg_bx, m_rg_lambda, m_final_g[None, :]]
    rep_v = [v_b_ada, v_norm_g, v_attn_sinks, v_conv_b, v_rg_ba, v_rg_bx, v_rg_lambda, v_final_g[None, :]]
    loss, reps, ada_res, conv_res = _small_update(
        vec_all.reshape(N_DEV, VEC_WIDTH), c_all.T, me.astype(jnp.int32).reshape(1), rep_w, rep_m, rep_v,
        (w_ada[0], m_w_ada[0], v_w_ada[0]), (conv_w[0], m_conv_w[0], v_conv_w[0]),
    )

    w_in_res = _w_in_update(own_w, land_w, w_in_t, m_w_in[0].T, v_w_in[0].T)
    shard_res = _shard_update(
        own_3, land_3, land_g,
        (w_attn_proj, w_rnn_proj, w_out, rg_wa, rg_wx),
        (m_w_attn_proj, m_w_rnn_proj, m_w_out, m_rg_wa, m_rg_wx),
        (v_w_attn_proj, v_w_rnn_proj, v_w_out, v_rg_wa, v_rg_wx),
    )

    def by_weight(k):
        b, n, s, cb, ba, bx, lam, fg = reps[k]
        ap, rp, out, wa, wx = shard_res[k]
        return [ada_res[k][None], b, n, w_in_res[k].T[None], s, conv_res[k][None], cb, wa, ba, wx, bx, lam, ap, rp, out, fg[0]]

    return (loss[0, 0], grad_x[None], *by_weight(0), *by_weight(1), *by_weight(2), *by_weight(3))
```

```python
import math

import jax
import jax.numpy as jnp
from jax import lax
from jax.experimental import pallas as pl
from jax.experimental.pallas import tpu as pltpu

F32 = jnp.float32
BF16 = jnp.bfloat16

D = 1024
HEAD_DIM = 64
ROT_DIM = 16
ROPE_THETA = 500000.0
WINDOW = 128
LRU_C = 8.0
CONV_WIDTH = 4
NORM_EPS = 1e-6
IN_W = 6656
SEG_Q, SEG_K, SEG_V, SEG_GA, SEG_XR, SEG_GR, SEG_MA, SEG_MR = 0, 1024, 1280, 1536, 2560, 3584, 4608, 5632
N_DEV = 8
W_IN_SHARD = IN_W // N_DEV
ROW_SHARD = D // N_DEV
GATE_SHARD = 256 // N_DEV

ADAM_LR = 0.001
ADAM_B1 = 0.9
ADAM_B2 = 0.999
ADAM_EPS = 1e-08
ADAM_WD = 0.01
ADAM_STEP = 10

VMEM_LIMIT = 60 * 1024 * 1024
MESH = pl.DeviceIdType.MESH


def _dot(a, b):
    return jnp.dot(a, b, preferred_element_type=F32)


def _dot_nt(a, b):
    return lax.dot_general(a, b, (((1,), (1,)), ((), ())), preferred_element_type=F32)


def _dot_tn(a, b):
    return lax.dot_general(a, b, (((0,), (0,)), ((), ())), preferred_element_type=F32)


def _sigmoid(z):
    return 1.0 / (1.0 + jnp.exp(-z))


def _params(n_grid, vmem=VMEM_LIMIT):
    return pltpu.CompilerParams(dimension_semantics=("arbitrary",) * n_grid, vmem_limit_bytes=vmem)


def _position():
    return lax.axis_index("x"), lax.axis_index("y"), lax.axis_index("c")


def _flip(pos, k):
    x, y, c = pos
    return (1 - x if k & 4 else x, 1 - y if k & 2 else y, 1 - c if k & 1 else c)


def _slot(pos):
    return 4 * pos[0] + 2 * pos[1] + pos[2]


def _all_gather_direct(name, arrs):
    n = len(arrs)

    def body(*refs):
        ins, outs = refs[:n], refs[n : 2 * n]
        send_sems, recv_sems, local_sems = refs[2 * n :]
        me = _position()
        mine = [pltpu.make_async_copy(ins[a], outs[a].at[_slot(me)], local_sems.at[a]) for a in range(n)]
        for cp in mine:
            cp.start()

        def copy(a, k, block):
            return pltpu.make_async_remote_copy(
                src_ref=ins[a],
                dst_ref=outs[a].at[_slot(block)],
                send_sem=send_sems.at[a * 7 + k - 1],
                recv_sem=recv_sems.at[a * 7 + k - 1],
                device_id=_flip(me, k),
                device_id_type=MESH,
            )

        sent = [copy(a, k, me) for a in range(n) for k in range(1, N_DEV)]
        for cp in sent:
            cp.start()
        for a in range(n):
            for k in range(1, N_DEV):
                copy(a, k, _flip(me, k)).wait_recv()
        for cp in sent:
            cp.wait_send()
        for cp in mine:
            cp.wait()

    any_spec = pl.BlockSpec(memory_space=pl.ANY)
    return pl.pallas_call(
        body,
        name=name,
        out_shape=[jax.ShapeDtypeStruct((N_DEV,) + a.shape, a.dtype) for a in arrs],
        in_specs=[any_spec] * n,
        out_specs=[any_spec] * n,
        scratch_shapes=[
            pltpu.SemaphoreType.DMA((7 * n,)),
            pltpu.SemaphoreType.DMA((7 * n,)),
            pltpu.SemaphoreType.DMA((n,)),
        ],
    )(*arrs)


def _direct_gather_copies(src_hbm, all_hbm, send_sems, recv_sems, arriving):
    me = _position()
    return [
        pltpu.make_async_remote_copy(
            src_ref=src_hbm,
            dst_ref=all_hbm.at[_slot(_flip(me, k) if arriving else me)],
            send_sem=send_sems.at[k - 1],
            recv_sem=recv_sems.at[k - 1],
            device_id=_flip(me, k),
            device_id_type=MESH,
        )
        for k in range(1, N_DEV)
    ]


CHIP_FLIPS = (2, 4, 6)


def _chip_of(pos):
    return 2 * pos[0] + pos[1]


def _chip_copies(srcs, lands, views, send_sems, recv_sems):
    me = _position()
    copies = []
    for a, (src, land) in enumerate(zip(srcs, lands)):
        for j, k in enumerate(CHIP_FLIPS):
            peer = _flip(me, k)
            copies.append(
                pltpu.make_async_remote_copy(
                    src_ref=views[a](src, peer),
                    dst_ref=land.at[j],
                    send_sem=send_sems.at[len(CHIP_FLIPS) * a + j],
                    recv_sem=recv_sems.at[len(CHIP_FLIPS) * a + j],
                    device_id=peer,
                    device_id_type=MESH,
                )
            )
    return copies


def _pair_add(name, own, recv, index, grid, chip_axis, own_spec, recv_spec, mine_spec, mine_shape):
    def body(idx_ref, own_ref, recv_ref, sb_ref, mine_ref):
        total = own_ref[...] + recv_ref[...].astype(F32)
        sb_ref[...] = total.astype(BF16)

        @pl.when(pl.program_id(chip_axis) == idx_ref[1])
        def _():
            mine_ref[...] = total

    return pl.pallas_call(
        body,
        name=name,
        grid_spec=pltpu.PrefetchScalarGridSpec(
            num_scalar_prefetch=1, grid=grid, in_specs=[own_spec, recv_spec], out_specs=[recv_spec, mine_spec]
        ),
        out_shape=[jax.ShapeDtypeStruct(recv.shape, BF16), jax.ShapeDtypeStruct(mine_shape, F32)],
        compiler_params=_params(len(grid)),
    )(index, own, recv)


PACK_GATE = 3 * ROW_SHARD
PACK_ROWS = PACK_GATE + 2 * GATE_SHARD


def _fwd_in(x, c_rows, conv_rows, w_ada, b_cols, norm_g, w_shard, tables, pack, where):
    t = x.shape[0]
    tm = min(1024, t)
    tn = IN_W // 4
    n_i, n_j = t // tm, IN_W // tn
    n_rope = (SEG_V - SEG_Q) // 128

    n_ada = w_ada.shape[1]

    def body(where_ref, x_ref, wada_ref, bcols_ref, ng_ref, cos_ref, s1_ref, s2_ref, shard_hbm, pack_hbm, c_hbm, conv_hbm,
             proj_ref, h_ref, mod_ref, wt_hbm, all_hbm, call_hbm, convall_hbm,
             wall, hbuf, cbuf, mcols, mod_all, mod_buf,
             w_send, w_recv, load_sems, local_sem, p_send, p_recv, p_local,
             c_send, c_recv, c_local, v_send, v_recv, v_local, m_send, m_recv):
        j = pl.program_id(0)
        i = pl.program_id(1)
        me = _position()
        sibling = _flip(me, 1)
        chip = where_ref[0]
        slot = where_ref[1]

        def modulation():
            own_c = pltpu.make_async_copy(c_hbm, call_hbm.at[_slot(me)], c_local)
            own_v = pltpu.make_async_copy(conv_hbm, convall_hbm.at[_slot(me)], v_local)
            own_c.start()
            own_v.start()
            for cp in _direct_gather_copies(c_hbm, call_hbm, c_send, c_recv, arriving=False):
                cp.start()
            for cp in _direct_gather_copies(conv_hbm, convall_hbm, v_send, v_recv, arriving=False):
                cp.start()
            for cp in _direct_gather_copies(c_hbm, call_hbm, c_send, c_recv, arriving=True):
                cp.wait_recv()
            own_c.wait()
            fetch = pltpu.make_async_copy(call_hbm, cbuf, c_local)
            fetch.start()
            fetch.wait()
            mcols[...] = _dot(cbuf[:, 0, :].astype(BF16), wada_ref[...].astype(BF16)) + bcols_ref[...]
            mod_all[slot] = mcols[...]
            swaps = lambda arriving: [
                pltpu.make_async_remote_copy(
                    src_ref=mcols,
                    dst_ref=mod_all.at[_slot(_flip(me, k) if arriving else me)],
                    send_sem=m_send.at[k - 1],
                    recv_sem=m_recv.at[k - 1],
                    device_id=_flip(me, k),
                    device_id_type=MESH,
                )
                for k in range(1, N_DEV)
            ]
            for cp in swaps(False):
                cp.start()
            for cp in swaps(True):
                cp.wait_recv()
            mine = lax.broadcasted_iota(jnp.int32, (N_DEV, n_ada), 0) == slot
            for s in range(N_DEV):
                row = jnp.sum(jnp.where(mine, mod_all[s], 0.0), axis=0, keepdims=True)
                mod_buf[:, n_ada * s : n_ada * (s + 1)] = row
            for cp in swaps(False):
                cp.wait_send()
            for cp in _direct_gather_copies(c_hbm, call_hbm, c_send, c_recv, arriving=False):
                cp.wait_send()
            for cp in _direct_gather_copies(conv_hbm, convall_hbm, v_send, v_recv, arriving=True):
                cp.wait_recv()
            for cp in _direct_gather_copies(conv_hbm, convall_hbm, v_send, v_recv, arriving=False):
                cp.wait_send()
            own_v.wait()

        def shard_of(block):
            return wall.at[_chip_of(block), pl.ds(W_IN_SHARD * block[2], W_IN_SHARD), :]

        def shard_copy(k, block, to, src=None):
            return pltpu.make_async_remote_copy(
                src_ref=shard_of(block),
                dst_ref=shard_of(block),
                send_sem=w_send.at[k],
                recv_sem=w_recv.at[k],
                device_id=to,
                device_id_type=MESH,
            )

        own_shard = pltpu.make_async_copy(shard_hbm, shard_of(me), local_sem)
        own_pack = pltpu.make_async_copy(pack_hbm, all_hbm.at[_slot(me)], p_local)

        def keep_tile(q):
            return [
                pltpu.make_async_copy(wall.at[q, pl.ds(W_IN_SHARD * c, W_IN_SHARD), :], wt_hbm.at[2 * q + c], load_sems.at[2 * (q ^ chip) + c])
                for c in range(2)
            ]

        @pl.when((j == 0) & (i == 0))
        def _():
            modulation()
            mod_ref[...] = mod_buf[...]
            own_shard.start()
            own_shard.wait()
            shard_copy(0, me, sibling).start()
            for rel in range(1, 4):
                shard_copy(rel, me, _flip(me, 2 * rel)).start()
            own_pack.start()
            for cp in _direct_gather_copies(pack_hbm, all_hbm, p_send, p_recv, arriving=False):
                cp.start()
            shard_copy(0, sibling, me).wait_recv()
            for cp in keep_tile(chip):
                cp.start()

        for rel in range(1, 4):

            @pl.when((j == rel) & (i == 0))
            def _(rel=rel):
                there = _flip(me, 2 * rel)
                shard_copy(rel, there, me).wait_recv()
                shard_copy(3 + rel, there, sibling).start()
                shard_copy(3 + rel, _flip(sibling, 2 * rel), me).wait_recv()
                for cp in keep_tile(chip ^ rel):
                    cp.start()

        rows = pl.ds(pl.multiple_of(i * tm, tm), tm)

        @pl.when(j == 0)
        def _():
            xv = x_ref[...]
            r = lax.rsqrt(jnp.mean(xv * xv, axis=-1, keepdims=True) + NORM_EPS)
            shift = mod_buf[:, 0:D]
            scale = mod_buf[:, D : 2 * D]
            hv = (((xv * r) * ng_ref[...]) * (1.0 + scale) + shift).astype(BF16)
            hbuf[rows, :] = hv
            h_ref[...] = hv

        proj_ref[...] = _dot_nt(hbuf[rows, :], wall[chip ^ j])

        @pl.when((chip ^ j) == 0)
        def _():
            cos, s1, s2 = cos_ref[...], s1_ref[...], s2_ref[...]
            for g in range(n_rope):
                cols = slice(128 * g, 128 * (g + 1))
                proj_ref[:, cols] = _rope(proj_ref[:, cols], cos, s1, s2)

        @pl.when((j == n_j - 1) & (i == n_i - 1))
        def _():
            shard_copy(0, me, sibling).wait_send()
            for rel in range(1, 4):
                shard_copy(rel, me, _flip(me, 2 * rel)).wait_send()
                shard_copy(3 + rel, _flip(me, 2 * rel), sibling).wait_send()
            for q in range(N_DEV // 2):
                for cp in keep_tile(q):
                    cp.wait()
            for cp in _direct_gather_copies(pack_hbm, all_hbm, p_send, p_recv, arriving=True):
                cp.wait_recv()
            for cp in _direct_gather_copies(pack_hbm, all_hbm, p_send, p_recv, arriving=False):
                cp.wait_send()
            own_pack.wait()

    tab = pl.BlockSpec((tm, 128), lambda j, i, c: (i, 0))
    any_spec = pl.BlockSpec(memory_space=pl.ANY)
    return pl.pallas_call(
        body,
        name="fwd_in",
        grid_spec=pltpu.PrefetchScalarGridSpec(
            num_scalar_prefetch=1,
            grid=(n_j, n_i),
            in_specs=[
                pl.BlockSpec((tm, D), lambda j, i, c: (jnp.where(j == 0, i, 0), 0)),
                pl.BlockSpec(w_ada.shape, lambda j, i, c: (0, 0)),
                pl.BlockSpec((1, n_ada), lambda j, i, c: (0, 0)),
                pl.BlockSpec((1, D), lambda j, i, c: (0, 0)),
                tab, tab, tab, any_spec, any_spec, any_spec, any_spec,
            ],
            out_specs=[pl.BlockSpec((tm, tn), lambda j, i, c: (i, c[0] ^ j)),
                       pl.BlockSpec((tm, D), lambda j, i, c: (jnp.where(j == 0, i, n_i - 1), 0)),
                       pl.BlockSpec((1, 3 * D), lambda j, i, c: (0, 0)),
                       any_spec, any_spec, any_spec, any_spec],
            scratch_shapes=[
                pltpu.VMEM((N_DEV // 2, tn, D), BF16),
                pltpu.VMEM((t, D), BF16),
                pltpu.VMEM((N_DEV,) + c_rows.shape, F32),
                pltpu.VMEM((N_DEV, n_ada), F32),
                pltpu.VMEM((N_DEV, N_DEV, n_ada), F32),
                pltpu.VMEM((1, 3 * D), F32),
                pltpu.SemaphoreType.DMA((N_DEV - 1,)),
                pltpu.SemaphoreType.DMA((N_DEV - 1,)),
                pltpu.SemaphoreType.DMA((N_DEV,)),
                pltpu.SemaphoreType.DMA,
                pltpu.SemaphoreType.DMA((N_DEV - 1,)),
                pltpu.SemaphoreType.DMA((N_DEV - 1,)),
                pltpu.SemaphoreType.DMA,
                pltpu.SemaphoreType.DMA((N_DEV - 1,)),
                pltpu.SemaphoreType.DMA((N_DEV - 1,)),
                pltpu.SemaphoreType.DMA,
                pltpu.SemaphoreType.DMA((N_DEV - 1,)),
                pltpu.SemaphoreType.DMA((N_DEV - 1,)),
                pltpu.SemaphoreType.DMA,
                pltpu.SemaphoreType.DMA((N_DEV - 1,)),
                pltpu.SemaphoreType.DMA((N_DEV - 1,)),
            ],
        ),
        out_shape=[jax.ShapeDtypeStruct((t, IN_W), F32), jax.ShapeDtypeStruct((t, D), BF16),
                   jax.ShapeDtypeStruct((1, 3 * D), F32),
                   jax.ShapeDtypeStruct((N_DEV,) + w_shard.shape, w_shard.dtype),
                   jax.ShapeDtypeStruct((N_DEV,) + pack.shape, pack.dtype),
                   jax.ShapeDtypeStruct((N_DEV,) + c_rows.shape, F32),
                   jax.ShapeDtypeStruct((N_DEV,) + conv_rows.shape, F32)],
        compiler_params=_params(2),
    )(where, x, w_ada, b_cols, norm_g, *tables, w_shard, pack, c_rows, conv_rows)


def _rope_tables(positions):
    half = ROT_DIM // 2
    inv_freq = ROPE_THETA ** (-jnp.arange(0, ROT_DIM, 2, dtype=F32) / ROT_DIM)
    pos = positions.astype(F32)
    lane = jnp.arange(128) % HEAD_DIM
    freq_lane = jnp.where(lane < ROT_DIM, jnp.tile(inv_freq, 128 // half), 0.0)
    ang = pos[:, None] * freq_lane[None, :]
    sin = jnp.sin(ang)
    lane_tables = (jnp.cos(ang), jnp.where(lane < half, -sin, 0.0), jnp.where((lane >= half) & (lane < ROT_DIM), sin, 0.0))
    ang_t = inv_freq[:, None] * pos[None, :]
    return lane_tables, (jnp.cos(ang_t), jnp.sin(ang_t))


def _rope(tv, cos, s1, s2):
    cols = []
    for g in range(tv.shape[1] // 128):
        blk = tv[:, 128 * g : 128 * (g + 1)]
        cols.append(blk * cos + pltpu.roll(blk, 128 - 8, 1) * s1 + pltpu.roll(blk, 8, 1) * s2)
    return cols[0] if len(cols) == 1 else jnp.concatenate(cols, axis=1)


def _rope_transposed(gv, cos, s1, s2):
    cols = []
    for g in range(gv.shape[1] // 128):
        blk = gv[:, 128 * g : 128 * (g + 1)]
        cols.append(blk * cos + pltpu.roll(blk * s1, 8, 1) + pltpu.roll(blk * s2, 128 - 8, 1))
    return cols[0] if len(cols) == 1 else jnp.concatenate(cols, axis=1)


def _lane_lo():
    return lax.broadcasted_iota(jnp.int32, (1, 128), 1) < HEAD_DIM


def _row_lo():
    return lax.broadcasted_iota(jnp.int32, (128, 1), 0) < HEAD_DIM


def _split_heads(tile, j):
    lo = _lane_lo()
    parts = []
    for g in (2 * j, 2 * j + 1):
        blk = tile[:, 128 * g : 128 * (g + 1)]
        parts += [jnp.where(lo, blk, 0.0), jnp.where(lo, 0.0, blk)]
    return jnp.concatenate(parts, axis=0)


def _dup_lanes(kv, j):
    rolled = pltpu.roll(kv, HEAD_DIM, 1)
    lo = _lane_lo()
    return jnp.where(lo, kv, rolled) if j == 0 else jnp.where(lo, rolled, kv)


def _fold_lanes(acc):
    return acc + pltpu.roll(acc, HEAD_DIM, 1)


def _fill_bias_t(bias_ref):
    kj = lax.broadcasted_iota(jnp.int32, (256, 128), 0)
    qi = lax.broadcasted_iota(jnp.int32, (256, 128), 1)
    diff = qi + WINDOW - kj
    band = (diff >= 0) & (diff < WINDOW)
    bias_ref[0] = jnp.where(band, 0.0, -1e30)
    bias_ref[1] = jnp.where(band & (kj >= 128), 0.0, -1e30)


def _sink_row(sink_ref, pair, j):
    return jnp.concatenate([jnp.full((1, 128), sink_ref[0, 8 * pair + 4 * j + h], F32) for h in range(4)], axis=1)


def _softmax_t(q4, kw, bias4, sink):
    s = _dot_nt(kw, q4) + bias4
    m = jnp.maximum(jnp.max(s, axis=0, keepdims=True), sink)
    p = jnp.exp(s - m)
    e_sink = jnp.exp(sink - m)
    denom = jnp.sum(p, axis=0, keepdims=True) + e_sink
    return p, e_sink, denom


def _attn_specs(t, tq):
    return dict(
        q=pl.BlockSpec((tq, 512), lambda p, i: (i, p)),
        k=pl.BlockSpec((t, 128), lambda p, i: (0, SEG_K // 128 + p)),
        v=pl.BlockSpec((t, 128), lambda p, i: (0, SEG_V // 128 + p)),
        sink=pl.BlockSpec(memory_space=pltpu.SMEM),
        act=pl.BlockSpec((tq, 512), lambda p, i: (i, p)),
    )


def _attn_fwd(proj, sinks):
    t = proj.shape[0]
    tq = min(512, t)
    nb = tq // 128
    n_tiles = t // tq
    sp = _attn_specs(t, tq)

    def body(q_ref, k_ref, v_ref, sink_ref, o_ref, kd, vdt, bias_ref):
        pair = pl.program_id(0)
        i = pl.program_id(1)

        @pl.when(i == 0)
        def _():
            kv, vv = k_ref[...], v_ref[...]
            for j in range(2):
                kd[j, 0:128, :] = jnp.zeros((128, 128), BF16)
                kd[j, 128:, :] = _dup_lanes(kv, j).astype(BF16)
                vdt[j, :, 0:128] = jnp.zeros((128, 128), BF16)
                vdt[j, :, 128:] = _dup_lanes(vv, j).T.astype(BF16)
            _fill_bias_t(bias_ref)

        row_lo = _row_lo()

        def block(b, carry):
            r0 = pl.multiple_of(b * 128, 128)
            gb = i * nb + b
            keys = pl.ds(pl.multiple_of(gb * 128, 128), 256)
            rows = pl.ds(r0, 128)
            qt = q_ref[rows, :] * (1.0 / math.sqrt(HEAD_DIM))
            bias = bias_ref[jnp.where(gb == 0, 1, 0)]
            bias4 = jnp.concatenate([bias] * 4, axis=1)
            cols = []
            for j in range(2):
                q4 = _split_heads(qt, j).astype(BF16)
                p, _, denom = _softmax_t(q4, kd[j, keys, :], bias4, _sink_row(sink_ref, pair, j))
                ot = _dot(vdt[j, :, keys], p.astype(BF16)) * (1.0 / denom)
                for gi in range(2):
                    z = jnp.where(row_lo, ot[:, 256 * gi : 256 * gi + 128], ot[:, 256 * gi + 128 : 256 * gi + 256])
                    cols.append(z.T)
            o_ref[rows, :] = jnp.concatenate(cols, axis=1).astype(BF16)
            return carry

        lax.fori_loop(0, nb, block, 0, unroll=True)

    return pl.pallas_call(
        body,
        name="attn_fwd",
        grid=(2, n_tiles),
        in_specs=[sp["q"], sp["k"], sp["v"], sp["sink"]],
        out_specs=sp["act"],
        out_shape=jax.ShapeDtypeStruct((t, D), BF16),
        scratch_shapes=[pltpu.VMEM((2, t + 128, 128), BF16), pltpu.VMEM((2, 128, t + 128), BF16), pltpu.VMEM((2, 256, 128), F32)],
        compiler_params=_params(2),
    )(proj, proj, proj, sinks)


def _attn_bwd(proj, o, d_o, tables, sinks, part_3):
    t = proj.shape[0]
    tq = min(512, t)
    nb = tq // 128
    n_tiles = t // tq
    (cos, s1, s2), (cos_t, sin_t) = tables
    sp = _attn_specs(t, tq)
    scale = 1.0 / math.sqrt(HEAD_DIM)
    half = ROT_DIM // 2

    def unrope_rows(z, c, s):
        out = []
        for base in (0, HEAD_DIM):
            g0, g1 = z[base : base + half], z[base + half : base + ROT_DIM]
            out += [g0 * c + g1 * s, g1 * c - g0 * s, z[base + ROT_DIM : base + HEAD_DIM]]
        return jnp.concatenate(out, axis=0)

    def swap(part_hbm, recv_hbm, send_sem, recv_sem):
        sibling = _flip(_position(), 1)
        return pltpu.make_async_remote_copy(
            src_ref=part_hbm.at[:, :, sibling[2]], dst_ref=recv_hbm, send_sem=send_sem, recv_sem=recv_sem,
            device_id=sibling, device_id_type=MESH,
        )

    def body(q_ref, k_ref, v_ref, o_ref, do_ref, ct_ref, st_ref, ck, s1k, s2k, sink_ref, part_hbm,
             dq_ref, dk_ref, dv_ref, dsink_ref, recv_hbm, kd, kdt, vd, bias_ref, dkacc, dvacc, send_sem, recv_sem):
        pair = pl.program_id(0)
        i = pl.program_id(1)

        @pl.when((pair == 0) & (i == 0))
        def _():
            swap(part_hbm, recv_hbm, send_sem, recv_sem).start()

        @pl.when((pair == 1) & (i == n_tiles - 1))
        def _():
            swap(part_hbm, recv_hbm, send_sem, recv_sem).wait()

        @pl.when(i == 0)
        def _():
            kv, vv = k_ref[...], v_ref[...]
            for j in range(2):
                kdup = _dup_lanes(kv, j)
                kd[j, 0:128, :] = jnp.zeros((128, 128), BF16)
                kd[j, 128:, :] = kdup.astype(BF16)
                kdt[j, :, 0:128] = jnp.zeros((128, 128), BF16)
                kdt[j, :, 128:] = kdup.T.astype(BF16)
                vd[j, 0:128, :] = jnp.zeros((128, 128), BF16)
                vd[j, 128:, :] = _dup_lanes(vv, j).astype(BF16)
            _fill_bias_t(bias_ref)
            dkacc[...] = jnp.zeros_like(dkacc)
            dvacc[...] = jnp.zeros_like(dvacc)
            dsink_ref[...] = jnp.zeros_like(dsink_ref)

        row_lo = _row_lo()
        ones = jnp.ones((8, 128), BF16)

        def block(b, carry):
            r0 = pl.multiple_of(b * 128, 128)
            gb = i * nb + b
            keys = pl.ds(pl.multiple_of(gb * 128, 128), 256)
            rows = pl.ds(r0, 128)
            qt = q_ref[rows, :] * scale
            dob = do_ref[rows, :].astype(F32)
            prod = dob * o_ref[rows, :].astype(F32)
            ct, st = ct_ref[:, rows], st_ref[:, rows]
            bias = bias_ref[jnp.where(gb == 0, 1, 0)]
            bias4 = jnp.concatenate([bias] * 4, axis=1)
            cols = []
            for j in range(2):
                q4 = _split_heads(qt, j).astype(BF16)
                kw = kd[j, keys, :]
                p, e_sink, denom = _softmax_t(q4, kw, bias4, _sink_row(sink_ref, pair, j))
                inv = 1.0 / denom
                prob = p * inv
                do4 = _split_heads(dob, j).astype(BF16)
                delta = _dot_nt(ones, _split_heads(prod, j).astype(BF16))[0:1, :]
                dp = _dot_nt(vd[j, keys, :], do4)
                ds = (prob * (dp - delta)).astype(BF16)
                dkacc[j, keys, :] = dkacc[j, keys, :] + _dot(ds, q4)
                dvacc[j, keys, :] = dvacc[j, keys, :] + _dot(prob.astype(BF16), do4)
                dqt = _dot(kdt[j, :, keys], ds) * scale
                dsink = -(e_sink * inv) * delta
                for h in range(4):
                    row = 4 * j + h
                    total = jnp.sum(dsink[:, 128 * h : 128 * (h + 1)], axis=1, keepdims=True)
                    dsink_ref[0, row : row + 1, :] = dsink_ref[0, row : row + 1, :] + jnp.broadcast_to(total, (1, 128))
                for gi in range(2):
                    z = jnp.where(row_lo, dqt[:, 256 * gi : 256 * gi + 128], dqt[:, 256 * gi + 128 : 256 * gi + 256])
                    cols.append(unrope_rows(z, ct, st).T)
            dq_ref[rows, :] = jnp.concatenate(cols, axis=1).astype(BF16)
            return carry

        lax.fori_loop(0, nb, block, 0, unroll=True)

        @pl.when(i == n_tiles - 1)
        def _():
            lo = _lane_lo()
            dk = jnp.where(lo, _fold_lanes(dkacc[0, 128:, :]), _fold_lanes(dkacc[1, 128:, :]))
            dk_ref[...] = _rope_transposed(dk, ck[...], s1k[...], s2k[...]).astype(BF16)
            dv_ref[...] = jnp.where(lo, _fold_lanes(dvacc[0, 128:, :]), _fold_lanes(dvacc[1, 128:, :])).astype(BF16)

    kv_out = pl.BlockSpec((t, 128), lambda p, i: (0, p))
    tab_t = pl.BlockSpec((half, tq), lambda p, i: (0, i))
    tab_k = pl.BlockSpec((t, 128), lambda p, i: (0, 0))
    return pl.pallas_call(
        body,
        name="attn_bwd",
        grid=(2, n_tiles),
        in_specs=[sp["q"], sp["k"], sp["v"], sp["act"], sp["act"], tab_t, tab_t, tab_k, tab_k, tab_k, sp["sink"],
                  pl.BlockSpec(memory_space=pl.ANY)],
        out_specs=[sp["act"], kv_out, kv_out, pl.BlockSpec((1, 8, 128), lambda p, i: (p, 0, 0)), pl.BlockSpec(memory_space=pl.ANY)],
        out_shape=[
            jax.ShapeDtypeStruct((t, D), BF16),
            jax.ShapeDtypeStruct((t, 256), BF16),
            jax.ShapeDtypeStruct((t, 256), BF16),
            jax.ShapeDtypeStruct((2, 8, 128), F32),
            jax.ShapeDtypeStruct((3, 4, ROW_SHARD, D), F32),
        ],
        scratch_shapes=[pltpu.VMEM((2, t + 128, 128), BF16), pltpu.VMEM((2, 128, t + 128), BF16), pltpu.VMEM((2, t + 128, 128), BF16),
                        pltpu.VMEM((2, 256, 128), F32), pltpu.VMEM((2, t + 128, 128), F32), pltpu.VMEM((2, t + 128, 128), F32),
                        pltpu.SemaphoreType.DMA, pltpu.SemaphoreType.DMA],
        compiler_params=_params(2),
    )(proj, proj, proj, o, d_o, cos_t, sin_t, cos, s1, s2, sinks, part_3)


HALF = 512


def _softplus(y):
    u = jnp.exp(-jnp.abs(y))
    w = 1.0 + u
    log1p = jnp.where(w == 1.0, u, jnp.log(w) * (u / jnp.where(w == 1.0, 1.0, w - 1.0)))
    return jnp.maximum(y, 0.0) + log1p


def _gate_block(w_ref, blk):
    return w_ref[:, :, 256 * blk : 256 * (blk + 1)].reshape(256, 256)


def _gate_matmul(act_bf, w_ref):
    return jnp.concatenate([_dot(act_bf[:, 256 * blk : 256 * (blk + 1)], _gate_block(w_ref, blk)) for blk in range(2)], axis=1)


def _gate_matmul_t(grad_bf, w_ref):
    return jnp.concatenate([_dot_nt(grad_bf[:, 256 * blk : 256 * (blk + 1)], _gate_block(w_ref, blk)) for blk in range(2)], axis=1)


def _conv_and_gates(xe_ref, tm, cw_ref, cb_ref, wa_ref, wx_ref, ba_ref, bx_ref, lam_ref, keep):
    xc = cb_ref[...] + sum(cw_ref[k : k + 1, :] * xe_ref[pl.ds(5 + k, tm), :] for k in range(CONV_WIDTH))
    xc_bf = xc.astype(BF16)
    r = _sigmoid(_gate_matmul(xc_bf, wa_ref) + ba_ref[...])
    ig = _sigmoid(_gate_matmul(xc_bf, wx_ref) + bx_ref[...])
    sp = _softplus(-lam_ref[...])
    log_a = -LRU_C * r * sp
    tanh = jnp.tanh(-log_a)
    mult = jnp.sqrt(2.0 * tanh / (1.0 + tanh))
    a = jnp.where(keep, jnp.exp(log_a), 0.0)
    mult = jnp.where(keep, mult, 1.0)
    return xc, xc_bf, r, ig, sp, a, mult


def _scan_rows(a_ref, b_ref, h_ref, carry_ref, tm, reverse):
    width = a_ref.shape[1]
    row = lax.broadcasted_iota(jnp.int32, (8, width), 0)

    def group(g, carry):
        gg = tm // 8 - 1 - g if reverse else g
        r0 = pl.multiple_of(gg * 8, 8)
        a = a_ref[pl.ds(r0, 8), :]
        b = b_ref[pl.ds(r0, 8), :]
        for s in (1, 2, 4):
            shift = 8 - s if reverse else s
            a_sh = pltpu.roll(a, shift, 0)
            b_sh = pltpu.roll(b, shift, 0)
            ok = (row < 8 - s) if reverse else (row >= s)
            b = jnp.where(ok, a * b_sh + b, b)
            a = jnp.where(ok, a * a_sh, a)
        h = a * carry + b
        h_ref[pl.ds(r0, 8), :] = h
        last = h[0:1, :] if reverse else h[7:8, :]
        return jnp.broadcast_to(last, (8, width))

    carry_ref[...] = lax.fori_loop(0, tm // 8, group, carry_ref[...])


def _rnn_specs(t, tm, order):
    n = t // tm
    return dict(
        tile=pl.BlockSpec((tm, HALF), lambda ch, i: (order(i, n), ch)),
        before=lambda col0: pl.BlockSpec((8, HALF), lambda ch, i: (jnp.maximum(order(i, n) * (tm // 8) - 1, 0), col0 + ch)),
        proj_tile=lambda col0: pl.BlockSpec((tm, HALF), lambda ch, i: (order(i, n), col0 + ch)),
        keep=pl.BlockSpec((tm, 1), lambda ch, i: (order(i, n), 0)),
        gates=lambda which: pl.BlockSpec((N_DEV, GATE_SHARD, HALF), lambda ch, i: (0, PACK_GATE // GATE_SHARD + which, ch)),
        gate_grads=pl.BlockSpec((2, 2, 256, 256), lambda ch, i: (0, ch, 0, 0)),
        vec=pl.BlockSpec((1, HALF), lambda ch, i: (0, ch)),
        conv=pl.BlockSpec((CONV_WIDTH, HALF), lambda ch, i: (0, ch)),
    )


def _rnn_fwd(proj, keep, w_all, rg_ba, rg_bx, rg_lambda, conv_w, conv_b):
    t = proj.shape[0]
    tm = min(512, t)
    sp = _rnn_specs(t, tm, lambda i, n: i)

    def body(xr_ref, xb_ref, keep_ref, wa_ref, wx_ref, ba_ref, bx_ref, lam_ref, cw_ref, cb_ref, hh_ref, xe, a_buf, b_buf, carry):
        i = pl.program_id(1)

        @pl.when(i == 0)
        def _():
            carry[...] = jnp.zeros_like(carry)

        xe[0:8, :] = jnp.where(i > 0, xb_ref[...], 0.0)
        xe[8:, :] = xr_ref[...]
        keep_rows = keep_ref[...] != 0.0
        xc, _, _, ig, _, a, mult = _conv_and_gates(xe, tm, cw_ref, cb_ref, wa_ref, wx_ref, ba_ref, bx_ref, lam_ref, keep_rows)
        a_buf[...] = a
        b_buf[...] = mult * (ig * xc)
        _scan_rows(a_buf, b_buf, hh_ref, carry, tm, reverse=False)

    return pl.pallas_call(
        body,
        name="rnn_fwd",
        grid=(2, t // tm),
        in_specs=[sp["proj_tile"](SEG_XR // HALF), sp["before"](SEG_XR // HALF), sp["keep"], sp["gates"](0), sp["gates"](1),
                  sp["vec"], sp["vec"], sp["vec"], sp["conv"], sp["vec"]],
        out_specs=sp["tile"],
        out_shape=jax.ShapeDtypeStruct((t, D), F32),
        scratch_shapes=[pltpu.VMEM((tm + 8, HALF), F32), pltpu.VMEM((tm, HALF), F32), pltpu.VMEM((tm, HALF), F32), pltpu.VMEM((8, HALF), F32)],
        compiler_params=_params(2),
    )(proj, proj, keep, w_all, w_all, rg_ba, rg_bx, rg_lambda, conv_w, conv_b)


def _rnn_bwd(proj, hh, d_hh, keep, w_all, rg_ba, rg_bx, rg_lambda, conv_w, conv_b, sum_3_bf):
    t = proj.shape[0]
    tm = min(512, t)
    n_tiles = t // tm
    sp = _rnn_specs(t, tm, lambda i, n: n - 1 - i)

    def body(xr_ref, xb_ref, hh_ref, hb_ref, dhh_ref, keep_ref, wa_ref, wx_ref, ba_ref, bx_ref, lam_ref, cw_ref, cb_ref,
             s3_hbm, dxr_ref, gw_ref, gvec_ref, l3_hbm, xe, he, ae, c_buf, g_buf, dxe, carry, a_next, send_sems, recv_sems):
        ch = pl.program_id(0)
        i = pl.program_id(1)
        first_tile = i == n_tiles - 1
        views = [lambda ref, peer: ref.at[:, _chip_of(peer)]]

        @pl.when((ch == 0) & (i == 0))
        def _():
            for cp in _chip_copies([s3_hbm], [l3_hbm], views, send_sems, recv_sems):
                cp.start()

        @pl.when((ch == 1) & (i == n_tiles - 1))
        def _():
            for cp in _chip_copies([s3_hbm], [l3_hbm], views, send_sems, recv_sems):
                cp.wait_recv()
                cp.wait_send()

        @pl.when(i == 0)
        def _():
            carry[...] = jnp.zeros_like(carry)
            a_next[...] = jnp.zeros_like(a_next)
            dxe[tm:, :] = jnp.zeros((8, HALF), F32)
            gw_ref[...] = jnp.zeros_like(gw_ref)
            gvec_ref[...] = jnp.zeros_like(gvec_ref)

        xe[0:8, :] = jnp.where(first_tile, 0.0, xb_ref[...])
        xe[8:, :] = xr_ref[...]
        he[0:8, :] = jnp.where(first_tile, 0.0, hb_ref[...])
        he[8:, :] = hh_ref[...]
        keep_rows = keep_ref[...] != 0.0
        xc, xc_bf, r, ig, sp_, a, mult = _conv_and_gates(xe, tm, cw_ref, cb_ref, wa_ref, wx_ref, ba_ref, bx_ref, lam_ref, keep_rows)

        ae[0:tm, :] = a
        ae[tm:, :] = a_next[...]
        a_next[...] = jnp.broadcast_to(a[0:1, :], (8, HALF))
        c_buf[...] = ae[pl.ds(1, tm), :]
        g_buf[...] = dhh_ref[...]
        _scan_rows(c_buf, g_buf, g_buf, carry, tm, reverse=True)
        g = g_buf[...]

        h_prev = he[pl.ds(7, tm), :]
        bb = ig * xc
        d_ig = g * mult * xc
        d_xc = g * mult * ig
        d_log_a = jnp.where(keep_rows, g * h_prev * a - (g * bb) * (a * a) / mult, 0.0)
        d_r = d_log_a * (-LRU_C * sp_)
        d_sp = jnp.sum(d_log_a * (-LRU_C * r), axis=0, keepdims=True)
        d_zr = d_r * r * (1.0 - r)
        d_zi = d_ig * ig * (1.0 - ig)
        d_zr_bf = d_zr.astype(BF16)
        d_zi_bf = d_zi.astype(BF16)
        d_xc = d_xc + _gate_matmul_t(d_zr_bf, wa_ref) + _gate_matmul_t(d_zi_bf, wx_ref)
        for blk in range(2):
            cols = slice(256 * blk, 256 * (blk + 1))
            gw_ref[0, blk] = gw_ref[0, blk] + _dot_tn(xc_bf[:, cols], d_zr_bf[:, cols])
            gw_ref[1, blk] = gw_ref[1, blk] + _dot_tn(xc_bf[:, cols], d_zi_bf[:, cols])

        dxe[0:tm, :] = d_xc
        shifted = [dxe[pl.ds(3 - k, tm), :] for k in range(CONV_WIDTH)]
        dxr = sum(cw_ref[k : k + 1, :] * shifted[k] for k in range(CONV_WIDTH))
        dxr_ref[...] = dxr.astype(BF16)
        dxe[tm:, :] = d_xc[0:8, :]

        lam = lam_ref[...]
        xr_tile = xr_ref[...]
        sums = [
            jnp.sum(d_zr, axis=0, keepdims=True),
            jnp.sum(d_zi, axis=0, keepdims=True),
            d_sp * (-_sigmoid(-lam)),
            jnp.sum(d_xc, axis=0, keepdims=True),
        ] + [jnp.sum(xr_tile * shifted[k], axis=0, keepdims=True) for k in range(CONV_WIDTH)]
        gvec_ref[...] = gvec_ref[...] + jnp.concatenate(sums, axis=0)

    return pl.pallas_call(
        body,
        name="rnn_bwd",
        grid=(2, n_tiles),
        in_specs=[
            sp["proj_tile"](SEG_XR // HALF), sp["before"](SEG_XR // HALF), sp["tile"], sp["before"](0), sp["tile"], sp["keep"],
            sp["gates"](0), sp["gates"](1), sp["vec"], sp["vec"], sp["vec"], sp["conv"], sp["vec"], pl.BlockSpec(memory_space=pl.ANY),
        ],
        out_specs=[sp["tile"], sp["gate_grads"], pl.BlockSpec((8, HALF), lambda ch, i: (0, ch)), pl.BlockSpec(memory_space=pl.ANY)],
        out_shape=[
            jax.ShapeDtypeStruct((t, D), BF16),
            jax.ShapeDtypeStruct((2, 4, 256, 256), F32),
            jax.ShapeDtypeStruct((8, D), F32),
            jax.ShapeDtypeStruct((len(CHIP_FLIPS), 3, ROW_SHARD, D), BF16),
        ],
        scratch_shapes=[
            pltpu.VMEM((tm + 8, HALF), F32),
            pltpu.VMEM((tm + 8, HALF), F32),
            pltpu.VMEM((tm + 8, HALF), F32),
            pltpu.VMEM((tm, HALF), F32),
            pltpu.VMEM((tm, HALF), F32),
            pltpu.VMEM((tm + 8, HALF), F32),
            pltpu.VMEM((8, HALF), F32),
            pltpu.VMEM((8, HALF), F32),
            pltpu.SemaphoreType.DMA((len(CHIP_FLIPS),)),
            pltpu.SemaphoreType.DMA((len(CHIP_FLIPS),)),
        ],
        compiler_params=_params(2),
    )(proj, proj, hh, hh, d_hh, keep, w_all, w_all, rg_ba, rg_bx, rg_lambda, conv_w, conv_b, sum_3_bf)


def _mid(x, target, o, hh, proj, w_all, gate, final_g):
    t = x.shape[0]
    tm = min(256, t)
    n_tiles = t // tm
    full = pl.BlockSpec((tm, D), lambda i: (i, 0))
    half = lambda col: pl.BlockSpec((tm, 512), lambda i: (i, col))
    vec = pl.BlockSpec((1, D), lambda i: (0, 0))
    any_spec = pl.BlockSpec(memory_space=pl.ANY)

    def body(x_ref, tg_ref, o_ref, hh_ref, ga0, ga1, gr0, gr1, ma0, ma1, mr0, mr1, w_hbm, gate_ref, fg_ref,
             dres_ref, do_ref, dhh_ref, dga_ref, dgr_ref, dma_ref, dmr_ref, gw_hbm, vec_ref, w, gw, sem, w_sems):
        i = pl.program_id(0)

        @pl.when(i == 0)
        def _():
            loads = [
                pltpu.make_async_copy(
                    w_hbm.at[s, pl.ds(ROW_SHARD * k, ROW_SHARD), :], w.at[k, pl.ds(ROW_SHARD * s, ROW_SHARD), :], w_sems.at[k * N_DEV + s]
                )
                for k in range(3)
                for s in range(N_DEV)
            ]
            for cp in loads:
                cp.start()
            for cp in loads:
                cp.wait()
            gw[...] = jnp.zeros_like(gw)
            vec_ref[...] = jnp.zeros_like(vec_ref)

        cat = lambda a, b: jnp.concatenate([a[...], b[...]], axis=1)
        ga, gr, ma, mr = cat(ga0, ga1), cat(gr0, gr1), cat(ma0, ma1), cat(mr0, mr1)
        ov, hv, gate_v, fg = o_ref[...].astype(F32), hh_ref[...], gate_ref[...], fg_ref[...]
        sg_a = _sigmoid(ga)
        sg_r = _sigmoid(gr)
        silu_a = ga * sg_a
        silu_r = gr * sg_r
        ya = (ov * silu_a).astype(BF16)
        yr = (hv * silu_r).astype(BF16)
        pa = _dot(ya, w[0])
        pr = _dot(yr, w[1])
        sa = _sigmoid(ma)
        sr = _sigmoid(mr)
        merged = (sa * pa + sr * pr).astype(BF16)
        mo = _dot(merged, w[2])
        out = x_ref[...] + gate_v * mo
        r2 = lax.rsqrt(jnp.mean(out * out, axis=-1, keepdims=True) + NORM_EPS)
        outn = out * r2
        err = outn * fg - tg_ref[...]
        loss = 0.5 * jnp.sum(jnp.mean(err * err, axis=-1, keepdims=True), axis=0, keepdims=True)
        dy = err * (1.0 / D)
        g_final = jnp.sum(dy * outn, axis=0, keepdims=True)
        doutn = dy * fg
        dout = r2 * (doutn - outn * jnp.mean(doutn * outn, axis=-1, keepdims=True))
        dres_ref[...] = dout
        d_gate = jnp.sum(dout * mo, axis=0, keepdims=True)
        dmo = (dout * gate_v).astype(BF16)
        dmerged = _dot_nt(dmo, w[2])
        gw[2] = gw[2] + _dot_tn(merged, dmo)
        dpa = (dmerged * sa).astype(BF16)
        dpr = (dmerged * sr).astype(BF16)
        dma_ref[...] = (dmerged * pa * (sa * (1.0 - sa))).astype(BF16)
        dmr_ref[...] = (dmerged * pr * (sr * (1.0 - sr))).astype(BF16)
        dya = _dot_nt(dpa, w[0])
        dyr = _dot_nt(dpr, w[1])
        gw[0] = gw[0] + _dot_tn(ya, dpa)
        gw[1] = gw[1] + _dot_tn(yr, dpr)
        do_ref[...] = (dya * silu_a).astype(BF16)
        dhh_ref[...] = dyr * silu_r
        dga_ref[...] = (dya * ov * (sg_a * (1.0 + ga * (1.0 - sg_a)))).astype(BF16)
        dgr_ref[...] = (dyr * hv * (sg_r * (1.0 + gr * (1.0 - sg_r)))).astype(BF16)
        zeros = jnp.zeros((5, D), F32)
        vec_ref[...] = vec_ref[...] + jnp.concatenate([g_final, d_gate, jnp.broadcast_to(loss, (1, D)), zeros], axis=0)

        @pl.when(i == n_tiles - 1)
        def _():
            cp = pltpu.make_async_copy(gw, gw_hbm, sem)
            cp.start()
            cp.wait()

    bf_tile = jax.ShapeDtypeStruct((t, D), BF16)
    f_tile = jax.ShapeDtypeStruct((t, D), F32)
    return pl.pallas_call(
        body,
        name="mid",
        grid=(n_tiles,),
        in_specs=[full, full, full, full,
                  half(SEG_GA // 512), half(SEG_GA // 512 + 1), half(SEG_GR // 512), half(SEG_GR // 512 + 1),
                  half(SEG_MA // 512), half(SEG_MA // 512 + 1), half(SEG_MR // 512), half(SEG_MR // 512 + 1),
                  any_spec, vec, vec],
        out_specs=[full, full, full, full, full, full, full, any_spec, pl.BlockSpec((8, D), lambda i: (0, 0))],
        out_shape=[f_tile, bf_tile, f_tile, bf_tile, bf_tile, bf_tile, bf_tile,
                   jax.ShapeDtypeStruct((3, D, D), F32), jax.ShapeDtypeStruct((8, D), F32)],
        scratch_shapes=[pltpu.VMEM((3, D, D), BF16), pltpu.VMEM((3, D, D), F32), pltpu.SemaphoreType.DMA, pltpu.SemaphoreType.DMA((3 * N_DEV,))],
        compiler_params=_params(1),
    )(x, target, o, hh, proj, proj, proj, proj, proj, proj, proj, proj, w_all, gate, final_g)


_SEGMENTS = ((SEG_Q, 1024), (SEG_K, 256), (SEG_V, 256), (SEG_GA, 1024), (SEG_XR, 1024), (SEG_GR, 1024), (SEG_MA, 1024), (SEG_MR, 1024))


def _seg_specs(tm):
    return [pl.BlockSpec((tm, width), lambda i: (i, 0)) for _, width in _SEGMENTS]


def _bwd_dh(dsegs, w_all, x, dres, mod, norm_g):
    t = x.shape[0]
    tm = min(512, t)
    n_tiles = t // tm
    n_seg = len(_SEGMENTS)
    full = pl.BlockSpec((tm, D), lambda i: (i, 0))

    def body(*refs):
        segs = refs[:n_seg]
        w_hbm, x_ref, dres_ref, mod_ref, ng_ref, gx_ref, vec_ref, w, sems = refs[n_seg:]
        i = pl.program_id(0)

        @pl.when(i == 0)
        def _():
            loads = [
                pltpu.make_async_copy(w_hbm.at[s], w.at[pl.ds(W_IN_SHARD * s, W_IN_SHARD), :], sems.at[s])
                for s in range(N_DEV)
            ]
            for cp in loads:
                cp.start()
            for cp in loads:
                cp.wait()
            vec_ref[...] = jnp.zeros_like(vec_ref)

        dh = sum(_dot(seg[...], w[start : start + width, :]) for seg, (start, width) in zip(segs, _SEGMENTS))
        xv = x_ref[...]
        r1 = lax.rsqrt(jnp.mean(xv * xv, axis=-1, keepdims=True) + NORM_EPS)
        xn = xv * r1
        ng = ng_ref[...]
        scale1 = 1.0 + mod_ref[:, D : 2 * D]
        d_shift = jnp.sum(dh, axis=0, keepdims=True)
        d_scale = jnp.sum(dh * (xn * ng), axis=0, keepdims=True)
        g_norm = jnp.sum(dh * xn * scale1, axis=0, keepdims=True)
        dxn = dh * (ng * scale1)
        dx = r1 * (dxn - xn * jnp.mean(dxn * xn, axis=-1, keepdims=True))
        gx_ref[...] = dres_ref[...] + dx
        vec_ref[...] = vec_ref[...] + jnp.concatenate([d_shift, d_scale, g_norm, jnp.zeros((5, D), F32)], axis=0)

    any_spec = pl.BlockSpec(memory_space=pl.ANY)
    return pl.pallas_call(
        body,
        name="bwd_dh",
        grid=(n_tiles,),
        in_specs=_seg_specs(tm) + [any_spec, full, full, pl.BlockSpec((1, 3 * D), lambda i: (0, 0)), pl.BlockSpec((1, D), lambda i: (0, 0))],
        out_specs=[full, pl.BlockSpec((8, D), lambda i: (0, 0))],
        out_shape=[jax.ShapeDtypeStruct((t, D), F32), jax.ShapeDtypeStruct((8, D), F32)],
        scratch_shapes=[pltpu.VMEM((IN_W, D), BF16), pltpu.SemaphoreType.DMA((N_DEV,))],
        compiler_params=_params(1),
    )(*dsegs, w_all, x, dres, mod, norm_g)


_HBM = pl.BlockSpec(memory_space=pltpu.HBM)
_SEM = pl.BlockSpec(memory_space=pltpu.SEMAPHORE)
_EFFECT = pltpu.SideEffectType.DATAFLOW_SIDE_EFFECTING


def _w_exchange_copies(sum_ref, land_ref, sems):
    me = _position()
    n = len(CHIP_FLIPS)
    return [
        pltpu.make_async_remote_copy(
            src_ref=sum_ref.at[_chip_of(_flip(me, k))], dst_ref=land_ref.at[j], send_sem=sems[j], recv_sem=sems[n + j],
            device_id=_flip(me, k), device_id_type=MESH,
        )
        for j, k in enumerate(CHIP_FLIPS)
    ]


def _w_exchange_start(sum_w_bf):
    n_sems = 2 * len(CHIP_FLIPS)
    land = lax.empty((len(CHIP_FLIPS), W_IN_SHARD, D), BF16)

    def body(sum_ref, land_ref, *outs):
        for cp in _w_exchange_copies(sum_ref, land_ref, outs[:n_sems]):
            cp.start()
        token = outs[n_sems + 2]
        token[...] = jnp.zeros_like(token)

    res = pl.pallas_call(
        body,
        name="w_exchange_start",
        out_shape=(*[pltpu.SemaphoreType.DMA(())] * n_sems, pltpu.HBM(sum_w_bf.shape, BF16), pltpu.HBM(land.shape, BF16),
                   jax.ShapeDtypeStruct((8, 128), F32)),
        in_specs=(_HBM, _HBM),
        out_specs=(*[_SEM] * n_sems, _HBM, _HBM, pl.BlockSpec(memory_space=pltpu.VMEM)),
        input_output_aliases={0: n_sems, 1: n_sems + 1},
        compiler_params=pltpu.CompilerParams(has_side_effects=_EFFECT),
    )(pltpu.with_memory_space_constraint(sum_w_bf, pltpu.HBM), pltpu.with_memory_space_constraint(land, pltpu.HBM))
    return res[:n_sems], res[n_sems], res[n_sems + 1], res[n_sems + 2]


def _w_exchange_wait(sems, sum_thru, land_thru, after):
    n_sems = len(sems)

    def body(sum_ref, land_ref, *rest):
        for cp in _w_exchange_copies(sum_ref, land_ref, rest[:n_sems]):
            cp.wait_send()
            cp.wait_recv()

    return pl.pallas_call(
        body,
        name="w_exchange_wait",
        out_shape=(pltpu.HBM(sum_thru.shape, BF16), pltpu.HBM(land_thru.shape, BF16)),
        in_specs=(_HBM, _HBM, *[_SEM] * n_sems, pl.BlockSpec(memory_space=pl.ANY)),
        out_specs=(_HBM, _HBM),
        input_output_aliases={0: 0, 1: 1},
        compiler_params=pltpu.CompilerParams(has_side_effects=_EFFECT),
    )(sum_thru, land_thru, *sems, after)[1]


def _bwd_gw(dsegs, h, g_gates):
    t = h.shape[0]
    tm = min(512, t)
    n_tiles = t // tm
    n_seg = len(_SEGMENTS)
    n_chips = N_DEV // 2

    def gate_copies(g_hbm, lg_hbm, send_sems, recv_sems, arriving):
        me = _position()
        return [
            pltpu.make_async_remote_copy(
                src_ref=g_hbm.at[:, :, _slot(_flip(me, k))],
                dst_ref=lg_hbm.at[_slot(_flip(me, k) if arriving else me)],
                send_sem=send_sems.at[k - 1],
                recv_sem=recv_sems.at[k - 1],
                device_id=_flip(me, k),
                device_id_type=MESH,
            )
            for k in range(1, N_DEV)
        ]

    def body(*refs):
        segs = refs[:n_seg]
        (h_ref, g_hbm, own_hbm, sum_hbm, lg_hbm, acc, stage, land, own_sems, pair_send, pair_recv, send_sems, recv_sems, g_local) = refs[n_seg:]
        i = pl.program_id(0)
        slot = _slot(_position())
        own_gates = pltpu.make_async_copy(g_hbm.at[:, :, slot], lg_hbm.at[slot], g_local)

        @pl.when(i == 0)
        def _():
            own_gates.start()
            for cp in gate_copies(g_hbm, lg_hbm, send_sems, recv_sems, arriving=False):
                cp.start()
            acc[...] = jnp.zeros_like(acc)

        hv = h_ref[...]
        for seg, (start, width) in zip(segs, _SEGMENTS):
            acc[start : start + width, :] = acc[start : start + width, :] + _dot_tn(seg[...], hv)

        @pl.when(i == n_tiles - 1)
        def _():
            me = _position()
            sibling = _flip(me, 1)

            def rows(q, c):
                return pl.ds(pl.multiple_of(W_IN_SHARD * (2 * q + c), 8), W_IN_SHARD)

            def to_sibling(q):
                return pltpu.make_async_remote_copy(
                    src_ref=stage.at[q % 2], dst_ref=land.at[q], send_sem=pair_send.at[q], recv_sem=pair_recv.at[q],
                    device_id=sibling, device_id_type=MESH,
                )

            for q in range(n_chips):
                if q >= 2:
                    to_sibling(q - 2).wait_send()
                stage[q % 2] = acc[rows(q, sibling[2]), :].astype(BF16)
                to_sibling(q).start()
            for q in range(n_chips - 2, n_chips):
                to_sibling(q).wait_send()
            sums = [pltpu.make_async_copy(stage.at[q % 2], sum_hbm.at[q], own_sems.at[q]) for q in range(n_chips)]
            for q in range(n_chips):
                to_sibling(q).wait_recv()
                if q >= 2:
                    sums[q - 2].wait()
                total = acc[rows(q, me[2]), :] + land[q].astype(F32)
                acc[rows(q, me[2]), :] = total
                stage[q % 2] = total.astype(BF16)
                sums[q].start()
            mine = pltpu.make_async_copy(acc.at[rows(_chip_of(me), me[2]), :], own_hbm, own_sems.at[n_chips])
            mine.start()
            for q in range(n_chips - 2, n_chips):
                sums[q].wait()
            mine.wait()
            for cp in gate_copies(g_hbm, lg_hbm, send_sems, recv_sems, arriving=True):
                cp.wait_recv()
            for cp in gate_copies(g_hbm, lg_hbm, send_sems, recv_sems, arriving=False):
                cp.wait_send()
            own_gates.wait()

    any_spec = pl.BlockSpec(memory_space=pl.ANY)
    n_copies = N_DEV - 1
    return pl.pallas_call(
        body,
        name="bwd_gw",
        grid=(n_tiles,),
        in_specs=_seg_specs(tm) + [pl.BlockSpec((tm, D), lambda i: (i, 0)), any_spec],
        out_specs=[any_spec, any_spec, any_spec],
        out_shape=[jax.ShapeDtypeStruct((W_IN_SHARD, D), F32),
                   jax.ShapeDtypeStruct((n_chips, W_IN_SHARD, D), BF16),
                   jax.ShapeDtypeStruct((N_DEV, 2, 4, GATE_SHARD, 256), F32)],
        scratch_shapes=[pltpu.VMEM((IN_W, D), F32), pltpu.VMEM((2, W_IN_SHARD, D), BF16), pltpu.VMEM((n_chips, W_IN_SHARD, D), BF16),
                        pltpu.SemaphoreType.DMA((n_chips + 1,)), pltpu.SemaphoreType.DMA((n_chips,)), pltpu.SemaphoreType.DMA((n_chips,)),
                        pltpu.SemaphoreType.DMA((n_copies,)), pltpu.SemaphoreType.DMA((n_copies,)), pltpu.SemaphoreType.DMA],
        compiler_params=_params(1),
    )(*dsegs, h, g_gates)


def _adamw(w, g, m, v):
    m = ADAM_B1 * m + (1.0 - ADAM_B1) * g
    v = ADAM_B2 * v + (1.0 - ADAM_B2) * (g * g)
    m_hat = m / (1.0 - ADAM_B1**ADAM_STEP)
    v_hat = v / (1.0 - ADAM_B2**ADAM_STEP)
    delta = -ADAM_LR * (m_hat / (jnp.sqrt(v_hat) + ADAM_EPS) + ADAM_WD * w)
    return delta, m, v


def _w_in_update(own, land, w_t, m_t, v_t):
    n_slots, rows, cols = land.shape
    rb = 208
    spec = pl.BlockSpec((rb, cols), lambda i: (i, 0))

    def body(o_ref, l_ref, w_ref, m_ref, v_ref, g_ref, d_ref, m2_ref, v2_ref):
        total = o_ref[...]
        for s in range(n_slots):
            total = total + l_ref[s].astype(F32)
        g_ref[...] = total
        d_ref[...], m2_ref[...], v2_ref[...] = _adamw(w_ref[...], total, m_ref[...], v_ref[...])

    shape = jax.ShapeDtypeStruct((rows, cols), F32)
    return pl.pallas_call(
        body,
        name="w_in_update",
        grid=(rows // rb,),
        in_specs=[spec, pl.BlockSpec((n_slots, rb, cols), lambda i: (0, i, 0)), spec, spec, spec],
        out_specs=[spec] * 4,
        out_shape=[shape] * 4,
        compiler_params=_params(1),
    )(own, land, w_t, m_t, v_t)


def _shard_update(own_3, land_3, land_g, weights, m_in, v_in):
    n = len(weights)

    def body(*refs):
        o3_ref, l3_ref, lg_ref = refs[:3]
        w_refs, m_refs, v_refs = (refs[3 + n * k : 3 + n * (k + 1)] for k in range(3))
        outs = refs[3 + 3 * n :]
        for k in range(n):
            if k < 3:
                g = o3_ref[k]
                for slot in range(3):
                    g = g + l3_ref[slot, k].astype(F32)
            else:
                g = lg_ref[0, k - 3]
                for slot in range(1, N_DEV):
                    g = g + lg_ref[slot, k - 3]
            d, m2, v2 = _adamw(w_refs[k][0], g, m_refs[k][0], v_refs[k][0])
            for which, val in enumerate((g, d, m2, v2)):
                outs[which * n + k][0] = val

    shapes = [jax.ShapeDtypeStruct(w.shape, F32) for w in weights]
    res = pl.pallas_call(
        body, name="shard_update", out_shape=shapes * 4, compiler_params=pltpu.CompilerParams(vmem_limit_bytes=VMEM_LIMIT)
    )(own_3, land_3, land_g, *weights, *m_in, *v_in)
    return [res[n * k : n * (k + 1)] for k in range(4)]


VEC_LAYOUT = (("b_ada", 3 * D), ("norm_g", D), ("attn_sinks", 128), ("conv_b", D), ("rg_ba", D), ("rg_bx", D), ("rg_lambda", D),
              ("final_g", D), ("conv_w", CONV_WIDTH * D), ("loss", 128))
VEC_OFFSET = {name: sum(w for _, w in VEC_LAYOUT[:k]) for k, (name, _) in enumerate(VEC_LAYOUT)}
VEC_ROWS = 112
VEC_WIDTH = VEC_ROWS * 128
assert sum(w for _, w in VEC_LAYOUT) <= VEC_WIDTH
N_REPLICATED = 8


def _small_update(vec_all, c_all_t, me, rep_w, rep_m, rep_v, ada, conv):
    n_ada = ada[0].shape[1]
    n_conv = conv[0].shape[1]

    def body(*refs):
        me_ref, vec_ref, ct_ref = refs[:3]
        w_refs, m_refs, v_refs = (refs[3 + N_REPLICATED * k : 3 + N_REPLICATED * (k + 1)] for k in range(3))
        pos = 3 + 3 * N_REPLICATED
        ada_refs, conv_refs = refs[pos : pos + 3], refs[pos + 3 : pos + 6]
        outs = refs[pos + 6 :]
        loss_ref, rep_outs = outs[0], outs[1 : 1 + 4 * N_REPLICATED]
        ada_outs, conv_outs = outs[1 + 4 * N_REPLICATED : 5 + 4 * N_REPLICATED], outs[5 + 4 * N_REPLICATED : 9 + 4 * N_REPLICATED]
        conv_buf = outs[9 + 4 * N_REPLICATED]

        total = vec_ref[0:1, :]
        for slot in range(1, N_DEV):
            total = total + vec_ref[slot : slot + 1, :]
        loss_ref[...] = total[:, VEC_OFFSET["loss"] : VEC_OFFSET["loss"] + 1]
        for k, (name, _) in enumerate(VEC_LAYOUT[:N_REPLICATED]):
            width = w_refs[k].shape[1]
            g = total[:, VEC_OFFSET[name] : VEC_OFFSET[name] + width]
            d, m2, v2 = _adamw(w_refs[k][...], g, m_refs[k][...], v_refs[k][...])
            for which, val in enumerate((g, d, m2, v2)):
                rep_outs[which * N_REPLICATED + k][...] = val

        device = me_ref[0]
        for k in range(CONV_WIDTH):
            conv_buf[k : k + 1, :] = total[:, VEC_OFFSET["conv_w"] + D * k : VEC_OFFSET["conv_w"] + D * (k + 1)]
        g_conv = conv_buf[:, pl.ds(pl.multiple_of(device * n_conv, 128), n_conv)]
        for ref, val in zip(conv_outs, (g_conv,) + _adamw(conv_refs[0][...], g_conv, conv_refs[1][...], conv_refs[2][...])):
            ref[...] = val

        dmod = vec_ref[:, pl.ds(pl.multiple_of(device * n_ada, 128), n_ada)]
        g_ada = _dot(ct_ref[...], dmod)
        for ref, val in zip(ada_outs, (g_ada,) + _adamw(ada_refs[0][...], g_ada, ada_refs[1][...], ada_refs[2][...])):
            ref[...] = val

    vmem = pl.BlockSpec(memory_space=pltpu.VMEM)
    rep_shapes = [jax.ShapeDtypeStruct(w.shape, F32) for w in rep_w]
    out_shape = ([jax.ShapeDtypeStruct((1, 1), F32)] + rep_shapes * 4 + [jax.ShapeDtypeStruct(ada[0].shape, F32)] * 4
                 + [jax.ShapeDtypeStruct(conv[0].shape, F32)] * 4)
    n_in = 3 + 3 * N_REPLICATED + 6
    res = pl.pallas_call(
        body,
        name="small_update",
        in_specs=[pl.BlockSpec(memory_space=pltpu.SMEM)] + [vmem] * (n_in - 1),
        out_specs=[vmem] * len(out_shape),
        out_shape=out_shape,
        scratch_shapes=[pltpu.VMEM((CONV_WIDTH, D), F32)],
        compiler_params=pltpu.CompilerParams(vmem_limit_bytes=VMEM_LIMIT),
    )(me, vec_all, c_all_t, *rep_w, *rep_m, *rep_v, *ada, *conv)
    loss = res[0]
    reps = [res[1 + N_REPLICATED * k : 1 + N_REPLICATED * (k + 1)] for k in range(4)]
    return loss, reps, res[1 + 4 * N_REPLICATED : 5 + 4 * N_REPLICATED], res[5 + 4 * N_REPLICATED : 9 + 4 * N_REPLICATED]


def kernel(x, c, positions, w_ada, b_ada, norm_g, w_in, attn_sinks, conv_w, conv_b, rg_wa, rg_ba, rg_wx, rg_bx, rg_lambda, w_attn_proj, w_rnn_proj, w_out, final_g, loss_target, m_w_ada, m_b_ada, m_norm_g, m_w_in, m_attn_sinks, m_conv_w, m_conv_b, m_rg_wa, m_rg_ba, m_rg_wx, m_rg_bx, m_rg_lambda, m_w_attn_proj, m_w_rnn_proj, m_w_out, m_final_g, v_w_ada, v_b_ada, v_norm_g, v_w_in, v_attn_sinks, v_conv_w, v_conv_b, v_rg_wa, v_rg_ba, v_rg_wx, v_rg_bx, v_rg_lambda, v_w_attn_proj, v_w_rnn_proj, v_w_out, v_final_g):
    my_x, my_y, my_c = _position()
    me = _slot((my_x, my_y, my_c))
    x2, target = x[0], loss_target[0]
    pos = positions[0]

    gate_rows = lambda w: w[0].transpose(1, 0, 2).reshape(GATE_SHARD, D)
    w_pack = jnp.concatenate([w_attn_proj[0], w_rnn_proj[0], w_out[0], gate_rows(rg_wa), gate_rows(rg_wx)], axis=0).astype(BF16)
    conv_rows = jnp.pad(conv_w[0], ((0, 8 - CONV_WIDTH), (0, 0)))
    w_in_t = w_in[0].T
    n_ada = w_ada.shape[2]
    b_cols = lax.dynamic_slice(b_ada, (0, me * n_ada), (1, n_ada))

    tables = _rope_tables(pos)
    where = jnp.stack([_chip_of((my_x, my_y)), me]).astype(jnp.int32)
    proj, h, mod, w_t_all, w_all, c_rows, conv_all = _fwd_in(
        x2, jnp.broadcast_to(c, (8, D)), conv_rows, w_ada[0], b_cols, norm_g, w_in_t.astype(BF16), tables[0], w_pack, where
    )
    c_all = c_rows[:, 0, :]
    conv_full = conv_all[:, :CONV_WIDTH, :].transpose(1, 0, 2).reshape(CONV_WIDTH, D)
    gate = mod[:, 2 * D :]
    o = _attn_fwd(proj, attn_sinks)
    keep = (pos != 0).astype(F32)[:, None]
    hh = _rnn_fwd(proj, keep, w_all, rg_ba, rg_bx, rg_lambda, conv_full, conv_b)

    dres, d_o, d_hh, d_ga, d_gr, d_ma, d_mr, gw3, mid_vec = _mid(x2, target, o, hh, proj, w_all, gate, final_g[None, :])
    index = jnp.stack([my_c, _chip_of((my_x, my_y))]).astype(jnp.int32)
    part_3 = gw3.reshape(3, 4, 2, ROW_SHARD, D)
    d_q, d_k, d_v, d_sink, recv_3 = _attn_bwd(proj, o, d_o, tables, attn_sinks, part_3)
    sum_3_bf, own_3 = _pair_add(
        "pair_add_3", part_3, recv_3, index, (4,), 0,
        pl.BlockSpec((3, None, None, ROW_SHARD, D), lambda q, idx: (0, q, idx[0], 0, 0)),
        pl.BlockSpec((3, None, ROW_SHARD, D), lambda q, idx: (0, q, 0, 0)),
        pl.BlockSpec((3, ROW_SHARD, D), lambda q, idx: (0, 0, 0)), (3, ROW_SHARD, D),
    )
    d_xr, g_gates, rnn_vec, land_3 = _rnn_bwd(proj, hh, d_hh, keep, w_all, rg_ba, rg_bx, rg_lambda, conv_full, conv_b, sum_3_bf)
    dsegs = (d_q, d_k, d_v, d_ga, d_xr, d_gr, d_ma, d_mr)
    own_w, sum_w_bf, land_g = _bwd_gw(dsegs, h, g_gates.reshape(2, 4, N_DEV, GATE_SHARD, 256))
    w_sems, sum_w_thru, land_w_thru, token = _w_exchange_start(sum_w_bf)
    grad_x, dh_vec = _bwd_dh(dsegs, w_t_all, x2, dres, mod + token[0:1, 0:1], norm_g)
    land_w = _w_exchange_wait(w_sems, sum_w_thru, land_w_thru, dh_vec)

    pieces = dict(
        b_ada=jnp.concatenate([dh_vec[0:1], dh_vec[1:2], mid_vec[1:2]], axis=1),
        norm_g=dh_vec[2:3],
        attn_sinks=jnp.pad(d_sink[:, :, 0].reshape(1, 16), ((0, 0), (0, 112))),
        conv_b=rnn_vec[3:4],
        rg_ba=rnn_vec[0:1],
        rg_bx=rnn_vec[1:2],
        rg_lambda=rnn_vec[2:3],
        final_g=mid_vec[0:1],
        conv_w=rnn_vec[4:8].reshape(1, CONV_WIDTH * D),
        loss=mid_vec[2:3, 0:128],
    )
    vec = jnp.concatenate([pieces[name] for name, _ in VEC_LAYOUT], axis=1)
    vec = jnp.pad(vec, ((0, 0), (0, VEC_WIDTH - vec.shape[1]))).reshape(VEC_ROWS, 128)
    (vec_all,) = _all_gather_direct("ag_vec", [vec])
    rep_w = [b_ada, norm_g, attn_sinks, conv_b, rg_ba, rg_bx, rg_lambda, final_g[None, :]]
    rep_m = [m_b_ada, m_norm_g, m_attn_sinks, m_conv_b, m_rg_ba, m_rg_bx, m_rg_lambda, m_final_g[None, :]]
    rep_v = [v_b_ada, v_norm_g, v_attn_sinks, v_conv_b, v_rg_ba, v_rg_bx, v_rg_lambda, v_final_g[None, :]]
    loss, reps, ada_res, conv_res = _small_update(
        vec_all.reshape(N_DEV, VEC_WIDTH), c_all.T, me.astype(jnp.int32).reshape(1), rep_w, rep_m, rep_v,
        (w_ada[0], m_w_ada[0], v_w_ada[0]), (conv_w[0], m_conv_w[0], v_conv_w[0]),
    )

    w_in_res = _w_in_update(own_w, land_w, w_in_t, m_w_in[0].T, v_w_in[0].T)
    shard_res = _shard_update(
        own_3, land_3, land_g,
        (w_attn_proj, w_rnn_proj, w_out, rg_wa, rg_wx),
        (m_w_attn_proj, m_w_rnn_proj, m_w_out, m_rg_wa, m_rg_wx),
        (v_w_attn_proj, v_w_rnn_proj, v_w_out, v_rg_wa, v_rg_wx),
    )

    def by_weight(k):
        b, n, s, cb, ba, bx, lam, fg = reps[k]
        ap, rp, out, wa, wx = shard_res[k]
        return [ada_res[k][None], b, n, w_in_res[k].T[None], s, conv_res[k][None], cb, wa, ba, wx, bx, lam, ap, rp, out, fg[0]]

    return (loss[0, 0], grad_x[None], *by_weight(0), *by_weight(1), *by_weight(2), *by_weight(3))
```

```python
import math

import jax
import jax.numpy as jnp
from jax import lax
from jax.experimental import pallas as pl
from jax.experimental.pallas import tpu as pltpu

F32 = jnp.float32
BF16 = jnp.bfloat16

D = 1024
HEAD_DIM = 64
ROT_DIM = 16
ROPE_THETA = 500000.0
WINDOW = 128
LRU_C = 8.0
CONV_WIDTH = 4
NORM_EPS = 1e-6
IN_W = 6656
SEG_Q, SEG_K, SEG_V, SEG_GA, SEG_XR, SEG_GR, SEG_MA, SEG_MR = 0, 1024, 1280, 1536, 2560, 3584, 4608, 5632
N_DEV = 8
W_IN_SHARD = IN_W // N_DEV
ROW_SHARD = D // N_DEV
GATE_SHARD = 256 // N_DEV

ADAM_LR = 0.001
ADAM_B1 = 0.9
ADAM_B2 = 0.999
ADAM_EPS = 1e-08
ADAM_WD = 0.01
ADAM_STEP = 10

VMEM_LIMIT = 60 * 1024 * 1024
MESH = pl.DeviceIdType.MESH


def _dot(a, b):
    return jnp.dot(a, b, preferred_element_type=F32)


def _dot_nt(a, b):
    return lax.dot_general(a, b, (((1,), (1,)), ((), ())), preferred_element_type=F32)


def _dot_tn(a, b):
    return lax.dot_general(a, b, (((0,), (0,)), ((), ())), preferred_element_type=F32)


def _sigmoid(z):
    return 1.0 / (1.0 + jnp.exp(-z))


def _params(n_grid, vmem=VMEM_LIMIT):
    return pltpu.CompilerParams(dimension_semantics=("arbitrary",) * n_grid, vmem_limit_bytes=vmem)


def _position():
    return lax.axis_index("x"), lax.axis_index("y"), lax.axis_index("c")


def _flip(pos, k):
    x, y, c = pos
    return (1 - x if k & 4 else x, 1 - y if k & 2 else y, 1 - c if k & 1 else c)


def _slot(pos):
    return 4 * pos[0] + 2 * pos[1] + pos[2]


def _all_gather_direct(name, arrs):
    n = len(arrs)

    def body(*refs):
        ins, outs = refs[:n], refs[n : 2 * n]
        send_sems, recv_sems, local_sems = refs[2 * n :]
        me = _position()
        mine = [pltpu.make_async_copy(ins[a], outs[a].at[_slot(me)], local_sems.at[a]) for a in range(n)]
        for cp in mine:
            cp.start()

        def copy(a, k, block):
            return pltpu.make_async_remote_copy(
                src_ref=ins[a],
                dst_ref=outs[a].at[_slot(block)],
                send_sem=send_sems.at[a * 7 + k - 1],
                recv_sem=recv_sems.at[a * 7 + k - 1],
                device_id=_flip(me, k),
                device_id_type=MESH,
            )

        sent = [copy(a, k, me) for a in range(n) for k in range(1, N_DEV)]
        for cp in sent:
            cp.start()
        for a in range(n):
            for k in range(1, N_DEV):
                copy(a, k, _flip(me, k)).wait_recv()
        for cp in sent:
            cp.wait_send()
        for cp in mine:
            cp.wait()

    any_spec = pl.BlockSpec(memory_space=pl.ANY)
    return pl.pallas_call(
        body,
        name=name,
        out_shape=[jax.ShapeDtypeStruct((N_DEV,) + a.shape, a.dtype) for a in arrs],
        in_specs=[any_spec] * n,
        out_specs=[any_spec] * n,
        scratch_shapes=[
            pltpu.SemaphoreType.DMA((7 * n,)),
            pltpu.SemaphoreType.DMA((7 * n,)),
            pltpu.SemaphoreType.DMA((n,)),
        ],
    )(*arrs)


def _direct_gather_copies(src_hbm, all_hbm, send_sems, recv_sems, arriving):
    me = _position()
    return [
        pltpu.make_async_remote_copy(
            src_ref=src_hbm,
            dst_ref=all_hbm.at[_slot(_flip(me, k) if arriving else me)],
            send_sem=send_sems.at[k - 1],
            recv_sem=recv_sems.at[k - 1],
            device_id=_flip(me, k),
            device_id_type=MESH,
        )
        for k in range(1, N_DEV)
    ]


CHIP_FLIPS = (2, 4, 6)


def _chip_of(pos):
    return 2 * pos[0] + pos[1]


def _chip_copies(srcs, lands, views, send_sems, recv_sems):
    me = _position()
    copies = []
    for a, (src, land) in enumerate(zip(srcs, lands)):
        for j, k in enumerate(CHIP_FLIPS):
            peer = _flip(me, k)
            copies.append(
                pltpu.make_async_remote_copy(
                    src_ref=views[a](src, peer),
                    dst_ref=land.at[j],
                    send_sem=send_sems.at[len(CHIP_FLIPS) * a + j],
                    recv_sem=recv_sems.at[len(CHIP_FLIPS) * a + j],
                    device_id=peer,
                    device_id_type=MESH,
                )
            )
    return copies


PACK_GATE = 3 * ROW_SHARD
PACK_ROWS = PACK_GATE + 2 * GATE_SHARD


def _fwd_in(x, c_rows, conv_rows, w_ada, b_cols, norm_g, w_shard, tables, pack, where):
    t = x.shape[0]
    tm = min(1024, t)
    tn = IN_W // 4
    n_i, n_j = t // tm, IN_W // tn
    n_rope = (SEG_V - SEG_Q) // 128

    n_ada = w_ada.shape[1]

    def body(where_ref, x_ref, wada_ref, bcols_ref, ng_ref, cos_ref, s1_ref, s2_ref, shard_hbm, pack_hbm, c_hbm, conv_hbm,
             proj_ref, h_ref, mod_ref, wt_hbm, all_hbm, call_hbm, convall_hbm,
             wall, hbuf, cbuf, mcols, mod_all, mod_buf,
             w_send, w_recv, load_sems, local_sem, p_send, p_recv, p_local,
             c_send, c_recv, c_local, v_send, v_recv, v_local, m_send, m_recv):
        j = pl.program_id(0)
        i = pl.program_id(1)
        me = _position()
        sibling = _flip(me, 1)
        chip = where_ref[0]
        slot = where_ref[1]

        def modulation():
            own_c = pltpu.make_async_copy(c_hbm, call_hbm.at[_slot(me)], c_local)
            own_v = pltpu.make_async_copy(conv_hbm, convall_hbm.at[_slot(me)], v_local)
            own_c.start()
            own_v.start()
            for cp in _direct_gather_copies(c_hbm, call_hbm, c_send, c_recv, arriving=False):
                cp.start()
            for cp in _direct_gather_copies(conv_hbm, convall_hbm, v_send, v_recv, arriving=False):
                cp.start()
            for cp in _direct_gather_copies(c_hbm, call_hbm, c_send, c_recv, arriving=True):
                cp.wait_recv()
            own_c.wait()
            fetch = pltpu.make_async_copy(call_hbm, cbuf, c_local)
            fetch.start()
            fetch.wait()
            mcols[...] = _dot(cbuf[:, 0, :].astype(BF16), wada_ref[...].astype(BF16)) + bcols_ref[...]
            mod_all[slot] = mcols[...]
            swaps = lambda arriving: [
                pltpu.make_async_remote_copy(
                    src_ref=mcols,
                    dst_ref=mod_all.at[_slot(_flip(me, k) if arriving else me)],
                    send_sem=m_send.at[k - 1],
                    recv_sem=m_recv.at[k - 1],
                    device_id=_flip(me, k),
                    device_id_type=MESH,
                )
                for k in range(1, N_DEV)
            ]
            for cp in swaps(False):
                cp.start()
            for cp in swaps(True):
                cp.wait_recv()
            mine = lax.broadcasted_iota(jnp.int32, (N_DEV, n_ada), 0) == slot
            for s in range(N_DEV):
                row = jnp.sum(jnp.where(mine, mod_all[s], 0.0), axis=0, keepdims=True)
                mod_buf[:, n_ada * s : n_ada * (s + 1)] = row
            for cp in swaps(False):
                cp.wait_send()
            for cp in _direct_gather_copies(c_hbm, call_hbm, c_send, c_recv, arriving=False):
                cp.wait_send()
            for cp in _direct_gather_copies(conv_hbm, convall_hbm, v_send, v_recv, arriving=True):
                cp.wait_recv()
            for cp in _direct_gather_copies(conv_hbm, convall_hbm, v_send, v_recv, arriving=False):
                cp.wait_send()
            own_v.wait()

        def shard_of(block):
            return wall.at[_chip_of(block), pl.ds(W_IN_SHARD * block[2], W_IN_SHARD), :]

        def shard_copy(k, block, to, src=None):
            return pltpu.make_async_remote_copy(
                src_ref=shard_of(block),
                dst_ref=shard_of(block),
                send_sem=w_send.at[k],
                recv_sem=w_recv.at[k],
                device_id=to,
                device_id_type=MESH,
            )

        own_shard = pltpu.make_async_copy(shard_hbm, shard_of(me), local_sem)
        own_pack = pltpu.make_async_copy(pack_hbm, all_hbm.at[_slot(me)], p_local)

        def keep_tile(q):
            return [
                pltpu.make_async_copy(wall.at[q, pl.ds(W_IN_SHARD * c, W_IN_SHARD), :], wt_hbm.at[2 * q + c], load_sems.at[2 * (q ^ chip) + c])
                for c in range(2)
            ]

        @pl.when((j == 0) & (i == 0))
        def _():
            modulation()
            mod_ref[...] = mod_buf[...]
            own_shard.start()
            own_shard.wait()
            shard_copy(0, me, sibling).start()
            for rel in range(1, 4):
                shard_copy(rel, me, _flip(me, 2 * rel)).start()
            own_pack.start()
            for cp in _direct_gather_copies(pack_hbm, all_hbm, p_send, p_recv, arriving=False):
                cp.start()
            shard_copy(0, sibling, me).wait_recv()
            for cp in keep_tile(chip):
                cp.start()

        for rel in range(1, 4):

            @pl.when((j == rel) & (i == 0))
            def _(rel=rel):
                there = _flip(me, 2 * rel)
                shard_copy(rel, there, me).wait_recv()
                shard_copy(3 + rel, there, sibling).start()
                shard_copy(3 + rel, _flip(sibling, 2 * rel), me).wait_recv()
                for cp in keep_tile(chip ^ rel):
                    cp.start()

        rows = pl.ds(pl.multiple_of(i * tm, tm), tm)

        @pl.when(j == 0)
        def _():
            xv = x_ref[...]
            r = lax.rsqrt(jnp.mean(xv * xv, axis=-1, keepdims=True) + NORM_EPS)
            shift = mod_buf[:, 0:D]
            scale = mod_buf[:, D : 2 * D]
            hv = (((xv * r) * ng_ref[...]) * (1.0 + scale) + shift).astype(BF16)
            hbuf[rows, :] = hv
            h_ref[...] = hv

        proj_ref[...] = _dot_nt(hbuf[rows, :], wall[chip ^ j])

        @pl.when((chip ^ j) == 0)
        def _():
            cos, s1, s2 = cos_ref[...], s1_ref[...], s2_ref[...]
            for g in range(n_rope):
                cols = slice(128 * g, 128 * (g + 1))
                proj_ref[:, cols] = _rope(proj_ref[:, cols], cos, s1, s2)

        @pl.when((j == n_j - 1) & (i == n_i - 1))
        def _():
            shard_copy(0, me, sibling).wait_send()
            for rel in range(1, 4):
                shard_copy(rel, me, _flip(me, 2 * rel)).wait_send()
                shard_copy(3 + rel, _flip(me, 2 * rel), sibling).wait_send()
            for q in range(N_DEV // 2):
                for cp in keep_tile(q):
                    cp.wait()
            for cp in _direct_gather_copies(pack_hbm, all_hbm, p_send, p_recv, arriving=True):
                cp.wait_recv()
            for cp in _direct_gather_copies(pack_hbm, all_hbm, p_send, p_recv, arriving=False):
                cp.wait_send()
            own_pack.wait()

    tab = pl.BlockSpec((tm, 128), lambda j, i, c: (i, 0))
    any_spec = pl.BlockSpec(memory_space=pl.ANY)
    return pl.pallas_call(
        body,
        name="fwd_in",
        grid_spec=pltpu.PrefetchScalarGridSpec(
            num_scalar_prefetch=1,
            grid=(n_j, n_i),
            in_specs=[
                pl.BlockSpec((tm, D), lambda j, i, c: (jnp.where(j == 0, i, 0), 0)),
                pl.BlockSpec(w_ada.shape, lambda j, i, c: (0, 0)),
                pl.BlockSpec((1, n_ada), lambda j, i, c: (0, 0)),
                pl.BlockSpec((1, D), lambda j, i, c: (0, 0)),
                tab, tab, tab, any_spec, any_spec, any_spec, any_spec,
            ],
            out_specs=[pl.BlockSpec((tm, tn), lambda j, i, c: (i, c[0] ^ j)),
                       pl.BlockSpec((tm, D), lambda j, i, c: (jnp.where(j == 0, i, n_i - 1), 0)),
                       pl.BlockSpec((1, 3 * D), lambda j, i, c: (0, 0)),
                       any_spec, any_spec, any_spec, any_spec],
            scratch_shapes=[
                pltpu.VMEM((N_DEV // 2, tn, D), BF16),
                pltpu.VMEM((t, D), BF16),
                pltpu.VMEM((N_DEV,) + c_rows.shape, F32),
                pltpu.VMEM((N_DEV, n_ada), F32),
                pltpu.VMEM((N_DEV, N_DEV, n_ada), F32),
                pltpu.VMEM((1, 3 * D), F32),
                pltpu.SemaphoreType.DMA((N_DEV - 1,)),
                pltpu.SemaphoreType.DMA((N_DEV - 1,)),
                pltpu.SemaphoreType.DMA((N_DEV,)),
                pltpu.SemaphoreType.DMA,
                pltpu.SemaphoreType.DMA((N_DEV - 1,)),
                pltpu.SemaphoreType.DMA((N_DEV - 1,)),
                pltpu.SemaphoreType.DMA,
                pltpu.SemaphoreType.DMA((N_DEV - 1,)),
                pltpu.SemaphoreType.DMA((N_DEV - 1,)),
                pltpu.SemaphoreType.DMA,
                pltpu.SemaphoreType.DMA((N_DEV - 1,)),
                pltpu.SemaphoreType.DMA((N_DEV - 1,)),
                pltpu.SemaphoreType.DMA,
                pltpu.SemaphoreType.DMA((N_DEV - 1,)),
                pltpu.SemaphoreType.DMA((N_DEV - 1,)),
            ],
        ),
        out_shape=[jax.ShapeDtypeStruct((t, IN_W), F32), jax.ShapeDtypeStruct((t, D), BF16),
                   jax.ShapeDtypeStruct((1, 3 * D), F32),
                   jax.ShapeDtypeStruct((N_DEV,) + w_shard.shape, w_shard.dtype),
                   jax.ShapeDtypeStruct((N_DEV,) + pack.shape, pack.dtype),
                   jax.ShapeDtypeStruct((N_DEV,) + c_rows.shape, F32),
                   jax.ShapeDtypeStruct((N_DEV,) + conv_rows.shape, F32)],
        compiler_params=_params(2),
    )(where, x, w_ada, b_cols, norm_g, *tables, w_shard, pack, c_rows, conv_rows)


def _rope_tables(positions):
    half = ROT_DIM // 2
    inv_freq = ROPE_THETA ** (-jnp.arange(0, ROT_DIM, 2, dtype=F32) / ROT_DIM)
    pos = positions.astype(F32)
    lane = jnp.arange(128) % HEAD_DIM
    freq_lane = jnp.where(lane < ROT_DIM, jnp.tile(inv_freq, 128 // half), 0.0)
    ang = pos[:, None] * freq_lane[None, :]
    sin = jnp.sin(ang)
    lane_tables = (jnp.cos(ang), jnp.where(lane < half, -sin, 0.0), jnp.where((lane >= half) & (lane < ROT_DIM), sin, 0.0))
    ang_t = inv_freq[:, None] * pos[None, :]
    return lane_tables, (jnp.cos(ang_t), jnp.sin(ang_t))


def _rope(tv, cos, s1, s2):
    cols = []
    for g in range(tv.shape[1] // 128):
        blk = tv[:, 128 * g : 128 * (g + 1)]
        cols.append(blk * cos + pltpu.roll(blk, 128 - 8, 1) * s1 + pltpu.roll(blk, 8, 1) * s2)
    return cols[0] if len(cols) == 1 else jnp.concatenate(cols, axis=1)


def _rope_transposed(gv, cos, s1, s2):
    cols = []
    for g in range(gv.shape[1] // 128):
        blk = gv[:, 128 * g : 128 * (g + 1)]
        cols.append(blk * cos + pltpu.roll(blk * s1, 8, 1) + pltpu.roll(blk * s2, 128 - 8, 1))
    return cols[0] if len(cols) == 1 else jnp.concatenate(cols, axis=1)


def _lane_lo():
    return lax.broadcasted_iota(jnp.int32, (1, 128), 1) < HEAD_DIM


def _row_lo():
    return lax.broadcasted_iota(jnp.int32, (128, 1), 0) < HEAD_DIM


def _split_heads(tile, j):
    lo = _lane_lo()
    parts = []
    for g in (2 * j, 2 * j + 1):
        blk = tile[:, 128 * g : 128 * (g + 1)]
        parts += [jnp.where(lo, blk, 0.0), jnp.where(lo, 0.0, blk)]
    return jnp.concatenate(parts, axis=0)


def _dup_lanes(kv, j):
    rolled = pltpu.roll(kv, HEAD_DIM, 1)
    lo = _lane_lo()
    return jnp.where(lo, kv, rolled) if j == 0 else jnp.where(lo, rolled, kv)


def _fold_lanes(acc):
    return acc + pltpu.roll(acc, HEAD_DIM, 1)


def _fill_bias_t(bias_ref):
    kj = lax.broadcasted_iota(jnp.int32, (256, 128), 0)
    qi = lax.broadcasted_iota(jnp.int32, (256, 128), 1)
    diff = qi + WINDOW - kj
    band = (diff >= 0) & (diff < WINDOW)
    bias_ref[0] = jnp.where(band, 0.0, -1e30)
    bias_ref[1] = jnp.where(band & (kj >= 128), 0.0, -1e30)


def _sink_row(sink_ref, pair, j):
    return jnp.concatenate([jnp.full((1, 128), sink_ref[0, 8 * pair + 4 * j + h], F32) for h in range(4)], axis=1)


def _softmax_t(q4, kw, bias4, sink):
    s = _dot_nt(kw, q4) + bias4
    m = jnp.maximum(jnp.max(s, axis=0, keepdims=True), sink)
    p = jnp.exp(s - m)
    e_sink = jnp.exp(sink - m)
    denom = jnp.sum(p, axis=0, keepdims=True) + e_sink
    return p, e_sink, denom


def _attn_specs(t, tq):
    return dict(
        q=pl.BlockSpec((tq, 512), lambda p, i: (i, p)),
        k=pl.BlockSpec((t, 128), lambda p, i: (0, SEG_K // 128 + p)),
        v=pl.BlockSpec((t, 128), lambda p, i: (0, SEG_V // 128 + p)),
        sink=pl.BlockSpec(memory_space=pltpu.SMEM),
        act=pl.BlockSpec((tq, 512), lambda p, i: (i, p)),
    )


def _attn_fwd(proj, sinks):
    t = proj.shape[0]
    tq = min(512, t)
    nb = tq // 128
    n_tiles = t // tq
    sp = _attn_specs(t, tq)

    def body(q_ref, k_ref, v_ref, sink_ref, o_ref, kd, vdt, bias_ref):
        pair = pl.program_id(0)
        i = pl.program_id(1)

        @pl.when(i == 0)
        def _():
            kv, vv = k_ref[...], v_ref[...]
            for j in range(2):
                kd[j, 0:128, :] = jnp.zeros((128, 128), BF16)
                kd[j, 128:, :] = _dup_lanes(kv, j).astype(BF16)
                vdt[j, :, 0:128] = jnp.zeros((128, 128), BF16)
                vdt[j, :, 128:] = _dup_lanes(vv, j).T.astype(BF16)
            _fill_bias_t(bias_ref)

        row_lo = _row_lo()

        def block(b, carry):
            r0 = pl.multiple_of(b * 128, 128)
            gb = i * nb + b
            keys = pl.ds(pl.multiple_of(gb * 128, 128), 256)
            rows = pl.ds(r0, 128)
            qt = q_ref[rows, :] * (1.0 / math.sqrt(HEAD_DIM))
            bias = bias_ref[jnp.where(gb == 0, 1, 0)]
            bias4 = jnp.concatenate([bias] * 4, axis=1)
            cols = []
            for j in range(2):
                q4 = _split_heads(qt, j).astype(BF16)
                p, _, denom = _softmax_t(q4, kd[j, keys, :], bias4, _sink_row(sink_ref, pair, j))
                ot = _dot(vdt[j, :, keys], p.astype(BF16)) * (1.0 / denom)
                for gi in range(2):
                    z = jnp.where(row_lo, ot[:, 256 * gi : 256 * gi + 128], ot[:, 256 * gi + 128 : 256 * gi + 256])
                    cols.append(z.T)
            o_ref[rows, :] = jnp.concatenate(cols, axis=1).astype(BF16)
            return carry

        lax.fori_loop(0, nb, block, 0, unroll=True)

    return pl.pallas_call(
        body,
        name="attn_fwd",
        grid=(2, n_tiles),
        in_specs=[sp["q"], sp["k"], sp["v"], sp["sink"]],
        out_specs=sp["act"],
        out_shape=jax.ShapeDtypeStruct((t, D), BF16),
        scratch_shapes=[pltpu.VMEM((2, t + 128, 128), BF16), pltpu.VMEM((2, 128, t + 128), BF16), pltpu.VMEM((2, 256, 128), F32)],
        compiler_params=_params(2),
    )(proj, proj, proj, sinks)


def _attn_bwd(proj, o, d_o, tables, sinks, part_3):
    t = proj.shape[0]
    tq = min(512, t)
    nb = tq // 128
    n_tiles = t // tq
    (cos, s1, s2), (cos_t, sin_t) = tables
    sp = _attn_specs(t, tq)
    scale = 1.0 / math.sqrt(HEAD_DIM)
    half = ROT_DIM // 2

    def unrope_rows(z, c, s):
        out = []
        for base in (0, HEAD_DIM):
            g0, g1 = z[base : base + half], z[base + half : base + ROT_DIM]
            out += [g0 * c + g1 * s, g1 * c - g0 * s, z[base + ROT_DIM : base + HEAD_DIM]]
        return jnp.concatenate(out, axis=0)

    def swap(part_hbm, land, send_sem, recv_sem):
        sibling = _flip(_position(), 1)
        return pltpu.make_async_remote_copy(
            src_ref=part_hbm.at[:, :, sibling[2]], dst_ref=land, send_sem=send_sem, recv_sem=recv_sem,
            device_id=sibling, device_id_type=MESH,
        )

    def body(q_ref, k_ref, v_ref, o_ref, do_ref, ct_ref, st_ref, ck, s1k, s2k, sink_ref, part_hbm,
             dq_ref, dk_ref, dv_ref, dsink_ref, sum_hbm, mine_hbm, kd, kdt, vd, bias_ref, dkacc, dvacc, own, land, stage,
             send_sem, recv_sem, local_sems):
        pair = pl.program_id(0)
        i = pl.program_id(1)
        me = _position()
        load_own = pltpu.make_async_copy(part_hbm.at[:, :, me[2]], own, local_sems.at[0])

        @pl.when((pair == 0) & (i == 0))
        def _():
            swap(part_hbm, land, send_sem, recv_sem).start()
            load_own.start()

        @pl.when(i == 0)
        def _():
            kv, vv = k_ref[...], v_ref[...]
            for j in range(2):
                kdup = _dup_lanes(kv, j)
                kd[j, 0:128, :] = jnp.zeros((128, 128), BF16)
                kd[j, 128:, :] = kdup.astype(BF16)
                kdt[j, :, 0:128] = jnp.zeros((128, 128), BF16)
                kdt[j, :, 128:] = kdup.T.astype(BF16)
                vd[j, 0:128, :] = jnp.zeros((128, 128), BF16)
                vd[j, 128:, :] = _dup_lanes(vv, j).astype(BF16)
            _fill_bias_t(bias_ref)
            dkacc[...] = jnp.zeros_like(dkacc)
            dvacc[...] = jnp.zeros_like(dvacc)
            dsink_ref[...] = jnp.zeros_like(dsink_ref)

        row_lo = _row_lo()
        ones = jnp.ones((8, 128), BF16)

        def block(b, carry):
            r0 = pl.multiple_of(b * 128, 128)
            gb = i * nb + b
            keys = pl.ds(pl.multiple_of(gb * 128, 128), 256)
            rows = pl.ds(r0, 128)
            qt = q_ref[rows, :] * scale
            dob = do_ref[rows, :].astype(F32)
            prod = dob * o_ref[rows, :].astype(F32)
            ct, st = ct_ref[:, rows], st_ref[:, rows]
            bias = bias_ref[jnp.where(gb == 0, 1, 0)]
            bias4 = jnp.concatenate([bias] * 4, axis=1)
            cols = []
            for j in range(2):
                q4 = _split_heads(qt, j).astype(BF16)
                kw = kd[j, keys, :]
                p, e_sink, denom = _softmax_t(q4, kw, bias4, _sink_row(sink_ref, pair, j))
                inv = 1.0 / denom
                prob = p * inv
                do4 = _split_heads(dob, j).astype(BF16)
                delta = _dot_nt(ones, _split_heads(prod, j).astype(BF16))[0:1, :]
                dp = _dot_nt(vd[j, keys, :], do4)
                ds = (prob * (dp - delta)).astype(BF16)
                dkacc[j, keys, :] = dkacc[j, keys, :] + _dot(ds, q4)
                dvacc[j, keys, :] = dvacc[j, keys, :] + _dot(prob.astype(BF16), do4)
                dqt = _dot(kdt[j, :, keys], ds) * scale
                dsink = -(e_sink * inv) * delta
                for h in range(4):
                    row = 4 * j + h
                    total = jnp.sum(dsink[:, 128 * h : 128 * (h + 1)], axis=1, keepdims=True)
                    dsink_ref[0, row : row + 1, :] = dsink_ref[0, row : row + 1, :] + jnp.broadcast_to(total, (1, 128))
                for gi in range(2):
                    z = jnp.where(row_lo, dqt[:, 256 * gi : 256 * gi + 128], dqt[:, 256 * gi + 128 : 256 * gi + 256])
                    cols.append(unrope_rows(z, ct, st).T)
            dq_ref[rows, :] = jnp.concatenate(cols, axis=1).astype(BF16)
            return carry

        lax.fori_loop(0, nb, block, 0, unroll=True)

        @pl.when(i == n_tiles - 1)
        def _():
            lo = _lane_lo()
            dk = jnp.where(lo, _fold_lanes(dkacc[0, 128:, :]), _fold_lanes(dkacc[1, 128:, :]))
            dk_ref[...] = _rope_transposed(dk, ck[...], s1k[...], s2k[...]).astype(BF16)
            dv_ref[...] = jnp.where(lo, _fold_lanes(dvacc[0, 128:, :]), _fold_lanes(dvacc[1, 128:, :])).astype(BF16)

        @pl.when((pair == 1) & (i == n_tiles - 1))
        def _():
            swap(part_hbm, land, send_sem, recv_sem).wait()
            load_own.wait()
            own[...] = own[...] + land[...]
            stage[...] = own[...].astype(BF16)
            results = [pltpu.make_async_copy(stage, sum_hbm, local_sems.at[1]),
                       pltpu.make_async_copy(own.at[:, _chip_of(me)], mine_hbm, local_sems.at[2])]
            for cp in results:
                cp.start()
            for cp in results:
                cp.wait()

    kv_out = pl.BlockSpec((t, 128), lambda p, i: (0, p))
    tab_t = pl.BlockSpec((half, tq), lambda p, i: (0, i))
    tab_k = pl.BlockSpec((t, 128), lambda p, i: (0, 0))
    return pl.pallas_call(
        body,
        name="attn_bwd",
        grid=(2, n_tiles),
        in_specs=[sp["q"], sp["k"], sp["v"], sp["act"], sp["act"], tab_t, tab_t, tab_k, tab_k, tab_k, sp["sink"],
                  pl.BlockSpec(memory_space=pl.ANY)],
        out_specs=[sp["act"], kv_out, kv_out, pl.BlockSpec((1, 8, 128), lambda p, i: (p, 0, 0)),
                   pl.BlockSpec(memory_space=pl.ANY), pl.BlockSpec(memory_space=pl.ANY)],
        out_shape=[
            jax.ShapeDtypeStruct((t, D), BF16),
            jax.ShapeDtypeStruct((t, 256), BF16),
            jax.ShapeDtypeStruct((t, 256), BF16),
            jax.ShapeDtypeStruct((2, 8, 128), F32),
            jax.ShapeDtypeStruct((3, 4, ROW_SHARD, D), BF16),
            jax.ShapeDtypeStruct((3, ROW_SHARD, D), F32),
        ],
        scratch_shapes=[pltpu.VMEM((2, t + 128, 128), BF16), pltpu.VMEM((2, 128, t + 128), BF16), pltpu.VMEM((2, t + 128, 128), BF16),
                        pltpu.VMEM((2, 256, 128), F32), pltpu.VMEM((2, t + 128, 128), F32), pltpu.VMEM((2, t + 128, 128), F32),
                        pltpu.VMEM((3, 4, ROW_SHARD, D), F32), pltpu.VMEM((3, 4, ROW_SHARD, D), F32), pltpu.VMEM((3, 4, ROW_SHARD, D), BF16),
                        pltpu.SemaphoreType.DMA, pltpu.SemaphoreType.DMA, pltpu.SemaphoreType.DMA((3,))],
        compiler_params=_params(2),
    )(proj, proj, proj, o, d_o, cos_t, sin_t, cos, s1, s2, sinks, part_3)


HALF = 512


def _softplus(y):
    u = jnp.exp(-jnp.abs(y))
    w = 1.0 + u
    log1p = jnp.where(w == 1.0, u, jnp.log(w) * (u / jnp.where(w == 1.0, 1.0, w - 1.0)))
    return jnp.maximum(y, 0.0) + log1p


def _gate_block(w_ref, blk):
    return w_ref[:, :, 256 * blk : 256 * (blk + 1)].reshape(256, 256)


def _gate_matmul(act_bf, w_ref):
    return jnp.concatenate([_dot(act_bf[:, 256 * blk : 256 * (blk + 1)], _gate_block(w_ref, blk)) for blk in range(2)], axis=1)


def _gate_matmul_t(grad_bf, w_ref):
    return jnp.concatenate([_dot_nt(grad_bf[:, 256 * blk : 256 * (blk + 1)], _gate_block(w_ref, blk)) for blk in range(2)], axis=1)


def _conv_and_gates(xe_ref, tm, cw_ref, cb_ref, wa_ref, wx_ref, ba_ref, bx_ref, lam_ref, keep):
    xc = cb_ref[...] + sum(cw_ref[k : k + 1, :] * xe_ref[pl.ds(5 + k, tm), :] for k in range(CONV_WIDTH))
    xc_bf = xc.astype(BF16)
    r = _sigmoid(_gate_matmul(xc_bf, wa_ref) + ba_ref[...])
    ig = _sigmoid(_gate_matmul(xc_bf, wx_ref) + bx_ref[...])
    sp = _softplus(-lam_ref[...])
    log_a = -LRU_C * r * sp
    tanh = jnp.tanh(-log_a)
    mult = jnp.sqrt(2.0 * tanh / (1.0 + tanh))
    a = jnp.where(keep, jnp.exp(log_a), 0.0)
    mult = jnp.where(keep, mult, 1.0)
    return xc, xc_bf, r, ig, sp, a, mult


def _scan_rows(a_ref, b_ref, h_ref, carry_ref, tm, reverse):
    width = a_ref.shape[1]
    row = lax.broadcasted_iota(jnp.int32, (8, width), 0)

    def group(g, carry):
        gg = tm // 8 - 1 - g if reverse else g
        r0 = pl.multiple_of(gg * 8, 8)
        a = a_ref[pl.ds(r0, 8), :]
        b = b_ref[pl.ds(r0, 8), :]
        for s in (1, 2, 4):
            shift = 8 - s if reverse else s
            a_sh = pltpu.roll(a, shift, 0)
            b_sh = pltpu.roll(b, shift, 0)
            ok = (row < 8 - s) if reverse else (row >= s)
            b = jnp.where(ok, a * b_sh + b, b)
            a = jnp.where(ok, a * a_sh, a)
        h = a * carry + b
        h_ref[pl.ds(r0, 8), :] = h
        last = h[0:1, :] if reverse else h[7:8, :]
        return jnp.broadcast_to(last, (8, width))

    carry_ref[...] = lax.fori_loop(0, tm // 8, group, carry_ref[...])


def _rnn_specs(t, tm, order):
    n = t // tm
    return dict(
        tile=pl.BlockSpec((tm, HALF), lambda ch, i: (order(i, n), ch)),
        before=lambda col0: pl.BlockSpec((8, HALF), lambda ch, i: (jnp.maximum(order(i, n) * (tm // 8) - 1, 0), col0 + ch)),
        proj_tile=lambda col0: pl.BlockSpec((tm, HALF), lambda ch, i: (order(i, n), col0 + ch)),
        keep=pl.BlockSpec((tm, 1), lambda ch, i: (order(i, n), 0)),
        gates=lambda which: pl.BlockSpec((N_DEV, GATE_SHARD, HALF), lambda ch, i: (0, PACK_GATE // GATE_SHARD + which, ch)),
        gate_grads=pl.BlockSpec((2, 2, 256, 256), lambda ch, i: (0, ch, 0, 0)),
        vec=pl.BlockSpec((1, HALF), lambda ch, i: (0, ch)),
        conv=pl.BlockSpec((CONV_WIDTH, HALF), lambda ch, i: (0, ch)),
    )


def _rnn_fwd(proj, keep, w_all, rg_ba, rg_bx, rg_lambda, conv_w, conv_b):
    t = proj.shape[0]
    tm = min(512, t)
    sp = _rnn_specs(t, tm, lambda i, n: i)

    def body(xr_ref, xb_ref, keep_ref, wa_ref, wx_ref, ba_ref, bx_ref, lam_ref, cw_ref, cb_ref, hh_ref, xe, a_buf, b_buf, carry):
        i = pl.program_id(1)

        @pl.when(i == 0)
        def _():
            carry[...] = jnp.zeros_like(carry)

        xe[0:8, :] = jnp.where(i > 0, xb_ref[...], 0.0)
        xe[8:, :] = xr_ref[...]
        keep_rows = keep_ref[...] != 0.0
        xc, _, _, ig, _, a, mult = _conv_and_gates(xe, tm, cw_ref, cb_ref, wa_ref, wx_ref, ba_ref, bx_ref, lam_ref, keep_rows)
        a_buf[...] = a
        b_buf[...] = mult * (ig * xc)
        _scan_rows(a_buf, b_buf, hh_ref, carry, tm, reverse=False)

    return pl.pallas_call(
        body,
        name="rnn_fwd",
        grid=(2, t // tm),
        in_specs=[sp["proj_tile"](SEG_XR // HALF), sp["before"](SEG_XR // HALF), sp["keep"], sp["gates"](0), sp["gates"](1),
                  sp["vec"], sp["vec"], sp["vec"], sp["conv"], sp["vec"]],
        out_specs=sp["tile"],
        out_shape=jax.ShapeDtypeStruct((t, D), F32),
        scratch_shapes=[pltpu.VMEM((tm + 8, HALF), F32), pltpu.VMEM((tm, HALF), F32), pltpu.VMEM((tm, HALF), F32), pltpu.VMEM((8, HALF), F32)],
        compiler_params=_params(2),
    )(proj, proj, keep, w_all, w_all, rg_ba, rg_bx, rg_lambda, conv_w, conv_b)


def _rnn_bwd(proj, hh, d_hh, keep, w_all, rg_ba, rg_bx, rg_lambda, conv_w, conv_b, sum_3_bf):
    t = proj.shape[0]
    tm = min(512, t)
    n_tiles = t // tm
    sp = _rnn_specs(t, tm, lambda i, n: n - 1 - i)

    def body(xr_ref, xb_ref, hh_ref, hb_ref, dhh_ref, keep_ref, wa_ref, wx_ref, ba_ref, bx_ref, lam_ref, cw_ref, cb_ref,
             s3_hbm, dxr_ref, gw_ref, gvec_ref, l3_hbm, xe, he, ae, c_buf, g_buf, dxe, carry, a_next, send_sems, recv_sems):
        ch = pl.program_id(0)
        i = pl.program_id(1)
        first_tile = i == n_tiles - 1
        views = [lambda ref, peer: ref.at[:, _chip_of(peer)]]

        @pl.when((ch == 0) & (i == 0))
        def _():
            for cp in _chip_copies([s3_hbm], [l3_hbm], views, send_sems, recv_sems):
                cp.start()

        @pl.when((ch == 1) & (i == n_tiles - 1))
        def _():
            for cp in _chip_copies([s3_hbm], [l3_hbm], views, send_sems, recv_sems):
                cp.wait_recv()
                cp.wait_send()

        @pl.when(i == 0)
        def _():
            carry[...] = jnp.zeros_like(carry)
            a_next[...] = jnp.zeros_like(a_next)
            dxe[tm:, :] = jnp.zeros((8, HALF), F32)
            gw_ref[...] = jnp.zeros_like(gw_ref)
            gvec_ref[...] = jnp.zeros_like(gvec_ref)

        xe[0:8, :] = jnp.where(first_tile, 0.0, xb_ref[...])
        xe[8:, :] = xr_ref[...]
        he[0:8, :] = jnp.where(first_tile, 0.0, hb_ref[...])
        he[8:, :] = hh_ref[...]
        keep_rows = keep_ref[...] != 0.0
        xc, xc_bf, r, ig, sp_, a, mult = _conv_and_gates(xe, tm, cw_ref, cb_ref, wa_ref, wx_ref, ba_ref, bx_ref, lam_ref, keep_rows)

        ae[0:tm, :] = a
        ae[tm:, :] = a_next[...]
        a_next[...] = jnp.broadcast_to(a[0:1, :], (8, HALF))
        c_buf[...] = ae[pl.ds(1, tm), :]
        g_buf[...] = dhh_ref[...]
        _scan_rows(c_buf, g_buf, g_buf, carry, tm, reverse=True)
        g = g_buf[...]

        h_prev = he[pl.ds(7, tm), :]
        bb = ig * xc
        d_ig = g * mult * xc
        d_xc = g * mult * ig
        d_log_a = jnp.where(keep_rows, g * h_prev * a - (g * bb) * (a * a) / mult, 0.0)
        d_r = d_log_a * (-LRU_C * sp_)
        d_sp = jnp.sum(d_log_a * (-LRU_C * r), axis=0, keepdims=True)
        d_zr = d_r * r * (1.0 - r)
        d_zi = d_ig * ig * (1.0 - ig)
        d_zr_bf = d_zr.astype(BF16)
        d_zi_bf = d_zi.astype(BF16)
        d_xc = d_xc + _gate_matmul_t(d_zr_bf, wa_ref) + _gate_matmul_t(d_zi_bf, wx_ref)
        for blk in range(2):
            cols = slice(256 * blk, 256 * (blk + 1))
            gw_ref[0, blk] = gw_ref[0, blk] + _dot_tn(xc_bf[:, cols], d_zr_bf[:, cols])
            gw_ref[1, blk] = gw_ref[1, blk] + _dot_tn(xc_bf[:, cols], d_zi_bf[:, cols])

        dxe[0:tm, :] = d_xc
        shifted = [dxe[pl.ds(3 - k, tm), :] for k in range(CONV_WIDTH)]
        dxr = sum(cw_ref[k : k + 1, :] * shifted[k] for k in range(CONV_WIDTH))
        dxr_ref[...] = dxr.astype(BF16)
        dxe[tm:, :] = d_xc[0:8, :]

        lam = lam_ref[...]
        xr_tile = xr_ref[...]
        sums = [
            jnp.sum(d_zr, axis=0, keepdims=True),
            jnp.sum(d_zi, axis=0, keepdims=True),
            d_sp * (-_sigmoid(-lam)),
            jnp.sum(d_xc, axis=0, keepdims=True),
        ] + [jnp.sum(xr_tile * shifted[k], axis=0, keepdims=True) for k in range(CONV_WIDTH)]
        gvec_ref[...] = gvec_ref[...] + jnp.concatenate(sums, axis=0)

    return pl.pallas_call(
        body,
        name="rnn_bwd",
        grid=(2, n_tiles),
        in_specs=[
            sp["proj_tile"](SEG_XR // HALF), sp["before"](SEG_XR // HALF), sp["tile"], sp["before"](0), sp["tile"], sp["keep"],
            sp["gates"](0), sp["gates"](1), sp["vec"], sp["vec"], sp["vec"], sp["conv"], sp["vec"], pl.BlockSpec(memory_space=pl.ANY),
        ],
        out_specs=[sp["tile"], sp["gate_grads"], pl.BlockSpec((8, HALF), lambda ch, i: (0, ch)), pl.BlockSpec(memory_space=pl.ANY)],
        out_shape=[
            jax.ShapeDtypeStruct((t, D), BF16),
            jax.ShapeDtypeStruct((2, 4, 256, 256), F32),
            jax.ShapeDtypeStruct((8, D), F32),
            jax.ShapeDtypeStruct((len(CHIP_FLIPS), 3, ROW_SHARD, D), BF16),
        ],
        scratch_shapes=[
            pltpu.VMEM((tm + 8, HALF), F32),
            pltpu.VMEM((tm + 8, HALF), F32),
            pltpu.VMEM((tm + 8, HALF), F32),
            pltpu.VMEM((tm, HALF), F32),
            pltpu.VMEM((tm, HALF), F32),
            pltpu.VMEM((tm + 8, HALF), F32),
            pltpu.VMEM((8, HALF), F32),
            pltpu.VMEM((8, HALF), F32),
            pltpu.SemaphoreType.DMA((len(CHIP_FLIPS),)),
            pltpu.SemaphoreType.DMA((len(CHIP_FLIPS),)),
        ],
        compiler_params=_params(2),
    )(proj, proj, hh, hh, d_hh, keep, w_all, w_all, rg_ba, rg_bx, rg_lambda, conv_w, conv_b, sum_3_bf)


def _mid(x, target, o, hh, proj, w_all, gate, final_g):
    t = x.shape[0]
    tm = min(256, t)
    n_tiles = t // tm
    full = pl.BlockSpec((tm, D), lambda i: (i, 0))
    half = lambda col: pl.BlockSpec((tm, 512), lambda i: (i, col))
    vec = pl.BlockSpec((1, D), lambda i: (0, 0))
    any_spec = pl.BlockSpec(memory_space=pl.ANY)

    def body(x_ref, tg_ref, o_ref, hh_ref, ga0, ga1, gr0, gr1, ma0, ma1, mr0, mr1, w_hbm, gate_ref, fg_ref,
             dres_ref, do_ref, dhh_ref, dga_ref, dgr_ref, dma_ref, dmr_ref, gw_hbm, vec_ref, w, gw, sem, w_sems):
        i = pl.program_id(0)

        @pl.when(i == 0)
        def _():
            loads = [
                pltpu.make_async_copy(
                    w_hbm.at[s, pl.ds(ROW_SHARD * k, ROW_SHARD), :], w.at[k, pl.ds(ROW_SHARD * s, ROW_SHARD), :], w_sems.at[k * N_DEV + s]
                )
                for k in range(3)
                for s in range(N_DEV)
            ]
            for cp in loads:
                cp.start()
            for cp in loads:
                cp.wait()
            gw[...] = jnp.zeros_like(gw)
            vec_ref[...] = jnp.zeros_like(vec_ref)

        cat = lambda a, b: jnp.concatenate([a[...], b[...]], axis=1)
        ga, gr, ma, mr = cat(ga0, ga1), cat(gr0, gr1), cat(ma0, ma1), cat(mr0, mr1)
        ov, hv, gate_v, fg = o_ref[...].astype(F32), hh_ref[...], gate_ref[...], fg_ref[...]
        sg_a = _sigmoid(ga)
        sg_r = _sigmoid(gr)
        silu_a = ga * sg_a
        silu_r = gr * sg_r
        ya = (ov * silu_a).astype(BF16)
        yr = (hv * silu_r).astype(BF16)
        pa = _dot(ya, w[0])
        pr = _dot(yr, w[1])
        sa = _sigmoid(ma)
        sr = _sigmoid(mr)
        merged = (sa * pa + sr * pr).astype(BF16)
        mo = _dot(merged, w[2])
        out = x_ref[...] + gate_v * mo
        r2 = lax.rsqrt(jnp.mean(out * out, axis=-1, keepdims=True) + NORM_EPS)
        outn = out * r2
        err = outn * fg - tg_ref[...]
        loss = 0.5 * jnp.sum(jnp.mean(err * err, axis=-1, keepdims=True), axis=0, keepdims=True)
        dy = err * (1.0 / D)
        g_final = jnp.sum(dy * outn, axis=0, keepdims=True)
        doutn = dy * fg
        dout = r2 * (doutn - outn * jnp.mean(doutn * outn, axis=-1, keepdims=True))
        dres_ref[...] = dout
        d_gate = jnp.sum(dout * mo, axis=0, keepdims=True)
        dmo = (dout * gate_v).astype(BF16)
        dmerged = _dot_nt(dmo, w[2])
        gw[2] = gw[2] + _dot_tn(merged, dmo)
        dpa = (dmerged * sa).astype(BF16)
        dpr = (dmerged * sr).astype(BF16)
        dma_ref[...] = (dmerged * pa * (sa * (1.0 - sa))).astype(BF16)
        dmr_ref[...] = (dmerged * pr * (sr * (1.0 - sr))).astype(BF16)
        dya = _dot_nt(dpa, w[0])
        dyr = _dot_nt(dpr, w[1])
        gw[0] = gw[0] + _dot_tn(ya, dpa)
        gw[1] = gw[1] + _dot_tn(yr, dpr)
        do_ref[...] = (dya * silu_a).astype(BF16)
        dhh_ref[...] = dyr * silu_r
        dga_ref[...] = (dya * ov * (sg_a * (1.0 + ga * (1.0 - sg_a)))).astype(BF16)
        dgr_ref[...] = (dyr * hv * (sg_r * (1.0 + gr * (1.0 - sg_r)))).astype(BF16)
        zeros = jnp.zeros((5, D), F32)
        vec_ref[...] = vec_ref[...] + jnp.concatenate([g_final, d_gate, jnp.broadcast_to(loss, (1, D)), zeros], axis=0)

        @pl.when(i == n_tiles - 1)
        def _():
            cp = pltpu.make_async_copy(gw, gw_hbm, sem)
            cp.start()
            cp.wait()

    bf_tile = jax.ShapeDtypeStruct((t, D), BF16)
    f_tile = jax.ShapeDtypeStruct((t, D), F32)
    return pl.pallas_call(
        body,
        name="mid",
        grid=(n_tiles,),
        in_specs=[full, full, full, full,
                  half(SEG_GA // 512), half(SEG_GA // 512 + 1), half(SEG_GR // 512), half(SEG_GR // 512 + 1),
                  half(SEG_MA // 512), half(SEG_MA // 512 + 1), half(SEG_MR // 512), half(SEG_MR // 512 + 1),
                  any_spec, vec, vec],
        out_specs=[full, full, full, full, full, full, full, any_spec, pl.BlockSpec((8, D), lambda i: (0, 0))],
        out_shape=[f_tile, bf_tile, f_tile, bf_tile, bf_tile, bf_tile, bf_tile,
                   jax.ShapeDtypeStruct((3, D, D), F32), jax.ShapeDtypeStruct((8, D), F32)],
        scratch_shapes=[pltpu.VMEM((3, D, D), BF16), pltpu.VMEM((3, D, D), F32), pltpu.SemaphoreType.DMA, pltpu.SemaphoreType.DMA((3 * N_DEV,))],
        compiler_params=_params(1),
    )(x, target, o, hh, proj, proj, proj, proj, proj, proj, proj, proj, w_all, gate, final_g)


_SEGMENTS = ((SEG_Q, 1024), (SEG_K, 256), (SEG_V, 256), (SEG_GA, 1024), (SEG_XR, 1024), (SEG_GR, 1024), (SEG_MA, 1024), (SEG_MR, 1024))


def _seg_specs(tm):
    return [pl.BlockSpec((tm, width), lambda i: (i, 0)) for _, width in _SEGMENTS]


def _bwd_dh(dsegs, w_all, x, dres, mod, norm_g):
    t = x.shape[0]
    tm = min(512, t)
    n_tiles = t // tm
    n_seg = len(_SEGMENTS)
    full = pl.BlockSpec((tm, D), lambda i: (i, 0))

    def body(*refs):
        segs = refs[:n_seg]
        w_hbm, x_ref, dres_ref, mod_ref, ng_ref, gx_ref, vec_ref, w, sems = refs[n_seg:]
        i = pl.program_id(0)

        @pl.when(i == 0)
        def _():
            loads = [
                pltpu.make_async_copy(w_hbm.at[s], w.at[pl.ds(W_IN_SHARD * s, W_IN_SHARD), :], sems.at[s])
                for s in range(N_DEV)
            ]
            for cp in loads:
                cp.start()
            for cp in loads:
                cp.wait()
            vec_ref[...] = jnp.zeros_like(vec_ref)

        dh = sum(_dot(seg[...], w[start : start + width, :]) for seg, (start, width) in zip(segs, _SEGMENTS))
        xv = x_ref[...]
        r1 = lax.rsqrt(jnp.mean(xv * xv, axis=-1, keepdims=True) + NORM_EPS)
        xn = xv * r1
        ng = ng_ref[...]
        scale1 = 1.0 + mod_ref[:, D : 2 * D]
        d_shift = jnp.sum(dh, axis=0, keepdims=True)
        d_scale = jnp.sum(dh * (xn * ng), axis=0, keepdims=True)
        g_norm = jnp.sum(dh * xn * scale1, axis=0, keepdims=True)
        dxn = dh * (ng * scale1)
        dx = r1 * (dxn - xn * jnp.mean(dxn * xn, axis=-1, keepdims=True))
        gx_ref[...] = dres_ref[...] + dx
        vec_ref[...] = vec_ref[...] + jnp.concatenate([d_shift, d_scale, g_norm, jnp.zeros((5, D), F32)], axis=0)

    any_spec = pl.BlockSpec(memory_space=pl.ANY)
    return pl.pallas_call(
        body,
        name="bwd_dh",
        grid=(n_tiles,),
        in_specs=_seg_specs(tm) + [any_spec, full, full, pl.BlockSpec((1, 3 * D), lambda i: (0, 0)), pl.BlockSpec((1, D), lambda i: (0, 0))],
        out_specs=[full, pl.BlockSpec((8, D), lambda i: (0, 0))],
        out_shape=[jax.ShapeDtypeStruct((t, D), F32), jax.ShapeDtypeStruct((8, D), F32)],
        scratch_shapes=[pltpu.VMEM((IN_W, D), BF16), pltpu.SemaphoreType.DMA((N_DEV,))],
        compiler_params=_params(1),
    )(*dsegs, w_all, x, dres, mod, norm_g)


_HBM = pl.BlockSpec(memory_space=pltpu.HBM)
_SEM = pl.BlockSpec(memory_space=pltpu.SEMAPHORE)
_EFFECT = pltpu.SideEffectType.DATAFLOW_SIDE_EFFECTING


def _w_exchange_copies(sum_ref, land_ref, sems):
    me = _position()
    n = len(CHIP_FLIPS)
    return [
        pltpu.make_async_remote_copy(
            src_ref=sum_ref.at[_chip_of(_flip(me, k))], dst_ref=land_ref.at[j], send_sem=sems[j], recv_sem=sems[n + j],
            device_id=_flip(me, k), device_id_type=MESH,
        )
        for j, k in enumerate(CHIP_FLIPS)
    ]


def _w_exchange_start(sum_w_bf):
    n_sems = 2 * len(CHIP_FLIPS)
    land = lax.empty((len(CHIP_FLIPS), W_IN_SHARD, D), BF16)

    def body(sum_ref, land_ref, *outs):
        for cp in _w_exchange_copies(sum_ref, land_ref, outs[:n_sems]):
            cp.start()
        token = outs[n_sems + 2]
        token[...] = jnp.zeros_like(token)

    res = pl.pallas_call(
        body,
        name="w_exchange_start",
        out_shape=(*[pltpu.SemaphoreType.DMA(())] * n_sems, pltpu.HBM(sum_w_bf.shape, BF16), pltpu.HBM(land.shape, BF16),
                   jax.ShapeDtypeStruct((8, 128), F32)),
        in_specs=(_HBM, _HBM),
        out_specs=(*[_SEM] * n_sems, _HBM, _HBM, pl.BlockSpec(memory_space=pltpu.VMEM)),
        input_output_aliases={0: n_sems, 1: n_sems + 1},
        compiler_params=pltpu.CompilerParams(has_side_effects=_EFFECT),
    )(pltpu.with_memory_space_constraint(sum_w_bf, pltpu.HBM), pltpu.with_memory_space_constraint(land, pltpu.HBM))
    return res[:n_sems], res[n_sems], res[n_sems + 1], res[n_sems + 2]


def _w_exchange_wait(sems, sum_thru, land_thru, after):
    n_sems = len(sems)

    def body(sum_ref, land_ref, *rest):
        for cp in _w_exchange_copies(sum_ref, land_ref, rest[:n_sems]):
            cp.wait_send()
            cp.wait_recv()

    return pl.pallas_call(
        body,
        name="w_exchange_wait",
        out_shape=(pltpu.HBM(sum_thru.shape, BF16), pltpu.HBM(land_thru.shape, BF16)),
        in_specs=(_HBM, _HBM, *[_SEM] * n_sems, pl.BlockSpec(memory_space=pl.ANY)),
        out_specs=(_HBM, _HBM),
        input_output_aliases={0: 0, 1: 1},
        compiler_params=pltpu.CompilerParams(has_side_effects=_EFFECT),
    )(sum_thru, land_thru, *sems, after)[1]


def _bwd_gw(dsegs, h, g_gates):
    t = h.shape[0]
    tm = min(512, t)
    n_tiles = t // tm
    n_seg = len(_SEGMENTS)
    n_chips = N_DEV // 2

    def gate_copies(g_hbm, lg_hbm, send_sems, recv_sems, arriving):
        me = _position()
        return [
            pltpu.make_async_remote_copy(
                src_ref=g_hbm.at[:, :, _slot(_flip(me, k))],
                dst_ref=lg_hbm.at[_slot(_flip(me, k) if arriving else me)],
                send_sem=send_sems.at[k - 1],
                recv_sem=recv_sems.at[k - 1],
                device_id=_flip(me, k),
                device_id_type=MESH,
            )
            for k in range(1, N_DEV)
        ]

    def body(*refs):
        segs = refs[:n_seg]
        (h_ref, g_hbm, own_hbm, sum_hbm, lg_hbm, acc, stage, land, own_sems, pair_send, pair_recv, send_sems, recv_sems, g_local) = refs[n_seg:]
        i = pl.program_id(0)
        slot = _slot(_position())
        own_gates = pltpu.make_async_copy(g_hbm.at[:, :, slot], lg_hbm.at[slot], g_local)

        @pl.when(i == 0)
        def _():
            own_gates.start()
            for cp in gate_copies(g_hbm, lg_hbm, send_sems, recv_sems, arriving=False):
                cp.start()
            acc[...] = jnp.zeros_like(acc)

        hv = h_ref[...]
        for seg, (start, width) in zip(segs, _SEGMENTS):
            acc[start : start + width, :] = acc[start : start + width, :] + _dot_tn(seg[...], hv)

        @pl.when(i == n_tiles - 1)
        def _():
            me = _position()
            sibling = _flip(me, 1)

            def rows(q, c):
                return pl.ds(pl.multiple_of(W_IN_SHARD * (2 * q + c), 8), W_IN_SHARD)

            def to_sibling(q):
                return pltpu.make_async_remote_copy(
                    src_ref=stage.at[q % 2], dst_ref=land.at[q], send_sem=pair_send.at[q], recv_sem=pair_recv.at[q],
                    device_id=sibling, device_id_type=MESH,
                )

            for q in range(n_chips):
                if q >= 2:
                    to_sibling(q - 2).wait_send()
                stage[q % 2] = acc[rows(q, sibling[2]), :].astype(BF16)
                to_sibling(q).start()
            for q in range(n_chips - 2, n_chips):
                to_sibling(q).wait_send()
            sums = [pltpu.make_async_copy(stage.at[q % 2], sum_hbm.at[q], own_sems.at[q]) for q in range(n_chips)]
            for q in range(n_chips):
                to_sibling(q).wait_recv()
                if q >= 2:
                    sums[q - 2].wait()
                total = acc[rows(q, me[2]), :] + land[q].astype(F32)
                acc[rows(q, me[2]), :] = total
                stage[q % 2] = total.astype(BF16)
                sums[q].start()
            mine = pltpu.make_async_copy(acc.at[rows(_chip_of(me), me[2]), :], own_hbm, own_sems.at[n_chips])
            mine.start()
            for q in range(n_chips - 2, n_chips):
                sums[q].wait()
            mine.wait()
            for cp in gate_copies(g_hbm, lg_hbm, send_sems, recv_sems, arriving=True):
                cp.wait_recv()
            for cp in gate_copies(g_hbm, lg_hbm, send_sems, recv_sems, arriving=False):
                cp.wait_send()
            own_gates.wait()

    any_spec = pl.BlockSpec(memory_space=pl.ANY)
    n_copies = N_DEV - 1
    return pl.pallas_call(
        body,
        name="bwd_gw",
        grid=(n_tiles,),
        in_specs=_seg_specs(tm) + [pl.BlockSpec((tm, D), lambda i: (i, 0)), any_spec],
        out_specs=[any_spec, any_spec, any_spec],
        out_shape=[jax.ShapeDtypeStruct((W_IN_SHARD, D), F32),
                   jax.ShapeDtypeStruct((n_chips, W_IN_SHARD, D), BF16),
                   jax.ShapeDtypeStruct((N_DEV, 2, 4, GATE_SHARD, 256), F32)],
        scratch_shapes=[pltpu.VMEM((IN_W, D), F32), pltpu.VMEM((2, W_IN_SHARD, D), BF16), pltpu.VMEM((n_chips, W_IN_SHARD, D), BF16),
                        pltpu.SemaphoreType.DMA((n_chips + 1,)), pltpu.SemaphoreType.DMA((n_chips,)), pltpu.SemaphoreType.DMA((n_chips,)),
                        pltpu.SemaphoreType.DMA((n_copies,)), pltpu.SemaphoreType.DMA((n_copies,)), pltpu.SemaphoreType.DMA],
        compiler_params=_params(1),
    )(*dsegs, h, g_gates)


def _adamw(w, g, m, v):
    m = ADAM_B1 * m + (1.0 - ADAM_B1) * g
    v = ADAM_B2 * v + (1.0 - ADAM_B2) * (g * g)
    m_hat = m / (1.0 - ADAM_B1**ADAM_STEP)
    v_hat = v / (1.0 - ADAM_B2**ADAM_STEP)
    delta = -ADAM_LR * (m_hat / (jnp.sqrt(v_hat) + ADAM_EPS) + ADAM_WD * w)
    return delta, m, v


def _w_in_update(own, land, w_t, m_t, v_t):
    n_slots, rows, cols = land.shape
    rb = 104
    spec = pl.BlockSpec((rb, cols), lambda i: (i, 0))

    def body(o_ref, l_ref, w_ref, m_ref, v_ref, g_ref, d_ref, m2_ref, v2_ref):
        total = o_ref[...]
        for s in range(n_slots):
            total = total + l_ref[s].astype(F32)
        g_ref[...] = total
        d_ref[...], m2_ref[...], v2_ref[...] = _adamw(w_ref[...], total, m_ref[...], v_ref[...])

    shape = jax.ShapeDtypeStruct((rows, cols), F32)
    return pl.pallas_call(
        body,
        name="w_in_update",
        grid=(rows // rb,),
        in_specs=[spec, pl.BlockSpec((n_slots, rb, cols), lambda i: (0, i, 0)), spec, spec, spec],
        out_specs=[spec] * 4,
        out_shape=[shape] * 4,
        compiler_params=_params(1),
    )(own, land, w_t, m_t, v_t)


def _shard_update(own_3, land_3, land_g, weights, m_in, v_in):
    n = len(weights)

    def body(*refs):
        o3_ref, l3_ref, lg_ref = refs[:3]
        w_refs, m_refs, v_refs = (refs[3 + n * k : 3 + n * (k + 1)] for k in range(3))
        outs = refs[3 + 3 * n :]
        for k in range(n):
            if k < 3:
                g = o3_ref[k]
                for slot in range(3):
                    g = g + l3_ref[slot, k].astype(F32)
            else:
                g = lg_ref[0, k - 3]
                for slot in range(1, N_DEV):
                    g = g + lg_ref[slot, k - 3]
            d, m2, v2 = _adamw(w_refs[k][0], g, m_refs[k][0], v_refs[k][0])
            for which, val in enumerate((g, d, m2, v2)):
                outs[which * n + k][0] = val

    shapes = [jax.ShapeDtypeStruct(w.shape, F32) for w in weights]
    res = pl.pallas_call(
        body, name="shard_update", out_shape=shapes * 4, compiler_params=pltpu.CompilerParams(vmem_limit_bytes=VMEM_LIMIT)
    )(own_3, land_3, land_g, *weights, *m_in, *v_in)
    return [res[n * k : n * (k + 1)] for k in range(4)]


VEC_LAYOUT = (("b_ada", 3 * D), ("norm_g", D), ("attn_sinks", 128), ("conv_b", D), ("rg_ba", D), ("rg_bx", D), ("rg_lambda", D),
              ("final_g", D), ("conv_w", CONV_WIDTH * D), ("loss", 128))
VEC_OFFSET = {name: sum(w for _, w in VEC_LAYOUT[:k]) for k, (name, _) in enumerate(VEC_LAYOUT)}
VEC_ROWS = 112
VEC_WIDTH = VEC_ROWS * 128
assert sum(w for _, w in VEC_LAYOUT) <= VEC_WIDTH
N_REPLICATED = 8


def _small_update(vec_all, c_all_t, me, rep_w, rep_m, rep_v, ada, conv):
    n_ada = ada[0].shape[1]
    n_conv = conv[0].shape[1]

    def body(*refs):
        me_ref, vec_ref, ct_ref = refs[:3]
        w_refs, m_refs, v_refs = (refs[3 + N_REPLICATED * k : 3 + N_REPLICATED * (k + 1)] for k in range(3))
        pos = 3 + 3 * N_REPLICATED
        ada_refs, conv_refs = refs[pos : pos + 3], refs[pos + 3 : pos + 6]
        outs = refs[pos + 6 :]
        loss_ref, rep_outs = outs[0], outs[1 : 1 + 4 * N_REPLICATED]
        ada_outs, conv_outs = outs[1 + 4 * N_REPLICATED : 5 + 4 * N_REPLICATED], outs[5 + 4 * N_REPLICATED : 9 + 4 * N_REPLICATED]
        conv_buf = outs[9 + 4 * N_REPLICATED]

        total = vec_ref[0:1, :]
        for slot in range(1, N_DEV):
            total = total + vec_ref[slot : slot + 1, :]
        loss_ref[...] = total[:, VEC_OFFSET["loss"] : VEC_OFFSET["loss"] + 1]
        for k, (name, _) in enumerate(VEC_LAYOUT[:N_REPLICATED]):
            width = w_refs[k].shape[1]
            g = total[:, VEC_OFFSET[name] : VEC_OFFSET[name] + width]
            d, m2, v2 = _adamw(w_refs[k][...], g, m_refs[k][...], v_refs[k][...])
            for which, val in enumerate((g, d, m2, v2)):
                rep_outs[which * N_REPLICATED + k][...] = val

        device = me_ref[0]
        for k in range(CONV_WIDTH):
            conv_buf[k : k + 1, :] = total[:, VEC_OFFSET["conv_w"] + D * k : VEC_OFFSET["conv_w"] + D * (k + 1)]
        g_conv = conv_buf[:, pl.ds(pl.multiple_of(device * n_conv, 128), n_conv)]
        for ref, val in zip(conv_outs, (g_conv,) + _adamw(conv_refs[0][...], g_conv, conv_refs[1][...], conv_refs[2][...])):
            ref[...] = val

        dmod = vec_ref[:, pl.ds(pl.multiple_of(device * n_ada, 128), n_ada)]
        g_ada = _dot(ct_ref[...], dmod)
        for ref, val in zip(ada_outs, (g_ada,) + _adamw(ada_refs[0][...], g_ada, ada_refs[1][...], ada_refs[2][...])):
            ref[...] = val

    vmem = pl.BlockSpec(memory_space=pltpu.VMEM)
    rep_shapes = [jax.ShapeDtypeStruct(w.shape, F32) for w in rep_w]
    out_shape = ([jax.ShapeDtypeStruct((1, 1), F32)] + rep_shapes * 4 + [jax.ShapeDtypeStruct(ada[0].shape, F32)] * 4
                 + [jax.ShapeDtypeStruct(conv[0].shape, F32)] * 4)
    n_in = 3 + 3 * N_REPLICATED + 6
    res = pl.pallas_call(
        body,
        name="small_update",
        in_specs=[pl.BlockSpec(memory_space=pltpu.SMEM)] + [vmem] * (n_in - 1),
        out_specs=[vmem] * len(out_shape),
        out_shape=out_shape,
        scratch_shapes=[pltpu.VMEM((CONV_WIDTH, D), F32)],
        compiler_params=pltpu.CompilerParams(vmem_limit_bytes=VMEM_LIMIT),
    )(me, vec_all, c_all_t, *rep_w, *rep_m, *rep_v, *ada, *conv)
    loss = res[0]
    reps = [res[1 + N_REPLICATED * k : 1 + N_REPLICATED * (k + 1)] for k in range(4)]
    return loss, reps, res[1 + 4 * N_REPLICATED : 5 + 4 * N_REPLICATED], res[5 + 4 * N_REPLICATED : 9 + 4 * N_REPLICATED]


def kernel(x, c, positions, w_ada, b_ada, norm_g, w_in, attn_sinks, conv_w, conv_b, rg_wa, rg_ba, rg_wx, rg_bx, rg_lambda, w_attn_proj, w_rnn_proj, w_out, final_g, loss_target, m_w_ada, m_b_ada, m_norm_g, m_w_in, m_attn_sinks, m_conv_w, m_conv_b, m_rg_wa, m_rg_ba, m_rg_wx, m_rg_bx, m_rg_lambda, m_w_attn_proj, m_w_rnn_proj, m_w_out, m_final_g, v_w_ada, v_b_ada, v_norm_g, v_w_in, v_attn_sinks, v_conv_w, v_conv_b, v_rg_wa, v_rg_ba, v_rg_wx, v_rg_bx, v_rg_lambda, v_w_attn_proj, v_w_rnn_proj, v_w_out, v_final_g):
    my_x, my_y, my_c = _position()
    me = _slot((my_x, my_y, my_c))
    x2, target = x[0], loss_target[0]
    pos = positions[0]

    gate_rows = lambda w: w[0].transpose(1, 0, 2).reshape(GATE_SHARD, D)
    w_pack = jnp.concatenate([w_attn_proj[0], w_rnn_proj[0], w_out[0], gate_rows(rg_wa), gate_rows(rg_wx)], axis=0).astype(BF16)
    conv_rows = jnp.pad(conv_w[0], ((0, 8 - CONV_WIDTH), (0, 0)))
    w_in_t = w_in[0].T
    n_ada = w_ada.shape[2]
    b_cols = lax.dynamic_slice(b_ada, (0, me * n_ada), (1, n_ada))

    tables = _rope_tables(pos)
    where = jnp.stack([_chip_of((my_x, my_y)), me]).astype(jnp.int32)
    proj, h, mod, w_t_all, w_all, c_rows, conv_all = _fwd_in(
        x2, jnp.broadcast_to(c, (8, D)), conv_rows, w_ada[0], b_cols, norm_g, w_in_t.astype(BF16), tables[0], w_pack, where
    )
    c_all = c_rows[:, 0, :]
    conv_full = conv_all[:, :CONV_WIDTH, :].transpose(1, 0, 2).reshape(CONV_WIDTH, D)
    gate = mod[:, 2 * D :]
    o = _attn_fwd(proj, attn_sinks)
    keep = (pos != 0).astype(F32)[:, None]
    hh = _rnn_fwd(proj, keep, w_all, rg_ba, rg_bx, rg_lambda, conv_full, conv_b)

    dres, d_o, d_hh, d_ga, d_gr, d_ma, d_mr, gw3, mid_vec = _mid(x2, target, o, hh, proj, w_all, gate, final_g[None, :])
    part_3 = gw3.reshape(3, 4, 2, ROW_SHARD, D)
    d_q, d_k, d_v, d_sink, sum_3_bf, own_3 = _attn_bwd(proj, o, d_o, tables, attn_sinks, part_3)
    d_xr, g_gates, rnn_vec, land_3 = _rnn_bwd(proj, hh, d_hh, keep, w_all, rg_ba, rg_bx, rg_lambda, conv_full, conv_b, sum_3_bf)
    dsegs = (d_q, d_k, d_v, d_ga, d_xr, d_gr, d_ma, d_mr)
    own_w, sum_w_bf, land_g = _bwd_gw(dsegs, h, g_gates.reshape(2, 4, N_DEV, GATE_SHARD, 256))
    w_sems, sum_w_thru, land_w_thru, token = _w_exchange_start(sum_w_bf)
    grad_x, dh_vec = _bwd_dh(dsegs, w_t_all, x2, dres, mod + token[0:1, 0:1], norm_g)
    land_w = _w_exchange_wait(w_sems, sum_w_thru, land_w_thru, dh_vec)

    pieces = dict(
        b_ada=jnp.concatenate([dh_vec[0:1], dh_vec[1:2], mid_vec[1:2]], axis=1),
        norm_g=dh_vec[2:3],
        attn_sinks=jnp.pad(d_sink[:, :, 0].reshape(1, 16), ((0, 0), (0, 112))),
        conv_b=rnn_vec[3:4],
        rg_ba=rnn_vec[0:1],
        rg_bx=rnn_vec[1:2],
        rg_lambda=rnn_vec[2:3],
        final_g=mid_vec[0:1],
        conv_w=rnn_vec[4:8].reshape(1, CONV_WIDTH * D),
        loss=mid_vec[2:3, 0:128],
    )
    vec = jnp.concatenate([pieces[name] for name, _ in VEC_LAYOUT], axis=1)
    vec = jnp.pad(vec, ((0, 0), (0, VEC_WIDTH - vec.shape[1]))).reshape(VEC_ROWS, 128)
    (vec_all,) = _all_gather_direct("ag_vec", [vec])
    rep_w = [b_ada, norm_g, attn_sinks, conv_b, rg_ba, rg_bx, rg_lambda, final_g[None, :]]
    rep_m = [m_b_ada, m_norm_g, m_attn_sinks, m_conv_b, m_rg_ba, m_rg_bx, m_rg_lambda, m_final_g[None, :]]
    rep_v = [v_b_ada, v_norm_g, v_attn_sinks, v_conv_b, v_rg_ba, v_rg_bx, v_rg_lambda, v_final_g[None, :]]
    loss, reps, ada_res, conv_res = _small_update(
        vec_all.reshape(N_DEV, VEC_WIDTH), c_all.T, me.astype(jnp.int32).reshape(1), rep_w, rep_m, rep_v,
        (w_ada[0], m_w_ada[0], v_w_ada[0]), (conv_w[0], m_conv_w[0], v_conv_w[0]),
    )

    w_in_res = _w_in_update(own_w, land_w, w_in_t, m_w_in[0].T, v_w_in[0].T)
    shard_res = _shard_update(
        own_3, land_3, land_g,
        (w_attn_proj, w_rnn_proj, w_out, rg_wa, rg_wx),
        (m_w_attn_proj, m_w_rnn_proj, m_w_out, m_rg_wa, m_rg_wx),
        (v_w_attn_proj, v_w_rnn_proj, v_w_out, v_rg_wa, v_rg_wx),
    )

    def by_weight(k):
        b, n, s, cb, ba, bx, lam, fg = reps[k]
        ap, rp, out, wa, wx = shard_res[k]
        return [ada_res[k][None], b, n, w_in_res[k].T[None], s, conv_res[k][None], cb, wa, ba, wx, bx, lam, ap, rp, out, fg[0]]

    return (loss[0, 0], grad_x[None], *by_weight(0), *by_weight(1), *by_weight(2), *by_weight(3))
```

```python
import math

import jax
import jax.numpy as jnp
from jax import lax
from jax.experimental import pallas as pl
from jax.experimental.pallas import tpu as pltpu

F32 = jnp.float32
BF16 = jnp.bfloat16

D = 1024
HEAD_DIM = 64
ROT_DIM = 16
ROPE_THETA = 500000.0
WINDOW = 128
LRU_C = 8.0
CONV_WIDTH = 4
NORM_EPS = 1e-6
IN_W = 6656
SEG_Q, SEG_K, SEG_V, SEG_GA, SEG_XR, SEG_GR, SEG_MA, SEG_MR = 0, 1024, 1280, 1536, 2560, 3584, 4608, 5632
N_DEV = 8
W_IN_SHARD = IN_W // N_DEV
ROW_SHARD = D // N_DEV
GATE_SHARD = 256 // N_DEV

ADAM_LR = 0.001
ADAM_B1 = 0.9
ADAM_B2 = 0.999
ADAM_EPS = 1e-08
ADAM_WD = 0.01
ADAM_STEP = 10

VMEM_LIMIT = 60 * 1024 * 1024
MESH = pl.DeviceIdType.MESH


def _dot(a, b):
    return jnp.dot(a, b, preferred_element_type=F32)


def _dot_nt(a, b):
    return lax.dot_general(a, b, (((1,), (1,)), ((), ())), preferred_element_type=F32)


def _dot_tn(a, b):
    return lax.dot_general(a, b, (((0,), (0,)), ((), ())), preferred_element_type=F32)


def _sigmoid(z):
    return 1.0 / (1.0 + jnp.exp(-z))


def _params(n_grid, vmem=VMEM_LIMIT):
    return pltpu.CompilerParams(dimension_semantics=("arbitrary",) * n_grid, vmem_limit_bytes=vmem)


def _position():
    return lax.axis_index("x"), lax.axis_index("y"), lax.axis_index("c")


def _flip(pos, k):
    x, y, c = pos
    return (1 - x if k & 4 else x, 1 - y if k & 2 else y, 1 - c if k & 1 else c)


def _slot(pos):
    return 4 * pos[0] + 2 * pos[1] + pos[2]


def _all_gather_direct(name, arrs):
    n = len(arrs)

    def body(*refs):
        ins, outs = refs[:n], refs[n : 2 * n]
        send_sems, recv_sems, local_sems = refs[2 * n :]
        me = _position()
        mine = [pltpu.make_async_copy(ins[a], outs[a].at[_slot(me)], local_sems.at[a]) for a in range(n)]
        for cp in mine:
            cp.start()

        def copy(a, k, block):
            return pltpu.make_async_remote_copy(
                src_ref=ins[a],
                dst_ref=outs[a].at[_slot(block)],
                send_sem=send_sems.at[a * 7 + k - 1],
                recv_sem=recv_sems.at[a * 7 + k - 1],
                device_id=_flip(me, k),
                device_id_type=MESH,
            )

        sent = [copy(a, k, me) for a in range(n) for k in range(1, N_DEV)]
        for cp in sent:
            cp.start()
        for a in range(n):
            for k in range(1, N_DEV):
                copy(a, k, _flip(me, k)).wait_recv()
        for cp in sent:
            cp.wait_send()
        for cp in mine:
            cp.wait()

    any_spec = pl.BlockSpec(memory_space=pl.ANY)
    return pl.pallas_call(
        body,
        name=name,
        out_shape=[jax.ShapeDtypeStruct((N_DEV,) + a.shape, a.dtype) for a in arrs],
        in_specs=[any_spec] * n,
        out_specs=[any_spec] * n,
        scratch_shapes=[
            pltpu.SemaphoreType.DMA((7 * n,)),
            pltpu.SemaphoreType.DMA((7 * n,)),
            pltpu.SemaphoreType.DMA((n,)),
        ],
    )(*arrs)


def _direct_gather_copies(src_hbm, all_hbm, send_sems, recv_sems, arriving):
    me = _position()
    return [
        pltpu.make_async_remote_copy(
            src_ref=src_hbm,
            dst_ref=all_hbm.at[_slot(_flip(me, k) if arriving else me)],
            send_sem=send_sems.at[k - 1],
            recv_sem=recv_sems.at[k - 1],
            device_id=_flip(me, k),
            device_id_type=MESH,
        )
        for k in range(1, N_DEV)
    ]


CHIP_FLIPS = (2, 4, 6)


def _chip_of(pos):
    return 2 * pos[0] + pos[1]


def _chip_copies(srcs, lands, views, send_sems, recv_sems):
    me = _position()
    copies = []
    for a, (src, land) in enumerate(zip(srcs, lands)):
        for j, k in enumerate(CHIP_FLIPS):
            peer = _flip(me, k)
            copies.append(
                pltpu.make_async_remote_copy(
                    src_ref=views[a](src, peer),
                    dst_ref=land.at[j],
                    send_sem=send_sems.at[len(CHIP_FLIPS) * a + j],
                    recv_sem=recv_sems.at[len(CHIP_FLIPS) * a + j],
                    device_id=peer,
                    device_id_type=MESH,
                )
            )
    return copies


PACK_GATE = 3 * ROW_SHARD
PACK_ROWS = PACK_GATE + 2 * GATE_SHARD


def _fwd_in(x, c_rows, conv_rows, w_ada, b_cols, norm_g, w_shard, tables, pack, where):
    t = x.shape[0]
    tm = min(1024, t)
    tn = IN_W // 4
    n_i, n_j = t // tm, IN_W // tn
    n_rope = (SEG_V - SEG_Q) // 128

    n_ada = w_ada.shape[1]

    def body(where_ref, x_ref, wada_ref, bcols_ref, ng_ref, ct_ref, st_ref, shard_hbm, pack_hbm, c_hbm, conv_hbm,
             proj_ref, h_ref, mod_ref, lane_hbm, wt_hbm, all_hbm, call_hbm, convall_hbm,
             wall, hbuf, cbuf, mcols, mod_all, mod_buf, lane,
             w_send, w_recv, load_sems, local_sem, lane_sem, p_send, p_recv, p_local,
             c_send, c_recv, c_local, v_send, v_recv, v_local, m_send, m_recv):
        j = pl.program_id(0)
        i = pl.program_id(1)
        me = _position()
        sibling = _flip(me, 1)
        chip = where_ref[0]
        slot = where_ref[1]

        def modulation():
            own_c = pltpu.make_async_copy(c_hbm, call_hbm.at[_slot(me)], c_local)
            own_v = pltpu.make_async_copy(conv_hbm, convall_hbm.at[_slot(me)], v_local)
            own_c.start()
            own_v.start()
            for cp in _direct_gather_copies(c_hbm, call_hbm, c_send, c_recv, arriving=False):
                cp.start()
            for cp in _direct_gather_copies(conv_hbm, convall_hbm, v_send, v_recv, arriving=False):
                cp.start()
            for cp in _direct_gather_copies(c_hbm, call_hbm, c_send, c_recv, arriving=True):
                cp.wait_recv()
            own_c.wait()
            fetch = pltpu.make_async_copy(call_hbm, cbuf, c_local)
            fetch.start()
            fetch.wait()
            mcols[...] = _dot(cbuf[:, 0, :].astype(BF16), wada_ref[...].astype(BF16)) + bcols_ref[...]
            mod_all[slot] = mcols[...]
            swaps = lambda arriving: [
                pltpu.make_async_remote_copy(
                    src_ref=mcols,
                    dst_ref=mod_all.at[_slot(_flip(me, k) if arriving else me)],
                    send_sem=m_send.at[k - 1],
                    recv_sem=m_recv.at[k - 1],
                    device_id=_flip(me, k),
                    device_id_type=MESH,
                )
                for k in range(1, N_DEV)
            ]
            for cp in swaps(False):
                cp.start()
            for cp in swaps(True):
                cp.wait_recv()
            mine = lax.broadcasted_iota(jnp.int32, (N_DEV, n_ada), 0) == slot
            for s in range(N_DEV):
                row = jnp.sum(jnp.where(mine, mod_all[s], 0.0), axis=0, keepdims=True)
                mod_buf[:, n_ada * s : n_ada * (s + 1)] = row
            for cp in swaps(False):
                cp.wait_send()
            for cp in _direct_gather_copies(c_hbm, call_hbm, c_send, c_recv, arriving=False):
                cp.wait_send()
            for cp in _direct_gather_copies(conv_hbm, convall_hbm, v_send, v_recv, arriving=True):
                cp.wait_recv()
            for cp in _direct_gather_copies(conv_hbm, convall_hbm, v_send, v_recv, arriving=False):
                cp.wait_send()
            own_v.wait()

        def shard_of(block):
            return wall.at[_chip_of(block), pl.ds(W_IN_SHARD * block[2], W_IN_SHARD), :]

        def shard_copy(k, block, to, src=None):
            return pltpu.make_async_remote_copy(
                src_ref=shard_of(block),
                dst_ref=shard_of(block),
                send_sem=w_send.at[k],
                recv_sem=w_recv.at[k],
                device_id=to,
                device_id_type=MESH,
            )

        own_shard = pltpu.make_async_copy(shard_hbm, shard_of(me), local_sem)
        own_pack = pltpu.make_async_copy(pack_hbm, all_hbm.at[_slot(me)], p_local)
        keep_lane = pltpu.make_async_copy(lane, lane_hbm, lane_sem)

        def lane_tables():
            d = lax.broadcasted_iota(jnp.int32, (128, 128), 0) % HEAD_DIM
            half = ROT_DIM // 2

            def fill(b, carry):
                cols = pl.ds(pl.multiple_of(128 * b, 128), 128)
                cos = jnp.concatenate([ct_ref[:, cols]] * (128 // half), axis=0)
                sin = jnp.concatenate([st_ref[:, cols]] * (128 // half), axis=0)
                lane[0, cols, :] = jnp.where(d < ROT_DIM, cos, 1.0).T
                lane[1, cols, :] = jnp.where(d < half, -sin, 0.0).T
                lane[2, cols, :] = jnp.where((d >= half) & (d < ROT_DIM), sin, 0.0).T
                return carry

            lax.fori_loop(0, t // 128, fill, 0)

        def keep_tile(q):
            return [
                pltpu.make_async_copy(wall.at[q, pl.ds(W_IN_SHARD * c, W_IN_SHARD), :], wt_hbm.at[2 * q + c], load_sems.at[2 * (q ^ chip) + c])
                for c in range(2)
            ]

        @pl.when((j == 0) & (i == 0))
        def _():
            modulation()
            mod_ref[...] = mod_buf[...]
            own_shard.start()
            own_shard.wait()
            shard_copy(0, me, sibling).start()
            for rel in range(1, 4):
                shard_copy(rel, me, _flip(me, 2 * rel)).start()
            own_pack.start()
            for cp in _direct_gather_copies(pack_hbm, all_hbm, p_send, p_recv, arriving=False):
                cp.start()
            lane_tables()
            keep_lane.start()
            shard_copy(0, sibling, me).wait_recv()
            for cp in keep_tile(chip):
                cp.start()

        for rel in range(1, 4):

            @pl.when((j == rel) & (i == 0))
            def _(rel=rel):
                there = _flip(me, 2 * rel)
                shard_copy(rel, there, me).wait_recv()
                shard_copy(3 + rel, there, sibling).start()
                shard_copy(3 + rel, _flip(sibling, 2 * rel), me).wait_recv()
                for cp in keep_tile(chip ^ rel):
                    cp.start()

        rows = pl.ds(pl.multiple_of(i * tm, tm), tm)

        @pl.when(j == 0)
        def _():
            xv = x_ref[...]
            r = lax.rsqrt(jnp.mean(xv * xv, axis=-1, keepdims=True) + NORM_EPS)
            shift = mod_buf[:, 0:D]
            scale = mod_buf[:, D : 2 * D]
            hv = (((xv * r) * ng_ref[...]) * (1.0 + scale) + shift).astype(BF16)
            hbuf[rows, :] = hv
            h_ref[...] = hv

        proj_ref[...] = _dot_nt(hbuf[rows, :], wall[chip ^ j])

        @pl.when((chip ^ j) == 0)
        def _():
            cos, s1, s2 = lane[0, rows, :], lane[1, rows, :], lane[2, rows, :]
            for g in range(n_rope):
                cols = slice(128 * g, 128 * (g + 1))
                proj_ref[:, cols] = _rope(proj_ref[:, cols], cos, s1, s2)

        @pl.when((j == n_j - 1) & (i == n_i - 1))
        def _():
            shard_copy(0, me, sibling).wait_send()
            for rel in range(1, 4):
                shard_copy(rel, me, _flip(me, 2 * rel)).wait_send()
                shard_copy(3 + rel, _flip(me, 2 * rel), sibling).wait_send()
            for q in range(N_DEV // 2):
                for cp in keep_tile(q):
                    cp.wait()
            for cp in _direct_gather_copies(pack_hbm, all_hbm, p_send, p_recv, arriving=True):
                cp.wait_recv()
            for cp in _direct_gather_copies(pack_hbm, all_hbm, p_send, p_recv, arriving=False):
                cp.wait_send()
            own_pack.wait()
            keep_lane.wait()

    tab = pl.BlockSpec((ROT_DIM // 2, t), lambda j, i, c: (0, 0))
    any_spec = pl.BlockSpec(memory_space=pl.ANY)
    return pl.pallas_call(
        body,
        name="fwd_in",
        grid_spec=pltpu.PrefetchScalarGridSpec(
            num_scalar_prefetch=1,
            grid=(n_j, n_i),
            in_specs=[
                pl.BlockSpec((tm, D), lambda j, i, c: (jnp.where(j == 0, i, 0), 0)),
                pl.BlockSpec(w_ada.shape, lambda j, i, c: (0, 0)),
                pl.BlockSpec((1, n_ada), lambda j, i, c: (0, 0)),
                pl.BlockSpec((1, D), lambda j, i, c: (0, 0)),
                tab, tab, any_spec, any_spec, any_spec, any_spec,
            ],
            out_specs=[pl.BlockSpec((tm, tn), lambda j, i, c: (i, c[0] ^ j)),
                       pl.BlockSpec((tm, D), lambda j, i, c: (jnp.where(j == 0, i, n_i - 1), 0)),
                       pl.BlockSpec((1, 3 * D), lambda j, i, c: (0, 0)),
                       any_spec, any_spec, any_spec, any_spec, any_spec],
            scratch_shapes=[
                pltpu.VMEM((N_DEV // 2, tn, D), BF16),
                pltpu.VMEM((t, D), BF16),
                pltpu.VMEM((N_DEV,) + c_rows.shape, F32),
                pltpu.VMEM((N_DEV, n_ada), F32),
                pltpu.VMEM((N_DEV, N_DEV, n_ada), F32),
                pltpu.VMEM((1, 3 * D), F32),
                pltpu.VMEM((3, t, 128), F32),
                pltpu.SemaphoreType.DMA((N_DEV - 1,)),
                pltpu.SemaphoreType.DMA((N_DEV - 1,)),
                pltpu.SemaphoreType.DMA((N_DEV,)),
                pltpu.SemaphoreType.DMA,
                pltpu.SemaphoreType.DMA,
                pltpu.SemaphoreType.DMA((N_DEV - 1,)),
                pltpu.SemaphoreType.DMA((N_DEV - 1,)),
                pltpu.SemaphoreType.DMA,
                pltpu.SemaphoreType.DMA((N_DEV - 1,)),
                pltpu.SemaphoreType.DMA((N_DEV - 1,)),
                pltpu.SemaphoreType.DMA,
                pltpu.SemaphoreType.DMA((N_DEV - 1,)),
                pltpu.SemaphoreType.DMA((N_DEV - 1,)),
                pltpu.SemaphoreType.DMA,
                pltpu.SemaphoreType.DMA((N_DEV - 1,)),
                pltpu.SemaphoreType.DMA((N_DEV - 1,)),
            ],
        ),
        out_shape=[jax.ShapeDtypeStruct((t, IN_W), F32), jax.ShapeDtypeStruct((t, D), BF16),
                   jax.ShapeDtypeStruct((1, 3 * D), F32),
                   jax.ShapeDtypeStruct((3, t, 128), F32),
                   jax.ShapeDtypeStruct((N_DEV,) + w_shard.shape, w_shard.dtype),
                   jax.ShapeDtypeStruct((N_DEV,) + pack.shape, pack.dtype),
                   jax.ShapeDtypeStruct((N_DEV,) + c_rows.shape, F32),
                   jax.ShapeDtypeStruct((N_DEV,) + conv_rows.shape, F32)],
        compiler_params=_params(2),
    )(where, x, w_ada, b_cols, norm_g, *tables, w_shard, pack, c_rows, conv_rows)


def _rope_tables(positions):
    inv_freq = ROPE_THETA ** (-jnp.arange(0, ROT_DIM, 2, dtype=F32) / ROT_DIM)
    ang_t = inv_freq[:, None] * positions.astype(F32)[None, :]
    return jnp.cos(ang_t), jnp.sin(ang_t)


def _rope(tv, cos, s1, s2):
    cols = []
    for g in range(tv.shape[1] // 128):
        blk = tv[:, 128 * g : 128 * (g + 1)]
        cols.append(blk * cos + pltpu.roll(blk, 128 - 8, 1) * s1 + pltpu.roll(blk, 8, 1) * s2)
    return cols[0] if len(cols) == 1 else jnp.concatenate(cols, axis=1)


def _rope_transposed(gv, cos, s1, s2):
    cols = []
    for g in range(gv.shape[1] // 128):
        blk = gv[:, 128 * g : 128 * (g + 1)]
        cols.append(blk * cos + pltpu.roll(blk * s1, 8, 1) + pltpu.roll(blk * s2, 128 - 8, 1))
    return cols[0] if len(cols) == 1 else jnp.concatenate(cols, axis=1)


def _lane_lo():
    return lax.broadcasted_iota(jnp.int32, (1, 128), 1) < HEAD_DIM


def _row_lo():
    return lax.broadcasted_iota(jnp.int32, (128, 1), 0) < HEAD_DIM


def _split_heads(tile, j):
    lo = _lane_lo()
    parts = []
    for g in (2 * j, 2 * j + 1):
        blk = tile[:, 128 * g : 128 * (g + 1)]
        parts += [jnp.where(lo, blk, 0.0), jnp.where(lo, 0.0, blk)]
    return jnp.concatenate(parts, axis=0)


def _dup_lanes(kv, j):
    rolled = pltpu.roll(kv, HEAD_DIM, 1)
    lo = _lane_lo()
    return jnp.where(lo, kv, rolled) if j == 0 else jnp.where(lo, rolled, kv)


def _fold_lanes(acc):
    return acc + pltpu.roll(acc, HEAD_DIM, 1)


def _fill_bias_t(bias_ref):
    kj = lax.broadcasted_iota(jnp.int32, (256, 128), 0)
    qi = lax.broadcasted_iota(jnp.int32, (256, 128), 1)
    diff = qi + WINDOW - kj
    band = (diff >= 0) & (diff < WINDOW)
    bias_ref[0] = jnp.where(band, 0.0, -1e30)
    bias_ref[1] = jnp.where(band & (kj >= 128), 0.0, -1e30)


def _sink_row(sink_ref, pair, j):
    return jnp.concatenate([jnp.full((1, 128), sink_ref[0, 8 * pair + 4 * j + h], F32) for h in range(4)], axis=1)


def _softmax_t(q4, kw, bias4, sink):
    s = _dot_nt(kw, q4) + bias4
    m = jnp.maximum(jnp.max(s, axis=0, keepdims=True), sink)
    p = jnp.exp(s - m)
    e_sink = jnp.exp(sink - m)
    denom = jnp.sum(p, axis=0, keepdims=True) + e_sink
    return p, e_sink, denom


def _attn_specs(t, tq):
    return dict(
        q=pl.BlockSpec((tq, 512), lambda p, i: (i, p)),
        k=pl.BlockSpec((t, 128), lambda p, i: (0, SEG_K // 128 + p)),
        v=pl.BlockSpec((t, 128), lambda p, i: (0, SEG_V // 128 + p)),
        sink=pl.BlockSpec(memory_space=pltpu.SMEM),
        act=pl.BlockSpec((tq, 512), lambda p, i: (i, p)),
    )


def _attn_fwd(proj, sinks):
    t = proj.shape[0]
    tq = min(512, t)
    nb = tq // 128
    n_tiles = t // tq
    sp = _attn_specs(t, tq)

    def body(q_ref, k_ref, v_ref, sink_ref, o_ref, kd, vdt, bias_ref):
        pair = pl.program_id(0)
        i = pl.program_id(1)

        @pl.when(i == 0)
        def _():
            kv, vv = k_ref[...], v_ref[...]
            for j in range(2):
                kd[j, 0:128, :] = jnp.zeros((128, 128), BF16)
                kd[j, 128:, :] = _dup_lanes(kv, j).astype(BF16)
                vdt[j, :, 0:128] = jnp.zeros((128, 128), BF16)
                vdt[j, :, 128:] = _dup_lanes(vv, j).T.astype(BF16)
            _fill_bias_t(bias_ref)

        row_lo = _row_lo()

        def block(b, carry):
            r0 = pl.multiple_of(b * 128, 128)
            gb = i * nb + b
            keys = pl.ds(pl.multiple_of(gb * 128, 128), 256)
            rows = pl.ds(r0, 128)
            qt = q_ref[rows, :] * (1.0 / math.sqrt(HEAD_DIM))
            bias = bias_ref[jnp.where(gb == 0, 1, 0)]
            bias4 = jnp.concatenate([bias] * 4, axis=1)
            cols = []
            for j in range(2):
                q4 = _split_heads(qt, j).astype(BF16)
                p, _, denom = _softmax_t(q4, kd[j, keys, :], bias4, _sink_row(sink_ref, pair, j))
                ot = _dot(vdt[j, :, keys], p.astype(BF16)) * (1.0 / denom)
                for gi in range(2):
                    z = jnp.where(row_lo, ot[:, 256 * gi : 256 * gi + 128], ot[:, 256 * gi + 128 : 256 * gi + 256])
                    cols.append(z.T)
            o_ref[rows, :] = jnp.concatenate(cols, axis=1).astype(BF16)
            return carry

        lax.fori_loop(0, nb, block, 0, unroll=True)

    return pl.pallas_call(
        body,
        name="attn_fwd",
        grid=(2, n_tiles),
        in_specs=[sp["q"], sp["k"], sp["v"], sp["sink"]],
        out_specs=sp["act"],
        out_shape=jax.ShapeDtypeStruct((t, D), BF16),
        scratch_shapes=[pltpu.VMEM((2, t + 128, 128), BF16), pltpu.VMEM((2, 128, t + 128), BF16), pltpu.VMEM((2, 256, 128), F32)],
        compiler_params=_params(2),
    )(proj, proj, proj, sinks)


def _attn_bwd(proj, o, d_o, tables, sinks, part_3):
    t = proj.shape[0]
    tq = min(512, t)
    nb = tq // 128
    n_tiles = t // tq
    lane_tables, (cos_t, sin_t) = tables
    sp = _attn_specs(t, tq)
    scale = 1.0 / math.sqrt(HEAD_DIM)
    half = ROT_DIM // 2

    def unrope_rows(z, c, s):
        out = []
        for base in (0, HEAD_DIM):
            g0, g1 = z[base : base + half], z[base + half : base + ROT_DIM]
            out += [g0 * c + g1 * s, g1 * c - g0 * s, z[base + ROT_DIM : base + HEAD_DIM]]
        return jnp.concatenate(out, axis=0)

    def swap(part_hbm, land, send_sem, recv_sem):
        sibling = _flip(_position(), 1)
        return pltpu.make_async_remote_copy(
            src_ref=part_hbm.at[:, :, sibling[2]], dst_ref=land, send_sem=send_sem, recv_sem=recv_sem,
            device_id=sibling, device_id_type=MESH,
        )

    def body(q_ref, k_ref, v_ref, o_ref, do_ref, ct_ref, st_ref, ck, s1k, s2k, sink_ref, part_hbm,
             dq_ref, dk_ref, dv_ref, dsink_ref, sum_hbm, mine_hbm, kd, kdt, vd, bias_ref, dkacc, dvacc, own, land, stage,
             send_sem, recv_sem, local_sems):
        pair = pl.program_id(0)
        i = pl.program_id(1)
        me = _position()
        load_own = pltpu.make_async_copy(part_hbm.at[:, :, me[2]], own, local_sems.at[0])

        @pl.when((pair == 0) & (i == 0))
        def _():
            swap(part_hbm, land, send_sem, recv_sem).start()
            load_own.start()

        @pl.when(i == 0)
        def _():
            kv, vv = k_ref[...], v_ref[...]
            for j in range(2):
                kdup = _dup_lanes(kv, j)
                kd[j, 0:128, :] = jnp.zeros((128, 128), BF16)
                kd[j, 128:, :] = kdup.astype(BF16)
                kdt[j, :, 0:128] = jnp.zeros((128, 128), BF16)
                kdt[j, :, 128:] = kdup.T.astype(BF16)
                vd[j, 0:128, :] = jnp.zeros((128, 128), BF16)
                vd[j, 128:, :] = _dup_lanes(vv, j).astype(BF16)
            _fill_bias_t(bias_ref)
            dkacc[...] = jnp.zeros_like(dkacc)
            dvacc[...] = jnp.zeros_like(dvacc)
            dsink_ref[...] = jnp.zeros_like(dsink_ref)

        row_lo = _row_lo()
        ones = jnp.ones((8, 128), BF16)

        def block(b, carry):
            r0 = pl.multiple_of(b * 128, 128)
            gb = i * nb + b
            keys = pl.ds(pl.multiple_of(gb * 128, 128), 256)
            rows = pl.ds(r0, 128)
            qt = q_ref[rows, :] * scale
            dob = do_ref[rows, :].astype(F32)
            prod = dob * o_ref[rows, :].astype(F32)
            ct, st = ct_ref[:, rows], st_ref[:, rows]
            bias = bias_ref[jnp.where(gb == 0, 1, 0)]
            bias4 = jnp.concatenate([bias] * 4, axis=1)
            cols = []
            for j in range(2):
                q4 = _split_heads(qt, j).astype(BF16)
                kw = kd[j, keys, :]
                p, e_sink, denom = _softmax_t(q4, kw, bias4, _sink_row(sink_ref, pair, j))
                inv = 1.0 / denom
                prob = p * inv
                do4 = _split_heads(dob, j).astype(BF16)
                delta = _dot_nt(ones, _split_heads(prod, j).astype(BF16))[0:1, :]
                dp = _dot_nt(vd[j, keys, :], do4)
                ds = (prob * (dp - delta)).astype(BF16)
                dkacc[j, keys, :] = dkacc[j, keys, :] + _dot(ds, q4)
                dvacc[j, keys, :] = dvacc[j, keys, :] + _dot(prob.astype(BF16), do4)
                dqt = _dot(kdt[j, :, keys], ds) * scale
                dsink = -(e_sink * inv) * delta
                for h in range(4):
                    row = 4 * j + h
                    total = jnp.sum(dsink[:, 128 * h : 128 * (h + 1)], axis=1, keepdims=True)
                    dsink_ref[0, row : row + 1, :] = dsink_ref[0, row : row + 1, :] + jnp.broadcast_to(total, (1, 128))
                for gi in range(2):
                    z = jnp.where(row_lo, dqt[:, 256 * gi : 256 * gi + 128], dqt[:, 256 * gi + 128 : 256 * gi + 256])
                    cols.append(unrope_rows(z, ct, st).T)
            dq_ref[rows, :] = jnp.concatenate(cols, axis=1).astype(BF16)
            return carry

        lax.fori_loop(0, nb, block, 0, unroll=True)

        @pl.when(i == n_tiles - 1)
        def _():
            lo = _lane_lo()
            dk = jnp.where(lo, _fold_lanes(dkacc[0, 128:, :]), _fold_lanes(dkacc[1, 128:, :]))
            dk_ref[...] = _rope_transposed(dk, ck[...], s1k[...], s2k[...]).astype(BF16)
            dv_ref[...] = jnp.where(lo, _fold_lanes(dvacc[0, 128:, :]), _fold_lanes(dvacc[1, 128:, :])).astype(BF16)

        @pl.when((pair == 1) & (i == n_tiles - 1))
        def _():
            swap(part_hbm, land, send_sem, recv_sem).wait()
            load_own.wait()
            own[...] = own[...] + land[...]
            stage[...] = own[...].astype(BF16)
            results = [pltpu.make_async_copy(stage, sum_hbm, local_sems.at[1]),
                       pltpu.make_async_copy(own.at[:, _chip_of(me)], mine_hbm, local_sems.at[2])]
            for cp in results:
                cp.start()
            for cp in results:
                cp.wait()

    kv_out = pl.BlockSpec((t, 128), lambda p, i: (0, p))
    tab_t = pl.BlockSpec((half, tq), lambda p, i: (0, i))
    tab_k = [pl.BlockSpec((None, t, 128), lambda p, i, a=a: (a, 0, 0)) for a in range(3)]
    return pl.pallas_call(
        body,
        name="attn_bwd",
        grid=(2, n_tiles),
        in_specs=[sp["q"], sp["k"], sp["v"], sp["act"], sp["act"], tab_t, tab_t, *tab_k, sp["sink"],
                  pl.BlockSpec(memory_space=pl.ANY)],
        out_specs=[sp["act"], kv_out, kv_out, pl.BlockSpec((1, 8, 128), lambda p, i: (p, 0, 0)),
                   pl.BlockSpec(memory_space=pl.ANY), pl.BlockSpec(memory_space=pl.ANY)],
        out_shape=[
            jax.ShapeDtypeStruct((t, D), BF16),
            jax.ShapeDtypeStruct((t, 256), BF16),
            jax.ShapeDtypeStruct((t, 256), BF16),
            jax.ShapeDtypeStruct((2, 8, 128), F32),
            jax.ShapeDtypeStruct((3, 4, ROW_SHARD, D), BF16),
            jax.ShapeDtypeStruct((3, ROW_SHARD, D), F32),
        ],
        scratch_shapes=[pltpu.VMEM((2, t + 128, 128), BF16), pltpu.VMEM((2, 128, t + 128), BF16), pltpu.VMEM((2, t + 128, 128), BF16),
                        pltpu.VMEM((2, 256, 128), F32), pltpu.VMEM((2, t + 128, 128), F32), pltpu.VMEM((2, t + 128, 128), F32),
                        pltpu.VMEM((3, 4, ROW_SHARD, D), F32), pltpu.VMEM((3, 4, ROW_SHARD, D), F32), pltpu.VMEM((3, 4, ROW_SHARD, D), BF16),
                        pltpu.SemaphoreType.DMA, pltpu.SemaphoreType.DMA, pltpu.SemaphoreType.DMA((3,))],
        compiler_params=_params(2),
    )(proj, proj, proj, o, d_o, cos_t, sin_t, lane_tables, lane_tables, lane_tables, sinks, part_3)


HALF = 512


def _softplus(y):
    u = jnp.exp(-jnp.abs(y))
    w = 1.0 + u
    log1p = jnp.where(w == 1.0, u, jnp.log(w) * (u / jnp.where(w == 1.0, 1.0, w - 1.0)))
    return jnp.maximum(y, 0.0) + log1p


def _gate_block(w_ref, blk):
    return w_ref[:, :, 256 * blk : 256 * (blk + 1)].reshape(256, 256)


def _gate_matmul(act_bf, w_ref):
    return jnp.concatenate([_dot(act_bf[:, 256 * blk : 256 * (blk + 1)], _gate_block(w_ref, blk)) for blk in range(2)], axis=1)


def _gate_matmul_t(grad_bf, w_ref):
    return jnp.concatenate([_dot_nt(grad_bf[:, 256 * blk : 256 * (blk + 1)], _gate_block(w_ref, blk)) for blk in range(2)], axis=1)


def _conv_and_gates(xe_ref, tm, cw_ref, cb_ref, wa_ref, wx_ref, ba_ref, bx_ref, lam_ref, keep):
    xc = cb_ref[...] + sum(cw_ref[k : k + 1, :] * xe_ref[pl.ds(5 + k, tm), :] for k in range(CONV_WIDTH))
    xc_bf = xc.astype(BF16)
    r = _sigmoid(_gate_matmul(xc_bf, wa_ref) + ba_ref[...])
    ig = _sigmoid(_gate_matmul(xc_bf, wx_ref) + bx_ref[...])
    sp = _softplus(-lam_ref[...])
    log_a = -LRU_C * r * sp
    tanh = jnp.tanh(-log_a)
    mult = jnp.sqrt(2.0 * tanh / (1.0 + tanh))
    a = jnp.where(keep, jnp.exp(log_a), 0.0)
    mult = jnp.where(keep, mult, 1.0)
    return xc, xc_bf, r, ig, sp, a, mult


def _scan_rows(a_ref, b_ref, h_ref, carry_ref, tm, reverse):
    width = a_ref.shape[1]
    row = lax.broadcasted_iota(jnp.int32, (8, width), 0)

    def group(g, carry):
        gg = tm // 8 - 1 - g if reverse else g
        r0 = pl.multiple_of(gg * 8, 8)
        a = a_ref[pl.ds(r0, 8), :]
        b = b_ref[pl.ds(r0, 8), :]
        for s in (1, 2, 4):
            shift = 8 - s if reverse else s
            a_sh = pltpu.roll(a, shift, 0)
            b_sh = pltpu.roll(b, shift, 0)
            ok = (row < 8 - s) if reverse else (row >= s)
            b = jnp.where(ok, a * b_sh + b, b)
            a = jnp.where(ok, a * a_sh, a)
        h = a * carry + b
        h_ref[pl.ds(r0, 8), :] = h
        last = h[0:1, :] if reverse else h[7:8, :]
        return jnp.broadcast_to(last, (8, width))

    carry_ref[...] = lax.fori_loop(0, tm // 8, group, carry_ref[...])


def _rnn_specs(t, tm, order):
    n = t // tm
    return dict(
        tile=pl.BlockSpec((tm, HALF), lambda ch, i: (order(i, n), ch)),
        before=lambda col0: pl.BlockSpec((8, HALF), lambda ch, i: (jnp.maximum(order(i, n) * (tm // 8) - 1, 0), col0 + ch)),
        proj_tile=lambda col0: pl.BlockSpec((tm, HALF), lambda ch, i: (order(i, n), col0 + ch)),
        keep=pl.BlockSpec((tm, 1), lambda ch, i: (order(i, n), 0)),
        gates=lambda which: pl.BlockSpec((N_DEV, GATE_SHARD, HALF), lambda ch, i: (0, PACK_GATE // GATE_SHARD + which, ch)),
        gate_grads=pl.BlockSpec((2, 2, 256, 256), lambda ch, i: (0, ch, 0, 0)),
        vec=pl.BlockSpec((1, HALF), lambda ch, i: (0, ch)),
        conv=pl.BlockSpec((CONV_WIDTH, HALF), lambda ch, i: (0, ch)),
    )


def _rnn_fwd(proj, keep, w_all, rg_ba, rg_bx, rg_lambda, conv_w, conv_b):
    t = proj.shape[0]
    tm = min(512, t)
    sp = _rnn_specs(t, tm, lambda i, n: i)

    def body(xr_ref, xb_ref, keep_ref, wa_ref, wx_ref, ba_ref, bx_ref, lam_ref, cw_ref, cb_ref, hh_ref, xe, a_buf, b_buf, carry):
        i = pl.program_id(1)

        @pl.when(i == 0)
        def _():
            carry[...] = jnp.zeros_like(carry)

        xe[0:8, :] = jnp.where(i > 0, xb_ref[...], 0.0)
        xe[8:, :] = xr_ref[...]
        keep_rows = keep_ref[...] != 0.0
        xc, _, _, ig, _, a, mult = _conv_and_gates(xe, tm, cw_ref, cb_ref, wa_ref, wx_ref, ba_ref, bx_ref, lam_ref, keep_rows)
        a_buf[...] = a
        b_buf[...] = mult * (ig * xc)
        _scan_rows(a_buf, b_buf, hh_ref, carry, tm, reverse=False)

    return pl.pallas_call(
        body,
        name="rnn_fwd",
        grid=(2, t // tm),
        in_specs=[sp["proj_tile"](SEG_XR // HALF), sp["before"](SEG_XR // HALF), sp["keep"], sp["gates"](0), sp["gates"](1),
                  sp["vec"], sp["vec"], sp["vec"], sp["conv"], sp["vec"]],
        out_specs=sp["tile"],
        out_shape=jax.ShapeDtypeStruct((t, D), F32),
        scratch_shapes=[pltpu.VMEM((tm + 8, HALF), F32), pltpu.VMEM((tm, HALF), F32), pltpu.VMEM((tm, HALF), F32), pltpu.VMEM((8, HALF), F32)],
        compiler_params=_params(2),
    )(proj, proj, keep, w_all, w_all, rg_ba, rg_bx, rg_lambda, conv_w, conv_b)


def _rnn_bwd(proj, hh, d_hh, keep, w_all, rg_ba, rg_bx, rg_lambda, conv_w, conv_b, sum_3_bf):
    t = proj.shape[0]
    tm = min(512, t)
    n_tiles = t // tm
    sp = _rnn_specs(t, tm, lambda i, n: n - 1 - i)

    def body(xr_ref, xb_ref, hh_ref, hb_ref, dhh_ref, keep_ref, wa_ref, wx_ref, ba_ref, bx_ref, lam_ref, cw_ref, cb_ref,
             s3_hbm, dxr_ref, gw_ref, gvec_ref, l3_hbm, xe, he, ae, c_buf, g_buf, dxe, carry, a_next, send_sems, recv_sems):
        ch = pl.program_id(0)
        i = pl.program_id(1)
        first_tile = i == n_tiles - 1
        views = [lambda ref, peer: ref.at[:, _chip_of(peer)]]

        @pl.when((ch == 0) & (i == 0))
        def _():
            for cp in _chip_copies([s3_hbm], [l3_hbm], views, send_sems, recv_sems):
                cp.start()

        @pl.when((ch == 1) & (i == n_tiles - 1))
        def _():
            for cp in _chip_copies([s3_hbm], [l3_hbm], views, send_sems, recv_sems):
                cp.wait_recv()
                cp.wait_send()

        @pl.when(i == 0)
        def _():
            carry[...] = jnp.zeros_like(carry)
            a_next[...] = jnp.zeros_like(a_next)
            dxe[tm:, :] = jnp.zeros((8, HALF), F32)
            gw_ref[...] = jnp.zeros_like(gw_ref)
            gvec_ref[...] = jnp.zeros_like(gvec_ref)

        xe[0:8, :] = jnp.where(first_tile, 0.0, xb_ref[...])
        xe[8:, :] = xr_ref[...]
        he[0:8, :] = jnp.where(first_tile, 0.0, hb_ref[...])
        he[8:, :] = hh_ref[...]
        keep_rows = keep_ref[...] != 0.0
        xc, xc_bf, r, ig, sp_, a, mult = _conv_and_gates(xe, tm, cw_ref, cb_ref, wa_ref, wx_ref, ba_ref, bx_ref, lam_ref, keep_rows)

        ae[0:tm, :] = a
        ae[tm:, :] = a_next[...]
        a_next[...] = jnp.broadcast_to(a[0:1, :], (8, HALF))
        c_buf[...] = ae[pl.ds(1, tm), :]
        g_buf[...] = dhh_ref[...]
        _scan_rows(c_buf, g_buf, g_buf, carry, tm, reverse=True)
        g = g_buf[...]

        h_prev = he[pl.ds(7, tm), :]
        bb = ig * xc
        d_ig = g * mult * xc
        d_xc = g * mult * ig
        d_log_a = jnp.where(keep_rows, g * h_prev * a - (g * bb) * (a * a) / mult, 0.0)
        d_r = d_log_a * (-LRU_C * sp_)
        d_sp = jnp.sum(d_log_a * (-LRU_C * r), axis=0, keepdims=True)
        d_zr = d_r * r * (1.0 - r)
        d_zi = d_ig * ig * (1.0 - ig)
        d_zr_bf = d_zr.astype(BF16)
        d_zi_bf = d_zi.astype(BF16)
        d_xc = d_xc + _gate_matmul_t(d_zr_bf, wa_ref) + _gate_matmul_t(d_zi_bf, wx_ref)
        for blk in range(2):
            cols = slice(256 * blk, 256 * (blk + 1))
            gw_ref[0, blk] = gw_ref[0, blk] + _dot_tn(xc_bf[:, cols], d_zr_bf[:, cols])
            gw_ref[1, blk] = gw_ref[1, blk] + _dot_tn(xc_bf[:, cols], d_zi_bf[:, cols])

        dxe[0:tm, :] = d_xc
        shifted = [dxe[pl.ds(3 - k, tm), :] for k in range(CONV_WIDTH)]
        dxr = sum(cw_ref[k : k + 1, :] * shifted[k] for k in range(CONV_WIDTH))
        dxr_ref[...] = dxr.astype(BF16)
        dxe[tm:, :] = d_xc[0:8, :]

        lam = lam_ref[...]
        xr_tile = xr_ref[...]
        sums = [
            jnp.sum(d_zr, axis=0, keepdims=True),
            jnp.sum(d_zi, axis=0, keepdims=True),
            d_sp * (-_sigmoid(-lam)),
            jnp.sum(d_xc, axis=0, keepdims=True),
        ] + [jnp.sum(xr_tile * shifted[k], axis=0, keepdims=True) for k in range(CONV_WIDTH)]
        gvec_ref[...] = gvec_ref[...] + jnp.concatenate(sums, axis=0)

    return pl.pallas_call(
        body,
        name="rnn_bwd",
        grid=(2, n_tiles),
        in_specs=[
            sp["proj_tile"](SEG_XR // HALF), sp["before"](SEG_XR // HALF), sp["tile"], sp["before"](0), sp["tile"], sp["keep"],
            sp["gates"](0), sp["gates"](1), sp["vec"], sp["vec"], sp["vec"], sp["conv"], sp["vec"], pl.BlockSpec(memory_space=pl.ANY),
        ],
        out_specs=[sp["tile"], sp["gate_grads"], pl.BlockSpec((8, HALF), lambda ch, i: (0, ch)), pl.BlockSpec(memory_space=pl.ANY)],
        out_shape=[
            jax.ShapeDtypeStruct((t, D), BF16),
            jax.ShapeDtypeStruct((2, 4, 256, 256), F32),
            jax.ShapeDtypeStruct((8, D), F32),
            jax.ShapeDtypeStruct((len(CHIP_FLIPS), 3, ROW_SHARD, D), BF16),
        ],
        scratch_shapes=[
            pltpu.VMEM((tm + 8, HALF), F32),
            pltpu.VMEM((tm + 8, HALF), F32),
            pltpu.VMEM((tm + 8, HALF), F32),
            pltpu.VMEM((tm, HALF), F32),
            pltpu.VMEM((tm, HALF), F32),
            pltpu.VMEM((tm + 8, HALF), F32),
            pltpu.VMEM((8, HALF), F32),
            pltpu.VMEM((8, HALF), F32),
            pltpu.SemaphoreType.DMA((len(CHIP_FLIPS),)),
            pltpu.SemaphoreType.DMA((len(CHIP_FLIPS),)),
        ],
        compiler_params=_params(2),
    )(proj, proj, hh, hh, d_hh, keep, w_all, w_all, rg_ba, rg_bx, rg_lambda, conv_w, conv_b, sum_3_bf)


def _mid(x, target, o, hh, proj, w_all, gate, final_g):
    t = x.shape[0]
    tm = min(256, t)
    n_tiles = t // tm
    full = pl.BlockSpec((tm, D), lambda i: (i, 0))
    half = lambda col: pl.BlockSpec((tm, 512), lambda i: (i, col))
    vec = pl.BlockSpec((1, D), lambda i: (0, 0))
    any_spec = pl.BlockSpec(memory_space=pl.ANY)

    def body(x_ref, tg_ref, o_ref, hh_ref, ga0, ga1, gr0, gr1, ma0, ma1, mr0, mr1, w_hbm, gate_ref, fg_ref,
             dres_ref, do_ref, dhh_ref, dga_ref, dgr_ref, dma_ref, dmr_ref, gw_hbm, vec_ref, w, gw, sem, w_sems):
        i = pl.program_id(0)

        @pl.when(i == 0)
        def _():
            loads = [
                pltpu.make_async_copy(
                    w_hbm.at[s, pl.ds(ROW_SHARD * k, ROW_SHARD), :], w.at[k, pl.ds(ROW_SHARD * s, ROW_SHARD), :], w_sems.at[k * N_DEV + s]
                )
                for k in range(3)
                for s in range(N_DEV)
            ]
            for cp in loads:
                cp.start()
            for cp in loads:
                cp.wait()
            gw[...] = jnp.zeros_like(gw)
            vec_ref[...] = jnp.zeros_like(vec_ref)

        cat = lambda a, b: jnp.concatenate([a[...], b[...]], axis=1)
        ga, gr, ma, mr = cat(ga0, ga1), cat(gr0, gr1), cat(ma0, ma1), cat(mr0, mr1)
        ov, hv, gate_v, fg = o_ref[...].astype(F32), hh_ref[...], gate_ref[...], fg_ref[...]
        sg_a = _sigmoid(ga)
        sg_r = _sigmoid(gr)
        silu_a = ga * sg_a
        silu_r = gr * sg_r
        ya = (ov * silu_a).astype(BF16)
        yr = (hv * silu_r).astype(BF16)
        pa = _dot(ya, w[0])
        pr = _dot(yr, w[1])
        sa = _sigmoid(ma)
        sr = _sigmoid(mr)
        merged = (sa * pa + sr * pr).astype(BF16)
        mo = _dot(merged, w[2])
        out = x_ref[...] + gate_v * mo
        r2 = lax.rsqrt(jnp.mean(out * out, axis=-1, keepdims=True) + NORM_EPS)
        outn = out * r2
        err = outn * fg - tg_ref[...]
        loss = 0.5 * jnp.sum(jnp.mean(err * err, axis=-1, keepdims=True), axis=0, keepdims=True)
        dy = err * (1.0 / D)
        g_final = jnp.sum(dy * outn, axis=0, keepdims=True)
        doutn = dy * fg
        dout = r2 * (doutn - outn * jnp.mean(doutn * outn, axis=-1, keepdims=True))
        dres_ref[...] = dout
        d_gate = jnp.sum(dout * mo, axis=0, keepdims=True)
        dmo = (dout * gate_v).astype(BF16)
        dmerged = _dot_nt(dmo, w[2])
        gw[2] = gw[2] + _dot_tn(merged, dmo)
        dpa = (dmerged * sa).astype(BF16)
        dpr = (dmerged * sr).astype(BF16)
        dma_ref[...] = (dmerged * pa * (sa * (1.0 - sa))).astype(BF16)
        dmr_ref[...] = (dmerged * pr * (sr * (1.0 - sr))).astype(BF16)
        dya = _dot_nt(dpa, w[0])
        dyr = _dot_nt(dpr, w[1])
        gw[0] = gw[0] + _dot_tn(ya, dpa)
        gw[1] = gw[1] + _dot_tn(yr, dpr)
        do_ref[...] = (dya * silu_a).astype(BF16)
        dhh_ref[...] = dyr * silu_r
        dga_ref[...] = (dya * ov * (sg_a * (1.0 + ga * (1.0 - sg_a)))).astype(BF16)
        dgr_ref[...] = (dyr * hv * (sg_r * (1.0 + gr * (1.0 - sg_r)))).astype(BF16)
        zeros = jnp.zeros((5, D), F32)
        vec_ref[...] = vec_ref[...] + jnp.concatenate([g_final, d_gate, jnp.broadcast_to(loss, (1, D)), zeros], axis=0)

        @pl.when(i == n_tiles - 1)
        def _():
            cp = pltpu.make_async_copy(gw, gw_hbm, sem)
            cp.start()
            cp.wait()

    bf_tile = jax.ShapeDtypeStruct((t, D), BF16)
    f_tile = jax.ShapeDtypeStruct((t, D), F32)
    return pl.pallas_call(
        body,
        name="mid",
        grid=(n_tiles,),
        in_specs=[full, full, full, full,
                  half(SEG_GA // 512), half(SEG_GA // 512 + 1), half(SEG_GR // 512), half(SEG_GR // 512 + 1),
                  half(SEG_MA // 512), half(SEG_MA // 512 + 1), half(SEG_MR // 512), half(SEG_MR // 512 + 1),
                  any_spec, vec, vec],
        out_specs=[full, full, full, full, full, full, full, any_spec, pl.BlockSpec((8, D), lambda i: (0, 0))],
        out_shape=[f_tile, bf_tile, f_tile, bf_tile, bf_tile, bf_tile, bf_tile,
                   jax.ShapeDtypeStruct((3, D, D), F32), jax.ShapeDtypeStruct((8, D), F32)],
        scratch_shapes=[pltpu.VMEM((3, D, D), BF16), pltpu.VMEM((3, D, D), F32), pltpu.SemaphoreType.DMA, pltpu.SemaphoreType.DMA((3 * N_DEV,))],
        compiler_params=_params(1),
    )(x, target, o, hh, proj, proj, proj, proj, proj, proj, proj, proj, w_all, gate, final_g)


_SEGMENTS = ((SEG_Q, 1024), (SEG_K, 256), (SEG_V, 256), (SEG_GA, 1024), (SEG_XR, 1024), (SEG_GR, 1024), (SEG_MA, 1024), (SEG_MR, 1024))


def _seg_specs(tm):
    return [pl.BlockSpec((tm, width), lambda i: (i, 0)) for _, width in _SEGMENTS]


def _bwd_dh(dsegs, w_all, x, dres, mod, norm_g):
    t = x.shape[0]
    tm = min(512, t)
    n_tiles = t // tm
    n_seg = len(_SEGMENTS)
    full = pl.BlockSpec((tm, D), lambda i: (i, 0))

    def body(*refs):
        segs = refs[:n_seg]
        w_hbm, x_ref, dres_ref, mod_ref, ng_ref, gx_ref, vec_ref, w, sems = refs[n_seg:]
        i = pl.program_id(0)

        @pl.when(i == 0)
        def _():
            loads = [
                pltpu.make_async_copy(w_hbm.at[s], w.at[pl.ds(W_IN_SHARD * s, W_IN_SHARD), :], sems.at[s])
                for s in range(N_DEV)
            ]
            for cp in loads:
                cp.start()
            for cp in loads:
                cp.wait()
            vec_ref[...] = jnp.zeros_like(vec_ref)

        dh = sum(_dot(seg[...], w[start : start + width, :]) for seg, (start, width) in zip(segs, _SEGMENTS))
        xv = x_ref[...]
        r1 = lax.rsqrt(jnp.mean(xv * xv, axis=-1, keepdims=True) + NORM_EPS)
        xn = xv * r1
        ng = ng_ref[...]
        scale1 = 1.0 + mod_ref[:, D : 2 * D]
        d_shift = jnp.sum(dh, axis=0, keepdims=True)
        d_scale = jnp.sum(dh * (xn * ng), axis=0, keepdims=True)
        g_norm = jnp.sum(dh * xn * scale1, axis=0, keepdims=True)
        dxn = dh * (ng * scale1)
        dx = r1 * (dxn - xn * jnp.mean(dxn * xn, axis=-1, keepdims=True))
        gx_ref[...] = dres_ref[...] + dx
        vec_ref[...] = vec_ref[...] + jnp.concatenate([d_shift, d_scale, g_norm, jnp.zeros((5, D), F32)], axis=0)

    any_spec = pl.BlockSpec(memory_space=pl.ANY)
    return pl.pallas_call(
        body,
        name="bwd_dh",
        grid=(n_tiles,),
        in_specs=_seg_specs(tm) + [any_spec, full, full, pl.BlockSpec((1, 3 * D), lambda i: (0, 0)), pl.BlockSpec((1, D), lambda i: (0, 0))],
        out_specs=[full, pl.BlockSpec((8, D), lambda i: (0, 0))],
        out_shape=[jax.ShapeDtypeStruct((t, D), F32), jax.ShapeDtypeStruct((8, D), F32)],
        scratch_shapes=[pltpu.VMEM((IN_W, D), BF16), pltpu.SemaphoreType.DMA((N_DEV,))],
        compiler_params=_params(1),
    )(*dsegs, w_all, x, dres, mod, norm_g)


_HBM = pl.BlockSpec(memory_space=pltpu.HBM)
_SEM = pl.BlockSpec(memory_space=pltpu.SEMAPHORE)
_EFFECT = pltpu.SideEffectType.DATAFLOW_SIDE_EFFECTING


def _w_exchange_copies(sum_ref, land_ref, sems):
    me = _position()
    n = len(CHIP_FLIPS)
    return [
        pltpu.make_async_remote_copy(
            src_ref=sum_ref.at[_chip_of(_flip(me, k))], dst_ref=land_ref.at[j], send_sem=sems[j], recv_sem=sems[n + j],
            device_id=_flip(me, k), device_id_type=MESH,
        )
        for j, k in enumerate(CHIP_FLIPS)
    ]


def _w_exchange_start(sum_w_bf):
    n_sems = 2 * len(CHIP_FLIPS)
    land = lax.empty((len(CHIP_FLIPS), W_IN_SHARD, D), BF16)

    def body(sum_ref, land_ref, *outs):
        for cp in _w_exchange_copies(sum_ref, land_ref, outs[:n_sems]):
            cp.start()
        token = outs[n_sems + 2]
        token[...] = jnp.zeros_like(token)

    res = pl.pallas_call(
        body,
        name="w_exchange_start",
        out_shape=(*[pltpu.SemaphoreType.DMA(())] * n_sems, pltpu.HBM(sum_w_bf.shape, BF16), pltpu.HBM(land.shape, BF16),
                   jax.ShapeDtypeStruct((8, 128), F32)),
        in_specs=(_HBM, _HBM),
        out_specs=(*[_SEM] * n_sems, _HBM, _HBM, pl.BlockSpec(memory_space=pltpu.VMEM)),
        input_output_aliases={0: n_sems, 1: n_sems + 1},
        compiler_params=pltpu.CompilerParams(has_side_effects=_EFFECT),
    )(pltpu.with_memory_space_constraint(sum_w_bf, pltpu.HBM), pltpu.with_memory_space_constraint(land, pltpu.HBM))
    return res[:n_sems], res[n_sems], res[n_sems + 1], res[n_sems + 2]


def _w_exchange_wait(sems, sum_thru, land_thru, after):
    n_sems = len(sems)

    def body(sum_ref, land_ref, *rest):
        for cp in _w_exchange_copies(sum_ref, land_ref, rest[:n_sems]):
            cp.wait_send()
            cp.wait_recv()

    return pl.pallas_call(
        body,
        name="w_exchange_wait",
        out_shape=(pltpu.HBM(sum_thru.shape, BF16), pltpu.HBM(land_thru.shape, BF16)),
        in_specs=(_HBM, _HBM, *[_SEM] * n_sems, pl.BlockSpec(memory_space=pl.ANY)),
        out_specs=(_HBM, _HBM),
        input_output_aliases={0: 0, 1: 1},
        compiler_params=pltpu.CompilerParams(has_side_effects=_EFFECT),
    )(sum_thru, land_thru, *sems, after)[1]


def _bwd_gw(dsegs, h, g_gates):
    t = h.shape[0]
    tm = min(512, t)
    n_tiles = t // tm
    n_seg = len(_SEGMENTS)
    n_chips = N_DEV // 2

    def gate_copies(g_hbm, lg_hbm, send_sems, recv_sems, arriving):
        me = _position()
        return [
            pltpu.make_async_remote_copy(
                src_ref=g_hbm.at[:, :, _slot(_flip(me, k))],
                dst_ref=lg_hbm.at[_slot(_flip(me, k) if arriving else me)],
                send_sem=send_sems.at[k - 1],
                recv_sem=recv_sems.at[k - 1],
                device_id=_flip(me, k),
                device_id_type=MESH,
            )
            for k in range(1, N_DEV)
        ]

    def body(*refs):
        segs = refs[:n_seg]
        (h_ref, g_hbm, own_hbm, sum_hbm, lg_hbm, acc, stage, land, own_sems, pair_send, pair_recv, send_sems, recv_sems, g_local) = refs[n_seg:]
        i = pl.program_id(0)
        slot = _slot(_position())
        own_gates = pltpu.make_async_copy(g_hbm.at[:, :, slot], lg_hbm.at[slot], g_local)

        @pl.when(i == 0)
        def _():
            own_gates.start()
            for cp in gate_copies(g_hbm, lg_hbm, send_sems, recv_sems, arriving=False):
                cp.start()
            acc[...] = jnp.zeros_like(acc)

        hv = h_ref[...]
        for seg, (start, width) in zip(segs, _SEGMENTS):
            acc[start : start + width, :] = acc[start : start + width, :] + _dot_tn(seg[...], hv)

        @pl.when(i == n_tiles - 1)
        def _():
            me = _position()
            sibling = _flip(me, 1)

            def rows(q, c):
                return pl.ds(pl.multiple_of(W_IN_SHARD * (2 * q + c), 8), W_IN_SHARD)

            def to_sibling(q):
                return pltpu.make_async_remote_copy(
                    src_ref=stage.at[q % 2], dst_ref=land.at[q], send_sem=pair_send.at[q], recv_sem=pair_recv.at[q],
                    device_id=sibling, device_id_type=MESH,
                )

            for q in range(n_chips):
                if q >= 2:
                    to_sibling(q - 2).wait_send()
                stage[q % 2] = acc[rows(q, sibling[2]), :].astype(BF16)
                to_sibling(q).start()
            for q in range(n_chips - 2, n_chips):
                to_sibling(q).wait_send()
            sums = [pltpu.make_async_copy(stage.at[q % 2], sum_hbm.at[q], own_sems.at[q]) for q in range(n_chips)]
            for q in range(n_chips):
                to_sibling(q).wait_recv()
                if q >= 2:
                    sums[q - 2].wait()
                total = acc[rows(q, me[2]), :] + land[q].astype(F32)
                acc[rows(q, me[2]), :] = total
                stage[q % 2] = total.astype(BF16)
                sums[q].start()
            mine = pltpu.make_async_copy(acc.at[rows(_chip_of(me), me[2]), :], own_hbm, own_sems.at[n_chips])
            mine.start()
            for q in range(n_chips - 2, n_chips):
                sums[q].wait()
            mine.wait()
            for cp in gate_copies(g_hbm, lg_hbm, send_sems, recv_sems, arriving=True):
                cp.wait_recv()
            for cp in gate_copies(g_hbm, lg_hbm, send_sems, recv_sems, arriving=False):
                cp.wait_send()
            own_gates.wait()

    any_spec = pl.BlockSpec(memory_space=pl.ANY)
    n_copies = N_DEV - 1
    return pl.pallas_call(
        body,
        name="bwd_gw",
        grid=(n_tiles,),
        in_specs=_seg_specs(tm) + [pl.BlockSpec((tm, D), lambda i: (i, 0)), any_spec],
        out_specs=[any_spec, any_spec, any_spec],
        out_shape=[jax.ShapeDtypeStruct((W_IN_SHARD, D), F32),
                   jax.ShapeDtypeStruct((n_chips, W_IN_SHARD, D), BF16),
                   jax.ShapeDtypeStruct((N_DEV, 2, 4, GATE_SHARD, 256), F32)],
        scratch_shapes=[pltpu.VMEM((IN_W, D), F32), pltpu.VMEM((2, W_IN_SHARD, D), BF16), pltpu.VMEM((n_chips, W_IN_SHARD, D), BF16),
                        pltpu.SemaphoreType.DMA((n_chips + 1,)), pltpu.SemaphoreType.DMA((n_chips,)), pltpu.SemaphoreType.DMA((n_chips,)),
                        pltpu.SemaphoreType.DMA((n_copies,)), pltpu.SemaphoreType.DMA((n_copies,)), pltpu.SemaphoreType.DMA],
        compiler_params=_params(1),
    )(*dsegs, h, g_gates)


def _adamw(w, g, m, v):
    m = ADAM_B1 * m + (1.0 - ADAM_B1) * g
    v = ADAM_B2 * v + (1.0 - ADAM_B2) * (g * g)
    m_hat = m / (1.0 - ADAM_B1**ADAM_STEP)
    v_hat = v / (1.0 - ADAM_B2**ADAM_STEP)
    delta = -ADAM_LR * (m_hat / (jnp.sqrt(v_hat) + ADAM_EPS) + ADAM_WD * w)
    return delta, m, v


def _w_in_update(own, land, w_t, m_t, v_t):
    n_slots, rows, cols = land.shape
    rb = 208
    spec = pl.BlockSpec((rb, cols), lambda i: (i, 0))

    def body(o_ref, l_ref, w_ref, m_ref, v_ref, g_ref, d_ref, m2_ref, v2_ref):
        total = o_ref[...]
        for s in range(n_slots):
            total = total + l_ref[s].astype(F32)
        g_ref[...] = total
        d_ref[...], m2_ref[...], v2_ref[...] = _adamw(w_ref[...], total, m_ref[...], v_ref[...])

    shape = jax.ShapeDtypeStruct((rows, cols), F32)
    return pl.pallas_call(
        body,
        name="w_in_update",
        grid=(rows // rb,),
        in_specs=[spec, pl.BlockSpec((n_slots, rb, cols), lambda i: (0, i, 0)), spec, spec, spec],
        out_specs=[spec] * 4,
        out_shape=[shape] * 4,
        compiler_params=_params(1),
    )(own, land, w_t, m_t, v_t)


def _shard_update(own_3, land_3, land_g, weights, m_in, v_in):
    n = len(weights)

    def body(*refs):
        o3_ref, l3_ref, lg_ref = refs[:3]
        w_refs, m_refs, v_refs = (refs[3 + n * k : 3 + n * (k + 1)] for k in range(3))
        outs = refs[3 + 3 * n :]
        for k in range(n):
            if k < 3:
                g = o3_ref[k]
                for slot in range(3):
                    g = g + l3_ref[slot, k].astype(F32)
            else:
                g = lg_ref[0, k - 3]
                for slot in range(1, N_DEV):
                    g = g + lg_ref[slot, k - 3]
            d, m2, v2 = _adamw(w_refs[k][0], g, m_refs[k][0], v_refs[k][0])
            for which, val in enumerate((g, d, m2, v2)):
                outs[which * n + k][0] = val

    shapes = [jax.ShapeDtypeStruct(w.shape, F32) for w in weights]
    res = pl.pallas_call(
        body, name="shard_update", out_shape=shapes * 4, compiler_params=pltpu.CompilerParams(vmem_limit_bytes=VMEM_LIMIT)
    )(own_3, land_3, land_g, *weights, *m_in, *v_in)
    return [res[n * k : n * (k + 1)] for k in range(4)]


VEC_LAYOUT = (("b_ada", 3 * D), ("norm_g", D), ("attn_sinks", 128), ("conv_b", D), ("rg_ba", D), ("rg_bx", D), ("rg_lambda", D),
              ("final_g", D), ("conv_w", CONV_WIDTH * D), ("loss", 128))
VEC_OFFSET = {name: sum(w for _, w in VEC_LAYOUT[:k]) for k, (name, _) in enumerate(VEC_LAYOUT)}
VEC_ROWS = 112
VEC_WIDTH = VEC_ROWS * 128
assert sum(w for _, w in VEC_LAYOUT) <= VEC_WIDTH
N_REPLICATED = 8


def _small_update(vec_all, c_all_t, me, rep_w, rep_m, rep_v, ada, conv):
    n_ada = ada[0].shape[1]
    n_conv = conv[0].shape[1]

    def body(*refs):
        me_ref, vec_ref, ct_ref = refs[:3]
        w_refs, m_refs, v_refs = (refs[3 + N_REPLICATED * k : 3 + N_REPLICATED * (k + 1)] for k in range(3))
        pos = 3 + 3 * N_REPLICATED
        ada_refs, conv_refs = refs[pos : pos + 3], refs[pos + 3 : pos + 6]
        outs = refs[pos + 6 :]
        loss_ref, rep_outs = outs[0], outs[1 : 1 + 4 * N_REPLICATED]
        ada_outs, conv_outs = outs[1 + 4 * N_REPLICATED : 5 + 4 * N_REPLICATED], outs[5 + 4 * N_REPLICATED : 9 + 4 * N_REPLICATED]
        conv_buf = outs[9 + 4 * N_REPLICATED]

        total = vec_ref[0:1, :]
        for slot in range(1, N_DEV):
            total = total + vec_ref[slot : slot + 1, :]
        loss_ref[...] = total[:, VEC_OFFSET["loss"] : VEC_OFFSET["loss"] + 1]
        for k, (name, _) in enumerate(VEC_LAYOUT[:N_REPLICATED]):
            width = w_refs[k].shape[1]
            g = total[:, VEC_OFFSET[name] : VEC_OFFSET[name] + width]
            d, m2, v2 = _adamw(w_refs[k][...], g, m_refs[k][...], v_refs[k][...])
            for which, val in enumerate((g, d, m2, v2)):
                rep_outs[which * N_REPLICATED + k][...] = val

        device = me_ref[0]
        for k in range(CONV_WIDTH):
            conv_buf[k : k + 1, :] = total[:, VEC_OFFSET["conv_w"] + D * k : VEC_OFFSET["conv_w"] + D * (k + 1)]
        g_conv = conv_buf[:, pl.ds(pl.multiple_of(device * n_conv, 128), n_conv)]
        for ref, val in zip(conv_outs, (g_conv,) + _adamw(conv_refs[0][...], g_conv, conv_refs[1][...], conv_refs[2][...])):
            ref[...] = val

        dmod = vec_ref[:, pl.ds(pl.multiple_of(device * n_ada, 128), n_ada)]
        g_ada = _dot(ct_ref[...], dmod)
        for ref, val in zip(ada_outs, (g_ada,) + _adamw(ada_refs[0][...], g_ada, ada_refs[1][...], ada_refs[2][...])):
            ref[...] = val

    vmem = pl.BlockSpec(memory_space=pltpu.VMEM)
    rep_shapes = [jax.ShapeDtypeStruct(w.shape, F32) for w in rep_w]
    out_shape = ([jax.ShapeDtypeStruct((1, 1), F32)] + rep_shapes * 4 + [jax.ShapeDtypeStruct(ada[0].shape, F32)] * 4
                 + [jax.ShapeDtypeStruct(conv[0].shape, F32)] * 4)
    n_in = 3 + 3 * N_REPLICATED + 6
    res = pl.pallas_call(
        body,
        name="small_update",
        in_specs=[pl.BlockSpec(memory_space=pltpu.SMEM)] + [vmem] * (n_in - 1),
        out_specs=[vmem] * len(out_shape),
        out_shape=out_shape,
        scratch_shapes=[pltpu.VMEM((CONV_WIDTH, D), F32)],
        compiler_params=pltpu.CompilerParams(vmem_limit_bytes=VMEM_LIMIT),
    )(me, vec_all, c_all_t, *rep_w, *rep_m, *rep_v, *ada, *conv)
    loss = res[0]
    reps = [res[1 + N_REPLICATED * k : 1 + N_REPLICATED * (k + 1)] for k in range(4)]
    return loss, reps, res[1 + 4 * N_REPLICATED : 5 + 4 * N_REPLICATED], res[5 + 4 * N_REPLICATED : 9 + 4 * N_REPLICATED]


def kernel(x, c, positions, w_ada, b_ada, norm_g, w_in, attn_sinks, conv_w, conv_b, rg_wa, rg_ba, rg_wx, rg_bx, rg_lambda, w_attn_proj, w_rnn_proj, w_out, final_g, loss_target, m_w_ada, m_b_ada, m_norm_g, m_w_in, m_attn_sinks, m_conv_w, m_conv_b, m_rg_wa, m_rg_ba, m_rg_wx, m_rg_bx, m_rg_lambda, m_w_attn_proj, m_w_rnn_proj, m_w_out, m_final_g, v_w_ada, v_b_ada, v_norm_g, v_w_in, v_attn_sinks, v_conv_w, v_conv_b, v_rg_wa, v_rg_ba, v_rg_wx, v_rg_bx, v_rg_lambda, v_w_attn_proj, v_w_rnn_proj, v_w_out, v_final_g):
    my_x, my_y, my_c = _position()
    me = _slot((my_x, my_y, my_c))
    x2, target = x[0], loss_target[0]
    pos = positions[0]

    gate_rows = lambda w: w[0].transpose(1, 0, 2).reshape(GATE_SHARD, D)
    w_pack = jnp.concatenate([w_attn_proj[0], w_rnn_proj[0], w_out[0], gate_rows(rg_wa), gate_rows(rg_wx)], axis=0).astype(BF16)
    conv_rows = jnp.pad(conv_w[0], ((0, 8 - CONV_WIDTH), (0, 0)))
    w_in_t = w_in[0].T
    n_ada = w_ada.shape[2]
    b_cols = lax.dynamic_slice(b_ada, (0, me * n_ada), (1, n_ada))

    tables = _rope_tables(pos)
    where = jnp.stack([_chip_of((my_x, my_y)), me]).astype(jnp.int32)
    proj, h, mod, lane_tables, w_t_all, w_all, c_rows, conv_all = _fwd_in(
        x2, jnp.broadcast_to(c, (8, D)), conv_rows, w_ada[0], b_cols, norm_g, w_in_t.astype(BF16), tables, w_pack, where
    )
    c_all = c_rows[:, 0, :]
    conv_full = conv_all[:, :CONV_WIDTH, :].transpose(1, 0, 2).reshape(CONV_WIDTH, D)
    gate = mod[:, 2 * D :]
    o = _attn_fwd(proj, attn_sinks)
    keep = (pos != 0).astype(F32)[:, None]
    hh = _rnn_fwd(proj, keep, w_all, rg_ba, rg_bx, rg_lambda, conv_full, conv_b)

    dres, d_o, d_hh, d_ga, d_gr, d_ma, d_mr, gw3, mid_vec = _mid(x2, target, o, hh, proj, w_all, gate, final_g[None, :])
    part_3 = gw3.reshape(3, 4, 2, ROW_SHARD, D)
    d_q, d_k, d_v, d_sink, sum_3_bf, own_3 = _attn_bwd(proj, o, d_o, (lane_tables, tables), attn_sinks, part_3)
    d_xr, g_gates, rnn_vec, land_3 = _rnn_bwd(proj, hh, d_hh, keep, w_all, rg_ba, rg_bx, rg_lambda, conv_full, conv_b, sum_3_bf)
    dsegs = (d_q, d_k, d_v, d_ga, d_xr, d_gr, d_ma, d_mr)
    own_w, sum_w_bf, land_g = _bwd_gw(dsegs, h, g_gates.reshape(2, 4, N_DEV, GATE_SHARD, 256))
    w_sems, sum_w_thru, land_w_thru, token = _w_exchange_start(sum_w_bf)
    grad_x, dh_vec = _bwd_dh(dsegs, w_t_all, x2, dres, mod + token[0:1, 0:1], norm_g)
    land_w = _w_exchange_wait(w_sems, sum_w_thru, land_w_thru, dh_vec)

    pieces = dict(
        b_ada=jnp.concatenate([dh_vec[0:1], dh_vec[1:2], mid_vec[1:2]], axis=1),
        norm_g=dh_vec[2:3],
        attn_sinks=jnp.pad(d_sink[:, :, 0].reshape(1, 16), ((0, 0), (0, 112))),
        conv_b=rnn_vec[3:4],
        rg_ba=rnn_vec[0:1],
        rg_bx=rnn_vec[1:2],
        rg_lambda=rnn_vec[2:3],
        final_g=mid_vec[0:1],
        conv_w=rnn_vec[4:8].reshape(1, CONV_WIDTH * D),
        loss=mid_vec[2:3, 0:128],
    )
    vec = jnp.concatenate([pieces[name] for name, _ in VEC_LAYOUT], axis=1)
    vec = jnp.pad(vec, ((0, 0), (0, VEC_WIDTH - vec.shape[1]))).reshape(VEC_ROWS, 128)
    (vec_all,) = _all_gather_direct("ag_vec", [vec])
    rep_w = [b_ada, norm_g, attn_sinks, conv_b, rg_ba, rg_bx, rg_lambda, final_g[None, :]]
    rep_m = [m_b_ada, m_norm_g, m_attn_sinks, m_conv_b, m_rg_ba, m_rg_bx, m_rg_lambda, m_final_g[None, :]]
    rep_v = [v_b_ada, v_norm_g, v_attn_sinks, v_conv_b, v_rg_ba, v_rg_bx, v_rg_lambda, v_final_g[None, :]]
    loss, reps, ada_res, conv_res = _small_update(
        vec_all.reshape(N_DEV, VEC_WIDTH), c_all.T, me.astype(jnp.int32).reshape(1), rep_w, rep_m, rep_v,
        (w_ada[0], m_w_ada[0], v_w_ada[0]), (conv_w[0], m_conv_w[0], v_conv_w[0]),
    )

    w_in_res = _w_in_update(own_w, land_w, w_in_t, m_w_in[0].T, v_w_in[0].T)
    shard_res = _shard_update(
        own_3, land_3, land_g,
        (w_attn_proj, w_rnn_proj, w_out, rg_wa, rg_wx),
        (m_w_attn_proj, m_w_rnn_proj, m_w_out, m_rg_wa, m_rg_wx),
        (v_w_attn_proj, v_w_rnn_proj, v_w_out, v_rg_wa, v_rg_wx),
    )

    def by_weight(k):
        b, n, s, cb, ba, bx, lam, fg = reps[k]
        ap, rp, out, wa, wx = shard_res[k]
        return [ada_res[k][None], b, n, w_in_res[k].T[None], s, conv_res[k][None], cb, wa, ba, wx, bx, lam, ap, rp, out, fg[0]]

    return (loss[0, 0], grad_x[None], *by_weight(0), *by_weight(1), *by_weight(2), *by_weight(3))
```

```python
import math

import jax
import jax.numpy as jnp
from jax import lax
from jax.experimental import pallas as pl
from jax.experimental.pallas import tpu as pltpu

F32 = jnp.float32
BF16 = jnp.bfloat16

D = 1024
HEAD_DIM = 64
ROT_DIM = 16
ROPE_THETA = 500000.0
WINDOW = 128
LRU_C = 8.0
CONV_WIDTH = 4
NORM_EPS = 1e-6
IN_W = 6656
SEG_Q, SEG_K, SEG_V, SEG_GA, SEG_XR, SEG_GR, SEG_MA, SEG_MR = 0, 1024, 1280, 1536, 2560, 3584, 4608, 5632
N_DEV = 8
W_IN_SHARD = IN_W // N_DEV
ROW_SHARD = D // N_DEV
GATE_SHARD = 256 // N_DEV

ADAM_LR = 0.001
ADAM_B1 = 0.9
ADAM_B2 = 0.999
ADAM_EPS = 1e-08
ADAM_WD = 0.01
ADAM_STEP = 10

VMEM_LIMIT = 60 * 1024 * 1024
MESH = pl.DeviceIdType.MESH


def _dot(a, b):
    return jnp.dot(a, b, preferred_element_type=F32)


def _dot_nt(a, b):
    return lax.dot_general(a, b, (((1,), (1,)), ((), ())), preferred_element_type=F32)


def _dot_tn(a, b):
    return lax.dot_general(a, b, (((0,), (0,)), ((), ())), preferred_element_type=F32)


def _sigmoid(z):
    return 1.0 / (1.0 + jnp.exp(-z))


def _params(n_grid, vmem=VMEM_LIMIT):
    return pltpu.CompilerParams(dimension_semantics=("arbitrary",) * n_grid, vmem_limit_bytes=vmem)


def _position():
    return lax.axis_index("x"), lax.axis_index("y"), lax.axis_index("c")


def _flip(pos, k):
    x, y, c = pos
    return (1 - x if k & 4 else x, 1 - y if k & 2 else y, 1 - c if k & 1 else c)


def _slot(pos):
    return 4 * pos[0] + 2 * pos[1] + pos[2]


def _all_gather_direct(name, arrs):
    n = len(arrs)

    def body(*refs):
        ins, outs = refs[:n], refs[n : 2 * n]
        send_sems, recv_sems, local_sems = refs[2 * n :]
        me = _position()
        mine = [pltpu.make_async_copy(ins[a], outs[a].at[_slot(me)], local_sems.at[a]) for a in range(n)]
        for cp in mine:
            cp.start()

        def copy(a, k, block):
            return pltpu.make_async_remote_copy(
                src_ref=ins[a],
                dst_ref=outs[a].at[_slot(block)],
                send_sem=send_sems.at[a * 7 + k - 1],
                recv_sem=recv_sems.at[a * 7 + k - 1],
                device_id=_flip(me, k),
                device_id_type=MESH,
            )

        sent = [copy(a, k, me) for a in range(n) for k in range(1, N_DEV)]
        for cp in sent:
            cp.start()
        for a in range(n):
            for k in range(1, N_DEV):
                copy(a, k, _flip(me, k)).wait_recv()
        for cp in sent:
            cp.wait_send()
        for cp in mine:
            cp.wait()

    any_spec = pl.BlockSpec(memory_space=pl.ANY)
    return pl.pallas_call(
        body,
        name=name,
        out_shape=[jax.ShapeDtypeStruct((N_DEV,) + a.shape, a.dtype) for a in arrs],
        in_specs=[any_spec] * n,
        out_specs=[any_spec] * n,
        scratch_shapes=[
            pltpu.SemaphoreType.DMA((7 * n,)),
            pltpu.SemaphoreType.DMA((7 * n,)),
            pltpu.SemaphoreType.DMA((n,)),
        ],
    )(*arrs)


def _direct_gather_copies(src_hbm, all_hbm, send_sems, recv_sems, arriving):
    me = _position()
    return [
        pltpu.make_async_remote_copy(
            src_ref=src_hbm,
            dst_ref=all_hbm.at[_slot(_flip(me, k) if arriving else me)],
            send_sem=send_sems.at[k - 1],
            recv_sem=recv_sems.at[k - 1],
            device_id=_flip(me, k),
            device_id_type=MESH,
        )
        for k in range(1, N_DEV)
    ]


CHIP_FLIPS = (2, 4, 6)


def _chip_of(pos):
    return 2 * pos[0] + pos[1]


def _chip_copies(srcs, lands, views, send_sems, recv_sems):
    me = _position()
    copies = []
    for a, (src, land) in enumerate(zip(srcs, lands)):
        for j, k in enumerate(CHIP_FLIPS):
            peer = _flip(me, k)
            copies.append(
                pltpu.make_async_remote_copy(
                    src_ref=views[a](src, peer),
                    dst_ref=land.at[j],
                    send_sem=send_sems.at[len(CHIP_FLIPS) * a + j],
                    recv_sem=recv_sems.at[len(CHIP_FLIPS) * a + j],
                    device_id=peer,
                    device_id_type=MESH,
                )
            )
    return copies


PACK_GATE = 3 * ROW_SHARD
PACK_ROWS = PACK_GATE + 2 * GATE_SHARD


def _fwd_in(x, c_rows, conv_rows, w_ada, b_cols, norm_g, w_shard, tables, pack, where):
    t = x.shape[0]
    tm = min(1024, t)
    tn = IN_W // 4
    n_i, n_j = t // tm, IN_W // tn
    n_rope = (SEG_V - SEG_Q) // 128

    n_ada = w_ada.shape[1]

    def body(where_ref, x_ref, wada_ref, bcols_ref, ng_ref, ct_ref, st_ref, shard_hbm, pack_hbm, c_hbm, conv_hbm,
             proj_ref, h_ref, mod_ref, lane_hbm, wt_hbm, all_hbm, call_hbm, convall_hbm,
             wall, hbuf, cbuf, mcols, mod_all, mod_buf, lane,
             w_send, w_recv, load_sems, local_sem, lane_sem, p_send, p_recv, p_local,
             c_send, c_recv, c_local, v_send, v_recv, v_local, m_send, m_recv):
        j = pl.program_id(0)
        i = pl.program_id(1)
        me = _position()
        sibling = _flip(me, 1)
        chip = where_ref[0]
        slot = where_ref[1]

        def modulation():
            own_c = pltpu.make_async_copy(c_hbm, call_hbm.at[_slot(me)], c_local)
            own_v = pltpu.make_async_copy(conv_hbm, convall_hbm.at[_slot(me)], v_local)
            own_c.start()
            own_v.start()
            for cp in _direct_gather_copies(c_hbm, call_hbm, c_send, c_recv, arriving=False):
                cp.start()
            for cp in _direct_gather_copies(conv_hbm, convall_hbm, v_send, v_recv, arriving=False):
                cp.start()
            for cp in _direct_gather_copies(c_hbm, call_hbm, c_send, c_recv, arriving=True):
                cp.wait_recv()
            own_c.wait()
            fetch = pltpu.make_async_copy(call_hbm, cbuf, c_local)
            fetch.start()
            fetch.wait()
            mcols[...] = _dot(cbuf[:, 0, :].astype(BF16), wada_ref[...].astype(BF16)) + bcols_ref[...]
            mod_all[slot] = mcols[...]
            swaps = lambda arriving: [
                pltpu.make_async_remote_copy(
                    src_ref=mcols,
                    dst_ref=mod_all.at[_slot(_flip(me, k) if arriving else me)],
                    send_sem=m_send.at[k - 1],
                    recv_sem=m_recv.at[k - 1],
                    device_id=_flip(me, k),
                    device_id_type=MESH,
                )
                for k in range(1, N_DEV)
            ]
            for cp in swaps(False):
                cp.start()
            for cp in swaps(True):
                cp.wait_recv()
            mine = lax.broadcasted_iota(jnp.int32, (N_DEV, n_ada), 0) == slot
            for s in range(N_DEV):
                row = jnp.sum(jnp.where(mine, mod_all[s], 0.0), axis=0, keepdims=True)
                mod_buf[:, n_ada * s : n_ada * (s + 1)] = row
            for cp in swaps(False):
                cp.wait_send()
            for cp in _direct_gather_copies(c_hbm, call_hbm, c_send, c_recv, arriving=False):
                cp.wait_send()
            for cp in _direct_gather_copies(conv_hbm, convall_hbm, v_send, v_recv, arriving=True):
                cp.wait_recv()
            for cp in _direct_gather_copies(conv_hbm, convall_hbm, v_send, v_recv, arriving=False):
                cp.wait_send()
            own_v.wait()

        def shard_of(block):
            return wall.at[_chip_of(block), pl.ds(W_IN_SHARD * block[2], W_IN_SHARD), :]

        def shard_copy(k, block, to, src=None):
            return pltpu.make_async_remote_copy(
                src_ref=shard_of(block),
                dst_ref=shard_of(block),
                send_sem=w_send.at[k],
                recv_sem=w_recv.at[k],
                device_id=to,
                device_id_type=MESH,
            )

        own_shard = pltpu.make_async_copy(shard_hbm, shard_of(me), local_sem)
        own_pack = pltpu.make_async_copy(pack_hbm, all_hbm.at[_slot(me)], p_local)
        keep_lane = pltpu.make_async_copy(lane, lane_hbm, lane_sem)

        def lane_tables():
            d = lax.broadcasted_iota(jnp.int32, (128, 128), 0) % HEAD_DIM
            half = ROT_DIM // 2

            def fill(b, carry):
                cols = pl.ds(pl.multiple_of(128 * b, 128), 128)
                cos = jnp.concatenate([ct_ref[:, cols]] * (128 // half), axis=0)
                sin = jnp.concatenate([st_ref[:, cols]] * (128 // half), axis=0)
                lane[0, cols, :] = jnp.where(d < ROT_DIM, cos, 1.0).T
                lane[1, cols, :] = jnp.where(d < half, -sin, 0.0).T
                lane[2, cols, :] = jnp.where((d >= half) & (d < ROT_DIM), sin, 0.0).T
                return carry

            lax.fori_loop(0, t // 128, fill, 0)

        def keep_tile(q):
            return [
                pltpu.make_async_copy(wall.at[q, pl.ds(W_IN_SHARD * c, W_IN_SHARD), :], wt_hbm.at[2 * q + c], load_sems.at[2 * (q ^ chip) + c])
                for c in range(2)
            ]

        @pl.when((j == 0) & (i == 0))
        def _():
            modulation()
            mod_ref[...] = mod_buf[...]
            own_shard.start()
            own_shard.wait()
            shard_copy(0, me, sibling).start()
            for rel in range(1, 4):
                shard_copy(rel, me, _flip(me, 2 * rel)).start()
            own_pack.start()
            for cp in _direct_gather_copies(pack_hbm, all_hbm, p_send, p_recv, arriving=False):
                cp.start()
            lane_tables()
            keep_lane.start()
            shard_copy(0, sibling, me).wait_recv()
            for cp in keep_tile(chip):
                cp.start()

        for rel in range(1, 4):

            @pl.when((j == rel) & (i == 0))
            def _(rel=rel):
                there = _flip(me, 2 * rel)
                shard_copy(rel, there, me).wait_recv()
                shard_copy(3 + rel, there, sibling).start()
                shard_copy(3 + rel, _flip(sibling, 2 * rel), me).wait_recv()
                for cp in keep_tile(chip ^ rel):
                    cp.start()

        rows = pl.ds(pl.multiple_of(i * tm, tm), tm)

        @pl.when(j == 0)
        def _():
            xv = x_ref[...]
            r = lax.rsqrt(jnp.mean(xv * xv, axis=-1, keepdims=True) + NORM_EPS)
            shift = mod_buf[:, 0:D]
            scale = mod_buf[:, D : 2 * D]
            hv = (((xv * r) * ng_ref[...]) * (1.0 + scale) + shift).astype(BF16)
            hbuf[rows, :] = hv
            h_ref[...] = hv

        proj_ref[...] = _dot_nt(hbuf[rows, :], wall[chip ^ j])

        @pl.when((chip ^ j) == 0)
        def _():
            cos, s1, s2 = lane[0, rows, :], lane[1, rows, :], lane[2, rows, :]
            for g in range(n_rope):
                cols = slice(128 * g, 128 * (g + 1))
                proj_ref[:, cols] = _rope(proj_ref[:, cols], cos, s1, s2)

        @pl.when((j == n_j - 1) & (i == n_i - 1))
        def _():
            shard_copy(0, me, sibling).wait_send()
            for rel in range(1, 4):
                shard_copy(rel, me, _flip(me, 2 * rel)).wait_send()
                shard_copy(3 + rel, _flip(me, 2 * rel), sibling).wait_send()
            for q in range(N_DEV // 2):
                for cp in keep_tile(q):
                    cp.wait()
            for cp in _direct_gather_copies(pack_hbm, all_hbm, p_send, p_recv, arriving=True):
                cp.wait_recv()
            for cp in _direct_gather_copies(pack_hbm, all_hbm, p_send, p_recv, arriving=False):
                cp.wait_send()
            own_pack.wait()
            keep_lane.wait()

    tab = pl.BlockSpec((ROT_DIM // 2, t), lambda j, i, c: (0, 0))
    any_spec = pl.BlockSpec(memory_space=pl.ANY)
    return pl.pallas_call(
        body,
        name="fwd_in",
        grid_spec=pltpu.PrefetchScalarGridSpec(
            num_scalar_prefetch=1,
            grid=(n_j, n_i),
            in_specs=[
                pl.BlockSpec((tm, D), lambda j, i, c: (jnp.where(j == 0, i, 0), 0)),
                pl.BlockSpec(w_ada.shape, lambda j, i, c: (0, 0)),
                pl.BlockSpec((1, n_ada), lambda j, i, c: (0, 0)),
                pl.BlockSpec((1, D), lambda j, i, c: (0, 0)),
                tab, tab, any_spec, any_spec, any_spec, any_spec,
            ],
            out_specs=[pl.BlockSpec((tm, tn), lambda j, i, c: (i, c[0] ^ j)),
                       pl.BlockSpec((tm, D), lambda j, i, c: (jnp.where(j == 0, i, n_i - 1), 0)),
                       pl.BlockSpec((1, 3 * D), lambda j, i, c: (0, 0)),
                       any_spec, any_spec, any_spec, any_spec, any_spec],
            scratch_shapes=[
                pltpu.VMEM((N_DEV // 2, tn, D), BF16),
                pltpu.VMEM((t, D), BF16),
                pltpu.VMEM((N_DEV,) + c_rows.shape, F32),
                pltpu.VMEM((N_DEV, n_ada), F32),
                pltpu.VMEM((N_DEV, N_DEV, n_ada), F32),
                pltpu.VMEM((1, 3 * D), F32),
                pltpu.VMEM((3, t, 128), F32),
                pltpu.SemaphoreType.DMA((N_DEV - 1,)),
                pltpu.SemaphoreType.DMA((N_DEV - 1,)),
                pltpu.SemaphoreType.DMA((N_DEV,)),
                pltpu.SemaphoreType.DMA,
                pltpu.SemaphoreType.DMA,
                pltpu.SemaphoreType.DMA((N_DEV - 1,)),
                pltpu.SemaphoreType.DMA((N_DEV - 1,)),
                pltpu.SemaphoreType.DMA,
                pltpu.SemaphoreType.DMA((N_DEV - 1,)),
                pltpu.SemaphoreType.DMA((N_DEV - 1,)),
                pltpu.SemaphoreType.DMA,
                pltpu.SemaphoreType.DMA((N_DEV - 1,)),
                pltpu.SemaphoreType.DMA((N_DEV - 1,)),
                pltpu.SemaphoreType.DMA,
                pltpu.SemaphoreType.DMA((N_DEV - 1,)),
                pltpu.SemaphoreType.DMA((N_DEV - 1,)),
            ],
        ),
        out_shape=[jax.ShapeDtypeStruct((t, IN_W), F32), jax.ShapeDtypeStruct((t, D), BF16),
                   jax.ShapeDtypeStruct((1, 3 * D), F32),
                   jax.ShapeDtypeStruct((3, t, 128), F32),
                   jax.ShapeDtypeStruct((N_DEV,) + w_shard.shape, w_shard.dtype),
                   jax.ShapeDtypeStruct((N_DEV,) + pack.shape, pack.dtype),
                   jax.ShapeDtypeStruct((N_DEV,) + c_rows.shape, F32),
                   jax.ShapeDtypeStruct((N_DEV,) + conv_rows.shape, F32)],
        compiler_params=_params(2),
    )(where, x, w_ada, b_cols, norm_g, *tables, w_shard, pack, c_rows, conv_rows)


def _rope_tables(positions):
    inv_freq = ROPE_THETA ** (-jnp.arange(0, ROT_DIM, 2, dtype=F32) / ROT_DIM)
    ang_t = inv_freq[:, None] * positions.astype(F32)[None, :]
    return jnp.cos(ang_t), jnp.sin(ang_t)


def _rope(tv, cos, s1, s2):
    cols = []
    for g in range(tv.shape[1] // 128):
        blk = tv[:, 128 * g : 128 * (g + 1)]
        cols.append(blk * cos + pltpu.roll(blk, 128 - 8, 1) * s1 + pltpu.roll(blk, 8, 1) * s2)
    return cols[0] if len(cols) == 1 else jnp.concatenate(cols, axis=1)


def _rope_transposed(gv, cos, s1, s2):
    cols = []
    for g in range(gv.shape[1] // 128):
        blk = gv[:, 128 * g : 128 * (g + 1)]
        cols.append(blk * cos + pltpu.roll(blk * s1, 8, 1) + pltpu.roll(blk * s2, 128 - 8, 1))
    return cols[0] if len(cols) == 1 else jnp.concatenate(cols, axis=1)


def _lane_lo():
    return lax.broadcasted_iota(jnp.int32, (1, 128), 1) < HEAD_DIM


def _row_lo():
    return lax.broadcasted_iota(jnp.int32, (128, 1), 0) < HEAD_DIM


def _split_heads(tile, j):
    lo = _lane_lo()
    parts = []
    for g in (2 * j, 2 * j + 1):
        blk = tile[:, 128 * g : 128 * (g + 1)]
        parts += [jnp.where(lo, blk, 0.0), jnp.where(lo, 0.0, blk)]
    return jnp.concatenate(parts, axis=0)


def _dup_lanes(kv, j):
    rolled = pltpu.roll(kv, HEAD_DIM, 1)
    lo = _lane_lo()
    return jnp.where(lo, kv, rolled) if j == 0 else jnp.where(lo, rolled, kv)


def _fold_lanes(acc):
    return acc + pltpu.roll(acc, HEAD_DIM, 1)


def _fill_bias_t(bias_ref):
    kj = lax.broadcasted_iota(jnp.int32, (256, 128), 0)
    qi = lax.broadcasted_iota(jnp.int32, (256, 128), 1)
    diff = qi + WINDOW - kj
    band = (diff >= 0) & (diff < WINDOW)
    bias_ref[0] = jnp.where(band, 0.0, -1e30)
    bias_ref[1] = jnp.where(band & (kj >= 128), 0.0, -1e30)


def _sink_row(sink_ref, pair, j):
    return jnp.concatenate([jnp.full((1, 128), sink_ref[0, 8 * pair + 4 * j + h], F32) for h in range(4)], axis=1)


def _softmax_t(q4, kw, bias4, sink):
    s = _dot_nt(kw, q4) + bias4
    m = jnp.maximum(jnp.max(s, axis=0, keepdims=True), sink)
    p = jnp.exp(s - m)
    e_sink = jnp.exp(sink - m)
    denom = jnp.sum(p, axis=0, keepdims=True) + e_sink
    return p, e_sink, denom


def _attn_specs(t, tq):
    return dict(
        q=pl.BlockSpec((tq, 512), lambda p, i: (i, p)),
        k=pl.BlockSpec((t, 128), lambda p, i: (0, SEG_K // 128 + p)),
        v=pl.BlockSpec((t, 128), lambda p, i: (0, SEG_V // 128 + p)),
        sink=pl.BlockSpec(memory_space=pltpu.SMEM),
        act=pl.BlockSpec((tq, 512), lambda p, i: (i, p)),
    )


def _attn_fwd(proj, sinks):
    t = proj.shape[0]
    tq = min(512, t)
    nb = tq // 128
    n_tiles = t // tq
    sp = _attn_specs(t, tq)

    def body(q_ref, k_ref, v_ref, sink_ref, o_ref, kd, vdt, bias_ref):
        pair = pl.program_id(0)
        i = pl.program_id(1)

        @pl.when(i == 0)
        def _():
            kv, vv = k_ref[...], v_ref[...]
            for j in range(2):
                kd[j, 0:128, :] = jnp.zeros((128, 128), BF16)
                kd[j, 128:, :] = _dup_lanes(kv, j).astype(BF16)
                vdt[j, :, 0:128] = jnp.zeros((128, 128), BF16)
                vdt[j, :, 128:] = _dup_lanes(vv, j).T.astype(BF16)
            _fill_bias_t(bias_ref)

        row_lo = _row_lo()

        def block(b, carry):
            r0 = pl.multiple_of(b * 128, 128)
            gb = i * nb + b
            keys = pl.ds(pl.multiple_of(gb * 128, 128), 256)
            rows = pl.ds(r0, 128)
            qt = q_ref[rows, :] * (1.0 / math.sqrt(HEAD_DIM))
            bias = bias_ref[jnp.where(gb == 0, 1, 0)]
            bias4 = jnp.concatenate([bias] * 4, axis=1)
            cols = []
            for j in range(2):
                q4 = _split_heads(qt, j).astype(BF16)
                p, _, denom = _softmax_t(q4, kd[j, keys, :], bias4, _sink_row(sink_ref, pair, j))
                ot = _dot(vdt[j, :, keys], p.astype(BF16)) * (1.0 / denom)
                for gi in range(2):
                    z = jnp.where(row_lo, ot[:, 256 * gi : 256 * gi + 128], ot[:, 256 * gi + 128 : 256 * gi + 256])
                    cols.append(z.T)
            o_ref[rows, :] = jnp.concatenate(cols, axis=1).astype(BF16)
            return carry

        lax.fori_loop(0, nb, block, 0, unroll=True)

    return pl.pallas_call(
        body,
        name="attn_fwd",
        grid=(2, n_tiles),
        in_specs=[sp["q"], sp["k"], sp["v"], sp["sink"]],
        out_specs=sp["act"],
        out_shape=jax.ShapeDtypeStruct((t, D), BF16),
        scratch_shapes=[pltpu.VMEM((2, t + 128, 128), BF16), pltpu.VMEM((2, 128, t + 128), BF16), pltpu.VMEM((2, 256, 128), F32)],
        compiler_params=_params(2),
    )(proj, proj, proj, sinks)


def _attn_bwd(proj, o, d_o, tables, sinks, part_3):
    t = proj.shape[0]
    tq = min(512, t)
    nb = tq // 128
    n_tiles = t // tq
    lane_tables, (cos_t, sin_t) = tables
    sp = _attn_specs(t, tq)
    scale = 1.0 / math.sqrt(HEAD_DIM)
    half = ROT_DIM // 2

    def unrope_rows(z, c, s):
        out = []
        for base in (0, HEAD_DIM):
            g0, g1 = z[base : base + half], z[base + half : base + ROT_DIM]
            out += [g0 * c + g1 * s, g1 * c - g0 * s, z[base + ROT_DIM : base + HEAD_DIM]]
        return jnp.concatenate(out, axis=0)

    def swap(part_hbm, land, send_sem, recv_sem):
        sibling = _flip(_position(), 1)
        return pltpu.make_async_remote_copy(
            src_ref=part_hbm.at[:, :, sibling[2]], dst_ref=land, send_sem=send_sem, recv_sem=recv_sem,
            device_id=sibling, device_id_type=MESH,
        )

    def body(q_ref, k_ref, v_ref, o_ref, do_ref, ct_ref, st_ref, ck, s1k, s2k, sink_ref, part_hbm,
             dq_ref, dk_ref, dv_ref, dsink_ref, sum_hbm, mine_hbm, kd, kdt, vd, bias_ref, dkacc, dvacc, own, land, stage,
             send_sem, recv_sem, local_sems):
        pair = pl.program_id(0)
        i = pl.program_id(1)
        me = _position()
        load_own = pltpu.make_async_copy(part_hbm.at[:, :, me[2]], own, local_sems.at[0])

        @pl.when((pair == 0) & (i == 0))
        def _():
            swap(part_hbm, land, send_sem, recv_sem).start()
            load_own.start()

        @pl.when(i == 0)
        def _():
            kv, vv = k_ref[...], v_ref[...]
            for j in range(2):
                kdup = _dup_lanes(kv, j)
                kd[j, 0:128, :] = jnp.zeros((128, 128), BF16)
                kd[j, 128:, :] = kdup.astype(BF16)
                kdt[j, :, 0:128] = jnp.zeros((128, 128), BF16)
                kdt[j, :, 128:] = kdup.T.astype(BF16)
                vd[j, 0:128, :] = jnp.zeros((128, 128), BF16)
                vd[j, 128:, :] = _dup_lanes(vv, j).astype(BF16)
            _fill_bias_t(bias_ref)
            dkacc[...] = jnp.zeros_like(dkacc)
            dvacc[...] = jnp.zeros_like(dvacc)
            dsink_ref[...] = jnp.zeros_like(dsink_ref)

        row_lo = _row_lo()
        ones = jnp.ones((8, 128), BF16)

        def block(b, carry):
            r0 = pl.multiple_of(b * 128, 128)
            gb = i * nb + b
            keys = pl.ds(pl.multiple_of(gb * 128, 128), 256)
            rows = pl.ds(r0, 128)
            qt = q_ref[rows, :] * scale
            dob = do_ref[rows, :].astype(F32)
            prod = dob * o_ref[rows, :].astype(F32)
            ct, st = ct_ref[:, rows], st_ref[:, rows]
            bias = bias_ref[jnp.where(gb == 0, 1, 0)]
            bias4 = jnp.concatenate([bias] * 4, axis=1)
            cols = []
            for j in range(2):
                q4 = _split_heads(qt, j).astype(BF16)
                kw = kd[j, keys, :]
                p, e_sink, denom = _softmax_t(q4, kw, bias4, _sink_row(sink_ref, pair, j))
                inv = 1.0 / denom
                prob = p * inv
                do4 = _split_heads(dob, j).astype(BF16)
                delta = _dot_nt(ones, _split_heads(prod, j).astype(BF16))[0:1, :]
                dp = _dot_nt(vd[j, keys, :], do4)
                ds = (prob * (dp - delta)).astype(BF16)
                dkacc[j, keys, :] = dkacc[j, keys, :] + _dot(ds, q4)
                dvacc[j, keys, :] = dvacc[j, keys, :] + _dot(prob.astype(BF16), do4)
                dqt = _dot(kdt[j, :, keys], ds) * scale
                dsink = -(e_sink * inv) * delta
                for h in range(4):
                    row = 4 * j + h
                    total = jnp.sum(dsink[:, 128 * h : 128 * (h + 1)], axis=1, keepdims=True)
                    dsink_ref[0, row : row + 1, :] = dsink_ref[0, row : row + 1, :] + jnp.broadcast_to(total, (1, 128))
                for gi in range(2):
                    z = jnp.where(row_lo, dqt[:, 256 * gi : 256 * gi + 128], dqt[:, 256 * gi + 128 : 256 * gi + 256])
                    cols.append(unrope_rows(z, ct, st).T)
            dq_ref[rows, :] = jnp.concatenate(cols, axis=1).astype(BF16)
            return carry

        lax.fori_loop(0, nb, block, 0, unroll=True)

        @pl.when(i == n_tiles - 1)
        def _():
            lo = _lane_lo()
            dk = jnp.where(lo, _fold_lanes(dkacc[0, 128:, :]), _fold_lanes(dkacc[1, 128:, :]))
            dk_ref[...] = _rope_transposed(dk, ck[...], s1k[...], s2k[...]).astype(BF16)
            dv_ref[...] = jnp.where(lo, _fold_lanes(dvacc[0, 128:, :]), _fold_lanes(dvacc[1, 128:, :])).astype(BF16)

        @pl.when((pair == 1) & (i == n_tiles - 1))
        def _():
            swap(part_hbm, land, send_sem, recv_sem).wait()
            load_own.wait()
            own[...] = own[...] + land[...]
            stage[...] = own[...].astype(BF16)
            results = [pltpu.make_async_copy(stage, sum_hbm, local_sems.at[1]),
                       pltpu.make_async_copy(own.at[:, _chip_of(me)], mine_hbm, local_sems.at[2])]
            for cp in results:
                cp.start()
            for cp in results:
                cp.wait()

    kv_out = pl.BlockSpec((t, 128), lambda p, i: (0, p))
    tab_t = pl.BlockSpec((half, tq), lambda p, i: (0, i))
    tab_k = [pl.BlockSpec((None, t, 128), lambda p, i, a=a: (a, 0, 0)) for a in range(3)]
    return pl.pallas_call(
        body,
        name="attn_bwd",
        grid=(2, n_tiles),
        in_specs=[sp["q"], sp["k"], sp["v"], sp["act"], sp["act"], tab_t, tab_t, *tab_k, sp["sink"],
                  pl.BlockSpec(memory_space=pl.ANY)],
        out_specs=[sp["act"], kv_out, kv_out, pl.BlockSpec((1, 8, 128), lambda p, i: (p, 0, 0)),
                   pl.BlockSpec(memory_space=pl.ANY), pl.BlockSpec(memory_space=pl.ANY)],
        out_shape=[
            jax.ShapeDtypeStruct((t, D), BF16),
            jax.ShapeDtypeStruct((t, 256), BF16),
            jax.ShapeDtypeStruct((t, 256), BF16),
            jax.ShapeDtypeStruct((2, 8, 128), F32),
            jax.ShapeDtypeStruct((3, 4, ROW_SHARD, D), BF16),
            jax.ShapeDtypeStruct((3, ROW_SHARD, D), F32),
        ],
        scratch_shapes=[pltpu.VMEM((2, t + 128, 128), BF16), pltpu.VMEM((2, 128, t + 128), BF16), pltpu.VMEM((2, t + 128, 128), BF16),
                        pltpu.VMEM((2, 256, 128), F32), pltpu.VMEM((2, t + 128, 128), F32), pltpu.VMEM((2, t + 128, 128), F32),
                        pltpu.VMEM((3, 4, ROW_SHARD, D), F32), pltpu.VMEM((3, 4, ROW_SHARD, D), F32), pltpu.VMEM((3, 4, ROW_SHARD, D), BF16),
                        pltpu.SemaphoreType.DMA, pltpu.SemaphoreType.DMA, pltpu.SemaphoreType.DMA((3,))],
        compiler_params=_params(2),
    )(proj, proj, proj, o, d_o, cos_t, sin_t, lane_tables, lane_tables, lane_tables, sinks, part_3)


HALF = 512


def _softplus(y):
    u = jnp.exp(-jnp.abs(y))
    w = 1.0 + u
    log1p = jnp.where(w == 1.0, u, jnp.log(w) * (u / jnp.where(w == 1.0, 1.0, w - 1.0)))
    return jnp.maximum(y, 0.0) + log1p


def _gate_block(w_ref, blk):
    return w_ref[:, :, 256 * blk : 256 * (blk + 1)].reshape(256, 256)


def _gate_matmul(act_bf, w_ref):
    return jnp.concatenate([_dot(act_bf[:, 256 * blk : 256 * (blk + 1)], _gate_block(w_ref, blk)) for blk in range(2)], axis=1)


def _gate_matmul_t(grad_bf, w_ref):
    return jnp.concatenate([_dot_nt(grad_bf[:, 256 * blk : 256 * (blk + 1)], _gate_block(w_ref, blk)) for blk in range(2)], axis=1)


def _conv_and_gates(xe_ref, tm, cw_ref, cb_ref, wa_ref, wx_ref, ba_ref, bx_ref, lam_ref, keep):
    xc = cb_ref[...] + sum(cw_ref[k : k + 1, :] * xe_ref[pl.ds(5 + k, tm), :] for k in range(CONV_WIDTH))
    xc_bf = xc.astype(BF16)
    r = _sigmoid(_gate_matmul(xc_bf, wa_ref) + ba_ref[...])
    ig = _sigmoid(_gate_matmul(xc_bf, wx_ref) + bx_ref[...])
    sp = _softplus(-lam_ref[...])
    log_a = -LRU_C * r * sp
    tanh = jnp.tanh(-log_a)
    mult = jnp.sqrt(2.0 * tanh / (1.0 + tanh))
    a = jnp.where(keep, jnp.exp(log_a), 0.0)
    mult = jnp.where(keep, mult, 1.0)
    return xc, xc_bf, r, ig, sp, a, mult


def _scan_rows(a_ref, b_ref, h_ref, carry_ref, tm, reverse):
    width = a_ref.shape[1]
    row = lax.broadcasted_iota(jnp.int32, (8, width), 0)

    def group(g, carry):
        gg = tm // 8 - 1 - g if reverse else g
        r0 = pl.multiple_of(gg * 8, 8)
        a = a_ref[pl.ds(r0, 8), :]
        b = b_ref[pl.ds(r0, 8), :]
        for s in (1, 2, 4):
            shift = 8 - s if reverse else s
            a_sh = pltpu.roll(a, shift, 0)
            b_sh = pltpu.roll(b, shift, 0)
            ok = (row < 8 - s) if reverse else (row >= s)
            b = jnp.where(ok, a * b_sh + b, b)
            a = jnp.where(ok, a * a_sh, a)
        h = a * carry + b
        h_ref[pl.ds(r0, 8), :] = h
        last = h[0:1, :] if reverse else h[7:8, :]
        return jnp.broadcast_to(last, (8, width))

    carry_ref[...] = lax.fori_loop(0, tm // 8, group, carry_ref[...])


def _rnn_specs(t, tm, order):
    n = t // tm
    return dict(
        tile=pl.BlockSpec((tm, HALF), lambda ch, i: (order(i, n), ch)),
        before=lambda col0: pl.BlockSpec((8, HALF), lambda ch, i: (jnp.maximum(order(i, n) * (tm // 8) - 1, 0), col0 + ch)),
        proj_tile=lambda col0: pl.BlockSpec((tm, HALF), lambda ch, i: (order(i, n), col0 + ch)),
        keep=pl.BlockSpec((tm, 1), lambda ch, i: (order(i, n), 0)),
        gates=lambda which: pl.BlockSpec((N_DEV, GATE_SHARD, HALF), lambda ch, i: (0, PACK_GATE // GATE_SHARD + which, ch)),
        gate_grads=pl.BlockSpec((2, 2, 256, 256), lambda ch, i: (0, ch, 0, 0)),
        vec=pl.BlockSpec((1, HALF), lambda ch, i: (0, ch)),
        conv=pl.BlockSpec((CONV_WIDTH, HALF), lambda ch, i: (0, ch)),
    )


def _rnn_fwd(proj, keep, w_all, rg_ba, rg_bx, rg_lambda, conv_w, conv_b):
    t = proj.shape[0]
    tm = min(512, t)
    sp = _rnn_specs(t, tm, lambda i, n: i)

    def body(xr_ref, xb_ref, keep_ref, wa_ref, wx_ref, ba_ref, bx_ref, lam_ref, cw_ref, cb_ref, hh_ref, xe, a_buf, b_buf, carry):
        i = pl.program_id(1)

        @pl.when(i == 0)
        def _():
            carry[...] = jnp.zeros_like(carry)

        xe[0:8, :] = jnp.where(i > 0, xb_ref[...], 0.0)
        xe[8:, :] = xr_ref[...]
        keep_rows = keep_ref[...] != 0.0
        xc, _, _, ig, _, a, mult = _conv_and_gates(xe, tm, cw_ref, cb_ref, wa_ref, wx_ref, ba_ref, bx_ref, lam_ref, keep_rows)
        a_buf[...] = a
        b_buf[...] = mult * (ig * xc)
        _scan_rows(a_buf, b_buf, hh_ref, carry, tm, reverse=False)

    return pl.pallas_call(
        body,
        name="rnn_fwd",
        grid=(2, t // tm),
        in_specs=[sp["proj_tile"](SEG_XR // HALF), sp["before"](SEG_XR // HALF), sp["keep"], sp["gates"](0), sp["gates"](1),
                  sp["vec"], sp["vec"], sp["vec"], sp["conv"], sp["vec"]],
        out_specs=sp["tile"],
        out_shape=jax.ShapeDtypeStruct((t, D), F32),
        scratch_shapes=[pltpu.VMEM((tm + 8, HALF), F32), pltpu.VMEM((tm, HALF), F32), pltpu.VMEM((tm, HALF), F32), pltpu.VMEM((8, HALF), F32)],
        compiler_params=_params(2),
    )(proj, proj, keep, w_all, w_all, rg_ba, rg_bx, rg_lambda, conv_w, conv_b)


def _rnn_bwd(proj, hh, d_hh, keep, w_all, rg_ba, rg_bx, rg_lambda, conv_w, conv_b, sum_3_bf):
    t = proj.shape[0]
    tm = min(512, t)
    n_tiles = t // tm
    sp = _rnn_specs(t, tm, lambda i, n: n - 1 - i)

    def body(xr_ref, xb_ref, hh_ref, hb_ref, dhh_ref, keep_ref, wa_ref, wx_ref, ba_ref, bx_ref, lam_ref, cw_ref, cb_ref,
             s3_hbm, dxr_ref, gw_ref, gvec_ref, l3_hbm, xe, he, ae, c_buf, g_buf, dxe, carry, a_next, send_sems, recv_sems):
        ch = pl.program_id(0)
        i = pl.program_id(1)
        first_tile = i == n_tiles - 1
        views = [lambda ref, peer: ref.at[:, _chip_of(peer)]]

        @pl.when((ch == 0) & (i == 0))
        def _():
            for cp in _chip_copies([s3_hbm], [l3_hbm], views, send_sems, recv_sems):
                cp.start()

        @pl.when((ch == 1) & (i == n_tiles - 1))
        def _():
            for cp in _chip_copies([s3_hbm], [l3_hbm], views, send_sems, recv_sems):
                cp.wait_recv()
                cp.wait_send()

        @pl.when(i == 0)
        def _():
            carry[...] = jnp.zeros_like(carry)
            a_next[...] = jnp.zeros_like(a_next)
            dxe[tm:, :] = jnp.zeros((8, HALF), F32)
            gw_ref[...] = jnp.zeros_like(gw_ref)
            gvec_ref[...] = jnp.zeros_like(gvec_ref)

        xe[0:8, :] = jnp.where(first_tile, 0.0, xb_ref[...])
        xe[8:, :] = xr_ref[...]
        he[0:8, :] = jnp.where(first_tile, 0.0, hb_ref[...])
        he[8:, :] = hh_ref[...]
        keep_rows = keep_ref[...] != 0.0
        xc, xc_bf, r, ig, sp_, a, mult = _conv_and_gates(xe, tm, cw_ref, cb_ref, wa_ref, wx_ref, ba_ref, bx_ref, lam_ref, keep_rows)

        ae[0:tm, :] = a
        ae[tm:, :] = a_next[...]
        a_next[...] = jnp.broadcast_to(a[0:1, :], (8, HALF))
        c_buf[...] = ae[pl.ds(1, tm), :]
        g_buf[...] = dhh_ref[...]
        _scan_rows(c_buf, g_buf, g_buf, carry, tm, reverse=True)
        g = g_buf[...]

        h_prev = he[pl.ds(7, tm), :]
        bb = ig * xc
        d_ig = g * mult * xc
        d_xc = g * mult * ig
        d_log_a = jnp.where(keep_rows, g * h_prev * a - (g * bb) * (a * a) / mult, 0.0)
        d_r = d_log_a * (-LRU_C * sp_)
        d_sp = jnp.sum(d_log_a * (-LRU_C * r), axis=0, keepdims=True)
        d_zr = d_r * r * (1.0 - r)
        d_zi = d_ig * ig * (1.0 - ig)
        d_zr_bf = d_zr.astype(BF16)
        d_zi_bf = d_zi.astype(BF16)
        d_xc = d_xc + _gate_matmul_t(d_zr_bf, wa_ref) + _gate_matmul_t(d_zi_bf, wx_ref)
        for blk in range(2):
            cols = slice(256 * blk, 256 * (blk + 1))
            gw_ref[0, blk] = gw_ref[0, blk] + _dot_tn(xc_bf[:, cols], d_zr_bf[:, cols])
            gw_ref[1, blk] = gw_ref[1, blk] + _dot_tn(xc_bf[:, cols], d_zi_bf[:, cols])

        dxe[0:tm, :] = d_xc
        shifted = [dxe[pl.ds(3 - k, tm), :] for k in range(CONV_WIDTH)]
        dxr = sum(cw_ref[k : k + 1, :] * shifted[k] for k in range(CONV_WIDTH))
        dxr_ref[...] = dxr.astype(BF16)
        dxe[tm:, :] = d_xc[0:8, :]

        lam = lam_ref[...]
        xr_tile = xr_ref[...]
        sums = [
            jnp.sum(d_zr, axis=0, keepdims=True),
            jnp.sum(d_zi, axis=0, keepdims=True),
            d_sp * (-_sigmoid(-lam)),
            jnp.sum(d_xc, axis=0, keepdims=True),
        ] + [jnp.sum(xr_tile * shifted[k], axis=0, keepdims=True) for k in range(CONV_WIDTH)]
        gvec_ref[...] = gvec_ref[...] + jnp.concatenate(sums, axis=0)

    return pl.pallas_call(
        body,
        name="rnn_bwd",
        grid=(2, n_tiles),
        in_specs=[
            sp["proj_tile"](SEG_XR // HALF), sp["before"](SEG_XR // HALF), sp["tile"], sp["before"](0), sp["tile"], sp["keep"],
            sp["gates"](0), sp["gates"](1), sp["vec"], sp["vec"], sp["vec"], sp["conv"], sp["vec"], pl.BlockSpec(memory_space=pl.ANY),
        ],
        out_specs=[sp["tile"], sp["gate_grads"], pl.BlockSpec((8, HALF), lambda ch, i: (0, ch)), pl.BlockSpec(memory_space=pl.ANY)],
        out_shape=[
            jax.ShapeDtypeStruct((t, D), BF16),
            jax.ShapeDtypeStruct((2, 4, 256, 256), F32),
            jax.ShapeDtypeStruct((8, D), F32),
            jax.ShapeDtypeStruct((len(CHIP_FLIPS), 3, ROW_SHARD, D), BF16),
        ],
        scratch_shapes=[
            pltpu.VMEM((tm + 8, HALF), F32),
            pltpu.VMEM((tm + 8, HALF), F32),
            pltpu.VMEM((tm + 8, HALF), F32),
            pltpu.VMEM((tm, HALF), F32),
            pltpu.VMEM((tm, HALF), F32),
            pltpu.VMEM((tm + 8, HALF), F32),
            pltpu.VMEM((8, HALF), F32),
            pltpu.VMEM((8, HALF), F32),
            pltpu.SemaphoreType.DMA((len(CHIP_FLIPS),)),
            pltpu.SemaphoreType.DMA((len(CHIP_FLIPS),)),
        ],
        compiler_params=_params(2),
    )(proj, proj, hh, hh, d_hh, keep, w_all, w_all, rg_ba, rg_bx, rg_lambda, conv_w, conv_b, sum_3_bf)


def _mid(x, target, o, hh, proj, w_all, gate, final_g):
    t = x.shape[0]
    tm = min(256, t)
    n_tiles = t // tm
    full = pl.BlockSpec((tm, D), lambda i: (i, 0))
    half = lambda col: pl.BlockSpec((tm, 512), lambda i: (i, col))
    vec = pl.BlockSpec((1, D), lambda i: (0, 0))
    any_spec = pl.BlockSpec(memory_space=pl.ANY)

    def body(x_ref, tg_ref, o_ref, hh_ref, ga0, ga1, gr0, gr1, ma0, ma1, mr0, mr1, w_hbm, gate_ref, fg_ref,
             dres_ref, do_ref, dhh_ref, dga_ref, dgr_ref, dma_ref, dmr_ref, gw_hbm, vec_ref, w, gw, gw_sems, w_sems):
        i = pl.program_id(0)
        loads = [
            pltpu.make_async_copy(
                w_hbm.at[s, pl.ds(ROW_SHARD * k, ROW_SHARD), :], w.at[k, pl.ds(ROW_SHARD * s, ROW_SHARD), :], w_sems.at[k * N_DEV + s]
            )
            for k in range(3)
            for s in range(N_DEV)
        ]
        keep = [pltpu.make_async_copy(gw.at[k], gw_hbm.at[k], gw_sems.at[k]) for k in range(3)]

        def at_first_step_wait(copies):
            @pl.when(i == 0)
            def _():
                for cp in copies:
                    cp.wait()

        @pl.when(i == 0)
        def _():
            for cp in loads:
                cp.start()
            gw[...] = jnp.zeros_like(gw)
            vec_ref[...] = jnp.zeros_like(vec_ref)

        cat = lambda a, b: jnp.concatenate([a[...], b[...]], axis=1)
        ga, gr, ma, mr = cat(ga0, ga1), cat(gr0, gr1), cat(ma0, ma1), cat(mr0, mr1)
        ov, hv, gate_v, fg = o_ref[...].astype(F32), hh_ref[...], gate_ref[...], fg_ref[...]
        sg_a = _sigmoid(ga)
        sg_r = _sigmoid(gr)
        silu_a = ga * sg_a
        silu_r = gr * sg_r
        ya = (ov * silu_a).astype(BF16)
        yr = (hv * silu_r).astype(BF16)
        at_first_step_wait(loads[: 2 * N_DEV])
        pa = _dot(ya, w[0])
        pr = _dot(yr, w[1])
        sa = _sigmoid(ma)
        sr = _sigmoid(mr)
        merged = (sa * pa + sr * pr).astype(BF16)
        at_first_step_wait(loads[2 * N_DEV :])
        mo = _dot(merged, w[2])
        out = x_ref[...] + gate_v * mo
        r2 = lax.rsqrt(jnp.mean(out * out, axis=-1, keepdims=True) + NORM_EPS)
        outn = out * r2
        err = outn * fg - tg_ref[...]
        loss = 0.5 * jnp.sum(jnp.mean(err * err, axis=-1, keepdims=True), axis=0, keepdims=True)
        dy = err * (1.0 / D)
        g_final = jnp.sum(dy * outn, axis=0, keepdims=True)
        doutn = dy * fg
        dout = r2 * (doutn - outn * jnp.mean(doutn * outn, axis=-1, keepdims=True))
        dres_ref[...] = dout
        d_gate = jnp.sum(dout * mo, axis=0, keepdims=True)
        dmo = (dout * gate_v).astype(BF16)
        dmerged = _dot_nt(dmo, w[2])
        gw[2] = gw[2] + _dot_tn(merged, dmo)

        @pl.when(i == n_tiles - 1)
        def _():
            keep[2].start()

        dpa = (dmerged * sa).astype(BF16)
        dpr = (dmerged * sr).astype(BF16)
        dma_ref[...] = (dmerged * pa * (sa * (1.0 - sa))).astype(BF16)
        dmr_ref[...] = (dmerged * pr * (sr * (1.0 - sr))).astype(BF16)
        dya = _dot_nt(dpa, w[0])
        dyr = _dot_nt(dpr, w[1])
        gw[0] = gw[0] + _dot_tn(ya, dpa)
        gw[1] = gw[1] + _dot_tn(yr, dpr)
        do_ref[...] = (dya * silu_a).astype(BF16)
        dhh_ref[...] = dyr * silu_r
        dga_ref[...] = (dya * ov * (sg_a * (1.0 + ga * (1.0 - sg_a)))).astype(BF16)
        dgr_ref[...] = (dyr * hv * (sg_r * (1.0 + gr * (1.0 - sg_r)))).astype(BF16)
        zeros = jnp.zeros((5, D), F32)
        vec_ref[...] = vec_ref[...] + jnp.concatenate([g_final, d_gate, jnp.broadcast_to(loss, (1, D)), zeros], axis=0)

        @pl.when(i == n_tiles - 1)
        def _():
            for cp in keep[:2]:
                cp.start()
            for cp in keep:
                cp.wait()

    bf_tile = jax.ShapeDtypeStruct((t, D), BF16)
    f_tile = jax.ShapeDtypeStruct((t, D), F32)
    return pl.pallas_call(
        body,
        name="mid",
        grid=(n_tiles,),
        in_specs=[full, full, full, full,
                  half(SEG_GA // 512), half(SEG_GA // 512 + 1), half(SEG_GR // 512), half(SEG_GR // 512 + 1),
                  half(SEG_MA // 512), half(SEG_MA // 512 + 1), half(SEG_MR // 512), half(SEG_MR // 512 + 1),
                  any_spec, vec, vec],
        out_specs=[full, full, full, full, full, full, full, any_spec, pl.BlockSpec((8, D), lambda i: (0, 0))],
        out_shape=[f_tile, bf_tile, f_tile, bf_tile, bf_tile, bf_tile, bf_tile,
                   jax.ShapeDtypeStruct((3, D, D), F32), jax.ShapeDtypeStruct((8, D), F32)],
        scratch_shapes=[pltpu.VMEM((3, D, D), BF16), pltpu.VMEM((3, D, D), F32), pltpu.SemaphoreType.DMA((3,)), pltpu.SemaphoreType.DMA((3 * N_DEV,))],
        compiler_params=_params(1),
    )(x, target, o, hh, proj, proj, proj, proj, proj, proj, proj, proj, w_all, gate, final_g)


_SEGMENTS = ((SEG_Q, 1024), (SEG_K, 256), (SEG_V, 256), (SEG_GA, 1024), (SEG_XR, 1024), (SEG_GR, 1024), (SEG_MA, 1024), (SEG_MR, 1024))


def _seg_specs(tm):
    return [pl.BlockSpec((tm, width), lambda i: (i, 0)) for _, width in _SEGMENTS]


def _bwd_dh(dsegs, w_all, x, dres, mod, norm_g):
    t = x.shape[0]
    tm = min(512, t)
    n_tiles = t // tm
    n_seg = len(_SEGMENTS)
    full = pl.BlockSpec((tm, D), lambda i: (i, 0))

    def body(*refs):
        segs = refs[:n_seg]
        w_hbm, x_ref, dres_ref, mod_ref, ng_ref, gx_ref, vec_ref, w, sems = refs[n_seg:]
        i = pl.program_id(0)
        loads = [
            pltpu.make_async_copy(w_hbm.at[s], w.at[pl.ds(W_IN_SHARD * s, W_IN_SHARD), :], sems.at[s])
            for s in range(N_DEV)
        ]

        @pl.when(i == 0)
        def _():
            for cp in loads:
                cp.start()
            vec_ref[...] = jnp.zeros_like(vec_ref)

        dh = None
        waited = 0
        for seg, (start, width) in zip(segs, _SEGMENTS):
            needed = -(-(start + width) // W_IN_SHARD)
            if needed > waited:

                @pl.when(i == 0)
                def _(arrived=loads[waited:needed]):
                    for cp in arrived:
                        cp.wait()

                waited = needed
            part = _dot(seg[...], w[start : start + width, :])
            dh = part if dh is None else dh + part
        xv = x_ref[...]
        r1 = lax.rsqrt(jnp.mean(xv * xv, axis=-1, keepdims=True) + NORM_EPS)
        xn = xv * r1
        ng = ng_ref[...]
        scale1 = 1.0 + mod_ref[:, D : 2 * D]
        d_shift = jnp.sum(dh, axis=0, keepdims=True)
        d_scale = jnp.sum(dh * (xn * ng), axis=0, keepdims=True)
        g_norm = jnp.sum(dh * xn * scale1, axis=0, keepdims=True)
        dxn = dh * (ng * scale1)
        dx = r1 * (dxn - xn * jnp.mean(dxn * xn, axis=-1, keepdims=True))
        gx_ref[...] = dres_ref[...] + dx
        vec_ref[...] = vec_ref[...] + jnp.concatenate([d_shift, d_scale, g_norm, jnp.zeros((5, D), F32)], axis=0)

    any_spec = pl.BlockSpec(memory_space=pl.ANY)
    return pl.pallas_call(
        body,
        name="bwd_dh",
        grid=(n_tiles,),
        in_specs=_seg_specs(tm) + [any_spec, full, full, pl.BlockSpec((1, 3 * D), lambda i: (0, 0)), pl.BlockSpec((1, D), lambda i: (0, 0))],
        out_specs=[full, pl.BlockSpec((8, D), lambda i: (0, 0))],
        out_shape=[jax.ShapeDtypeStruct((t, D), F32), jax.ShapeDtypeStruct((8, D), F32)],
        scratch_shapes=[pltpu.VMEM((IN_W, D), BF16), pltpu.SemaphoreType.DMA((N_DEV,))],
        compiler_params=_params(1),
    )(*dsegs, w_all, x, dres, mod, norm_g)


_HBM = pl.BlockSpec(memory_space=pltpu.HBM)
_SEM = pl.BlockSpec(memory_space=pltpu.SEMAPHORE)
_EFFECT = pltpu.SideEffectType.DATAFLOW_SIDE_EFFECTING


def _w_exchange_copies(sum_ref, land_ref, sems):
    me = _position()
    n = len(CHIP_FLIPS)
    return [
        pltpu.make_async_remote_copy(
            src_ref=sum_ref.at[_chip_of(_flip(me, k))], dst_ref=land_ref.at[j], send_sem=sems[j], recv_sem=sems[n + j],
            device_id=_flip(me, k), device_id_type=MESH,
        )
        for j, k in enumerate(CHIP_FLIPS)
    ]


def _w_exchange_start(sum_w_bf):
    n_sems = 2 * len(CHIP_FLIPS)
    land = lax.empty((len(CHIP_FLIPS), W_IN_SHARD, D), BF16)

    def body(sum_ref, land_ref, *outs):
        for cp in _w_exchange_copies(sum_ref, land_ref, outs[:n_sems]):
            cp.start()
        token = outs[n_sems + 2]
        token[...] = jnp.zeros_like(token)

    res = pl.pallas_call(
        body,
        name="w_exchange_start",
        out_shape=(*[pltpu.SemaphoreType.DMA(())] * n_sems, pltpu.HBM(sum_w_bf.shape, BF16), pltpu.HBM(land.shape, BF16),
                   jax.ShapeDtypeStruct((8, 128), F32)),
        in_specs=(_HBM, _HBM),
        out_specs=(*[_SEM] * n_sems, _HBM, _HBM, pl.BlockSpec(memory_space=pltpu.VMEM)),
        input_output_aliases={0: n_sems, 1: n_sems + 1},
        compiler_params=pltpu.CompilerParams(has_side_effects=_EFFECT),
    )(pltpu.with_memory_space_constraint(sum_w_bf, pltpu.HBM), pltpu.with_memory_space_constraint(land, pltpu.HBM))
    return res[:n_sems], res[n_sems], res[n_sems + 1], res[n_sems + 2]


def _w_exchange_wait(sems, sum_thru, land_thru, after):
    n_sems = len(sems)

    def body(sum_ref, land_ref, *rest):
        for cp in _w_exchange_copies(sum_ref, land_ref, rest[:n_sems]):
            cp.wait_send()
            cp.wait_recv()

    return pl.pallas_call(
        body,
        name="w_exchange_wait",
        out_shape=(pltpu.HBM(sum_thru.shape, BF16), pltpu.HBM(land_thru.shape, BF16)),
        in_specs=(_HBM, _HBM, *[_SEM] * n_sems, pl.BlockSpec(memory_space=pl.ANY)),
        out_specs=(_HBM, _HBM),
        input_output_aliases={0: 0, 1: 1},
        compiler_params=pltpu.CompilerParams(has_side_effects=_EFFECT),
    )(sum_thru, land_thru, *sems, after)[1]


def _bwd_gw(dsegs, h, g_gates):
    t = h.shape[0]
    tm = min(512, t)
    n_tiles = t // tm
    n_seg = len(_SEGMENTS)
    n_chips = N_DEV // 2

    def gate_copies(g_hbm, lg_hbm, send_sems, recv_sems, arriving):
        me = _position()
        return [
            pltpu.make_async_remote_copy(
                src_ref=g_hbm.at[:, :, _slot(_flip(me, k))],
                dst_ref=lg_hbm.at[_slot(_flip(me, k) if arriving else me)],
                send_sem=send_sems.at[k - 1],
                recv_sem=recv_sems.at[k - 1],
                device_id=_flip(me, k),
                device_id_type=MESH,
            )
            for k in range(1, N_DEV)
        ]

    def body(*refs):
        segs = refs[:n_seg]
        (h_ref, g_hbm, own_hbm, sum_hbm, lg_hbm, acc, stage, land, own_sems, pair_send, pair_recv, send_sems, recv_sems, g_local) = refs[n_seg:]
        i = pl.program_id(0)
        slot = _slot(_position())
        own_gates = pltpu.make_async_copy(g_hbm.at[:, :, slot], lg_hbm.at[slot], g_local)

        @pl.when(i == 0)
        def _():
            own_gates.start()
            for cp in gate_copies(g_hbm, lg_hbm, send_sems, recv_sems, arriving=False):
                cp.start()
            acc[...] = jnp.zeros_like(acc)

        hv = h_ref[...]
        for seg, (start, width) in zip(segs, _SEGMENTS):
            acc[start : start + width, :] = acc[start : start + width, :] + _dot_tn(seg[...], hv)

        @pl.when(i == n_tiles - 1)
        def _():
            me = _position()
            sibling = _flip(me, 1)

            def rows(q, c):
                return pl.ds(pl.multiple_of(W_IN_SHARD * (2 * q + c), 8), W_IN_SHARD)

            def to_sibling(q):
                return pltpu.make_async_remote_copy(
                    src_ref=stage.at[q % 2], dst_ref=land.at[q], send_sem=pair_send.at[q], recv_sem=pair_recv.at[q],
                    device_id=sibling, device_id_type=MESH,
                )

            for q in range(n_chips):
                if q >= 2:
                    to_sibling(q - 2).wait_send()
                stage[q % 2] = acc[rows(q, sibling[2]), :].astype(BF16)
                to_sibling(q).start()
            for q in range(n_chips - 2, n_chips):
                to_sibling(q).wait_send()
            sums = [pltpu.make_async_copy(stage.at[q % 2], sum_hbm.at[q], own_sems.at[q]) for q in range(n_chips)]
            for q in range(n_chips):
                to_sibling(q).wait_recv()
                if q >= 2:
                    sums[q - 2].wait()
                total = acc[rows(q, me[2]), :] + land[q].astype(F32)
                acc[rows(q, me[2]), :] = total
                stage[q % 2] = total.astype(BF16)
                sums[q].start()
            mine = pltpu.make_async_copy(acc.at[rows(_chip_of(me), me[2]), :], own_hbm, own_sems.at[n_chips])
            mine.start()
            for q in range(n_chips - 2, n_chips):
                sums[q].wait()
            mine.wait()
            for cp in gate_copies(g_hbm, lg_hbm, send_sems, recv_sems, arriving=True):
                cp.wait_recv()
            for cp in gate_copies(g_hbm, lg_hbm, send_sems, recv_sems, arriving=False):
                cp.wait_send()
            own_gates.wait()

    any_spec = pl.BlockSpec(memory_space=pl.ANY)
    n_copies = N_DEV - 1
    return pl.pallas_call(
        body,
        name="bwd_gw",
        grid=(n_tiles,),
        in_specs=_seg_specs(tm) + [pl.BlockSpec((tm, D), lambda i: (i, 0)), any_spec],
        out_specs=[any_spec, any_spec, any_spec],
        out_shape=[jax.ShapeDtypeStruct((W_IN_SHARD, D), F32),
                   jax.ShapeDtypeStruct((n_chips, W_IN_SHARD, D), BF16),
                   jax.ShapeDtypeStruct((N_DEV, 2, 4, GATE_SHARD, 256), F32)],
        scratch_shapes=[pltpu.VMEM((IN_W, D), F32), pltpu.VMEM((2, W_IN_SHARD, D), BF16), pltpu.VMEM((n_chips, W_IN_SHARD, D), BF16),
                        pltpu.SemaphoreType.DMA((n_chips + 1,)), pltpu.SemaphoreType.DMA((n_chips,)), pltpu.SemaphoreType.DMA((n_chips,)),
                        pltpu.SemaphoreType.DMA((n_copies,)), pltpu.SemaphoreType.DMA((n_copies,)), pltpu.SemaphoreType.DMA],
        compiler_params=_params(1),
    )(*dsegs, h, g_gates)


def _adamw(w, g, m, v):
    m = ADAM_B1 * m + (1.0 - ADAM_B1) * g
    v = ADAM_B2 * v + (1.0 - ADAM_B2) * (g * g)
    m_hat = m / (1.0 - ADAM_B1**ADAM_STEP)
    v_hat = v / (1.0 - ADAM_B2**ADAM_STEP)
    delta = -ADAM_LR * (m_hat / (jnp.sqrt(v_hat) + ADAM_EPS) + ADAM_WD * w)
    return delta, m, v


def _w_in_update(own, land, w_t, m_t, v_t):
    n_slots, rows, cols = land.shape
    rb = 208
    spec = pl.BlockSpec((rb, cols), lambda i: (i, 0))

    def body(o_ref, l_ref, w_ref, m_ref, v_ref, g_ref, d_ref, m2_ref, v2_ref):
        total = o_ref[...]
        for s in range(n_slots):
            total = total + l_ref[s].astype(F32)
        g_ref[...] = total
        d_ref[...], m2_ref[...], v2_ref[...] = _adamw(w_ref[...], total, m_ref[...], v_ref[...])

    shape = jax.ShapeDtypeStruct((rows, cols), F32)
    return pl.pallas_call(
        body,
        name="w_in_update",
        grid=(rows // rb,),
        in_specs=[spec, pl.BlockSpec((n_slots, rb, cols), lambda i: (0, i, 0)), spec, spec, spec],
        out_specs=[spec] * 4,
        out_shape=[shape] * 4,
        compiler_params=_params(1),
    )(own, land, w_t, m_t, v_t)


def _shard_update(own_3, land_3, land_g, weights, m_in, v_in):
    n = len(weights)

    def body(*refs):
        o3_ref, l3_ref, lg_ref = refs[:3]
        w_refs, m_refs, v_refs = (refs[3 + n * k : 3 + n * (k + 1)] for k in range(3))
        outs = refs[3 + 3 * n :]
        for k in range(n):
            if k < 3:
                g = o3_ref[k]
                for slot in range(3):
                    g = g + l3_ref[slot, k].astype(F32)
            else:
                g = lg_ref[0, k - 3]
                for slot in range(1, N_DEV):
                    g = g + lg_ref[slot, k - 3]
            d, m2, v2 = _adamw(w_refs[k][0], g, m_refs[k][0], v_refs[k][0])
            for which, val in enumerate((g, d, m2, v2)):
                outs[which * n + k][0] = val

    shapes = [jax.ShapeDtypeStruct(w.shape, F32) for w in weights]
    res = pl.pallas_call(
        body, name="shard_update", out_shape=shapes * 4, compiler_params=pltpu.CompilerParams(vmem_limit_bytes=VMEM_LIMIT)
    )(own_3, land_3, land_g, *weights, *m_in, *v_in)
    return [res[n * k : n * (k + 1)] for k in range(4)]


VEC_LAYOUT = (("b_ada", 3 * D), ("norm_g", D), ("attn_sinks", 128), ("conv_b", D), ("rg_ba", D), ("rg_bx", D), ("rg_lambda", D),
              ("final_g", D), ("conv_w", CONV_WIDTH * D), ("loss", 128))
VEC_OFFSET = {name: sum(w for _, w in VEC_LAYOUT[:k]) for k, (name, _) in enumerate(VEC_LAYOUT)}
VEC_ROWS = 112
VEC_WIDTH = VEC_ROWS * 128
assert sum(w for _, w in VEC_LAYOUT) <= VEC_WIDTH
N_REPLICATED = 8


def _small_update(vec_all, c_all_t, me, rep_w, rep_m, rep_v, ada, conv):
    n_ada = ada[0].shape[1]
    n_conv = conv[0].shape[1]

    def body(*refs):
        me_ref, vec_ref, ct_ref = refs[:3]
        w_refs, m_refs, v_refs = (refs[3 + N_REPLICATED * k : 3 + N_REPLICATED * (k + 1)] for k in range(3))
        pos = 3 + 3 * N_REPLICATED
        ada_refs, conv_refs = refs[pos : pos + 3], refs[pos + 3 : pos + 6]
        outs = refs[pos + 6 :]
        loss_ref, rep_outs = outs[0], outs[1 : 1 + 4 * N_REPLICATED]
        ada_outs, conv_outs = outs[1 + 4 * N_REPLICATED : 5 + 4 * N_REPLICATED], outs[5 + 4 * N_REPLICATED : 9 + 4 * N_REPLICATED]
        conv_buf = outs[9 + 4 * N_REPLICATED]

        total = vec_ref[0:1, :]
        for slot in range(1, N_DEV):
            total = total + vec_ref[slot : slot + 1, :]
        loss_ref[...] = total[:, VEC_OFFSET["loss"] : VEC_OFFSET["loss"] + 1]
        for k, (name, _) in enumerate(VEC_LAYOUT[:N_REPLICATED]):
            width = w_refs[k].shape[1]
            g = total[:, VEC_OFFSET[name] : VEC_OFFSET[name] + width]
            d, m2, v2 = _adamw(w_refs[k][...], g, m_refs[k][...], v_refs[k][...])
            for which, val in enumerate((g, d, m2, v2)):
                rep_outs[which * N_REPLICATED + k][...] = val

        device = me_ref[0]
        for k in range(CONV_WIDTH):
            conv_buf[k : k + 1, :] = total[:, VEC_OFFSET["conv_w"] + D * k : VEC_OFFSET["conv_w"] + D * (k + 1)]
        g_conv = conv_buf[:, pl.ds(pl.multiple_of(device * n_conv, 128), n_conv)]
        for ref, val in zip(conv_outs, (g_conv,) + _adamw(conv_refs[0][...], g_conv, conv_refs[1][...], conv_refs[2][...])):
            ref[...] = val

        dmod = vec_ref[:, pl.ds(pl.multiple_of(device * n_ada, 128), n_ada)]
        g_ada = _dot(ct_ref[...], dmod)
        for ref, val in zip(ada_outs, (g_ada,) + _adamw(ada_refs[0][...], g_ada, ada_refs[1][...], ada_refs[2][...])):
            ref[...] = val

    vmem = pl.BlockSpec(memory_space=pltpu.VMEM)
    rep_shapes = [jax.ShapeDtypeStruct(w.shape, F32) for w in rep_w]
    out_shape = ([jax.ShapeDtypeStruct((1, 1), F32)] + rep_shapes * 4 + [jax.ShapeDtypeStruct(ada[0].shape, F32)] * 4
                 + [jax.ShapeDtypeStruct(conv[0].shape, F32)] * 4)
    n_in = 3 + 3 * N_REPLICATED + 6
    res = pl.pallas_call(
        body,
        name="small_update",
        in_specs=[pl.BlockSpec(memory_space=pltpu.SMEM)] + [vmem] * (n_in - 1),
        out_specs=[vmem] * len(out_shape),
        out_shape=out_shape,
        scratch_shapes=[pltpu.VMEM((CONV_WIDTH, D), F32)],
        compiler_params=pltpu.CompilerParams(vmem_limit_bytes=VMEM_LIMIT),
    )(me, vec_all, c_all_t, *rep_w, *rep_m, *rep_v, *ada, *conv)
    loss = res[0]
    reps = [res[1 + N_REPLICATED * k : 1 + N_REPLICATED * (k + 1)] for k in range(4)]
    return loss, reps, res[1 + 4 * N_REPLICATED : 5 + 4 * N_REPLICATED], res[5 + 4 * N_REPLICATED : 9 + 4 * N_REPLICATED]


def kernel(x, c, positions, w_ada, b_ada, norm_g, w_in, attn_sinks, conv_w, conv_b, rg_wa, rg_ba, rg_wx, rg_bx, rg_lambda, w_attn_proj, w_rnn_proj, w_out, final_g, loss_target, m_w_ada, m_b_ada, m_norm_g, m_w_in, m_attn_sinks, m_conv_w, m_conv_b, m_rg_wa, m_rg_ba, m_rg_wx, m_rg_bx, m_rg_lambda, m_w_attn_proj, m_w_rnn_proj, m_w_out, m_final_g, v_w_ada, v_b_ada, v_norm_g, v_w_in, v_attn_sinks, v_conv_w, v_conv_b, v_rg_wa, v_rg_ba, v_rg_wx, v_rg_bx, v_rg_lambda, v_w_attn_proj, v_w_rnn_proj, v_w_out, v_final_g):
    my_x, my_y, my_c = _position()
    me = _slot((my_x, my_y, my_c))
    x2, target = x[0], loss_target[0]
    pos = positions[0]

    gate_rows = lambda w: w[0].transpose(1, 0, 2).reshape(GATE_SHARD, D)
    w_pack = jnp.concatenate([w_attn_proj[0], w_rnn_proj[0], w_out[0], gate_rows(rg_wa), gate_rows(rg_wx)], axis=0).astype(BF16)
    conv_rows = jnp.pad(conv_w[0], ((0, 8 - CONV_WIDTH), (0, 0)))
    w_in_t = w_in[0].T
    n_ada = w_ada.shape[2]
    b_cols = lax.dynamic_slice(b_ada, (0, me * n_ada), (1, n_ada))

    tables = _rope_tables(pos)
    where = jnp.stack([_chip_of((my_x, my_y)), me]).astype(jnp.int32)
    proj, h, mod, lane_tables, w_t_all, w_all, c_rows, conv_all = _fwd_in(
        x2, jnp.broadcast_to(c, (8, D)), conv_rows, w_ada[0], b_cols, norm_g, w_in_t.astype(BF16), tables, w_pack, where
    )
    c_all = c_rows[:, 0, :]
    conv_full = conv_all[:, :CONV_WIDTH, :].transpose(1, 0, 2).reshape(CONV_WIDTH, D)
    gate = mod[:, 2 * D :]
    o = _attn_fwd(proj, attn_sinks)
    keep = (pos != 0).astype(F32)[:, None]
    hh = _rnn_fwd(proj, keep, w_all, rg_ba, rg_bx, rg_lambda, conv_full, conv_b)

    dres, d_o, d_hh, d_ga, d_gr, d_ma, d_mr, gw3, mid_vec = _mid(x2, target, o, hh, proj, w_all, gate, final_g[None, :])
    part_3 = gw3.reshape(3, 4, 2, ROW_SHARD, D)
    d_q, d_k, d_v, d_sink, sum_3_bf, own_3 = _attn_bwd(proj, o, d_o, (lane_tables, tables), attn_sinks, part_3)
    d_xr, g_gates, rnn_vec, land_3 = _rnn_bwd(proj, hh, d_hh, keep, w_all, rg_ba, rg_bx, rg_lambda, conv_full, conv_b, sum_3_bf)
    dsegs = (d_q, d_k, d_v, d_ga, d_xr, d_gr, d_ma, d_mr)
    own_w, sum_w_bf, land_g = _bwd_gw(dsegs, h, g_gates.reshape(2, 4, N_DEV, GATE_SHARD, 256))
    w_sems, sum_w_thru, land_w_thru, token = _w_exchange_start(sum_w_bf)
    grad_x, dh_vec = _bwd_dh(dsegs, w_t_all, x2, dres, mod + token[0:1, 0:1], norm_g)
    land_w = _w_exchange_wait(w_sems, sum_w_thru, land_w_thru, dh_vec)

    pieces = dict(
        b_ada=jnp.concatenate([dh_vec[0:1], dh_vec[1:2], mid_vec[1:2]], axis=1),
        norm_g=dh_vec[2:3],
        attn_sinks=jnp.pad(d_sink[:, :, 0].reshape(1, 16), ((0, 0), (0, 112))),
        conv_b=rnn_vec[3:4],
        rg_ba=rnn_vec[0:1],
        rg_bx=rnn_vec[1:2],
        rg_lambda=rnn_vec[2:3],
        final_g=mid_vec[0:1],
        conv_w=rnn_vec[4:8].reshape(1, CONV_WIDTH * D),
        loss=mid_vec[2:3, 0:128],
    )
    vec = jnp.concatenate([pieces[name] for name, _ in VEC_LAYOUT], axis=1)
    vec = jnp.pad(vec, ((0, 0), (0, VEC_WIDTH - vec.shape[1]))).reshape(VEC_ROWS, 128)
    (vec_all,) = _all_gather_direct("ag_vec", [vec])
    rep_w = [b_ada, norm_g, attn_sinks, conv_b, rg_ba, rg_bx, rg_lambda, final_g[None, :]]
    rep_m = [m_b_ada, m_norm_g, m_attn_sinks, m_conv_b, m_rg_ba, m_rg_bx, m_rg_lambda, m_final_g[None, :]]
    rep_v = [v_b_ada, v_norm_g, v_attn_sinks, v_conv_b, v_rg_ba, v_rg_bx, v_rg_lambda, v_final_g[None, :]]
    loss, reps, ada_res, conv_res = _small_update(
        vec_all.reshape(N_DEV, VEC_WIDTH), c_all.T, me.astype(jnp.int32).reshape(1), rep_w, rep_m, rep_v,
        (w_ada[0], m_w_ada[0], v_w_ada[0]), (conv_w[0], m_conv_w[0], v_conv_w[0]),
    )

    w_in_res = _w_in_update(own_w, land_w, w_in_t, m_w_in[0].T, v_w_in[0].T)
    shard_res = _shard_update(
        own_3, land_3, land_g,
        (w_attn_proj, w_rnn_proj, w_out, rg_wa, rg_wx),
        (m_w_attn_proj, m_w_rnn_proj, m_w_out, m_rg_wa, m_rg_wx),
        (v_w_attn_proj, v_w_rnn_proj, v_w_out, v_rg_wa, v_rg_wx),
    )

    def by_weight(k):
        b, n, s, cb, ba, bx, lam, fg = reps[k]
        ap, rp, out, wa, wx = shard_res[k]
        return [ada_res[k][None], b, n, w_in_res[k].T[None], s, conv_res[k][None], cb, wa, ba, wx, bx, lam, ap, rp, out, fg[0]]

    return (loss[0, 0], grad_x[None], *by_weight(0), *by_weight(1), *by_weight(2), *by_weight(3))
```

```python
import math

import jax
import jax.numpy as jnp
from jax import lax
from jax.experimental import pallas as pl
from jax.experimental.pallas import tpu as pltpu

F32 = jnp.float32
BF16 = jnp.bfloat16

D = 1024
HEAD_DIM = 64
ROT_DIM = 16
ROPE_THETA = 500000.0
WINDOW = 128
LRU_C = 8.0
CONV_WIDTH = 4
NORM_EPS = 1e-6
IN_W = 6656
SEG_Q, SEG_K, SEG_V, SEG_GA, SEG_XR, SEG_GR, SEG_MA, SEG_MR = 0, 1024, 1280, 1536, 2560, 3584, 4608, 5632
N_DEV = 8
W_IN_SHARD = IN_W // N_DEV
ROW_SHARD = D // N_DEV
GATE_SHARD = 256 // N_DEV

ADAM_LR = 0.001
ADAM_B1 = 0.9
ADAM_B2 = 0.999
ADAM_EPS = 1e-08
ADAM_WD = 0.01
ADAM_STEP = 10

VMEM_LIMIT = 60 * 1024 * 1024
MESH = pl.DeviceIdType.MESH


def _dot(a, b):
    return jnp.dot(a, b, preferred_element_type=F32)


def _dot_nt(a, b):
    return lax.dot_general(a, b, (((1,), (1,)), ((), ())), preferred_element_type=F32)


def _dot_tn(a, b):
    return lax.dot_general(a, b, (((0,), (0,)), ((), ())), preferred_element_type=F32)


def _sigmoid(z):
    return 1.0 / (1.0 + jnp.exp(-z))


def _params(n_grid, vmem=VMEM_LIMIT):
    return pltpu.CompilerParams(dimension_semantics=("arbitrary",) * n_grid, vmem_limit_bytes=vmem)


def _position():
    return lax.axis_index("x"), lax.axis_index("y"), lax.axis_index("c")


def _flip(pos, k):
    x, y, c = pos
    return (1 - x if k & 4 else x, 1 - y if k & 2 else y, 1 - c if k & 1 else c)


def _slot(pos):
    return 4 * pos[0] + 2 * pos[1] + pos[2]


def _all_gather_direct(name, arrs):
    n = len(arrs)

    def body(*refs):
        ins, outs = refs[:n], refs[n : 2 * n]
        send_sems, recv_sems, local_sems = refs[2 * n :]
        me = _position()
        mine = [pltpu.make_async_copy(ins[a], outs[a].at[_slot(me)], local_sems.at[a]) for a in range(n)]
        for cp in mine:
            cp.start()

        def copy(a, k, block):
            return pltpu.make_async_remote_copy(
                src_ref=ins[a],
                dst_ref=outs[a].at[_slot(block)],
                send_sem=send_sems.at[a * 7 + k - 1],
                recv_sem=recv_sems.at[a * 7 + k - 1],
                device_id=_flip(me, k),
                device_id_type=MESH,
            )

        sent = [copy(a, k, me) for a in range(n) for k in range(1, N_DEV)]
        for cp in sent:
            cp.start()
        for a in range(n):
            for k in range(1, N_DEV):
                copy(a, k, _flip(me, k)).wait_recv()
        for cp in sent:
            cp.wait_send()
        for cp in mine:
            cp.wait()

    any_spec = pl.BlockSpec(memory_space=pl.ANY)
    return pl.pallas_call(
        body,
        name=name,
        out_shape=[jax.ShapeDtypeStruct((N_DEV,) + a.shape, a.dtype) for a in arrs],
        in_specs=[any_spec] * n,
        out_specs=[any_spec] * n,
        scratch_shapes=[
            pltpu.SemaphoreType.DMA((7 * n,)),
            pltpu.SemaphoreType.DMA((7 * n,)),
            pltpu.SemaphoreType.DMA((n,)),
        ],
    )(*arrs)


def _direct_gather_copies(src_hbm, all_hbm, send_sems, recv_sems, arriving):
    me = _position()
    return [
        pltpu.make_async_remote_copy(
            src_ref=src_hbm,
            dst_ref=all_hbm.at[_slot(_flip(me, k) if arriving else me)],
            send_sem=send_sems.at[k - 1],
            recv_sem=recv_sems.at[k - 1],
            device_id=_flip(me, k),
            device_id_type=MESH,
        )
        for k in range(1, N_DEV)
    ]


CHIP_FLIPS = (2, 4, 6)


def _chip_of(pos):
    return 2 * pos[0] + pos[1]


def _chip_copies(srcs, lands, views, send_sems, recv_sems):
    me = _position()
    copies = []
    for a, (src, land) in enumerate(zip(srcs, lands)):
        for j, k in enumerate(CHIP_FLIPS):
            peer = _flip(me, k)
            copies.append(
                pltpu.make_async_remote_copy(
                    src_ref=views[a](src, peer),
                    dst_ref=land.at[j],
                    send_sem=send_sems.at[len(CHIP_FLIPS) * a + j],
                    recv_sem=recv_sems.at[len(CHIP_FLIPS) * a + j],
                    device_id=peer,
                    device_id_type=MESH,
                )
            )
    return copies


PACK_GATE = 3 * ROW_SHARD
PACK_ROWS = PACK_GATE + 2 * GATE_SHARD


def _fwd_in(x, c_rows, conv_rows, w_ada, b_cols, norm_g, w_shard, tables, pack, where):
    t = x.shape[0]
    tm = min(1024, t)
    tn = IN_W // 4
    n_i, n_j = t // tm, IN_W // tn
    n_rope = (SEG_V - SEG_Q) // 128

    n_ada = w_ada.shape[1]

    def body(where_ref, x_ref, wada_ref, bcols_ref, ng_ref, ct_ref, st_ref, shard_hbm, pack_hbm, c_hbm, conv_hbm,
             proj_ref, h_ref, mod_ref, lane_hbm, wt_hbm, all_hbm, call_hbm, convall_hbm,
             wall, hbuf, cbuf, mcols, mod_all, mod_buf, lane,
             w_send, w_recv, load_sems, local_sem, lane_sem, p_send, p_recv, p_local,
             c_send, c_recv, c_local, v_send, v_recv, v_local, m_send, m_recv):
        j = pl.program_id(0)
        i = pl.program_id(1)
        me = _position()
        sibling = _flip(me, 1)
        chip = where_ref[0]
        slot = where_ref[1]

        def modulation():
            own_c = pltpu.make_async_copy(c_hbm, call_hbm.at[_slot(me)], c_local)
            own_v = pltpu.make_async_copy(conv_hbm, convall_hbm.at[_slot(me)], v_local)
            own_c.start()
            own_v.start()
            for cp in _direct_gather_copies(c_hbm, call_hbm, c_send, c_recv, arriving=False):
                cp.start()
            for cp in _direct_gather_copies(conv_hbm, convall_hbm, v_send, v_recv, arriving=False):
                cp.start()
            for cp in _direct_gather_copies(c_hbm, call_hbm, c_send, c_recv, arriving=True):
                cp.wait_recv()
            own_c.wait()
            fetch = pltpu.make_async_copy(call_hbm, cbuf, c_local)
            fetch.start()
            fetch.wait()
            mcols[...] = _dot(cbuf[:, 0, :].astype(BF16), wada_ref[...].astype(BF16)) + bcols_ref[...]
            mod_all[slot] = mcols[...]
            swaps = lambda arriving: [
                pltpu.make_async_remote_copy(
                    src_ref=mcols,
                    dst_ref=mod_all.at[_slot(_flip(me, k) if arriving else me)],
                    send_sem=m_send.at[k - 1],
                    recv_sem=m_recv.at[k - 1],
                    device_id=_flip(me, k),
                    device_id_type=MESH,
                )
                for k in range(1, N_DEV)
            ]
            for cp in swaps(False):
                cp.start()
            for cp in swaps(True):
                cp.wait_recv()
            mine = lax.broadcasted_iota(jnp.int32, (N_DEV, n_ada), 0) == slot
            for s in range(N_DEV):
                row = jnp.sum(jnp.where(mine, mod_all[s], 0.0), axis=0, keepdims=True)
                mod_buf[:, n_ada * s : n_ada * (s + 1)] = row
            for cp in swaps(False):
                cp.wait_send()
            for cp in _direct_gather_copies(c_hbm, call_hbm, c_send, c_recv, arriving=False):
                cp.wait_send()
            for cp in _direct_gather_copies(conv_hbm, convall_hbm, v_send, v_recv, arriving=True):
                cp.wait_recv()
            for cp in _direct_gather_copies(conv_hbm, convall_hbm, v_send, v_recv, arriving=False):
                cp.wait_send()
            own_v.wait()

        def shard_of(block):
            return wall.at[_chip_of(block), pl.ds(W_IN_SHARD * block[2], W_IN_SHARD), :]

        def shard_copy(k, block, to, src=None):
            return pltpu.make_async_remote_copy(
                src_ref=shard_of(block),
                dst_ref=shard_of(block),
                send_sem=w_send.at[k],
                recv_sem=w_recv.at[k],
                device_id=to,
                device_id_type=MESH,
            )

        own_shard = pltpu.make_async_copy(shard_hbm, shard_of(me), local_sem)
        own_pack = pltpu.make_async_copy(pack_hbm, all_hbm.at[_slot(me)], p_local)
        keep_lane = pltpu.make_async_copy(lane, lane_hbm, lane_sem)

        def lane_tables():
            d = lax.broadcasted_iota(jnp.int32, (128, 128), 0) % HEAD_DIM
            half = ROT_DIM // 2

            def fill(b, carry):
                cols = pl.ds(pl.multiple_of(128 * b, 128), 128)
                cos = jnp.concatenate([ct_ref[:, cols]] * (128 // half), axis=0)
                sin = jnp.concatenate([st_ref[:, cols]] * (128 // half), axis=0)
                lane[0, cols, :] = jnp.where(d < ROT_DIM, cos, 1.0).T
                lane[1, cols, :] = jnp.where(d < half, -sin, 0.0).T
                lane[2, cols, :] = jnp.where((d >= half) & (d < ROT_DIM), sin, 0.0).T
                return carry

            lax.fori_loop(0, t // 128, fill, 0)

        def keep_tile(q):
            return [
                pltpu.make_async_copy(wall.at[q, pl.ds(W_IN_SHARD * c, W_IN_SHARD), :], wt_hbm.at[2 * q + c], load_sems.at[2 * (q ^ chip) + c])
                for c in range(2)
            ]

        @pl.when((j == 0) & (i == 0))
        def _():
            modulation()
            mod_ref[...] = mod_buf[...]
            own_shard.start()
            own_shard.wait()
            shard_copy(0, me, sibling).start()
            for rel in range(1, 4):
                shard_copy(rel, me, _flip(me, 2 * rel)).start()
            own_pack.start()
            for cp in _direct_gather_copies(pack_hbm, all_hbm, p_send, p_recv, arriving=False):
                cp.start()
            lane_tables()
            keep_lane.start()
            shard_copy(0, sibling, me).wait_recv()
            for cp in keep_tile(chip):
                cp.start()

        for rel in range(1, 4):

            @pl.when((j == 1) & (i == min(rel, n_i) - 1))
            def _(rel=rel):
                there = _flip(me, 2 * rel)
                shard_copy(rel, there, me).wait_recv()
                shard_copy(3 + rel, there, sibling).start()

            @pl.when((j == rel) & (i == 0))
            def _(rel=rel):
                shard_copy(3 + rel, _flip(sibling, 2 * rel), me).wait_recv()
                for cp in keep_tile(chip ^ rel):
                    cp.start()

        rows = pl.ds(pl.multiple_of(i * tm, tm), tm)

        @pl.when(j == 0)
        def _():
            xv = x_ref[...]
            r = lax.rsqrt(jnp.mean(xv * xv, axis=-1, keepdims=True) + NORM_EPS)
            shift = mod_buf[:, 0:D]
            scale = mod_buf[:, D : 2 * D]
            hv = (((xv * r) * ng_ref[...]) * (1.0 + scale) + shift).astype(BF16)
            hbuf[rows, :] = hv
            h_ref[...] = hv

        proj_ref[...] = _dot_nt(hbuf[rows, :], wall[chip ^ j])

        @pl.when((chip ^ j) == 0)
        def _():
            cos, s1, s2 = lane[0, rows, :], lane[1, rows, :], lane[2, rows, :]
            for g in range(n_rope):
                cols = slice(128 * g, 128 * (g + 1))
                proj_ref[:, cols] = _rope(proj_ref[:, cols], cos, s1, s2)

        @pl.when((j == n_j - 1) & (i == n_i - 1))
        def _():
            shard_copy(0, me, sibling).wait_send()
            for rel in range(1, 4):
                shard_copy(rel, me, _flip(me, 2 * rel)).wait_send()
                shard_copy(3 + rel, _flip(me, 2 * rel), sibling).wait_send()
            for q in range(N_DEV // 2):
                for cp in keep_tile(q):
                    cp.wait()
            for cp in _direct_gather_copies(pack_hbm, all_hbm, p_send, p_recv, arriving=True):
                cp.wait_recv()
            for cp in _direct_gather_copies(pack_hbm, all_hbm, p_send, p_recv, arriving=False):
                cp.wait_send()
            own_pack.wait()
            keep_lane.wait()

    tab = pl.BlockSpec((ROT_DIM // 2, t), lambda j, i, c: (0, 0))
    any_spec = pl.BlockSpec(memory_space=pl.ANY)
    return pl.pallas_call(
        body,
        name="fwd_in",
        grid_spec=pltpu.PrefetchScalarGridSpec(
            num_scalar_prefetch=1,
            grid=(n_j, n_i),
            in_specs=[
                pl.BlockSpec((tm, D), lambda j, i, c: (jnp.where(j == 0, i, 0), 0)),
                pl.BlockSpec(w_ada.shape, lambda j, i, c: (0, 0)),
                pl.BlockSpec((1, n_ada), lambda j, i, c: (0, 0)),
                pl.BlockSpec((1, D), lambda j, i, c: (0, 0)),
                tab, tab, any_spec, any_spec, any_spec, any_spec,
            ],
            out_specs=[pl.BlockSpec((tm, tn), lambda j, i, c: (i, c[0] ^ j)),
                       pl.BlockSpec((tm, D), lambda j, i, c: (jnp.where(j == 0, i, n_i - 1), 0)),
                       pl.BlockSpec((1, 3 * D), lambda j, i, c: (0, 0)),
                       any_spec, any_spec, any_spec, any_spec, any_spec],
            scratch_shapes=[
                pltpu.VMEM((N_DEV // 2, tn, D), BF16),
                pltpu.VMEM((t, D), BF16),
                pltpu.VMEM((N_DEV,) + c_rows.shape, F32),
                pltpu.VMEM((N_DEV, n_ada), F32),
                pltpu.VMEM((N_DEV, N_DEV, n_ada), F32),
                pltpu.VMEM((1, 3 * D), F32),
                pltpu.VMEM((3, t, 128), F32),
                pltpu.SemaphoreType.DMA((N_DEV - 1,)),
                pltpu.SemaphoreType.DMA((N_DEV - 1,)),
                pltpu.SemaphoreType.DMA((N_DEV,)),
                pltpu.SemaphoreType.DMA,
                pltpu.SemaphoreType.DMA,
                pltpu.SemaphoreType.DMA((N_DEV - 1,)),
                pltpu.SemaphoreType.DMA((N_DEV - 1,)),
                pltpu.SemaphoreType.DMA,
                pltpu.SemaphoreType.DMA((N_DEV - 1,)),
                pltpu.SemaphoreType.DMA((N_DEV - 1,)),
                pltpu.SemaphoreType.DMA,
                pltpu.SemaphoreType.DMA((N_DEV - 1,)),
                pltpu.SemaphoreType.DMA((N_DEV - 1,)),
                pltpu.SemaphoreType.DMA,
                pltpu.SemaphoreType.DMA((N_DEV - 1,)),
                pltpu.SemaphoreType.DMA((N_DEV - 1,)),
            ],
        ),
        out_shape=[jax.ShapeDtypeStruct((t, IN_W), F32), jax.ShapeDtypeStruct((t, D), BF16),
                   jax.ShapeDtypeStruct((1, 3 * D), F32),
                   jax.ShapeDtypeStruct((3, t, 128), F32),
                   jax.ShapeDtypeStruct((N_DEV,) + w_shard.shape, w_shard.dtype),
                   jax.ShapeDtypeStruct((N_DEV,) + pack.shape, pack.dtype),
                   jax.ShapeDtypeStruct((N_DEV,) + c_rows.shape, F32),
                   jax.ShapeDtypeStruct((N_DEV,) + conv_rows.shape, F32)],
        compiler_params=_params(2),
    )(where, x, w_ada, b_cols, norm_g, *tables, w_shard, pack, c_rows, conv_rows)


def _rope_tables(positions):
    inv_freq = ROPE_THETA ** (-jnp.arange(0, ROT_DIM, 2, dtype=F32) / ROT_DIM)
    ang_t = inv_freq[:, None] * positions.astype(F32)[None, :]
    return jnp.cos(ang_t), jnp.sin(ang_t)


def _rope(tv, cos, s1, s2):
    cols = []
    for g in range(tv.shape[1] // 128):
        blk = tv[:, 128 * g : 128 * (g + 1)]
        cols.append(blk * cos + pltpu.roll(blk, 128 - 8, 1) * s1 + pltpu.roll(blk, 8, 1) * s2)
    return cols[0] if len(cols) == 1 else jnp.concatenate(cols, axis=1)


def _rope_transposed(gv, cos, s1, s2):
    cols = []
    for g in range(gv.shape[1] // 128):
        blk = gv[:, 128 * g : 128 * (g + 1)]
        cols.append(blk * cos + pltpu.roll(blk * s1, 8, 1) + pltpu.roll(blk * s2, 128 - 8, 1))
    return cols[0] if len(cols) == 1 else jnp.concatenate(cols, axis=1)


def _lane_lo():
    return lax.broadcasted_iota(jnp.int32, (1, 128), 1) < HEAD_DIM


def _row_lo():
    return lax.broadcasted_iota(jnp.int32, (128, 1), 0) < HEAD_DIM


def _split_heads(tile, j):
    lo = _lane_lo()
    parts = []
    for g in (2 * j, 2 * j + 1):
        blk = tile[:, 128 * g : 128 * (g + 1)]
        parts += [jnp.where(lo, blk, 0.0), jnp.where(lo, 0.0, blk)]
    return jnp.concatenate(parts, axis=0)


def _dup_lanes(kv, j):
    rolled = pltpu.roll(kv, HEAD_DIM, 1)
    lo = _lane_lo()
    return jnp.where(lo, kv, rolled) if j == 0 else jnp.where(lo, rolled, kv)


def _fold_lanes(acc):
    return acc + pltpu.roll(acc, HEAD_DIM, 1)


def _fill_bias_t(bias_ref):
    kj = lax.broadcasted_iota(jnp.int32, (256, 128), 0)
    qi = lax.broadcasted_iota(jnp.int32, (256, 128), 1)
    diff = qi + WINDOW - kj
    band = (diff >= 0) & (diff < WINDOW)
    bias_ref[0] = jnp.where(band, 0.0, -1e30)
    bias_ref[1] = jnp.where(band & (kj >= 128), 0.0, -1e30)


def _sink_row(sink_ref, pair, j):
    return jnp.concatenate([jnp.full((1, 128), sink_ref[0, 8 * pair + 4 * j + h], F32) for h in range(4)], axis=1)


def _softmax_t(q4, kw, bias4, sink):
    s = _dot_nt(kw, q4) + bias4
    m = jnp.maximum(jnp.max(s, axis=0, keepdims=True), sink)
    p = jnp.exp(s - m)
    e_sink = jnp.exp(sink - m)
    denom = jnp.sum(p, axis=0, keepdims=True) + e_sink
    return p, e_sink, denom


def _attn_specs(t, tq):
    return dict(
        q=pl.BlockSpec((tq, 512), lambda p, i: (i, p)),
        k=pl.BlockSpec((t, 128), lambda p, i: (0, SEG_K // 128 + p)),
        v=pl.BlockSpec((t, 128), lambda p, i: (0, SEG_V // 128 + p)),
        sink=pl.BlockSpec(memory_space=pltpu.SMEM),
        act=pl.BlockSpec((tq, 512), lambda p, i: (i, p)),
    )


def _attn_fwd(proj, sinks):
    t = proj.shape[0]
    tq = min(512, t)
    nb = tq // 128
    n_tiles = t // tq
    sp = _attn_specs(t, tq)

    def body(q_ref, k_ref, v_ref, sink_ref, o_ref, kd, vdt, bias_ref):
        pair = pl.program_id(0)
        i = pl.program_id(1)

        @pl.when(i == 0)
        def _():
            kv, vv = k_ref[...], v_ref[...]
            for j in range(2):
                kd[j, 0:128, :] = jnp.zeros((128, 128), BF16)
                kd[j, 128:, :] = _dup_lanes(kv, j).astype(BF16)
                vdt[j, :, 0:128] = jnp.zeros((128, 128), BF16)
                vdt[j, :, 128:] = _dup_lanes(vv, j).T.astype(BF16)
            _fill_bias_t(bias_ref)

        row_lo = _row_lo()

        def block(b, carry):
            r0 = pl.multiple_of(b * 128, 128)
            gb = i * nb + b
            keys = pl.ds(pl.multiple_of(gb * 128, 128), 256)
            rows = pl.ds(r0, 128)
            qt = q_ref[rows, :] * (1.0 / math.sqrt(HEAD_DIM))
            bias = bias_ref[jnp.where(gb == 0, 1, 0)]
            bias4 = jnp.concatenate([bias] * 4, axis=1)
            cols = []
            for j in range(2):
                q4 = _split_heads(qt, j).astype(BF16)
                p, _, denom = _softmax_t(q4, kd[j, keys, :], bias4, _sink_row(sink_ref, pair, j))
                ot = _dot(vdt[j, :, keys], p.astype(BF16)) * (1.0 / denom)
                for gi in range(2):
                    z = jnp.where(row_lo, ot[:, 256 * gi : 256 * gi + 128], ot[:, 256 * gi + 128 : 256 * gi + 256])
                    cols.append(z.T)
            o_ref[rows, :] = jnp.concatenate(cols, axis=1).astype(BF16)
            return carry

        lax.fori_loop(0, nb, block, 0, unroll=True)

    return pl.pallas_call(
        body,
        name="attn_fwd",
        grid=(2, n_tiles),
        in_specs=[sp["q"], sp["k"], sp["v"], sp["sink"]],
        out_specs=sp["act"],
        out_shape=jax.ShapeDtypeStruct((t, D), BF16),
        scratch_shapes=[pltpu.VMEM((2, t + 128, 128), BF16), pltpu.VMEM((2, 128, t + 128), BF16), pltpu.VMEM((2, 256, 128), F32)],
        compiler_params=_params(2),
    )(proj, proj, proj, sinks)


def _attn_bwd(proj, o, d_o, tables, sinks, part_3):
    t = proj.shape[0]
    tq = min(512, t)
    nb = tq // 128
    n_tiles = t // tq
    lane_tables, (cos_t, sin_t) = tables
    sp = _attn_specs(t, tq)
    scale = 1.0 / math.sqrt(HEAD_DIM)
    half = ROT_DIM // 2

    def unrope_rows(z, c, s):
        out = []
        for base in (0, HEAD_DIM):
            g0, g1 = z[base : base + half], z[base + half : base + ROT_DIM]
            out += [g0 * c + g1 * s, g1 * c - g0 * s, z[base + ROT_DIM : base + HEAD_DIM]]
        return jnp.concatenate(out, axis=0)

    def swap(part_hbm, land, send_sem, recv_sem):
        sibling = _flip(_position(), 1)
        return pltpu.make_async_remote_copy(
            src_ref=part_hbm.at[:, :, sibling[2]], dst_ref=land, send_sem=send_sem, recv_sem=recv_sem,
            device_id=sibling, device_id_type=MESH,
        )

    def body(q_ref, k_ref, v_ref, o_ref, do_ref, ct_ref, st_ref, ck, s1k, s2k, sink_ref, part_hbm,
             dq_ref, dk_ref, dv_ref, dsink_ref, sum_hbm, mine_hbm, kd, kdt, vd, bias_ref, dkacc, dvacc, own, land, stage,
             send_sem, recv_sem, local_sems):
        pair = pl.program_id(0)
        i = pl.program_id(1)
        me = _position()
        load_own = pltpu.make_async_copy(part_hbm.at[:, :, me[2]], own, local_sems.at[0])

        @pl.when((pair == 0) & (i == 0))
        def _():
            swap(part_hbm, land, send_sem, recv_sem).start()
            load_own.start()

        @pl.when(i == 0)
        def _():
            kv, vv = k_ref[...], v_ref[...]
            for j in range(2):
                kdup = _dup_lanes(kv, j)
                kd[j, 0:128, :] = jnp.zeros((128, 128), BF16)
                kd[j, 128:, :] = kdup.astype(BF16)
                kdt[j, :, 0:128] = jnp.zeros((128, 128), BF16)
                kdt[j, :, 128:] = kdup.T.astype(BF16)
                vd[j, 0:128, :] = jnp.zeros((128, 128), BF16)
                vd[j, 128:, :] = _dup_lanes(vv, j).astype(BF16)
            _fill_bias_t(bias_ref)
            dkacc[...] = jnp.zeros_like(dkacc)
            dvacc[...] = jnp.zeros_like(dvacc)
            dsink_ref[...] = jnp.zeros_like(dsink_ref)

        row_lo = _row_lo()
        ones = jnp.ones((8, 128), BF16)

        def block(b, carry):
            r0 = pl.multiple_of(b * 128, 128)
            gb = i * nb + b
            keys = pl.ds(pl.multiple_of(gb * 128, 128), 256)
            rows = pl.ds(r0, 128)
            qt = q_ref[rows, :] * scale
            dob = do_ref[rows, :].astype(F32)
            prod = dob * o_ref[rows, :].astype(F32)
            ct, st = ct_ref[:, rows], st_ref[:, rows]
            bias = bias_ref[jnp.where(gb == 0, 1, 0)]
            bias4 = jnp.concatenate([bias] * 4, axis=1)
            cols = []
            for j in range(2):
                q4 = _split_heads(qt, j).astype(BF16)
                kw = kd[j, keys, :]
                p, e_sink, denom = _softmax_t(q4, kw, bias4, _sink_row(sink_ref, pair, j))
                inv = 1.0 / denom
                prob = p * inv
                do4 = _split_heads(dob, j).astype(BF16)
                delta = _dot_nt(ones, _split_heads(prod, j).astype(BF16))[0:1, :]
                dp = _dot_nt(vd[j, keys, :], do4)
                ds = (prob * (dp - delta)).astype(BF16)
                dkacc[j, keys, :] = dkacc[j, keys, :] + _dot(ds, q4)
                dvacc[j, keys, :] = dvacc[j, keys, :] + _dot(prob.astype(BF16), do4)
                dqt = _dot(kdt[j, :, keys], ds) * scale
                dsink = -(e_sink * inv) * delta
                for h in range(4):
                    row = 4 * j + h
                    total = jnp.sum(dsink[:, 128 * h : 128 * (h + 1)], axis=1, keepdims=True)
                    dsink_ref[0, row : row + 1, :] = dsink_ref[0, row : row + 1, :] + jnp.broadcast_to(total, (1, 128))
                for gi in range(2):
                    z = jnp.where(row_lo, dqt[:, 256 * gi : 256 * gi + 128], dqt[:, 256 * gi + 128 : 256 * gi + 256])
                    cols.append(unrope_rows(z, ct, st).T)
            dq_ref[rows, :] = jnp.concatenate(cols, axis=1).astype(BF16)
            return carry

        lax.fori_loop(0, nb, block, 0, unroll=True)

        @pl.when(i == n_tiles - 1)
        def _():
            lo = _lane_lo()
            dk = jnp.where(lo, _fold_lanes(dkacc[0, 128:, :]), _fold_lanes(dkacc[1, 128:, :]))
            dk_ref[...] = _rope_transposed(dk, ck[...], s1k[...], s2k[...]).astype(BF16)
            dv_ref[...] = jnp.where(lo, _fold_lanes(dvacc[0, 128:, :]), _fold_lanes(dvacc[1, 128:, :])).astype(BF16)

        @pl.when((pair == 1) & (i == n_tiles - 1))
        def _():
            swap(part_hbm, land, send_sem, recv_sem).wait()
            load_own.wait()
            own[...] = own[...] + land[...]
            stage[...] = own[...].astype(BF16)
            results = [pltpu.make_async_copy(stage, sum_hbm, local_sems.at[1]),
                       pltpu.make_async_copy(own.at[:, _chip_of(me)], mine_hbm, local_sems.at[2])]
            for cp in results:
                cp.start()
            for cp in results:
                cp.wait()

    kv_out = pl.BlockSpec((t, 128), lambda p, i: (0, p))
    tab_t = pl.BlockSpec((half, tq), lambda p, i: (0, i))
    tab_k = [pl.BlockSpec((None, t, 128), lambda p, i, a=a: (a, 0, 0)) for a in range(3)]
    return pl.pallas_call(
        body,
        name="attn_bwd",
        grid=(2, n_tiles),
        in_specs=[sp["q"], sp["k"], sp["v"], sp["act"], sp["act"], tab_t, tab_t, *tab_k, sp["sink"],
                  pl.BlockSpec(memory_space=pl.ANY)],
        out_specs=[sp["act"], kv_out, kv_out, pl.BlockSpec((1, 8, 128), lambda p, i: (p, 0, 0)),
                   pl.BlockSpec(memory_space=pl.ANY), pl.BlockSpec(memory_space=pl.ANY)],
        out_shape=[
            jax.ShapeDtypeStruct((t, D), BF16),
            jax.ShapeDtypeStruct((t, 256), BF16),
            jax.ShapeDtypeStruct((t, 256), BF16),
            jax.ShapeDtypeStruct((2, 8, 128), F32),
            jax.ShapeDtypeStruct((3, 4, ROW_SHARD, D), BF16),
            jax.ShapeDtypeStruct((3, ROW_SHARD, D), F32),
        ],
        scratch_shapes=[pltpu.VMEM((2, t + 128, 128), BF16), pltpu.VMEM((2, 128, t + 128), BF16), pltpu.VMEM((2, t + 128, 128), BF16),
                        pltpu.VMEM((2, 256, 128), F32), pltpu.VMEM((2, t + 128, 128), F32), pltpu.VMEM((2, t + 128, 128), F32),
                        pltpu.VMEM((3, 4, ROW_SHARD, D), F32), pltpu.VMEM((3, 4, ROW_SHARD, D), F32), pltpu.VMEM((3, 4, ROW_SHARD, D), BF16),
                        pltpu.SemaphoreType.DMA, pltpu.SemaphoreType.DMA, pltpu.SemaphoreType.DMA((3,))],
        compiler_params=_params(2),
    )(proj, proj, proj, o, d_o, cos_t, sin_t, lane_tables, lane_tables, lane_tables, sinks, part_3)


HALF = 512


def _softplus(y):
    u = jnp.exp(-jnp.abs(y))
    w = 1.0 + u
    log1p = jnp.where(w == 1.0, u, jnp.log(w) * (u / jnp.where(w == 1.0, 1.0, w - 1.0)))
    return jnp.maximum(y, 0.0) + log1p


def _gate_block(w_ref, blk):
    return w_ref[:, :, 256 * blk : 256 * (blk + 1)].reshape(256, 256)


def _gate_matmul(act_bf, w_ref):
    return jnp.concatenate([_dot(act_bf[:, 256 * blk : 256 * (blk + 1)], _gate_block(w_ref, blk)) for blk in range(2)], axis=1)


def _gate_matmul_t(grad_bf, w_ref):
    return jnp.concatenate([_dot_nt(grad_bf[:, 256 * blk : 256 * (blk + 1)], _gate_block(w_ref, blk)) for blk in range(2)], axis=1)


def _conv_and_gates(xe_ref, tm, cw_ref, cb_ref, wa_ref, wx_ref, ba_ref, bx_ref, lam_ref, keep):
    xc = cb_ref[...] + sum(cw_ref[k : k + 1, :] * xe_ref[pl.ds(5 + k, tm), :] for k in range(CONV_WIDTH))
    xc_bf = xc.astype(BF16)
    r = _sigmoid(_gate_matmul(xc_bf, wa_ref) + ba_ref[...])
    ig = _sigmoid(_gate_matmul(xc_bf, wx_ref) + bx_ref[...])
    sp = _softplus(-lam_ref[...])
    log_a = -LRU_C * r * sp
    tanh = jnp.tanh(-log_a)
    mult = jnp.sqrt(2.0 * tanh / (1.0 + tanh))
    a = jnp.where(keep, jnp.exp(log_a), 0.0)
    mult = jnp.where(keep, mult, 1.0)
    return xc, xc_bf, r, ig, sp, a, mult


def _scan_rows(a_ref, b_ref, h_ref, carry_ref, tm, reverse):
    width = a_ref.shape[1]
    row = lax.broadcasted_iota(jnp.int32, (8, width), 0)

    def group(g, carry):
        gg = tm // 8 - 1 - g if reverse else g
        r0 = pl.multiple_of(gg * 8, 8)
        a = a_ref[pl.ds(r0, 8), :]
        b = b_ref[pl.ds(r0, 8), :]
        for s in (1, 2, 4):
            shift = 8 - s if reverse else s
            a_sh = pltpu.roll(a, shift, 0)
            b_sh = pltpu.roll(b, shift, 0)
            ok = (row < 8 - s) if reverse else (row >= s)
            b = jnp.where(ok, a * b_sh + b, b)
            a = jnp.where(ok, a * a_sh, a)
        h = a * carry + b
        h_ref[pl.ds(r0, 8), :] = h
        last = h[0:1, :] if reverse else h[7:8, :]
        return jnp.broadcast_to(last, (8, width))

    carry_ref[...] = lax.fori_loop(0, tm // 8, group, carry_ref[...])


def _rnn_specs(t, tm, order):
    n = t // tm
    return dict(
        tile=pl.BlockSpec((tm, HALF), lambda ch, i: (order(i, n), ch)),
        before=lambda col0: pl.BlockSpec((8, HALF), lambda ch, i: (jnp.maximum(order(i, n) * (tm // 8) - 1, 0), col0 + ch)),
        proj_tile=lambda col0: pl.BlockSpec((tm, HALF), lambda ch, i: (order(i, n), col0 + ch)),
        keep=pl.BlockSpec((tm, 1), lambda ch, i: (order(i, n), 0)),
        gates=lambda which: pl.BlockSpec((N_DEV, GATE_SHARD, HALF), lambda ch, i: (0, PACK_GATE // GATE_SHARD + which, ch)),
        gate_grads=pl.BlockSpec((2, 2, 256, 256), lambda ch, i: (0, ch, 0, 0)),
        vec=pl.BlockSpec((1, HALF), lambda ch, i: (0, ch)),
        conv=pl.BlockSpec((CONV_WIDTH, HALF), lambda ch, i: (0, ch)),
    )


def _rnn_fwd(proj, keep, w_all, rg_ba, rg_bx, rg_lambda, conv_w, conv_b):
    t = proj.shape[0]
    tm = min(512, t)
    sp = _rnn_specs(t, tm, lambda i, n: i)

    def body(xr_ref, xb_ref, keep_ref, wa_ref, wx_ref, ba_ref, bx_ref, lam_ref, cw_ref, cb_ref, hh_ref, xe, a_buf, b_buf, carry):
        i = pl.program_id(1)

        @pl.when(i == 0)
        def _():
            carry[...] = jnp.zeros_like(carry)

        xe[0:8, :] = jnp.where(i > 0, xb_ref[...], 0.0)
        xe[8:, :] = xr_ref[...]
        keep_rows = keep_ref[...] != 0.0
        xc, _, _, ig, _, a, mult = _conv_and_gates(xe, tm, cw_ref, cb_ref, wa_ref, wx_ref, ba_ref, bx_ref, lam_ref, keep_rows)
        a_buf[...] = a
        b_buf[...] = mult * (ig * xc)
        _scan_rows(a_buf, b_buf, hh_ref, carry, tm, reverse=False)

    return pl.pallas_call(
        body,
        name="rnn_fwd",
        grid=(2, t // tm),
        in_specs=[sp["proj_tile"](SEG_XR // HALF), sp["before"](SEG_XR // HALF), sp["keep"], sp["gates"](0), sp["gates"](1),
                  sp["vec"], sp["vec"], sp["vec"], sp["conv"], sp["vec"]],
        out_specs=sp["tile"],
        out_shape=jax.ShapeDtypeStruct((t, D), F32),
        scratch_shapes=[pltpu.VMEM((tm + 8, HALF), F32), pltpu.VMEM((tm, HALF), F32), pltpu.VMEM((tm, HALF), F32), pltpu.VMEM((8, HALF), F32)],
        compiler_params=_params(2),
    )(proj, proj, keep, w_all, w_all, rg_ba, rg_bx, rg_lambda, conv_w, conv_b)


def _rnn_bwd(proj, hh, d_hh, keep, w_all, rg_ba, rg_bx, rg_lambda, conv_w, conv_b, sum_3_bf):
    t = proj.shape[0]
    tm = min(512, t)
    n_tiles = t // tm
    sp = _rnn_specs(t, tm, lambda i, n: n - 1 - i)

    def body(xr_ref, xb_ref, hh_ref, hb_ref, dhh_ref, keep_ref, wa_ref, wx_ref, ba_ref, bx_ref, lam_ref, cw_ref, cb_ref,
             s3_hbm, dxr_ref, gw_ref, gvec_ref, l3_hbm, xe, he, ae, c_buf, g_buf, dxe, carry, a_next, send_sems, recv_sems):
        ch = pl.program_id(0)
        i = pl.program_id(1)
        first_tile = i == n_tiles - 1
        views = [lambda ref, peer: ref.at[:, _chip_of(peer)]]

        @pl.when((ch == 0) & (i == 0))
        def _():
            for cp in _chip_copies([s3_hbm], [l3_hbm], views, send_sems, recv_sems):
                cp.start()

        @pl.when((ch == 1) & (i == n_tiles - 1))
        def _():
            for cp in _chip_copies([s3_hbm], [l3_hbm], views, send_sems, recv_sems):
                cp.wait_recv()
                cp.wait_send()

        @pl.when(i == 0)
        def _():
            carry[...] = jnp.zeros_like(carry)
            a_next[...] = jnp.zeros_like(a_next)
            dxe[tm:, :] = jnp.zeros((8, HALF), F32)
            gw_ref[...] = jnp.zeros_like(gw_ref)
            gvec_ref[...] = jnp.zeros_like(gvec_ref)

        xe[0:8, :] = jnp.where(first_tile, 0.0, xb_ref[...])
        xe[8:, :] = xr_ref[...]
        he[0:8, :] = jnp.where(first_tile, 0.0, hb_ref[...])
        he[8:, :] = hh_ref[...]
        keep_rows = keep_ref[...] != 0.0
        xc, xc_bf, r, ig, sp_, a, mult = _conv_and_gates(xe, tm, cw_ref, cb_ref, wa_ref, wx_ref, ba_ref, bx_ref, lam_ref, keep_rows)

        ae[0:tm, :] = a
        ae[tm:, :] = a_next[...]
        a_next[...] = jnp.broadcast_to(a[0:1, :], (8, HALF))
        c_buf[...] = ae[pl.ds(1, tm), :]
        g_buf[...] = dhh_ref[...]
        _scan_rows(c_buf, g_buf, g_buf, carry, tm, reverse=True)
        g = g_buf[...]

        h_prev = he[pl.ds(7, tm), :]
        bb = ig * xc
        d_ig = g * mult * xc
        d_xc = g * mult * ig
        d_log_a = jnp.where(keep_rows, g * h_prev * a - (g * bb) * (a * a) / mult, 0.0)
        d_r = d_log_a * (-LRU_C * sp_)
        d_sp = jnp.sum(d_log_a * (-LRU_C * r), axis=0, keepdims=True)
        d_zr = d_r * r * (1.0 - r)
        d_zi = d_ig * ig * (1.0 - ig)
        d_zr_bf = d_zr.astype(BF16)
        d_zi_bf = d_zi.astype(BF16)
        d_xc = d_xc + _gate_matmul_t(d_zr_bf, wa_ref) + _gate_matmul_t(d_zi_bf, wx_ref)
        for blk in range(2):
            cols = slice(256 * blk, 256 * (blk + 1))
            gw_ref[0, blk] = gw_ref[0, blk] + _dot_tn(xc_bf[:, cols], d_zr_bf[:, cols])
            gw_ref[1, blk] = gw_ref[1, blk] + _dot_tn(xc_bf[:, cols], d_zi_bf[:, cols])

        dxe[0:tm, :] = d_xc
        shifted = [dxe[pl.ds(3 - k, tm), :] for k in range(CONV_WIDTH)]
        dxr = sum(cw_ref[k : k + 1, :] * shifted[k] for k in range(CONV_WIDTH))
        dxr_ref[...] = dxr.astype(BF16)
        dxe[tm:, :] = d_xc[0:8, :]

        lam = lam_ref[...]
        xr_tile = xr_ref[...]
        sums = [
            jnp.sum(d_zr, axis=0, keepdims=True),
            jnp.sum(d_zi, axis=0, keepdims=True),
            d_sp * (-_sigmoid(-lam)),
            jnp.sum(d_xc, axis=0, keepdims=True),
        ] + [jnp.sum(xr_tile * shifted[k], axis=0, keepdims=True) for k in range(CONV_WIDTH)]
        gvec_ref[...] = gvec_ref[...] + jnp.concatenate(sums, axis=0)

    return pl.pallas_call(
        body,
        name="rnn_bwd",
        grid=(2, n_tiles),
        in_specs=[
            sp["proj_tile"](SEG_XR // HALF), sp["before"](SEG_XR // HALF), sp["tile"], sp["before"](0), sp["tile"], sp["keep"],
            sp["gates"](0), sp["gates"](1), sp["vec"], sp["vec"], sp["vec"], sp["conv"], sp["vec"], pl.BlockSpec(memory_space=pl.ANY),
        ],
        out_specs=[sp["tile"], sp["gate_grads"], pl.BlockSpec((8, HALF), lambda ch, i: (0, ch)), pl.BlockSpec(memory_space=pl.ANY)],
        out_shape=[
            jax.ShapeDtypeStruct((t, D), BF16),
            jax.ShapeDtypeStruct((2, 4, 256, 256), F32),
            jax.ShapeDtypeStruct((8, D), F32),
            jax.ShapeDtypeStruct((len(CHIP_FLIPS), 3, ROW_SHARD, D), BF16),
        ],
        scratch_shapes=[
            pltpu.VMEM((tm + 8, HALF), F32),
            pltpu.VMEM((tm + 8, HALF), F32),
            pltpu.VMEM((tm + 8, HALF), F32),
            pltpu.VMEM((tm, HALF), F32),
            pltpu.VMEM((tm, HALF), F32),
            pltpu.VMEM((tm + 8, HALF), F32),
            pltpu.VMEM((8, HALF), F32),
            pltpu.VMEM((8, HALF), F32),
            pltpu.SemaphoreType.DMA((len(CHIP_FLIPS),)),
            pltpu.SemaphoreType.DMA((len(CHIP_FLIPS),)),
        ],
        compiler_params=_params(2),
    )(proj, proj, hh, hh, d_hh, keep, w_all, w_all, rg_ba, rg_bx, rg_lambda, conv_w, conv_b, sum_3_bf)


def _mid(x, target, o, hh, proj, w_all, gate, final_g):
    t = x.shape[0]
    tm = min(256, t)
    n_tiles = t // tm
    full = pl.BlockSpec((tm, D), lambda i: (i, 0))
    half = lambda col: pl.BlockSpec((tm, 512), lambda i: (i, col))
    vec = pl.BlockSpec((1, D), lambda i: (0, 0))
    any_spec = pl.BlockSpec(memory_space=pl.ANY)

    def body(x_ref, tg_ref, o_ref, hh_ref, ga0, ga1, gr0, gr1, ma0, ma1, mr0, mr1, w_hbm, gate_ref, fg_ref,
             dres_ref, do_ref, dhh_ref, dga_ref, dgr_ref, dma_ref, dmr_ref, gw_hbm, vec_ref, w, gw, sem, w_sems):
        i = pl.program_id(0)

        @pl.when(i == 0)
        def _():
            loads = [
                pltpu.make_async_copy(
                    w_hbm.at[s, pl.ds(ROW_SHARD * k, ROW_SHARD), :], w.at[k, pl.ds(ROW_SHARD * s, ROW_SHARD), :], w_sems.at[k * N_DEV + s]
                )
                for k in range(3)
                for s in range(N_DEV)
            ]
            for cp in loads:
                cp.start()
            for cp in loads:
                cp.wait()
            gw[...] = jnp.zeros_like(gw)
            vec_ref[...] = jnp.zeros_like(vec_ref)

        cat = lambda a, b: jnp.concatenate([a[...], b[...]], axis=1)
        ga, gr, ma, mr = cat(ga0, ga1), cat(gr0, gr1), cat(ma0, ma1), cat(mr0, mr1)
        ov, hv, gate_v, fg = o_ref[...].astype(F32), hh_ref[...], gate_ref[...], fg_ref[...]
        sg_a = _sigmoid(ga)
        sg_r = _sigmoid(gr)
        silu_a = ga * sg_a
        silu_r = gr * sg_r
        ya = (ov * silu_a).astype(BF16)
        yr = (hv * silu_r).astype(BF16)
        pa = _dot(ya, w[0])
        pr = _dot(yr, w[1])
        sa = _sigmoid(ma)
        sr = _sigmoid(mr)
        merged = (sa * pa + sr * pr).astype(BF16)
        mo = _dot(merged, w[2])
        out = x_ref[...] + gate_v * mo
        r2 = lax.rsqrt(jnp.mean(out * out, axis=-1, keepdims=True) + NORM_EPS)
        outn = out * r2
        err = outn * fg - tg_ref[...]
        loss = 0.5 * jnp.sum(jnp.mean(err * err, axis=-1, keepdims=True), axis=0, keepdims=True)
        dy = err * (1.0 / D)
        g_final = jnp.sum(dy * outn, axis=0, keepdims=True)
        doutn = dy * fg
        dout = r2 * (doutn - outn * jnp.mean(doutn * outn, axis=-1, keepdims=True))
        dres_ref[...] = dout
        d_gate = jnp.sum(dout * mo, axis=0, keepdims=True)
        dmo = (dout * gate_v).astype(BF16)
        dmerged = _dot_nt(dmo, w[2])
        gw[2] = gw[2] + _dot_tn(merged, dmo)
        dpa = (dmerged * sa).astype(BF16)
        dpr = (dmerged * sr).astype(BF16)
        dma_ref[...] = (dmerged * pa * (sa * (1.0 - sa))).astype(BF16)
        dmr_ref[...] = (dmerged * pr * (sr * (1.0 - sr))).astype(BF16)
        dya = _dot_nt(dpa, w[0])
        dyr = _dot_nt(dpr, w[1])
        gw[0] = gw[0] + _dot_tn(ya, dpa)
        gw[1] = gw[1] + _dot_tn(yr, dpr)
        do_ref[...] = (dya * silu_a).astype(BF16)
        dhh_ref[...] = dyr * silu_r
        dga_ref[...] = (dya * ov * (sg_a * (1.0 + ga * (1.0 - sg_a)))).astype(BF16)
        dgr_ref[...] = (dyr * hv * (sg_r * (1.0 + gr * (1.0 - sg_r)))).astype(BF16)
        zeros = jnp.zeros((5, D), F32)
        vec_ref[...] = vec_ref[...] + jnp.concatenate([g_final, d_gate, jnp.broadcast_to(loss, (1, D)), zeros], axis=0)

        @pl.when(i == n_tiles - 1)
        def _():
            cp = pltpu.make_async_copy(gw, gw_hbm, sem)
            cp.start()
            cp.wait()

    bf_tile = jax.ShapeDtypeStruct((t, D), BF16)
    f_tile = jax.ShapeDtypeStruct((t, D), F32)
    return pl.pallas_call(
        body,
        name="mid",
        grid=(n_tiles,),
        in_specs=[full, full, full, full,
                  half(SEG_GA // 512), half(SEG_GA // 512 + 1), half(SEG_GR // 512), half(SEG_GR // 512 + 1),
                  half(SEG_MA // 512), half(SEG_MA // 512 + 1), half(SEG_MR // 512), half(SEG_MR // 512 + 1),
                  any_spec, vec, vec],
        out_specs=[full, full, full, full, full, full, full, any_spec, pl.BlockSpec((8, D), lambda i: (0, 0))],
        out_shape=[f_tile, bf_tile, f_tile, bf_tile, bf_tile, bf_tile, bf_tile,
                   jax.ShapeDtypeStruct((3, D, D), F32), jax.ShapeDtypeStruct((8, D), F32)],
        scratch_shapes=[pltpu.VMEM((3, D, D), BF16), pltpu.VMEM((3, D, D), F32), pltpu.SemaphoreType.DMA, pltpu.SemaphoreType.DMA((3 * N_DEV,))],
        compiler_params=_params(1),
    )(x, target, o, hh, proj, proj, proj, proj, proj, proj, proj, proj, w_all, gate, final_g)


_SEGMENTS = ((SEG_Q, 1024), (SEG_K, 256), (SEG_V, 256), (SEG_GA, 1024), (SEG_XR, 1024), (SEG_GR, 1024), (SEG_MA, 1024), (SEG_MR, 1024))


def _seg_specs(tm):
    return [pl.BlockSpec((tm, width), lambda i: (i, 0)) for _, width in _SEGMENTS]


def _bwd_dh(dsegs, w_all, x, dres, mod, norm_g):
    t = x.shape[0]
    tm = min(512, t)
    n_tiles = t // tm
    n_seg = len(_SEGMENTS)
    full = pl.BlockSpec((tm, D), lambda i: (i, 0))

    def body(*refs):
        segs = refs[:n_seg]
        w_hbm, x_ref, dres_ref, mod_ref, ng_ref, gx_ref, vec_ref, w, sems = refs[n_seg:]
        i = pl.program_id(0)

        @pl.when(i == 0)
        def _():
            loads = [
                pltpu.make_async_copy(w_hbm.at[s], w.at[pl.ds(W_IN_SHARD * s, W_IN_SHARD), :], sems.at[s])
                for s in range(N_DEV)
            ]
            for cp in loads:
                cp.start()
            for cp in loads:
                cp.wait()
            vec_ref[...] = jnp.zeros_like(vec_ref)

        dh = sum(_dot(seg[...], w[start : start + width, :]) for seg, (start, width) in zip(segs, _SEGMENTS))
        xv = x_ref[...]
        r1 = lax.rsqrt(jnp.mean(xv * xv, axis=-1, keepdims=True) + NORM_EPS)
        xn = xv * r1
        ng = ng_ref[...]
        scale1 = 1.0 + mod_ref[:, D : 2 * D]
        d_shift = jnp.sum(dh, axis=0, keepdims=True)
        d_scale = jnp.sum(dh * (xn * ng), axis=0, keepdims=True)
        g_norm = jnp.sum(dh * xn * scale1, axis=0, keepdims=True)
        dxn = dh * (ng * scale1)
        dx = r1 * (dxn - xn * jnp.mean(dxn * xn, axis=-1, keepdims=True))
        gx_ref[...] = dres_ref[...] + dx
        vec_ref[...] = vec_ref[...] + jnp.concatenate([d_shift, d_scale, g_norm, jnp.zeros((5, D), F32)], axis=0)

    any_spec = pl.BlockSpec(memory_space=pl.ANY)
    return pl.pallas_call(
        body,
        name="bwd_dh",
        grid=(n_tiles,),
        in_specs=_seg_specs(tm) + [any_spec, full, full, pl.BlockSpec((1, 3 * D), lambda i: (0, 0)), pl.BlockSpec((1, D), lambda i: (0, 0))],
        out_specs=[full, pl.BlockSpec((8, D), lambda i: (0, 0))],
        out_shape=[jax.ShapeDtypeStruct((t, D), F32), jax.ShapeDtypeStruct((8, D), F32)],
        scratch_shapes=[pltpu.VMEM((IN_W, D), BF16), pltpu.SemaphoreType.DMA((N_DEV,))],
        compiler_params=_params(1),
    )(*dsegs, w_all, x, dres, mod, norm_g)


_HBM = pl.BlockSpec(memory_space=pltpu.HBM)
_SEM = pl.BlockSpec(memory_space=pltpu.SEMAPHORE)
_EFFECT = pltpu.SideEffectType.DATAFLOW_SIDE_EFFECTING


def _w_exchange_copies(sum_ref, land_ref, sems):
    me = _position()
    n = len(CHIP_FLIPS)
    return [
        pltpu.make_async_remote_copy(
            src_ref=sum_ref.at[_chip_of(_flip(me, k))], dst_ref=land_ref.at[j], send_sem=sems[j], recv_sem=sems[n + j],
            device_id=_flip(me, k), device_id_type=MESH,
        )
        for j, k in enumerate(CHIP_FLIPS)
    ]


def _w_exchange_start(sum_w_bf):
    n_sems = 2 * len(CHIP_FLIPS)
    land = lax.empty((len(CHIP_FLIPS), W_IN_SHARD, D), BF16)

    def body(sum_ref, land_ref, *outs):
        for cp in _w_exchange_copies(sum_ref, land_ref, outs[:n_sems]):
            cp.start()
        token = outs[n_sems + 2]
        token[...] = jnp.zeros_like(token)

    res = pl.pallas_call(
        body,
        name="w_exchange_start",
        out_shape=(*[pltpu.SemaphoreType.DMA(())] * n_sems, pltpu.HBM(sum_w_bf.shape, BF16), pltpu.HBM(land.shape, BF16),
                   jax.ShapeDtypeStruct((8, 128), F32)),
        in_specs=(_HBM, _HBM),
        out_specs=(*[_SEM] * n_sems, _HBM, _HBM, pl.BlockSpec(memory_space=pltpu.VMEM)),
        input_output_aliases={0: n_sems, 1: n_sems + 1},
        compiler_params=pltpu.CompilerParams(has_side_effects=_EFFECT),
    )(pltpu.with_memory_space_constraint(sum_w_bf, pltpu.HBM), pltpu.with_memory_space_constraint(land, pltpu.HBM))
    return res[:n_sems], res[n_sems], res[n_sems + 1], res[n_sems + 2]


def _w_exchange_wait(sems, sum_thru, land_thru, after):
    n_sems = len(sems)

    def body(sum_ref, land_ref, *rest):
        for cp in _w_exchange_copies(sum_ref, land_ref, rest[:n_sems]):
            cp.wait_send()
            cp.wait_recv()

    return pl.pallas_call(
        body,
        name="w_exchange_wait",
        out_shape=(pltpu.HBM(sum_thru.shape, BF16), pltpu.HBM(land_thru.shape, BF16)),
        in_specs=(_HBM, _HBM, *[_SEM] * n_sems, pl.BlockSpec(memory_space=pl.ANY)),
        out_specs=(_HBM, _HBM),
        input_output_aliases={0: 0, 1: 1},
        compiler_params=pltpu.CompilerParams(has_side_effects=_EFFECT),
    )(sum_thru, land_thru, *sems, after)[1]


def _bwd_gw(dsegs, h, g_gates):
    t = h.shape[0]
    tm = min(512, t)
    n_tiles = t // tm
    n_seg = len(_SEGMENTS)
    n_chips = N_DEV // 2

    def gate_copies(g_hbm, lg_hbm, send_sems, recv_sems, arriving):
        me = _position()
        return [
            pltpu.make_async_remote_copy(
                src_ref=g_hbm.at[:, :, _slot(_flip(me, k))],
                dst_ref=lg_hbm.at[_slot(_flip(me, k) if arriving else me)],
                send_sem=send_sems.at[k - 1],
                recv_sem=recv_sems.at[k - 1],
                device_id=_flip(me, k),
                device_id_type=MESH,
            )
            for k in range(1, N_DEV)
        ]

    def body(*refs):
        segs = refs[:n_seg]
        (h_ref, g_hbm, own_hbm, sum_hbm, lg_hbm, acc, stage, land, own_sems, pair_send, pair_recv, send_sems, recv_sems, g_local) = refs[n_seg:]
        i = pl.program_id(0)
        slot = _slot(_position())
        own_gates = pltpu.make_async_copy(g_hbm.at[:, :, slot], lg_hbm.at[slot], g_local)

        @pl.when(i == 0)
        def _():
            own_gates.start()
            for cp in gate_copies(g_hbm, lg_hbm, send_sems, recv_sems, arriving=False):
                cp.start()
            acc[...] = jnp.zeros_like(acc)

        hv = h_ref[...]
        for seg, (start, width) in zip(segs, _SEGMENTS):
            acc[start : start + width, :] = acc[start : start + width, :] + _dot_tn(seg[...], hv)

        @pl.when(i == n_tiles - 1)
        def _():
            me = _position()
            sibling = _flip(me, 1)

            def rows(q, c):
                return pl.ds(pl.multiple_of(W_IN_SHARD * (2 * q + c), 8), W_IN_SHARD)

            def to_sibling(q):
                return pltpu.make_async_remote_copy(
                    src_ref=stage.at[q % 2], dst_ref=land.at[q], send_sem=pair_send.at[q], recv_sem=pair_recv.at[q],
                    device_id=sibling, device_id_type=MESH,
                )

            for q in range(n_chips):
                if q >= 2:
                    to_sibling(q - 2).wait_send()
                stage[q % 2] = acc[rows(q, sibling[2]), :].astype(BF16)
                to_sibling(q).start()
            for q in range(n_chips - 2, n_chips):
                to_sibling(q).wait_send()
            sums = [pltpu.make_async_copy(stage.at[q % 2], sum_hbm.at[q], own_sems.at[q]) for q in range(n_chips)]
            for q in range(n_chips):
                to_sibling(q).wait_recv()
                if q >= 2:
                    sums[q - 2].wait()
                total = acc[rows(q, me[2]), :] + land[q].astype(F32)
                acc[rows(q, me[2]), :] = total
                stage[q % 2] = total.astype(BF16)
                sums[q].start()
            mine = pltpu.make_async_copy(acc.at[rows(_chip_of(me), me[2]), :], own_hbm, own_sems.at[n_chips])
            mine.start()
            for q in range(n_chips - 2, n_chips):
                sums[q].wait()
            mine.wait()
            for cp in gate_copies(g_hbm, lg_hbm, send_sems, recv_sems, arriving=True):
                cp.wait_recv()
            for cp in gate_copies(g_hbm, lg_hbm, send_sems, recv_sems, arriving=False):
                cp.wait_send()
            own_gates.wait()

    any_spec = pl.BlockSpec(memory_space=pl.ANY)
    n_copies = N_DEV - 1
    return pl.pallas_call(
        body,
        name="bwd_gw",
        grid=(n_tiles,),
        in_specs=_seg_specs(tm) + [pl.BlockSpec((tm, D), lambda i: (i, 0)), any_spec],
        out_specs=[any_spec, any_spec, any_spec],
        out_shape=[jax.ShapeDtypeStruct((W_IN_SHARD, D), F32),
                   jax.ShapeDtypeStruct((n_chips, W_IN_SHARD, D), BF16),
                   jax.ShapeDtypeStruct((N_DEV, 2, 4, GATE_SHARD, 256), F32)],
        scratch_shapes=[pltpu.VMEM((IN_W, D), F32), pltpu.VMEM((2, W_IN_SHARD, D), BF16), pltpu.VMEM((n_chips, W_IN_SHARD, D), BF16),
                        pltpu.SemaphoreType.DMA((n_chips + 1,)), pltpu.SemaphoreType.DMA((n_chips,)), pltpu.SemaphoreType.DMA((n_chips,)),
                        pltpu.SemaphoreType.DMA((n_copies,)), pltpu.SemaphoreType.DMA((n_copies,)), pltpu.SemaphoreType.DMA],
        compiler_params=_params(1),
    )(*dsegs, h, g_gates)


def _adamw(w, g, m, v):
    m = ADAM_B1 * m + (1.0 - ADAM_B1) * g
    v = ADAM_B2 * v + (1.0 - ADAM_B2) * (g * g)
    m_hat = m / (1.0 - ADAM_B1**ADAM_STEP)
    v_hat = v / (1.0 - ADAM_B2**ADAM_STEP)
    delta = -ADAM_LR * (m_hat / (jnp.sqrt(v_hat) + ADAM_EPS) + ADAM_WD * w)
    return delta, m, v


def _w_in_update(own, land, w_t, m_t, v_t):
    n_slots, rows, cols = land.shape
    rb = 208
    spec = pl.BlockSpec((rb, cols), lambda i: (i, 0))

    def body(o_ref, l_ref, w_ref, m_ref, v_ref, g_ref, d_ref, m2_ref, v2_ref):
        total = o_ref[...]
        for s in range(n_slots):
            total = total + l_ref[s].astype(F32)
        g_ref[...] = total
        d_ref[...], m2_ref[...], v2_ref[...] = _adamw(w_ref[...], total, m_ref[...], v_ref[...])

    shape = jax.ShapeDtypeStruct((rows, cols), F32)
    return pl.pallas_call(
        body,
        name="w_in_update",
        grid=(rows // rb,),
        in_specs=[spec, pl.BlockSpec((n_slots, rb, cols), lambda i: (0, i, 0)), spec, spec, spec],
        out_specs=[spec] * 4,
        out_shape=[shape] * 4,
        compiler_params=_params(1),
    )(own, land, w_t, m_t, v_t)


def _shard_update(own_3, land_3, land_g, weights, m_in, v_in):
    n = len(weights)

    def body(*refs):
        o3_ref, l3_ref, lg_ref = refs[:3]
        w_refs, m_refs, v_refs = (refs[3 + n * k : 3 + n * (k + 1)] for k in range(3))
        outs = refs[3 + 3 * n :]
        for k in range(n):
            if k < 3:
                g = o3_ref[k]
                for slot in range(3):
                    g = g + l3_ref[slot, k].astype(F32)
            else:
                g = lg_ref[0, k - 3]
                for slot in range(1, N_DEV):
                    g = g + lg_ref[slot, k - 3]
            d, m2, v2 = _adamw(w_refs[k][0], g, m_refs[k][0], v_refs[k][0])
            for which, val in enumerate((g, d, m2, v2)):
                outs[which * n + k][0] = val

    shapes = [jax.ShapeDtypeStruct(w.shape, F32) for w in weights]
    res = pl.pallas_call(
        body, name="shard_update", out_shape=shapes * 4, compiler_params=pltpu.CompilerParams(vmem_limit_bytes=VMEM_LIMIT)
    )(own_3, land_3, land_g, *weights, *m_in, *v_in)
    return [res[n * k : n * (k + 1)] for k in range(4)]


VEC_LAYOUT = (("b_ada", 3 * D), ("norm_g", D), ("attn_sinks", 128), ("conv_b", D), ("rg_ba", D), ("rg_bx", D), ("rg_lambda", D),
              ("final_g", D), ("conv_w", CONV_WIDTH * D), ("loss", 128))
VEC_OFFSET = {name: sum(w for _, w in VEC_LAYOUT[:k]) for k, (name, _) in enumerate(VEC_LAYOUT)}
VEC_ROWS = 112
VEC_WIDTH = VEC_ROWS * 128
assert sum(w for _, w in VEC_LAYOUT) <= VEC_WIDTH
N_REPLICATED = 8


def _small_update(vec_all, c_all_t, me, rep_w, rep_m, rep_v, ada, conv):
    n_ada = ada[0].shape[1]
    n_conv = conv[0].shape[1]

    def body(*refs):
        me_ref, vec_ref, ct_ref = refs[:3]
        w_refs, m_refs, v_refs = (refs[3 + N_REPLICATED * k : 3 + N_REPLICATED * (k + 1)] for k in range(3))
        pos = 3 + 3 * N_REPLICATED
        ada_refs, conv_refs = refs[pos : pos + 3], refs[pos + 3 : pos + 6]
        outs = refs[pos + 6 :]
        loss_ref, rep_outs = outs[0], outs[1 : 1 + 4 * N_REPLICATED]
        ada_outs, conv_outs = outs[1 + 4 * N_REPLICATED : 5 + 4 * N_REPLICATED], outs[5 + 4 * N_REPLICATED : 9 + 4 * N_REPLICATED]
        conv_buf = outs[9 + 4 * N_REPLICATED]

        total = vec_ref[0:1, :]
        for slot in range(1, N_DEV):
            total = total + vec_ref[slot : slot + 1, :]
        loss_ref[...] = total[:, VEC_OFFSET["loss"] : VEC_OFFSET["loss"] + 1]
        for k, (name, _) in enumerate(VEC_LAYOUT[:N_REPLICATED]):
            width = w_refs[k].shape[1]
            g = total[:, VEC_OFFSET[name] : VEC_OFFSET[name] + width]
            d, m2, v2 = _adamw(w_refs[k][...], g, m_refs[k][...], v_refs[k][...])
            for which, val in enumerate((g, d, m2, v2)):
                rep_outs[which * N_REPLICATED + k][...] = val

        device = me_ref[0]
        for k in range(CONV_WIDTH):
            conv_buf[k : k + 1, :] = total[:, VEC_OFFSET["conv_w"] + D * k : VEC_OFFSET["conv_w"] + D * (k + 1)]
        g_conv = conv_buf[:, pl.ds(pl.multiple_of(device * n_conv, 128), n_conv)]
        for ref, val in zip(conv_outs, (g_conv,) + _adamw(conv_refs[0][...], g_conv, conv_refs[1][...], conv_refs[2][...])):
            ref[...] = val

        dmod = vec_ref[:, pl.ds(pl.multiple_of(device * n_ada, 128), n_ada)]
        g_ada = _dot(ct_ref[...], dmod)
        for ref, val in zip(ada_outs, (g_ada,) + _adamw(ada_refs[0][...], g_ada, ada_refs[1][...], ada_refs[2][...])):
            ref[...] = val

    vmem = pl.BlockSpec(memory_space=pltpu.VMEM)
    rep_shapes = [jax.ShapeDtypeStruct(w.shape, F32) for w in rep_w]
    out_shape = ([jax.ShapeDtypeStruct((1, 1), F32)] + rep_shapes * 4 + [jax.ShapeDtypeStruct(ada[0].shape, F32)] * 4
                 + [jax.ShapeDtypeStruct(conv[0].shape, F32)] * 4)
    n_in = 3 + 3 * N_REPLICATED + 6
    res = pl.pallas_call(
        body,
        name="small_update",
        in_specs=[pl.BlockSpec(memory_space=pltpu.SMEM)] + [vmem] * (n_in - 1),
        out_specs=[vmem] * len(out_shape),
        out_shape=out_shape,
        scratch_shapes=[pltpu.VMEM((CONV_WIDTH, D), F32)],
        compiler_params=pltpu.CompilerParams(vmem_limit_bytes=VMEM_LIMIT),
    )(me, vec_all, c_all_t, *rep_w, *rep_m, *rep_v, *ada, *conv)
    loss = res[0]
    reps = [res[1 + N_REPLICATED * k : 1 + N_REPLICATED * (k + 1)] for k in range(4)]
    return loss, reps, res[1 + 4 * N_REPLICATED : 5 + 4 * N_REPLICATED], res[5 + 4 * N_REPLICATED : 9 + 4 * N_REPLICATED]


def kernel(x, c, positions, w_ada, b_ada, norm_g, w_in, attn_sinks, conv_w, conv_b, rg_wa, rg_ba, rg_wx, rg_bx, rg_lambda, w_attn_proj, w_rnn_proj, w_out, final_g, loss_target, m_w_ada, m_b_ada, m_norm_g, m_w_in, m_attn_sinks, m_conv_w, m_conv_b, m_rg_wa, m_rg_ba, m_rg_wx, m_rg_bx, m_rg_lambda, m_w_attn_proj, m_w_rnn_proj, m_w_out, m_final_g, v_w_ada, v_b_ada, v_norm_g, v_w_in, v_attn_sinks, v_conv_w, v_conv_b, v_rg_wa, v_rg_ba, v_rg_wx, v_rg_bx, v_rg_lambda, v_w_attn_proj, v_w_rnn_proj, v_w_out, v_final_g):
    my_x, my_y, my_c = _position()
    me = _slot((my_x, my_y, my_c))
    x2, target = x[0], loss_target[0]
    pos = positions[0]

    gate_rows = lambda w: w[0].transpose(1, 0, 2).reshape(GATE_SHARD, D)
    w_pack = jnp.concatenate([w_attn_proj[0], w_rnn_proj[0], w_out[0], gate_rows(rg_wa), gate_rows(rg_wx)], axis=0).astype(BF16)
    conv_rows = jnp.pad(conv_w[0], ((0, 8 - CONV_WIDTH), (0, 0)))
    w_in_t = w_in[0].T
    n_ada = w_ada.shape[2]
    b_cols = lax.dynamic_slice(b_ada, (0, me * n_ada), (1, n_ada))

    tables = _rope_tables(pos)
    where = jnp.stack([_chip_of((my_x, my_y)), me]).astype(jnp.int32)
    proj, h, mod, lane_tables, w_t_all, w_all, c_rows, conv_all = _fwd_in(
        x2, jnp.broadcast_to(c, (8, D)), conv_rows, w_ada[0], b_cols, norm_g, w_in_t.astype(BF16), tables, w_pack, where
    )
    c_all = c_rows[:, 0, :]
    conv_full = conv_all[:, :CONV_WIDTH, :].transpose(1, 0, 2).reshape(CONV_WIDTH, D)
    gate = mod[:, 2 * D :]
    o = _attn_fwd(proj, attn_sinks)
    keep = (pos != 0).astype(F32)[:, None]
    hh = _rnn_fwd(proj, keep, w_all, rg_ba, rg_bx, rg_lambda, conv_full, conv_b)

    dres, d_o, d_hh, d_ga, d_gr, d_ma, d_mr, gw3, mid_vec = _mid(x2, target, o, hh, proj, w_all, gate, final_g[None, :])
    part_3 = gw3.reshape(3, 4, 2, ROW_SHARD, D)
    d_q, d_k, d_v, d_sink, sum_3_bf, own_3 = _attn_bwd(proj, o, d_o, (lane_tables, tables), attn_sinks, part_3)
    d_xr, g_gates, rnn_vec, land_3 = _rnn_bwd(proj, hh, d_hh, keep, w_all, rg_ba, rg_bx, rg_lambda, conv_full, conv_b, sum_3_bf)
    dsegs = (d_q, d_k, d_v, d_ga, d_xr, d_gr, d_ma, d_mr)
    own_w, sum_w_bf, land_g = _bwd_gw(dsegs, h, g_gates.reshape(2, 4, N_DEV, GATE_SHARD, 256))
    w_sems, sum_w_thru, land_w_thru, token = _w_exchange_start(sum_w_bf)
    grad_x, dh_vec = _bwd_dh(dsegs, w_t_all, x2, dres, mod + token[0:1, 0:1], norm_g)
    land_w = _w_exchange_wait(w_sems, sum_w_thru, land_w_thru, dh_vec)

    pieces = dict(
        b_ada=jnp.concatenate([dh_vec[0:1], dh_vec[1:2], mid_vec[1:2]], axis=1),
        norm_g=dh_vec[2:3],
        attn_sinks=jnp.pad(d_sink[:, :, 0].reshape(1, 16), ((0, 0), (0, 112))),
        conv_b=rnn_vec[3:4],
        rg_ba=rnn_vec[0:1],
        rg_bx=rnn_vec[1:2],
        rg_lambda=rnn_vec[2:3],
        final_g=mid_vec[0:1],
        conv_w=rnn_vec[4:8].reshape(1, CONV_WIDTH * D),
        loss=mid_vec[2:3, 0:128],
    )
    vec = jnp.concatenate([pieces[name] for name, _ in VEC_LAYOUT], axis=1)
    vec = jnp.pad(vec, ((0, 0), (0, VEC_WIDTH - vec.shape[1]))).reshape(VEC_ROWS, 128)
    (vec_all,) = _all_gather_direct("ag_vec", [vec])
    rep_w = [b_ada, norm_g, attn_sinks, conv_b, rg_ba, rg_bx, rg_lambda, final_g[None, :]]
    rep_m = [m_b_ada, m_norm_g, m_attn_sinks, m_conv_b, m_rg_ba, m_rg_bx, m_rg_lambda, m_final_g[None, :]]
    rep_v = [v_b_ada, v_norm_g, v_attn_sinks, v_conv_b, v_rg_ba, v_rg_bx, v_rg_lambda, v_final_g[None, :]]
    loss, reps, ada_res, conv_res = _small_update(
        vec_all.reshape(N_DEV, VEC_WIDTH), c_all.T, me.astype(jnp.int32).reshape(1), rep_w, rep_m, rep_v,
        (w_ada[0], m_w_ada[0], v_w_ada[0]), (conv_w[0], m_conv_w[0], v_conv_w[0]),
    )

    w_in_res = _w_in_update(own_w, land_w, w_in_t, m_w_in[0].T, v_w_in[0].T)
    shard_res = _shard_update(
        own_3, land_3, land_g,
        (w_attn_proj, w_rnn_proj, w_out, rg_wa, rg_wx),
        (m_w_attn_proj, m_w_rnn_proj, m_w_out, m_rg_wa, m_rg_wx),
        (v_w_attn_proj, v_w_rnn_proj, v_w_out, v_rg_wa, v_rg_wx),
    )

    def by_weight(k):
        b, n, s, cb, ba, bx, lam, fg = reps[k]
        ap, rp, out, wa, wx = shard_res[k]
        return [ada_res[k][None], b, n, w_in_res[k].T[None], s, conv_res[k][None], cb, wa, ba, wx, bx, lam, ap, rp, out, fg[0]]

    return (loss[0, 0], grad_x[None], *by_weight(0), *by_weight(1), *by_weight(2), *by_weight(3))
```

```python
import math

import jax
import jax.numpy as jnp
from jax import lax
from jax.experimental import pallas as pl
from jax.experimental.pallas import tpu as pltpu

F32 = jnp.float32
BF16 = jnp.bfloat16

D = 1024
HEAD_DIM = 64
ROT_DIM = 16
ROPE_THETA = 500000.0
WINDOW = 128
LRU_C = 8.0
CONV_WIDTH = 4
NORM_EPS = 1e-6
IN_W = 6656
SEG_Q, SEG_K, SEG_V, SEG_GA, SEG_XR, SEG_GR, SEG_MA, SEG_MR = 0, 1024, 1280, 1536, 2560, 3584, 4608, 5632
N_DEV = 8
W_IN_SHARD = IN_W // N_DEV
ROW_SHARD = D // N_DEV
GATE_SHARD = 256 // N_DEV

ADAM_LR = 0.001
ADAM_B1 = 0.9
ADAM_B2 = 0.999
ADAM_EPS = 1e-08
ADAM_WD = 0.01
ADAM_STEP = 10

VMEM_LIMIT = 60 * 1024 * 1024
MESH = pl.DeviceIdType.MESH


def _dot(a, b):
    return jnp.dot(a, b, preferred_element_type=F32)


def _dot_nt(a, b):
    return lax.dot_general(a, b, (((1,), (1,)), ((), ())), preferred_element_type=F32)


def _dot_tn(a, b):
    return lax.dot_general(a, b, (((0,), (0,)), ((), ())), preferred_element_type=F32)


def _sigmoid(z):
    return 1.0 / (1.0 + jnp.exp(-z))


def _params(n_grid, vmem=VMEM_LIMIT):
    return pltpu.CompilerParams(dimension_semantics=("arbitrary",) * n_grid, vmem_limit_bytes=vmem)


def _position():
    return lax.axis_index("x"), lax.axis_index("y"), lax.axis_index("c")


def _flip(pos, k):
    x, y, c = pos
    return (1 - x if k & 4 else x, 1 - y if k & 2 else y, 1 - c if k & 1 else c)


def _slot(pos):
    return 4 * pos[0] + 2 * pos[1] + pos[2]


def _direct_gather_copies(src_hbm, all_hbm, send_sems, recv_sems, arriving):
    me = _position()
    return [
        pltpu.make_async_remote_copy(
            src_ref=src_hbm,
            dst_ref=all_hbm.at[_slot(_flip(me, k) if arriving else me)],
            send_sem=send_sems.at[k - 1],
            recv_sem=recv_sems.at[k - 1],
            device_id=_flip(me, k),
            device_id_type=MESH,
        )
        for k in range(1, N_DEV)
    ]


CHIP_FLIPS = (2, 4, 6)


def _chip_of(pos):
    return 2 * pos[0] + pos[1]


def _chip_copies(srcs, lands, views, send_sems, recv_sems):
    me = _position()
    copies = []
    for a, (src, land) in enumerate(zip(srcs, lands)):
        for j, k in enumerate(CHIP_FLIPS):
            peer = _flip(me, k)
            copies.append(
                pltpu.make_async_remote_copy(
                    src_ref=views[a](src, peer),
                    dst_ref=land.at[j],
                    send_sem=send_sems.at[len(CHIP_FLIPS) * a + j],
                    recv_sem=recv_sems.at[len(CHIP_FLIPS) * a + j],
                    device_id=peer,
                    device_id_type=MESH,
                )
            )
    return copies


PACK_GATE = 3 * ROW_SHARD
PACK_ROWS = PACK_GATE + 2 * GATE_SHARD


def _fwd_in(x, c_rows, conv_rows, w_ada, b_cols, norm_g, w_shard, tables, pack, where):
    t = x.shape[0]
    tm = min(1024, t)
    tn = IN_W // 4
    n_i, n_j = t // tm, IN_W // tn
    n_rope = (SEG_V - SEG_Q) // 128

    n_ada = w_ada.shape[1]

    def body(where_ref, x_ref, wada_ref, bcols_ref, ng_ref, ct_ref, st_ref, shard_hbm, pack_hbm, c_hbm, conv_hbm,
             proj_ref, h_ref, mod_ref, lane_hbm, wt_hbm, all_hbm, call_hbm, convall_hbm,
             wall, hbuf, cbuf, mcols, mod_all, mod_buf, lane,
             w_send, w_recv, load_sems, local_sem, lane_sem, p_send, p_recv, p_local,
             c_send, c_recv, c_local, v_send, v_recv, v_local, m_send, m_recv):
        j = pl.program_id(0)
        i = pl.program_id(1)
        me = _position()
        sibling = _flip(me, 1)
        chip = where_ref[0]
        slot = where_ref[1]

        def modulation():
            own_c = pltpu.make_async_copy(c_hbm, call_hbm.at[_slot(me)], c_local)
            own_v = pltpu.make_async_copy(conv_hbm, convall_hbm.at[_slot(me)], v_local)
            own_c.start()
            own_v.start()
            for cp in _direct_gather_copies(c_hbm, call_hbm, c_send, c_recv, arriving=False):
                cp.start()
            for cp in _direct_gather_copies(conv_hbm, convall_hbm, v_send, v_recv, arriving=False):
                cp.start()
            for cp in _direct_gather_copies(c_hbm, call_hbm, c_send, c_recv, arriving=True):
                cp.wait_recv()
            own_c.wait()
            fetch = pltpu.make_async_copy(call_hbm, cbuf, c_local)
            fetch.start()
            fetch.wait()
            mcols[...] = _dot(cbuf[:, 0, :].astype(BF16), wada_ref[...].astype(BF16)) + bcols_ref[...]
            mod_all[slot] = mcols[...]
            swaps = lambda arriving: [
                pltpu.make_async_remote_copy(
                    src_ref=mcols,
                    dst_ref=mod_all.at[_slot(_flip(me, k) if arriving else me)],
                    send_sem=m_send.at[k - 1],
                    recv_sem=m_recv.at[k - 1],
                    device_id=_flip(me, k),
                    device_id_type=MESH,
                )
                for k in range(1, N_DEV)
            ]
            for cp in swaps(False):
                cp.start()
            for cp in swaps(True):
                cp.wait_recv()
            mine = lax.broadcasted_iota(jnp.int32, (N_DEV, n_ada), 0) == slot
            for s in range(N_DEV):
                row = jnp.sum(jnp.where(mine, mod_all[s], 0.0), axis=0, keepdims=True)
                mod_buf[:, n_ada * s : n_ada * (s + 1)] = row
            for cp in swaps(False):
                cp.wait_send()
            for cp in _direct_gather_copies(c_hbm, call_hbm, c_send, c_recv, arriving=False):
                cp.wait_send()
            for cp in _direct_gather_copies(conv_hbm, convall_hbm, v_send, v_recv, arriving=True):
                cp.wait_recv()
            for cp in _direct_gather_copies(conv_hbm, convall_hbm, v_send, v_recv, arriving=False):
                cp.wait_send()
            own_v.wait()

        def shard_of(block):
            return wall.at[_chip_of(block), pl.ds(W_IN_SHARD * block[2], W_IN_SHARD), :]

        def shard_copy(k, block, to, src=None):
            return pltpu.make_async_remote_copy(
                src_ref=shard_of(block),
                dst_ref=shard_of(block),
                send_sem=w_send.at[k],
                recv_sem=w_recv.at[k],
                device_id=to,
                device_id_type=MESH,
            )

        own_shard = pltpu.make_async_copy(shard_hbm, shard_of(me), local_sem)
        own_pack = pltpu.make_async_copy(pack_hbm, all_hbm.at[_slot(me)], p_local)
        keep_lane = pltpu.make_async_copy(lane, lane_hbm, lane_sem)

        def lane_tables():
            d = lax.broadcasted_iota(jnp.int32, (128, 128), 0) % HEAD_DIM
            half = ROT_DIM // 2

            def fill(b, carry):
                cols = pl.ds(pl.multiple_of(128 * b, 128), 128)
                cos = jnp.concatenate([ct_ref[:, cols]] * (128 // half), axis=0)
                sin = jnp.concatenate([st_ref[:, cols]] * (128 // half), axis=0)
                lane[0, cols, :] = jnp.where(d < ROT_DIM, cos, 1.0).T
                lane[1, cols, :] = jnp.where(d < half, -sin, 0.0).T
                lane[2, cols, :] = jnp.where((d >= half) & (d < ROT_DIM), sin, 0.0).T
                return carry

            lax.fori_loop(0, t // 128, fill, 0)

        def keep_tile(q):
            return [
                pltpu.make_async_copy(wall.at[q, pl.ds(W_IN_SHARD * c, W_IN_SHARD), :], wt_hbm.at[2 * q + c], load_sems.at[2 * (q ^ chip) + c])
                for c in range(2)
            ]

        @pl.when((j == 0) & (i == 0))
        def _():
            modulation()
            mod_ref[...] = mod_buf[...]
            own_shard.start()
            own_shard.wait()
            shard_copy(0, me, sibling).start()
            for rel in range(1, 4):
                shard_copy(rel, me, _flip(me, 2 * rel)).start()
            own_pack.start()
            for cp in _direct_gather_copies(pack_hbm, all_hbm, p_send, p_recv, arriving=False):
                cp.start()
            lane_tables()
            keep_lane.start()
            shard_copy(0, sibling, me).wait_recv()
            for cp in keep_tile(chip):
                cp.start()

        for rel in range(1, 4):

            @pl.when((j == 1) & (i == min(rel, n_i) - 1))
            def _(rel=rel):
                there = _flip(me, 2 * rel)
                shard_copy(rel, there, me).wait_recv()
                shard_copy(3 + rel, there, sibling).start()

            @pl.when((j == rel) & (i == 0))
            def _(rel=rel):
                shard_copy(3 + rel, _flip(sibling, 2 * rel), me).wait_recv()
                for cp in keep_tile(chip ^ rel):
                    cp.start()

        rows = pl.ds(pl.multiple_of(i * tm, tm), tm)

        @pl.when(j == 0)
        def _():
            xv = x_ref[...]
            r = lax.rsqrt(jnp.mean(xv * xv, axis=-1, keepdims=True) + NORM_EPS)
            shift = mod_buf[:, 0:D]
            scale = mod_buf[:, D : 2 * D]
            hv = (((xv * r) * ng_ref[...]) * (1.0 + scale) + shift).astype(BF16)
            hbuf[rows, :] = hv
            h_ref[...] = hv

        proj_ref[...] = _dot_nt(hbuf[rows, :], wall[chip ^ j])

        @pl.when((chip ^ j) == 0)
        def _():
            cos, s1, s2 = lane[0, rows, :], lane[1, rows, :], lane[2, rows, :]
            for g in range(n_rope):
                cols = slice(128 * g, 128 * (g + 1))
                proj_ref[:, cols] = _rope(proj_ref[:, cols], cos, s1, s2)

        @pl.when((j == n_j - 1) & (i == n_i - 1))
        def _():
            shard_copy(0, me, sibling).wait_send()
            for rel in range(1, 4):
                shard_copy(rel, me, _flip(me, 2 * rel)).wait_send()
                shard_copy(3 + rel, _flip(me, 2 * rel), sibling).wait_send()
            for q in range(N_DEV // 2):
                for cp in keep_tile(q):
                    cp.wait()
            for cp in _direct_gather_copies(pack_hbm, all_hbm, p_send, p_recv, arriving=True):
                cp.wait_recv()
            for cp in _direct_gather_copies(pack_hbm, all_hbm, p_send, p_recv, arriving=False):
                cp.wait_send()
            own_pack.wait()
            keep_lane.wait()

    tab = pl.BlockSpec((ROT_DIM // 2, t), lambda j, i, c: (0, 0))
    any_spec = pl.BlockSpec(memory_space=pl.ANY)
    return pl.pallas_call(
        body,
        name="fwd_in",
        grid_spec=pltpu.PrefetchScalarGridSpec(
            num_scalar_prefetch=1,
            grid=(n_j, n_i),
            in_specs=[
                pl.BlockSpec((tm, D), lambda j, i, c: (jnp.where(j == 0, i, 0), 0)),
                pl.BlockSpec(w_ada.shape, lambda j, i, c: (0, 0)),
                pl.BlockSpec((1, n_ada), lambda j, i, c: (0, 0)),
                pl.BlockSpec((1, D), lambda j, i, c: (0, 0)),
                tab, tab, any_spec, any_spec, any_spec, any_spec,
            ],
            out_specs=[pl.BlockSpec((tm, tn), lambda j, i, c: (i, c[0] ^ j)),
                       pl.BlockSpec((tm, D), lambda j, i, c: (jnp.where(j == 0, i, n_i - 1), 0)),
                       pl.BlockSpec((1, 3 * D), lambda j, i, c: (0, 0)),
                       any_spec, any_spec, any_spec, any_spec, any_spec],
            scratch_shapes=[
                pltpu.VMEM((N_DEV // 2, tn, D), BF16),
                pltpu.VMEM((t, D), BF16),
                pltpu.VMEM((N_DEV,) + c_rows.shape, F32),
                pltpu.VMEM((N_DEV, n_ada), F32),
                pltpu.VMEM((N_DEV, N_DEV, n_ada), F32),
                pltpu.VMEM((1, 3 * D), F32),
                pltpu.VMEM((3, t, 128), F32),
                pltpu.SemaphoreType.DMA((N_DEV - 1,)),
                pltpu.SemaphoreType.DMA((N_DEV - 1,)),
                pltpu.SemaphoreType.DMA((N_DEV,)),
                pltpu.SemaphoreType.DMA,
                pltpu.SemaphoreType.DMA,
                pltpu.SemaphoreType.DMA((N_DEV - 1,)),
                pltpu.SemaphoreType.DMA((N_DEV - 1,)),
                pltpu.SemaphoreType.DMA,
                pltpu.SemaphoreType.DMA((N_DEV - 1,)),
                pltpu.SemaphoreType.DMA((N_DEV - 1,)),
                pltpu.SemaphoreType.DMA,
                pltpu.SemaphoreType.DMA((N_DEV - 1,)),
                pltpu.SemaphoreType.DMA((N_DEV - 1,)),
                pltpu.SemaphoreType.DMA,
                pltpu.SemaphoreType.DMA((N_DEV - 1,)),
                pltpu.SemaphoreType.DMA((N_DEV - 1,)),
            ],
        ),
        out_shape=[jax.ShapeDtypeStruct((t, IN_W), F32), jax.ShapeDtypeStruct((t, D), BF16),
                   jax.ShapeDtypeStruct((1, 3 * D), F32),
                   jax.ShapeDtypeStruct((3, t, 128), F32),
                   jax.ShapeDtypeStruct((N_DEV,) + w_shard.shape, w_shard.dtype),
                   jax.ShapeDtypeStruct((N_DEV,) + pack.shape, pack.dtype),
                   jax.ShapeDtypeStruct((N_DEV,) + c_rows.shape, F32),
                   jax.ShapeDtypeStruct((N_DEV,) + conv_rows.shape, F32)],
        compiler_params=_params(2),
    )(where, x, w_ada, b_cols, norm_g, *tables, w_shard, pack, c_rows, conv_rows)


def _rope_tables(positions):
    inv_freq = ROPE_THETA ** (-jnp.arange(0, ROT_DIM, 2, dtype=F32) / ROT_DIM)
    ang_t = inv_freq[:, None] * positions.astype(F32)[None, :]
    return jnp.cos(ang_t), jnp.sin(ang_t)


def _rope(tv, cos, s1, s2):
    cols = []
    for g in range(tv.shape[1] // 128):
        blk = tv[:, 128 * g : 128 * (g + 1)]
        cols.append(blk * cos + pltpu.roll(blk, 128 - 8, 1) * s1 + pltpu.roll(blk, 8, 1) * s2)
    return cols[0] if len(cols) == 1 else jnp.concatenate(cols, axis=1)


def _rope_transposed(gv, cos, s1, s2):
    cols = []
    for g in range(gv.shape[1] // 128):
        blk = gv[:, 128 * g : 128 * (g + 1)]
        cols.append(blk * cos + pltpu.roll(blk * s1, 8, 1) + pltpu.roll(blk * s2, 128 - 8, 1))
    return cols[0] if len(cols) == 1 else jnp.concatenate(cols, axis=1)


def _lane_lo():
    return lax.broadcasted_iota(jnp.int32, (1, 128), 1) < HEAD_DIM


def _row_lo():
    return lax.broadcasted_iota(jnp.int32, (128, 1), 0) < HEAD_DIM


def _split_heads(tile, j):
    lo = _lane_lo()
    parts = []
    for g in (2 * j, 2 * j + 1):
        blk = tile[:, 128 * g : 128 * (g + 1)]
        parts += [jnp.where(lo, blk, 0.0), jnp.where(lo, 0.0, blk)]
    return jnp.concatenate(parts, axis=0)


def _dup_lanes(kv, j):
    rolled = pltpu.roll(kv, HEAD_DIM, 1)
    lo = _lane_lo()
    return jnp.where(lo, kv, rolled) if j == 0 else jnp.where(lo, rolled, kv)


def _fold_lanes(acc):
    return acc + pltpu.roll(acc, HEAD_DIM, 1)


def _fill_bias_t(bias_ref):
    kj = lax.broadcasted_iota(jnp.int32, (256, 128), 0)
    qi = lax.broadcasted_iota(jnp.int32, (256, 128), 1)
    diff = qi + WINDOW - kj
    band = (diff >= 0) & (diff < WINDOW)
    bias_ref[0] = jnp.where(band, 0.0, -1e30)
    bias_ref[1] = jnp.where(band & (kj >= 128), 0.0, -1e30)


def _sink_row(sink_ref, pair, j):
    return jnp.concatenate([jnp.full((1, 128), sink_ref[0, 8 * pair + 4 * j + h], F32) for h in range(4)], axis=1)


def _softmax_t(q4, kw, bias4, sink):
    s = _dot_nt(kw, q4) + bias4
    m = jnp.maximum(jnp.max(s, axis=0, keepdims=True), sink)
    p = jnp.exp(s - m)
    e_sink = jnp.exp(sink - m)
    denom = jnp.sum(p, axis=0, keepdims=True) + e_sink
    return p, e_sink, denom


def _attn_specs(t, tq):
    return dict(
        q=pl.BlockSpec((tq, 512), lambda p, i: (i, p)),
        k=pl.BlockSpec((t, 128), lambda p, i: (0, SEG_K // 128 + p)),
        v=pl.BlockSpec((t, 128), lambda p, i: (0, SEG_V // 128 + p)),
        sink=pl.BlockSpec(memory_space=pltpu.SMEM),
        act=pl.BlockSpec((tq, 512), lambda p, i: (i, p)),
    )


def _attn_fwd(proj, sinks):
    t = proj.shape[0]
    tq = min(512, t)
    nb = tq // 128
    n_tiles = t // tq
    sp = _attn_specs(t, tq)

    def body(q_ref, k_ref, v_ref, sink_ref, o_ref, kd, vdt, bias_ref):
        pair = pl.program_id(0)
        i = pl.program_id(1)

        @pl.when(i == 0)
        def _():
            kv, vv = k_ref[...], v_ref[...]
            for j in range(2):
                kd[j, 0:128, :] = jnp.zeros((128, 128), BF16)
                kd[j, 128:, :] = _dup_lanes(kv, j).astype(BF16)
                vdt[j, :, 0:128] = jnp.zeros((128, 128), BF16)
                vdt[j, :, 128:] = _dup_lanes(vv, j).T.astype(BF16)
            _fill_bias_t(bias_ref)

        row_lo = _row_lo()

        def block(b, carry):
            r0 = pl.multiple_of(b * 128, 128)
            gb = i * nb + b
            keys = pl.ds(pl.multiple_of(gb * 128, 128), 256)
            rows = pl.ds(r0, 128)
            qt = q_ref[rows, :] * (1.0 / math.sqrt(HEAD_DIM))
            bias = bias_ref[jnp.where(gb == 0, 1, 0)]
            bias4 = jnp.concatenate([bias] * 4, axis=1)
            cols = []
            for j in range(2):
                q4 = _split_heads(qt, j).astype(BF16)
                p, _, denom = _softmax_t(q4, kd[j, keys, :], bias4, _sink_row(sink_ref, pair, j))
                ot = _dot(vdt[j, :, keys], p.astype(BF16)) * (1.0 / denom)
                for gi in range(2):
                    z = jnp.where(row_lo, ot[:, 256 * gi : 256 * gi + 128], ot[:, 256 * gi + 128 : 256 * gi + 256])
                    cols.append(z.T)
            o_ref[rows, :] = jnp.concatenate(cols, axis=1).astype(BF16)
            return carry

        lax.fori_loop(0, nb, block, 0, unroll=True)

    return pl.pallas_call(
        body,
        name="attn_fwd",
        grid=(2, n_tiles),
        in_specs=[sp["q"], sp["k"], sp["v"], sp["sink"]],
        out_specs=sp["act"],
        out_shape=jax.ShapeDtypeStruct((t, D), BF16),
        scratch_shapes=[pltpu.VMEM((2, t + 128, 128), BF16), pltpu.VMEM((2, 128, t + 128), BF16), pltpu.VMEM((2, 256, 128), F32)],
        compiler_params=_params(2),
    )(proj, proj, proj, sinks)


def _attn_bwd(proj, o, d_o, tables, sinks, part_3):
    t = proj.shape[0]
    tq = min(512, t)
    nb = tq // 128
    n_tiles = t // tq
    lane_tables, (cos_t, sin_t) = tables
    sp = _attn_specs(t, tq)
    scale = 1.0 / math.sqrt(HEAD_DIM)
    half = ROT_DIM // 2

    def unrope_rows(z, c, s):
        out = []
        for base in (0, HEAD_DIM):
            g0, g1 = z[base : base + half], z[base + half : base + ROT_DIM]
            out += [g0 * c + g1 * s, g1 * c - g0 * s, z[base + ROT_DIM : base + HEAD_DIM]]
        return jnp.concatenate(out, axis=0)

    def swap(part_hbm, land, send_sem, recv_sem):
        sibling = _flip(_position(), 1)
        return pltpu.make_async_remote_copy(
            src_ref=part_hbm.at[:, :, sibling[2]], dst_ref=land, send_sem=send_sem, recv_sem=recv_sem,
            device_id=sibling, device_id_type=MESH,
        )

    def body(q_ref, k_ref, v_ref, o_ref, do_ref, ct_ref, st_ref, ck, s1k, s2k, sink_ref, part_hbm,
             dq_ref, dk_ref, dv_ref, dsink_ref, sum_hbm, mine_hbm, kd, kdt, vd, bias_ref, dkacc, dvacc, own, land, stage,
             send_sem, recv_sem, local_sems):
        pair = pl.program_id(0)
        i = pl.program_id(1)
        me = _position()
        load_own = pltpu.make_async_copy(part_hbm.at[:, :, me[2]], own, local_sems.at[0])

        @pl.when((pair == 0) & (i == 0))
        def _():
            swap(part_hbm, land, send_sem, recv_sem).start()
            load_own.start()

        @pl.when(i == 0)
        def _():
            kv, vv = k_ref[...], v_ref[...]
            for j in range(2):
                kdup = _dup_lanes(kv, j)
                kd[j, 0:128, :] = jnp.zeros((128, 128), BF16)
                kd[j, 128:, :] = kdup.astype(BF16)
                kdt[j, :, 0:128] = jnp.zeros((128, 128), BF16)
                kdt[j, :, 128:] = kdup.T.astype(BF16)
                vd[j, 0:128, :] = jnp.zeros((128, 128), BF16)
                vd[j, 128:, :] = _dup_lanes(vv, j).astype(BF16)
            _fill_bias_t(bias_ref)
            dkacc[...] = jnp.zeros_like(dkacc)
            dvacc[...] = jnp.zeros_like(dvacc)
            dsink_ref[...] = jnp.zeros_like(dsink_ref)

        row_lo = _row_lo()
        ones = jnp.ones((8, 128), BF16)

        def block(b, carry):
            r0 = pl.multiple_of(b * 128, 128)
            gb = i * nb + b
            keys = pl.ds(pl.multiple_of(gb * 128, 128), 256)
            rows = pl.ds(r0, 128)
            qt = q_ref[rows, :] * scale
            dob = do_ref[rows, :].astype(F32)
            prod = dob * o_ref[rows, :].astype(F32)
            ct, st = ct_ref[:, rows], st_ref[:, rows]
            bias = bias_ref[jnp.where(gb == 0, 1, 0)]
            bias4 = jnp.concatenate([bias] * 4, axis=1)
            cols = []
            for j in range(2):
                q4 = _split_heads(qt, j).astype(BF16)
                kw = kd[j, keys, :]
                p, e_sink, denom = _softmax_t(q4, kw, bias4, _sink_row(sink_ref, pair, j))
                inv = 1.0 / denom
                prob = p * inv
                do4 = _split_heads(dob, j).astype(BF16)
                delta = _dot_nt(ones, _split_heads(prod, j).astype(BF16))[0:1, :]
                dp = _dot_nt(vd[j, keys, :], do4)
                ds = (prob * (dp - delta)).astype(BF16)
                dkacc[j, keys, :] = dkacc[j, keys, :] + _dot(ds, q4)
                dvacc[j, keys, :] = dvacc[j, keys, :] + _dot(prob.astype(BF16), do4)
                dqt = _dot(kdt[j, :, keys], ds) * scale
                dsink = -(e_sink * inv) * delta
                for h in range(4):
                    row = 4 * j + h
                    total = jnp.sum(dsink[:, 128 * h : 128 * (h + 1)], axis=1, keepdims=True)
                    dsink_ref[0, row : row + 1, :] = dsink_ref[0, row : row + 1, :] + jnp.broadcast_to(total, (1, 128))
                for gi in range(2):
                    z = jnp.where(row_lo, dqt[:, 256 * gi : 256 * gi + 128], dqt[:, 256 * gi + 128 : 256 * gi + 256])
                    cols.append(unrope_rows(z, ct, st).T)
            dq_ref[rows, :] = jnp.concatenate(cols, axis=1).astype(BF16)
            return carry

        lax.fori_loop(0, nb, block, 0, unroll=True)

        @pl.when(i == n_tiles - 1)
        def _():
            lo = _lane_lo()
            dk = jnp.where(lo, _fold_lanes(dkacc[0, 128:, :]), _fold_lanes(dkacc[1, 128:, :]))
            dk_ref[...] = _rope_transposed(dk, ck[...], s1k[...], s2k[...]).astype(BF16)
            dv_ref[...] = jnp.where(lo, _fold_lanes(dvacc[0, 128:, :]), _fold_lanes(dvacc[1, 128:, :])).astype(BF16)

        @pl.when((pair == 1) & (i == n_tiles - 1))
        def _():
            swap(part_hbm, land, send_sem, recv_sem).wait()
            load_own.wait()
            own[...] = own[...] + land[...]
            stage[...] = own[...].astype(BF16)
            results = [pltpu.make_async_copy(stage, sum_hbm, local_sems.at[1]),
                       pltpu.make_async_copy(own.at[:, _chip_of(me)], mine_hbm, local_sems.at[2])]
            for cp in results:
                cp.start()
            for cp in results:
                cp.wait()

    kv_out = pl.BlockSpec((t, 128), lambda p, i: (0, p))
    tab_t = pl.BlockSpec((half, tq), lambda p, i: (0, i))
    tab_k = [pl.BlockSpec((None, t, 128), lambda p, i, a=a: (a, 0, 0)) for a in range(3)]
    return pl.pallas_call(
        body,
        name="attn_bwd",
        grid=(2, n_tiles),
        in_specs=[sp["q"], sp["k"], sp["v"], sp["act"], sp["act"], tab_t, tab_t, *tab_k, sp["sink"],
                  pl.BlockSpec(memory_space=pl.ANY)],
        out_specs=[sp["act"], kv_out, kv_out, pl.BlockSpec((1, 8, 128), lambda p, i: (p, 0, 0)),
                   pl.BlockSpec(memory_space=pl.ANY), pl.BlockSpec(memory_space=pl.ANY)],
        out_shape=[
            jax.ShapeDtypeStruct((t, D), BF16),
            jax.ShapeDtypeStruct((t, 256), BF16),
            jax.ShapeDtypeStruct((t, 256), BF16),
            jax.ShapeDtypeStruct((2, 8, 128), F32),
            jax.ShapeDtypeStruct((3, 4, ROW_SHARD, D), BF16),
            jax.ShapeDtypeStruct((3, ROW_SHARD, D), F32),
        ],
        scratch_shapes=[pltpu.VMEM((2, t + 128, 128), BF16), pltpu.VMEM((2, 128, t + 128), BF16), pltpu.VMEM((2, t + 128, 128), BF16),
                        pltpu.VMEM((2, 256, 128), F32), pltpu.VMEM((2, t + 128, 128), F32), pltpu.VMEM((2, t + 128, 128), F32),
                        pltpu.VMEM((3, 4, ROW_SHARD, D), F32), pltpu.VMEM((3, 4, ROW_SHARD, D), F32), pltpu.VMEM((3, 4, ROW_SHARD, D), BF16),
                        pltpu.SemaphoreType.DMA, pltpu.SemaphoreType.DMA, pltpu.SemaphoreType.DMA((3,))],
        compiler_params=_params(2),
    )(proj, proj, proj, o, d_o, cos_t, sin_t, lane_tables, lane_tables, lane_tables, sinks, part_3)


HALF = 512


def _softplus(y):
    u = jnp.exp(-jnp.abs(y))
    w = 1.0 + u
    log1p = jnp.where(w == 1.0, u, jnp.log(w) * (u / jnp.where(w == 1.0, 1.0, w - 1.0)))
    return jnp.maximum(y, 0.0) + log1p


def _gate_block(w_ref, blk):
    return w_ref[:, :, 256 * blk : 256 * (blk + 1)].reshape(256, 256)


def _gate_matmul(act_bf, w_ref):
    return jnp.concatenate([_dot(act_bf[:, 256 * blk : 256 * (blk + 1)], _gate_block(w_ref, blk)) for blk in range(2)], axis=1)


def _gate_matmul_t(grad_bf, w_ref):
    return jnp.concatenate([_dot_nt(grad_bf[:, 256 * blk : 256 * (blk + 1)], _gate_block(w_ref, blk)) for blk in range(2)], axis=1)


def _conv_and_gates(xe_ref, tm, cw_ref, cb_ref, wa_ref, wx_ref, ba_ref, bx_ref, lam_ref, keep):
    xc = cb_ref[...] + sum(cw_ref[k : k + 1, :] * xe_ref[pl.ds(5 + k, tm), :] for k in range(CONV_WIDTH))
    xc_bf = xc.astype(BF16)
    r = _sigmoid(_gate_matmul(xc_bf, wa_ref) + ba_ref[...])
    ig = _sigmoid(_gate_matmul(xc_bf, wx_ref) + bx_ref[...])
    sp = _softplus(-lam_ref[...])
    log_a = -LRU_C * r * sp
    tanh = jnp.tanh(-log_a)
    mult = jnp.sqrt(2.0 * tanh / (1.0 + tanh))
    a = jnp.where(keep, jnp.exp(log_a), 0.0)
    mult = jnp.where(keep, mult, 1.0)
    return xc, xc_bf, r, ig, sp, a, mult


def _scan_rows(a_ref, b_ref, h_ref, carry_ref, tm, reverse):
    width = a_ref.shape[1]
    row = lax.broadcasted_iota(jnp.int32, (8, width), 0)

    def group(g, carry):
        gg = tm // 8 - 1 - g if reverse else g
        r0 = pl.multiple_of(gg * 8, 8)
        a = a_ref[pl.ds(r0, 8), :]
        b = b_ref[pl.ds(r0, 8), :]
        for s in (1, 2, 4):
            shift = 8 - s if reverse else s
            a_sh = pltpu.roll(a, shift, 0)
            b_sh = pltpu.roll(b, shift, 0)
            ok = (row < 8 - s) if reverse else (row >= s)
            b = jnp.where(ok, a * b_sh + b, b)
            a = jnp.where(ok, a * a_sh, a)
        h = a * carry + b
        h_ref[pl.ds(r0, 8), :] = h
        last = h[0:1, :] if reverse else h[7:8, :]
        return jnp.broadcast_to(last, (8, width))

    carry_ref[...] = lax.fori_loop(0, tm // 8, group, carry_ref[...])


def _rnn_specs(t, tm, order):
    n = t // tm
    return dict(
        tile=pl.BlockSpec((tm, HALF), lambda ch, i: (order(i, n), ch)),
        before=lambda col0: pl.BlockSpec((8, HALF), lambda ch, i: (jnp.maximum(order(i, n) * (tm // 8) - 1, 0), col0 + ch)),
        proj_tile=lambda col0: pl.BlockSpec((tm, HALF), lambda ch, i: (order(i, n), col0 + ch)),
        keep=pl.BlockSpec((tm, 1), lambda ch, i: (order(i, n), 0)),
        gates=lambda which: pl.BlockSpec((N_DEV, GATE_SHARD, HALF), lambda ch, i: (0, PACK_GATE // GATE_SHARD + which, ch)),
        gate_grads=pl.BlockSpec((2, 2, 256, 256), lambda ch, i: (0, ch, 0, 0)),
        vec=pl.BlockSpec((1, HALF), lambda ch, i: (0, ch)),
        conv=pl.BlockSpec((CONV_WIDTH, HALF), lambda ch, i: (0, ch)),
    )


def _rnn_fwd(proj, keep, w_all, rg_ba, rg_bx, rg_lambda, conv_w, conv_b):
    t = proj.shape[0]
    tm = min(512, t)
    sp = _rnn_specs(t, tm, lambda i, n: i)

    def body(xr_ref, xb_ref, keep_ref, wa_ref, wx_ref, ba_ref, bx_ref, lam_ref, cw_ref, cb_ref, hh_ref, xe, a_buf, b_buf, carry):
        i = pl.program_id(1)

        @pl.when(i == 0)
        def _():
            carry[...] = jnp.zeros_like(carry)

        xe[0:8, :] = jnp.where(i > 0, xb_ref[...], 0.0)
        xe[8:, :] = xr_ref[...]
        keep_rows = keep_ref[...] != 0.0
        xc, _, _, ig, _, a, mult = _conv_and_gates(xe, tm, cw_ref, cb_ref, wa_ref, wx_ref, ba_ref, bx_ref, lam_ref, keep_rows)
        a_buf[...] = a
        b_buf[...] = mult * (ig * xc)
        _scan_rows(a_buf, b_buf, hh_ref, carry, tm, reverse=False)

    return pl.pallas_call(
        body,
        name="rnn_fwd",
        grid=(2, t // tm),
        in_specs=[sp["proj_tile"](SEG_XR // HALF), sp["before"](SEG_XR // HALF), sp["keep"], sp["gates"](0), sp["gates"](1),
                  sp["vec"], sp["vec"], sp["vec"], sp["conv"], sp["vec"]],
        out_specs=sp["tile"],
        out_shape=jax.ShapeDtypeStruct((t, D), F32),
        scratch_shapes=[pltpu.VMEM((tm + 8, HALF), F32), pltpu.VMEM((tm, HALF), F32), pltpu.VMEM((tm, HALF), F32), pltpu.VMEM((8, HALF), F32)],
        compiler_params=_params(2),
    )(proj, proj, keep, w_all, w_all, rg_ba, rg_bx, rg_lambda, conv_w, conv_b)


def _rnn_bwd(proj, hh, d_hh, keep, w_all, rg_ba, rg_bx, rg_lambda, conv_w, conv_b, sum_3_bf):
    t = proj.shape[0]
    tm = min(512, t)
    n_tiles = t // tm
    sp = _rnn_specs(t, tm, lambda i, n: n - 1 - i)

    def body(xr_ref, xb_ref, hh_ref, hb_ref, dhh_ref, keep_ref, wa_ref, wx_ref, ba_ref, bx_ref, lam_ref, cw_ref, cb_ref,
             s3_hbm, dxr_ref, gw_ref, gvec_ref, l3_hbm, xe, he, ae, c_buf, g_buf, dxe, carry, a_next, send_sems, recv_sems):
        ch = pl.program_id(0)
        i = pl.program_id(1)
        first_tile = i == n_tiles - 1
        views = [lambda ref, peer: ref.at[:, _chip_of(peer)]]

        @pl.when((ch == 0) & (i == 0))
        def _():
            for cp in _chip_copies([s3_hbm], [l3_hbm], views, send_sems, recv_sems):
                cp.start()

        @pl.when((ch == 1) & (i == n_tiles - 1))
        def _():
            for cp in _chip_copies([s3_hbm], [l3_hbm], views, send_sems, recv_sems):
                cp.wait_recv()
                cp.wait_send()

        @pl.when(i == 0)
        def _():
            carry[...] = jnp.zeros_like(carry)
            a_next[...] = jnp.zeros_like(a_next)
            dxe[tm:, :] = jnp.zeros((8, HALF), F32)
            gw_ref[...] = jnp.zeros_like(gw_ref)
            gvec_ref[...] = jnp.zeros_like(gvec_ref)

        xe[0:8, :] = jnp.where(first_tile, 0.0, xb_ref[...])
        xe[8:, :] = xr_ref[...]
        he[0:8, :] = jnp.where(first_tile, 0.0, hb_ref[...])
        he[8:, :] = hh_ref[...]
        keep_rows = keep_ref[...] != 0.0
        xc, xc_bf, r, ig, sp_, a, mult = _conv_and_gates(xe, tm, cw_ref, cb_ref, wa_ref, wx_ref, ba_ref, bx_ref, lam_ref, keep_rows)

        ae[0:tm, :] = a
        ae[tm:, :] = a_next[...]
        a_next[...] = jnp.broadcast_to(a[0:1, :], (8, HALF))
        c_buf[...] = ae[pl.ds(1, tm), :]
        g_buf[...] = dhh_ref[...]
        _scan_rows(c_buf, g_buf, g_buf, carry, tm, reverse=True)
        g = g_buf[...]

        h_prev = he[pl.ds(7, tm), :]
        bb = ig * xc
        d_ig = g * mult * xc
        d_xc = g * mult * ig
        d_log_a = jnp.where(keep_rows, g * h_prev * a - (g * bb) * (a * a) / mult, 0.0)
        d_r = d_log_a * (-LRU_C * sp_)
        d_sp = jnp.sum(d_log_a * (-LRU_C * r), axis=0, keepdims=True)
        d_zr = d_r * r * (1.0 - r)
        d_zi = d_ig * ig * (1.0 - ig)
        d_zr_bf = d_zr.astype(BF16)
        d_zi_bf = d_zi.astype(BF16)
        d_xc = d_xc + _gate_matmul_t(d_zr_bf, wa_ref) + _gate_matmul_t(d_zi_bf, wx_ref)
        for blk in range(2):
            cols = slice(256 * blk, 256 * (blk + 1))
            gw_ref[0, blk] = gw_ref[0, blk] + _dot_tn(xc_bf[:, cols], d_zr_bf[:, cols])
            gw_ref[1, blk] = gw_ref[1, blk] + _dot_tn(xc_bf[:, cols], d_zi_bf[:, cols])

        dxe[0:tm, :] = d_xc
        shifted = [dxe[pl.ds(3 - k, tm), :] for k in range(CONV_WIDTH)]
        dxr = sum(cw_ref[k : k + 1, :] * shifted[k] for k in range(CONV_WIDTH))
        dxr_ref[...] = dxr.astype(BF16)
        dxe[tm:, :] = d_xc[0:8, :]

        lam = lam_ref[...]
        xr_tile = xr_ref[...]
        sums = [
            jnp.sum(d_zr, axis=0, keepdims=True),
            jnp.sum(d_zi, axis=0, keepdims=True),
            d_sp * (-_sigmoid(-lam)),
            jnp.sum(d_xc, axis=0, keepdims=True),
        ] + [jnp.sum(xr_tile * shifted[k], axis=0, keepdims=True) for k in range(CONV_WIDTH)]
        gvec_ref[...] = gvec_ref[...] + jnp.concatenate(sums, axis=0)

    return pl.pallas_call(
        body,
        name="rnn_bwd",
        grid=(2, n_tiles),
        in_specs=[
            sp["proj_tile"](SEG_XR // HALF), sp["before"](SEG_XR // HALF), sp["tile"], sp["before"](0), sp["tile"], sp["keep"],
            sp["gates"](0), sp["gates"](1), sp["vec"], sp["vec"], sp["vec"], sp["conv"], sp["vec"], pl.BlockSpec(memory_space=pl.ANY),
        ],
        out_specs=[sp["tile"], sp["gate_grads"], pl.BlockSpec((8, HALF), lambda ch, i: (0, ch)), pl.BlockSpec(memory_space=pl.ANY)],
        out_shape=[
            jax.ShapeDtypeStruct((t, D), BF16),
            jax.ShapeDtypeStruct((2, 4, 256, 256), F32),
            jax.ShapeDtypeStruct((8, D), F32),
            jax.ShapeDtypeStruct((len(CHIP_FLIPS), 3, ROW_SHARD, D), BF16),
        ],
        scratch_shapes=[
            pltpu.VMEM((tm + 8, HALF), F32),
            pltpu.VMEM((tm + 8, HALF), F32),
            pltpu.VMEM((tm + 8, HALF), F32),
            pltpu.VMEM((tm, HALF), F32),
            pltpu.VMEM((tm, HALF), F32),
            pltpu.VMEM((tm + 8, HALF), F32),
            pltpu.VMEM((8, HALF), F32),
            pltpu.VMEM((8, HALF), F32),
            pltpu.SemaphoreType.DMA((len(CHIP_FLIPS),)),
            pltpu.SemaphoreType.DMA((len(CHIP_FLIPS),)),
        ],
        compiler_params=_params(2),
    )(proj, proj, hh, hh, d_hh, keep, w_all, w_all, rg_ba, rg_bx, rg_lambda, conv_w, conv_b, sum_3_bf)


def _mid(x, target, o, hh, proj, w_all, gate, final_g):
    t = x.shape[0]
    tm = min(256, t)
    n_tiles = t // tm
    full = pl.BlockSpec((tm, D), lambda i: (i, 0))
    half = lambda col: pl.BlockSpec((tm, 512), lambda i: (i, col))
    vec = pl.BlockSpec((1, D), lambda i: (0, 0))
    any_spec = pl.BlockSpec(memory_space=pl.ANY)

    def body(x_ref, tg_ref, o_ref, hh_ref, ga0, ga1, gr0, gr1, ma0, ma1, mr0, mr1, w_hbm, gate_ref, fg_ref,
             dres_ref, do_ref, dhh_ref, dga_ref, dgr_ref, dma_ref, dmr_ref, gw_hbm, vec_ref, w, gw, sem, w_sems):
        i = pl.program_id(0)

        @pl.when(i == 0)
        def _():
            loads = [
                pltpu.make_async_copy(
                    w_hbm.at[s, pl.ds(ROW_SHARD * k, ROW_SHARD), :], w.at[k, pl.ds(ROW_SHARD * s, ROW_SHARD), :], w_sems.at[k * N_DEV + s]
                )
                for k in range(3)
                for s in range(N_DEV)
            ]
            for cp in loads:
                cp.start()
            for cp in loads:
                cp.wait()
            gw[...] = jnp.zeros_like(gw)
            vec_ref[...] = jnp.zeros_like(vec_ref)

        cat = lambda a, b: jnp.concatenate([a[...], b[...]], axis=1)
        ga, gr, ma, mr = cat(ga0, ga1), cat(gr0, gr1), cat(ma0, ma1), cat(mr0, mr1)
        ov, hv, gate_v, fg = o_ref[...].astype(F32), hh_ref[...], gate_ref[...], fg_ref[...]
        sg_a = _sigmoid(ga)
        sg_r = _sigmoid(gr)
        silu_a = ga * sg_a
        silu_r = gr * sg_r
        ya = (ov * silu_a).astype(BF16)
        yr = (hv * silu_r).astype(BF16)
        pa = _dot(ya, w[0])
        pr = _dot(yr, w[1])
        sa = _sigmoid(ma)
        sr = _sigmoid(mr)
        merged = (sa * pa + sr * pr).astype(BF16)
        mo = _dot(merged, w[2])
        out = x_ref[...] + gate_v * mo
        r2 = lax.rsqrt(jnp.mean(out * out, axis=-1, keepdims=True) + NORM_EPS)
        outn = out * r2
        err = outn * fg - tg_ref[...]
        loss = 0.5 * jnp.sum(jnp.mean(err * err, axis=-1, keepdims=True), axis=0, keepdims=True)
        dy = err * (1.0 / D)
        g_final = jnp.sum(dy * outn, axis=0, keepdims=True)
        doutn = dy * fg
        dout = r2 * (doutn - outn * jnp.mean(doutn * outn, axis=-1, keepdims=True))
        dres_ref[...] = dout
        d_gate = jnp.sum(dout * mo, axis=0, keepdims=True)
        dmo = (dout * gate_v).astype(BF16)
        dmerged = _dot_nt(dmo, w[2])
        gw[2] = gw[2] + _dot_tn(merged, dmo)
        dpa = (dmerged * sa).astype(BF16)
        dpr = (dmerged * sr).astype(BF16)
        dma_ref[...] = (dmerged * pa * (sa * (1.0 - sa))).astype(BF16)
        dmr_ref[...] = (dmerged * pr * (sr * (1.0 - sr))).astype(BF16)
        dya = _dot_nt(dpa, w[0])
        dyr = _dot_nt(dpr, w[1])
        gw[0] = gw[0] + _dot_tn(ya, dpa)
        gw[1] = gw[1] + _dot_tn(yr, dpr)
        do_ref[...] = (dya * silu_a).astype(BF16)
        dhh_ref[...] = dyr * silu_r
        dga_ref[...] = (dya * ov * (sg_a * (1.0 + ga * (1.0 - sg_a)))).astype(BF16)
        dgr_ref[...] = (dyr * hv * (sg_r * (1.0 + gr * (1.0 - sg_r)))).astype(BF16)
        zeros = jnp.zeros((5, D), F32)
        vec_ref[...] = vec_ref[...] + jnp.concatenate([g_final, d_gate, jnp.broadcast_to(loss, (1, D)), zeros], axis=0)

        @pl.when(i == n_tiles - 1)
        def _():
            cp = pltpu.make_async_copy(gw, gw_hbm, sem)
            cp.start()
            cp.wait()

    bf_tile = jax.ShapeDtypeStruct((t, D), BF16)
    f_tile = jax.ShapeDtypeStruct((t, D), F32)
    return pl.pallas_call(
        body,
        name="mid",
        grid=(n_tiles,),
        in_specs=[full, full, full, full,
                  half(SEG_GA // 512), half(SEG_GA // 512 + 1), half(SEG_GR // 512), half(SEG_GR // 512 + 1),
                  half(SEG_MA // 512), half(SEG_MA // 512 + 1), half(SEG_MR // 512), half(SEG_MR // 512 + 1),
                  any_spec, vec, vec],
        out_specs=[full, full, full, full, full, full, full, any_spec, pl.BlockSpec((8, D), lambda i: (0, 0))],
        out_shape=[f_tile, bf_tile, f_tile, bf_tile, bf_tile, bf_tile, bf_tile,
                   jax.ShapeDtypeStruct((3, D, D), F32), jax.ShapeDtypeStruct((8, D), F32)],
        scratch_shapes=[pltpu.VMEM((3, D, D), BF16), pltpu.VMEM((3, D, D), F32), pltpu.SemaphoreType.DMA, pltpu.SemaphoreType.DMA((3 * N_DEV,))],
        compiler_params=_params(1),
    )(x, target, o, hh, proj, proj, proj, proj, proj, proj, proj, proj, w_all, gate, final_g)


_SEGMENTS = ((SEG_Q, 1024), (SEG_K, 256), (SEG_V, 256), (SEG_GA, 1024), (SEG_XR, 1024), (SEG_GR, 1024), (SEG_MA, 1024), (SEG_MR, 1024))


def _seg_specs(tm):
    return [pl.BlockSpec((tm, width), lambda i: (i, 0)) for _, width in _SEGMENTS]


def _bwd_dh(dsegs, w_all, x, dres, mod, norm_g):
    t = x.shape[0]
    tm = min(512, t)
    n_tiles = t // tm
    n_seg = len(_SEGMENTS)
    full = pl.BlockSpec((tm, D), lambda i: (i, 0))

    def body(*refs):
        segs = refs[:n_seg]
        w_hbm, x_ref, dres_ref, mod_ref, ng_ref, gx_ref, vec_ref, w, sems = refs[n_seg:]
        i = pl.program_id(0)

        @pl.when(i == 0)
        def _():
            loads = [
                pltpu.make_async_copy(w_hbm.at[s], w.at[pl.ds(W_IN_SHARD * s, W_IN_SHARD), :], sems.at[s])
                for s in range(N_DEV)
            ]
            for cp in loads:
                cp.start()
            for cp in loads:
                cp.wait()
            vec_ref[...] = jnp.zeros_like(vec_ref)

        dh = sum(_dot(seg[...], w[start : start + width, :]) for seg, (start, width) in zip(segs, _SEGMENTS))
        xv = x_ref[...]
        r1 = lax.rsqrt(jnp.mean(xv * xv, axis=-1, keepdims=True) + NORM_EPS)
        xn = xv * r1
        ng = ng_ref[...]
        scale1 = 1.0 + mod_ref[:, D : 2 * D]
        d_shift = jnp.sum(dh, axis=0, keepdims=True)
        d_scale = jnp.sum(dh * (xn * ng), axis=0, keepdims=True)
        g_norm = jnp.sum(dh * xn * scale1, axis=0, keepdims=True)
        dxn = dh * (ng * scale1)
        dx = r1 * (dxn - xn * jnp.mean(dxn * xn, axis=-1, keepdims=True))
        gx_ref[...] = dres_ref[...] + dx
        vec_ref[...] = vec_ref[...] + jnp.concatenate([d_shift, d_scale, g_norm, jnp.zeros((5, D), F32)], axis=0)

    any_spec = pl.BlockSpec(memory_space=pl.ANY)
    return pl.pallas_call(
        body,
        name="bwd_dh",
        grid=(n_tiles,),
        in_specs=_seg_specs(tm) + [any_spec, full, full, pl.BlockSpec((1, 3 * D), lambda i: (0, 0)), pl.BlockSpec((1, D), lambda i: (0, 0))],
        out_specs=[full, pl.BlockSpec((8, D), lambda i: (0, 0))],
        out_shape=[jax.ShapeDtypeStruct((t, D), F32), jax.ShapeDtypeStruct((8, D), F32)],
        scratch_shapes=[pltpu.VMEM((IN_W, D), BF16), pltpu.SemaphoreType.DMA((N_DEV,))],
        compiler_params=_params(1),
    )(*dsegs, w_all, x, dres, mod, norm_g)


_HBM = pl.BlockSpec(memory_space=pltpu.HBM)
_SEM = pl.BlockSpec(memory_space=pltpu.SEMAPHORE)
_EFFECT = pltpu.SideEffectType.DATAFLOW_SIDE_EFFECTING


def _w_exchange_copies(sum_ref, land_ref, sems):
    me = _position()
    n = len(CHIP_FLIPS)
    remote = [
        pltpu.make_async_remote_copy(
            src_ref=sum_ref.at[_chip_of(_flip(me, k))], dst_ref=land_ref.at[j], send_sem=sems[j], recv_sem=sems[n + j],
            device_id=_flip(me, k), device_id_type=MESH,
        )
        for j, k in enumerate(CHIP_FLIPS)
    ]
    return remote, []


def _vec_exchange_copies(vec_ref, land_ref, sems):
    me = _position()
    n = N_DEV - 1
    remote = [
        pltpu.make_async_remote_copy(
            src_ref=vec_ref, dst_ref=land_ref.at[_slot(me)], send_sem=sems[k - 1], recv_sem=sems[n + k - 1],
            device_id=_flip(me, k), device_id_type=MESH,
        )
        for k in range(1, N_DEV)
    ]
    return remote, [pltpu.make_async_copy(vec_ref, land_ref.at[_slot(me)], sems[2 * n])]


def _exchange_start(name, copies, n_sems, src, land_shape):
    land = lax.empty(land_shape, src.dtype)

    def body(src_ref, land_ref, *outs):
        remote, local = copies(src_ref, land_ref, outs[:n_sems])
        for cp in remote + local:
            cp.start()
        token = outs[n_sems + 2]
        token[...] = jnp.zeros_like(token)

    res = pl.pallas_call(
        body,
        name=name,
        out_shape=(*[pltpu.SemaphoreType.DMA(())] * n_sems, pltpu.HBM(src.shape, src.dtype), pltpu.HBM(land.shape, src.dtype),
                   jax.ShapeDtypeStruct((8, 128), F32)),
        in_specs=(_HBM, _HBM),
        out_specs=(*[_SEM] * n_sems, _HBM, _HBM, pl.BlockSpec(memory_space=pltpu.VMEM)),
        input_output_aliases={0: n_sems, 1: n_sems + 1},
        compiler_params=pltpu.CompilerParams(has_side_effects=_EFFECT),
    )(pltpu.with_memory_space_constraint(src, pltpu.HBM), pltpu.with_memory_space_constraint(land, pltpu.HBM))
    return res[:n_sems], res[n_sems], res[n_sems + 1], res[n_sems + 2]


def _exchange_wait(name, copies, sems, src_thru, land_thru, after):
    n_sems = len(sems)

    def body(src_ref, land_ref, *rest):
        remote, local = copies(src_ref, land_ref, rest[:n_sems])
        for cp in remote:
            cp.wait_send()
            cp.wait_recv()
        for cp in local:
            cp.wait()

    return pl.pallas_call(
        body,
        name=name,
        out_shape=(pltpu.HBM(src_thru.shape, src_thru.dtype), pltpu.HBM(land_thru.shape, land_thru.dtype)),
        in_specs=(_HBM, _HBM, *[_SEM] * n_sems, *[pl.BlockSpec(memory_space=pl.ANY)] * len(after)),
        out_specs=(_HBM, _HBM),
        input_output_aliases={0: 0, 1: 1},
        compiler_params=pltpu.CompilerParams(has_side_effects=_EFFECT),
    )(src_thru, land_thru, *sems, *after)[1]


def _bwd_gw(dsegs, h, g_gates):
    t = h.shape[0]
    tm = min(512, t)
    n_tiles = t // tm
    n_seg = len(_SEGMENTS)
    n_chips = N_DEV // 2

    def gate_copies(g_hbm, lg_hbm, send_sems, recv_sems, arriving):
        me = _position()
        return [
            pltpu.make_async_remote_copy(
                src_ref=g_hbm.at[:, :, _slot(_flip(me, k))],
                dst_ref=lg_hbm.at[_slot(_flip(me, k) if arriving else me)],
                send_sem=send_sems.at[k - 1],
                recv_sem=recv_sems.at[k - 1],
                device_id=_flip(me, k),
                device_id_type=MESH,
            )
            for k in range(1, N_DEV)
        ]

    def body(*refs):
        segs = refs[:n_seg]
        (h_ref, g_hbm, own_hbm, sum_hbm, lg_hbm, acc, stage, land, own_sems, pair_send, pair_recv, send_sems, recv_sems, g_local) = refs[n_seg:]
        i = pl.program_id(0)
        slot = _slot(_position())
        own_gates = pltpu.make_async_copy(g_hbm.at[:, :, slot], lg_hbm.at[slot], g_local)

        @pl.when(i == 0)
        def _():
            own_gates.start()
            for cp in gate_copies(g_hbm, lg_hbm, send_sems, recv_sems, arriving=False):
                cp.start()
            acc[...] = jnp.zeros_like(acc)

        hv = h_ref[...]
        for seg, (start, width) in zip(segs, _SEGMENTS):
            acc[start : start + width, :] = acc[start : start + width, :] + _dot_tn(seg[...], hv)

        @pl.when(i == n_tiles - 1)
        def _():
            me = _position()
            sibling = _flip(me, 1)

            def rows(q, c):
                return pl.ds(pl.multiple_of(W_IN_SHARD * (2 * q + c), 8), W_IN_SHARD)

            def to_sibling(q):
                return pltpu.make_async_remote_copy(
                    src_ref=stage.at[q % 2], dst_ref=land.at[q], send_sem=pair_send.at[q], recv_sem=pair_recv.at[q],
                    device_id=sibling, device_id_type=MESH,
                )

            for q in range(n_chips):
                if q >= 2:
                    to_sibling(q - 2).wait_send()
                stage[q % 2] = acc[rows(q, sibling[2]), :].astype(BF16)
                to_sibling(q).start()
            for q in range(n_chips - 2, n_chips):
                to_sibling(q).wait_send()
            sums = [pltpu.make_async_copy(stage.at[q % 2], sum_hbm.at[q], own_sems.at[q]) for q in range(n_chips)]
            for q in range(n_chips):
                to_sibling(q).wait_recv()
                if q >= 2:
                    sums[q - 2].wait()
                total = acc[rows(q, me[2]), :] + land[q].astype(F32)
                acc[rows(q, me[2]), :] = total
                stage[q % 2] = total.astype(BF16)
                sums[q].start()
            mine = pltpu.make_async_copy(acc.at[rows(_chip_of(me), me[2]), :], own_hbm, own_sems.at[n_chips])
            mine.start()
            for q in range(n_chips - 2, n_chips):
                sums[q].wait()
            mine.wait()
            for cp in gate_copies(g_hbm, lg_hbm, send_sems, recv_sems, arriving=True):
                cp.wait_recv()
            for cp in gate_copies(g_hbm, lg_hbm, send_sems, recv_sems, arriving=False):
                cp.wait_send()
            own_gates.wait()

    any_spec = pl.BlockSpec(memory_space=pl.ANY)
    n_copies = N_DEV - 1
    return pl.pallas_call(
        body,
        name="bwd_gw",
        grid=(n_tiles,),
        in_specs=_seg_specs(tm) + [pl.BlockSpec((tm, D), lambda i: (i, 0)), any_spec],
        out_specs=[any_spec, any_spec, any_spec],
        out_shape=[jax.ShapeDtypeStruct((W_IN_SHARD, D), F32),
                   jax.ShapeDtypeStruct((n_chips, W_IN_SHARD, D), BF16),
                   jax.ShapeDtypeStruct((N_DEV, 2, 4, GATE_SHARD, 256), F32)],
        scratch_shapes=[pltpu.VMEM((IN_W, D), F32), pltpu.VMEM((2, W_IN_SHARD, D), BF16), pltpu.VMEM((n_chips, W_IN_SHARD, D), BF16),
                        pltpu.SemaphoreType.DMA((n_chips + 1,)), pltpu.SemaphoreType.DMA((n_chips,)), pltpu.SemaphoreType.DMA((n_chips,)),
                        pltpu.SemaphoreType.DMA((n_copies,)), pltpu.SemaphoreType.DMA((n_copies,)), pltpu.SemaphoreType.DMA],
        compiler_params=_params(1),
    )(*dsegs, h, g_gates)


def _adamw(w, g, m, v):
    m = ADAM_B1 * m + (1.0 - ADAM_B1) * g
    v = ADAM_B2 * v + (1.0 - ADAM_B2) * (g * g)
    m_hat = m / (1.0 - ADAM_B1**ADAM_STEP)
    v_hat = v / (1.0 - ADAM_B2**ADAM_STEP)
    delta = -ADAM_LR * (m_hat / (jnp.sqrt(v_hat) + ADAM_EPS) + ADAM_WD * w)
    return delta, m, v


def _w_in_update(own, land, w_t, m_t, v_t, token):
    n_slots, rows, cols = land.shape
    rb = 208
    spec = pl.BlockSpec((rb, cols), lambda i: (i, 0))

    def body(o_ref, l_ref, w_ref, m_ref, v_ref, token_ref, g_ref, d_ref, m2_ref, v2_ref):
        total = o_ref[...]
        for s in range(n_slots):
            total = total + l_ref[s].astype(F32)
        g_ref[...] = total
        d_ref[...], m2_ref[...], v2_ref[...] = _adamw(w_ref[...], total, m_ref[...], v_ref[...])

    shape = jax.ShapeDtypeStruct((rows, cols), F32)
    return pl.pallas_call(
        body,
        name="w_in_update",
        grid=(rows // rb,),
        in_specs=[spec, pl.BlockSpec((n_slots, rb, cols), lambda i: (0, i, 0)), spec, spec, spec, pl.BlockSpec(token.shape, lambda i: (0, 0))],
        out_specs=[spec] * 4,
        out_shape=[shape] * 4,
        compiler_params=_params(1),
    )(own, land, w_t, m_t, v_t, token)


def _shard_update(own_3, land_3, land_g, weights, m_in, v_in, token):
    n = len(weights)

    def body(*refs):
        o3_ref, l3_ref, lg_ref = refs[:3]
        w_refs, m_refs, v_refs = (refs[3 + n * k : 3 + n * (k + 1)] for k in range(3))
        outs = refs[4 + 3 * n :]
        for k in range(n):
            if k < 3:
                g = o3_ref[k]
                for slot in range(3):
                    g = g + l3_ref[slot, k].astype(F32)
            else:
                g = lg_ref[0, k - 3]
                for slot in range(1, N_DEV):
                    g = g + lg_ref[slot, k - 3]
            d, m2, v2 = _adamw(w_refs[k][0], g, m_refs[k][0], v_refs[k][0])
            for which, val in enumerate((g, d, m2, v2)):
                outs[which * n + k][0] = val

    shapes = [jax.ShapeDtypeStruct(w.shape, F32) for w in weights]
    res = pl.pallas_call(
        body, name="shard_update", out_shape=shapes * 4, compiler_params=pltpu.CompilerParams(vmem_limit_bytes=VMEM_LIMIT)
    )(own_3, land_3, land_g, *weights, *m_in, *v_in, token)
    return [res[n * k : n * (k + 1)] for k in range(4)]


VEC_LAYOUT = (("b_ada", 3 * D), ("norm_g", D), ("attn_sinks", 128), ("conv_b", D), ("rg_ba", D), ("rg_bx", D), ("rg_lambda", D),
              ("final_g", D), ("conv_w", CONV_WIDTH * D), ("loss", 128))
VEC_OFFSET = {name: sum(w for _, w in VEC_LAYOUT[:k]) for k, (name, _) in enumerate(VEC_LAYOUT)}
VEC_ROWS = 112
VEC_WIDTH = VEC_ROWS * 128
assert sum(w for _, w in VEC_LAYOUT) <= VEC_WIDTH
N_REPLICATED = 8


def _small_update(vec_all, c_all_t, me, rep_w, rep_m, rep_v, ada, conv):
    n_ada = ada[0].shape[1]
    n_conv = conv[0].shape[1]

    def body(*refs):
        me_ref, vec_ref, ct_ref = refs[:3]
        w_refs, m_refs, v_refs = (refs[3 + N_REPLICATED * k : 3 + N_REPLICATED * (k + 1)] for k in range(3))
        pos = 3 + 3 * N_REPLICATED
        ada_refs, conv_refs = refs[pos : pos + 3], refs[pos + 3 : pos + 6]
        outs = refs[pos + 6 :]
        loss_ref, rep_outs = outs[0], outs[1 : 1 + 4 * N_REPLICATED]
        ada_outs, conv_outs = outs[1 + 4 * N_REPLICATED : 5 + 4 * N_REPLICATED], outs[5 + 4 * N_REPLICATED : 9 + 4 * N_REPLICATED]
        conv_buf = outs[9 + 4 * N_REPLICATED]

        total = vec_ref[0:1, :]
        for slot in range(1, N_DEV):
            total = total + vec_ref[slot : slot + 1, :]
        loss_ref[...] = total[:, VEC_OFFSET["loss"] : VEC_OFFSET["loss"] + 1]
        for k, (name, _) in enumerate(VEC_LAYOUT[:N_REPLICATED]):
            width = w_refs[k].shape[1]
            g = total[:, VEC_OFFSET[name] : VEC_OFFSET[name] + width]
            d, m2, v2 = _adamw(w_refs[k][...], g, m_refs[k][...], v_refs[k][...])
            for which, val in enumerate((g, d, m2, v2)):
                rep_outs[which * N_REPLICATED + k][...] = val

        device = me_ref[0]
        for k in range(CONV_WIDTH):
            conv_buf[k : k + 1, :] = total[:, VEC_OFFSET["conv_w"] + D * k : VEC_OFFSET["conv_w"] + D * (k + 1)]
        g_conv = conv_buf[:, pl.ds(pl.multiple_of(device * n_conv, 128), n_conv)]
        for ref, val in zip(conv_outs, (g_conv,) + _adamw(conv_refs[0][...], g_conv, conv_refs[1][...], conv_refs[2][...])):
            ref[...] = val

        dmod = vec_ref[:, pl.ds(pl.multiple_of(device * n_ada, 128), n_ada)]
        g_ada = _dot(ct_ref[...], dmod)
        for ref, val in zip(ada_outs, (g_ada,) + _adamw(ada_refs[0][...], g_ada, ada_refs[1][...], ada_refs[2][...])):
            ref[...] = val

    vmem = pl.BlockSpec(memory_space=pltpu.VMEM)
    rep_shapes = [jax.ShapeDtypeStruct(w.shape, F32) for w in rep_w]
    out_shape = ([jax.ShapeDtypeStruct((1, 1), F32)] + rep_shapes * 4 + [jax.ShapeDtypeStruct(ada[0].shape, F32)] * 4
                 + [jax.ShapeDtypeStruct(conv[0].shape, F32)] * 4)
    n_in = 3 + 3 * N_REPLICATED + 6
    res = pl.pallas_call(
        body,
        name="small_update",
        in_specs=[pl.BlockSpec(memory_space=pltpu.SMEM)] + [vmem] * (n_in - 1),
        out_specs=[vmem] * len(out_shape),
        out_shape=out_shape,
        scratch_shapes=[pltpu.VMEM((CONV_WIDTH, D), F32)],
        compiler_params=pltpu.CompilerParams(vmem_limit_bytes=VMEM_LIMIT),
    )(me, vec_all, c_all_t, *rep_w, *rep_m, *rep_v, *ada, *conv)
    loss = res[0]
    reps = [res[1 + N_REPLICATED * k : 1 + N_REPLICATED * (k + 1)] for k in range(4)]
    return loss, reps, res[1 + 4 * N_REPLICATED : 5 + 4 * N_REPLICATED], res[5 + 4 * N_REPLICATED : 9 + 4 * N_REPLICATED]


def kernel(x, c, positions, w_ada, b_ada, norm_g, w_in, attn_sinks, conv_w, conv_b, rg_wa, rg_ba, rg_wx, rg_bx, rg_lambda, w_attn_proj, w_rnn_proj, w_out, final_g, loss_target, m_w_ada, m_b_ada, m_norm_g, m_w_in, m_attn_sinks, m_conv_w, m_conv_b, m_rg_wa, m_rg_ba, m_rg_wx, m_rg_bx, m_rg_lambda, m_w_attn_proj, m_w_rnn_proj, m_w_out, m_final_g, v_w_ada, v_b_ada, v_norm_g, v_w_in, v_attn_sinks, v_conv_w, v_conv_b, v_rg_wa, v_rg_ba, v_rg_wx, v_rg_bx, v_rg_lambda, v_w_attn_proj, v_w_rnn_proj, v_w_out, v_final_g):
    my_x, my_y, my_c = _position()
    me = _slot((my_x, my_y, my_c))
    x2, target = x[0], loss_target[0]
    pos = positions[0]

    gate_rows = lambda w: w[0].transpose(1, 0, 2).reshape(GATE_SHARD, D)
    w_pack = jnp.concatenate([w_attn_proj[0], w_rnn_proj[0], w_out[0], gate_rows(rg_wa), gate_rows(rg_wx)], axis=0).astype(BF16)
    conv_rows = jnp.pad(conv_w[0], ((0, 8 - CONV_WIDTH), (0, 0)))
    w_in_t = w_in[0].T
    n_ada = w_ada.shape[2]
    b_cols = lax.dynamic_slice(b_ada, (0, me * n_ada), (1, n_ada))

    tables = _rope_tables(pos)
    where = jnp.stack([_chip_of((my_x, my_y)), me]).astype(jnp.int32)
    proj, h, mod, lane_tables, w_t_all, w_all, c_rows, conv_all = _fwd_in(
        x2, jnp.broadcast_to(c, (8, D)), conv_rows, w_ada[0], b_cols, norm_g, w_in_t.astype(BF16), tables, w_pack, where
    )
    c_all = c_rows[:, 0, :]
    conv_full = conv_all[:, :CONV_WIDTH, :].transpose(1, 0, 2).reshape(CONV_WIDTH, D)
    gate = mod[:, 2 * D :]
    o = _attn_fwd(proj, attn_sinks)
    keep = (pos != 0).astype(F32)[:, None]
    hh = _rnn_fwd(proj, keep, w_all, rg_ba, rg_bx, rg_lambda, conv_full, conv_b)

    dres, d_o, d_hh, d_ga, d_gr, d_ma, d_mr, gw3, mid_vec = _mid(x2, target, o, hh, proj, w_all, gate, final_g[None, :])
    part_3 = gw3.reshape(3, 4, 2, ROW_SHARD, D)
    d_q, d_k, d_v, d_sink, sum_3_bf, own_3 = _attn_bwd(proj, o, d_o, (lane_tables, tables), attn_sinks, part_3)
    d_xr, g_gates, rnn_vec, land_3 = _rnn_bwd(proj, hh, d_hh, keep, w_all, rg_ba, rg_bx, rg_lambda, conv_full, conv_b, sum_3_bf)
    dsegs = (d_q, d_k, d_v, d_ga, d_xr, d_gr, d_ma, d_mr)
    own_w, sum_w_bf, land_g = _bwd_gw(dsegs, h, g_gates.reshape(2, 4, N_DEV, GATE_SHARD, 256))
    w_sems, sum_w_thru, land_w_thru, token = _exchange_start(
        "w_exchange_start", _w_exchange_copies, 2 * len(CHIP_FLIPS), sum_w_bf, (len(CHIP_FLIPS), W_IN_SHARD, D)
    )
    grad_x, dh_vec = _bwd_dh(dsegs, w_t_all, x2, dres, mod + token[0:1, 0:1], norm_g)
    land_w = _exchange_wait("w_exchange_wait", _w_exchange_copies, w_sems, sum_w_thru, land_w_thru, (dh_vec,))

    pieces = dict(
        b_ada=jnp.concatenate([dh_vec[0:1], dh_vec[1:2], mid_vec[1:2]], axis=1),
        norm_g=dh_vec[2:3],
        attn_sinks=jnp.pad(d_sink[:, :, 0].reshape(1, 16), ((0, 0), (0, 112))),
        conv_b=rnn_vec[3:4],
        rg_ba=rnn_vec[0:1],
        rg_bx=rnn_vec[1:2],
        rg_lambda=rnn_vec[2:3],
        final_g=mid_vec[0:1],
        conv_w=rnn_vec[4:8].reshape(1, CONV_WIDTH * D),
        loss=mid_vec[2:3, 0:128],
    )
    vec = jnp.concatenate([pieces[name] for name, _ in VEC_LAYOUT], axis=1)
    vec = jnp.pad(vec, ((0, 0), (0, VEC_WIDTH - vec.shape[1]))).reshape(VEC_ROWS, 128)
    vec_sems, vec_thru, vec_land, vec_token = _exchange_start(
        "vec_exchange_start", _vec_exchange_copies, 2 * N_DEV - 1, vec, (N_DEV, VEC_ROWS, 128)
    )
    w_in_res = _w_in_update(own_w, land_w, w_in_t, m_w_in[0].T, v_w_in[0].T, vec_token)
    shard_res = _shard_update(
        own_3, land_3, land_g,
        (w_attn_proj, w_rnn_proj, w_out, rg_wa, rg_wx),
        (m_w_attn_proj, m_w_rnn_proj, m_w_out, m_rg_wa, m_rg_wx),
        (v_w_attn_proj, v_w_rnn_proj, v_w_out, v_rg_wa, v_rg_wx),
        vec_token,
    )
    vec_all = _exchange_wait(
        "vec_exchange_wait", _vec_exchange_copies, vec_sems, vec_thru, vec_land, (w_in_res[0], shard_res[0][0])
    )
    rep_w = [b_ada, norm_g, attn_sinks, conv_b, rg_ba, rg_bx, rg_lambda, final_g[None, :]]
    rep_m = [m_b_ada, m_norm_g, m_attn_sinks, m_conv_b, m_rg_ba, m_rg_bx, m_rg_lambda, m_final_g[None, :]]
    rep_v = [v_b_ada, v_norm_g, v_attn_sinks, v_conv_b, v_rg_ba, v_rg_bx, v_rg_lambda, v_final_g[None, :]]
    loss, reps, ada_res, conv_res = _small_update(
        vec_all.reshape(N_DEV, VEC_WIDTH), c_all.T, me.astype(jnp.int32).reshape(1), rep_w, rep_m, rep_v,
        (w_ada[0], m_w_ada[0], v_w_ada[0]), (conv_w[0], m_conv_w[0], v_conv_w[0]),
    )

    def by_weight(k):
        b, n, s, cb, ba, bx, lam, fg = reps[k]
        ap, rp, out, wa, wx = shard_res[k]
        return [ada_res[k][None], b, n, w_in_res[k].T[None], s, conv_res[k][None], cb, wa, ba, wx, bx, lam, ap, rp, out, fg[0]]

    return (loss[0, 0], grad_x[None], *by_weight(0), *by_weight(1), *by_weight(2), *by_weight(3))
```

```python
import math

import jax
import jax.numpy as jnp
from jax import lax
from jax.experimental import pallas as pl
from jax.experimental.pallas import tpu as pltpu

F32 = jnp.float32
BF16 = jnp.bfloat16

D = 1024
HEAD_DIM = 64
ROT_DIM = 16
ROPE_THETA = 500000.0
WINDOW = 128
LRU_C = 8.0
CONV_WIDTH = 4
NORM_EPS = 1e-6
IN_W = 6656
SEG_Q, SEG_K, SEG_V, SEG_GA, SEG_XR, SEG_GR, SEG_MA, SEG_MR = 0, 1024, 1280, 1536, 2560, 3584, 4608, 5632
N_DEV = 8
W_IN_SHARD = IN_W // N_DEV
ROW_SHARD = D // N_DEV
GATE_SHARD = 256 // N_DEV

ADAM_LR = 0.001
ADAM_B1 = 0.9
ADAM_B2 = 0.999
ADAM_EPS = 1e-08
ADAM_WD = 0.01
ADAM_STEP = 10

VMEM_LIMIT = 60 * 1024 * 1024
MESH = pl.DeviceIdType.MESH


def _dot(a, b):
    return jnp.dot(a, b, preferred_element_type=F32)


def _dot_nt(a, b):
    return lax.dot_general(a, b, (((1,), (1,)), ((), ())), preferred_element_type=F32)


def _dot_tn(a, b):
    return lax.dot_general(a, b, (((0,), (0,)), ((), ())), preferred_element_type=F32)


def _sigmoid(z):
    return 1.0 / (1.0 + jnp.exp(-z))


def _params(n_grid, vmem=VMEM_LIMIT):
    return pltpu.CompilerParams(dimension_semantics=("arbitrary",) * n_grid, vmem_limit_bytes=vmem)


def _position():
    return lax.axis_index("x"), lax.axis_index("y"), lax.axis_index("c")


def _flip(pos, k):
    x, y, c = pos
    return (1 - x if k & 4 else x, 1 - y if k & 2 else y, 1 - c if k & 1 else c)


def _slot(pos):
    return 4 * pos[0] + 2 * pos[1] + pos[2]


def _direct_gather_copies(src_hbm, all_hbm, send_sems, recv_sems, arriving):
    me = _position()
    return [
        pltpu.make_async_remote_copy(
            src_ref=src_hbm,
            dst_ref=all_hbm.at[_slot(_flip(me, k) if arriving else me)],
            send_sem=send_sems.at[k - 1],
            recv_sem=recv_sems.at[k - 1],
            device_id=_flip(me, k),
            device_id_type=MESH,
        )
        for k in range(1, N_DEV)
    ]


CHIP_FLIPS = (2, 4, 6)


def _chip_of(pos):
    return 2 * pos[0] + pos[1]


def _chip_copies(srcs, lands, views, send_sems, recv_sems):
    me = _position()
    copies = []
    for a, (src, land) in enumerate(zip(srcs, lands)):
        for j, k in enumerate(CHIP_FLIPS):
            peer = _flip(me, k)
            copies.append(
                pltpu.make_async_remote_copy(
                    src_ref=views[a](src, peer),
                    dst_ref=land.at[j],
                    send_sem=send_sems.at[len(CHIP_FLIPS) * a + j],
                    recv_sem=recv_sems.at[len(CHIP_FLIPS) * a + j],
                    device_id=peer,
                    device_id_type=MESH,
                )
            )
    return copies


PACK_GATE = 3 * ROW_SHARD
PACK_ROWS = PACK_GATE + 2 * GATE_SHARD


def _fwd_in(x, c_rows, conv_rows, w_ada, b_cols, norm_g, w_shard, tables, pack, where):
    t = x.shape[0]
    tm = min(1024, t)
    tn = IN_W // 4
    n_i, n_j = t // tm, IN_W // tn
    n_rope = (SEG_V - SEG_Q) // 128

    n_ada = w_ada.shape[1]

    def body(where_ref, x_ref, wada_ref, bcols_ref, ng_ref, ct_ref, st_ref, shard_hbm, pack_hbm, c_hbm, conv_hbm,
             proj_ref, h_ref, mod_ref, lane_hbm, wt_hbm, all_hbm, call_hbm, convall_hbm,
             wall, hbuf, cbuf, mcols, mod_all, mod_buf, lane,
             w_send, w_recv, load_sems, local_sem, lane_sem, p_send, p_recv, p_local,
             c_send, c_recv, c_local, v_send, v_recv, v_local, m_send, m_recv):
        j = pl.program_id(0)
        i = pl.program_id(1)
        me = _position()
        sibling = _flip(me, 1)
        chip = where_ref[0]
        slot = where_ref[1]

        def modulation():
            own_c = pltpu.make_async_copy(c_hbm, call_hbm.at[_slot(me)], c_local)
            own_v = pltpu.make_async_copy(conv_hbm, convall_hbm.at[_slot(me)], v_local)
            own_c.start()
            own_v.start()
            for cp in _direct_gather_copies(c_hbm, call_hbm, c_send, c_recv, arriving=False):
                cp.start()
            for cp in _direct_gather_copies(conv_hbm, convall_hbm, v_send, v_recv, arriving=False):
                cp.start()
            for cp in _direct_gather_copies(c_hbm, call_hbm, c_send, c_recv, arriving=True):
                cp.wait_recv()
            own_c.wait()
            fetch = pltpu.make_async_copy(call_hbm, cbuf, c_local)
            fetch.start()
            fetch.wait()
            mcols[...] = _dot(cbuf[:, 0, :].astype(BF16), wada_ref[...].astype(BF16)) + bcols_ref[...]
            mod_all[slot] = mcols[...]
            swaps = lambda arriving: [
                pltpu.make_async_remote_copy(
                    src_ref=mcols,
                    dst_ref=mod_all.at[_slot(_flip(me, k) if arriving else me)],
                    send_sem=m_send.at[k - 1],
                    recv_sem=m_recv.at[k - 1],
                    device_id=_flip(me, k),
                    device_id_type=MESH,
                )
                for k in range(1, N_DEV)
            ]
            for cp in swaps(False):
                cp.start()
            for cp in swaps(True):
                cp.wait_recv()
            mine = lax.broadcasted_iota(jnp.int32, (N_DEV, n_ada), 0) == slot
            for s in range(N_DEV):
                row = jnp.sum(jnp.where(mine, mod_all[s], 0.0), axis=0, keepdims=True)
                mod_buf[:, n_ada * s : n_ada * (s + 1)] = row
            for cp in swaps(False):
                cp.wait_send()
            for cp in _direct_gather_copies(c_hbm, call_hbm, c_send, c_recv, arriving=False):
                cp.wait_send()
            for cp in _direct_gather_copies(conv_hbm, convall_hbm, v_send, v_recv, arriving=True):
                cp.wait_recv()
            for cp in _direct_gather_copies(conv_hbm, convall_hbm, v_send, v_recv, arriving=False):
                cp.wait_send()
            own_v.wait()

        def shard_of(block):
            return wall.at[_chip_of(block), pl.ds(W_IN_SHARD * block[2], W_IN_SHARD), :]

        def shard_copy(k, block, to, src=None):
            return pltpu.make_async_remote_copy(
                src_ref=shard_of(block),
                dst_ref=shard_of(block),
                send_sem=w_send.at[k],
                recv_sem=w_recv.at[k],
                device_id=to,
                device_id_type=MESH,
            )

        own_shard = pltpu.make_async_copy(shard_hbm, shard_of(me), local_sem)
        own_pack = pltpu.make_async_copy(pack_hbm, all_hbm.at[_slot(me)], p_local)
        keep_lane = pltpu.make_async_copy(lane, lane_hbm, lane_sem)

        def lane_tables():
            d = lax.broadcasted_iota(jnp.int32, (128, 128), 0) % HEAD_DIM
            half = ROT_DIM // 2

            def fill(b, carry):
                cols = pl.ds(pl.multiple_of(128 * b, 128), 128)
                cos = jnp.concatenate([ct_ref[:, cols]] * (128 // half), axis=0)
                sin = jnp.concatenate([st_ref[:, cols]] * (128 // half), axis=0)
                lane[0, cols, :] = jnp.where(d < ROT_DIM, cos, 1.0).T
                lane[1, cols, :] = jnp.where(d < half, -sin, 0.0).T
                lane[2, cols, :] = jnp.where((d >= half) & (d < ROT_DIM), sin, 0.0).T
                return carry

            lax.fori_loop(0, t // 128, fill, 0)

        def keep_tile(q):
            return [
                pltpu.make_async_copy(wall.at[q, pl.ds(W_IN_SHARD * c, W_IN_SHARD), :], wt_hbm.at[2 * q + c], load_sems.at[2 * (q ^ chip) + c])
                for c in range(2)
            ]

        @pl.when((j == 0) & (i == 0))
        def _():
            modulation()
            mod_ref[...] = mod_buf[...]
            own_shard.start()
            own_shard.wait()
            shard_copy(0, me, sibling).start()
            for rel in range(1, 4):
                shard_copy(rel, me, _flip(me, 2 * rel)).start()
            own_pack.start()
            for cp in _direct_gather_copies(pack_hbm, all_hbm, p_send, p_recv, arriving=False):
                cp.start()
            lane_tables()
            keep_lane.start()
            shard_copy(0, sibling, me).wait_recv()
            for cp in keep_tile(chip):
                cp.start()

        for rel in range(1, 4):

            @pl.when((j == 1) & (i == min(rel, n_i) - 1))
            def _(rel=rel):
                there = _flip(me, 2 * rel)
                shard_copy(rel, there, me).wait_recv()
                shard_copy(3 + rel, there, sibling).start()

            @pl.when((j == rel) & (i == 0))
            def _(rel=rel):
                shard_copy(3 + rel, _flip(sibling, 2 * rel), me).wait_recv()
                for cp in keep_tile(chip ^ rel):
                    cp.start()

        rows = pl.ds(pl.multiple_of(i * tm, tm), tm)

        @pl.when(j == 0)
        def _():
            xv = x_ref[...]
            r = lax.rsqrt(jnp.mean(xv * xv, axis=-1, keepdims=True) + NORM_EPS)
            shift = mod_buf[:, 0:D]
            scale = mod_buf[:, D : 2 * D]
            hv = (((xv * r) * ng_ref[...]) * (1.0 + scale) + shift).astype(BF16)
            hbuf[rows, :] = hv
            h_ref[...] = hv

        proj_ref[...] = _dot_nt(hbuf[rows, :], wall[chip ^ j])

        @pl.when((chip ^ j) == 0)
        def _():
            cos, s1, s2 = lane[0, rows, :], lane[1, rows, :], lane[2, rows, :]
            for g in range(n_rope):
                cols = slice(128 * g, 128 * (g + 1))
                proj_ref[:, cols] = _rope(proj_ref[:, cols], cos, s1, s2)

        @pl.when((j == n_j - 1) & (i == n_i - 1))
        def _():
            shard_copy(0, me, sibling).wait_send()
            for rel in range(1, 4):
                shard_copy(rel, me, _flip(me, 2 * rel)).wait_send()
                shard_copy(3 + rel, _flip(me, 2 * rel), sibling).wait_send()
            for q in range(N_DEV // 2):
                for cp in keep_tile(q):
                    cp.wait()
            for cp in _direct_gather_copies(pack_hbm, all_hbm, p_send, p_recv, arriving=True):
                cp.wait_recv()
            for cp in _direct_gather_copies(pack_hbm, all_hbm, p_send, p_recv, arriving=False):
                cp.wait_send()
            own_pack.wait()
            keep_lane.wait()

    tab = pl.BlockSpec((ROT_DIM // 2, t), lambda j, i, c: (0, 0))
    any_spec = pl.BlockSpec(memory_space=pl.ANY)
    return pl.pallas_call(
        body,
        name="fwd_in",
        grid_spec=pltpu.PrefetchScalarGridSpec(
            num_scalar_prefetch=1,
            grid=(n_j, n_i),
            in_specs=[
                pl.BlockSpec((tm, D), lambda j, i, c: (jnp.where(j == 0, i, 0), 0)),
                pl.BlockSpec(w_ada.shape, lambda j, i, c: (0, 0)),
                pl.BlockSpec((1, n_ada), lambda j, i, c: (0, 0)),
                pl.BlockSpec((1, D), lambda j, i, c: (0, 0)),
                tab, tab, any_spec, any_spec, any_spec, any_spec,
            ],
            out_specs=[pl.BlockSpec((tm, tn), lambda j, i, c: (i, c[0] ^ j)),
                       pl.BlockSpec((tm, D), lambda j, i, c: (jnp.where(j == 0, i, n_i - 1), 0)),
                       pl.BlockSpec((1, 3 * D), lambda j, i, c: (0, 0)),
                       any_spec, any_spec, any_spec, any_spec, any_spec],
            scratch_shapes=[
                pltpu.VMEM((N_DEV // 2, tn, D), BF16),
                pltpu.VMEM((t, D), BF16),
                pltpu.VMEM((N_DEV,) + c_rows.shape, F32),
                pltpu.VMEM((N_DEV, n_ada), F32),
                pltpu.VMEM((N_DEV, N_DEV, n_ada), F32),
                pltpu.VMEM((1, 3 * D), F32),
                pltpu.VMEM((3, t, 128), F32),
                pltpu.SemaphoreType.DMA((N_DEV - 1,)),
                pltpu.SemaphoreType.DMA((N_DEV - 1,)),
                pltpu.SemaphoreType.DMA((N_DEV,)),
                pltpu.SemaphoreType.DMA,
                pltpu.SemaphoreType.DMA,
                pltpu.SemaphoreType.DMA((N_DEV - 1,)),
                pltpu.SemaphoreType.DMA((N_DEV - 1,)),
                pltpu.SemaphoreType.DMA,
                pltpu.SemaphoreType.DMA((N_DEV - 1,)),
                pltpu.SemaphoreType.DMA((N_DEV - 1,)),
                pltpu.SemaphoreType.DMA,
                pltpu.SemaphoreType.DMA((N_DEV - 1,)),
                pltpu.SemaphoreType.DMA((N_DEV - 1,)),
                pltpu.SemaphoreType.DMA,
                pltpu.SemaphoreType.DMA((N_DEV - 1,)),
                pltpu.SemaphoreType.DMA((N_DEV - 1,)),
            ],
        ),
        out_shape=[jax.ShapeDtypeStruct((t, IN_W), F32), jax.ShapeDtypeStruct((t, D), BF16),
                   jax.ShapeDtypeStruct((1, 3 * D), F32),
                   jax.ShapeDtypeStruct((3, t, 128), F32),
                   jax.ShapeDtypeStruct((N_DEV,) + w_shard.shape, w_shard.dtype),
                   jax.ShapeDtypeStruct((N_DEV,) + pack.shape, pack.dtype),
                   jax.ShapeDtypeStruct((N_DEV,) + c_rows.shape, F32),
                   jax.ShapeDtypeStruct((N_DEV,) + conv_rows.shape, F32)],
        compiler_params=_params(2),
    )(where, x, w_ada, b_cols, norm_g, *tables, w_shard, pack, c_rows, conv_rows)


def _rope_tables(positions):
    inv_freq = ROPE_THETA ** (-jnp.arange(0, ROT_DIM, 2, dtype=F32) / ROT_DIM)
    ang_t = inv_freq[:, None] * positions.astype(F32)[None, :]
    return jnp.cos(ang_t), jnp.sin(ang_t)


def _rope(tv, cos, s1, s2):
    cols = []
    for g in range(tv.shape[1] // 128):
        blk = tv[:, 128 * g : 128 * (g + 1)]
        cols.append(blk * cos + pltpu.roll(blk, 128 - 8, 1) * s1 + pltpu.roll(blk, 8, 1) * s2)
    return cols[0] if len(cols) == 1 else jnp.concatenate(cols, axis=1)


def _rope_transposed(gv, cos, s1, s2):
    cols = []
    for g in range(gv.shape[1] // 128):
        blk = gv[:, 128 * g : 128 * (g + 1)]
        cols.append(blk * cos + pltpu.roll(blk * s1, 8, 1) + pltpu.roll(blk * s2, 128 - 8, 1))
    return cols[0] if len(cols) == 1 else jnp.concatenate(cols, axis=1)


def _lane_lo():
    return lax.broadcasted_iota(jnp.int32, (1, 128), 1) < HEAD_DIM


def _row_lo():
    return lax.broadcasted_iota(jnp.int32, (128, 1), 0) < HEAD_DIM


def _split_heads(tile, j):
    lo = _lane_lo()
    parts = []
    for g in (2 * j, 2 * j + 1):
        blk = tile[:, 128 * g : 128 * (g + 1)]
        parts += [jnp.where(lo, blk, 0.0), jnp.where(lo, 0.0, blk)]
    return jnp.concatenate(parts, axis=0)


def _dup_lanes(kv, j):
    rolled = pltpu.roll(kv, HEAD_DIM, 1)
    lo = _lane_lo()
    return jnp.where(lo, kv, rolled) if j == 0 else jnp.where(lo, rolled, kv)


def _fold_lanes(acc):
    return acc + pltpu.roll(acc, HEAD_DIM, 1)


def _fill_bias_t(bias_ref):
    kj = lax.broadcasted_iota(jnp.int32, (256, 128), 0)
    qi = lax.broadcasted_iota(jnp.int32, (256, 128), 1)
    diff = qi + WINDOW - kj
    band = (diff >= 0) & (diff < WINDOW)
    bias_ref[0] = jnp.where(band, 0.0, -1e30)
    bias_ref[1] = jnp.where(band & (kj >= 128), 0.0, -1e30)


def _sink_row(sink_ref, pair, j):
    return jnp.concatenate([jnp.full((1, 128), sink_ref[0, 8 * pair + 4 * j + h], F32) for h in range(4)], axis=1)


def _softmax_t(q4, kw, bias4, sink):
    s = _dot_nt(kw, q4) + bias4
    m = jnp.maximum(jnp.max(s, axis=0, keepdims=True), sink)
    p = jnp.exp(s - m)
    e_sink = jnp.exp(sink - m)
    denom = jnp.sum(p, axis=0, keepdims=True) + e_sink
    return p, e_sink, denom


def _attn_specs(t, tq):
    return dict(
        q=pl.BlockSpec((tq, 512), lambda p, i: (i, p)),
        k=pl.BlockSpec((t, 128), lambda p, i: (0, SEG_K // 128 + p)),
        v=pl.BlockSpec((t, 128), lambda p, i: (0, SEG_V // 128 + p)),
        sink=pl.BlockSpec(memory_space=pltpu.SMEM),
        act=pl.BlockSpec((tq, 512), lambda p, i: (i, p)),
    )


def _attn_fwd(proj, sinks):
    t = proj.shape[0]
    tq = min(512, t)
    nb = tq // 128
    n_tiles = t // tq
    sp = _attn_specs(t, tq)

    def body(q_ref, k_ref, v_ref, sink_ref, o_ref, kd, vdt, bias_ref):
        pair = pl.program_id(0)
        i = pl.program_id(1)

        @pl.when(i == 0)
        def _():
            kv, vv = k_ref[...], v_ref[...]
            for j in range(2):
                kd[j, 0:128, :] = jnp.zeros((128, 128), BF16)
                kd[j, 128:, :] = _dup_lanes(kv, j).astype(BF16)
                vdt[j, :, 0:128] = jnp.zeros((128, 128), BF16)
                vdt[j, :, 128:] = _dup_lanes(vv, j).T.astype(BF16)
            _fill_bias_t(bias_ref)

        row_lo = _row_lo()

        def block(b, carry):
            r0 = pl.multiple_of(b * 128, 128)
            gb = i * nb + b
            keys = pl.ds(pl.multiple_of(gb * 128, 128), 256)
            rows = pl.ds(r0, 128)
            qt = q_ref[rows, :] * (1.0 / math.sqrt(HEAD_DIM))
            bias = bias_ref[jnp.where(gb == 0, 1, 0)]
            bias4 = jnp.concatenate([bias] * 4, axis=1)
            cols = []
            for j in range(2):
                q4 = _split_heads(qt, j).astype(BF16)
                p, _, denom = _softmax_t(q4, kd[j, keys, :], bias4, _sink_row(sink_ref, pair, j))
                ot = _dot(vdt[j, :, keys], p.astype(BF16)) * (1.0 / denom)
                for gi in range(2):
                    z = jnp.where(row_lo, ot[:, 256 * gi : 256 * gi + 128], ot[:, 256 * gi + 128 : 256 * gi + 256])
                    cols.append(z.T)
            o_ref[rows, :] = jnp.concatenate(cols, axis=1).astype(BF16)
            return carry

        lax.fori_loop(0, nb, block, 0, unroll=True)

    return pl.pallas_call(
        body,
        name="attn_fwd",
        grid=(2, n_tiles),
        in_specs=[sp["q"], sp["k"], sp["v"], sp["sink"]],
        out_specs=sp["act"],
        out_shape=jax.ShapeDtypeStruct((t, D), BF16),
        scratch_shapes=[pltpu.VMEM((2, t + 128, 128), BF16), pltpu.VMEM((2, 128, t + 128), BF16), pltpu.VMEM((2, 256, 128), F32)],
        compiler_params=_params(2),
    )(proj, proj, proj, sinks)


def _attn_bwd(proj, o, d_o, tables, sinks, part_3):
    t = proj.shape[0]
    tq = min(512, t)
    nb = tq // 128
    n_tiles = t // tq
    lane_tables, (cos_t, sin_t) = tables
    sp = _attn_specs(t, tq)
    scale = 1.0 / math.sqrt(HEAD_DIM)
    half = ROT_DIM // 2

    def unrope_rows(z, c, s):
        out = []
        for base in (0, HEAD_DIM):
            g0, g1 = z[base : base + half], z[base + half : base + ROT_DIM]
            out += [g0 * c + g1 * s, g1 * c - g0 * s, z[base + ROT_DIM : base + HEAD_DIM]]
        return jnp.concatenate(out, axis=0)

    def swap(part_hbm, land, send_sem, recv_sem):
        sibling = _flip(_position(), 1)
        return pltpu.make_async_remote_copy(
            src_ref=part_hbm.at[:, :, sibling[2]], dst_ref=land, send_sem=send_sem, recv_sem=recv_sem,
            device_id=sibling, device_id_type=MESH,
        )

    def body(q_ref, k_ref, v_ref, o_ref, do_ref, ct_ref, st_ref, ck, s1k, s2k, sink_ref, part_hbm,
             dq_ref, dk_ref, dv_ref, dsink_ref, sum_hbm, mine_hbm, kd, kdt, vd, bias_ref, dkacc, dvacc, own, land, stage,
             send_sem, recv_sem, local_sems):
        pair = pl.program_id(0)
        i = pl.program_id(1)
        me = _position()
        load_own = pltpu.make_async_copy(part_hbm.at[:, :, me[2]], own, local_sems.at[0])

        @pl.when((pair == 0) & (i == 0))
        def _():
            swap(part_hbm, land, send_sem, recv_sem).start()
            load_own.start()

        @pl.when(i == 0)
        def _():
            kv, vv = k_ref[...], v_ref[...]
            for j in range(2):
                kdup = _dup_lanes(kv, j)
                kd[j, 0:128, :] = jnp.zeros((128, 128), BF16)
                kd[j, 128:, :] = kdup.astype(BF16)
                kdt[j, :, 0:128] = jnp.zeros((128, 128), BF16)
                kdt[j, :, 128:] = kdup.T.astype(BF16)
                vd[j, 0:128, :] = jnp.zeros((128, 128), BF16)
                vd[j, 128:, :] = _dup_lanes(vv, j).astype(BF16)
            _fill_bias_t(bias_ref)
            dkacc[...] = jnp.zeros_like(dkacc)
            dvacc[...] = jnp.zeros_like(dvacc)
            dsink_ref[...] = jnp.zeros_like(dsink_ref)

        row_lo = _row_lo()
        ones = jnp.ones((8, 128), BF16)

        def block(b, carry):
            r0 = pl.multiple_of(b * 128, 128)
            gb = i * nb + b
            keys = pl.ds(pl.multiple_of(gb * 128, 128), 256)
            rows = pl.ds(r0, 128)
            qt = q_ref[rows, :] * scale
            dob = do_ref[rows, :].astype(F32)
            prod = dob * o_ref[rows, :].astype(F32)
            ct, st = ct_ref[:, rows], st_ref[:, rows]
            bias = bias_ref[jnp.where(gb == 0, 1, 0)]
            bias4 = jnp.concatenate([bias] * 4, axis=1)
            cols = []
            for j in range(2):
                q4 = _split_heads(qt, j).astype(BF16)
                kw = kd[j, keys, :]
                p, e_sink, denom = _softmax_t(q4, kw, bias4, _sink_row(sink_ref, pair, j))
                inv = 1.0 / denom
                prob = p * inv
                do4 = _split_heads(dob, j).astype(BF16)
                delta = _dot_nt(ones, _split_heads(prod, j).astype(BF16))[0:1, :]
                dp = _dot_nt(vd[j, keys, :], do4)
                ds = (prob * (dp - delta)).astype(BF16)
                dkacc[j, keys, :] = dkacc[j, keys, :] + _dot(ds, q4)
                dvacc[j, keys, :] = dvacc[j, keys, :] + _dot(prob.astype(BF16), do4)
                dqt = _dot(kdt[j, :, keys], ds) * scale
                dsink = -(e_sink * inv) * delta
                for h in range(4):
                    row = 4 * j + h
                    total = jnp.sum(dsink[:, 128 * h : 128 * (h + 1)], axis=1, keepdims=True)
                    dsink_ref[0, row : row + 1, :] = dsink_ref[0, row : row + 1, :] + jnp.broadcast_to(total, (1, 128))
                for gi in range(2):
                    z = jnp.where(row_lo, dqt[:, 256 * gi : 256 * gi + 128], dqt[:, 256 * gi + 128 : 256 * gi + 256])
                    cols.append(unrope_rows(z, ct, st).T)
            dq_ref[rows, :] = jnp.concatenate(cols, axis=1).astype(BF16)
            return carry

        lax.fori_loop(0, nb, block, 0, unroll=True)

        @pl.when(i == n_tiles - 1)
        def _():
            lo = _lane_lo()
            dk = jnp.where(lo, _fold_lanes(dkacc[0, 128:, :]), _fold_lanes(dkacc[1, 128:, :]))
            dk_ref[...] = _rope_transposed(dk, ck[...], s1k[...], s2k[...]).astype(BF16)
            dv_ref[...] = jnp.where(lo, _fold_lanes(dvacc[0, 128:, :]), _fold_lanes(dvacc[1, 128:, :])).astype(BF16)

        @pl.when((pair == 1) & (i == n_tiles - 1))
        def _():
            swap(part_hbm, land, send_sem, recv_sem).wait()
            load_own.wait()
            own[...] = own[...] + land[...]
            stage[...] = own[...].astype(BF16)
            results = [pltpu.make_async_copy(stage, sum_hbm, local_sems.at[1]),
                       pltpu.make_async_copy(own.at[:, _chip_of(me)], mine_hbm, local_sems.at[2])]
            for cp in results:
                cp.start()
            for cp in results:
                cp.wait()

    kv_out = pl.BlockSpec((t, 128), lambda p, i: (0, p))
    tab_t = pl.BlockSpec((half, tq), lambda p, i: (0, i))
    tab_k = [pl.BlockSpec((None, t, 128), lambda p, i, a=a: (a, 0, 0)) for a in range(3)]
    return pl.pallas_call(
        body,
        name="attn_bwd",
        grid=(2, n_tiles),
        in_specs=[sp["q"], sp["k"], sp["v"], sp["act"], sp["act"], tab_t, tab_t, *tab_k, sp["sink"],
                  pl.BlockSpec(memory_space=pl.ANY)],
        out_specs=[sp["act"], kv_out, kv_out, pl.BlockSpec((1, 8, 128), lambda p, i: (p, 0, 0)),
                   pl.BlockSpec(memory_space=pl.ANY), pl.BlockSpec(memory_space=pl.ANY)],
        out_shape=[
            jax.ShapeDtypeStruct((t, D), BF16),
            jax.ShapeDtypeStruct((t, 256), BF16),
            jax.ShapeDtypeStruct((t, 256), BF16),
            jax.ShapeDtypeStruct((2, 8, 128), F32),
            jax.ShapeDtypeStruct((3, 4, ROW_SHARD, D), BF16),
            jax.ShapeDtypeStruct((3, ROW_SHARD, D), F32),
        ],
        scratch_shapes=[pltpu.VMEM((2, t + 128, 128), BF16), pltpu.VMEM((2, 128, t + 128), BF16), pltpu.VMEM((2, t + 128, 128), BF16),
                        pltpu.VMEM((2, 256, 128), F32), pltpu.VMEM((2, t + 128, 128), F32), pltpu.VMEM((2, t + 128, 128), F32),
                        pltpu.VMEM((3, 4, ROW_SHARD, D), F32), pltpu.VMEM((3, 4, ROW_SHARD, D), F32), pltpu.VMEM((3, 4, ROW_SHARD, D), BF16),
                        pltpu.SemaphoreType.DMA, pltpu.SemaphoreType.DMA, pltpu.SemaphoreType.DMA((3,))],
        compiler_params=_params(2),
    )(proj, proj, proj, o, d_o, cos_t, sin_t, lane_tables, lane_tables, lane_tables, sinks, part_3)


HALF = 512


def _softplus(y):
    u = jnp.exp(-jnp.abs(y))
    w = 1.0 + u
    log1p = jnp.where(w == 1.0, u, jnp.log(w) * (u / jnp.where(w == 1.0, 1.0, w - 1.0)))
    return jnp.maximum(y, 0.0) + log1p


def _gate_block(w_ref, blk):
    return w_ref[:, :, 256 * blk : 256 * (blk + 1)].reshape(256, 256)


def _gate_matmul(act_bf, w_ref):
    return jnp.concatenate([_dot(act_bf[:, 256 * blk : 256 * (blk + 1)], _gate_block(w_ref, blk)) for blk in range(2)], axis=1)


def _gate_matmul_t(grad_bf, w_ref):
    return jnp.concatenate([_dot_nt(grad_bf[:, 256 * blk : 256 * (blk + 1)], _gate_block(w_ref, blk)) for blk in range(2)], axis=1)


def _conv_and_gates(xe_ref, tm, cw_ref, cb_ref, wa_ref, wx_ref, ba_ref, bx_ref, lam_ref, keep):
    xc = cb_ref[...] + sum(cw_ref[k : k + 1, :] * xe_ref[pl.ds(5 + k, tm), :] for k in range(CONV_WIDTH))
    xc_bf = xc.astype(BF16)
    r = _sigmoid(_gate_matmul(xc_bf, wa_ref) + ba_ref[...])
    ig = _sigmoid(_gate_matmul(xc_bf, wx_ref) + bx_ref[...])
    sp = _softplus(-lam_ref[...])
    log_a = -LRU_C * r * sp
    tanh = jnp.tanh(-log_a)
    mult = jnp.sqrt(2.0 * tanh / (1.0 + tanh))
    a = jnp.where(keep, jnp.exp(log_a), 0.0)
    mult = jnp.where(keep, mult, 1.0)
    return xc, xc_bf, r, ig, sp, a, mult


def _scan_rows(a_ref, b_ref, h_ref, carry_ref, tm, reverse):
    width = a_ref.shape[1]
    row = lax.broadcasted_iota(jnp.int32, (8, width), 0)

    def group(g, carry):
        gg = tm // 8 - 1 - g if reverse else g
        r0 = pl.multiple_of(gg * 8, 8)
        a = a_ref[pl.ds(r0, 8), :]
        b = b_ref[pl.ds(r0, 8), :]
        for s in (1, 2, 4):
            shift = 8 - s if reverse else s
            a_sh = pltpu.roll(a, shift, 0)
            b_sh = pltpu.roll(b, shift, 0)
            ok = (row < 8 - s) if reverse else (row >= s)
            b = jnp.where(ok, a * b_sh + b, b)
            a = jnp.where(ok, a * a_sh, a)
        h = a * carry + b
        h_ref[pl.ds(r0, 8), :] = h
        last = h[0:1, :] if reverse else h[7:8, :]
        return jnp.broadcast_to(last, (8, width))

    carry_ref[...] = lax.fori_loop(0, tm // 8, group, carry_ref[...])


def _rnn_specs(t, tm, order):
    n = t // tm
    return dict(
        tile=pl.BlockSpec((tm, HALF), lambda ch, i: (order(i, n), ch)),
        before=lambda col0: pl.BlockSpec((8, HALF), lambda ch, i: (jnp.maximum(order(i, n) * (tm // 8) - 1, 0), col0 + ch)),
        proj_tile=lambda col0: pl.BlockSpec((tm, HALF), lambda ch, i: (order(i, n), col0 + ch)),
        keep=pl.BlockSpec((tm, 1), lambda ch, i: (order(i, n), 0)),
        gates=lambda which: pl.BlockSpec((N_DEV, GATE_SHARD, HALF), lambda ch, i: (0, PACK_GATE // GATE_SHARD + which, ch)),
        gate_grads=pl.BlockSpec((2, 2, 256, 256), lambda ch, i: (0, ch, 0, 0)),
        vec=pl.BlockSpec((1, HALF), lambda ch, i: (0, ch)),
        conv=pl.BlockSpec((CONV_WIDTH, HALF), lambda ch, i: (0, ch)),
    )


def _rnn_fwd(proj, keep, w_all, rg_ba, rg_bx, rg_lambda, conv_w, conv_b):
    t = proj.shape[0]
    tm = min(512, t)
    sp = _rnn_specs(t, tm, lambda i, n: i)

    def body(xr_ref, xb_ref, keep_ref, wa_ref, wx_ref, ba_ref, bx_ref, lam_ref, cw_ref, cb_ref, hh_ref, xe, a_buf, b_buf, carry):
        i = pl.program_id(1)

        @pl.when(i == 0)
        def _():
            carry[...] = jnp.zeros_like(carry)

        xe[0:8, :] = jnp.where(i > 0, xb_ref[...], 0.0)
        xe[8:, :] = xr_ref[...]
        keep_rows = keep_ref[...] != 0.0
        xc, _, _, ig, _, a, mult = _conv_and_gates(xe, tm, cw_ref, cb_ref, wa_ref, wx_ref, ba_ref, bx_ref, lam_ref, keep_rows)
        a_buf[...] = a
        b_buf[...] = mult * (ig * xc)
        _scan_rows(a_buf, b_buf, hh_ref, carry, tm, reverse=False)

    return pl.pallas_call(
        body,
        name="rnn_fwd",
        grid=(2, t // tm),
        in_specs=[sp["proj_tile"](SEG_XR // HALF), sp["before"](SEG_XR // HALF), sp["keep"], sp["gates"](0), sp["gates"](1),
                  sp["vec"], sp["vec"], sp["vec"], sp["conv"], sp["vec"]],
        out_specs=sp["tile"],
        out_shape=jax.ShapeDtypeStruct((t, D), F32),
        scratch_shapes=[pltpu.VMEM((tm + 8, HALF), F32), pltpu.VMEM((tm, HALF), F32), pltpu.VMEM((tm, HALF), F32), pltpu.VMEM((8, HALF), F32)],
        compiler_params=_params(2),
    )(proj, proj, keep, w_all, w_all, rg_ba, rg_bx, rg_lambda, conv_w, conv_b)


def _rnn_bwd(proj, hh, d_hh, keep, w_all, rg_ba, rg_bx, rg_lambda, conv_w, conv_b, sum_3_bf):
    t = proj.shape[0]
    tm = min(512, t)
    n_tiles = t // tm
    sp = _rnn_specs(t, tm, lambda i, n: n - 1 - i)

    def body(xr_ref, xb_ref, hh_ref, hb_ref, dhh_ref, keep_ref, wa_ref, wx_ref, ba_ref, bx_ref, lam_ref, cw_ref, cb_ref,
             s3_hbm, dxr_ref, gw_ref, gvec_ref, l3_hbm, xe, he, ae, c_buf, g_buf, dxe, carry, a_next, send_sems, recv_sems):
        ch = pl.program_id(0)
        i = pl.program_id(1)
        first_tile = i == n_tiles - 1
        views = [lambda ref, peer: ref.at[:, _chip_of(peer)]]

        @pl.when((ch == 0) & (i == 0))
        def _():
            for cp in _chip_copies([s3_hbm], [l3_hbm], views, send_sems, recv_sems):
                cp.start()

        @pl.when((ch == 1) & (i == n_tiles - 1))
        def _():
            for cp in _chip_copies([s3_hbm], [l3_hbm], views, send_sems, recv_sems):
                cp.wait_recv()
                cp.wait_send()

        @pl.when(i == 0)
        def _():
            carry[...] = jnp.zeros_like(carry)
            a_next[...] = jnp.zeros_like(a_next)
            dxe[tm:, :] = jnp.zeros((8, HALF), F32)
            gw_ref[...] = jnp.zeros_like(gw_ref)
            gvec_ref[...] = jnp.zeros_like(gvec_ref)

        xe[0:8, :] = jnp.where(first_tile, 0.0, xb_ref[...])
        xe[8:, :] = xr_ref[...]
        he[0:8, :] = jnp.where(first_tile, 0.0, hb_ref[...])
        he[8:, :] = hh_ref[...]
        keep_rows = keep_ref[...] != 0.0
        xc, xc_bf, r, ig, sp_, a, mult = _conv_and_gates(xe, tm, cw_ref, cb_ref, wa_ref, wx_ref, ba_ref, bx_ref, lam_ref, keep_rows)

        ae[0:tm, :] = a
        ae[tm:, :] = a_next[...]
        a_next[...] = jnp.broadcast_to(a[0:1, :], (8, HALF))
        c_buf[...] = ae[pl.ds(1, tm), :]
        g_buf[...] = dhh_ref[...]
        _scan_rows(c_buf, g_buf, g_buf, carry, tm, reverse=True)
        g = g_buf[...]

        h_prev = he[pl.ds(7, tm), :]
        bb = ig * xc
        d_ig = g * mult * xc
        d_xc = g * mult * ig
        d_log_a = jnp.where(keep_rows, g * h_prev * a - (g * bb) * (a * a) / mult, 0.0)
        d_r = d_log_a * (-LRU_C * sp_)
        d_sp = jnp.sum(d_log_a * (-LRU_C * r), axis=0, keepdims=True)
        d_zr = d_r * r * (1.0 - r)
        d_zi = d_ig * ig * (1.0 - ig)
        d_zr_bf = d_zr.astype(BF16)
        d_zi_bf = d_zi.astype(BF16)
        d_xc = d_xc + _gate_matmul_t(d_zr_bf, wa_ref) + _gate_matmul_t(d_zi_bf, wx_ref)
        for blk in range(2):
            cols = slice(256 * blk, 256 * (blk + 1))
            gw_ref[0, blk] = gw_ref[0, blk] + _dot_tn(xc_bf[:, cols], d_zr_bf[:, cols])
            gw_ref[1, blk] = gw_ref[1, blk] + _dot_tn(xc_bf[:, cols], d_zi_bf[:, cols])

        dxe[0:tm, :] = d_xc
        shifted = [dxe[pl.ds(3 - k, tm), :] for k in range(CONV_WIDTH)]
        dxr = sum(cw_ref[k : k + 1, :] * shifted[k] for k in range(CONV_WIDTH))
        dxr_ref[...] = dxr.astype(BF16)
        dxe[tm:, :] = d_xc[0:8, :]

        lam = lam_ref[...]
        xr_tile = xr_ref[...]
        sums = [
            jnp.sum(d_zr, axis=0, keepdims=True),
            jnp.sum(d_zi, axis=0, keepdims=True),
            d_sp * (-_sigmoid(-lam)),
            jnp.sum(d_xc, axis=0, keepdims=True),
        ] + [jnp.sum(xr_tile * shifted[k], axis=0, keepdims=True) for k in range(CONV_WIDTH)]
        gvec_ref[...] = gvec_ref[...] + jnp.concatenate(sums, axis=0)

    return pl.pallas_call(
        body,
        name="rnn_bwd",
        grid=(2, n_tiles),
        in_specs=[
            sp["proj_tile"](SEG_XR // HALF), sp["before"](SEG_XR // HALF), sp["tile"], sp["before"](0), sp["tile"], sp["keep"],
            sp["gates"](0), sp["gates"](1), sp["vec"], sp["vec"], sp["vec"], sp["conv"], sp["vec"], pl.BlockSpec(memory_space=pl.ANY),
        ],
        out_specs=[sp["tile"], sp["gate_grads"], pl.BlockSpec((8, HALF), lambda ch, i: (0, ch)), pl.BlockSpec(memory_space=pl.ANY)],
        out_shape=[
            jax.ShapeDtypeStruct((t, D), BF16),
            jax.ShapeDtypeStruct((2, 4, 256, 256), F32),
            jax.ShapeDtypeStruct((8, D), F32),
            jax.ShapeDtypeStruct((len(CHIP_FLIPS), 3, ROW_SHARD, D), BF16),
        ],
        scratch_shapes=[
            pltpu.VMEM((tm + 8, HALF), F32),
            pltpu.VMEM((tm + 8, HALF), F32),
            pltpu.VMEM((tm + 8, HALF), F32),
            pltpu.VMEM((tm, HALF), F32),
            pltpu.VMEM((tm, HALF), F32),
            pltpu.VMEM((tm + 8, HALF), F32),
            pltpu.VMEM((8, HALF), F32),
            pltpu.VMEM((8, HALF), F32),
            pltpu.SemaphoreType.DMA((len(CHIP_FLIPS),)),
            pltpu.SemaphoreType.DMA((len(CHIP_FLIPS),)),
        ],
        compiler_params=_params(2),
    )(proj, proj, hh, hh, d_hh, keep, w_all, w_all, rg_ba, rg_bx, rg_lambda, conv_w, conv_b, sum_3_bf)


def _mid(x, target, o, hh, proj, w_all, gate, final_g):
    t = x.shape[0]
    tm = min(256, t)
    n_tiles = t // tm
    full = pl.BlockSpec((tm, D), lambda i: (i, 0))
    half = lambda col: pl.BlockSpec((tm, 512), lambda i: (i, col))
    vec = pl.BlockSpec((1, D), lambda i: (0, 0))
    any_spec = pl.BlockSpec(memory_space=pl.ANY)

    def body(x_ref, tg_ref, o_ref, hh_ref, ga0, ga1, gr0, gr1, ma0, ma1, mr0, mr1, w_hbm, gate_ref, fg_ref,
             dres_ref, do_ref, dhh_ref, dga_ref, dgr_ref, dma_ref, dmr_ref, gw_hbm, vec_ref, w, gw, sem, w_sems):
        i = pl.program_id(0)

        @pl.when(i == 0)
        def _():
            loads = [
                pltpu.make_async_copy(
                    w_hbm.at[s, pl.ds(ROW_SHARD * k, ROW_SHARD), :], w.at[k, pl.ds(ROW_SHARD * s, ROW_SHARD), :], w_sems.at[k * N_DEV + s]
                )
                for k in range(3)
                for s in range(N_DEV)
            ]
            for cp in loads:
                cp.start()
            for cp in loads:
                cp.wait()
            gw[...] = jnp.zeros_like(gw)
            vec_ref[...] = jnp.zeros_like(vec_ref)

        cat = lambda a, b: jnp.concatenate([a[...], b[...]], axis=1)
        ga, gr, ma, mr = cat(ga0, ga1), cat(gr0, gr1), cat(ma0, ma1), cat(mr0, mr1)
        ov, hv, gate_v, fg = o_ref[...].astype(F32), hh_ref[...], gate_ref[...], fg_ref[...]
        sg_a = _sigmoid(ga)
        sg_r = _sigmoid(gr)
        silu_a = ga * sg_a
        silu_r = gr * sg_r
        ya = (ov * silu_a).astype(BF16)
        yr = (hv * silu_r).astype(BF16)
        pa = _dot(ya, w[0])
        pr = _dot(yr, w[1])
        sa = _sigmoid(ma)
        sr = _sigmoid(mr)
        merged = (sa * pa + sr * pr).astype(BF16)
        mo = _dot(merged, w[2])
        out = x_ref[...] + gate_v * mo
        r2 = lax.rsqrt(jnp.mean(out * out, axis=-1, keepdims=True) + NORM_EPS)
        outn = out * r2
        err = outn * fg - tg_ref[...]
        loss = 0.5 * jnp.sum(jnp.mean(err * err, axis=-1, keepdims=True), axis=0, keepdims=True)
        dy = err * (1.0 / D)
        g_final = jnp.sum(dy * outn, axis=0, keepdims=True)
        doutn = dy * fg
        dout = r2 * (doutn - outn * jnp.mean(doutn * outn, axis=-1, keepdims=True))
        dres_ref[...] = dout
        d_gate = jnp.sum(dout * mo, axis=0, keepdims=True)
        dmo = (dout * gate_v).astype(BF16)
        dmerged = _dot_nt(dmo, w[2])
        gw[2] = gw[2] + _dot_tn(merged, dmo)
        dpa = (dmerged * sa).astype(BF16)
        dpr = (dmerged * sr).astype(BF16)
        dma_ref[...] = (dmerged * pa * (sa * (1.0 - sa))).astype(BF16)
        dmr_ref[...] = (dmerged * pr * (sr * (1.0 - sr))).astype(BF16)
        dya = _dot_nt(dpa, w[0])
        dyr = _dot_nt(dpr, w[1])
        gw[0] = gw[0] + _dot_tn(ya, dpa)
        gw[1] = gw[1] + _dot_tn(yr, dpr)
        do_ref[...] = (dya * silu_a).astype(BF16)
        dhh_ref[...] = dyr * silu_r
        dga_ref[...] = (dya * ov * (sg_a * (1.0 + ga * (1.0 - sg_a)))).astype(BF16)
        dgr_ref[...] = (dyr * hv * (sg_r * (1.0 + gr * (1.0 - sg_r)))).astype(BF16)
        zeros = jnp.zeros((5, D), F32)
        vec_ref[...] = vec_ref[...] + jnp.concatenate([g_final, d_gate, jnp.broadcast_to(loss, (1, D)), zeros], axis=0)

        @pl.when(i == n_tiles - 1)
        def _():
            cp = pltpu.make_async_copy(gw, gw_hbm, sem)
            cp.start()
            cp.wait()

    bf_tile = jax.ShapeDtypeStruct((t, D), BF16)
    f_tile = jax.ShapeDtypeStruct((t, D), F32)
    return pl.pallas_call(
        body,
        name="mid",
        grid=(n_tiles,),
        in_specs=[full, full, full, full,
                  half(SEG_GA // 512), half(SEG_GA // 512 + 1), half(SEG_GR // 512), half(SEG_GR // 512 + 1),
                  half(SEG_MA // 512), half(SEG_MA // 512 + 1), half(SEG_MR // 512), half(SEG_MR // 512 + 1),
                  any_spec, vec, vec],
        out_specs=[full, full, full, full, full, full, full, any_spec, pl.BlockSpec((8, D), lambda i: (0, 0))],
        out_shape=[f_tile, bf_tile, f_tile, bf_tile, bf_tile, bf_tile, bf_tile,
                   jax.ShapeDtypeStruct((3, D, D), F32), jax.ShapeDtypeStruct((8, D), F32)],
        scratch_shapes=[pltpu.VMEM((3, D, D), BF16), pltpu.VMEM((3, D, D), F32), pltpu.SemaphoreType.DMA, pltpu.SemaphoreType.DMA((3 * N_DEV,))],
        compiler_params=_params(1),
    )(x, target, o, hh, proj, proj, proj, proj, proj, proj, proj, proj, w_all, gate, final_g)


_SEGMENTS = ((SEG_Q, 1024), (SEG_K, 256), (SEG_V, 256), (SEG_GA, 1024), (SEG_XR, 1024), (SEG_GR, 1024), (SEG_MA, 1024), (SEG_MR, 1024))


def _seg_specs(tm):
    return [pl.BlockSpec((tm, width), lambda i: (i, 0)) for _, width in _SEGMENTS]


def _bwd_dh(dsegs, w_all, x, dres, mod, norm_g, token):
    t = x.shape[0]
    tm = min(512, t)
    n_tiles = t // tm
    n_seg = len(_SEGMENTS)
    full = pl.BlockSpec((tm, D), lambda i: (i, 0))

    def body(*refs):
        segs = refs[:n_seg]
        w_hbm, x_ref, dres_ref, mod_ref, ng_ref, token_ref, gx_ref, vec_ref, w, sems = refs[n_seg:]
        i = pl.program_id(0)

        @pl.when(i == 0)
        def _():
            loads = [
                pltpu.make_async_copy(w_hbm.at[s], w.at[pl.ds(W_IN_SHARD * s, W_IN_SHARD), :], sems.at[s])
                for s in range(N_DEV)
            ]
            for cp in loads:
                cp.start()
            for cp in loads:
                cp.wait()
            vec_ref[...] = jnp.zeros_like(vec_ref)

        dh = sum(_dot(seg[...], w[start : start + width, :]) for seg, (start, width) in zip(segs, _SEGMENTS))
        xv = x_ref[...]
        r1 = lax.rsqrt(jnp.mean(xv * xv, axis=-1, keepdims=True) + NORM_EPS)
        xn = xv * r1
        ng = ng_ref[...]
        scale1 = 1.0 + mod_ref[:, D : 2 * D]
        d_shift = jnp.sum(dh, axis=0, keepdims=True)
        d_scale = jnp.sum(dh * (xn * ng), axis=0, keepdims=True)
        g_norm = jnp.sum(dh * xn * scale1, axis=0, keepdims=True)
        dxn = dh * (ng * scale1)
        dx = r1 * (dxn - xn * jnp.mean(dxn * xn, axis=-1, keepdims=True))
        gx_ref[...] = dres_ref[...] + dx
        vec_ref[...] = vec_ref[...] + jnp.concatenate([d_shift, d_scale, g_norm, jnp.zeros((5, D), F32)], axis=0)

    any_spec = pl.BlockSpec(memory_space=pl.ANY)
    return pl.pallas_call(
        body,
        name="bwd_dh",
        grid=(n_tiles,),
        in_specs=_seg_specs(tm) + [any_spec, full, full, pl.BlockSpec((1, 3 * D), lambda i: (0, 0)), pl.BlockSpec((1, D), lambda i: (0, 0)),
                                   pl.BlockSpec(token.shape, lambda i: (0, 0))],
        out_specs=[full, pl.BlockSpec((8, D), lambda i: (0, 0))],
        out_shape=[jax.ShapeDtypeStruct((t, D), F32), jax.ShapeDtypeStruct((8, D), F32)],
        scratch_shapes=[pltpu.VMEM((IN_W, D), BF16), pltpu.SemaphoreType.DMA((N_DEV,))],
        compiler_params=_params(1),
    )(*dsegs, w_all, x, dres, mod, norm_g, token)


_HBM = pl.BlockSpec(memory_space=pltpu.HBM)
_SEM = pl.BlockSpec(memory_space=pltpu.SEMAPHORE)
_EFFECT = pltpu.SideEffectType.DATAFLOW_SIDE_EFFECTING


def _w_exchange_copies(sum_ref, land_ref, sems):
    me = _position()
    n = len(CHIP_FLIPS)
    remote = [
        pltpu.make_async_remote_copy(
            src_ref=sum_ref.at[_chip_of(_flip(me, k))], dst_ref=land_ref.at[j], send_sem=sems[j], recv_sem=sems[n + j],
            device_id=_flip(me, k), device_id_type=MESH,
        )
        for j, k in enumerate(CHIP_FLIPS)
    ]
    return remote, []


def _vec_exchange_copies(vec_ref, land_ref, sems):
    me = _position()
    n = N_DEV - 1
    remote = [
        pltpu.make_async_remote_copy(
            src_ref=vec_ref, dst_ref=land_ref.at[_slot(me)], send_sem=sems[k - 1], recv_sem=sems[n + k - 1],
            device_id=_flip(me, k), device_id_type=MESH,
        )
        for k in range(1, N_DEV)
    ]
    return remote, [pltpu.make_async_copy(vec_ref, land_ref.at[_slot(me)], sems[2 * n])]


def _exchange_start(name, copies, n_sems, src, land_shape):
    land = lax.empty(land_shape, src.dtype)

    def body(src_ref, land_ref, *outs):
        remote, local = copies(src_ref, land_ref, outs[:n_sems])
        for cp in remote + local:
            cp.start()
        token = outs[n_sems + 2]
        token[...] = jnp.zeros_like(token)

    res = pl.pallas_call(
        body,
        name=name,
        out_shape=(*[pltpu.SemaphoreType.DMA(())] * n_sems, pltpu.HBM(src.shape, src.dtype), pltpu.HBM(land.shape, src.dtype),
                   jax.ShapeDtypeStruct((8, 128), F32)),
        in_specs=(_HBM, _HBM),
        out_specs=(*[_SEM] * n_sems, _HBM, _HBM, pl.BlockSpec(memory_space=pltpu.VMEM)),
        input_output_aliases={0: n_sems, 1: n_sems + 1},
        compiler_params=pltpu.CompilerParams(has_side_effects=_EFFECT),
    )(pltpu.with_memory_space_constraint(src, pltpu.HBM), pltpu.with_memory_space_constraint(land, pltpu.HBM))
    return res[:n_sems], res[n_sems], res[n_sems + 1], res[n_sems + 2]


def _exchange_wait(name, copies, sems, src_thru, land_thru, after):
    n_sems = len(sems)

    def body(src_ref, land_ref, *rest):
        remote, local = copies(src_ref, land_ref, rest[:n_sems])
        for cp in remote:
            cp.wait_send()
            cp.wait_recv()
        for cp in local:
            cp.wait()

    return pl.pallas_call(
        body,
        name=name,
        out_shape=(pltpu.HBM(src_thru.shape, src_thru.dtype), pltpu.HBM(land_thru.shape, land_thru.dtype)),
        in_specs=(_HBM, _HBM, *[_SEM] * n_sems, *[pl.BlockSpec(memory_space=pl.ANY)] * len(after)),
        out_specs=(_HBM, _HBM),
        input_output_aliases={0: 0, 1: 1},
        compiler_params=pltpu.CompilerParams(has_side_effects=_EFFECT),
    )(src_thru, land_thru, *sems, *after)[1]


def _bwd_gw(dsegs, h, g_gates):
    t = h.shape[0]
    tm = min(512, t)
    n_tiles = t // tm
    n_seg = len(_SEGMENTS)
    n_chips = N_DEV // 2

    def gate_copies(g_hbm, lg_hbm, send_sems, recv_sems, arriving):
        me = _position()
        return [
            pltpu.make_async_remote_copy(
                src_ref=g_hbm.at[:, :, _slot(_flip(me, k))],
                dst_ref=lg_hbm.at[_slot(_flip(me, k) if arriving else me)],
                send_sem=send_sems.at[k - 1],
                recv_sem=recv_sems.at[k - 1],
                device_id=_flip(me, k),
                device_id_type=MESH,
            )
            for k in range(1, N_DEV)
        ]

    def body(*refs):
        segs = refs[:n_seg]
        (h_ref, g_hbm, own_hbm, sum_hbm, lg_hbm, acc, stage, land, own_sems, pair_send, pair_recv, send_sems, recv_sems, g_local) = refs[n_seg:]
        i = pl.program_id(0)
        slot = _slot(_position())
        own_gates = pltpu.make_async_copy(g_hbm.at[:, :, slot], lg_hbm.at[slot], g_local)

        @pl.when(i == 0)
        def _():
            own_gates.start()
            for cp in gate_copies(g_hbm, lg_hbm, send_sems, recv_sems, arriving=False):
                cp.start()
            acc[...] = jnp.zeros_like(acc)

        hv = h_ref[...]
        for seg, (start, width) in zip(segs, _SEGMENTS):
            acc[start : start + width, :] = acc[start : start + width, :] + _dot_tn(seg[...], hv)

        @pl.when(i == n_tiles - 1)
        def _():
            me = _position()
            sibling = _flip(me, 1)

            def rows(q, c):
                return pl.ds(pl.multiple_of(W_IN_SHARD * (2 * q + c), 8), W_IN_SHARD)

            def to_sibling(q):
                return pltpu.make_async_remote_copy(
                    src_ref=stage.at[q % 2], dst_ref=land.at[q], send_sem=pair_send.at[q], recv_sem=pair_recv.at[q],
                    device_id=sibling, device_id_type=MESH,
                )

            for q in range(n_chips):
                if q >= 2:
                    to_sibling(q - 2).wait_send()
                stage[q % 2] = acc[rows(q, sibling[2]), :].astype(BF16)
                to_sibling(q).start()
            for q in range(n_chips - 2, n_chips):
                to_sibling(q).wait_send()
            sums = [pltpu.make_async_copy(stage.at[q % 2], sum_hbm.at[q], own_sems.at[q]) for q in range(n_chips)]
            for q in range(n_chips):
                to_sibling(q).wait_recv()
                if q >= 2:
                    sums[q - 2].wait()
                total = acc[rows(q, me[2]), :] + land[q].astype(F32)
                acc[rows(q, me[2]), :] = total
                stage[q % 2] = total.astype(BF16)
                sums[q].start()
            mine = pltpu.make_async_copy(acc.at[rows(_chip_of(me), me[2]), :], own_hbm, own_sems.at[n_chips])
            mine.start()
            for q in range(n_chips - 2, n_chips):
                sums[q].wait()
            mine.wait()
            for cp in gate_copies(g_hbm, lg_hbm, send_sems, recv_sems, arriving=True):
                cp.wait_recv()
            for cp in gate_copies(g_hbm, lg_hbm, send_sems, recv_sems, arriving=False):
                cp.wait_send()
            own_gates.wait()

    any_spec = pl.BlockSpec(memory_space=pl.ANY)
    n_copies = N_DEV - 1
    return pl.pallas_call(
        body,
        name="bwd_gw",
        grid=(n_tiles,),
        in_specs=_seg_specs(tm) + [pl.BlockSpec((tm, D), lambda i: (i, 0)), any_spec],
        out_specs=[any_spec, any_spec, any_spec],
        out_shape=[jax.ShapeDtypeStruct((W_IN_SHARD, D), F32),
                   jax.ShapeDtypeStruct((n_chips, W_IN_SHARD, D), BF16),
                   jax.ShapeDtypeStruct((N_DEV, 2, 4, GATE_SHARD, 256), F32)],
        scratch_shapes=[pltpu.VMEM((IN_W, D), F32), pltpu.VMEM((2, W_IN_SHARD, D), BF16), pltpu.VMEM((n_chips, W_IN_SHARD, D), BF16),
                        pltpu.SemaphoreType.DMA((n_chips + 1,)), pltpu.SemaphoreType.DMA((n_chips,)), pltpu.SemaphoreType.DMA((n_chips,)),
                        pltpu.SemaphoreType.DMA((n_copies,)), pltpu.SemaphoreType.DMA((n_copies,)), pltpu.SemaphoreType.DMA],
        compiler_params=_params(1),
    )(*dsegs, h, g_gates)


def _adamw(w, g, m, v):
    m = ADAM_B1 * m + (1.0 - ADAM_B1) * g
    v = ADAM_B2 * v + (1.0 - ADAM_B2) * (g * g)
    m_hat = m / (1.0 - ADAM_B1**ADAM_STEP)
    v_hat = v / (1.0 - ADAM_B2**ADAM_STEP)
    delta = -ADAM_LR * (m_hat / (jnp.sqrt(v_hat) + ADAM_EPS) + ADAM_WD * w)
    return delta, m, v


def _w_in_update(own, land, w_t, m_t, v_t, token):
    n_slots, rows, cols = land.shape
    rb = 208
    spec = pl.BlockSpec((rb, cols), lambda i: (i, 0))

    def body(o_ref, l_ref, w_ref, m_ref, v_ref, token_ref, g_ref, d_ref, m2_ref, v2_ref):
        total = o_ref[...]
        for s in range(n_slots):
            total = total + l_ref[s].astype(F32)
        g_ref[...] = total
        d_ref[...], m2_ref[...], v2_ref[...] = _adamw(w_ref[...], total, m_ref[...], v_ref[...])

    shape = jax.ShapeDtypeStruct((rows, cols), F32)
    return pl.pallas_call(
        body,
        name="w_in_update",
        grid=(rows // rb,),
        in_specs=[spec, pl.BlockSpec((n_slots, rb, cols), lambda i: (0, i, 0)), spec, spec, spec, pl.BlockSpec(token.shape, lambda i: (0, 0))],
        out_specs=[spec] * 4,
        out_shape=[shape] * 4,
        compiler_params=_params(1),
    )(own, land, w_t, m_t, v_t, token)


def _shard_update(own_3, land_3, land_g, weights, m_in, v_in, token):
    n = len(weights)

    def body(*refs):
        o3_ref, l3_ref, lg_ref = refs[:3]
        w_refs, m_refs, v_refs = (refs[3 + n * k : 3 + n * (k + 1)] for k in range(3))
        outs = refs[4 + 3 * n :]
        for k in range(n):
            if k < 3:
                g = o3_ref[k]
                for slot in range(3):
                    g = g + l3_ref[slot, k].astype(F32)
            else:
                g = lg_ref[0, k - 3]
                for slot in range(1, N_DEV):
                    g = g + lg_ref[slot, k - 3]
            d, m2, v2 = _adamw(w_refs[k][0], g, m_refs[k][0], v_refs[k][0])
            for which, val in enumerate((g, d, m2, v2)):
                outs[which * n + k][0] = val

    shapes = [jax.ShapeDtypeStruct(w.shape, F32) for w in weights]
    res = pl.pallas_call(
        body, name="shard_update", out_shape=shapes * 4, compiler_params=pltpu.CompilerParams(vmem_limit_bytes=VMEM_LIMIT)
    )(own_3, land_3, land_g, *weights, *m_in, *v_in, token)
    return [res[n * k : n * (k + 1)] for k in range(4)]


VEC_LAYOUT = (("b_ada", 3 * D), ("norm_g", D), ("attn_sinks", 128), ("conv_b", D), ("rg_ba", D), ("rg_bx", D), ("rg_lambda", D),
              ("final_g", D), ("conv_w", CONV_WIDTH * D), ("loss", 128))
VEC_OFFSET = {name: sum(w for _, w in VEC_LAYOUT[:k]) for k, (name, _) in enumerate(VEC_LAYOUT)}
VEC_ROWS = 112
VEC_WIDTH = VEC_ROWS * 128
assert sum(w for _, w in VEC_LAYOUT) <= VEC_WIDTH
N_REPLICATED = 8


def _small_update(vec_all, c_all_t, me, rep_w, rep_m, rep_v, ada, conv):
    n_ada = ada[0].shape[1]
    n_conv = conv[0].shape[1]

    def body(*refs):
        me_ref, vec_ref, ct_ref = refs[:3]
        w_refs, m_refs, v_refs = (refs[3 + N_REPLICATED * k : 3 + N_REPLICATED * (k + 1)] for k in range(3))
        pos = 3 + 3 * N_REPLICATED
        ada_refs, conv_refs = refs[pos : pos + 3], refs[pos + 3 : pos + 6]
        outs = refs[pos + 6 :]
        loss_ref, rep_outs = outs[0], outs[1 : 1 + 4 * N_REPLICATED]
        ada_outs, conv_outs = outs[1 + 4 * N_REPLICATED : 5 + 4 * N_REPLICATED], outs[5 + 4 * N_REPLICATED : 9 + 4 * N_REPLICATED]
        conv_buf = outs[9 + 4 * N_REPLICATED]

        total = vec_ref[0:1, :]
        for slot in range(1, N_DEV):
            total = total + vec_ref[slot : slot + 1, :]
        loss_ref[...] = total[:, VEC_OFFSET["loss"] : VEC_OFFSET["loss"] + 1]
        for k, (name, _) in enumerate(VEC_LAYOUT[:N_REPLICATED]):
            width = w_refs[k].shape[1]
            g = total[:, VEC_OFFSET[name] : VEC_OFFSET[name] + width]
            d, m2, v2 = _adamw(w_refs[k][...], g, m_refs[k][...], v_refs[k][...])
            for which, val in enumerate((g, d, m2, v2)):
                rep_outs[which * N_REPLICATED + k][...] = val

        device = me_ref[0]
        for k in range(CONV_WIDTH):
            conv_buf[k : k + 1, :] = total[:, VEC_OFFSET["conv_w"] + D * k : VEC_OFFSET["conv_w"] + D * (k + 1)]
        g_conv = conv_buf[:, pl.ds(pl.multiple_of(device * n_conv, 128), n_conv)]
        for ref, val in zip(conv_outs, (g_conv,) + _adamw(conv_refs[0][...], g_conv, conv_refs[1][...], conv_refs[2][...])):
            ref[...] = val

        dmod = vec_ref[:, pl.ds(pl.multiple_of(device * n_ada, 128), n_ada)]
        g_ada = _dot(ct_ref[...], dmod)
        for ref, val in zip(ada_outs, (g_ada,) + _adamw(ada_refs[0][...], g_ada, ada_refs[1][...], ada_refs[2][...])):
            ref[...] = val

    vmem = pl.BlockSpec(memory_space=pltpu.VMEM)
    rep_shapes = [jax.ShapeDtypeStruct(w.shape, F32) for w in rep_w]
    out_shape = ([jax.ShapeDtypeStruct((1, 1), F32)] + rep_shapes * 4 + [jax.ShapeDtypeStruct(ada[0].shape, F32)] * 4
                 + [jax.ShapeDtypeStruct(conv[0].shape, F32)] * 4)
    n_in = 3 + 3 * N_REPLICATED + 6
    res = pl.pallas_call(
        body,
        name="small_update",
        in_specs=[pl.BlockSpec(memory_space=pltpu.SMEM)] + [vmem] * (n_in - 1),
        out_specs=[vmem] * len(out_shape),
        out_shape=out_shape,
        scratch_shapes=[pltpu.VMEM((CONV_WIDTH, D), F32)],
        compiler_params=pltpu.CompilerParams(vmem_limit_bytes=VMEM_LIMIT),
    )(me, vec_all, c_all_t, *rep_w, *rep_m, *rep_v, *ada, *conv)
    loss = res[0]
    reps = [res[1 + N_REPLICATED * k : 1 + N_REPLICATED * (k + 1)] for k in range(4)]
    return loss, reps, res[1 + 4 * N_REPLICATED : 5 + 4 * N_REPLICATED], res[5 + 4 * N_REPLICATED : 9 + 4 * N_REPLICATED]


def kernel(x, c, positions, w_ada, b_ada, norm_g, w_in, attn_sinks, conv_w, conv_b, rg_wa, rg_ba, rg_wx, rg_bx, rg_lambda, w_attn_proj, w_rnn_proj, w_out, final_g, loss_target, m_w_ada, m_b_ada, m_norm_g, m_w_in, m_attn_sinks, m_conv_w, m_conv_b, m_rg_wa, m_rg_ba, m_rg_wx, m_rg_bx, m_rg_lambda, m_w_attn_proj, m_w_rnn_proj, m_w_out, m_final_g, v_w_ada, v_b_ada, v_norm_g, v_w_in, v_attn_sinks, v_conv_w, v_conv_b, v_rg_wa, v_rg_ba, v_rg_wx, v_rg_bx, v_rg_lambda, v_w_attn_proj, v_w_rnn_proj, v_w_out, v_final_g):
    my_x, my_y, my_c = _position()
    me = _slot((my_x, my_y, my_c))
    x2, target = x[0], loss_target[0]
    pos = positions[0]

    gate_rows = lambda w: w[0].transpose(1, 0, 2).reshape(GATE_SHARD, D)
    w_pack = jnp.concatenate([w_attn_proj[0], w_rnn_proj[0], w_out[0], gate_rows(rg_wa), gate_rows(rg_wx)], axis=0).astype(BF16)
    conv_rows = jnp.pad(conv_w[0], ((0, 8 - CONV_WIDTH), (0, 0)))
    w_in_t = w_in[0].T
    n_ada = w_ada.shape[2]
    b_cols = lax.dynamic_slice(b_ada, (0, me * n_ada), (1, n_ada))

    tables = _rope_tables(pos)
    where = jnp.stack([_chip_of((my_x, my_y)), me]).astype(jnp.int32)
    proj, h, mod, lane_tables, w_t_all, w_all, c_rows, conv_all = _fwd_in(
        x2, jnp.broadcast_to(c, (8, D)), conv_rows, w_ada[0], b_cols, norm_g, w_in_t.astype(BF16), tables, w_pack, where
    )
    c_all = c_rows[:, 0, :]
    conv_full = conv_all[:, :CONV_WIDTH, :].transpose(1, 0, 2).reshape(CONV_WIDTH, D)
    gate = mod[:, 2 * D :]
    o = _attn_fwd(proj, attn_sinks)
    keep = (pos != 0).astype(F32)[:, None]
    hh = _rnn_fwd(proj, keep, w_all, rg_ba, rg_bx, rg_lambda, conv_full, conv_b)

    dres, d_o, d_hh, d_ga, d_gr, d_ma, d_mr, gw3, mid_vec = _mid(x2, target, o, hh, proj, w_all, gate, final_g[None, :])
    part_3 = gw3.reshape(3, 4, 2, ROW_SHARD, D)
    d_q, d_k, d_v, d_sink, sum_3_bf, own_3 = _attn_bwd(proj, o, d_o, (lane_tables, tables), attn_sinks, part_3)
    d_xr, g_gates, rnn_vec, land_3 = _rnn_bwd(proj, hh, d_hh, keep, w_all, rg_ba, rg_bx, rg_lambda, conv_full, conv_b, sum_3_bf)
    dsegs = (d_q, d_k, d_v, d_ga, d_xr, d_gr, d_ma, d_mr)
    own_w, sum_w_bf, land_g = _bwd_gw(dsegs, h, g_gates.reshape(2, 4, N_DEV, GATE_SHARD, 256))
    w_sems, sum_w_thru, land_w_thru, token = _exchange_start(
        "w_exchange_start", _w_exchange_copies, 2 * len(CHIP_FLIPS), sum_w_bf, (len(CHIP_FLIPS), W_IN_SHARD, D)
    )
    grad_x, dh_vec = _bwd_dh(dsegs, w_t_all, x2, dres, mod, norm_g, token)
    land_w = _exchange_wait("w_exchange_wait", _w_exchange_copies, w_sems, sum_w_thru, land_w_thru, (dh_vec,))

    pieces = dict(
        b_ada=jnp.concatenate([dh_vec[0:1], dh_vec[1:2], mid_vec[1:2]], axis=1),
        norm_g=dh_vec[2:3],
        attn_sinks=jnp.pad(d_sink[:, :, 0].reshape(1, 16), ((0, 0), (0, 112))),
        conv_b=rnn_vec[3:4],
        rg_ba=rnn_vec[0:1],
        rg_bx=rnn_vec[1:2],
        rg_lambda=rnn_vec[2:3],
        final_g=mid_vec[0:1],
        conv_w=rnn_vec[4:8].reshape(1, CONV_WIDTH * D),
        loss=mid_vec[2:3, 0:128],
    )
    vec = jnp.concatenate([pieces[name] for name, _ in VEC_LAYOUT], axis=1)
    vec = jnp.pad(vec, ((0, 0), (0, VEC_WIDTH - vec.shape[1]))).reshape(VEC_ROWS, 128)
    vec_sems, vec_thru, vec_land, vec_token = _exchange_start(
        "vec_exchange_start", _vec_exchange_copies, 2 * N_DEV - 1, vec, (N_DEV, VEC_ROWS, 128)
    )
    w_in_res = _w_in_update(own_w, land_w, w_in_t, m_w_in[0].T, v_w_in[0].T, vec_token)
    shard_res = _shard_update(
        own_3, land_3, land_g,
        (w_attn_proj, w_rnn_proj, w_out, rg_wa, rg_wx),
        (m_w_attn_proj, m_w_rnn_proj, m_w_out, m_rg_wa, m_rg_wx),
        (v_w_attn_proj, v_w_rnn_proj, v_w_out, v_rg_wa, v_rg_wx),
        vec_token,
    )
    vec_all = _exchange_wait(
        "vec_exchange_wait", _vec_exchange_copies, vec_sems, vec_thru, vec_land, (w_in_res[0], shard_res[0][0])
    )
    rep_w = [b_ada, norm_g, attn_sinks, conv_b, rg_ba, rg_bx, rg_lambda, final_g[None, :]]
    rep_m = [m_b_ada, m_norm_g, m_attn_sinks, m_conv_b, m_rg_ba, m_rg_bx, m_rg_lambda, m_final_g[None, :]]
    rep_v = [v_b_ada, v_norm_g, v_attn_sinks, v_conv_b, v_rg_ba, v_rg_bx, v_rg_lambda, v_final_g[None, :]]
    loss, reps, ada_res, conv_res = _small_update(
        vec_all.reshape(N_DEV, VEC_WIDTH), c_all.T, me.astype(jnp.int32).reshape(1), rep_w, rep_m, rep_v,
        (w_ada[0], m_w_ada[0], v_w_ada[0]), (conv_w[0], m_conv_w[0], v_conv_w[0]),
    )

    def by_weight(k):
        b, n, s, cb, ba, bx, lam, fg = reps[k]
        ap, rp, out, wa, wx = shard_res[k]
        return [ada_res[k][None], b, n, w_in_res[k].T[None], s, conv_res[k][None], cb, wa, ba, wx, bx, lam, ap, rp, out, fg[0]]

    return (loss[0, 0], grad_x[None], *by_weight(0), *by_weight(1), *by_weight(2), *by_weight(3))
```

```python
import math

import jax
import jax.numpy as jnp
from jax import lax
from jax.experimental import pallas as pl
from jax.experimental.pallas import tpu as pltpu

F32 = jnp.float32
BF16 = jnp.bfloat16

D = 1024
HEAD_DIM = 64
ROT_DIM = 16
ROPE_THETA = 500000.0
WINDOW = 128
LRU_C = 8.0
CONV_WIDTH = 4
NORM_EPS = 1e-6
IN_W = 6656
SEG_Q, SEG_K, SEG_V, SEG_GA, SEG_XR, SEG_GR, SEG_MA, SEG_MR = 0, 1024, 1280, 1536, 2560, 3584, 4608, 5632
N_DEV = 8
W_IN_SHARD = IN_W // N_DEV
ROW_SHARD = D // N_DEV
GATE_SHARD = 256 // N_DEV

ADAM_LR = 0.001
ADAM_B1 = 0.9
ADAM_B2 = 0.999
ADAM_EPS = 1e-08
ADAM_WD = 0.01
ADAM_STEP = 10

VMEM_LIMIT = 60 * 1024 * 1024
MESH = pl.DeviceIdType.MESH


def _dot(a, b):
    return jnp.dot(a, b, preferred_element_type=F32)


def _dot_nt(a, b):
    return lax.dot_general(a, b, (((1,), (1,)), ((), ())), preferred_element_type=F32)


def _dot_tn(a, b):
    return lax.dot_general(a, b, (((0,), (0,)), ((), ())), preferred_element_type=F32)


def _sigmoid(z):
    return 1.0 / (1.0 + jnp.exp(-z))


def _params(n_grid, vmem=VMEM_LIMIT):
    return pltpu.CompilerParams(dimension_semantics=("arbitrary",) * n_grid, vmem_limit_bytes=vmem)


def _position():
    return lax.axis_index("x"), lax.axis_index("y"), lax.axis_index("c")


def _flip(pos, k):
    x, y, c = pos
    return (1 - x if k & 4 else x, 1 - y if k & 2 else y, 1 - c if k & 1 else c)


def _slot(pos):
    return 4 * pos[0] + 2 * pos[1] + pos[2]


def _direct_gather_copies(src_hbm, all_hbm, send_sems, recv_sems, arriving):
    me = _position()
    return [
        pltpu.make_async_remote_copy(
            src_ref=src_hbm,
            dst_ref=all_hbm.at[_slot(_flip(me, k) if arriving else me)],
            send_sem=send_sems.at[k - 1],
            recv_sem=recv_sems.at[k - 1],
            device_id=_flip(me, k),
            device_id_type=MESH,
        )
        for k in range(1, N_DEV)
    ]


CHIP_FLIPS = (2, 4, 6)


def _chip_of(pos):
    return 2 * pos[0] + pos[1]


def _chip_copies(srcs, lands, views, send_sems, recv_sems):
    me = _position()
    copies = []
    for a, (src, land) in enumerate(zip(srcs, lands)):
        for j, k in enumerate(CHIP_FLIPS):
            peer = _flip(me, k)
            copies.append(
                pltpu.make_async_remote_copy(
                    src_ref=views[a](src, peer),
                    dst_ref=land.at[j],
                    send_sem=send_sems.at[len(CHIP_FLIPS) * a + j],
                    recv_sem=recv_sems.at[len(CHIP_FLIPS) * a + j],
                    device_id=peer,
                    device_id_type=MESH,
                )
            )
    return copies


PACK_GATE = 3 * ROW_SHARD
PACK_ROWS = PACK_GATE + 2 * GATE_SHARD


def _fwd_in(x, c_rows, conv_rows, w_ada, b_cols, norm_g, w_shard, tables, pack, where):
    t = x.shape[0]
    tm = min(1024, t)
    tn = IN_W // 4
    n_i, n_j = t // tm, IN_W // tn
    n_rope = (SEG_V - SEG_Q) // 128

    n_ada = w_ada.shape[1]

    def body(where_ref, x_ref, wada_ref, bcols_ref, ng_ref, ct_ref, st_ref, shard_hbm, pack_hbm, c_hbm, conv_hbm,
             proj_ref, h_ref, mod_ref, lane_hbm, wt_hbm, all_hbm, call_hbm, convall_hbm,
             wall, hbuf, cbuf, mcols, mod_all, mod_buf, lane,
             w_send, w_recv, load_sems, local_sem, lane_sem, p_send, p_recv, p_local,
             c_send, c_recv, c_local, v_send, v_recv, v_local, m_send, m_recv):
        j = pl.program_id(0)
        i = pl.program_id(1)
        me = _position()
        sibling = _flip(me, 1)
        chip = where_ref[0]
        slot = where_ref[1]

        def modulation():
            own_c = pltpu.make_async_copy(c_hbm, call_hbm.at[_slot(me)], c_local)
            own_v = pltpu.make_async_copy(conv_hbm, convall_hbm.at[_slot(me)], v_local)
            own_c.start()
            own_v.start()
            for cp in _direct_gather_copies(c_hbm, call_hbm, c_send, c_recv, arriving=False):
                cp.start()
            for cp in _direct_gather_copies(conv_hbm, convall_hbm, v_send, v_recv, arriving=False):
                cp.start()
            for cp in _direct_gather_copies(c_hbm, call_hbm, c_send, c_recv, arriving=True):
                cp.wait_recv()
            own_c.wait()
            fetch = pltpu.make_async_copy(call_hbm, cbuf, c_local)
            fetch.start()
            fetch.wait()
            mcols[...] = _dot(cbuf[:, 0, :].astype(BF16), wada_ref[...].astype(BF16)) + bcols_ref[...]
            mod_all[slot] = mcols[...]
            swaps = lambda arriving: [
                pltpu.make_async_remote_copy(
                    src_ref=mcols,
                    dst_ref=mod_all.at[_slot(_flip(me, k) if arriving else me)],
                    send_sem=m_send.at[k - 1],
                    recv_sem=m_recv.at[k - 1],
                    device_id=_flip(me, k),
                    device_id_type=MESH,
                )
                for k in range(1, N_DEV)
            ]
            for cp in swaps(False):
                cp.start()
            for cp in swaps(True):
                cp.wait_recv()
            mine = lax.broadcasted_iota(jnp.int32, (N_DEV, n_ada), 0) == slot
            for s in range(N_DEV):
                row = jnp.sum(jnp.where(mine, mod_all[s], 0.0), axis=0, keepdims=True)
                mod_buf[:, n_ada * s : n_ada * (s + 1)] = row
            for cp in swaps(False):
                cp.wait_send()
            for cp in _direct_gather_copies(c_hbm, call_hbm, c_send, c_recv, arriving=False):
                cp.wait_send()
            for cp in _direct_gather_copies(conv_hbm, convall_hbm, v_send, v_recv, arriving=True):
                cp.wait_recv()
            for cp in _direct_gather_copies(conv_hbm, convall_hbm, v_send, v_recv, arriving=False):
                cp.wait_send()
            own_v.wait()

        def shard_of(block):
            return wall.at[_chip_of(block), pl.ds(W_IN_SHARD * block[2], W_IN_SHARD), :]

        def shard_copy(k, block, to, src=None):
            return pltpu.make_async_remote_copy(
                src_ref=shard_of(block),
                dst_ref=shard_of(block),
                send_sem=w_send.at[k],
                recv_sem=w_recv.at[k],
                device_id=to,
                device_id_type=MESH,
            )

        own_shard = pltpu.make_async_copy(shard_hbm, shard_of(me), local_sem)
        own_pack = pltpu.make_async_copy(pack_hbm, all_hbm.at[_slot(me)], p_local)
        keep_lane = pltpu.make_async_copy(lane, lane_hbm, lane_sem)

        def lane_tables():
            d = lax.broadcasted_iota(jnp.int32, (128, 128), 0) % HEAD_DIM
            half = ROT_DIM // 2

            def fill(b, carry):
                cols = pl.ds(pl.multiple_of(128 * b, 128), 128)
                cos = jnp.concatenate([ct_ref[:, cols]] * (128 // half), axis=0)
                sin = jnp.concatenate([st_ref[:, cols]] * (128 // half), axis=0)
                lane[0, cols, :] = jnp.where(d < ROT_DIM, cos, 1.0).T
                lane[1, cols, :] = jnp.where(d < half, -sin, 0.0).T
                lane[2, cols, :] = jnp.where((d >= half) & (d < ROT_DIM), sin, 0.0).T
                return carry

            lax.fori_loop(0, t // 128, fill, 0)

        def keep_tile(q):
            return [
                pltpu.make_async_copy(wall.at[q, pl.ds(W_IN_SHARD * c, W_IN_SHARD), :], wt_hbm.at[2 * q + c], load_sems.at[2 * (q ^ chip) + c])
                for c in range(2)
            ]

        @pl.when((j == 0) & (i == 0))
        def _():
            modulation()
            mod_ref[...] = mod_buf[...]
            own_shard.start()
            own_shard.wait()
            shard_copy(0, me, sibling).start()
            for rel in range(1, 4):
                shard_copy(rel, me, _flip(me, 2 * rel)).start()
            own_pack.start()
            for cp in _direct_gather_copies(pack_hbm, all_hbm, p_send, p_recv, arriving=False):
                cp.start()
            lane_tables()
            keep_lane.start()
            shard_copy(0, sibling, me).wait_recv()
            for cp in keep_tile(chip):
                cp.start()

        for rel in range(1, 4):

            @pl.when((j == 1) & (i == min(rel, n_i) - 1))
            def _(rel=rel):
                there = _flip(me, 2 * rel)
                shard_copy(rel, there, me).wait_recv()
                shard_copy(3 + rel, there, sibling).start()

            @pl.when((j == rel) & (i == 0))
            def _(rel=rel):
                shard_copy(3 + rel, _flip(sibling, 2 * rel), me).wait_recv()
                for cp in keep_tile(chip ^ rel):
                    cp.start()

        rows = pl.ds(pl.multiple_of(i * tm, tm), tm)

        @pl.when(j == 0)
        def _():
            xv = x_ref[...]
            r = lax.rsqrt(jnp.mean(xv * xv, axis=-1, keepdims=True) + NORM_EPS)
            shift = mod_buf[:, 0:D]
            scale = mod_buf[:, D : 2 * D]
            hv = (((xv * r) * ng_ref[...]) * (1.0 + scale) + shift).astype(BF16)
            hbuf[rows, :] = hv
            h_ref[...] = hv

        proj_ref[...] = _dot_nt(hbuf[rows, :], wall[chip ^ j])

        @pl.when((chip ^ j) == 0)
        def _():
            cos, s1, s2 = lane[0, rows, :], lane[1, rows, :], lane[2, rows, :]
            for g in range(n_rope):
                cols = slice(128 * g, 128 * (g + 1))
                proj_ref[:, cols] = _rope(proj_ref[:, cols], cos, s1, s2)

        @pl.when((j == n_j - 1) & (i == n_i - 1))
        def _():
            shard_copy(0, me, sibling).wait_send()
            for rel in range(1, 4):
                shard_copy(rel, me, _flip(me, 2 * rel)).wait_send()
                shard_copy(3 + rel, _flip(me, 2 * rel), sibling).wait_send()
            for q in range(N_DEV // 2):
                for cp in keep_tile(q):
                    cp.wait()
            for cp in _direct_gather_copies(pack_hbm, all_hbm, p_send, p_recv, arriving=True):
                cp.wait_recv()
            for cp in _direct_gather_copies(pack_hbm, all_hbm, p_send, p_recv, arriving=False):
                cp.wait_send()
            own_pack.wait()
            keep_lane.wait()

    tab = pl.BlockSpec((ROT_DIM // 2, t), lambda j, i, c: (0, 0))
    any_spec = pl.BlockSpec(memory_space=pl.ANY)
    return pl.pallas_call(
        body,
        name="fwd_in",
        grid_spec=pltpu.PrefetchScalarGridSpec(
            num_scalar_prefetch=1,
            grid=(n_j, n_i),
            in_specs=[
                pl.BlockSpec((tm, D), lambda j, i, c: (jnp.where(j == 0, i, 0), 0)),
                pl.BlockSpec(w_ada.shape, lambda j, i, c: (0, 0)),
                pl.BlockSpec((1, n_ada), lambda j, i, c: (0, c[1])),
                pl.BlockSpec((1, D), lambda j, i, c: (0, 0)),
                tab, tab, any_spec, any_spec, any_spec, any_spec,
            ],
            out_specs=[pl.BlockSpec((tm, tn), lambda j, i, c: (i, c[0] ^ j)),
                       pl.BlockSpec((tm, D), lambda j, i, c: (jnp.where(j == 0, i, n_i - 1), 0)),
                       pl.BlockSpec((1, 3 * D), lambda j, i, c: (0, 0)),
                       any_spec, any_spec, any_spec, any_spec, any_spec],
            scratch_shapes=[
                pltpu.VMEM((N_DEV // 2, tn, D), BF16),
                pltpu.VMEM((t, D), BF16),
                pltpu.VMEM((N_DEV,) + c_rows.shape, F32),
                pltpu.VMEM((N_DEV, n_ada), F32),
                pltpu.VMEM((N_DEV, N_DEV, n_ada), F32),
                pltpu.VMEM((1, 3 * D), F32),
                pltpu.VMEM((3, t, 128), F32),
                pltpu.SemaphoreType.DMA((N_DEV - 1,)),
                pltpu.SemaphoreType.DMA((N_DEV - 1,)),
                pltpu.SemaphoreType.DMA((N_DEV,)),
                pltpu.SemaphoreType.DMA,
                pltpu.SemaphoreType.DMA,
                pltpu.SemaphoreType.DMA((N_DEV - 1,)),
                pltpu.SemaphoreType.DMA((N_DEV - 1,)),
                pltpu.SemaphoreType.DMA,
                pltpu.SemaphoreType.DMA((N_DEV - 1,)),
                pltpu.SemaphoreType.DMA((N_DEV - 1,)),
                pltpu.SemaphoreType.DMA,
                pltpu.SemaphoreType.DMA((N_DEV - 1,)),
                pltpu.SemaphoreType.DMA((N_DEV - 1,)),
                pltpu.SemaphoreType.DMA,
                pltpu.SemaphoreType.DMA((N_DEV - 1,)),
                pltpu.SemaphoreType.DMA((N_DEV - 1,)),
            ],
        ),
        out_shape=[jax.ShapeDtypeStruct((t, IN_W), F32), jax.ShapeDtypeStruct((t, D), BF16),
                   jax.ShapeDtypeStruct((1, 3 * D), F32),
                   jax.ShapeDtypeStruct((3, t, 128), F32),
                   jax.ShapeDtypeStruct((N_DEV,) + w_shard.shape, w_shard.dtype),
                   jax.ShapeDtypeStruct((N_DEV,) + pack.shape, pack.dtype),
                   jax.ShapeDtypeStruct((N_DEV,) + c_rows.shape, F32),
                   jax.ShapeDtypeStruct((N_DEV,) + conv_rows.shape, F32)],
        compiler_params=_params(2),
    )(where, x, w_ada, b_cols, norm_g, *tables, w_shard, pack, c_rows, conv_rows)


def _rope_tables(positions):
    inv_freq = ROPE_THETA ** (-jnp.arange(0, ROT_DIM, 2, dtype=F32) / ROT_DIM)
    ang_t = inv_freq[:, None] * positions.astype(F32)[None, :]
    return jnp.cos(ang_t), jnp.sin(ang_t)


def _rope(tv, cos, s1, s2):
    cols = []
    for g in range(tv.shape[1] // 128):
        blk = tv[:, 128 * g : 128 * (g + 1)]
        cols.append(blk * cos + pltpu.roll(blk, 128 - 8, 1) * s1 + pltpu.roll(blk, 8, 1) * s2)
    return cols[0] if len(cols) == 1 else jnp.concatenate(cols, axis=1)


def _rope_transposed(gv, cos, s1, s2):
    cols = []
    for g in range(gv.shape[1] // 128):
        blk = gv[:, 128 * g : 128 * (g + 1)]
        cols.append(blk * cos + pltpu.roll(blk * s1, 8, 1) + pltpu.roll(blk * s2, 128 - 8, 1))
    return cols[0] if len(cols) == 1 else jnp.concatenate(cols, axis=1)


def _lane_lo():
    return lax.broadcasted_iota(jnp.int32, (1, 128), 1) < HEAD_DIM


def _row_lo():
    return lax.broadcasted_iota(jnp.int32, (128, 1), 0) < HEAD_DIM


def _split_heads(tile, j):
    lo = _lane_lo()
    parts = []
    for g in (2 * j, 2 * j + 1):
        blk = tile[:, 128 * g : 128 * (g + 1)]
        parts += [jnp.where(lo, blk, 0.0), jnp.where(lo, 0.0, blk)]
    return jnp.concatenate(parts, axis=0)


def _dup_lanes(kv, j):
    rolled = pltpu.roll(kv, HEAD_DIM, 1)
    lo = _lane_lo()
    return jnp.where(lo, kv, rolled) if j == 0 else jnp.where(lo, rolled, kv)


def _fold_lanes(acc):
    return acc + pltpu.roll(acc, HEAD_DIM, 1)


def _fill_bias_t(bias_ref):
    kj = lax.broadcasted_iota(jnp.int32, (256, 128), 0)
    qi = lax.broadcasted_iota(jnp.int32, (256, 128), 1)
    diff = qi + WINDOW - kj
    band = (diff >= 0) & (diff < WINDOW)
    bias_ref[0] = jnp.where(band, 0.0, -1e30)
    bias_ref[1] = jnp.where(band & (kj >= 128), 0.0, -1e30)


def _sink_row(sink_ref, pair, j):
    return jnp.concatenate([jnp.full((1, 128), sink_ref[0, 8 * pair + 4 * j + h], F32) for h in range(4)], axis=1)


def _softmax_t(q4, kw, bias4, sink):
    s = _dot_nt(kw, q4) + bias4
    m = jnp.maximum(jnp.max(s, axis=0, keepdims=True), sink)
    p = jnp.exp(s - m)
    e_sink = jnp.exp(sink - m)
    denom = jnp.sum(p, axis=0, keepdims=True) + e_sink
    return p, e_sink, denom


def _attn_specs(t, tq):
    return dict(
        q=pl.BlockSpec((tq, 512), lambda p, i: (i, p)),
        k=pl.BlockSpec((t, 128), lambda p, i: (0, SEG_K // 128 + p)),
        v=pl.BlockSpec((t, 128), lambda p, i: (0, SEG_V // 128 + p)),
        sink=pl.BlockSpec(memory_space=pltpu.SMEM),
        act=pl.BlockSpec((tq, 512), lambda p, i: (i, p)),
    )


def _attn_fwd(proj, sinks):
    t = proj.shape[0]
    tq = min(512, t)
    nb = tq // 128
    n_tiles = t // tq
    sp = _attn_specs(t, tq)

    def body(q_ref, k_ref, v_ref, sink_ref, o_ref, kd, vdt, bias_ref):
        pair = pl.program_id(0)
        i = pl.program_id(1)

        @pl.when(i == 0)
        def _():
            kv, vv = k_ref[...], v_ref[...]
            for j in range(2):
                kd[j, 0:128, :] = jnp.zeros((128, 128), BF16)
                kd[j, 128:, :] = _dup_lanes(kv, j).astype(BF16)
                vdt[j, :, 0:128] = jnp.zeros((128, 128), BF16)
                vdt[j, :, 128:] = _dup_lanes(vv, j).T.astype(BF16)
            _fill_bias_t(bias_ref)

        row_lo = _row_lo()

        def block(b, carry):
            r0 = pl.multiple_of(b * 128, 128)
            gb = i * nb + b
            keys = pl.ds(pl.multiple_of(gb * 128, 128), 256)
            rows = pl.ds(r0, 128)
            qt = q_ref[rows, :] * (1.0 / math.sqrt(HEAD_DIM))
            bias = bias_ref[jnp.where(gb == 0, 1, 0)]
            bias4 = jnp.concatenate([bias] * 4, axis=1)
            cols = []
            for j in range(2):
                q4 = _split_heads(qt, j).astype(BF16)
                p, _, denom = _softmax_t(q4, kd[j, keys, :], bias4, _sink_row(sink_ref, pair, j))
                ot = _dot(vdt[j, :, keys], p.astype(BF16)) * (1.0 / denom)
                for gi in range(2):
                    z = jnp.where(row_lo, ot[:, 256 * gi : 256 * gi + 128], ot[:, 256 * gi + 128 : 256 * gi + 256])
                    cols.append(z.T)
            o_ref[rows, :] = jnp.concatenate(cols, axis=1).astype(BF16)
            return carry

        lax.fori_loop(0, nb, block, 0, unroll=True)

    return pl.pallas_call(
        body,
        name="attn_fwd",
        grid=(2, n_tiles),
        in_specs=[sp["q"], sp["k"], sp["v"], sp["sink"]],
        out_specs=sp["act"],
        out_shape=jax.ShapeDtypeStruct((t, D), BF16),
        scratch_shapes=[pltpu.VMEM((2, t + 128, 128), BF16), pltpu.VMEM((2, 128, t + 128), BF16), pltpu.VMEM((2, 256, 128), F32)],
        compiler_params=_params(2),
    )(proj, proj, proj, sinks)


def _attn_bwd(proj, o, d_o, tables, sinks, part_3):
    t = proj.shape[0]
    tq = min(512, t)
    nb = tq // 128
    n_tiles = t // tq
    lane_tables, (cos_t, sin_t) = tables
    sp = _attn_specs(t, tq)
    scale = 1.0 / math.sqrt(HEAD_DIM)
    half = ROT_DIM // 2

    def unrope_rows(z, c, s):
        out = []
        for base in (0, HEAD_DIM):
            g0, g1 = z[base : base + half], z[base + half : base + ROT_DIM]
            out += [g0 * c + g1 * s, g1 * c - g0 * s, z[base + ROT_DIM : base + HEAD_DIM]]
        return jnp.concatenate(out, axis=0)

    def swap(part_hbm, land, send_sem, recv_sem):
        sibling = _flip(_position(), 1)
        return pltpu.make_async_remote_copy(
            src_ref=part_hbm.at[:, :, sibling[2]], dst_ref=land, send_sem=send_sem, recv_sem=recv_sem,
            device_id=sibling, device_id_type=MESH,
        )

    def body(q_ref, k_ref, v_ref, o_ref, do_ref, ct_ref, st_ref, ck, s1k, s2k, sink_ref, part_hbm,
             dq_ref, dk_ref, dv_ref, dsink_ref, sum_hbm, mine_hbm, kd, kdt, vd, bias_ref, dkacc, dvacc, own, land, stage,
             send_sem, recv_sem, local_sems):
        pair = pl.program_id(0)
        i = pl.program_id(1)
        me = _position()
        load_own = pltpu.make_async_copy(part_hbm.at[:, :, me[2]], own, local_sems.at[0])

        @pl.when((pair == 0) & (i == 0))
        def _():
            swap(part_hbm, land, send_sem, recv_sem).start()
            load_own.start()

        @pl.when(i == 0)
        def _():
            kv, vv = k_ref[...], v_ref[...]
            for j in range(2):
                kdup = _dup_lanes(kv, j)
                kd[j, 0:128, :] = jnp.zeros((128, 128), BF16)
                kd[j, 128:, :] = kdup.astype(BF16)
                kdt[j, :, 0:128] = jnp.zeros((128, 128), BF16)
                kdt[j, :, 128:] = kdup.T.astype(BF16)
                vd[j, 0:128, :] = jnp.zeros((128, 128), BF16)
                vd[j, 128:, :] = _dup_lanes(vv, j).astype(BF16)
            _fill_bias_t(bias_ref)
            dkacc[...] = jnp.zeros_like(dkacc)
            dvacc[...] = jnp.zeros_like(dvacc)
            dsink_ref[...] = jnp.zeros_like(dsink_ref)

        row_lo = _row_lo()
        ones = jnp.ones((8, 128), BF16)

        def block(b, carry):
            r0 = pl.multiple_of(b * 128, 128)
            gb = i * nb + b
            keys = pl.ds(pl.multiple_of(gb * 128, 128), 256)
            rows = pl.ds(r0, 128)
            qt = q_ref[rows, :] * scale
            dob = do_ref[rows, :].astype(F32)
            prod = dob * o_ref[rows, :].astype(F32)
            ct, st = ct_ref[:, rows], st_ref[:, rows]
            bias = bias_ref[jnp.where(gb == 0, 1, 0)]
            bias4 = jnp.concatenate([bias] * 4, axis=1)
            cols = []
            for j in range(2):
                q4 = _split_heads(qt, j).astype(BF16)
                kw = kd[j, keys, :]
                p, e_sink, denom = _softmax_t(q4, kw, bias4, _sink_row(sink_ref, pair, j))
                inv = 1.0 / denom
                prob = p * inv
                do4 = _split_heads(dob, j).astype(BF16)
                delta = _dot_nt(ones, _split_heads(prod, j).astype(BF16))[0:1, :]
                dp = _dot_nt(vd[j, keys, :], do4)
                ds = (prob * (dp - delta)).astype(BF16)
                dkacc[j, keys, :] = dkacc[j, keys, :] + _dot(ds, q4)
                dvacc[j, keys, :] = dvacc[j, keys, :] + _dot(prob.astype(BF16), do4)
                dqt = _dot(kdt[j, :, keys], ds) * scale
                dsink = -(e_sink * inv) * delta
                for h in range(4):
                    row = 4 * j + h
                    total = jnp.sum(dsink[:, 128 * h : 128 * (h + 1)], axis=1, keepdims=True)
                    dsink_ref[0, row : row + 1, :] = dsink_ref[0, row : row + 1, :] + jnp.broadcast_to(total, (1, 128))
                for gi in range(2):
                    z = jnp.where(row_lo, dqt[:, 256 * gi : 256 * gi + 128], dqt[:, 256 * gi + 128 : 256 * gi + 256])
                    cols.append(unrope_rows(z, ct, st).T)
            dq_ref[rows, :] = jnp.concatenate(cols, axis=1).astype(BF16)
            return carry

        lax.fori_loop(0, nb, block, 0, unroll=True)

        @pl.when(i == n_tiles - 1)
        def _():
            lo = _lane_lo()
            dk = jnp.where(lo, _fold_lanes(dkacc[0, 128:, :]), _fold_lanes(dkacc[1, 128:, :]))
            dk_ref[...] = _rope_transposed(dk, ck[...], s1k[...], s2k[...]).astype(BF16)
            dv_ref[...] = jnp.where(lo, _fold_lanes(dvacc[0, 128:, :]), _fold_lanes(dvacc[1, 128:, :])).astype(BF16)

        @pl.when((pair == 1) & (i == n_tiles - 1))
        def _():
            swap(part_hbm, land, send_sem, recv_sem).wait()
            load_own.wait()
            own[...] = own[...] + land[...]
            stage[...] = own[...].astype(BF16)
            results = [pltpu.make_async_copy(stage, sum_hbm, local_sems.at[1]),
                       pltpu.make_async_copy(own.at[:, _chip_of(me)], mine_hbm, local_sems.at[2])]
            for cp in results:
                cp.start()
            for cp in results:
                cp.wait()

    kv_out = pl.BlockSpec((t, 128), lambda p, i: (0, p))
    tab_t = pl.BlockSpec((half, tq), lambda p, i: (0, i))
    tab_k = [pl.BlockSpec((None, t, 128), lambda p, i, a=a: (a, 0, 0)) for a in range(3)]
    return pl.pallas_call(
        body,
        name="attn_bwd",
        grid=(2, n_tiles),
        in_specs=[sp["q"], sp["k"], sp["v"], sp["act"], sp["act"], tab_t, tab_t, *tab_k, sp["sink"],
                  pl.BlockSpec(memory_space=pl.ANY)],
        out_specs=[sp["act"], kv_out, kv_out, pl.BlockSpec((1, 8, 128), lambda p, i: (p, 0, 0)),
                   pl.BlockSpec(memory_space=pl.ANY), pl.BlockSpec(memory_space=pl.ANY)],
        out_shape=[
            jax.ShapeDtypeStruct((t, D), BF16),
            jax.ShapeDtypeStruct((t, 256), BF16),
            jax.ShapeDtypeStruct((t, 256), BF16),
            jax.ShapeDtypeStruct((2, 8, 128), F32),
            jax.ShapeDtypeStruct((3, 4, ROW_SHARD, D), BF16),
            jax.ShapeDtypeStruct((3, ROW_SHARD, D), F32),
        ],
        scratch_shapes=[pltpu.VMEM((2, t + 128, 128), BF16), pltpu.VMEM((2, 128, t + 128), BF16), pltpu.VMEM((2, t + 128, 128), BF16),
                        pltpu.VMEM((2, 256, 128), F32), pltpu.VMEM((2, t + 128, 128), F32), pltpu.VMEM((2, t + 128, 128), F32),
                        pltpu.VMEM((3, 4, ROW_SHARD, D), F32), pltpu.VMEM((3, 4, ROW_SHARD, D), F32), pltpu.VMEM((3, 4, ROW_SHARD, D), BF16),
                        pltpu.SemaphoreType.DMA, pltpu.SemaphoreType.DMA, pltpu.SemaphoreType.DMA((3,))],
        compiler_params=_params(2),
    )(proj, proj, proj, o, d_o, cos_t, sin_t, lane_tables, lane_tables, lane_tables, sinks, part_3)


HALF = 512


def _softplus(y):
    u = jnp.exp(-jnp.abs(y))
    w = 1.0 + u
    log1p = jnp.where(w == 1.0, u, jnp.log(w) * (u / jnp.where(w == 1.0, 1.0, w - 1.0)))
    return jnp.maximum(y, 0.0) + log1p


def _gate_block(w_ref, blk):
    return w_ref[:, :, 256 * blk : 256 * (blk + 1)].reshape(256, 256)


def _gate_matmul(act_bf, w_ref):
    return jnp.concatenate([_dot(act_bf[:, 256 * blk : 256 * (blk + 1)], _gate_block(w_ref, blk)) for blk in range(2)], axis=1)


def _gate_matmul_t(grad_bf, w_ref):
    return jnp.concatenate([_dot_nt(grad_bf[:, 256 * blk : 256 * (blk + 1)], _gate_block(w_ref, blk)) for blk in range(2)], axis=1)


def _conv_and_gates(xe_ref, tm, cw_ref, cb_ref, wa_ref, wx_ref, ba_ref, bx_ref, lam_ref, keep):
    xc = cb_ref[...] + sum(cw_ref[k : k + 1, :] * xe_ref[pl.ds(5 + k, tm), :] for k in range(CONV_WIDTH))
    xc_bf = xc.astype(BF16)
    r = _sigmoid(_gate_matmul(xc_bf, wa_ref) + ba_ref[...])
    ig = _sigmoid(_gate_matmul(xc_bf, wx_ref) + bx_ref[...])
    sp = _softplus(-lam_ref[...])
    log_a = -LRU_C * r * sp
    tanh = jnp.tanh(-log_a)
    mult = jnp.sqrt(2.0 * tanh / (1.0 + tanh))
    a = jnp.where(keep, jnp.exp(log_a), 0.0)
    mult = jnp.where(keep, mult, 1.0)
    return xc, xc_bf, r, ig, sp, a, mult


def _scan_rows(a_ref, b_ref, h_ref, carry_ref, tm, reverse):
    width = a_ref.shape[1]
    row = lax.broadcasted_iota(jnp.int32, (8, width), 0)

    def group(g, carry):
        gg = tm // 8 - 1 - g if reverse else g
        r0 = pl.multiple_of(gg * 8, 8)
        a = a_ref[pl.ds(r0, 8), :]
        b = b_ref[pl.ds(r0, 8), :]
        for s in (1, 2, 4):
            shift = 8 - s if reverse else s
            a_sh = pltpu.roll(a, shift, 0)
            b_sh = pltpu.roll(b, shift, 0)
            ok = (row < 8 - s) if reverse else (row >= s)
            b = jnp.where(ok, a * b_sh + b, b)
            a = jnp.where(ok, a * a_sh, a)
        h = a * carry + b
        h_ref[pl.ds(r0, 8), :] = h
        last = h[0:1, :] if reverse else h[7:8, :]
        return jnp.broadcast_to(last, (8, width))

    carry_ref[...] = lax.fori_loop(0, tm // 8, group, carry_ref[...])


def _rnn_specs(t, tm, order):
    n = t // tm
    return dict(
        tile=pl.BlockSpec((tm, HALF), lambda ch, i: (order(i, n), ch)),
        before=lambda col0: pl.BlockSpec((8, HALF), lambda ch, i: (jnp.maximum(order(i, n) * (tm // 8) - 1, 0), col0 + ch)),
        proj_tile=lambda col0: pl.BlockSpec((tm, HALF), lambda ch, i: (order(i, n), col0 + ch)),
        keep=pl.BlockSpec((tm, 1), lambda ch, i: (order(i, n), 0)),
        gates=lambda which: pl.BlockSpec((N_DEV, GATE_SHARD, HALF), lambda ch, i: (0, PACK_GATE // GATE_SHARD + which, ch)),
        gate_grads=pl.BlockSpec((2, 2, 256, 256), lambda ch, i: (0, ch, 0, 0)),
        vec=pl.BlockSpec((1, HALF), lambda ch, i: (0, ch)),
        conv=pl.BlockSpec((CONV_WIDTH, HALF), lambda ch, i: (0, ch)),
    )


def _rnn_fwd(proj, keep, w_all, rg_ba, rg_bx, rg_lambda, conv_w, conv_b):
    t = proj.shape[0]
    tm = min(512, t)
    sp = _rnn_specs(t, tm, lambda i, n: i)

    def body(xr_ref, xb_ref, keep_ref, wa_ref, wx_ref, ba_ref, bx_ref, lam_ref, cw_ref, cb_ref, hh_ref, xe, a_buf, b_buf, carry):
        i = pl.program_id(1)

        @pl.when(i == 0)
        def _():
            carry[...] = jnp.zeros_like(carry)

        xe[0:8, :] = jnp.where(i > 0, xb_ref[...], 0.0)
        xe[8:, :] = xr_ref[...]
        keep_rows = keep_ref[...] != 0.0
        xc, _, _, ig, _, a, mult = _conv_and_gates(xe, tm, cw_ref, cb_ref, wa_ref, wx_ref, ba_ref, bx_ref, lam_ref, keep_rows)
        a_buf[...] = a
        b_buf[...] = mult * (ig * xc)
        _scan_rows(a_buf, b_buf, hh_ref, carry, tm, reverse=False)

    return pl.pallas_call(
        body,
        name="rnn_fwd",
        grid=(2, t // tm),
        in_specs=[sp["proj_tile"](SEG_XR // HALF), sp["before"](SEG_XR // HALF), sp["keep"], sp["gates"](0), sp["gates"](1),
                  sp["vec"], sp["vec"], sp["vec"], sp["conv"], sp["vec"]],
        out_specs=sp["tile"],
        out_shape=jax.ShapeDtypeStruct((t, D), F32),
        scratch_shapes=[pltpu.VMEM((tm + 8, HALF), F32), pltpu.VMEM((tm, HALF), F32), pltpu.VMEM((tm, HALF), F32), pltpu.VMEM((8, HALF), F32)],
        compiler_params=_params(2),
    )(proj, proj, keep, w_all, w_all, rg_ba, rg_bx, rg_lambda, conv_w, conv_b)


def _rnn_bwd(proj, hh, d_hh, keep, w_all, rg_ba, rg_bx, rg_lambda, conv_w, conv_b, sum_3_bf):
    t = proj.shape[0]
    tm = min(512, t)
    n_tiles = t // tm
    sp = _rnn_specs(t, tm, lambda i, n: n - 1 - i)

    def body(xr_ref, xb_ref, hh_ref, hb_ref, dhh_ref, keep_ref, wa_ref, wx_ref, ba_ref, bx_ref, lam_ref, cw_ref, cb_ref,
             s3_hbm, dxr_ref, gw_ref, gvec_ref, l3_hbm, xe, he, ae, c_buf, g_buf, dxe, carry, a_next, send_sems, recv_sems):
        ch = pl.program_id(0)
        i = pl.program_id(1)
        first_tile = i == n_tiles - 1
        views = [lambda ref, peer: ref.at[:, _chip_of(peer)]]

        @pl.when((ch == 0) & (i == 0))
        def _():
            for cp in _chip_copies([s3_hbm], [l3_hbm], views, send_sems, recv_sems):
                cp.start()

        @pl.when((ch == 1) & (i == n_tiles - 1))
        def _():
            for cp in _chip_copies([s3_hbm], [l3_hbm], views, send_sems, recv_sems):
                cp.wait_recv()
                cp.wait_send()

        @pl.when(i == 0)
        def _():
            carry[...] = jnp.zeros_like(carry)
            a_next[...] = jnp.zeros_like(a_next)
            dxe[tm:, :] = jnp.zeros((8, HALF), F32)
            gw_ref[...] = jnp.zeros_like(gw_ref)
            gvec_ref[...] = jnp.zeros_like(gvec_ref)

        xe[0:8, :] = jnp.where(first_tile, 0.0, xb_ref[...])
        xe[8:, :] = xr_ref[...]
        he[0:8, :] = jnp.where(first_tile, 0.0, hb_ref[...])
        he[8:, :] = hh_ref[...]
        keep_rows = keep_ref[...] != 0.0
        xc, xc_bf, r, ig, sp_, a, mult = _conv_and_gates(xe, tm, cw_ref, cb_ref, wa_ref, wx_ref, ba_ref, bx_ref, lam_ref, keep_rows)

        ae[0:tm, :] = a
        ae[tm:, :] = a_next[...]
        a_next[...] = jnp.broadcast_to(a[0:1, :], (8, HALF))
        c_buf[...] = ae[pl.ds(1, tm), :]
        g_buf[...] = dhh_ref[...]
        _scan_rows(c_buf, g_buf, g_buf, carry, tm, reverse=True)
        g = g_buf[...]

        h_prev = he[pl.ds(7, tm), :]
        bb = ig * xc
        d_ig = g * mult * xc
        d_xc = g * mult * ig
        d_log_a = jnp.where(keep_rows, g * h_prev * a - (g * bb) * (a * a) / mult, 0.0)
        d_r = d_log_a * (-LRU_C * sp_)
        d_sp = jnp.sum(d_log_a * (-LRU_C * r), axis=0, keepdims=True)
        d_zr = d_r * r * (1.0 - r)
        d_zi = d_ig * ig * (1.0 - ig)
        d_zr_bf = d_zr.astype(BF16)
        d_zi_bf = d_zi.astype(BF16)
        d_xc = d_xc + _gate_matmul_t(d_zr_bf, wa_ref) + _gate_matmul_t(d_zi_bf, wx_ref)
        for blk in range(2):
            cols = slice(256 * blk, 256 * (blk + 1))
            gw_ref[0, blk] = gw_ref[0, blk] + _dot_tn(xc_bf[:, cols], d_zr_bf[:, cols])
            gw_ref[1, blk] = gw_ref[1, blk] + _dot_tn(xc_bf[:, cols], d_zi_bf[:, cols])

        dxe[0:tm, :] = d_xc
        shifted = [dxe[pl.ds(3 - k, tm), :] for k in range(CONV_WIDTH)]
        dxr = sum(cw_ref[k : k + 1, :] * shifted[k] for k in range(CONV_WIDTH))
        dxr_ref[...] = dxr.astype(BF16)
        dxe[tm:, :] = d_xc[0:8, :]

        lam = lam_ref[...]
        xr_tile = xr_ref[...]
        sums = [
            jnp.sum(d_zr, axis=0, keepdims=True),
            jnp.sum(d_zi, axis=0, keepdims=True),
            d_sp * (-_sigmoid(-lam)),
            jnp.sum(d_xc, axis=0, keepdims=True),
        ] + [jnp.sum(xr_tile * shifted[k], axis=0, keepdims=True) for k in range(CONV_WIDTH)]
        gvec_ref[...] = gvec_ref[...] + jnp.concatenate(sums, axis=0)

    return pl.pallas_call(
        body,
        name="rnn_bwd",
        grid=(2, n_tiles),
        in_specs=[
            sp["proj_tile"](SEG_XR // HALF), sp["before"](SEG_XR // HALF), sp["tile"], sp["before"](0), sp["tile"], sp["keep"],
            sp["gates"](0), sp["gates"](1), sp["vec"], sp["vec"], sp["vec"], sp["conv"], sp["vec"], pl.BlockSpec(memory_space=pl.ANY),
        ],
        out_specs=[sp["tile"], sp["gate_grads"], pl.BlockSpec((8, HALF), lambda ch, i: (0, ch)), pl.BlockSpec(memory_space=pl.ANY)],
        out_shape=[
            jax.ShapeDtypeStruct((t, D), BF16),
            jax.ShapeDtypeStruct((2, 4, 256, 256), F32),
            jax.ShapeDtypeStruct((8, D), F32),
            jax.ShapeDtypeStruct((len(CHIP_FLIPS), 3, ROW_SHARD, D), BF16),
        ],
        scratch_shapes=[
            pltpu.VMEM((tm + 8, HALF), F32),
            pltpu.VMEM((tm + 8, HALF), F32),
            pltpu.VMEM((tm + 8, HALF), F32),
            pltpu.VMEM((tm, HALF), F32),
            pltpu.VMEM((tm, HALF), F32),
            pltpu.VMEM((tm + 8, HALF), F32),
            pltpu.VMEM((8, HALF), F32),
            pltpu.VMEM((8, HALF), F32),
            pltpu.SemaphoreType.DMA((len(CHIP_FLIPS),)),
            pltpu.SemaphoreType.DMA((len(CHIP_FLIPS),)),
        ],
        compiler_params=_params(2),
    )(proj, proj, hh, hh, d_hh, keep, w_all, w_all, rg_ba, rg_bx, rg_lambda, conv_w, conv_b, sum_3_bf)


def _mid(x, target, o, hh, proj, w_all, gate, final_g):
    t = x.shape[0]
    tm = min(256, t)
    n_tiles = t // tm
    full = pl.BlockSpec((tm, D), lambda i: (i, 0))
    half = lambda col: pl.BlockSpec((tm, 512), lambda i: (i, col))
    vec = pl.BlockSpec((1, D), lambda i: (0, 0))
    any_spec = pl.BlockSpec(memory_space=pl.ANY)

    def body(x_ref, tg_ref, o_ref, hh_ref, ga0, ga1, gr0, gr1, ma0, ma1, mr0, mr1, w_hbm, gate_ref, fg_ref,
             dres_ref, do_ref, dhh_ref, dga_ref, dgr_ref, dma_ref, dmr_ref, gw_hbm, vec_ref, w, gw, sem, w_sems):
        i = pl.program_id(0)

        @pl.when(i == 0)
        def _():
            loads = [
                pltpu.make_async_copy(
                    w_hbm.at[s, pl.ds(ROW_SHARD * k, ROW_SHARD), :], w.at[k, pl.ds(ROW_SHARD * s, ROW_SHARD), :], w_sems.at[k * N_DEV + s]
                )
                for k in range(3)
                for s in range(N_DEV)
            ]
            for cp in loads:
                cp.start()
            for cp in loads:
                cp.wait()
            gw[...] = jnp.zeros_like(gw)
            vec_ref[...] = jnp.zeros_like(vec_ref)

        cat = lambda a, b: jnp.concatenate([a[...], b[...]], axis=1)
        ga, gr, ma, mr = cat(ga0, ga1), cat(gr0, gr1), cat(ma0, ma1), cat(mr0, mr1)
        ov, hv, gate_v, fg = o_ref[...].astype(F32), hh_ref[...], gate_ref[...], fg_ref[...]
        sg_a = _sigmoid(ga)
        sg_r = _sigmoid(gr)
        silu_a = ga * sg_a
        silu_r = gr * sg_r
        ya = (ov * silu_a).astype(BF16)
        yr = (hv * silu_r).astype(BF16)
        pa = _dot(ya, w[0])
        pr = _dot(yr, w[1])
        sa = _sigmoid(ma)
        sr = _sigmoid(mr)
        merged = (sa * pa + sr * pr).astype(BF16)
        mo = _dot(merged, w[2])
        out = x_ref[...] + gate_v * mo
        r2 = lax.rsqrt(jnp.mean(out * out, axis=-1, keepdims=True) + NORM_EPS)
        outn = out * r2
        err = outn * fg - tg_ref[...]
        loss = 0.5 * jnp.sum(jnp.mean(err * err, axis=-1, keepdims=True), axis=0, keepdims=True)
        dy = err * (1.0 / D)
        g_final = jnp.sum(dy * outn, axis=0, keepdims=True)
        doutn = dy * fg
        dout = r2 * (doutn - outn * jnp.mean(doutn * outn, axis=-1, keepdims=True))
        dres_ref[...] = dout
        d_gate = jnp.sum(dout * mo, axis=0, keepdims=True)
        dmo = (dout * gate_v).astype(BF16)
        dmerged = _dot_nt(dmo, w[2])
        gw[2] = gw[2] + _dot_tn(merged, dmo)
        dpa = (dmerged * sa).astype(BF16)
        dpr = (dmerged * sr).astype(BF16)
        dma_ref[...] = (dmerged * pa * (sa * (1.0 - sa))).astype(BF16)
        dmr_ref[...] = (dmerged * pr * (sr * (1.0 - sr))).astype(BF16)
        dya = _dot_nt(dpa, w[0])
        dyr = _dot_nt(dpr, w[1])
        gw[0] = gw[0] + _dot_tn(ya, dpa)
        gw[1] = gw[1] + _dot_tn(yr, dpr)
        do_ref[...] = (dya * silu_a).astype(BF16)
        dhh_ref[...] = dyr * silu_r
        dga_ref[...] = (dya * ov * (sg_a * (1.0 + ga * (1.0 - sg_a)))).astype(BF16)
        dgr_ref[...] = (dyr * hv * (sg_r * (1.0 + gr * (1.0 - sg_r)))).astype(BF16)
        zeros = jnp.zeros((5, D), F32)
        vec_ref[...] = vec_ref[...] + jnp.concatenate([g_final, d_gate, jnp.broadcast_to(loss, (1, D)), zeros], axis=0)

        @pl.when(i == n_tiles - 1)
        def _():
            cp = pltpu.make_async_copy(gw, gw_hbm, sem)
            cp.start()
            cp.wait()

    bf_tile = jax.ShapeDtypeStruct((t, D), BF16)
    f_tile = jax.ShapeDtypeStruct((t, D), F32)
    return pl.pallas_call(
        body,
        name="mid",
        grid=(n_tiles,),
        in_specs=[full, full, full, full,
                  half(SEG_GA // 512), half(SEG_GA // 512 + 1), half(SEG_GR // 512), half(SEG_GR // 512 + 1),
                  half(SEG_MA // 512), half(SEG_MA // 512 + 1), half(SEG_MR // 512), half(SEG_MR // 512 + 1),
                  any_spec, pl.BlockSpec((1, D), lambda i: (0, 2)), vec],
        out_specs=[full, full, full, full, full, full, full, any_spec, pl.BlockSpec((8, D), lambda i: (0, 0))],
        out_shape=[f_tile, bf_tile, f_tile, bf_tile, bf_tile, bf_tile, bf_tile,
                   jax.ShapeDtypeStruct((3, D, D), F32), jax.ShapeDtypeStruct((8, D), F32)],
        scratch_shapes=[pltpu.VMEM((3, D, D), BF16), pltpu.VMEM((3, D, D), F32), pltpu.SemaphoreType.DMA, pltpu.SemaphoreType.DMA((3 * N_DEV,))],
        compiler_params=_params(1),
    )(x, target, o, hh, proj, proj, proj, proj, proj, proj, proj, proj, w_all, gate, final_g)


_SEGMENTS = ((SEG_Q, 1024), (SEG_K, 256), (SEG_V, 256), (SEG_GA, 1024), (SEG_XR, 1024), (SEG_GR, 1024), (SEG_MA, 1024), (SEG_MR, 1024))


def _seg_specs(tm):
    return [pl.BlockSpec((tm, width), lambda i: (i, 0)) for _, width in _SEGMENTS]


def _bwd_dh(dsegs, w_all, x, dres, mod, norm_g, token):
    t = x.shape[0]
    tm = min(512, t)
    n_tiles = t // tm
    n_seg = len(_SEGMENTS)
    full = pl.BlockSpec((tm, D), lambda i: (i, 0))

    def body(*refs):
        segs = refs[:n_seg]
        w_hbm, x_ref, dres_ref, mod_ref, ng_ref, token_ref, gx_ref, vec_ref, w, sems = refs[n_seg:]
        i = pl.program_id(0)

        @pl.when(i == 0)
        def _():
            loads = [
                pltpu.make_async_copy(w_hbm.at[s], w.at[pl.ds(W_IN_SHARD * s, W_IN_SHARD), :], sems.at[s])
                for s in range(N_DEV)
            ]
            for cp in loads:
                cp.start()
            for cp in loads:
                cp.wait()
            vec_ref[...] = jnp.zeros_like(vec_ref)

        dh = sum(_dot(seg[...], w[start : start + width, :]) for seg, (start, width) in zip(segs, _SEGMENTS))
        xv = x_ref[...]
        r1 = lax.rsqrt(jnp.mean(xv * xv, axis=-1, keepdims=True) + NORM_EPS)
        xn = xv * r1
        ng = ng_ref[...]
        scale1 = 1.0 + mod_ref[:, D : 2 * D]
        d_shift = jnp.sum(dh, axis=0, keepdims=True)
        d_scale = jnp.sum(dh * (xn * ng), axis=0, keepdims=True)
        g_norm = jnp.sum(dh * xn * scale1, axis=0, keepdims=True)
        dxn = dh * (ng * scale1)
        dx = r1 * (dxn - xn * jnp.mean(dxn * xn, axis=-1, keepdims=True))
        gx_ref[...] = dres_ref[...] + dx
        vec_ref[...] = vec_ref[...] + jnp.concatenate([d_shift, d_scale, g_norm, jnp.zeros((5, D), F32)], axis=0)

    any_spec = pl.BlockSpec(memory_space=pl.ANY)
    return pl.pallas_call(
        body,
        name="bwd_dh",
        grid=(n_tiles,),
        in_specs=_seg_specs(tm) + [any_spec, full, full, pl.BlockSpec((1, 3 * D), lambda i: (0, 0)), pl.BlockSpec((1, D), lambda i: (0, 0)),
                                   pl.BlockSpec(token.shape, lambda i: (0, 0))],
        out_specs=[full, pl.BlockSpec((8, D), lambda i: (0, 0))],
        out_shape=[jax.ShapeDtypeStruct((t, D), F32), jax.ShapeDtypeStruct((8, D), F32)],
        scratch_shapes=[pltpu.VMEM((IN_W, D), BF16), pltpu.SemaphoreType.DMA((N_DEV,))],
        compiler_params=_params(1),
    )(*dsegs, w_all, x, dres, mod, norm_g, token)


_HBM = pl.BlockSpec(memory_space=pltpu.HBM)
_SEM = pl.BlockSpec(memory_space=pltpu.SEMAPHORE)
_EFFECT = pltpu.SideEffectType.DATAFLOW_SIDE_EFFECTING


def _w_exchange_copies(sum_ref, land_ref, sems):
    me = _position()
    n = len(CHIP_FLIPS)
    remote = [
        pltpu.make_async_remote_copy(
            src_ref=sum_ref.at[_chip_of(_flip(me, k))], dst_ref=land_ref.at[j], send_sem=sems[j], recv_sem=sems[n + j],
            device_id=_flip(me, k), device_id_type=MESH,
        )
        for j, k in enumerate(CHIP_FLIPS)
    ]
    return remote, []


def _vec_exchange_copies(vec_ref, land_ref, sems):
    me = _position()
    n = N_DEV - 1
    remote = [
        pltpu.make_async_remote_copy(
            src_ref=vec_ref, dst_ref=land_ref.at[_slot(me)], send_sem=sems[k - 1], recv_sem=sems[n + k - 1],
            device_id=_flip(me, k), device_id_type=MESH,
        )
        for k in range(1, N_DEV)
    ]
    return remote, [pltpu.make_async_copy(vec_ref, land_ref.at[_slot(me)], sems[2 * n])]


def _exchange_start(name, copies, n_sems, src, land_shape):
    land = lax.empty(land_shape, src.dtype)

    def body(src_ref, land_ref, *outs):
        remote, local = copies(src_ref, land_ref, outs[:n_sems])
        for cp in remote + local:
            cp.start()
        token = outs[n_sems + 2]
        token[...] = jnp.zeros_like(token)

    res = pl.pallas_call(
        body,
        name=name,
        out_shape=(*[pltpu.SemaphoreType.DMA(())] * n_sems, pltpu.HBM(src.shape, src.dtype), pltpu.HBM(land.shape, src.dtype),
                   jax.ShapeDtypeStruct((8, 128), F32)),
        in_specs=(_HBM, _HBM),
        out_specs=(*[_SEM] * n_sems, _HBM, _HBM, pl.BlockSpec(memory_space=pltpu.VMEM)),
        input_output_aliases={0: n_sems, 1: n_sems + 1},
        compiler_params=pltpu.CompilerParams(has_side_effects=_EFFECT),
    )(pltpu.with_memory_space_constraint(src, pltpu.HBM), pltpu.with_memory_space_constraint(land, pltpu.HBM))
    return res[:n_sems], res[n_sems], res[n_sems + 1], res[n_sems + 2]


def _exchange_wait(name, copies, sems, src_thru, land_thru, after):
    n_sems = len(sems)

    def body(src_ref, land_ref, *rest):
        remote, local = copies(src_ref, land_ref, rest[:n_sems])
        for cp in remote:
            cp.wait_send()
            cp.wait_recv()
        for cp in local:
            cp.wait()

    return pl.pallas_call(
        body,
        name=name,
        out_shape=(pltpu.HBM(src_thru.shape, src_thru.dtype), pltpu.HBM(land_thru.shape, land_thru.dtype)),
        in_specs=(_HBM, _HBM, *[_SEM] * n_sems, *[pl.BlockSpec(memory_space=pl.ANY)] * len(after)),
        out_specs=(_HBM, _HBM),
        input_output_aliases={0: 0, 1: 1},
        compiler_params=pltpu.CompilerParams(has_side_effects=_EFFECT),
    )(src_thru, land_thru, *sems, *after)[1]


def _bwd_gw(dsegs, h, g_gates):
    t = h.shape[0]
    tm = min(512, t)
    n_tiles = t // tm
    n_seg = len(_SEGMENTS)
    n_chips = N_DEV // 2

    def gate_copies(g_hbm, lg_hbm, send_sems, recv_sems, arriving):
        me = _position()
        return [
            pltpu.make_async_remote_copy(
                src_ref=g_hbm.at[:, :, _slot(_flip(me, k))],
                dst_ref=lg_hbm.at[_slot(_flip(me, k) if arriving else me)],
                send_sem=send_sems.at[k - 1],
                recv_sem=recv_sems.at[k - 1],
                device_id=_flip(me, k),
                device_id_type=MESH,
            )
            for k in range(1, N_DEV)
        ]

    def body(*refs):
        segs = refs[:n_seg]
        (h_ref, g_hbm, own_hbm, sum_hbm, lg_hbm, acc, stage, land, own_sems, pair_send, pair_recv, send_sems, recv_sems, g_local) = refs[n_seg:]
        i = pl.program_id(0)
        slot = _slot(_position())
        own_gates = pltpu.make_async_copy(g_hbm.at[:, :, slot], lg_hbm.at[slot], g_local)

        @pl.when(i == 0)
        def _():
            own_gates.start()
            for cp in gate_copies(g_hbm, lg_hbm, send_sems, recv_sems, arriving=False):
                cp.start()
            acc[...] = jnp.zeros_like(acc)

        hv = h_ref[...]
        for seg, (start, width) in zip(segs, _SEGMENTS):
            acc[start : start + width, :] = acc[start : start + width, :] + _dot_tn(seg[...], hv)

        @pl.when(i == n_tiles - 1)
        def _():
            me = _position()
            sibling = _flip(me, 1)

            def rows(q, c):
                return pl.ds(pl.multiple_of(W_IN_SHARD * (2 * q + c), 8), W_IN_SHARD)

            def to_sibling(q):
                return pltpu.make_async_remote_copy(
                    src_ref=stage.at[q % 2], dst_ref=land.at[q], send_sem=pair_send.at[q], recv_sem=pair_recv.at[q],
                    device_id=sibling, device_id_type=MESH,
                )

            for q in range(n_chips):
                if q >= 2:
                    to_sibling(q - 2).wait_send()
                stage[q % 2] = acc[rows(q, sibling[2]), :].astype(BF16)
                to_sibling(q).start()
            for q in range(n_chips - 2, n_chips):
                to_sibling(q).wait_send()
            sums = [pltpu.make_async_copy(stage.at[q % 2], sum_hbm.at[q], own_sems.at[q]) for q in range(n_chips)]
            for q in range(n_chips):
                to_sibling(q).wait_recv()
                if q >= 2:
                    sums[q - 2].wait()
                total = acc[rows(q, me[2]), :] + land[q].astype(F32)
                acc[rows(q, me[2]), :] = total
                stage[q % 2] = total.astype(BF16)
                sums[q].start()
            mine = pltpu.make_async_copy(acc.at[rows(_chip_of(me), me[2]), :], own_hbm, own_sems.at[n_chips])
            mine.start()
            for q in range(n_chips - 2, n_chips):
                sums[q].wait()
            mine.wait()
            for cp in gate_copies(g_hbm, lg_hbm, send_sems, recv_sems, arriving=True):
                cp.wait_recv()
            for cp in gate_copies(g_hbm, lg_hbm, send_sems, recv_sems, arriving=False):
                cp.wait_send()
            own_gates.wait()

    any_spec = pl.BlockSpec(memory_space=pl.ANY)
    n_copies = N_DEV - 1
    return pl.pallas_call(
        body,
        name="bwd_gw",
        grid=(n_tiles,),
        in_specs=_seg_specs(tm) + [pl.BlockSpec((tm, D), lambda i: (i, 0)), any_spec],
        out_specs=[any_spec, any_spec, any_spec],
        out_shape=[jax.ShapeDtypeStruct((W_IN_SHARD, D), F32),
                   jax.ShapeDtypeStruct((n_chips, W_IN_SHARD, D), BF16),
                   jax.ShapeDtypeStruct((N_DEV, 2, 4, GATE_SHARD, 256), F32)],
        scratch_shapes=[pltpu.VMEM((IN_W, D), F32), pltpu.VMEM((2, W_IN_SHARD, D), BF16), pltpu.VMEM((n_chips, W_IN_SHARD, D), BF16),
                        pltpu.SemaphoreType.DMA((n_chips + 1,)), pltpu.SemaphoreType.DMA((n_chips,)), pltpu.SemaphoreType.DMA((n_chips,)),
                        pltpu.SemaphoreType.DMA((n_copies,)), pltpu.SemaphoreType.DMA((n_copies,)), pltpu.SemaphoreType.DMA],
        compiler_params=_params(1),
    )(*dsegs, h, g_gates)


def _adamw(w, g, m, v):
    m = ADAM_B1 * m + (1.0 - ADAM_B1) * g
    v = ADAM_B2 * v + (1.0 - ADAM_B2) * (g * g)
    m_hat = m / (1.0 - ADAM_B1**ADAM_STEP)
    v_hat = v / (1.0 - ADAM_B2**ADAM_STEP)
    delta = -ADAM_LR * (m_hat / (jnp.sqrt(v_hat) + ADAM_EPS) + ADAM_WD * w)
    return delta, m, v


def _w_in_update(own, land, w_t, m_t, v_t, token):
    n_slots, rows, cols = land.shape
    rb = 208
    spec = pl.BlockSpec((rb, cols), lambda i: (i, 0))

    def body(o_ref, l_ref, w_ref, m_ref, v_ref, token_ref, g_ref, d_ref, m2_ref, v2_ref):
        total = o_ref[...]
        for s in range(n_slots):
            total = total + l_ref[s].astype(F32)
        g_ref[...] = total
        d_ref[...], m2_ref[...], v2_ref[...] = _adamw(w_ref[...], total, m_ref[...], v_ref[...])

    shape = jax.ShapeDtypeStruct((rows, cols), F32)
    return pl.pallas_call(
        body,
        name="w_in_update",
        grid=(rows // rb,),
        in_specs=[spec, pl.BlockSpec((n_slots, rb, cols), lambda i: (0, i, 0)), spec, spec, spec, pl.BlockSpec(token.shape, lambda i: (0, 0))],
        out_specs=[spec] * 4,
        out_shape=[shape] * 4,
        compiler_params=_params(1),
    )(own, land, w_t, m_t, v_t, token)


def _shard_update(own_3, land_3, land_g, weights, m_in, v_in, token):
    n = len(weights)

    def body(*refs):
        o3_ref, l3_ref, lg_ref = refs[:3]
        w_refs, m_refs, v_refs = (refs[3 + n * k : 3 + n * (k + 1)] for k in range(3))
        outs = refs[4 + 3 * n :]
        for k in range(n):
            if k < 3:
                g = o3_ref[k]
                for slot in range(3):
                    g = g + l3_ref[slot, k].astype(F32)
            else:
                g = lg_ref[0, k - 3]
                for slot in range(1, N_DEV):
                    g = g + lg_ref[slot, k - 3]
            d, m2, v2 = _adamw(w_refs[k][0], g, m_refs[k][0], v_refs[k][0])
            for which, val in enumerate((g, d, m2, v2)):
                outs[which * n + k][0] = val

    shapes = [jax.ShapeDtypeStruct(w.shape, F32) for w in weights]
    res = pl.pallas_call(
        body, name="shard_update", out_shape=shapes * 4, compiler_params=pltpu.CompilerParams(vmem_limit_bytes=VMEM_LIMIT)
    )(own_3, land_3, land_g, *weights, *m_in, *v_in, token)
    return [res[n * k : n * (k + 1)] for k in range(4)]


VEC_LAYOUT = (("b_ada", 3 * D), ("norm_g", D), ("attn_sinks", 128), ("conv_b", D), ("rg_ba", D), ("rg_bx", D), ("rg_lambda", D),
              ("final_g", D), ("conv_w", CONV_WIDTH * D), ("loss", 128))
VEC_OFFSET = {name: sum(w for _, w in VEC_LAYOUT[:k]) for k, (name, _) in enumerate(VEC_LAYOUT)}
VEC_ROWS = 112
VEC_WIDTH = VEC_ROWS * 128
assert sum(w for _, w in VEC_LAYOUT) <= VEC_WIDTH
N_REPLICATED = 8


def _small_update(vec_all, c_all_t, me, rep_w, rep_m, rep_v, ada, conv):
    n_ada = ada[0].shape[1]
    n_conv = conv[0].shape[1]

    def body(*refs):
        me_ref, vec_ref, ct_ref = refs[:3]
        w_refs, m_refs, v_refs = (refs[3 + N_REPLICATED * k : 3 + N_REPLICATED * (k + 1)] for k in range(3))
        pos = 3 + 3 * N_REPLICATED
        ada_refs, conv_refs = refs[pos : pos + 3], refs[pos + 3 : pos + 6]
        outs = refs[pos + 6 :]
        loss_ref, rep_outs = outs[0], outs[1 : 1 + 4 * N_REPLICATED]
        ada_outs, conv_outs = outs[1 + 4 * N_REPLICATED : 5 + 4 * N_REPLICATED], outs[5 + 4 * N_REPLICATED : 9 + 4 * N_REPLICATED]
        conv_buf = outs[9 + 4 * N_REPLICATED]

        total = vec_ref[0:1, :]
        for slot in range(1, N_DEV):
            total = total + vec_ref[slot : slot + 1, :]
        loss_ref[...] = total[:, VEC_OFFSET["loss"] : VEC_OFFSET["loss"] + 1]
        for k, (name, _) in enumerate(VEC_LAYOUT[:N_REPLICATED]):
            width = w_refs[k].shape[1]
            g = total[:, VEC_OFFSET[name] : VEC_OFFSET[name] + width]
            d, m2, v2 = _adamw(w_refs[k][...], g, m_refs[k][...], v_refs[k][...])
            for which, val in enumerate((g, d, m2, v2)):
                rep_outs[which * N_REPLICATED + k][...] = val

        device = me_ref[0]
        for k in range(CONV_WIDTH):
            conv_buf[k : k + 1, :] = total[:, VEC_OFFSET["conv_w"] + D * k : VEC_OFFSET["conv_w"] + D * (k + 1)]
        g_conv = conv_buf[:, pl.ds(pl.multiple_of(device * n_conv, 128), n_conv)]
        for ref, val in zip(conv_outs, (g_conv,) + _adamw(conv_refs[0][...], g_conv, conv_refs[1][...], conv_refs[2][...])):
            ref[...] = val

        dmod = vec_ref[:, pl.ds(pl.multiple_of(device * n_ada, 128), n_ada)]
        g_ada = _dot(ct_ref[...], dmod)
        for ref, val in zip(ada_outs, (g_ada,) + _adamw(ada_refs[0][...], g_ada, ada_refs[1][...], ada_refs[2][...])):
            ref[...] = val

    vmem = pl.BlockSpec(memory_space=pltpu.VMEM)
    rep_shapes = [jax.ShapeDtypeStruct(w.shape, F32) for w in rep_w]
    out_shape = ([jax.ShapeDtypeStruct((1, 1), F32)] + rep_shapes * 4 + [jax.ShapeDtypeStruct(ada[0].shape, F32)] * 4
                 + [jax.ShapeDtypeStruct(conv[0].shape, F32)] * 4)
    n_in = 3 + 3 * N_REPLICATED + 6
    res = pl.pallas_call(
        body,
        name="small_update",
        in_specs=[pl.BlockSpec(memory_space=pltpu.SMEM)] + [vmem] * (n_in - 1),
        out_specs=[vmem] * len(out_shape),
        out_shape=out_shape,
        scratch_shapes=[pltpu.VMEM((CONV_WIDTH, D), F32)],
        compiler_params=pltpu.CompilerParams(vmem_limit_bytes=VMEM_LIMIT),
    )(me, vec_all, c_all_t, *rep_w, *rep_m, *rep_v, *ada, *conv)
    loss = res[0]
    reps = [res[1 + N_REPLICATED * k : 1 + N_REPLICATED * (k + 1)] for k in range(4)]
    return loss, reps, res[1 + 4 * N_REPLICATED : 5 + 4 * N_REPLICATED], res[5 + 4 * N_REPLICATED : 9 + 4 * N_REPLICATED]


def kernel(x, c, positions, w_ada, b_ada, norm_g, w_in, attn_sinks, conv_w, conv_b, rg_wa, rg_ba, rg_wx, rg_bx, rg_lambda, w_attn_proj, w_rnn_proj, w_out, final_g, loss_target, m_w_ada, m_b_ada, m_norm_g, m_w_in, m_attn_sinks, m_conv_w, m_conv_b, m_rg_wa, m_rg_ba, m_rg_wx, m_rg_bx, m_rg_lambda, m_w_attn_proj, m_w_rnn_proj, m_w_out, m_final_g, v_w_ada, v_b_ada, v_norm_g, v_w_in, v_attn_sinks, v_conv_w, v_conv_b, v_rg_wa, v_rg_ba, v_rg_wx, v_rg_bx, v_rg_lambda, v_w_attn_proj, v_w_rnn_proj, v_w_out, v_final_g):
    my_x, my_y, my_c = _position()
    me = _slot((my_x, my_y, my_c))
    x2, target = x[0], loss_target[0]
    pos = positions[0]

    gate_rows = lambda w: w[0].transpose(1, 0, 2).reshape(GATE_SHARD, D)
    w_pack = jnp.concatenate([w_attn_proj[0], w_rnn_proj[0], w_out[0], gate_rows(rg_wa), gate_rows(rg_wx)], axis=0).astype(BF16)
    conv_rows = jnp.pad(conv_w[0], ((0, 8 - CONV_WIDTH), (0, 0)))
    w_in_t = w_in[0].T

    tables = _rope_tables(pos)
    where = jnp.stack([_chip_of((my_x, my_y)), me]).astype(jnp.int32)
    proj, h, mod, lane_tables, w_t_all, w_all, c_rows, conv_all = _fwd_in(
        x2, jnp.broadcast_to(c, (8, D)), conv_rows, w_ada[0], b_ada, norm_g, w_in_t.astype(BF16), tables, w_pack, where
    )
    c_all = c_rows[:, 0, :]
    conv_full = conv_all[:, :CONV_WIDTH, :].transpose(1, 0, 2).reshape(CONV_WIDTH, D)
    o = _attn_fwd(proj, attn_sinks)
    keep = (pos != 0).astype(F32)[:, None]
    hh = _rnn_fwd(proj, keep, w_all, rg_ba, rg_bx, rg_lambda, conv_full, conv_b)

    dres, d_o, d_hh, d_ga, d_gr, d_ma, d_mr, gw3, mid_vec = _mid(x2, target, o, hh, proj, w_all, mod, final_g[None, :])
    part_3 = gw3.reshape(3, 4, 2, ROW_SHARD, D)
    d_q, d_k, d_v, d_sink, sum_3_bf, own_3 = _attn_bwd(proj, o, d_o, (lane_tables, tables), attn_sinks, part_3)
    d_xr, g_gates, rnn_vec, land_3 = _rnn_bwd(proj, hh, d_hh, keep, w_all, rg_ba, rg_bx, rg_lambda, conv_full, conv_b, sum_3_bf)
    dsegs = (d_q, d_k, d_v, d_ga, d_xr, d_gr, d_ma, d_mr)
    own_w, sum_w_bf, land_g = _bwd_gw(dsegs, h, g_gates.reshape(2, 4, N_DEV, GATE_SHARD, 256))
    w_sems, sum_w_thru, land_w_thru, token = _exchange_start(
        "w_exchange_start", _w_exchange_copies, 2 * len(CHIP_FLIPS), sum_w_bf, (len(CHIP_FLIPS), W_IN_SHARD, D)
    )
    grad_x, dh_vec = _bwd_dh(dsegs, w_t_all, x2, dres, mod, norm_g, token)
    land_w = _exchange_wait("w_exchange_wait", _w_exchange_copies, w_sems, sum_w_thru, land_w_thru, (dh_vec,))

    pieces = dict(
        b_ada=jnp.concatenate([dh_vec[0:1], dh_vec[1:2], mid_vec[1:2]], axis=1),
        norm_g=dh_vec[2:3],
        attn_sinks=jnp.pad(d_sink[:, :, 0].reshape(1, 16), ((0, 0), (0, 112))),
        conv_b=rnn_vec[3:4],
        rg_ba=rnn_vec[0:1],
        rg_bx=rnn_vec[1:2],
        rg_lambda=rnn_vec[2:3],
        final_g=mid_vec[0:1],
        conv_w=rnn_vec[4:8].reshape(1, CONV_WIDTH * D),
        loss=mid_vec[2:3, 0:128],
    )
    vec = jnp.concatenate([pieces[name] for name, _ in VEC_LAYOUT], axis=1)
    vec = jnp.pad(vec, ((0, 0), (0, VEC_WIDTH - vec.shape[1]))).reshape(VEC_ROWS, 128)
    vec_sems, vec_thru, vec_land, vec_token = _exchange_start(
        "vec_exchange_start", _vec_exchange_copies, 2 * N_DEV - 1, vec, (N_DEV, VEC_ROWS, 128)
    )
    w_in_res = _w_in_update(own_w, land_w, w_in_t, m_w_in[0].T, v_w_in[0].T, vec_token)
    shard_res = _shard_update(
        own_3, land_3, land_g,
        (w_attn_proj, w_rnn_proj, w_out, rg_wa, rg_wx),
        (m_w_attn_proj, m_w_rnn_proj, m_w_out, m_rg_wa, m_rg_wx),
        (v_w_attn_proj, v_w_rnn_proj, v_w_out, v_rg_wa, v_rg_wx),
        vec_token,
    )
    vec_all = _exchange_wait(
        "vec_exchange_wait", _vec_exchange_copies, vec_sems, vec_thru, vec_land, (w_in_res[0], shard_res[0][0])
    )
    rep_w = [b_ada, norm_g, attn_sinks, conv_b, rg_ba, rg_bx, rg_lambda, final_g[None, :]]
    rep_m = [m_b_ada, m_norm_g, m_attn_sinks, m_conv_b, m_rg_ba, m_rg_bx, m_rg_lambda, m_final_g[None, :]]
    rep_v = [v_b_ada, v_norm_g, v_attn_sinks, v_conv_b, v_rg_ba, v_rg_bx, v_rg_lambda, v_final_g[None, :]]
    loss, reps, ada_res, conv_res = _small_update(
        vec_all.reshape(N_DEV, VEC_WIDTH), c_all.T, me.astype(jnp.int32).reshape(1), rep_w, rep_m, rep_v,
        (w_ada[0], m_w_ada[0], v_w_ada[0]), (conv_w[0], m_conv_w[0], v_conv_w[0]),
    )

    def by_weight(k):
        b, n, s, cb, ba, bx, lam, fg = reps[k]
        ap, rp, out, wa, wx = shard_res[k]
        return [ada_res[k][None], b, n, w_in_res[k].T[None], s, conv_res[k][None], cb, wa, ba, wx, bx, lam, ap, rp, out, fg[0]]

    return (loss[0, 0], grad_x[None], *by_weight(0), *by_weight(1), *by_weight(2), *by_weight(3))
```

```python
import math

import jax
import jax.numpy as jnp
from jax import lax
from jax.experimental import pallas as pl
from jax.experimental.pallas import tpu as pltpu

F32 = jnp.float32
BF16 = jnp.bfloat16

D = 1024
HEAD_DIM = 64
ROT_DIM = 16
ROPE_THETA = 500000.0
WINDOW = 128
LRU_C = 8.0
CONV_WIDTH = 4
NORM_EPS = 1e-6
IN_W = 6656
SEG_Q, SEG_K, SEG_V, SEG_GA, SEG_XR, SEG_GR, SEG_MA, SEG_MR = 0, 1024, 1280, 1536, 2560, 3584, 4608, 5632
N_DEV = 8
W_IN_SHARD = IN_W // N_DEV
ROW_SHARD = D // N_DEV
GATE_SHARD = 256 // N_DEV

ADAM_LR = 0.001
ADAM_B1 = 0.9
ADAM_B2 = 0.999
ADAM_EPS = 1e-08
ADAM_WD = 0.01
ADAM_STEP = 10

VMEM_LIMIT = 60 * 1024 * 1024
MESH = pl.DeviceIdType.MESH


def _dot(a, b):
    return jnp.dot(a, b, preferred_element_type=F32)


def _dot_nt(a, b):
    return lax.dot_general(a, b, (((1,), (1,)), ((), ())), preferred_element_type=F32)


def _dot_tn(a, b):
    return lax.dot_general(a, b, (((0,), (0,)), ((), ())), preferred_element_type=F32)


def _sigmoid(z):
    return 1.0 / (1.0 + jnp.exp(-z))


def _params(n_grid, vmem=VMEM_LIMIT):
    return pltpu.CompilerParams(dimension_semantics=("arbitrary",) * n_grid, vmem_limit_bytes=vmem)


def _position():
    return lax.axis_index("x"), lax.axis_index("y"), lax.axis_index("c")


def _flip(pos, k):
    x, y, c = pos
    return (1 - x if k & 4 else x, 1 - y if k & 2 else y, 1 - c if k & 1 else c)


def _slot(pos):
    return 4 * pos[0] + 2 * pos[1] + pos[2]


def _direct_gather_copies(src_hbm, all_hbm, send_sems, recv_sems, arriving):
    me = _position()
    return [
        pltpu.make_async_remote_copy(
            src_ref=src_hbm,
            dst_ref=all_hbm.at[_slot(_flip(me, k) if arriving else me)],
            send_sem=send_sems.at[k - 1],
            recv_sem=recv_sems.at[k - 1],
            device_id=_flip(me, k),
            device_id_type=MESH,
        )
        for k in range(1, N_DEV)
    ]


CHIP_FLIPS = (2, 4, 6)


def _chip_of(pos):
    return 2 * pos[0] + pos[1]


def _chip_copies(srcs, lands, views, send_sems, recv_sems):
    me = _position()
    copies = []
    for a, (src, land) in enumerate(zip(srcs, lands)):
        for j, k in enumerate(CHIP_FLIPS):
            peer = _flip(me, k)
            copies.append(
                pltpu.make_async_remote_copy(
                    src_ref=views[a](src, peer),
                    dst_ref=land.at[j],
                    send_sem=send_sems.at[len(CHIP_FLIPS) * a + j],
                    recv_sem=recv_sems.at[len(CHIP_FLIPS) * a + j],
                    device_id=peer,
                    device_id_type=MESH,
                )
            )
    return copies


PACK_GATE = 3 * ROW_SHARD
PACK_ROWS = PACK_GATE + 2 * GATE_SHARD


def _fwd_in(x, c_rows, conv_rows, w_ada, b_cols, norm_g, w_shard, tables, pack, where):
    t = x.shape[0]
    tm = min(1024, t)
    tn = IN_W // 4
    n_i, n_j = t // tm, IN_W // tn
    n_rope = (SEG_V - SEG_Q) // 128

    n_ada = w_ada.shape[1]

    def body(where_ref, x_ref, wada_ref, bcols_ref, ng_ref, ct_ref, st_ref, shard_hbm, pack_hbm, c_hbm, conv_hbm,
             proj_ref, h_ref, mod_ref, lane_hbm, wt_hbm, all_hbm, call_hbm, convall_hbm,
             wall, hbuf, cbuf, mcols, mod_all, mod_buf, lane,
             w_send, w_recv, load_sems, local_sem, lane_sem, p_send, p_recv, p_local,
             c_send, c_recv, c_local, v_send, v_recv, v_local, m_send, m_recv):
        j = pl.program_id(0)
        i = pl.program_id(1)
        me = _position()
        sibling = _flip(me, 1)
        chip = where_ref[0]
        slot = where_ref[1]

        def modulation():
            own_c = pltpu.make_async_copy(c_hbm, call_hbm.at[_slot(me)], c_local)
            own_v = pltpu.make_async_copy(conv_hbm, convall_hbm.at[_slot(me)], v_local)
            own_c.start()
            own_v.start()
            for cp in _direct_gather_copies(c_hbm, call_hbm, c_send, c_recv, arriving=False):
                cp.start()
            for cp in _direct_gather_copies(conv_hbm, convall_hbm, v_send, v_recv, arriving=False):
                cp.start()
            for cp in _direct_gather_copies(c_hbm, call_hbm, c_send, c_recv, arriving=True):
                cp.wait_recv()
            own_c.wait()
            fetch = pltpu.make_async_copy(call_hbm, cbuf, c_local)
            fetch.start()
            fetch.wait()
            mcols[...] = _dot(cbuf[:, 0, :].astype(BF16), wada_ref[...].astype(BF16)) + bcols_ref[...]
            mod_all[slot] = mcols[...]
            swaps = lambda arriving: [
                pltpu.make_async_remote_copy(
                    src_ref=mcols,
                    dst_ref=mod_all.at[_slot(_flip(me, k) if arriving else me)],
                    send_sem=m_send.at[k - 1],
                    recv_sem=m_recv.at[k - 1],
                    device_id=_flip(me, k),
                    device_id_type=MESH,
                )
                for k in range(1, N_DEV)
            ]
            for cp in swaps(False):
                cp.start()
            for cp in swaps(True):
                cp.wait_recv()
            mine = lax.broadcasted_iota(jnp.int32, (N_DEV, n_ada), 0) == slot
            for s in range(N_DEV):
                row = jnp.sum(jnp.where(mine, mod_all[s], 0.0), axis=0, keepdims=True)
                mod_buf[:, n_ada * s : n_ada * (s + 1)] = row
            for cp in swaps(False):
                cp.wait_send()
            for cp in _direct_gather_copies(c_hbm, call_hbm, c_send, c_recv, arriving=False):
                cp.wait_send()
            for cp in _direct_gather_copies(conv_hbm, convall_hbm, v_send, v_recv, arriving=True):
                cp.wait_recv()
            for cp in _direct_gather_copies(conv_hbm, convall_hbm, v_send, v_recv, arriving=False):
                cp.wait_send()
            own_v.wait()

        def shard_of(block):
            return wall.at[_chip_of(block), pl.ds(W_IN_SHARD * block[2], W_IN_SHARD), :]

        def shard_copy(k, block, to, src=None):
            return pltpu.make_async_remote_copy(
                src_ref=shard_of(block),
                dst_ref=shard_of(block),
                send_sem=w_send.at[k],
                recv_sem=w_recv.at[k],
                device_id=to,
                device_id_type=MESH,
            )

        own_shard = pltpu.make_async_copy(shard_hbm, shard_of(me), local_sem)
        own_pack = pltpu.make_async_copy(pack_hbm, all_hbm.at[_slot(me)], p_local)
        keep_lane = pltpu.make_async_copy(lane, lane_hbm, lane_sem)

        def lane_tables():
            d = lax.broadcasted_iota(jnp.int32, (128, 128), 0) % HEAD_DIM
            half = ROT_DIM // 2

            def fill(b, carry):
                cols = pl.ds(pl.multiple_of(128 * b, 128), 128)
                cos = jnp.concatenate([ct_ref[:, cols]] * (128 // half), axis=0)
                sin = jnp.concatenate([st_ref[:, cols]] * (128 // half), axis=0)
                lane[0, cols, :] = jnp.where(d < ROT_DIM, cos, 1.0).T
                lane[1, cols, :] = jnp.where(d < half, -sin, 0.0).T
                lane[2, cols, :] = jnp.where((d >= half) & (d < ROT_DIM), sin, 0.0).T
                return carry

            lax.fori_loop(0, t // 128, fill, 0)

        def keep_tile(q):
            return [
                pltpu.make_async_copy(wall.at[q, pl.ds(W_IN_SHARD * c, W_IN_SHARD), :], wt_hbm.at[2 * q + c], load_sems.at[2 * (q ^ chip) + c])
                for c in range(2)
            ]

        @pl.when((j == 0) & (i == 0))
        def _():
            modulation()
            mod_ref[...] = mod_buf[...]
            own_shard.start()
            own_shard.wait()
            shard_copy(0, me, sibling).start()
            for rel in range(1, 4):
                shard_copy(rel, me, _flip(me, 2 * rel)).start()
            own_pack.start()
            for cp in _direct_gather_copies(pack_hbm, all_hbm, p_send, p_recv, arriving=False):
                cp.start()
            lane_tables()
            keep_lane.start()
            shard_copy(0, sibling, me).wait_recv()
            for cp in keep_tile(chip):
                cp.start()

        for rel in range(1, 4):

            @pl.when((j == 1) & (i == min(rel, n_i) - 1))
            def _(rel=rel):
                there = _flip(me, 2 * rel)
                shard_copy(rel, there, me).wait_recv()
                shard_copy(3 + rel, there, sibling).start()

            @pl.when((j == rel) & (i == 0))
            def _(rel=rel):
                shard_copy(3 + rel, _flip(sibling, 2 * rel), me).wait_recv()
                for cp in keep_tile(chip ^ rel):
                    cp.start()

        rows = pl.ds(pl.multiple_of(i * tm, tm), tm)

        @pl.when(j == 0)
        def _():
            xv = x_ref[...]
            r = lax.rsqrt(jnp.mean(xv * xv, axis=-1, keepdims=True) + NORM_EPS)
            shift = mod_buf[:, 0:D]
            scale = mod_buf[:, D : 2 * D]
            hv = (((xv * r) * ng_ref[...]) * (1.0 + scale) + shift).astype(BF16)
            hbuf[rows, :] = hv
            h_ref[...] = hv

        proj_ref[...] = _dot_nt(hbuf[rows, :], wall[chip ^ j])

        @pl.when((chip ^ j) == 0)
        def _():
            cos, s1, s2 = lane[0, rows, :], lane[1, rows, :], lane[2, rows, :]
            for g in range(n_rope):
                cols = slice(128 * g, 128 * (g + 1))
                proj_ref[:, cols] = _rope(proj_ref[:, cols], cos, s1, s2)

        @pl.when((j == n_j - 1) & (i == n_i - 1))
        def _():
            shard_copy(0, me, sibling).wait_send()
            for rel in range(1, 4):
                shard_copy(rel, me, _flip(me, 2 * rel)).wait_send()
                shard_copy(3 + rel, _flip(me, 2 * rel), sibling).wait_send()
            for q in range(N_DEV // 2):
                for cp in keep_tile(q):
                    cp.wait()
            for cp in _direct_gather_copies(pack_hbm, all_hbm, p_send, p_recv, arriving=True):
                cp.wait_recv()
            for cp in _direct_gather_copies(pack_hbm, all_hbm, p_send, p_recv, arriving=False):
                cp.wait_send()
            own_pack.wait()
            keep_lane.wait()

    tab = pl.BlockSpec((ROT_DIM // 2, t), lambda j, i, c: (0, 0))
    any_spec = pl.BlockSpec(memory_space=pl.ANY)
    return pl.pallas_call(
        body,
        name="fwd_in",
        grid_spec=pltpu.PrefetchScalarGridSpec(
            num_scalar_prefetch=1,
            grid=(n_j, n_i),
            in_specs=[
                pl.BlockSpec((tm, D), lambda j, i, c: (jnp.where(j == 0, i, 0), 0)),
                pl.BlockSpec(w_ada.shape, lambda j, i, c: (0, 0)),
                pl.BlockSpec((1, n_ada), lambda j, i, c: (0, c[1])),
                pl.BlockSpec((1, D), lambda j, i, c: (0, 0)),
                tab, tab, any_spec, any_spec, any_spec, any_spec,
            ],
            out_specs=[pl.BlockSpec((tm, tn), lambda j, i, c: (i, c[0] ^ j)),
                       pl.BlockSpec((tm, D), lambda j, i, c: (jnp.where(j == 0, i, n_i - 1), 0)),
                       pl.BlockSpec((1, 3 * D), lambda j, i, c: (0, 0)),
                       any_spec, any_spec, any_spec, any_spec, any_spec],
            scratch_shapes=[
                pltpu.VMEM((N_DEV // 2, tn, D), BF16),
                pltpu.VMEM((t, D), BF16),
                pltpu.VMEM((N_DEV,) + c_rows.shape, F32),
                pltpu.VMEM((N_DEV, n_ada), F32),
                pltpu.VMEM((N_DEV, N_DEV, n_ada), F32),
                pltpu.VMEM((1, 3 * D), F32),
                pltpu.VMEM((3, t, 128), F32),
                pltpu.SemaphoreType.DMA((N_DEV - 1,)),
                pltpu.SemaphoreType.DMA((N_DEV - 1,)),
                pltpu.SemaphoreType.DMA((N_DEV,)),
                pltpu.SemaphoreType.DMA,
                pltpu.SemaphoreType.DMA,
                pltpu.SemaphoreType.DMA((N_DEV - 1,)),
                pltpu.SemaphoreType.DMA((N_DEV - 1,)),
                pltpu.SemaphoreType.DMA,
                pltpu.SemaphoreType.DMA((N_DEV - 1,)),
                pltpu.SemaphoreType.DMA((N_DEV - 1,)),
                pltpu.SemaphoreType.DMA,
                pltpu.SemaphoreType.DMA((N_DEV - 1,)),
                pltpu.SemaphoreType.DMA((N_DEV - 1,)),
                pltpu.SemaphoreType.DMA,
                pltpu.SemaphoreType.DMA((N_DEV - 1,)),
                pltpu.SemaphoreType.DMA((N_DEV - 1,)),
            ],
        ),
        out_shape=[jax.ShapeDtypeStruct((t, IN_W), F32), jax.ShapeDtypeStruct((t, D), BF16),
                   jax.ShapeDtypeStruct((1, 3 * D), F32),
                   jax.ShapeDtypeStruct((3, t, 128), F32),
                   jax.ShapeDtypeStruct((N_DEV,) + w_shard.shape, w_shard.dtype),
                   jax.ShapeDtypeStruct((N_DEV,) + pack.shape, pack.dtype),
                   jax.ShapeDtypeStruct((N_DEV,) + c_rows.shape, F32),
                   jax.ShapeDtypeStruct((N_DEV,) + conv_rows.shape, F32)],
        compiler_params=_params(2),
    )(where, x, w_ada, b_cols, norm_g, *tables, w_shard, pack, c_rows, conv_rows)


def _rope_tables(positions):
    inv_freq = ROPE_THETA ** (-jnp.arange(0, ROT_DIM, 2, dtype=F32) / ROT_DIM)
    ang_t = inv_freq[:, None] * positions.astype(F32)[None, :]
    return jnp.cos(ang_t), jnp.sin(ang_t)


def _rope(tv, cos, s1, s2):
    cols = []
    for g in range(tv.shape[1] // 128):
        blk = tv[:, 128 * g : 128 * (g + 1)]
        cols.append(blk * cos + pltpu.roll(blk, 128 - 8, 1) * s1 + pltpu.roll(blk, 8, 1) * s2)
    return cols[0] if len(cols) == 1 else jnp.concatenate(cols, axis=1)


def _rope_transposed(gv, cos, s1, s2):
    cols = []
    for g in range(gv.shape[1] // 128):
        blk = gv[:, 128 * g : 128 * (g + 1)]
        cols.append(blk * cos + pltpu.roll(blk * s1, 8, 1) + pltpu.roll(blk * s2, 128 - 8, 1))
    return cols[0] if len(cols) == 1 else jnp.concatenate(cols, axis=1)


def _lane_lo():
    return lax.broadcasted_iota(jnp.int32, (1, 128), 1) < HEAD_DIM


def _row_lo():
    return lax.broadcasted_iota(jnp.int32, (128, 1), 0) < HEAD_DIM


def _split_heads(tile, j):
    lo = _lane_lo()
    parts = []
    for g in (2 * j, 2 * j + 1):
        blk = tile[:, 128 * g : 128 * (g + 1)]
        parts += [jnp.where(lo, blk, 0.0), jnp.where(lo, 0.0, blk)]
    return jnp.concatenate(parts, axis=0)


def _dup_lanes(kv, j):
    rolled = pltpu.roll(kv, HEAD_DIM, 1)
    lo = _lane_lo()
    return jnp.where(lo, kv, rolled) if j == 0 else jnp.where(lo, rolled, kv)


def _fold_lanes(acc):
    return acc + pltpu.roll(acc, HEAD_DIM, 1)


def _fill_bias_t(bias_ref):
    kj = lax.broadcasted_iota(jnp.int32, (256, 128), 0)
    qi = lax.broadcasted_iota(jnp.int32, (256, 128), 1)
    diff = qi + WINDOW - kj
    band = (diff >= 0) & (diff < WINDOW)
    bias_ref[0] = jnp.where(band, 0.0, -1e30)
    bias_ref[1] = jnp.where(band & (kj >= 128), 0.0, -1e30)


def _sink_row(sink_ref, pair, j):
    return jnp.concatenate([jnp.full((1, 128), sink_ref[0, 8 * pair + 4 * j + h], F32) for h in range(4)], axis=1)


def _softmax_t(q4, kw, bias4, sink):
    s = _dot_nt(kw, q4) + bias4
    m = jnp.maximum(jnp.max(s, axis=0, keepdims=True), sink)
    p = jnp.exp(s - m)
    e_sink = jnp.exp(sink - m)
    denom = jnp.sum(p, axis=0, keepdims=True) + e_sink
    return p, e_sink, denom


def _attn_specs(t, tq):
    return dict(
        q=pl.BlockSpec((tq, 512), lambda p, i: (i, p)),
        k=pl.BlockSpec((t, 128), lambda p, i: (0, SEG_K // 128 + p)),
        v=pl.BlockSpec((t, 128), lambda p, i: (0, SEG_V // 128 + p)),
        sink=pl.BlockSpec(memory_space=pltpu.SMEM),
        act=pl.BlockSpec((tq, 512), lambda p, i: (i, p)),
    )


def _attn_fwd(proj, sinks):
    t = proj.shape[0]
    tq = min(512, t)
    nb = tq // 128
    n_tiles = t // tq
    sp = _attn_specs(t, tq)

    def body(q_ref, k_ref, v_ref, sink_ref, o_ref, kd, vdt, bias_ref):
        pair = pl.program_id(0)
        i = pl.program_id(1)

        @pl.when(i == 0)
        def _():
            kv, vv = k_ref[...], v_ref[...]
            for j in range(2):
                kd[j, 0:128, :] = jnp.zeros((128, 128), BF16)
                kd[j, 128:, :] = _dup_lanes(kv, j).astype(BF16)
                vdt[j, :, 0:128] = jnp.zeros((128, 128), BF16)
                vdt[j, :, 128:] = _dup_lanes(vv, j).T.astype(BF16)
            _fill_bias_t(bias_ref)

        row_lo = _row_lo()

        def block(b, carry):
            r0 = pl.multiple_of(b * 128, 128)
            gb = i * nb + b
            keys = pl.ds(pl.multiple_of(gb * 128, 128), 256)
            rows = pl.ds(r0, 128)
            qt = q_ref[rows, :] * (1.0 / math.sqrt(HEAD_DIM))
            bias = bias_ref[jnp.where(gb == 0, 1, 0)]
            bias4 = jnp.concatenate([bias] * 4, axis=1)
            cols = []
            for j in range(2):
                q4 = _split_heads(qt, j).astype(BF16)
                p, _, denom = _softmax_t(q4, kd[j, keys, :], bias4, _sink_row(sink_ref, pair, j))
                ot = _dot(vdt[j, :, keys], p.astype(BF16)) * (1.0 / denom)
                for gi in range(2):
                    z = jnp.where(row_lo, ot[:, 256 * gi : 256 * gi + 128], ot[:, 256 * gi + 128 : 256 * gi + 256])
                    cols.append(z.T)
            o_ref[rows, :] = jnp.concatenate(cols, axis=1).astype(BF16)
            return carry

        lax.fori_loop(0, nb, block, 0, unroll=True)

    return pl.pallas_call(
        body,
        name="attn_fwd",
        grid=(2, n_tiles),
        in_specs=[sp["q"], sp["k"], sp["v"], sp["sink"]],
        out_specs=sp["act"],
        out_shape=jax.ShapeDtypeStruct((t, D), BF16),
        scratch_shapes=[pltpu.VMEM((2, t + 128, 128), BF16), pltpu.VMEM((2, 128, t + 128), BF16), pltpu.VMEM((2, 256, 128), F32)],
        compiler_params=_params(2),
    )(proj, proj, proj, sinks)


def _attn_bwd(proj, o, d_o, tables, sinks, part_3):
    t = proj.shape[0]
    tq = min(512, t)
    nb = tq // 128
    n_tiles = t // tq
    lane_tables, (cos_t, sin_t) = tables
    sp = _attn_specs(t, tq)
    scale = 1.0 / math.sqrt(HEAD_DIM)
    half = ROT_DIM // 2

    def unrope_rows(z, c, s):
        out = []
        for base in (0, HEAD_DIM):
            g0, g1 = z[base : base + half], z[base + half : base + ROT_DIM]
            out += [g0 * c + g1 * s, g1 * c - g0 * s, z[base + ROT_DIM : base + HEAD_DIM]]
        return jnp.concatenate(out, axis=0)

    def swap(part_hbm, land, send_sem, recv_sem):
        sibling = _flip(_position(), 1)
        return pltpu.make_async_remote_copy(
            src_ref=part_hbm.at[:, :, sibling[2]], dst_ref=land, send_sem=send_sem, recv_sem=recv_sem,
            device_id=sibling, device_id_type=MESH,
        )

    def body(q_ref, k_ref, v_ref, o_ref, do_ref, ct_ref, st_ref, ck, s1k, s2k, sink_ref, part_hbm,
             dq_ref, dk_ref, dv_ref, dsink_ref, sum_hbm, mine_hbm, kd, kdt, vd, bias_ref, dkacc, dvacc, own, land, stage,
             send_sem, recv_sem, local_sems):
        pair = pl.program_id(0)
        i = pl.program_id(1)
        me = _position()
        load_own = pltpu.make_async_copy(part_hbm.at[:, :, me[2]], own, local_sems.at[0])

        @pl.when((pair == 0) & (i == 0))
        def _():
            swap(part_hbm, land, send_sem, recv_sem).start()
            load_own.start()

        @pl.when(i == 0)
        def _():
            kv, vv = k_ref[...], v_ref[...]
            for j in range(2):
                kdup = _dup_lanes(kv, j)
                kd[j, 0:128, :] = jnp.zeros((128, 128), BF16)
                kd[j, 128:, :] = kdup.astype(BF16)
                kdt[j, :, 0:128] = jnp.zeros((128, 128), BF16)
                kdt[j, :, 128:] = kdup.T.astype(BF16)
                vd[j, 0:128, :] = jnp.zeros((128, 128), BF16)
                vd[j, 128:, :] = _dup_lanes(vv, j).astype(BF16)
            _fill_bias_t(bias_ref)
            dkacc[...] = jnp.zeros_like(dkacc)
            dvacc[...] = jnp.zeros_like(dvacc)
            dsink_ref[...] = jnp.zeros_like(dsink_ref)

        row_lo = _row_lo()
        ones = jnp.ones((8, 128), BF16)

        def block(b, carry):
            r0 = pl.multiple_of(b * 128, 128)
            gb = i * nb + b
            keys = pl.ds(pl.multiple_of(gb * 128, 128), 256)
            rows = pl.ds(r0, 128)
            qt = q_ref[rows, :] * scale
            dob = do_ref[rows, :].astype(F32)
            prod = dob * o_ref[rows, :].astype(F32)
            ct, st = ct_ref[:, rows], st_ref[:, rows]
            bias = bias_ref[jnp.where(gb == 0, 1, 0)]
            bias4 = jnp.concatenate([bias] * 4, axis=1)
            cols = []
            for j in range(2):
                q4 = _split_heads(qt, j).astype(BF16)
                kw = kd[j, keys, :]
                p, e_sink, denom = _softmax_t(q4, kw, bias4, _sink_row(sink_ref, pair, j))
                inv = 1.0 / denom
                prob = p * inv
                do4 = _split_heads(dob, j).astype(BF16)
                delta = _dot_nt(ones, _split_heads(prod, j).astype(BF16))[0:1, :]
                dp = _dot_nt(vd[j, keys, :], do4)
                ds = (prob * (dp - delta)).astype(BF16)
                dkacc[j, keys, :] = dkacc[j, keys, :] + _dot(ds, q4)
                dvacc[j, keys, :] = dvacc[j, keys, :] + _dot(prob.astype(BF16), do4)
                dqt = _dot(kdt[j, :, keys], ds) * scale
                dsink = -(e_sink * inv) * delta
                for h in range(4):
                    row = 4 * j + h
                    total = jnp.sum(dsink[:, 128 * h : 128 * (h + 1)], axis=1, keepdims=True)
                    dsink_ref[0, row : row + 1, :] = dsink_ref[0, row : row + 1, :] + jnp.broadcast_to(total, (1, 128))
                for gi in range(2):
                    z = jnp.where(row_lo, dqt[:, 256 * gi : 256 * gi + 128], dqt[:, 256 * gi + 128 : 256 * gi + 256])
                    cols.append(unrope_rows(z, ct, st).T)
            dq_ref[rows, :] = jnp.concatenate(cols, axis=1).astype(BF16)
            return carry

        lax.fori_loop(0, nb, block, 0, unroll=True)

        @pl.when(i == n_tiles - 1)
        def _():
            lo = _lane_lo()
            dk = jnp.where(lo, _fold_lanes(dkacc[0, 128:, :]), _fold_lanes(dkacc[1, 128:, :]))
            dk_ref[...] = _rope_transposed(dk, ck[...], s1k[...], s2k[...]).astype(BF16)
            dv_ref[...] = jnp.where(lo, _fold_lanes(dvacc[0, 128:, :]), _fold_lanes(dvacc[1, 128:, :])).astype(BF16)

        @pl.when((pair == 1) & (i == n_tiles - 1))
        def _():
            swap(part_hbm, land, send_sem, recv_sem).wait()
            load_own.wait()
            own[...] = own[...] + land[...]
            stage[...] = own[...].astype(BF16)
            results = [pltpu.make_async_copy(stage, sum_hbm, local_sems.at[1]),
                       pltpu.make_async_copy(own.at[:, _chip_of(me)], mine_hbm, local_sems.at[2])]
            for cp in results:
                cp.start()
            for cp in results:
                cp.wait()

    kv_out = pl.BlockSpec((t, 128), lambda p, i: (0, p))
    tab_t = pl.BlockSpec((half, tq), lambda p, i: (0, i))
    tab_k = [pl.BlockSpec((None, t, 128), lambda p, i, a=a: (a, 0, 0)) for a in range(3)]
    return pl.pallas_call(
        body,
        name="attn_bwd",
        grid=(2, n_tiles),
        in_specs=[sp["q"], sp["k"], sp["v"], sp["act"], sp["act"], tab_t, tab_t, *tab_k, sp["sink"],
                  pl.BlockSpec(memory_space=pl.ANY)],
        out_specs=[sp["act"], kv_out, kv_out, pl.BlockSpec((1, 8, 128), lambda p, i: (p, 0, 0)),
                   pl.BlockSpec(memory_space=pl.ANY), pl.BlockSpec(memory_space=pl.ANY)],
        out_shape=[
            jax.ShapeDtypeStruct((t, D), BF16),
            jax.ShapeDtypeStruct((t, 256), BF16),
            jax.ShapeDtypeStruct((t, 256), BF16),
            jax.ShapeDtypeStruct((2, 8, 128), F32),
            jax.ShapeDtypeStruct((3, 4, ROW_SHARD, D), BF16),
            jax.ShapeDtypeStruct((3, ROW_SHARD, D), F32),
        ],
        scratch_shapes=[pltpu.VMEM((2, t + 128, 128), BF16), pltpu.VMEM((2, 128, t + 128), BF16), pltpu.VMEM((2, t + 128, 128), BF16),
                        pltpu.VMEM((2, 256, 128), F32), pltpu.VMEM((2, t + 128, 128), F32), pltpu.VMEM((2, t + 128, 128), F32),
                        pltpu.VMEM((3, 4, ROW_SHARD, D), F32), pltpu.VMEM((3, 4, ROW_SHARD, D), F32), pltpu.VMEM((3, 4, ROW_SHARD, D), BF16),
                        pltpu.SemaphoreType.DMA, pltpu.SemaphoreType.DMA, pltpu.SemaphoreType.DMA((3,))],
        compiler_params=_params(2),
    )(proj, proj, proj, o, d_o, cos_t, sin_t, lane_tables, lane_tables, lane_tables, sinks, part_3)


HALF = 512


def _softplus(y):
    u = jnp.exp(-jnp.abs(y))
    w = 1.0 + u
    log1p = jnp.where(w == 1.0, u, jnp.log(w) * (u / jnp.where(w == 1.0, 1.0, w - 1.0)))
    return jnp.maximum(y, 0.0) + log1p


def _gate_block(w_ref, blk):
    return w_ref[:, :, 256 * blk : 256 * (blk + 1)].reshape(256, 256)


def _gate_matmul(act_bf, w_ref):
    return jnp.concatenate([_dot(act_bf[:, 256 * blk : 256 * (blk + 1)], _gate_block(w_ref, blk)) for blk in range(2)], axis=1)


def _gate_matmul_t(grad_bf, w_ref):
    return jnp.concatenate([_dot_nt(grad_bf[:, 256 * blk : 256 * (blk + 1)], _gate_block(w_ref, blk)) for blk in range(2)], axis=1)


def _conv_and_gates(xe_ref, tm, cw_ref, cb_ref, wa_ref, wx_ref, ba_ref, bx_ref, lam_ref, keep):
    xc = cb_ref[...] + sum(cw_ref[k : k + 1, :] * xe_ref[pl.ds(5 + k, tm), :] for k in range(CONV_WIDTH))
    xc_bf = xc.astype(BF16)
    r = _sigmoid(_gate_matmul(xc_bf, wa_ref) + ba_ref[...])
    ig = _sigmoid(_gate_matmul(xc_bf, wx_ref) + bx_ref[...])
    sp = _softplus(-lam_ref[...])
    log_a = -LRU_C * r * sp
    tanh = jnp.tanh(-log_a)
    mult = jnp.sqrt(2.0 * tanh / (1.0 + tanh))
    a = jnp.where(keep, jnp.exp(log_a), 0.0)
    mult = jnp.where(keep, mult, 1.0)
    return xc, xc_bf, r, ig, sp, a, mult


def _scan_rows(a_ref, b_ref, h_ref, carry_ref, tm, reverse):
    width = a_ref.shape[1]
    row = lax.broadcasted_iota(jnp.int32, (8, width), 0)

    def group(g, carry):
        gg = tm // 8 - 1 - g if reverse else g
        r0 = pl.multiple_of(gg * 8, 8)
        a = a_ref[pl.ds(r0, 8), :]
        b = b_ref[pl.ds(r0, 8), :]
        for s in (1, 2, 4):
            shift = 8 - s if reverse else s
            a_sh = pltpu.roll(a, shift, 0)
            b_sh = pltpu.roll(b, shift, 0)
            ok = (row < 8 - s) if reverse else (row >= s)
            b = jnp.where(ok, a * b_sh + b, b)
            a = jnp.where(ok, a * a_sh, a)
        h = a * carry + b
        h_ref[pl.ds(r0, 8), :] = h
        last = h[0:1, :] if reverse else h[7:8, :]
        return jnp.broadcast_to(last, (8, width))

    carry_ref[...] = lax.fori_loop(0, tm // 8, group, carry_ref[...])


def _rnn_specs(t, tm, order):
    n = t // tm
    return dict(
        tile=pl.BlockSpec((tm, HALF), lambda ch, i: (order(i, n), ch)),
        before=lambda col0: pl.BlockSpec((8, HALF), lambda ch, i: (jnp.maximum(order(i, n) * (tm // 8) - 1, 0), col0 + ch)),
        proj_tile=lambda col0: pl.BlockSpec((tm, HALF), lambda ch, i: (order(i, n), col0 + ch)),
        keep=pl.BlockSpec((tm, 1), lambda ch, i: (order(i, n), 0)),
        gates=lambda which: pl.BlockSpec((N_DEV, GATE_SHARD, HALF), lambda ch, i: (0, PACK_GATE // GATE_SHARD + which, ch)),
        gate_grads=pl.BlockSpec((2, 2, 256, 256), lambda ch, i: (0, ch, 0, 0)),
        vec=pl.BlockSpec((1, HALF), lambda ch, i: (0, ch)),
        conv=pl.BlockSpec((CONV_WIDTH, HALF), lambda ch, i: (0, ch)),
    )


def _rnn_fwd(proj, keep, w_all, rg_ba, rg_bx, rg_lambda, conv_w, conv_b):
    t = proj.shape[0]
    tm = min(512, t)
    sp = _rnn_specs(t, tm, lambda i, n: i)

    def body(xr_ref, xb_ref, keep_ref, wa_ref, wx_ref, ba_ref, bx_ref, lam_ref, cw_ref, cb_ref, hh_ref, xe, a_buf, b_buf, carry):
        i = pl.program_id(1)

        @pl.when(i == 0)
        def _():
            carry[...] = jnp.zeros_like(carry)

        xe[0:8, :] = jnp.where(i > 0, xb_ref[...], 0.0)
        xe[8:, :] = xr_ref[...]
        keep_rows = keep_ref[...] != 0.0
        xc, _, _, ig, _, a, mult = _conv_and_gates(xe, tm, cw_ref, cb_ref, wa_ref, wx_ref, ba_ref, bx_ref, lam_ref, keep_rows)
        a_buf[...] = a
        b_buf[...] = mult * (ig * xc)
        _scan_rows(a_buf, b_buf, hh_ref, carry, tm, reverse=False)

    return pl.pallas_call(
        body,
        name="rnn_fwd",
        grid=(2, t // tm),
        in_specs=[sp["proj_tile"](SEG_XR // HALF), sp["before"](SEG_XR // HALF), sp["keep"], sp["gates"](0), sp["gates"](1),
                  sp["vec"], sp["vec"], sp["vec"], sp["conv"], sp["vec"]],
        out_specs=sp["tile"],
        out_shape=jax.ShapeDtypeStruct((t, D), F32),
        scratch_shapes=[pltpu.VMEM((tm + 8, HALF), F32), pltpu.VMEM((tm, HALF), F32), pltpu.VMEM((tm, HALF), F32), pltpu.VMEM((8, HALF), F32)],
        compiler_params=_params(2),
    )(proj, proj, keep, w_all, w_all, rg_ba, rg_bx, rg_lambda, conv_w, conv_b)


def _rnn_bwd(proj, hh, d_hh, keep, w_all, rg_ba, rg_bx, rg_lambda, conv_w, conv_b, sum_3_bf):
    t = proj.shape[0]
    tm = min(512, t)
    n_tiles = t // tm
    sp = _rnn_specs(t, tm, lambda i, n: n - 1 - i)

    def body(xr_ref, xb_ref, hh_ref, hb_ref, dhh_ref, keep_ref, wa_ref, wx_ref, ba_ref, bx_ref, lam_ref, cw_ref, cb_ref,
             s3_hbm, dxr_ref, gw_ref, gvec_ref, l3_hbm, xe, he, ae, c_buf, g_buf, dxe, carry, a_next, send_sems, recv_sems):
        ch = pl.program_id(0)
        i = pl.program_id(1)
        first_tile = i == n_tiles - 1
        views = [lambda ref, peer: ref.at[:, _chip_of(peer)]]

        @pl.when((ch == 0) & (i == 0))
        def _():
            for cp in _chip_copies([s3_hbm], [l3_hbm], views, send_sems, recv_sems):
                cp.start()

        @pl.when((ch == 1) & (i == n_tiles - 1))
        def _():
            for cp in _chip_copies([s3_hbm], [l3_hbm], views, send_sems, recv_sems):
                cp.wait_recv()
                cp.wait_send()

        @pl.when(i == 0)
        def _():
            carry[...] = jnp.zeros_like(carry)
            a_next[...] = jnp.zeros_like(a_next)
            dxe[tm:, :] = jnp.zeros((8, HALF), F32)
            gw_ref[...] = jnp.zeros_like(gw_ref)
            gvec_ref[...] = jnp.zeros_like(gvec_ref)

        xe[0:8, :] = jnp.where(first_tile, 0.0, xb_ref[...])
        xe[8:, :] = xr_ref[...]
        he[0:8, :] = jnp.where(first_tile, 0.0, hb_ref[...])
        he[8:, :] = hh_ref[...]
        keep_rows = keep_ref[...] != 0.0
        xc, xc_bf, r, ig, sp_, a, mult = _conv_and_gates(xe, tm, cw_ref, cb_ref, wa_ref, wx_ref, ba_ref, bx_ref, lam_ref, keep_rows)

        ae[0:tm, :] = a
        ae[tm:, :] = a_next[...]
        a_next[...] = jnp.broadcast_to(a[0:1, :], (8, HALF))
        c_buf[...] = ae[pl.ds(1, tm), :]
        g_buf[...] = dhh_ref[...]
        _scan_rows(c_buf, g_buf, g_buf, carry, tm, reverse=True)
        g = g_buf[...]

        h_prev = he[pl.ds(7, tm), :]
        bb = ig * xc
        d_ig = g * mult * xc
        d_xc = g * mult * ig
        d_log_a = jnp.where(keep_rows, g * h_prev * a - (g * bb) * (a * a) / mult, 0.0)
        d_r = d_log_a * (-LRU_C * sp_)
        d_sp = jnp.sum(d_log_a * (-LRU_C * r), axis=0, keepdims=True)
        d_zr = d_r * r * (1.0 - r)
        d_zi = d_ig * ig * (1.0 - ig)
        d_zr_bf = d_zr.astype(BF16)
        d_zi_bf = d_zi.astype(BF16)
        d_xc = d_xc + _gate_matmul_t(d_zr_bf, wa_ref) + _gate_matmul_t(d_zi_bf, wx_ref)
        for blk in range(2):
            cols = slice(256 * blk, 256 * (blk + 1))
            gw_ref[0, blk] = gw_ref[0, blk] + _dot_tn(xc_bf[:, cols], d_zr_bf[:, cols])
            gw_ref[1, blk] = gw_ref[1, blk] + _dot_tn(xc_bf[:, cols], d_zi_bf[:, cols])

        dxe[0:tm, :] = d_xc
        shifted = [dxe[pl.ds(3 - k, tm), :] for k in range(CONV_WIDTH)]
        dxr = sum(cw_ref[k : k + 1, :] * shifted[k] for k in range(CONV_WIDTH))
        dxr_ref[...] = dxr.astype(BF16)
        dxe[tm:, :] = d_xc[0:8, :]

        lam = lam_ref[...]
        xr_tile = xr_ref[...]
        sums = [
            jnp.sum(d_zr, axis=0, keepdims=True),
            jnp.sum(d_zi, axis=0, keepdims=True),
            d_sp * (-_sigmoid(-lam)),
            jnp.sum(d_xc, axis=0, keepdims=True),
        ] + [jnp.sum(xr_tile * shifted[k], axis=0, keepdims=True) for k in range(CONV_WIDTH)]
        gvec_ref[...] = gvec_ref[...] + jnp.concatenate(sums, axis=0)

    return pl.pallas_call(
        body,
        name="rnn_bwd",
        grid=(2, n_tiles),
        in_specs=[
            sp["proj_tile"](SEG_XR // HALF), sp["before"](SEG_XR // HALF), sp["tile"], sp["before"](0), sp["tile"], sp["keep"],
            sp["gates"](0), sp["gates"](1), sp["vec"], sp["vec"], sp["vec"], sp["conv"], sp["vec"], pl.BlockSpec(memory_space=pl.ANY),
        ],
        out_specs=[sp["tile"], sp["gate_grads"], pl.BlockSpec((8, HALF), lambda ch, i: (0, ch)), pl.BlockSpec(memory_space=pl.ANY)],
        out_shape=[
            jax.ShapeDtypeStruct((t, D), BF16),
            jax.ShapeDtypeStruct((2, 4, 256, 256), F32),
            jax.ShapeDtypeStruct((8, D), F32),
            jax.ShapeDtypeStruct((len(CHIP_FLIPS), 3, ROW_SHARD, D), BF16),
        ],
        scratch_shapes=[
            pltpu.VMEM((tm + 8, HALF), F32),
            pltpu.VMEM((tm + 8, HALF), F32),
            pltpu.VMEM((tm + 8, HALF), F32),
            pltpu.VMEM((tm, HALF), F32),
            pltpu.VMEM((tm, HALF), F32),
            pltpu.VMEM((tm + 8, HALF), F32),
            pltpu.VMEM((8, HALF), F32),
            pltpu.VMEM((8, HALF), F32),
            pltpu.SemaphoreType.DMA((len(CHIP_FLIPS),)),
            pltpu.SemaphoreType.DMA((len(CHIP_FLIPS),)),
        ],
        compiler_params=_params(2),
    )(proj, proj, hh, hh, d_hh, keep, w_all, w_all, rg_ba, rg_bx, rg_lambda, conv_w, conv_b, sum_3_bf)


def _mid(x, target, o, hh, proj, w_all, gate, final_g):
    t = x.shape[0]
    tm = min(256, t)
    n_tiles = t // tm
    full = pl.BlockSpec((tm, D), lambda i: (i, 0))
    half = lambda col: pl.BlockSpec((tm, 512), lambda i: (i, col))
    vec = pl.BlockSpec((1, D), lambda i: (0, 0))
    any_spec = pl.BlockSpec(memory_space=pl.ANY)

    def body(x_ref, tg_ref, o_ref, hh_ref, ga0, ga1, gr0, gr1, ma0, ma1, mr0, mr1, w_hbm, gate_ref, fg_ref,
             dres_ref, do_ref, dhh_ref, dga_ref, dgr_ref, dma_ref, dmr_ref, gw_hbm, vec_ref, w, gw, sem, w_sems):
        i = pl.program_id(0)

        @pl.when(i == 0)
        def _():
            loads = [
                pltpu.make_async_copy(
                    w_hbm.at[s, pl.ds(ROW_SHARD * k, ROW_SHARD), :], w.at[k, pl.ds(ROW_SHARD * s, ROW_SHARD), :], w_sems.at[k * N_DEV + s]
                )
                for k in range(3)
                for s in range(N_DEV)
            ]
            for cp in loads:
                cp.start()
            for cp in loads:
                cp.wait()
            gw[...] = jnp.zeros_like(gw)
            vec_ref[...] = jnp.zeros_like(vec_ref)

        cat = lambda a, b: jnp.concatenate([a[...], b[...]], axis=1)
        ga, gr, ma, mr = cat(ga0, ga1), cat(gr0, gr1), cat(ma0, ma1), cat(mr0, mr1)
        ov, hv, gate_v, fg = o_ref[...].astype(F32), hh_ref[...], gate_ref[...], fg_ref[...]
        sg_a = _sigmoid(ga)
        sg_r = _sigmoid(gr)
        silu_a = ga * sg_a
        silu_r = gr * sg_r
        ya = (ov * silu_a).astype(BF16)
        yr = (hv * silu_r).astype(BF16)
        pa = _dot(ya, w[0])
        pr = _dot(yr, w[1])
        sa = _sigmoid(ma)
        sr = _sigmoid(mr)
        merged = (sa * pa + sr * pr).astype(BF16)
        mo = _dot(merged, w[2])
        out = x_ref[...] + gate_v * mo
        r2 = lax.rsqrt(jnp.mean(out * out, axis=-1, keepdims=True) + NORM_EPS)
        outn = out * r2
        err = outn * fg - tg_ref[...]
        loss = 0.5 * jnp.sum(jnp.mean(err * err, axis=-1, keepdims=True), axis=0, keepdims=True)
        dy = err * (1.0 / D)
        g_final = jnp.sum(dy * outn, axis=0, keepdims=True)
        doutn = dy * fg
        dout = r2 * (doutn - outn * jnp.mean(doutn * outn, axis=-1, keepdims=True))
        dres_ref[...] = dout
        d_gate = jnp.sum(dout * mo, axis=0, keepdims=True)
        dmo = (dout * gate_v).astype(BF16)
        dmerged = _dot_nt(dmo, w[2])
        gw[2] = gw[2] + _dot_tn(merged, dmo)
        dpa = (dmerged * sa).astype(BF16)
        dpr = (dmerged * sr).astype(BF16)
        dma_ref[...] = (dmerged * pa * (sa * (1.0 - sa))).astype(BF16)
        dmr_ref[...] = (dmerged * pr * (sr * (1.0 - sr))).astype(BF16)
        dya = _dot_nt(dpa, w[0])
        dyr = _dot_nt(dpr, w[1])
        gw[0] = gw[0] + _dot_tn(ya, dpa)
        gw[1] = gw[1] + _dot_tn(yr, dpr)
        do_ref[...] = (dya * silu_a).astype(BF16)
        dhh_ref[...] = dyr * silu_r
        dga_ref[...] = (dya * ov * (sg_a * (1.0 + ga * (1.0 - sg_a)))).astype(BF16)
        dgr_ref[...] = (dyr * hv * (sg_r * (1.0 + gr * (1.0 - sg_r)))).astype(BF16)
        zeros = jnp.zeros((5, D), F32)
        vec_ref[...] = vec_ref[...] + jnp.concatenate([g_final, d_gate, jnp.broadcast_to(loss, (1, D)), zeros], axis=0)

        @pl.when(i == n_tiles - 1)
        def _():
            cp = pltpu.make_async_copy(gw, gw_hbm, sem)
            cp.start()
            cp.wait()

    bf_tile = jax.ShapeDtypeStruct((t, D), BF16)
    f_tile = jax.ShapeDtypeStruct((t, D), F32)
    return pl.pallas_call(
        body,
        name="mid",
        grid=(n_tiles,),
        in_specs=[full, full, full, full,
                  half(SEG_GA // 512), half(SEG_GA // 512 + 1), half(SEG_GR // 512), half(SEG_GR // 512 + 1),
                  half(SEG_MA // 512), half(SEG_MA // 512 + 1), half(SEG_MR // 512), half(SEG_MR // 512 + 1),
                  any_spec, pl.BlockSpec((1, D), lambda i: (0, 2)), vec],
        out_specs=[full, full, full, full, full, full, full, any_spec, pl.BlockSpec((8, D), lambda i: (0, 0))],
        out_shape=[f_tile, bf_tile, f_tile, bf_tile, bf_tile, bf_tile, bf_tile,
                   jax.ShapeDtypeStruct((3, D, D), F32), jax.ShapeDtypeStruct((8, D), F32)],
        scratch_shapes=[pltpu.VMEM((3, D, D), BF16), pltpu.VMEM((3, D, D), F32), pltpu.SemaphoreType.DMA, pltpu.SemaphoreType.DMA((3 * N_DEV,))],
        compiler_params=_params(1),
    )(x, target, o, hh, proj, proj, proj, proj, proj, proj, proj, proj, w_all, gate, final_g)


_SEGMENTS = ((SEG_Q, 1024), (SEG_K, 256), (SEG_V, 256), (SEG_GA, 1024), (SEG_XR, 1024), (SEG_GR, 1024), (SEG_MA, 1024), (SEG_MR, 1024))


def _seg_specs(tm):
    return [pl.BlockSpec((tm, width), lambda i: (i, 0)) for _, width in _SEGMENTS]


def _bwd_dh(dsegs, w_all, x, dres, mod, norm_g, token):
    t = x.shape[0]
    tm = min(512, t)
    n_tiles = t // tm
    n_seg = len(_SEGMENTS)
    full = pl.BlockSpec((tm, D), lambda i: (i, 0))

    def body(*refs):
        segs = refs[:n_seg]
        w_hbm, x_ref, dres_ref, mod_ref, ng_ref, token_ref, gx_ref, vec_ref, w, sems = refs[n_seg:]
        i = pl.program_id(0)

        @pl.when(i == 0)
        def _():
            loads = [
                pltpu.make_async_copy(w_hbm.at[s], w.at[pl.ds(W_IN_SHARD * s, W_IN_SHARD), :], sems.at[s])
                for s in range(N_DEV)
            ]
            for cp in loads:
                cp.start()
            for cp in loads:
                cp.wait()
            vec_ref[...] = jnp.zeros_like(vec_ref)

        dh = sum(_dot(seg[...], w[start : start + width, :]) for seg, (start, width) in zip(segs, _SEGMENTS))
        xv = x_ref[...]
        r1 = lax.rsqrt(jnp.mean(xv * xv, axis=-1, keepdims=True) + NORM_EPS)
        xn = xv * r1
        ng = ng_ref[...]
        scale1 = 1.0 + mod_ref[:, D : 2 * D]
        d_shift = jnp.sum(dh, axis=0, keepdims=True)
        d_scale = jnp.sum(dh * (xn * ng), axis=0, keepdims=True)
        g_norm = jnp.sum(dh * xn * scale1, axis=0, keepdims=True)
        dxn = dh * (ng * scale1)
        dx = r1 * (dxn - xn * jnp.mean(dxn * xn, axis=-1, keepdims=True))
        gx_ref[...] = dres_ref[...] + dx
        vec_ref[...] = vec_ref[...] + jnp.concatenate([d_shift, d_scale, g_norm, jnp.zeros((5, D), F32)], axis=0)

    any_spec = pl.BlockSpec(memory_space=pl.ANY)
    return pl.pallas_call(
        body,
        name="bwd_dh",
        grid=(n_tiles,),
        in_specs=_seg_specs(tm) + [any_spec, full, full, pl.BlockSpec((1, 3 * D), lambda i: (0, 0)), pl.BlockSpec((1, D), lambda i: (0, 0)),
                                   pl.BlockSpec(token.shape, lambda i: (0, 0))],
        out_specs=[full, pl.BlockSpec((8, D), lambda i: (0, 0))],
        out_shape=[jax.ShapeDtypeStruct((t, D), F32), jax.ShapeDtypeStruct((8, D), F32)],
        scratch_shapes=[pltpu.VMEM((IN_W, D), BF16), pltpu.SemaphoreType.DMA((N_DEV,))],
        compiler_params=_params(1),
    )(*dsegs, w_all, x, dres, mod, norm_g, token)


_HBM = pl.BlockSpec(memory_space=pltpu.HBM)
_SEM = pl.BlockSpec(memory_space=pltpu.SEMAPHORE)
_EFFECT = pltpu.SideEffectType.DATAFLOW_SIDE_EFFECTING


def _w_exchange_copies(sum_ref, land_ref, sems):
    me = _position()
    n = len(CHIP_FLIPS)
    remote = [
        pltpu.make_async_remote_copy(
            src_ref=sum_ref.at[_chip_of(_flip(me, k))], dst_ref=land_ref.at[j], send_sem=sems[j], recv_sem=sems[n + j],
            device_id=_flip(me, k), device_id_type=MESH,
        )
        for j, k in enumerate(CHIP_FLIPS)
    ]
    return remote, []


def _vec_exchange_copies(vec_ref, land_ref, sems):
    me = _position()
    n = N_DEV - 1
    remote = [
        pltpu.make_async_remote_copy(
            src_ref=vec_ref, dst_ref=land_ref.at[_slot(me)], send_sem=sems[k - 1], recv_sem=sems[n + k - 1],
            device_id=_flip(me, k), device_id_type=MESH,
        )
        for k in range(1, N_DEV)
    ]
    return remote, [pltpu.make_async_copy(vec_ref, land_ref.at[_slot(me)], sems[2 * n])]


def _exchange_start(name, copies, n_sems, src, land_shape):
    land = lax.empty(land_shape, src.dtype)

    def body(src_ref, land_ref, *outs):
        remote, local = copies(src_ref, land_ref, outs[:n_sems])
        for cp in remote + local:
            cp.start()
        token = outs[n_sems + 2]
        token[...] = jnp.zeros_like(token)

    res = pl.pallas_call(
        body,
        name=name,
        out_shape=(*[pltpu.SemaphoreType.DMA(())] * n_sems, pltpu.HBM(src.shape, src.dtype), pltpu.HBM(land.shape, src.dtype),
                   jax.ShapeDtypeStruct((8, 128), F32)),
        in_specs=(_HBM, _HBM),
        out_specs=(*[_SEM] * n_sems, _HBM, _HBM, pl.BlockSpec(memory_space=pltpu.VMEM)),
        input_output_aliases={0: n_sems, 1: n_sems + 1},
        compiler_params=pltpu.CompilerParams(has_side_effects=_EFFECT),
    )(pltpu.with_memory_space_constraint(src, pltpu.HBM), pltpu.with_memory_space_constraint(land, pltpu.HBM))
    return res[:n_sems], res[n_sems], res[n_sems + 1], res[n_sems + 2]


def _exchange_wait(name, copies, sems, src_thru, land_thru, after):
    n_sems = len(sems)

    def body(src_ref, land_ref, *rest):
        remote, local = copies(src_ref, land_ref, rest[:n_sems])
        for cp in remote:
            cp.wait_send()
            cp.wait_recv()
        for cp in local:
            cp.wait()

    return pl.pallas_call(
        body,
        name=name,
        out_shape=(pltpu.HBM(src_thru.shape, src_thru.dtype), pltpu.HBM(land_thru.shape, land_thru.dtype)),
        in_specs=(_HBM, _HBM, *[_SEM] * n_sems, *[pl.BlockSpec(memory_space=pl.ANY)] * len(after)),
        out_specs=(_HBM, _HBM),
        input_output_aliases={0: 0, 1: 1},
        compiler_params=pltpu.CompilerParams(has_side_effects=_EFFECT),
    )(src_thru, land_thru, *sems, *after)[1]


def _bwd_gw(dsegs, h, g_gates):
    t = h.shape[0]
    tm = min(512, t)
    n_tiles = t // tm
    n_seg = len(_SEGMENTS)
    n_chips = N_DEV // 2

    def gate_copies(g_hbm, lg_hbm, send_sems, recv_sems, arriving):
        me = _position()
        return [
            pltpu.make_async_remote_copy(
                src_ref=g_hbm.at[:, :, _slot(_flip(me, k))],
                dst_ref=lg_hbm.at[_slot(_flip(me, k) if arriving else me)],
                send_sem=send_sems.at[k - 1],
                recv_sem=recv_sems.at[k - 1],
                device_id=_flip(me, k),
                device_id_type=MESH,
            )
            for k in range(1, N_DEV)
        ]

    def body(*refs):
        segs = refs[:n_seg]
        (h_ref, g_hbm, own_hbm, sum_hbm, lg_hbm, acc, stage, land, own_sems, pair_send, pair_recv, send_sems, recv_sems, g_local) = refs[n_seg:]
        i = pl.program_id(0)
        slot = _slot(_position())
        own_gates = pltpu.make_async_copy(g_hbm.at[:, :, slot], lg_hbm.at[slot], g_local)

        @pl.when(i == 0)
        def _():
            own_gates.start()
            for cp in gate_copies(g_hbm, lg_hbm, send_sems, recv_sems, arriving=False):
                cp.start()
            acc[...] = jnp.zeros_like(acc)

        def accumulate(k, hv):
            start, width = _SEGMENTS[k]
            acc[start : start + width, :] = acc[start : start + width, :] + _dot_tn(segs[k][...], hv)

        @pl.when(i < n_tiles - 1)
        def _():
            hv = h_ref[...]
            for k in range(n_seg):
                accumulate(k, hv)

        @pl.when(i == n_tiles - 1)
        def _():
            me = _position()
            sibling = _flip(me, 1)

            def rows(q, c):
                return pl.ds(pl.multiple_of(W_IN_SHARD * (2 * q + c), 8), W_IN_SHARD)

            def to_sibling(q):
                return pltpu.make_async_remote_copy(
                    src_ref=stage.at[q % 2], dst_ref=land.at[q], send_sem=pair_send.at[q], recv_sem=pair_recv.at[q],
                    device_id=sibling, device_id_type=MESH,
                )

            hv = h_ref[...]
            q = 0
            for k, (start, width) in enumerate(_SEGMENTS):
                accumulate(k, hv)
                while q < n_chips and 2 * W_IN_SHARD * (q + 1) <= start + width:
                    if q >= 2:
                        to_sibling(q - 2).wait_send()
                    stage[q % 2] = acc[rows(q, sibling[2]), :].astype(BF16)
                    to_sibling(q).start()
                    q += 1
            assert q == n_chips
            for q in range(n_chips - 2, n_chips):
                to_sibling(q).wait_send()
            sums = [pltpu.make_async_copy(stage.at[q % 2], sum_hbm.at[q], own_sems.at[q]) for q in range(n_chips)]
            for q in range(n_chips):
                to_sibling(q).wait_recv()
                if q >= 2:
                    sums[q - 2].wait()
                total = acc[rows(q, me[2]), :] + land[q].astype(F32)
                acc[rows(q, me[2]), :] = total
                stage[q % 2] = total.astype(BF16)
                sums[q].start()
            mine = pltpu.make_async_copy(acc.at[rows(_chip_of(me), me[2]), :], own_hbm, own_sems.at[n_chips])
            mine.start()
            for q in range(n_chips - 2, n_chips):
                sums[q].wait()
            mine.wait()
            for cp in gate_copies(g_hbm, lg_hbm, send_sems, recv_sems, arriving=True):
                cp.wait_recv()
            for cp in gate_copies(g_hbm, lg_hbm, send_sems, recv_sems, arriving=False):
                cp.wait_send()
            own_gates.wait()

    any_spec = pl.BlockSpec(memory_space=pl.ANY)
    n_copies = N_DEV - 1
    return pl.pallas_call(
        body,
        name="bwd_gw",
        grid=(n_tiles,),
        in_specs=_seg_specs(tm) + [pl.BlockSpec((tm, D), lambda i: (i, 0)), any_spec],
        out_specs=[any_spec, any_spec, any_spec],
        out_shape=[jax.ShapeDtypeStruct((W_IN_SHARD, D), F32),
                   jax.ShapeDtypeStruct((n_chips, W_IN_SHARD, D), BF16),
                   jax.ShapeDtypeStruct((N_DEV, 2, 4, GATE_SHARD, 256), F32)],
        scratch_shapes=[pltpu.VMEM((IN_W, D), F32), pltpu.VMEM((2, W_IN_SHARD, D), BF16), pltpu.VMEM((n_chips, W_IN_SHARD, D), BF16),
                        pltpu.SemaphoreType.DMA((n_chips + 1,)), pltpu.SemaphoreType.DMA((n_chips,)), pltpu.SemaphoreType.DMA((n_chips,)),
                        pltpu.SemaphoreType.DMA((n_copies,)), pltpu.SemaphoreType.DMA((n_copies,)), pltpu.SemaphoreType.DMA],
        compiler_params=_params(1),
    )(*dsegs, h, g_gates)


def _adamw(w, g, m, v):
    m = ADAM_B1 * m + (1.0 - ADAM_B1) * g
    v = ADAM_B2 * v + (1.0 - ADAM_B2) * (g * g)
    m_hat = m / (1.0 - ADAM_B1**ADAM_STEP)
    v_hat = v / (1.0 - ADAM_B2**ADAM_STEP)
    delta = -ADAM_LR * (m_hat / (jnp.sqrt(v_hat) + ADAM_EPS) + ADAM_WD * w)
    return delta, m, v


def _w_in_update(own, land, w_t, m_t, v_t, token):
    n_slots, rows, cols = land.shape
    rb = 208
    spec = pl.BlockSpec((rb, cols), lambda i: (i, 0))

    def body(o_ref, l_ref, w_ref, m_ref, v_ref, token_ref, g_ref, d_ref, m2_ref, v2_ref):
        total = o_ref[...]
        for s in range(n_slots):
            total = total + l_ref[s].astype(F32)
        g_ref[...] = total
        d_ref[...], m2_ref[...], v2_ref[...] = _adamw(w_ref[...], total, m_ref[...], v_ref[...])

    shape = jax.ShapeDtypeStruct((rows, cols), F32)
    return pl.pallas_call(
        body,
        name="w_in_update",
        grid=(rows // rb,),
        in_specs=[spec, pl.BlockSpec((n_slots, rb, cols), lambda i: (0, i, 0)), spec, spec, spec, pl.BlockSpec(token.shape, lambda i: (0, 0))],
        out_specs=[spec] * 4,
        out_shape=[shape] * 4,
        compiler_params=_params(1),
    )(own, land, w_t, m_t, v_t, token)


def _shard_update(own_3, land_3, land_g, weights, m_in, v_in, token):
    n = len(weights)

    def body(*refs):
        o3_ref, l3_ref, lg_ref = refs[:3]
        w_refs, m_refs, v_refs = (refs[3 + n * k : 3 + n * (k + 1)] for k in range(3))
        outs = refs[4 + 3 * n :]
        for k in range(n):
            if k < 3:
                g = o3_ref[k]
                for slot in range(3):
                    g = g + l3_ref[slot, k].astype(F32)
            else:
                g = lg_ref[0, k - 3]
                for slot in range(1, N_DEV):
                    g = g + lg_ref[slot, k - 3]
            d, m2, v2 = _adamw(w_refs[k][0], g, m_refs[k][0], v_refs[k][0])
            for which, val in enumerate((g, d, m2, v2)):
                outs[which * n + k][0] = val

    shapes = [jax.ShapeDtypeStruct(w.shape, F32) for w in weights]
    res = pl.pallas_call(
        body, name="shard_update", out_shape=shapes * 4, compiler_params=pltpu.CompilerParams(vmem_limit_bytes=VMEM_LIMIT)
    )(own_3, land_3, land_g, *weights, *m_in, *v_in, token)
    return [res[n * k : n * (k + 1)] for k in range(4)]


VEC_LAYOUT = (("b_ada", 3 * D), ("norm_g", D), ("attn_sinks", 128), ("conv_b", D), ("rg_ba", D), ("rg_bx", D), ("rg_lambda", D),
              ("final_g", D), ("conv_w", CONV_WIDTH * D), ("loss", 128))
VEC_OFFSET = {name: sum(w for _, w in VEC_LAYOUT[:k]) for k, (name, _) in enumerate(VEC_LAYOUT)}
VEC_ROWS = 112
VEC_WIDTH = VEC_ROWS * 128
assert sum(w for _, w in VEC_LAYOUT) <= VEC_WIDTH
N_REPLICATED = 8


def _small_update(vec_all, c_all_t, me, rep_w, rep_m, rep_v, ada, conv):
    n_ada = ada[0].shape[1]
    n_conv = conv[0].shape[1]

    def body(*refs):
        me_ref, vec_ref, ct_ref = refs[:3]
        w_refs, m_refs, v_refs = (refs[3 + N_REPLICATED * k : 3 + N_REPLICATED * (k + 1)] for k in range(3))
        pos = 3 + 3 * N_REPLICATED
        ada_refs, conv_refs = refs[pos : pos + 3], refs[pos + 3 : pos + 6]
        outs = refs[pos + 6 :]
        loss_ref, rep_outs = outs[0], outs[1 : 1 + 4 * N_REPLICATED]
        ada_outs, conv_outs = outs[1 + 4 * N_REPLICATED : 5 + 4 * N_REPLICATED], outs[5 + 4 * N_REPLICATED : 9 + 4 * N_REPLICATED]
        conv_buf = outs[9 + 4 * N_REPLICATED]

        total = vec_ref[0:1, :]
        for slot in range(1, N_DEV):
            total = total + vec_ref[slot : slot + 1, :]
        loss_ref[...] = total[:, VEC_OFFSET["loss"] : VEC_OFFSET["loss"] + 1]
        for k, (name, _) in enumerate(VEC_LAYOUT[:N_REPLICATED]):
            width = w_refs[k].shape[1]
            g = total[:, VEC_OFFSET[name] : VEC_OFFSET[name] + width]
            d, m2, v2 = _adamw(w_refs[k][...], g, m_refs[k][...], v_refs[k][...])
            for which, val in enumerate((g, d, m2, v2)):
                rep_outs[which * N_REPLICATED + k][...] = val

        device = me_ref[0]
        for k in range(CONV_WIDTH):
            conv_buf[k : k + 1, :] = total[:, VEC_OFFSET["conv_w"] + D * k : VEC_OFFSET["conv_w"] + D * (k + 1)]
        g_conv = conv_buf[:, pl.ds(pl.multiple_of(device * n_conv, 128), n_conv)]
        for ref, val in zip(conv_outs, (g_conv,) + _adamw(conv_refs[0][...], g_conv, conv_refs[1][...], conv_refs[2][...])):
            ref[...] = val

        dmod = vec_ref[:, pl.ds(pl.multiple_of(device * n_ada, 128), n_ada)]
        g_ada = _dot(ct_ref[...], dmod)
        for ref, val in zip(ada_outs, (g_ada,) + _adamw(ada_refs[0][...], g_ada, ada_refs[1][...], ada_refs[2][...])):
            ref[...] = val

    vmem = pl.BlockSpec(memory_space=pltpu.VMEM)
    rep_shapes = [jax.ShapeDtypeStruct(w.shape, F32) for w in rep_w]
    out_shape = ([jax.ShapeDtypeStruct((1, 1), F32)] + rep_shapes * 4 + [jax.ShapeDtypeStruct(ada[0].shape, F32)] * 4
                 + [jax.ShapeDtypeStruct(conv[0].shape, F32)] * 4)
    n_in = 3 + 3 * N_REPLICATED + 6
    res = pl.pallas_call(
        body,
        name="small_update",
        in_specs=[pl.BlockSpec(memory_space=pltpu.SMEM)] + [vmem] * (n_in - 1),
        out_specs=[vmem] * len(out_shape),
        out_shape=out_shape,
        scratch_shapes=[pltpu.VMEM((CONV_WIDTH, D), F32)],
        compiler_params=pltpu.CompilerParams(vmem_limit_bytes=VMEM_LIMIT),
    )(me, vec_all, c_all_t, *rep_w, *rep_m, *rep_v, *ada, *conv)
    loss = res[0]
    reps = [res[1 + N_REPLICATED * k : 1 + N_REPLICATED * (k + 1)] for k in range(4)]
    return loss, reps, res[1 + 4 * N_REPLICATED : 5 + 4 * N_REPLICATED], res[5 + 4 * N_REPLICATED : 9 + 4 * N_REPLICATED]


def kernel(x, c, positions, w_ada, b_ada, norm_g, w_in, attn_sinks, conv_w, conv_b, rg_wa, rg_ba, rg_wx, rg_bx, rg_lambda, w_attn_proj, w_rnn_proj, w_out, final_g, loss_target, m_w_ada, m_b_ada, m_norm_g, m_w_in, m_attn_sinks, m_conv_w, m_conv_b, m_rg_wa, m_rg_ba, m_rg_wx, m_rg_bx, m_rg_lambda, m_w_attn_proj, m_w_rnn_proj, m_w_out, m_final_g, v_w_ada, v_b_ada, v_norm_g, v_w_in, v_attn_sinks, v_conv_w, v_conv_b, v_rg_wa, v_rg_ba, v_rg_wx, v_rg_bx, v_rg_lambda, v_w_attn_proj, v_w_rnn_proj, v_w_out, v_final_g):
    my_x, my_y, my_c = _position()
    me = _slot((my_x, my_y, my_c))
    x2, target = x[0], loss_target[0]
    pos = positions[0]

    gate_rows = lambda w: w[0].transpose(1, 0, 2).reshape(GATE_SHARD, D)
    w_pack = jnp.concatenate([w_attn_proj[0], w_rnn_proj[0], w_out[0], gate_rows(rg_wa), gate_rows(rg_wx)], axis=0).astype(BF16)
    conv_rows = jnp.pad(conv_w[0], ((0, 8 - CONV_WIDTH), (0, 0)))
    w_in_t = w_in[0].T

    tables = _rope_tables(pos)
    where = jnp.stack([_chip_of((my_x, my_y)), me]).astype(jnp.int32)
    proj, h, mod, lane_tables, w_t_all, w_all, c_rows, conv_all = _fwd_in(
        x2, jnp.broadcast_to(c, (8, D)), conv_rows, w_ada[0], b_ada, norm_g, w_in_t.astype(BF16), tables, w_pack, where
    )
    c_all = c_rows[:, 0, :]
    conv_full = conv_all[:, :CONV_WIDTH, :].transpose(1, 0, 2).reshape(CONV_WIDTH, D)
    o = _attn_fwd(proj, attn_sinks)
    keep = (pos != 0).astype(F32)[:, None]
    hh = _rnn_fwd(proj, keep, w_all, rg_ba, rg_bx, rg_lambda, conv_full, conv_b)

    dres, d_o, d_hh, d_ga, d_gr, d_ma, d_mr, gw3, mid_vec = _mid(x2, target, o, hh, proj, w_all, mod, final_g[None, :])
    part_3 = gw3.reshape(3, 4, 2, ROW_SHARD, D)
    d_q, d_k, d_v, d_sink, sum_3_bf, own_3 = _attn_bwd(proj, o, d_o, (lane_tables, tables), attn_sinks, part_3)
    d_xr, g_gates, rnn_vec, land_3 = _rnn_bwd(proj, hh, d_hh, keep, w_all, rg_ba, rg_bx, rg_lambda, conv_full, conv_b, sum_3_bf)
    dsegs = (d_q, d_k, d_v, d_ga, d_xr, d_gr, d_ma, d_mr)
    own_w, sum_w_bf, land_g = _bwd_gw(dsegs, h, g_gates.reshape(2, 4, N_DEV, GATE_SHARD, 256))
    w_sems, sum_w_thru, land_w_thru, token = _exchange_start(
        "w_exchange_start", _w_exchange_copies, 2 * len(CHIP_FLIPS), sum_w_bf, (len(CHIP_FLIPS), W_IN_SHARD, D)
    )
    grad_x, dh_vec = _bwd_dh(dsegs, w_t_all, x2, dres, mod, norm_g, token)
    land_w = _exchange_wait("w_exchange_wait", _w_exchange_copies, w_sems, sum_w_thru, land_w_thru, (dh_vec,))

    pieces = dict(
        b_ada=jnp.concatenate([dh_vec[0:1], dh_vec[1:2], mid_vec[1:2]], axis=1),
        norm_g=dh_vec[2:3],
        attn_sinks=jnp.pad(d_sink[:, :, 0].reshape(1, 16), ((0, 0), (0, 112))),
        conv_b=rnn_vec[3:4],
        rg_ba=rnn_vec[0:1],
        rg_bx=rnn_vec[1:2],
        rg_lambda=rnn_vec[2:3],
        final_g=mid_vec[0:1],
        conv_w=rnn_vec[4:8].reshape(1, CONV_WIDTH * D),
        loss=mid_vec[2:3, 0:128],
    )
    vec = jnp.concatenate([pieces[name] for name, _ in VEC_LAYOUT], axis=1)
    vec = jnp.pad(vec, ((0, 0), (0, VEC_WIDTH - vec.shape[1]))).reshape(VEC_ROWS, 128)
    vec_sems, vec_thru, vec_land, vec_token = _exchange_start(
        "vec_exchange_start", _vec_exchange_copies, 2 * N_DEV - 1, vec, (N_DEV, VEC_ROWS, 128)
    )
    w_in_res = _w_in_update(own_w, land_w, w_in_t, m_w_in[0].T, v_w_in[0].T, vec_token)
    shard_res = _shard_update(
        own_3, land_3, land_g,
        (w_attn_proj, w_rnn_proj, w_out, rg_wa, rg_wx),
        (m_w_attn_proj, m_w_rnn_proj, m_w_out, m_rg_wa, m_rg_wx),
        (v_w_attn_proj, v_w_rnn_proj, v_w_out, v_rg_wa, v_rg_wx),
        vec_token,
    )
    vec_all = _exchange_wait(
        "vec_exchange_wait", _vec_exchange_copies, vec_sems, vec_thru, vec_land, (w_in_res[0], shard_res[0][0])
    )
    rep_w = [b_ada, norm_g, attn_sinks, conv_b, rg_ba, rg_bx, rg_lambda, final_g[None, :]]
    rep_m = [m_b_ada, m_norm_g, m_attn_sinks, m_conv_b, m_rg_ba, m_rg_bx, m_rg_lambda, m_final_g[None, :]]
    rep_v = [v_b_ada, v_norm_g, v_attn_sinks, v_conv_b, v_rg_ba, v_rg_bx, v_rg_lambda, v_final_g[None, :]]
    loss, reps, ada_res, conv_res = _small_update(
        vec_all.reshape(N_DEV, VEC_WIDTH), c_all.T, me.astype(jnp.int32).reshape(1), rep_w, rep_m, rep_v,
        (w_ada[0], m_w_ada[0], v_w_ada[0]), (conv_w[0], m_conv_w[0], v_conv_w[0]),
    )

    def by_weight(k):
        b, n, s, cb, ba, bx, lam, fg = reps[k]
        ap, rp, out, wa, wx = shard_res[k]
        return [ada_res[k][None], b, n, w_in_res[k].T[None], s, conv_res[k][None], cb, wa, ba, wx, bx, lam, ap, rp, out, fg[0]]

    return (loss[0, 0], grad_x[None], *by_weight(0), *by_weight(1), *by_weight(2), *by_weight(3))
```

```python
import math

import jax
import jax.numpy as jnp
from jax import lax
from jax.experimental import pallas as pl
from jax.experimental.pallas import tpu as pltpu

F32 = jnp.float32
BF16 = jnp.bfloat16

D = 1024
HEAD_DIM = 64
ROT_DIM = 16
ROPE_THETA = 500000.0
WINDOW = 128
LRU_C = 8.0
CONV_WIDTH = 4
NORM_EPS = 1e-6
IN_W = 6656
SEG_Q, SEG_K, SEG_V, SEG_GA, SEG_XR, SEG_GR, SEG_MA, SEG_MR = 0, 1024, 1280, 1536, 2560, 3584, 4608, 5632
N_DEV = 8
W_IN_SHARD = IN_W // N_DEV
ROW_SHARD = D // N_DEV
GATE_SHARD = 256 // N_DEV

ADAM_LR = 0.001
ADAM_B1 = 0.9
ADAM_B2 = 0.999
ADAM_EPS = 1e-08
ADAM_WD = 0.01
ADAM_STEP = 10

VMEM_LIMIT = 60 * 1024 * 1024
MESH = pl.DeviceIdType.MESH


def _dot(a, b):
    return jnp.dot(a, b, preferred_element_type=F32)


def _dot_nt(a, b):
    return lax.dot_general(a, b, (((1,), (1,)), ((), ())), preferred_element_type=F32)


def _dot_tn(a, b):
    return lax.dot_general(a, b, (((0,), (0,)), ((), ())), preferred_element_type=F32)


def _sigmoid(z):
    return 1.0 / (1.0 + jnp.exp(-z))


def _params(n_grid, vmem=VMEM_LIMIT):
    return pltpu.CompilerParams(dimension_semantics=("arbitrary",) * n_grid, vmem_limit_bytes=vmem)


def _position():
    return lax.axis_index("x"), lax.axis_index("y"), lax.axis_index("c")


def _flip(pos, k):
    x, y, c = pos
    return (1 - x if k & 4 else x, 1 - y if k & 2 else y, 1 - c if k & 1 else c)


def _slot(pos):
    return 4 * pos[0] + 2 * pos[1] + pos[2]


def _direct_gather_copies(src_hbm, all_hbm, send_sems, recv_sems, arriving):
    me = _position()
    return [
        pltpu.make_async_remote_copy(
            src_ref=src_hbm,
            dst_ref=all_hbm.at[_slot(_flip(me, k) if arriving else me)],
            send_sem=send_sems.at[k - 1],
            recv_sem=recv_sems.at[k - 1],
            device_id=_flip(me, k),
            device_id_type=MESH,
        )
        for k in range(1, N_DEV)
    ]


CHIP_FLIPS = (2, 4, 6)


def _chip_of(pos):
    return 2 * pos[0] + pos[1]


def _chip_copies(srcs, lands, views, send_sems, recv_sems):
    me = _position()
    copies = []
    for a, (src, land) in enumerate(zip(srcs, lands)):
        for j, k in enumerate(CHIP_FLIPS):
            peer = _flip(me, k)
            copies.append(
                pltpu.make_async_remote_copy(
                    src_ref=views[a](src, peer),
                    dst_ref=land.at[j],
                    send_sem=send_sems.at[len(CHIP_FLIPS) * a + j],
                    recv_sem=recv_sems.at[len(CHIP_FLIPS) * a + j],
                    device_id=peer,
                    device_id_type=MESH,
                )
            )
    return copies


PACK_GATE = 3 * ROW_SHARD
PACK_ROWS = PACK_GATE + 2 * GATE_SHARD


def _fwd_in(x, c_rows, conv_rows, w_ada, b_cols, norm_g, w_shard, tables, pack, where):
    t = x.shape[0]
    tm = min(1024, t)
    tn = IN_W // 4
    n_i, n_j = t // tm, IN_W // tn
    n_rope = (SEG_V - SEG_Q) // 128

    n_ada = w_ada.shape[1]

    def body(where_ref, x_ref, wada_ref, bcols_ref, ng_ref, ct_ref, st_ref, shard_hbm, pack_hbm, c_hbm, conv_hbm,
             proj_ref, h_ref, mod_ref, lane_hbm, wt_hbm, all_hbm, call_hbm, convall_hbm,
             wall, hbuf, cbuf, mcols, mod_all, mod_buf, lane,
             w_send, w_recv, load_sems, local_sem, lane_sem, p_send, p_recv, p_local,
             c_send, c_recv, c_local, v_send, v_recv, v_local, m_send, m_recv):
        j = pl.program_id(0)
        i = pl.program_id(1)
        me = _position()
        sibling = _flip(me, 1)
        chip = where_ref[0]
        slot = where_ref[1]

        def modulation():
            own_c = pltpu.make_async_copy(c_hbm, call_hbm.at[_slot(me)], c_local)
            own_v = pltpu.make_async_copy(conv_hbm, convall_hbm.at[_slot(me)], v_local)
            own_c.start()
            own_v.start()
            for cp in _direct_gather_copies(c_hbm, call_hbm, c_send, c_recv, arriving=False):
                cp.start()
            for cp in _direct_gather_copies(conv_hbm, convall_hbm, v_send, v_recv, arriving=False):
                cp.start()
            for cp in _direct_gather_copies(c_hbm, call_hbm, c_send, c_recv, arriving=True):
                cp.wait_recv()
            own_c.wait()
            fetch = pltpu.make_async_copy(call_hbm, cbuf, c_local)
            fetch.start()
            fetch.wait()
            mcols[...] = _dot(cbuf[:, 0, :].astype(BF16), wada_ref[...].astype(BF16)) + bcols_ref[...]
            mod_all[slot] = mcols[...]
            swaps = lambda arriving: [
                pltpu.make_async_remote_copy(
                    src_ref=mcols,
                    dst_ref=mod_all.at[_slot(_flip(me, k) if arriving else me)],
                    send_sem=m_send.at[k - 1],
                    recv_sem=m_recv.at[k - 1],
                    device_id=_flip(me, k),
                    device_id_type=MESH,
                )
                for k in range(1, N_DEV)
            ]
            for cp in swaps(False):
                cp.start()
            for cp in swaps(True):
                cp.wait_recv()
            mine = lax.broadcasted_iota(jnp.int32, (N_DEV, n_ada), 0) == slot
            for s in range(N_DEV):
                row = jnp.sum(jnp.where(mine, mod_all[s], 0.0), axis=0, keepdims=True)
                mod_buf[:, n_ada * s : n_ada * (s + 1)] = row
            for cp in swaps(False):
                cp.wait_send()
            for cp in _direct_gather_copies(c_hbm, call_hbm, c_send, c_recv, arriving=False):
                cp.wait_send()
            for cp in _direct_gather_copies(conv_hbm, convall_hbm, v_send, v_recv, arriving=True):
                cp.wait_recv()
            for cp in _direct_gather_copies(conv_hbm, convall_hbm, v_send, v_recv, arriving=False):
                cp.wait_send()
            own_v.wait()

        def shard_of(block):
            return wall.at[_chip_of(block), pl.ds(W_IN_SHARD * block[2], W_IN_SHARD), :]

        def shard_copy(k, block, to, src=None):
            return pltpu.make_async_remote_copy(
                src_ref=shard_of(block),
                dst_ref=shard_of(block),
                send_sem=w_send.at[k],
                recv_sem=w_recv.at[k],
                device_id=to,
                device_id_type=MESH,
            )

        own_shard = pltpu.make_async_copy(shard_hbm, shard_of(me), local_sem)
        own_pack = pltpu.make_async_copy(pack_hbm, all_hbm.at[_slot(me)], p_local)
        keep_lane = pltpu.make_async_copy(lane, lane_hbm, lane_sem)

        def lane_tables():
            d = lax.broadcasted_iota(jnp.int32, (128, 128), 0) % HEAD_DIM
            half = ROT_DIM // 2

            def fill(b, carry):
                cols = pl.ds(pl.multiple_of(128 * b, 128), 128)
                cos = jnp.concatenate([ct_ref[:, cols]] * (128 // half), axis=0)
                sin = jnp.concatenate([st_ref[:, cols]] * (128 // half), axis=0)
                lane[0, cols, :] = jnp.where(d < ROT_DIM, cos, 1.0).T
                lane[1, cols, :] = jnp.where(d < half, -sin, 0.0).T
                lane[2, cols, :] = jnp.where((d >= half) & (d < ROT_DIM), sin, 0.0).T
                return carry

            lax.fori_loop(0, t // 128, fill, 0)

        def keep_tile(q):
            return [
                pltpu.make_async_copy(wall.at[q, pl.ds(W_IN_SHARD * c, W_IN_SHARD), :], wt_hbm.at[2 * q + c], load_sems.at[2 * (q ^ chip) + c])
                for c in range(2)
            ]

        @pl.when((j == 0) & (i == 0))
        def _():
            modulation()
            mod_ref[...] = mod_buf[...]
            own_shard.start()
            own_shard.wait()
            shard_copy(0, me, sibling).start()
            for rel in range(1, 4):
                shard_copy(rel, me, _flip(me, 2 * rel)).start()
            own_pack.start()
            for cp in _direct_gather_copies(pack_hbm, all_hbm, p_send, p_recv, arriving=False):
                cp.start()
            lane_tables()
            keep_lane.start()
            shard_copy(0, sibling, me).wait_recv()
            for cp in keep_tile(chip):
                cp.start()

        for rel in range(1, 4):

            @pl.when((j == 1) & (i == min(rel, n_i) - 1))
            def _(rel=rel):
                there = _flip(me, 2 * rel)
                shard_copy(rel, there, me).wait_recv()
                shard_copy(3 + rel, there, sibling).start()

            @pl.when((j == rel) & (i == 0))
            def _(rel=rel):
                shard_copy(3 + rel, _flip(sibling, 2 * rel), me).wait_recv()
                for cp in keep_tile(chip ^ rel):
                    cp.start()

        rows = pl.ds(pl.multiple_of(i * tm, tm), tm)

        @pl.when(j == 0)
        def _():
            xv = x_ref[...]
            r = lax.rsqrt(jnp.mean(xv * xv, axis=-1, keepdims=True) + NORM_EPS)
            shift = mod_buf[:, 0:D]
            scale = mod_buf[:, D : 2 * D]
            hv = (((xv * r) * ng_ref[...]) * (1.0 + scale) + shift).astype(BF16)
            hbuf[rows, :] = hv
            h_ref[...] = hv

        proj_ref[...] = _dot_nt(hbuf[rows, :], wall[chip ^ j])

        @pl.when((chip ^ j) == 0)
        def _():
            cos, s1, s2 = lane[0, rows, :], lane[1, rows, :], lane[2, rows, :]
            for g in range(n_rope):
                cols = slice(128 * g, 128 * (g + 1))
                proj_ref[:, cols] = _rope(proj_ref[:, cols], cos, s1, s2)

        @pl.when((j == n_j - 1) & (i == n_i - 1))
        def _():
            shard_copy(0, me, sibling).wait_send()
            for rel in range(1, 4):
                shard_copy(rel, me, _flip(me, 2 * rel)).wait_send()
                shard_copy(3 + rel, _flip(me, 2 * rel), sibling).wait_send()
            for q in range(N_DEV // 2):
                for cp in keep_tile(q):
                    cp.wait()
            for cp in _direct_gather_copies(pack_hbm, all_hbm, p_send, p_recv, arriving=True):
                cp.wait_recv()
            for cp in _direct_gather_copies(pack_hbm, all_hbm, p_send, p_recv, arriving=False):
                cp.wait_send()
            own_pack.wait()
            keep_lane.wait()

    tab = pl.BlockSpec((ROT_DIM // 2, t), lambda j, i, c: (0, 0))
    any_spec = pl.BlockSpec(memory_space=pl.ANY)
    return pl.pallas_call(
        body,
        name="fwd_in",
        grid_spec=pltpu.PrefetchScalarGridSpec(
            num_scalar_prefetch=1,
            grid=(n_j, n_i),
            in_specs=[
                pl.BlockSpec((tm, D), lambda j, i, c: (jnp.where(j == 0, i, 0), 0)),
                pl.BlockSpec(w_ada.shape, lambda j, i, c: (0, 0)),
                pl.BlockSpec((1, n_ada), lambda j, i, c: (0, c[1])),
                pl.BlockSpec((1, D), lambda j, i, c: (0, 0)),
                tab, tab, any_spec, any_spec, any_spec, any_spec,
            ],
            out_specs=[pl.BlockSpec((tm, tn), lambda j, i, c: (i, c[0] ^ j)),
                       pl.BlockSpec((tm, D), lambda j, i, c: (jnp.where(j == 0, i, n_i - 1), 0)),
                       pl.BlockSpec((1, 3 * D), lambda j, i, c: (0, 0)),
                       any_spec, any_spec, any_spec, any_spec, any_spec],
            scratch_shapes=[
                pltpu.VMEM((N_DEV // 2, tn, D), BF16),
                pltpu.VMEM((t, D), BF16),
                pltpu.VMEM((N_DEV,) + c_rows.shape, F32),
                pltpu.VMEM((N_DEV, n_ada), F32),
                pltpu.VMEM((N_DEV, N_DEV, n_ada), F32),
                pltpu.VMEM((1, 3 * D), F32),
                pltpu.VMEM((3, t, 128), F32),
                pltpu.SemaphoreType.DMA((N_DEV - 1,)),
                pltpu.SemaphoreType.DMA((N_DEV - 1,)),
                pltpu.SemaphoreType.DMA((N_DEV,)),
                pltpu.SemaphoreType.DMA,
                pltpu.SemaphoreType.DMA,
                pltpu.SemaphoreType.DMA((N_DEV - 1,)),
                pltpu.SemaphoreType.DMA((N_DEV - 1,)),
                pltpu.SemaphoreType.DMA,
                pltpu.SemaphoreType.DMA((N_DEV - 1,)),
                pltpu.SemaphoreType.DMA((N_DEV - 1,)),
                pltpu.SemaphoreType.DMA,
                pltpu.SemaphoreType.DMA((N_DEV - 1,)),
                pltpu.SemaphoreType.DMA((N_DEV - 1,)),
                pltpu.SemaphoreType.DMA,
                pltpu.SemaphoreType.DMA((N_DEV - 1,)),
                pltpu.SemaphoreType.DMA((N_DEV - 1,)),
            ],
        ),
        out_shape=[jax.ShapeDtypeStruct((t, IN_W), F32), jax.ShapeDtypeStruct((t, D), BF16),
                   jax.ShapeDtypeStruct((1, 3 * D), F32),
                   jax.ShapeDtypeStruct((3, t, 128), F32),
                   jax.ShapeDtypeStruct((N_DEV,) + w_shard.shape, w_shard.dtype),
                   jax.ShapeDtypeStruct((N_DEV,) + pack.shape, pack.dtype),
                   jax.ShapeDtypeStruct((N_DEV,) + c_rows.shape, F32),
                   jax.ShapeDtypeStruct((N_DEV,) + conv_rows.shape, F32)],
        compiler_params=_params(2),
    )(where, x, w_ada, b_cols, norm_g, *tables, w_shard, pack, c_rows, conv_rows)


def _rope_tables(positions):
    inv_freq = ROPE_THETA ** (-jnp.arange(0, ROT_DIM, 2, dtype=F32) / ROT_DIM)
    ang_t = inv_freq[:, None] * positions.astype(F32)[None, :]
    return jnp.cos(ang_t), jnp.sin(ang_t)


def _rope(tv, cos, s1, s2):
    cols = []
    for g in range(tv.shape[1] // 128):
        blk = tv[:, 128 * g : 128 * (g + 1)]
        cols.append(blk * cos + pltpu.roll(blk, 128 - 8, 1) * s1 + pltpu.roll(blk, 8, 1) * s2)
    return cols[0] if len(cols) == 1 else jnp.concatenate(cols, axis=1)


def _rope_transposed(gv, cos, s1, s2):
    cols = []
    for g in range(gv.shape[1] // 128):
        blk = gv[:, 128 * g : 128 * (g + 1)]
        cols.append(blk * cos + pltpu.roll(blk * s1, 8, 1) + pltpu.roll(blk * s2, 128 - 8, 1))
    return cols[0] if len(cols) == 1 else jnp.concatenate(cols, axis=1)


def _lane_lo():
    return lax.broadcasted_iota(jnp.int32, (1, 128), 1) < HEAD_DIM


def _row_lo():
    return lax.broadcasted_iota(jnp.int32, (128, 1), 0) < HEAD_DIM


def _split_heads(tile, j):
    lo = _lane_lo()
    parts = []
    for g in (2 * j, 2 * j + 1):
        blk = tile[:, 128 * g : 128 * (g + 1)]
        parts += [jnp.where(lo, blk, 0.0), jnp.where(lo, 0.0, blk)]
    return jnp.concatenate(parts, axis=0)


def _dup_lanes(kv, j):
    rolled = pltpu.roll(kv, HEAD_DIM, 1)
    lo = _lane_lo()
    return jnp.where(lo, kv, rolled) if j == 0 else jnp.where(lo, rolled, kv)


def _fold_lanes(acc):
    return acc + pltpu.roll(acc, HEAD_DIM, 1)


def _fill_bias_t(bias_ref):
    kj = lax.broadcasted_iota(jnp.int32, (256, 128), 0)
    qi = lax.broadcasted_iota(jnp.int32, (256, 128), 1)
    diff = qi + WINDOW - kj
    band = (diff >= 0) & (diff < WINDOW)
    bias_ref[0] = jnp.where(band, 0.0, -1e30)
    bias_ref[1] = jnp.where(band & (kj >= 128), 0.0, -1e30)


def _sink_row(sink_ref, pair, j):
    return jnp.concatenate([jnp.full((1, 128), sink_ref[0, 8 * pair + 4 * j + h], F32) for h in range(4)], axis=1)


def _softmax_t(q4, kw, bias4, sink):
    s = _dot_nt(kw, q4) + bias4
    m = jnp.maximum(jnp.max(s, axis=0, keepdims=True), sink)
    p = jnp.exp(s - m)
    e_sink = jnp.exp(sink - m)
    denom = jnp.sum(p, axis=0, keepdims=True) + e_sink
    return p, e_sink, denom


def _attn_specs(t, tq):
    return dict(
        q=pl.BlockSpec((tq, 512), lambda p, i: (i, p)),
        k=pl.BlockSpec((t, 128), lambda p, i: (0, SEG_K // 128 + p)),
        v=pl.BlockSpec((t, 128), lambda p, i: (0, SEG_V // 128 + p)),
        sink=pl.BlockSpec(memory_space=pltpu.SMEM),
        act=pl.BlockSpec((tq, 512), lambda p, i: (i, p)),
    )


def _attn_fwd(proj, sinks):
    t = proj.shape[0]
    tq = min(512, t)
    nb = tq // 128
    n_tiles = t // tq
    sp = _attn_specs(t, tq)

    def body(q_ref, k_ref, v_ref, sink_ref, o_ref, kd, vdt, bias_ref):
        pair = pl.program_id(0)
        i = pl.program_id(1)

        @pl.when(i == 0)
        def _():
            kv, vv = k_ref[...], v_ref[...]
            for j in range(2):
                kd[j, 0:128, :] = jnp.zeros((128, 128), BF16)
                kd[j, 128:, :] = _dup_lanes(kv, j).astype(BF16)
                vdt[j, :, 0:128] = jnp.zeros((128, 128), BF16)
                vdt[j, :, 128:] = _dup_lanes(vv, j).T.astype(BF16)
            _fill_bias_t(bias_ref)

        row_lo = _row_lo()

        def block(b, carry):
            r0 = pl.multiple_of(b * 128, 128)
            gb = i * nb + b
            keys = pl.ds(pl.multiple_of(gb * 128, 128), 256)
            rows = pl.ds(r0, 128)
            qt = q_ref[rows, :] * (1.0 / math.sqrt(HEAD_DIM))
            bias = bias_ref[jnp.where(gb == 0, 1, 0)]
            bias4 = jnp.concatenate([bias] * 4, axis=1)
            cols = []
            for j in range(2):
                q4 = _split_heads(qt, j).astype(BF16)
                p, _, denom = _softmax_t(q4, kd[j, keys, :], bias4, _sink_row(sink_ref, pair, j))
                ot = _dot(vdt[j, :, keys], p.astype(BF16)) * (1.0 / denom)
                for gi in range(2):
                    z = jnp.where(row_lo, ot[:, 256 * gi : 256 * gi + 128], ot[:, 256 * gi + 128 : 256 * gi + 256])
                    cols.append(z.T)
            o_ref[rows, :] = jnp.concatenate(cols, axis=1).astype(BF16)
            return carry

        lax.fori_loop(0, nb, block, 0, unroll=True)

    return pl.pallas_call(
        body,
        name="attn_fwd",
        grid=(2, n_tiles),
        in_specs=[sp["q"], sp["k"], sp["v"], sp["sink"]],
        out_specs=sp["act"],
        out_shape=jax.ShapeDtypeStruct((t, D), BF16),
        scratch_shapes=[pltpu.VMEM((2, t + 128, 128), BF16), pltpu.VMEM((2, 128, t + 128), BF16), pltpu.VMEM((2, 256, 128), F32)],
        compiler_params=_params(2),
    )(proj, proj, proj, sinks)


def _attn_bwd(proj, o, d_o, tables, sinks, part_3):
    t = proj.shape[0]
    tq = min(512, t)
    nb = tq // 128
    n_tiles = t // tq
    lane_tables, (cos_t, sin_t) = tables
    sp = _attn_specs(t, tq)
    scale = 1.0 / math.sqrt(HEAD_DIM)
    half = ROT_DIM // 2

    def unrope_rows(z, c, s):
        out = []
        for base in (0, HEAD_DIM):
            g0, g1 = z[base : base + half], z[base + half : base + ROT_DIM]
            out += [g0 * c + g1 * s, g1 * c - g0 * s, z[base + ROT_DIM : base + HEAD_DIM]]
        return jnp.concatenate(out, axis=0)

    def swap(part_hbm, land, send_sem, recv_sem):
        sibling = _flip(_position(), 1)
        return pltpu.make_async_remote_copy(
            src_ref=part_hbm.at[:, :, sibling[2]], dst_ref=land, send_sem=send_sem, recv_sem=recv_sem,
            device_id=sibling, device_id_type=MESH,
        )

    def body(q_ref, k_ref, v_ref, o_ref, do_ref, ct_ref, st_ref, ck, s1k, s2k, sink_ref, part_hbm,
             dq_ref, dk_ref, dv_ref, dsink_ref, sum_hbm, mine_hbm, kd, kdt, vd, bias_ref, dkacc, dvacc, own, land, stage,
             send_sem, recv_sem, local_sems):
        pair = pl.program_id(0)
        i = pl.program_id(1)
        me = _position()
        load_own = pltpu.make_async_copy(part_hbm.at[:, :, me[2]], own, local_sems.at[0])

        @pl.when((pair == 0) & (i == 0))
        def _():
            swap(part_hbm, land, send_sem, recv_sem).start()
            load_own.start()

        @pl.when(i == 0)
        def _():
            kv, vv = k_ref[...], v_ref[...]
            for j in range(2):
                kdup = _dup_lanes(kv, j)
                kd[j, 0:128, :] = jnp.zeros((128, 128), BF16)
                kd[j, 128:, :] = kdup.astype(BF16)
                kdt[j, :, 0:128] = jnp.zeros((128, 128), BF16)
                kdt[j, :, 128:] = kdup.T.astype(BF16)
                vd[j, 0:128, :] = jnp.zeros((128, 128), BF16)
                vd[j, 128:, :] = _dup_lanes(vv, j).astype(BF16)
            _fill_bias_t(bias_ref)
            dkacc[...] = jnp.zeros_like(dkacc)
            dvacc[...] = jnp.zeros_like(dvacc)
            dsink_ref[...] = jnp.zeros_like(dsink_ref)

        row_lo = _row_lo()
        ones = jnp.ones((8, 128), BF16)

        def block(b, carry):
            r0 = pl.multiple_of(b * 128, 128)
            gb = i * nb + b
            keys = pl.ds(pl.multiple_of(gb * 128, 128), 256)
            rows = pl.ds(r0, 128)
            qt = q_ref[rows, :] * scale
            dob = do_ref[rows, :].astype(F32)
            prod = dob * o_ref[rows, :].astype(F32)
            ct, st = ct_ref[:, rows], st_ref[:, rows]
            bias = bias_ref[jnp.where(gb == 0, 1, 0)]
            bias4 = jnp.concatenate([bias] * 4, axis=1)
            cols = []
            for j in range(2):
                q4 = _split_heads(qt, j).astype(BF16)
                kw = kd[j, keys, :]
                p, e_sink, denom = _softmax_t(q4, kw, bias4, _sink_row(sink_ref, pair, j))
                inv = 1.0 / denom
                prob = p * inv
                do4 = _split_heads(dob, j).astype(BF16)
                delta = _dot_nt(ones, _split_heads(prod, j).astype(BF16))[0:1, :]
                dp = _dot_nt(vd[j, keys, :], do4)
                ds = (prob * (dp - delta)).astype(BF16)
                dkacc[j, keys, :] = dkacc[j, keys, :] + _dot(ds, q4)
                dvacc[j, keys, :] = dvacc[j, keys, :] + _dot(prob.astype(BF16), do4)
                dqt = _dot(kdt[j, :, keys], ds) * scale
                dsink = -(e_sink * inv) * delta
                for h in range(4):
                    row = 4 * j + h
                    total = jnp.sum(dsink[:, 128 * h : 128 * (h + 1)], axis=1, keepdims=True)
                    dsink_ref[0, row : row + 1, :] = dsink_ref[0, row : row + 1, :] + jnp.broadcast_to(total, (1, 128))
                for gi in range(2):
                    z = jnp.where(row_lo, dqt[:, 256 * gi : 256 * gi + 128], dqt[:, 256 * gi + 128 : 256 * gi + 256])
                    cols.append(unrope_rows(z, ct, st).T)
            dq_ref[rows, :] = jnp.concatenate(cols, axis=1).astype(BF16)
            return carry

        lax.fori_loop(0, nb, block, 0, unroll=True)

        @pl.when(i == n_tiles - 1)
        def _():
            lo = _lane_lo()
            dk = jnp.where(lo, _fold_lanes(dkacc[0, 128:, :]), _fold_lanes(dkacc[1, 128:, :]))
            dk_ref[...] = _rope_transposed(dk, ck[...], s1k[...], s2k[...]).astype(BF16)
            dv_ref[...] = jnp.where(lo, _fold_lanes(dvacc[0, 128:, :]), _fold_lanes(dvacc[1, 128:, :])).astype(BF16)

        @pl.when((pair == 1) & (i == n_tiles - 1))
        def _():
            swap(part_hbm, land, send_sem, recv_sem).wait()
            load_own.wait()
            own[...] = own[...] + land[...]
            stage[...] = own[...].astype(BF16)
            results = [pltpu.make_async_copy(stage, sum_hbm, local_sems.at[1]),
                       pltpu.make_async_copy(own.at[:, _chip_of(me)], mine_hbm, local_sems.at[2])]
            for cp in results:
                cp.start()
            for cp in results:
                cp.wait()

    kv_out = pl.BlockSpec((t, 128), lambda p, i: (0, p))
    tab_t = pl.BlockSpec((half, tq), lambda p, i: (0, i))
    tab_k = [pl.BlockSpec((None, t, 128), lambda p, i, a=a: (a, 0, 0)) for a in range(3)]
    return pl.pallas_call(
        body,
        name="attn_bwd",
        grid=(2, n_tiles),
        in_specs=[sp["q"], sp["k"], sp["v"], sp["act"], sp["act"], tab_t, tab_t, *tab_k, sp["sink"],
                  pl.BlockSpec(memory_space=pl.ANY)],
        out_specs=[sp["act"], kv_out, kv_out, pl.BlockSpec((1, 8, 128), lambda p, i: (p, 0, 0)),
                   pl.BlockSpec(memory_space=pl.ANY), pl.BlockSpec(memory_space=pl.ANY)],
        out_shape=[
            jax.ShapeDtypeStruct((t, D), BF16),
            jax.ShapeDtypeStruct((t, 256), BF16),
            jax.ShapeDtypeStruct((t, 256), BF16),
            jax.ShapeDtypeStruct((2, 8, 128), F32),
            jax.ShapeDtypeStruct((3, 4, ROW_SHARD, D), BF16),
            jax.ShapeDtypeStruct((3, ROW_SHARD, D), F32),
        ],
        scratch_shapes=[pltpu.VMEM((2, t + 128, 128), BF16), pltpu.VMEM((2, 128, t + 128), BF16), pltpu.VMEM((2, t + 128, 128), BF16),
                        pltpu.VMEM((2, 256, 128), F32), pltpu.VMEM((2, t + 128, 128), F32), pltpu.VMEM((2, t + 128, 128), F32),
                        pltpu.VMEM((3, 4, ROW_SHARD, D), F32), pltpu.VMEM((3, 4, ROW_SHARD, D), F32), pltpu.VMEM((3, 4, ROW_SHARD, D), BF16),
                        pltpu.SemaphoreType.DMA, pltpu.SemaphoreType.DMA, pltpu.SemaphoreType.DMA((3,))],
        compiler_params=_params(2),
    )(proj, proj, proj, o, d_o, cos_t, sin_t, lane_tables, lane_tables, lane_tables, sinks, part_3)


HALF = 512


def _softplus(y):
    u = jnp.exp(-jnp.abs(y))
    w = 1.0 + u
    log1p = jnp.where(w == 1.0, u, jnp.log(w) * (u / jnp.where(w == 1.0, 1.0, w - 1.0)))
    return jnp.maximum(y, 0.0) + log1p


def _gate_block(w_ref, blk):
    return w_ref[:, :, 256 * blk : 256 * (blk + 1)].reshape(256, 256)


def _gate_matmul(act_bf, w_ref):
    return jnp.concatenate([_dot(act_bf[:, 256 * blk : 256 * (blk + 1)], _gate_block(w_ref, blk)) for blk in range(2)], axis=1)


def _gate_matmul_t(grad_bf, w_ref):
    return jnp.concatenate([_dot_nt(grad_bf[:, 256 * blk : 256 * (blk + 1)], _gate_block(w_ref, blk)) for blk in range(2)], axis=1)


def _conv_and_gates(xe_ref, tm, cw_ref, cb_ref, wa_ref, wx_ref, ba_ref, bx_ref, lam_ref, keep):
    xc = cb_ref[...] + sum(cw_ref[k : k + 1, :] * xe_ref[pl.ds(5 + k, tm), :] for k in range(CONV_WIDTH))
    xc_bf = xc.astype(BF16)
    r = _sigmoid(_gate_matmul(xc_bf, wa_ref) + ba_ref[...])
    ig = _sigmoid(_gate_matmul(xc_bf, wx_ref) + bx_ref[...])
    sp = _softplus(-lam_ref[...])
    log_a = -LRU_C * r * sp
    tanh = jnp.tanh(-log_a)
    mult = jnp.sqrt(2.0 * tanh / (1.0 + tanh))
    a = jnp.where(keep, jnp.exp(log_a), 0.0)
    mult = jnp.where(keep, mult, 1.0)
    return xc, xc_bf, r, ig, sp, a, mult


def _scan_rows(a_ref, b_ref, h_ref, carry_ref, tm, reverse):
    width = a_ref.shape[1]
    row = lax.broadcasted_iota(jnp.int32, (8, width), 0)

    def group(g, carry):
        gg = tm // 8 - 1 - g if reverse else g
        r0 = pl.multiple_of(gg * 8, 8)
        a = a_ref[pl.ds(r0, 8), :]
        b = b_ref[pl.ds(r0, 8), :]
        for s in (1, 2, 4):
            shift = 8 - s if reverse else s
            a_sh = pltpu.roll(a, shift, 0)
            b_sh = pltpu.roll(b, shift, 0)
            ok = (row < 8 - s) if reverse else (row >= s)
            b = jnp.where(ok, a * b_sh + b, b)
            a = jnp.where(ok, a * a_sh, a)
        h = a * carry + b
        h_ref[pl.ds(r0, 8), :] = h
        last = h[0:1, :] if reverse else h[7:8, :]
        return jnp.broadcast_to(last, (8, width))

    carry_ref[...] = lax.fori_loop(0, tm // 8, group, carry_ref[...])


def _rnn_specs(t, tm, order):
    n = t // tm
    return dict(
        tile=pl.BlockSpec((tm, HALF), lambda ch, i: (order(i, n), ch)),
        before=lambda col0: pl.BlockSpec((8, HALF), lambda ch, i: (jnp.maximum(order(i, n) * (tm // 8) - 1, 0), col0 + ch)),
        proj_tile=lambda col0: pl.BlockSpec((tm, HALF), lambda ch, i: (order(i, n), col0 + ch)),
        keep=pl.BlockSpec((tm, 1), lambda ch, i: (order(i, n), 0)),
        gates=lambda which: pl.BlockSpec((N_DEV, GATE_SHARD, HALF), lambda ch, i: (0, PACK_GATE // GATE_SHARD + which, ch)),
        gate_grads=pl.BlockSpec((2, 2, 256, 256), lambda ch, i: (0, ch, 0, 0)),
        vec=pl.BlockSpec((1, HALF), lambda ch, i: (0, ch)),
        conv=pl.BlockSpec((CONV_WIDTH, HALF), lambda ch, i: (0, ch)),
    )


def _rnn_fwd(proj, keep, w_all, rg_ba, rg_bx, rg_lambda, conv_w, conv_b):
    t = proj.shape[0]
    tm = min(512, t)
    sp = _rnn_specs(t, tm, lambda i, n: i)

    def body(xr_ref, xb_ref, keep_ref, wa_ref, wx_ref, ba_ref, bx_ref, lam_ref, cw_ref, cb_ref, hh_ref, xe, a_buf, b_buf, carry):
        i = pl.program_id(1)

        @pl.when(i == 0)
        def _():
            carry[...] = jnp.zeros_like(carry)

        xe[0:8, :] = jnp.where(i > 0, xb_ref[...], 0.0)
        xe[8:, :] = xr_ref[...]
        keep_rows = keep_ref[...] != 0.0
        xc, _, _, ig, _, a, mult = _conv_and_gates(xe, tm, cw_ref, cb_ref, wa_ref, wx_ref, ba_ref, bx_ref, lam_ref, keep_rows)
        a_buf[...] = a
        b_buf[...] = mult * (ig * xc)
        _scan_rows(a_buf, b_buf, hh_ref, carry, tm, reverse=False)

    return pl.pallas_call(
        body,
        name="rnn_fwd",
        grid=(2, t // tm),
        in_specs=[sp["proj_tile"](SEG_XR // HALF), sp["before"](SEG_XR // HALF), sp["keep"], sp["gates"](0), sp["gates"](1),
                  sp["vec"], sp["vec"], sp["vec"], sp["conv"], sp["vec"]],
        out_specs=sp["tile"],
        out_shape=jax.ShapeDtypeStruct((t, D), F32),
        scratch_shapes=[pltpu.VMEM((tm + 8, HALF), F32), pltpu.VMEM((tm, HALF), F32), pltpu.VMEM((tm, HALF), F32), pltpu.VMEM((8, HALF), F32)],
        compiler_params=_params(2),
    )(proj, proj, keep, w_all, w_all, rg_ba, rg_bx, rg_lambda, conv_w, conv_b)


def _rnn_bwd(proj, hh, d_hh, keep, w_all, rg_ba, rg_bx, rg_lambda, conv_w, conv_b, sum_3_bf):
    t = proj.shape[0]
    tm = min(512, t)
    n_tiles = t // tm
    sp = _rnn_specs(t, tm, lambda i, n: n - 1 - i)

    def body(xr_ref, xb_ref, hh_ref, hb_ref, dhh_ref, keep_ref, wa_ref, wx_ref, ba_ref, bx_ref, lam_ref, cw_ref, cb_ref,
             s3_hbm, dxr_ref, gw_ref, gvec_ref, l3_hbm, xe, he, ae, c_buf, g_buf, dxe, carry, a_next, send_sems, recv_sems):
        ch = pl.program_id(0)
        i = pl.program_id(1)
        first_tile = i == n_tiles - 1
        views = [lambda ref, peer: ref.at[:, _chip_of(peer)]]

        @pl.when((ch == 0) & (i == 0))
        def _():
            for cp in _chip_copies([s3_hbm], [l3_hbm], views, send_sems, recv_sems):
                cp.start()

        @pl.when((ch == 1) & (i == n_tiles - 1))
        def _():
            for cp in _chip_copies([s3_hbm], [l3_hbm], views, send_sems, recv_sems):
                cp.wait_recv()
                cp.wait_send()

        @pl.when(i == 0)
        def _():
            carry[...] = jnp.zeros_like(carry)
            a_next[...] = jnp.zeros_like(a_next)
            dxe[tm:, :] = jnp.zeros((8, HALF), F32)
            gw_ref[...] = jnp.zeros_like(gw_ref)
            gvec_ref[...] = jnp.zeros_like(gvec_ref)

        xe[0:8, :] = jnp.where(first_tile, 0.0, xb_ref[...])
        xe[8:, :] = xr_ref[...]
        he[0:8, :] = jnp.where(first_tile, 0.0, hb_ref[...])
        he[8:, :] = hh_ref[...]
        keep_rows = keep_ref[...] != 0.0
        xc, xc_bf, r, ig, sp_, a, mult = _conv_and_gates(xe, tm, cw_ref, cb_ref, wa_ref, wx_ref, ba_ref, bx_ref, lam_ref, keep_rows)

        ae[0:tm, :] = a
        ae[tm:, :] = a_next[...]
        a_next[...] = jnp.broadcast_to(a[0:1, :], (8, HALF))
        c_buf[...] = ae[pl.ds(1, tm), :]
        g_buf[...] = dhh_ref[...]
        _scan_rows(c_buf, g_buf, g_buf, carry, tm, reverse=True)
        g = g_buf[...]

        h_prev = he[pl.ds(7, tm), :]
        bb = ig * xc
        d_ig = g * mult * xc
        d_xc = g * mult * ig
        d_log_a = jnp.where(keep_rows, g * h_prev * a - (g * bb) * (a * a) / mult, 0.0)
        d_r = d_log_a * (-LRU_C * sp_)
        d_sp = jnp.sum(d_log_a * (-LRU_C * r), axis=0, keepdims=True)
        d_zr = d_r * r * (1.0 - r)
        d_zi = d_ig * ig * (1.0 - ig)
        d_zr_bf = d_zr.astype(BF16)
        d_zi_bf = d_zi.astype(BF16)
        d_xc = d_xc + _gate_matmul_t(d_zr_bf, wa_ref) + _gate_matmul_t(d_zi_bf, wx_ref)
        for blk in range(2):
            cols = slice(256 * blk, 256 * (blk + 1))
            gw_ref[0, blk] = gw_ref[0, blk] + _dot_tn(xc_bf[:, cols], d_zr_bf[:, cols])
            gw_ref[1, blk] = gw_ref[1, blk] + _dot_tn(xc_bf[:, cols], d_zi_bf[:, cols])

        dxe[0:tm, :] = d_xc
        shifted = [dxe[pl.ds(3 - k, tm), :] for k in range(CONV_WIDTH)]
        dxr = sum(cw_ref[k : k + 1, :] * shifted[k] for k in range(CONV_WIDTH))
        dxr_ref[...] = dxr.astype(BF16)
        dxe[tm:, :] = d_xc[0:8, :]

        lam = lam_ref[...]
        xr_tile = xr_ref[...]
        sums = [
            jnp.sum(d_zr, axis=0, keepdims=True),
            jnp.sum(d_zi, axis=0, keepdims=True),
            d_sp * (-_sigmoid(-lam)),
            jnp.sum(d_xc, axis=0, keepdims=True),
        ] + [jnp.sum(xr_tile * shifted[k], axis=0, keepdims=True) for k in range(CONV_WIDTH)]
        gvec_ref[...] = gvec_ref[...] + jnp.concatenate(sums, axis=0)

    return pl.pallas_call(
        body,
        name="rnn_bwd",
        grid=(2, n_tiles),
        in_specs=[
            sp["proj_tile"](SEG_XR // HALF), sp["before"](SEG_XR // HALF), sp["tile"], sp["before"](0), sp["tile"], sp["keep"],
            sp["gates"](0), sp["gates"](1), sp["vec"], sp["vec"], sp["vec"], sp["conv"], sp["vec"], pl.BlockSpec(memory_space=pl.ANY),
        ],
        out_specs=[sp["tile"], sp["gate_grads"], pl.BlockSpec((8, HALF), lambda ch, i: (0, ch)), pl.BlockSpec(memory_space=pl.ANY)],
        out_shape=[
            jax.ShapeDtypeStruct((t, D), BF16),
            jax.ShapeDtypeStruct((2, 4, 256, 256), F32),
            jax.ShapeDtypeStruct((8, D), F32),
            jax.ShapeDtypeStruct((len(CHIP_FLIPS), 3, ROW_SHARD, D), BF16),
        ],
        scratch_shapes=[
            pltpu.VMEM((tm + 8, HALF), F32),
            pltpu.VMEM((tm + 8, HALF), F32),
            pltpu.VMEM((tm + 8, HALF), F32),
            pltpu.VMEM((tm, HALF), F32),
            pltpu.VMEM((tm, HALF), F32),
            pltpu.VMEM((tm + 8, HALF), F32),
            pltpu.VMEM((8, HALF), F32),
            pltpu.VMEM((8, HALF), F32),
            pltpu.SemaphoreType.DMA((len(CHIP_FLIPS),)),
            pltpu.SemaphoreType.DMA((len(CHIP_FLIPS),)),
        ],
        compiler_params=_params(2),
    )(proj, proj, hh, hh, d_hh, keep, w_all, w_all, rg_ba, rg_bx, rg_lambda, conv_w, conv_b, sum_3_bf)


def _mid(x, target, o, hh, proj, w_all, gate, final_g):
    t = x.shape[0]
    tm = min(256, t)
    n_tiles = t // tm
    full = pl.BlockSpec((tm, D), lambda i: (i, 0))
    half = lambda col: pl.BlockSpec((tm, 512), lambda i: (i, col))
    vec = pl.BlockSpec((1, D), lambda i: (0, 0))
    any_spec = pl.BlockSpec(memory_space=pl.ANY)

    def body(x_ref, tg_ref, o_ref, hh_ref, ga0, ga1, gr0, gr1, ma0, ma1, mr0, mr1, w_hbm, gate_ref, fg_ref,
             dres_ref, do_ref, dhh_ref, dga_ref, dgr_ref, dma_ref, dmr_ref, gw_hbm, vec_ref, w, gw, sem, w_sems):
        i = pl.program_id(0)

        @pl.when(i == 0)
        def _():
            loads = [
                pltpu.make_async_copy(
                    w_hbm.at[s, pl.ds(ROW_SHARD * k, ROW_SHARD), :], w.at[k, pl.ds(ROW_SHARD * s, ROW_SHARD), :], w_sems.at[k * N_DEV + s]
                )
                for k in range(3)
                for s in range(N_DEV)
            ]
            for cp in loads:
                cp.start()
            for cp in loads:
                cp.wait()
            gw[...] = jnp.zeros_like(gw)
            vec_ref[...] = jnp.zeros_like(vec_ref)

        keep = [pltpu.make_async_copy(gw.at[k], gw_hbm.at[k], sem.at[k]) for k in range(3)]

        def step(last):
            cat = lambda a, b: jnp.concatenate([a[...], b[...]], axis=1)
            ga, gr, ma, mr = cat(ga0, ga1), cat(gr0, gr1), cat(ma0, ma1), cat(mr0, mr1)
            ov, hv, gate_v, fg = o_ref[...].astype(F32), hh_ref[...], gate_ref[...], fg_ref[...]
            sg_a = _sigmoid(ga)
            sg_r = _sigmoid(gr)
            silu_a = ga * sg_a
            silu_r = gr * sg_r
            ya = (ov * silu_a).astype(BF16)
            yr = (hv * silu_r).astype(BF16)
            pa = _dot(ya, w[0])
            pr = _dot(yr, w[1])
            sa = _sigmoid(ma)
            sr = _sigmoid(mr)
            merged = (sa * pa + sr * pr).astype(BF16)
            mo = _dot(merged, w[2])
            out = x_ref[...] + gate_v * mo
            r2 = lax.rsqrt(jnp.mean(out * out, axis=-1, keepdims=True) + NORM_EPS)
            outn = out * r2
            err = outn * fg - tg_ref[...]
            loss = 0.5 * jnp.sum(jnp.mean(err * err, axis=-1, keepdims=True), axis=0, keepdims=True)
            dy = err * (1.0 / D)
            g_final = jnp.sum(dy * outn, axis=0, keepdims=True)
            doutn = dy * fg
            dout = r2 * (doutn - outn * jnp.mean(doutn * outn, axis=-1, keepdims=True))
            dres_ref[...] = dout
            d_gate = jnp.sum(dout * mo, axis=0, keepdims=True)
            dmo = (dout * gate_v).astype(BF16)
            dmerged = _dot_nt(dmo, w[2])
            gw[2] = gw[2] + _dot_tn(merged, dmo)
            if last:
                keep[2].start()
            dpa = (dmerged * sa).astype(BF16)
            dpr = (dmerged * sr).astype(BF16)
            dma_ref[...] = (dmerged * pa * (sa * (1.0 - sa))).astype(BF16)
            dmr_ref[...] = (dmerged * pr * (sr * (1.0 - sr))).astype(BF16)
            dya = _dot_nt(dpa, w[0])
            dyr = _dot_nt(dpr, w[1])
            gw[0] = gw[0] + _dot_tn(ya, dpa)
            gw[1] = gw[1] + _dot_tn(yr, dpr)
            if last:
                keep[0].start()
                keep[1].start()
            do_ref[...] = (dya * silu_a).astype(BF16)
            dhh_ref[...] = dyr * silu_r
            dga_ref[...] = (dya * ov * (sg_a * (1.0 + ga * (1.0 - sg_a)))).astype(BF16)
            dgr_ref[...] = (dyr * hv * (sg_r * (1.0 + gr * (1.0 - sg_r)))).astype(BF16)
            zeros = jnp.zeros((5, D), F32)
            vec_ref[...] = vec_ref[...] + jnp.concatenate([g_final, d_gate, jnp.broadcast_to(loss, (1, D)), zeros], axis=0)
            if last:
                for cp in keep:
                    cp.wait()

        @pl.when(i < n_tiles - 1)
        def _():
            step(False)

        @pl.when(i == n_tiles - 1)
        def _():
            step(True)

    bf_tile = jax.ShapeDtypeStruct((t, D), BF16)
    f_tile = jax.ShapeDtypeStruct((t, D), F32)
    return pl.pallas_call(
        body,
        name="mid",
        grid=(n_tiles,),
        in_specs=[full, full, full, full,
                  half(SEG_GA // 512), half(SEG_GA // 512 + 1), half(SEG_GR // 512), half(SEG_GR // 512 + 1),
                  half(SEG_MA // 512), half(SEG_MA // 512 + 1), half(SEG_MR // 512), half(SEG_MR // 512 + 1),
                  any_spec, pl.BlockSpec((1, D), lambda i: (0, 2)), vec],
        out_specs=[full, full, full, full, full, full, full, any_spec, pl.BlockSpec((8, D), lambda i: (0, 0))],
        out_shape=[f_tile, bf_tile, f_tile, bf_tile, bf_tile, bf_tile, bf_tile,
                   jax.ShapeDtypeStruct((3, D, D), F32), jax.ShapeDtypeStruct((8, D), F32)],
        scratch_shapes=[pltpu.VMEM((3, D, D), BF16), pltpu.VMEM((3, D, D), F32), pltpu.SemaphoreType.DMA((3,)), pltpu.SemaphoreType.DMA((3 * N_DEV,))],
        compiler_params=_params(1),
    )(x, target, o, hh, proj, proj, proj, proj, proj, proj, proj, proj, w_all, gate, final_g)


_SEGMENTS = ((SEG_Q, 1024), (SEG_K, 256), (SEG_V, 256), (SEG_GA, 1024), (SEG_XR, 1024), (SEG_GR, 1024), (SEG_MA, 1024), (SEG_MR, 1024))


def _seg_specs(tm):
    return [pl.BlockSpec((tm, width), lambda i: (i, 0)) for _, width in _SEGMENTS]


def _bwd_dh(dsegs, w_all, x, dres, mod, norm_g, token):
    t = x.shape[0]
    tm = min(512, t)
    n_tiles = t // tm
    n_seg = len(_SEGMENTS)
    full = pl.BlockSpec((tm, D), lambda i: (i, 0))

    def body(*refs):
        segs = refs[:n_seg]
        w_hbm, x_ref, dres_ref, mod_ref, ng_ref, token_ref, gx_ref, vec_ref, w, sems = refs[n_seg:]
        i = pl.program_id(0)

        @pl.when(i == 0)
        def _():
            loads = [
                pltpu.make_async_copy(w_hbm.at[s], w.at[pl.ds(W_IN_SHARD * s, W_IN_SHARD), :], sems.at[s])
                for s in range(N_DEV)
            ]
            for cp in loads:
                cp.start()
            for cp in loads:
                cp.wait()
            vec_ref[...] = jnp.zeros_like(vec_ref)

        dh = sum(_dot(seg[...], w[start : start + width, :]) for seg, (start, width) in zip(segs, _SEGMENTS))
        xv = x_ref[...]
        r1 = lax.rsqrt(jnp.mean(xv * xv, axis=-1, keepdims=True) + NORM_EPS)
        xn = xv * r1
        ng = ng_ref[...]
        scale1 = 1.0 + mod_ref[:, D : 2 * D]
        d_shift = jnp.sum(dh, axis=0, keepdims=True)
        d_scale = jnp.sum(dh * (xn * ng), axis=0, keepdims=True)
        g_norm = jnp.sum(dh * xn * scale1, axis=0, keepdims=True)
        dxn = dh * (ng * scale1)
        dx = r1 * (dxn - xn * jnp.mean(dxn * xn, axis=-1, keepdims=True))
        gx_ref[...] = dres_ref[...] + dx
        vec_ref[...] = vec_ref[...] + jnp.concatenate([d_shift, d_scale, g_norm, jnp.zeros((5, D), F32)], axis=0)

    any_spec = pl.BlockSpec(memory_space=pl.ANY)
    return pl.pallas_call(
        body,
        name="bwd_dh",
        grid=(n_tiles,),
        in_specs=_seg_specs(tm) + [any_spec, full, full, pl.BlockSpec((1, 3 * D), lambda i: (0, 0)), pl.BlockSpec((1, D), lambda i: (0, 0)),
                                   pl.BlockSpec(token.shape, lambda i: (0, 0))],
        out_specs=[full, pl.BlockSpec((8, D), lambda i: (0, 0))],
        out_shape=[jax.ShapeDtypeStruct((t, D), F32), jax.ShapeDtypeStruct((8, D), F32)],
        scratch_shapes=[pltpu.VMEM((IN_W, D), BF16), pltpu.SemaphoreType.DMA((N_DEV,))],
        compiler_params=_params(1),
    )(*dsegs, w_all, x, dres, mod, norm_g, token)


_HBM = pl.BlockSpec(memory_space=pltpu.HBM)
_SEM = pl.BlockSpec(memory_space=pltpu.SEMAPHORE)
_EFFECT = pltpu.SideEffectType.DATAFLOW_SIDE_EFFECTING


def _w_exchange_copies(sum_ref, land_ref, sems):
    me = _position()
    n = len(CHIP_FLIPS)
    remote = [
        pltpu.make_async_remote_copy(
            src_ref=sum_ref.at[_chip_of(_flip(me, k))], dst_ref=land_ref.at[j], send_sem=sems[j], recv_sem=sems[n + j],
            device_id=_flip(me, k), device_id_type=MESH,
        )
        for j, k in enumerate(CHIP_FLIPS)
    ]
    return remote, []


def _vec_exchange_copies(vec_ref, land_ref, sems):
    me = _position()
    n = N_DEV - 1
    remote = [
        pltpu.make_async_remote_copy(
            src_ref=vec_ref, dst_ref=land_ref.at[_slot(me)], send_sem=sems[k - 1], recv_sem=sems[n + k - 1],
            device_id=_flip(me, k), device_id_type=MESH,
        )
        for k in range(1, N_DEV)
    ]
    return remote, [pltpu.make_async_copy(vec_ref, land_ref.at[_slot(me)], sems[2 * n])]


def _exchange_start(name, copies, n_sems, src, land_shape):
    land = lax.empty(land_shape, src.dtype)

    def body(src_ref, land_ref, *outs):
        remote, local = copies(src_ref, land_ref, outs[:n_sems])
        for cp in remote + local:
            cp.start()
        token = outs[n_sems + 2]
        token[...] = jnp.zeros_like(token)

    res = pl.pallas_call(
        body,
        name=name,
        out_shape=(*[pltpu.SemaphoreType.DMA(())] * n_sems, pltpu.HBM(src.shape, src.dtype), pltpu.HBM(land.shape, src.dtype),
                   jax.ShapeDtypeStruct((8, 128), F32)),
        in_specs=(_HBM, _HBM),
        out_specs=(*[_SEM] * n_sems, _HBM, _HBM, pl.BlockSpec(memory_space=pltpu.VMEM)),
        input_output_aliases={0: n_sems, 1: n_sems + 1},
        compiler_params=pltpu.CompilerParams(has_side_effects=_EFFECT),
    )(pltpu.with_memory_space_constraint(src, pltpu.HBM), pltpu.with_memory_space_constraint(land, pltpu.HBM))
    return res[:n_sems], res[n_sems], res[n_sems + 1], res[n_sems + 2]


def _exchange_wait(name, copies, sems, src_thru, land_thru, after):
    n_sems = len(sems)

    def body(src_ref, land_ref, *rest):
        remote, local = copies(src_ref, land_ref, rest[:n_sems])
        for cp in remote:
            cp.wait_send()
            cp.wait_recv()
        for cp in local:
            cp.wait()

    return pl.pallas_call(
        body,
        name=name,
        out_shape=(pltpu.HBM(src_thru.shape, src_thru.dtype), pltpu.HBM(land_thru.shape, land_thru.dtype)),
        in_specs=(_HBM, _HBM, *[_SEM] * n_sems, *[pl.BlockSpec(memory_space=pl.ANY)] * len(after)),
        out_specs=(_HBM, _HBM),
        input_output_aliases={0: 0, 1: 1},
        compiler_params=pltpu.CompilerParams(has_side_effects=_EFFECT),
    )(src_thru, land_thru, *sems, *after)[1]


def _bwd_gw(dsegs, h, g_gates):
    t = h.shape[0]
    tm = min(512, t)
    n_tiles = t // tm
    n_seg = len(_SEGMENTS)
    n_chips = N_DEV // 2

    def gate_copies(g_hbm, lg_hbm, send_sems, recv_sems, arriving):
        me = _position()
        return [
            pltpu.make_async_remote_copy(
                src_ref=g_hbm.at[:, :, _slot(_flip(me, k))],
                dst_ref=lg_hbm.at[_slot(_flip(me, k) if arriving else me)],
                send_sem=send_sems.at[k - 1],
                recv_sem=recv_sems.at[k - 1],
                device_id=_flip(me, k),
                device_id_type=MESH,
            )
            for k in range(1, N_DEV)
        ]

    def body(*refs):
        segs = refs[:n_seg]
        (h_ref, g_hbm, own_hbm, sum_hbm, lg_hbm, acc, stage, land, own_sems, pair_send, pair_recv, send_sems, recv_sems, g_local) = refs[n_seg:]
        i = pl.program_id(0)
        slot = _slot(_position())
        own_gates = pltpu.make_async_copy(g_hbm.at[:, :, slot], lg_hbm.at[slot], g_local)

        @pl.when(i == 0)
        def _():
            own_gates.start()
            for cp in gate_copies(g_hbm, lg_hbm, send_sems, recv_sems, arriving=False):
                cp.start()
            acc[...] = jnp.zeros_like(acc)

        def accumulate(k, hv):
            start, width = _SEGMENTS[k]
            acc[start : start + width, :] = acc[start : start + width, :] + _dot_tn(segs[k][...], hv)

        @pl.when(i < n_tiles - 1)
        def _():
            hv = h_ref[...]
            for k in range(n_seg):
                accumulate(k, hv)

        @pl.when(i == n_tiles - 1)
        def _():
            me = _position()
            sibling = _flip(me, 1)

            def rows(q, c):
                return pl.ds(pl.multiple_of(W_IN_SHARD * (2 * q + c), 8), W_IN_SHARD)

            def to_sibling(q):
                return pltpu.make_async_remote_copy(
                    src_ref=stage.at[q % 2], dst_ref=land.at[q], send_sem=pair_send.at[q], recv_sem=pair_recv.at[q],
                    device_id=sibling, device_id_type=MESH,
                )

            hv = h_ref[...]
            q = 0
            for k, (start, width) in enumerate(_SEGMENTS):
                accumulate(k, hv)
                while q < n_chips and 2 * W_IN_SHARD * (q + 1) <= start + width:
                    if q >= 2:
                        to_sibling(q - 2).wait_send()
                    stage[q % 2] = acc[rows(q, sibling[2]), :].astype(BF16)
                    to_sibling(q).start()
                    q += 1
            assert q == n_chips
            for q in range(n_chips - 2, n_chips):
                to_sibling(q).wait_send()
            sums = [pltpu.make_async_copy(stage.at[q % 2], sum_hbm.at[q], own_sems.at[q]) for q in range(n_chips)]
            for q in range(n_chips):
                to_sibling(q).wait_recv()
                if q >= 2:
                    sums[q - 2].wait()
                total = acc[rows(q, me[2]), :] + land[q].astype(F32)
                acc[rows(q, me[2]), :] = total
                stage[q % 2] = total.astype(BF16)
                sums[q].start()
            mine = pltpu.make_async_copy(acc.at[rows(_chip_of(me), me[2]), :], own_hbm, own_sems.at[n_chips])
            mine.start()
            for q in range(n_chips - 2, n_chips):
                sums[q].wait()
            mine.wait()
            for cp in gate_copies(g_hbm, lg_hbm, send_sems, recv_sems, arriving=True):
                cp.wait_recv()
            for cp in gate_copies(g_hbm, lg_hbm, send_sems, recv_sems, arriving=False):
                cp.wait_send()
            own_gates.wait()

    any_spec = pl.BlockSpec(memory_space=pl.ANY)
    n_copies = N_DEV - 1
    return pl.pallas_call(
        body,
        name="bwd_gw",
        grid=(n_tiles,),
        in_specs=_seg_specs(tm) + [pl.BlockSpec((tm, D), lambda i: (i, 0)), any_spec],
        out_specs=[any_spec, any_spec, any_spec],
        out_shape=[jax.ShapeDtypeStruct((W_IN_SHARD, D), F32),
                   jax.ShapeDtypeStruct((n_chips, W_IN_SHARD, D), BF16),
                   jax.ShapeDtypeStruct((N_DEV, 2, 4, GATE_SHARD, 256), F32)],
        scratch_shapes=[pltpu.VMEM((IN_W, D), F32), pltpu.VMEM((2, W_IN_SHARD, D), BF16), pltpu.VMEM((n_chips, W_IN_SHARD, D), BF16),
                        pltpu.SemaphoreType.DMA((n_chips + 1,)), pltpu.SemaphoreType.DMA((n_chips,)), pltpu.SemaphoreType.DMA((n_chips,)),
                        pltpu.SemaphoreType.DMA((n_copies,)), pltpu.SemaphoreType.DMA((n_copies,)), pltpu.SemaphoreType.DMA],
        compiler_params=_params(1),
    )(*dsegs, h, g_gates)


def _adamw(w, g, m, v):
    m = ADAM_B1 * m + (1.0 - ADAM_B1) * g
    v = ADAM_B2 * v + (1.0 - ADAM_B2) * (g * g)
    m_hat = m / (1.0 - ADAM_B1**ADAM_STEP)
    v_hat = v / (1.0 - ADAM_B2**ADAM_STEP)
    delta = -ADAM_LR * (m_hat / (jnp.sqrt(v_hat) + ADAM_EPS) + ADAM_WD * w)
    return delta, m, v


def _w_in_update(own, land, w_t, m_t, v_t, token):
    n_slots, rows, cols = land.shape
    rb = 208
    spec = pl.BlockSpec((rb, cols), lambda i: (i, 0))

    def body(o_ref, l_ref, w_ref, m_ref, v_ref, token_ref, g_ref, d_ref, m2_ref, v2_ref):
        total = o_ref[...]
        for s in range(n_slots):
            total = total + l_ref[s].astype(F32)
        g_ref[...] = total
        d_ref[...], m2_ref[...], v2_ref[...] = _adamw(w_ref[...], total, m_ref[...], v_ref[...])

    shape = jax.ShapeDtypeStruct((rows, cols), F32)
    return pl.pallas_call(
        body,
        name="w_in_update",
        grid=(rows // rb,),
        in_specs=[spec, pl.BlockSpec((n_slots, rb, cols), lambda i: (0, i, 0)), spec, spec, spec, pl.BlockSpec(token.shape, lambda i: (0, 0))],
        out_specs=[spec] * 4,
        out_shape=[shape] * 4,
        compiler_params=_params(1),
    )(own, land, w_t, m_t, v_t, token)


def _shard_update(own_3, land_3, land_g, weights, m_in, v_in, token):
    n = len(weights)

    def body(*refs):
        o3_ref, l3_ref, lg_ref = refs[:3]
        w_refs, m_refs, v_refs = (refs[3 + n * k : 3 + n * (k + 1)] for k in range(3))
        outs = refs[4 + 3 * n :]
        for k in range(n):
            if k < 3:
                g = o3_ref[k]
                for slot in range(3):
                    g = g + l3_ref[slot, k].astype(F32)
            else:
                g = lg_ref[0, k - 3]
                for slot in range(1, N_DEV):
                    g = g + lg_ref[slot, k - 3]
            d, m2, v2 = _adamw(w_refs[k][0], g, m_refs[k][0], v_refs[k][0])
            for which, val in enumerate((g, d, m2, v2)):
                outs[which * n + k][0] = val

    shapes = [jax.ShapeDtypeStruct(w.shape, F32) for w in weights]
    res = pl.pallas_call(
        body, name="shard_update", out_shape=shapes * 4, compiler_params=pltpu.CompilerParams(vmem_limit_bytes=VMEM_LIMIT)
    )(own_3, land_3, land_g, *weights, *m_in, *v_in, token)
    return [res[n * k : n * (k + 1)] for k in range(4)]


VEC_LAYOUT = (("b_ada", 3 * D), ("norm_g", D), ("attn_sinks", 128), ("conv_b", D), ("rg_ba", D), ("rg_bx", D), ("rg_lambda", D),
              ("final_g", D), ("conv_w", CONV_WIDTH * D), ("loss", 128))
VEC_OFFSET = {name: sum(w for _, w in VEC_LAYOUT[:k]) for k, (name, _) in enumerate(VEC_LAYOUT)}
VEC_ROWS = 112
VEC_WIDTH = VEC_ROWS * 128
assert sum(w for _, w in VEC_LAYOUT) <= VEC_WIDTH
N_REPLICATED = 8


def _small_update(vec_all, c_all_t, me, rep_w, rep_m, rep_v, ada, conv):
    n_ada = ada[0].shape[1]
    n_conv = conv[0].shape[1]

    def body(*refs):
        me_ref, vec_ref, ct_ref = refs[:3]
        w_refs, m_refs, v_refs = (refs[3 + N_REPLICATED * k : 3 + N_REPLICATED * (k + 1)] for k in range(3))
        pos = 3 + 3 * N_REPLICATED
        ada_refs, conv_refs = refs[pos : pos + 3], refs[pos + 3 : pos + 6]
        outs = refs[pos + 6 :]
        loss_ref, rep_outs = outs[0], outs[1 : 1 + 4 * N_REPLICATED]
        ada_outs, conv_outs = outs[1 + 4 * N_REPLICATED : 5 + 4 * N_REPLICATED], outs[5 + 4 * N_REPLICATED : 9 + 4 * N_REPLICATED]
        conv_buf = outs[9 + 4 * N_REPLICATED]

        total = vec_ref[0:1, :]
        for slot in range(1, N_DEV):
            total = total + vec_ref[slot : slot + 1, :]
        loss_ref[...] = total[:, VEC_OFFSET["loss"] : VEC_OFFSET["loss"] + 1]
        for k, (name, _) in enumerate(VEC_LAYOUT[:N_REPLICATED]):
            width = w_refs[k].shape[1]
            g = total[:, VEC_OFFSET[name] : VEC_OFFSET[name] + width]
            d, m2, v2 = _adamw(w_refs[k][...], g, m_refs[k][...], v_refs[k][...])
            for which, val in enumerate((g, d, m2, v2)):
                rep_outs[which * N_REPLICATED + k][...] = val

        device = me_ref[0]
        for k in range(CONV_WIDTH):
            conv_buf[k : k + 1, :] = total[:, VEC_OFFSET["conv_w"] + D * k : VEC_OFFSET["conv_w"] + D * (k + 1)]
        g_conv = conv_buf[:, pl.ds(pl.multiple_of(device * n_conv, 128), n_conv)]
        for ref, val in zip(conv_outs, (g_conv,) + _adamw(conv_refs[0][...], g_conv, conv_refs[1][...], conv_refs[2][...])):
            ref[...] = val

        dmod = vec_ref[:, pl.ds(pl.multiple_of(device * n_ada, 128), n_ada)]
        g_ada = _dot(ct_ref[...], dmod)
        for ref, val in zip(ada_outs, (g_ada,) + _adamw(ada_refs[0][...], g_ada, ada_refs[1][...], ada_refs[2][...])):
            ref[...] = val

    vmem = pl.BlockSpec(memory_space=pltpu.VMEM)
    rep_shapes = [jax.ShapeDtypeStruct(w.shape, F32) for w in rep_w]
    out_shape = ([jax.ShapeDtypeStruct((1, 1), F32)] + rep_shapes * 4 + [jax.ShapeDtypeStruct(ada[0].shape, F32)] * 4
                 + [jax.ShapeDtypeStruct(conv[0].shape, F32)] * 4)
    n_in = 3 + 3 * N_REPLICATED + 6
    res = pl.pallas_call(
        body,
        name="small_update",
        in_specs=[pl.BlockSpec(memory_space=pltpu.SMEM)] + [vmem] * (n_in - 1),
        out_specs=[vmem] * len(out_shape),
        out_shape=out_shape,
        scratch_shapes=[pltpu.VMEM((CONV_WIDTH, D), F32)],
        compiler_params=pltpu.CompilerParams(vmem_limit_bytes=VMEM_LIMIT),
    )(me, vec_all, c_all_t, *rep_w, *rep_m, *rep_v, *ada, *conv)
    loss = res[0]
    reps = [res[1 + N_REPLICATED * k : 1 + N_REPLICATED * (k + 1)] for k in range(4)]
    return loss, reps, res[1 + 4 * N_REPLICATED : 5 + 4 * N_REPLICATED], res[5 + 4 * N_REPLICATED : 9 + 4 * N_REPLICATED]


def kernel(x, c, positions, w_ada, b_ada, norm_g, w_in, attn_sinks, conv_w, conv_b, rg_wa, rg_ba, rg_wx, rg_bx, rg_lambda, w_attn_proj, w_rnn_proj, w_out, final_g, loss_target, m_w_ada, m_b_ada, m_norm_g, m_w_in, m_attn_sinks, m_conv_w, m_conv_b, m_rg_wa, m_rg_ba, m_rg_wx, m_rg_bx, m_rg_lambda, m_w_attn_proj, m_w_rnn_proj, m_w_out, m_final_g, v_w_ada, v_b_ada, v_norm_g, v_w_in, v_attn_sinks, v_conv_w, v_conv_b, v_rg_wa, v_rg_ba, v_rg_wx, v_rg_bx, v_rg_lambda, v_w_attn_proj, v_w_rnn_proj, v_w_out, v_final_g):
    my_x, my_y, my_c = _position()
    me = _slot((my_x, my_y, my_c))
    x2, target = x[0], loss_target[0]
    pos = positions[0]

    gate_rows = lambda w: w[0].transpose(1, 0, 2).reshape(GATE_SHARD, D)
    w_pack = jnp.concatenate([w_attn_proj[0], w_rnn_proj[0], w_out[0], gate_rows(rg_wa), gate_rows(rg_wx)], axis=0).astype(BF16)
    conv_rows = jnp.pad(conv_w[0], ((0, 8 - CONV_WIDTH), (0, 0)))
    w_in_t = w_in[0].T

    tables = _rope_tables(pos)
    where = jnp.stack([_chip_of((my_x, my_y)), me]).astype(jnp.int32)
    proj, h, mod, lane_tables, w_t_all, w_all, c_rows, conv_all = _fwd_in(
        x2, jnp.broadcast_to(c, (8, D)), conv_rows, w_ada[0], b_ada, norm_g, w_in_t.astype(BF16), tables, w_pack, where
    )
    c_all = c_rows[:, 0, :]
    conv_full = conv_all[:, :CONV_WIDTH, :].transpose(1, 0, 2).reshape(CONV_WIDTH, D)
    o = _attn_fwd(proj, attn_sinks)
    keep = (pos != 0).astype(F32)[:, None]
    hh = _rnn_fwd(proj, keep, w_all, rg_ba, rg_bx, rg_lambda, conv_full, conv_b)

    dres, d_o, d_hh, d_ga, d_gr, d_ma, d_mr, gw3, mid_vec = _mid(x2, target, o, hh, proj, w_all, mod, final_g[None, :])
    part_3 = gw3.reshape(3, 4, 2, ROW_SHARD, D)
    d_q, d_k, d_v, d_sink, sum_3_bf, own_3 = _attn_bwd(proj, o, d_o, (lane_tables, tables), attn_sinks, part_3)
    d_xr, g_gates, rnn_vec, land_3 = _rnn_bwd(proj, hh, d_hh, keep, w_all, rg_ba, rg_bx, rg_lambda, conv_full, conv_b, sum_3_bf)
    dsegs = (d_q, d_k, d_v, d_ga, d_xr, d_gr, d_ma, d_mr)
    own_w, sum_w_bf, land_g = _bwd_gw(dsegs, h, g_gates.reshape(2, 4, N_DEV, GATE_SHARD, 256))
    w_sems, sum_w_thru, land_w_thru, token = _exchange_start(
        "w_exchange_start", _w_exchange_copies, 2 * len(CHIP_FLIPS), sum_w_bf, (len(CHIP_FLIPS), W_IN_SHARD, D)
    )
    grad_x, dh_vec = _bwd_dh(dsegs, w_t_all, x2, dres, mod, norm_g, token)
    land_w = _exchange_wait("w_exchange_wait", _w_exchange_copies, w_sems, sum_w_thru, land_w_thru, (dh_vec,))

    pieces = dict(
        b_ada=jnp.concatenate([dh_vec[0:1], dh_vec[1:2], mid_vec[1:2]], axis=1),
        norm_g=dh_vec[2:3],
        attn_sinks=jnp.pad(d_sink[:, :, 0].reshape(1, 16), ((0, 0), (0, 112))),
        conv_b=rnn_vec[3:4],
        rg_ba=rnn_vec[0:1],
        rg_bx=rnn_vec[1:2],
        rg_lambda=rnn_vec[2:3],
        final_g=mid_vec[0:1],
        conv_w=rnn_vec[4:8].reshape(1, CONV_WIDTH * D),
        loss=mid_vec[2:3, 0:128],
    )
    vec = jnp.concatenate([pieces[name] for name, _ in VEC_LAYOUT], axis=1)
    vec = jnp.pad(vec, ((0, 0), (0, VEC_WIDTH - vec.shape[1]))).reshape(VEC_ROWS, 128)
    vec_sems, vec_thru, vec_land, vec_token = _exchange_start(
        "vec_exchange_start", _vec_exchange_copies, 2 * N_DEV - 1, vec, (N_DEV, VEC_ROWS, 128)
    )
    w_in_res = _w_in_update(own_w, land_w, w_in_t, m_w_in[0].T, v_w_in[0].T, vec_token)
    shard_res = _shard_update(
        own_3, land_3, land_g,
        (w_attn_proj, w_rnn_proj, w_out, rg_wa, rg_wx),
        (m_w_attn_proj, m_w_rnn_proj, m_w_out, m_rg_wa, m_rg_wx),
        (v_w_attn_proj, v_w_rnn_proj, v_w_out, v_rg_wa, v_rg_wx),
        vec_token,
    )
    vec_all = _exchange_wait(
        "vec_exchange_wait", _vec_exchange_copies, vec_sems, vec_thru, vec_land, (w_in_res[0], shard_res[0][0])
    )
    rep_w = [b_ada, norm_g, attn_sinks, conv_b, rg_ba, rg_bx, rg_lambda, final_g[None, :]]
    rep_m = [m_b_ada, m_norm_g, m_attn_sinks, m_conv_b, m_rg_ba, m_rg_bx, m_rg_lambda, m_final_g[None, :]]
    rep_v = [v_b_ada, v_norm_g, v_attn_sinks, v_conv_b, v_rg_ba, v_rg_bx, v_rg_lambda, v_final_g[None, :]]
    loss, reps, ada_res, conv_res = _small_update(
        vec_all.reshape(N_DEV, VEC_WIDTH), c_all.T, me.astype(jnp.int32).reshape(1), rep_w, rep_m, rep_v,
        (w_ada[0], m_w_ada[0], v_w_ada[0]), (conv_w[0], m_conv_w[0], v_conv_w[0]),
    )

    def by_weight(k):
        b, n, s, cb, ba, bx, lam, fg = reps[k]
        ap, rp, out, wa, wx = shard_res[k]
        return [ada_res[k][None], b, n, w_in_res[k].T[None], s, conv_res[k][None], cb, wa, ba, wx, bx, lam, ap, rp, out, fg[0]]

    return (loss[0, 0], grad_x[None], *by_weight(0), *by_weight(1), *by_weight(2), *by_weight(3))
```
